```python
import math
import jax, jax.numpy as jnp
from jax import lax
import numpy as np

D_MODEL = 1024
BATCH = 8
SEQ = 4096
DEPTH = 4

N_A_LAYERS = DEPTH // 2
N_B_LAYERS = DEPTH - N_A_LAYERS

CHUNK = 128
A_WIDTH = 2 * D_MODEL
A_GROUPS = 8
A_GROUP_DIM = A_WIDTH // A_GROUPS

N_HEADS = 16
N_KV_HEADS = 4
Q_PER_KV = N_HEADS // N_KV_HEADS
HEAD_DIM = 64
WINDOW = 128
BLOCK = 128

N_BUCKETS = 32
MAX_DISTANCE = 128

D_FF = 4 * D_MODEL
EPS = 1e-6

kernel_name = "yoco_sgu_swa_sink_hybrid"


def rms_norm(x, g):
    xf = x.astype(jnp.float32)
    y = xf * lax.rsqrt(jnp.mean(xf * xf, axis=-1, keepdims=True) + EPS)
    return (y * g.astype(jnp.float32)).astype(x.dtype)


def layer_norm(x, g):
    xf = x.astype(jnp.float32)
    mu = jnp.mean(xf, axis=-1, keepdims=True)
    xc = xf - mu
    y = xc * lax.rsqrt(jnp.mean(xc * xc, axis=-1, keepdims=True) + EPS)
    return (y * g.astype(jnp.float32)).astype(x.dtype)


def t5_causal_bucket(dist):
    max_exact = N_BUCKETS // 2
    d = jnp.maximum(dist, 0)
    log_ratio = jnp.log(jnp.maximum(d, 1).astype(jnp.float32) / max_exact) / math.log(MAX_DISTANCE / max_exact)
    large = jnp.minimum(max_exact + (log_ratio * (N_BUCKETS - max_exact)).astype(jnp.int32), N_BUCKETS - 1)
    return jnp.where(d < max_exact, d, large)


def band_bias_and_mask(rel_bias, n_blocks):
    qi = jnp.arange(BLOCK)[:, None]
    kj = jnp.arange(2 * BLOCK)[None, :]
    dist = qi + BLOCK - kj
    in_window = (dist >= 0) & (dist < WINDOW)
    bias = jnp.transpose(rel_bias.astype(jnp.float32)[t5_causal_bucket(dist)], (2, 0, 1))
    k_pos = (jnp.arange(n_blocks)[:, None, None] - 1) * BLOCK + kj[None]
    mask = in_window[None] & (k_pos >= 0)
    return bias, mask


def chunked_sgu_mixer(h, w_in, ln_g, w_s, b_s, w_out):
    B, S, _ = h.shape
    uv = jax.nn.gelu(h @ w_in, approximate=False)
    u, v = jnp.split(uv, 2, axis=-1)
    v = layer_norm(v, ln_g).reshape(B, S // CHUNK, CHUNK, A_GROUPS, A_GROUP_DIM)
    causal = jnp.tril(jnp.ones((CHUNK, CHUNK), dtype=bool))
    w_causal = jnp.where(causal[None], w_s, jnp.zeros_like(w_s))
    mixed = jnp.einsum('gij,bnjgd->bnigd', w_causal, v) + jnp.transpose(b_s)[None, None, :, :, None]
    gated = u * mixed.reshape(B, S, A_WIDTH)
    return gated @ w_out


def shared_band_kv(h, kv_norm_g, w_k, w_v):
    B, S, _ = h.shape
    nb = S // BLOCK
    hn = rms_norm(h, kv_norm_g)
    k = (hn @ w_k).reshape(B, nb, BLOCK, N_KV_HEADS, HEAD_DIM)
    v = (hn @ w_v).reshape(B, nb, BLOCK, N_KV_HEADS, HEAD_DIM)

    def band(t):
        prev = jnp.pad(t, ((0, 0), (1, 0), (0, 0), (0, 0), (0, 0)))[:, :-1]
        return jnp.concatenate([prev, t], axis=2)

    return band(k), band(v)


def sliding_sink_attention(h, k_band, v_band, w_q, sinks, w_o, band_bias, band_mask):
    B, S, _ = h.shape
    nb = S // BLOCK
    q = (h @ w_q).reshape(B, nb, BLOCK, N_KV_HEADS, Q_PER_KV, HEAD_DIM)
    logits = jnp.einsum('bnqkgd,bnjkd->bnkgqj', q, k_band).astype(jnp.float32) * (HEAD_DIM ** -0.5)
    logits = logits + band_bias.reshape(N_KV_HEADS, Q_PER_KV, BLOCK, 2 * BLOCK)
    logits = jnp.where(band_mask[None, :, None, None], logits, jnp.finfo(jnp.float32).min)
    sink = sinks.astype(jnp.float32).reshape(N_KV_HEADS, Q_PER_KV)[None, None, :, :, None, None]
    m = jnp.maximum(jnp.max(logits, axis=-1, keepdims=True), sink)
    e = jnp.exp(logits - m)
    probs = e / (jnp.sum(e, axis=-1, keepdims=True) + jnp.exp(sink - m))
    out = jnp.einsum('bnkgqj,bnjkd->bnqkgd', probs.astype(v_band.dtype), v_band)
    return out.reshape(B, S, N_HEADS * HEAD_DIM) @ w_o


def sq_relu_mlp(h, w1, w2):
    return jnp.square(jax.nn.relu(h @ w1)) @ w2


def _fwd_setup_inputs(seed: int = 0) -> dict:
    key = jax.random.key(seed)
    ks = jax.random.split(key, 20)
    f32 = jnp.float32

    def nrm(k, shape, scale):
        return jax.random.normal(k, shape, f32) * scale

    return {
        "x": jax.random.normal(ks[0], (BATCH, SEQ, D_MODEL), f32),
        "mix_norm_g": 1.0 + nrm(ks[1], (DEPTH, D_MODEL), 0.05),
        "ffn_norm_g": 1.0 + nrm(ks[2], (DEPTH, D_MODEL), 0.05),
        "a_w_in": nrm(ks[3], (N_A_LAYERS, D_MODEL, 2 * A_WIDTH), D_MODEL ** -0.5),
        "a_ln_g": 1.0 + nrm(ks[4], (N_A_LAYERS, A_WIDTH), 0.05),
        "a_w_spatial": nrm(ks[5], (N_A_LAYERS, A_GROUPS, CHUNK, CHUNK), CHUNK ** -0.5),
        "a_b_spatial": 1.0 + nrm(ks[6], (N_A_LAYERS, A_GROUPS, CHUNK), 0.1),
        "a_w_out": nrm(ks[7], (N_A_LAYERS, A_WIDTH, D_MODEL), A_WIDTH ** -0.5),
        "kv_norm_g": 1.0 + nrm(ks[8], (D_MODEL,), 0.05),
        "w_k": nrm(ks[9], (D_MODEL, N_KV_HEADS * HEAD_DIM), D_MODEL ** -0.5),
        "w_v": nrm(ks[10], (D_MODEL, N_KV_HEADS * HEAD_DIM), D_MODEL ** -0.5),
        "b_w_q": nrm(ks[11], (N_B_LAYERS, D_MODEL, N_HEADS * HEAD_DIM), D_MODEL ** -0.5),
        "b_sinks": nrm(ks[12], (N_B_LAYERS, N_HEADS), 0.5),
        "b_w_o": nrm(ks[13], (N_B_LAYERS, N_HEADS * HEAD_DIM, D_MODEL), (N_HEADS * HEAD_DIM) ** -0.5),
        "rel_bias": nrm(ks[14], (N_BUCKETS, N_HEADS), 0.5),
        "ffn_w1": nrm(ks[15], (DEPTH, D_MODEL, D_FF), D_MODEL ** -0.5),
        "ffn_w2": nrm(ks[16], (DEPTH, D_FF, D_MODEL), 0.5 * D_FF ** -0.5),
        "final_norm_g": 1.0 + nrm(ks[17], (D_MODEL,), 0.05),
    }


def _fwd_reference(x, mix_norm_g, ffn_norm_g, a_w_in, a_ln_g, a_w_spatial, a_b_spatial, a_w_out,
              kv_norm_g, w_k, w_v, b_w_q, b_sinks, b_w_o, rel_bias, ffn_w1, ffn_w2, final_norm_g):
    _, S, _ = x.shape
    band_bias, band_mask = band_bias_and_mask(rel_bias, S // BLOCK)
    h = x
    k_band = None
    v_band = None
    for layer in range(DEPTH):
        hn = rms_norm(h, mix_norm_g[layer])
        if layer < N_A_LAYERS:
            i = layer
            h = h + chunked_sgu_mixer(hn, a_w_in[i], a_ln_g[i], a_w_spatial[i], a_b_spatial[i], a_w_out[i])
        else:
            if layer == N_A_LAYERS:
                k_band, v_band = shared_band_kv(h, kv_norm_g, w_k, w_v)
                hn = rms_norm(h, mix_norm_g[layer])
            j = layer - N_A_LAYERS
            h = h + sliding_sink_attention(hn, k_band, v_band, b_w_q[j], b_sinks[j], b_w_o[j], band_bias, band_mask)
        h = h + sq_relu_mlp(rms_norm(h, ffn_norm_g[layer]), ffn_w1[layer], ffn_w2[layer])
    return rms_norm(h, final_norm_g)


import jax as _jax
import jax.numpy as _jnp

TWIN_FORMAT = 'train_step'
FWD_PARAMS = ['x', 'mix_norm_g', 'ffn_norm_g', 'a_w_in', 'a_ln_g', 'a_w_spatial', 'a_b_spatial', 'a_w_out', 'kv_norm_g', 'w_k', 'w_v', 'b_w_q', 'b_sinks', 'b_w_o', 'rel_bias', 'ffn_w1', 'ffn_w2', 'final_norm_g']
TWIN_WEIGHTS = ['mix_norm_g', 'ffn_norm_g', 'a_w_in', 'a_ln_g', 'a_w_spatial', 'a_b_spatial', 'a_w_out', 'kv_norm_g', 'w_k', 'w_v', 'b_w_q', 'b_sinks', 'b_w_o', 'rel_bias', 'ffn_w1', 'ffn_w2', 'final_norm_g']
TWIN_DIFF_INPUT = 'x'
TWIN_INPUTS = ['x', 'mix_norm_g', 'ffn_norm_g', 'a_w_in', 'a_ln_g', 'a_w_spatial', 'a_b_spatial', 'a_w_out', 'kv_norm_g', 'w_k', 'w_v', 'b_w_q', 'b_sinks', 'b_w_o', 'rel_bias', 'ffn_w1', 'ffn_w2', 'final_norm_g', 'loss_target', 'm_mix_norm_g', 'm_ffn_norm_g', 'm_a_w_in', 'm_a_ln_g', 'm_a_w_spatial', 'm_a_b_spatial', 'm_a_w_out', 'm_kv_norm_g', 'm_w_k', 'm_w_v', 'm_b_w_q', 'm_b_sinks', 'm_b_w_o', 'm_rel_bias', 'm_ffn_w1', 'm_ffn_w2', 'm_final_norm_g', 'v_mix_norm_g', 'v_ffn_norm_g', 'v_a_w_in', 'v_a_ln_g', 'v_a_w_spatial', 'v_a_b_spatial', 'v_a_w_out', 'v_kv_norm_g', 'v_w_k', 'v_w_v', 'v_b_w_q', 'v_b_sinks', 'v_b_w_o', 'v_rel_bias', 'v_ffn_w1', 'v_ffn_w2', 'v_final_norm_g']
TWIN_OUTPUTS = ['loss', 'grad_x', 'grad_mix_norm_g', 'grad_ffn_norm_g', 'grad_a_w_in', 'grad_a_ln_g', 'grad_a_w_spatial', 'grad_a_b_spatial', 'grad_a_w_out', 'grad_kv_norm_g', 'grad_w_k', 'grad_w_v', 'grad_b_w_q', 'grad_b_sinks', 'grad_b_w_o', 'grad_rel_bias', 'grad_ffn_w1', 'grad_ffn_w2', 'grad_final_norm_g', 'delta_mix_norm_g', 'delta_ffn_norm_g', 'delta_a_w_in', 'delta_a_ln_g', 'delta_a_w_spatial', 'delta_a_b_spatial', 'delta_a_w_out', 'delta_kv_norm_g', 'delta_w_k', 'delta_w_v', 'delta_b_w_q', 'delta_b_sinks', 'delta_b_w_o', 'delta_rel_bias', 'delta_ffn_w1', 'delta_ffn_w2', 'delta_final_norm_g', 'new_m_mix_norm_g', 'new_m_ffn_norm_g', 'new_m_a_w_in', 'new_m_a_ln_g', 'new_m_a_w_spatial', 'new_m_a_b_spatial', 'new_m_a_w_out', 'new_m_kv_norm_g', 'new_m_w_k', 'new_m_w_v', 'new_m_b_w_q', 'new_m_b_sinks', 'new_m_b_w_o', 'new_m_rel_bias', 'new_m_ffn_w1', 'new_m_ffn_w2', 'new_m_final_norm_g', 'new_v_mix_norm_g', 'new_v_ffn_norm_g', 'new_v_a_w_in', 'new_v_a_ln_g', 'new_v_a_w_spatial', 'new_v_a_b_spatial', 'new_v_a_w_out', 'new_v_kv_norm_g', 'new_v_w_k', 'new_v_w_v', 'new_v_b_w_q', 'new_v_b_sinks', 'new_v_b_w_o', 'new_v_rel_bias', 'new_v_ffn_w1', 'new_v_ffn_w2', 'new_v_final_norm_g']
TWIN_LEAF_KINDS = {'loss': 'loss', 'grad_x': 'grad_x', 'grad_mix_norm_g': 'grad_w', 'grad_ffn_norm_g': 'grad_w', 'grad_a_w_in': 'grad_w', 'grad_a_ln_g': 'grad_w', 'grad_a_w_spatial': 'grad_w', 'grad_a_b_spatial': 'grad_w', 'grad_a_w_out': 'grad_w', 'grad_kv_norm_g': 'grad_w', 'grad_w_k': 'grad_w', 'grad_w_v': 'grad_w', 'grad_b_w_q': 'grad_w', 'grad_b_sinks': 'grad_w', 'grad_b_w_o': 'grad_w', 'grad_rel_bias': 'grad_w', 'grad_ffn_w1': 'grad_w', 'grad_ffn_w2': 'grad_w', 'grad_final_norm_g': 'grad_w', 'delta_mix_norm_g': 'delta_w', 'delta_ffn_norm_g': 'delta_w', 'delta_a_w_in': 'delta_w', 'delta_a_ln_g': 'delta_w', 'delta_a_w_spatial': 'delta_w', 'delta_a_b_spatial': 'delta_w', 'delta_a_w_out': 'delta_w', 'delta_kv_norm_g': 'delta_w', 'delta_w_k': 'delta_w', 'delta_w_v': 'delta_w', 'delta_b_w_q': 'delta_w', 'delta_b_sinks': 'delta_w', 'delta_b_w_o': 'delta_w', 'delta_rel_bias': 'delta_w', 'delta_ffn_w1': 'delta_w', 'delta_ffn_w2': 'delta_w', 'delta_final_norm_g': 'delta_w', 'new_m_mix_norm_g': 'new_m', 'new_m_ffn_norm_g': 'new_m', 'new_m_a_w_in': 'new_m', 'new_m_a_ln_g': 'new_m', 'new_m_a_w_spatial': 'new_m', 'new_m_a_b_spatial': 'new_m', 'new_m_a_w_out': 'new_m', 'new_m_kv_norm_g': 'new_m', 'new_m_w_k': 'new_m', 'new_m_w_v': 'new_m', 'new_m_b_w_q': 'new_m', 'new_m_b_sinks': 'new_m', 'new_m_b_w_o': 'new_m', 'new_m_rel_bias': 'new_m', 'new_m_ffn_w1': 'new_m', 'new_m_ffn_w2': 'new_m', 'new_m_final_norm_g': 'new_m', 'new_v_mix_norm_g': 'new_v', 'new_v_ffn_norm_g': 'new_v', 'new_v_a_w_in': 'new_v', 'new_v_a_ln_g': 'new_v', 'new_v_a_w_spatial': 'new_v', 'new_v_a_b_spatial': 'new_v', 'new_v_a_w_out': 'new_v', 'new_v_kv_norm_g': 'new_v', 'new_v_w_k': 'new_v', 'new_v_w_v': 'new_v', 'new_v_b_w_q': 'new_v', 'new_v_b_sinks': 'new_v', 'new_v_b_w_o': 'new_v', 'new_v_rel_bias': 'new_v', 'new_v_ffn_w1': 'new_v', 'new_v_ffn_w2': 'new_v', 'new_v_final_norm_g': 'new_v'}


def _forward(args):
    return _fwd_reference(*[args[k] for k in FWD_PARAMS])


def _output_shape():
    out = _jax.eval_shape(lambda: _forward(_fwd_setup_inputs(0)))
    return out.shape, out.dtype

N_MICROBATCH = 1
ADAM_LR = 0.001
ADAM_B1 = 0.9
ADAM_B2 = 0.999
ADAM_EPS = 1e-08
ADAM_WD = 0.01
ADAM_STEP = 10
PER_EXAMPLE_BATCH_AXIS = {'x': 0, 'loss_target': 0}
SHARED_INPUTS = []
_WEIGHT_DTYPES = {'mix_norm_g': _jnp.float32, 'ffn_norm_g': _jnp.float32, 'a_w_in': _jnp.float32, 'a_ln_g': _jnp.float32, 'a_w_spatial': _jnp.float32, 'a_b_spatial': _jnp.float32, 'a_w_out': _jnp.float32, 'kv_norm_g': _jnp.float32, 'w_k': _jnp.float32, 'w_v': _jnp.float32, 'b_w_q': _jnp.float32, 'b_sinks': _jnp.float32, 'b_w_o': _jnp.float32, 'rel_bias': _jnp.float32, 'ffn_w1': _jnp.float32, 'ffn_w2': _jnp.float32, 'final_norm_g': _jnp.float32}
MOMENT_SCALE = {'mix_norm_g': 1.053678e-01, 'ffn_norm_g': 1.061780e-01, 'a_w_in': 7.156165e-02, 'a_ln_g': 4.573196e-02, 'a_w_spatial': 6.084938e-02, 'a_b_spatial': 8.876253e-02, 'a_w_out': 2.234887e-01, 'kv_norm_g': 2.535682e-01, 'w_k': 4.662975e-02, 'w_v': 5.083810e-01, 'b_w_q': 1.637725e-02, 'b_sinks': 1.804940e-02, 'b_w_o': 1.917640e-01, 'rel_bias': 3.174743e-02, 'ffn_w1': 5.133735e-02, 'ffn_w2': 3.516245e-01, 'final_norm_g': 3.340623e+01}


def _to_microbatches(a, axis):
    t = _jnp.moveaxis(a, axis, 0)
    t = t.reshape((N_MICROBATCH, t.shape[0] // N_MICROBATCH) + t.shape[1:])
    return _jnp.moveaxis(t, 1, axis + 1)


def setup_inputs(seed: int = 0) -> dict:
    inp = _fwd_setup_inputs(seed)
    key = _jax.random.fold_in(_jax.random.key(seed), 7919)
    shape, _ = _output_shape()
    out = dict(inp)
    out["loss_target"] = _jax.random.normal(_jax.random.fold_in(key, 0), shape, _jnp.float32)
    for i, name in enumerate(TWIN_WEIGHTS):
        w = inp[name].astype(_jnp.float32)
        if MOMENT_SCALE is None:
            s = _jnp.sqrt(_jnp.mean(_jnp.square(w)) + 1e-30)
        else:
            s = MOMENT_SCALE[name]
        km, kv = _jax.random.split(_jax.random.fold_in(key, i + 1))
        out[name] = w
        out["m_" + name] = s * _jax.random.normal(km, w.shape, _jnp.float32)
        out["v_" + name] = (s * s) * _jax.random.uniform(kv, w.shape, _jnp.float32, 0.5, 1.5)
    if N_MICROBATCH > 1:
        for name, axis in PER_EXAMPLE_BATCH_AXIS.items():
            out[name] = _to_microbatches(out[name], axis)
    return {'x': out['x'], 'mix_norm_g': out['mix_norm_g'], 'ffn_norm_g': out['ffn_norm_g'], 'a_w_in': out['a_w_in'], 'a_ln_g': out['a_ln_g'], 'a_w_spatial': out['a_w_spatial'], 'a_b_spatial': out['a_b_spatial'], 'a_w_out': out['a_w_out'], 'kv_norm_g': out['kv_norm_g'], 'w_k': out['w_k'], 'w_v': out['w_v'], 'b_w_q': out['b_w_q'], 'b_sinks': out['b_sinks'], 'b_w_o': out['b_w_o'], 'rel_bias': out['rel_bias'], 'ffn_w1': out['ffn_w1'], 'ffn_w2': out['ffn_w2'], 'final_norm_g': out['final_norm_g'], 'loss_target': out['loss_target'], 'm_mix_norm_g': out['m_mix_norm_g'], 'm_ffn_norm_g': out['m_ffn_norm_g'], 'm_a_w_in': out['m_a_w_in'], 'm_a_ln_g': out['m_a_ln_g'], 'm_a_w_spatial': out['m_a_w_spatial'], 'm_a_b_spatial': out['m_a_b_spatial'], 'm_a_w_out': out['m_a_w_out'], 'm_kv_norm_g': out['m_kv_norm_g'], 'm_w_k': out['m_w_k'], 'm_w_v': out['m_w_v'], 'm_b_w_q': out['m_b_w_q'], 'm_b_sinks': out['m_b_sinks'], 'm_b_w_o': out['m_b_w_o'], 'm_rel_bias': out['m_rel_bias'], 'm_ffn_w1': out['m_ffn_w1'], 'm_ffn_w2': out['m_ffn_w2'], 'm_final_norm_g': out['m_final_norm_g'], 'v_mix_norm_g': out['v_mix_norm_g'], 'v_ffn_norm_g': out['v_ffn_norm_g'], 'v_a_w_in': out['v_a_w_in'], 'v_a_ln_g': out['v_a_ln_g'], 'v_a_w_spatial': out['v_a_w_spatial'], 'v_a_b_spatial': out['v_a_b_spatial'], 'v_a_w_out': out['v_a_w_out'], 'v_kv_norm_g': out['v_kv_norm_g'], 'v_w_k': out['v_w_k'], 'v_w_v': out['v_w_v'], 'v_b_w_q': out['v_b_w_q'], 'v_b_sinks': out['v_b_sinks'], 'v_b_w_o': out['v_b_w_o'], 'v_rel_bias': out['v_rel_bias'], 'v_ffn_w1': out['v_ffn_w1'], 'v_ffn_w2': out['v_ffn_w2'], 'v_final_norm_g': out['v_final_norm_g']}


def _loss(weights, diff, rest, loss_target):
    with _jax.named_scope("forward"):
        args = {**rest, TWIN_DIFF_INPUT: diff, **{k: w.astype(_WEIGHT_DTYPES[k]) for k, w in weights.items()}}
        y = _forward(args)
    with _jax.named_scope("loss_head"):
        err = _jnp.square(y.astype(_jnp.float32) - loss_target)
        return 0.5 * _jnp.sum(_jnp.mean(err, axis=-1)) if err.ndim else 0.5 * err


def _adamw(w, g, m, v):
    m = ADAM_B1 * m + (1.0 - ADAM_B1) * g
    v = ADAM_B2 * v + (1.0 - ADAM_B2) * _jnp.square(g)
    m_hat = m / (1.0 - ADAM_B1 ** ADAM_STEP)
    v_hat = v / (1.0 - ADAM_B2 ** ADAM_STEP)
    delta = -ADAM_LR * (m_hat / (_jnp.sqrt(v_hat) + ADAM_EPS) + ADAM_WD * w)
    return delta, m, v


def reference(x, mix_norm_g, ffn_norm_g, a_w_in, a_ln_g, a_w_spatial, a_b_spatial, a_w_out, kv_norm_g, w_k, w_v, b_w_q, b_sinks, b_w_o, rel_bias, ffn_w1, ffn_w2, final_norm_g, loss_target, m_mix_norm_g, m_ffn_norm_g, m_a_w_in, m_a_ln_g, m_a_w_spatial, m_a_b_spatial, m_a_w_out, m_kv_norm_g, m_w_k, m_w_v, m_b_w_q, m_b_sinks, m_b_w_o, m_rel_bias, m_ffn_w1, m_ffn_w2, m_final_norm_g, v_mix_norm_g, v_ffn_norm_g, v_a_w_in, v_a_ln_g, v_a_w_spatial, v_a_b_spatial, v_a_w_out, v_kv_norm_g, v_w_k, v_w_v, v_b_w_q, v_b_sinks, v_b_w_o, v_rel_bias, v_ffn_w1, v_ffn_w2, v_final_norm_g):
    given = dict(x=x, mix_norm_g=mix_norm_g, ffn_norm_g=ffn_norm_g, a_w_in=a_w_in, a_ln_g=a_ln_g, a_w_spatial=a_w_spatial, a_b_spatial=a_b_spatial, a_w_out=a_w_out, kv_norm_g=kv_norm_g, w_k=w_k, w_v=w_v, b_w_q=b_w_q, b_sinks=b_sinks, b_w_o=b_w_o, rel_bias=rel_bias, ffn_w1=ffn_w1, ffn_w2=ffn_w2, final_norm_g=final_norm_g, loss_target=loss_target, m_mix_norm_g=m_mix_norm_g, m_ffn_norm_g=m_ffn_norm_g, m_a_w_in=m_a_w_in, m_a_ln_g=m_a_ln_g, m_a_w_spatial=m_a_w_spatial, m_a_b_spatial=m_a_b_spatial, m_a_w_out=m_a_w_out, m_kv_norm_g=m_kv_norm_g, m_w_k=m_w_k, m_w_v=m_w_v, m_b_w_q=m_b_w_q, m_b_sinks=m_b_sinks, m_b_w_o=m_b_w_o, m_rel_bias=m_rel_bias, m_ffn_w1=m_ffn_w1, m_ffn_w2=m_ffn_w2, m_final_norm_g=m_final_norm_g, v_mix_norm_g=v_mix_norm_g, v_ffn_norm_g=v_ffn_norm_g, v_a_w_in=v_a_w_in, v_a_ln_g=v_a_ln_g, v_a_w_spatial=v_a_w_spatial, v_a_b_spatial=v_a_b_spatial, v_a_w_out=v_a_w_out, v_kv_norm_g=v_kv_norm_g, v_w_k=v_w_k, v_w_v=v_w_v, v_b_w_q=v_b_w_q, v_b_sinks=v_b_sinks, v_b_w_o=v_b_w_o, v_rel_bias=v_rel_bias, v_ffn_w1=v_ffn_w1, v_ffn_w2=v_ffn_w2, v_final_norm_g=v_final_norm_g)
    weights = {n: given[n] for n in TWIN_WEIGHTS}
    shared = {n: given[n] for n in SHARED_INPUTS}
    per_example = {n: given[n] for n in ['x']}
    grad_fn = _jax.value_and_grad(_loss, argnums=(0, 1))

    def one_microbatch(ex, loss_target):
        ex = dict(ex)
        diff = ex.pop(TWIN_DIFF_INPUT)
        return grad_fn(weights, diff, {**shared, **ex}, loss_target)

    if N_MICROBATCH == 1:
        loss, (grad_w, grad_x) = one_microbatch(per_example, given["loss_target"])
    else:
        def body(carry, xs):
            loss_sum, grad_sum = carry
            l_k, (gw_k, gx_k) = one_microbatch(xs[0], xs[1])
            with _jax.named_scope("update"):
                return (loss_sum + l_k, _jax.tree.map(_jnp.add, grad_sum, gw_k)), gx_k

        init = (_jnp.zeros((), _jnp.float32), _jax.tree.map(_jnp.zeros_like, weights))
        (loss, grad_w), grad_x = _jax.lax.scan(body, init, (per_example, given["loss_target"]))
    with _jax.named_scope("update"):
        delta_w, new_m, new_v = {}, {}, {}
        for n in TWIN_WEIGHTS:
            delta_w[n], new_m[n], new_v[n] = _adamw(weights[n], grad_w[n], given["m_" + n], given["v_" + n])
    return (loss, grad_x, *[grad_w[n] for n in TWIN_WEIGHTS], *[delta_w[n] for n in TWIN_WEIGHTS],
            *[new_m[n] for n in TWIN_WEIGHTS], *[new_v[n] for n in TWIN_WEIGHTS])
```

```python
import functools
import math

import numpy as np
import jax
import jax.numpy as jnp
from jax import lax
from jax.experimental import pallas as pl
from jax.experimental.pallas import tpu as pltpu

F32 = jnp.float32
BF16 = jnp.bfloat16

NDEV = 8
EPS = 1e-6
CHUNK = 128
A_GROUPS = 8
N_HEADS = 16
N_KV_HEADS = 4
Q_PER_KV = N_HEADS // N_KV_HEADS
HEAD_DIM = 64
BLOCK = 128
N_BUCKETS = 32
MAX_DISTANCE = 128
ADAM_LR, ADAM_B1, ADAM_B2, ADAM_EPS, ADAM_WD, ADAM_STEP = 0.001, 0.9, 0.999, 1e-08, 0.01, 10
LANES = 128
VMEM_LIMIT = 56 * 1024 * 1024
INV_SQRT2 = 0.7071067811865476
INV_SQRT_2PI = 0.3989422804014327
MESH_AXES = ("x", "y", "c")

HBM = pl.BlockSpec(memory_space=pltpu.HBM)
SMEM = pl.BlockSpec(memory_space=pltpu.SMEM)


def _params(n_grid):
    return pltpu.CompilerParams(dimension_semantics=("arbitrary",) * n_grid, vmem_limit_bytes=VMEM_LIMIT)


def _rstd(h):
    return lax.rsqrt(jnp.mean(h * h, axis=-1, keepdims=True) + EPS)


def _rms_bwd(dhn, h, g, dres):
    r = _rstd(h)
    xh = h * r
    dg = jnp.sum(dhn * xh, axis=0, keepdims=True)
    dxh = dhn * g
    dx = r * (dxh - xh * jnp.mean(dxh * xh, axis=-1, keepdims=True))
    return dres + dx, dg


def _gelu(z):
    return 0.5 * z * (1.0 + lax.erf(z * INV_SQRT2))


def _gelu_grad(z):
    return 0.5 * (1.0 + lax.erf(z * INV_SQRT2)) + z * (jnp.exp(-0.5 * z * z) * INV_SQRT_2PI)


def _dot(a, b, dims):
    return lax.dot_general(a, b, (dims, ((), ())), preferred_element_type=F32)


NN = ((1,), (0,))
NT = ((1,), (1,))
TN = ((0,), (0,))


def _mm(name, ins, in_specs, out_shapes, out_specs, *, grid, dims, nk, acc_shape, load_a, load_b, epilogue,
        aliases=None):
    n_in, n_out = len(ins), len(out_shapes)
    kax = len(grid) - 1

    def body(*refs):
        in_refs = refs[:n_in]
        out_refs = refs[n_in:n_in + n_out]
        a = load_a(in_refs, out_refs)
        b = load_b(in_refs)
        prod = _dot(a, b, dims)
        if nk == 1:
            epilogue(prod, in_refs, out_refs)
        else:
            acc = refs[n_in + n_out]
            k = pl.program_id(kax)

            @pl.when(k == 0)
            def _():
                acc[...] = prod

            @pl.when(k > 0)
            def _():
                acc[...] += prod

            @pl.when(k == nk - 1)
            def _():
                epilogue(acc[...], in_refs, out_refs)

    return pl.pallas_call(
        body, name=name, grid=grid, in_specs=in_specs, out_specs=out_specs, out_shape=out_shapes,
        scratch_shapes=[pltpu.VMEM(acc_shape, F32)] if nk > 1 else [],
        input_output_aliases=aliases or {}, compiler_params=_params(len(grid)),
    )(*ins)


def _bf(ref_idx):
    return lambda in_refs, *_: in_refs[ref_idx][...].astype(BF16)


def _b_view(ref_idx, rows):
    def load(in_refs):
        b = in_refs[ref_idx][...]
        return b.reshape(rows, b.shape[-1])
    return load


class Cfg:
    pass


def _config(x, a_w_in, a_w_out, w_k, b_w_q, b_w_o, ffn_w1, ffn_w2):
    c = Cfg()
    c.S, c.D = x.shape[1], x.shape[2]
    c.LA, _, c.cw = a_w_in.shape
    c.AW2 = NDEV * c.cw
    c.AW = c.AW2 // 2
    c.gd = c.AW // A_GROUPS
    c.ar = a_w_out.shape[1]
    c.LF, _, c.fw = ffn_w1.shape
    c.fr = ffn_w2.shape[1]
    c.LB, c.qr, c.DQ = b_w_q.shape
    c.orr = b_w_o.shape[1]
    c.kr, c.DKV = w_k.shape
    c.tm = min(512, c.S)
    c.tms = min(256, c.S)
    c.nb = c.S // BLOCK
    assert c.cw == c.fw == c.fr and c.AW == NDEV * c.ar and c.D == NDEV * c.qr == NDEV * c.kr
    assert c.DQ == NDEV * c.orr == N_HEADS * HEAD_DIM and c.DKV == N_KV_HEADS * HEAD_DIM
    assert c.S % c.tm == 0 and c.S % c.tms == 0 and c.tms % CHUNK == 0 and c.gd % LANES == 0
    c.off_w2, c.off_wout = 0, c.LF * c.fr
    c.off_wo = c.off_wout + c.LA * c.ar
    c.Rr = c.off_wo + c.LB * c.orr
    assert c.off_wout % c.ar == 0 and c.off_wo % c.orr == 0
    return c


def _cached_rms(h_idx, g_idx, hn_out_idx, jax_axis=1):
    def load(in_refs, out_refs):
        hn_ref = out_refs[hn_out_idx]

        @pl.when(pl.program_id(jax_axis) == 0)
        def _():
            h = in_refs[h_idx][...]
            hn_ref[...] = (h * _rstd(h) * in_refs[g_idx][...]).astype(BF16)

        return hn_ref[...]
    return load


def _a_in_fwd(c, h, g, wc_slab, l):
    S, D, cw, tm = c.S, c.D, c.cw, c.tm

    def epilogue(z, in_refs, out_refs):
        out_refs[0][...] = z.astype(BF16)
        out_refs[1][...] = _gelu(z).astype(BF16)

    return _mm(
        f"a_in_fwd{l}", [h, wc_slab, g],
        [pl.BlockSpec((tm, D), lambda i, j, k: (i, 0)),
         pl.BlockSpec((None, None, D, cw), lambda i, j, k: (j, l, 0, 0)),
         pl.BlockSpec((1, D), lambda i, j, k: (0, 0))],
        [jax.ShapeDtypeStruct((S, c.AW2), BF16), jax.ShapeDtypeStruct((S, c.AW2), BF16),
         jax.ShapeDtypeStruct((S, D), BF16)],
        [pl.BlockSpec((tm, cw), lambda i, j, k: (i, j)), pl.BlockSpec((tm, cw), lambda i, j, k: (i, j)),
         pl.BlockSpec((tm, D), lambda i, j, k: (i, 0))],
        grid=(S // tm, NDEV, 1), dims=NN, nk=1, acc_shape=None,
        load_a=_cached_rms(0, 2, 2), load_b=lambda r: r[1][...], epilogue=epilogue)


def _rms_mm_rows(c, name, h, g, slab, blk_rows, blk_idx, n_out):
    S, D, tm = c.S, c.D, c.tm

    def epilogue(acc, in_refs, out_refs):
        out_refs[0][...] = acc.astype(BF16)

    return _mm(
        name, [h, slab, g],
        [pl.BlockSpec((tm, D), lambda i, j, k: (i, 0)),
         pl.BlockSpec((NDEV, blk_rows, n_out), lambda i, j, k: (0, blk_idx, 0)),
         pl.BlockSpec((1, D), lambda i, j, k: (0, 0))],
        [jax.ShapeDtypeStruct((S, n_out), BF16), jax.ShapeDtypeStruct((S, D), BF16)],
        [pl.BlockSpec((tm, n_out), lambda i, j, k: (i, 0)), pl.BlockSpec((tm, D), lambda i, j, k: (i, 0))],
        grid=(S // tm, 1, 1), dims=NN, nk=1, acc_shape=None,
        load_a=_cached_rms(0, 2, 1), load_b=_b_view(1, NDEV * blk_rows), epilogue=epilogue)


def _mm_res(c, name, a, slab, blk_rows, blk_idx, res):
    S, D, tm = c.S, c.D, c.tm
    K = NDEV * blk_rows

    def epilogue(acc, in_refs, out_refs):
        out_refs[0][...] = in_refs[2][...] + acc

    return _mm(
        name, [a, slab, res],
        [pl.BlockSpec((tm, K), lambda i, j, k: (i, 0)),
         pl.BlockSpec((NDEV, blk_rows, D), lambda i, j, k: (0, blk_idx, 0)),
         pl.BlockSpec((tm, D), lambda i, j, k: (i, 0))],
        [jax.ShapeDtypeStruct((S, D), F32)], [pl.BlockSpec((tm, D), lambda i, j, k: (i, 0))],
        grid=(S // tm, 1, 1), dims=NN, nk=1, acc_shape=None,
        load_a=_bf(0), load_b=_b_view(1, K), epilogue=epilogue)[0]


def _sgu_masks():
    ii = lax.broadcasted_iota(jnp.int32, (CHUNK, CHUNK), 0)
    jj = lax.broadcasted_iota(jnp.int32, (CHUNK, CHUNK), 1)
    return ii >= jj


def _sgu_fwd(c, l, a, ln_g, wc, b_t):
    S, AW, gd, tm = c.S, c.AW, c.gd, c.tms

    def body(a_ref, lng_ref, wc_ref, bt_ref, out_ref):
        va = a_ref[:, AW:].astype(F32)
        xc = va - jnp.mean(va, axis=-1, keepdims=True)
        vn = (xc * lax.rsqrt(jnp.mean(xc * xc, axis=-1, keepdims=True) + EPS) * lng_ref[...]).astype(BF16)
        for ch in range(tm // CHUNK):
            rows = slice(ch * CHUNK, (ch + 1) * CHUNK)
            for g in range(A_GROUPS):
                cols = slice(g * gd, (g + 1) * gd)
                mixed = _dot(wc_ref[g], vn[rows, cols], NN) + bt_ref[:, g:g + 1]
                out_ref[rows, cols] = (a_ref[rows, cols].astype(F32) * mixed).astype(BF16)

    return pl.pallas_call(
        body, name=f"sgu_fwd{l}", grid=(S // tm,),
        in_specs=[pl.BlockSpec((tm, 2 * AW), lambda i: (i, 0)), pl.BlockSpec((1, AW), lambda i: (0, 0)),
                  pl.BlockSpec((A_GROUPS, CHUNK, CHUNK), lambda i: (0, 0, 0)),
                  pl.BlockSpec((CHUNK, A_GROUPS), lambda i: (0, 0))],
        out_specs=pl.BlockSpec((tm, AW), lambda i: (i, 0)),
        out_shape=jax.ShapeDtypeStruct((S, AW), BF16), compiler_params=_params(1),
    )(a, ln_g, wc, b_t)


def _ffn_fwd(c, l, h, g, wc_slab, wr_slab):
    S, D, fw, tm = c.S, c.D, c.fw, c.tm
    lc = c.LA + l
    w2_blk = c.off_w2 // c.fr + l

    def body(h_ref, g_ref, w1_ref, w2_ref, p_ref, out_ref, hn_ref, acc_ref):
        j = pl.program_id(1)

        @pl.when(j == 0)
        def _():
            h = h_ref[...]
            hn_ref[...] = (h * _rstd(h) * g_ref[...]).astype(BF16)

        p = jnp.maximum(_dot(hn_ref[...], w1_ref[...], NN), 0.0)
        p_ref[...] = p.astype(BF16)
        contrib = _dot((p * p).astype(BF16), w2_ref[...], NN)

        @pl.when(j == 0)
        def _():
            acc_ref[...] = contrib

        @pl.when(j > 0)
        def _():
            acc_ref[...] += contrib

        @pl.when(j == NDEV - 1)
        def _():
            out_ref[...] = h_ref[...] + acc_ref[...]

    return pl.pallas_call(
        body, name=f"ffn_fwd{l}", grid=(S // tm, NDEV),
        in_specs=[pl.BlockSpec((tm, D), lambda i, j: (i, 0)), pl.BlockSpec((1, D), lambda i, j: (0, 0)),
                  pl.BlockSpec((None, None, D, fw), lambda i, j: (j, lc, 0, 0)),
                  pl.BlockSpec((None, c.fr, D), lambda i, j: (j, w2_blk, 0))],
        out_specs=[pl.BlockSpec((tm, fw), lambda i, j: (i, j)), pl.BlockSpec((tm, D), lambda i, j: (i, 0)),
                   pl.BlockSpec((tm, D), lambda i, j: (i, 0))],
        out_shape=[jax.ShapeDtypeStruct((S, NDEV * fw), BF16), jax.ShapeDtypeStruct((S, D), F32),
                   jax.ShapeDtypeStruct((S, D), BF16)],
        scratch_shapes=[pltpu.VMEM((tm, D), F32)], compiler_params=_params(2),
    )(h, g, wc_slab, wr_slab)


def _bucket_table():
    qi = np.arange(BLOCK)[:, None]
    kj = np.arange(2 * BLOCK)[None, :]
    d = np.maximum(qi + BLOCK - kj, 0)
    max_exact = N_BUCKETS // 2
    ratio = np.log(np.maximum(d, 1).astype(np.float32) / np.float32(max_exact)) / np.float32(
        math.log(MAX_DISTANCE / max_exact))
    large = np.minimum(max_exact + (ratio.astype(np.float32) * np.float32(N_BUCKETS - max_exact)).astype(np.int32),
                       N_BUCKETS - 1)
    return np.where(d < max_exact, d, large).astype(np.int32)


def _bucket_onehot():
    b = jnp.asarray(_bucket_table().reshape(1, -1))
    return (b == lax.broadcasted_iota(jnp.int32, (N_BUCKETS, b.shape[1]), 0)).astype(F32)


def _band_bias(rel_bias_t, onehot):
    def body(r_ref, oh_ref, out_ref):
        out_ref[...] = lax.dot_general(r_ref[...], oh_ref[...], (NN, ((), ())), preferred_element_type=F32,
                                       precision=lax.Precision.HIGHEST)

    n = onehot.shape[1]
    return pl.pallas_call(body, name="band_bias", out_shape=jax.ShapeDtypeStruct((N_HEADS, n), F32),
                          compiler_params=_params(0))(rel_bias_t, onehot)


def _band_bias_grad(dbias_list, onehot):
    n_in = len(dbias_list)

    def body(*refs):
        oh_ref, out_ref = refs[n_in], refs[n_in + 1]
        d = refs[0][...]
        for r in refs[1:n_in]:
            d = d + r[...]
        out_ref[...] = lax.dot_general(d, oh_ref[...], (NT, ((), ())), preferred_element_type=F32,
                                       precision=lax.Precision.HIGHEST)

    return pl.pallas_call(body, name="band_bias_grad", out_shape=jax.ShapeDtypeStruct((N_HEADS, N_BUCKETS), F32),
                          compiler_params=_params(0))(*dbias_list, onehot)


def _attn_probs(n, q_ref, kvc_ref, kvp_ref, bias_ref, sink_ref, kh, dkv):
    lo, hi = kh * HEAD_DIM, (kh + 1) * HEAD_DIM
    kb = jnp.concatenate([kvp_ref[:, lo:hi], kvc_ref[:, lo:hi]], axis=0)
    vb = jnp.concatenate([kvp_ref[:, dkv + lo:dkv + hi], kvc_ref[:, dkv + lo:dkv + hi]], axis=0)
    heads = [kh * Q_PER_KV + g for g in range(Q_PER_KV)]
    q4 = jnp.concatenate([q_ref[:, h * HEAD_DIM:(h + 1) * HEAD_DIM] for h in heads], axis=0)
    qi = lax.broadcasted_iota(jnp.int32, (BLOCK, 2 * BLOCK), 0)
    kj = lax.broadcasted_iota(jnp.int32, (BLOCK, 2 * BLOCK), 1)
    dist = qi + BLOCK - kj
    mask = (dist >= 0) & (dist < BLOCK) & ((kj >= BLOCK) | (n > 0))
    mask4 = jnp.concatenate([mask] * Q_PER_KV, axis=0)
    bias4 = bias_ref[kh * Q_PER_KV:(kh + 1) * Q_PER_KV].reshape(Q_PER_KV * BLOCK, 2 * BLOCK)
    s = _dot(q4, kb, NT) * (HEAD_DIM ** -0.5) + bias4
    s = jnp.where(mask4, s, jnp.finfo(F32).min)
    sink4 = jnp.concatenate([jnp.full((BLOCK, 1), sink_ref[h], F32) for h in heads], axis=0)
    m = jnp.maximum(jnp.max(s, axis=-1, keepdims=True), sink4)
    e = jnp.exp(s - m)
    es = jnp.exp(sink4 - m)
    den = jnp.sum(e, axis=-1, keepdims=True) + es
    return e / den, es / den, q4, kb, vb, heads


def _attn_specs(c):
    dq, dkv2 = c.DQ, 2 * c.DKV
    return [pl.BlockSpec((BLOCK, dq), lambda n: (n, 0)),
            pl.BlockSpec((BLOCK, dkv2), lambda n: (n, 0)),
            pl.BlockSpec((BLOCK, dkv2), lambda n: (jnp.maximum(n - 1, 0), 0))]


def _attn_fwd(c, l, q, kv, bias, sinks):
    S, dq = c.S, c.DQ

    def body(q_ref, kvc_ref, kvp_ref, bias_ref, sink_ref, o_ref):
        n = pl.program_id(0)
        for kh in range(N_KV_HEADS):
            p, _, _, _, vb, heads = _attn_probs(n, q_ref, kvc_ref, kvp_ref, bias_ref, sink_ref, kh, c.DKV)
            o4 = _dot(p.astype(BF16), vb, NN)
            for g, h in enumerate(heads):
                o_ref[:, h * HEAD_DIM:(h + 1) * HEAD_DIM] = o4[g * BLOCK:(g + 1) * BLOCK].astype(BF16)

    return pl.pallas_call(
        body, name=f"attn_fwd{l}", grid=(c.nb,),
        in_specs=_attn_specs(c) + [pl.BlockSpec((N_HEADS, BLOCK, 2 * BLOCK), lambda n: (0, 0, 0)), SMEM],
        out_specs=pl.BlockSpec((BLOCK, dq), lambda n: (n, 0)),
        out_shape=jax.ShapeDtypeStruct((S, dq), BF16), compiler_params=_params(1),
    )(q, kv, kv, bias, sinks)


def _final_loss(c, h, g, target):
    S, D, tm = c.S, c.D, c.tm

    def body(h_ref, g_ref, t_ref, dh_ref, dg_ref, loss_ref):
        i = pl.program_id(0)
        h = h_ref[...]
        gg = g_ref[...]
        r = _rstd(h)
        xh = h * r
        err = xh * gg - t_ref[...]
        lp = jnp.sum(jnp.sum(err * err, axis=1, keepdims=True), axis=0, keepdims=True) * (0.5 / D)
        dx, dg = _rms_bwd(err * (1.0 / D), h, gg, 0.0)
        dh_ref[...] = dx

        @pl.when(i == 0)
        def _():
            dg_ref[...] = dg
            loss_ref[...] = jnp.broadcast_to(lp, loss_ref.shape)

        @pl.when(i > 0)
        def _():
            dg_ref[...] += dg
            loss_ref[...] += jnp.broadcast_to(lp, loss_ref.shape)

    row = pl.BlockSpec((tm, D), lambda i: (i, 0))
    return pl.pallas_call(
        body, name="final_loss", grid=(S // tm,),
        in_specs=[row, pl.BlockSpec((1, D), lambda i: (0, 0)), row],
        out_specs=[row, pl.BlockSpec((1, D), lambda i: (0, 0)), pl.BlockSpec((1, LANES), lambda i: (0, 0))],
        out_shape=[jax.ShapeDtypeStruct((S, D), F32), jax.ShapeDtypeStruct((1, D), F32),
                   jax.ShapeDtypeStruct((1, LANES), F32)],
        compiler_params=_params(1),
    )(h, g, target)


def _rms_bwd_epilogue(h_idx, g_idx, res_idx):
    def epilogue(dhn, in_refs, out_refs):
        dh, dg = _rms_bwd(dhn, in_refs[h_idx][...], in_refs[g_idx][...], in_refs[res_idx][...])
        out_refs[0][...] = dh
        i = pl.program_id(0)

        @pl.when(i == 0)
        def _():
            out_refs[1][...] = dg

        @pl.when(i > 0)
        def _():
            out_refs[1][...] += dg
    return epilogue


def _stream_outs(c):
    S, D, tm = c.S, c.D, c.tm
    return ([jax.ShapeDtypeStruct((S, D), F32), jax.ShapeDtypeStruct((1, D), F32)],
            [pl.BlockSpec((tm, D), lambda i, j, k: (i, 0)), pl.BlockSpec((1, D), lambda i, j, k: (0, 0))])


def _row_specs(c):
    tm, D = c.tm, c.D
    return [pl.BlockSpec((tm, D), lambda i, j, k: (i, 0)), pl.BlockSpec((1, D), lambda i, j, k: (0, 0)),
            pl.BlockSpec((tm, D), lambda i, j, k: (i, 0))]


def _bwd_rows_to_stream(c, name, dy_list, slab, blk_rows, blk_idx, n_in_cols, h, g, dres):
    S, D, tm = c.S, c.D, c.tm
    nd = len(dy_list)

    def load_a(in_refs, out_refs):
        a = in_refs[0][...]
        for r in in_refs[1:nd]:
            a = a + r[...]
        return a.astype(BF16)

    shapes, specs = _stream_outs(c)
    return _mm(
        name, [*dy_list, slab, h, g, dres],
        [pl.BlockSpec((tm, n_in_cols), lambda i, j, k: (i, 0))] * nd
        + [pl.BlockSpec((NDEV, blk_rows, n_in_cols), lambda i, j, k: (0, blk_idx, 0))] + _row_specs(c),
        shapes, specs, grid=(S // tm, 1, 1), dims=NT, nk=1, acc_shape=None,
        load_a=load_a, load_b=_b_view(nd, NDEV * blk_rows), epilogue=_rms_bwd_epilogue(nd + 1, nd + 2, nd + 3))


def _bwd_cols_to_stream(c, name, dy, wc_slab, l, h, g, dres):
    S, D, cw, tm = c.S, c.D, c.cw, c.tm
    shapes, specs = _stream_outs(c)
    return _mm(
        name, [dy, wc_slab, h, g, dres],
        [pl.BlockSpec((tm, cw), lambda i, j, k: (i, k)),
         pl.BlockSpec((None, None, D, cw), lambda i, j, k: (k, l, 0, 0))] + _row_specs(c),
        shapes, specs, grid=(S // tm, 1, NDEV), dims=NT, nk=NDEV, acc_shape=(tm, D),
        load_a=_bf(0), load_b=lambda r: r[1][...], epilogue=_rms_bwd_epilogue(2, 3, 4))


def _bwd_rows_data(c, name, dy, slab, blk_rows, blk_idx):
    S, D, tm = c.S, c.D, c.tm
    K = NDEV * blk_rows

    def epilogue(acc, in_refs, out_refs):
        out_refs[0][...] = acc.astype(BF16)

    return _mm(
        name, [dy, slab],
        [pl.BlockSpec((tm, D), lambda i, j, k: (i, 0)),
         pl.BlockSpec((NDEV, blk_rows, D), lambda i, j, k: (0, blk_idx, 0))],
        [jax.ShapeDtypeStruct((S, K), BF16)], [pl.BlockSpec((tm, K), lambda i, j, k: (i, 0))],
        grid=(S // tm, 1, 1), dims=NT, nk=1, acc_shape=None,
        load_a=_bf(0), load_b=_b_view(1, K), epilogue=epilogue)[0]


def _wgrad_rows(c, name, a, b_list, n_a, n_b, slab, blk_rows, blk_idx):
    S, tm = c.S, c.tm
    nb_in = len(b_list)
    aliased = slab is not None

    def load_b(in_refs):
        b = in_refs[1][...]
        for r in in_refs[2:1 + nb_in]:
            b = b + r[...]
        return b.astype(BF16)

    def epilogue(acc, in_refs, out_refs):
        out_refs[0][...] = acc.reshape(NDEV, blk_rows, n_b).astype(BF16)

    ins = [a, *b_list] + ([slab] if aliased else [])
    in_specs = ([pl.BlockSpec((tm, n_a), lambda i, j, k: (k, 0))]
                + [pl.BlockSpec((tm, n_b), lambda i, j, k: (k, 0))] * nb_in + ([HBM] if aliased else []))
    out_shape = jax.ShapeDtypeStruct(slab.shape if aliased else (NDEV, blk_rows, n_b), BF16)
    return _mm(
        name, ins, in_specs, [out_shape],
        [pl.BlockSpec((NDEV, blk_rows, n_b), lambda i, j, k: (0, blk_idx, 0))],
        grid=(1, 1, S // tm), dims=TN, nk=S // tm, acc_shape=(n_a, n_b),
        load_a=_bf(0), load_b=load_b, epilogue=epilogue,
        aliases={1 + nb_in: 0} if aliased else None)[0]


def _wgrad_cols(c, name, a, b, slab, l):
    S, D, cw, tm = c.S, c.D, c.cw, c.tm

    def epilogue(acc, in_refs, out_refs):
        out_refs[0][...] = acc.astype(BF16)

    return _mm(
        name, [a, b, slab],
        [pl.BlockSpec((tm, D), lambda j, i, k: (k, 0)), pl.BlockSpec((tm, cw), lambda j, i, k: (k, j)), HBM],
        [jax.ShapeDtypeStruct(slab.shape, BF16)],
        [pl.BlockSpec((None, None, D, cw), lambda j, i, k: (j, l, 0, 0))],
        grid=(NDEV, 1, S // tm), dims=TN, nk=S // tm, acc_shape=(D, cw),
        load_a=_bf(0), load_b=_bf(1), epilogue=epilogue, aliases={2: 0})[0]


def _ffn_bwd_data(c, l, dh, p, wc_slab, wr_slab, h, g):
    S, D, fw, tm = c.S, c.D, c.fw, c.tm
    lc = c.LA + l
    w2_blk = c.off_w2 // c.fr + l

    def body(dh_ref, p_ref, w1_ref, w2_ref, h_ref, g_ref, da_ref, out_ref, dg_ref, acc_ref, dhb_ref):
        i, j = pl.program_id(0), pl.program_id(1)

        @pl.when(j == 0)
        def _():
            dhb_ref[...] = dh_ref[...].astype(BF16)

        da = (_dot(dhb_ref[...], w2_ref[...], NT) * (2.0 * p_ref[...].astype(F32))).astype(BF16)
        da_ref[...] = da
        contrib = _dot(da, w1_ref[...], NT)

        @pl.when(j == 0)
        def _():
            acc_ref[...] = contrib

        @pl.when(j > 0)
        def _():
            acc_ref[...] += contrib

        @pl.when(j == NDEV - 1)
        def _():
            dx, dg = _rms_bwd(acc_ref[...], h_ref[...], g_ref[...], dh_ref[...])
            out_ref[...] = dx

            @pl.when(i == 0)
            def _():
                dg_ref[...] = dg

            @pl.when(i > 0)
            def _():
                dg_ref[...] += dg

    row = pl.BlockSpec((tm, D), lambda i, j: (i, 0))
    return pl.pallas_call(
        body, name=f"ffn_bwd_data{l}", grid=(S // tm, NDEV),
        in_specs=[row, pl.BlockSpec((tm, fw), lambda i, j: (i, j)),
                  pl.BlockSpec((None, None, D, fw), lambda i, j: (j, lc, 0, 0)),
                  pl.BlockSpec((None, c.fr, D), lambda i, j: (j, w2_blk, 0)),
                  row, pl.BlockSpec((1, D), lambda i, j: (0, 0))],
        out_specs=[pl.BlockSpec((tm, fw), lambda i, j: (i, j)), row, pl.BlockSpec((1, D), lambda i, j: (0, 0))],
        out_shape=[jax.ShapeDtypeStruct((S, NDEV * fw), BF16), jax.ShapeDtypeStruct((S, D), F32),
                   jax.ShapeDtypeStruct((1, D), F32)],
        scratch_shapes=[pltpu.VMEM((tm, D), F32), pltpu.VMEM((tm, D), BF16)], compiler_params=_params(2),
    )(dh, p, wc_slab, wr_slab, h, g)


def _ffn_bwd_w(c, l, hn, da, p, dh, dwc_slab, dwr_slab):
    S, D, fw, tm = c.S, c.D, c.fw, c.tm
    lc = c.LA + l
    w2_blk = c.off_w2 // c.fr + l
    ni = S // tm

    def body(hn_ref, da_ref, p_ref, dh_ref, _c, _r, dw1_ref, dw2_ref, acc1, acc2):
        i = pl.program_id(1)
        c1 = _dot(hn_ref[...], da_ref[...], TN)
        pf = p_ref[...].astype(F32)
        c2 = _dot((pf * pf).astype(BF16), dh_ref[...].astype(BF16), TN)

        @pl.when(i == 0)
        def _():
            acc1[...] = c1
            acc2[...] = c2

        @pl.when(i > 0)
        def _():
            acc1[...] += c1
            acc2[...] += c2

        @pl.when(i == ni - 1)
        def _():
            dw1_ref[...] = acc1[...].astype(BF16)
            dw2_ref[...] = acc2[...].astype(BF16)

    return pl.pallas_call(
        body, name=f"ffn_bwd_w{l}", grid=(NDEV, ni),
        in_specs=[pl.BlockSpec((tm, D), lambda j, i: (i, 0)), pl.BlockSpec((tm, fw), lambda j, i: (i, j)),
                  pl.BlockSpec((tm, fw), lambda j, i: (i, j)), pl.BlockSpec((tm, D), lambda j, i: (i, 0)), HBM, HBM],
        out_specs=[pl.BlockSpec((None, None, D, fw), lambda j, i: (j, lc, 0, 0)),
                   pl.BlockSpec((None, c.fr, D), lambda j, i: (j, w2_blk, 0))],
        out_shape=[jax.ShapeDtypeStruct(dwc_slab.shape, BF16), jax.ShapeDtypeStruct(dwr_slab.shape, BF16)],
        scratch_shapes=[pltpu.VMEM((D, fw), F32), pltpu.VMEM((c.fr, D), F32)],
        input_output_aliases={4: 0, 5: 1}, compiler_params=_params(2),
    )(hn, da, p, dh, dwc_slab, dwr_slab)


def _attn_bwd(c, l, q, kv, do, bias, sinks):
    S, dq, dkv = c.S, c.DQ, c.DKV
    nb = c.nb

    def body(q_ref, kvc_ref, kvp_ref, do_ref, bias_ref, sink_ref, dq_ref, dkv_ref, dbias_ref, dsink_ref, dsink_acc):
        n = pl.program_id(0)

        @pl.when(n == 0)
        def _():
            dkv_ref[...] = jnp.zeros_like(dkv_ref)
            dbias_ref[...] = jnp.zeros_like(dbias_ref)
            dsink_acc[...] = jnp.zeros_like(dsink_acc)

        rows_c = pl.ds(pl.multiple_of(n * BLOCK, BLOCK), BLOCK)
        rows_p = pl.ds(pl.multiple_of(jnp.maximum(n - 1, 0) * BLOCK, BLOCK), BLOCK)
        for kh in range(N_KV_HEADS):
            p, ps, q4, kb, vb, heads = _attn_probs(n, q_ref, kvc_ref, kvp_ref, bias_ref, sink_ref, kh, dkv)
            do4 = jnp.concatenate([do_ref[:, h * HEAD_DIM:(h + 1) * HEAD_DIM] for h in heads], axis=0)
            dp = _dot(do4, vb, NT)
            delta = jnp.sum(p * dp, axis=-1, keepdims=True)
            ds = p * (dp - delta)
            dbias_ref[kh * Q_PER_KV:(kh + 1) * Q_PER_KV] += ds.reshape(Q_PER_KV, BLOCK, 2 * BLOCK)
            dcol = -(ps * delta)
            ds16 = (ds * (HEAD_DIM ** -0.5)).astype(BF16)
            dq4 = _dot(ds16, kb, NN)
            for g, h in enumerate(heads):
                dsink_acc[:, h:h + 1] += dcol[g * BLOCK:(g + 1) * BLOCK]
                dq_ref[:, h * HEAD_DIM:(h + 1) * HEAD_DIM] = dq4[g * BLOCK:(g + 1) * BLOCK].astype(BF16)
            dkb = _dot(ds16, q4, TN)
            dvb = _dot(p.astype(BF16), do4, TN)
            lo, hi = kh * HEAD_DIM, (kh + 1) * HEAD_DIM
            dkv_ref[rows_p, lo:hi] += dkb[:BLOCK]
            dkv_ref[rows_c, lo:hi] += dkb[BLOCK:]
            dkv_ref[rows_p, dkv + lo:dkv + hi] += dvb[:BLOCK]
            dkv_ref[rows_c, dkv + lo:dkv + hi] += dvb[BLOCK:]

        @pl.when(n == nb - 1)
        def _():
            dsink_ref[...] = jnp.sum(dsink_acc[...], axis=0, keepdims=True)

    return pl.pallas_call(
        body, name=f"attn_bwd{l}", grid=(nb,),
        in_specs=_attn_specs(c) + [pl.BlockSpec((BLOCK, dq), lambda n: (n, 0)),
                                   pl.BlockSpec((N_HEADS, BLOCK, 2 * BLOCK), lambda n: (0, 0, 0)), SMEM],
        out_specs=[pl.BlockSpec((BLOCK, dq), lambda n: (n, 0)), pl.BlockSpec((S, 2 * dkv), lambda n: (0, 0)),
                   pl.BlockSpec((N_HEADS, BLOCK, 2 * BLOCK), lambda n: (0, 0, 0)),
                   pl.BlockSpec((1, N_HEADS), lambda n: (0, 0))],
        out_shape=[jax.ShapeDtypeStruct((S, dq), BF16), jax.ShapeDtypeStruct((S, 2 * dkv), F32),
                   jax.ShapeDtypeStruct((N_HEADS, BLOCK, 2 * BLOCK), F32), jax.ShapeDtypeStruct((1, N_HEADS), F32)],
        scratch_shapes=[pltpu.VMEM((BLOCK, N_HEADS), F32)], compiler_params=_params(1),
    )(q, kv, kv, do, bias, sinks)


def _sgu_bwd(c, l, a, z, dgated, ln_g, wc, wc_t, b_t):
    S, AW, gd, tm = c.S, c.AW, c.gd, c.tms

    def body(a_ref, z_ref, dg_ref, lng_ref, wc_ref, wct_ref, bt_ref, dz_ref, dws_ref, dbt_ref, dlng_ref, dvn_ref):
        i = pl.program_id(0)

        @pl.when(i == 0)
        def _():
            dws_ref[...] = jnp.zeros_like(dws_ref)
            dbt_ref[...] = jnp.zeros_like(dbt_ref)
            dlng_ref[...] = jnp.zeros_like(dlng_ref)

        lng = lng_ref[...]
        va = a_ref[:, AW:].astype(F32)
        xc = va - jnp.mean(va, axis=-1, keepdims=True)
        rstd = lax.rsqrt(jnp.mean(xc * xc, axis=-1, keepdims=True) + EPS)
        xh = xc * rstd
        vn = (xh * lng).astype(BF16)
        causal = _sgu_masks()
        for ch in range(tm // CHUNK):
            rows = slice(ch * CHUNK, (ch + 1) * CHUNK)
            for g in range(A_GROUPS):
                cols = slice(g * gd, (g + 1) * gd)
                blk = vn[rows, cols]
                mixed = _dot(wc_ref[g], blk, NN) + bt_ref[:, g:g + 1]
                dgb = dg_ref[rows, cols].astype(F32)
                dm = dgb * a_ref[rows, cols].astype(F32)
                dbt_ref[:, g:g + 1] += jnp.sum(dm, axis=1, keepdims=True)
                dm16 = dm.astype(BF16)
                dws_ref[g] += jnp.where(causal, _dot(dm16, blk, NT), 0.0)
                dvn_ref[rows, cols] = _dot(wct_ref[g], dm16, NN)
                dz_ref[rows, cols] = (dgb * mixed * _gelu_grad(z_ref[rows, cols].astype(F32))).astype(BF16)
        dvn = dvn_ref[...]
        dlng_ref[...] += jnp.sum(dvn * xh, axis=0, keepdims=True)
        dxh = dvn * lng
        dva = rstd * (dxh - jnp.mean(dxh, axis=-1, keepdims=True) - xh * jnp.mean(dxh * xh, axis=-1, keepdims=True))
        dz_ref[:, AW:] = (dva * _gelu_grad(z_ref[:, AW:].astype(F32))).astype(BF16)

    wide = pl.BlockSpec((tm, 2 * AW), lambda i: (i, 0))
    wsp = pl.BlockSpec((A_GROUPS, CHUNK, CHUNK), lambda i: (0, 0, 0))
    btsp = pl.BlockSpec((CHUNK, A_GROUPS), lambda i: (0, 0))
    return pl.pallas_call(
        body, name=f"sgu_bwd{l}", grid=(S // tm,),
        in_specs=[wide, wide, pl.BlockSpec((tm, AW), lambda i: (i, 0)), pl.BlockSpec((1, AW), lambda i: (0, 0)),
                  wsp, wsp, btsp],
        out_specs=[wide, wsp, btsp, pl.BlockSpec((1, AW), lambda i: (0, 0))],
        out_shape=[jax.ShapeDtypeStruct((S, 2 * AW), BF16), jax.ShapeDtypeStruct((A_GROUPS, CHUNK, CHUNK), F32),
                   jax.ShapeDtypeStruct((CHUNK, A_GROUPS), F32), jax.ShapeDtypeStruct((1, AW), F32)],
        scratch_shapes=[pltpu.VMEM((tm, AW), F32)], compiler_params=_params(1),
    )(a, z, dgated, ln_g, wc, wc_t, b_t)


def _adamw(name, parts, part_block, part_index, w, m, v, tr):
    R, C = w.shape
    bc1 = 1.0 - ADAM_B1 ** ADAM_STEP
    bc2 = 1.0 - ADAM_B2 ** ADAM_STEP

    def body(p_ref, w_ref, m_ref, v_ref, g_ref, d_ref, nm_ref, nv_ref):
        g = p_ref[0].astype(F32)
        for s in range(1, NDEV):
            g = g + p_ref[s].astype(F32)
        nm = ADAM_B1 * m_ref[...] + (1.0 - ADAM_B1) * g
        nv = ADAM_B2 * v_ref[...] + (1.0 - ADAM_B2) * (g * g)
        g_ref[...] = g
        nm_ref[...] = nm
        nv_ref[...] = nv
        d_ref[...] = -ADAM_LR * ((nm / bc1) / (jnp.sqrt(nv / bc2) + ADAM_EPS) + ADAM_WD * w_ref[...])

    row = pl.BlockSpec((tr, C), lambda i: (i, 0))
    out = jax.ShapeDtypeStruct((R, C), F32)
    return pl.pallas_call(
        body, name=name, grid=(R // tr,),
        in_specs=[pl.BlockSpec(part_block, part_index), row, row, row],
        out_specs=[row, row, row, row], out_shape=[out, out, out, out], compiler_params=_params(1),
    )(parts, w, m, v)


def _place():
    return lax.axis_index("x"), lax.axis_index("y"), lax.axis_index("c")


def _slot(px, py, pc):
    return 4 * px + 2 * py + pc


def _all_gather(name, arrs):
    n = len(arrs)

    def body(*refs):
        in_refs, out_refs = refs[:n], refs[n:2 * n]
        send_sems, recv_sems, local_sems = refs[2 * n:]
        x, y, c = _place()
        me, sibling = (x, y, c), (x, y, 1 - c)
        chips = [(1 - x, y), (x, 1 - y), (1 - x, 1 - y)]

        def copy(a, k, block, to, src=None):
            dst = out_refs[a].at[_slot(*block)]
            return pltpu.make_async_remote_copy(
                src_ref=dst if src is None else src, dst_ref=dst, send_sem=send_sems.at[a, k],
                recv_sem=recv_sems.at[a, k], device_id=to, device_id_type=pl.DeviceIdType.MESH)

        mine = [pltpu.make_async_copy(in_refs[a], out_refs[a].at[_slot(*me)], local_sems.at[a]) for a in range(n)]
        for cp in mine:
            cp.start()
        first = []
        for a in range(n):
            first.append(copy(a, 0, me, sibling, src=in_refs[a]))
            first += [copy(a, 1 + j, me, (*chip, c), src=in_refs[a]) for j, chip in enumerate(chips)]
        for cp in first:
            cp.start()
        passed = []
        for j, chip in enumerate(chips):
            for a in range(n):
                copy(a, 1 + j, (*chip, c), me).wait_recv()
                fwd = copy(a, 4 + j, (*chip, c), sibling)
                fwd.start()
                passed.append(fwd)
        for a in range(n):
            copy(a, 0, sibling, me).wait_recv()
            for j, chip in enumerate(chips):
                copy(a, 4 + j, (*chip, 1 - c), me).wait_recv()
        for cp in first + passed:
            cp.wait_send()
        for cp in mine:
            cp.wait()

    return pl.pallas_call(
        body, name=name,
        out_shape=[jax.ShapeDtypeStruct((NDEV, *a.shape), a.dtype) for a in arrs],
        in_specs=[HBM] * n, out_specs=[HBM] * n,
        scratch_shapes=[pltpu.SemaphoreType.DMA((n, 7)), pltpu.SemaphoreType.DMA((n, 7)),
                        pltpu.SemaphoreType.DMA((n,))],
        compiler_params=pltpu.CompilerParams(has_side_effects=True),
    )(*arrs)


def _exchange(name, slabs):
    n = len(slabs)

    def body(*refs):
        in_refs, out_refs = refs[:n], refs[n:2 * n]
        send_sems, recv_sems, local_sems = refs[2 * n:]
        x, y, c = _place()
        mine = [pltpu.make_async_copy(in_refs[a].at[_slot(x, y, c)], out_refs[a].at[_slot(x, y, c)], local_sems.at[a])
                for a in range(n)]
        for cp in mine:
            cp.start()
        copies = []
        for k in range(1, NDEV):
            px, py, pc = x ^ ((k >> 2) & 1), y ^ ((k >> 1) & 1), c ^ (k & 1)
            for a in range(n):
                copies.append(pltpu.make_async_remote_copy(
                    src_ref=in_refs[a].at[_slot(px, py, pc)], dst_ref=out_refs[a].at[_slot(x, y, c)],
                    send_sem=send_sems.at[a, k - 1], recv_sem=recv_sems.at[a, k - 1],
                    device_id=(px, py, pc), device_id_type=pl.DeviceIdType.MESH))
        for cp in copies:
            cp.start()
        for cp in copies:
            cp.wait_recv()
        for cp in copies:
            cp.wait_send()
        for cp in mine:
            cp.wait()

    return pl.pallas_call(
        body, name=name,
        out_shape=[jax.ShapeDtypeStruct(s.shape, s.dtype) for s in slabs],
        in_specs=[HBM] * n, out_specs=[HBM] * n,
        scratch_shapes=[pltpu.SemaphoreType.DMA((n, 7)), pltpu.SemaphoreType.DMA((n, 7)),
                        pltpu.SemaphoreType.DMA((n,))],
        compiler_params=pltpu.CompilerParams(has_side_effects=True),
    )(*slabs)


def _rows128(t):
    flat = t.reshape(-1)
    n = flat.shape[0]
    rows = -(-n // (8 * LANES)) * 8
    return jnp.pad(flat, (0, rows * LANES - n)).reshape(rows, LANES)


def kernel(x, mix_norm_g, ffn_norm_g, a_w_in, a_ln_g, a_w_spatial, a_b_spatial, a_w_out, kv_norm_g, w_k, w_v, b_w_q, b_sinks, b_w_o, rel_bias, ffn_w1, ffn_w2, final_norm_g, loss_target, m_mix_norm_g, m_ffn_norm_g, m_a_w_in, m_a_ln_g, m_a_w_spatial, m_a_b_spatial, m_a_w_out, m_kv_norm_g, m_w_k, m_w_v, m_b_w_q, m_b_sinks, m_b_w_o, m_rel_bias, m_ffn_w1, m_ffn_w2, m_final_norm_g, v_mix_norm_g, v_ffn_norm_g, v_a_w_in, v_a_ln_g, v_a_w_spatial, v_a_b_spatial, v_a_w_out, v_kv_norm_g, v_w_k, v_w_v, v_b_w_q, v_b_sinks, v_b_w_o, v_rel_bias, v_ffn_w1, v_ffn_w2, v_final_norm_g):
    c = _config(x, a_w_in, a_w_out, w_k, b_w_q, b_w_o, ffn_w1, ffn_w2)
    S, D, LA, LB, LF = c.S, c.D, c.LA, c.LB, c.LF
    weights = dict(mix_norm_g=mix_norm_g, ffn_norm_g=ffn_norm_g, a_w_in=a_w_in, a_ln_g=a_ln_g, a_w_spatial=a_w_spatial,
                   a_b_spatial=a_b_spatial, a_w_out=a_w_out, kv_norm_g=kv_norm_g, w_k=w_k, w_v=w_v, b_w_q=b_w_q,
                   b_sinks=b_sinks, b_w_o=b_w_o, rel_bias=rel_bias, ffn_w1=ffn_w1, ffn_w2=ffn_w2,
                   final_norm_g=final_norm_g)
    m_in = dict(mix_norm_g=m_mix_norm_g, ffn_norm_g=m_ffn_norm_g, a_w_in=m_a_w_in, a_ln_g=m_a_ln_g,
                a_w_spatial=m_a_w_spatial, a_b_spatial=m_a_b_spatial, a_w_out=m_a_w_out, kv_norm_g=m_kv_norm_g,
                w_k=m_w_k, w_v=m_w_v, b_w_q=m_b_w_q, b_sinks=m_b_sinks, b_w_o=m_b_w_o, rel_bias=m_rel_bias,
                ffn_w1=m_ffn_w1, ffn_w2=m_ffn_w2, final_norm_g=m_final_norm_g)
    v_in = dict(mix_norm_g=v_mix_norm_g, ffn_norm_g=v_ffn_norm_g, a_w_in=v_a_w_in, a_ln_g=v_a_ln_g,
                a_w_spatial=v_a_w_spatial, a_b_spatial=v_a_b_spatial, a_w_out=v_a_w_out, kv_norm_g=v_kv_norm_g,
                w_k=v_w_k, w_v=v_w_v, b_w_q=v_b_w_q, b_sinks=v_b_sinks, b_w_o=v_b_w_o, rel_bias=v_rel_bias,
                ffn_w1=v_ffn_w1, ffn_w2=v_ffn_w2, final_norm_g=v_final_norm_g)
    names = list(weights)

    wc_loc = jnp.concatenate([a_w_in, ffn_w1], axis=0).astype(BF16)
    wr_loc = jnp.concatenate([ffn_w2.reshape(LF * c.fr, D), a_w_out.reshape(LA * c.ar, D),
                              b_w_o.reshape(LB * c.orr, D)], axis=0).astype(BF16)
    wq_loc = b_w_q.reshape(LB * c.qr, c.DQ).astype(BF16)
    wkv_loc = jnp.concatenate([w_k, w_v], axis=1).astype(BF16)
    wc_slab, wr_slab, wq_slab, wkv_slab, lng_all = _all_gather(
        "gather_weights", [wc_loc, wr_loc, wq_loc, wkv_loc, a_ln_g])
    ln_g_full = jnp.transpose(lng_all, (1, 0, 2)).reshape(LA, 1, c.AW)

    causal = jnp.tril(jnp.ones((CHUNK, CHUNK), bool))
    wsp = jnp.where(causal[None, None], a_w_spatial, 0.0)
    wsp16 = wsp.astype(BF16)
    wsp16_t = jnp.swapaxes(wsp, -1, -2).astype(BF16)
    bsp_t = jnp.swapaxes(a_b_spatial, -1, -2)
    mix_g = mix_norm_g.reshape(-1, 1, D)
    ffn_g = ffn_norm_g.reshape(-1, 1, D)
    kv_g = kv_norm_g.reshape(1, D)
    fin_g = final_norm_g.reshape(1, D)
    onehot = _bucket_onehot()
    bias = _band_bias(rel_bias.T, onehot).reshape(N_HEADS, BLOCK, 2 * BLOCK)

    h = x.reshape(S, D)
    sav_a, sav_b = [], []
    for l in range(LA):
        z, a, hn = _a_in_fwd(c, h, mix_g[l], wc_slab, l)
        gated = _sgu_fwd(c, l, a, ln_g_full[l], wsp16[l], bsp_t[l])
        h1 = _mm_res(c, f"a_out_fwd{l}", gated, wr_slab, c.ar, c.off_wout // c.ar + l, h)
        p, h2, hnf = _ffn_fwd(c, l, h1, ffn_g[l], wc_slab, wr_slab)
        sav_a.append((h, z, a, hn, gated, h1, p, hnf))
        h = h2
    h_kv = h
    kv, hkv = _rms_mm_rows(c, "kv_fwd", h, kv_g, wkv_slab, c.kr, 0, 2 * c.DKV)
    for l in range(LB):
        q, hn = _rms_mm_rows(c, f"q_fwd{l}", h, mix_g[LA + l], wq_slab, c.qr, l, c.DQ)
        o = _attn_fwd(c, l, q, kv, bias, b_sinks[l])
        h1 = _mm_res(c, f"o_fwd{l}", o, wr_slab, c.orr, c.off_wo // c.orr + l, h)
        p, h2, hnf = _ffn_fwd(c, LA + l, h1, ffn_g[LA + l], wc_slab, wr_slab)
        sav_b.append((h, q, hn, o, h1, p, hnf))
        h = h2
    dh, d_fin_g, loss_row = _final_loss(c, h, fin_g, loss_target.reshape(S, D))
    loss = lax.psum(loss_row[0, 0], MESH_AXES)

    dwc = lax.empty(wc_slab.shape, BF16)
    dwr = lax.empty(wr_slab.shape, BF16)
    dwq = lax.empty(wq_slab.shape, BF16)
    d_mix_g, d_ffn_g = [None] * (LA + LB), [None] * (LA + LB)
    dkv_list, dbias_list, dsink_list = [], [], [None] * LB
    for l in reversed(range(LB)):
        h0, q, hn, o, h1, p, hnf = sav_b[l]
        da, dh1, d_ffn_g[LA + l] = _ffn_bwd_data(c, LA + l, dh, p, wc_slab, wr_slab, h1, ffn_g[LA + l])
        dwc, dwr = _ffn_bwd_w(c, LA + l, hnf, da, p, dh, dwc, dwr)
        wo_blk = c.off_wo // c.orr + l
        do = _bwd_rows_data(c, f"o_bwd_data{l}", dh1, wr_slab, c.orr, wo_blk)
        dwr = _wgrad_rows(c, f"o_bwd_w{l}", o, [dh1], c.DQ, D, dwr, c.orr, wo_blk)
        dq, dkv, dbias, dsink_list[l] = _attn_bwd(c, l, q, kv, do, bias, b_sinks[l])
        dkv_list.append(dkv)
        dbias_list.append(dbias)
        dwq = _wgrad_rows(c, f"q_bwd_w{l}", hn, [dq], D, c.DQ, dwq, c.qr, l)
        dh, d_mix_g[LA + l] = _bwd_rows_to_stream(c, f"q_bwd_data{l}", [dq], wq_slab, c.qr, l, c.DQ, h0,
                                                  mix_g[LA + l], dh1)
    dwkv = _wgrad_rows(c, "kv_bwd_w", hkv, dkv_list, D, 2 * c.DKV, None, c.kr, 0)
    dh, d_kv_g = _bwd_rows_to_stream(c, "kv_bwd_data", dkv_list, wkv_slab, c.kr, 0, 2 * c.DKV, h_kv, kv_g, dh)
    d_rel_t = _band_bias_grad([d.reshape(N_HEADS, -1) for d in dbias_list], onehot)
    d_wsp, d_bsp, d_lng = [None] * LA, [None] * LA, [None] * LA
    for l in reversed(range(LA)):
        h0, z, a, hn, gated, h1, p, hnf = sav_a[l]
        da, dh1, d_ffn_g[l] = _ffn_bwd_data(c, l, dh, p, wc_slab, wr_slab, h1, ffn_g[l])
        dwc, dwr = _ffn_bwd_w(c, l, hnf, da, p, dh, dwc, dwr)
        wout_blk = c.off_wout // c.ar + l
        dgated = _bwd_rows_data(c, f"a_out_bwd_data{l}", dh1, wr_slab, c.ar, wout_blk)
        dwr = _wgrad_rows(c, f"a_out_bwd_w{l}", gated, [dh1], c.AW, D, dwr, c.ar, wout_blk)
        dz, d_wsp[l], dbt, d_lng[l] = _sgu_bwd(c, l, a, z, dgated, ln_g_full[l], wsp16[l], wsp16_t[l], bsp_t[l])
        d_bsp[l] = dbt.T
        dwc = _wgrad_cols(c, f"a_in_bwd_w{l}", hn, dz, dwc, l)
        dh, d_mix_g[l] = _bwd_cols_to_stream(c, f"a_in_bwd_data{l}", dz, wc_slab, l, h0, mix_g[l], dh1)
    grad_x = dh.reshape(1, S, D)

    rc, rr, rq, rkv = _exchange("exchange_grads", [dwc, dwr, dwq, dwkv])
    small = {
        "mix_norm_g": jnp.concatenate(d_mix_g, axis=0), "ffn_norm_g": jnp.concatenate(d_ffn_g, axis=0),
        "a_w_spatial": jnp.stack(d_wsp), "a_b_spatial": jnp.stack(d_bsp), "kv_norm_g": d_kv_g,
        "b_sinks": jnp.concatenate(dsink_list, axis=0), "rel_bias": d_rel_t.T, "final_norm_g": d_fin_g,
    }
    small_names = list(small)
    packs = [_rows128(small[k]) for k in small_names] + [_rows128(jnp.concatenate(d_lng, axis=0))]
    offs = np.cumsum([0] + [p.shape[0] for p in packs])
    (small_all,) = _all_gather("gather_small_grads", [jnp.concatenate(packs, axis=0)])
    Rs = int(offs[-1])

    grads, deltas, new_m, new_v = {}, {}, {}, {}

    def put(k, outs, shape):
        grads[k], deltas[k], new_m[k], new_v[k] = (t.reshape(shape) for t in outs)

    def pack_state(d):
        return jnp.concatenate([_rows128(d[k]) for k in small_names]
                               + [jnp.zeros((packs[-1].shape[0], LANES), F32)], axis=0)

    outs = _adamw("adamw_small", small_all, (NDEV, Rs, LANES), lambda i: (0, 0, 0),
                  pack_state(weights), pack_state(m_in), pack_state(v_in), Rs)
    for n_, k in enumerate(small_names):
        shape = weights[k].shape
        size = int(np.prod(shape))
        put(k, [t[int(offs[n_]):int(offs[n_ + 1])].reshape(-1)[:size] for t in outs], shape)
    me = _slot(*_place())
    lng_sum = outs[0][int(offs[-2]):int(offs[-1])].reshape(-1)[:LA * c.AW].reshape(LA, c.AW)
    lng_mine = lax.dynamic_slice_in_dim(lng_sum, me * c.ar, c.ar, axis=1)
    lng_parts = jnp.concatenate([lng_mine[None], jnp.zeros((NDEV - 1, LA, c.ar), F32)], axis=0)
    put("a_ln_g", _adamw("adamw_ln_g", lng_parts, (NDEV, LA, c.ar), lambda i: (0, 0, 0),
                         a_ln_g, m_in["a_ln_g"], v_in["a_ln_g"], LA), a_ln_g.shape)

    def sharded(k, parts, rows_off, ncols, col_blk=0):
        w2 = weights[k].reshape(-1, ncols)
        R = w2.shape[0]
        tr = min(256, R)
        assert R % tr == 0 and rows_off % tr == 0
        view = parts.reshape(NDEV, -1, parts.shape[-1])
        put(k, _adamw(f"adamw_{k}", view, (NDEV, tr, ncols), lambda i: (0, rows_off // tr + i, col_blk),
                      w2, m_in[k].reshape(-1, ncols), v_in[k].reshape(-1, ncols), tr), weights[k].shape)

    sharded("a_w_in", rc, 0, c.cw)
    sharded("ffn_w1", rc, LA * D, c.cw)
    sharded("ffn_w2", rr, c.off_w2, D)
    sharded("a_w_out", rr, c.off_wout, D)
    sharded("b_w_o", rr, c.off_wo, D)
    sharded("b_w_q", rq, 0, c.DQ)
    sharded("w_k", rkv, 0, c.DKV, 0)
    sharded("w_v", rkv, 0, c.DKV, 1)

    return (loss, grad_x, *[grads[k] for k in names], *[deltas[k] for k in names],
            *[new_m[k] for k in names], *[new_v[k] for k in names])
```

```python
import numpy as np
import math
import jax
import jax.numpy as jnp
from jax import lax
from jax.experimental import pallas as pl
from jax.experimental.pallas import tpu as pltpu

F32 = jnp.float32
BF16 = jnp.bfloat16

NDEV = 8
EPS = 1e-6
CHUNK = 128
A_GROUPS = 8
N_HEADS = 16
N_KV_HEADS = 4
Q_PER_KV = N_HEADS // N_KV_HEADS
HEAD_DIM = 64
BLOCK = 128
N_BUCKETS = 32
MAX_DISTANCE = 128
ADAM_LR, ADAM_B1, ADAM_B2, ADAM_EPS, ADAM_WD, ADAM_STEP = 0.001, 0.9, 0.999, 1e-08, 0.01, 10
LANES = 128
VMEM_LIMIT = 56 * 1024 * 1024
INV_SQRT2 = 0.7071067811865476
INV_SQRT_2PI = 0.3989422804014327
MESH_AXES = ("x", "y", "c")
EXCHANGE_LAG = 2

HBM = pl.BlockSpec(memory_space=pltpu.HBM)
SMEM = pl.BlockSpec(memory_space=pltpu.SMEM)
ANY = pl.BlockSpec(memory_space=pl.ANY)
SEM = pl.BlockSpec(memory_space=pltpu.SEMAPHORE)
MESH = pl.DeviceIdType.MESH


def _params(n_grid):
    return pltpu.CompilerParams(dimension_semantics=("arbitrary",) * n_grid, vmem_limit_bytes=VMEM_LIMIT)


def _pcall(body, *, ins, in_specs, dep=None, **kw):
    n_in = len(ins)
    if dep is None:
        return pl.pallas_call(body, in_specs=list(in_specs), **kw)(*ins)

    def with_dep(*refs):
        body(*refs[:n_in], *refs[n_in + 1:])

    return pl.pallas_call(with_dep, in_specs=[*in_specs, ANY], **kw)(*ins, dep)


class _Seq:
    def __init__(self):
        self.last = None

    def __call__(self, fn, *args, **kw):
        out = fn(*args, dep=self.last, **kw)
        self.last = out[0] if isinstance(out, (tuple, list)) else out
        return out


def _rstd(h):
    return lax.rsqrt(jnp.mean(h * h, axis=-1, keepdims=True) + EPS)


def _rms_bwd(dhn, h, g, dres):
    r = _rstd(h)
    xh = h * r
    dg = jnp.sum(dhn * xh, axis=0, keepdims=True)
    dxh = dhn * g
    dx = r * (dxh - xh * jnp.mean(dxh * xh, axis=-1, keepdims=True))
    return dres + dx, dg


def _gelu(z):
    return 0.5 * z * (1.0 + lax.erf(z * INV_SQRT2))


def _gelu_grad(z):
    return 0.5 * (1.0 + lax.erf(z * INV_SQRT2)) + z * (jnp.exp(-0.5 * z * z) * INV_SQRT_2PI)


def _dot(a, b, dims):
    return lax.dot_general(a, b, (dims, ((), ())), preferred_element_type=F32)


NN = ((1,), (0,))
NT = ((1,), (1,))
TN = ((0,), (0,))


def _mm(name, ins, in_specs, out_shapes, out_specs, *, grid, dims, nk, acc_shape, load_a, load_b, epilogue,
        dep=None):
    n_in, n_out = len(ins), len(out_shapes)
    kax = len(grid) - 1

    def body(*refs):
        in_refs = refs[:n_in]
        out_refs = refs[n_in:n_in + n_out]
        a = load_a(in_refs, out_refs)
        b = load_b(in_refs)
        prod = _dot(a, b, dims)
        if nk == 1:
            epilogue(prod, in_refs, out_refs)
        else:
            acc = refs[n_in + n_out]
            k = pl.program_id(kax)

            @pl.when(k == 0)
            def _():
                acc[...] = prod

            @pl.when(k > 0)
            def _():
                acc[...] += prod

            @pl.when(k == nk - 1)
            def _():
                epilogue(acc[...], in_refs, out_refs)

    return _pcall(
        body, name=name, ins=ins, in_specs=in_specs, dep=dep, grid=grid, out_specs=out_specs, out_shape=out_shapes,
        scratch_shapes=[pltpu.VMEM(acc_shape, F32)] if nk > 1 else [], compiler_params=_params(len(grid)))


def _bf(ref_idx):
    return lambda in_refs, *_: in_refs[ref_idx][...].astype(BF16)


def _b_view(ref_idx, rows):
    def load(in_refs):
        b = in_refs[ref_idx][...]
        return b.reshape(rows, b.shape[-1])
    return load


class Cfg:
    pass


def _config(x, a_w_in, a_w_out, w_k, b_w_q, b_w_o, ffn_w1, ffn_w2):
    c = Cfg()
    c.S, c.D = x.shape[1], x.shape[2]
    c.LA, _, c.cw = a_w_in.shape
    c.AW2 = NDEV * c.cw
    c.AW = c.AW2 // 2
    c.gd = c.AW // A_GROUPS
    c.ar = a_w_out.shape[1]
    c.LF, _, c.fw = ffn_w1.shape
    c.fr = ffn_w2.shape[1]
    c.LB, c.qr, c.DQ = b_w_q.shape
    c.orr = b_w_o.shape[1]
    c.kr, c.DKV = w_k.shape
    c.tm = min(512, c.S)
    c.tms = min(256, c.S)
    c.nb = c.S // BLOCK
    assert c.cw == c.fw == c.fr and c.AW == NDEV * c.ar and c.D == NDEV * c.qr == NDEV * c.kr
    assert c.DQ == NDEV * c.orr == N_HEADS * HEAD_DIM and c.DKV == N_KV_HEADS * HEAD_DIM
    assert c.S % c.tm == 0 and c.S % c.tms == 0 and c.tms % CHUNK == 0 and c.gd % LANES == 0
    assert c.fr % c.ar == 0 and c.fr % c.orr == 0
    assert c.LA >= 1 and c.LB >= 1 and c.LF == c.LA + c.LB
    return c


def _cached_rms(h_idx, g_idx, hn_out_idx, jax_axis=1):
    def load(in_refs, out_refs):
        hn_ref = out_refs[hn_out_idx]

        @pl.when(pl.program_id(jax_axis) == 0)
        def _():
            h = in_refs[h_idx][...]
            hn_ref[...] = (h * _rstd(h) * in_refs[g_idx][...]).astype(BF16)

        return hn_ref[...]
    return load


def _a_in_fwd(c, name, h, g, col, ci, dep=None):
    S, D, cw, tm = c.S, c.D, c.cw, c.tm

    def epilogue(z, in_refs, out_refs):
        out_refs[0][...] = z.astype(BF16)
        out_refs[1][...] = _gelu(z).astype(BF16)

    return _mm(
        name, [h, col, g],
        [pl.BlockSpec((tm, D), lambda i, j, k: (i, 0)),
         pl.BlockSpec((None, None, D, cw), lambda i, j, k: (j, ci, 0, 0)),
         pl.BlockSpec((1, D), lambda i, j, k: (0, 0))],
        [jax.ShapeDtypeStruct((S, c.AW2), BF16), jax.ShapeDtypeStruct((S, c.AW2), BF16),
         jax.ShapeDtypeStruct((S, D), BF16)],
        [pl.BlockSpec((tm, cw), lambda i, j, k: (i, j)), pl.BlockSpec((tm, cw), lambda i, j, k: (i, j)),
         pl.BlockSpec((tm, D), lambda i, j, k: (i, 0))],
        grid=(S // tm, NDEV, 1), dims=NN, nk=1, acc_shape=None,
        load_a=_cached_rms(0, 2, 2), load_b=lambda r: r[1][...], epilogue=epilogue, dep=dep)


def _rms_mm_rows(c, name, h, g, slab, blk_rows, blk_idx, n_out, dep=None):
    S, D, tm = c.S, c.D, c.tm

    def epilogue(acc, in_refs, out_refs):
        out_refs[0][...] = acc.astype(BF16)

    return _mm(
        name, [h, slab, g],
        [pl.BlockSpec((tm, D), lambda i, j, k: (i, 0)),
         pl.BlockSpec((NDEV, blk_rows, n_out), lambda i, j, k: (0, blk_idx, 0)),
         pl.BlockSpec((1, D), lambda i, j, k: (0, 0))],
        [jax.ShapeDtypeStruct((S, n_out), BF16), jax.ShapeDtypeStruct((S, D), BF16)],
        [pl.BlockSpec((tm, n_out), lambda i, j, k: (i, 0)), pl.BlockSpec((tm, D), lambda i, j, k: (i, 0))],
        grid=(S // tm, 1, 1), dims=NN, nk=1, acc_shape=None,
        load_a=_cached_rms(0, 2, 1), load_b=_b_view(1, NDEV * blk_rows), epilogue=epilogue, dep=dep)


def _mm_res(c, name, a, slab, blk_rows, blk_idx, res, dep=None):
    S, D, tm = c.S, c.D, c.tm
    K = NDEV * blk_rows

    def epilogue(acc, in_refs, out_refs):
        out_refs[0][...] = in_refs[2][...] + acc

    return _mm(
        name, [a, slab, res],
        [pl.BlockSpec((tm, K), lambda i, j, k: (i, 0)),
         pl.BlockSpec((NDEV, blk_rows, D), lambda i, j, k: (0, blk_idx, 0)),
         pl.BlockSpec((tm, D), lambda i, j, k: (i, 0))],
        [jax.ShapeDtypeStruct((S, D), F32)], [pl.BlockSpec((tm, D), lambda i, j, k: (i, 0))],
        grid=(S // tm, 1, 1), dims=NN, nk=1, acc_shape=None,
        load_a=_bf(0), load_b=_b_view(1, K), epilogue=epilogue, dep=dep)[0]


def _sgu_masks():
    ii = lax.broadcasted_iota(jnp.int32, (CHUNK, CHUNK), 0)
    jj = lax.broadcasted_iota(jnp.int32, (CHUNK, CHUNK), 1)
    return ii >= jj


def _sgu_fwd(c, name, a, ln_g, wc, b_t, dep=None):
    S, AW, gd, tm = c.S, c.AW, c.gd, c.tms

    def body(a_ref, lng_ref, wc_ref, bt_ref, out_ref):
        va = a_ref[:, AW:].astype(F32)
        xc = va - jnp.mean(va, axis=-1, keepdims=True)
        vn = (xc * lax.rsqrt(jnp.mean(xc * xc, axis=-1, keepdims=True) + EPS) * lng_ref[...]).astype(BF16)
        for ch in range(tm // CHUNK):
            rows = slice(ch * CHUNK, (ch + 1) * CHUNK)
            for g in range(A_GROUPS):
                cols = slice(g * gd, (g + 1) * gd)
                mixed = _dot(wc_ref[g], vn[rows, cols], NN) + bt_ref[:, g:g + 1]
                out_ref[rows, cols] = (a_ref[rows, cols].astype(F32) * mixed).astype(BF16)

    return _pcall(
        body, name=name, ins=[a, ln_g, wc, b_t], dep=dep, grid=(S // tm,),
        in_specs=[pl.BlockSpec((tm, 2 * AW), lambda i: (i, 0)), pl.BlockSpec((1, AW), lambda i: (0, 0)),
                  pl.BlockSpec((A_GROUPS, CHUNK, CHUNK), lambda i: (0, 0, 0)),
                  pl.BlockSpec((CHUNK, A_GROUPS), lambda i: (0, 0))],
        out_specs=pl.BlockSpec((tm, AW), lambda i: (i, 0)),
        out_shape=jax.ShapeDtypeStruct((S, AW), BF16), compiler_params=_params(1))


def _ffn_fwd(c, name, h, g, col, ci, rows, dep=None):
    S, D, fw, tm = c.S, c.D, c.fw, c.tm

    def body(h_ref, g_ref, w1_ref, w2_ref, p_ref, out_ref, hn_ref, acc_ref):
        j = pl.program_id(1)

        @pl.when(j == 0)
        def _():
            h = h_ref[...]
            hn_ref[...] = (h * _rstd(h) * g_ref[...]).astype(BF16)

        p = jnp.maximum(_dot(hn_ref[...], w1_ref[...], NN), 0.0)
        p_ref[...] = p.astype(BF16)
        contrib = _dot((p * p).astype(BF16), w2_ref[...], NN)

        @pl.when(j == 0)
        def _():
            acc_ref[...] = contrib

        @pl.when(j > 0)
        def _():
            acc_ref[...] += contrib

        @pl.when(j == NDEV - 1)
        def _():
            out_ref[...] = h_ref[...] + acc_ref[...]

    return _pcall(
        body, name=name, ins=[h, g, col, rows], dep=dep, grid=(S // tm, NDEV),
        in_specs=[pl.BlockSpec((tm, D), lambda i, j: (i, 0)), pl.BlockSpec((1, D), lambda i, j: (0, 0)),
                  pl.BlockSpec((None, None, D, fw), lambda i, j: (j, ci, 0, 0)),
                  pl.BlockSpec((None, c.fr, D), lambda i, j: (j, 0, 0))],
        out_specs=[pl.BlockSpec((tm, fw), lambda i, j: (i, j)), pl.BlockSpec((tm, D), lambda i, j: (i, 0)),
                   pl.BlockSpec((tm, D), lambda i, j: (i, 0))],
        out_shape=[jax.ShapeDtypeStruct((S, NDEV * fw), BF16), jax.ShapeDtypeStruct((S, D), F32),
                   jax.ShapeDtypeStruct((S, D), BF16)],
        scratch_shapes=[pltpu.VMEM((tm, D), F32)], compiler_params=_params(2))


def _bucket_table():
    qi = np.arange(BLOCK)[:, None]
    kj = np.arange(2 * BLOCK)[None, :]
    d = np.maximum(qi + BLOCK - kj, 0)
    max_exact = N_BUCKETS // 2
    ratio = np.log(np.maximum(d, 1).astype(np.float32) / np.float32(max_exact)) / np.float32(
        math.log(MAX_DISTANCE / max_exact))
    large = np.minimum(max_exact + (ratio.astype(np.float32) * np.float32(N_BUCKETS - max_exact)).astype(np.int32),
                       N_BUCKETS - 1)
    return np.where(d < max_exact, d, large).astype(np.int32)


def _bucket_onehot():
    b = jnp.asarray(_bucket_table().reshape(1, -1))
    return (b == lax.broadcasted_iota(jnp.int32, (N_BUCKETS, b.shape[1]), 0)).astype(F32)


def _whole(t):
    return pl.BlockSpec(t.shape, lambda: (0,) * t.ndim)


def _band_bias(rel_bias_t, onehot, dep=None):
    def body(r_ref, oh_ref, out_ref):
        out_ref[...] = lax.dot_general(r_ref[...], oh_ref[...], (NN, ((), ())), preferred_element_type=F32,
                                       precision=lax.Precision.HIGHEST)

    n = onehot.shape[1]
    return _pcall(body, name="band_bias", ins=[rel_bias_t, onehot], in_specs=[_whole(rel_bias_t), _whole(onehot)],
                  dep=dep, out_shape=jax.ShapeDtypeStruct((N_HEADS, n), F32), compiler_params=_params(0))


def _band_bias_grad(dbias_list, onehot, dep=None):
    n_in = len(dbias_list)

    def body(*refs):
        oh_ref, out_ref = refs[n_in], refs[n_in + 1]
        d = refs[0][...]
        for r in refs[1:n_in]:
            d = d + r[...]
        out_ref[...] = lax.dot_general(d, oh_ref[...], (NT, ((), ())), preferred_element_type=F32,
                                       precision=lax.Precision.HIGHEST)

    ins = [*dbias_list, onehot]
    return _pcall(body, name="band_bias_grad", ins=ins, in_specs=[_whole(t) for t in ins], dep=dep,
                  out_shape=jax.ShapeDtypeStruct((N_HEADS, N_BUCKETS), F32), compiler_params=_params(0))


def _attn_probs(n, q_ref, kvc_ref, kvp_ref, bias_ref, sink_ref, kh, dkv):
    lo, hi = kh * HEAD_DIM, (kh + 1) * HEAD_DIM
    kb = jnp.concatenate([kvp_ref[:, lo:hi], kvc_ref[:, lo:hi]], axis=0)
    vb = jnp.concatenate([kvp_ref[:, dkv + lo:dkv + hi], kvc_ref[:, dkv + lo:dkv + hi]], axis=0)
    heads = [kh * Q_PER_KV + g for g in range(Q_PER_KV)]
    q4 = jnp.concatenate([q_ref[:, h * HEAD_DIM:(h + 1) * HEAD_DIM] for h in heads], axis=0)
    qi = lax.broadcasted_iota(jnp.int32, (BLOCK, 2 * BLOCK), 0)
    kj = lax.broadcasted_iota(jnp.int32, (BLOCK, 2 * BLOCK), 1)
    dist = qi + BLOCK - kj
    mask = (dist >= 0) & (dist < BLOCK) & ((kj >= BLOCK) | (n > 0))
    mask4 = jnp.concatenate([mask] * Q_PER_KV, axis=0)
    bias4 = bias_ref[kh * Q_PER_KV:(kh + 1) * Q_PER_KV].reshape(Q_PER_KV * BLOCK, 2 * BLOCK)
    s = _dot(q4, kb, NT) * (HEAD_DIM ** -0.5) + bias4
    s = jnp.where(mask4, s, jnp.finfo(F32).min)
    sink4 = jnp.concatenate([jnp.full((BLOCK, 1), sink_ref[h], F32) for h in heads], axis=0)
    m = jnp.maximum(jnp.max(s, axis=-1, keepdims=True), sink4)
    e = jnp.exp(s - m)
    es = jnp.exp(sink4 - m)
    den = jnp.sum(e, axis=-1, keepdims=True) + es
    return e / den, es / den, q4, kb, vb, heads


def _attn_specs(c):
    dq, dkv2 = c.DQ, 2 * c.DKV
    return [pl.BlockSpec((BLOCK, dq), lambda n: (n, 0)),
            pl.BlockSpec((BLOCK, dkv2), lambda n: (n, 0)),
            pl.BlockSpec((BLOCK, dkv2), lambda n: (jnp.maximum(n - 1, 0), 0))]


def _attn_fwd(c, name, q, kv, bias, sinks, dep=None):
    S, dq = c.S, c.DQ

    def body(q_ref, kvc_ref, kvp_ref, bias_ref, sink_ref, o_ref):
        n = pl.program_id(0)
        for kh in range(N_KV_HEADS):
            p, _, _, _, vb, heads = _attn_probs(n, q_ref, kvc_ref, kvp_ref, bias_ref, sink_ref, kh, c.DKV)
            o4 = _dot(p.astype(BF16), vb, NN)
            for g, h in enumerate(heads):
                o_ref[:, h * HEAD_DIM:(h + 1) * HEAD_DIM] = o4[g * BLOCK:(g + 1) * BLOCK].astype(BF16)

    return _pcall(
        body, name=name, ins=[q, kv, kv, bias, sinks], dep=dep, grid=(c.nb,),
        in_specs=_attn_specs(c) + [pl.BlockSpec((N_HEADS, BLOCK, 2 * BLOCK), lambda n: (0, 0, 0)), SMEM],
        out_specs=pl.BlockSpec((BLOCK, dq), lambda n: (n, 0)),
        out_shape=jax.ShapeDtypeStruct((S, dq), BF16), compiler_params=_params(1))


def _final_loss(c, h, g, target, dep=None):
    S, D, tm = c.S, c.D, c.tm

    def body(h_ref, g_ref, t_ref, dh_ref, dg_ref, loss_ref):
        i = pl.program_id(0)
        h = h_ref[...]
        gg = g_ref[...]
        r = _rstd(h)
        xh = h * r
        err = xh * gg - t_ref[...]
        lp = jnp.sum(jnp.sum(err * err, axis=1, keepdims=True), axis=0, keepdims=True) * (0.5 / D)
        dx, dg = _rms_bwd(err * (1.0 / D), h, gg, 0.0)
        dh_ref[...] = dx

        @pl.when(i == 0)
        def _():
            dg_ref[...] = dg
            loss_ref[...] = jnp.broadcast_to(lp, loss_ref.shape)

        @pl.when(i > 0)
        def _():
            dg_ref[...] += dg
            loss_ref[...] += jnp.broadcast_to(lp, loss_ref.shape)

    row = pl.BlockSpec((tm, D), lambda i: (i, 0))
    return _pcall(
        body, name="final_loss", ins=[h, g, target], dep=dep, grid=(S // tm,),
        in_specs=[row, pl.BlockSpec((1, D), lambda i: (0, 0)), row],
        out_specs=[row, pl.BlockSpec((1, D), lambda i: (0, 0)), pl.BlockSpec((1, LANES), lambda i: (0, 0))],
        out_shape=[jax.ShapeDtypeStruct((S, D), F32), jax.ShapeDtypeStruct((1, D), F32),
                   jax.ShapeDtypeStruct((1, LANES), F32)],
        compiler_params=_params(1))


def _rms_bwd_epilogue(h_idx, g_idx, res_idx):
    def epilogue(dhn, in_refs, out_refs):
        dh, dg = _rms_bwd(dhn, in_refs[h_idx][...], in_refs[g_idx][...], in_refs[res_idx][...])
        out_refs[0][...] = dh
        i = pl.program_id(0)

        @pl.when(i == 0)
        def _():
            out_refs[1][...] = dg

        @pl.when(i > 0)
        def _():
            out_refs[1][...] += dg
    return epilogue


def _stream_outs(c):
    S, D, tm = c.S, c.D, c.tm
    return ([jax.ShapeDtypeStruct((S, D), F32), jax.ShapeDtypeStruct((1, D), F32)],
            [pl.BlockSpec((tm, D), lambda i, j, k: (i, 0)), pl.BlockSpec((1, D), lambda i, j, k: (0, 0))])


def _row_specs(c):
    tm, D = c.tm, c.D
    return [pl.BlockSpec((tm, D), lambda i, j, k: (i, 0)), pl.BlockSpec((1, D), lambda i, j, k: (0, 0)),
            pl.BlockSpec((tm, D), lambda i, j, k: (i, 0))]


def _bwd_rows_to_stream(c, name, dy_list, slab, blk_rows, blk_idx, n_in_cols, h, g, dres, dep=None):
    S, D, tm = c.S, c.D, c.tm
    nd = len(dy_list)

    def load_a(in_refs, out_refs):
        a = in_refs[0][...]
        for r in in_refs[1:nd]:
            a = a + r[...]
        return a.astype(BF16)

    shapes, specs = _stream_outs(c)
    return _mm(
        name, [*dy_list, slab, h, g, dres],
        [pl.BlockSpec((tm, n_in_cols), lambda i, j, k: (i, 0))] * nd
        + [pl.BlockSpec((NDEV, blk_rows, n_in_cols), lambda i, j, k: (0, blk_idx, 0))] + _row_specs(c),
        shapes, specs, grid=(S // tm, 1, 1), dims=NT, nk=1, acc_shape=None,
        load_a=load_a, load_b=_b_view(nd, NDEV * blk_rows), epilogue=_rms_bwd_epilogue(nd + 1, nd + 2, nd + 3),
        dep=dep)


def _bwd_cols_to_stream(c, name, dy, col, ci, h, g, dres, dep=None):
    S, D, cw, tm = c.S, c.D, c.cw, c.tm
    shapes, specs = _stream_outs(c)
    return _mm(
        name, [dy, col, h, g, dres],
        [pl.BlockSpec((tm, cw), lambda i, j, k: (i, k)),
         pl.BlockSpec((None, None, D, cw), lambda i, j, k: (k, ci, 0, 0))] + _row_specs(c),
        shapes, specs, grid=(S // tm, 1, NDEV), dims=NT, nk=NDEV, acc_shape=(tm, D),
        load_a=_bf(0), load_b=lambda r: r[1][...], epilogue=_rms_bwd_epilogue(2, 3, 4), dep=dep)


def _bwd_rows_data(c, name, dy, slab, blk_rows, blk_idx, dep=None):
    S, D, tm = c.S, c.D, c.tm
    K = NDEV * blk_rows

    def epilogue(acc, in_refs, out_refs):
        out_refs[0][...] = acc.astype(BF16)

    return _mm(
        name, [dy, slab],
        [pl.BlockSpec((tm, D), lambda i, j, k: (i, 0)),
         pl.BlockSpec((NDEV, blk_rows, D), lambda i, j, k: (0, blk_idx, 0))],
        [jax.ShapeDtypeStruct((S, K), BF16)], [pl.BlockSpec((tm, K), lambda i, j, k: (i, 0))],
        grid=(S // tm, 1, 1), dims=NT, nk=1, acc_shape=None,
        load_a=_bf(0), load_b=_b_view(1, K), epilogue=epilogue, dep=dep)[0]


def _wgrad_rows(c, name, a, b_list, n_a, n_b, dep=None):
    S, tm = c.S, c.tm
    nb_in = len(b_list)
    blk_rows = n_a // NDEV

    def load_b(in_refs):
        b = in_refs[1][...]
        for r in in_refs[2:1 + nb_in]:
            b = b + r[...]
        return b.astype(BF16)

    def epilogue(acc, in_refs, out_refs):
        out_refs[0][...] = acc.reshape(NDEV, blk_rows, n_b).astype(BF16)

    return _mm(
        name, [a, *b_list],
        [pl.BlockSpec((tm, n_a), lambda i, j, k: (k, 0))] + [pl.BlockSpec((tm, n_b), lambda i, j, k: (k, 0))] * nb_in,
        [jax.ShapeDtypeStruct((NDEV, blk_rows, n_b), BF16)],
        [pl.BlockSpec((NDEV, blk_rows, n_b), lambda i, j, k: (0, 0, 0))],
        grid=(1, 1, S // tm), dims=TN, nk=S // tm, acc_shape=(n_a, n_b),
        load_a=_bf(0), load_b=load_b, epilogue=epilogue, dep=dep)[0]


def _wgrad_cols(c, name, a, b, dep=None):
    S, D, cw, tm = c.S, c.D, c.cw, c.tm

    def epilogue(acc, in_refs, out_refs):
        out_refs[0][...] = acc.astype(BF16)

    return _mm(
        name, [a, b],
        [pl.BlockSpec((tm, D), lambda j, i, k: (k, 0)), pl.BlockSpec((tm, cw), lambda j, i, k: (k, j))],
        [jax.ShapeDtypeStruct((NDEV, D, cw), BF16)],
        [pl.BlockSpec((None, D, cw), lambda j, i, k: (j, 0, 0))],
        grid=(NDEV, 1, S // tm), dims=TN, nk=S // tm, acc_shape=(D, cw),
        load_a=_bf(0), load_b=_bf(1), epilogue=epilogue, dep=dep)[0]


def _ffn_bwd_data(c, name, dh, p, col, ci, rows, h, g, dep=None):
    S, D, fw, tm = c.S, c.D, c.fw, c.tm

    def body(dh_ref, p_ref, w1_ref, w2_ref, h_ref, g_ref, da_ref, out_ref, dg_ref, acc_ref, dhb_ref):
        i, j = pl.program_id(0), pl.program_id(1)

        @pl.when(j == 0)
        def _():
            dhb_ref[...] = dh_ref[...].astype(BF16)

        da = (_dot(dhb_ref[...], w2_ref[...], NT) * (2.0 * p_ref[...].astype(F32))).astype(BF16)
        da_ref[...] = da
        contrib = _dot(da, w1_ref[...], NT)

        @pl.when(j == 0)
        def _():
            acc_ref[...] = contrib

        @pl.when(j > 0)
        def _():
            acc_ref[...] += contrib

        @pl.when(j == NDEV - 1)
        def _():
            dx, dg = _rms_bwd(acc_ref[...], h_ref[...], g_ref[...], dh_ref[...])
            out_ref[...] = dx

            @pl.when(i == 0)
            def _():
                dg_ref[...] = dg

            @pl.when(i > 0)
            def _():
                dg_ref[...] += dg

    row = pl.BlockSpec((tm, D), lambda i, j: (i, 0))
    return _pcall(
        body, name=name, ins=[dh, p, col, rows, h, g], dep=dep, grid=(S // tm, NDEV),
        in_specs=[row, pl.BlockSpec((tm, fw), lambda i, j: (i, j)),
                  pl.BlockSpec((None, None, D, fw), lambda i, j: (j, ci, 0, 0)),
                  pl.BlockSpec((None, c.fr, D), lambda i, j: (j, 0, 0)),
                  row, pl.BlockSpec((1, D), lambda i, j: (0, 0))],
        out_specs=[pl.BlockSpec((tm, fw), lambda i, j: (i, j)), row, pl.BlockSpec((1, D), lambda i, j: (0, 0))],
        out_shape=[jax.ShapeDtypeStruct((S, NDEV * fw), BF16), jax.ShapeDtypeStruct((S, D), F32),
                   jax.ShapeDtypeStruct((1, D), F32)],
        scratch_shapes=[pltpu.VMEM((tm, D), F32), pltpu.VMEM((tm, D), BF16)], compiler_params=_params(2))


def _ffn_bwd_w(c, name, hn, da, p, dh, dep=None):
    S, D, fw, tm = c.S, c.D, c.fw, c.tm
    ni = S // tm

    def body(hn_ref, da_ref, p_ref, dh_ref, dw1_ref, dw2_ref, acc1, acc2):
        i = pl.program_id(1)
        c1 = _dot(hn_ref[...], da_ref[...], TN)
        pf = p_ref[...].astype(F32)
        c2 = _dot((pf * pf).astype(BF16), dh_ref[...].astype(BF16), TN)

        @pl.when(i == 0)
        def _():
            acc1[...] = c1
            acc2[...] = c2

        @pl.when(i > 0)
        def _():
            acc1[...] += c1
            acc2[...] += c2

        @pl.when(i == ni - 1)
        def _():
            dw1_ref[...] = acc1[...].astype(BF16)
            dw2_ref[...] = acc2[...].astype(BF16)

    return _pcall(
        body, name=name, ins=[hn, da, p, dh], dep=dep, grid=(NDEV, ni),
        in_specs=[pl.BlockSpec((tm, D), lambda j, i: (i, 0)), pl.BlockSpec((tm, fw), lambda j, i: (i, j)),
                  pl.BlockSpec((tm, fw), lambda j, i: (i, j)), pl.BlockSpec((tm, D), lambda j, i: (i, 0))],
        out_specs=[pl.BlockSpec((None, D, fw), lambda j, i: (j, 0, 0)),
                   pl.BlockSpec((None, c.fr, D), lambda j, i: (j, 0, 0))],
        out_shape=[jax.ShapeDtypeStruct((NDEV, D, fw), BF16), jax.ShapeDtypeStruct((NDEV, c.fr, D), BF16)],
        scratch_shapes=[pltpu.VMEM((D, fw), F32), pltpu.VMEM((c.fr, D), F32)], compiler_params=_params(2))


def _attn_bwd(c, name, q, kv, do, bias, sinks, dep=None):
    S, dq, dkv = c.S, c.DQ, c.DKV
    nb = c.nb

    def body(q_ref, kvc_ref, kvp_ref, do_ref, bias_ref, sink_ref, dq_ref, dkv_ref, dbias_ref, dsink_ref, dsink_acc):
        n = pl.program_id(0)

        @pl.when(n == 0)
        def _():
            dkv_ref[...] = jnp.zeros_like(dkv_ref)
            dbias_ref[...] = jnp.zeros_like(dbias_ref)
            dsink_acc[...] = jnp.zeros_like(dsink_acc)

        rows_c = pl.ds(pl.multiple_of(n * BLOCK, BLOCK), BLOCK)
        rows_p = pl.ds(pl.multiple_of(jnp.maximum(n - 1, 0) * BLOCK, BLOCK), BLOCK)
        for kh in range(N_KV_HEADS):
            p, ps, q4, kb, vb, heads = _attn_probs(n, q_ref, kvc_ref, kvp_ref, bias_ref, sink_ref, kh, dkv)
            do4 = jnp.concatenate([do_ref[:, h * HEAD_DIM:(h + 1) * HEAD_DIM] for h in heads], axis=0)
            dp = _dot(do4, vb, NT)
            delta = jnp.sum(p * dp, axis=-1, keepdims=True)
            ds = p * (dp - delta)
            dbias_ref[kh * Q_PER_KV:(kh + 1) * Q_PER_KV] += ds.reshape(Q_PER_KV, BLOCK, 2 * BLOCK)
            dcol = -(ps * delta)
            ds16 = (ds * (HEAD_DIM ** -0.5)).astype(BF16)
            dq4 = _dot(ds16, kb, NN)
            for g, h in enumerate(heads):
                dsink_acc[:, h:h + 1] += dcol[g * BLOCK:(g + 1) * BLOCK]
                dq_ref[:, h * HEAD_DIM:(h + 1) * HEAD_DIM] = dq4[g * BLOCK:(g + 1) * BLOCK].astype(BF16)
            dkb = _dot(ds16, q4, TN)
            dvb = _dot(p.astype(BF16), do4, TN)
            lo, hi = kh * HEAD_DIM, (kh + 1) * HEAD_DIM
            dkv_ref[rows_p, lo:hi] += dkb[:BLOCK]
            dkv_ref[rows_c, lo:hi] += dkb[BLOCK:]
            dkv_ref[rows_p, dkv + lo:dkv + hi] += dvb[:BLOCK]
            dkv_ref[rows_c, dkv + lo:dkv + hi] += dvb[BLOCK:]

        @pl.when(n == nb - 1)
        def _():
            dsink_ref[...] = jnp.sum(dsink_acc[...], axis=0, keepdims=True)

    return _pcall(
        body, name=name, ins=[q, kv, kv, do, bias, sinks], dep=dep, grid=(nb,),
        in_specs=_attn_specs(c) + [pl.BlockSpec((BLOCK, dq), lambda n: (n, 0)),
                                   pl.BlockSpec((N_HEADS, BLOCK, 2 * BLOCK), lambda n: (0, 0, 0)), SMEM],
        out_specs=[pl.BlockSpec((BLOCK, dq), lambda n: (n, 0)), pl.BlockSpec((S, 2 * dkv), lambda n: (0, 0)),
                   pl.BlockSpec((N_HEADS, BLOCK, 2 * BLOCK), lambda n: (0, 0, 0)),
                   pl.BlockSpec((1, N_HEADS), lambda n: (0, 0))],
        out_shape=[jax.ShapeDtypeStruct((S, dq), BF16), jax.ShapeDtypeStruct((S, 2 * dkv), F32),
                   jax.ShapeDtypeStruct((N_HEADS, BLOCK, 2 * BLOCK), F32), jax.ShapeDtypeStruct((1, N_HEADS), F32)],
        scratch_shapes=[pltpu.VMEM((BLOCK, N_HEADS), F32)], compiler_params=_params(1))


def _sgu_bwd(c, name, a, z, dgated, ln_g, wc, wc_t, b_t, dep=None):
    S, AW, gd, tm = c.S, c.AW, c.gd, c.tms

    def body(a_ref, z_ref, dg_ref, lng_ref, wc_ref, wct_ref, bt_ref, dz_ref, dws_ref, dbt_ref, dlng_ref, dvn_ref):
        i = pl.program_id(0)

        @pl.when(i == 0)
        def _():
            dws_ref[...] = jnp.zeros_like(dws_ref)
            dbt_ref[...] = jnp.zeros_like(dbt_ref)
            dlng_ref[...] = jnp.zeros_like(dlng_ref)

        lng = lng_ref[...]
        va = a_ref[:, AW:].astype(F32)
        xc = va - jnp.mean(va, axis=-1, keepdims=True)
        rstd = lax.rsqrt(jnp.mean(xc * xc, axis=-1, keepdims=True) + EPS)
        xh = xc * rstd
        vn = (xh * lng).astype(BF16)
        causal = _sgu_masks()
        for ch in range(tm // CHUNK):
            rows = slice(ch * CHUNK, (ch + 1) * CHUNK)
            for g in range(A_GROUPS):
                cols = slice(g * gd, (g + 1) * gd)
                blk = vn[rows, cols]
                mixed = _dot(wc_ref[g], blk, NN) + bt_ref[:, g:g + 1]
                dgb = dg_ref[rows, cols].astype(F32)
                dm = dgb * a_ref[rows, cols].astype(F32)
                dbt_ref[:, g:g + 1] += jnp.sum(dm, axis=1, keepdims=True)
                dm16 = dm.astype(BF16)
                dws_ref[g] += jnp.where(causal, _dot(dm16, blk, NT), 0.0)
                dvn_ref[rows, cols] = _dot(wct_ref[g], dm16, NN)
                dz_ref[rows, cols] = (dgb * mixed * _gelu_grad(z_ref[rows, cols].astype(F32))).astype(BF16)
        dvn = dvn_ref[...]
        dlng_ref[...] += jnp.sum(dvn * xh, axis=0, keepdims=True)
        dxh = dvn * lng
        dva = rstd * (dxh - jnp.mean(dxh, axis=-1, keepdims=True) - xh * jnp.mean(dxh * xh, axis=-1, keepdims=True))
        dz_ref[:, AW:] = (dva * _gelu_grad(z_ref[:, AW:].astype(F32))).astype(BF16)

    wide = pl.BlockSpec((tm, 2 * AW), lambda i: (i, 0))
    wsp = pl.BlockSpec((A_GROUPS, CHUNK, CHUNK), lambda i: (0, 0, 0))
    btsp = pl.BlockSpec((CHUNK, A_GROUPS), lambda i: (0, 0))
    return _pcall(
        body, name=name, ins=[a, z, dgated, ln_g, wc, wc_t, b_t], dep=dep, grid=(S // tm,),
        in_specs=[wide, wide, pl.BlockSpec((tm, AW), lambda i: (i, 0)), pl.BlockSpec((1, AW), lambda i: (0, 0)),
                  wsp, wsp, btsp],
        out_specs=[wide, wsp, btsp, pl.BlockSpec((1, AW), lambda i: (0, 0))],
        out_shape=[jax.ShapeDtypeStruct((S, 2 * AW), BF16), jax.ShapeDtypeStruct((A_GROUPS, CHUNK, CHUNK), F32),
                   jax.ShapeDtypeStruct((CHUNK, A_GROUPS), F32), jax.ShapeDtypeStruct((1, AW), F32)],
        scratch_shapes=[pltpu.VMEM((tm, AW), F32)], compiler_params=_params(1))


def _adamw(name, parts, part_block, part_index, w, m, v, tr, row_off=0, n_rows=None, prev=None, dep=None):
    R, C = w.shape
    n_rows = R if n_rows is None else n_rows
    assert n_rows % tr == 0 and row_off % tr == 0
    bc1 = 1.0 - ADAM_B1 ** ADAM_STEP
    bc2 = 1.0 - ADAM_B2 ** ADAM_STEP

    def body(p_ref, w_ref, m_ref, v_ref, *rest):
        g_ref, d_ref, nm_ref, nv_ref = rest[-4:]
        g = p_ref[0].astype(F32)
        for s in range(1, NDEV):
            g = g + p_ref[s].astype(F32)
        nm = ADAM_B1 * m_ref[...] + (1.0 - ADAM_B1) * g
        nv = ADAM_B2 * v_ref[...] + (1.0 - ADAM_B2) * (g * g)
        g_ref[...] = g
        nm_ref[...] = nm
        nv_ref[...] = nv
        d_ref[...] = -ADAM_LR * ((nm / bc1) / (jnp.sqrt(nv / bc2) + ADAM_EPS) + ADAM_WD * w_ref[...])

    ob = row_off // tr
    row = pl.BlockSpec((tr, C), lambda i: (ob + i, 0))
    out = jax.ShapeDtypeStruct((R, C), F32)
    chained = prev is not None
    return _pcall(
        body, name=name, ins=[parts, w, m, v] + (list(prev) if chained else []), dep=dep, grid=(n_rows // tr,),
        in_specs=[pl.BlockSpec(part_block, part_index), row, row, row] + ([ANY] * 4 if chained else []),
        out_specs=[row, row, row, row], out_shape=[out, out, out, out],
        input_output_aliases={4 + t: t for t in range(4)} if chained else {}, compiler_params=_params(1))


def _place():
    return lax.axis_index("x"), lax.axis_index("y"), lax.axis_index("c")


def _slot(px, py, pc):
    return 4 * px + 2 * py + pc


def _peer(k, x, y, c):
    return x ^ ((k >> 2) & 1), y ^ ((k >> 1) & 1), c ^ (k & 1)


def _all_gather(name, arrs, dep=None):
    n = len(arrs)

    def body(*refs):
        in_refs, out_refs = refs[:n], refs[n:2 * n]
        send_sems, recv_sems, local_sems = refs[2 * n:]
        x, y, c = _place()
        me, sibling = (x, y, c), (x, y, 1 - c)
        chips = [(1 - x, y), (x, 1 - y), (1 - x, 1 - y)]

        def copy(a, k, block, to, src=None):
            dst = out_refs[a].at[_slot(*block)]
            return pltpu.make_async_remote_copy(
                src_ref=dst if src is None else src, dst_ref=dst, send_sem=send_sems.at[a, k],
                recv_sem=recv_sems.at[a, k], device_id=to, device_id_type=MESH)

        mine = [pltpu.make_async_copy(in_refs[a], out_refs[a].at[_slot(*me)], local_sems.at[a]) for a in range(n)]
        for cp in mine:
            cp.start()
        first = []
        for a in range(n):
            first.append(copy(a, 0, me, sibling, src=in_refs[a]))
            first += [copy(a, 1 + j, me, (*chip, c), src=in_refs[a]) for j, chip in enumerate(chips)]
        for cp in first:
            cp.start()
        passed = []
        for j, chip in enumerate(chips):
            for a in range(n):
                copy(a, 1 + j, (*chip, c), me).wait_recv()
                fwd = copy(a, 4 + j, (*chip, c), sibling)
                fwd.start()
                passed.append(fwd)
        for a in range(n):
            copy(a, 0, sibling, me).wait_recv()
            for j, chip in enumerate(chips):
                copy(a, 4 + j, (*chip, 1 - c), me).wait_recv()
        for cp in first + passed:
            cp.wait_send()
        for cp in mine:
            cp.wait()

    return _pcall(
        body, name=name, ins=list(arrs), in_specs=[HBM] * n, dep=dep,
        out_shape=[jax.ShapeDtypeStruct((NDEV, *a.shape), a.dtype) for a in arrs], out_specs=[HBM] * n,
        scratch_shapes=[pltpu.SemaphoreType.DMA((n, 7)), pltpu.SemaphoreType.DMA((n, 7)),
                        pltpu.SemaphoreType.DMA((n,))],
        compiler_params=pltpu.CompilerParams(has_side_effects=True))


def _send_copy(gather, src_ref, land_ref, send_sems, recv_sems, a, k, place):
    x, y, c = place
    peer = _peer(k, x, y, c)
    return pltpu.make_async_remote_copy(
        src_ref=src_ref if gather else src_ref.at[_slot(*peer)], dst_ref=land_ref.at[_slot(x, y, c)],
        send_sem=send_sems.at[a * (NDEV - 1) + k - 1], recv_sem=recv_sems.at[a * (NDEV - 1) + k - 1],
        device_id=peer, device_id_type=MESH)


def _send_start(name, srcs, lands, gather, dep=None):
    n = len(srcs)

    def body(*refs):
        src_refs, land_refs = refs[:n], refs[n:2 * n]
        send_sems, recv_sems = refs[2 * n], refs[2 * n + 1]
        token = refs[-1]
        place = _place()
        for k in range(1, NDEV):
            for a in range(n):
                _send_copy(gather, src_refs[a], land_refs[a], send_sems, recv_sems, a, k, place).start()
        token[...] = jnp.zeros_like(token)

    thru = [pltpu.HBM(t.shape, t.dtype) for t in [*srcs, *lands]]
    out = _pcall(
        body, name=name, ins=[pltpu.with_memory_space_constraint(t, pltpu.HBM) for t in [*srcs, *lands]],
        in_specs=[HBM] * (2 * n), dep=dep,
        out_shape=(pltpu.SemaphoreType.DMA((n * (NDEV - 1),)), pltpu.SemaphoreType.DMA((n * (NDEV - 1),)), *thru,
                   jax.ShapeDtypeStruct((8, LANES), F32)),
        out_specs=(SEM, SEM, *[HBM] * (2 * n), pl.BlockSpec(memory_space=pltpu.VMEM)),
        input_output_aliases={i: 2 + i for i in range(2 * n)},
        compiler_params=pltpu.CompilerParams(has_side_effects=pltpu.SideEffectType.DATAFLOW_SIDE_EFFECTING))
    return out[-1], out[0], out[1], list(out[2:2 + n]), list(out[2 + n:2 + 2 * n])


def _send_wait(name, started, gather, dep=None):
    _, send_sems, recv_sems, srcs, lands = started
    n = len(srcs)

    def body(*refs):
        src_refs, land_refs = refs[:n], refs[n:2 * n]
        ssem, rsem = refs[2 * n], refs[2 * n + 1]
        place = _place()
        for k in range(1, NDEV):
            for a in range(n):
                cp = _send_copy(gather, src_refs[a], land_refs[a], ssem, rsem, a, k, place)
                cp.wait_send()
                cp.wait_recv()

    thru = [pltpu.HBM(t.shape, t.dtype) for t in [*srcs, *lands]]
    out = _pcall(
        body, name=name, ins=[*srcs, *lands, send_sems, recv_sems], in_specs=[HBM] * (2 * n) + [SEM, SEM], dep=dep,
        out_shape=tuple(thru), out_specs=tuple([HBM] * (2 * n)), input_output_aliases={i: i for i in range(2 * n)},
        compiler_params=pltpu.CompilerParams(has_side_effects=pltpu.SideEffectType.DATAFLOW_SIDE_EFFECTING))
    return list(out[n:])


def _landing(block, me):
    zone = lax.empty((NDEV, *block.shape), block.dtype)
    return lax.dynamic_update_slice(zone, block[None], (me,) + (0,) * block.ndim)


def _rows128(t):
    flat = t.reshape(-1)
    n = flat.shape[0]
    rows = -(-n // (8 * LANES)) * 8
    return jnp.pad(flat, (0, rows * LANES - n)).reshape(rows, LANES)


def kernel(x, mix_norm_g, ffn_norm_g, a_w_in, a_ln_g, a_w_spatial, a_b_spatial, a_w_out, kv_norm_g, w_k, w_v, b_w_q, b_sinks, b_w_o, rel_bias, ffn_w1, ffn_w2, final_norm_g, loss_target, m_mix_norm_g, m_ffn_norm_g, m_a_w_in, m_a_ln_g, m_a_w_spatial, m_a_b_spatial, m_a_w_out, m_kv_norm_g, m_w_k, m_w_v, m_b_w_q, m_b_sinks, m_b_w_o, m_rel_bias, m_ffn_w1, m_ffn_w2, m_final_norm_g, v_mix_norm_g, v_ffn_norm_g, v_a_w_in, v_a_ln_g, v_a_w_spatial, v_a_b_spatial, v_a_w_out, v_kv_norm_g, v_w_k, v_w_v, v_b_w_q, v_b_sinks, v_b_w_o, v_rel_bias, v_ffn_w1, v_ffn_w2, v_final_norm_g):
    c = _config(x, a_w_in, a_w_out, w_k, b_w_q, b_w_o, ffn_w1, ffn_w2)
    S, D, LA, LB, LF = c.S, c.D, c.LA, c.LB, c.LF
    weights = dict(mix_norm_g=mix_norm_g, ffn_norm_g=ffn_norm_g, a_w_in=a_w_in, a_ln_g=a_ln_g, a_w_spatial=a_w_spatial,
                   a_b_spatial=a_b_spatial, a_w_out=a_w_out, kv_norm_g=kv_norm_g, w_k=w_k, w_v=w_v, b_w_q=b_w_q,
                   b_sinks=b_sinks, b_w_o=b_w_o, rel_bias=rel_bias, ffn_w1=ffn_w1, ffn_w2=ffn_w2,
                   final_norm_g=final_norm_g)
    m_in = dict(mix_norm_g=m_mix_norm_g, ffn_norm_g=m_ffn_norm_g, a_w_in=m_a_w_in, a_ln_g=m_a_ln_g,
                a_w_spatial=m_a_w_spatial, a_b_spatial=m_a_b_spatial, a_w_out=m_a_w_out, kv_norm_g=m_kv_norm_g,
                w_k=m_w_k, w_v=m_w_v, b_w_q=m_b_w_q, b_sinks=m_b_sinks, b_w_o=m_b_w_o, rel_bias=m_rel_bias,
                ffn_w1=m_ffn_w1, ffn_w2=m_ffn_w2, final_norm_g=m_final_norm_g)
    v_in = dict(mix_norm_g=v_mix_norm_g, ffn_norm_g=v_ffn_norm_g, a_w_in=v_a_w_in, a_ln_g=v_a_ln_g,
                a_w_spatial=v_a_w_spatial, a_b_spatial=v_a_b_spatial, a_w_out=v_a_w_out, kv_norm_g=v_kv_norm_g,
                w_k=v_w_k, w_v=v_w_v, b_w_q=v_b_w_q, b_sinks=v_b_sinks, b_w_o=v_b_w_o, rel_bias=v_rel_bias,
                ffn_w1=v_ffn_w1, ffn_w2=v_ffn_w2, final_norm_g=v_final_norm_g)
    names = list(weights)
    seq = _Seq()
    me = _slot(*_place())
    bf = lambda t: t.astype(BF16)

    def rows_of(l):
        second = a_w_out[l] if l < LA else b_w_o[l - LA]
        return bf(jnp.concatenate([ffn_w2[l], second], axis=0))

    groups = [[bf(a_w_in[0])[None], a_ln_g],
              [bf(ffn_w1[0])[None], rows_of(0)]]
    for l in range(1, LA):
        groups.append([bf(jnp.stack([a_w_in[l], ffn_w1[l]])), rows_of(l)])
    for l in range(LB):
        extra = [bf(jnp.concatenate([w_k, w_v], axis=1))] if l == 0 else []
        groups.append(extra + [bf(b_w_q[l]), bf(ffn_w1[LA + l])[None], rows_of(LA + l)])
    started = [seq(_send_start, f"weights_start{i}", grp, [_landing(t, me) for t in grp], True)
               for i, grp in enumerate(groups)]

    def arrive(i):
        return seq(_send_wait, f"weights_wait{i}", started[i], True)

    causal = jnp.tril(jnp.ones((CHUNK, CHUNK), bool))
    wsp = jnp.where(causal[None, None], a_w_spatial, 0.0)
    wsp16 = wsp.astype(BF16)
    wsp16_t = jnp.swapaxes(wsp, -1, -2).astype(BF16)
    bsp_t = jnp.swapaxes(a_b_spatial, -1, -2)
    mix_g = mix_norm_g.reshape(-1, 1, D)
    ffn_g = ffn_norm_g.reshape(-1, 1, D)
    kv_g = kv_norm_g.reshape(1, D)
    fin_g = final_norm_g.reshape(1, D)
    onehot = _bucket_onehot()
    bias = seq(_band_bias, rel_bias.T, onehot).reshape(N_HEADS, BLOCK, 2 * BLOCK)

    h = x.reshape(S, D)
    sav_a, sav_b, wts_a, wts_b = [], [], [], []
    for l in range(LA):
        if l == 0:
            w_in, lng_all = arrive(0)
            ln_g_full = jnp.transpose(lng_all, (1, 0, 2)).reshape(LA, 1, c.AW)
            in_i, w1_i = 0, 0
        else:
            w_in, rows = arrive(l + 1)
            w1, in_i, w1_i = w_in, 0, 1
        z, a, hn = seq(_a_in_fwd, c, f"a_in_fwd{l}", h, mix_g[l], w_in, in_i)
        gated = seq(_sgu_fwd, c, f"sgu_fwd{l}", a, ln_g_full[l], wsp16[l], bsp_t[l])
        if l == 0:
            w1, rows = arrive(1)
        h1 = seq(_mm_res, c, f"a_out_fwd{l}", gated, rows, c.ar, c.fr // c.ar, h)
        p, h2, hnf = seq(_ffn_fwd, c, f"ffn_fwd{l}", h1, ffn_g[l], w1, w1_i, rows)
        sav_a.append((h, z, a, hn, gated, h1, p, hnf))
        wts_a.append((w_in, in_i, w1, w1_i, rows))
        h = h2
    h_kv = h
    for l in range(LB):
        got = arrive(LA + 1 + l)
        if l == 0:
            wkv, got = got[0], got[1:]
            kv, hkv = seq(_rms_mm_rows, c, "kv_fwd", h, kv_g, wkv, c.kr, 0, 2 * c.DKV)
        wq, w1, rows = got
        q, hn = seq(_rms_mm_rows, c, f"q_fwd{l}", h, mix_g[LA + l], wq, c.qr, 0, c.DQ)
        o = seq(_attn_fwd, c, f"attn_fwd{l}", q, kv, bias, b_sinks[l])
        h1 = seq(_mm_res, c, f"o_fwd{l}", o, rows, c.orr, c.fr // c.orr, h)
        p, h2, hnf = seq(_ffn_fwd, c, f"ffn_fwd{LA + l}", h1, ffn_g[LA + l], w1, 0, rows)
        sav_b.append((h, q, hn, o, h1, p, hnf))
        wts_b.append((wq, w1, rows))
        h = h2
    dh, d_fin_g, loss_row = seq(_final_loss, c, h, fin_g, loss_target.reshape(S, D))
    loss = lax.psum(loss_row[0, 0], MESH_AXES)

    results = {}
    in_flight = []

    def update(k, parts, layer, col_blk=0):
        w = weights[k]
        rows_l, ncols = (w.shape[-2], w.shape[-1]) if w.ndim == 3 else w.shape
        flat = lambda t: t.reshape(-1, ncols)
        tr = min(256, rows_l)
        results[k] = seq(_adamw, f"adamw_{k}{layer}", parts, (NDEV, tr, ncols), lambda i: (0, i, col_blk),
                         flat(w), flat(m_in[k]), flat(v_in[k]), tr, row_off=layer * rows_l, n_rows=rows_l,
                         prev=results.get(k))

    def land(tag, entry):
        lands = seq(_send_wait, f"grads_wait_{tag}", entry[1], False)
        for keys, parts in zip(entry[0], lands):
            for k, layer, col_blk in keys:
                update(k, parts, layer, col_blk)

    def send(tag, items):
        slabs = [t for _, t in items]
        own = [_landing(lax.dynamic_index_in_dim(t, me, 0, keepdims=False), me) for t in slabs]
        st = seq(_send_start, f"grads_start_{tag}", slabs, own, False)
        in_flight.append((tag, ([k for k, _ in items], st)))
        while len(in_flight) > EXCHANGE_LAG:
            land(*in_flight.pop(0))

    d_mix_g, d_ffn_g = [None] * LF, [None] * LF
    dkv_list, dbias_list, dsink_list = [], [], [None] * LB

    def ffn_bwd(lf, dh, h1, p, hnf, w1, w1_i, rows):
        da, dh1, d_ffn_g[lf] = seq(_ffn_bwd_data, c, f"ffn_bwd_data{lf}", dh, p, w1, w1_i, rows, h1, ffn_g[lf])
        dw1, dw2 = seq(_ffn_bwd_w, c, f"ffn_bwd_w{lf}", hnf, da, p, dh)
        send(f"ffn{lf}", [([("ffn_w1", lf, 0)], dw1), ([("ffn_w2", lf, 0)], dw2)])
        return dh1

    for l in reversed(range(LB)):
        h0, q, hn, o, h1, p, hnf = sav_b[l]
        wq, w1, rows = wts_b[l]
        dh1 = ffn_bwd(LA + l, dh, h1, p, hnf, w1, 0, rows)
        do = seq(_bwd_rows_data, c, f"o_bwd_data{l}", dh1, rows, c.orr, c.fr // c.orr)
        dwo = seq(_wgrad_rows, c, f"o_bwd_w{l}", o, [dh1], c.DQ, D)
        dq, dkv, dbias, dsink_list[l] = seq(_attn_bwd, c, f"attn_bwd{l}", q, kv, do, bias, b_sinks[l])
        dkv_list.append(dkv)
        dbias_list.append(dbias)
        dwq = seq(_wgrad_rows, c, f"q_bwd_w{l}", hn, [dq], D, c.DQ)
        send(f"attn{l}", [([("b_w_o", l, 0)], dwo), ([("b_w_q", l, 0)], dwq)])
        dh, d_mix_g[LA + l] = seq(_bwd_rows_to_stream, c, f"q_bwd_data{l}", [dq], wq, c.qr, 0, c.DQ, h0,
                                  mix_g[LA + l], dh1)
    dwkv = seq(_wgrad_rows, c, "kv_bwd_w", hkv, dkv_list, D, 2 * c.DKV)
    send("kv", [([("w_k", 0, 0), ("w_v", 0, 1)], dwkv)])
    dh, d_kv_g = seq(_bwd_rows_to_stream, c, "kv_bwd_data", dkv_list, wkv, c.kr, 0, 2 * c.DKV, h_kv, kv_g, dh)
    d_rel_t = seq(_band_bias_grad, [d.reshape(N_HEADS, -1) for d in dbias_list], onehot)
    d_wsp, d_bsp, d_lng = [None] * LA, [None] * LA, [None] * LA
    for l in reversed(range(LA)):
        h0, z, a, hn, gated, h1, p, hnf = sav_a[l]
        w_in, in_i, w1, w1_i, rows = wts_a[l]
        dh1 = ffn_bwd(l, dh, h1, p, hnf, w1, w1_i, rows)
        dgated = seq(_bwd_rows_data, c, f"a_out_bwd_data{l}", dh1, rows, c.ar, c.fr // c.ar)
        dwout = seq(_wgrad_rows, c, f"a_out_bwd_w{l}", gated, [dh1], c.AW, D)
        send(f"a_out{l}", [([("a_w_out", l, 0)], dwout)])
        dz, d_wsp[l], dbt, d_lng[l] = seq(_sgu_bwd, c, f"sgu_bwd{l}", a, z, dgated, ln_g_full[l], wsp16[l],
                                          wsp16_t[l], bsp_t[l])
        d_bsp[l] = dbt.T
        dwin = seq(_wgrad_cols, c, f"a_in_bwd_w{l}", hn, dz)
        send(f"a_in{l}", [([("a_w_in", l, 0)], dwin)])
        dh, d_mix_g[l] = seq(_bwd_cols_to_stream, c, f"a_in_bwd_data{l}", dz, w_in, in_i, h0, mix_g[l], dh1)
    grad_x = dh.reshape(1, S, D)

    small = {
        "mix_norm_g": jnp.concatenate(d_mix_g, axis=0), "ffn_norm_g": jnp.concatenate(d_ffn_g, axis=0),
        "a_w_spatial": jnp.stack(d_wsp), "a_b_spatial": jnp.stack(d_bsp), "kv_norm_g": d_kv_g,
        "b_sinks": jnp.concatenate(dsink_list, axis=0), "rel_bias": d_rel_t.T, "final_norm_g": d_fin_g,
    }
    small_names = list(small)
    packs = [_rows128(small[k]) for k in small_names] + [_rows128(jnp.concatenate(d_lng, axis=0))]
    offs = np.cumsum([0] + [p.shape[0] for p in packs])
    (small_all,) = seq(_all_gather, "gather_small_grads", [jnp.concatenate(packs, axis=0)])
    Rs = int(offs[-1])
    while in_flight:
        land(*in_flight.pop(0))

    grads, deltas, new_m, new_v = {}, {}, {}, {}

    def put(k, outs, shape):
        grads[k], deltas[k], new_m[k], new_v[k] = (t.reshape(shape) for t in outs)

    def pack_state(d):
        return jnp.concatenate([_rows128(d[k]) for k in small_names]
                               + [jnp.zeros((packs[-1].shape[0], LANES), F32)], axis=0)

    outs = seq(_adamw, "adamw_small", small_all, (NDEV, Rs, LANES), lambda i: (0, 0, 0),
               pack_state(weights), pack_state(m_in), pack_state(v_in), Rs)
    for n_, k in enumerate(small_names):
        shape = weights[k].shape
        size = int(np.prod(shape))
        put(k, [t[int(offs[n_]):int(offs[n_ + 1])].reshape(-1)[:size] for t in outs], shape)
    lng_sum = outs[0][int(offs[-2]):int(offs[-1])].reshape(-1)[:LA * c.AW].reshape(LA, c.AW)
    lng_mine = lax.dynamic_slice_in_dim(lng_sum, me * c.ar, c.ar, axis=1)
    lng_parts = jnp.concatenate([lng_mine[None], jnp.zeros((NDEV - 1, LA, c.ar), F32)], axis=0)
    put("a_ln_g", seq(_adamw, "adamw_ln_g", lng_parts, (NDEV, LA, c.ar), lambda i: (0, 0, 0),
                      a_ln_g, m_in["a_ln_g"], v_in["a_ln_g"], LA), a_ln_g.shape)
    for k in ("a_w_in", "ffn_w1", "ffn_w2", "a_w_out", "b_w_o", "b_w_q", "w_k", "w_v"):
        put(k, results[k], weights[k].shape)

    return (loss, grad_x, *[grads[k] for k in names], *[deltas[k] for k in names],
            *[new_m[k] for k in names], *[new_v[k] for k in names])
```

```python
import numpy as np
import math
import jax
import jax.numpy as jnp
from jax import lax
from jax.experimental import pallas as pl
from jax.experimental.pallas import tpu as pltpu

F32 = jnp.float32
BF16 = jnp.bfloat16

NDEV = 8
EPS = 1e-6
CHUNK = 128
A_GROUPS = 8
N_HEADS = 16
N_KV_HEADS = 4
Q_PER_KV = N_HEADS // N_KV_HEADS
HEAD_DIM = 64
BLOCK = 128
N_BUCKETS = 32
MAX_DISTANCE = 128
ADAM_LR, ADAM_B1, ADAM_B2, ADAM_EPS, ADAM_WD, ADAM_STEP = 0.001, 0.9, 0.999, 1e-08, 0.01, 10
LANES = 128
VMEM_LIMIT = 56 * 1024 * 1024
INV_SQRT2 = 0.7071067811865476
INV_SQRT_2PI = 0.3989422804014327
MESH_AXES = ("x", "y", "c")
EXCHANGE_LAG = 2

HBM = pl.BlockSpec(memory_space=pltpu.HBM)
SMEM = pl.BlockSpec(memory_space=pltpu.SMEM)
ANY = pl.BlockSpec(memory_space=pl.ANY)
SEM = pl.BlockSpec(memory_space=pltpu.SEMAPHORE)
MESH = pl.DeviceIdType.MESH


def _params(n_grid):
    return pltpu.CompilerParams(dimension_semantics=("arbitrary",) * n_grid, vmem_limit_bytes=VMEM_LIMIT)


def _pcall(body, *, ins, in_specs, dep=None, **kw):
    n_in = len(ins)
    if dep is None:
        return pl.pallas_call(body, in_specs=list(in_specs), **kw)(*ins)

    def with_dep(*refs):
        body(*refs[:n_in], *refs[n_in + 1:])

    return pl.pallas_call(with_dep, in_specs=[*in_specs, ANY], **kw)(*ins, dep)


class _Seq:
    def __init__(self):
        self.last = None

    def __call__(self, fn, *args, **kw):
        out = fn(*args, dep=self.last, **kw)
        self.last = out[0] if isinstance(out, (tuple, list)) else out
        return out


def _rstd(h):
    return lax.rsqrt(jnp.mean(h * h, axis=-1, keepdims=True) + EPS)


def _rms_bwd(dhn, h, g, dres):
    r = _rstd(h)
    xh = h * r
    dg = jnp.sum(dhn * xh, axis=0, keepdims=True)
    dxh = dhn * g
    dx = r * (dxh - xh * jnp.mean(dxh * xh, axis=-1, keepdims=True))
    return dres + dx, dg


def _gelu(z):
    return 0.5 * z * (1.0 + lax.erf(z * INV_SQRT2))


def _gelu_grad(z):
    return 0.5 * (1.0 + lax.erf(z * INV_SQRT2)) + z * (jnp.exp(-0.5 * z * z) * INV_SQRT_2PI)


def _dot(a, b, dims):
    return lax.dot_general(a, b, (dims, ((), ())), preferred_element_type=F32)


NN = ((1,), (0,))
NT = ((1,), (1,))
TN = ((0,), (0,))


def _mm(name, ins, in_specs, out_shapes, out_specs, *, grid, dims, nk, acc_shape, load_a, load_b, epilogue,
        dep=None):
    n_in, n_out = len(ins), len(out_shapes)
    kax = len(grid) - 1

    def body(*refs):
        in_refs = refs[:n_in]
        out_refs = refs[n_in:n_in + n_out]
        a = load_a(in_refs, out_refs)
        b = load_b(in_refs)
        prod = _dot(a, b, dims)
        if nk == 1:
            epilogue(prod, in_refs, out_refs)
        else:
            acc = refs[n_in + n_out]
            k = pl.program_id(kax)

            @pl.when(k == 0)
            def _():
                acc[...] = prod

            @pl.when(k > 0)
            def _():
                acc[...] += prod

            @pl.when(k == nk - 1)
            def _():
                epilogue(acc[...], in_refs, out_refs)

    return _pcall(
        body, name=name, ins=ins, in_specs=in_specs, dep=dep, grid=grid, out_specs=out_specs, out_shape=out_shapes,
        scratch_shapes=[pltpu.VMEM(acc_shape, F32)] if nk > 1 else [], compiler_params=_params(len(grid)))


def _bf(ref_idx):
    return lambda in_refs, *_: in_refs[ref_idx][...].astype(BF16)


def _b_view(ref_idx, rows):
    def load(in_refs):
        b = in_refs[ref_idx][...]
        return b.reshape(rows, b.shape[-1])
    return load


class Cfg:
    pass


def _config(x, a_w_in, a_w_out, w_k, b_w_q, b_w_o, ffn_w1, ffn_w2):
    c = Cfg()
    c.S, c.D = x.shape[1], x.shape[2]
    c.LA, _, c.cw = a_w_in.shape
    c.AW2 = NDEV * c.cw
    c.AW = c.AW2 // 2
    c.gd = c.AW // A_GROUPS
    c.ar = a_w_out.shape[1]
    c.LF, _, c.fw = ffn_w1.shape
    c.fr = ffn_w2.shape[1]
    c.LB, c.qr, c.DQ = b_w_q.shape
    c.orr = b_w_o.shape[1]
    c.kr, c.DKV = w_k.shape
    c.tm = min(512, c.S)
    c.tms = min(256, c.S)
    c.nb = c.S // BLOCK
    assert c.cw == c.fw == c.fr and c.AW == NDEV * c.ar and c.D == NDEV * c.qr == NDEV * c.kr
    assert c.DQ == NDEV * c.orr == N_HEADS * HEAD_DIM and c.DKV == N_KV_HEADS * HEAD_DIM
    assert c.S % c.tm == 0 and c.S % c.tms == 0 and c.tms % CHUNK == 0 and c.gd % LANES == 0
    assert c.fr % c.ar == 0 and c.fr % c.orr == 0
    assert c.LA >= 1 and c.LB >= 1 and c.LF == c.LA + c.LB
    return c


def _cached_rms(h_idx, g_idx, hn_out_idx, jax_axis=1):
    def load(in_refs, out_refs):
        hn_ref = out_refs[hn_out_idx]

        @pl.when(pl.program_id(jax_axis) == 0)
        def _():
            h = in_refs[h_idx][...]
            hn_ref[...] = (h * _rstd(h) * in_refs[g_idx][...]).astype(BF16)

        return hn_ref[...]
    return load


def _a_in_fwd(c, name, h, g, col, ci, dep=None):
    S, D, cw, tm = c.S, c.D, c.cw, c.tm

    def epilogue(z, in_refs, out_refs):
        out_refs[0][...] = z.astype(BF16)
        out_refs[1][...] = _gelu(z).astype(BF16)

    return _mm(
        name, [h, col, g],
        [pl.BlockSpec((tm, D), lambda i, j, k: (i, 0)),
         pl.BlockSpec((None, None, D, cw), lambda i, j, k: (j, ci, 0, 0)),
         pl.BlockSpec((1, D), lambda i, j, k: (0, 0))],
        [jax.ShapeDtypeStruct((S, c.AW2), BF16), jax.ShapeDtypeStruct((S, c.AW2), BF16),
         jax.ShapeDtypeStruct((S, D), BF16)],
        [pl.BlockSpec((tm, cw), lambda i, j, k: (i, j)), pl.BlockSpec((tm, cw), lambda i, j, k: (i, j)),
         pl.BlockSpec((tm, D), lambda i, j, k: (i, 0))],
        grid=(S // tm, NDEV, 1), dims=NN, nk=1, acc_shape=None,
        load_a=_cached_rms(0, 2, 2), load_b=lambda r: r[1][...], epilogue=epilogue, dep=dep)


def _rms_mm_rows(c, name, h, g, slab, blk_rows, blk_idx, n_out, dep=None):
    S, D, tm = c.S, c.D, c.tm

    def epilogue(acc, in_refs, out_refs):
        out_refs[0][...] = acc.astype(BF16)

    return _mm(
        name, [h, slab, g],
        [pl.BlockSpec((tm, D), lambda i, j, k: (i, 0)),
         pl.BlockSpec((NDEV, blk_rows, n_out), lambda i, j, k: (0, blk_idx, 0)),
         pl.BlockSpec((1, D), lambda i, j, k: (0, 0))],
        [jax.ShapeDtypeStruct((S, n_out), BF16), jax.ShapeDtypeStruct((S, D), BF16)],
        [pl.BlockSpec((tm, n_out), lambda i, j, k: (i, 0)), pl.BlockSpec((tm, D), lambda i, j, k: (i, 0))],
        grid=(S // tm, 1, 1), dims=NN, nk=1, acc_shape=None,
        load_a=_cached_rms(0, 2, 1), load_b=_b_view(1, NDEV * blk_rows), epilogue=epilogue, dep=dep)


def _mm_res(c, name, a, slab, blk_rows, blk_idx, res, dep=None):
    S, D, tm = c.S, c.D, c.tm
    K = NDEV * blk_rows

    def epilogue(acc, in_refs, out_refs):
        out_refs[0][...] = in_refs[2][...] + acc

    return _mm(
        name, [a, slab, res],
        [pl.BlockSpec((tm, K), lambda i, j, k: (i, 0)),
         pl.BlockSpec((NDEV, blk_rows, D), lambda i, j, k: (0, blk_idx, 0)),
         pl.BlockSpec((tm, D), lambda i, j, k: (i, 0))],
        [jax.ShapeDtypeStruct((S, D), F32)], [pl.BlockSpec((tm, D), lambda i, j, k: (i, 0))],
        grid=(S // tm, 1, 1), dims=NN, nk=1, acc_shape=None,
        load_a=_bf(0), load_b=_b_view(1, K), epilogue=epilogue, dep=dep)[0]


def _sgu_masks():
    ii = lax.broadcasted_iota(jnp.int32, (CHUNK, CHUNK), 0)
    jj = lax.broadcasted_iota(jnp.int32, (CHUNK, CHUNK), 1)
    return ii >= jj


def _sgu_fwd(c, name, a, ln_g, wc, b_t, dep=None):
    S, AW, gd, tm = c.S, c.AW, c.gd, c.tms

    def body(a_ref, lng_ref, wc_ref, bt_ref, out_ref):
        va = a_ref[:, AW:].astype(F32)
        xc = va - jnp.mean(va, axis=-1, keepdims=True)
        vn = (xc * lax.rsqrt(jnp.mean(xc * xc, axis=-1, keepdims=True) + EPS) * lng_ref[...]).astype(BF16)
        for ch in range(tm // CHUNK):
            rows = slice(ch * CHUNK, (ch + 1) * CHUNK)
            for g in range(A_GROUPS):
                cols = slice(g * gd, (g + 1) * gd)
                mixed = _dot(wc_ref[g], vn[rows, cols], NN) + bt_ref[:, g:g + 1]
                out_ref[rows, cols] = (a_ref[rows, cols].astype(F32) * mixed).astype(BF16)

    return _pcall(
        body, name=name, ins=[a, ln_g, wc, b_t], dep=dep, grid=(S // tm,),
        in_specs=[pl.BlockSpec((tm, 2 * AW), lambda i: (i, 0)), pl.BlockSpec((1, AW), lambda i: (0, 0)),
                  pl.BlockSpec((A_GROUPS, CHUNK, CHUNK), lambda i: (0, 0, 0)),
                  pl.BlockSpec((CHUNK, A_GROUPS), lambda i: (0, 0))],
        out_specs=pl.BlockSpec((tm, AW), lambda i: (i, 0)),
        out_shape=jax.ShapeDtypeStruct((S, AW), BF16), compiler_params=_params(1))


def _ffn_fwd(c, name, h, g, col, ci, rows, dep=None):
    S, D, fw, tm = c.S, c.D, c.fw, c.tm

    def body(h_ref, g_ref, w1_ref, w2_ref, p_ref, out_ref, hn_ref, acc_ref):
        j = pl.program_id(1)

        @pl.when(j == 0)
        def _():
            h = h_ref[...]
            hn_ref[...] = (h * _rstd(h) * g_ref[...]).astype(BF16)

        p = jnp.maximum(_dot(hn_ref[...], w1_ref[...], NN), 0.0)
        p_ref[...] = p.astype(BF16)
        contrib = _dot((p * p).astype(BF16), w2_ref[...], NN)

        @pl.when(j == 0)
        def _():
            acc_ref[...] = contrib

        @pl.when(j > 0)
        def _():
            acc_ref[...] += contrib

        @pl.when(j == NDEV - 1)
        def _():
            out_ref[...] = h_ref[...] + acc_ref[...]

    return _pcall(
        body, name=name, ins=[h, g, col, rows], dep=dep, grid=(S // tm, NDEV),
        in_specs=[pl.BlockSpec((tm, D), lambda i, j: (i, 0)), pl.BlockSpec((1, D), lambda i, j: (0, 0)),
                  pl.BlockSpec((None, None, D, fw), lambda i, j: (j, ci, 0, 0)),
                  pl.BlockSpec((None, c.fr, D), lambda i, j: (j, 0, 0))],
        out_specs=[pl.BlockSpec((tm, fw), lambda i, j: (i, j)), pl.BlockSpec((tm, D), lambda i, j: (i, 0)),
                   pl.BlockSpec((tm, D), lambda i, j: (i, 0))],
        out_shape=[jax.ShapeDtypeStruct((S, NDEV * fw), BF16), jax.ShapeDtypeStruct((S, D), F32),
                   jax.ShapeDtypeStruct((S, D), BF16)],
        scratch_shapes=[pltpu.VMEM((tm, D), F32)], compiler_params=_params(2))


def _bucket_table():
    qi = np.arange(BLOCK)[:, None]
    kj = np.arange(2 * BLOCK)[None, :]
    d = np.maximum(qi + BLOCK - kj, 0)
    max_exact = N_BUCKETS // 2
    ratio = np.log(np.maximum(d, 1).astype(np.float32) / np.float32(max_exact)) / np.float32(
        math.log(MAX_DISTANCE / max_exact))
    large = np.minimum(max_exact + (ratio.astype(np.float32) * np.float32(N_BUCKETS - max_exact)).astype(np.int32),
                       N_BUCKETS - 1)
    return np.where(d < max_exact, d, large).astype(np.int32)


def _bucket_onehot():
    b = jnp.asarray(_bucket_table().reshape(1, -1))
    return (b == lax.broadcasted_iota(jnp.int32, (N_BUCKETS, b.shape[1]), 0)).astype(F32)


def _whole(t):
    return pl.BlockSpec(t.shape, lambda: (0,) * t.ndim)


def _band_bias(rel_bias_t, onehot, dep=None):
    def body(r_ref, oh_ref, out_ref):
        out_ref[...] = lax.dot_general(r_ref[...], oh_ref[...], (NN, ((), ())), preferred_element_type=F32,
                                       precision=lax.Precision.HIGHEST)

    n = onehot.shape[1]
    return _pcall(body, name="band_bias", ins=[rel_bias_t, onehot], in_specs=[_whole(rel_bias_t), _whole(onehot)],
                  dep=dep, out_shape=jax.ShapeDtypeStruct((N_HEADS, n), F32), compiler_params=_params(0))


def _band_bias_grad(dbias_list, onehot, dep=None):
    n_in = len(dbias_list)

    def body(*refs):
        oh_ref, out_ref = refs[n_in], refs[n_in + 1]
        d = refs[0][...]
        for r in refs[1:n_in]:
            d = d + r[...]
        out_ref[...] = lax.dot_general(d, oh_ref[...], (NT, ((), ())), preferred_element_type=F32,
                                       precision=lax.Precision.HIGHEST)

    ins = [*dbias_list, onehot]
    return _pcall(body, name="band_bias_grad", ins=ins, in_specs=[_whole(t) for t in ins], dep=dep,
                  out_shape=jax.ShapeDtypeStruct((N_HEADS, N_BUCKETS), F32), compiler_params=_params(0))


KV_PAIRS = N_KV_HEADS // 2
PAIR_ROWS = 2 * Q_PER_KV * BLOCK
MASKED = float(np.finfo(np.float32).min) / 2


def _slot_cols(w):
    lead = w.shape[:-1]
    return w.reshape(*lead, KV_PAIRS, 2, Q_PER_KV, HEAD_DIM).swapaxes(-3, -2).reshape(*lead, N_HEADS * HEAD_DIM)


def _unslot_cols(w):
    lead = w.shape[:-1]
    return w.reshape(*lead, KV_PAIRS, Q_PER_KV, 2, HEAD_DIM).swapaxes(-3, -2).reshape(*lead, N_HEADS * HEAD_DIM)


def _slot_rows(blocks):
    n = blocks.shape[-1]
    return blocks.reshape(KV_PAIRS, 2, Q_PER_KV, HEAD_DIM, n).swapaxes(1, 2).reshape(blocks.shape)


def _unslot_rows(blocks):
    n = blocks.shape[-1]
    return blocks.reshape(KV_PAIRS, Q_PER_KV, 2, HEAD_DIM, n).swapaxes(1, 2).reshape(blocks.shape)


def _slot_bias(bias):
    qi = np.arange(BLOCK)[:, None]
    kj = np.arange(2 * BLOCK)[None, :]
    dist = qi + BLOCK - kj
    window = (dist >= 0) & (dist < BLOCK)
    b = bias.reshape(KV_PAIRS, 2, Q_PER_KV, BLOCK, 2 * BLOCK).swapaxes(1, 2).reshape(KV_PAIRS, PAIR_ROWS, 2 * BLOCK)
    tile = lambda mk: jnp.asarray(np.tile(mk, (2 * Q_PER_KV, 1)))[None]
    return jnp.stack([jnp.where(tile(window & (kj >= BLOCK)), b, MASKED), jnp.where(tile(window), b, MASKED)])


def _unslot_bias(db):
    return db.reshape(KV_PAIRS, Q_PER_KV, 2, BLOCK, 2 * BLOCK).swapaxes(1, 2).reshape(N_HEADS, -1)


def _pair_operands(ref, kvp, low, scale=None):
    parts = []
    for g in range(Q_PER_KV):
        xg = ref[:, (kvp * Q_PER_KV + g) * LANES:(kvp * Q_PER_KV + g + 1) * LANES]
        if scale is not None:
            xg = xg * scale
        zero = jnp.zeros_like(xg)
        parts += [jnp.where(low, xg, zero), jnp.where(low, zero, xg)]
    return jnp.concatenate(parts, axis=0)


def _pair_select(t, g, low):
    return jnp.where(low, t[2 * g * BLOCK:(2 * g + 1) * BLOCK], t[(2 * g + 1) * BLOCK:(2 * g + 2) * BLOCK])


def _attn_probs(q_ref, kvc_ref, kvp_ref, bias_ref, sink_ref, kvp, dkv, low):
    lanes = slice(kvp * LANES, (kvp + 1) * LANES)
    vlanes = slice(dkv + kvp * LANES, dkv + (kvp + 1) * LANES)
    k2 = jnp.concatenate([kvp_ref[:, lanes], kvc_ref[:, lanes]], axis=0)
    v2 = jnp.concatenate([kvp_ref[:, vlanes], kvc_ref[:, vlanes]], axis=0)
    qm = _pair_operands(q_ref, kvp, low, scale=HEAD_DIM ** -0.5)
    s = _dot(qm, k2, NT) + bias_ref[kvp]
    heads = [(2 * kvp + par) * Q_PER_KV + g for g in range(Q_PER_KV) for par in range(2)]
    sink = jnp.concatenate([jnp.full((BLOCK, 1), sink_ref[h], F32) for h in heads], axis=0)
    m = jnp.maximum(jnp.max(s, axis=-1, keepdims=True), sink)
    e = jnp.exp(s - m)
    es = jnp.exp(sink - m)
    inv = 1.0 / (jnp.sum(e, axis=-1, keepdims=True) + es)
    return e * inv, es * inv, qm, k2, v2


def _attn_specs(c):
    dq, dkv2 = c.DQ, 2 * c.DKV
    return [pl.BlockSpec((BLOCK, dq), lambda n: (n, 0)),
            pl.BlockSpec((BLOCK, dkv2), lambda n: (n, 0)),
            pl.BlockSpec((BLOCK, dkv2), lambda n: (jnp.maximum(n - 1, 0), 0))]


def _bias_spec():
    return pl.BlockSpec((None, KV_PAIRS, PAIR_ROWS, 2 * BLOCK), lambda n: (jnp.minimum(n, 1), 0, 0, 0))


def _low_lanes():
    return lax.broadcasted_iota(jnp.int32, (BLOCK, LANES), 1) < HEAD_DIM


def _attn_fwd(c, name, q, kv, bias, sinks, dep=None):
    S, dq = c.S, c.DQ

    def body(q_ref, kvc_ref, kvp_ref, bias_ref, sink_ref, o_ref):
        low = _low_lanes()
        for kvp in range(KV_PAIRS):
            p, _, _, _, v2 = _attn_probs(q_ref, kvc_ref, kvp_ref, bias_ref, sink_ref, kvp, c.DKV, low)
            o = _dot(p.astype(BF16), v2, NN)
            for g in range(Q_PER_KV):
                grp = kvp * Q_PER_KV + g
                o_ref[:, grp * LANES:(grp + 1) * LANES] = _pair_select(o, g, low).astype(BF16)

    return _pcall(
        body, name=name, ins=[q, kv, kv, bias, sinks], dep=dep, grid=(c.nb,),
        in_specs=_attn_specs(c) + [_bias_spec(), SMEM],
        out_specs=pl.BlockSpec((BLOCK, dq), lambda n: (n, 0)),
        out_shape=jax.ShapeDtypeStruct((S, dq), BF16), compiler_params=_params(1))


def _final_loss(c, h, g, target, dep=None):
    S, D, tm = c.S, c.D, c.tm

    def body(h_ref, g_ref, t_ref, dh_ref, dg_ref, loss_ref):
        i = pl.program_id(0)
        h = h_ref[...]
        gg = g_ref[...]
        r = _rstd(h)
        xh = h * r
        err = xh * gg - t_ref[...]
        lp = jnp.sum(jnp.sum(err * err, axis=1, keepdims=True), axis=0, keepdims=True) * (0.5 / D)
        dx, dg = _rms_bwd(err * (1.0 / D), h, gg, 0.0)
        dh_ref[...] = dx

        @pl.when(i == 0)
        def _():
            dg_ref[...] = dg
            loss_ref[...] = jnp.broadcast_to(lp, loss_ref.shape)

        @pl.when(i > 0)
        def _():
            dg_ref[...] += dg
            loss_ref[...] += jnp.broadcast_to(lp, loss_ref.shape)

    row = pl.BlockSpec((tm, D), lambda i: (i, 0))
    return _pcall(
        body, name="final_loss", ins=[h, g, target], dep=dep, grid=(S // tm,),
        in_specs=[row, pl.BlockSpec((1, D), lambda i: (0, 0)), row],
        out_specs=[row, pl.BlockSpec((1, D), lambda i: (0, 0)), pl.BlockSpec((1, LANES), lambda i: (0, 0))],
        out_shape=[jax.ShapeDtypeStruct((S, D), F32), jax.ShapeDtypeStruct((1, D), F32),
                   jax.ShapeDtypeStruct((1, LANES), F32)],
        compiler_params=_params(1))


def _rms_bwd_epilogue(h_idx, g_idx, res_idx):
    def epilogue(dhn, in_refs, out_refs):
        dh, dg = _rms_bwd(dhn, in_refs[h_idx][...], in_refs[g_idx][...], in_refs[res_idx][...])
        out_refs[0][...] = dh
        i = pl.program_id(0)

        @pl.when(i == 0)
        def _():
            out_refs[1][...] = dg

        @pl.when(i > 0)
        def _():
            out_refs[1][...] += dg
    return epilogue


def _stream_outs(c):
    S, D, tm = c.S, c.D, c.tm
    return ([jax.ShapeDtypeStruct((S, D), F32), jax.ShapeDtypeStruct((1, D), F32)],
            [pl.BlockSpec((tm, D), lambda i, j, k: (i, 0)), pl.BlockSpec((1, D), lambda i, j, k: (0, 0))])


def _row_specs(c):
    tm, D = c.tm, c.D
    return [pl.BlockSpec((tm, D), lambda i, j, k: (i, 0)), pl.BlockSpec((1, D), lambda i, j, k: (0, 0)),
            pl.BlockSpec((tm, D), lambda i, j, k: (i, 0))]


def _bwd_rows_to_stream(c, name, dy_list, slab, blk_rows, blk_idx, n_in_cols, h, g, dres, dep=None):
    S, D, tm = c.S, c.D, c.tm
    nd = len(dy_list)

    def load_a(in_refs, out_refs):
        a = in_refs[0][...]
        for r in in_refs[1:nd]:
            a = a + r[...]
        return a.astype(BF16)

    shapes, specs = _stream_outs(c)
    return _mm(
        name, [*dy_list, slab, h, g, dres],
        [pl.BlockSpec((tm, n_in_cols), lambda i, j, k: (i, 0))] * nd
        + [pl.BlockSpec((NDEV, blk_rows, n_in_cols), lambda i, j, k: (0, blk_idx, 0))] + _row_specs(c),
        shapes, specs, grid=(S // tm, 1, 1), dims=NT, nk=1, acc_shape=None,
        load_a=load_a, load_b=_b_view(nd, NDEV * blk_rows), epilogue=_rms_bwd_epilogue(nd + 1, nd + 2, nd + 3),
        dep=dep)


def _bwd_cols_to_stream(c, name, dy, col, ci, h, g, dres, dep=None):
    S, D, cw, tm = c.S, c.D, c.cw, c.tm
    shapes, specs = _stream_outs(c)
    return _mm(
        name, [dy, col, h, g, dres],
        [pl.BlockSpec((tm, cw), lambda i, j, k: (i, k)),
         pl.BlockSpec((None, None, D, cw), lambda i, j, k: (k, ci, 0, 0))] + _row_specs(c),
        shapes, specs, grid=(S // tm, 1, NDEV), dims=NT, nk=NDEV, acc_shape=(tm, D),
        load_a=_bf(0), load_b=lambda r: r[1][...], epilogue=_rms_bwd_epilogue(2, 3, 4), dep=dep)


def _bwd_rows_data(c, name, dy, slab, blk_rows, blk_idx, dep=None):
    S, D, tm = c.S, c.D, c.tm
    K = NDEV * blk_rows

    def epilogue(acc, in_refs, out_refs):
        out_refs[0][...] = acc.astype(BF16)

    return _mm(
        name, [dy, slab],
        [pl.BlockSpec((tm, D), lambda i, j, k: (i, 0)),
         pl.BlockSpec((NDEV, blk_rows, D), lambda i, j, k: (0, blk_idx, 0))],
        [jax.ShapeDtypeStruct((S, K), BF16)], [pl.BlockSpec((tm, K), lambda i, j, k: (i, 0))],
        grid=(S // tm, 1, 1), dims=NT, nk=1, acc_shape=None,
        load_a=_bf(0), load_b=_b_view(1, K), epilogue=epilogue, dep=dep)[0]


def _wgrad_rows(c, name, a, b_list, n_a, n_b, dep=None):
    S, tm = c.S, c.tm
    nb_in = len(b_list)
    blk_rows = n_a // NDEV

    def load_b(in_refs):
        b = in_refs[1][...]
        for r in in_refs[2:1 + nb_in]:
            b = b + r[...]
        return b.astype(BF16)

    def epilogue(acc, in_refs, out_refs):
        out_refs[0][...] = acc.reshape(NDEV, blk_rows, n_b).astype(BF16)

    return _mm(
        name, [a, *b_list],
        [pl.BlockSpec((tm, n_a), lambda i, j, k: (k, 0))] + [pl.BlockSpec((tm, n_b), lambda i, j, k: (k, 0))] * nb_in,
        [jax.ShapeDtypeStruct((NDEV, blk_rows, n_b), BF16)],
        [pl.BlockSpec((NDEV, blk_rows, n_b), lambda i, j, k: (0, 0, 0))],
        grid=(1, 1, S // tm), dims=TN, nk=S // tm, acc_shape=(n_a, n_b),
        load_a=_bf(0), load_b=load_b, epilogue=epilogue, dep=dep)[0]


def _wgrad_cols(c, name, a, b, dep=None):
    S, D, cw, tm = c.S, c.D, c.cw, c.tm

    def epilogue(acc, in_refs, out_refs):
        out_refs[0][...] = acc.astype(BF16)

    return _mm(
        name, [a, b],
        [pl.BlockSpec((tm, D), lambda j, i, k: (k, 0)), pl.BlockSpec((tm, cw), lambda j, i, k: (k, j))],
        [jax.ShapeDtypeStruct((NDEV, D, cw), BF16)],
        [pl.BlockSpec((None, D, cw), lambda j, i, k: (j, 0, 0))],
        grid=(NDEV, 1, S // tm), dims=TN, nk=S // tm, acc_shape=(D, cw),
        load_a=_bf(0), load_b=_bf(1), epilogue=epilogue, dep=dep)[0]


def _ffn_bwd_data(c, name, dh, p, col, ci, rows, h, g, dep=None):
    S, D, fw, tm = c.S, c.D, c.fw, c.tm

    def body(dh_ref, p_ref, w1_ref, w2_ref, h_ref, g_ref, da_ref, out_ref, dg_ref, acc_ref, dhb_ref):
        i, j = pl.program_id(0), pl.program_id(1)

        @pl.when(j == 0)
        def _():
            dhb_ref[...] = dh_ref[...].astype(BF16)

        da = (_dot(dhb_ref[...], w2_ref[...], NT) * (2.0 * p_ref[...].astype(F32))).astype(BF16)
        da_ref[...] = da
        contrib = _dot(da, w1_ref[...], NT)

        @pl.when(j == 0)
        def _():
            acc_ref[...] = contrib

        @pl.when(j > 0)
        def _():
            acc_ref[...] += contrib

        @pl.when(j == NDEV - 1)
        def _():
            dx, dg = _rms_bwd(acc_ref[...], h_ref[...], g_ref[...], dh_ref[...])
            out_ref[...] = dx

            @pl.when(i == 0)
            def _():
                dg_ref[...] = dg

            @pl.when(i > 0)
            def _():
                dg_ref[...] += dg

    row = pl.BlockSpec((tm, D), lambda i, j: (i, 0))
    return _pcall(
        body, name=name, ins=[dh, p, col, rows, h, g], dep=dep, grid=(S // tm, NDEV),
        in_specs=[row, pl.BlockSpec((tm, fw), lambda i, j: (i, j)),
                  pl.BlockSpec((None, None, D, fw), lambda i, j: (j, ci, 0, 0)),
                  pl.BlockSpec((None, c.fr, D), lambda i, j: (j, 0, 0)),
                  row, pl.BlockSpec((1, D), lambda i, j: (0, 0))],
        out_specs=[pl.BlockSpec((tm, fw), lambda i, j: (i, j)), row, pl.BlockSpec((1, D), lambda i, j: (0, 0))],
        out_shape=[jax.ShapeDtypeStruct((S, NDEV * fw), BF16), jax.ShapeDtypeStruct((S, D), F32),
                   jax.ShapeDtypeStruct((1, D), F32)],
        scratch_shapes=[pltpu.VMEM((tm, D), F32), pltpu.VMEM((tm, D), BF16)], compiler_params=_params(2))


def _ffn_bwd_w(c, name, hn, da, p, dh, dep=None):
    S, D, fw, tm = c.S, c.D, c.fw, c.tm
    ni = S // tm

    def body(hn_ref, da_ref, p_ref, dh_ref, dw1_ref, dw2_ref, acc1, acc2):
        i = pl.program_id(1)
        c1 = _dot(hn_ref[...], da_ref[...], TN)
        pf = p_ref[...].astype(F32)
        c2 = _dot((pf * pf).astype(BF16), dh_ref[...].astype(BF16), TN)

        @pl.when(i == 0)
        def _():
            acc1[...] = c1
            acc2[...] = c2

        @pl.when(i > 0)
        def _():
            acc1[...] += c1
            acc2[...] += c2

        @pl.when(i == ni - 1)
        def _():
            dw1_ref[...] = acc1[...].astype(BF16)
            dw2_ref[...] = acc2[...].astype(BF16)

    return _pcall(
        body, name=name, ins=[hn, da, p, dh], dep=dep, grid=(NDEV, ni),
        in_specs=[pl.BlockSpec((tm, D), lambda j, i: (i, 0)), pl.BlockSpec((tm, fw), lambda j, i: (i, j)),
                  pl.BlockSpec((tm, fw), lambda j, i: (i, j)), pl.BlockSpec((tm, D), lambda j, i: (i, 0))],
        out_specs=[pl.BlockSpec((None, D, fw), lambda j, i: (j, 0, 0)),
                   pl.BlockSpec((None, c.fr, D), lambda j, i: (j, 0, 0))],
        out_shape=[jax.ShapeDtypeStruct((NDEV, D, fw), BF16), jax.ShapeDtypeStruct((NDEV, c.fr, D), BF16)],
        scratch_shapes=[pltpu.VMEM((D, fw), F32), pltpu.VMEM((c.fr, D), F32)], compiler_params=_params(2))


def _attn_bwd(c, name, q, kv, do, bias, sinks, dep=None):
    S, dq, dkv = c.S, c.DQ, c.DKV
    nb = c.nb
    scale = HEAD_DIM ** -0.5

    def body(q_ref, kvc_ref, kvp_ref, do_ref, bias_ref, sink_ref, dq_ref, dkv_ref, dbias_ref, dsink_ref, dsink_acc):
        n = pl.program_id(0)

        @pl.when(n == 0)
        def _():
            dkv_ref[...] = jnp.zeros_like(dkv_ref)
            dbias_ref[...] = jnp.zeros_like(dbias_ref)
            dsink_acc[...] = jnp.zeros_like(dsink_acc)

        low = _low_lanes()
        rows_c = pl.ds(pl.multiple_of(n * BLOCK, BLOCK), BLOCK)
        rows_p = pl.ds(pl.multiple_of(jnp.maximum(n - 1, 0) * BLOCK, BLOCK), BLOCK)
        for kvp in range(KV_PAIRS):
            p, ps, qm, k2, v2 = _attn_probs(q_ref, kvc_ref, kvp_ref, bias_ref, sink_ref, kvp, dkv, low)
            dom = _pair_operands(do_ref, kvp, low)
            dp = _dot(dom, v2, NT)
            delta = jnp.sum(p * dp, axis=-1, keepdims=True)
            ds = p * (dp - delta)
            dbias_ref[kvp] += ds
            dsink_acc[:, kvp:kvp + 1] += -(ps * delta)
            ds16 = ds.astype(BF16)
            dqm = _dot(ds16, k2, NN) * scale
            for g in range(Q_PER_KV):
                grp = kvp * Q_PER_KV + g
                dq_ref[:, grp * LANES:(grp + 1) * LANES] = _pair_select(dqm, g, low).astype(BF16)
            dk2 = _dot(ds16, qm, TN)
            dv2 = _dot(p.astype(BF16), dom, TN)
            lanes = slice(kvp * LANES, (kvp + 1) * LANES)
            vlanes = slice(dkv + kvp * LANES, dkv + (kvp + 1) * LANES)
            dkv_ref[rows_p, lanes] += dk2[:BLOCK]
            dkv_ref[rows_c, lanes] += dk2[BLOCK:]
            dkv_ref[rows_p, vlanes] += dv2[:BLOCK]
            dkv_ref[rows_c, vlanes] += dv2[BLOCK:]

        @pl.when(n == nb - 1)
        def _():
            dsink_ref[...] = jnp.sum(dsink_acc[...].reshape(2 * Q_PER_KV, BLOCK, KV_PAIRS), axis=1)

    return _pcall(
        body, name=name, ins=[q, kv, kv, do, bias, sinks], dep=dep, grid=(nb,),
        in_specs=_attn_specs(c) + [pl.BlockSpec((BLOCK, dq), lambda n: (n, 0)), _bias_spec(), SMEM],
        out_specs=[pl.BlockSpec((BLOCK, dq), lambda n: (n, 0)), pl.BlockSpec((S, 2 * dkv), lambda n: (0, 0)),
                   pl.BlockSpec((KV_PAIRS, PAIR_ROWS, 2 * BLOCK), lambda n: (0, 0, 0)),
                   pl.BlockSpec((2 * Q_PER_KV, KV_PAIRS), lambda n: (0, 0))],
        out_shape=[jax.ShapeDtypeStruct((S, dq), BF16), jax.ShapeDtypeStruct((S, 2 * dkv), F32),
                   jax.ShapeDtypeStruct((KV_PAIRS, PAIR_ROWS, 2 * BLOCK), F32),
                   jax.ShapeDtypeStruct((2 * Q_PER_KV, KV_PAIRS), F32)],
        scratch_shapes=[pltpu.VMEM((PAIR_ROWS, KV_PAIRS), F32)], compiler_params=_params(1))


def _sgu_bwd(c, name, a, z, dgated, ln_g, wc, wc_t, b_t, dep=None):
    S, AW, gd, tm = c.S, c.AW, c.gd, c.tms

    def body(a_ref, z_ref, dg_ref, lng_ref, wc_ref, wct_ref, bt_ref, dz_ref, dws_ref, dbt_ref, dlng_ref, dvn_ref):
        i = pl.program_id(0)

        @pl.when(i == 0)
        def _():
            dws_ref[...] = jnp.zeros_like(dws_ref)
            dbt_ref[...] = jnp.zeros_like(dbt_ref)
            dlng_ref[...] = jnp.zeros_like(dlng_ref)

        lng = lng_ref[...]
        va = a_ref[:, AW:].astype(F32)
        xc = va - jnp.mean(va, axis=-1, keepdims=True)
        rstd = lax.rsqrt(jnp.mean(xc * xc, axis=-1, keepdims=True) + EPS)
        xh = xc * rstd
        vn = (xh * lng).astype(BF16)
        causal = _sgu_masks()
        for ch in range(tm // CHUNK):
            rows = slice(ch * CHUNK, (ch + 1) * CHUNK)
            for g in range(A_GROUPS):
                cols = slice(g * gd, (g + 1) * gd)
                blk = vn[rows, cols]
                mixed = _dot(wc_ref[g], blk, NN) + bt_ref[:, g:g + 1]
                dgb = dg_ref[rows, cols].astype(F32)
                dm = dgb * a_ref[rows, cols].astype(F32)
                dbt_ref[:, g:g + 1] += jnp.sum(dm, axis=1, keepdims=True)
                dm16 = dm.astype(BF16)
                dws_ref[g] += jnp.where(causal, _dot(dm16, blk, NT), 0.0)
                dvn_ref[rows, cols] = _dot(wct_ref[g], dm16, NN)
                dz_ref[rows, cols] = (dgb * mixed * _gelu_grad(z_ref[rows, cols].astype(F32))).astype(BF16)
        dvn = dvn_ref[...]
        dlng_ref[...] += jnp.sum(dvn * xh, axis=0, keepdims=True)
        dxh = dvn * lng
        dva = rstd * (dxh - jnp.mean(dxh, axis=-1, keepdims=True) - xh * jnp.mean(dxh * xh, axis=-1, keepdims=True))
        dz_ref[:, AW:] = (dva * _gelu_grad(z_ref[:, AW:].astype(F32))).astype(BF16)

    wide = pl.BlockSpec((tm, 2 * AW), lambda i: (i, 0))
    wsp = pl.BlockSpec((A_GROUPS, CHUNK, CHUNK), lambda i: (0, 0, 0))
    btsp = pl.BlockSpec((CHUNK, A_GROUPS), lambda i: (0, 0))
    return _pcall(
        body, name=name, ins=[a, z, dgated, ln_g, wc, wc_t, b_t], dep=dep, grid=(S // tm,),
        in_specs=[wide, wide, pl.BlockSpec((tm, AW), lambda i: (i, 0)), pl.BlockSpec((1, AW), lambda i: (0, 0)),
                  wsp, wsp, btsp],
        out_specs=[wide, wsp, btsp, pl.BlockSpec((1, AW), lambda i: (0, 0))],
        out_shape=[jax.ShapeDtypeStruct((S, 2 * AW), BF16), jax.ShapeDtypeStruct((A_GROUPS, CHUNK, CHUNK), F32),
                   jax.ShapeDtypeStruct((CHUNK, A_GROUPS), F32), jax.ShapeDtypeStruct((1, AW), F32)],
        scratch_shapes=[pltpu.VMEM((tm, AW), F32)], compiler_params=_params(1))


def _adamw(name, parts, part_block, part_index, w, m, v, tr, row_off=0, n_rows=None, prev=None, dep=None):
    R, C = w.shape
    n_rows = R if n_rows is None else n_rows
    assert n_rows % tr == 0 and row_off % tr == 0
    bc1 = 1.0 - ADAM_B1 ** ADAM_STEP
    bc2 = 1.0 - ADAM_B2 ** ADAM_STEP

    def body(p_ref, w_ref, m_ref, v_ref, *rest):
        g_ref, d_ref, nm_ref, nv_ref = rest[-4:]
        g = p_ref[0].astype(F32)
        for s in range(1, NDEV):
            g = g + p_ref[s].astype(F32)
        nm = ADAM_B1 * m_ref[...] + (1.0 - ADAM_B1) * g
        nv = ADAM_B2 * v_ref[...] + (1.0 - ADAM_B2) * (g * g)
        g_ref[...] = g
        nm_ref[...] = nm
        nv_ref[...] = nv
        d_ref[...] = -ADAM_LR * ((nm / bc1) / (jnp.sqrt(nv / bc2) + ADAM_EPS) + ADAM_WD * w_ref[...])

    ob = row_off // tr
    row = pl.BlockSpec((tr, C), lambda i: (ob + i, 0))
    out = jax.ShapeDtypeStruct((R, C), F32)
    chained = prev is not None
    return _pcall(
        body, name=name, ins=[parts, w, m, v] + (list(prev) if chained else []), dep=dep, grid=(n_rows // tr,),
        in_specs=[pl.BlockSpec(part_block, part_index), row, row, row] + ([ANY] * 4 if chained else []),
        out_specs=[row, row, row, row], out_shape=[out, out, out, out],
        input_output_aliases={4 + t: t for t in range(4)} if chained else {}, compiler_params=_params(1))


def _place():
    return lax.axis_index("x"), lax.axis_index("y"), lax.axis_index("c")


def _slot(px, py, pc):
    return 4 * px + 2 * py + pc


def _peer(k, x, y, c):
    return x ^ ((k >> 2) & 1), y ^ ((k >> 1) & 1), c ^ (k & 1)


def _all_gather(name, arrs, dep=None):
    n = len(arrs)

    def body(*refs):
        in_refs, out_refs = refs[:n], refs[n:2 * n]
        send_sems, recv_sems, local_sems = refs[2 * n:]
        x, y, c = _place()
        me, sibling = (x, y, c), (x, y, 1 - c)
        chips = [(1 - x, y), (x, 1 - y), (1 - x, 1 - y)]

        def copy(a, k, block, to, src=None):
            dst = out_refs[a].at[_slot(*block)]
            return pltpu.make_async_remote_copy(
                src_ref=dst if src is None else src, dst_ref=dst, send_sem=send_sems.at[a, k],
                recv_sem=recv_sems.at[a, k], device_id=to, device_id_type=MESH)

        mine = [pltpu.make_async_copy(in_refs[a], out_refs[a].at[_slot(*me)], local_sems.at[a]) for a in range(n)]
        for cp in mine:
            cp.start()
        first = []
        for a in range(n):
            first.append(copy(a, 0, me, sibling, src=in_refs[a]))
            first += [copy(a, 1 + j, me, (*chip, c), src=in_refs[a]) for j, chip in enumerate(chips)]
        for cp in first:
            cp.start()
        passed = []
        for j, chip in enumerate(chips):
            for a in range(n):
                copy(a, 1 + j, (*chip, c), me).wait_recv()
                fwd = copy(a, 4 + j, (*chip, c), sibling)
                fwd.start()
                passed.append(fwd)
        for a in range(n):
            copy(a, 0, sibling, me).wait_recv()
            for j, chip in enumerate(chips):
                copy(a, 4 + j, (*chip, 1 - c), me).wait_recv()
        for cp in first + passed:
            cp.wait_send()
        for cp in mine:
            cp.wait()

    return _pcall(
        body, name=name, ins=list(arrs), in_specs=[HBM] * n, dep=dep,
        out_shape=[jax.ShapeDtypeStruct((NDEV, *a.shape), a.dtype) for a in arrs], out_specs=[HBM] * n,
        scratch_shapes=[pltpu.SemaphoreType.DMA((n, 7)), pltpu.SemaphoreType.DMA((n, 7)),
                        pltpu.SemaphoreType.DMA((n,))],
        compiler_params=pltpu.CompilerParams(has_side_effects=True))


def _send_copy(gather, src_ref, land_ref, send_sems, recv_sems, a, k, place):
    x, y, c = place
    peer = _peer(k, x, y, c)
    return pltpu.make_async_remote_copy(
        src_ref=src_ref if gather else src_ref.at[_slot(*peer)], dst_ref=land_ref.at[_slot(x, y, c)],
        send_sem=send_sems.at[a * (NDEV - 1) + k - 1], recv_sem=recv_sems.at[a * (NDEV - 1) + k - 1],
        device_id=peer, device_id_type=MESH)


def _send_start(name, srcs, lands, gather, dep=None):
    n = len(srcs)

    def body(*refs):
        src_refs, land_refs = refs[:n], refs[n:2 * n]
        send_sems, recv_sems = refs[2 * n], refs[2 * n + 1]
        token = refs[-1]
        place = _place()
        for k in range(1, NDEV):
            for a in range(n):
                _send_copy(gather, src_refs[a], land_refs[a], send_sems, recv_sems, a, k, place).start()
        token[...] = jnp.zeros_like(token)

    thru = [pltpu.HBM(t.shape, t.dtype) for t in [*srcs, *lands]]
    out = _pcall(
        body, name=name, ins=[pltpu.with_memory_space_constraint(t, pltpu.HBM) for t in [*srcs, *lands]],
        in_specs=[HBM] * (2 * n), dep=dep,
        out_shape=(pltpu.SemaphoreType.DMA((n * (NDEV - 1),)), pltpu.SemaphoreType.DMA((n * (NDEV - 1),)), *thru,
                   jax.ShapeDtypeStruct((8, LANES), F32)),
        out_specs=(SEM, SEM, *[HBM] * (2 * n), pl.BlockSpec(memory_space=pltpu.VMEM)),
        input_output_aliases={i: 2 + i for i in range(2 * n)},
        compiler_params=pltpu.CompilerParams(has_side_effects=pltpu.SideEffectType.DATAFLOW_SIDE_EFFECTING))
    return out[-1], out[0], out[1], list(out[2:2 + n]), list(out[2 + n:2 + 2 * n])


def _send_wait(name, started, gather, dep=None):
    _, send_sems, recv_sems, srcs, lands = started
    n = len(srcs)

    def body(*refs):
        src_refs, land_refs = refs[:n], refs[n:2 * n]
        ssem, rsem = refs[2 * n], refs[2 * n + 1]
        place = _place()
        for k in range(1, NDEV):
            for a in range(n):
                cp = _send_copy(gather, src_refs[a], land_refs[a], ssem, rsem, a, k, place)
                cp.wait_send()
                cp.wait_recv()

    thru = [pltpu.HBM(t.shape, t.dtype) for t in [*srcs, *lands]]
    out = _pcall(
        body, name=name, ins=[*srcs, *lands, send_sems, recv_sems], in_specs=[HBM] * (2 * n) + [SEM, SEM], dep=dep,
        out_shape=tuple(thru), out_specs=tuple([HBM] * (2 * n)), input_output_aliases={i: i for i in range(2 * n)},
        compiler_params=pltpu.CompilerParams(has_side_effects=pltpu.SideEffectType.DATAFLOW_SIDE_EFFECTING))
    return list(out[n:])


def _landing(block, me):
    zone = lax.empty((NDEV, *block.shape), block.dtype)
    return lax.dynamic_update_slice(zone, block[None], (me,) + (0,) * block.ndim)


def _rows128(t):
    flat = t.reshape(-1)
    n = flat.shape[0]
    rows = -(-n // (8 * LANES)) * 8
    return jnp.pad(flat, (0, rows * LANES - n)).reshape(rows, LANES)


def kernel(x, mix_norm_g, ffn_norm_g, a_w_in, a_ln_g, a_w_spatial, a_b_spatial, a_w_out, kv_norm_g, w_k, w_v, b_w_q, b_sinks, b_w_o, rel_bias, ffn_w1, ffn_w2, final_norm_g, loss_target, m_mix_norm_g, m_ffn_norm_g, m_a_w_in, m_a_ln_g, m_a_w_spatial, m_a_b_spatial, m_a_w_out, m_kv_norm_g, m_w_k, m_w_v, m_b_w_q, m_b_sinks, m_b_w_o, m_rel_bias, m_ffn_w1, m_ffn_w2, m_final_norm_g, v_mix_norm_g, v_ffn_norm_g, v_a_w_in, v_a_ln_g, v_a_w_spatial, v_a_b_spatial, v_a_w_out, v_kv_norm_g, v_w_k, v_w_v, v_b_w_q, v_b_sinks, v_b_w_o, v_rel_bias, v_ffn_w1, v_ffn_w2, v_final_norm_g):
    c = _config(x, a_w_in, a_w_out, w_k, b_w_q, b_w_o, ffn_w1, ffn_w2)
    S, D, LA, LB, LF = c.S, c.D, c.LA, c.LB, c.LF
    weights = dict(mix_norm_g=mix_norm_g, ffn_norm_g=ffn_norm_g, a_w_in=a_w_in, a_ln_g=a_ln_g, a_w_spatial=a_w_spatial,
                   a_b_spatial=a_b_spatial, a_w_out=a_w_out, kv_norm_g=kv_norm_g, w_k=w_k, w_v=w_v, b_w_q=b_w_q,
                   b_sinks=b_sinks, b_w_o=b_w_o, rel_bias=rel_bias, ffn_w1=ffn_w1, ffn_w2=ffn_w2,
                   final_norm_g=final_norm_g)
    m_in = dict(mix_norm_g=m_mix_norm_g, ffn_norm_g=m_ffn_norm_g, a_w_in=m_a_w_in, a_ln_g=m_a_ln_g,
                a_w_spatial=m_a_w_spatial, a_b_spatial=m_a_b_spatial, a_w_out=m_a_w_out, kv_norm_g=m_kv_norm_g,
                w_k=m_w_k, w_v=m_w_v, b_w_q=m_b_w_q, b_sinks=m_b_sinks, b_w_o=m_b_w_o, rel_bias=m_rel_bias,
                ffn_w1=m_ffn_w1, ffn_w2=m_ffn_w2, final_norm_g=m_final_norm_g)
    v_in = dict(mix_norm_g=v_mix_norm_g, ffn_norm_g=v_ffn_norm_g, a_w_in=v_a_w_in, a_ln_g=v_a_ln_g,
                a_w_spatial=v_a_w_spatial, a_b_spatial=v_a_b_spatial, a_w_out=v_a_w_out, kv_norm_g=v_kv_norm_g,
                w_k=v_w_k, w_v=v_w_v, b_w_q=v_b_w_q, b_sinks=v_b_sinks, b_w_o=v_b_w_o, rel_bias=v_rel_bias,
                ffn_w1=v_ffn_w1, ffn_w2=v_ffn_w2, final_norm_g=v_final_norm_g)
    names = list(weights)
    seq = _Seq()
    me = _slot(*_place())
    bf = lambda t: t.astype(BF16)

    def rows_of(l):
        second = a_w_out[l] if l < LA else b_w_o[l - LA]
        return bf(jnp.concatenate([ffn_w2[l], second], axis=0))

    groups = [[bf(a_w_in[0])[None], a_ln_g],
              [bf(ffn_w1[0])[None], rows_of(0)]]
    for l in range(1, LA):
        groups.append([bf(jnp.stack([a_w_in[l], ffn_w1[l]])), rows_of(l)])
    for l in range(LB):
        extra = [bf(jnp.concatenate([w_k, w_v], axis=1))] if l == 0 else []
        groups.append(extra + [bf(b_w_q[l]), bf(ffn_w1[LA + l])[None], rows_of(LA + l)])
    started = [seq(_send_start, f"weights_start{i}", grp, [_landing(t, me) for t in grp], True)
               for i, grp in enumerate(groups)]

    def arrive(i):
        return seq(_send_wait, f"weights_wait{i}", started[i], True)

    causal = jnp.tril(jnp.ones((CHUNK, CHUNK), bool))
    wsp = jnp.where(causal[None, None], a_w_spatial, 0.0)
    wsp16 = wsp.astype(BF16)
    wsp16_t = jnp.swapaxes(wsp, -1, -2).astype(BF16)
    bsp_t = jnp.swapaxes(a_b_spatial, -1, -2)
    mix_g = mix_norm_g.reshape(-1, 1, D)
    ffn_g = ffn_norm_g.reshape(-1, 1, D)
    kv_g = kv_norm_g.reshape(1, D)
    fin_g = final_norm_g.reshape(1, D)
    onehot = _bucket_onehot()
    bias = _slot_bias(seq(_band_bias, rel_bias.T, onehot).reshape(N_HEADS, BLOCK, 2 * BLOCK))

    h = x.reshape(S, D)
    sav_a, sav_b, wts_a, wts_b = [], [], [], []
    for l in range(LA):
        if l == 0:
            w_in, lng_all = arrive(0)
            ln_g_full = jnp.transpose(lng_all, (1, 0, 2)).reshape(LA, 1, c.AW)
            in_i, w1_i = 0, 0
        else:
            w_in, rows = arrive(l + 1)
            w1, in_i, w1_i = w_in, 0, 1
        z, a, hn = seq(_a_in_fwd, c, f"a_in_fwd{l}", h, mix_g[l], w_in, in_i)
        gated = seq(_sgu_fwd, c, f"sgu_fwd{l}", a, ln_g_full[l], wsp16[l], bsp_t[l])
        if l == 0:
            w1, rows = arrive(1)
        h1 = seq(_mm_res, c, f"a_out_fwd{l}", gated, rows, c.ar, c.fr // c.ar, h)
        p, h2, hnf = seq(_ffn_fwd, c, f"ffn_fwd{l}", h1, ffn_g[l], w1, w1_i, rows)
        sav_a.append((h, z, a, hn, gated, h1, p, hnf))
        wts_a.append((w_in, in_i, w1, w1_i, rows))
        h = h2
    h_kv = h
    for l in range(LB):
        got = arrive(LA + 1 + l)
        if l == 0:
            wkv, got = got[0], got[1:]
            kv, hkv = seq(_rms_mm_rows, c, "kv_fwd", h, kv_g, wkv, c.kr, 0, 2 * c.DKV)
        wq, w1, rows = got
        wq = _slot_cols(wq)
        wo = _slot_rows(rows[:, c.fr:, :])
        q, hn = seq(_rms_mm_rows, c, f"q_fwd{l}", h, mix_g[LA + l], wq, c.qr, 0, c.DQ)
        o = seq(_attn_fwd, c, f"attn_fwd{l}", q, kv, bias, b_sinks[l])
        h1 = seq(_mm_res, c, f"o_fwd{l}", o, wo, c.orr, 0, h)
        p, h2, hnf = seq(_ffn_fwd, c, f"ffn_fwd{LA + l}", h1, ffn_g[LA + l], w1, 0, rows)
        sav_b.append((h, q, hn, o, h1, p, hnf))
        wts_b.append((wq, wo, w1, rows))
        h = h2
    dh, d_fin_g, loss_row = seq(_final_loss, c, h, fin_g, loss_target.reshape(S, D))
    loss = lax.psum(loss_row[0, 0], MESH_AXES)

    results = {}
    in_flight = []

    def update(k, parts, layer, col_blk=0):
        w = weights[k]
        rows_l, ncols = (w.shape[-2], w.shape[-1]) if w.ndim == 3 else w.shape
        flat = lambda t: t.reshape(-1, ncols)
        tr = min(256, rows_l)
        results[k] = seq(_adamw, f"adamw_{k}{layer}", parts, (NDEV, tr, ncols), lambda i: (0, i, col_blk),
                         flat(w), flat(m_in[k]), flat(v_in[k]), tr, row_off=layer * rows_l, n_rows=rows_l,
                         prev=results.get(k))

    def land(tag, entry):
        lands = seq(_send_wait, f"grads_wait_{tag}", entry[1], False)
        for keys, parts in zip(entry[0], lands):
            for k, layer, col_blk in keys:
                update(k, parts, layer, col_blk)

    def send(tag, items):
        slabs = [t for _, t in items]
        own = [_landing(lax.dynamic_index_in_dim(t, me, 0, keepdims=False), me) for t in slabs]
        st = seq(_send_start, f"grads_start_{tag}", slabs, own, False)
        in_flight.append((tag, ([k for k, _ in items], st)))
        while len(in_flight) > EXCHANGE_LAG:
            land(*in_flight.pop(0))

    d_mix_g, d_ffn_g = [None] * LF, [None] * LF
    dkv_list, dbias_list, dsink_list = [], [], [None] * LB

    def ffn_bwd(lf, dh, h1, p, hnf, w1, w1_i, rows):
        da, dh1, d_ffn_g[lf] = seq(_ffn_bwd_data, c, f"ffn_bwd_data{lf}", dh, p, w1, w1_i, rows, h1, ffn_g[lf])
        dw1, dw2 = seq(_ffn_bwd_w, c, f"ffn_bwd_w{lf}", hnf, da, p, dh)
        send(f"ffn{lf}", [([("ffn_w1", lf, 0)], dw1), ([("ffn_w2", lf, 0)], dw2)])
        return dh1

    for l in reversed(range(LB)):
        h0, q, hn, o, h1, p, hnf = sav_b[l]
        wq, wo, w1, rows = wts_b[l]
        dh1 = ffn_bwd(LA + l, dh, h1, p, hnf, w1, 0, rows)
        do = seq(_bwd_rows_data, c, f"o_bwd_data{l}", dh1, wo, c.orr, 0)
        dwo = _unslot_rows(seq(_wgrad_rows, c, f"o_bwd_w{l}", o, [dh1], c.DQ, D))
        dq, dkv, dbias, dsink = seq(_attn_bwd, c, f"attn_bwd{l}", q, kv, do, bias, b_sinks[l])
        dsink_list[l] = dsink.reshape(Q_PER_KV, 2, KV_PAIRS).transpose(2, 1, 0).reshape(1, N_HEADS)
        dkv_list.append(dkv)
        dbias_list.append(_unslot_bias(dbias))
        dwq = _unslot_cols(seq(_wgrad_rows, c, f"q_bwd_w{l}", hn, [dq], D, c.DQ))
        send(f"attn{l}", [([("b_w_o", l, 0)], dwo), ([("b_w_q", l, 0)], dwq)])
        dh, d_mix_g[LA + l] = seq(_bwd_rows_to_stream, c, f"q_bwd_data{l}", [dq], wq, c.qr, 0, c.DQ, h0,
                                  mix_g[LA + l], dh1)
    dwkv = seq(_wgrad_rows, c, "kv_bwd_w", hkv, dkv_list, D, 2 * c.DKV)
    send("kv", [([("w_k", 0, 0), ("w_v", 0, 1)], dwkv)])
    dh, d_kv_g = seq(_bwd_rows_to_stream, c, "kv_bwd_data", dkv_list, wkv, c.kr, 0, 2 * c.DKV, h_kv, kv_g, dh)
    d_rel_t = seq(_band_bias_grad, dbias_list, onehot)
    d_wsp, d_bsp, d_lng = [None] * LA, [None] * LA, [None] * LA
    for l in reversed(range(LA)):
        h0, z, a, hn, gated, h1, p, hnf = sav_a[l]
        w_in, in_i, w1, w1_i, rows = wts_a[l]
        dh1 = ffn_bwd(l, dh, h1, p, hnf, w1, w1_i, rows)
        dgated = seq(_bwd_rows_data, c, f"a_out_bwd_data{l}", dh1, rows, c.ar, c.fr // c.ar)
        dwout = seq(_wgrad_rows, c, f"a_out_bwd_w{l}", gated, [dh1], c.AW, D)
        send(f"a_out{l}", [([("a_w_out", l, 0)], dwout)])
        dz, d_wsp[l], dbt, d_lng[l] = seq(_sgu_bwd, c, f"sgu_bwd{l}", a, z, dgated, ln_g_full[l], wsp16[l],
                                          wsp16_t[l], bsp_t[l])
        d_bsp[l] = dbt.T
        dwin = seq(_wgrad_cols, c, f"a_in_bwd_w{l}", hn, dz)
        send(f"a_in{l}", [([("a_w_in", l, 0)], dwin)])
        dh, d_mix_g[l] = seq(_bwd_cols_to_stream, c, f"a_in_bwd_data{l}", dz, w_in, in_i, h0, mix_g[l], dh1)
    grad_x = dh.reshape(1, S, D)

    small = {
        "mix_norm_g": jnp.concatenate(d_mix_g, axis=0), "ffn_norm_g": jnp.concatenate(d_ffn_g, axis=0),
        "a_w_spatial": jnp.stack(d_wsp), "a_b_spatial": jnp.stack(d_bsp), "kv_norm_g": d_kv_g,
        "b_sinks": jnp.concatenate(dsink_list, axis=0), "rel_bias": d_rel_t.T, "final_norm_g": d_fin_g,
    }
    small_names = list(small)
    packs = [_rows128(small[k]) for k in small_names] + [_rows128(jnp.concatenate(d_lng, axis=0))]
    offs = np.cumsum([0] + [p.shape[0] for p in packs])
    (small_all,) = seq(_all_gather, "gather_small_grads", [jnp.concatenate(packs, axis=0)])
    Rs = int(offs[-1])
    while in_flight:
        land(*in_flight.pop(0))

    grads, deltas, new_m, new_v = {}, {}, {}, {}

    def put(k, outs, shape):
        grads[k], deltas[k], new_m[k], new_v[k] = (t.reshape(shape) for t in outs)

    def pack_state(d):
        return jnp.concatenate([_rows128(d[k]) for k in small_names]
                               + [jnp.zeros((packs[-1].shape[0], LANES), F32)], axis=0)

    outs = seq(_adamw, "adamw_small", small_all, (NDEV, Rs, LANES), lambda i: (0, 0, 0),
               pack_state(weights), pack_state(m_in), pack_state(v_in), Rs)
    for n_, k in enumerate(small_names):
        shape = weights[k].shape
        size = int(np.prod(shape))
        put(k, [t[int(offs[n_]):int(offs[n_ + 1])].reshape(-1)[:size] for t in outs], shape)
    lng_sum = outs[0][int(offs[-2]):int(offs[-1])].reshape(-1)[:LA * c.AW].reshape(LA, c.AW)
    lng_mine = lax.dynamic_slice_in_dim(lng_sum, me * c.ar, c.ar, axis=1)
    lng_parts = jnp.concatenate([lng_mine[None], jnp.zeros((NDEV - 1, LA, c.ar), F32)], axis=0)
    put("a_ln_g", seq(_adamw, "adamw_ln_g", lng_parts, (NDEV, LA, c.ar), lambda i: (0, 0, 0),
                      a_ln_g, m_in["a_ln_g"], v_in["a_ln_g"], LA), a_ln_g.shape)
    for k in ("a_w_in", "ffn_w1", "ffn_w2", "a_w_out", "b_w_o", "b_w_q", "w_k", "w_v"):
        put(k, results[k], weights[k].shape)

    return (loss, grad_x, *[grads[k] for k in names], *[deltas[k] for k in names],
            *[new_m[k] for k in names], *[new_v[k] for k in names])
```

```python
import numpy as np
import math
import jax
import jax.numpy as jnp
from jax import lax
from jax.experimental import pallas as pl
from jax.experimental.pallas import tpu as pltpu

F32 = jnp.float32
BF16 = jnp.bfloat16

NDEV = 8
EPS = 1e-6
CHUNK = 128
A_GROUPS = 8
N_HEADS = 16
N_KV_HEADS = 4
Q_PER_KV = N_HEADS // N_KV_HEADS
HEAD_DIM = 64
BLOCK = 128
N_BUCKETS = 32
MAX_DISTANCE = 128
ADAM_LR, ADAM_B1, ADAM_B2, ADAM_EPS, ADAM_WD, ADAM_STEP = 0.001, 0.9, 0.999, 1e-08, 0.01, 10
LANES = 128
VMEM_LIMIT = 56 * 1024 * 1024
INV_SQRT2 = 0.7071067811865476
INV_SQRT_2PI = 0.3989422804014327
MESH_AXES = ("x", "y", "c")
EXCHANGE_LAG = 2

HBM = pl.BlockSpec(memory_space=pltpu.HBM)
SMEM = pl.BlockSpec(memory_space=pltpu.SMEM)
ANY = pl.BlockSpec(memory_space=pl.ANY)
SEM = pl.BlockSpec(memory_space=pltpu.SEMAPHORE)
MESH = pl.DeviceIdType.MESH


def _params(n_grid):
    return pltpu.CompilerParams(dimension_semantics=("arbitrary",) * n_grid, vmem_limit_bytes=VMEM_LIMIT)


def _const(block, index_map):
    return pl.BlockSpec(block, index_map, pipeline_mode=pl.Buffered(1))


def _pcall(body, *, ins, in_specs, dep=None, **kw):
    n_in = len(ins)
    if dep is None:
        return pl.pallas_call(body, in_specs=list(in_specs), **kw)(*ins)

    def with_dep(*refs):
        body(*refs[:n_in], *refs[n_in + 1:])

    return pl.pallas_call(with_dep, in_specs=[*in_specs, ANY], **kw)(*ins, dep)


class _Seq:
    def __init__(self):
        self.last = None

    def __call__(self, fn, *args, **kw):
        out = fn(*args, dep=self.last, **kw)
        self.last = out[0] if isinstance(out, (tuple, list)) else out
        return out


def _rstd(h):
    return lax.rsqrt(jnp.mean(h * h, axis=-1, keepdims=True) + EPS)


def _rms_bwd(dhn, h, g, dres):
    r = _rstd(h)
    xh = h * r
    dg = jnp.sum(dhn * xh, axis=0, keepdims=True)
    dxh = dhn * g
    dx = r * (dxh - xh * jnp.mean(dxh * xh, axis=-1, keepdims=True))
    return dres + dx, dg


def _gelu(z):
    return 0.5 * z * (1.0 + lax.erf(z * INV_SQRT2))


def _gelu_grad(z):
    return 0.5 * (1.0 + lax.erf(z * INV_SQRT2)) + z * (jnp.exp(-0.5 * z * z) * INV_SQRT_2PI)


def _dot(a, b, dims):
    return lax.dot_general(a, b, (dims, ((), ())), preferred_element_type=F32)


NN = ((1,), (0,))
NT = ((1,), (1,))
TN = ((0,), (0,))


def _mm(name, ins, in_specs, out_shapes, out_specs, *, grid, dims, nk, acc_shape, load_a, load_b, epilogue,
        dep=None):
    n_in, n_out = len(ins), len(out_shapes)
    kax = len(grid) - 1

    def body(*refs):
        in_refs = refs[:n_in]
        out_refs = refs[n_in:n_in + n_out]
        a = load_a(in_refs, out_refs)
        b = load_b(in_refs)
        prod = _dot(a, b, dims)
        if nk == 1:
            epilogue(prod, in_refs, out_refs)
        else:
            acc = refs[n_in + n_out]
            k = pl.program_id(kax)

            @pl.when(k == 0)
            def _():
                acc[...] = prod

            @pl.when(k > 0)
            def _():
                acc[...] += prod

            @pl.when(k == nk - 1)
            def _():
                epilogue(acc[...], in_refs, out_refs)

    return _pcall(
        body, name=name, ins=ins, in_specs=in_specs, dep=dep, grid=grid, out_specs=out_specs, out_shape=out_shapes,
        scratch_shapes=[pltpu.VMEM(acc_shape, F32)] if nk > 1 else [], compiler_params=_params(len(grid)))


def _bf(ref_idx):
    return lambda in_refs, *_: in_refs[ref_idx][...].astype(BF16)


def _b_view(ref_idx, rows):
    def load(in_refs):
        b = in_refs[ref_idx][...]
        return b.reshape(rows, b.shape[-1])
    return load


class Cfg:
    pass


def _config(x, a_w_in, a_w_out, w_k, b_w_q, b_w_o, ffn_w1, ffn_w2):
    c = Cfg()
    c.S, c.D = x.shape[1], x.shape[2]
    c.LA, _, c.cw = a_w_in.shape
    c.AW2 = NDEV * c.cw
    c.AW = c.AW2 // 2
    c.gd = c.AW // A_GROUPS
    c.ar = a_w_out.shape[1]
    c.LF, _, c.fw = ffn_w1.shape
    c.fr = ffn_w2.shape[1]
    c.LB, c.qr, c.DQ = b_w_q.shape
    c.orr = b_w_o.shape[1]
    c.kr, c.DKV = w_k.shape
    c.tm = min(512, c.S)
    c.tms = min(256, c.S)
    c.nb = c.S // BLOCK
    assert c.cw == c.fw == c.fr and c.AW == NDEV * c.ar and c.D == NDEV * c.qr == NDEV * c.kr
    assert c.DQ == NDEV * c.orr == N_HEADS * HEAD_DIM and c.DKV == N_KV_HEADS * HEAD_DIM
    assert c.S % c.tm == 0 and c.S % c.tms == 0 and c.tms % CHUNK == 0 and c.gd % LANES == 0
    assert c.fr % c.ar == 0 and c.fr % c.orr == 0
    assert c.LA >= 1 and c.LB >= 1 and c.LF == c.LA + c.LB
    return c


def _cached_rms(h_idx, g_idx, hn_out_idx, jax_axis=1):
    def load(in_refs, out_refs):
        hn_ref = out_refs[hn_out_idx]

        @pl.when(pl.program_id(jax_axis) == 0)
        def _():
            h = in_refs[h_idx][...]
            hn_ref[...] = (h * _rstd(h) * in_refs[g_idx][...]).astype(BF16)

        return hn_ref[...]
    return load


def _a_in_fwd(c, name, h, g, col, ci, dep=None):
    S, D, cw, tm = c.S, c.D, c.cw, c.tm

    def body(h_ref, g_ref, w_ref, z_ref, a_ref, hn_ref):
        h = h_ref[...]
        hn = (h * _rstd(h) * g_ref[...]).astype(BF16)
        hn_ref[...] = hn
        for j in range(NDEV):
            cols = slice(j * cw, (j + 1) * cw)
            z = _dot(hn, w_ref[j], NN)
            z_ref[:, cols] = z.astype(BF16)
            a_ref[:, cols] = _gelu(z).astype(BF16)

    row = pl.BlockSpec((tm, D), lambda i: (i, 0))
    wide = pl.BlockSpec((tm, c.AW2), lambda i: (i, 0))
    return _pcall(
        body, name=name, ins=[h, g, col], dep=dep, grid=(S // tm,),
        in_specs=[row, pl.BlockSpec((1, D), lambda i: (0, 0)), _const((NDEV, None, D, cw), lambda i: (0, ci, 0, 0))],
        out_specs=[wide, wide, row],
        out_shape=[jax.ShapeDtypeStruct((S, c.AW2), BF16), jax.ShapeDtypeStruct((S, c.AW2), BF16),
                   jax.ShapeDtypeStruct((S, D), BF16)],
        compiler_params=_params(1))


def _rms_mm_rows(c, name, h, g, slab, blk_rows, blk_idx, n_out, dep=None):
    S, D, tm = c.S, c.D, c.tm

    def epilogue(acc, in_refs, out_refs):
        out_refs[0][...] = acc.astype(BF16)

    return _mm(
        name, [h, slab, g],
        [pl.BlockSpec((tm, D), lambda i, j, k: (i, 0)),
         pl.BlockSpec((NDEV, blk_rows, n_out), lambda i, j, k: (0, blk_idx, 0)),
         pl.BlockSpec((1, D), lambda i, j, k: (0, 0))],
        [jax.ShapeDtypeStruct((S, n_out), BF16), jax.ShapeDtypeStruct((S, D), BF16)],
        [pl.BlockSpec((tm, n_out), lambda i, j, k: (i, 0)), pl.BlockSpec((tm, D), lambda i, j, k: (i, 0))],
        grid=(S // tm, 1, 1), dims=NN, nk=1, acc_shape=None,
        load_a=_cached_rms(0, 2, 1), load_b=_b_view(1, NDEV * blk_rows), epilogue=epilogue, dep=dep)


def _mm_res(c, name, a, slab, blk_rows, blk_idx, res, dep=None):
    S, D, tm = c.S, c.D, c.tm
    K = NDEV * blk_rows

    def epilogue(acc, in_refs, out_refs):
        out_refs[0][...] = in_refs[2][...] + acc

    return _mm(
        name, [a, slab, res],
        [pl.BlockSpec((tm, K), lambda i, j, k: (i, 0)),
         pl.BlockSpec((NDEV, blk_rows, D), lambda i, j, k: (0, blk_idx, 0)),
         pl.BlockSpec((tm, D), lambda i, j, k: (i, 0))],
        [jax.ShapeDtypeStruct((S, D), F32)], [pl.BlockSpec((tm, D), lambda i, j, k: (i, 0))],
        grid=(S // tm, 1, 1), dims=NN, nk=1, acc_shape=None,
        load_a=_bf(0), load_b=_b_view(1, K), epilogue=epilogue, dep=dep)[0]


def _sgu_masks():
    ii = lax.broadcasted_iota(jnp.int32, (CHUNK, CHUNK), 0)
    jj = lax.broadcasted_iota(jnp.int32, (CHUNK, CHUNK), 1)
    return ii >= jj


def _sgu_fwd(c, name, a, ln_g, wc, b_t, dep=None):
    S, AW, gd, tm = c.S, c.AW, c.gd, c.tms

    def body(a_ref, lng_ref, wc_ref, bt_ref, out_ref):
        va = a_ref[:, AW:].astype(F32)
        xc = va - jnp.mean(va, axis=-1, keepdims=True)
        vn = (xc * lax.rsqrt(jnp.mean(xc * xc, axis=-1, keepdims=True) + EPS) * lng_ref[...]).astype(BF16)
        for ch in range(tm // CHUNK):
            rows = slice(ch * CHUNK, (ch + 1) * CHUNK)
            for g in range(A_GROUPS):
                cols = slice(g * gd, (g + 1) * gd)
                mixed = _dot(wc_ref[g], vn[rows, cols], NN) + bt_ref[:, g:g + 1]
                out_ref[rows, cols] = (a_ref[rows, cols].astype(F32) * mixed).astype(BF16)

    return _pcall(
        body, name=name, ins=[a, ln_g, wc, b_t], dep=dep, grid=(S // tm,),
        in_specs=[pl.BlockSpec((tm, 2 * AW), lambda i: (i, 0)), pl.BlockSpec((1, AW), lambda i: (0, 0)),
                  pl.BlockSpec((A_GROUPS, CHUNK, CHUNK), lambda i: (0, 0, 0)),
                  pl.BlockSpec((CHUNK, A_GROUPS), lambda i: (0, 0))],
        out_specs=pl.BlockSpec((tm, AW), lambda i: (i, 0)),
        out_shape=jax.ShapeDtypeStruct((S, AW), BF16), compiler_params=_params(1))


def _ffn_fwd(c, name, h, g, col, ci, rows, dep=None):
    S, D, fw, tm = c.S, c.D, c.fw, c.tm
    F = NDEV * fw

    def body(h_ref, g_ref, w1_ref, w2_ref, p_ref, out_ref, hn_ref, r_ref):
        h = h_ref[...]
        hn = (h * _rstd(h) * g_ref[...]).astype(BF16)
        hn_ref[...] = hn
        for j in range(NDEV):
            cols = slice(j * fw, (j + 1) * fw)
            p = jnp.maximum(_dot(hn, w1_ref[j], NN), 0.0)
            p_ref[:, cols] = p.astype(BF16)
            r_ref[:, cols] = (p * p).astype(BF16)
        out_ref[...] = h + _dot(r_ref[...], w2_ref[...].reshape(F, D), NN)

    row = pl.BlockSpec((tm, D), lambda i: (i, 0))
    return _pcall(
        body, name=name, ins=[h, g, col, rows], dep=dep, grid=(S // tm,),
        in_specs=[row, pl.BlockSpec((1, D), lambda i: (0, 0)),
                  _const((NDEV, None, D, fw), lambda i: (0, ci, 0, 0)), _const((NDEV, c.fr, D), lambda i: (0, 0, 0))],
        out_specs=[pl.BlockSpec((tm, F), lambda i: (i, 0)), row, row],
        out_shape=[jax.ShapeDtypeStruct((S, F), BF16), jax.ShapeDtypeStruct((S, D), F32),
                   jax.ShapeDtypeStruct((S, D), BF16)],
        scratch_shapes=[pltpu.VMEM((tm, F), BF16)], compiler_params=_params(1))


def _bucket_table():
    qi = np.arange(BLOCK)[:, None]
    kj = np.arange(2 * BLOCK)[None, :]
    d = np.maximum(qi + BLOCK - kj, 0)
    max_exact = N_BUCKETS // 2
    ratio = np.log(np.maximum(d, 1).astype(np.float32) / np.float32(max_exact)) / np.float32(
        math.log(MAX_DISTANCE / max_exact))
    large = np.minimum(max_exact + (ratio.astype(np.float32) * np.float32(N_BUCKETS - max_exact)).astype(np.int32),
                       N_BUCKETS - 1)
    return np.where(d < max_exact, d, large).astype(np.int32)


def _bucket_onehot():
    b = jnp.asarray(_bucket_table().reshape(1, -1))
    return (b == lax.broadcasted_iota(jnp.int32, (N_BUCKETS, b.shape[1]), 0)).astype(F32)


def _whole(t):
    return pl.BlockSpec(t.shape, lambda: (0,) * t.ndim)


def _band_bias(rel_bias_t, onehot, dep=None):
    def body(r_ref, oh_ref, out_ref):
        out_ref[...] = lax.dot_general(r_ref[...], oh_ref[...], (NN, ((), ())), preferred_element_type=F32,
                                       precision=lax.Precision.HIGHEST)

    n = onehot.shape[1]
    return _pcall(body, name="band_bias", ins=[rel_bias_t, onehot], in_specs=[_whole(rel_bias_t), _whole(onehot)],
                  dep=dep, out_shape=jax.ShapeDtypeStruct((N_HEADS, n), F32), compiler_params=_params(0))


def _band_bias_grad(dbias_list, onehot, dep=None):
    n_in = len(dbias_list)

    def body(*refs):
        oh_ref, out_ref = refs[n_in], refs[n_in + 1]
        d = refs[0][...]
        for r in refs[1:n_in]:
            d = d + r[...]
        out_ref[...] = lax.dot_general(d, oh_ref[...], (NT, ((), ())), preferred_element_type=F32,
                                       precision=lax.Precision.HIGHEST)

    ins = [*dbias_list, onehot]
    return _pcall(body, name="band_bias_grad", ins=ins, in_specs=[_whole(t) for t in ins], dep=dep,
                  out_shape=jax.ShapeDtypeStruct((N_HEADS, N_BUCKETS), F32), compiler_params=_params(0))


KV_PAIRS = N_KV_HEADS // 2
PAIR_ROWS = 2 * Q_PER_KV * BLOCK
MASKED = float(np.finfo(np.float32).min) / 2


def _slot_cols(w):
    lead = w.shape[:-1]
    return w.reshape(*lead, KV_PAIRS, 2, Q_PER_KV, HEAD_DIM).swapaxes(-3, -2).reshape(*lead, N_HEADS * HEAD_DIM)


def _unslot_cols(w):
    lead = w.shape[:-1]
    return w.reshape(*lead, KV_PAIRS, Q_PER_KV, 2, HEAD_DIM).swapaxes(-3, -2).reshape(*lead, N_HEADS * HEAD_DIM)


def _slot_rows(blocks):
    n = blocks.shape[-1]
    return blocks.reshape(KV_PAIRS, 2, Q_PER_KV, HEAD_DIM, n).swapaxes(1, 2).reshape(blocks.shape)


def _unslot_rows(blocks):
    n = blocks.shape[-1]
    return blocks.reshape(KV_PAIRS, Q_PER_KV, 2, HEAD_DIM, n).swapaxes(1, 2).reshape(blocks.shape)


def _slot_bias(bias):
    qi = np.arange(BLOCK)[:, None]
    kj = np.arange(2 * BLOCK)[None, :]
    dist = qi + BLOCK - kj
    window = (dist >= 0) & (dist < BLOCK)
    b = bias.reshape(KV_PAIRS, 2, Q_PER_KV, BLOCK, 2 * BLOCK).swapaxes(1, 2).reshape(KV_PAIRS, PAIR_ROWS, 2 * BLOCK)
    tile = lambda mk: jnp.asarray(np.tile(mk, (2 * Q_PER_KV, 1)))[None]
    return jnp.stack([jnp.where(tile(window & (kj >= BLOCK)), b, MASKED), jnp.where(tile(window), b, MASKED)])


def _unslot_bias(db):
    return db.reshape(KV_PAIRS, Q_PER_KV, 2, BLOCK, 2 * BLOCK).swapaxes(1, 2).reshape(N_HEADS, -1)


def _pair_operands(ref, kvp, low, scale=None):
    parts = []
    for g in range(Q_PER_KV):
        xg = ref[:, (kvp * Q_PER_KV + g) * LANES:(kvp * Q_PER_KV + g + 1) * LANES]
        if scale is not None:
            xg = xg * scale
        zero = jnp.zeros_like(xg)
        parts += [jnp.where(low, xg, zero), jnp.where(low, zero, xg)]
    return jnp.concatenate(parts, axis=0)


def _pair_select(t, g, low):
    return jnp.where(low, t[2 * g * BLOCK:(2 * g + 1) * BLOCK], t[(2 * g + 1) * BLOCK:(2 * g + 2) * BLOCK])


def _attn_probs(q_ref, kvc_ref, kvp_ref, bias_ref, sink_ref, kvp, dkv, low):
    lanes = slice(kvp * LANES, (kvp + 1) * LANES)
    vlanes = slice(dkv + kvp * LANES, dkv + (kvp + 1) * LANES)
    k2 = jnp.concatenate([kvp_ref[:, lanes], kvc_ref[:, lanes]], axis=0)
    v2 = jnp.concatenate([kvp_ref[:, vlanes], kvc_ref[:, vlanes]], axis=0)
    qm = _pair_operands(q_ref, kvp, low, scale=HEAD_DIM ** -0.5)
    s = _dot(qm, k2, NT) + bias_ref[kvp]
    heads = [(2 * kvp + par) * Q_PER_KV + g for g in range(Q_PER_KV) for par in range(2)]
    sink = jnp.concatenate([jnp.full((BLOCK, 1), sink_ref[h], F32) for h in heads], axis=0)
    m = jnp.maximum(jnp.max(s, axis=-1, keepdims=True), sink)
    e = jnp.exp(s - m)
    es = jnp.exp(sink - m)
    inv = 1.0 / (jnp.sum(e, axis=-1, keepdims=True) + es)
    return e * inv, es * inv, qm, k2, v2


def _attn_specs(c):
    dq, dkv2 = c.DQ, 2 * c.DKV
    return [pl.BlockSpec((BLOCK, dq), lambda n: (n, 0)),
            pl.BlockSpec((BLOCK, dkv2), lambda n: (n, 0)),
            pl.BlockSpec((BLOCK, dkv2), lambda n: (jnp.maximum(n - 1, 0), 0))]


def _bias_spec():
    return pl.BlockSpec((None, KV_PAIRS, PAIR_ROWS, 2 * BLOCK), lambda n: (jnp.minimum(n, 1), 0, 0, 0))


def _low_lanes():
    return lax.broadcasted_iota(jnp.int32, (BLOCK, LANES), 1) < HEAD_DIM


def _attn_fwd(c, name, q, kv, bias, sinks, dep=None):
    S, dq = c.S, c.DQ

    def body(q_ref, kvc_ref, kvp_ref, bias_ref, sink_ref, o_ref):
        low = _low_lanes()
        for kvp in range(KV_PAIRS):
            p, _, _, _, v2 = _attn_probs(q_ref, kvc_ref, kvp_ref, bias_ref, sink_ref, kvp, c.DKV, low)
            o = _dot(p.astype(BF16), v2, NN)
            for g in range(Q_PER_KV):
                grp = kvp * Q_PER_KV + g
                o_ref[:, grp * LANES:(grp + 1) * LANES] = _pair_select(o, g, low).astype(BF16)

    return _pcall(
        body, name=name, ins=[q, kv, kv, bias, sinks], dep=dep, grid=(c.nb,),
        in_specs=_attn_specs(c) + [_bias_spec(), SMEM],
        out_specs=pl.BlockSpec((BLOCK, dq), lambda n: (n, 0)),
        out_shape=jax.ShapeDtypeStruct((S, dq), BF16), compiler_params=_params(1))


def _final_loss(c, h, g, target, dep=None):
    S, D, tm = c.S, c.D, c.tm

    def body(h_ref, g_ref, t_ref, dh_ref, dg_ref, loss_ref):
        i = pl.program_id(0)
        h = h_ref[...]
        gg = g_ref[...]
        r = _rstd(h)
        xh = h * r
        err = xh * gg - t_ref[...]
        lp = jnp.sum(jnp.sum(err * err, axis=1, keepdims=True), axis=0, keepdims=True) * (0.5 / D)
        dx, dg = _rms_bwd(err * (1.0 / D), h, gg, 0.0)
        dh_ref[...] = dx

        @pl.when(i == 0)
        def _():
            dg_ref[...] = dg
            loss_ref[...] = jnp.broadcast_to(lp, loss_ref.shape)

        @pl.when(i > 0)
        def _():
            dg_ref[...] += dg
            loss_ref[...] += jnp.broadcast_to(lp, loss_ref.shape)

    row = pl.BlockSpec((tm, D), lambda i: (i, 0))
    return _pcall(
        body, name="final_loss", ins=[h, g, target], dep=dep, grid=(S // tm,),
        in_specs=[row, pl.BlockSpec((1, D), lambda i: (0, 0)), row],
        out_specs=[row, pl.BlockSpec((1, D), lambda i: (0, 0)), pl.BlockSpec((1, LANES), lambda i: (0, 0))],
        out_shape=[jax.ShapeDtypeStruct((S, D), F32), jax.ShapeDtypeStruct((1, D), F32),
                   jax.ShapeDtypeStruct((1, LANES), F32)],
        compiler_params=_params(1))


def _rms_bwd_epilogue(h_idx, g_idx, res_idx):
    def epilogue(dhn, in_refs, out_refs):
        dh, dg = _rms_bwd(dhn, in_refs[h_idx][...], in_refs[g_idx][...], in_refs[res_idx][...])
        out_refs[0][...] = dh
        i = pl.program_id(0)

        @pl.when(i == 0)
        def _():
            out_refs[1][...] = dg

        @pl.when(i > 0)
        def _():
            out_refs[1][...] += dg
    return epilogue


def _stream_outs(c):
    S, D, tm = c.S, c.D, c.tm
    return ([jax.ShapeDtypeStruct((S, D), F32), jax.ShapeDtypeStruct((1, D), F32)],
            [pl.BlockSpec((tm, D), lambda i, j, k: (i, 0)), pl.BlockSpec((1, D), lambda i, j, k: (0, 0))])


def _row_specs(c):
    tm, D = c.tm, c.D
    return [pl.BlockSpec((tm, D), lambda i, j, k: (i, 0)), pl.BlockSpec((1, D), lambda i, j, k: (0, 0)),
            pl.BlockSpec((tm, D), lambda i, j, k: (i, 0))]


def _bwd_rows_to_stream(c, name, dy_list, slab, blk_rows, blk_idx, n_in_cols, h, g, dres, dep=None):
    S, D, tm = c.S, c.D, c.tm
    nd = len(dy_list)

    def load_a(in_refs, out_refs):
        a = in_refs[0][...]
        for r in in_refs[1:nd]:
            a = a + r[...]
        return a.astype(BF16)

    shapes, specs = _stream_outs(c)
    return _mm(
        name, [*dy_list, slab, h, g, dres],
        [pl.BlockSpec((tm, n_in_cols), lambda i, j, k: (i, 0))] * nd
        + [pl.BlockSpec((NDEV, blk_rows, n_in_cols), lambda i, j, k: (0, blk_idx, 0))] + _row_specs(c),
        shapes, specs, grid=(S // tm, 1, 1), dims=NT, nk=1, acc_shape=None,
        load_a=load_a, load_b=_b_view(nd, NDEV * blk_rows), epilogue=_rms_bwd_epilogue(nd + 1, nd + 2, nd + 3),
        dep=dep)


def _bwd_cols_to_stream(c, name, dy, wt, h, g, dres, dep=None):
    S, D, cw, tm = c.S, c.D, c.cw, c.tm
    K = NDEV * cw
    shapes, specs = _stream_outs(c)
    return _mm(
        name, [dy, wt, h, g, dres],
        [pl.BlockSpec((tm, K), lambda i, j, k: (i, 0)), _const((NDEV, cw, D), lambda i, j, k: (0, 0, 0))] + _row_specs(c),
        shapes, specs, grid=(S // tm, 1, 1), dims=NN, nk=1, acc_shape=None,
        load_a=_bf(0), load_b=_b_view(1, K), epilogue=_rms_bwd_epilogue(2, 3, 4), dep=dep)


def _bwd_rows_data(c, name, dy, slab, blk_rows, blk_idx, dep=None):
    S, D, tm = c.S, c.D, c.tm
    K = NDEV * blk_rows

    def epilogue(acc, in_refs, out_refs):
        out_refs[0][...] = acc.astype(BF16)

    return _mm(
        name, [dy, slab],
        [pl.BlockSpec((tm, D), lambda i, j, k: (i, 0)),
         pl.BlockSpec((NDEV, blk_rows, D), lambda i, j, k: (0, blk_idx, 0))],
        [jax.ShapeDtypeStruct((S, K), BF16)], [pl.BlockSpec((tm, K), lambda i, j, k: (i, 0))],
        grid=(S // tm, 1, 1), dims=NT, nk=1, acc_shape=None,
        load_a=_bf(0), load_b=_b_view(1, K), epilogue=epilogue, dep=dep)[0]


def _wgrad_rows(c, name, a, b_list, n_a, n_b, dep=None):
    S, tm = c.S, c.tm
    nb_in = len(b_list)
    blk_rows = n_a // NDEV

    def load_b(in_refs):
        b = in_refs[1][...]
        for r in in_refs[2:1 + nb_in]:
            b = b + r[...]
        return b.astype(BF16)

    def epilogue(acc, in_refs, out_refs):
        out_refs[0][...] = acc.reshape(NDEV, blk_rows, n_b).astype(BF16)

    return _mm(
        name, [a, *b_list],
        [pl.BlockSpec((tm, n_a), lambda i, j, k: (k, 0))] + [pl.BlockSpec((tm, n_b), lambda i, j, k: (k, 0))] * nb_in,
        [jax.ShapeDtypeStruct((NDEV, blk_rows, n_b), BF16)],
        [pl.BlockSpec((NDEV, blk_rows, n_b), lambda i, j, k: (0, 0, 0))],
        grid=(1, 1, S // tm), dims=TN, nk=S // tm, acc_shape=(n_a, n_b),
        load_a=_bf(0), load_b=load_b, epilogue=epilogue, dep=dep)[0]


def _wgrad_cols(c, name, a, b, dep=None):
    S, D, cw = c.S, c.D, c.cw

    def body(a_ref, b_ref, out_ref):
        out_ref[...] = _dot(a_ref[...], b_ref[...], TN).astype(BF16)

    return _pcall(
        body, name=name, ins=[a, b], dep=dep, grid=(NDEV,),
        in_specs=[_const((S, D), lambda j: (0, 0)), pl.BlockSpec((S, cw), lambda j: (0, j))],
        out_specs=pl.BlockSpec((None, D, cw), lambda j: (j, 0, 0)),
        out_shape=jax.ShapeDtypeStruct((NDEV, D, cw), BF16), compiler_params=_params(1))


def _ffn_bwd_data(c, name, dh, p, w1t, rows, h, g, dep=None):
    S, D, fw, tm = c.S, c.D, c.fw, c.tm
    F = NDEV * fw

    def body(dh_ref, p_ref, w1t_ref, w2_ref, h_ref, g_ref, da_ref, out_ref, dg_ref, dhb_ref):
        i = pl.program_id(0)
        dh = dh_ref[...]
        dhb = dh.astype(BF16)
        dhb_ref[...] = dhb
        for j in range(NDEV):
            cols = slice(j * fw, (j + 1) * fw)
            da_ref[:, cols] = (_dot(dhb, w2_ref[j], NT) * (2.0 * p_ref[:, cols].astype(F32))).astype(BF16)
        dx, dg = _rms_bwd(_dot(da_ref[...], w1t_ref[...].reshape(F, D), NN), h_ref[...], g_ref[...], dh)
        out_ref[...] = dx

        @pl.when(i == 0)
        def _():
            dg_ref[...] = dg

        @pl.when(i > 0)
        def _():
            dg_ref[...] += dg

    row = pl.BlockSpec((tm, D), lambda i: (i, 0))
    wide = pl.BlockSpec((tm, F), lambda i: (i, 0))
    return _pcall(
        body, name=name, ins=[dh, p, w1t, rows, h, g], dep=dep, grid=(S // tm,),
        in_specs=[row, wide, _const((NDEV, fw, D), lambda i: (0, 0, 0)), _const((NDEV, c.fr, D), lambda i: (0, 0, 0)),
                  row, pl.BlockSpec((1, D), lambda i: (0, 0))],
        out_specs=[wide, row, pl.BlockSpec((1, D), lambda i: (0, 0)), row],
        out_shape=[jax.ShapeDtypeStruct((S, F), BF16), jax.ShapeDtypeStruct((S, D), F32),
                   jax.ShapeDtypeStruct((1, D), F32), jax.ShapeDtypeStruct((S, D), BF16)],
        compiler_params=_params(1))


def _ffn_bwd_w(c, name, hn, da, p, dhb, dep=None):
    S, D, fw = c.S, c.D, c.fw

    def body(hn_ref, da_ref, p_ref, dhb_ref, dw1_ref, dw2_ref):
        dw1_ref[...] = _dot(hn_ref[...], da_ref[...], TN).astype(BF16)
        pf = p_ref[...].astype(F32)
        dw2_ref[...] = _dot((pf * pf).astype(BF16), dhb_ref[...], TN).astype(BF16)

    panel = pl.BlockSpec((S, fw), lambda j: (0, j))
    return _pcall(
        body, name=name, ins=[hn, da, p, dhb], dep=dep, grid=(NDEV,),
        in_specs=[_const((S, D), lambda j: (0, 0)), panel, panel, _const((S, D), lambda j: (0, 0))],
        out_specs=[pl.BlockSpec((None, D, fw), lambda j: (j, 0, 0)), pl.BlockSpec((None, c.fr, D), lambda j: (j, 0, 0))],
        out_shape=[jax.ShapeDtypeStruct((NDEV, D, fw), BF16), jax.ShapeDtypeStruct((NDEV, c.fr, D), BF16)],
        compiler_params=_params(1))


def _attn_bwd(c, name, q, kv, do, bias, sinks, dep=None):
    S, dq, dkv = c.S, c.DQ, c.DKV
    nb = c.nb
    scale = HEAD_DIM ** -0.5

    def body(q_ref, kvc_ref, kvp_ref, do_ref, bias_ref, sink_ref, dq_ref, dkv_ref, dbias_ref, dsink_ref, dsink_acc):
        n = pl.program_id(0)

        @pl.when(n == 0)
        def _():
            dkv_ref[...] = jnp.zeros_like(dkv_ref)
            dbias_ref[...] = jnp.zeros_like(dbias_ref)
            dsink_acc[...] = jnp.zeros_like(dsink_acc)

        low = _low_lanes()
        rows_c = pl.ds(pl.multiple_of(n * BLOCK, BLOCK), BLOCK)
        rows_p = pl.ds(pl.multiple_of(jnp.maximum(n - 1, 0) * BLOCK, BLOCK), BLOCK)
        for kvp in range(KV_PAIRS):
            p, ps, qm, k2, v2 = _attn_probs(q_ref, kvc_ref, kvp_ref, bias_ref, sink_ref, kvp, dkv, low)
            dom = _pair_operands(do_ref, kvp, low)
            dp = _dot(dom, v2, NT)
            delta = jnp.sum(p * dp, axis=-1, keepdims=True)
            ds = p * (dp - delta)
            dbias_ref[kvp] += ds
            dsink_acc[:, kvp:kvp + 1] += -(ps * delta)
            ds16 = ds.astype(BF16)
            dqm = _dot(ds16, k2, NN) * scale
            for g in range(Q_PER_KV):
                grp = kvp * Q_PER_KV + g
                dq_ref[:, grp * LANES:(grp + 1) * LANES] = _pair_select(dqm, g, low).astype(BF16)
            dk2 = _dot(ds16, qm, TN)
            dv2 = _dot(p.astype(BF16), dom, TN)
            lanes = slice(kvp * LANES, (kvp + 1) * LANES)
            vlanes = slice(dkv + kvp * LANES, dkv + (kvp + 1) * LANES)
            dkv_ref[rows_p, lanes] += dk2[:BLOCK]
            dkv_ref[rows_c, lanes] += dk2[BLOCK:]
            dkv_ref[rows_p, vlanes] += dv2[:BLOCK]
            dkv_ref[rows_c, vlanes] += dv2[BLOCK:]

        @pl.when(n == nb - 1)
        def _():
            dsink_ref[...] = jnp.sum(dsink_acc[...].reshape(2 * Q_PER_KV, BLOCK, KV_PAIRS), axis=1)

    return _pcall(
        body, name=name, ins=[q, kv, kv, do, bias, sinks], dep=dep, grid=(nb,),
        in_specs=_attn_specs(c) + [pl.BlockSpec((BLOCK, dq), lambda n: (n, 0)), _bias_spec(), SMEM],
        out_specs=[pl.BlockSpec((BLOCK, dq), lambda n: (n, 0)), pl.BlockSpec((S, 2 * dkv), lambda n: (0, 0)),
                   pl.BlockSpec((KV_PAIRS, PAIR_ROWS, 2 * BLOCK), lambda n: (0, 0, 0)),
                   pl.BlockSpec((2 * Q_PER_KV, KV_PAIRS), lambda n: (0, 0))],
        out_shape=[jax.ShapeDtypeStruct((S, dq), BF16), jax.ShapeDtypeStruct((S, 2 * dkv), F32),
                   jax.ShapeDtypeStruct((KV_PAIRS, PAIR_ROWS, 2 * BLOCK), F32),
                   jax.ShapeDtypeStruct((2 * Q_PER_KV, KV_PAIRS), F32)],
        scratch_shapes=[pltpu.VMEM((PAIR_ROWS, KV_PAIRS), F32)], compiler_params=_params(1))


def _sgu_bwd(c, name, a, z, dgated, ln_g, wc, wc_t, b_t, dep=None):
    S, AW, gd, tm = c.S, c.AW, c.gd, c.tms

    def body(a_ref, z_ref, dg_ref, lng_ref, wc_ref, wct_ref, bt_ref, dz_ref, dws_ref, dbt_ref, dlng_ref, dvn_ref):
        i = pl.program_id(0)

        @pl.when(i == 0)
        def _():
            dws_ref[...] = jnp.zeros_like(dws_ref)
            dbt_ref[...] = jnp.zeros_like(dbt_ref)
            dlng_ref[...] = jnp.zeros_like(dlng_ref)

        lng = lng_ref[...]
        va = a_ref[:, AW:].astype(F32)
        xc = va - jnp.mean(va, axis=-1, keepdims=True)
        rstd = lax.rsqrt(jnp.mean(xc * xc, axis=-1, keepdims=True) + EPS)
        xh = xc * rstd
        vn = (xh * lng).astype(BF16)
        causal = _sgu_masks()
        for ch in range(tm // CHUNK):
            rows = slice(ch * CHUNK, (ch + 1) * CHUNK)
            for g in range(A_GROUPS):
                cols = slice(g * gd, (g + 1) * gd)
                blk = vn[rows, cols]
                mixed = _dot(wc_ref[g], blk, NN) + bt_ref[:, g:g + 1]
                dgb = dg_ref[rows, cols].astype(F32)
                dm = dgb * a_ref[rows, cols].astype(F32)
                dbt_ref[:, g:g + 1] += jnp.sum(dm, axis=1, keepdims=True)
                dm16 = dm.astype(BF16)
                dws_ref[g] += jnp.where(causal, _dot(dm16, blk, NT), 0.0)
                dvn_ref[rows, cols] = _dot(wct_ref[g], dm16, NN)
                dz_ref[rows, cols] = (dgb * mixed * _gelu_grad(z_ref[rows, cols].astype(F32))).astype(BF16)
        dvn = dvn_ref[...]
        dlng_ref[...] += jnp.sum(dvn * xh, axis=0, keepdims=True)
        dxh = dvn * lng
        dva = rstd * (dxh - jnp.mean(dxh, axis=-1, keepdims=True) - xh * jnp.mean(dxh * xh, axis=-1, keepdims=True))
        dz_ref[:, AW:] = (dva * _gelu_grad(z_ref[:, AW:].astype(F32))).astype(BF16)

    wide = pl.BlockSpec((tm, 2 * AW), lambda i: (i, 0))
    wsp = pl.BlockSpec((A_GROUPS, CHUNK, CHUNK), lambda i: (0, 0, 0))
    btsp = pl.BlockSpec((CHUNK, A_GROUPS), lambda i: (0, 0))
    return _pcall(
        body, name=name, ins=[a, z, dgated, ln_g, wc, wc_t, b_t], dep=dep, grid=(S // tm,),
        in_specs=[wide, wide, pl.BlockSpec((tm, AW), lambda i: (i, 0)), pl.BlockSpec((1, AW), lambda i: (0, 0)),
                  wsp, wsp, btsp],
        out_specs=[wide, wsp, btsp, pl.BlockSpec((1, AW), lambda i: (0, 0))],
        out_shape=[jax.ShapeDtypeStruct((S, 2 * AW), BF16), jax.ShapeDtypeStruct((A_GROUPS, CHUNK, CHUNK), F32),
                   jax.ShapeDtypeStruct((CHUNK, A_GROUPS), F32), jax.ShapeDtypeStruct((1, AW), F32)],
        scratch_shapes=[pltpu.VMEM((tm, AW), F32)], compiler_params=_params(1))


def _adamw(name, parts, part_block, part_index, w, m, v, tr, row_off=0, n_rows=None, prev=None, dep=None):
    R, C = w.shape
    n_rows = R if n_rows is None else n_rows
    assert n_rows % tr == 0 and row_off % tr == 0
    bc1 = 1.0 - ADAM_B1 ** ADAM_STEP
    bc2 = 1.0 - ADAM_B2 ** ADAM_STEP

    def body(p_ref, w_ref, m_ref, v_ref, *rest):
        g_ref, d_ref, nm_ref, nv_ref = rest[-4:]
        g = p_ref[0].astype(F32)
        for s in range(1, NDEV):
            g = g + p_ref[s].astype(F32)
        nm = ADAM_B1 * m_ref[...] + (1.0 - ADAM_B1) * g
        nv = ADAM_B2 * v_ref[...] + (1.0 - ADAM_B2) * (g * g)
        g_ref[...] = g
        nm_ref[...] = nm
        nv_ref[...] = nv
        d_ref[...] = -ADAM_LR * ((nm / bc1) / (jnp.sqrt(nv / bc2) + ADAM_EPS) + ADAM_WD * w_ref[...])

    ob = row_off // tr
    row = pl.BlockSpec((tr, C), lambda i: (ob + i, 0))
    out = jax.ShapeDtypeStruct((R, C), F32)
    chained = prev is not None
    return _pcall(
        body, name=name, ins=[parts, w, m, v] + (list(prev) if chained else []), dep=dep, grid=(n_rows // tr,),
        in_specs=[pl.BlockSpec(part_block, part_index), row, row, row] + ([ANY] * 4 if chained else []),
        out_specs=[row, row, row, row], out_shape=[out, out, out, out],
        input_output_aliases={4 + t: t for t in range(4)} if chained else {}, compiler_params=_params(1))


def _place():
    return lax.axis_index("x"), lax.axis_index("y"), lax.axis_index("c")


def _slot(px, py, pc):
    return 4 * px + 2 * py + pc


def _peer(k, x, y, c):
    return x ^ ((k >> 2) & 1), y ^ ((k >> 1) & 1), c ^ (k & 1)


def _all_gather(name, arrs, dep=None):
    n = len(arrs)

    def body(*refs):
        in_refs, out_refs = refs[:n], refs[n:2 * n]
        send_sems, recv_sems, local_sems = refs[2 * n:]
        x, y, c = _place()
        me, sibling = (x, y, c), (x, y, 1 - c)
        chips = [(1 - x, y), (x, 1 - y), (1 - x, 1 - y)]

        def copy(a, k, block, to, src=None):
            dst = out_refs[a].at[_slot(*block)]
            return pltpu.make_async_remote_copy(
                src_ref=dst if src is None else src, dst_ref=dst, send_sem=send_sems.at[a, k],
                recv_sem=recv_sems.at[a, k], device_id=to, device_id_type=MESH)

        mine = [pltpu.make_async_copy(in_refs[a], out_refs[a].at[_slot(*me)], local_sems.at[a]) for a in range(n)]
        for cp in mine:
            cp.start()
        first = []
        for a in range(n):
            first.append(copy(a, 0, me, sibling, src=in_refs[a]))
            first += [copy(a, 1 + j, me, (*chip, c), src=in_refs[a]) for j, chip in enumerate(chips)]
        for cp in first:
            cp.start()
        passed = []
        for j, chip in enumerate(chips):
            for a in range(n):
                copy(a, 1 + j, (*chip, c), me).wait_recv()
                fwd = copy(a, 4 + j, (*chip, c), sibling)
                fwd.start()
                passed.append(fwd)
        for a in range(n):
            copy(a, 0, sibling, me).wait_recv()
            for j, chip in enumerate(chips):
                copy(a, 4 + j, (*chip, 1 - c), me).wait_recv()
        for cp in first + passed:
            cp.wait_send()
        for cp in mine:
            cp.wait()

    return _pcall(
        body, name=name, ins=list(arrs), in_specs=[HBM] * n, dep=dep,
        out_shape=[jax.ShapeDtypeStruct((NDEV, *a.shape), a.dtype) for a in arrs], out_specs=[HBM] * n,
        scratch_shapes=[pltpu.SemaphoreType.DMA((n, 7)), pltpu.SemaphoreType.DMA((n, 7)),
                        pltpu.SemaphoreType.DMA((n,))],
        compiler_params=pltpu.CompilerParams(has_side_effects=True))


def _send_copy(gather, src_ref, land_ref, send_sems, recv_sems, a, k, place):
    x, y, c = place
    peer = _peer(k, x, y, c)
    return pltpu.make_async_remote_copy(
        src_ref=src_ref if gather else src_ref.at[_slot(*peer)], dst_ref=land_ref.at[_slot(x, y, c)],
        send_sem=send_sems.at[a * (NDEV - 1) + k - 1], recv_sem=recv_sems.at[a * (NDEV - 1) + k - 1],
        device_id=peer, device_id_type=MESH)


def _send_start(name, srcs, lands, gather, dep=None):
    n = len(srcs)

    def body(*refs):
        src_refs, land_refs = refs[:n], refs[n:2 * n]
        send_sems, recv_sems = refs[2 * n], refs[2 * n + 1]
        token = refs[-1]
        place = _place()
        for k in range(1, NDEV):
            for a in range(n):
                _send_copy(gather, src_refs[a], land_refs[a], send_sems, recv_sems, a, k, place).start()
        token[...] = jnp.zeros_like(token)

    thru = [pltpu.HBM(t.shape, t.dtype) for t in [*srcs, *lands]]
    out = _pcall(
        body, name=name, ins=[pltpu.with_memory_space_constraint(t, pltpu.HBM) for t in [*srcs, *lands]],
        in_specs=[HBM] * (2 * n), dep=dep,
        out_shape=(pltpu.SemaphoreType.DMA((n * (NDEV - 1),)), pltpu.SemaphoreType.DMA((n * (NDEV - 1),)), *thru,
                   jax.ShapeDtypeStruct((8, LANES), F32)),
        out_specs=(SEM, SEM, *[HBM] * (2 * n), pl.BlockSpec(memory_space=pltpu.VMEM)),
        input_output_aliases={i: 2 + i for i in range(2 * n)},
        compiler_params=pltpu.CompilerParams(has_side_effects=pltpu.SideEffectType.DATAFLOW_SIDE_EFFECTING))
    return out[-1], out[0], out[1], list(out[2:2 + n]), list(out[2 + n:2 + 2 * n])


def _send_wait(name, started, gather, dep=None):
    _, send_sems, recv_sems, srcs, lands = started
    n = len(srcs)

    def body(*refs):
        src_refs, land_refs = refs[:n], refs[n:2 * n]
        ssem, rsem = refs[2 * n], refs[2 * n + 1]
        place = _place()
        for k in range(1, NDEV):
            for a in range(n):
                cp = _send_copy(gather, src_refs[a], land_refs[a], ssem, rsem, a, k, place)
                cp.wait_send()
                cp.wait_recv()

    thru = [pltpu.HBM(t.shape, t.dtype) for t in [*srcs, *lands]]
    out = _pcall(
        body, name=name, ins=[*srcs, *lands, send_sems, recv_sems], in_specs=[HBM] * (2 * n) + [SEM, SEM], dep=dep,
        out_shape=tuple(thru), out_specs=tuple([HBM] * (2 * n)), input_output_aliases={i: i for i in range(2 * n)},
        compiler_params=pltpu.CompilerParams(has_side_effects=pltpu.SideEffectType.DATAFLOW_SIDE_EFFECTING))
    return list(out[n:])


def _landing(block, me):
    zone = lax.empty((NDEV, *block.shape), block.dtype)
    return lax.dynamic_update_slice(zone, block[None], (me,) + (0,) * block.ndim)


def _rows128(t):
    flat = t.reshape(-1)
    n = flat.shape[0]
    rows = -(-n // (8 * LANES)) * 8
    return jnp.pad(flat, (0, rows * LANES - n)).reshape(rows, LANES)


def kernel(x, mix_norm_g, ffn_norm_g, a_w_in, a_ln_g, a_w_spatial, a_b_spatial, a_w_out, kv_norm_g, w_k, w_v, b_w_q, b_sinks, b_w_o, rel_bias, ffn_w1, ffn_w2, final_norm_g, loss_target, m_mix_norm_g, m_ffn_norm_g, m_a_w_in, m_a_ln_g, m_a_w_spatial, m_a_b_spatial, m_a_w_out, m_kv_norm_g, m_w_k, m_w_v, m_b_w_q, m_b_sinks, m_b_w_o, m_rel_bias, m_ffn_w1, m_ffn_w2, m_final_norm_g, v_mix_norm_g, v_ffn_norm_g, v_a_w_in, v_a_ln_g, v_a_w_spatial, v_a_b_spatial, v_a_w_out, v_kv_norm_g, v_w_k, v_w_v, v_b_w_q, v_b_sinks, v_b_w_o, v_rel_bias, v_ffn_w1, v_ffn_w2, v_final_norm_g):
    c = _config(x, a_w_in, a_w_out, w_k, b_w_q, b_w_o, ffn_w1, ffn_w2)
    S, D, LA, LB, LF = c.S, c.D, c.LA, c.LB, c.LF
    weights = dict(mix_norm_g=mix_norm_g, ffn_norm_g=ffn_norm_g, a_w_in=a_w_in, a_ln_g=a_ln_g, a_w_spatial=a_w_spatial,
                   a_b_spatial=a_b_spatial, a_w_out=a_w_out, kv_norm_g=kv_norm_g, w_k=w_k, w_v=w_v, b_w_q=b_w_q,
                   b_sinks=b_sinks, b_w_o=b_w_o, rel_bias=rel_bias, ffn_w1=ffn_w1, ffn_w2=ffn_w2,
                   final_norm_g=final_norm_g)
    m_in = dict(mix_norm_g=m_mix_norm_g, ffn_norm_g=m_ffn_norm_g, a_w_in=m_a_w_in, a_ln_g=m_a_ln_g,
                a_w_spatial=m_a_w_spatial, a_b_spatial=m_a_b_spatial, a_w_out=m_a_w_out, kv_norm_g=m_kv_norm_g,
                w_k=m_w_k, w_v=m_w_v, b_w_q=m_b_w_q, b_sinks=m_b_sinks, b_w_o=m_b_w_o, rel_bias=m_rel_bias,
                ffn_w1=m_ffn_w1, ffn_w2=m_ffn_w2, final_norm_g=m_final_norm_g)
    v_in = dict(mix_norm_g=v_mix_norm_g, ffn_norm_g=v_ffn_norm_g, a_w_in=v_a_w_in, a_ln_g=v_a_ln_g,
                a_w_spatial=v_a_w_spatial, a_b_spatial=v_a_b_spatial, a_w_out=v_a_w_out, kv_norm_g=v_kv_norm_g,
                w_k=v_w_k, w_v=v_w_v, b_w_q=v_b_w_q, b_sinks=v_b_sinks, b_w_o=v_b_w_o, rel_bias=v_rel_bias,
                ffn_w1=v_ffn_w1, ffn_w2=v_ffn_w2, final_norm_g=v_final_norm_g)
    names = list(weights)
    seq = _Seq()
    me = _slot(*_place())
    bf = lambda t: t.astype(BF16)

    def rows_of(l):
        second = a_w_out[l] if l < LA else b_w_o[l - LA]
        return bf(jnp.concatenate([ffn_w2[l], second], axis=0))

    groups = [[bf(a_w_in[0])[None], a_ln_g],
              [bf(ffn_w1[0])[None], rows_of(0)]]
    for l in range(1, LA):
        groups.append([bf(jnp.stack([a_w_in[l], ffn_w1[l]])), rows_of(l)])
    for l in range(LB):
        extra = [bf(jnp.concatenate([w_k, w_v], axis=1))] if l == 0 else []
        groups.append(extra + [bf(b_w_q[l]), bf(ffn_w1[LA + l])[None], rows_of(LA + l)])
    started = [seq(_send_start, f"weights_start{i}", grp, [_landing(t, me) for t in grp], True)
               for i, grp in enumerate(groups)]

    def arrive(i):
        return seq(_send_wait, f"weights_wait{i}", started[i], True)

    causal = jnp.tril(jnp.ones((CHUNK, CHUNK), bool))
    wsp = jnp.where(causal[None, None], a_w_spatial, 0.0)
    wsp16 = wsp.astype(BF16)
    wsp16_t = jnp.swapaxes(wsp, -1, -2).astype(BF16)
    bsp_t = jnp.swapaxes(a_b_spatial, -1, -2)
    mix_g = mix_norm_g.reshape(-1, 1, D)
    ffn_g = ffn_norm_g.reshape(-1, 1, D)
    kv_g = kv_norm_g.reshape(1, D)
    fin_g = final_norm_g.reshape(1, D)
    onehot = _bucket_onehot()
    bias = _slot_bias(seq(_band_bias, rel_bias.T, onehot).reshape(N_HEADS, BLOCK, 2 * BLOCK))

    h = x.reshape(S, D)
    sav_a, sav_b, wts_a, wts_b = [], [], [], []
    for l in range(LA):
        if l == 0:
            w_in, lng_all = arrive(0)
            ln_g_full = jnp.transpose(lng_all, (1, 0, 2)).reshape(LA, 1, c.AW)
            in_i, w1_i = 0, 0
        else:
            w_in, rows = arrive(l + 1)
            w1, in_i, w1_i = w_in, 0, 1
        z, a, hn = seq(_a_in_fwd, c, f"a_in_fwd{l}", h, mix_g[l], w_in, in_i)
        gated = seq(_sgu_fwd, c, f"sgu_fwd{l}", a, ln_g_full[l], wsp16[l], bsp_t[l])
        if l == 0:
            w1, rows = arrive(1)
        h1 = seq(_mm_res, c, f"a_out_fwd{l}", gated, rows, c.ar, c.fr // c.ar, h)
        p, h2, hnf = seq(_ffn_fwd, c, f"ffn_fwd{l}", h1, ffn_g[l], w1, w1_i, rows)
        sav_a.append((h, z, a, hn, gated, h1, p, hnf))
        wts_a.append((w_in, in_i, w1, w1_i, rows))
        h = h2
    h_kv = h
    for l in range(LB):
        got = arrive(LA + 1 + l)
        if l == 0:
            wkv, got = got[0], got[1:]
            kv, hkv = seq(_rms_mm_rows, c, "kv_fwd", h, kv_g, wkv, c.kr, 0, 2 * c.DKV)
        wq, w1, rows = got
        wq = _slot_cols(wq)
        wo = _slot_rows(rows[:, c.fr:, :])
        q, hn = seq(_rms_mm_rows, c, f"q_fwd{l}", h, mix_g[LA + l], wq, c.qr, 0, c.DQ)
        o = seq(_attn_fwd, c, f"attn_fwd{l}", q, kv, bias, b_sinks[l])
        h1 = seq(_mm_res, c, f"o_fwd{l}", o, wo, c.orr, 0, h)
        p, h2, hnf = seq(_ffn_fwd, c, f"ffn_fwd{LA + l}", h1, ffn_g[LA + l], w1, 0, rows)
        sav_b.append((h, q, hn, o, h1, p, hnf))
        wts_b.append((wq, wo, w1, rows))
        h = h2
    dh, d_fin_g, loss_row = seq(_final_loss, c, h, fin_g, loss_target.reshape(S, D))
    loss = lax.psum(loss_row[0, 0], MESH_AXES)

    results = {}
    in_flight = []

    def update(k, parts, layer, col_blk=0):
        w = weights[k]
        rows_l, ncols = (w.shape[-2], w.shape[-1]) if w.ndim == 3 else w.shape
        flat = lambda t: t.reshape(-1, ncols)
        tr = min(256, rows_l)
        results[k] = seq(_adamw, f"adamw_{k}{layer}", parts, (NDEV, tr, ncols), lambda i: (0, i, col_blk),
                         flat(w), flat(m_in[k]), flat(v_in[k]), tr, row_off=layer * rows_l, n_rows=rows_l,
                         prev=results.get(k))

    def land(tag, entry):
        lands = seq(_send_wait, f"grads_wait_{tag}", entry[1], False)
        for keys, parts in zip(entry[0], lands):
            for k, layer, col_blk in keys:
                update(k, parts, layer, col_blk)

    def send(tag, items):
        slabs = [t for _, t in items]
        own = [_landing(lax.dynamic_index_in_dim(t, me, 0, keepdims=False), me) for t in slabs]
        st = seq(_send_start, f"grads_start_{tag}", slabs, own, False)
        in_flight.append((tag, ([k for k, _ in items], st)))
        while len(in_flight) > EXCHANGE_LAG:
            land(*in_flight.pop(0))

    d_mix_g, d_ffn_g = [None] * LF, [None] * LF
    dkv_list, dbias_list, dsink_list = [], [], [None] * LB

    def ffn_bwd(lf, dh, h1, p, hnf, w1, w1_i, rows):
        w1t = jnp.swapaxes(w1[:, w1_i], -1, -2)
        da, dh1, d_ffn_g[lf], dhb = seq(_ffn_bwd_data, c, f"ffn_bwd_data{lf}", dh, p, w1t, rows, h1, ffn_g[lf])
        dw1, dw2 = seq(_ffn_bwd_w, c, f"ffn_bwd_w{lf}", hnf, da, p, dhb)
        send(f"ffn{lf}", [([("ffn_w1", lf, 0)], dw1), ([("ffn_w2", lf, 0)], dw2)])
        return dh1

    for l in reversed(range(LB)):
        h0, q, hn, o, h1, p, hnf = sav_b[l]
        wq, wo, w1, rows = wts_b[l]
        dh1 = ffn_bwd(LA + l, dh, h1, p, hnf, w1, 0, rows)
        do = seq(_bwd_rows_data, c, f"o_bwd_data{l}", dh1, wo, c.orr, 0)
        dwo = _unslot_rows(seq(_wgrad_rows, c, f"o_bwd_w{l}", o, [dh1], c.DQ, D))
        dq, dkv, dbias, dsink = seq(_attn_bwd, c, f"attn_bwd{l}", q, kv, do, bias, b_sinks[l])
        dsink_list[l] = dsink.reshape(Q_PER_KV, 2, KV_PAIRS).transpose(2, 1, 0).reshape(1, N_HEADS)
        dkv_list.append(dkv)
        dbias_list.append(_unslot_bias(dbias))
        dwq = _unslot_cols(seq(_wgrad_rows, c, f"q_bwd_w{l}", hn, [dq], D, c.DQ))
        send(f"attn{l}", [([("b_w_o", l, 0)], dwo), ([("b_w_q", l, 0)], dwq)])
        dh, d_mix_g[LA + l] = seq(_bwd_rows_to_stream, c, f"q_bwd_data{l}", [dq], wq, c.qr, 0, c.DQ, h0,
                                  mix_g[LA + l], dh1)
    dwkv = seq(_wgrad_rows, c, "kv_bwd_w", hkv, dkv_list, D, 2 * c.DKV)
    send("kv", [([("w_k", 0, 0), ("w_v", 0, 1)], dwkv)])
    dh, d_kv_g = seq(_bwd_rows_to_stream, c, "kv_bwd_data", dkv_list, wkv, c.kr, 0, 2 * c.DKV, h_kv, kv_g, dh)
    d_rel_t = seq(_band_bias_grad, dbias_list, onehot)
    d_wsp, d_bsp, d_lng = [None] * LA, [None] * LA, [None] * LA
    for l in reversed(range(LA)):
        h0, z, a, hn, gated, h1, p, hnf = sav_a[l]
        w_in, in_i, w1, w1_i, rows = wts_a[l]
        dh1 = ffn_bwd(l, dh, h1, p, hnf, w1, w1_i, rows)
        dgated = seq(_bwd_rows_data, c, f"a_out_bwd_data{l}", dh1, rows, c.ar, c.fr // c.ar)
        dwout = seq(_wgrad_rows, c, f"a_out_bwd_w{l}", gated, [dh1], c.AW, D)
        send(f"a_out{l}", [([("a_w_out", l, 0)], dwout)])
        dz, d_wsp[l], dbt, d_lng[l] = seq(_sgu_bwd, c, f"sgu_bwd{l}", a, z, dgated, ln_g_full[l], wsp16[l],
                                          wsp16_t[l], bsp_t[l])
        d_bsp[l] = dbt.T
        dwin = seq(_wgrad_cols, c, f"a_in_bwd_w{l}", hn, dz)
        send(f"a_in{l}", [([("a_w_in", l, 0)], dwin)])
        dh, d_mix_g[l] = seq(_bwd_cols_to_stream, c, f"a_in_bwd_data{l}", dz, jnp.swapaxes(w_in[:, in_i], -1, -2), h0,
                             mix_g[l], dh1)
    grad_x = dh.reshape(1, S, D)

    small = {
        "mix_norm_g": jnp.concatenate(d_mix_g, axis=0), "ffn_norm_g": jnp.concatenate(d_ffn_g, axis=0),
        "a_w_spatial": jnp.stack(d_wsp), "a_b_spatial": jnp.stack(d_bsp), "kv_norm_g": d_kv_g,
        "b_sinks": jnp.concatenate(dsink_list, axis=0), "rel_bias": d_rel_t.T, "final_norm_g": d_fin_g,
    }
    small_names = list(small)
    packs = [_rows128(small[k]) for k in small_names] + [_rows128(jnp.concatenate(d_lng, axis=0))]
    offs = np.cumsum([0] + [p.shape[0] for p in packs])
    (small_all,) = seq(_all_gather, "gather_small_grads", [jnp.concatenate(packs, axis=0)])
    Rs = int(offs[-1])
    while in_flight:
        land(*in_flight.pop(0))

    grads, deltas, new_m, new_v = {}, {}, {}, {}

    def put(k, outs, shape):
        grads[k], deltas[k], new_m[k], new_v[k] = (t.reshape(shape) for t in outs)

    def pack_state(d):
        return jnp.concatenate([_rows128(d[k]) for k in small_names]
                               + [jnp.zeros((packs[-1].shape[0], LANES), F32)], axis=0)

    outs = seq(_adamw, "adamw_small", small_all, (NDEV, Rs, LANES), lambda i: (0, 0, 0),
               pack_state(weights), pack_state(m_in), pack_state(v_in), Rs)
    for n_, k in enumerate(small_names):
        shape = weights[k].shape
        size = int(np.prod(shape))
        put(k, [t[int(offs[n_]):int(offs[n_ + 1])].reshape(-1)[:size] for t in outs], shape)
    lng_sum = outs[0][int(offs[-2]):int(offs[-1])].reshape(-1)[:LA * c.AW].reshape(LA, c.AW)
    lng_mine = lax.dynamic_slice_in_dim(lng_sum, me * c.ar, c.ar, axis=1)
    lng_parts = jnp.concatenate([lng_mine[None], jnp.zeros((NDEV - 1, LA, c.ar), F32)], axis=0)
    put("a_ln_g", seq(_adamw, "adamw_ln_g", lng_parts, (NDEV, LA, c.ar), lambda i: (0, 0, 0),
                      a_ln_g, m_in["a_ln_g"], v_in["a_ln_g"], LA), a_ln_g.shape)
    for k in ("a_w_in", "ffn_w1", "ffn_w2", "a_w_out", "b_w_o", "b_w_q", "w_k", "w_v"):
        put(k, results[k], weights[k].shape)

    return (loss, grad_x, *[grads[k] for k in names], *[deltas[k] for k in names],
            *[new_m[k] for k in names], *[new_v[k] for k in names])
```

```python
import numpy as np
import math
import jax
import jax.numpy as jnp
from jax import lax
from jax.experimental import pallas as pl
from jax.experimental.pallas import tpu as pltpu

F32 = jnp.float32
BF16 = jnp.bfloat16

NDEV = 8
EPS = 1e-6
CHUNK = 128
A_GROUPS = 8
N_HEADS = 16
N_KV_HEADS = 4
Q_PER_KV = N_HEADS // N_KV_HEADS
HEAD_DIM = 64
BLOCK = 128
N_BUCKETS = 32
MAX_DISTANCE = 128
ADAM_LR, ADAM_B1, ADAM_B2, ADAM_EPS, ADAM_WD, ADAM_STEP = 0.001, 0.9, 0.999, 1e-08, 0.01, 10
LANES = 128
VMEM_LIMIT = 56 * 1024 * 1024
INV_SQRT2 = 0.7071067811865476
INV_SQRT_2PI = 0.3989422804014327
MESH_AXES = ("x", "y", "c")
EXCHANGE_LAG = 2

HBM = pl.BlockSpec(memory_space=pltpu.HBM)
SMEM = pl.BlockSpec(memory_space=pltpu.SMEM)
ANY = pl.BlockSpec(memory_space=pl.ANY)
SEM = pl.BlockSpec(memory_space=pltpu.SEMAPHORE)
MESH = pl.DeviceIdType.MESH


def _params(n_grid):
    return pltpu.CompilerParams(dimension_semantics=("arbitrary",) * n_grid, vmem_limit_bytes=VMEM_LIMIT)


def _const(block, index_map):
    return pl.BlockSpec(block, index_map, pipeline_mode=pl.Buffered(1))


def _pcall(body, *, ins, in_specs, dep=None, **kw):
    n_in = len(ins)
    if dep is None:
        return pl.pallas_call(body, in_specs=list(in_specs), **kw)(*ins)

    def with_dep(*refs):
        body(*refs[:n_in], *refs[n_in + 1:])

    return pl.pallas_call(with_dep, in_specs=[*in_specs, ANY], **kw)(*ins, dep)


class _Seq:
    def __init__(self):
        self.last = None

    def __call__(self, fn, *args, **kw):
        out = fn(*args, dep=self.last, **kw)
        self.last = out[0] if isinstance(out, (tuple, list)) else out
        return out


def _rstd(h):
    return lax.rsqrt(jnp.mean(h * h, axis=-1, keepdims=True) + EPS)


def _rms_bwd(dhn, h, g, dres):
    r = _rstd(h)
    xh = h * r
    dg = jnp.sum(dhn * xh, axis=0, keepdims=True)
    dxh = dhn * g
    dx = r * (dxh - xh * jnp.mean(dxh * xh, axis=-1, keepdims=True))
    return dres + dx, dg


def _gelu(z):
    return 0.5 * z * (1.0 + lax.erf(z * INV_SQRT2))


def _gelu_grad(z):
    return 0.5 * (1.0 + lax.erf(z * INV_SQRT2)) + z * (jnp.exp(-0.5 * z * z) * INV_SQRT_2PI)


def _dot(a, b, dims):
    return lax.dot_general(a, b, (dims, ((), ())), preferred_element_type=F32)


NN = ((1,), (0,))
NT = ((1,), (1,))
TN = ((0,), (0,))


def _mm(name, ins, in_specs, out_shapes, out_specs, *, grid, dims, nk, acc_shape, load_a, load_b, epilogue,
        dep=None):
    n_in, n_out = len(ins), len(out_shapes)
    kax = len(grid) - 1

    def body(*refs):
        in_refs = refs[:n_in]
        out_refs = refs[n_in:n_in + n_out]
        a = load_a(in_refs, out_refs)
        b = load_b(in_refs)
        prod = _dot(a, b, dims)
        if nk == 1:
            epilogue(prod, in_refs, out_refs)
        else:
            acc = refs[n_in + n_out]
            k = pl.program_id(kax)

            @pl.when(k == 0)
            def _():
                acc[...] = prod

            @pl.when(k > 0)
            def _():
                acc[...] += prod

            @pl.when(k == nk - 1)
            def _():
                epilogue(acc[...], in_refs, out_refs)

    return _pcall(
        body, name=name, ins=ins, in_specs=in_specs, dep=dep, grid=grid, out_specs=out_specs, out_shape=out_shapes,
        scratch_shapes=[pltpu.VMEM(acc_shape, F32)] if nk > 1 else [], compiler_params=_params(len(grid)))


def _bf(ref_idx):
    return lambda in_refs, *_: in_refs[ref_idx][...].astype(BF16)


def _b_view(ref_idx, rows):
    def load(in_refs):
        b = in_refs[ref_idx][...]
        return b.reshape(rows, b.shape[-1])
    return load


class Cfg:
    pass


def _config(x, a_w_in, a_w_out, w_k, b_w_q, b_w_o, ffn_w1, ffn_w2):
    c = Cfg()
    c.S, c.D = x.shape[1], x.shape[2]
    c.LA, _, c.cw = a_w_in.shape
    c.AW2 = NDEV * c.cw
    c.AW = c.AW2 // 2
    c.gd = c.AW // A_GROUPS
    c.ar = a_w_out.shape[1]
    c.LF, _, c.fw = ffn_w1.shape
    c.fr = ffn_w2.shape[1]
    c.LB, c.qr, c.DQ = b_w_q.shape
    c.orr = b_w_o.shape[1]
    c.kr, c.DKV = w_k.shape
    c.tm = min(512, c.S)
    c.tms = min(256, c.S)
    c.nb = c.S // BLOCK
    assert c.cw == c.fw == c.fr and c.AW == NDEV * c.ar and c.D == NDEV * c.qr == NDEV * c.kr
    assert c.DQ == NDEV * c.orr == N_HEADS * HEAD_DIM and c.DKV == N_KV_HEADS * HEAD_DIM
    assert c.S % c.tm == 0 and c.S % c.tms == 0 and c.tms % CHUNK == 0 and c.gd % LANES == 0
    assert c.fr % c.ar == 0 and c.fr % c.orr == 0
    assert c.LA >= 1 and c.LB >= 1 and c.LF == c.LA + c.LB
    return c


def _cached_rms(h_idx, g_idx, hn_out_idx, jax_axis=1):
    def load(in_refs, out_refs):
        hn_ref = out_refs[hn_out_idx]

        @pl.when(pl.program_id(jax_axis) == 0)
        def _():
            h = in_refs[h_idx][...]
            hn_ref[...] = (h * _rstd(h) * in_refs[g_idx][...]).astype(BF16)

        return hn_ref[...]
    return load


def _a_in_fwd(c, name, h, g, col, ci, dep=None):
    S, D, cw, tm = c.S, c.D, c.cw, c.tm

    def body(h_ref, g_ref, w_ref, z_ref, a_ref, hn_ref):
        h = h_ref[...]
        hn = (h * _rstd(h) * g_ref[...]).astype(BF16)
        hn_ref[...] = hn
        for j in range(NDEV):
            cols = slice(j * cw, (j + 1) * cw)
            z = _dot(hn, w_ref[j], NT)
            z_ref[:, cols] = z.astype(BF16)
            a_ref[:, cols] = _gelu(z).astype(BF16)

    row = pl.BlockSpec((tm, D), lambda i: (i, 0))
    wide = pl.BlockSpec((tm, c.AW2), lambda i: (i, 0))
    return _pcall(
        body, name=name, ins=[h, g, col], dep=dep, grid=(S // tm,),
        in_specs=[row, pl.BlockSpec((1, D), lambda i: (0, 0)), _const((NDEV, None, cw, D), lambda i: (0, ci, 0, 0))],
        out_specs=[wide, wide, row],
        out_shape=[jax.ShapeDtypeStruct((S, c.AW2), BF16), jax.ShapeDtypeStruct((S, c.AW2), BF16),
                   jax.ShapeDtypeStruct((S, D), BF16)],
        compiler_params=_params(1))


def _rms_mm_rows(c, name, h, g, slab, blk_rows, blk_idx, n_out, dep=None):
    S, D, tm = c.S, c.D, c.tm

    def epilogue(acc, in_refs, out_refs):
        out_refs[0][...] = acc.astype(BF16)

    return _mm(
        name, [h, slab, g],
        [pl.BlockSpec((tm, D), lambda i, j, k: (i, 0)),
         pl.BlockSpec((NDEV, blk_rows, n_out), lambda i, j, k: (0, blk_idx, 0)),
         pl.BlockSpec((1, D), lambda i, j, k: (0, 0))],
        [jax.ShapeDtypeStruct((S, n_out), BF16), jax.ShapeDtypeStruct((S, D), BF16)],
        [pl.BlockSpec((tm, n_out), lambda i, j, k: (i, 0)), pl.BlockSpec((tm, D), lambda i, j, k: (i, 0))],
        grid=(S // tm, 1, 1), dims=NN, nk=1, acc_shape=None,
        load_a=_cached_rms(0, 2, 1), load_b=_b_view(1, NDEV * blk_rows), epilogue=epilogue, dep=dep)


def _mm_res(c, name, a, slab, blk_rows, blk_idx, res, dep=None):
    S, D, tm = c.S, c.D, c.tm
    K = NDEV * blk_rows

    def epilogue(acc, in_refs, out_refs):
        out_refs[0][...] = in_refs[2][...] + acc

    return _mm(
        name, [a, slab, res],
        [pl.BlockSpec((tm, K), lambda i, j, k: (i, 0)),
         pl.BlockSpec((NDEV, blk_rows, D), lambda i, j, k: (0, blk_idx, 0)),
         pl.BlockSpec((tm, D), lambda i, j, k: (i, 0))],
        [jax.ShapeDtypeStruct((S, D), F32)], [pl.BlockSpec((tm, D), lambda i, j, k: (i, 0))],
        grid=(S // tm, 1, 1), dims=NN, nk=1, acc_shape=None,
        load_a=_bf(0), load_b=_b_view(1, K), epilogue=epilogue, dep=dep)[0]


def _sgu_masks():
    ii = lax.broadcasted_iota(jnp.int32, (CHUNK, CHUNK), 0)
    jj = lax.broadcasted_iota(jnp.int32, (CHUNK, CHUNK), 1)
    return ii >= jj


def _sgu_fwd(c, name, a, ln_g, wc, b_t, dep=None):
    S, AW, gd, tm = c.S, c.AW, c.gd, c.tms

    def body(a_ref, lng_ref, wc_ref, bt_ref, out_ref):
        va = a_ref[:, AW:].astype(F32)
        xc = va - jnp.mean(va, axis=-1, keepdims=True)
        vn = (xc * lax.rsqrt(jnp.mean(xc * xc, axis=-1, keepdims=True) + EPS) * lng_ref[...]).astype(BF16)
        for ch in range(tm // CHUNK):
            rows = slice(ch * CHUNK, (ch + 1) * CHUNK)
            for g in range(A_GROUPS):
                cols = slice(g * gd, (g + 1) * gd)
                mixed = _dot(wc_ref[g], vn[rows, cols], NN) + bt_ref[:, g:g + 1]
                out_ref[rows, cols] = (a_ref[rows, cols].astype(F32) * mixed).astype(BF16)

    return _pcall(
        body, name=name, ins=[a, ln_g, wc, b_t], dep=dep, grid=(S // tm,),
        in_specs=[pl.BlockSpec((tm, 2 * AW), lambda i: (i, 0)), pl.BlockSpec((1, AW), lambda i: (0, 0)),
                  pl.BlockSpec((A_GROUPS, CHUNK, CHUNK), lambda i: (0, 0, 0)),
                  pl.BlockSpec((CHUNK, A_GROUPS), lambda i: (0, 0))],
        out_specs=pl.BlockSpec((tm, AW), lambda i: (i, 0)),
        out_shape=jax.ShapeDtypeStruct((S, AW), BF16), compiler_params=_params(1))


def _ffn_fwd(c, name, h, g, col, ci, rows, dep=None):
    S, D, fw, tm = c.S, c.D, c.fw, c.tm
    F = NDEV * fw

    def body(h_ref, g_ref, w1_ref, w2_ref, p_ref, out_ref, hn_ref, r_ref):
        h = h_ref[...]
        hn = (h * _rstd(h) * g_ref[...]).astype(BF16)
        hn_ref[...] = hn
        for j in range(NDEV):
            cols = slice(j * fw, (j + 1) * fw)
            p = jnp.maximum(_dot(hn, w1_ref[j], NT), 0.0)
            p_ref[:, cols] = p.astype(BF16)
            r_ref[:, cols] = (p * p).astype(BF16)
        out_ref[...] = h + _dot(r_ref[...], w2_ref[...].reshape(F, D), NN)

    row = pl.BlockSpec((tm, D), lambda i: (i, 0))
    return _pcall(
        body, name=name, ins=[h, g, col, rows], dep=dep, grid=(S // tm,),
        in_specs=[row, pl.BlockSpec((1, D), lambda i: (0, 0)),
                  _const((NDEV, None, fw, D), lambda i: (0, ci, 0, 0)), _const((NDEV, c.fr, D), lambda i: (0, 0, 0))],
        out_specs=[pl.BlockSpec((tm, F), lambda i: (i, 0)), row, row],
        out_shape=[jax.ShapeDtypeStruct((S, F), BF16), jax.ShapeDtypeStruct((S, D), F32),
                   jax.ShapeDtypeStruct((S, D), BF16)],
        scratch_shapes=[pltpu.VMEM((tm, F), BF16)], compiler_params=_params(1))


def _bucket_table():
    qi = np.arange(BLOCK)[:, None]
    kj = np.arange(2 * BLOCK)[None, :]
    d = np.maximum(qi + BLOCK - kj, 0)
    max_exact = N_BUCKETS // 2
    ratio = np.log(np.maximum(d, 1).astype(np.float32) / np.float32(max_exact)) / np.float32(
        math.log(MAX_DISTANCE / max_exact))
    large = np.minimum(max_exact + (ratio.astype(np.float32) * np.float32(N_BUCKETS - max_exact)).astype(np.int32),
                       N_BUCKETS - 1)
    return np.where(d < max_exact, d, large).astype(np.int32)


def _bucket_onehot():
    b = jnp.asarray(_bucket_table().reshape(1, -1))
    return (b == lax.broadcasted_iota(jnp.int32, (N_BUCKETS, b.shape[1]), 0)).astype(F32)


def _whole(t):
    return pl.BlockSpec(t.shape, lambda: (0,) * t.ndim)


def _band_bias(rel_bias_t, onehot, dep=None):
    def body(r_ref, oh_ref, out_ref):
        out_ref[...] = lax.dot_general(r_ref[...], oh_ref[...], (NN, ((), ())), preferred_element_type=F32,
                                       precision=lax.Precision.HIGHEST)

    n = onehot.shape[1]
    return _pcall(body, name="band_bias", ins=[rel_bias_t, onehot], in_specs=[_whole(rel_bias_t), _whole(onehot)],
                  dep=dep, out_shape=jax.ShapeDtypeStruct((N_HEADS, n), F32), compiler_params=_params(0))


def _band_bias_grad(dbias_list, onehot, dep=None):
    n_in = len(dbias_list)

    def body(*refs):
        oh_ref, out_ref = refs[n_in], refs[n_in + 1]
        d = refs[0][...]
        for r in refs[1:n_in]:
            d = d + r[...]
        out_ref[...] = lax.dot_general(d, oh_ref[...], (NT, ((), ())), preferred_element_type=F32,
                                       precision=lax.Precision.HIGHEST)

    ins = [*dbias_list, onehot]
    return _pcall(body, name="band_bias_grad", ins=ins, in_specs=[_whole(t) for t in ins], dep=dep,
                  out_shape=jax.ShapeDtypeStruct((N_HEADS, N_BUCKETS), F32), compiler_params=_params(0))


KV_PAIRS = N_KV_HEADS // 2
PAIR_ROWS = 2 * Q_PER_KV * BLOCK
MASKED = float(np.finfo(np.float32).min) / 2


def _slot_cols(w):
    lead = w.shape[:-1]
    return w.reshape(*lead, KV_PAIRS, 2, Q_PER_KV, HEAD_DIM).swapaxes(-3, -2).reshape(*lead, N_HEADS * HEAD_DIM)


def _unslot_cols(w):
    lead = w.shape[:-1]
    return w.reshape(*lead, KV_PAIRS, Q_PER_KV, 2, HEAD_DIM).swapaxes(-3, -2).reshape(*lead, N_HEADS * HEAD_DIM)


def _slot_rows(blocks):
    n = blocks.shape[-1]
    return blocks.reshape(KV_PAIRS, 2, Q_PER_KV, HEAD_DIM, n).swapaxes(1, 2).reshape(blocks.shape)


def _unslot_rows(blocks):
    n = blocks.shape[-1]
    return blocks.reshape(KV_PAIRS, Q_PER_KV, 2, HEAD_DIM, n).swapaxes(1, 2).reshape(blocks.shape)


def _slot_bias(bias):
    qi = np.arange(BLOCK)[:, None]
    kj = np.arange(2 * BLOCK)[None, :]
    dist = qi + BLOCK - kj
    window = (dist >= 0) & (dist < BLOCK)
    b = bias.reshape(KV_PAIRS, 2, Q_PER_KV, BLOCK, 2 * BLOCK).swapaxes(1, 2).reshape(KV_PAIRS, PAIR_ROWS, 2 * BLOCK)
    tile = lambda mk: jnp.asarray(np.tile(mk, (2 * Q_PER_KV, 1)))[None]
    return jnp.stack([jnp.where(tile(window & (kj >= BLOCK)), b, MASKED), jnp.where(tile(window), b, MASKED)])


def _unslot_bias(db):
    return db.reshape(KV_PAIRS, Q_PER_KV, 2, BLOCK, 2 * BLOCK).swapaxes(1, 2).reshape(N_HEADS, -1)


def _pair_operands(ref, kvp, low, scale=None):
    parts = []
    for g in range(Q_PER_KV):
        xg = ref[:, (kvp * Q_PER_KV + g) * LANES:(kvp * Q_PER_KV + g + 1) * LANES]
        if scale is not None:
            xg = xg * scale
        zero = jnp.zeros_like(xg)
        parts += [jnp.where(low, xg, zero), jnp.where(low, zero, xg)]
    return jnp.concatenate(parts, axis=0)


def _pair_select(t, g, low):
    return jnp.where(low, t[2 * g * BLOCK:(2 * g + 1) * BLOCK], t[(2 * g + 1) * BLOCK:(2 * g + 2) * BLOCK])


def _attn_probs(q_ref, kvc_ref, kvp_ref, bias_ref, sink_ref, kvp, dkv, low):
    lanes = slice(kvp * LANES, (kvp + 1) * LANES)
    vlanes = slice(dkv + kvp * LANES, dkv + (kvp + 1) * LANES)
    k2 = jnp.concatenate([kvp_ref[:, lanes], kvc_ref[:, lanes]], axis=0)
    v2 = jnp.concatenate([kvp_ref[:, vlanes], kvc_ref[:, vlanes]], axis=0)
    qm = _pair_operands(q_ref, kvp, low, scale=HEAD_DIM ** -0.5)
    s = _dot(qm, k2, NT) + bias_ref[kvp]
    heads = [(2 * kvp + par) * Q_PER_KV + g for g in range(Q_PER_KV) for par in range(2)]
    sink = jnp.concatenate([jnp.full((BLOCK, 1), sink_ref[h], F32) for h in heads], axis=0)
    m = jnp.maximum(jnp.max(s, axis=-1, keepdims=True), sink)
    e = jnp.exp(s - m)
    es = jnp.exp(sink - m)
    inv = 1.0 / (jnp.sum(e, axis=-1, keepdims=True) + es)
    return e * inv, es * inv, qm, k2, v2


def _attn_specs(c):
    dq, dkv2 = c.DQ, 2 * c.DKV
    return [pl.BlockSpec((BLOCK, dq), lambda n: (n, 0)),
            pl.BlockSpec((BLOCK, dkv2), lambda n: (n, 0)),
            pl.BlockSpec((BLOCK, dkv2), lambda n: (jnp.maximum(n - 1, 0), 0))]


def _bias_spec():
    return pl.BlockSpec((None, KV_PAIRS, PAIR_ROWS, 2 * BLOCK), lambda n: (jnp.minimum(n, 1), 0, 0, 0))


def _low_lanes():
    return lax.broadcasted_iota(jnp.int32, (BLOCK, LANES), 1) < HEAD_DIM


def _attn_fwd(c, name, q, kv, bias, sinks, dep=None):
    S, dq = c.S, c.DQ

    def body(q_ref, kvc_ref, kvp_ref, bias_ref, sink_ref, o_ref):
        low = _low_lanes()
        for kvp in range(KV_PAIRS):
            p, _, _, _, v2 = _attn_probs(q_ref, kvc_ref, kvp_ref, bias_ref, sink_ref, kvp, c.DKV, low)
            o = _dot(p.astype(BF16), v2, NN)
            for g in range(Q_PER_KV):
                grp = kvp * Q_PER_KV + g
                o_ref[:, grp * LANES:(grp + 1) * LANES] = _pair_select(o, g, low).astype(BF16)

    return _pcall(
        body, name=name, ins=[q, kv, kv, bias, sinks], dep=dep, grid=(c.nb,),
        in_specs=_attn_specs(c) + [_bias_spec(), SMEM],
        out_specs=pl.BlockSpec((BLOCK, dq), lambda n: (n, 0)),
        out_shape=jax.ShapeDtypeStruct((S, dq), BF16), compiler_params=_params(1))


def _final_loss(c, h, g, target, dep=None):
    S, D, tm = c.S, c.D, c.tm

    def body(h_ref, g_ref, t_ref, dh_ref, dg_ref, loss_ref):
        i = pl.program_id(0)
        h = h_ref[...]
        gg = g_ref[...]
        r = _rstd(h)
        xh = h * r
        err = xh * gg - t_ref[...]
        lp = jnp.sum(jnp.sum(err * err, axis=1, keepdims=True), axis=0, keepdims=True) * (0.5 / D)
        dx, dg = _rms_bwd(err * (1.0 / D), h, gg, 0.0)
        dh_ref[...] = dx

        @pl.when(i == 0)
        def _():
            dg_ref[...] = dg
            loss_ref[...] = jnp.broadcast_to(lp, loss_ref.shape)

        @pl.when(i > 0)
        def _():
            dg_ref[...] += dg
            loss_ref[...] += jnp.broadcast_to(lp, loss_ref.shape)

    row = pl.BlockSpec((tm, D), lambda i: (i, 0))
    return _pcall(
        body, name="final_loss", ins=[h, g, target], dep=dep, grid=(S // tm,),
        in_specs=[row, pl.BlockSpec((1, D), lambda i: (0, 0)), row],
        out_specs=[row, pl.BlockSpec((1, D), lambda i: (0, 0)), pl.BlockSpec((1, LANES), lambda i: (0, 0))],
        out_shape=[jax.ShapeDtypeStruct((S, D), F32), jax.ShapeDtypeStruct((1, D), F32),
                   jax.ShapeDtypeStruct((1, LANES), F32)],
        compiler_params=_params(1))


def _rms_bwd_epilogue(h_idx, g_idx, res_idx):
    def epilogue(dhn, in_refs, out_refs):
        dh, dg = _rms_bwd(dhn, in_refs[h_idx][...], in_refs[g_idx][...], in_refs[res_idx][...])
        out_refs[0][...] = dh
        i = pl.program_id(0)

        @pl.when(i == 0)
        def _():
            out_refs[1][...] = dg

        @pl.when(i > 0)
        def _():
            out_refs[1][...] += dg
    return epilogue


def _stream_outs(c):
    S, D, tm = c.S, c.D, c.tm
    return ([jax.ShapeDtypeStruct((S, D), F32), jax.ShapeDtypeStruct((1, D), F32)],
            [pl.BlockSpec((tm, D), lambda i, j, k: (i, 0)), pl.BlockSpec((1, D), lambda i, j, k: (0, 0))])


def _row_specs(c):
    tm, D = c.tm, c.D
    return [pl.BlockSpec((tm, D), lambda i, j, k: (i, 0)), pl.BlockSpec((1, D), lambda i, j, k: (0, 0)),
            pl.BlockSpec((tm, D), lambda i, j, k: (i, 0))]


def _bwd_rows_to_stream(c, name, dy_list, slab, blk_rows, blk_idx, n_in_cols, h, g, dres, dep=None):
    S, D, tm = c.S, c.D, c.tm
    nd = len(dy_list)

    def load_a(in_refs, out_refs):
        a = in_refs[0][...]
        for r in in_refs[1:nd]:
            a = a + r[...]
        return a.astype(BF16)

    shapes, specs = _stream_outs(c)
    return _mm(
        name, [*dy_list, slab, h, g, dres],
        [pl.BlockSpec((tm, n_in_cols), lambda i, j, k: (i, 0))] * nd
        + [pl.BlockSpec((NDEV, blk_rows, n_in_cols), lambda i, j, k: (0, blk_idx, 0))] + _row_specs(c),
        shapes, specs, grid=(S // tm, 1, 1), dims=NT, nk=1, acc_shape=None,
        load_a=load_a, load_b=_b_view(nd, NDEV * blk_rows), epilogue=_rms_bwd_epilogue(nd + 1, nd + 2, nd + 3),
        dep=dep)


def _bwd_cols_to_stream(c, name, dy, col, ci, h, g, dres, dep=None):
    S, D, cw, tm = c.S, c.D, c.cw, c.tm
    K = NDEV * cw
    shapes, specs = _stream_outs(c)
    return _mm(
        name, [dy, col, h, g, dres],
        [pl.BlockSpec((tm, K), lambda i, j, k: (i, 0)),
         _const((NDEV, None, cw, D), lambda i, j, k: (0, ci, 0, 0))] + _row_specs(c),
        shapes, specs, grid=(S // tm, 1, 1), dims=NN, nk=1, acc_shape=None,
        load_a=_bf(0), load_b=_b_view(1, K), epilogue=_rms_bwd_epilogue(2, 3, 4), dep=dep)


def _bwd_rows_data(c, name, dy, slab, blk_rows, blk_idx, dep=None):
    S, D, tm = c.S, c.D, c.tm
    K = NDEV * blk_rows

    def epilogue(acc, in_refs, out_refs):
        out_refs[0][...] = acc.astype(BF16)

    return _mm(
        name, [dy, slab],
        [pl.BlockSpec((tm, D), lambda i, j, k: (i, 0)),
         pl.BlockSpec((NDEV, blk_rows, D), lambda i, j, k: (0, blk_idx, 0))],
        [jax.ShapeDtypeStruct((S, K), BF16)], [pl.BlockSpec((tm, K), lambda i, j, k: (i, 0))],
        grid=(S // tm, 1, 1), dims=NT, nk=1, acc_shape=None,
        load_a=_bf(0), load_b=_b_view(1, K), epilogue=epilogue, dep=dep)[0]


def _wgrad_rows(c, name, a, b_list, n_a, n_b, dep=None):
    S, tm = c.S, c.tm
    nb_in = len(b_list)
    blk_rows = n_a // NDEV

    def load_b(in_refs):
        b = in_refs[1][...]
        for r in in_refs[2:1 + nb_in]:
            b = b + r[...]
        return b.astype(BF16)

    def epilogue(acc, in_refs, out_refs):
        out_refs[0][...] = acc.reshape(NDEV, blk_rows, n_b).astype(BF16)

    return _mm(
        name, [a, *b_list],
        [pl.BlockSpec((tm, n_a), lambda i, j, k: (k, 0))] + [pl.BlockSpec((tm, n_b), lambda i, j, k: (k, 0))] * nb_in,
        [jax.ShapeDtypeStruct((NDEV, blk_rows, n_b), BF16)],
        [pl.BlockSpec((NDEV, blk_rows, n_b), lambda i, j, k: (0, 0, 0))],
        grid=(1, 1, S // tm), dims=TN, nk=S // tm, acc_shape=(n_a, n_b),
        load_a=_bf(0), load_b=load_b, epilogue=epilogue, dep=dep)[0]


def _wgrad_cols(c, name, a, b, dep=None):
    S, D, cw = c.S, c.D, c.cw

    def body(a_ref, b_ref, out_ref):
        out_ref[...] = _dot(a_ref[...], b_ref[...], TN).astype(BF16)

    return _pcall(
        body, name=name, ins=[a, b], dep=dep, grid=(NDEV,),
        in_specs=[_const((S, D), lambda j: (0, 0)), pl.BlockSpec((S, cw), lambda j: (0, j))],
        out_specs=pl.BlockSpec((None, D, cw), lambda j: (j, 0, 0)),
        out_shape=jax.ShapeDtypeStruct((NDEV, D, cw), BF16), compiler_params=_params(1))


def _ffn_bwd_data(c, name, dh, p, col, ci, rows, h, g, dep=None):
    S, D, fw, tm = c.S, c.D, c.fw, c.tm
    F = NDEV * fw

    def body(dh_ref, p_ref, w1t_ref, w2_ref, h_ref, g_ref, da_ref, out_ref, dg_ref, dhb_ref):
        i = pl.program_id(0)
        dh = dh_ref[...]
        dhb = dh.astype(BF16)
        dhb_ref[...] = dhb
        for j in range(NDEV):
            cols = slice(j * fw, (j + 1) * fw)
            da_ref[:, cols] = (_dot(dhb, w2_ref[j], NT) * (2.0 * p_ref[:, cols].astype(F32))).astype(BF16)
        dx, dg = _rms_bwd(_dot(da_ref[...], w1t_ref[...].reshape(F, D), NN), h_ref[...], g_ref[...], dh)
        out_ref[...] = dx

        @pl.when(i == 0)
        def _():
            dg_ref[...] = dg

        @pl.when(i > 0)
        def _():
            dg_ref[...] += dg

    row = pl.BlockSpec((tm, D), lambda i: (i, 0))
    wide = pl.BlockSpec((tm, F), lambda i: (i, 0))
    return _pcall(
        body, name=name, ins=[dh, p, col, rows, h, g], dep=dep, grid=(S // tm,),
        in_specs=[row, wide, _const((NDEV, None, fw, D), lambda i: (0, ci, 0, 0)),
                  _const((NDEV, c.fr, D), lambda i: (0, 0, 0)),
                  row, pl.BlockSpec((1, D), lambda i: (0, 0))],
        out_specs=[wide, row, pl.BlockSpec((1, D), lambda i: (0, 0)), row],
        out_shape=[jax.ShapeDtypeStruct((S, F), BF16), jax.ShapeDtypeStruct((S, D), F32),
                   jax.ShapeDtypeStruct((1, D), F32), jax.ShapeDtypeStruct((S, D), BF16)],
        compiler_params=_params(1))


def _ffn_bwd_w(c, name, hn, da, p, dhb, dep=None):
    S, D, fw = c.S, c.D, c.fw

    def body(hn_ref, da_ref, p_ref, dhb_ref, dw1_ref, dw2_ref):
        dw1_ref[...] = _dot(hn_ref[...], da_ref[...], TN).astype(BF16)
        pf = p_ref[...].astype(F32)
        dw2_ref[...] = _dot((pf * pf).astype(BF16), dhb_ref[...], TN).astype(BF16)

    panel = pl.BlockSpec((S, fw), lambda j: (0, j))
    return _pcall(
        body, name=name, ins=[hn, da, p, dhb], dep=dep, grid=(NDEV,),
        in_specs=[_const((S, D), lambda j: (0, 0)), panel, panel, _const((S, D), lambda j: (0, 0))],
        out_specs=[pl.BlockSpec((None, D, fw), lambda j: (j, 0, 0)), pl.BlockSpec((None, c.fr, D), lambda j: (j, 0, 0))],
        out_shape=[jax.ShapeDtypeStruct((NDEV, D, fw), BF16), jax.ShapeDtypeStruct((NDEV, c.fr, D), BF16)],
        compiler_params=_params(1))


def _attn_bwd(c, name, q, kv, do, bias, sinks, dep=None):
    S, dq, dkv = c.S, c.DQ, c.DKV
    nb = c.nb
    scale = HEAD_DIM ** -0.5

    def body(q_ref, kvc_ref, kvp_ref, do_ref, bias_ref, sink_ref, dq_ref, dkv_ref, dbias_ref, dsink_ref, dsink_acc):
        n = pl.program_id(0)

        @pl.when(n == 0)
        def _():
            dkv_ref[...] = jnp.zeros_like(dkv_ref)
            dbias_ref[...] = jnp.zeros_like(dbias_ref)
            dsink_acc[...] = jnp.zeros_like(dsink_acc)

        low = _low_lanes()
        rows_c = pl.ds(pl.multiple_of(n * BLOCK, BLOCK), BLOCK)
        rows_p = pl.ds(pl.multiple_of(jnp.maximum(n - 1, 0) * BLOCK, BLOCK), BLOCK)
        for kvp in range(KV_PAIRS):
            p, ps, qm, k2, v2 = _attn_probs(q_ref, kvc_ref, kvp_ref, bias_ref, sink_ref, kvp, dkv, low)
            dom = _pair_operands(do_ref, kvp, low)
            dp = _dot(dom, v2, NT)
            delta = jnp.sum(p * dp, axis=-1, keepdims=True)
            ds = p * (dp - delta)
            dbias_ref[kvp] += ds
            dsink_acc[:, kvp:kvp + 1] += -(ps * delta)
            ds16 = ds.astype(BF16)
            dqm = _dot(ds16, k2, NN) * scale
            for g in range(Q_PER_KV):
                grp = kvp * Q_PER_KV + g
                dq_ref[:, grp * LANES:(grp + 1) * LANES] = _pair_select(dqm, g, low).astype(BF16)
            dk2 = _dot(ds16, qm, TN)
            dv2 = _dot(p.astype(BF16), dom, TN)
            lanes = slice(kvp * LANES, (kvp + 1) * LANES)
            vlanes = slice(dkv + kvp * LANES, dkv + (kvp + 1) * LANES)
            dkv_ref[rows_p, lanes] += dk2[:BLOCK]
            dkv_ref[rows_c, lanes] += dk2[BLOCK:]
            dkv_ref[rows_p, vlanes] += dv2[:BLOCK]
            dkv_ref[rows_c, vlanes] += dv2[BLOCK:]

        @pl.when(n == nb - 1)
        def _():
            dsink_ref[...] = jnp.sum(dsink_acc[...].reshape(2 * Q_PER_KV, BLOCK, KV_PAIRS), axis=1)

    return _pcall(
        body, name=name, ins=[q, kv, kv, do, bias, sinks], dep=dep, grid=(nb,),
        in_specs=_attn_specs(c) + [pl.BlockSpec((BLOCK, dq), lambda n: (n, 0)), _bias_spec(), SMEM],
        out_specs=[pl.BlockSpec((BLOCK, dq), lambda n: (n, 0)), pl.BlockSpec((S, 2 * dkv), lambda n: (0, 0)),
                   pl.BlockSpec((KV_PAIRS, PAIR_ROWS, 2 * BLOCK), lambda n: (0, 0, 0)),
                   pl.BlockSpec((2 * Q_PER_KV, KV_PAIRS), lambda n: (0, 0))],
        out_shape=[jax.ShapeDtypeStruct((S, dq), BF16), jax.ShapeDtypeStruct((S, 2 * dkv), F32),
                   jax.ShapeDtypeStruct((KV_PAIRS, PAIR_ROWS, 2 * BLOCK), F32),
                   jax.ShapeDtypeStruct((2 * Q_PER_KV, KV_PAIRS), F32)],
        scratch_shapes=[pltpu.VMEM((PAIR_ROWS, KV_PAIRS), F32)], compiler_params=_params(1))


def _sgu_bwd(c, name, a, z, dgated, ln_g, wc, wc_t, b_t, dep=None):
    S, AW, gd, tm = c.S, c.AW, c.gd, c.tms

    def body(a_ref, z_ref, dg_ref, lng_ref, wc_ref, wct_ref, bt_ref, dz_ref, dws_ref, dbt_ref, dlng_ref, dvn_ref):
        i = pl.program_id(0)

        @pl.when(i == 0)
        def _():
            dws_ref[...] = jnp.zeros_like(dws_ref)
            dbt_ref[...] = jnp.zeros_like(dbt_ref)
            dlng_ref[...] = jnp.zeros_like(dlng_ref)

        lng = lng_ref[...]
        va = a_ref[:, AW:].astype(F32)
        xc = va - jnp.mean(va, axis=-1, keepdims=True)
        rstd = lax.rsqrt(jnp.mean(xc * xc, axis=-1, keepdims=True) + EPS)
        xh = xc * rstd
        vn = (xh * lng).astype(BF16)
        causal = _sgu_masks()
        for ch in range(tm // CHUNK):
            rows = slice(ch * CHUNK, (ch + 1) * CHUNK)
            for g in range(A_GROUPS):
                cols = slice(g * gd, (g + 1) * gd)
                blk = vn[rows, cols]
                mixed = _dot(wc_ref[g], blk, NN) + bt_ref[:, g:g + 1]
                dgb = dg_ref[rows, cols].astype(F32)
                dm = dgb * a_ref[rows, cols].astype(F32)
                dbt_ref[:, g:g + 1] += jnp.sum(dm, axis=1, keepdims=True)
                dm16 = dm.astype(BF16)
                dws_ref[g] += jnp.where(causal, _dot(dm16, blk, NT), 0.0)
                dvn_ref[rows, cols] = _dot(wct_ref[g], dm16, NN)
                dz_ref[rows, cols] = (dgb * mixed * _gelu_grad(z_ref[rows, cols].astype(F32))).astype(BF16)
        dvn = dvn_ref[...]
        dlng_ref[...] += jnp.sum(dvn * xh, axis=0, keepdims=True)
        dxh = dvn * lng
        dva = rstd * (dxh - jnp.mean(dxh, axis=-1, keepdims=True) - xh * jnp.mean(dxh * xh, axis=-1, keepdims=True))
        dz_ref[:, AW:] = (dva * _gelu_grad(z_ref[:, AW:].astype(F32))).astype(BF16)

    wide = pl.BlockSpec((tm, 2 * AW), lambda i: (i, 0))
    wsp = pl.BlockSpec((A_GROUPS, CHUNK, CHUNK), lambda i: (0, 0, 0))
    btsp = pl.BlockSpec((CHUNK, A_GROUPS), lambda i: (0, 0))
    return _pcall(
        body, name=name, ins=[a, z, dgated, ln_g, wc, wc_t, b_t], dep=dep, grid=(S // tm,),
        in_specs=[wide, wide, pl.BlockSpec((tm, AW), lambda i: (i, 0)), pl.BlockSpec((1, AW), lambda i: (0, 0)),
                  wsp, wsp, btsp],
        out_specs=[wide, wsp, btsp, pl.BlockSpec((1, AW), lambda i: (0, 0))],
        out_shape=[jax.ShapeDtypeStruct((S, 2 * AW), BF16), jax.ShapeDtypeStruct((A_GROUPS, CHUNK, CHUNK), F32),
                   jax.ShapeDtypeStruct((CHUNK, A_GROUPS), F32), jax.ShapeDtypeStruct((1, AW), F32)],
        scratch_shapes=[pltpu.VMEM((tm, AW), F32)], compiler_params=_params(1))


def _adamw(name, parts, part_block, part_index, w, m, v, tr, row_off=0, n_rows=None, prev=None, dep=None):
    R, C = w.shape
    n_rows = R if n_rows is None else n_rows
    assert n_rows % tr == 0 and row_off % tr == 0
    bc1 = 1.0 - ADAM_B1 ** ADAM_STEP
    bc2 = 1.0 - ADAM_B2 ** ADAM_STEP

    def body(p_ref, w_ref, m_ref, v_ref, *rest):
        g_ref, d_ref, nm_ref, nv_ref = rest[-4:]
        g = p_ref[0].astype(F32)
        for s in range(1, NDEV):
            g = g + p_ref[s].astype(F32)
        nm = ADAM_B1 * m_ref[...] + (1.0 - ADAM_B1) * g
        nv = ADAM_B2 * v_ref[...] + (1.0 - ADAM_B2) * (g * g)
        g_ref[...] = g
        nm_ref[...] = nm
        nv_ref[...] = nv
        d_ref[...] = -ADAM_LR * ((nm / bc1) / (jnp.sqrt(nv / bc2) + ADAM_EPS) + ADAM_WD * w_ref[...])

    ob = row_off // tr
    row = pl.BlockSpec((tr, C), lambda i: (ob + i, 0))
    out = jax.ShapeDtypeStruct((R, C), F32)
    chained = prev is not None
    return _pcall(
        body, name=name, ins=[parts, w, m, v] + (list(prev) if chained else []), dep=dep, grid=(n_rows // tr,),
        in_specs=[pl.BlockSpec(part_block, part_index), row, row, row] + ([ANY] * 4 if chained else []),
        out_specs=[row, row, row, row], out_shape=[out, out, out, out],
        input_output_aliases={4 + t: t for t in range(4)} if chained else {}, compiler_params=_params(1))


def _place():
    return lax.axis_index("x"), lax.axis_index("y"), lax.axis_index("c")


def _slot(px, py, pc):
    return 4 * px + 2 * py + pc


def _peer(k, x, y, c):
    return x ^ ((k >> 2) & 1), y ^ ((k >> 1) & 1), c ^ (k & 1)


def _all_gather(name, arrs, dep=None):
    n = len(arrs)

    def body(*refs):
        in_refs, out_refs = refs[:n], refs[n:2 * n]
        send_sems, recv_sems, local_sems = refs[2 * n:]
        x, y, c = _place()
        me, sibling = (x, y, c), (x, y, 1 - c)
        chips = [(1 - x, y), (x, 1 - y), (1 - x, 1 - y)]

        def copy(a, k, block, to, src=None):
            dst = out_refs[a].at[_slot(*block)]
            return pltpu.make_async_remote_copy(
                src_ref=dst if src is None else src, dst_ref=dst, send_sem=send_sems.at[a, k],
                recv_sem=recv_sems.at[a, k], device_id=to, device_id_type=MESH)

        mine = [pltpu.make_async_copy(in_refs[a], out_refs[a].at[_slot(*me)], local_sems.at[a]) for a in range(n)]
        for cp in mine:
            cp.start()
        first = []
        for a in range(n):
            first.append(copy(a, 0, me, sibling, src=in_refs[a]))
            first += [copy(a, 1 + j, me, (*chip, c), src=in_refs[a]) for j, chip in enumerate(chips)]
        for cp in first:
            cp.start()
        passed = []
        for j, chip in enumerate(chips):
            for a in range(n):
                copy(a, 1 + j, (*chip, c), me).wait_recv()
                fwd = copy(a, 4 + j, (*chip, c), sibling)
                fwd.start()
                passed.append(fwd)
        for a in range(n):
            copy(a, 0, sibling, me).wait_recv()
            for j, chip in enumerate(chips):
                copy(a, 4 + j, (*chip, 1 - c), me).wait_recv()
        for cp in first + passed:
            cp.wait_send()
        for cp in mine:
            cp.wait()

    return _pcall(
        body, name=name, ins=list(arrs), in_specs=[HBM] * n, dep=dep,
        out_shape=[jax.ShapeDtypeStruct((NDEV, *a.shape), a.dtype) for a in arrs], out_specs=[HBM] * n,
        scratch_shapes=[pltpu.SemaphoreType.DMA((n, 7)), pltpu.SemaphoreType.DMA((n, 7)),
                        pltpu.SemaphoreType.DMA((n,))],
        compiler_params=pltpu.CompilerParams(has_side_effects=True))


SEND_PEERS = {"exchange": tuple(range(1, NDEV)), "gather": (1, 2, 4, 6), "forward": (2, 4, 6)}


def _send_copies(mode, src_refs, land_refs, send_sems, recv_sems):
    x, y, c = _place()
    me = _slot(x, y, c)
    peers = SEND_PEERS[mode]
    copies = []
    for i, k in enumerate(peers):
        peer = _peer(k, x, y, c)
        for a, land in enumerate(land_refs):
            if mode == "exchange":
                src, dst, to = src_refs[a].at[_slot(*peer)], land.at[me], peer
            elif mode == "gather":
                src, dst, to = src_refs[a], land.at[me], peer
            else:
                src = dst = land.at[_slot(*peer)]
                to = (x, y, 1 - c)
            s = a * len(peers) + i
            copies.append(pltpu.make_async_remote_copy(src_ref=src, dst_ref=dst, send_sem=send_sems.at[s],
                                                       recv_sem=recv_sems.at[s], device_id=to, device_id_type=MESH))
    return copies


def _send_start(name, srcs, lands, mode, dep=None):
    n_src, n = len(srcs), len(lands)
    n_sem = n * len(SEND_PEERS[mode])

    def body(*refs):
        src_refs, land_refs = refs[:n_src], refs[n_src:n_src + n]
        send_sems, recv_sems = refs[n_src + n], refs[n_src + n + 1]
        token = refs[-1]
        for cp in _send_copies(mode, src_refs, land_refs, send_sems, recv_sems):
            cp.start()
        token[...] = jnp.zeros_like(token)

    thru = [pltpu.HBM(t.shape, t.dtype) for t in [*srcs, *lands]]
    out = _pcall(
        body, name=name, ins=[pltpu.with_memory_space_constraint(t, pltpu.HBM) for t in [*srcs, *lands]],
        in_specs=[HBM] * (n_src + n), dep=dep,
        out_shape=(pltpu.SemaphoreType.DMA((n_sem,)), pltpu.SemaphoreType.DMA((n_sem,)), *thru,
                   jax.ShapeDtypeStruct((8, LANES), F32)),
        out_specs=(SEM, SEM, *[HBM] * (n_src + n), pl.BlockSpec(memory_space=pltpu.VMEM)),
        input_output_aliases={i: 2 + i for i in range(n_src + n)},
        compiler_params=pltpu.CompilerParams(has_side_effects=pltpu.SideEffectType.DATAFLOW_SIDE_EFFECTING))
    return out[-1], out[0], out[1], list(out[2:2 + n_src]), list(out[2 + n_src:2 + n_src + n])


def _send_wait(name, started, mode, dep=None):
    _, send_sems, recv_sems, srcs, lands = started
    n_src, n = len(srcs), len(lands)

    def body(*refs):
        src_refs, land_refs = refs[:n_src], refs[n_src:n_src + n]
        ssem, rsem = refs[n_src + n], refs[n_src + n + 1]
        for cp in _send_copies(mode, src_refs, land_refs, ssem, rsem):
            cp.wait_send()
            cp.wait_recv()

    thru = [pltpu.HBM(t.shape, t.dtype) for t in [*srcs, *lands]]
    out = _pcall(
        body, name=name, ins=[*srcs, *lands, send_sems, recv_sems], in_specs=[HBM] * (n_src + n) + [SEM, SEM], dep=dep,
        out_shape=tuple(thru), out_specs=tuple([HBM] * (n_src + n)),
        input_output_aliases={i: i for i in range(n_src + n)},
        compiler_params=pltpu.CompilerParams(has_side_effects=pltpu.SideEffectType.DATAFLOW_SIDE_EFFECTING))
    return list(out[n_src:])


def _landing(block, me):
    zone = lax.empty((NDEV, *block.shape), block.dtype)
    return lax.dynamic_update_slice(zone, block[None], (me,) + (0,) * block.ndim)


def _rows128(t):
    flat = t.reshape(-1)
    n = flat.shape[0]
    rows = -(-n // (8 * LANES)) * 8
    return jnp.pad(flat, (0, rows * LANES - n)).reshape(rows, LANES)


def kernel(x, mix_norm_g, ffn_norm_g, a_w_in, a_ln_g, a_w_spatial, a_b_spatial, a_w_out, kv_norm_g, w_k, w_v, b_w_q, b_sinks, b_w_o, rel_bias, ffn_w1, ffn_w2, final_norm_g, loss_target, m_mix_norm_g, m_ffn_norm_g, m_a_w_in, m_a_ln_g, m_a_w_spatial, m_a_b_spatial, m_a_w_out, m_kv_norm_g, m_w_k, m_w_v, m_b_w_q, m_b_sinks, m_b_w_o, m_rel_bias, m_ffn_w1, m_ffn_w2, m_final_norm_g, v_mix_norm_g, v_ffn_norm_g, v_a_w_in, v_a_ln_g, v_a_w_spatial, v_a_b_spatial, v_a_w_out, v_kv_norm_g, v_w_k, v_w_v, v_b_w_q, v_b_sinks, v_b_w_o, v_rel_bias, v_ffn_w1, v_ffn_w2, v_final_norm_g):
    c = _config(x, a_w_in, a_w_out, w_k, b_w_q, b_w_o, ffn_w1, ffn_w2)
    S, D, LA, LB, LF = c.S, c.D, c.LA, c.LB, c.LF
    weights = dict(mix_norm_g=mix_norm_g, ffn_norm_g=ffn_norm_g, a_w_in=a_w_in, a_ln_g=a_ln_g, a_w_spatial=a_w_spatial,
                   a_b_spatial=a_b_spatial, a_w_out=a_w_out, kv_norm_g=kv_norm_g, w_k=w_k, w_v=w_v, b_w_q=b_w_q,
                   b_sinks=b_sinks, b_w_o=b_w_o, rel_bias=rel_bias, ffn_w1=ffn_w1, ffn_w2=ffn_w2,
                   final_norm_g=final_norm_g)
    m_in = dict(mix_norm_g=m_mix_norm_g, ffn_norm_g=m_ffn_norm_g, a_w_in=m_a_w_in, a_ln_g=m_a_ln_g,
                a_w_spatial=m_a_w_spatial, a_b_spatial=m_a_b_spatial, a_w_out=m_a_w_out, kv_norm_g=m_kv_norm_g,
                w_k=m_w_k, w_v=m_w_v, b_w_q=m_b_w_q, b_sinks=m_b_sinks, b_w_o=m_b_w_o, rel_bias=m_rel_bias,
                ffn_w1=m_ffn_w1, ffn_w2=m_ffn_w2, final_norm_g=m_final_norm_g)
    v_in = dict(mix_norm_g=v_mix_norm_g, ffn_norm_g=v_ffn_norm_g, a_w_in=v_a_w_in, a_ln_g=v_a_ln_g,
                a_w_spatial=v_a_w_spatial, a_b_spatial=v_a_b_spatial, a_w_out=v_a_w_out, kv_norm_g=v_kv_norm_g,
                w_k=v_w_k, w_v=v_w_v, b_w_q=v_b_w_q, b_sinks=v_b_sinks, b_w_o=v_b_w_o, rel_bias=v_rel_bias,
                ffn_w1=v_ffn_w1, ffn_w2=v_ffn_w2, final_norm_g=v_final_norm_g)
    names = list(weights)
    seq = _Seq()
    me = _slot(*_place())
    bf = lambda t: t.astype(BF16)

    def rows_of(l):
        second = a_w_out[l] if l < LA else b_w_o[l - LA]
        return bf(jnp.concatenate([ffn_w2[l], second], axis=0))

    tr = lambda t: bf(jnp.swapaxes(t, -1, -2))
    groups = [[tr(a_w_in[0])[None], a_ln_g],
              [tr(ffn_w1[0])[None], rows_of(0)]]
    for l in range(1, LA):
        groups.append([tr(jnp.stack([a_w_in[l], ffn_w1[l]])), rows_of(l)])
    for l in range(LB):
        extra = [bf(jnp.concatenate([w_k, w_v], axis=1))] if l == 0 else []
        groups.append(extra + [bf(b_w_q[l]), tr(ffn_w1[LA + l])[None], rows_of(LA + l)])
    started = [seq(_send_start, f"weights_start{i}", grp, [_landing(t, me) for t in grp], "gather")
               for i, grp in enumerate(groups)]
    forwarding = {}

    def forward(i):
        lands = seq(_send_wait, f"weights_wait{i}", started[i], "gather")
        forwarding[i] = seq(_send_start, f"weights_forward{i}", [], lands, "forward")

    def arrive(i):
        if i not in forwarding:
            forward(i)
        return seq(_send_wait, f"weights_arrive{i}", forwarding[i], "forward")

    causal = jnp.tril(jnp.ones((CHUNK, CHUNK), bool))
    wsp = jnp.where(causal[None, None], a_w_spatial, 0.0)
    wsp16 = wsp.astype(BF16)
    wsp16_t = jnp.swapaxes(wsp, -1, -2).astype(BF16)
    bsp_t = jnp.swapaxes(a_b_spatial, -1, -2)
    mix_g = mix_norm_g.reshape(-1, 1, D)
    ffn_g = ffn_norm_g.reshape(-1, 1, D)
    kv_g = kv_norm_g.reshape(1, D)
    fin_g = final_norm_g.reshape(1, D)
    onehot = _bucket_onehot()
    bias = _slot_bias(seq(_band_bias, rel_bias.T, onehot).reshape(N_HEADS, BLOCK, 2 * BLOCK))

    h = x.reshape(S, D)
    sav_a, sav_b, wts_a, wts_b = [], [], [], []
    for l in range(LA):
        if l == 0:
            w_in, lng_all = arrive(0)
            ln_g_full = jnp.transpose(lng_all, (1, 0, 2)).reshape(LA, 1, c.AW)
            in_i, w1_i = 0, 0
        else:
            w_in, rows = arrive(l + 1)
            w1, in_i, w1_i = w_in, 0, 1
        z, a, hn = seq(_a_in_fwd, c, f"a_in_fwd{l}", h, mix_g[l], w_in, in_i)
        if l == 0:
            forward(1)
        gated = seq(_sgu_fwd, c, f"sgu_fwd{l}", a, ln_g_full[l], wsp16[l], bsp_t[l])
        if l == 0:
            w1, rows = arrive(1)
        h1 = seq(_mm_res, c, f"a_out_fwd{l}", gated, rows, c.ar, c.fr // c.ar, h)
        if l == LA - 1:
            forward(LA + 1)
        p, h2, hnf = seq(_ffn_fwd, c, f"ffn_fwd{l}", h1, ffn_g[l], w1, w1_i, rows)
        sav_a.append((h, z, a, hn, gated, h1, p, hnf))
        wts_a.append((w_in, in_i, w1, w1_i, rows))
        h = h2
    h_kv = h
    for l in range(LB):
        got = arrive(LA + 1 + l)
        if l == 0:
            wkv, got = got[0], got[1:]
            kv, hkv = seq(_rms_mm_rows, c, "kv_fwd", h, kv_g, wkv, c.kr, 0, 2 * c.DKV)
        wq, w1, rows = got
        wq = _slot_cols(wq)
        wo = _slot_rows(rows[:, c.fr:, :])
        q, hn = seq(_rms_mm_rows, c, f"q_fwd{l}", h, mix_g[LA + l], wq, c.qr, 0, c.DQ)
        o = seq(_attn_fwd, c, f"attn_fwd{l}", q, kv, bias, b_sinks[l])
        h1 = seq(_mm_res, c, f"o_fwd{l}", o, wo, c.orr, 0, h)
        if l + 1 < LB:
            forward(LA + 2 + l)
        p, h2, hnf = seq(_ffn_fwd, c, f"ffn_fwd{LA + l}", h1, ffn_g[LA + l], w1, 0, rows)
        sav_b.append((h, q, hn, o, h1, p, hnf))
        wts_b.append((wq, wo, w1, rows))
        h = h2
    dh, d_fin_g, loss_row = seq(_final_loss, c, h, fin_g, loss_target.reshape(S, D))
    loss = lax.psum(loss_row[0, 0], MESH_AXES)

    results = {}
    in_flight = []

    def update(k, parts, layer, col_blk=0):
        w = weights[k]
        rows_l, ncols = (w.shape[-2], w.shape[-1]) if w.ndim == 3 else w.shape
        flat = lambda t: t.reshape(-1, ncols)
        tr = min(256, rows_l)
        results[k] = seq(_adamw, f"adamw_{k}{layer}", parts, (NDEV, tr, ncols), lambda i: (0, i, col_blk),
                         flat(w), flat(m_in[k]), flat(v_in[k]), tr, row_off=layer * rows_l, n_rows=rows_l,
                         prev=results.get(k))

    def land(tag, entry):
        lands = seq(_send_wait, f"grads_wait_{tag}", entry[1], "exchange")
        for keys, parts in zip(entry[0], lands):
            for k, layer, col_blk in keys:
                update(k, parts, layer, col_blk)

    def send(tag, items):
        slabs = [t for _, t in items]
        own = [_landing(lax.dynamic_index_in_dim(t, me, 0, keepdims=False), me) for t in slabs]
        st = seq(_send_start, f"grads_start_{tag}", slabs, own, "exchange")
        in_flight.append((tag, ([k for k, _ in items], st)))
        while len(in_flight) > EXCHANGE_LAG:
            land(*in_flight.pop(0))

    d_mix_g, d_ffn_g = [None] * LF, [None] * LF
    dkv_list, dbias_list, dsink_list = [], [], [None] * LB

    def ffn_bwd(lf, dh, h1, p, hnf, w1, w1_i, rows):
        da, dh1, d_ffn_g[lf], dhb = seq(_ffn_bwd_data, c, f"ffn_bwd_data{lf}", dh, p, w1, w1_i, rows, h1, ffn_g[lf])
        dw1, dw2 = seq(_ffn_bwd_w, c, f"ffn_bwd_w{lf}", hnf, da, p, dhb)
        send(f"ffn{lf}", [([("ffn_w1", lf, 0)], dw1), ([("ffn_w2", lf, 0)], dw2)])
        return dh1

    for l in reversed(range(LB)):
        h0, q, hn, o, h1, p, hnf = sav_b[l]
        wq, wo, w1, rows = wts_b[l]
        dh1 = ffn_bwd(LA + l, dh, h1, p, hnf, w1, 0, rows)
        do = seq(_bwd_rows_data, c, f"o_bwd_data{l}", dh1, wo, c.orr, 0)
        dwo = _unslot_rows(seq(_wgrad_rows, c, f"o_bwd_w{l}", o, [dh1], c.DQ, D))
        dq, dkv, dbias, dsink = seq(_attn_bwd, c, f"attn_bwd{l}", q, kv, do, bias, b_sinks[l])
        dsink_list[l] = dsink.reshape(Q_PER_KV, 2, KV_PAIRS).transpose(2, 1, 0).reshape(1, N_HEADS)
        dkv_list.append(dkv)
        dbias_list.append(_unslot_bias(dbias))
        dwq = _unslot_cols(seq(_wgrad_rows, c, f"q_bwd_w{l}", hn, [dq], D, c.DQ))
        send(f"attn{l}", [([("b_w_o", l, 0)], dwo), ([("b_w_q", l, 0)], dwq)])
        dh, d_mix_g[LA + l] = seq(_bwd_rows_to_stream, c, f"q_bwd_data{l}", [dq], wq, c.qr, 0, c.DQ, h0,
                                  mix_g[LA + l], dh1)
    dwkv = seq(_wgrad_rows, c, "kv_bwd_w", hkv, dkv_list, D, 2 * c.DKV)
    send("kv", [([("w_k", 0, 0), ("w_v", 0, 1)], dwkv)])
    dh, d_kv_g = seq(_bwd_rows_to_stream, c, "kv_bwd_data", dkv_list, wkv, c.kr, 0, 2 * c.DKV, h_kv, kv_g, dh)
    d_rel_t = seq(_band_bias_grad, dbias_list, onehot)
    d_wsp, d_bsp, d_lng = [None] * LA, [None] * LA, [None] * LA
    for l in reversed(range(LA)):
        h0, z, a, hn, gated, h1, p, hnf = sav_a[l]
        w_in, in_i, w1, w1_i, rows = wts_a[l]
        dh1 = ffn_bwd(l, dh, h1, p, hnf, w1, w1_i, rows)
        dgated = seq(_bwd_rows_data, c, f"a_out_bwd_data{l}", dh1, rows, c.ar, c.fr // c.ar)
        dwout = seq(_wgrad_rows, c, f"a_out_bwd_w{l}", gated, [dh1], c.AW, D)
        send(f"a_out{l}", [([("a_w_out", l, 0)], dwout)])
        dz, d_wsp[l], dbt, d_lng[l] = seq(_sgu_bwd, c, f"sgu_bwd{l}", a, z, dgated, ln_g_full[l], wsp16[l],
                                          wsp16_t[l], bsp_t[l])
        d_bsp[l] = dbt.T
        dwin = seq(_wgrad_cols, c, f"a_in_bwd_w{l}", hn, dz)
        send(f"a_in{l}", [([("a_w_in", l, 0)], dwin)])
        dh, d_mix_g[l] = seq(_bwd_cols_to_stream, c, f"a_in_bwd_data{l}", dz, w_in, in_i, h0, mix_g[l], dh1)
    grad_x = dh.reshape(1, S, D)

    small = {
        "mix_norm_g": jnp.concatenate(d_mix_g, axis=0), "ffn_norm_g": jnp.concatenate(d_ffn_g, axis=0),
        "a_w_spatial": jnp.stack(d_wsp), "a_b_spatial": jnp.stack(d_bsp), "kv_norm_g": d_kv_g,
        "b_sinks": jnp.concatenate(dsink_list, axis=0), "rel_bias": d_rel_t.T, "final_norm_g": d_fin_g,
    }
    small_names = list(small)
    packs = [_rows128(small[k]) for k in small_names] + [_rows128(jnp.concatenate(d_lng, axis=0))]
    offs = np.cumsum([0] + [p.shape[0] for p in packs])
    (small_all,) = seq(_all_gather, "gather_small_grads", [jnp.concatenate(packs, axis=0)])
    Rs = int(offs[-1])
    while in_flight:
        land(*in_flight.pop(0))

    grads, deltas, new_m, new_v = {}, {}, {}, {}

    def put(k, outs, shape):
        grads[k], deltas[k], new_m[k], new_v[k] = (t.reshape(shape) for t in outs)

    def pack_state(d):
        return jnp.concatenate([_rows128(d[k]) for k in small_names]
                               + [jnp.zeros((packs[-1].shape[0], LANES), F32)], axis=0)

    outs = seq(_adamw, "adamw_small", small_all, (NDEV, Rs, LANES), lambda i: (0, 0, 0),
               pack_state(weights), pack_state(m_in), pack_state(v_in), Rs)
    for n_, k in enumerate(small_names):
        shape = weights[k].shape
        size = int(np.prod(shape))
        put(k, [t[int(offs[n_]):int(offs[n_ + 1])].reshape(-1)[:size] for t in outs], shape)
    lng_sum = outs[0][int(offs[-2]):int(offs[-1])].reshape(-1)[:LA * c.AW].reshape(LA, c.AW)
    lng_mine = lax.dynamic_slice_in_dim(lng_sum, me * c.ar, c.ar, axis=1)
    lng_parts = jnp.concatenate([lng_mine[None], jnp.zeros((NDEV - 1, LA, c.ar), F32)], axis=0)
    put("a_ln_g", seq(_adamw, "adamw_ln_g", lng_parts, (NDEV, LA, c.ar), lambda i: (0, 0, 0),
                      a_ln_g, m_in["a_ln_g"], v_in["a_ln_g"], LA), a_ln_g.shape)
    for k in ("a_w_in", "ffn_w1", "ffn_w2", "a_w_out", "b_w_o", "b_w_q", "w_k", "w_v"):
        put(k, results[k], weights[k].shape)

    return (loss, grad_x, *[grads[k] for k in names], *[deltas[k] for k in names],
            *[new_m[k] for k in names], *[new_v[k] for k in names])
```

```python
import numpy as np
import math
import jax
import jax.numpy as jnp
from jax import lax
from jax.experimental import pallas as pl
from jax.experimental.pallas import tpu as pltpu

F32 = jnp.float32
BF16 = jnp.bfloat16

NDEV = 8
EPS = 1e-6
CHUNK = 128
A_GROUPS = 8
N_HEADS = 16
N_KV_HEADS = 4
Q_PER_KV = N_HEADS // N_KV_HEADS
HEAD_DIM = 64
BLOCK = 128
N_BUCKETS = 32
MAX_DISTANCE = 128
ADAM_LR, ADAM_B1, ADAM_B2, ADAM_EPS, ADAM_WD, ADAM_STEP = 0.001, 0.9, 0.999, 1e-08, 0.01, 10
LANES = 128
VMEM_LIMIT = 56 * 1024 * 1024
INV_SQRT2 = 0.7071067811865476
INV_SQRT_2PI = 0.3989422804014327
MESH_AXES = ("x", "y", "c")
EXCHANGE_LAG = 2

HBM = pl.BlockSpec(memory_space=pltpu.HBM)
SMEM = pl.BlockSpec(memory_space=pltpu.SMEM)
ANY = pl.BlockSpec(memory_space=pl.ANY)
SEM = pl.BlockSpec(memory_space=pltpu.SEMAPHORE)
MESH = pl.DeviceIdType.MESH


def _params(n_grid):
    return pltpu.CompilerParams(dimension_semantics=("arbitrary",) * n_grid, vmem_limit_bytes=VMEM_LIMIT)


def _const(block, index_map):
    return pl.BlockSpec(block, index_map, pipeline_mode=pl.Buffered(1))


def _pcall(body, *, ins, in_specs, dep=None, **kw):
    n_in = len(ins)
    if dep is None or any(dep is t for t in ins):
        return pl.pallas_call(body, in_specs=list(in_specs), **kw)(*ins)

    def with_dep(*refs):
        body(*refs[:n_in], *refs[n_in + 1:])

    return pl.pallas_call(with_dep, in_specs=[*in_specs, ANY], **kw)(*ins, dep)


class _Seq:
    def __init__(self):
        self.last = None

    def __call__(self, fn, *args, **kw):
        out = fn(*args, dep=self.last, **kw)
        self.last = out[0] if isinstance(out, (tuple, list)) else out
        return out


def _rstd(h):
    return lax.rsqrt(jnp.mean(h * h, axis=-1, keepdims=True) + EPS)


def _rms_bwd(dhn, h, g, dres):
    r = _rstd(h)
    xh = h * r
    dg = jnp.sum(dhn * xh, axis=0, keepdims=True)
    dxh = dhn * g
    dx = r * (dxh - xh * jnp.mean(dxh * xh, axis=-1, keepdims=True))
    return dres + dx, dg


def _gelu(z):
    return 0.5 * z * (1.0 + lax.erf(z * INV_SQRT2))


def _gelu_grad(z):
    return 0.5 * (1.0 + lax.erf(z * INV_SQRT2)) + z * (jnp.exp(-0.5 * z * z) * INV_SQRT_2PI)


def _dot(a, b, dims):
    return lax.dot_general(a, b, (dims, ((), ())), preferred_element_type=F32)


NN = ((1,), (0,))
NT = ((1,), (1,))
TN = ((0,), (0,))


def _mm(name, ins, in_specs, out_shapes, out_specs, *, grid, dims, nk, acc_shape, load_a, load_b, epilogue,
        dep=None):
    n_in, n_out = len(ins), len(out_shapes)
    kax = len(grid) - 1

    def body(*refs):
        in_refs = refs[:n_in]
        out_refs = refs[n_in:n_in + n_out]
        a = load_a(in_refs, out_refs)
        b = load_b(in_refs)
        prod = _dot(a, b, dims)
        if nk == 1:
            epilogue(prod, in_refs, out_refs)
        else:
            acc = refs[n_in + n_out]
            k = pl.program_id(kax)

            @pl.when(k == 0)
            def _():
                acc[...] = prod

            @pl.when(k > 0)
            def _():
                acc[...] += prod

            @pl.when(k == nk - 1)
            def _():
                epilogue(acc[...], in_refs, out_refs)

    return _pcall(
        body, name=name, ins=ins, in_specs=in_specs, dep=dep, grid=grid, out_specs=out_specs, out_shape=out_shapes,
        scratch_shapes=[pltpu.VMEM(acc_shape, F32)] if nk > 1 else [], compiler_params=_params(len(grid)))


def _bf(ref_idx):
    return lambda in_refs, *_: in_refs[ref_idx][...].astype(BF16)


def _b_view(ref_idx, rows):
    def load(in_refs):
        b = in_refs[ref_idx][...]
        return b.reshape(rows, b.shape[-1])
    return load


class Cfg:
    pass


def _config(x, a_w_in, a_w_out, w_k, b_w_q, b_w_o, ffn_w1, ffn_w2):
    c = Cfg()
    c.S, c.D = x.shape[1], x.shape[2]
    c.LA, _, c.cw = a_w_in.shape
    c.AW2 = NDEV * c.cw
    c.AW = c.AW2 // 2
    c.gd = c.AW // A_GROUPS
    c.ar = a_w_out.shape[1]
    c.LF, _, c.fw = ffn_w1.shape
    c.fr = ffn_w2.shape[1]
    c.LB, c.qr, c.DQ = b_w_q.shape
    c.orr = b_w_o.shape[1]
    c.kr, c.DKV = w_k.shape
    c.tm = min(512, c.S)
    c.tms = min(256, c.S)
    c.nb = c.S // BLOCK
    assert c.cw == c.fw == c.fr and c.AW == NDEV * c.ar and c.D == NDEV * c.qr == NDEV * c.kr
    assert c.DQ == NDEV * c.orr == N_HEADS * HEAD_DIM and c.DKV == N_KV_HEADS * HEAD_DIM
    assert c.S % c.tm == 0 and c.S % c.tms == 0 and c.tms % CHUNK == 0 and c.gd % LANES == 0
    assert c.fr % c.ar == 0 and c.fr % c.orr == 0
    assert c.LA >= 1 and c.LB >= 1 and c.LF == c.LA + c.LB
    return c


def _cached_rms(h_idx, g_idx, hn_out_idx, jax_axis=1):
    def load(in_refs, out_refs):
        hn_ref = out_refs[hn_out_idx]

        @pl.when(pl.program_id(jax_axis) == 0)
        def _():
            h = in_refs[h_idx][...]
            hn_ref[...] = (h * _rstd(h) * in_refs[g_idx][...]).astype(BF16)

        return hn_ref[...]
    return load


def _a_in_fwd(c, name, h, g, col, ci, dep=None):
    S, D, cw, tm = c.S, c.D, c.cw, c.tm

    def body(h_ref, g_ref, w_ref, z_ref, a_ref, hn_ref):
        h = h_ref[...]
        hn = (h * _rstd(h) * g_ref[...]).astype(BF16)
        hn_ref[...] = hn
        for j in range(NDEV):
            cols = slice(j * cw, (j + 1) * cw)
            z = _dot(hn, w_ref[j], NT)
            z_ref[:, cols] = z.astype(BF16)
            a_ref[:, cols] = _gelu(z).astype(BF16)

    row = pl.BlockSpec((tm, D), lambda i: (i, 0))
    wide = pl.BlockSpec((tm, c.AW2), lambda i: (i, 0))
    return _pcall(
        body, name=name, ins=[h, g, col], dep=dep, grid=(S // tm,),
        in_specs=[row, pl.BlockSpec((1, D), lambda i: (0, 0)), _const((NDEV, None, cw, D), lambda i: (0, ci, 0, 0))],
        out_specs=[wide, wide, row],
        out_shape=[jax.ShapeDtypeStruct((S, c.AW2), BF16), jax.ShapeDtypeStruct((S, c.AW2), BF16),
                   jax.ShapeDtypeStruct((S, D), BF16)],
        compiler_params=_params(1))


def _rms_mm_rows(c, name, h, g, slab, blk_rows, blk_idx, n_out, dep=None):
    S, D, tm = c.S, c.D, c.tm

    def epilogue(acc, in_refs, out_refs):
        out_refs[0][...] = acc.astype(BF16)

    return _mm(
        name, [h, slab, g],
        [pl.BlockSpec((tm, D), lambda i, j, k: (i, 0)),
         pl.BlockSpec((NDEV, blk_rows, n_out), lambda i, j, k: (0, blk_idx, 0)),
         pl.BlockSpec((1, D), lambda i, j, k: (0, 0))],
        [jax.ShapeDtypeStruct((S, n_out), BF16), jax.ShapeDtypeStruct((S, D), BF16)],
        [pl.BlockSpec((tm, n_out), lambda i, j, k: (i, 0)), pl.BlockSpec((tm, D), lambda i, j, k: (i, 0))],
        grid=(S // tm, 1, 1), dims=NN, nk=1, acc_shape=None,
        load_a=_cached_rms(0, 2, 1), load_b=_b_view(1, NDEV * blk_rows), epilogue=epilogue, dep=dep)


def _mm_res(c, name, a, slab, blk_rows, blk_idx, res, dep=None):
    S, D, tm = c.S, c.D, c.tm
    K = NDEV * blk_rows

    def epilogue(acc, in_refs, out_refs):
        out_refs[0][...] = in_refs[2][...] + acc

    return _mm(
        name, [a, slab, res],
        [pl.BlockSpec((tm, K), lambda i, j, k: (i, 0)),
         pl.BlockSpec((NDEV, blk_rows, D), lambda i, j, k: (0, blk_idx, 0)),
         pl.BlockSpec((tm, D), lambda i, j, k: (i, 0))],
        [jax.ShapeDtypeStruct((S, D), F32)], [pl.BlockSpec((tm, D), lambda i, j, k: (i, 0))],
        grid=(S // tm, 1, 1), dims=NN, nk=1, acc_shape=None,
        load_a=_bf(0), load_b=_b_view(1, K), epilogue=epilogue, dep=dep)[0]


def _sgu_masks():
    ii = lax.broadcasted_iota(jnp.int32, (CHUNK, CHUNK), 0)
    jj = lax.broadcasted_iota(jnp.int32, (CHUNK, CHUNK), 1)
    return ii >= jj


def _sgu_fwd(c, name, a, ln_g, wc, b_t, dep=None):
    S, AW, gd, tm = c.S, c.AW, c.gd, c.tms

    def body(a_ref, lng_ref, wc_ref, bt_ref, out_ref):
        va = a_ref[:, AW:].astype(F32)
        xc = va - jnp.mean(va, axis=-1, keepdims=True)
        vn = (xc * lax.rsqrt(jnp.mean(xc * xc, axis=-1, keepdims=True) + EPS) * lng_ref[...]).astype(BF16)
        for ch in range(tm // CHUNK):
            rows = slice(ch * CHUNK, (ch + 1) * CHUNK)
            for g in range(A_GROUPS):
                cols = slice(g * gd, (g + 1) * gd)
                mixed = _dot(wc_ref[g], vn[rows, cols], NN) + bt_ref[:, g:g + 1]
                out_ref[rows, cols] = (a_ref[rows, cols].astype(F32) * mixed).astype(BF16)

    return _pcall(
        body, name=name, ins=[a, ln_g, wc, b_t], dep=dep, grid=(S // tm,),
        in_specs=[pl.BlockSpec((tm, 2 * AW), lambda i: (i, 0)), pl.BlockSpec((1, AW), lambda i: (0, 0)),
                  pl.BlockSpec((A_GROUPS, CHUNK, CHUNK), lambda i: (0, 0, 0)),
                  pl.BlockSpec((CHUNK, A_GROUPS), lambda i: (0, 0))],
        out_specs=pl.BlockSpec((tm, AW), lambda i: (i, 0)),
        out_shape=jax.ShapeDtypeStruct((S, AW), BF16), compiler_params=_params(1))


def _ffn_fwd(c, name, h, g, col, ci, rows, dep=None):
    S, D, fw, tm = c.S, c.D, c.fw, c.tm
    F = NDEV * fw

    def body(h_ref, g_ref, w1_ref, w2_ref, p_ref, out_ref, hn_ref, r_ref):
        h = h_ref[...]
        hn = (h * _rstd(h) * g_ref[...]).astype(BF16)
        hn_ref[...] = hn
        for j in range(NDEV):
            cols = slice(j * fw, (j + 1) * fw)
            p = jnp.maximum(_dot(hn, w1_ref[j], NT), 0.0)
            p_ref[:, cols] = p.astype(BF16)
            r_ref[:, cols] = (p * p).astype(BF16)
        out_ref[...] = h + _dot(r_ref[...], w2_ref[...].reshape(F, D), NN)

    row = pl.BlockSpec((tm, D), lambda i: (i, 0))
    return _pcall(
        body, name=name, ins=[h, g, col, rows], dep=dep, grid=(S // tm,),
        in_specs=[row, pl.BlockSpec((1, D), lambda i: (0, 0)),
                  _const((NDEV, None, fw, D), lambda i: (0, ci, 0, 0)), _const((NDEV, c.fr, D), lambda i: (0, 0, 0))],
        out_specs=[pl.BlockSpec((tm, F), lambda i: (i, 0)), row, row],
        out_shape=[jax.ShapeDtypeStruct((S, F), BF16), jax.ShapeDtypeStruct((S, D), F32),
                   jax.ShapeDtypeStruct((S, D), BF16)],
        scratch_shapes=[pltpu.VMEM((tm, F), BF16)], compiler_params=_params(1))


def _bucket_table():
    qi = np.arange(BLOCK)[:, None]
    kj = np.arange(2 * BLOCK)[None, :]
    d = np.maximum(qi + BLOCK - kj, 0)
    max_exact = N_BUCKETS // 2
    ratio = np.log(np.maximum(d, 1).astype(np.float32) / np.float32(max_exact)) / np.float32(
        math.log(MAX_DISTANCE / max_exact))
    large = np.minimum(max_exact + (ratio.astype(np.float32) * np.float32(N_BUCKETS - max_exact)).astype(np.int32),
                       N_BUCKETS - 1)
    return np.where(d < max_exact, d, large).astype(np.int32)


def _bucket_onehot():
    b = jnp.asarray(_bucket_table().reshape(1, -1))
    return (b == lax.broadcasted_iota(jnp.int32, (N_BUCKETS, b.shape[1]), 0)).astype(F32)


def _whole(t):
    return pl.BlockSpec(t.shape, lambda: (0,) * t.ndim)


def _band_bias(rel_bias_t, onehot, dep=None):
    def body(r_ref, oh_ref, out_ref):
        out_ref[...] = lax.dot_general(r_ref[...], oh_ref[...], (NN, ((), ())), preferred_element_type=F32,
                                       precision=lax.Precision.HIGHEST)

    n = onehot.shape[1]
    return _pcall(body, name="band_bias", ins=[rel_bias_t, onehot], in_specs=[_whole(rel_bias_t), _whole(onehot)],
                  dep=dep, out_shape=jax.ShapeDtypeStruct((N_HEADS, n), F32), compiler_params=_params(0))


def _band_bias_grad(dbias_list, onehot, dep=None):
    n_in = len(dbias_list)

    def body(*refs):
        oh_ref, out_ref = refs[n_in], refs[n_in + 1]
        d = refs[0][...]
        for r in refs[1:n_in]:
            d = d + r[...]
        out_ref[...] = lax.dot_general(d, oh_ref[...], (NT, ((), ())), preferred_element_type=F32,
                                       precision=lax.Precision.HIGHEST)

    ins = [*dbias_list, onehot]
    return _pcall(body, name="band_bias_grad", ins=ins, in_specs=[_whole(t) for t in ins], dep=dep,
                  out_shape=jax.ShapeDtypeStruct((N_HEADS, N_BUCKETS), F32), compiler_params=_params(0))


KV_PAIRS = N_KV_HEADS // 2
PAIR_ROWS = 2 * Q_PER_KV * BLOCK
MASKED = float(np.finfo(np.float32).min) / 2


def _slot_cols(w):
    lead = w.shape[:-1]
    return w.reshape(*lead, KV_PAIRS, 2, Q_PER_KV, HEAD_DIM).swapaxes(-3, -2).reshape(*lead, N_HEADS * HEAD_DIM)


def _unslot_cols(w):
    lead = w.shape[:-1]
    return w.reshape(*lead, KV_PAIRS, Q_PER_KV, 2, HEAD_DIM).swapaxes(-3, -2).reshape(*lead, N_HEADS * HEAD_DIM)


def _slot_rows(blocks):
    n = blocks.shape[-1]
    return blocks.reshape(KV_PAIRS, 2, Q_PER_KV, HEAD_DIM, n).swapaxes(1, 2).reshape(blocks.shape)


def _unslot_rows(blocks):
    n = blocks.shape[-1]
    return blocks.reshape(KV_PAIRS, Q_PER_KV, 2, HEAD_DIM, n).swapaxes(1, 2).reshape(blocks.shape)


def _slot_bias(bias):
    qi = np.arange(BLOCK)[:, None]
    kj = np.arange(2 * BLOCK)[None, :]
    dist = qi + BLOCK - kj
    window = (dist >= 0) & (dist < BLOCK)
    b = bias.reshape(KV_PAIRS, 2, Q_PER_KV, BLOCK, 2 * BLOCK).swapaxes(1, 2).reshape(KV_PAIRS, PAIR_ROWS, 2 * BLOCK)
    tile = lambda mk: jnp.asarray(np.tile(mk, (2 * Q_PER_KV, 1)))[None]
    return jnp.stack([jnp.where(tile(window & (kj >= BLOCK)), b, MASKED), jnp.where(tile(window), b, MASKED)])


def _unslot_bias(db):
    return db.reshape(KV_PAIRS, Q_PER_KV, 2, BLOCK, 2 * BLOCK).swapaxes(1, 2).reshape(N_HEADS, -1)


def _pair_kv(kvc_ref, kvp_ref, kvp, dkv):
    lanes = slice(kvp * LANES, (kvp + 1) * LANES)
    vlanes = slice(dkv + kvp * LANES, dkv + (kvp + 1) * LANES)
    k2 = jnp.concatenate([kvp_ref[:, lanes], kvc_ref[:, lanes]], axis=0)
    v2 = jnp.concatenate([kvp_ref[:, vlanes], kvc_ref[:, vlanes]], axis=0)
    return k2, v2


def _head_operand(ref, grp, par, low, scale=None):
    xg = ref[:, grp * LANES:(grp + 1) * LANES]
    if scale is not None:
        xg = xg * scale
    zero = jnp.zeros_like(xg)
    return jnp.where(low, xg, zero) if par == 0 else jnp.where(low, zero, xg)


def _head_probs(qh, k2, bias_rows, sink):
    s = _dot(qh, k2, NT) + bias_rows
    m = jnp.maximum(jnp.max(s, axis=-1, keepdims=True), sink)
    e = jnp.exp(s - m)
    es = jnp.exp(sink - m)
    inv = 1.0 / (jnp.sum(e, axis=-1, keepdims=True) + es)
    return e * inv, es * inv


def _attn_specs(c):
    dq, dkv2 = c.DQ, 2 * c.DKV
    return [pl.BlockSpec((BLOCK, dq), lambda n: (n, 0)),
            pl.BlockSpec((BLOCK, dkv2), lambda n: (n, 0)),
            pl.BlockSpec((BLOCK, dkv2), lambda n: (jnp.maximum(n - 1, 0), 0))]


def _bias_spec():
    return pl.BlockSpec((None, KV_PAIRS, PAIR_ROWS, 2 * BLOCK), lambda n: (jnp.minimum(n, 1), 0, 0, 0))


def _low_lanes():
    return lax.broadcasted_iota(jnp.int32, (BLOCK, LANES), 1) < HEAD_DIM


def _attn_fwd(c, name, q, kv, bias, sinks, dep=None):
    S, dq = c.S, c.DQ

    def body(q_ref, kvc_ref, kvp_ref, bias_ref, sink_ref, o_ref):
        low = _low_lanes()
        for kvp in range(KV_PAIRS):
            k2, v2 = _pair_kv(kvc_ref, kvp_ref, kvp, c.DKV)
            for g in range(Q_PER_KV):
                grp = kvp * Q_PER_KV + g
                halves = []
                for par in range(2):
                    r = 2 * g + par
                    qh = _head_operand(q_ref, grp, par, low, scale=HEAD_DIM ** -0.5)
                    p, _ = _head_probs(qh, k2, bias_ref[kvp, r * BLOCK:(r + 1) * BLOCK, :],
                                       sink_ref[(2 * kvp + par) * Q_PER_KV + g])
                    halves.append(_dot(p.astype(BF16), v2, NN))
                o_ref[:, grp * LANES:(grp + 1) * LANES] = jnp.where(low, halves[0], halves[1]).astype(BF16)

    return _pcall(
        body, name=name, ins=[q, kv, kv, bias, sinks], dep=dep, grid=(c.nb,),
        in_specs=_attn_specs(c) + [_bias_spec(), SMEM],
        out_specs=pl.BlockSpec((BLOCK, dq), lambda n: (n, 0)),
        out_shape=jax.ShapeDtypeStruct((S, dq), BF16), compiler_params=_params(1))


def _final_loss(c, h, g, target, dep=None):
    S, D, tm = c.S, c.D, c.tm

    def body(h_ref, g_ref, t_ref, dh_ref, dg_ref, loss_ref):
        i = pl.program_id(0)
        h = h_ref[...]
        gg = g_ref[...]
        r = _rstd(h)
        xh = h * r
        err = xh * gg - t_ref[...]
        lp = jnp.sum(jnp.sum(err * err, axis=1, keepdims=True), axis=0, keepdims=True) * (0.5 / D)
        dx, dg = _rms_bwd(err * (1.0 / D), h, gg, 0.0)
        dh_ref[...] = dx

        @pl.when(i == 0)
        def _():
            dg_ref[...] = dg
            loss_ref[...] = jnp.broadcast_to(lp, loss_ref.shape)

        @pl.when(i > 0)
        def _():
            dg_ref[...] += dg
            loss_ref[...] += jnp.broadcast_to(lp, loss_ref.shape)

    row = pl.BlockSpec((tm, D), lambda i: (i, 0))
    return _pcall(
        body, name="final_loss", ins=[h, g, target], dep=dep, grid=(S // tm,),
        in_specs=[row, pl.BlockSpec((1, D), lambda i: (0, 0)), row],
        out_specs=[row, pl.BlockSpec((1, D), lambda i: (0, 0)), pl.BlockSpec((1, LANES), lambda i: (0, 0))],
        out_shape=[jax.ShapeDtypeStruct((S, D), F32), jax.ShapeDtypeStruct((1, D), F32),
                   jax.ShapeDtypeStruct((1, LANES), F32)],
        compiler_params=_params(1))


def _rms_bwd_epilogue(h_idx, g_idx, res_idx):
    def epilogue(dhn, in_refs, out_refs):
        dh, dg = _rms_bwd(dhn, in_refs[h_idx][...], in_refs[g_idx][...], in_refs[res_idx][...])
        out_refs[0][...] = dh
        i = pl.program_id(0)

        @pl.when(i == 0)
        def _():
            out_refs[1][...] = dg

        @pl.when(i > 0)
        def _():
            out_refs[1][...] += dg
    return epilogue


def _stream_outs(c):
    S, D, tm = c.S, c.D, c.tm
    return ([jax.ShapeDtypeStruct((S, D), F32), jax.ShapeDtypeStruct((1, D), F32)],
            [pl.BlockSpec((tm, D), lambda i, j, k: (i, 0)), pl.BlockSpec((1, D), lambda i, j, k: (0, 0))])


def _row_specs(c):
    tm, D = c.tm, c.D
    return [pl.BlockSpec((tm, D), lambda i, j, k: (i, 0)), pl.BlockSpec((1, D), lambda i, j, k: (0, 0)),
            pl.BlockSpec((tm, D), lambda i, j, k: (i, 0))]


def _bwd_rows_to_stream(c, name, dy_list, slab, blk_rows, blk_idx, n_in_cols, h, g, dres, dep=None):
    S, D, tm = c.S, c.D, c.tm
    nd = len(dy_list)

    def load_a(in_refs, out_refs):
        a = in_refs[0][...]
        for r in in_refs[1:nd]:
            a = a + r[...]
        return a.astype(BF16)

    shapes, specs = _stream_outs(c)
    return _mm(
        name, [*dy_list, slab, h, g, dres],
        [pl.BlockSpec((tm, n_in_cols), lambda i, j, k: (i, 0))] * nd
        + [pl.BlockSpec((NDEV, blk_rows, n_in_cols), lambda i, j, k: (0, blk_idx, 0))] + _row_specs(c),
        shapes, specs, grid=(S // tm, 1, 1), dims=NT, nk=1, acc_shape=None,
        load_a=load_a, load_b=_b_view(nd, NDEV * blk_rows), epilogue=_rms_bwd_epilogue(nd + 1, nd + 2, nd + 3),
        dep=dep)


def _bwd_cols_to_stream(c, name, dy, col, ci, h, g, dres, dep=None):
    S, D, cw, tm = c.S, c.D, c.cw, c.tm
    K = NDEV * cw
    shapes, specs = _stream_outs(c)
    return _mm(
        name, [dy, col, h, g, dres],
        [pl.BlockSpec((tm, K), lambda i, j, k: (i, 0)),
         _const((NDEV, None, cw, D), lambda i, j, k: (0, ci, 0, 0))] + _row_specs(c),
        shapes, specs, grid=(S // tm, 1, 1), dims=NN, nk=1, acc_shape=None,
        load_a=_bf(0), load_b=_b_view(1, K), epilogue=_rms_bwd_epilogue(2, 3, 4), dep=dep)


def _bwd_rows_data(c, name, dy, slab, blk_rows, blk_idx, dep=None):
    S, D, tm = c.S, c.D, c.tm
    K = NDEV * blk_rows

    def epilogue(acc, in_refs, out_refs):
        out_refs[0][...] = acc.astype(BF16)

    return _mm(
        name, [dy, slab],
        [pl.BlockSpec((tm, D), lambda i, j, k: (i, 0)),
         pl.BlockSpec((NDEV, blk_rows, D), lambda i, j, k: (0, blk_idx, 0))],
        [jax.ShapeDtypeStruct((S, K), BF16)], [pl.BlockSpec((tm, K), lambda i, j, k: (i, 0))],
        grid=(S // tm, 1, 1), dims=NT, nk=1, acc_shape=None,
        load_a=_bf(0), load_b=_b_view(1, K), epilogue=epilogue, dep=dep)[0]


def _wgrad_rows(c, name, a, b_list, n_a, n_b, dep=None):
    S, tm = c.S, c.tm
    nb_in = len(b_list)
    blk_rows = n_a // NDEV

    def load_b(in_refs):
        b = in_refs[1][...]
        for r in in_refs[2:1 + nb_in]:
            b = b + r[...]
        return b.astype(BF16)

    def epilogue(acc, in_refs, out_refs):
        out_refs[0][...] = acc.reshape(NDEV, blk_rows, n_b).astype(BF16)

    return _mm(
        name, [a, *b_list],
        [pl.BlockSpec((tm, n_a), lambda i, j, k: (k, 0))] + [pl.BlockSpec((tm, n_b), lambda i, j, k: (k, 0))] * nb_in,
        [jax.ShapeDtypeStruct((NDEV, blk_rows, n_b), BF16)],
        [pl.BlockSpec((NDEV, blk_rows, n_b), lambda i, j, k: (0, 0, 0))],
        grid=(1, 1, S // tm), dims=TN, nk=S // tm, acc_shape=(n_a, n_b),
        load_a=_bf(0), load_b=load_b, epilogue=epilogue, dep=dep)[0]


def _wgrad_cols(c, name, a, b, dep=None):
    S, D, cw = c.S, c.D, c.cw

    def body(a_ref, b_ref, out_ref):
        out_ref[...] = _dot(a_ref[...], b_ref[...], TN).astype(BF16)

    return _pcall(
        body, name=name, ins=[a, b], dep=dep, grid=(NDEV,),
        in_specs=[_const((S, D), lambda j: (0, 0)), pl.BlockSpec((S, cw), lambda j: (0, j))],
        out_specs=pl.BlockSpec((None, D, cw), lambda j: (j, 0, 0)),
        out_shape=jax.ShapeDtypeStruct((NDEV, D, cw), BF16), compiler_params=_params(1))


def _ffn_bwd_data(c, name, dh, p, col, ci, rows, h, g, dep=None):
    S, D, fw, tm = c.S, c.D, c.fw, c.tm
    F = NDEV * fw

    def body(dh_ref, p_ref, w1t_ref, w2_ref, h_ref, g_ref, da_ref, out_ref, dg_ref, dhb_ref):
        i = pl.program_id(0)
        dh = dh_ref[...]
        dhb = dh.astype(BF16)
        dhb_ref[...] = dhb
        for j in range(NDEV):
            cols = slice(j * fw, (j + 1) * fw)
            da_ref[:, cols] = (_dot(dhb, w2_ref[j], NT) * (2.0 * p_ref[:, cols].astype(F32))).astype(BF16)
        dx, dg = _rms_bwd(_dot(da_ref[...], w1t_ref[...].reshape(F, D), NN), h_ref[...], g_ref[...], dh)
        out_ref[...] = dx

        @pl.when(i == 0)
        def _():
            dg_ref[...] = dg

        @pl.when(i > 0)
        def _():
            dg_ref[...] += dg

    row = pl.BlockSpec((tm, D), lambda i: (i, 0))
    wide = pl.BlockSpec((tm, F), lambda i: (i, 0))
    return _pcall(
        body, name=name, ins=[dh, p, col, rows, h, g], dep=dep, grid=(S // tm,),
        in_specs=[row, wide, _const((NDEV, None, fw, D), lambda i: (0, ci, 0, 0)),
                  _const((NDEV, c.fr, D), lambda i: (0, 0, 0)),
                  row, pl.BlockSpec((1, D), lambda i: (0, 0))],
        out_specs=[wide, row, pl.BlockSpec((1, D), lambda i: (0, 0)), row],
        out_shape=[jax.ShapeDtypeStruct((S, F), BF16), jax.ShapeDtypeStruct((S, D), F32),
                   jax.ShapeDtypeStruct((1, D), F32), jax.ShapeDtypeStruct((S, D), BF16)],
        compiler_params=_params(1))


def _ffn_bwd_w(c, name, hn, da, p, dhb, dep=None):
    S, D, fw = c.S, c.D, c.fw

    def body(hn_ref, da_ref, p_ref, dhb_ref, dw1_ref, dw2_ref):
        dw1_ref[...] = _dot(hn_ref[...], da_ref[...], TN).astype(BF16)
        pf = p_ref[...].astype(F32)
        dw2_ref[...] = _dot((pf * pf).astype(BF16), dhb_ref[...], TN).astype(BF16)

    panel = pl.BlockSpec((S, fw), lambda j: (0, j))
    return _pcall(
        body, name=name, ins=[hn, da, p, dhb], dep=dep, grid=(NDEV,),
        in_specs=[_const((S, D), lambda j: (0, 0)), panel, panel, _const((S, D), lambda j: (0, 0))],
        out_specs=[pl.BlockSpec((None, D, fw), lambda j: (j, 0, 0)), pl.BlockSpec((None, c.fr, D), lambda j: (j, 0, 0))],
        out_shape=[jax.ShapeDtypeStruct((NDEV, D, fw), BF16), jax.ShapeDtypeStruct((NDEV, c.fr, D), BF16)],
        compiler_params=_params(1))


def _attn_bwd(c, name, q, kv, do, bias, sinks, dep=None):
    S, dq, dkv = c.S, c.DQ, c.DKV
    nb = c.nb
    scale = HEAD_DIM ** -0.5

    def body(q_ref, kvc_ref, kvp_ref, do_ref, bias_ref, sink_ref, dq_ref, dkv_ref, dbias_ref, dsink_ref, dsink_acc,
             ds_sc, p_sc, qm_sc, dom_sc):
        n = pl.program_id(0)

        @pl.when(n == 0)
        def _():
            dkv_ref[...] = jnp.zeros_like(dkv_ref)
            dbias_ref[...] = jnp.zeros_like(dbias_ref)
            dsink_acc[...] = jnp.zeros_like(dsink_acc)

        low = _low_lanes()
        rows_c = pl.ds(pl.multiple_of(n * BLOCK, BLOCK), BLOCK)
        rows_p = pl.ds(pl.multiple_of(jnp.maximum(n - 1, 0) * BLOCK, BLOCK), BLOCK)
        for kvp in range(KV_PAIRS):
            k2, v2 = _pair_kv(kvc_ref, kvp_ref, kvp, dkv)
            for g in range(Q_PER_KV):
                grp = kvp * Q_PER_KV + g
                halves = []
                for par in range(2):
                    rows = slice((2 * g + par) * BLOCK, (2 * g + par + 1) * BLOCK)
                    qh = _head_operand(q_ref, grp, par, low, scale=scale)
                    doh = _head_operand(do_ref, grp, par, low)
                    p, ps = _head_probs(qh, k2, bias_ref[kvp, rows, :], sink_ref[(2 * kvp + par) * Q_PER_KV + g])
                    dp = _dot(doh, v2, NT)
                    delta = jnp.sum(p * dp, axis=-1, keepdims=True)
                    ds = p * (dp - delta)
                    dbias_ref[kvp, rows, :] += ds
                    dsink_acc[rows, kvp:kvp + 1] += -(ps * delta)
                    ds16 = ds.astype(BF16)
                    halves.append(_dot(ds16, k2, NN) * scale)
                    ds_sc[rows, :] = ds16
                    p_sc[rows, :] = p.astype(BF16)
                    qm_sc[rows, :] = qh
                    dom_sc[rows, :] = doh
                dq_ref[:, grp * LANES:(grp + 1) * LANES] = jnp.where(low, halves[0], halves[1]).astype(BF16)
            dk2 = _dot(ds_sc[...], qm_sc[...], TN)
            dv2 = _dot(p_sc[...], dom_sc[...], TN)
            lanes = slice(kvp * LANES, (kvp + 1) * LANES)
            vlanes = slice(dkv + kvp * LANES, dkv + (kvp + 1) * LANES)
            dkv_ref[rows_p, lanes] += dk2[:BLOCK]
            dkv_ref[rows_c, lanes] += dk2[BLOCK:]
            dkv_ref[rows_p, vlanes] += dv2[:BLOCK]
            dkv_ref[rows_c, vlanes] += dv2[BLOCK:]

        @pl.when(n == nb - 1)
        def _():
            dsink_ref[...] = jnp.sum(dsink_acc[...].reshape(2 * Q_PER_KV, BLOCK, KV_PAIRS), axis=1)

    return _pcall(
        body, name=name, ins=[q, kv, kv, do, bias, sinks], dep=dep, grid=(nb,),
        in_specs=_attn_specs(c) + [pl.BlockSpec((BLOCK, dq), lambda n: (n, 0)), _bias_spec(), SMEM],
        out_specs=[pl.BlockSpec((BLOCK, dq), lambda n: (n, 0)), pl.BlockSpec((S, 2 * dkv), lambda n: (0, 0)),
                   pl.BlockSpec((KV_PAIRS, PAIR_ROWS, 2 * BLOCK), lambda n: (0, 0, 0)),
                   pl.BlockSpec((2 * Q_PER_KV, KV_PAIRS), lambda n: (0, 0))],
        out_shape=[jax.ShapeDtypeStruct((S, dq), BF16), jax.ShapeDtypeStruct((S, 2 * dkv), F32),
                   jax.ShapeDtypeStruct((KV_PAIRS, PAIR_ROWS, 2 * BLOCK), F32),
                   jax.ShapeDtypeStruct((2 * Q_PER_KV, KV_PAIRS), F32)],
        scratch_shapes=[pltpu.VMEM((PAIR_ROWS, KV_PAIRS), F32), pltpu.VMEM((PAIR_ROWS, 2 * BLOCK), BF16),
                        pltpu.VMEM((PAIR_ROWS, 2 * BLOCK), BF16), pltpu.VMEM((PAIR_ROWS, LANES), BF16),
                        pltpu.VMEM((PAIR_ROWS, LANES), BF16)],
        compiler_params=_params(1))


def _sgu_bwd(c, name, a, z, dgated, ln_g, wc, wc_t, b_t, dep=None):
    S, AW, gd, tm = c.S, c.AW, c.gd, c.tms

    def body(a_ref, z_ref, dg_ref, lng_ref, wc_ref, wct_ref, bt_ref, dz_ref, dws_ref, dbt_ref, dlng_ref, dvn_ref):
        i = pl.program_id(0)

        @pl.when(i == 0)
        def _():
            dws_ref[...] = jnp.zeros_like(dws_ref)
            dbt_ref[...] = jnp.zeros_like(dbt_ref)
            dlng_ref[...] = jnp.zeros_like(dlng_ref)

        lng = lng_ref[...]
        va = a_ref[:, AW:].astype(F32)
        xc = va - jnp.mean(va, axis=-1, keepdims=True)
        rstd = lax.rsqrt(jnp.mean(xc * xc, axis=-1, keepdims=True) + EPS)
        xh = xc * rstd
        vn = (xh * lng).astype(BF16)
        causal = _sgu_masks()
        for ch in range(tm // CHUNK):
            rows = slice(ch * CHUNK, (ch + 1) * CHUNK)
            for g in range(A_GROUPS):
                cols = slice(g * gd, (g + 1) * gd)
                blk = vn[rows, cols]
                mixed = _dot(wc_ref[g], blk, NN) + bt_ref[:, g:g + 1]
                dgb = dg_ref[rows, cols].astype(F32)
                dm = dgb * a_ref[rows, cols].astype(F32)
                dbt_ref[:, g:g + 1] += jnp.sum(dm, axis=1, keepdims=True)
                dm16 = dm.astype(BF16)
                dws_ref[g] += jnp.where(causal, _dot(dm16, blk, NT), 0.0)
                dvn_ref[rows, cols] = _dot(wct_ref[g], dm16, NN)
                dz_ref[rows, cols] = (dgb * mixed * _gelu_grad(z_ref[rows, cols].astype(F32))).astype(BF16)
        dvn = dvn_ref[...]
        dlng_ref[...] += jnp.sum(dvn * xh, axis=0, keepdims=True)
        dxh = dvn * lng
        dva = rstd * (dxh - jnp.mean(dxh, axis=-1, keepdims=True) - xh * jnp.mean(dxh * xh, axis=-1, keepdims=True))
        dz_ref[:, AW:] = (dva * _gelu_grad(z_ref[:, AW:].astype(F32))).astype(BF16)

    wide = pl.BlockSpec((tm, 2 * AW), lambda i: (i, 0))
    wsp = pl.BlockSpec((A_GROUPS, CHUNK, CHUNK), lambda i: (0, 0, 0))
    btsp = pl.BlockSpec((CHUNK, A_GROUPS), lambda i: (0, 0))
    return _pcall(
        body, name=name, ins=[a, z, dgated, ln_g, wc, wc_t, b_t], dep=dep, grid=(S // tm,),
        in_specs=[wide, wide, pl.BlockSpec((tm, AW), lambda i: (i, 0)), pl.BlockSpec((1, AW), lambda i: (0, 0)),
                  wsp, wsp, btsp],
        out_specs=[wide, wsp, btsp, pl.BlockSpec((1, AW), lambda i: (0, 0))],
        out_shape=[jax.ShapeDtypeStruct((S, 2 * AW), BF16), jax.ShapeDtypeStruct((A_GROUPS, CHUNK, CHUNK), F32),
                   jax.ShapeDtypeStruct((CHUNK, A_GROUPS), F32), jax.ShapeDtypeStruct((1, AW), F32)],
        scratch_shapes=[pltpu.VMEM((tm, AW), F32)], compiler_params=_params(1))


def _adamw(name, parts, part_block, part_index, w, m, v, tr, row_off=0, n_rows=None, prev=None, dep=None):
    R, C = w.shape
    n_rows = R if n_rows is None else n_rows
    assert n_rows % tr == 0 and row_off % tr == 0
    bc1 = 1.0 - ADAM_B1 ** ADAM_STEP
    bc2 = 1.0 - ADAM_B2 ** ADAM_STEP

    def body(p_ref, w_ref, m_ref, v_ref, *rest):
        g_ref, d_ref, nm_ref, nv_ref = rest[-4:]
        g = p_ref[0].astype(F32)
        for s in range(1, part_block[0]):
            g = g + p_ref[s].astype(F32)
        nm = ADAM_B1 * m_ref[...] + (1.0 - ADAM_B1) * g
        nv = ADAM_B2 * v_ref[...] + (1.0 - ADAM_B2) * (g * g)
        g_ref[...] = g
        nm_ref[...] = nm
        nv_ref[...] = nv
        d_ref[...] = -ADAM_LR * ((nm / bc1) / (jnp.sqrt(nv / bc2) + ADAM_EPS) + ADAM_WD * w_ref[...])

    ob = row_off // tr
    row = pl.BlockSpec((tr, C), lambda i: (ob + i, 0))
    out = jax.ShapeDtypeStruct((R, C), F32)
    chained = prev is not None
    return _pcall(
        body, name=name, ins=[parts, w, m, v] + (list(prev) if chained else []), dep=dep, grid=(n_rows // tr,),
        in_specs=[pl.BlockSpec(part_block, part_index), row, row, row] + ([ANY] * 4 if chained else []),
        out_specs=[row, row, row, row], out_shape=[out, out, out, out],
        input_output_aliases={4 + t: t for t in range(4)} if chained else {}, compiler_params=_params(1))


def _sum_parts(name, parts, dep=None):
    def body(p_ref, out_ref):
        g = p_ref[0]
        for s in range(1, parts.shape[0]):
            g = g + p_ref[s]
        out_ref[...] = g

    return _pcall(body, name=name, ins=[parts], in_specs=[_whole(parts)], dep=dep,
                  out_shape=jax.ShapeDtypeStruct(parts.shape[1:], F32), compiler_params=_params(0))


def _place():
    return lax.axis_index("x"), lax.axis_index("y"), lax.axis_index("c")


def _slot(px, py, pc):
    return 4 * px + 2 * py + pc


def _peer(k, x, y, c):
    return x ^ ((k >> 2) & 1), y ^ ((k >> 1) & 1), c ^ (k & 1)


SEND_PEERS = {"exchange": tuple(range(1, NDEV)), "gather": (1, 2, 4, 6), "forward": (2, 4, 6),
              "broadcast": tuple(range(1, NDEV))}


def _send_copies(mode, src_refs, land_refs, send_sems, recv_sems):
    x, y, c = _place()
    me = _slot(x, y, c)
    peers = SEND_PEERS[mode]
    copies = []
    for i, k in enumerate(peers):
        peer = _peer(k, x, y, c)
        for a, land in enumerate(land_refs):
            if mode == "exchange":
                src, dst, to = src_refs[a].at[_slot(*peer)], land.at[me], peer
            elif mode in ("gather", "broadcast"):
                src, dst, to = src_refs[a], land.at[me], peer
            else:
                src = dst = land.at[_slot(*peer)]
                to = (x, y, 1 - c)
            s = a * len(peers) + i
            copies.append(pltpu.make_async_remote_copy(src_ref=src, dst_ref=dst, send_sem=send_sems.at[s],
                                                       recv_sem=recv_sems.at[s], device_id=to, device_id_type=MESH))
    return copies


def _send_start(name, srcs, lands, mode, dep=None):
    n_src, n = len(srcs), len(lands)
    n_sem = n * len(SEND_PEERS[mode])

    def body(*refs):
        src_refs, land_refs = refs[:n_src], refs[n_src:n_src + n]
        send_sems, recv_sems = refs[n_src + n], refs[n_src + n + 1]
        token = refs[-1]
        for cp in _send_copies(mode, src_refs, land_refs, send_sems, recv_sems):
            cp.start()
        token[...] = jnp.zeros_like(token)

    thru = [pltpu.HBM(t.shape, t.dtype) for t in [*srcs, *lands]]
    if any(dep is t for t in [*srcs, *lands]):
        dep = None
    out = _pcall(
        body, name=name, ins=[pltpu.with_memory_space_constraint(t, pltpu.HBM) for t in [*srcs, *lands]],
        in_specs=[HBM] * (n_src + n), dep=dep,
        out_shape=(pltpu.SemaphoreType.DMA((n_sem,)), pltpu.SemaphoreType.DMA((n_sem,)), *thru,
                   jax.ShapeDtypeStruct((8, LANES), F32)),
        out_specs=(SEM, SEM, *[HBM] * (n_src + n), pl.BlockSpec(memory_space=pltpu.VMEM)),
        input_output_aliases={i: 2 + i for i in range(n_src + n)},
        compiler_params=pltpu.CompilerParams(has_side_effects=pltpu.SideEffectType.DATAFLOW_SIDE_EFFECTING))
    return out[-1], out[0], out[1], list(out[2:2 + n_src]), list(out[2 + n_src:2 + n_src + n])


def _send_wait(name, started, mode, dep=None):
    _, send_sems, recv_sems, srcs, lands = started
    n_src, n = len(srcs), len(lands)

    def body(*refs):
        src_refs, land_refs = refs[:n_src], refs[n_src:n_src + n]
        ssem, rsem = refs[n_src + n], refs[n_src + n + 1]
        for cp in _send_copies(mode, src_refs, land_refs, ssem, rsem):
            cp.wait_send()
            cp.wait_recv()

    thru = [pltpu.HBM(t.shape, t.dtype) for t in [*srcs, *lands]]
    out = _pcall(
        body, name=name, ins=[*srcs, *lands, send_sems, recv_sems], in_specs=[HBM] * (n_src + n) + [SEM, SEM], dep=dep,
        out_shape=tuple(thru), out_specs=tuple([HBM] * (n_src + n)),
        input_output_aliases={i: i for i in range(n_src + n)},
        compiler_params=pltpu.CompilerParams(has_side_effects=pltpu.SideEffectType.DATAFLOW_SIDE_EFFECTING))
    return list(out[n_src:])


def _landing(block, me):
    zone = lax.empty((NDEV, *block.shape), block.dtype)
    return lax.dynamic_update_slice(zone, block[None], (me,) + (0,) * block.ndim)


def _rows128(t):
    flat = t.reshape(-1)
    n = flat.shape[0]
    rows = -(-n // (8 * LANES)) * 8
    return jnp.pad(flat, (0, rows * LANES - n)).reshape(rows, LANES)


def kernel(x, mix_norm_g, ffn_norm_g, a_w_in, a_ln_g, a_w_spatial, a_b_spatial, a_w_out, kv_norm_g, w_k, w_v, b_w_q, b_sinks, b_w_o, rel_bias, ffn_w1, ffn_w2, final_norm_g, loss_target, m_mix_norm_g, m_ffn_norm_g, m_a_w_in, m_a_ln_g, m_a_w_spatial, m_a_b_spatial, m_a_w_out, m_kv_norm_g, m_w_k, m_w_v, m_b_w_q, m_b_sinks, m_b_w_o, m_rel_bias, m_ffn_w1, m_ffn_w2, m_final_norm_g, v_mix_norm_g, v_ffn_norm_g, v_a_w_in, v_a_ln_g, v_a_w_spatial, v_a_b_spatial, v_a_w_out, v_kv_norm_g, v_w_k, v_w_v, v_b_w_q, v_b_sinks, v_b_w_o, v_rel_bias, v_ffn_w1, v_ffn_w2, v_final_norm_g):
    c = _config(x, a_w_in, a_w_out, w_k, b_w_q, b_w_o, ffn_w1, ffn_w2)
    S, D, LA, LB, LF = c.S, c.D, c.LA, c.LB, c.LF
    weights = dict(mix_norm_g=mix_norm_g, ffn_norm_g=ffn_norm_g, a_w_in=a_w_in, a_ln_g=a_ln_g, a_w_spatial=a_w_spatial,
                   a_b_spatial=a_b_spatial, a_w_out=a_w_out, kv_norm_g=kv_norm_g, w_k=w_k, w_v=w_v, b_w_q=b_w_q,
                   b_sinks=b_sinks, b_w_o=b_w_o, rel_bias=rel_bias, ffn_w1=ffn_w1, ffn_w2=ffn_w2,
                   final_norm_g=final_norm_g)
    m_in = dict(mix_norm_g=m_mix_norm_g, ffn_norm_g=m_ffn_norm_g, a_w_in=m_a_w_in, a_ln_g=m_a_ln_g,
                a_w_spatial=m_a_w_spatial, a_b_spatial=m_a_b_spatial, a_w_out=m_a_w_out, kv_norm_g=m_kv_norm_g,
                w_k=m_w_k, w_v=m_w_v, b_w_q=m_b_w_q, b_sinks=m_b_sinks, b_w_o=m_b_w_o, rel_bias=m_rel_bias,
                ffn_w1=m_ffn_w1, ffn_w2=m_ffn_w2, final_norm_g=m_final_norm_g)
    v_in = dict(mix_norm_g=v_mix_norm_g, ffn_norm_g=v_ffn_norm_g, a_w_in=v_a_w_in, a_ln_g=v_a_ln_g,
                a_w_spatial=v_a_w_spatial, a_b_spatial=v_a_b_spatial, a_w_out=v_a_w_out, kv_norm_g=v_kv_norm_g,
                w_k=v_w_k, w_v=v_w_v, b_w_q=v_b_w_q, b_sinks=v_b_sinks, b_w_o=v_b_w_o, rel_bias=v_rel_bias,
                ffn_w1=v_ffn_w1, ffn_w2=v_ffn_w2, final_norm_g=v_final_norm_g)
    names = list(weights)
    seq = _Seq()
    me = _slot(*_place())
    bf = lambda t: t.astype(BF16)

    def rows_of(l):
        second = a_w_out[l] if l < LA else b_w_o[l - LA]
        return bf(jnp.concatenate([ffn_w2[l], second], axis=0))

    tr = lambda t: bf(jnp.swapaxes(t, -1, -2))
    groups = [[tr(a_w_in[0])[None], a_ln_g],
              [tr(ffn_w1[0])[None], rows_of(0)]]
    for l in range(1, LA):
        groups.append([tr(jnp.stack([a_w_in[l], ffn_w1[l]])), rows_of(l)])
    for l in range(LB):
        extra = [bf(jnp.concatenate([w_k, w_v], axis=1))] if l == 0 else []
        groups.append(extra + [bf(b_w_q[l]), tr(ffn_w1[LA + l])[None], rows_of(LA + l)])
    started = [seq(_send_start, f"weights_start{i}", grp, [_landing(t, me) for t in grp], "gather")
               for i, grp in enumerate(groups)]
    forwarding = {}

    def forward(i):
        lands = seq(_send_wait, f"weights_wait{i}", started[i], "gather")
        forwarding[i] = seq(_send_start, f"weights_forward{i}", [], lands, "forward")

    def arrive(i):
        if i not in forwarding:
            forward(i)
        return seq(_send_wait, f"weights_arrive{i}", forwarding[i], "forward")

    causal = jnp.tril(jnp.ones((CHUNK, CHUNK), bool))
    wsp = jnp.where(causal[None, None], a_w_spatial, 0.0)
    wsp16 = wsp.astype(BF16)
    wsp16_t = jnp.swapaxes(wsp, -1, -2).astype(BF16)
    bsp_t = jnp.swapaxes(a_b_spatial, -1, -2)
    mix_g = mix_norm_g.reshape(-1, 1, D)
    ffn_g = ffn_norm_g.reshape(-1, 1, D)
    kv_g = kv_norm_g.reshape(1, D)
    fin_g = final_norm_g.reshape(1, D)
    onehot = _bucket_onehot()
    bias = _slot_bias(seq(_band_bias, rel_bias.T, onehot).reshape(N_HEADS, BLOCK, 2 * BLOCK))

    h = x.reshape(S, D)
    sav_a, sav_b, wts_a, wts_b = [], [], [], []
    for l in range(LA):
        if l == 0:
            w_in, lng_all = arrive(0)
            ln_g_full = jnp.transpose(lng_all, (1, 0, 2)).reshape(LA, 1, c.AW)
            in_i, w1_i = 0, 0
        else:
            w_in, rows = arrive(l + 1)
            w1, in_i, w1_i = w_in, 0, 1
        z, a, hn = seq(_a_in_fwd, c, f"a_in_fwd{l}", h, mix_g[l], w_in, in_i)
        if l == 0:
            forward(1)
        gated = seq(_sgu_fwd, c, f"sgu_fwd{l}", a, ln_g_full[l], wsp16[l], bsp_t[l])
        if l == 0:
            w1, rows = arrive(1)
        h1 = seq(_mm_res, c, f"a_out_fwd{l}", gated, rows, c.ar, c.fr // c.ar, h)
        if l == LA - 1:
            forward(LA + 1)
        p, h2, hnf = seq(_ffn_fwd, c, f"ffn_fwd{l}", h1, ffn_g[l], w1, w1_i, rows)
        sav_a.append((h, z, a, hn, gated, h1, p, hnf))
        wts_a.append((w_in, in_i, w1, w1_i, rows))
        h = h2
    h_kv = h
    for l in range(LB):
        got = arrive(LA + 1 + l)
        if l == 0:
            wkv, got = got[0], got[1:]
            kv, hkv = seq(_rms_mm_rows, c, "kv_fwd", h, kv_g, wkv, c.kr, 0, 2 * c.DKV)
        wq, w1, rows = got
        wq = _slot_cols(wq)
        wo = _slot_rows(rows[:, c.fr:, :])
        q, hn = seq(_rms_mm_rows, c, f"q_fwd{l}", h, mix_g[LA + l], wq, c.qr, 0, c.DQ)
        o = seq(_attn_fwd, c, f"attn_fwd{l}", q, kv, bias, b_sinks[l])
        h1 = seq(_mm_res, c, f"o_fwd{l}", o, wo, c.orr, 0, h)
        if l + 1 < LB:
            forward(LA + 2 + l)
        p, h2, hnf = seq(_ffn_fwd, c, f"ffn_fwd{LA + l}", h1, ffn_g[LA + l], w1, 0, rows)
        sav_b.append((h, q, hn, o, h1, p, hnf))
        wts_b.append((wq, wo, w1, rows))
        h = h2
    dh, d_fin_g, loss_row = seq(_final_loss, c, h, fin_g, loss_target.reshape(S, D))
    loss = lax.psum(loss_row[0, 0], MESH_AXES)

    results = {}
    in_flight = []

    def update(k, parts, layer, col_blk=0):
        w = weights[k]
        rows_l, ncols = (w.shape[-2], w.shape[-1]) if w.ndim == 3 else w.shape
        flat = lambda t: t.reshape(-1, ncols)
        tr = min(256, rows_l)
        results[k] = seq(_adamw, f"adamw_{k}{layer}", parts, (NDEV, tr, ncols), lambda i: (0, i, col_blk),
                         flat(w), flat(m_in[k]), flat(v_in[k]), tr, row_off=layer * rows_l, n_rows=rows_l,
                         prev=results.get(k))

    def land(tag, entry):
        lands = seq(_send_wait, f"grads_wait_{tag}", entry[1], "exchange")
        for keys, parts in zip(entry[0], lands):
            for k, layer, col_blk in keys:
                update(k, parts, layer, col_blk)

    def send(tag, items):
        slabs = [t for _, t in items]
        own = [_landing(lax.dynamic_index_in_dim(t, me, 0, keepdims=False), me) for t in slabs]
        st = seq(_send_start, f"grads_start_{tag}", slabs, own, "exchange")
        in_flight.append((tag, ([k for k, _ in items], st)))
        while len(in_flight) > EXCHANGE_LAG:
            land(*in_flight.pop(0))

    d_mix_g, d_ffn_g = [None] * LF, [None] * LF
    dkv_list, dbias_list, dsink_list = [], [], [None] * LB

    def ffn_bwd(lf, dh, h1, p, hnf, w1, w1_i, rows):
        da, dh1, d_ffn_g[lf], dhb = seq(_ffn_bwd_data, c, f"ffn_bwd_data{lf}", dh, p, w1, w1_i, rows, h1, ffn_g[lf])
        dw1, dw2 = seq(_ffn_bwd_w, c, f"ffn_bwd_w{lf}", hnf, da, p, dhb)
        send(f"ffn{lf}", [([("ffn_w1", lf, 0)], dw1), ([("ffn_w2", lf, 0)], dw2)])
        return dh1

    for l in reversed(range(LB)):
        h0, q, hn, o, h1, p, hnf = sav_b[l]
        wq, wo, w1, rows = wts_b[l]
        dh1 = ffn_bwd(LA + l, dh, h1, p, hnf, w1, 0, rows)
        do = seq(_bwd_rows_data, c, f"o_bwd_data{l}", dh1, wo, c.orr, 0)
        dwo = _unslot_rows(seq(_wgrad_rows, c, f"o_bwd_w{l}", o, [dh1], c.DQ, D))
        dq, dkv, dbias, dsink = seq(_attn_bwd, c, f"attn_bwd{l}", q, kv, do, bias, b_sinks[l])
        dsink_list[l] = dsink.reshape(Q_PER_KV, 2, KV_PAIRS).transpose(2, 1, 0).reshape(1, N_HEADS)
        dkv_list.append(dkv)
        dbias_list.append(_unslot_bias(dbias))
        dwq = _unslot_cols(seq(_wgrad_rows, c, f"q_bwd_w{l}", hn, [dq], D, c.DQ))
        send(f"attn{l}", [([("b_w_o", l, 0)], dwo), ([("b_w_q", l, 0)], dwq)])
        dh, d_mix_g[LA + l] = seq(_bwd_rows_to_stream, c, f"q_bwd_data{l}", [dq], wq, c.qr, 0, c.DQ, h0,
                                  mix_g[LA + l], dh1)
    dwkv = seq(_wgrad_rows, c, "kv_bwd_w", hkv, dkv_list, D, 2 * c.DKV)
    send("kv", [([("w_k", 0, 0), ("w_v", 0, 1)], dwkv)])
    dh, d_kv_g = seq(_bwd_rows_to_stream, c, "kv_bwd_data", dkv_list, wkv, c.kr, 0, 2 * c.DKV, h_kv, kv_g, dh)
    d_rel_t = seq(_band_bias_grad, dbias_list, onehot)
    d_wsp, d_bsp, d_lng = [None] * LA, [None] * LA, [None] * LA
    for l in reversed(range(LA)):
        h0, z, a, hn, gated, h1, p, hnf = sav_a[l]
        w_in, in_i, w1, w1_i, rows = wts_a[l]
        dh1 = ffn_bwd(l, dh, h1, p, hnf, w1, w1_i, rows)
        dgated = seq(_bwd_rows_data, c, f"a_out_bwd_data{l}", dh1, rows, c.ar, c.fr // c.ar)
        dwout = seq(_wgrad_rows, c, f"a_out_bwd_w{l}", gated, [dh1], c.AW, D)
        send(f"a_out{l}", [([("a_w_out", l, 0)], dwout)])
        dz, d_wsp[l], dbt, d_lng[l] = seq(_sgu_bwd, c, f"sgu_bwd{l}", a, z, dgated, ln_g_full[l], wsp16[l],
                                          wsp16_t[l], bsp_t[l])
        d_bsp[l] = dbt.T
        dwin = seq(_wgrad_cols, c, f"a_in_bwd_w{l}", hn, dz)
        send(f"a_in{l}", [([("a_w_in", l, 0)], dwin)])
        dh, d_mix_g[l] = seq(_bwd_cols_to_stream, c, f"a_in_bwd_data{l}", dz, w_in, in_i, h0, mix_g[l], dh1)
    grad_x = dh.reshape(1, S, D)

    small = {
        "mix_norm_g": jnp.concatenate(d_mix_g, axis=0), "ffn_norm_g": jnp.concatenate(d_ffn_g, axis=0),
        "a_w_spatial": jnp.stack(d_wsp), "a_b_spatial": jnp.stack(d_bsp), "kv_norm_g": d_kv_g,
        "b_sinks": jnp.concatenate(dsink_list, axis=0), "rel_bias": d_rel_t.T, "final_norm_g": d_fin_g,
    }
    small_names = list(small)
    packs = [_rows128(small[k]) for k in small_names] + [_rows128(jnp.concatenate(d_lng, axis=0))]
    offs = np.cumsum([0] + [p.shape[0] for p in packs])
    tail_rows = int(-offs[-1] % (8 * NDEV)) + packs[-1].shape[0]
    Rs = int(offs[-2]) + tail_rows
    packed = jnp.concatenate(packs + [jnp.zeros((Rs - int(offs[-1]), LANES), F32)], axis=0)
    slab = packed.reshape(NDEV, Rs // NDEV, LANES)
    st = seq(_send_start, "small_grads_start", [slab],
             [_landing(lax.dynamic_index_in_dim(slab, me, 0, keepdims=False), me)], "exchange")
    while in_flight:
        land(*in_flight.pop(0))
    (parts,) = seq(_send_wait, "small_grads_wait", st, "exchange")
    mine = seq(_sum_parts, "small_grads_sum", parts)
    st = seq(_send_start, "small_sums_start", [mine], [_landing(mine, me)], "broadcast")
    (sums,) = seq(_send_wait, "small_sums_wait", st, "broadcast")
    small_all = sums.reshape(1, Rs, LANES)

    grads, deltas, new_m, new_v = {}, {}, {}, {}

    def put(k, outs, shape):
        grads[k], deltas[k], new_m[k], new_v[k] = (t.reshape(shape) for t in outs)

    def pack_state(d):
        return jnp.concatenate([_rows128(d[k]) for k in small_names] + [jnp.zeros((tail_rows, LANES), F32)], axis=0)

    outs = seq(_adamw, "adamw_small", small_all, (1, Rs, LANES), lambda i: (0, 0, 0),
               pack_state(weights), pack_state(m_in), pack_state(v_in), Rs)
    for n_, k in enumerate(small_names):
        shape = weights[k].shape
        size = int(np.prod(shape))
        put(k, [t[int(offs[n_]):int(offs[n_ + 1])].reshape(-1)[:size] for t in outs], shape)
    lng_sum = outs[0][int(offs[-2]):int(offs[-1])].reshape(-1)[:LA * c.AW].reshape(LA, c.AW)
    lng_mine = lax.dynamic_slice_in_dim(lng_sum, me * c.ar, c.ar, axis=1)
    lng_parts = jnp.concatenate([lng_mine[None], jnp.zeros((NDEV - 1, LA, c.ar), F32)], axis=0)
    put("a_ln_g", seq(_adamw, "adamw_ln_g", lng_parts, (NDEV, LA, c.ar), lambda i: (0, 0, 0),
                      a_ln_g, m_in["a_ln_g"], v_in["a_ln_g"], LA), a_ln_g.shape)
    for k in ("a_w_in", "ffn_w1", "ffn_w2", "a_w_out", "b_w_o", "b_w_q", "w_k", "w_v"):
        put(k, results[k], weights[k].shape)

    return (loss, grad_x, *[grads[k] for k in names], *[deltas[k] for k in names],
            *[new_m[k] for k in names], *[new_v[k] for k in names])
```

```python
import numpy as np
import math
import jax
import jax.numpy as jnp
from jax import lax
from jax.experimental import pallas as pl
from jax.experimental.pallas import tpu as pltpu

F32 = jnp.float32
BF16 = jnp.bfloat16

NDEV = 8
EPS = 1e-6
CHUNK = 128
A_GROUPS = 8
N_HEADS = 16
N_KV_HEADS = 4
Q_PER_KV = N_HEADS // N_KV_HEADS
HEAD_DIM = 64
BLOCK = 128
N_BUCKETS = 32
MAX_DISTANCE = 128
ADAM_LR, ADAM_B1, ADAM_B2, ADAM_EPS, ADAM_WD, ADAM_STEP = 0.001, 0.9, 0.999, 1e-08, 0.01, 10
LANES = 128
VMEM_LIMIT = 56 * 1024 * 1024
INV_SQRT2 = 0.7071067811865476
INV_SQRT_2PI = 0.3989422804014327
MESH_AXES = ("x", "y", "c")
EXCHANGE_LAG = 2

HBM = pl.BlockSpec(memory_space=pltpu.HBM)
SMEM = pl.BlockSpec(memory_space=pltpu.SMEM)
ANY = pl.BlockSpec(memory_space=pl.ANY)
SEM = pl.BlockSpec(memory_space=pltpu.SEMAPHORE)
MESH = pl.DeviceIdType.MESH


def _params(n_grid):
    return pltpu.CompilerParams(dimension_semantics=("arbitrary",) * n_grid, vmem_limit_bytes=VMEM_LIMIT)


def _const(block, index_map):
    return pl.BlockSpec(block, index_map, pipeline_mode=pl.Buffered(1))


def _pcall(body, *, ins, in_specs, dep=None, **kw):
    n_in = len(ins)
    if dep is None or any(dep is t for t in ins):
        return pl.pallas_call(body, in_specs=list(in_specs), **kw)(*ins)

    def with_dep(*refs):
        body(*refs[:n_in], *refs[n_in + 1:])

    return pl.pallas_call(with_dep, in_specs=[*in_specs, ANY], **kw)(*ins, dep)


class _Seq:
    def __init__(self):
        self.last = None

    def __call__(self, fn, *args, **kw):
        out = fn(*args, dep=self.last, **kw)
        self.last = out[0] if isinstance(out, (tuple, list)) else out
        return out


def _rstd(h):
    return lax.rsqrt(jnp.mean(h * h, axis=-1, keepdims=True) + EPS)


def _rms_bwd(dhn, h, g, dres):
    r = _rstd(h)
    xh = h * r
    dg = jnp.sum(dhn * xh, axis=0, keepdims=True)
    dxh = dhn * g
    dx = r * (dxh - xh * jnp.mean(dxh * xh, axis=-1, keepdims=True))
    return dres + dx, dg


def _gelu(z):
    return 0.5 * z * (1.0 + lax.erf(z * INV_SQRT2))


def _gelu_grad(z):
    return 0.5 * (1.0 + lax.erf(z * INV_SQRT2)) + z * (jnp.exp(-0.5 * z * z) * INV_SQRT_2PI)


def _dot(a, b, dims):
    return lax.dot_general(a, b, (dims, ((), ())), preferred_element_type=F32)


NN = ((1,), (0,))
NT = ((1,), (1,))
TN = ((0,), (0,))


def _mm(name, ins, in_specs, out_shapes, out_specs, *, grid, dims, nk, acc_shape, load_a, load_b, epilogue,
        dep=None):
    n_in, n_out = len(ins), len(out_shapes)
    kax = len(grid) - 1

    def body(*refs):
        in_refs = refs[:n_in]
        out_refs = refs[n_in:n_in + n_out]
        a = load_a(in_refs, out_refs)
        b = load_b(in_refs)
        prod = _dot(a, b, dims)
        if nk == 1:
            epilogue(prod, in_refs, out_refs)
        else:
            acc = refs[n_in + n_out]
            k = pl.program_id(kax)

            @pl.when(k == 0)
            def _():
                acc[...] = prod

            @pl.when(k > 0)
            def _():
                acc[...] += prod

            @pl.when(k == nk - 1)
            def _():
                epilogue(acc[...], in_refs, out_refs)

    return _pcall(
        body, name=name, ins=ins, in_specs=in_specs, dep=dep, grid=grid, out_specs=out_specs, out_shape=out_shapes,
        scratch_shapes=[pltpu.VMEM(acc_shape, F32)] if nk > 1 else [], compiler_params=_params(len(grid)))


def _bf(ref_idx):
    return lambda in_refs, *_: in_refs[ref_idx][...].astype(BF16)


def _b_view(ref_idx, rows):
    def load(in_refs):
        b = in_refs[ref_idx][...]
        return b.reshape(rows, b.shape[-1])
    return load


class Cfg:
    pass


def _config(x, a_w_in, a_w_out, w_k, b_w_q, b_w_o, ffn_w1, ffn_w2):
    c = Cfg()
    c.S, c.D = x.shape[1], x.shape[2]
    c.LA, _, c.cw = a_w_in.shape
    c.AW2 = NDEV * c.cw
    c.AW = c.AW2 // 2
    c.gd = c.AW // A_GROUPS
    c.ar = a_w_out.shape[1]
    c.LF, _, c.fw = ffn_w1.shape
    c.fr = ffn_w2.shape[1]
    c.LB, c.qr, c.DQ = b_w_q.shape
    c.orr = b_w_o.shape[1]
    c.kr, c.DKV = w_k.shape
    c.tm = min(1024, c.S)
    c.tmw = min(512, c.S)
    c.tms = min(256, c.S)
    c.nb = c.S // BLOCK
    assert c.cw == c.fw == c.fr and c.AW == NDEV * c.ar and c.D == NDEV * c.qr == NDEV * c.kr
    assert c.DQ == NDEV * c.orr == N_HEADS * HEAD_DIM and c.DKV == N_KV_HEADS * HEAD_DIM
    assert c.S % c.tm == 0 and c.S % c.tmw == 0 and c.S % c.tms == 0 and c.tms % CHUNK == 0 and c.gd % LANES == 0
    assert c.fr % c.ar == 0 and c.fr % c.orr == 0
    assert c.LA >= 1 and c.LB >= 1 and c.LF == c.LA + c.LB
    return c


def _cached_rms(h_idx, g_idx, hn_out_idx, jax_axis=1):
    def load(in_refs, out_refs):
        hn_ref = out_refs[hn_out_idx]

        @pl.when(pl.program_id(jax_axis) == 0)
        def _():
            h = in_refs[h_idx][...]
            hn_ref[...] = (h * _rstd(h) * in_refs[g_idx][...]).astype(BF16)

        return hn_ref[...]
    return load


def _a_in_fwd(c, name, h, g, col, ci, dep=None):
    S, D, cw, tm = c.S, c.D, c.cw, c.tmw

    def body(h_ref, g_ref, w_ref, z_ref, a_ref, hn_ref):
        h = h_ref[...]
        hn = (h * _rstd(h) * g_ref[...]).astype(BF16)
        hn_ref[...] = hn
        for j in range(NDEV):
            cols = slice(j * cw, (j + 1) * cw)
            z = _dot(hn, w_ref[j], NT)
            z_ref[:, cols] = z.astype(BF16)
            a_ref[:, cols] = _gelu(z).astype(BF16)

    row = pl.BlockSpec((tm, D), lambda i: (i, 0))
    wide = pl.BlockSpec((tm, c.AW2), lambda i: (i, 0))
    return _pcall(
        body, name=name, ins=[h, g, col], dep=dep, grid=(S // tm,),
        in_specs=[row, pl.BlockSpec((1, D), lambda i: (0, 0)), _const((NDEV, None, cw, D), lambda i: (0, ci, 0, 0))],
        out_specs=[wide, wide, row],
        out_shape=[jax.ShapeDtypeStruct((S, c.AW2), BF16), jax.ShapeDtypeStruct((S, c.AW2), BF16),
                   jax.ShapeDtypeStruct((S, D), BF16)],
        compiler_params=_params(1))


def _rms_mm_rows(c, name, h, g, slab, blk_rows, blk_idx, n_out, dep=None):
    S, D, tm = c.S, c.D, c.tm

    def epilogue(acc, in_refs, out_refs):
        out_refs[0][...] = acc.astype(BF16)

    return _mm(
        name, [h, slab, g],
        [pl.BlockSpec((tm, D), lambda i, j, k: (i, 0)),
         pl.BlockSpec((NDEV, blk_rows, n_out), lambda i, j, k: (0, blk_idx, 0)),
         pl.BlockSpec((1, D), lambda i, j, k: (0, 0))],
        [jax.ShapeDtypeStruct((S, n_out), BF16), jax.ShapeDtypeStruct((S, D), BF16)],
        [pl.BlockSpec((tm, n_out), lambda i, j, k: (i, 0)), pl.BlockSpec((tm, D), lambda i, j, k: (i, 0))],
        grid=(S // tm, 1, 1), dims=NN, nk=1, acc_shape=None,
        load_a=_cached_rms(0, 2, 1), load_b=_b_view(1, NDEV * blk_rows), epilogue=epilogue, dep=dep)


def _mm_res(c, name, a, slab, blk_rows, blk_idx, res, dep=None):
    S, D, tm = c.S, c.D, c.tm
    K = NDEV * blk_rows

    def epilogue(acc, in_refs, out_refs):
        out_refs[0][...] = in_refs[2][...] + acc

    return _mm(
        name, [a, slab, res],
        [pl.BlockSpec((tm, K), lambda i, j, k: (i, 0)),
         pl.BlockSpec((NDEV, blk_rows, D), lambda i, j, k: (0, blk_idx, 0)),
         pl.BlockSpec((tm, D), lambda i, j, k: (i, 0))],
        [jax.ShapeDtypeStruct((S, D), F32)], [pl.BlockSpec((tm, D), lambda i, j, k: (i, 0))],
        grid=(S // tm, 1, 1), dims=NN, nk=1, acc_shape=None,
        load_a=_bf(0), load_b=_b_view(1, K), epilogue=epilogue, dep=dep)[0]


def _sgu_masks():
    ii = lax.broadcasted_iota(jnp.int32, (CHUNK, CHUNK), 0)
    jj = lax.broadcasted_iota(jnp.int32, (CHUNK, CHUNK), 1)
    return ii >= jj


def _sgu_fwd(c, name, a, ln_g, wc, b_t, dep=None):
    S, AW, gd, tm = c.S, c.AW, c.gd, c.tms

    def body(a_ref, lng_ref, wc_ref, bt_ref, out_ref):
        va = a_ref[:, AW:].astype(F32)
        xc = va - jnp.mean(va, axis=-1, keepdims=True)
        vn = (xc * lax.rsqrt(jnp.mean(xc * xc, axis=-1, keepdims=True) + EPS) * lng_ref[...]).astype(BF16)
        for ch in range(tm // CHUNK):
            rows = slice(ch * CHUNK, (ch + 1) * CHUNK)
            for g in range(A_GROUPS):
                cols = slice(g * gd, (g + 1) * gd)
                mixed = _dot(wc_ref[g], vn[rows, cols], NN) + bt_ref[:, g:g + 1]
                out_ref[rows, cols] = (a_ref[rows, cols].astype(F32) * mixed).astype(BF16)

    return _pcall(
        body, name=name, ins=[a, ln_g, wc, b_t], dep=dep, grid=(S // tm,),
        in_specs=[pl.BlockSpec((tm, 2 * AW), lambda i: (i, 0)), pl.BlockSpec((1, AW), lambda i: (0, 0)),
                  pl.BlockSpec((A_GROUPS, CHUNK, CHUNK), lambda i: (0, 0, 0)),
                  pl.BlockSpec((CHUNK, A_GROUPS), lambda i: (0, 0))],
        out_specs=pl.BlockSpec((tm, AW), lambda i: (i, 0)),
        out_shape=jax.ShapeDtypeStruct((S, AW), BF16), compiler_params=_params(1))


def _ffn_fwd(c, name, h, g, col, ci, rows, dep=None):
    S, D, fw, tm = c.S, c.D, c.fw, c.tmw
    F = NDEV * fw

    def body(h_ref, g_ref, w1_ref, w2_ref, p_ref, out_ref, hn_ref, r_ref):
        h = h_ref[...]
        hn = (h * _rstd(h) * g_ref[...]).astype(BF16)
        hn_ref[...] = hn
        for j in range(NDEV):
            cols = slice(j * fw, (j + 1) * fw)
            p = jnp.maximum(_dot(hn, w1_ref[j], NT), 0.0)
            p_ref[:, cols] = p.astype(BF16)
            r_ref[:, cols] = (p * p).astype(BF16)
        out_ref[...] = h + _dot(r_ref[...], w2_ref[...].reshape(F, D), NN)

    row = pl.BlockSpec((tm, D), lambda i: (i, 0))
    return _pcall(
        body, name=name, ins=[h, g, col, rows], dep=dep, grid=(S // tm,),
        in_specs=[row, pl.BlockSpec((1, D), lambda i: (0, 0)),
                  _const((NDEV, None, fw, D), lambda i: (0, ci, 0, 0)), _const((NDEV, c.fr, D), lambda i: (0, 0, 0))],
        out_specs=[pl.BlockSpec((tm, F), lambda i: (i, 0)), row, row],
        out_shape=[jax.ShapeDtypeStruct((S, F), BF16), jax.ShapeDtypeStruct((S, D), F32),
                   jax.ShapeDtypeStruct((S, D), BF16)],
        scratch_shapes=[pltpu.VMEM((tm, F), BF16)], compiler_params=_params(1))


def _bucket_table():
    qi = np.arange(BLOCK)[:, None]
    kj = np.arange(2 * BLOCK)[None, :]
    d = np.maximum(qi + BLOCK - kj, 0)
    max_exact = N_BUCKETS // 2
    ratio = np.log(np.maximum(d, 1).astype(np.float32) / np.float32(max_exact)) / np.float32(
        math.log(MAX_DISTANCE / max_exact))
    large = np.minimum(max_exact + (ratio.astype(np.float32) * np.float32(N_BUCKETS - max_exact)).astype(np.int32),
                       N_BUCKETS - 1)
    return np.where(d < max_exact, d, large).astype(np.int32)


def _bucket_onehot():
    b = jnp.asarray(_bucket_table().reshape(1, -1))
    return (b == lax.broadcasted_iota(jnp.int32, (N_BUCKETS, b.shape[1]), 0)).astype(F32)


def _whole(t):
    return pl.BlockSpec(t.shape, lambda: (0,) * t.ndim)


def _band_bias(rel_bias_t, onehot, dep=None):
    def body(r_ref, oh_ref, out_ref):
        out_ref[...] = lax.dot_general(r_ref[...], oh_ref[...], (NN, ((), ())), preferred_element_type=F32,
                                       precision=lax.Precision.HIGHEST)

    n = onehot.shape[1]
    return _pcall(body, name="band_bias", ins=[rel_bias_t, onehot], in_specs=[_whole(rel_bias_t), _whole(onehot)],
                  dep=dep, out_shape=jax.ShapeDtypeStruct((N_HEADS, n), F32), compiler_params=_params(0))


def _band_bias_grad(dbias_list, onehot, dep=None):
    n_in = len(dbias_list)

    def body(*refs):
        oh_ref, out_ref = refs[n_in], refs[n_in + 1]
        d = refs[0][...]
        for r in refs[1:n_in]:
            d = d + r[...]
        out_ref[...] = lax.dot_general(d, oh_ref[...], (NT, ((), ())), preferred_element_type=F32,
                                       precision=lax.Precision.HIGHEST)

    ins = [*dbias_list, onehot]
    return _pcall(body, name="band_bias_grad", ins=ins, in_specs=[_whole(t) for t in ins], dep=dep,
                  out_shape=jax.ShapeDtypeStruct((N_HEADS, N_BUCKETS), F32), compiler_params=_params(0))


KV_PAIRS = N_KV_HEADS // 2
PAIR_ROWS = 2 * Q_PER_KV * BLOCK
MASKED = float(np.finfo(np.float32).min) / 2


def _slot_cols(w):
    lead = w.shape[:-1]
    return w.reshape(*lead, KV_PAIRS, 2, Q_PER_KV, HEAD_DIM).swapaxes(-3, -2).reshape(*lead, N_HEADS * HEAD_DIM)


def _unslot_cols(w):
    lead = w.shape[:-1]
    return w.reshape(*lead, KV_PAIRS, Q_PER_KV, 2, HEAD_DIM).swapaxes(-3, -2).reshape(*lead, N_HEADS * HEAD_DIM)


def _slot_rows(blocks):
    n = blocks.shape[-1]
    return blocks.reshape(KV_PAIRS, 2, Q_PER_KV, HEAD_DIM, n).swapaxes(1, 2).reshape(blocks.shape)


def _unslot_rows(blocks):
    n = blocks.shape[-1]
    return blocks.reshape(KV_PAIRS, Q_PER_KV, 2, HEAD_DIM, n).swapaxes(1, 2).reshape(blocks.shape)


def _slot_bias(bias):
    qi = np.arange(BLOCK)[:, None]
    kj = np.arange(2 * BLOCK)[None, :]
    dist = qi + BLOCK - kj
    window = (dist >= 0) & (dist < BLOCK)
    b = bias.reshape(KV_PAIRS, 2, Q_PER_KV, BLOCK, 2 * BLOCK).swapaxes(1, 2).reshape(KV_PAIRS, PAIR_ROWS, 2 * BLOCK)
    tile = lambda mk: jnp.asarray(np.tile(mk, (2 * Q_PER_KV, 1)))[None]
    return jnp.stack([jnp.where(tile(window & (kj >= BLOCK)), b, MASKED), jnp.where(tile(window), b, MASKED)])


def _unslot_bias(db):
    return db.reshape(KV_PAIRS, Q_PER_KV, 2, BLOCK, 2 * BLOCK).swapaxes(1, 2).reshape(N_HEADS, -1)


def _pair_kv(kvc_ref, kvp_ref, kvp, dkv):
    lanes = slice(kvp * LANES, (kvp + 1) * LANES)
    vlanes = slice(dkv + kvp * LANES, dkv + (kvp + 1) * LANES)
    k2 = jnp.concatenate([kvp_ref[:, lanes], kvc_ref[:, lanes]], axis=0)
    v2 = jnp.concatenate([kvp_ref[:, vlanes], kvc_ref[:, vlanes]], axis=0)
    return k2, v2


def _head_operand(ref, grp, par, low, scale=None):
    xg = ref[:, grp * LANES:(grp + 1) * LANES]
    if scale is not None:
        xg = xg * scale
    zero = jnp.zeros_like(xg)
    return jnp.where(low, xg, zero) if par == 0 else jnp.where(low, zero, xg)


def _head_probs(qh, k2, bias_rows, sink):
    s = _dot(qh, k2, NT) + bias_rows
    m = jnp.maximum(jnp.max(s, axis=-1, keepdims=True), sink)
    e = jnp.exp(s - m)
    es = jnp.exp(sink - m)
    inv = 1.0 / (jnp.sum(e, axis=-1, keepdims=True) + es)
    return e * inv, es * inv


def _attn_specs(c):
    dq, dkv2 = c.DQ, 2 * c.DKV
    return [pl.BlockSpec((BLOCK, dq), lambda n: (n, 0)),
            pl.BlockSpec((BLOCK, dkv2), lambda n: (n, 0)),
            pl.BlockSpec((BLOCK, dkv2), lambda n: (jnp.maximum(n - 1, 0), 0))]


def _bias_spec():
    return pl.BlockSpec((None, KV_PAIRS, PAIR_ROWS, 2 * BLOCK), lambda n: (jnp.minimum(n, 1), 0, 0, 0))


def _low_lanes():
    return lax.broadcasted_iota(jnp.int32, (BLOCK, LANES), 1) < HEAD_DIM


def _attn_fwd(c, name, q, kv, bias, sinks, dep=None):
    S, dq = c.S, c.DQ

    def body(q_ref, kvc_ref, kvp_ref, bias_ref, sink_ref, o_ref):
        low = _low_lanes()
        for kvp in range(KV_PAIRS):
            k2, v2 = _pair_kv(kvc_ref, kvp_ref, kvp, c.DKV)
            for g in range(Q_PER_KV):
                grp = kvp * Q_PER_KV + g
                halves = []
                for par in range(2):
                    r = 2 * g + par
                    qh = _head_operand(q_ref, grp, par, low, scale=HEAD_DIM ** -0.5)
                    p, _ = _head_probs(qh, k2, bias_ref[kvp, r * BLOCK:(r + 1) * BLOCK, :],
                                       sink_ref[(2 * kvp + par) * Q_PER_KV + g])
                    halves.append(_dot(p.astype(BF16), v2, NN))
                o_ref[:, grp * LANES:(grp + 1) * LANES] = jnp.where(low, halves[0], halves[1]).astype(BF16)

    return _pcall(
        body, name=name, ins=[q, kv, kv, bias, sinks], dep=dep, grid=(c.nb,),
        in_specs=_attn_specs(c) + [_bias_spec(), SMEM],
        out_specs=pl.BlockSpec((BLOCK, dq), lambda n: (n, 0)),
        out_shape=jax.ShapeDtypeStruct((S, dq), BF16), compiler_params=_params(1))


def _final_loss(c, h, g, target, dep=None):
    S, D, tm = c.S, c.D, c.tm

    def body(h_ref, g_ref, t_ref, dh_ref, dg_ref, loss_ref):
        i = pl.program_id(0)
        h = h_ref[...]
        gg = g_ref[...]
        r = _rstd(h)
        xh = h * r
        err = xh * gg - t_ref[...]
        lp = jnp.sum(jnp.sum(err * err, axis=1, keepdims=True), axis=0, keepdims=True) * (0.5 / D)
        dx, dg = _rms_bwd(err * (1.0 / D), h, gg, 0.0)
        dh_ref[...] = dx

        @pl.when(i == 0)
        def _():
            dg_ref[...] = dg
            loss_ref[...] = jnp.broadcast_to(lp, loss_ref.shape)

        @pl.when(i > 0)
        def _():
            dg_ref[...] += dg
            loss_ref[...] += jnp.broadcast_to(lp, loss_ref.shape)

    row = pl.BlockSpec((tm, D), lambda i: (i, 0))
    return _pcall(
        body, name="final_loss", ins=[h, g, target], dep=dep, grid=(S // tm,),
        in_specs=[row, pl.BlockSpec((1, D), lambda i: (0, 0)), row],
        out_specs=[row, pl.BlockSpec((1, D), lambda i: (0, 0)), pl.BlockSpec((1, LANES), lambda i: (0, 0))],
        out_shape=[jax.ShapeDtypeStruct((S, D), F32), jax.ShapeDtypeStruct((1, D), F32),
                   jax.ShapeDtypeStruct((1, LANES), F32)],
        compiler_params=_params(1))


def _rms_bwd_epilogue(h_idx, g_idx, res_idx):
    def epilogue(dhn, in_refs, out_refs):
        dh, dg = _rms_bwd(dhn, in_refs[h_idx][...], in_refs[g_idx][...], in_refs[res_idx][...])
        out_refs[0][...] = dh
        i = pl.program_id(0)

        @pl.when(i == 0)
        def _():
            out_refs[1][...] = dg

        @pl.when(i > 0)
        def _():
            out_refs[1][...] += dg
    return epilogue


def _stream_outs(c, tm):
    S, D = c.S, c.D
    return ([jax.ShapeDtypeStruct((S, D), F32), jax.ShapeDtypeStruct((1, D), F32)],
            [pl.BlockSpec((tm, D), lambda i, j, k: (i, 0)), pl.BlockSpec((1, D), lambda i, j, k: (0, 0))])


def _row_specs(c, tm):
    D = c.D
    return [pl.BlockSpec((tm, D), lambda i, j, k: (i, 0)), pl.BlockSpec((1, D), lambda i, j, k: (0, 0)),
            pl.BlockSpec((tm, D), lambda i, j, k: (i, 0))]


def _bwd_rows_to_stream(c, name, dy_list, slab, blk_rows, blk_idx, n_in_cols, h, g, dres, dep=None):
    S, D, tm = c.S, c.D, c.tm
    nd = len(dy_list)

    def load_a(in_refs, out_refs):
        a = in_refs[0][...]
        for r in in_refs[1:nd]:
            a = a + r[...]
        return a.astype(BF16)

    shapes, specs = _stream_outs(c, tm)
    return _mm(
        name, [*dy_list, slab, h, g, dres],
        [pl.BlockSpec((tm, n_in_cols), lambda i, j, k: (i, 0))] * nd
        + [pl.BlockSpec((NDEV, blk_rows, n_in_cols), lambda i, j, k: (0, blk_idx, 0))] + _row_specs(c, tm),
        shapes, specs, grid=(S // tm, 1, 1), dims=NT, nk=1, acc_shape=None,
        load_a=load_a, load_b=_b_view(nd, NDEV * blk_rows), epilogue=_rms_bwd_epilogue(nd + 1, nd + 2, nd + 3),
        dep=dep)


def _bwd_cols_to_stream(c, name, dy, col, ci, h, g, dres, dep=None):
    S, D, cw, tm = c.S, c.D, c.cw, c.tmw
    K = NDEV * cw
    shapes, specs = _stream_outs(c, tm)
    return _mm(
        name, [dy, col, h, g, dres],
        [pl.BlockSpec((tm, K), lambda i, j, k: (i, 0)),
         _const((NDEV, None, cw, D), lambda i, j, k: (0, ci, 0, 0))] + _row_specs(c, tm),
        shapes, specs, grid=(S // tm, 1, 1), dims=NN, nk=1, acc_shape=None,
        load_a=_bf(0), load_b=_b_view(1, K), epilogue=_rms_bwd_epilogue(2, 3, 4), dep=dep)


def _bwd_rows_data(c, name, dy, slab, blk_rows, blk_idx, dep=None):
    S, D, tm = c.S, c.D, c.tm
    K = NDEV * blk_rows

    def epilogue(acc, in_refs, out_refs):
        out_refs[0][...] = acc.astype(BF16)

    return _mm(
        name, [dy, slab],
        [pl.BlockSpec((tm, D), lambda i, j, k: (i, 0)),
         pl.BlockSpec((NDEV, blk_rows, D), lambda i, j, k: (0, blk_idx, 0))],
        [jax.ShapeDtypeStruct((S, K), BF16)], [pl.BlockSpec((tm, K), lambda i, j, k: (i, 0))],
        grid=(S // tm, 1, 1), dims=NT, nk=1, acc_shape=None,
        load_a=_bf(0), load_b=_b_view(1, K), epilogue=epilogue, dep=dep)[0]


def _wgrad_rows(c, name, a, b_list, n_a, n_b, dep=None):
    S, tm = c.S, c.tm
    nb_in = len(b_list)
    blk_rows = n_a // NDEV

    def load_b(in_refs):
        b = in_refs[1][...]
        for r in in_refs[2:1 + nb_in]:
            b = b + r[...]
        return b.astype(BF16)

    def epilogue(acc, in_refs, out_refs):
        out_refs[0][...] = acc.reshape(NDEV, blk_rows, n_b).astype(BF16)

    return _mm(
        name, [a, *b_list],
        [pl.BlockSpec((tm, n_a), lambda i, j, k: (k, 0))] + [pl.BlockSpec((tm, n_b), lambda i, j, k: (k, 0))] * nb_in,
        [jax.ShapeDtypeStruct((NDEV, blk_rows, n_b), BF16)],
        [pl.BlockSpec((NDEV, blk_rows, n_b), lambda i, j, k: (0, 0, 0))],
        grid=(1, 1, S // tm), dims=TN, nk=S // tm, acc_shape=(n_a, n_b),
        load_a=_bf(0), load_b=load_b, epilogue=epilogue, dep=dep)[0]


def _wgrad_cols(c, name, a, b, dep=None):
    S, D, cw = c.S, c.D, c.cw

    def body(a_ref, b_ref, out_ref):
        out_ref[...] = _dot(a_ref[...], b_ref[...], TN).astype(BF16)

    return _pcall(
        body, name=name, ins=[a, b], dep=dep, grid=(NDEV,),
        in_specs=[_const((S, D), lambda j: (0, 0)), pl.BlockSpec((S, cw), lambda j: (0, j))],
        out_specs=pl.BlockSpec((None, D, cw), lambda j: (j, 0, 0)),
        out_shape=jax.ShapeDtypeStruct((NDEV, D, cw), BF16), compiler_params=_params(1))


def _ffn_bwd_data(c, name, dh, p, col, ci, rows, h, g, dep=None):
    S, D, fw, tm = c.S, c.D, c.fw, c.tmw
    F = NDEV * fw

    def body(dh_ref, p_ref, w1t_ref, w2_ref, h_ref, g_ref, da_ref, out_ref, dg_ref, dhb_ref):
        i = pl.program_id(0)
        dh = dh_ref[...]
        dhb = dh.astype(BF16)
        dhb_ref[...] = dhb
        for j in range(NDEV):
            cols = slice(j * fw, (j + 1) * fw)
            da_ref[:, cols] = (_dot(dhb, w2_ref[j], NT) * (2.0 * p_ref[:, cols].astype(F32))).astype(BF16)
        dx, dg = _rms_bwd(_dot(da_ref[...], w1t_ref[...].reshape(F, D), NN), h_ref[...], g_ref[...], dh)
        out_ref[...] = dx

        @pl.when(i == 0)
        def _():
            dg_ref[...] = dg

        @pl.when(i > 0)
        def _():
            dg_ref[...] += dg

    row = pl.BlockSpec((tm, D), lambda i: (i, 0))
    wide = pl.BlockSpec((tm, F), lambda i: (i, 0))
    return _pcall(
        body, name=name, ins=[dh, p, col, rows, h, g], dep=dep, grid=(S // tm,),
        in_specs=[row, wide, _const((NDEV, None, fw, D), lambda i: (0, ci, 0, 0)),
                  _const((NDEV, c.fr, D), lambda i: (0, 0, 0)),
                  row, pl.BlockSpec((1, D), lambda i: (0, 0))],
        out_specs=[wide, row, pl.BlockSpec((1, D), lambda i: (0, 0)), row],
        out_shape=[jax.ShapeDtypeStruct((S, F), BF16), jax.ShapeDtypeStruct((S, D), F32),
                   jax.ShapeDtypeStruct((1, D), F32), jax.ShapeDtypeStruct((S, D), BF16)],
        compiler_params=_params(1))


def _ffn_bwd_w(c, name, hn, da, p, dhb, dep=None):
    S, D, fw = c.S, c.D, c.fw

    def body(hn_ref, da_ref, p_ref, dhb_ref, dw1_ref, dw2_ref):
        dw1_ref[...] = _dot(hn_ref[...], da_ref[...], TN).astype(BF16)
        pf = p_ref[...].astype(F32)
        dw2_ref[...] = _dot((pf * pf).astype(BF16), dhb_ref[...], TN).astype(BF16)

    panel = pl.BlockSpec((S, fw), lambda j: (0, j))
    return _pcall(
        body, name=name, ins=[hn, da, p, dhb], dep=dep, grid=(NDEV,),
        in_specs=[_const((S, D), lambda j: (0, 0)), panel, panel, _const((S, D), lambda j: (0, 0))],
        out_specs=[pl.BlockSpec((None, D, fw), lambda j: (j, 0, 0)), pl.BlockSpec((None, c.fr, D), lambda j: (j, 0, 0))],
        out_shape=[jax.ShapeDtypeStruct((NDEV, D, fw), BF16), jax.ShapeDtypeStruct((NDEV, c.fr, D), BF16)],
        compiler_params=_params(1))


def _attn_bwd(c, name, q, kv, do, bias, sinks, dep=None):
    S, dq, dkv = c.S, c.DQ, c.DKV
    nb = c.nb
    scale = HEAD_DIM ** -0.5

    def body(q_ref, kvc_ref, kvp_ref, do_ref, bias_ref, sink_ref, dq_ref, dkv_ref, dbias_ref, dsink_ref, dsink_acc,
             ds_sc, p_sc, qm_sc, dom_sc):
        n = pl.program_id(0)

        @pl.when(n == 0)
        def _():
            dkv_ref[...] = jnp.zeros_like(dkv_ref)
            dbias_ref[...] = jnp.zeros_like(dbias_ref)
            dsink_acc[...] = jnp.zeros_like(dsink_acc)

        low = _low_lanes()
        rows_c = pl.ds(pl.multiple_of(n * BLOCK, BLOCK), BLOCK)
        rows_p = pl.ds(pl.multiple_of(jnp.maximum(n - 1, 0) * BLOCK, BLOCK), BLOCK)
        for kvp in range(KV_PAIRS):
            k2, v2 = _pair_kv(kvc_ref, kvp_ref, kvp, dkv)
            for g in range(Q_PER_KV):
                grp = kvp * Q_PER_KV + g
                halves = []
                for par in range(2):
                    rows = slice((2 * g + par) * BLOCK, (2 * g + par + 1) * BLOCK)
                    qh = _head_operand(q_ref, grp, par, low, scale=scale)
                    doh = _head_operand(do_ref, grp, par, low)
                    p, ps = _head_probs(qh, k2, bias_ref[kvp, rows, :], sink_ref[(2 * kvp + par) * Q_PER_KV + g])
                    dp = _dot(doh, v2, NT)
                    delta = jnp.sum(p * dp, axis=-1, keepdims=True)
                    ds = p * (dp - delta)
                    dbias_ref[kvp, rows, :] += ds
                    dsink_acc[rows, kvp:kvp + 1] += -(ps * delta)
                    ds16 = ds.astype(BF16)
                    halves.append(_dot(ds16, k2, NN) * scale)
                    ds_sc[rows, :] = ds16
                    p_sc[rows, :] = p.astype(BF16)
                    qm_sc[rows, :] = qh
                    dom_sc[rows, :] = doh
                dq_ref[:, grp * LANES:(grp + 1) * LANES] = jnp.where(low, halves[0], halves[1]).astype(BF16)
            dk2 = _dot(ds_sc[...], qm_sc[...], TN)
            dv2 = _dot(p_sc[...], dom_sc[...], TN)
            lanes = slice(kvp * LANES, (kvp + 1) * LANES)
            vlanes = slice(dkv + kvp * LANES, dkv + (kvp + 1) * LANES)
            dkv_ref[rows_p, lanes] += dk2[:BLOCK]
            dkv_ref[rows_c, lanes] += dk2[BLOCK:]
            dkv_ref[rows_p, vlanes] += dv2[:BLOCK]
            dkv_ref[rows_c, vlanes] += dv2[BLOCK:]

        @pl.when(n == nb - 1)
        def _():
            dsink_ref[...] = jnp.sum(dsink_acc[...].reshape(2 * Q_PER_KV, BLOCK, KV_PAIRS), axis=1)

    return _pcall(
        body, name=name, ins=[q, kv, kv, do, bias, sinks], dep=dep, grid=(nb,),
        in_specs=_attn_specs(c) + [pl.BlockSpec((BLOCK, dq), lambda n: (n, 0)), _bias_spec(), SMEM],
        out_specs=[pl.BlockSpec((BLOCK, dq), lambda n: (n, 0)), pl.BlockSpec((S, 2 * dkv), lambda n: (0, 0)),
                   pl.BlockSpec((KV_PAIRS, PAIR_ROWS, 2 * BLOCK), lambda n: (0, 0, 0)),
                   pl.BlockSpec((2 * Q_PER_KV, KV_PAIRS), lambda n: (0, 0))],
        out_shape=[jax.ShapeDtypeStruct((S, dq), BF16), jax.ShapeDtypeStruct((S, 2 * dkv), F32),
                   jax.ShapeDtypeStruct((KV_PAIRS, PAIR_ROWS, 2 * BLOCK), F32),
                   jax.ShapeDtypeStruct((2 * Q_PER_KV, KV_PAIRS), F32)],
        scratch_shapes=[pltpu.VMEM((PAIR_ROWS, KV_PAIRS), F32), pltpu.VMEM((PAIR_ROWS, 2 * BLOCK), BF16),
                        pltpu.VMEM((PAIR_ROWS, 2 * BLOCK), BF16), pltpu.VMEM((PAIR_ROWS, LANES), BF16),
                        pltpu.VMEM((PAIR_ROWS, LANES), BF16)],
        compiler_params=_params(1))


def _sgu_bwd(c, name, a, z, dgated, ln_g, wc, wc_t, b_t, dep=None):
    S, AW, gd, tm = c.S, c.AW, c.gd, c.tms

    def body(a_ref, z_ref, dg_ref, lng_ref, wc_ref, wct_ref, bt_ref, dz_ref, dws_ref, dbt_ref, dlng_ref, dvn_ref):
        i = pl.program_id(0)

        @pl.when(i == 0)
        def _():
            dws_ref[...] = jnp.zeros_like(dws_ref)
            dbt_ref[...] = jnp.zeros_like(dbt_ref)
            dlng_ref[...] = jnp.zeros_like(dlng_ref)

        lng = lng_ref[...]
        va = a_ref[:, AW:].astype(F32)
        xc = va - jnp.mean(va, axis=-1, keepdims=True)
        rstd = lax.rsqrt(jnp.mean(xc * xc, axis=-1, keepdims=True) + EPS)
        xh = xc * rstd
        vn = (xh * lng).astype(BF16)
        causal = _sgu_masks()
        for ch in range(tm // CHUNK):
            rows = slice(ch * CHUNK, (ch + 1) * CHUNK)
            for g in range(A_GROUPS):
                cols = slice(g * gd, (g + 1) * gd)
                blk = vn[rows, cols]
                mixed = _dot(wc_ref[g], blk, NN) + bt_ref[:, g:g + 1]
                dgb = dg_ref[rows, cols].astype(F32)
                dm = dgb * a_ref[rows, cols].astype(F32)
                dbt_ref[:, g:g + 1] += jnp.sum(dm, axis=1, keepdims=True)
                dm16 = dm.astype(BF16)
                dws_ref[g] += jnp.where(causal, _dot(dm16, blk, NT), 0.0)
                dvn_ref[rows, cols] = _dot(wct_ref[g], dm16, NN)
                dz_ref[rows, cols] = (dgb * mixed * _gelu_grad(z_ref[rows, cols].astype(F32))).astype(BF16)
        dvn = dvn_ref[...]
        dlng_ref[...] += jnp.sum(dvn * xh, axis=0, keepdims=True)
        dxh = dvn * lng
        dva = rstd * (dxh - jnp.mean(dxh, axis=-1, keepdims=True) - xh * jnp.mean(dxh * xh, axis=-1, keepdims=True))
        dz_ref[:, AW:] = (dva * _gelu_grad(z_ref[:, AW:].astype(F32))).astype(BF16)

    wide = pl.BlockSpec((tm, 2 * AW), lambda i: (i, 0))
    wsp = pl.BlockSpec((A_GROUPS, CHUNK, CHUNK), lambda i: (0, 0, 0))
    btsp = pl.BlockSpec((CHUNK, A_GROUPS), lambda i: (0, 0))
    return _pcall(
        body, name=name, ins=[a, z, dgated, ln_g, wc, wc_t, b_t], dep=dep, grid=(S // tm,),
        in_specs=[wide, wide, pl.BlockSpec((tm, AW), lambda i: (i, 0)), pl.BlockSpec((1, AW), lambda i: (0, 0)),
                  wsp, wsp, btsp],
        out_specs=[wide, wsp, btsp, pl.BlockSpec((1, AW), lambda i: (0, 0))],
        out_shape=[jax.ShapeDtypeStruct((S, 2 * AW), BF16), jax.ShapeDtypeStruct((A_GROUPS, CHUNK, CHUNK), F32),
                   jax.ShapeDtypeStruct((CHUNK, A_GROUPS), F32), jax.ShapeDtypeStruct((1, AW), F32)],
        scratch_shapes=[pltpu.VMEM((tm, AW), F32)], compiler_params=_params(1))


def _adamw(name, parts, part_block, part_index, w, m, v, tr, row_off=0, n_rows=None, prev=None, dep=None):
    R, C = w.shape
    n_rows = R if n_rows is None else n_rows
    assert n_rows % tr == 0 and row_off % tr == 0
    bc1 = 1.0 - ADAM_B1 ** ADAM_STEP
    bc2 = 1.0 - ADAM_B2 ** ADAM_STEP

    def body(p_ref, w_ref, m_ref, v_ref, *rest):
        g_ref, d_ref, nm_ref, nv_ref = rest[-4:]
        g = p_ref[0].astype(F32)
        for s in range(1, part_block[0]):
            g = g + p_ref[s].astype(F32)
        nm = ADAM_B1 * m_ref[...] + (1.0 - ADAM_B1) * g
        nv = ADAM_B2 * v_ref[...] + (1.0 - ADAM_B2) * (g * g)
        g_ref[...] = g
        nm_ref[...] = nm
        nv_ref[...] = nv
        d_ref[...] = -ADAM_LR * ((nm * (1.0 / bc1)) / (jnp.sqrt(nv * (1.0 / bc2)) + ADAM_EPS) + ADAM_WD * w_ref[...])

    ob = row_off // tr
    row = pl.BlockSpec((tr, C), lambda i: (ob + i, 0))
    out = jax.ShapeDtypeStruct((R, C), F32)
    chained = prev is not None
    return _pcall(
        body, name=name, ins=[parts, w, m, v] + (list(prev) if chained else []), dep=dep, grid=(n_rows // tr,),
        in_specs=[pl.BlockSpec(part_block, part_index), row, row, row] + ([ANY] * 4 if chained else []),
        out_specs=[row, row, row, row], out_shape=[out, out, out, out],
        input_output_aliases={4 + t: t for t in range(4)} if chained else {}, compiler_params=_params(1))


def _sum_parts(name, parts, dep=None):
    def body(p_ref, out_ref):
        g = p_ref[0]
        for s in range(1, parts.shape[0]):
            g = g + p_ref[s]
        out_ref[...] = g

    return _pcall(body, name=name, ins=[parts], in_specs=[_whole(parts)], dep=dep,
                  out_shape=jax.ShapeDtypeStruct(parts.shape[1:], F32), compiler_params=_params(0))


def _place():
    return lax.axis_index("x"), lax.axis_index("y"), lax.axis_index("c")


def _slot(px, py, pc):
    return 4 * px + 2 * py + pc


def _peer(k, x, y, c):
    return x ^ ((k >> 2) & 1), y ^ ((k >> 1) & 1), c ^ (k & 1)


SEND_PEERS = {"exchange": tuple(range(1, NDEV)), "gather": (1, 2, 4, 6), "forward": (2, 4, 6),
              "broadcast": tuple(range(1, NDEV))}


def _send_copies(mode, src_refs, land_refs, send_sems, recv_sems):
    x, y, c = _place()
    me = _slot(x, y, c)
    peers = SEND_PEERS[mode]
    copies = []
    for i, k in enumerate(peers):
        peer = _peer(k, x, y, c)
        for a, land in enumerate(land_refs):
            if mode == "exchange":
                src, dst, to = src_refs[a].at[_slot(*peer)], land.at[me], peer
            elif mode in ("gather", "broadcast"):
                src, dst, to = src_refs[a], land.at[me], peer
            else:
                src = dst = land.at[_slot(*peer)]
                to = (x, y, 1 - c)
            s = a * len(peers) + i
            copies.append(pltpu.make_async_remote_copy(src_ref=src, dst_ref=dst, send_sem=send_sems.at[s],
                                                       recv_sem=recv_sems.at[s], device_id=to, device_id_type=MESH))
    return copies


def _send_start(name, srcs, lands, mode, dep=None):
    n_src, n = len(srcs), len(lands)
    n_sem = n * len(SEND_PEERS[mode])

    def body(*refs):
        src_refs, land_refs = refs[:n_src], refs[n_src:n_src + n]
        send_sems, recv_sems = refs[n_src + n], refs[n_src + n + 1]
        token = refs[-1]
        for cp in _send_copies(mode, src_refs, land_refs, send_sems, recv_sems):
            cp.start()
        token[...] = jnp.zeros_like(token)

    thru = [pltpu.HBM(t.shape, t.dtype) for t in [*srcs, *lands]]
    if any(dep is t for t in [*srcs, *lands]):
        dep = None
    out = _pcall(
        body, name=name, ins=[pltpu.with_memory_space_constraint(t, pltpu.HBM) for t in [*srcs, *lands]],
        in_specs=[HBM] * (n_src + n), dep=dep,
        out_shape=(pltpu.SemaphoreType.DMA((n_sem,)), pltpu.SemaphoreType.DMA((n_sem,)), *thru,
                   jax.ShapeDtypeStruct((8, LANES), F32)),
        out_specs=(SEM, SEM, *[HBM] * (n_src + n), pl.BlockSpec(memory_space=pltpu.VMEM)),
        input_output_aliases={i: 2 + i for i in range(n_src + n)},
        compiler_params=pltpu.CompilerParams(has_side_effects=pltpu.SideEffectType.DATAFLOW_SIDE_EFFECTING))
    return out[-1], out[0], out[1], list(out[2:2 + n_src]), list(out[2 + n_src:2 + n_src + n])


def _send_wait(name, started, mode, dep=None):
    _, send_sems, recv_sems, srcs, lands = started
    n_src, n = len(srcs), len(lands)

    def body(*refs):
        src_refs, land_refs = refs[:n_src], refs[n_src:n_src + n]
        ssem, rsem = refs[n_src + n], refs[n_src + n + 1]
        for cp in _send_copies(mode, src_refs, land_refs, ssem, rsem):
            cp.wait_send()
            cp.wait_recv()

    thru = [pltpu.HBM(t.shape, t.dtype) for t in [*srcs, *lands]]
    out = _pcall(
        body, name=name, ins=[*srcs, *lands, send_sems, recv_sems], in_specs=[HBM] * (n_src + n) + [SEM, SEM], dep=dep,
        out_shape=tuple(thru), out_specs=tuple([HBM] * (n_src + n)),
        input_output_aliases={i: i for i in range(n_src + n)},
        compiler_params=pltpu.CompilerParams(has_side_effects=pltpu.SideEffectType.DATAFLOW_SIDE_EFFECTING))
    return list(out[n_src:])


def _landing(block, me):
    zone = lax.empty((NDEV, *block.shape), block.dtype)
    return lax.dynamic_update_slice(zone, block[None], (me,) + (0,) * block.ndim)


def _rows128(t):
    flat = t.reshape(-1)
    n = flat.shape[0]
    rows = -(-n // (8 * LANES)) * 8
    return jnp.pad(flat, (0, rows * LANES - n)).reshape(rows, LANES)


def kernel(x, mix_norm_g, ffn_norm_g, a_w_in, a_ln_g, a_w_spatial, a_b_spatial, a_w_out, kv_norm_g, w_k, w_v, b_w_q, b_sinks, b_w_o, rel_bias, ffn_w1, ffn_w2, final_norm_g, loss_target, m_mix_norm_g, m_ffn_norm_g, m_a_w_in, m_a_ln_g, m_a_w_spatial, m_a_b_spatial, m_a_w_out, m_kv_norm_g, m_w_k, m_w_v, m_b_w_q, m_b_sinks, m_b_w_o, m_rel_bias, m_ffn_w1, m_ffn_w2, m_final_norm_g, v_mix_norm_g, v_ffn_norm_g, v_a_w_in, v_a_ln_g, v_a_w_spatial, v_a_b_spatial, v_a_w_out, v_kv_norm_g, v_w_k, v_w_v, v_b_w_q, v_b_sinks, v_b_w_o, v_rel_bias, v_ffn_w1, v_ffn_w2, v_final_norm_g):
    c = _config(x, a_w_in, a_w_out, w_k, b_w_q, b_w_o, ffn_w1, ffn_w2)
    S, D, LA, LB, LF = c.S, c.D, c.LA, c.LB, c.LF
    weights = dict(mix_norm_g=mix_norm_g, ffn_norm_g=ffn_norm_g, a_w_in=a_w_in, a_ln_g=a_ln_g, a_w_spatial=a_w_spatial,
                   a_b_spatial=a_b_spatial, a_w_out=a_w_out, kv_norm_g=kv_norm_g, w_k=w_k, w_v=w_v, b_w_q=b_w_q,
                   b_sinks=b_sinks, b_w_o=b_w_o, rel_bias=rel_bias, ffn_w1=ffn_w1, ffn_w2=ffn_w2,
                   final_norm_g=final_norm_g)
    m_in = dict(mix_norm_g=m_mix_norm_g, ffn_norm_g=m_ffn_norm_g, a_w_in=m_a_w_in, a_ln_g=m_a_ln_g,
                a_w_spatial=m_a_w_spatial, a_b_spatial=m_a_b_spatial, a_w_out=m_a_w_out, kv_norm_g=m_kv_norm_g,
                w_k=m_w_k, w_v=m_w_v, b_w_q=m_b_w_q, b_sinks=m_b_sinks, b_w_o=m_b_w_o, rel_bias=m_rel_bias,
                ffn_w1=m_ffn_w1, ffn_w2=m_ffn_w2, final_norm_g=m_final_norm_g)
    v_in = dict(mix_norm_g=v_mix_norm_g, ffn_norm_g=v_ffn_norm_g, a_w_in=v_a_w_in, a_ln_g=v_a_ln_g,
                a_w_spatial=v_a_w_spatial, a_b_spatial=v_a_b_spatial, a_w_out=v_a_w_out, kv_norm_g=v_kv_norm_g,
                w_k=v_w_k, w_v=v_w_v, b_w_q=v_b_w_q, b_sinks=v_b_sinks, b_w_o=v_b_w_o, rel_bias=v_rel_bias,
                ffn_w1=v_ffn_w1, ffn_w2=v_ffn_w2, final_norm_g=v_final_norm_g)
    names = list(weights)
    seq = _Seq()
    me = _slot(*_place())
    bf = lambda t: t.astype(BF16)

    def rows_of(l):
        second = a_w_out[l] if l < LA else b_w_o[l - LA]
        return bf(jnp.concatenate([ffn_w2[l], second], axis=0))

    tr = lambda t: bf(jnp.swapaxes(t, -1, -2))
    groups = [[tr(a_w_in[0])[None], a_ln_g],
              [bf(a_w_out[0])],
              [tr(ffn_w1[0])[None], bf(ffn_w2[0])]]
    for l in range(1, LA):
        groups.append([tr(jnp.stack([a_w_in[l], ffn_w1[l]])), rows_of(l)])
    for l in range(LB):
        extra = [bf(jnp.concatenate([w_k, w_v], axis=1))] if l == 0 else []
        groups.append(extra + [bf(b_w_q[l]), tr(ffn_w1[LA + l])[None], rows_of(LA + l)])
    started = [seq(_send_start, f"weights_start{i}", grp, [_landing(t, me) for t in grp], "gather")
               for i, grp in enumerate(groups)]
    forwarding = {}

    def forward(i):
        lands = seq(_send_wait, f"weights_wait{i}", started[i], "gather")
        forwarding[i] = seq(_send_start, f"weights_forward{i}", [], lands, "forward")

    def arrive(i):
        if i not in forwarding:
            forward(i)
        return seq(_send_wait, f"weights_arrive{i}", forwarding[i], "forward")

    causal = jnp.tril(jnp.ones((CHUNK, CHUNK), bool))
    wsp = jnp.where(causal[None, None], a_w_spatial, 0.0)
    wsp16 = wsp.astype(BF16)
    wsp16_t = jnp.swapaxes(wsp, -1, -2).astype(BF16)
    bsp_t = jnp.swapaxes(a_b_spatial, -1, -2)
    mix_g = mix_norm_g.reshape(-1, 1, D)
    ffn_g = ffn_norm_g.reshape(-1, 1, D)
    kv_g = kv_norm_g.reshape(1, D)
    fin_g = final_norm_g.reshape(1, D)
    onehot = _bucket_onehot()
    bias = _slot_bias(seq(_band_bias, rel_bias.T, onehot).reshape(N_HEADS, BLOCK, 2 * BLOCK))

    h = x.reshape(S, D)
    sav_a, sav_b, wts_a, wts_b = [], [], [], []
    for l in range(LA):
        if l == 0:
            w_in, lng_all = arrive(0)
            ln_g_full = jnp.transpose(lng_all, (1, 0, 2)).reshape(LA, 1, c.AW)
            in_i, w1_i, wout_i = 0, 0, 0
        else:
            w_in, rows = arrive(l + 2)
            w1, in_i, w1_i, wout, wout_i = w_in, 0, 1, rows, c.fr // c.ar
        z, a, hn = seq(_a_in_fwd, c, f"a_in_fwd{l}", h, mix_g[l], w_in, in_i)
        if l == 0:
            forward(1)
        gated = seq(_sgu_fwd, c, f"sgu_fwd{l}", a, ln_g_full[l], wsp16[l], bsp_t[l])
        if l == 0:
            (wout,) = arrive(1)
        h1 = seq(_mm_res, c, f"a_out_fwd{l}", gated, wout, c.ar, wout_i, h)
        if l == 0:
            w1, rows = arrive(2)
        if l == LA - 1:
            forward(LA + 2)
        p, h2, hnf = seq(_ffn_fwd, c, f"ffn_fwd{l}", h1, ffn_g[l], w1, w1_i, rows)
        sav_a.append((h, z, a, hn, gated, h1, p, hnf))
        wts_a.append((w_in, in_i, w1, w1_i, rows, wout, wout_i))
        h = h2
    h_kv = h
    for l in range(LB):
        got = arrive(LA + 2 + l)
        if l == 0:
            wkv, got = got[0], got[1:]
            kv, hkv = seq(_rms_mm_rows, c, "kv_fwd", h, kv_g, wkv, c.kr, 0, 2 * c.DKV)
        wq, w1, rows = got
        wq = _slot_cols(wq)
        wo = _slot_rows(rows[:, c.fr:, :])
        q, hn = seq(_rms_mm_rows, c, f"q_fwd{l}", h, mix_g[LA + l], wq, c.qr, 0, c.DQ)
        o = seq(_attn_fwd, c, f"attn_fwd{l}", q, kv, bias, b_sinks[l])
        h1 = seq(_mm_res, c, f"o_fwd{l}", o, wo, c.orr, 0, h)
        if l + 1 < LB:
            forward(LA + 3 + l)
        p, h2, hnf = seq(_ffn_fwd, c, f"ffn_fwd{LA + l}", h1, ffn_g[LA + l], w1, 0, rows)
        sav_b.append((h, q, hn, o, h1, p, hnf))
        wts_b.append((wq, wo, w1, rows))
        h = h2
    dh, d_fin_g, loss_row = seq(_final_loss, c, h, fin_g, loss_target.reshape(S, D))
    loss = lax.psum(loss_row[0, 0], MESH_AXES)

    results = {}
    in_flight = []

    def update(k, parts, layer, col_blk=0):
        w = weights[k]
        rows_l, ncols = (w.shape[-2], w.shape[-1]) if w.ndim == 3 else w.shape
        flat = lambda t: t.reshape(-1, ncols)
        tr = min(256, rows_l)
        results[k] = seq(_adamw, f"adamw_{k}{layer}", parts, (NDEV, tr, ncols), lambda i: (0, i, col_blk),
                         flat(w), flat(m_in[k]), flat(v_in[k]), tr, row_off=layer * rows_l, n_rows=rows_l,
                         prev=results.get(k))

    def land(tag, entry):
        lands = seq(_send_wait, f"grads_wait_{tag}", entry[1], "exchange")
        for keys, parts in zip(entry[0], lands):
            for k, layer, col_blk in keys:
                update(k, parts, layer, col_blk)

    def send(tag, items):
        slabs = [t for _, t in items]
        own = [_landing(lax.dynamic_index_in_dim(t, me, 0, keepdims=False), me) for t in slabs]
        st = seq(_send_start, f"grads_start_{tag}", slabs, own, "exchange")
        in_flight.append((tag, ([k for k, _ in items], st)))
        while len(in_flight) > EXCHANGE_LAG:
            land(*in_flight.pop(0))

    d_mix_g, d_ffn_g = [None] * LF, [None] * LF
    dkv_list, dbias_list, dsink_list = [], [], [None] * LB

    def ffn_bwd(lf, dh, h1, p, hnf, w1, w1_i, rows):
        da, dh1, d_ffn_g[lf], dhb = seq(_ffn_bwd_data, c, f"ffn_bwd_data{lf}", dh, p, w1, w1_i, rows, h1, ffn_g[lf])
        dw1, dw2 = seq(_ffn_bwd_w, c, f"ffn_bwd_w{lf}", hnf, da, p, dhb)
        send(f"ffn{lf}", [([("ffn_w1", lf, 0)], dw1), ([("ffn_w2", lf, 0)], dw2)])
        return dh1

    for l in reversed(range(LB)):
        h0, q, hn, o, h1, p, hnf = sav_b[l]
        wq, wo, w1, rows = wts_b[l]
        dh1 = ffn_bwd(LA + l, dh, h1, p, hnf, w1, 0, rows)
        do = seq(_bwd_rows_data, c, f"o_bwd_data{l}", dh1, wo, c.orr, 0)
        dwo = _unslot_rows(seq(_wgrad_rows, c, f"o_bwd_w{l}", o, [dh1], c.DQ, D))
        dq, dkv, dbias, dsink = seq(_attn_bwd, c, f"attn_bwd{l}", q, kv, do, bias, b_sinks[l])
        dsink_list[l] = dsink.reshape(Q_PER_KV, 2, KV_PAIRS).transpose(2, 1, 0).reshape(1, N_HEADS)
        dkv_list.append(dkv)
        dbias_list.append(_unslot_bias(dbias))
        dwq = _unslot_cols(seq(_wgrad_rows, c, f"q_bwd_w{l}", hn, [dq], D, c.DQ))
        send(f"attn{l}", [([("b_w_o", l, 0)], dwo), ([("b_w_q", l, 0)], dwq)])
        dh, d_mix_g[LA + l] = seq(_bwd_rows_to_stream, c, f"q_bwd_data{l}", [dq], wq, c.qr, 0, c.DQ, h0,
                                  mix_g[LA + l], dh1)
    dwkv = seq(_wgrad_rows, c, "kv_bwd_w", hkv, dkv_list, D, 2 * c.DKV)
    send("kv", [([("w_k", 0, 0), ("w_v", 0, 1)], dwkv)])
    dh, d_kv_g = seq(_bwd_rows_to_stream, c, "kv_bwd_data", dkv_list, wkv, c.kr, 0, 2 * c.DKV, h_kv, kv_g, dh)
    d_rel_t = seq(_band_bias_grad, dbias_list, onehot)
    d_wsp, d_bsp, d_lng = [None] * LA, [None] * LA, [None] * LA
    for l in reversed(range(LA)):
        h0, z, a, hn, gated, h1, p, hnf = sav_a[l]
        w_in, in_i, w1, w1_i, rows, wout, wout_i = wts_a[l]
        dh1 = ffn_bwd(l, dh, h1, p, hnf, w1, w1_i, rows)
        dgated = seq(_bwd_rows_data, c, f"a_out_bwd_data{l}", dh1, wout, c.ar, wout_i)
        dwout = seq(_wgrad_rows, c, f"a_out_bwd_w{l}", gated, [dh1], c.AW, D)
        send(f"a_out{l}", [([("a_w_out", l, 0)], dwout)])
        dz, d_wsp[l], dbt, d_lng[l] = seq(_sgu_bwd, c, f"sgu_bwd{l}", a, z, dgated, ln_g_full[l], wsp16[l],
                                          wsp16_t[l], bsp_t[l])
        d_bsp[l] = dbt.T
        dwin = seq(_wgrad_cols, c, f"a_in_bwd_w{l}", hn, dz)
        send(f"a_in{l}", [([("a_w_in", l, 0)], dwin)])
        dh, d_mix_g[l] = seq(_bwd_cols_to_stream, c, f"a_in_bwd_data{l}", dz, w_in, in_i, h0, mix_g[l], dh1)
    grad_x = dh.reshape(1, S, D)

    small = {
        "mix_norm_g": jnp.concatenate(d_mix_g, axis=0), "ffn_norm_g": jnp.concatenate(d_ffn_g, axis=0),
        "a_w_spatial": jnp.stack(d_wsp), "a_b_spatial": jnp.stack(d_bsp), "kv_norm_g": d_kv_g,
        "b_sinks": jnp.concatenate(dsink_list, axis=0), "rel_bias": d_rel_t.T, "final_norm_g": d_fin_g,
    }
    small_names = list(small)
    packs = [_rows128(small[k]) for k in small_names] + [_rows128(jnp.concatenate(d_lng, axis=0))]
    offs = np.cumsum([0] + [p.shape[0] for p in packs])
    tail_rows = int(-offs[-1] % (8 * NDEV)) + packs[-1].shape[0]
    Rs = int(offs[-2]) + tail_rows
    packed = jnp.concatenate(packs + [jnp.zeros((Rs - int(offs[-1]), LANES), F32)], axis=0)
    slab = packed.reshape(NDEV, Rs // NDEV, LANES)
    st = seq(_send_start, "small_grads_start", [slab],
             [_landing(lax.dynamic_index_in_dim(slab, me, 0, keepdims=False), me)], "exchange")
    while in_flight:
        land(*in_flight.pop(0))
    (parts,) = seq(_send_wait, "small_grads_wait", st, "exchange")
    mine = seq(_sum_parts, "small_grads_sum", parts)
    st = seq(_send_start, "small_sums_start", [mine], [_landing(mine, me)], "broadcast")
    (sums,) = seq(_send_wait, "small_sums_wait", st, "broadcast")
    small_all = sums.reshape(1, Rs, LANES)

    grads, deltas, new_m, new_v = {}, {}, {}, {}

    def put(k, outs, shape):
        grads[k], deltas[k], new_m[k], new_v[k] = (t.reshape(shape) for t in outs)

    def pack_state(d):
        return jnp.concatenate([_rows128(d[k]) for k in small_names] + [jnp.zeros((tail_rows, LANES), F32)], axis=0)

    outs = seq(_adamw, "adamw_small", small_all, (1, Rs, LANES), lambda i: (0, 0, 0),
               pack_state(weights), pack_state(m_in), pack_state(v_in), Rs)
    for n_, k in enumerate(small_names):
        shape = weights[k].shape
        size = int(np.prod(shape))
        put(k, [t[int(offs[n_]):int(offs[n_ + 1])].reshape(-1)[:size] for t in outs], shape)
    lng_sum = outs[0][int(offs[-2]):int(offs[-1])].reshape(-1)[:LA * c.AW].reshape(LA, c.AW)
    lng_mine = lax.dynamic_slice_in_dim(lng_sum, me * c.ar, c.ar, axis=1)
    lng_parts = jnp.concatenate([lng_mine[None], jnp.zeros((NDEV - 1, LA, c.ar), F32)], axis=0)
    put("a_ln_g", seq(_adamw, "adamw_ln_g", lng_parts, (NDEV, LA, c.ar), lambda i: (0, 0, 0),
                      a_ln_g, m_in["a_ln_g"], v_in["a_ln_g"], LA), a_ln_g.shape)
    for k in ("a_w_in", "ffn_w1", "ffn_w2", "a_w_out", "b_w_o", "b_w_q", "w_k", "w_v"):
        put(k, results[k], weights[k].shape)

    return (loss, grad_x, *[grads[k] for k in names], *[deltas[k] for k in names],
            *[new_m[k] for k in names], *[new_v[k] for k in names])
```

```python
import numpy as np
import math
import jax
import jax.numpy as jnp
from jax import lax
from jax.experimental import pallas as pl
from jax.experimental.pallas import tpu as pltpu

F32 = jnp.float32
BF16 = jnp.bfloat16

NDEV = 8
EPS = 1e-6
CHUNK = 128
A_GROUPS = 8
N_HEADS = 16
N_KV_HEADS = 4
Q_PER_KV = N_HEADS // N_KV_HEADS
HEAD_DIM = 64
BLOCK = 128
N_BUCKETS = 32
MAX_DISTANCE = 128
ADAM_LR, ADAM_B1, ADAM_B2, ADAM_EPS, ADAM_WD, ADAM_STEP = 0.001, 0.9, 0.999, 1e-08, 0.01, 10
LANES = 128
VMEM_LIMIT = 56 * 1024 * 1024
INV_SQRT2 = 0.7071067811865476
INV_SQRT_2PI = 0.3989422804014327
MESH_AXES = ("x", "y", "c")
EXCHANGE_LAG = 2

HBM = pl.BlockSpec(memory_space=pltpu.HBM)
SMEM = pl.BlockSpec(memory_space=pltpu.SMEM)
ANY = pl.BlockSpec(memory_space=pl.ANY)
SEM = pl.BlockSpec(memory_space=pltpu.SEMAPHORE)
MESH = pl.DeviceIdType.MESH


def _params(n_grid):
    return pltpu.CompilerParams(dimension_semantics=("arbitrary",) * n_grid, vmem_limit_bytes=VMEM_LIMIT)


def _const(block, index_map):
    return pl.BlockSpec(block, index_map, pipeline_mode=pl.Buffered(1))


def _pcall(body, *, ins, in_specs, dep=None, **kw):
    n_in = len(ins)
    if dep is None or any(dep is t for t in ins):
        return pl.pallas_call(body, in_specs=list(in_specs), **kw)(*ins)

    def with_dep(*refs):
        body(*refs[:n_in], *refs[n_in + 1:])

    return pl.pallas_call(with_dep, in_specs=[*in_specs, ANY], **kw)(*ins, dep)


class _Seq:
    def __init__(self):
        self.last = None

    def __call__(self, fn, *args, **kw):
        out = fn(*args, dep=self.last, **kw)
        self.last = out[0] if isinstance(out, (tuple, list)) else out
        return out


def _rstd(h):
    return lax.rsqrt(jnp.mean(h * h, axis=-1, keepdims=True) + EPS)


def _rms_bwd(dhn, h, g, dres):
    r = _rstd(h)
    xh = h * r
    dg = jnp.sum(dhn * xh, axis=0, keepdims=True)
    dxh = dhn * g
    dx = r * (dxh - xh * jnp.mean(dxh * xh, axis=-1, keepdims=True))
    return dres + dx, dg


def _gelu(z):
    return 0.5 * z * (1.0 + lax.erf(z * INV_SQRT2))


def _gelu_grad(z):
    return 0.5 * (1.0 + lax.erf(z * INV_SQRT2)) + z * (jnp.exp(-0.5 * z * z) * INV_SQRT_2PI)


def _dot(a, b, dims):
    return lax.dot_general(a, b, (dims, ((), ())), preferred_element_type=F32)


NN = ((1,), (0,))
NT = ((1,), (1,))
TN = ((0,), (0,))


def _mm(name, ins, in_specs, out_shapes, out_specs, *, grid, dims, nk, acc_shape, load_a, load_b, epilogue,
        dep=None):
    n_in, n_out = len(ins), len(out_shapes)
    kax = len(grid) - 1

    def body(*refs):
        in_refs = refs[:n_in]
        out_refs = refs[n_in:n_in + n_out]
        a = load_a(in_refs, out_refs)
        b = load_b(in_refs)
        prod = _dot(a, b, dims)
        if nk == 1:
            epilogue(prod, in_refs, out_refs)
        else:
            acc = refs[n_in + n_out]
            k = pl.program_id(kax)

            @pl.when(k == 0)
            def _():
                acc[...] = prod

            @pl.when(k > 0)
            def _():
                acc[...] += prod

            @pl.when(k == nk - 1)
            def _():
                epilogue(acc[...], in_refs, out_refs)

    return _pcall(
        body, name=name, ins=ins, in_specs=in_specs, dep=dep, grid=grid, out_specs=out_specs, out_shape=out_shapes,
        scratch_shapes=[pltpu.VMEM(acc_shape, F32)] if nk > 1 else [], compiler_params=_params(len(grid)))


def _bf(ref_idx):
    return lambda in_refs, *_: in_refs[ref_idx][...].astype(BF16)


def _b_view(ref_idx, rows):
    def load(in_refs):
        b = in_refs[ref_idx][...]
        return b.reshape(rows, b.shape[-1])
    return load


class Cfg:
    pass


def _config(x, a_w_in, a_w_out, w_k, b_w_q, b_w_o, ffn_w1, ffn_w2):
    c = Cfg()
    c.S, c.D = x.shape[1], x.shape[2]
    c.LA, _, c.cw = a_w_in.shape
    c.AW2 = NDEV * c.cw
    c.AW = c.AW2 // 2
    c.gd = c.AW // A_GROUPS
    c.ar = a_w_out.shape[1]
    c.LF, _, c.fw = ffn_w1.shape
    c.fr = ffn_w2.shape[1]
    c.LB, c.qr, c.DQ = b_w_q.shape
    c.orr = b_w_o.shape[1]
    c.kr, c.DKV = w_k.shape
    c.tm = min(1024, c.S)
    c.tmw = min(512, c.S)
    c.tms = min(256, c.S)
    c.nb = c.S // BLOCK
    assert c.cw == c.fw == c.fr and c.AW == NDEV * c.ar and c.D == NDEV * c.qr == NDEV * c.kr
    assert c.DQ == NDEV * c.orr == N_HEADS * HEAD_DIM and c.DKV == N_KV_HEADS * HEAD_DIM
    assert c.S % c.tm == 0 and c.S % c.tmw == 0 and c.S % c.tms == 0 and c.tms % CHUNK == 0 and c.gd % LANES == 0
    assert c.fr % c.ar == 0 and c.fr % c.orr == 0
    assert c.LA >= 1 and c.LB >= 1 and c.LF == c.LA + c.LB
    return c


def _cached_rms(h_idx, g_idx, hn_out_idx, jax_axis=1):
    def load(in_refs, out_refs):
        hn_ref = out_refs[hn_out_idx]

        @pl.when(pl.program_id(jax_axis) == 0)
        def _():
            h = in_refs[h_idx][...]
            hn_ref[...] = (h * _rstd(h) * in_refs[g_idx][...]).astype(BF16)

        return hn_ref[...]
    return load


def _a_in_fwd(c, name, h, g, col, ci, dep=None):
    S, D, cw, tm = c.S, c.D, c.cw, c.tmw

    def body(h_ref, g_ref, w_ref, z_ref, a_ref, hn_ref):
        h = h_ref[...]
        hn = (h * _rstd(h) * g_ref[...]).astype(BF16)
        hn_ref[...] = hn
        for j in range(NDEV):
            cols = slice(j * cw, (j + 1) * cw)
            z = _dot(hn, w_ref[j], NT)
            z_ref[:, cols] = z.astype(BF16)
            a_ref[:, cols] = _gelu(z).astype(BF16)

    row = pl.BlockSpec((tm, D), lambda i: (i, 0))
    wide = pl.BlockSpec((tm, c.AW2), lambda i: (i, 0))
    return _pcall(
        body, name=name, ins=[h, g, col], dep=dep, grid=(S // tm,),
        in_specs=[row, pl.BlockSpec((1, D), lambda i: (0, 0)), _const((NDEV, None, cw, D), lambda i: (0, ci, 0, 0))],
        out_specs=[wide, wide, row],
        out_shape=[jax.ShapeDtypeStruct((S, c.AW2), BF16), jax.ShapeDtypeStruct((S, c.AW2), BF16),
                   jax.ShapeDtypeStruct((S, D), BF16)],
        compiler_params=_params(1))


def _rms_mm_rows(c, name, h, g, slab, blk_rows, blk_idx, n_out, dep=None):
    S, D, tm = c.S, c.D, c.tm

    def epilogue(acc, in_refs, out_refs):
        out_refs[0][...] = acc.astype(BF16)

    return _mm(
        name, [h, slab, g],
        [pl.BlockSpec((tm, D), lambda i, j, k: (i, 0)),
         pl.BlockSpec((NDEV, blk_rows, n_out), lambda i, j, k: (0, blk_idx, 0)),
         pl.BlockSpec((1, D), lambda i, j, k: (0, 0))],
        [jax.ShapeDtypeStruct((S, n_out), BF16), jax.ShapeDtypeStruct((S, D), BF16)],
        [pl.BlockSpec((tm, n_out), lambda i, j, k: (i, 0)), pl.BlockSpec((tm, D), lambda i, j, k: (i, 0))],
        grid=(S // tm, 1, 1), dims=NN, nk=1, acc_shape=None,
        load_a=_cached_rms(0, 2, 1), load_b=_b_view(1, NDEV * blk_rows), epilogue=epilogue, dep=dep)


def _mm_res(c, name, a, slab, blk_rows, blk_idx, res, dep=None):
    S, D, tm = c.S, c.D, c.tm
    K = NDEV * blk_rows

    def epilogue(acc, in_refs, out_refs):
        out_refs[0][...] = in_refs[2][...] + acc

    return _mm(
        name, [a, slab, res],
        [pl.BlockSpec((tm, K), lambda i, j, k: (i, 0)),
         pl.BlockSpec((NDEV, blk_rows, D), lambda i, j, k: (0, blk_idx, 0)),
         pl.BlockSpec((tm, D), lambda i, j, k: (i, 0))],
        [jax.ShapeDtypeStruct((S, D), F32)], [pl.BlockSpec((tm, D), lambda i, j, k: (i, 0))],
        grid=(S // tm, 1, 1), dims=NN, nk=1, acc_shape=None,
        load_a=_bf(0), load_b=_b_view(1, K), epilogue=epilogue, dep=dep)[0]


def _sgu_masks():
    ii = lax.broadcasted_iota(jnp.int32, (CHUNK, CHUNK), 0)
    jj = lax.broadcasted_iota(jnp.int32, (CHUNK, CHUNK), 1)
    return ii >= jj


def _sgu_fwd(c, name, a, ln_g, wc, b_t, dep=None):
    S, AW, gd, tm = c.S, c.AW, c.gd, c.tms

    def body(a_ref, lng_ref, wc_ref, bt_ref, out_ref):
        va = a_ref[:, AW:].astype(F32)
        xc = va - jnp.mean(va, axis=-1, keepdims=True)
        vn = (xc * lax.rsqrt(jnp.mean(xc * xc, axis=-1, keepdims=True) + EPS) * lng_ref[...]).astype(BF16)
        for ch in range(tm // CHUNK):
            rows = slice(ch * CHUNK, (ch + 1) * CHUNK)
            for g in range(A_GROUPS):
                cols = slice(g * gd, (g + 1) * gd)
                mixed = _dot(wc_ref[g], vn[rows, cols], NN) + bt_ref[:, g:g + 1]
                out_ref[rows, cols] = (a_ref[rows, cols].astype(F32) * mixed).astype(BF16)

    return _pcall(
        body, name=name, ins=[a, ln_g, wc, b_t], dep=dep, grid=(S // tm,),
        in_specs=[pl.BlockSpec((tm, 2 * AW), lambda i: (i, 0)), pl.BlockSpec((1, AW), lambda i: (0, 0)),
                  pl.BlockSpec((A_GROUPS, CHUNK, CHUNK), lambda i: (0, 0, 0)),
                  pl.BlockSpec((CHUNK, A_GROUPS), lambda i: (0, 0))],
        out_specs=pl.BlockSpec((tm, AW), lambda i: (i, 0)),
        out_shape=jax.ShapeDtypeStruct((S, AW), BF16), compiler_params=_params(1))


def _ffn_fwd(c, name, h, g, col, ci, rows, dep=None):
    S, D, fw, tm = c.S, c.D, c.fw, c.tmw
    F = NDEV * fw

    def body(h_ref, g_ref, w1_ref, w2_ref, p_ref, out_ref, hn_ref, r_ref):
        h = h_ref[...]
        hn = (h * _rstd(h) * g_ref[...]).astype(BF16)
        hn_ref[...] = hn
        for j in range(NDEV):
            cols = slice(j * fw, (j + 1) * fw)
            p = jnp.maximum(_dot(hn, w1_ref[j], NT), 0.0)
            p_ref[:, cols] = p.astype(BF16)
            r_ref[:, cols] = (p * p).astype(BF16)
        out_ref[...] = h + _dot(r_ref[...], w2_ref[...].reshape(F, D), NN)

    row = pl.BlockSpec((tm, D), lambda i: (i, 0))
    return _pcall(
        body, name=name, ins=[h, g, col, rows], dep=dep, grid=(S // tm,),
        in_specs=[row, pl.BlockSpec((1, D), lambda i: (0, 0)),
                  _const((NDEV, None, fw, D), lambda i: (0, ci, 0, 0)), _const((NDEV, c.fr, D), lambda i: (0, 0, 0))],
        out_specs=[pl.BlockSpec((tm, F), lambda i: (i, 0)), row, row],
        out_shape=[jax.ShapeDtypeStruct((S, F), BF16), jax.ShapeDtypeStruct((S, D), F32),
                   jax.ShapeDtypeStruct((S, D), BF16)],
        scratch_shapes=[pltpu.VMEM((tm, F), BF16)], compiler_params=_params(1))


def _bucket_table():
    qi = np.arange(BLOCK)[:, None]
    kj = np.arange(2 * BLOCK)[None, :]
    d = np.maximum(qi + BLOCK - kj, 0)
    max_exact = N_BUCKETS // 2
    ratio = np.log(np.maximum(d, 1).astype(np.float32) / np.float32(max_exact)) / np.float32(
        math.log(MAX_DISTANCE / max_exact))
    large = np.minimum(max_exact + (ratio.astype(np.float32) * np.float32(N_BUCKETS - max_exact)).astype(np.int32),
                       N_BUCKETS - 1)
    return np.where(d < max_exact, d, large).astype(np.int32)


def _bucket_onehot():
    b = jnp.asarray(_bucket_table().reshape(1, -1))
    return (b == lax.broadcasted_iota(jnp.int32, (N_BUCKETS, b.shape[1]), 0)).astype(F32)


def _whole(t):
    return pl.BlockSpec(t.shape, lambda: (0,) * t.ndim)


def _band_bias(rel_bias_t, onehot, dep=None):
    def body(r_ref, oh_ref, out_ref):
        out_ref[...] = lax.dot_general(r_ref[...], oh_ref[...], (NN, ((), ())), preferred_element_type=F32,
                                       precision=lax.Precision.HIGHEST)

    n = onehot.shape[1]
    return _pcall(body, name="band_bias", ins=[rel_bias_t, onehot], in_specs=[_whole(rel_bias_t), _whole(onehot)],
                  dep=dep, out_shape=jax.ShapeDtypeStruct((N_HEADS, n), F32), compiler_params=_params(0))


def _band_bias_grad(dbias_list, onehot, dep=None):
    n_in = len(dbias_list)

    def body(*refs):
        oh_ref, out_ref = refs[n_in], refs[n_in + 1]
        d = refs[0][...]
        for r in refs[1:n_in]:
            d = d + r[...]
        out_ref[...] = lax.dot_general(d, oh_ref[...], (NT, ((), ())), preferred_element_type=F32,
                                       precision=lax.Precision.HIGHEST)

    ins = [*dbias_list, onehot]
    return _pcall(body, name="band_bias_grad", ins=ins, in_specs=[_whole(t) for t in ins], dep=dep,
                  out_shape=jax.ShapeDtypeStruct((N_HEADS, N_BUCKETS), F32), compiler_params=_params(0))


KV_PAIRS = N_KV_HEADS // 2
PAIR_ROWS = 2 * Q_PER_KV * BLOCK
MASKED = float(np.finfo(np.float32).min) / 2


def _slot_cols(w):
    lead = w.shape[:-1]
    return w.reshape(*lead, KV_PAIRS, 2, Q_PER_KV, HEAD_DIM).swapaxes(-3, -2).reshape(*lead, N_HEADS * HEAD_DIM)


def _unslot_cols(w):
    lead = w.shape[:-1]
    return w.reshape(*lead, KV_PAIRS, Q_PER_KV, 2, HEAD_DIM).swapaxes(-3, -2).reshape(*lead, N_HEADS * HEAD_DIM)


def _slot_rows(blocks):
    n = blocks.shape[-1]
    return blocks.reshape(KV_PAIRS, 2, Q_PER_KV, HEAD_DIM, n).swapaxes(1, 2).reshape(blocks.shape)


def _unslot_rows(blocks):
    n = blocks.shape[-1]
    return blocks.reshape(KV_PAIRS, Q_PER_KV, 2, HEAD_DIM, n).swapaxes(1, 2).reshape(blocks.shape)


def _slot_bias(bias):
    qi = np.arange(BLOCK)[:, None]
    kj = np.arange(2 * BLOCK)[None, :]
    dist = qi + BLOCK - kj
    window = (dist >= 0) & (dist < BLOCK)
    b = bias.reshape(KV_PAIRS, 2, Q_PER_KV, BLOCK, 2 * BLOCK).swapaxes(1, 2).reshape(KV_PAIRS, PAIR_ROWS, 2 * BLOCK)
    tile = lambda mk: jnp.asarray(np.tile(mk, (2 * Q_PER_KV, 1)))[None]
    return jnp.stack([jnp.where(tile(window & (kj >= BLOCK)), b, MASKED), jnp.where(tile(window), b, MASKED)])


def _unslot_bias(db):
    return db.reshape(KV_PAIRS, Q_PER_KV, 2, BLOCK, 2 * BLOCK).swapaxes(1, 2).reshape(N_HEADS, -1)


def _pair_kv(kvc_ref, kvp_ref, kvp, dkv):
    lanes = slice(kvp * LANES, (kvp + 1) * LANES)
    vlanes = slice(dkv + kvp * LANES, dkv + (kvp + 1) * LANES)
    k2 = jnp.concatenate([kvp_ref[:, lanes], kvc_ref[:, lanes]], axis=0)
    v2 = jnp.concatenate([kvp_ref[:, vlanes], kvc_ref[:, vlanes]], axis=0)
    return k2, v2


def _head_operand(ref, grp, par, low, scale=None):
    xg = ref[:, grp * LANES:(grp + 1) * LANES]
    if scale is not None:
        xg = xg * scale
    zero = jnp.zeros_like(xg)
    return jnp.where(low, xg, zero) if par == 0 else jnp.where(low, zero, xg)


def _head_probs(qh, k2, bias_rows, sink):
    s = _dot(qh, k2, NT) + bias_rows
    m = jnp.maximum(jnp.max(s, axis=-1, keepdims=True), sink)
    e = jnp.exp(s - m)
    es = jnp.exp(sink - m)
    inv = 1.0 / (jnp.sum(e, axis=-1, keepdims=True) + es)
    return e * inv, es * inv


def _attn_specs(c):
    dq, dkv2 = c.DQ, 2 * c.DKV
    return [pl.BlockSpec((BLOCK, dq), lambda n: (n, 0)),
            pl.BlockSpec((BLOCK, dkv2), lambda n: (n, 0)),
            pl.BlockSpec((BLOCK, dkv2), lambda n: (jnp.maximum(n - 1, 0), 0))]


def _bias_spec():
    return pl.BlockSpec((None, KV_PAIRS, PAIR_ROWS, 2 * BLOCK), lambda n: (jnp.minimum(n, 1), 0, 0, 0))


def _low_lanes():
    return lax.broadcasted_iota(jnp.int32, (BLOCK, LANES), 1) < HEAD_DIM


def _attn_fwd(c, name, q, kv, bias, sinks, dep=None):
    S, dq = c.S, c.DQ

    def body(q_ref, kvc_ref, kvp_ref, bias_ref, sink_ref, o_ref):
        low = _low_lanes()
        for kvp in range(KV_PAIRS):
            k2, v2 = _pair_kv(kvc_ref, kvp_ref, kvp, c.DKV)
            for g in range(Q_PER_KV):
                grp = kvp * Q_PER_KV + g
                halves = []
                for par in range(2):
                    r = 2 * g + par
                    qh = _head_operand(q_ref, grp, par, low, scale=HEAD_DIM ** -0.5)
                    p, _ = _head_probs(qh, k2, bias_ref[kvp, r * BLOCK:(r + 1) * BLOCK, :],
                                       sink_ref[(2 * kvp + par) * Q_PER_KV + g])
                    halves.append(_dot(p.astype(BF16), v2, NN))
                o_ref[:, grp * LANES:(grp + 1) * LANES] = jnp.where(low, halves[0], halves[1]).astype(BF16)

    return _pcall(
        body, name=name, ins=[q, kv, kv, bias, sinks], dep=dep, grid=(c.nb,),
        in_specs=_attn_specs(c) + [_bias_spec(), SMEM],
        out_specs=pl.BlockSpec((BLOCK, dq), lambda n: (n, 0)),
        out_shape=jax.ShapeDtypeStruct((S, dq), BF16), compiler_params=_params(1))


def _final_loss(c, h, g, target, dep=None):
    S, D, tm = c.S, c.D, c.tm

    def body(h_ref, g_ref, t_ref, dh_ref, dg_ref, loss_ref):
        i = pl.program_id(0)
        h = h_ref[...]
        gg = g_ref[...]
        r = _rstd(h)
        xh = h * r
        err = xh * gg - t_ref[...]
        lp = jnp.sum(jnp.sum(err * err, axis=1, keepdims=True), axis=0, keepdims=True) * (0.5 / D)
        dx, dg = _rms_bwd(err * (1.0 / D), h, gg, 0.0)
        dh_ref[...] = dx

        @pl.when(i == 0)
        def _():
            dg_ref[...] = dg
            loss_ref[...] = jnp.broadcast_to(lp, loss_ref.shape)

        @pl.when(i > 0)
        def _():
            dg_ref[...] += dg
            loss_ref[...] += jnp.broadcast_to(lp, loss_ref.shape)

    row = pl.BlockSpec((tm, D), lambda i: (i, 0))
    return _pcall(
        body, name="final_loss", ins=[h, g, target], dep=dep, grid=(S // tm,),
        in_specs=[row, pl.BlockSpec((1, D), lambda i: (0, 0)), row],
        out_specs=[row, pl.BlockSpec((1, D), lambda i: (0, 0)), pl.BlockSpec((1, LANES), lambda i: (0, 0))],
        out_shape=[jax.ShapeDtypeStruct((S, D), F32), jax.ShapeDtypeStruct((1, D), F32),
                   jax.ShapeDtypeStruct((1, LANES), F32)],
        compiler_params=_params(1))


def _rms_bwd_epilogue(h_idx, g_idx, res_idx):
    def epilogue(dhn, in_refs, out_refs):
        dh, dg = _rms_bwd(dhn, in_refs[h_idx][...], in_refs[g_idx][...], in_refs[res_idx][...])
        out_refs[0][...] = dh
        i = pl.program_id(0)

        @pl.when(i == 0)
        def _():
            out_refs[1][...] = dg

        @pl.when(i > 0)
        def _():
            out_refs[1][...] += dg
    return epilogue


def _stream_outs(c, tm):
    S, D = c.S, c.D
    return ([jax.ShapeDtypeStruct((S, D), F32), jax.ShapeDtypeStruct((1, D), F32)],
            [pl.BlockSpec((tm, D), lambda i, j, k: (i, 0)), pl.BlockSpec((1, D), lambda i, j, k: (0, 0))])


def _row_specs(c, tm):
    D = c.D
    return [pl.BlockSpec((tm, D), lambda i, j, k: (i, 0)), pl.BlockSpec((1, D), lambda i, j, k: (0, 0)),
            pl.BlockSpec((tm, D), lambda i, j, k: (i, 0))]


def _bwd_rows_to_stream(c, name, dy_list, slab, blk_rows, blk_idx, n_in_cols, h, g, dres, dep=None):
    S, D, tm = c.S, c.D, c.tm
    nd = len(dy_list)

    def load_a(in_refs, out_refs):
        a = in_refs[0][...]
        for r in in_refs[1:nd]:
            a = a + r[...]
        return a.astype(BF16)

    shapes, specs = _stream_outs(c, tm)
    return _mm(
        name, [*dy_list, slab, h, g, dres],
        [pl.BlockSpec((tm, n_in_cols), lambda i, j, k: (i, 0))] * nd
        + [pl.BlockSpec((NDEV, blk_rows, n_in_cols), lambda i, j, k: (0, blk_idx, 0))] + _row_specs(c, tm),
        shapes, specs, grid=(S // tm, 1, 1), dims=NT, nk=1, acc_shape=None,
        load_a=load_a, load_b=_b_view(nd, NDEV * blk_rows), epilogue=_rms_bwd_epilogue(nd + 1, nd + 2, nd + 3),
        dep=dep)


def _bwd_cols_to_stream(c, name, dy, col, ci, h, g, dres, dep=None):
    S, D, cw, tm = c.S, c.D, c.cw, c.tmw
    K = NDEV * cw
    shapes, specs = _stream_outs(c, tm)
    return _mm(
        name, [dy, col, h, g, dres],
        [pl.BlockSpec((tm, K), lambda i, j, k: (i, 0)),
         _const((NDEV, None, cw, D), lambda i, j, k: (0, ci, 0, 0))] + _row_specs(c, tm),
        shapes, specs, grid=(S // tm, 1, 1), dims=NN, nk=1, acc_shape=None,
        load_a=_bf(0), load_b=_b_view(1, K), epilogue=_rms_bwd_epilogue(2, 3, 4), dep=dep)


def _bwd_rows_data(c, name, dy, slab, blk_rows, blk_idx, dep=None):
    S, D, tm = c.S, c.D, c.tm
    K = NDEV * blk_rows

    def epilogue(acc, in_refs, out_refs):
        out_refs[0][...] = acc.astype(BF16)

    return _mm(
        name, [dy, slab],
        [pl.BlockSpec((tm, D), lambda i, j, k: (i, 0)),
         pl.BlockSpec((NDEV, blk_rows, D), lambda i, j, k: (0, blk_idx, 0))],
        [jax.ShapeDtypeStruct((S, K), BF16)], [pl.BlockSpec((tm, K), lambda i, j, k: (i, 0))],
        grid=(S // tm, 1, 1), dims=NT, nk=1, acc_shape=None,
        load_a=_bf(0), load_b=_b_view(1, K), epilogue=epilogue, dep=dep)[0]


def _wgrad_rows(c, name, a, b_list, n_a, n_b, dep=None):
    S, tm = c.S, c.tm
    nb_in = len(b_list)
    blk_rows = n_a // NDEV

    def load_b(in_refs):
        b = in_refs[1][...]
        for r in in_refs[2:1 + nb_in]:
            b = b + r[...]
        return b.astype(BF16)

    def epilogue(acc, in_refs, out_refs):
        out_refs[0][...] = acc.reshape(NDEV, blk_rows, n_b).astype(BF16)

    return _mm(
        name, [a, *b_list],
        [pl.BlockSpec((tm, n_a), lambda i, j, k: (k, 0))] + [pl.BlockSpec((tm, n_b), lambda i, j, k: (k, 0))] * nb_in,
        [jax.ShapeDtypeStruct((NDEV, blk_rows, n_b), BF16)],
        [pl.BlockSpec((NDEV, blk_rows, n_b), lambda i, j, k: (0, 0, 0))],
        grid=(1, 1, S // tm), dims=TN, nk=S // tm, acc_shape=(n_a, n_b),
        load_a=_bf(0), load_b=load_b, epilogue=epilogue, dep=dep)[0]


def _wgrad_cols(c, name, a, b, dep=None):
    S, D, cw = c.S, c.D, c.cw

    def body(a_ref, b_ref, out_ref):
        out_ref[...] = _dot(a_ref[...], b_ref[...], TN).astype(BF16)

    return _pcall(
        body, name=name, ins=[a, b], dep=dep, grid=(NDEV,),
        in_specs=[_const((S, D), lambda j: (0, 0)), pl.BlockSpec((S, cw), lambda j: (0, j))],
        out_specs=pl.BlockSpec((None, D, cw), lambda j: (j, 0, 0)),
        out_shape=jax.ShapeDtypeStruct((NDEV, D, cw), BF16), compiler_params=_params(1))


def _ffn_bwd_data(c, name, dh, p, col, ci, rows, h, g, dep=None):
    S, D, fw, tm = c.S, c.D, c.fw, c.tmw
    F = NDEV * fw

    def body(dh_ref, p_ref, w1t_ref, w2_ref, h_ref, g_ref, da_ref, out_ref, dg_ref, dhb_ref):
        i = pl.program_id(0)
        dh = dh_ref[...]
        dhb = dh.astype(BF16)
        dhb_ref[...] = dhb
        for j in range(NDEV):
            cols = slice(j * fw, (j + 1) * fw)
            da_ref[:, cols] = (_dot(dhb, w2_ref[j], NT) * (2.0 * p_ref[:, cols].astype(F32))).astype(BF16)
        dx, dg = _rms_bwd(_dot(da_ref[...], w1t_ref[...].reshape(F, D), NN), h_ref[...], g_ref[...], dh)
        out_ref[...] = dx

        @pl.when(i == 0)
        def _():
            dg_ref[...] = dg

        @pl.when(i > 0)
        def _():
            dg_ref[...] += dg

    row = pl.BlockSpec((tm, D), lambda i: (i, 0))
    wide = pl.BlockSpec((tm, F), lambda i: (i, 0))
    return _pcall(
        body, name=name, ins=[dh, p, col, rows, h, g], dep=dep, grid=(S // tm,),
        in_specs=[row, wide, _const((NDEV, None, fw, D), lambda i: (0, ci, 0, 0)),
                  _const((NDEV, c.fr, D), lambda i: (0, 0, 0)),
                  row, pl.BlockSpec((1, D), lambda i: (0, 0))],
        out_specs=[wide, row, pl.BlockSpec((1, D), lambda i: (0, 0)), row],
        out_shape=[jax.ShapeDtypeStruct((S, F), BF16), jax.ShapeDtypeStruct((S, D), F32),
                   jax.ShapeDtypeStruct((1, D), F32), jax.ShapeDtypeStruct((S, D), BF16)],
        compiler_params=_params(1))


def _ffn_bwd_w(c, name, hn, da, p, dhb, dep=None):
    S, D, fw = c.S, c.D, c.fw

    def body(hn_ref, da_ref, p_ref, dhb_ref, dw1_ref, dw2_ref):
        dw1_ref[...] = _dot(hn_ref[...], da_ref[...], TN).astype(BF16)
        pf = p_ref[...].astype(F32)
        dw2_ref[...] = _dot((pf * pf).astype(BF16), dhb_ref[...], TN).astype(BF16)

    panel = pl.BlockSpec((S, fw), lambda j: (0, j))
    return _pcall(
        body, name=name, ins=[hn, da, p, dhb], dep=dep, grid=(NDEV,),
        in_specs=[_const((S, D), lambda j: (0, 0)), panel, panel, _const((S, D), lambda j: (0, 0))],
        out_specs=[pl.BlockSpec((None, D, fw), lambda j: (j, 0, 0)), pl.BlockSpec((None, c.fr, D), lambda j: (j, 0, 0))],
        out_shape=[jax.ShapeDtypeStruct((NDEV, D, fw), BF16), jax.ShapeDtypeStruct((NDEV, c.fr, D), BF16)],
        compiler_params=_params(1))


def _attn_bwd(c, name, q, kv, do, bias, sinks, dep=None):
    S, dq, dkv = c.S, c.DQ, c.DKV
    nb = c.nb
    scale = HEAD_DIM ** -0.5

    def body(q_ref, kvc_ref, kvp_ref, do_ref, bias_ref, sink_ref, dq_ref, dkv_ref, dbias_ref, dsink_ref, dsink_acc,
             ds_sc, p_sc, qm_sc, dom_sc):
        n = pl.program_id(0)

        @pl.when(n == 0)
        def _():
            dkv_ref[...] = jnp.zeros_like(dkv_ref)
            dbias_ref[...] = jnp.zeros_like(dbias_ref)
            dsink_acc[...] = jnp.zeros_like(dsink_acc)

        low = _low_lanes()
        rows_c = pl.ds(pl.multiple_of(n * BLOCK, BLOCK), BLOCK)
        rows_p = pl.ds(pl.multiple_of(jnp.maximum(n - 1, 0) * BLOCK, BLOCK), BLOCK)
        for kvp in range(KV_PAIRS):
            k2, v2 = _pair_kv(kvc_ref, kvp_ref, kvp, dkv)
            for g in range(Q_PER_KV):
                grp = kvp * Q_PER_KV + g
                halves = []
                for par in range(2):
                    rows = slice((2 * g + par) * BLOCK, (2 * g + par + 1) * BLOCK)
                    qh = _head_operand(q_ref, grp, par, low, scale=scale)
                    doh = _head_operand(do_ref, grp, par, low)
                    p, ps = _head_probs(qh, k2, bias_ref[kvp, rows, :], sink_ref[(2 * kvp + par) * Q_PER_KV + g])
                    dp = _dot(doh, v2, NT)
                    delta = jnp.sum(p * dp, axis=-1, keepdims=True)
                    ds = p * (dp - delta)
                    dbias_ref[kvp, rows, :] += ds
                    dsink_acc[rows, kvp:kvp + 1] += -(ps * delta)
                    ds16 = ds.astype(BF16)
                    halves.append(_dot(ds16, k2, NN) * scale)
                    ds_sc[rows, :] = ds16
                    p_sc[rows, :] = p.astype(BF16)
                    qm_sc[rows, :] = qh
                    dom_sc[rows, :] = doh
                dq_ref[:, grp * LANES:(grp + 1) * LANES] = jnp.where(low, halves[0], halves[1]).astype(BF16)
            dk2 = _dot(ds_sc[...], qm_sc[...], TN)
            dv2 = _dot(p_sc[...], dom_sc[...], TN)
            lanes = slice(kvp * LANES, (kvp + 1) * LANES)
            vlanes = slice(dkv + kvp * LANES, dkv + (kvp + 1) * LANES)
            dkv_ref[rows_p, lanes] += dk2[:BLOCK]
            dkv_ref[rows_c, lanes] += dk2[BLOCK:]
            dkv_ref[rows_p, vlanes] += dv2[:BLOCK]
            dkv_ref[rows_c, vlanes] += dv2[BLOCK:]

        @pl.when(n == nb - 1)
        def _():
            dsink_ref[...] = jnp.sum(dsink_acc[...].reshape(2 * Q_PER_KV, BLOCK, KV_PAIRS), axis=1)

    return _pcall(
        body, name=name, ins=[q, kv, kv, do, bias, sinks], dep=dep, grid=(nb,),
        in_specs=_attn_specs(c) + [pl.BlockSpec((BLOCK, dq), lambda n: (n, 0)), _bias_spec(), SMEM],
        out_specs=[pl.BlockSpec((BLOCK, dq), lambda n: (n, 0)), pl.BlockSpec((S, 2 * dkv), lambda n: (0, 0)),
                   pl.BlockSpec((KV_PAIRS, PAIR_ROWS, 2 * BLOCK), lambda n: (0, 0, 0)),
                   pl.BlockSpec((2 * Q_PER_KV, KV_PAIRS), lambda n: (0, 0))],
        out_shape=[jax.ShapeDtypeStruct((S, dq), BF16), jax.ShapeDtypeStruct((S, 2 * dkv), F32),
                   jax.ShapeDtypeStruct((KV_PAIRS, PAIR_ROWS, 2 * BLOCK), F32),
                   jax.ShapeDtypeStruct((2 * Q_PER_KV, KV_PAIRS), F32)],
        scratch_shapes=[pltpu.VMEM((PAIR_ROWS, KV_PAIRS), F32), pltpu.VMEM((PAIR_ROWS, 2 * BLOCK), BF16),
                        pltpu.VMEM((PAIR_ROWS, 2 * BLOCK), BF16), pltpu.VMEM((PAIR_ROWS, LANES), BF16),
                        pltpu.VMEM((PAIR_ROWS, LANES), BF16)],
        compiler_params=_params(1))


def _sgu_bwd(c, name, a, z, dgated, ln_g, wc, wc_t, b_t, dep=None):
    S, AW, gd, tm = c.S, c.AW, c.gd, c.tms

    def body(a_ref, z_ref, dg_ref, lng_ref, wc_ref, wct_ref, bt_ref, dz_ref, dws_ref, dbt_ref, dlng_ref, dvn_ref):
        i = pl.program_id(0)

        @pl.when(i == 0)
        def _():
            dws_ref[...] = jnp.zeros_like(dws_ref)
            dbt_ref[...] = jnp.zeros_like(dbt_ref)
            dlng_ref[...] = jnp.zeros_like(dlng_ref)

        lng = lng_ref[...]
        va = a_ref[:, AW:].astype(F32)
        xc = va - jnp.mean(va, axis=-1, keepdims=True)
        rstd = lax.rsqrt(jnp.mean(xc * xc, axis=-1, keepdims=True) + EPS)
        xh = xc * rstd
        vn = (xh * lng).astype(BF16)
        causal = _sgu_masks()
        for ch in range(tm // CHUNK):
            rows = slice(ch * CHUNK, (ch + 1) * CHUNK)
            for g in range(A_GROUPS):
                cols = slice(g * gd, (g + 1) * gd)
                blk = vn[rows, cols]
                mixed = _dot(wc_ref[g], blk, NN) + bt_ref[:, g:g + 1]
                dgb = dg_ref[rows, cols].astype(F32)
                dm = dgb * a_ref[rows, cols].astype(F32)
                dbt_ref[:, g:g + 1] += jnp.sum(dm, axis=1, keepdims=True)
                dm16 = dm.astype(BF16)
                dws_ref[g] += jnp.where(causal, _dot(dm16, blk, NT), 0.0)
                dvn_ref[rows, cols] = _dot(wct_ref[g], dm16, NN)
                dz_ref[rows, cols] = (dgb * mixed * _gelu_grad(z_ref[rows, cols].astype(F32))).astype(BF16)
        dvn = dvn_ref[...]
        dlng_ref[...] += jnp.sum(dvn * xh, axis=0, keepdims=True)
        dxh = dvn * lng
        dva = rstd * (dxh - jnp.mean(dxh, axis=-1, keepdims=True) - xh * jnp.mean(dxh * xh, axis=-1, keepdims=True))
        dz_ref[:, AW:] = (dva * _gelu_grad(z_ref[:, AW:].astype(F32))).astype(BF16)

    wide = pl.BlockSpec((tm, 2 * AW), lambda i: (i, 0))
    wsp = pl.BlockSpec((A_GROUPS, CHUNK, CHUNK), lambda i: (0, 0, 0))
    btsp = pl.BlockSpec((CHUNK, A_GROUPS), lambda i: (0, 0))
    return _pcall(
        body, name=name, ins=[a, z, dgated, ln_g, wc, wc_t, b_t], dep=dep, grid=(S // tm,),
        in_specs=[wide, wide, pl.BlockSpec((tm, AW), lambda i: (i, 0)), pl.BlockSpec((1, AW), lambda i: (0, 0)),
                  wsp, wsp, btsp],
        out_specs=[wide, wsp, btsp, pl.BlockSpec((1, AW), lambda i: (0, 0))],
        out_shape=[jax.ShapeDtypeStruct((S, 2 * AW), BF16), jax.ShapeDtypeStruct((A_GROUPS, CHUNK, CHUNK), F32),
                   jax.ShapeDtypeStruct((CHUNK, A_GROUPS), F32), jax.ShapeDtypeStruct((1, AW), F32)],
        scratch_shapes=[pltpu.VMEM((tm, AW), F32)], compiler_params=_params(1))


def _adamw(name, parts, part_block, part_index, w, m, v, tr, row_off=0, n_rows=None, prev=None, dep=None):
    R, C = w.shape
    n_rows = R if n_rows is None else n_rows
    assert n_rows % tr == 0 and row_off % tr == 0
    bc1 = 1.0 - ADAM_B1 ** ADAM_STEP
    bc2 = 1.0 - ADAM_B2 ** ADAM_STEP

    def body(p_ref, w_ref, m_ref, v_ref, *rest):
        g_ref, d_ref, nm_ref, nv_ref = rest[-4:]
        g = p_ref[0].astype(F32)
        for s in range(1, part_block[0]):
            g = g + p_ref[s].astype(F32)
        nm = ADAM_B1 * m_ref[...] + (1.0 - ADAM_B1) * g
        nv = ADAM_B2 * v_ref[...] + (1.0 - ADAM_B2) * (g * g)
        g_ref[...] = g
        nm_ref[...] = nm
        nv_ref[...] = nv
        d_ref[...] = -ADAM_LR * ((nm * (1.0 / bc1)) / (jnp.sqrt(nv * (1.0 / bc2)) + ADAM_EPS) + ADAM_WD * w_ref[...])

    ob = row_off // tr
    row = pl.BlockSpec((tr, C), lambda i: (ob + i, 0))
    out = jax.ShapeDtypeStruct((R, C), F32)
    chained = prev is not None
    return _pcall(
        body, name=name, ins=[parts, w, m, v] + (list(prev) if chained else []), dep=dep, grid=(n_rows // tr,),
        in_specs=[pl.BlockSpec(part_block, part_index), row, row, row] + ([ANY] * 4 if chained else []),
        out_specs=[row, row, row, row], out_shape=[out, out, out, out],
        input_output_aliases={4 + t: t for t in range(4)} if chained else {}, compiler_params=_params(1))


def _sum_parts(name, parts, dep=None):
    def body(p_ref, out_ref):
        g = p_ref[0]
        for s in range(1, parts.shape[0]):
            g = g + p_ref[s]
        out_ref[...] = g

    return _pcall(body, name=name, ins=[parts], in_specs=[_whole(parts)], dep=dep,
                  out_shape=jax.ShapeDtypeStruct(parts.shape[1:], F32), compiler_params=_params(0))


def _place():
    return lax.axis_index("x"), lax.axis_index("y"), lax.axis_index("c")


def _slot(px, py, pc):
    return 4 * px + 2 * py + pc


def _peer(k, x, y, c):
    return x ^ ((k >> 2) & 1), y ^ ((k >> 1) & 1), c ^ (k & 1)


SEND_PEERS = {"exchange": tuple(range(1, NDEV)), "gather": (1, 2, 4, 6), "forward": (2, 4, 6),
              "broadcast": tuple(range(1, NDEV))}


def _send_copies(mode, src_refs, land_refs, send_sems, recv_sems):
    x, y, c = _place()
    me = _slot(x, y, c)
    peers = SEND_PEERS[mode]
    copies = []
    for i, k in enumerate(peers):
        peer = _peer(k, x, y, c)
        for a, land in enumerate(land_refs):
            if mode == "exchange":
                src, dst, to = src_refs[a].at[_slot(*peer)], land.at[me], peer
            elif mode in ("gather", "broadcast"):
                src, dst, to = src_refs[a], land.at[me], peer
            else:
                src = dst = land.at[_slot(*peer)]
                to = (x, y, 1 - c)
            s = a * len(peers) + i
            copies.append(pltpu.make_async_remote_copy(src_ref=src, dst_ref=dst, send_sem=send_sems.at[s],
                                                       recv_sem=recv_sems.at[s], device_id=to, device_id_type=MESH))
    return copies


def _send_start_groups(name, groups, mode, dep=None):
    sizes = [(len(s), len(l)) for s, l in groups]
    flat = [t for s, l in groups for t in (*s, *l)]
    n_in, ng = len(flat), len(groups)

    def body(*refs):
        sems, token, at = refs[n_in:n_in + 2 * ng], refs[-1], 0
        for gi, (ns, nl) in enumerate(sizes):
            for cp in _send_copies(mode, refs[at:at + ns], refs[at + ns:at + ns + nl], sems[2 * gi], sems[2 * gi + 1]):
                cp.start()
            at += ns + nl
        token[...] = jnp.zeros_like(token)

    sem_shapes = [pltpu.SemaphoreType.DMA((nl * len(SEND_PEERS[mode]),)) for _, nl in sizes for _ in range(2)]
    if any(dep is t for t in flat):
        dep = None
    out = _pcall(
        body, name=name, ins=[pltpu.with_memory_space_constraint(t, pltpu.HBM) for t in flat],
        in_specs=[HBM] * n_in, dep=dep,
        out_shape=(*sem_shapes, *[pltpu.HBM(t.shape, t.dtype) for t in flat], jax.ShapeDtypeStruct((8, LANES), F32)),
        out_specs=(*[SEM] * (2 * ng), *[HBM] * n_in, pl.BlockSpec(memory_space=pltpu.VMEM)),
        input_output_aliases={i: 2 * ng + i for i in range(n_in)},
        compiler_params=pltpu.CompilerParams(has_side_effects=pltpu.SideEffectType.DATAFLOW_SIDE_EFFECTING))
    started, at = [], 2 * ng
    for gi, (ns, nl) in enumerate(sizes):
        started.append((out[-1], out[2 * gi], out[2 * gi + 1], list(out[at:at + ns]), list(out[at + ns:at + ns + nl])))
        at += ns + nl
    return started


def _send_start(name, srcs, lands, mode, dep=None):
    return _send_start_groups(name, [(srcs, lands)], mode, dep=dep)[0]


def _send_wait(name, started, mode, dep=None):
    _, send_sems, recv_sems, srcs, lands = started
    n_src, n = len(srcs), len(lands)

    def body(*refs):
        src_refs, land_refs = refs[:n_src], refs[n_src:n_src + n]
        ssem, rsem = refs[n_src + n], refs[n_src + n + 1]
        for cp in _send_copies(mode, src_refs, land_refs, ssem, rsem):
            cp.wait_send()
            cp.wait_recv()

    thru = [pltpu.HBM(t.shape, t.dtype) for t in [*srcs, *lands]]
    out = _pcall(
        body, name=name, ins=[*srcs, *lands, send_sems, recv_sems], in_specs=[HBM] * (n_src + n) + [SEM, SEM], dep=dep,
        out_shape=tuple(thru), out_specs=tuple([HBM] * (n_src + n)),
        input_output_aliases={i: i for i in range(n_src + n)},
        compiler_params=pltpu.CompilerParams(has_side_effects=pltpu.SideEffectType.DATAFLOW_SIDE_EFFECTING))
    return list(out[n_src:])


def _landing(block, me):
    zone = lax.empty((NDEV, *block.shape), block.dtype)
    return lax.dynamic_update_slice(zone, block[None], (me,) + (0,) * block.ndim)


def _rows128(t):
    flat = t.reshape(-1)
    n = flat.shape[0]
    rows = -(-n // (8 * LANES)) * 8
    return jnp.pad(flat, (0, rows * LANES - n)).reshape(rows, LANES)


def kernel(x, mix_norm_g, ffn_norm_g, a_w_in, a_ln_g, a_w_spatial, a_b_spatial, a_w_out, kv_norm_g, w_k, w_v, b_w_q, b_sinks, b_w_o, rel_bias, ffn_w1, ffn_w2, final_norm_g, loss_target, m_mix_norm_g, m_ffn_norm_g, m_a_w_in, m_a_ln_g, m_a_w_spatial, m_a_b_spatial, m_a_w_out, m_kv_norm_g, m_w_k, m_w_v, m_b_w_q, m_b_sinks, m_b_w_o, m_rel_bias, m_ffn_w1, m_ffn_w2, m_final_norm_g, v_mix_norm_g, v_ffn_norm_g, v_a_w_in, v_a_ln_g, v_a_w_spatial, v_a_b_spatial, v_a_w_out, v_kv_norm_g, v_w_k, v_w_v, v_b_w_q, v_b_sinks, v_b_w_o, v_rel_bias, v_ffn_w1, v_ffn_w2, v_final_norm_g):
    c = _config(x, a_w_in, a_w_out, w_k, b_w_q, b_w_o, ffn_w1, ffn_w2)
    S, D, LA, LB, LF = c.S, c.D, c.LA, c.LB, c.LF
    weights = dict(mix_norm_g=mix_norm_g, ffn_norm_g=ffn_norm_g, a_w_in=a_w_in, a_ln_g=a_ln_g, a_w_spatial=a_w_spatial,
                   a_b_spatial=a_b_spatial, a_w_out=a_w_out, kv_norm_g=kv_norm_g, w_k=w_k, w_v=w_v, b_w_q=b_w_q,
                   b_sinks=b_sinks, b_w_o=b_w_o, rel_bias=rel_bias, ffn_w1=ffn_w1, ffn_w2=ffn_w2,
                   final_norm_g=final_norm_g)
    m_in = dict(mix_norm_g=m_mix_norm_g, ffn_norm_g=m_ffn_norm_g, a_w_in=m_a_w_in, a_ln_g=m_a_ln_g,
                a_w_spatial=m_a_w_spatial, a_b_spatial=m_a_b_spatial, a_w_out=m_a_w_out, kv_norm_g=m_kv_norm_g,
                w_k=m_w_k, w_v=m_w_v, b_w_q=m_b_w_q, b_sinks=m_b_sinks, b_w_o=m_b_w_o, rel_bias=m_rel_bias,
                ffn_w1=m_ffn_w1, ffn_w2=m_ffn_w2, final_norm_g=m_final_norm_g)
    v_in = dict(mix_norm_g=v_mix_norm_g, ffn_norm_g=v_ffn_norm_g, a_w_in=v_a_w_in, a_ln_g=v_a_ln_g,
                a_w_spatial=v_a_w_spatial, a_b_spatial=v_a_b_spatial, a_w_out=v_a_w_out, kv_norm_g=v_kv_norm_g,
                w_k=v_w_k, w_v=v_w_v, b_w_q=v_b_w_q, b_sinks=v_b_sinks, b_w_o=v_b_w_o, rel_bias=v_rel_bias,
                ffn_w1=v_ffn_w1, ffn_w2=v_ffn_w2, final_norm_g=v_final_norm_g)
    names = list(weights)
    seq = _Seq()
    me = _slot(*_place())
    bf = lambda t: t.astype(BF16)

    def rows_of(l):
        second = a_w_out[l] if l < LA else b_w_o[l - LA]
        return bf(jnp.concatenate([ffn_w2[l], second], axis=0))

    tr = lambda t: bf(jnp.swapaxes(t, -1, -2))
    groups = []
    for l in range(LA):
        groups += [[tr(a_w_in[l])[None]] + ([a_ln_g] if l == 0 else []), [bf(a_w_out[l])],
                   [tr(ffn_w1[l])[None], bf(ffn_w2[l])]]
    for l in range(LB):
        extra = [bf(jnp.concatenate([w_k, w_v], axis=1))] if l == 0 else []
        groups.append(extra + [bf(b_w_q[l]), tr(ffn_w1[LA + l])[None], rows_of(LA + l)])
    started = seq(_send_start_groups, "weights_start", [(grp, [_landing(t, me) for t in grp]) for grp in groups],
                  "gather")
    seq.last = started[0][0]
    forwarding = {}

    def forward(i):
        lands = seq(_send_wait, f"weights_wait{i}", started[i], "gather")
        forwarding[i] = seq(_send_start, f"weights_forward{i}", [], lands, "forward")

    def arrive(i):
        if i not in forwarding:
            forward(i)
        return seq(_send_wait, f"weights_arrive{i}", forwarding[i], "forward")

    causal = jnp.tril(jnp.ones((CHUNK, CHUNK), bool))
    wsp = jnp.where(causal[None, None], a_w_spatial, 0.0)
    wsp16 = wsp.astype(BF16)
    wsp16_t = jnp.swapaxes(wsp, -1, -2).astype(BF16)
    bsp_t = jnp.swapaxes(a_b_spatial, -1, -2)
    mix_g = mix_norm_g.reshape(-1, 1, D)
    ffn_g = ffn_norm_g.reshape(-1, 1, D)
    kv_g = kv_norm_g.reshape(1, D)
    fin_g = final_norm_g.reshape(1, D)
    onehot = _bucket_onehot()
    bias = _slot_bias(seq(_band_bias, rel_bias.T, onehot).reshape(N_HEADS, BLOCK, 2 * BLOCK))

    h = x.reshape(S, D)
    sav_a, sav_b, wts_a, wts_b = [], [], [], []
    for l in range(LA):
        got = arrive(3 * l)
        w_in = got[0]
        if l == 0:
            ln_g_full = jnp.transpose(got[1], (1, 0, 2)).reshape(LA, 1, c.AW)
        z, a, hn = seq(_a_in_fwd, c, f"a_in_fwd{l}", h, mix_g[l], w_in, 0)
        forward(3 * l + 1)
        gated = seq(_sgu_fwd, c, f"sgu_fwd{l}", a, ln_g_full[l], wsp16[l], bsp_t[l])
        (wout,) = arrive(3 * l + 1)
        if l > 0:
            forward(3 * l + 2)
        h1 = seq(_mm_res, c, f"a_out_fwd{l}", gated, wout, c.ar, 0, h)
        w1, rows = arrive(3 * l + 2)
        if l == LA - 1:
            forward(3 * LA)
        p, h2, hnf = seq(_ffn_fwd, c, f"ffn_fwd{l}", h1, ffn_g[l], w1, 0, rows)
        sav_a.append((h, z, a, hn, gated, h1, p, hnf))
        wts_a.append((w_in, 0, w1, 0, rows, wout, 0))
        h = h2
    h_kv = h
    for l in range(LB):
        got = arrive(3 * LA + l)
        if l == 0:
            wkv, got = got[0], got[1:]
            kv, hkv = seq(_rms_mm_rows, c, "kv_fwd", h, kv_g, wkv, c.kr, 0, 2 * c.DKV)
        wq, w1, rows = got
        wq = _slot_cols(wq)
        wo = _slot_rows(rows[:, c.fr:, :])
        q, hn = seq(_rms_mm_rows, c, f"q_fwd{l}", h, mix_g[LA + l], wq, c.qr, 0, c.DQ)
        o = seq(_attn_fwd, c, f"attn_fwd{l}", q, kv, bias, b_sinks[l])
        h1 = seq(_mm_res, c, f"o_fwd{l}", o, wo, c.orr, 0, h)
        if l + 1 < LB:
            forward(3 * LA + l + 1)
        p, h2, hnf = seq(_ffn_fwd, c, f"ffn_fwd{LA + l}", h1, ffn_g[LA + l], w1, 0, rows)
        sav_b.append((h, q, hn, o, h1, p, hnf))
        wts_b.append((wq, wo, w1, rows))
        h = h2
    dh, d_fin_g, loss_row = seq(_final_loss, c, h, fin_g, loss_target.reshape(S, D))
    loss = lax.psum(loss_row[0, 0], MESH_AXES)

    results = {}
    in_flight = []

    def update(k, parts, layer, col_blk=0):
        w = weights[k]
        rows_l, ncols = (w.shape[-2], w.shape[-1]) if w.ndim == 3 else w.shape
        flat = lambda t: t.reshape(-1, ncols)
        tr = min(256, rows_l)
        results[k] = seq(_adamw, f"adamw_{k}{layer}", parts, (NDEV, tr, ncols), lambda i: (0, i, col_blk),
                         flat(w), flat(m_in[k]), flat(v_in[k]), tr, row_off=layer * rows_l, n_rows=rows_l,
                         prev=results.get(k))

    def land(tag, entry):
        lands = seq(_send_wait, f"grads_wait_{tag}", entry[1], "exchange")
        for keys, parts in zip(entry[0], lands):
            for k, layer, col_blk in keys:
                update(k, parts, layer, col_blk)

    def send(tag, items):
        slabs = [t for _, t in items]
        own = [_landing(lax.dynamic_index_in_dim(t, me, 0, keepdims=False), me) for t in slabs]
        st = seq(_send_start, f"grads_start_{tag}", slabs, own, "exchange")
        in_flight.append((tag, ([k for k, _ in items], st)))
        while len(in_flight) > EXCHANGE_LAG:
            land(*in_flight.pop(0))

    d_mix_g, d_ffn_g = [None] * LF, [None] * LF
    dkv_list, dbias_list, dsink_list = [], [], [None] * LB

    def ffn_bwd(lf, dh, h1, p, hnf, w1, w1_i, rows):
        da, dh1, d_ffn_g[lf], dhb = seq(_ffn_bwd_data, c, f"ffn_bwd_data{lf}", dh, p, w1, w1_i, rows, h1, ffn_g[lf])
        dw1, dw2 = seq(_ffn_bwd_w, c, f"ffn_bwd_w{lf}", hnf, da, p, dhb)
        send(f"ffn{lf}", [([("ffn_w1", lf, 0)], dw1), ([("ffn_w2", lf, 0)], dw2)])
        return dh1

    for l in reversed(range(LB)):
        h0, q, hn, o, h1, p, hnf = sav_b[l]
        wq, wo, w1, rows = wts_b[l]
        dh1 = ffn_bwd(LA + l, dh, h1, p, hnf, w1, 0, rows)
        do = seq(_bwd_rows_data, c, f"o_bwd_data{l}", dh1, wo, c.orr, 0)
        dwo = _unslot_rows(seq(_wgrad_rows, c, f"o_bwd_w{l}", o, [dh1], c.DQ, D))
        dq, dkv, dbias, dsink = seq(_attn_bwd, c, f"attn_bwd{l}", q, kv, do, bias, b_sinks[l])
        dsink_list[l] = dsink.reshape(Q_PER_KV, 2, KV_PAIRS).transpose(2, 1, 0).reshape(1, N_HEADS)
        dkv_list.append(dkv)
        dbias_list.append(_unslot_bias(dbias))
        dwq = _unslot_cols(seq(_wgrad_rows, c, f"q_bwd_w{l}", hn, [dq], D, c.DQ))
        send(f"attn{l}", [([("b_w_o", l, 0)], dwo), ([("b_w_q", l, 0)], dwq)])
        dh, d_mix_g[LA + l] = seq(_bwd_rows_to_stream, c, f"q_bwd_data{l}", [dq], wq, c.qr, 0, c.DQ, h0,
                                  mix_g[LA + l], dh1)
    dwkv = seq(_wgrad_rows, c, "kv_bwd_w", hkv, dkv_list, D, 2 * c.DKV)
    send("kv", [([("w_k", 0, 0), ("w_v", 0, 1)], dwkv)])
    dh, d_kv_g = seq(_bwd_rows_to_stream, c, "kv_bwd_data", dkv_list, wkv, c.kr, 0, 2 * c.DKV, h_kv, kv_g, dh)
    d_rel_t = seq(_band_bias_grad, dbias_list, onehot)
    d_wsp, d_bsp, d_lng = [None] * LA, [None] * LA, [None] * LA
    for l in reversed(range(LA)):
        h0, z, a, hn, gated, h1, p, hnf = sav_a[l]
        w_in, in_i, w1, w1_i, rows, wout, wout_i = wts_a[l]
        dh1 = ffn_bwd(l, dh, h1, p, hnf, w1, w1_i, rows)
        dgated = seq(_bwd_rows_data, c, f"a_out_bwd_data{l}", dh1, wout, c.ar, wout_i)
        dwout = seq(_wgrad_rows, c, f"a_out_bwd_w{l}", gated, [dh1], c.AW, D)
        send(f"a_out{l}", [([("a_w_out", l, 0)], dwout)])
        dz, d_wsp[l], dbt, d_lng[l] = seq(_sgu_bwd, c, f"sgu_bwd{l}", a, z, dgated, ln_g_full[l], wsp16[l],
                                          wsp16_t[l], bsp_t[l])
        d_bsp[l] = dbt.T
        dwin = seq(_wgrad_cols, c, f"a_in_bwd_w{l}", hn, dz)
        send(f"a_in{l}", [([("a_w_in", l, 0)], dwin)])
        dh, d_mix_g[l] = seq(_bwd_cols_to_stream, c, f"a_in_bwd_data{l}", dz, w_in, in_i, h0, mix_g[l], dh1)
    grad_x = dh.reshape(1, S, D)

    small = {
        "mix_norm_g": jnp.concatenate(d_mix_g, axis=0), "ffn_norm_g": jnp.concatenate(d_ffn_g, axis=0),
        "a_w_spatial": jnp.stack(d_wsp), "a_b_spatial": jnp.stack(d_bsp), "kv_norm_g": d_kv_g,
        "b_sinks": jnp.concatenate(dsink_list, axis=0), "rel_bias": d_rel_t.T, "final_norm_g": d_fin_g,
    }
    small_names = list(small)
    packs = [_rows128(small[k]) for k in small_names] + [_rows128(jnp.concatenate(d_lng, axis=0))]
    offs = np.cumsum([0] + [p.shape[0] for p in packs])
    tail_rows = int(-offs[-1] % (8 * NDEV)) + packs[-1].shape[0]
    Rs = int(offs[-2]) + tail_rows
    packed = jnp.concatenate(packs + [jnp.zeros((Rs - int(offs[-1]), LANES), F32)], axis=0)
    slab = packed.reshape(NDEV, Rs // NDEV, LANES)
    st = seq(_send_start, "small_grads_start", [slab],
             [_landing(lax.dynamic_index_in_dim(slab, me, 0, keepdims=False), me)], "exchange")
    while in_flight:
        land(*in_flight.pop(0))
    (parts,) = seq(_send_wait, "small_grads_wait", st, "exchange")
    mine = seq(_sum_parts, "small_grads_sum", parts)
    st = seq(_send_start, "small_sums_start", [mine], [_landing(mine, me)], "broadcast")
    (sums,) = seq(_send_wait, "small_sums_wait", st, "broadcast")
    small_all = sums.reshape(1, Rs, LANES)

    grads, deltas, new_m, new_v = {}, {}, {}, {}

    def put(k, outs, shape):
        grads[k], deltas[k], new_m[k], new_v[k] = (t.reshape(shape) for t in outs)

    def pack_state(d):
        return jnp.concatenate([_rows128(d[k]) for k in small_names] + [jnp.zeros((tail_rows, LANES), F32)], axis=0)

    outs = seq(_adamw, "adamw_small", small_all, (1, Rs, LANES), lambda i: (0, 0, 0),
               pack_state(weights), pack_state(m_in), pack_state(v_in), Rs)
    for n_, k in enumerate(small_names):
        shape = weights[k].shape
        size = int(np.prod(shape))
        put(k, [t[int(offs[n_]):int(offs[n_ + 1])].reshape(-1)[:size] for t in outs], shape)
    lng_sum = outs[0][int(offs[-2]):int(offs[-1])].reshape(-1)[:LA * c.AW].reshape(LA, c.AW)
    lng_mine = lax.dynamic_slice_in_dim(lng_sum, me * c.ar, c.ar, axis=1)
    lng_parts = jnp.concatenate([lng_mine[None], jnp.zeros((NDEV - 1, LA, c.ar), F32)], axis=0)
    put("a_ln_g", seq(_adamw, "adamw_ln_g", lng_parts, (NDEV, LA, c.ar), lambda i: (0, 0, 0),
                      a_ln_g, m_in["a_ln_g"], v_in["a_ln_g"], LA), a_ln_g.shape)
    for k in ("a_w_in", "ffn_w1", "ffn_w2", "a_w_out", "b_w_o", "b_w_q", "w_k", "w_v"):
        put(k, results[k], weights[k].shape)

    return (loss, grad_x, *[grads[k] for k in names], *[deltas[k] for k in names],
            *[new_m[k] for k in names], *[new_v[k] for k in names])
```

```python
import numpy as np
import math
import jax
import jax.numpy as jnp
from jax import lax
from jax.experimental import pallas as pl
from jax.experimental.pallas import tpu as pltpu

F32 = jnp.float32
BF16 = jnp.bfloat16

NDEV = 8
EPS = 1e-6
CHUNK = 128
A_GROUPS = 8
N_HEADS = 16
N_KV_HEADS = 4
Q_PER_KV = N_HEADS // N_KV_HEADS
HEAD_DIM = 64
BLOCK = 128
N_BUCKETS = 32
MAX_DISTANCE = 128
ADAM_LR, ADAM_B1, ADAM_B2, ADAM_EPS, ADAM_WD, ADAM_STEP = 0.001, 0.9, 0.999, 1e-08, 0.01, 10
LANES = 128
VMEM_LIMIT = 56 * 1024 * 1024
INV_SQRT2 = 0.7071067811865476
INV_SQRT_2PI = 0.3989422804014327
MESH_AXES = ("x", "y", "c")
EXCHANGE_LAG = 2

HBM = pl.BlockSpec(memory_space=pltpu.HBM)
SMEM = pl.BlockSpec(memory_space=pltpu.SMEM)
ANY = pl.BlockSpec(memory_space=pl.ANY)
SEM = pl.BlockSpec(memory_space=pltpu.SEMAPHORE)
MESH = pl.DeviceIdType.MESH


def _params(n_grid):
    return pltpu.CompilerParams(dimension_semantics=("arbitrary",) * n_grid, vmem_limit_bytes=VMEM_LIMIT)


def _const(block, index_map):
    return pl.BlockSpec(block, index_map, pipeline_mode=pl.Buffered(1))


def _pcall(body, *, ins, in_specs, dep=None, **kw):
    n_in = len(ins)
    if dep is None or any(dep is t for t in ins):
        return pl.pallas_call(body, in_specs=list(in_specs), **kw)(*ins)

    def with_dep(*refs):
        body(*refs[:n_in], *refs[n_in + 1:])

    return pl.pallas_call(with_dep, in_specs=[*in_specs, ANY], **kw)(*ins, dep)


class _Resident:
    def __init__(self, pairs, sems):
        self.first = pl.program_id(0) == 0
        self.copies = [pltpu.make_async_copy(src, dst, sems.at[k]) for k, (src, dst) in enumerate(pairs)]

        @pl.when(self.first)
        def _():
            for cp in self.copies:
                cp.start()

    def ready(self, k):
        @pl.when(self.first)
        def _():
            self.copies[k].wait()


class _Seq:
    def __init__(self):
        self.last = None

    def __call__(self, fn, *args, **kw):
        out = fn(*args, dep=self.last, **kw)
        self.last = out[0] if isinstance(out, (tuple, list)) else out
        return out


def _rstd(h):
    return lax.rsqrt(jnp.mean(h * h, axis=-1, keepdims=True) + EPS)


def _rms_bwd(dhn, h, g, dres):
    r = _rstd(h)
    xh = h * r
    dg = jnp.sum(dhn * xh, axis=0, keepdims=True)
    dxh = dhn * g
    dx = r * (dxh - xh * jnp.mean(dxh * xh, axis=-1, keepdims=True))
    return dres + dx, dg


def _gelu(z):
    return 0.5 * z * (1.0 + lax.erf(z * INV_SQRT2))


def _gelu_grad(z):
    return 0.5 * (1.0 + lax.erf(z * INV_SQRT2)) + z * (jnp.exp(-0.5 * z * z) * INV_SQRT_2PI)


def _dot(a, b, dims):
    return lax.dot_general(a, b, (dims, ((), ())), preferred_element_type=F32)


NN = ((1,), (0,))
NT = ((1,), (1,))
TN = ((0,), (0,))


def _mm(name, ins, in_specs, out_shapes, out_specs, *, grid, dims, nk, acc_shape, load_a, load_b, epilogue,
        dep=None):
    n_in, n_out = len(ins), len(out_shapes)
    kax = len(grid) - 1

    def body(*refs):
        in_refs = refs[:n_in]
        out_refs = refs[n_in:n_in + n_out]
        a = load_a(in_refs, out_refs)
        b = load_b(in_refs)
        prod = _dot(a, b, dims)
        if nk == 1:
            epilogue(prod, in_refs, out_refs)
        else:
            acc = refs[n_in + n_out]
            k = pl.program_id(kax)

            @pl.when(k == 0)
            def _():
                acc[...] = prod

            @pl.when(k > 0)
            def _():
                acc[...] += prod

            @pl.when(k == nk - 1)
            def _():
                epilogue(acc[...], in_refs, out_refs)

    return _pcall(
        body, name=name, ins=ins, in_specs=in_specs, dep=dep, grid=grid, out_specs=out_specs, out_shape=out_shapes,
        scratch_shapes=[pltpu.VMEM(acc_shape, F32)] if nk > 1 else [], compiler_params=_params(len(grid)))


def _bf(ref_idx):
    return lambda in_refs, *_: in_refs[ref_idx][...].astype(BF16)


def _b_view(ref_idx, rows):
    def load(in_refs):
        b = in_refs[ref_idx][...]
        return b.reshape(rows, b.shape[-1])
    return load


class Cfg:
    pass


def _config(x, a_w_in, a_w_out, w_k, b_w_q, b_w_o, ffn_w1, ffn_w2):
    c = Cfg()
    c.S, c.D = x.shape[1], x.shape[2]
    c.LA, _, c.cw = a_w_in.shape
    c.AW2 = NDEV * c.cw
    c.AW = c.AW2 // 2
    c.gd = c.AW // A_GROUPS
    c.ar = a_w_out.shape[1]
    c.LF, _, c.fw = ffn_w1.shape
    c.fr = ffn_w2.shape[1]
    c.LB, c.qr, c.DQ = b_w_q.shape
    c.orr = b_w_o.shape[1]
    c.kr, c.DKV = w_k.shape
    c.tm = min(1024, c.S)
    c.tmw = min(512, c.S)
    c.tms = min(256, c.S)
    c.nb = c.S // BLOCK
    assert c.cw == c.fw == c.fr and c.AW == NDEV * c.ar and c.D == NDEV * c.qr == NDEV * c.kr
    assert c.DQ == NDEV * c.orr == N_HEADS * HEAD_DIM and c.DKV == N_KV_HEADS * HEAD_DIM
    assert c.S % c.tm == 0 and c.S % c.tmw == 0 and c.S % c.tms == 0 and c.tms % CHUNK == 0 and c.gd % LANES == 0
    assert c.fr % c.ar == 0 and c.fr % c.orr == 0
    assert c.LA >= 1 and c.LB >= 1 and c.LF == c.LA + c.LB
    return c


def _cached_rms(h_idx, g_idx, hn_out_idx, jax_axis=1):
    def load(in_refs, out_refs):
        hn_ref = out_refs[hn_out_idx]

        @pl.when(pl.program_id(jax_axis) == 0)
        def _():
            h = in_refs[h_idx][...]
            hn_ref[...] = (h * _rstd(h) * in_refs[g_idx][...]).astype(BF16)

        return hn_ref[...]
    return load


def _a_in_fwd(c, name, h, g, col, ci, dep=None):
    S, D, cw, tm = c.S, c.D, c.cw, c.tmw

    def body(h_ref, g_ref, w_hbm, z_ref, a_ref, hn_ref, w_ref, sems):
        w = _Resident([(w_hbm.at[j, ci], w_ref.at[j]) for j in range(NDEV)], sems)
        h = h_ref[...]
        hn = (h * _rstd(h) * g_ref[...]).astype(BF16)
        hn_ref[...] = hn
        for j in range(NDEV):
            cols = slice(j * cw, (j + 1) * cw)
            w.ready(j)
            z = _dot(hn, w_ref[j], NT)
            z_ref[:, cols] = z.astype(BF16)
            a_ref[:, cols] = _gelu(z).astype(BF16)

    row = pl.BlockSpec((tm, D), lambda i: (i, 0))
    wide = pl.BlockSpec((tm, c.AW2), lambda i: (i, 0))
    return _pcall(
        body, name=name, ins=[h, g, col], dep=dep, grid=(S // tm,),
        in_specs=[row, pl.BlockSpec((1, D), lambda i: (0, 0)), ANY],
        out_specs=[wide, wide, row],
        out_shape=[jax.ShapeDtypeStruct((S, c.AW2), BF16), jax.ShapeDtypeStruct((S, c.AW2), BF16),
                   jax.ShapeDtypeStruct((S, D), BF16)],
        scratch_shapes=[pltpu.VMEM((NDEV, cw, D), BF16), pltpu.SemaphoreType.DMA((NDEV,))],
        compiler_params=_params(1))


def _rms_mm_rows(c, name, h, g, slab, blk_rows, blk_idx, n_out, dep=None):
    S, D, tm = c.S, c.D, c.tm

    def epilogue(acc, in_refs, out_refs):
        out_refs[0][...] = acc.astype(BF16)

    return _mm(
        name, [h, slab, g],
        [pl.BlockSpec((tm, D), lambda i, j, k: (i, 0)),
         pl.BlockSpec((NDEV, blk_rows, n_out), lambda i, j, k: (0, blk_idx, 0)),
         pl.BlockSpec((1, D), lambda i, j, k: (0, 0))],
        [jax.ShapeDtypeStruct((S, n_out), BF16), jax.ShapeDtypeStruct((S, D), BF16)],
        [pl.BlockSpec((tm, n_out), lambda i, j, k: (i, 0)), pl.BlockSpec((tm, D), lambda i, j, k: (i, 0))],
        grid=(S // tm, 1, 1), dims=NN, nk=1, acc_shape=None,
        load_a=_cached_rms(0, 2, 1), load_b=_b_view(1, NDEV * blk_rows), epilogue=epilogue, dep=dep)


def _mm_res(c, name, a, slab, blk_rows, blk_idx, res, dep=None):
    S, D, tm = c.S, c.D, c.tm
    K = NDEV * blk_rows

    def epilogue(acc, in_refs, out_refs):
        out_refs[0][...] = in_refs[2][...] + acc

    return _mm(
        name, [a, slab, res],
        [pl.BlockSpec((tm, K), lambda i, j, k: (i, 0)),
         pl.BlockSpec((NDEV, blk_rows, D), lambda i, j, k: (0, blk_idx, 0)),
         pl.BlockSpec((tm, D), lambda i, j, k: (i, 0))],
        [jax.ShapeDtypeStruct((S, D), F32)], [pl.BlockSpec((tm, D), lambda i, j, k: (i, 0))],
        grid=(S // tm, 1, 1), dims=NN, nk=1, acc_shape=None,
        load_a=_bf(0), load_b=_b_view(1, K), epilogue=epilogue, dep=dep)[0]


def _sgu_masks():
    ii = lax.broadcasted_iota(jnp.int32, (CHUNK, CHUNK), 0)
    jj = lax.broadcasted_iota(jnp.int32, (CHUNK, CHUNK), 1)
    return ii >= jj


def _sgu_fwd(c, name, a, ln_g, wc, b_t, dep=None):
    S, AW, gd, tm = c.S, c.AW, c.gd, c.tms

    def body(a_ref, lng_ref, wc_ref, bt_ref, out_ref):
        va = a_ref[:, AW:].astype(F32)
        xc = va - jnp.mean(va, axis=-1, keepdims=True)
        vn = (xc * lax.rsqrt(jnp.mean(xc * xc, axis=-1, keepdims=True) + EPS) * lng_ref[...]).astype(BF16)
        for ch in range(tm // CHUNK):
            rows = slice(ch * CHUNK, (ch + 1) * CHUNK)
            for g in range(A_GROUPS):
                cols = slice(g * gd, (g + 1) * gd)
                mixed = _dot(wc_ref[g], vn[rows, cols], NN) + bt_ref[:, g:g + 1]
                out_ref[rows, cols] = (a_ref[rows, cols].astype(F32) * mixed).astype(BF16)

    return _pcall(
        body, name=name, ins=[a, ln_g, wc, b_t], dep=dep, grid=(S // tm,),
        in_specs=[pl.BlockSpec((tm, 2 * AW), lambda i: (i, 0)), pl.BlockSpec((1, AW), lambda i: (0, 0)),
                  pl.BlockSpec((A_GROUPS, CHUNK, CHUNK), lambda i: (0, 0, 0)),
                  pl.BlockSpec((CHUNK, A_GROUPS), lambda i: (0, 0))],
        out_specs=pl.BlockSpec((tm, AW), lambda i: (i, 0)),
        out_shape=jax.ShapeDtypeStruct((S, AW), BF16), compiler_params=_params(1))


def _ffn_fwd(c, name, h, g, col, ci, rows, dep=None):
    S, D, fw, tm = c.S, c.D, c.fw, c.tmw
    F = NDEV * fw

    def body(h_ref, g_ref, w1_hbm, w2_hbm, p_ref, out_ref, hn_ref, r_ref, w1_ref, w2_ref, sems):
        w = _Resident([(w1_hbm.at[j, ci], w1_ref.at[j]) for j in range(NDEV)]
                      + [(w2_hbm.at[j], w2_ref.at[j]) for j in range(NDEV)], sems)
        h = h_ref[...]
        hn = (h * _rstd(h) * g_ref[...]).astype(BF16)
        hn_ref[...] = hn
        for j in range(NDEV):
            cols = slice(j * fw, (j + 1) * fw)
            w.ready(j)
            p = jnp.maximum(_dot(hn, w1_ref[j], NT), 0.0)
            p_ref[:, cols] = p.astype(BF16)
            r_ref[:, cols] = (p * p).astype(BF16)
        for j in range(NDEV):
            w.ready(NDEV + j)
        out_ref[...] = h + _dot(r_ref[...], w2_ref[...].reshape(F, D), NN)

    row = pl.BlockSpec((tm, D), lambda i: (i, 0))
    return _pcall(
        body, name=name, ins=[h, g, col, rows], dep=dep, grid=(S // tm,),
        in_specs=[row, pl.BlockSpec((1, D), lambda i: (0, 0)), ANY, ANY],
        out_specs=[pl.BlockSpec((tm, F), lambda i: (i, 0)), row, row],
        out_shape=[jax.ShapeDtypeStruct((S, F), BF16), jax.ShapeDtypeStruct((S, D), F32),
                   jax.ShapeDtypeStruct((S, D), BF16)],
        scratch_shapes=[pltpu.VMEM((tm, F), BF16), pltpu.VMEM((NDEV, fw, D), BF16), pltpu.VMEM((NDEV, c.fr, D), BF16),
                        pltpu.SemaphoreType.DMA((2 * NDEV,))],
        compiler_params=_params(1))


def _bucket_table():
    qi = np.arange(BLOCK)[:, None]
    kj = np.arange(2 * BLOCK)[None, :]
    d = np.maximum(qi + BLOCK - kj, 0)
    max_exact = N_BUCKETS // 2
    ratio = np.log(np.maximum(d, 1).astype(np.float32) / np.float32(max_exact)) / np.float32(
        math.log(MAX_DISTANCE / max_exact))
    large = np.minimum(max_exact + (ratio.astype(np.float32) * np.float32(N_BUCKETS - max_exact)).astype(np.int32),
                       N_BUCKETS - 1)
    return np.where(d < max_exact, d, large).astype(np.int32)


def _bucket_onehot():
    b = jnp.asarray(_bucket_table().reshape(1, -1))
    return (b == lax.broadcasted_iota(jnp.int32, (N_BUCKETS, b.shape[1]), 0)).astype(F32)


def _whole(t):
    return pl.BlockSpec(t.shape, lambda: (0,) * t.ndim)


def _band_bias(rel_bias_t, onehot, dep=None):
    def body(r_ref, oh_ref, out_ref):
        out_ref[...] = lax.dot_general(r_ref[...], oh_ref[...], (NN, ((), ())), preferred_element_type=F32,
                                       precision=lax.Precision.HIGHEST)

    n = onehot.shape[1]
    return _pcall(body, name="band_bias", ins=[rel_bias_t, onehot], in_specs=[_whole(rel_bias_t), _whole(onehot)],
                  dep=dep, out_shape=jax.ShapeDtypeStruct((N_HEADS, n), F32), compiler_params=_params(0))


def _band_bias_grad(dbias_list, onehot, dep=None):
    n_in = len(dbias_list)

    def body(*refs):
        oh_ref, out_ref = refs[n_in], refs[n_in + 1]
        d = refs[0][...]
        for r in refs[1:n_in]:
            d = d + r[...]
        out_ref[...] = lax.dot_general(d, oh_ref[...], (NT, ((), ())), preferred_element_type=F32,
                                       precision=lax.Precision.HIGHEST)

    ins = [*dbias_list, onehot]
    return _pcall(body, name="band_bias_grad", ins=ins, in_specs=[_whole(t) for t in ins], dep=dep,
                  out_shape=jax.ShapeDtypeStruct((N_HEADS, N_BUCKETS), F32), compiler_params=_params(0))


KV_PAIRS = N_KV_HEADS // 2
PAIR_ROWS = 2 * Q_PER_KV * BLOCK
MASKED = float(np.finfo(np.float32).min) / 2


def _slot_cols(w):
    lead = w.shape[:-1]
    return w.reshape(*lead, KV_PAIRS, 2, Q_PER_KV, HEAD_DIM).swapaxes(-3, -2).reshape(*lead, N_HEADS * HEAD_DIM)


def _unslot_cols(w):
    lead = w.shape[:-1]
    return w.reshape(*lead, KV_PAIRS, Q_PER_KV, 2, HEAD_DIM).swapaxes(-3, -2).reshape(*lead, N_HEADS * HEAD_DIM)


def _slot_rows(blocks):
    n = blocks.shape[-1]
    return blocks.reshape(KV_PAIRS, 2, Q_PER_KV, HEAD_DIM, n).swapaxes(1, 2).reshape(blocks.shape)


def _unslot_rows(blocks):
    n = blocks.shape[-1]
    return blocks.reshape(KV_PAIRS, Q_PER_KV, 2, HEAD_DIM, n).swapaxes(1, 2).reshape(blocks.shape)


def _slot_bias(bias):
    qi = np.arange(BLOCK)[:, None]
    kj = np.arange(2 * BLOCK)[None, :]
    dist = qi + BLOCK - kj
    window = (dist >= 0) & (dist < BLOCK)
    b = bias.reshape(KV_PAIRS, 2, Q_PER_KV, BLOCK, 2 * BLOCK).swapaxes(1, 2).reshape(KV_PAIRS, PAIR_ROWS, 2 * BLOCK)
    tile = lambda mk: jnp.asarray(np.tile(mk, (2 * Q_PER_KV, 1)))[None]
    return jnp.stack([jnp.where(tile(window & (kj >= BLOCK)), b, MASKED), jnp.where(tile(window), b, MASKED)])


def _unslot_bias(db):
    return db.reshape(KV_PAIRS, Q_PER_KV, 2, BLOCK, 2 * BLOCK).swapaxes(1, 2).reshape(N_HEADS, -1)


def _pair_kv(kvc_ref, kvp_ref, kvp, dkv):
    lanes = slice(kvp * LANES, (kvp + 1) * LANES)
    vlanes = slice(dkv + kvp * LANES, dkv + (kvp + 1) * LANES)
    k2 = jnp.concatenate([kvp_ref[:, lanes], kvc_ref[:, lanes]], axis=0)
    v2 = jnp.concatenate([kvp_ref[:, vlanes], kvc_ref[:, vlanes]], axis=0)
    return k2, v2


def _head_operand(ref, grp, par, low, scale=None):
    xg = ref[:, grp * LANES:(grp + 1) * LANES]
    if scale is not None:
        xg = xg * scale
    zero = jnp.zeros_like(xg)
    return jnp.where(low, xg, zero) if par == 0 else jnp.where(low, zero, xg)


def _head_probs(qh, k2, bias_rows, sink):
    s = _dot(qh, k2, NT) + bias_rows
    m = jnp.maximum(jnp.max(s, axis=-1, keepdims=True), sink)
    e = jnp.exp(s - m)
    es = jnp.exp(sink - m)
    inv = 1.0 / (jnp.sum(e, axis=-1, keepdims=True) + es)
    return e * inv, es * inv


def _attn_specs(c):
    dq, dkv2 = c.DQ, 2 * c.DKV
    return [pl.BlockSpec((BLOCK, dq), lambda n: (n, 0)),
            pl.BlockSpec((BLOCK, dkv2), lambda n: (n, 0)),
            pl.BlockSpec((BLOCK, dkv2), lambda n: (jnp.maximum(n - 1, 0), 0))]


def _bias_spec():
    return pl.BlockSpec((None, KV_PAIRS, PAIR_ROWS, 2 * BLOCK), lambda n: (jnp.minimum(n, 1), 0, 0, 0))


def _low_lanes():
    return lax.broadcasted_iota(jnp.int32, (BLOCK, LANES), 1) < HEAD_DIM


def _attn_fwd(c, name, q, kv, bias, sinks, dep=None):
    S, dq = c.S, c.DQ

    def body(q_ref, kvc_ref, kvp_ref, bias_ref, sink_ref, o_ref):
        low = _low_lanes()
        for kvp in range(KV_PAIRS):
            k2, v2 = _pair_kv(kvc_ref, kvp_ref, kvp, c.DKV)
            for g in range(Q_PER_KV):
                grp = kvp * Q_PER_KV + g
                halves = []
                for par in range(2):
                    r = 2 * g + par
                    qh = _head_operand(q_ref, grp, par, low, scale=HEAD_DIM ** -0.5)
                    p, _ = _head_probs(qh, k2, bias_ref[kvp, r * BLOCK:(r + 1) * BLOCK, :],
                                       sink_ref[(2 * kvp + par) * Q_PER_KV + g])
                    halves.append(_dot(p.astype(BF16), v2, NN))
                o_ref[:, grp * LANES:(grp + 1) * LANES] = jnp.where(low, halves[0], halves[1]).astype(BF16)

    return _pcall(
        body, name=name, ins=[q, kv, kv, bias, sinks], dep=dep, grid=(c.nb,),
        in_specs=_attn_specs(c) + [_bias_spec(), SMEM],
        out_specs=pl.BlockSpec((BLOCK, dq), lambda n: (n, 0)),
        out_shape=jax.ShapeDtypeStruct((S, dq), BF16), compiler_params=_params(1))


def _final_loss(c, h, g, target, dep=None):
    S, D, tm = c.S, c.D, c.tm

    def body(h_ref, g_ref, t_ref, dh_ref, dg_ref, loss_ref):
        i = pl.program_id(0)
        h = h_ref[...]
        gg = g_ref[...]
        r = _rstd(h)
        xh = h * r
        err = xh * gg - t_ref[...]
        lp = jnp.sum(jnp.sum(err * err, axis=1, keepdims=True), axis=0, keepdims=True) * (0.5 / D)
        dx, dg = _rms_bwd(err * (1.0 / D), h, gg, 0.0)
        dh_ref[...] = dx

        @pl.when(i == 0)
        def _():
            dg_ref[...] = dg
            loss_ref[...] = jnp.broadcast_to(lp, loss_ref.shape)

        @pl.when(i > 0)
        def _():
            dg_ref[...] += dg
            loss_ref[...] += jnp.broadcast_to(lp, loss_ref.shape)

    row = pl.BlockSpec((tm, D), lambda i: (i, 0))
    return _pcall(
        body, name="final_loss", ins=[h, g, target], dep=dep, grid=(S // tm,),
        in_specs=[row, pl.BlockSpec((1, D), lambda i: (0, 0)), row],
        out_specs=[row, pl.BlockSpec((1, D), lambda i: (0, 0)), pl.BlockSpec((1, LANES), lambda i: (0, 0))],
        out_shape=[jax.ShapeDtypeStruct((S, D), F32), jax.ShapeDtypeStruct((1, D), F32),
                   jax.ShapeDtypeStruct((1, LANES), F32)],
        compiler_params=_params(1))


def _rms_bwd_epilogue(h_idx, g_idx, res_idx):
    def epilogue(dhn, in_refs, out_refs):
        dh, dg = _rms_bwd(dhn, in_refs[h_idx][...], in_refs[g_idx][...], in_refs[res_idx][...])
        out_refs[0][...] = dh
        i = pl.program_id(0)

        @pl.when(i == 0)
        def _():
            out_refs[1][...] = dg

        @pl.when(i > 0)
        def _():
            out_refs[1][...] += dg
    return epilogue


def _stream_outs(c, tm):
    S, D = c.S, c.D
    return ([jax.ShapeDtypeStruct((S, D), F32), jax.ShapeDtypeStruct((1, D), F32)],
            [pl.BlockSpec((tm, D), lambda i, j, k: (i, 0)), pl.BlockSpec((1, D), lambda i, j, k: (0, 0))])


def _row_specs(c, tm):
    D = c.D
    return [pl.BlockSpec((tm, D), lambda i, j, k: (i, 0)), pl.BlockSpec((1, D), lambda i, j, k: (0, 0)),
            pl.BlockSpec((tm, D), lambda i, j, k: (i, 0))]


def _bwd_rows_to_stream(c, name, dy_list, slab, blk_rows, blk_idx, n_in_cols, h, g, dres, dep=None):
    S, D, tm = c.S, c.D, c.tm
    nd = len(dy_list)

    def load_a(in_refs, out_refs):
        a = in_refs[0][...]
        for r in in_refs[1:nd]:
            a = a + r[...]
        return a.astype(BF16)

    shapes, specs = _stream_outs(c, tm)
    return _mm(
        name, [*dy_list, slab, h, g, dres],
        [pl.BlockSpec((tm, n_in_cols), lambda i, j, k: (i, 0))] * nd
        + [pl.BlockSpec((NDEV, blk_rows, n_in_cols), lambda i, j, k: (0, blk_idx, 0))] + _row_specs(c, tm),
        shapes, specs, grid=(S // tm, 1, 1), dims=NT, nk=1, acc_shape=None,
        load_a=load_a, load_b=_b_view(nd, NDEV * blk_rows), epilogue=_rms_bwd_epilogue(nd + 1, nd + 2, nd + 3),
        dep=dep)


def _bwd_cols_to_stream(c, name, dy, col, ci, h, g, dres, dep=None):
    S, D, cw, tm = c.S, c.D, c.cw, c.tmw
    K = NDEV * cw
    shapes, specs = _stream_outs(c, tm)
    return _mm(
        name, [dy, col, h, g, dres],
        [pl.BlockSpec((tm, K), lambda i, j, k: (i, 0)),
         _const((NDEV, None, cw, D), lambda i, j, k: (0, ci, 0, 0))] + _row_specs(c, tm),
        shapes, specs, grid=(S // tm, 1, 1), dims=NN, nk=1, acc_shape=None,
        load_a=_bf(0), load_b=_b_view(1, K), epilogue=_rms_bwd_epilogue(2, 3, 4), dep=dep)


def _bwd_rows_data(c, name, dy, slab, blk_rows, blk_idx, dep=None):
    S, D, tm = c.S, c.D, c.tm
    K = NDEV * blk_rows

    def epilogue(acc, in_refs, out_refs):
        out_refs[0][...] = acc.astype(BF16)

    return _mm(
        name, [dy, slab],
        [pl.BlockSpec((tm, D), lambda i, j, k: (i, 0)),
         pl.BlockSpec((NDEV, blk_rows, D), lambda i, j, k: (0, blk_idx, 0))],
        [jax.ShapeDtypeStruct((S, K), BF16)], [pl.BlockSpec((tm, K), lambda i, j, k: (i, 0))],
        grid=(S // tm, 1, 1), dims=NT, nk=1, acc_shape=None,
        load_a=_bf(0), load_b=_b_view(1, K), epilogue=epilogue, dep=dep)[0]


def _wgrad_rows(c, name, a, b_list, n_a, n_b, dep=None):
    S, tm = c.S, c.tm
    nb_in = len(b_list)
    blk_rows = n_a // NDEV

    def load_b(in_refs):
        b = in_refs[1][...]
        for r in in_refs[2:1 + nb_in]:
            b = b + r[...]
        return b.astype(BF16)

    def epilogue(acc, in_refs, out_refs):
        out_refs[0][...] = acc.reshape(NDEV, blk_rows, n_b).astype(BF16)

    return _mm(
        name, [a, *b_list],
        [pl.BlockSpec((tm, n_a), lambda i, j, k: (k, 0))] + [pl.BlockSpec((tm, n_b), lambda i, j, k: (k, 0))] * nb_in,
        [jax.ShapeDtypeStruct((NDEV, blk_rows, n_b), BF16)],
        [pl.BlockSpec((NDEV, blk_rows, n_b), lambda i, j, k: (0, 0, 0))],
        grid=(1, 1, S // tm), dims=TN, nk=S // tm, acc_shape=(n_a, n_b),
        load_a=_bf(0), load_b=load_b, epilogue=epilogue, dep=dep)[0]


def _wgrad_cols(c, name, a, b, dep=None):
    S, D, cw = c.S, c.D, c.cw

    def body(a_ref, b_ref, out_ref):
        out_ref[...] = _dot(a_ref[...], b_ref[...], TN).astype(BF16)

    return _pcall(
        body, name=name, ins=[a, b], dep=dep, grid=(NDEV,),
        in_specs=[_const((S, D), lambda j: (0, 0)), pl.BlockSpec((S, cw), lambda j: (0, j))],
        out_specs=pl.BlockSpec((None, D, cw), lambda j: (j, 0, 0)),
        out_shape=jax.ShapeDtypeStruct((NDEV, D, cw), BF16), compiler_params=_params(1))


def _ffn_bwd_data(c, name, dh, p, col, ci, rows, h, g, dep=None):
    S, D, fw, tm = c.S, c.D, c.fw, c.tmw
    F = NDEV * fw

    def body(dh_ref, p_ref, w1t_hbm, w2_hbm, h_ref, g_ref, da_ref, out_ref, dg_ref, dhb_ref, w1t_ref, w2_ref, sems):
        i = pl.program_id(0)
        w = _Resident([(w2_hbm.at[j], w2_ref.at[j]) for j in range(NDEV)]
                      + [(w1t_hbm.at[j, ci], w1t_ref.at[j]) for j in range(NDEV)], sems)
        dh = dh_ref[...]
        dhb = dh.astype(BF16)
        dhb_ref[...] = dhb
        for j in range(NDEV):
            cols = slice(j * fw, (j + 1) * fw)
            w.ready(j)
            da_ref[:, cols] = (_dot(dhb, w2_ref[j], NT) * (2.0 * p_ref[:, cols].astype(F32))).astype(BF16)
        for j in range(NDEV):
            w.ready(NDEV + j)
        dx, dg = _rms_bwd(_dot(da_ref[...], w1t_ref[...].reshape(F, D), NN), h_ref[...], g_ref[...], dh)
        out_ref[...] = dx

        @pl.when(i == 0)
        def _():
            dg_ref[...] = dg

        @pl.when(i > 0)
        def _():
            dg_ref[...] += dg

    row = pl.BlockSpec((tm, D), lambda i: (i, 0))
    wide = pl.BlockSpec((tm, F), lambda i: (i, 0))
    return _pcall(
        body, name=name, ins=[dh, p, col, rows, h, g], dep=dep, grid=(S // tm,),
        in_specs=[row, wide, ANY, ANY, row, pl.BlockSpec((1, D), lambda i: (0, 0))],
        out_specs=[wide, row, pl.BlockSpec((1, D), lambda i: (0, 0)), row],
        out_shape=[jax.ShapeDtypeStruct((S, F), BF16), jax.ShapeDtypeStruct((S, D), F32),
                   jax.ShapeDtypeStruct((1, D), F32), jax.ShapeDtypeStruct((S, D), BF16)],
        scratch_shapes=[pltpu.VMEM((NDEV, fw, D), BF16), pltpu.VMEM((NDEV, c.fr, D), BF16),
                        pltpu.SemaphoreType.DMA((2 * NDEV,))],
        compiler_params=_params(1))


def _ffn_bwd_w(c, name, hn, da, p, dhb, dep=None):
    S, D, fw = c.S, c.D, c.fw

    def body(hn_hbm, da_ref, p_ref, dhb_hbm, dw1_ref, dw2_ref, hn_ref, dhb_ref, sems):
        w = _Resident([(hn_hbm, hn_ref), (dhb_hbm, dhb_ref)], sems)
        w.ready(0)
        dw1_ref[...] = _dot(hn_ref[...], da_ref[...], TN).astype(BF16)
        pf = p_ref[...].astype(F32)
        w.ready(1)
        dw2_ref[...] = _dot((pf * pf).astype(BF16), dhb_ref[...], TN).astype(BF16)

    panel = pl.BlockSpec((S, fw), lambda j: (0, j))
    return _pcall(
        body, name=name, ins=[hn, da, p, dhb], dep=dep, grid=(NDEV,),
        in_specs=[ANY, panel, panel, ANY],
        out_specs=[pl.BlockSpec((None, D, fw), lambda j: (j, 0, 0)), pl.BlockSpec((None, c.fr, D), lambda j: (j, 0, 0))],
        out_shape=[jax.ShapeDtypeStruct((NDEV, D, fw), BF16), jax.ShapeDtypeStruct((NDEV, c.fr, D), BF16)],
        scratch_shapes=[pltpu.VMEM((S, D), BF16), pltpu.VMEM((S, D), BF16), pltpu.SemaphoreType.DMA((2,))],
        compiler_params=_params(1))


def _attn_bwd(c, name, q, kv, do, bias, sinks, dep=None):
    S, dq, dkv = c.S, c.DQ, c.DKV
    nb = c.nb
    scale = HEAD_DIM ** -0.5

    def body(q_ref, kvc_ref, kvp_ref, do_ref, bias_ref, sink_ref, dq_ref, dkv_ref, dbias_ref, dsink_ref, dsink_acc,
             ds_sc, p_sc, qm_sc, dom_sc):
        n = pl.program_id(0)

        @pl.when(n == 0)
        def _():
            dkv_ref[...] = jnp.zeros_like(dkv_ref)
            dbias_ref[...] = jnp.zeros_like(dbias_ref)
            dsink_acc[...] = jnp.zeros_like(dsink_acc)

        low = _low_lanes()
        rows_c = pl.ds(pl.multiple_of(n * BLOCK, BLOCK), BLOCK)
        rows_p = pl.ds(pl.multiple_of(jnp.maximum(n - 1, 0) * BLOCK, BLOCK), BLOCK)
        for kvp in range(KV_PAIRS):
            k2, v2 = _pair_kv(kvc_ref, kvp_ref, kvp, dkv)
            for g in range(Q_PER_KV):
                grp = kvp * Q_PER_KV + g
                halves = []
                for par in range(2):
                    rows = slice((2 * g + par) * BLOCK, (2 * g + par + 1) * BLOCK)
                    qh = _head_operand(q_ref, grp, par, low, scale=scale)
                    doh = _head_operand(do_ref, grp, par, low)
                    p, ps = _head_probs(qh, k2, bias_ref[kvp, rows, :], sink_ref[(2 * kvp + par) * Q_PER_KV + g])
                    dp = _dot(doh, v2, NT)
                    delta = jnp.sum(p * dp, axis=-1, keepdims=True)
                    ds = p * (dp - delta)
                    dbias_ref[kvp, rows, :] += ds
                    dsink_acc[rows, kvp:kvp + 1] += -(ps * delta)
                    ds16 = ds.astype(BF16)
                    halves.append(_dot(ds16, k2, NN) * scale)
                    ds_sc[rows, :] = ds16
                    p_sc[rows, :] = p.astype(BF16)
                    qm_sc[rows, :] = qh
                    dom_sc[rows, :] = doh
                dq_ref[:, grp * LANES:(grp + 1) * LANES] = jnp.where(low, halves[0], halves[1]).astype(BF16)
            dk2 = _dot(ds_sc[...], qm_sc[...], TN)
            dv2 = _dot(p_sc[...], dom_sc[...], TN)
            lanes = slice(kvp * LANES, (kvp + 1) * LANES)
            vlanes = slice(dkv + kvp * LANES, dkv + (kvp + 1) * LANES)
            dkv_ref[rows_p, lanes] += dk2[:BLOCK]
            dkv_ref[rows_c, lanes] += dk2[BLOCK:]
            dkv_ref[rows_p, vlanes] += dv2[:BLOCK]
            dkv_ref[rows_c, vlanes] += dv2[BLOCK:]

        @pl.when(n == nb - 1)
        def _():
            dsink_ref[...] = jnp.sum(dsink_acc[...].reshape(2 * Q_PER_KV, BLOCK, KV_PAIRS), axis=1)

    return _pcall(
        body, name=name, ins=[q, kv, kv, do, bias, sinks], dep=dep, grid=(nb,),
        in_specs=_attn_specs(c) + [pl.BlockSpec((BLOCK, dq), lambda n: (n, 0)), _bias_spec(), SMEM],
        out_specs=[pl.BlockSpec((BLOCK, dq), lambda n: (n, 0)), pl.BlockSpec((S, 2 * dkv), lambda n: (0, 0)),
                   pl.BlockSpec((KV_PAIRS, PAIR_ROWS, 2 * BLOCK), lambda n: (0, 0, 0)),
                   pl.BlockSpec((2 * Q_PER_KV, KV_PAIRS), lambda n: (0, 0))],
        out_shape=[jax.ShapeDtypeStruct((S, dq), BF16), jax.ShapeDtypeStruct((S, 2 * dkv), F32),
                   jax.ShapeDtypeStruct((KV_PAIRS, PAIR_ROWS, 2 * BLOCK), F32),
                   jax.ShapeDtypeStruct((2 * Q_PER_KV, KV_PAIRS), F32)],
        scratch_shapes=[pltpu.VMEM((PAIR_ROWS, KV_PAIRS), F32), pltpu.VMEM((PAIR_ROWS, 2 * BLOCK), BF16),
                        pltpu.VMEM((PAIR_ROWS, 2 * BLOCK), BF16), pltpu.VMEM((PAIR_ROWS, LANES), BF16),
                        pltpu.VMEM((PAIR_ROWS, LANES), BF16)],
        compiler_params=_params(1))


def _sgu_bwd(c, name, a, z, dgated, ln_g, wc, wc_t, b_t, dep=None):
    S, AW, gd, tm = c.S, c.AW, c.gd, c.tms

    def body(a_ref, z_ref, dg_ref, lng_ref, wc_ref, wct_ref, bt_ref, dz_ref, dws_ref, dbt_ref, dlng_ref, dvn_ref):
        i = pl.program_id(0)

        @pl.when(i == 0)
        def _():
            dws_ref[...] = jnp.zeros_like(dws_ref)
            dbt_ref[...] = jnp.zeros_like(dbt_ref)
            dlng_ref[...] = jnp.zeros_like(dlng_ref)

        lng = lng_ref[...]
        va = a_ref[:, AW:].astype(F32)
        xc = va - jnp.mean(va, axis=-1, keepdims=True)
        rstd = lax.rsqrt(jnp.mean(xc * xc, axis=-1, keepdims=True) + EPS)
        xh = xc * rstd
        vn = (xh * lng).astype(BF16)
        causal = _sgu_masks()
        for ch in range(tm // CHUNK):
            rows = slice(ch * CHUNK, (ch + 1) * CHUNK)
            for g in range(A_GROUPS):
                cols = slice(g * gd, (g + 1) * gd)
                blk = vn[rows, cols]
                mixed = _dot(wc_ref[g], blk, NN) + bt_ref[:, g:g + 1]
                dgb = dg_ref[rows, cols].astype(F32)
                dm = dgb * a_ref[rows, cols].astype(F32)
                dbt_ref[:, g:g + 1] += jnp.sum(dm, axis=1, keepdims=True)
                dm16 = dm.astype(BF16)
                dws_ref[g] += jnp.where(causal, _dot(dm16, blk, NT), 0.0)
                dvn_ref[rows, cols] = _dot(wct_ref[g], dm16, NN)
                dz_ref[rows, cols] = (dgb * mixed * _gelu_grad(z_ref[rows, cols].astype(F32))).astype(BF16)
        dvn = dvn_ref[...]
        dlng_ref[...] += jnp.sum(dvn * xh, axis=0, keepdims=True)
        dxh = dvn * lng
        dva = rstd * (dxh - jnp.mean(dxh, axis=-1, keepdims=True) - xh * jnp.mean(dxh * xh, axis=-1, keepdims=True))
        dz_ref[:, AW:] = (dva * _gelu_grad(z_ref[:, AW:].astype(F32))).astype(BF16)

    wide = pl.BlockSpec((tm, 2 * AW), lambda i: (i, 0))
    wsp = pl.BlockSpec((A_GROUPS, CHUNK, CHUNK), lambda i: (0, 0, 0))
    btsp = pl.BlockSpec((CHUNK, A_GROUPS), lambda i: (0, 0))
    return _pcall(
        body, name=name, ins=[a, z, dgated, ln_g, wc, wc_t, b_t], dep=dep, grid=(S // tm,),
        in_specs=[wide, wide, pl.BlockSpec((tm, AW), lambda i: (i, 0)), pl.BlockSpec((1, AW), lambda i: (0, 0)),
                  wsp, wsp, btsp],
        out_specs=[wide, wsp, btsp, pl.BlockSpec((1, AW), lambda i: (0, 0))],
        out_shape=[jax.ShapeDtypeStruct((S, 2 * AW), BF16), jax.ShapeDtypeStruct((A_GROUPS, CHUNK, CHUNK), F32),
                   jax.ShapeDtypeStruct((CHUNK, A_GROUPS), F32), jax.ShapeDtypeStruct((1, AW), F32)],
        scratch_shapes=[pltpu.VMEM((tm, AW), F32)], compiler_params=_params(1))


def _adamw(name, parts, part_block, part_index, w, m, v, tr, row_off=0, n_rows=None, prev=None, dep=None):
    R, C = w.shape
    n_rows = R if n_rows is None else n_rows
    assert n_rows % tr == 0 and row_off % tr == 0
    bc1 = 1.0 - ADAM_B1 ** ADAM_STEP
    bc2 = 1.0 - ADAM_B2 ** ADAM_STEP

    def body(p_ref, w_ref, m_ref, v_ref, *rest):
        g_ref, d_ref, nm_ref, nv_ref = rest[-4:]
        g = p_ref[0].astype(F32)
        for s in range(1, part_block[0]):
            g = g + p_ref[s].astype(F32)
        nm = ADAM_B1 * m_ref[...] + (1.0 - ADAM_B1) * g
        nv = ADAM_B2 * v_ref[...] + (1.0 - ADAM_B2) * (g * g)
        g_ref[...] = g
        nm_ref[...] = nm
        nv_ref[...] = nv
        d_ref[...] = -ADAM_LR * ((nm * (1.0 / bc1)) / (jnp.sqrt(nv * (1.0 / bc2)) + ADAM_EPS) + ADAM_WD * w_ref[...])

    ob = row_off // tr
    row = pl.BlockSpec((tr, C), lambda i: (ob + i, 0))
    out = jax.ShapeDtypeStruct((R, C), F32)
    chained = prev is not None
    return _pcall(
        body, name=name, ins=[parts, w, m, v] + (list(prev) if chained else []), dep=dep, grid=(n_rows // tr,),
        in_specs=[pl.BlockSpec(part_block, part_index), row, row, row] + ([ANY] * 4 if chained else []),
        out_specs=[row, row, row, row], out_shape=[out, out, out, out],
        input_output_aliases={4 + t: t for t in range(4)} if chained else {}, compiler_params=_params(1))


def _sum_parts(name, parts, dep=None):
    def body(p_ref, out_ref):
        g = p_ref[0]
        for s in range(1, parts.shape[0]):
            g = g + p_ref[s]
        out_ref[...] = g

    return _pcall(body, name=name, ins=[parts], in_specs=[_whole(parts)], dep=dep,
                  out_shape=jax.ShapeDtypeStruct(parts.shape[1:], F32), compiler_params=_params(0))


def _place():
    return lax.axis_index("x"), lax.axis_index("y"), lax.axis_index("c")


def _slot(px, py, pc):
    return 4 * px + 2 * py + pc


def _peer(k, x, y, c):
    return x ^ ((k >> 2) & 1), y ^ ((k >> 1) & 1), c ^ (k & 1)


SEND_PEERS = {"exchange": tuple(range(1, NDEV)), "gather": (1, 2, 4, 6), "forward": (2, 4, 6),
              "broadcast": tuple(range(1, NDEV))}


def _send_copies(mode, src_refs, land_refs, send_sems, recv_sems):
    x, y, c = _place()
    me = _slot(x, y, c)
    peers = SEND_PEERS[mode]
    copies = []
    for i, k in enumerate(peers):
        peer = _peer(k, x, y, c)
        for a, land in enumerate(land_refs):
            if mode == "exchange":
                src, dst, to = src_refs[a].at[_slot(*peer)], land.at[me], peer
            elif mode in ("gather", "broadcast"):
                src, dst, to = src_refs[a], land.at[me], peer
            else:
                src = dst = land.at[_slot(*peer)]
                to = (x, y, 1 - c)
            s = a * len(peers) + i
            copies.append(pltpu.make_async_remote_copy(src_ref=src, dst_ref=dst, send_sem=send_sems.at[s],
                                                       recv_sem=recv_sems.at[s], device_id=to, device_id_type=MESH))
    return copies


def _send_start_groups(name, groups, mode, dep=None):
    sizes = [(len(s), len(l)) for s, l in groups]
    flat = [t for s, l in groups for t in (*s, *l)]
    n_in, ng = len(flat), len(groups)

    def body(*refs):
        sems, token, at = refs[n_in:n_in + 2 * ng], refs[-1], 0
        for gi, (ns, nl) in enumerate(sizes):
            for cp in _send_copies(mode, refs[at:at + ns], refs[at + ns:at + ns + nl], sems[2 * gi], sems[2 * gi + 1]):
                cp.start()
            at += ns + nl
        token[...] = jnp.zeros_like(token)

    sem_shapes = [pltpu.SemaphoreType.DMA((nl * len(SEND_PEERS[mode]),)) for _, nl in sizes for _ in range(2)]
    if any(dep is t for t in flat):
        dep = None
    out = _pcall(
        body, name=name, ins=[pltpu.with_memory_space_constraint(t, pltpu.HBM) for t in flat],
        in_specs=[HBM] * n_in, dep=dep,
        out_shape=(*sem_shapes, *[pltpu.HBM(t.shape, t.dtype) for t in flat], jax.ShapeDtypeStruct((8, LANES), F32)),
        out_specs=(*[SEM] * (2 * ng), *[HBM] * n_in, pl.BlockSpec(memory_space=pltpu.VMEM)),
        input_output_aliases={i: 2 * ng + i for i in range(n_in)},
        compiler_params=pltpu.CompilerParams(has_side_effects=pltpu.SideEffectType.DATAFLOW_SIDE_EFFECTING))
    started, at = [], 2 * ng
    for gi, (ns, nl) in enumerate(sizes):
        started.append((out[-1], out[2 * gi], out[2 * gi + 1], list(out[at:at + ns]), list(out[at + ns:at + ns + nl])))
        at += ns + nl
    return started


def _send_start(name, srcs, lands, mode, dep=None):
    return _send_start_groups(name, [(srcs, lands)], mode, dep=dep)[0]


def _send_wait(name, started, mode, dep=None):
    _, send_sems, recv_sems, srcs, lands = started
    n_src, n = len(srcs), len(lands)

    def body(*refs):
        src_refs, land_refs = refs[:n_src], refs[n_src:n_src + n]
        ssem, rsem = refs[n_src + n], refs[n_src + n + 1]
        for cp in _send_copies(mode, src_refs, land_refs, ssem, rsem):
            cp.wait_send()
            cp.wait_recv()

    thru = [pltpu.HBM(t.shape, t.dtype) for t in [*srcs, *lands]]
    out = _pcall(
        body, name=name, ins=[*srcs, *lands, send_sems, recv_sems], in_specs=[HBM] * (n_src + n) + [SEM, SEM], dep=dep,
        out_shape=tuple(thru), out_specs=tuple([HBM] * (n_src + n)),
        input_output_aliases={i: i for i in range(n_src + n)},
        compiler_params=pltpu.CompilerParams(has_side_effects=pltpu.SideEffectType.DATAFLOW_SIDE_EFFECTING))
    return list(out[n_src:])


def _landing(block, me):
    zone = lax.empty((NDEV, *block.shape), block.dtype)
    return lax.dynamic_update_slice(zone, block[None], (me,) + (0,) * block.ndim)


def _rows128(t):
    flat = t.reshape(-1)
    n = flat.shape[0]
    rows = -(-n // (8 * LANES)) * 8
    return jnp.pad(flat, (0, rows * LANES - n)).reshape(rows, LANES)


def kernel(x, mix_norm_g, ffn_norm_g, a_w_in, a_ln_g, a_w_spatial, a_b_spatial, a_w_out, kv_norm_g, w_k, w_v, b_w_q, b_sinks, b_w_o, rel_bias, ffn_w1, ffn_w2, final_norm_g, loss_target, m_mix_norm_g, m_ffn_norm_g, m_a_w_in, m_a_ln_g, m_a_w_spatial, m_a_b_spatial, m_a_w_out, m_kv_norm_g, m_w_k, m_w_v, m_b_w_q, m_b_sinks, m_b_w_o, m_rel_bias, m_ffn_w1, m_ffn_w2, m_final_norm_g, v_mix_norm_g, v_ffn_norm_g, v_a_w_in, v_a_ln_g, v_a_w_spatial, v_a_b_spatial, v_a_w_out, v_kv_norm_g, v_w_k, v_w_v, v_b_w_q, v_b_sinks, v_b_w_o, v_rel_bias, v_ffn_w1, v_ffn_w2, v_final_norm_g):
    c = _config(x, a_w_in, a_w_out, w_k, b_w_q, b_w_o, ffn_w1, ffn_w2)
    S, D, LA, LB, LF = c.S, c.D, c.LA, c.LB, c.LF
    weights = dict(mix_norm_g=mix_norm_g, ffn_norm_g=ffn_norm_g, a_w_in=a_w_in, a_ln_g=a_ln_g, a_w_spatial=a_w_spatial,
                   a_b_spatial=a_b_spatial, a_w_out=a_w_out, kv_norm_g=kv_norm_g, w_k=w_k, w_v=w_v, b_w_q=b_w_q,
                   b_sinks=b_sinks, b_w_o=b_w_o, rel_bias=rel_bias, ffn_w1=ffn_w1, ffn_w2=ffn_w2,
                   final_norm_g=final_norm_g)
    m_in = dict(mix_norm_g=m_mix_norm_g, ffn_norm_g=m_ffn_norm_g, a_w_in=m_a_w_in, a_ln_g=m_a_ln_g,
                a_w_spatial=m_a_w_spatial, a_b_spatial=m_a_b_spatial, a_w_out=m_a_w_out, kv_norm_g=m_kv_norm_g,
                w_k=m_w_k, w_v=m_w_v, b_w_q=m_b_w_q, b_sinks=m_b_sinks, b_w_o=m_b_w_o, rel_bias=m_rel_bias,
                ffn_w1=m_ffn_w1, ffn_w2=m_ffn_w2, final_norm_g=m_final_norm_g)
    v_in = dict(mix_norm_g=v_mix_norm_g, ffn_norm_g=v_ffn_norm_g, a_w_in=v_a_w_in, a_ln_g=v_a_ln_g,
                a_w_spatial=v_a_w_spatial, a_b_spatial=v_a_b_spatial, a_w_out=v_a_w_out, kv_norm_g=v_kv_norm_g,
                w_k=v_w_k, w_v=v_w_v, b_w_q=v_b_w_q, b_sinks=v_b_sinks, b_w_o=v_b_w_o, rel_bias=v_rel_bias,
                ffn_w1=v_ffn_w1, ffn_w2=v_ffn_w2, final_norm_g=v_final_norm_g)
    names = list(weights)
    seq = _Seq()
    me = _slot(*_place())
    bf = lambda t: t.astype(BF16)

    tr = lambda t: bf(jnp.swapaxes(t, -1, -2))
    groups = []
    for l in range(LA):
        groups += [[tr(a_w_in[l])[None]] + ([a_ln_g] if l == 0 else []), [bf(a_w_out[l])],
                   [tr(ffn_w1[l])[None], bf(ffn_w2[l])]]
    gb = 3 * LA
    for l in range(LB):
        extra = [bf(jnp.concatenate([w_k, w_v], axis=1))] if l == 0 else []
        groups += [extra + [bf(b_w_q[l]), bf(b_w_o[l])], [tr(ffn_w1[LA + l])[None], bf(ffn_w2[LA + l])]]
    started = seq(_send_start_groups, "weights_start", [(grp, [_landing(t, me) for t in grp]) for grp in groups],
                  "gather")
    seq.last = started[0][0]
    forwarding = {}

    def forward(i):
        lands = seq(_send_wait, f"weights_wait{i}", started[i], "gather")
        forwarding[i] = seq(_send_start, f"weights_forward{i}", [], lands, "forward")

    def arrive(i):
        if i not in forwarding:
            forward(i)
        return seq(_send_wait, f"weights_arrive{i}", forwarding[i], "forward")

    causal = jnp.tril(jnp.ones((CHUNK, CHUNK), bool))
    wsp = jnp.where(causal[None, None], a_w_spatial, 0.0)
    wsp16 = wsp.astype(BF16)
    wsp16_t = jnp.swapaxes(wsp, -1, -2).astype(BF16)
    bsp_t = jnp.swapaxes(a_b_spatial, -1, -2)
    mix_g = mix_norm_g.reshape(-1, 1, D)
    ffn_g = ffn_norm_g.reshape(-1, 1, D)
    kv_g = kv_norm_g.reshape(1, D)
    fin_g = final_norm_g.reshape(1, D)
    onehot = _bucket_onehot()
    bias = _slot_bias(seq(_band_bias, rel_bias.T, onehot).reshape(N_HEADS, BLOCK, 2 * BLOCK))

    h = x.reshape(S, D)
    sav_a, sav_b, wts_a, wts_b = [], [], [], []
    for l in range(LA):
        got = arrive(3 * l)
        w_in = got[0]
        if l == 0:
            ln_g_full = jnp.transpose(got[1], (1, 0, 2)).reshape(LA, 1, c.AW)
        z, a, hn = seq(_a_in_fwd, c, f"a_in_fwd{l}", h, mix_g[l], w_in, 0)
        forward(3 * l + 1)
        gated = seq(_sgu_fwd, c, f"sgu_fwd{l}", a, ln_g_full[l], wsp16[l], bsp_t[l])
        (wout,) = arrive(3 * l + 1)
        if l > 0:
            forward(3 * l + 2)
        h1 = seq(_mm_res, c, f"a_out_fwd{l}", gated, wout, c.ar, 0, h)
        w1, rows = arrive(3 * l + 2)
        if l == LA - 1:
            forward(gb)
        p, h2, hnf = seq(_ffn_fwd, c, f"ffn_fwd{l}", h1, ffn_g[l], w1, 0, rows)
        sav_a.append((h, z, a, hn, gated, h1, p, hnf))
        wts_a.append((w_in, 0, w1, 0, rows, wout, 0))
        h = h2
    h_kv = h
    for l in range(LB):
        got = arrive(gb + 2 * l)
        if l == 0:
            wkv, got = got[0], got[1:]
            kv, hkv = seq(_rms_mm_rows, c, "kv_fwd", h, kv_g, wkv, c.kr, 0, 2 * c.DKV)
        wq, wo = _slot_cols(got[0]), _slot_rows(got[1])
        q, hn = seq(_rms_mm_rows, c, f"q_fwd{l}", h, mix_g[LA + l], wq, c.qr, 0, c.DQ)
        forward(gb + 2 * l + 1)
        o = seq(_attn_fwd, c, f"attn_fwd{l}", q, kv, bias, b_sinks[l])
        h1 = seq(_mm_res, c, f"o_fwd{l}", o, wo, c.orr, 0, h)
        w1, rows = arrive(gb + 2 * l + 1)
        if l + 1 < LB:
            forward(gb + 2 * l + 2)
        p, h2, hnf = seq(_ffn_fwd, c, f"ffn_fwd{LA + l}", h1, ffn_g[LA + l], w1, 0, rows)
        sav_b.append((h, q, hn, o, h1, p, hnf))
        wts_b.append((wq, wo, w1, rows))
        h = h2
    dh, d_fin_g, loss_row = seq(_final_loss, c, h, fin_g, loss_target.reshape(S, D))
    loss = lax.psum(loss_row[0, 0], MESH_AXES)

    results = {}
    in_flight = []

    def update(k, parts, layer, col_blk=0):
        w = weights[k]
        rows_l, ncols = (w.shape[-2], w.shape[-1]) if w.ndim == 3 else w.shape
        flat = lambda t: t.reshape(-1, ncols)
        tr = min(256, rows_l)
        results[k] = seq(_adamw, f"adamw_{k}{layer}", parts, (NDEV, tr, ncols), lambda i: (0, i, col_blk),
                         flat(w), flat(m_in[k]), flat(v_in[k]), tr, row_off=layer * rows_l, n_rows=rows_l,
                         prev=results.get(k))

    def land(tag, entry):
        lands = seq(_send_wait, f"grads_wait_{tag}", entry[1], "exchange")
        for keys, parts in zip(entry[0], lands):
            for k, layer, col_blk in keys:
                update(k, parts, layer, col_blk)

    def send(tag, items):
        slabs = [t for _, t in items]
        own = [_landing(lax.dynamic_index_in_dim(t, me, 0, keepdims=False), me) for t in slabs]
        st = seq(_send_start, f"grads_start_{tag}", slabs, own, "exchange")
        in_flight.append((tag, ([k for k, _ in items], st)))
        while len(in_flight) > EXCHANGE_LAG:
            land(*in_flight.pop(0))

    d_mix_g, d_ffn_g = [None] * LF, [None] * LF
    dkv_list, dbias_list, dsink_list = [], [], [None] * LB

    def ffn_bwd(lf, dh, h1, p, hnf, w1, w1_i, rows):
        da, dh1, d_ffn_g[lf], dhb = seq(_ffn_bwd_data, c, f"ffn_bwd_data{lf}", dh, p, w1, w1_i, rows, h1, ffn_g[lf])
        dw1, dw2 = seq(_ffn_bwd_w, c, f"ffn_bwd_w{lf}", hnf, da, p, dhb)
        send(f"ffn{lf}", [([("ffn_w1", lf, 0)], dw1), ([("ffn_w2", lf, 0)], dw2)])
        return dh1

    for l in reversed(range(LB)):
        h0, q, hn, o, h1, p, hnf = sav_b[l]
        wq, wo, w1, rows = wts_b[l]
        dh1 = ffn_bwd(LA + l, dh, h1, p, hnf, w1, 0, rows)
        do = seq(_bwd_rows_data, c, f"o_bwd_data{l}", dh1, wo, c.orr, 0)
        dwo = _unslot_rows(seq(_wgrad_rows, c, f"o_bwd_w{l}", o, [dh1], c.DQ, D))
        dq, dkv, dbias, dsink = seq(_attn_bwd, c, f"attn_bwd{l}", q, kv, do, bias, b_sinks[l])
        dsink_list[l] = dsink.reshape(Q_PER_KV, 2, KV_PAIRS).transpose(2, 1, 0).reshape(1, N_HEADS)
        dkv_list.append(dkv)
        dbias_list.append(_unslot_bias(dbias))
        dwq = _unslot_cols(seq(_wgrad_rows, c, f"q_bwd_w{l}", hn, [dq], D, c.DQ))
        send(f"attn{l}", [([("b_w_o", l, 0)], dwo), ([("b_w_q", l, 0)], dwq)])
        dh, d_mix_g[LA + l] = seq(_bwd_rows_to_stream, c, f"q_bwd_data{l}", [dq], wq, c.qr, 0, c.DQ, h0,
                                  mix_g[LA + l], dh1)
    dwkv = seq(_wgrad_rows, c, "kv_bwd_w", hkv, dkv_list, D, 2 * c.DKV)
    send("kv", [([("w_k", 0, 0), ("w_v", 0, 1)], dwkv)])
    dh, d_kv_g = seq(_bwd_rows_to_stream, c, "kv_bwd_data", dkv_list, wkv, c.kr, 0, 2 * c.DKV, h_kv, kv_g, dh)
    d_rel_t = seq(_band_bias_grad, dbias_list, onehot)
    d_wsp, d_bsp, d_lng = [None] * LA, [None] * LA, [None] * LA
    for l in reversed(range(LA)):
        h0, z, a, hn, gated, h1, p, hnf = sav_a[l]
        w_in, in_i, w1, w1_i, rows, wout, wout_i = wts_a[l]
        dh1 = ffn_bwd(l, dh, h1, p, hnf, w1, w1_i, rows)
        dgated = seq(_bwd_rows_data, c, f"a_out_bwd_data{l}", dh1, wout, c.ar, wout_i)
        dwout = seq(_wgrad_rows, c, f"a_out_bwd_w{l}", gated, [dh1], c.AW, D)
        send(f"a_out{l}", [([("a_w_out", l, 0)], dwout)])
        dz, d_wsp[l], dbt, d_lng[l] = seq(_sgu_bwd, c, f"sgu_bwd{l}", a, z, dgated, ln_g_full[l], wsp16[l],
                                          wsp16_t[l], bsp_t[l])
        d_bsp[l] = dbt.T
        dwin = seq(_wgrad_cols, c, f"a_in_bwd_w{l}", hn, dz)
        send(f"a_in{l}", [([("a_w_in", l, 0)], dwin)])
        dh, d_mix_g[l] = seq(_bwd_cols_to_stream, c, f"a_in_bwd_data{l}", dz, w_in, in_i, h0, mix_g[l], dh1)
    grad_x = dh.reshape(1, S, D)

    small = {
        "mix_norm_g": jnp.concatenate(d_mix_g, axis=0), "ffn_norm_g": jnp.concatenate(d_ffn_g, axis=0),
        "a_w_spatial": jnp.stack(d_wsp), "a_b_spatial": jnp.stack(d_bsp), "kv_norm_g": d_kv_g,
        "b_sinks": jnp.concatenate(dsink_list, axis=0), "rel_bias": d_rel_t.T, "final_norm_g": d_fin_g,
    }
    small_names = list(small)
    packs = [_rows128(small[k]) for k in small_names] + [_rows128(jnp.concatenate(d_lng, axis=0))]
    offs = np.cumsum([0] + [p.shape[0] for p in packs])
    tail_rows = int(-offs[-1] % (8 * NDEV)) + packs[-1].shape[0]
    Rs = int(offs[-2]) + tail_rows
    packed = jnp.concatenate(packs + [jnp.zeros((Rs - int(offs[-1]), LANES), F32)], axis=0)
    slab = packed.reshape(NDEV, Rs // NDEV, LANES)
    st = seq(_send_start, "small_grads_start", [slab],
             [_landing(lax.dynamic_index_in_dim(slab, me, 0, keepdims=False), me)], "exchange")
    while in_flight:
        land(*in_flight.pop(0))
    (parts,) = seq(_send_wait, "small_grads_wait", st, "exchange")
    mine = seq(_sum_parts, "small_grads_sum", parts)
    st = seq(_send_start, "small_sums_start", [mine], [_landing(mine, me)], "broadcast")
    (sums,) = seq(_send_wait, "small_sums_wait", st, "broadcast")
    small_all = sums.reshape(1, Rs, LANES)

    grads, deltas, new_m, new_v = {}, {}, {}, {}

    def put(k, outs, shape):
        grads[k], deltas[k], new_m[k], new_v[k] = (t.reshape(shape) for t in outs)

    def pack_state(d):
        return jnp.concatenate([_rows128(d[k]) for k in small_names] + [jnp.zeros((tail_rows, LANES), F32)], axis=0)

    outs = seq(_adamw, "adamw_small", small_all, (1, Rs, LANES), lambda i: (0, 0, 0),
               pack_state(weights), pack_state(m_in), pack_state(v_in), Rs)
    for n_, k in enumerate(small_names):
        shape = weights[k].shape
        size = int(np.prod(shape))
        put(k, [t[int(offs[n_]):int(offs[n_ + 1])].reshape(-1)[:size] for t in outs], shape)
    lng_sum = outs[0][int(offs[-2]):int(offs[-1])].reshape(-1)[:LA * c.AW].reshape(LA, c.AW)
    lng_mine = lax.dynamic_slice_in_dim(lng_sum, me * c.ar, c.ar, axis=1)
    lng_parts = jnp.concatenate([lng_mine[None], jnp.zeros((NDEV - 1, LA, c.ar), F32)], axis=0)
    put("a_ln_g", seq(_adamw, "adamw_ln_g", lng_parts, (NDEV, LA, c.ar), lambda i: (0, 0, 0),
                      a_ln_g, m_in["a_ln_g"], v_in["a_ln_g"], LA), a_ln_g.shape)
    for k in ("a_w_in", "ffn_w1", "ffn_w2", "a_w_out", "b_w_o", "b_w_q", "w_k", "w_v"):
        put(k, results[k], weights[k].shape)

    return (loss, grad_x, *[grads[k] for k in names], *[deltas[k] for k in names],
            *[new_m[k] for k in names], *[new_v[k] for k in names])
```

```python
import numpy as np
import math
import jax
import jax.numpy as jnp
from jax import lax
from jax.experimental import pallas as pl
from jax.experimental.pallas import tpu as pltpu

F32 = jnp.float32
BF16 = jnp.bfloat16

NDEV = 8
EPS = 1e-6
CHUNK = 128
A_GROUPS = 8
N_HEADS = 16
N_KV_HEADS = 4
Q_PER_KV = N_HEADS // N_KV_HEADS
HEAD_DIM = 64
BLOCK = 128
N_BUCKETS = 32
MAX_DISTANCE = 128
ADAM_LR, ADAM_B1, ADAM_B2, ADAM_EPS, ADAM_WD, ADAM_STEP = 0.001, 0.9, 0.999, 1e-08, 0.01, 10
LANES = 128
VMEM_LIMIT = 56 * 1024 * 1024
INV_SQRT2 = 0.7071067811865476
INV_SQRT_2PI = 0.3989422804014327
MESH_AXES = ("x", "y", "c")
EXCHANGE_LAG = 2

HBM = pl.BlockSpec(memory_space=pltpu.HBM)
SMEM = pl.BlockSpec(memory_space=pltpu.SMEM)
ANY = pl.BlockSpec(memory_space=pl.ANY)
SEM = pl.BlockSpec(memory_space=pltpu.SEMAPHORE)
MESH = pl.DeviceIdType.MESH


def _params(n_grid):
    return pltpu.CompilerParams(dimension_semantics=("arbitrary",) * n_grid, vmem_limit_bytes=VMEM_LIMIT)


def _const(block, index_map):
    return pl.BlockSpec(block, index_map, pipeline_mode=pl.Buffered(1))


def _pcall(body, *, ins, in_specs, dep=None, **kw):
    n_in = len(ins)
    if dep is None or any(dep is t for t in ins):
        return pl.pallas_call(body, in_specs=list(in_specs), **kw)(*ins)

    def with_dep(*refs):
        body(*refs[:n_in], *refs[n_in + 1:])

    return pl.pallas_call(with_dep, in_specs=[*in_specs, ANY], **kw)(*ins, dep)


class _Seq:
    def __init__(self):
        self.last = None

    def __call__(self, fn, *args, **kw):
        out = fn(*args, dep=self.last, **kw)
        self.last = out[0] if isinstance(out, (tuple, list)) else out
        return out


def _rstd(h):
    return lax.rsqrt(jnp.mean(h * h, axis=-1, keepdims=True) + EPS)


def _rms_bwd(dhn, h, g, dres):
    r = _rstd(h)
    xh = h * r
    dg = jnp.sum(dhn * xh, axis=0, keepdims=True)
    dxh = dhn * g
    dx = r * (dxh - xh * jnp.mean(dxh * xh, axis=-1, keepdims=True))
    return dres + dx, dg


def _gelu(z):
    return 0.5 * z * (1.0 + lax.erf(z * INV_SQRT2))


def _gelu_grad(z):
    return 0.5 * (1.0 + lax.erf(z * INV_SQRT2)) + z * (jnp.exp(-0.5 * z * z) * INV_SQRT_2PI)


def _dot(a, b, dims):
    return lax.dot_general(a, b, (dims, ((), ())), preferred_element_type=F32)


NN = ((1,), (0,))
NT = ((1,), (1,))
TN = ((0,), (0,))


def _mm(name, ins, in_specs, out_shapes, out_specs, *, grid, dims, nk, acc_shape, load_a, load_b, epilogue,
        dep=None):
    n_in, n_out = len(ins), len(out_shapes)
    kax = len(grid) - 1

    def body(*refs):
        in_refs = refs[:n_in]
        out_refs = refs[n_in:n_in + n_out]
        a = load_a(in_refs, out_refs)
        b = load_b(in_refs)
        prod = _dot(a, b, dims)
        if nk == 1:
            epilogue(prod, in_refs, out_refs)
        else:
            acc = refs[n_in + n_out]
            k = pl.program_id(kax)

            @pl.when(k == 0)
            def _():
                acc[...] = prod

            @pl.when(k > 0)
            def _():
                acc[...] += prod

            @pl.when(k == nk - 1)
            def _():
                epilogue(acc[...], in_refs, out_refs)

    return _pcall(
        body, name=name, ins=ins, in_specs=in_specs, dep=dep, grid=grid, out_specs=out_specs, out_shape=out_shapes,
        scratch_shapes=[pltpu.VMEM(acc_shape, F32)] if nk > 1 else [], compiler_params=_params(len(grid)))


def _bf(ref_idx):
    return lambda in_refs, *_: in_refs[ref_idx][...].astype(BF16)


def _b_view(ref_idx, rows):
    def load(in_refs):
        b = in_refs[ref_idx][...]
        return b.reshape(rows, b.shape[-1])
    return load


class Cfg:
    pass


def _config(x, a_w_in, a_w_out, w_k, b_w_q, b_w_o, ffn_w1, ffn_w2):
    c = Cfg()
    c.S, c.D = x.shape[1], x.shape[2]
    c.LA, _, c.cw = a_w_in.shape
    c.AW2 = NDEV * c.cw
    c.AW = c.AW2 // 2
    c.gd = c.AW // A_GROUPS
    c.ar = a_w_out.shape[1]
    c.LF, _, c.fw = ffn_w1.shape
    c.fr = ffn_w2.shape[1]
    c.LB, c.qr, c.DQ = b_w_q.shape
    c.orr = b_w_o.shape[1]
    c.kr, c.DKV = w_k.shape
    c.tm = min(1024, c.S)
    c.tmw = min(512, c.S)
    c.tms = min(256, c.S)
    c.nb = c.S // BLOCK
    assert c.cw == c.fw == c.fr and c.AW == NDEV * c.ar and c.D == NDEV * c.qr == NDEV * c.kr
    assert c.DQ == NDEV * c.orr == N_HEADS * HEAD_DIM and c.DKV == N_KV_HEADS * HEAD_DIM
    assert c.S % c.tm == 0 and c.S % c.tmw == 0 and c.S % c.tms == 0 and c.tms % CHUNK == 0 and c.gd % LANES == 0
    assert c.LA >= 1 and c.LB >= 1 and c.LF == c.LA + c.LB
    return c


def _cached_rms(h_idx, g_idx, hn_out_idx, jax_axis=1):
    def load(in_refs, out_refs):
        hn_ref = out_refs[hn_out_idx]

        @pl.when(pl.program_id(jax_axis) == 0)
        def _():
            h = in_refs[h_idx][...]
            hn_ref[...] = (h * _rstd(h) * in_refs[g_idx][...]).astype(BF16)

        return hn_ref[...]
    return load


def _a_in_fwd(c, name, h, g, col, ci, dep=None):
    S, D, cw, tm = c.S, c.D, c.cw, c.tmw

    def body(h_ref, g_ref, w_ref, z_ref, a_ref, hn_ref):
        h = h_ref[...]
        hn = (h * _rstd(h) * g_ref[...]).astype(BF16)
        hn_ref[...] = hn
        for j in range(NDEV):
            cols = slice(j * cw, (j + 1) * cw)
            z = _dot(hn, w_ref[j], NT)
            z_ref[:, cols] = z.astype(BF16)
            a_ref[:, cols] = _gelu(z).astype(BF16)

    row = pl.BlockSpec((tm, D), lambda i: (i, 0))
    wide = pl.BlockSpec((tm, c.AW2), lambda i: (i, 0))
    return _pcall(
        body, name=name, ins=[h, g, col], dep=dep, grid=(S // tm,),
        in_specs=[row, pl.BlockSpec((1, D), lambda i: (0, 0)), _const((NDEV, None, cw, D), lambda i: (0, ci, 0, 0))],
        out_specs=[wide, wide, row],
        out_shape=[jax.ShapeDtypeStruct((S, c.AW2), BF16), jax.ShapeDtypeStruct((S, c.AW2), BF16),
                   jax.ShapeDtypeStruct((S, D), BF16)],
        compiler_params=_params(1))


def _rms_mm_rows(c, name, h, g, slab, blk_rows, blk_idx, n_out, dep=None):
    S, D, tm = c.S, c.D, c.tm

    def epilogue(acc, in_refs, out_refs):
        out_refs[0][...] = acc.astype(BF16)

    return _mm(
        name, [h, slab, g],
        [pl.BlockSpec((tm, D), lambda i, j, k: (i, 0)),
         pl.BlockSpec((NDEV, blk_rows, n_out), lambda i, j, k: (0, blk_idx, 0)),
         pl.BlockSpec((1, D), lambda i, j, k: (0, 0))],
        [jax.ShapeDtypeStruct((S, n_out), BF16), jax.ShapeDtypeStruct((S, D), BF16)],
        [pl.BlockSpec((tm, n_out), lambda i, j, k: (i, 0)), pl.BlockSpec((tm, D), lambda i, j, k: (i, 0))],
        grid=(S // tm, 1, 1), dims=NN, nk=1, acc_shape=None,
        load_a=_cached_rms(0, 2, 1), load_b=_b_view(1, NDEV * blk_rows), epilogue=epilogue, dep=dep)


def _mm_res(c, name, a, slab, blk_rows, blk_idx, res, dep=None):
    S, D, tm = c.S, c.D, c.tm
    K = NDEV * blk_rows

    def epilogue(acc, in_refs, out_refs):
        out_refs[0][...] = in_refs[2][...] + acc

    return _mm(
        name, [a, slab, res],
        [pl.BlockSpec((tm, K), lambda i, j, k: (i, 0)),
         pl.BlockSpec((NDEV, blk_rows, D), lambda i, j, k: (0, blk_idx, 0)),
         pl.BlockSpec((tm, D), lambda i, j, k: (i, 0))],
        [jax.ShapeDtypeStruct((S, D), F32)], [pl.BlockSpec((tm, D), lambda i, j, k: (i, 0))],
        grid=(S // tm, 1, 1), dims=NN, nk=1, acc_shape=None,
        load_a=_bf(0), load_b=_b_view(1, K), epilogue=epilogue, dep=dep)[0]


def _sgu_masks():
    ii = lax.broadcasted_iota(jnp.int32, (CHUNK, CHUNK), 0)
    jj = lax.broadcasted_iota(jnp.int32, (CHUNK, CHUNK), 1)
    return ii >= jj


def _sgu_fwd(c, name, a, ln_g, wc, b_t, dep=None):
    S, AW, gd, tm = c.S, c.AW, c.gd, c.tms

    def body(a_ref, lng_ref, wc_ref, bt_ref, out_ref):
        va = a_ref[:, AW:].astype(F32)
        xc = va - jnp.mean(va, axis=-1, keepdims=True)
        vn = (xc * lax.rsqrt(jnp.mean(xc * xc, axis=-1, keepdims=True) + EPS) * lng_ref[...]).astype(BF16)
        for ch in range(tm // CHUNK):
            rows = slice(ch * CHUNK, (ch + 1) * CHUNK)
            for g in range(A_GROUPS):
                cols = slice(g * gd, (g + 1) * gd)
                mixed = _dot(wc_ref[g], vn[rows, cols], NN) + bt_ref[:, g:g + 1]
                out_ref[rows, cols] = (a_ref[rows, cols].astype(F32) * mixed).astype(BF16)

    return _pcall(
        body, name=name, ins=[a, ln_g, wc, b_t], dep=dep, grid=(S // tm,),
        in_specs=[pl.BlockSpec((tm, 2 * AW), lambda i: (i, 0)), pl.BlockSpec((1, AW), lambda i: (0, 0)),
                  pl.BlockSpec((A_GROUPS, CHUNK, CHUNK), lambda i: (0, 0, 0)),
                  pl.BlockSpec((CHUNK, A_GROUPS), lambda i: (0, 0))],
        out_specs=pl.BlockSpec((tm, AW), lambda i: (i, 0)),
        out_shape=jax.ShapeDtypeStruct((S, AW), BF16), compiler_params=_params(1))


def _ffn_fwd(c, name, h, g, col, ci, rows, dep=None):
    S, D, fw, tm = c.S, c.D, c.fw, c.tmw
    F = NDEV * fw

    def body(h_ref, g_ref, w1_ref, w2_ref, p_ref, out_ref, hn_ref, r_ref):
        h = h_ref[...]
        hn = (h * _rstd(h) * g_ref[...]).astype(BF16)
        hn_ref[...] = hn
        for j in range(NDEV):
            cols = slice(j * fw, (j + 1) * fw)
            p = jnp.maximum(_dot(hn, w1_ref[j], NT), 0.0)
            p_ref[:, cols] = p.astype(BF16)
            r_ref[:, cols] = (p * p).astype(BF16)
        out_ref[...] = h + _dot(r_ref[...], w2_ref[...].reshape(F, D), NN)

    row = pl.BlockSpec((tm, D), lambda i: (i, 0))
    return _pcall(
        body, name=name, ins=[h, g, col, rows], dep=dep, grid=(S // tm,),
        in_specs=[row, pl.BlockSpec((1, D), lambda i: (0, 0)),
                  _const((NDEV, None, fw, D), lambda i: (0, ci, 0, 0)), _const((NDEV, c.fr, D), lambda i: (0, 0, 0))],
        out_specs=[pl.BlockSpec((tm, F), lambda i: (i, 0)), row, row],
        out_shape=[jax.ShapeDtypeStruct((S, F), BF16), jax.ShapeDtypeStruct((S, D), F32),
                   jax.ShapeDtypeStruct((S, D), BF16)],
        scratch_shapes=[pltpu.VMEM((tm, F), BF16)], compiler_params=_params(1))


def _bucket_table():
    qi = np.arange(BLOCK)[:, None]
    kj = np.arange(2 * BLOCK)[None, :]
    d = np.maximum(qi + BLOCK - kj, 0)
    max_exact = N_BUCKETS // 2
    ratio = np.log(np.maximum(d, 1).astype(np.float32) / np.float32(max_exact)) / np.float32(
        math.log(MAX_DISTANCE / max_exact))
    large = np.minimum(max_exact + (ratio.astype(np.float32) * np.float32(N_BUCKETS - max_exact)).astype(np.int32),
                       N_BUCKETS - 1)
    return np.where(d < max_exact, d, large).astype(np.int32)


def _bucket_onehot():
    b = jnp.asarray(_bucket_table().reshape(1, -1))
    return (b == lax.broadcasted_iota(jnp.int32, (N_BUCKETS, b.shape[1]), 0)).astype(F32)


def _whole(t):
    return pl.BlockSpec(t.shape, lambda: (0,) * t.ndim)


def _band_bias(rel_bias_t, onehot, dep=None):
    def body(r_ref, oh_ref, out_ref):
        out_ref[...] = lax.dot_general(r_ref[...], oh_ref[...], (NN, ((), ())), preferred_element_type=F32,
                                       precision=lax.Precision.HIGHEST)

    n = onehot.shape[1]
    return _pcall(body, name="band_bias", ins=[rel_bias_t, onehot], in_specs=[_whole(rel_bias_t), _whole(onehot)],
                  dep=dep, out_shape=jax.ShapeDtypeStruct((N_HEADS, n), F32), compiler_params=_params(0))


def _band_bias_grad(dbias_list, onehot, dep=None):
    n_in = len(dbias_list)

    def body(*refs):
        oh_ref, out_ref = refs[n_in], refs[n_in + 1]
        d = refs[0][...]
        for r in refs[1:n_in]:
            d = d + r[...]
        out_ref[...] = lax.dot_general(d, oh_ref[...], (NT, ((), ())), preferred_element_type=F32,
                                       precision=lax.Precision.HIGHEST)

    ins = [*dbias_list, onehot]
    return _pcall(body, name="band_bias_grad", ins=ins, in_specs=[_whole(t) for t in ins], dep=dep,
                  out_shape=jax.ShapeDtypeStruct((N_HEADS, N_BUCKETS), F32), compiler_params=_params(0))


KV_PAIRS = N_KV_HEADS // 2
PAIR_ROWS = 2 * Q_PER_KV * BLOCK
MASKED = float(np.finfo(np.float32).min) / 2


def _slot_cols(w):
    lead = w.shape[:-1]
    return w.reshape(*lead, KV_PAIRS, 2, Q_PER_KV, HEAD_DIM).swapaxes(-3, -2).reshape(*lead, N_HEADS * HEAD_DIM)


def _unslot_cols(w):
    lead = w.shape[:-1]
    return w.reshape(*lead, KV_PAIRS, Q_PER_KV, 2, HEAD_DIM).swapaxes(-3, -2).reshape(*lead, N_HEADS * HEAD_DIM)


def _slot_rows(blocks):
    n = blocks.shape[-1]
    return blocks.reshape(KV_PAIRS, 2, Q_PER_KV, HEAD_DIM, n).swapaxes(1, 2).reshape(blocks.shape)


def _unslot_rows(blocks):
    n = blocks.shape[-1]
    return blocks.reshape(KV_PAIRS, Q_PER_KV, 2, HEAD_DIM, n).swapaxes(1, 2).reshape(blocks.shape)


def _slot_bias(bias):
    qi = np.arange(BLOCK)[:, None]
    kj = np.arange(2 * BLOCK)[None, :]
    dist = qi + BLOCK - kj
    window = (dist >= 0) & (dist < BLOCK)
    b = bias.reshape(KV_PAIRS, 2, Q_PER_KV, BLOCK, 2 * BLOCK).swapaxes(1, 2).reshape(KV_PAIRS, PAIR_ROWS, 2 * BLOCK)
    tile = lambda mk: jnp.asarray(np.tile(mk, (2 * Q_PER_KV, 1)))[None]
    return jnp.stack([jnp.where(tile(window & (kj >= BLOCK)), b, MASKED), jnp.where(tile(window), b, MASKED)])


def _unslot_bias(db):
    return db.reshape(KV_PAIRS, Q_PER_KV, 2, BLOCK, 2 * BLOCK).swapaxes(1, 2).reshape(N_HEADS, -1)


def _pair_kv(kvc_ref, kvp_ref, kvp, dkv):
    lanes = slice(kvp * LANES, (kvp + 1) * LANES)
    vlanes = slice(dkv + kvp * LANES, dkv + (kvp + 1) * LANES)
    k2 = jnp.concatenate([kvp_ref[:, lanes], kvc_ref[:, lanes]], axis=0)
    v2 = jnp.concatenate([kvp_ref[:, vlanes], kvc_ref[:, vlanes]], axis=0)
    return k2, v2


def _head_operand(ref, grp, par, low, scale=None):
    xg = ref[:, grp * LANES:(grp + 1) * LANES]
    if scale is not None:
        xg = xg * scale
    zero = jnp.zeros_like(xg)
    return jnp.where(low, xg, zero) if par == 0 else jnp.where(low, zero, xg)


def _head_probs(qh, k2, bias_rows, sink):
    s = _dot(qh, k2, NT) + bias_rows
    m = jnp.maximum(jnp.max(s, axis=-1, keepdims=True), sink)
    e = jnp.exp(s - m)
    es = jnp.exp(sink - m)
    inv = 1.0 / (jnp.sum(e, axis=-1, keepdims=True) + es)
    return e * inv, es * inv


def _attn_specs(c):
    dq, dkv2 = c.DQ, 2 * c.DKV
    return [pl.BlockSpec((BLOCK, dq), lambda n: (n, 0)),
            pl.BlockSpec((BLOCK, dkv2), lambda n: (n, 0)),
            pl.BlockSpec((BLOCK, dkv2), lambda n: (jnp.maximum(n - 1, 0), 0))]


def _bias_spec():
    return pl.BlockSpec((None, KV_PAIRS, PAIR_ROWS, 2 * BLOCK), lambda n: (jnp.minimum(n, 1), 0, 0, 0))


def _low_lanes():
    return lax.broadcasted_iota(jnp.int32, (BLOCK, LANES), 1) < HEAD_DIM


def _attn_fwd(c, name, q, kv, bias, sinks, dep=None):
    S, dq = c.S, c.DQ

    def body(q_ref, kvc_ref, kvp_ref, bias_ref, sink_ref, o_ref):
        low = _low_lanes()
        for kvp in range(KV_PAIRS):
            k2, v2 = _pair_kv(kvc_ref, kvp_ref, kvp, c.DKV)
            for g in range(Q_PER_KV):
                grp = kvp * Q_PER_KV + g
                halves = []
                for par in range(2):
                    r = 2 * g + par
                    qh = _head_operand(q_ref, grp, par, low, scale=HEAD_DIM ** -0.5)
                    p, _ = _head_probs(qh, k2, bias_ref[kvp, r * BLOCK:(r + 1) * BLOCK, :],
                                       sink_ref[(2 * kvp + par) * Q_PER_KV + g])
                    halves.append(_dot(p.astype(BF16), v2, NN))
                o_ref[:, grp * LANES:(grp + 1) * LANES] = jnp.where(low, halves[0], halves[1]).astype(BF16)

    return _pcall(
        body, name=name, ins=[q, kv, kv, bias, sinks], dep=dep, grid=(c.nb,),
        in_specs=_attn_specs(c) + [_bias_spec(), SMEM],
        out_specs=pl.BlockSpec((BLOCK, dq), lambda n: (n, 0)),
        out_shape=jax.ShapeDtypeStruct((S, dq), BF16), compiler_params=_params(1))


def _final_loss(c, h, g, target, dep=None):
    S, D, tm = c.S, c.D, c.tm

    def body(h_ref, g_ref, t_ref, dh_ref, dg_ref, loss_ref):
        i = pl.program_id(0)
        h = h_ref[...]
        gg = g_ref[...]
        r = _rstd(h)
        xh = h * r
        err = xh * gg - t_ref[...]
        lp = jnp.sum(jnp.sum(err * err, axis=1, keepdims=True), axis=0, keepdims=True) * (0.5 / D)
        dx, dg = _rms_bwd(err * (1.0 / D), h, gg, 0.0)
        dh_ref[...] = dx

        @pl.when(i == 0)
        def _():
            dg_ref[...] = dg
            loss_ref[...] = jnp.broadcast_to(lp, loss_ref.shape)

        @pl.when(i > 0)
        def _():
            dg_ref[...] += dg
            loss_ref[...] += jnp.broadcast_to(lp, loss_ref.shape)

    row = pl.BlockSpec((tm, D), lambda i: (i, 0))
    return _pcall(
        body, name="final_loss", ins=[h, g, target], dep=dep, grid=(S // tm,),
        in_specs=[row, pl.BlockSpec((1, D), lambda i: (0, 0)), row],
        out_specs=[row, pl.BlockSpec((1, D), lambda i: (0, 0)), pl.BlockSpec((1, LANES), lambda i: (0, 0))],
        out_shape=[jax.ShapeDtypeStruct((S, D), F32), jax.ShapeDtypeStruct((1, D), F32),
                   jax.ShapeDtypeStruct((1, LANES), F32)],
        compiler_params=_params(1))


def _rms_bwd_epilogue(h_idx, g_idx, res_idx):
    def epilogue(dhn, in_refs, out_refs):
        dh, dg = _rms_bwd(dhn, in_refs[h_idx][...], in_refs[g_idx][...], in_refs[res_idx][...])
        out_refs[0][...] = dh
        i = pl.program_id(0)

        @pl.when(i == 0)
        def _():
            out_refs[1][...] = dg

        @pl.when(i > 0)
        def _():
            out_refs[1][...] += dg
    return epilogue


def _stream_outs(c, tm):
    S, D = c.S, c.D
    return ([jax.ShapeDtypeStruct((S, D), F32), jax.ShapeDtypeStruct((1, D), F32)],
            [pl.BlockSpec((tm, D), lambda i, j, k: (i, 0)), pl.BlockSpec((1, D), lambda i, j, k: (0, 0))])


def _row_specs(c, tm):
    D = c.D
    return [pl.BlockSpec((tm, D), lambda i, j, k: (i, 0)), pl.BlockSpec((1, D), lambda i, j, k: (0, 0)),
            pl.BlockSpec((tm, D), lambda i, j, k: (i, 0))]


def _bwd_rows_to_stream(c, name, dy_list, slab, blk_rows, blk_idx, n_in_cols, h, g, dres, dep=None):
    S, D, tm = c.S, c.D, c.tm
    nd = len(dy_list)

    def load_a(in_refs, out_refs):
        a = in_refs[0][...]
        for r in in_refs[1:nd]:
            a = a + r[...]
        return a.astype(BF16)

    shapes, specs = _stream_outs(c, tm)
    return _mm(
        name, [*dy_list, slab, h, g, dres],
        [pl.BlockSpec((tm, n_in_cols), lambda i, j, k: (i, 0))] * nd
        + [pl.BlockSpec((NDEV, blk_rows, n_in_cols), lambda i, j, k: (0, blk_idx, 0))] + _row_specs(c, tm),
        shapes, specs, grid=(S // tm, 1, 1), dims=NT, nk=1, acc_shape=None,
        load_a=load_a, load_b=_b_view(nd, NDEV * blk_rows), epilogue=_rms_bwd_epilogue(nd + 1, nd + 2, nd + 3),
        dep=dep)


def _bwd_cols_to_stream(c, name, dy, col, ci, h, g, dres, dep=None):
    S, D, cw, tm = c.S, c.D, c.cw, c.tmw
    K = NDEV * cw
    shapes, specs = _stream_outs(c, tm)
    return _mm(
        name, [dy, col, h, g, dres],
        [pl.BlockSpec((tm, K), lambda i, j, k: (i, 0)),
         _const((NDEV, None, cw, D), lambda i, j, k: (0, ci, 0, 0))] + _row_specs(c, tm),
        shapes, specs, grid=(S // tm, 1, 1), dims=NN, nk=1, acc_shape=None,
        load_a=_bf(0), load_b=_b_view(1, K), epilogue=_rms_bwd_epilogue(2, 3, 4), dep=dep)


def _bwd_rows_data(c, name, dy, slab, blk_rows, blk_idx, dep=None):
    S, D, tm = c.S, c.D, c.tm
    K = NDEV * blk_rows

    def epilogue(acc, in_refs, out_refs):
        out_refs[0][...] = acc.astype(BF16)

    return _mm(
        name, [dy, slab],
        [pl.BlockSpec((tm, D), lambda i, j, k: (i, 0)),
         pl.BlockSpec((NDEV, blk_rows, D), lambda i, j, k: (0, blk_idx, 0))],
        [jax.ShapeDtypeStruct((S, K), BF16)], [pl.BlockSpec((tm, K), lambda i, j, k: (i, 0))],
        grid=(S // tm, 1, 1), dims=NT, nk=1, acc_shape=None,
        load_a=_bf(0), load_b=_b_view(1, K), epilogue=epilogue, dep=dep)[0]


def _wgrad_rows(c, name, a, b_list, n_a, n_b, dep=None):
    S, tm = c.S, c.tm
    nb_in = len(b_list)
    blk_rows = n_a // NDEV

    def load_b(in_refs):
        b = in_refs[1][...]
        for r in in_refs[2:1 + nb_in]:
            b = b + r[...]
        return b.astype(BF16)

    def epilogue(acc, in_refs, out_refs):
        out_refs[0][...] = acc.reshape(NDEV, blk_rows, n_b).astype(BF16)

    return _mm(
        name, [a, *b_list],
        [pl.BlockSpec((tm, n_a), lambda i, j, k: (k, 0))] + [pl.BlockSpec((tm, n_b), lambda i, j, k: (k, 0))] * nb_in,
        [jax.ShapeDtypeStruct((NDEV, blk_rows, n_b), BF16)],
        [pl.BlockSpec((NDEV, blk_rows, n_b), lambda i, j, k: (0, 0, 0))],
        grid=(1, 1, S // tm), dims=TN, nk=S // tm, acc_shape=(n_a, n_b),
        load_a=_bf(0), load_b=load_b, epilogue=epilogue, dep=dep)[0]


def _wgrad_cols(c, name, a, b, dep=None):
    S, D, cw = c.S, c.D, c.cw

    def body(a_ref, b_ref, out_ref):
        out_ref[...] = _dot(a_ref[...], b_ref[...], TN).astype(BF16)

    return _pcall(
        body, name=name, ins=[a, b], dep=dep, grid=(NDEV,),
        in_specs=[_const((S, D), lambda j: (0, 0)), pl.BlockSpec((S, cw), lambda j: (0, j))],
        out_specs=pl.BlockSpec((None, D, cw), lambda j: (j, 0, 0)),
        out_shape=jax.ShapeDtypeStruct((NDEV, D, cw), BF16), compiler_params=_params(1))


def _ffn_bwd_data(c, name, dh, p, col, ci, rows, h, g, dep=None):
    S, D, fw, tm = c.S, c.D, c.fw, c.tmw
    F = NDEV * fw

    def body(dh_ref, p_ref, w1t_ref, w2_ref, h_ref, g_ref, da_ref, out_ref, dg_ref, dhb_ref):
        i = pl.program_id(0)
        dh = dh_ref[...]
        dhb = dh.astype(BF16)
        dhb_ref[...] = dhb
        for j in range(NDEV):
            cols = slice(j * fw, (j + 1) * fw)
            da_ref[:, cols] = (_dot(dhb, w2_ref[j], NT) * (2.0 * p_ref[:, cols].astype(F32))).astype(BF16)
        dx, dg = _rms_bwd(_dot(da_ref[...], w1t_ref[...].reshape(F, D), NN), h_ref[...], g_ref[...], dh)
        out_ref[...] = dx

        @pl.when(i == 0)
        def _():
            dg_ref[...] = dg

        @pl.when(i > 0)
        def _():
            dg_ref[...] += dg

    row = pl.BlockSpec((tm, D), lambda i: (i, 0))
    wide = pl.BlockSpec((tm, F), lambda i: (i, 0))
    return _pcall(
        body, name=name, ins=[dh, p, col, rows, h, g], dep=dep, grid=(S // tm,),
        in_specs=[row, wide, _const((NDEV, None, fw, D), lambda i: (0, ci, 0, 0)),
                  _const((NDEV, c.fr, D), lambda i: (0, 0, 0)),
                  row, pl.BlockSpec((1, D), lambda i: (0, 0))],
        out_specs=[wide, row, pl.BlockSpec((1, D), lambda i: (0, 0)), row],
        out_shape=[jax.ShapeDtypeStruct((S, F), BF16), jax.ShapeDtypeStruct((S, D), F32),
                   jax.ShapeDtypeStruct((1, D), F32), jax.ShapeDtypeStruct((S, D), BF16)],
        compiler_params=_params(1))


def _ffn_bwd_w(c, name, hn, da, p, dhb, dep=None):
    S, D, fw = c.S, c.D, c.fw

    def body(hn_ref, da_ref, p_ref, dhb_ref, dw1_ref, dw2_ref):
        dw1_ref[...] = _dot(hn_ref[...], da_ref[...], TN).astype(BF16)
        pf = p_ref[...].astype(F32)
        dw2_ref[...] = _dot((pf * pf).astype(BF16), dhb_ref[...], TN).astype(BF16)

    panel = pl.BlockSpec((S, fw), lambda j: (0, j))
    return _pcall(
        body, name=name, ins=[hn, da, p, dhb], dep=dep, grid=(NDEV,),
        in_specs=[_const((S, D), lambda j: (0, 0)), panel, panel, _const((S, D), lambda j: (0, 0))],
        out_specs=[pl.BlockSpec((None, D, fw), lambda j: (j, 0, 0)), pl.BlockSpec((None, c.fr, D), lambda j: (j, 0, 0))],
        out_shape=[jax.ShapeDtypeStruct((NDEV, D, fw), BF16), jax.ShapeDtypeStruct((NDEV, c.fr, D), BF16)],
        compiler_params=_params(1))


def _attn_bwd(c, name, q, kv, do, bias, sinks, dep=None):
    S, dq, dkv = c.S, c.DQ, c.DKV
    nb = c.nb
    scale = HEAD_DIM ** -0.5

    def body(q_ref, kvc_ref, kvp_ref, do_ref, bias_ref, sink_ref, dq_ref, dkv_ref, dbias_ref, dsink_ref, dsink_acc,
             ds_sc, p_sc, qm_sc, dom_sc):
        n = pl.program_id(0)

        @pl.when(n == 0)
        def _():
            dkv_ref[...] = jnp.zeros_like(dkv_ref)
            dbias_ref[...] = jnp.zeros_like(dbias_ref)
            dsink_acc[...] = jnp.zeros_like(dsink_acc)

        low = _low_lanes()
        rows_c = pl.ds(pl.multiple_of(n * BLOCK, BLOCK), BLOCK)
        rows_p = pl.ds(pl.multiple_of(jnp.maximum(n - 1, 0) * BLOCK, BLOCK), BLOCK)
        for kvp in range(KV_PAIRS):
            k2, v2 = _pair_kv(kvc_ref, kvp_ref, kvp, dkv)
            for g in range(Q_PER_KV):
                grp = kvp * Q_PER_KV + g
                halves = []
                for par in range(2):
                    rows = slice((2 * g + par) * BLOCK, (2 * g + par + 1) * BLOCK)
                    qh = _head_operand(q_ref, grp, par, low, scale=scale)
                    doh = _head_operand(do_ref, grp, par, low)
                    p, ps = _head_probs(qh, k2, bias_ref[kvp, rows, :], sink_ref[(2 * kvp + par) * Q_PER_KV + g])
                    dp = _dot(doh, v2, NT)
                    delta = jnp.sum(p * dp, axis=-1, keepdims=True)
                    ds = p * (dp - delta)
                    dbias_ref[kvp, rows, :] += ds
                    dsink_acc[rows, kvp:kvp + 1] += -(ps * delta)
                    ds16 = ds.astype(BF16)
                    halves.append(_dot(ds16, k2, NN) * scale)
                    ds_sc[rows, :] = ds16
                    p_sc[rows, :] = p.astype(BF16)
                    qm_sc[rows, :] = qh
                    dom_sc[rows, :] = doh
                dq_ref[:, grp * LANES:(grp + 1) * LANES] = jnp.where(low, halves[0], halves[1]).astype(BF16)
            dk2 = _dot(ds_sc[...], qm_sc[...], TN)
            dv2 = _dot(p_sc[...], dom_sc[...], TN)
            lanes = slice(kvp * LANES, (kvp + 1) * LANES)
            vlanes = slice(dkv + kvp * LANES, dkv + (kvp + 1) * LANES)
            dkv_ref[rows_p, lanes] += dk2[:BLOCK]
            dkv_ref[rows_c, lanes] += dk2[BLOCK:]
            dkv_ref[rows_p, vlanes] += dv2[:BLOCK]
            dkv_ref[rows_c, vlanes] += dv2[BLOCK:]

        @pl.when(n == nb - 1)
        def _():
            dsink_ref[...] = jnp.sum(dsink_acc[...].reshape(2 * Q_PER_KV, BLOCK, KV_PAIRS), axis=1)

    return _pcall(
        body, name=name, ins=[q, kv, kv, do, bias, sinks], dep=dep, grid=(nb,),
        in_specs=_attn_specs(c) + [pl.BlockSpec((BLOCK, dq), lambda n: (n, 0)), _bias_spec(), SMEM],
        out_specs=[pl.BlockSpec((BLOCK, dq), lambda n: (n, 0)), pl.BlockSpec((S, 2 * dkv), lambda n: (0, 0)),
                   pl.BlockSpec((KV_PAIRS, PAIR_ROWS, 2 * BLOCK), lambda n: (0, 0, 0)),
                   pl.BlockSpec((2 * Q_PER_KV, KV_PAIRS), lambda n: (0, 0))],
        out_shape=[jax.ShapeDtypeStruct((S, dq), BF16), jax.ShapeDtypeStruct((S, 2 * dkv), F32),
                   jax.ShapeDtypeStruct((KV_PAIRS, PAIR_ROWS, 2 * BLOCK), F32),
                   jax.ShapeDtypeStruct((2 * Q_PER_KV, KV_PAIRS), F32)],
        scratch_shapes=[pltpu.VMEM((PAIR_ROWS, KV_PAIRS), F32), pltpu.VMEM((PAIR_ROWS, 2 * BLOCK), BF16),
                        pltpu.VMEM((PAIR_ROWS, 2 * BLOCK), BF16), pltpu.VMEM((PAIR_ROWS, LANES), BF16),
                        pltpu.VMEM((PAIR_ROWS, LANES), BF16)],
        compiler_params=_params(1))


def _sgu_bwd(c, name, a, z, dgated, ln_g, wc, wc_t, b_t, dep=None):
    S, AW, gd, tm = c.S, c.AW, c.gd, c.tms

    def body(a_ref, z_ref, dg_ref, lng_ref, wc_ref, wct_ref, bt_ref, dz_ref, dws_ref, dbt_ref, dlng_ref, dvn_ref):
        i = pl.program_id(0)

        @pl.when(i == 0)
        def _():
            dws_ref[...] = jnp.zeros_like(dws_ref)
            dbt_ref[...] = jnp.zeros_like(dbt_ref)
            dlng_ref[...] = jnp.zeros_like(dlng_ref)

        lng = lng_ref[...]
        va = a_ref[:, AW:].astype(F32)
        xc = va - jnp.mean(va, axis=-1, keepdims=True)
        rstd = lax.rsqrt(jnp.mean(xc * xc, axis=-1, keepdims=True) + EPS)
        xh = xc * rstd
        vn = (xh * lng).astype(BF16)
        causal = _sgu_masks()
        for ch in range(tm // CHUNK):
            rows = slice(ch * CHUNK, (ch + 1) * CHUNK)
            for g in range(A_GROUPS):
                cols = slice(g * gd, (g + 1) * gd)
                blk = vn[rows, cols]
                mixed = _dot(wc_ref[g], blk, NN) + bt_ref[:, g:g + 1]
                dgb = dg_ref[rows, cols].astype(F32)
                dm = dgb * a_ref[rows, cols].astype(F32)
                dbt_ref[:, g:g + 1] += jnp.sum(dm, axis=1, keepdims=True)
                dm16 = dm.astype(BF16)
                dws_ref[g] += jnp.where(causal, _dot(dm16, blk, NT), 0.0)
                dvn_ref[rows, cols] = _dot(wct_ref[g], dm16, NN)
                dz_ref[rows, cols] = (dgb * mixed * _gelu_grad(z_ref[rows, cols].astype(F32))).astype(BF16)
        dvn = dvn_ref[...]
        dlng_ref[...] += jnp.sum(dvn * xh, axis=0, keepdims=True)
        dxh = dvn * lng
        dva = rstd * (dxh - jnp.mean(dxh, axis=-1, keepdims=True) - xh * jnp.mean(dxh * xh, axis=-1, keepdims=True))
        dz_ref[:, AW:] = (dva * _gelu_grad(z_ref[:, AW:].astype(F32))).astype(BF16)

    wide = pl.BlockSpec((tm, 2 * AW), lambda i: (i, 0))
    wsp = pl.BlockSpec((A_GROUPS, CHUNK, CHUNK), lambda i: (0, 0, 0))
    btsp = pl.BlockSpec((CHUNK, A_GROUPS), lambda i: (0, 0))
    return _pcall(
        body, name=name, ins=[a, z, dgated, ln_g, wc, wc_t, b_t], dep=dep, grid=(S // tm,),
        in_specs=[wide, wide, pl.BlockSpec((tm, AW), lambda i: (i, 0)), pl.BlockSpec((1, AW), lambda i: (0, 0)),
                  wsp, wsp, btsp],
        out_specs=[wide, wsp, btsp, pl.BlockSpec((1, AW), lambda i: (0, 0))],
        out_shape=[jax.ShapeDtypeStruct((S, 2 * AW), BF16), jax.ShapeDtypeStruct((A_GROUPS, CHUNK, CHUNK), F32),
                   jax.ShapeDtypeStruct((CHUNK, A_GROUPS), F32), jax.ShapeDtypeStruct((1, AW), F32)],
        scratch_shapes=[pltpu.VMEM((tm, AW), F32)], compiler_params=_params(1))


def _adamw(name, parts, part_block, part_index, w, m, v, tr, row_off=0, n_rows=None, prev=None, dep=None):
    R, C = w.shape
    n_rows = R if n_rows is None else n_rows
    assert n_rows % tr == 0 and row_off % tr == 0
    bc1 = 1.0 - ADAM_B1 ** ADAM_STEP
    bc2 = 1.0 - ADAM_B2 ** ADAM_STEP

    def body(p_ref, w_ref, m_ref, v_ref, *rest):
        g_ref, d_ref, nm_ref, nv_ref = rest[-4:]
        g = p_ref[0].astype(F32)
        for s in range(1, part_block[0]):
            g = g + p_ref[s].astype(F32)
        nm = ADAM_B1 * m_ref[...] + (1.0 - ADAM_B1) * g
        nv = ADAM_B2 * v_ref[...] + (1.0 - ADAM_B2) * (g * g)
        g_ref[...] = g
        nm_ref[...] = nm
        nv_ref[...] = nv
        d_ref[...] = -ADAM_LR * ((nm * (1.0 / bc1)) / (jnp.sqrt(nv * (1.0 / bc2)) + ADAM_EPS) + ADAM_WD * w_ref[...])

    ob = row_off // tr
    row = pl.BlockSpec((tr, C), lambda i: (ob + i, 0))
    out = jax.ShapeDtypeStruct((R, C), F32)
    chained = prev is not None
    return _pcall(
        body, name=name, ins=[parts, w, m, v] + (list(prev) if chained else []), dep=dep, grid=(n_rows // tr,),
        in_specs=[pl.BlockSpec(part_block, part_index), row, row, row] + ([ANY] * 4 if chained else []),
        out_specs=[row, row, row, row], out_shape=[out, out, out, out],
        input_output_aliases={4 + t: t for t in range(4)} if chained else {}, compiler_params=_params(1))


def _sum_parts(name, parts, dep=None):
    def body(p_ref, out_ref):
        g = p_ref[0]
        for s in range(1, parts.shape[0]):
            g = g + p_ref[s]
        out_ref[...] = g

    return _pcall(body, name=name, ins=[parts], in_specs=[_whole(parts)], dep=dep,
                  out_shape=jax.ShapeDtypeStruct(parts.shape[1:], F32), compiler_params=_params(0))


def _place():
    return lax.axis_index("x"), lax.axis_index("y"), lax.axis_index("c")


def _slot(px, py, pc):
    return 4 * px + 2 * py + pc


def _peer(k, x, y, c):
    return x ^ ((k >> 2) & 1), y ^ ((k >> 1) & 1), c ^ (k & 1)


SEND_PEERS = {"exchange": tuple(range(1, NDEV)), "gather": (1, 2, 4, 6), "forward": (2, 4, 6),
              "broadcast": tuple(range(1, NDEV))}


def _send_copies(mode, src_refs, land_refs, send_sems, recv_sems):
    x, y, c = _place()
    me = _slot(x, y, c)
    peers = SEND_PEERS[mode]
    copies = []
    for i, k in enumerate(peers):
        peer = _peer(k, x, y, c)
        for a, land in enumerate(land_refs):
            if mode == "exchange":
                src, dst, to = src_refs[a].at[_slot(*peer)], land.at[me], peer
            elif mode in ("gather", "broadcast"):
                src, dst, to = src_refs[a], land.at[me], peer
            else:
                src = dst = land.at[_slot(*peer)]
                to = (x, y, 1 - c)
            s = a * len(peers) + i
            copies.append(pltpu.make_async_remote_copy(src_ref=src, dst_ref=dst, send_sem=send_sems.at[s],
                                                       recv_sem=recv_sems.at[s], device_id=to, device_id_type=MESH))
    return copies


def _send_start_groups(name, groups, mode, dep=None):
    sizes = [(len(s), len(l)) for s, l in groups]
    flat = [t for s, l in groups for t in (*s, *l)]
    n_in, ng = len(flat), len(groups)

    def body(*refs):
        sems, token, at = refs[n_in:n_in + 2 * ng], refs[-1], 0
        for gi, (ns, nl) in enumerate(sizes):
            for cp in _send_copies(mode, refs[at:at + ns], refs[at + ns:at + ns + nl], sems[2 * gi], sems[2 * gi + 1]):
                cp.start()
            at += ns + nl
        token[...] = jnp.zeros_like(token)

    sem_shapes = [pltpu.SemaphoreType.DMA((nl * len(SEND_PEERS[mode]),)) for _, nl in sizes for _ in range(2)]
    if any(dep is t for t in flat):
        dep = None
    out = _pcall(
        body, name=name, ins=[pltpu.with_memory_space_constraint(t, pltpu.HBM) for t in flat],
        in_specs=[HBM] * n_in, dep=dep,
        out_shape=(*sem_shapes, *[pltpu.HBM(t.shape, t.dtype) for t in flat], jax.ShapeDtypeStruct((8, LANES), F32)),
        out_specs=(*[SEM] * (2 * ng), *[HBM] * n_in, pl.BlockSpec(memory_space=pltpu.VMEM)),
        input_output_aliases={i: 2 * ng + i for i in range(n_in)},
        compiler_params=pltpu.CompilerParams(has_side_effects=pltpu.SideEffectType.DATAFLOW_SIDE_EFFECTING))
    started, at = [], 2 * ng
    for gi, (ns, nl) in enumerate(sizes):
        started.append((out[-1], out[2 * gi], out[2 * gi + 1], list(out[at:at + ns]), list(out[at + ns:at + ns + nl])))
        at += ns + nl
    return started


def _send_start(name, srcs, lands, mode, dep=None):
    return _send_start_groups(name, [(srcs, lands)], mode, dep=dep)[0]


def _send_wait(name, started, mode, dep=None):
    _, send_sems, recv_sems, srcs, lands = started
    n_src, n = len(srcs), len(lands)

    def body(*refs):
        src_refs, land_refs = refs[:n_src], refs[n_src:n_src + n]
        ssem, rsem = refs[n_src + n], refs[n_src + n + 1]
        for cp in _send_copies(mode, src_refs, land_refs, ssem, rsem):
            cp.wait_send()
            cp.wait_recv()

    thru = [pltpu.HBM(t.shape, t.dtype) for t in [*srcs, *lands]]
    out = _pcall(
        body, name=name, ins=[*srcs, *lands, send_sems, recv_sems], in_specs=[HBM] * (n_src + n) + [SEM, SEM], dep=dep,
        out_shape=tuple(thru), out_specs=tuple([HBM] * (n_src + n)),
        input_output_aliases={i: i for i in range(n_src + n)},
        compiler_params=pltpu.CompilerParams(has_side_effects=pltpu.SideEffectType.DATAFLOW_SIDE_EFFECTING))
    return list(out[n_src:])


def _landing(block, me):
    zone = lax.empty((NDEV, *block.shape), block.dtype)
    return lax.dynamic_update_slice(zone, block[None], (me,) + (0,) * block.ndim)


def _rows128(t):
    flat = t.reshape(-1)
    n = flat.shape[0]
    rows = -(-n // (8 * LANES)) * 8
    return jnp.pad(flat, (0, rows * LANES - n)).reshape(rows, LANES)


def kernel(x, mix_norm_g, ffn_norm_g, a_w_in, a_ln_g, a_w_spatial, a_b_spatial, a_w_out, kv_norm_g, w_k, w_v, b_w_q, b_sinks, b_w_o, rel_bias, ffn_w1, ffn_w2, final_norm_g, loss_target, m_mix_norm_g, m_ffn_norm_g, m_a_w_in, m_a_ln_g, m_a_w_spatial, m_a_b_spatial, m_a_w_out, m_kv_norm_g, m_w_k, m_w_v, m_b_w_q, m_b_sinks, m_b_w_o, m_rel_bias, m_ffn_w1, m_ffn_w2, m_final_norm_g, v_mix_norm_g, v_ffn_norm_g, v_a_w_in, v_a_ln_g, v_a_w_spatial, v_a_b_spatial, v_a_w_out, v_kv_norm_g, v_w_k, v_w_v, v_b_w_q, v_b_sinks, v_b_w_o, v_rel_bias, v_ffn_w1, v_ffn_w2, v_final_norm_g):
    c = _config(x, a_w_in, a_w_out, w_k, b_w_q, b_w_o, ffn_w1, ffn_w2)
    S, D, LA, LB, LF = c.S, c.D, c.LA, c.LB, c.LF
    weights = dict(mix_norm_g=mix_norm_g, ffn_norm_g=ffn_norm_g, a_w_in=a_w_in, a_ln_g=a_ln_g, a_w_spatial=a_w_spatial,
                   a_b_spatial=a_b_spatial, a_w_out=a_w_out, kv_norm_g=kv_norm_g, w_k=w_k, w_v=w_v, b_w_q=b_w_q,
                   b_sinks=b_sinks, b_w_o=b_w_o, rel_bias=rel_bias, ffn_w1=ffn_w1, ffn_w2=ffn_w2,
                   final_norm_g=final_norm_g)
    m_in = dict(mix_norm_g=m_mix_norm_g, ffn_norm_g=m_ffn_norm_g, a_w_in=m_a_w_in, a_ln_g=m_a_ln_g,
                a_w_spatial=m_a_w_spatial, a_b_spatial=m_a_b_spatial, a_w_out=m_a_w_out, kv_norm_g=m_kv_norm_g,
                w_k=m_w_k, w_v=m_w_v, b_w_q=m_b_w_q, b_sinks=m_b_sinks, b_w_o=m_b_w_o, rel_bias=m_rel_bias,
                ffn_w1=m_ffn_w1, ffn_w2=m_ffn_w2, final_norm_g=m_final_norm_g)
    v_in = dict(mix_norm_g=v_mix_norm_g, ffn_norm_g=v_ffn_norm_g, a_w_in=v_a_w_in, a_ln_g=v_a_ln_g,
                a_w_spatial=v_a_w_spatial, a_b_spatial=v_a_b_spatial, a_w_out=v_a_w_out, kv_norm_g=v_kv_norm_g,
                w_k=v_w_k, w_v=v_w_v, b_w_q=v_b_w_q, b_sinks=v_b_sinks, b_w_o=v_b_w_o, rel_bias=v_rel_bias,
                ffn_w1=v_ffn_w1, ffn_w2=v_ffn_w2, final_norm_g=v_final_norm_g)
    names = list(weights)
    seq = _Seq()
    me = _slot(*_place())
    bf = lambda t: t.astype(BF16)

    tr = lambda t: bf(jnp.swapaxes(t, -1, -2))
    groups = []
    for l in range(LA):
        groups += [[tr(a_w_in[l])[None]] + ([a_ln_g] if l == 0 else []), [bf(a_w_out[l])],
                   [tr(ffn_w1[l])[None], bf(ffn_w2[l])]]
    gb = 3 * LA
    for l in range(LB):
        extra = [bf(jnp.concatenate([w_k, w_v], axis=1))] if l == 0 else []
        groups += [extra + [bf(b_w_q[l]), bf(b_w_o[l])], [tr(ffn_w1[LA + l])[None], bf(ffn_w2[LA + l])]]
    started = seq(_send_start_groups, "weights_start", [(grp, [_landing(t, me) for t in grp]) for grp in groups],
                  "gather")
    seq.last = started[0][0]
    forwarding = {}

    def forward(i):
        lands = seq(_send_wait, f"weights_wait{i}", started[i], "gather")
        forwarding[i] = seq(_send_start, f"weights_forward{i}", [], lands, "forward")

    def arrive(i):
        if i not in forwarding:
            forward(i)
        return seq(_send_wait, f"weights_arrive{i}", forwarding[i], "forward")

    causal = jnp.tril(jnp.ones((CHUNK, CHUNK), bool))
    wsp = jnp.where(causal[None, None], a_w_spatial, 0.0)
    wsp16 = wsp.astype(BF16)
    wsp16_t = jnp.swapaxes(wsp, -1, -2).astype(BF16)
    bsp_t = jnp.swapaxes(a_b_spatial, -1, -2)
    mix_g = mix_norm_g.reshape(-1, 1, D)
    ffn_g = ffn_norm_g.reshape(-1, 1, D)
    kv_g = kv_norm_g.reshape(1, D)
    fin_g = final_norm_g.reshape(1, D)
    onehot = _bucket_onehot()
    bias = _slot_bias(seq(_band_bias, rel_bias.T, onehot).reshape(N_HEADS, BLOCK, 2 * BLOCK))

    h = x.reshape(S, D)
    sav_a, sav_b, wts_a, wts_b = [], [], [], []
    for l in range(LA):
        got = arrive(3 * l)
        w_in = got[0]
        if l == 0:
            ln_g_full = jnp.transpose(got[1], (1, 0, 2)).reshape(LA, 1, c.AW)
        z, a, hn = seq(_a_in_fwd, c, f"a_in_fwd{l}", h, mix_g[l], w_in, 0)
        forward(3 * l + 1)
        gated = seq(_sgu_fwd, c, f"sgu_fwd{l}", a, ln_g_full[l], wsp16[l], bsp_t[l])
        (wout,) = arrive(3 * l + 1)
        if l > 0:
            forward(3 * l + 2)
        h1 = seq(_mm_res, c, f"a_out_fwd{l}", gated, wout, c.ar, 0, h)
        w1, rows = arrive(3 * l + 2)
        if l == LA - 1:
            forward(gb)
        p, h2, hnf = seq(_ffn_fwd, c, f"ffn_fwd{l}", h1, ffn_g[l], w1, 0, rows)
        sav_a.append((h, z, a, hn, gated, h1, p, hnf))
        wts_a.append((w_in, 0, w1, 0, rows, wout, 0))
        h = h2
    h_kv = h
    for l in range(LB):
        got = arrive(gb + 2 * l)
        if l == 0:
            wkv, got = got[0], got[1:]
            kv, hkv = seq(_rms_mm_rows, c, "kv_fwd", h, kv_g, wkv, c.kr, 0, 2 * c.DKV)
        wq, wo = _slot_cols(got[0]), _slot_rows(got[1])
        q, hn = seq(_rms_mm_rows, c, f"q_fwd{l}", h, mix_g[LA + l], wq, c.qr, 0, c.DQ)
        forward(gb + 2 * l + 1)
        o = seq(_attn_fwd, c, f"attn_fwd{l}", q, kv, bias, b_sinks[l])
        h1 = seq(_mm_res, c, f"o_fwd{l}", o, wo, c.orr, 0, h)
        w1, rows = arrive(gb + 2 * l + 1)
        if l + 1 < LB:
            forward(gb + 2 * l + 2)
        p, h2, hnf = seq(_ffn_fwd, c, f"ffn_fwd{LA + l}", h1, ffn_g[LA + l], w1, 0, rows)
        sav_b.append((h, q, hn, o, h1, p, hnf))
        wts_b.append((wq, wo, w1, rows))
        h = h2
    dh, d_fin_g, loss_row = seq(_final_loss, c, h, fin_g, loss_target.reshape(S, D))
    loss = lax.psum(loss_row[0, 0], MESH_AXES)

    results = {}
    in_flight = []

    def update(k, parts, layer, col_blk=0):
        w = weights[k]
        rows_l, ncols = (w.shape[-2], w.shape[-1]) if w.ndim == 3 else w.shape
        flat = lambda t: t.reshape(-1, ncols)
        tr = min(256, rows_l)
        results[k] = seq(_adamw, f"adamw_{k}{layer}", parts, (NDEV, tr, ncols), lambda i: (0, i, col_blk),
                         flat(w), flat(m_in[k]), flat(v_in[k]), tr, row_off=layer * rows_l, n_rows=rows_l,
                         prev=results.get(k))

    def land(tag, entry):
        lands = seq(_send_wait, f"grads_wait_{tag}", entry[1], "exchange")
        for keys, parts in zip(entry[0], lands):
            for k, layer, col_blk in keys:
                update(k, parts, layer, col_blk)

    def send(tag, items):
        slabs = [t for _, t in items]
        own = [_landing(lax.dynamic_index_in_dim(t, me, 0, keepdims=False), me) for t in slabs]
        st = seq(_send_start, f"grads_start_{tag}", slabs, own, "exchange")
        in_flight.append((tag, ([k for k, _ in items], st)))
        while len(in_flight) > EXCHANGE_LAG:
            land(*in_flight.pop(0))

    d_mix_g, d_ffn_g = [None] * LF, [None] * LF
    dkv_list, dbias_list, dsink_list = [], [], [None] * LB

    def ffn_bwd(lf, dh, h1, p, hnf, w1, w1_i, rows):
        da, dh1, d_ffn_g[lf], dhb = seq(_ffn_bwd_data, c, f"ffn_bwd_data{lf}", dh, p, w1, w1_i, rows, h1, ffn_g[lf])
        dw1, dw2 = seq(_ffn_bwd_w, c, f"ffn_bwd_w{lf}", hnf, da, p, dhb)
        send(f"ffn{lf}", [([("ffn_w1", lf, 0)], dw1), ([("ffn_w2", lf, 0)], dw2)])
        return dh1

    for l in reversed(range(LB)):
        h0, q, hn, o, h1, p, hnf = sav_b[l]
        wq, wo, w1, rows = wts_b[l]
        dh1 = ffn_bwd(LA + l, dh, h1, p, hnf, w1, 0, rows)
        do = seq(_bwd_rows_data, c, f"o_bwd_data{l}", dh1, wo, c.orr, 0)
        dwo = _unslot_rows(seq(_wgrad_rows, c, f"o_bwd_w{l}", o, [dh1], c.DQ, D))
        dq, dkv, dbias, dsink = seq(_attn_bwd, c, f"attn_bwd{l}", q, kv, do, bias, b_sinks[l])
        dsink_list[l] = dsink.reshape(Q_PER_KV, 2, KV_PAIRS).transpose(2, 1, 0).reshape(1, N_HEADS)
        dkv_list.append(dkv)
        dbias_list.append(_unslot_bias(dbias))
        dwq = _unslot_cols(seq(_wgrad_rows, c, f"q_bwd_w{l}", hn, [dq], D, c.DQ))
        send(f"attn{l}", [([("b_w_o", l, 0)], dwo), ([("b_w_q", l, 0)], dwq)])
        dh, d_mix_g[LA + l] = seq(_bwd_rows_to_stream, c, f"q_bwd_data{l}", [dq], wq, c.qr, 0, c.DQ, h0,
                                  mix_g[LA + l], dh1)
    dwkv = seq(_wgrad_rows, c, "kv_bwd_w", hkv, dkv_list, D, 2 * c.DKV)
    send("kv", [([("w_k", 0, 0), ("w_v", 0, 1)], dwkv)])
    dh, d_kv_g = seq(_bwd_rows_to_stream, c, "kv_bwd_data", dkv_list, wkv, c.kr, 0, 2 * c.DKV, h_kv, kv_g, dh)
    d_rel_t = seq(_band_bias_grad, dbias_list, onehot)
    d_wsp, d_bsp, d_lng = [None] * LA, [None] * LA, [None] * LA
    for l in reversed(range(LA)):
        h0, z, a, hn, gated, h1, p, hnf = sav_a[l]
        w_in, in_i, w1, w1_i, rows, wout, wout_i = wts_a[l]
        dh1 = ffn_bwd(l, dh, h1, p, hnf, w1, w1_i, rows)
        dgated = seq(_bwd_rows_data, c, f"a_out_bwd_data{l}", dh1, wout, c.ar, wout_i)
        dwout = seq(_wgrad_rows, c, f"a_out_bwd_w{l}", gated, [dh1], c.AW, D)
        send(f"a_out{l}", [([("a_w_out", l, 0)], dwout)])
        dz, d_wsp[l], dbt, d_lng[l] = seq(_sgu_bwd, c, f"sgu_bwd{l}", a, z, dgated, ln_g_full[l], wsp16[l],
                                          wsp16_t[l], bsp_t[l])
        d_bsp[l] = dbt.T
        dwin = seq(_wgrad_cols, c, f"a_in_bwd_w{l}", hn, dz)
        send(f"a_in{l}", [([("a_w_in", l, 0)], dwin)])
        dh, d_mix_g[l] = seq(_bwd_cols_to_stream, c, f"a_in_bwd_data{l}", dz, w_in, in_i, h0, mix_g[l], dh1)
    grad_x = dh.reshape(1, S, D)

    small = {
        "mix_norm_g": jnp.concatenate(d_mix_g, axis=0), "ffn_norm_g": jnp.concatenate(d_ffn_g, axis=0),
        "a_w_spatial": jnp.stack(d_wsp), "a_b_spatial": jnp.stack(d_bsp), "kv_norm_g": d_kv_g,
        "b_sinks": jnp.concatenate(dsink_list, axis=0), "rel_bias": d_rel_t.T, "final_norm_g": d_fin_g,
    }
    small_names = list(small)
    packs = [_rows128(small[k]) for k in small_names] + [_rows128(jnp.concatenate(d_lng, axis=0))]
    offs = np.cumsum([0] + [p.shape[0] for p in packs])
    tail_rows = int(-offs[-1] % (8 * NDEV)) + packs[-1].shape[0]
    Rs = int(offs[-2]) + tail_rows
    packed = jnp.concatenate(packs + [jnp.zeros((Rs - int(offs[-1]), LANES), F32)], axis=0)
    slab = packed.reshape(NDEV, Rs // NDEV, LANES)
    st = seq(_send_start, "small_grads_start", [slab],
             [_landing(lax.dynamic_index_in_dim(slab, me, 0, keepdims=False), me)], "exchange")
    while in_flight:
        land(*in_flight.pop(0))
    (parts,) = seq(_send_wait, "small_grads_wait", st, "exchange")
    mine = seq(_sum_parts, "small_grads_sum", parts)
    st = seq(_send_start, "small_sums_start", [mine], [_landing(mine, me)], "broadcast")
    (sums,) = seq(_send_wait, "small_sums_wait", st, "broadcast")
    small_all = sums.reshape(1, Rs, LANES)

    grads, deltas, new_m, new_v = {}, {}, {}, {}

    def put(k, outs, shape):
        grads[k], deltas[k], new_m[k], new_v[k] = (t.reshape(shape) for t in outs)

    def pack_state(d):
        return jnp.concatenate([_rows128(d[k]) for k in small_names] + [jnp.zeros((tail_rows, LANES), F32)], axis=0)

    outs = seq(_adamw, "adamw_small", small_all, (1, Rs, LANES), lambda i: (0, 0, 0),
               pack_state(weights), pack_state(m_in), pack_state(v_in), Rs)
    for n_, k in enumerate(small_names):
        shape = weights[k].shape
        size = int(np.prod(shape))
        put(k, [t[int(offs[n_]):int(offs[n_ + 1])].reshape(-1)[:size] for t in outs], shape)
    lng_sum = outs[0][int(offs[-2]):int(offs[-1])].reshape(-1)[:LA * c.AW].reshape(LA, c.AW)
    lng_mine = lax.dynamic_slice_in_dim(lng_sum, me * c.ar, c.ar, axis=1)
    lng_parts = jnp.concatenate([lng_mine[None], jnp.zeros((NDEV - 1, LA, c.ar), F32)], axis=0)
    put("a_ln_g", seq(_adamw, "adamw_ln_g", lng_parts, (NDEV, LA, c.ar), lambda i: (0, 0, 0),
                      a_ln_g, m_in["a_ln_g"], v_in["a_ln_g"], LA), a_ln_g.shape)
    for k in ("a_w_in", "ffn_w1", "ffn_w2", "a_w_out", "b_w_o", "b_w_q", "w_k", "w_v"):
        put(k, results[k], weights[k].shape)

    return (loss, grad_x, *[grads[k] for k in names], *[deltas[k] for k in names],
            *[new_m[k] for k in names], *[new_v[k] for k in names])
```

```python
import numpy as np
import math
import jax
import jax.numpy as jnp
from jax import lax
from jax.experimental import pallas as pl
from jax.experimental.pallas import tpu as pltpu

F32 = jnp.float32
BF16 = jnp.bfloat16

NDEV = 8
EPS = 1e-6
CHUNK = 128
A_GROUPS = 8
N_HEADS = 16
N_KV_HEADS = 4
Q_PER_KV = N_HEADS // N_KV_HEADS
HEAD_DIM = 64
BLOCK = 128
N_BUCKETS = 32
MAX_DISTANCE = 128
ADAM_LR, ADAM_B1, ADAM_B2, ADAM_EPS, ADAM_WD, ADAM_STEP = 0.001, 0.9, 0.999, 1e-08, 0.01, 10
LANES = 128
VMEM_LIMIT = 56 * 1024 * 1024
INV_SQRT2 = 0.7071067811865476
INV_SQRT_2PI = 0.3989422804014327
EXCHANGE_LAG = 4

HBM = pl.BlockSpec(memory_space=pltpu.HBM)
SMEM = pl.BlockSpec(memory_space=pltpu.SMEM)
ANY = pl.BlockSpec(memory_space=pl.ANY)
SEM = pl.BlockSpec(memory_space=pltpu.SEMAPHORE)
MESH = pl.DeviceIdType.MESH


def _params(n_grid):
    return pltpu.CompilerParams(dimension_semantics=("arbitrary",) * n_grid, vmem_limit_bytes=VMEM_LIMIT)


def _const(block, index_map):
    return pl.BlockSpec(block, index_map, pipeline_mode=pl.Buffered(1))


def _pcall(body, *, ins, in_specs, dep=None, **kw):
    n_in = len(ins)
    if dep is None or any(dep is t for t in ins):
        return pl.pallas_call(body, in_specs=list(in_specs), **kw)(*ins)

    def with_dep(*refs):
        body(*refs[:n_in], *refs[n_in + 1:])

    return pl.pallas_call(with_dep, in_specs=[*in_specs, ANY], **kw)(*ins, dep)


class _Seq:
    def __init__(self):
        self.last = None

    def __call__(self, fn, *args, **kw):
        out = fn(*args, dep=self.last, **kw)
        self.last = out[0] if isinstance(out, (tuple, list)) else out
        return out


def _rstd(h):
    return lax.rsqrt(jnp.mean(h * h, axis=-1, keepdims=True) + EPS)


def _rms_bwd(dhn, h, g, dres):
    r = _rstd(h)
    xh = h * r
    dg = jnp.sum(dhn * xh, axis=0, keepdims=True)
    dxh = dhn * g
    dx = r * (dxh - xh * jnp.mean(dxh * xh, axis=-1, keepdims=True))
    return dres + dx, dg


def _gelu(z):
    return 0.5 * z * (1.0 + lax.erf(z * INV_SQRT2))


def _gelu_grad(z):
    return 0.5 * (1.0 + lax.erf(z * INV_SQRT2)) + z * (jnp.exp(-0.5 * z * z) * INV_SQRT_2PI)


def _dot(a, b, dims):
    return lax.dot_general(a, b, (dims, ((), ())), preferred_element_type=F32)


NN = ((1,), (0,))
NT = ((1,), (1,))
TN = ((0,), (0,))


def _mm(name, ins, in_specs, out_shapes, out_specs, *, grid, dims, nk, acc_shape, load_a, load_b, epilogue,
        dep=None):
    n_in, n_out = len(ins), len(out_shapes)
    kax = len(grid) - 1

    def body(*refs):
        in_refs = refs[:n_in]
        out_refs = refs[n_in:n_in + n_out]
        a = load_a(in_refs, out_refs)
        b = load_b(in_refs)
        prod = _dot(a, b, dims)
        if nk == 1:
            epilogue(prod, in_refs, out_refs)
        else:
            acc = refs[n_in + n_out]
            k = pl.program_id(kax)

            @pl.when(k == 0)
            def _():
                acc[...] = prod

            @pl.when(k > 0)
            def _():
                acc[...] += prod

            @pl.when(k == nk - 1)
            def _():
                epilogue(acc[...], in_refs, out_refs)

    return _pcall(
        body, name=name, ins=ins, in_specs=in_specs, dep=dep, grid=grid, out_specs=out_specs, out_shape=out_shapes,
        scratch_shapes=[pltpu.VMEM(acc_shape, F32)] if nk > 1 else [], compiler_params=_params(len(grid)))


def _bf(ref_idx):
    return lambda in_refs, *_: in_refs[ref_idx][...].astype(BF16)


def _b_view(ref_idx, rows):
    def load(in_refs):
        b = in_refs[ref_idx][...]
        return b.reshape(rows, b.shape[-1])
    return load


class Cfg:
    pass


def _config(x, a_w_in, a_w_out, w_k, b_w_q, b_w_o, ffn_w1, ffn_w2):
    c = Cfg()
    c.S, c.D = x.shape[1], x.shape[2]
    c.LA, _, c.cw = a_w_in.shape
    c.AW2 = NDEV * c.cw
    c.AW = c.AW2 // 2
    c.gd = c.AW // A_GROUPS
    c.ar = a_w_out.shape[1]
    c.LF, _, c.fw = ffn_w1.shape
    c.fr = ffn_w2.shape[1]
    c.LB, c.qr, c.DQ = b_w_q.shape
    c.orr = b_w_o.shape[1]
    c.kr, c.DKV = w_k.shape
    c.tm = min(1024, c.S)
    c.tmw = min(512, c.S)
    c.tms = min(256, c.S)
    c.nb = c.S // BLOCK
    assert c.cw == c.fw == c.fr and c.AW == NDEV * c.ar and c.D == NDEV * c.qr == NDEV * c.kr
    assert c.DQ == NDEV * c.orr == N_HEADS * HEAD_DIM and c.DKV == N_KV_HEADS * HEAD_DIM
    assert c.S % c.tm == 0 and c.S % c.tmw == 0 and c.S % c.tms == 0 and c.tms % CHUNK == 0 and c.gd % LANES == 0
    assert c.LA >= 1 and c.LB >= 1 and c.LF == c.LA + c.LB
    return c


def _cached_rms(h_idx, g_idx, hn_out_idx, jax_axis=1):
    def load(in_refs, out_refs):
        hn_ref = out_refs[hn_out_idx]

        @pl.when(pl.program_id(jax_axis) == 0)
        def _():
            h = in_refs[h_idx][...]
            hn_ref[...] = (h * _rstd(h) * in_refs[g_idx][...]).astype(BF16)

        return hn_ref[...]
    return load


def _a_in_fwd(c, name, h, g, col, ci, dep=None):
    S, D, cw, tm = c.S, c.D, c.cw, c.tmw

    def body(h_ref, g_ref, w_ref, z_ref, a_ref, hn_ref):
        h = h_ref[...]
        hn = (h * _rstd(h) * g_ref[...]).astype(BF16)
        hn_ref[...] = hn
        for j in range(NDEV):
            cols = slice(j * cw, (j + 1) * cw)
            z = _dot(hn, w_ref[j], NT)
            z_ref[:, cols] = z.astype(BF16)
            a_ref[:, cols] = _gelu(z).astype(BF16)

    row = pl.BlockSpec((tm, D), lambda i: (i, 0))
    wide = pl.BlockSpec((tm, c.AW2), lambda i: (i, 0))
    return _pcall(
        body, name=name, ins=[h, g, col], dep=dep, grid=(S // tm,),
        in_specs=[row, pl.BlockSpec((1, D), lambda i: (0, 0)), _const((NDEV, None, cw, D), lambda i: (0, ci, 0, 0))],
        out_specs=[wide, wide, row],
        out_shape=[jax.ShapeDtypeStruct((S, c.AW2), BF16), jax.ShapeDtypeStruct((S, c.AW2), BF16),
                   jax.ShapeDtypeStruct((S, D), BF16)],
        compiler_params=_params(1))


def _rms_mm_rows(c, name, h, g, slab, blk_rows, blk_idx, n_out, dep=None):
    S, D, tm = c.S, c.D, c.tm

    def epilogue(acc, in_refs, out_refs):
        out_refs[0][...] = acc.astype(BF16)

    return _mm(
        name, [h, slab, g],
        [pl.BlockSpec((tm, D), lambda i, j, k: (i, 0)),
         pl.BlockSpec((NDEV, blk_rows, n_out), lambda i, j, k: (0, blk_idx, 0)),
         pl.BlockSpec((1, D), lambda i, j, k: (0, 0))],
        [jax.ShapeDtypeStruct((S, n_out), BF16), jax.ShapeDtypeStruct((S, D), BF16)],
        [pl.BlockSpec((tm, n_out), lambda i, j, k: (i, 0)), pl.BlockSpec((tm, D), lambda i, j, k: (i, 0))],
        grid=(S // tm, 1, 1), dims=NN, nk=1, acc_shape=None,
        load_a=_cached_rms(0, 2, 1), load_b=_b_view(1, NDEV * blk_rows), epilogue=epilogue, dep=dep)


def _mm_res(c, name, a, slab, blk_rows, blk_idx, res, dep=None):
    S, D, tm = c.S, c.D, c.tm
    K = NDEV * blk_rows

    def epilogue(acc, in_refs, out_refs):
        out_refs[0][...] = in_refs[2][...] + acc

    return _mm(
        name, [a, slab, res],
        [pl.BlockSpec((tm, K), lambda i, j, k: (i, 0)),
         pl.BlockSpec((NDEV, blk_rows, D), lambda i, j, k: (0, blk_idx, 0)),
         pl.BlockSpec((tm, D), lambda i, j, k: (i, 0))],
        [jax.ShapeDtypeStruct((S, D), F32)], [pl.BlockSpec((tm, D), lambda i, j, k: (i, 0))],
        grid=(S // tm, 1, 1), dims=NN, nk=1, acc_shape=None,
        load_a=_bf(0), load_b=_b_view(1, K), epilogue=epilogue, dep=dep)[0]


def _sgu_masks():
    ii = lax.broadcasted_iota(jnp.int32, (CHUNK, CHUNK), 0)
    jj = lax.broadcasted_iota(jnp.int32, (CHUNK, CHUNK), 1)
    return ii >= jj


def _sgu_fwd(c, name, a, ln_g, wc, b_t, dep=None):
    S, AW, gd, tm = c.S, c.AW, c.gd, c.tms

    def body(a_ref, lng_ref, wc_ref, bt_ref, out_ref):
        va = a_ref[:, AW:].astype(F32)
        xc = va - jnp.mean(va, axis=-1, keepdims=True)
        vn = (xc * lax.rsqrt(jnp.mean(xc * xc, axis=-1, keepdims=True) + EPS) * lng_ref[...]).astype(BF16)
        for ch in range(tm // CHUNK):
            rows = slice(ch * CHUNK, (ch + 1) * CHUNK)
            for g in range(A_GROUPS):
                cols = slice(g * gd, (g + 1) * gd)
                mixed = _dot(wc_ref[g], vn[rows, cols], NN) + bt_ref[:, g:g + 1]
                out_ref[rows, cols] = (a_ref[rows, cols].astype(F32) * mixed).astype(BF16)

    return _pcall(
        body, name=name, ins=[a, ln_g, wc, b_t], dep=dep, grid=(S // tm,),
        in_specs=[pl.BlockSpec((tm, 2 * AW), lambda i: (i, 0)), pl.BlockSpec((1, AW), lambda i: (0, 0)),
                  pl.BlockSpec((A_GROUPS, CHUNK, CHUNK), lambda i: (0, 0, 0)),
                  pl.BlockSpec((CHUNK, A_GROUPS), lambda i: (0, 0))],
        out_specs=pl.BlockSpec((tm, AW), lambda i: (i, 0)),
        out_shape=jax.ShapeDtypeStruct((S, AW), BF16), compiler_params=_params(1))


def _ffn_fwd(c, name, h, g, col, ci, rows, dep=None):
    S, D, fw, tm = c.S, c.D, c.fw, c.tmw
    F = NDEV * fw

    def body(h_ref, g_ref, w1_ref, w2_ref, p_ref, out_ref, hn_ref, r_ref):
        h = h_ref[...]
        hn = (h * _rstd(h) * g_ref[...]).astype(BF16)
        hn_ref[...] = hn
        for j in range(NDEV):
            cols = slice(j * fw, (j + 1) * fw)
            p = jnp.maximum(_dot(hn, w1_ref[j], NT), 0.0)
            p_ref[:, cols] = p.astype(BF16)
            r_ref[:, cols] = (p * p).astype(BF16)
        out_ref[...] = h + _dot(r_ref[...], w2_ref[...].reshape(F, D), NN)

    row = pl.BlockSpec((tm, D), lambda i: (i, 0))
    return _pcall(
        body, name=name, ins=[h, g, col, rows], dep=dep, grid=(S // tm,),
        in_specs=[row, pl.BlockSpec((1, D), lambda i: (0, 0)),
                  _const((NDEV, None, fw, D), lambda i: (0, ci, 0, 0)), _const((NDEV, c.fr, D), lambda i: (0, 0, 0))],
        out_specs=[pl.BlockSpec((tm, F), lambda i: (i, 0)), row, row],
        out_shape=[jax.ShapeDtypeStruct((S, F), BF16), jax.ShapeDtypeStruct((S, D), F32),
                   jax.ShapeDtypeStruct((S, D), BF16)],
        scratch_shapes=[pltpu.VMEM((tm, F), BF16)], compiler_params=_params(1))


def _bucket_table():
    qi = np.arange(BLOCK)[:, None]
    kj = np.arange(2 * BLOCK)[None, :]
    d = np.maximum(qi + BLOCK - kj, 0)
    max_exact = N_BUCKETS // 2
    ratio = np.log(np.maximum(d, 1).astype(np.float32) / np.float32(max_exact)) / np.float32(
        math.log(MAX_DISTANCE / max_exact))
    large = np.minimum(max_exact + (ratio.astype(np.float32) * np.float32(N_BUCKETS - max_exact)).astype(np.int32),
                       N_BUCKETS - 1)
    return np.where(d < max_exact, d, large).astype(np.int32)


def _bucket_onehot():
    b = jnp.asarray(_bucket_table().reshape(1, -1))
    return (b == lax.broadcasted_iota(jnp.int32, (N_BUCKETS, b.shape[1]), 0)).astype(F32)


def _whole(t):
    return pl.BlockSpec(t.shape, lambda: (0,) * t.ndim)


def _band_bias(rel_bias_t, onehot, dep=None):
    def body(r_ref, oh_ref, out_ref):
        out_ref[...] = lax.dot_general(r_ref[...], oh_ref[...], (NN, ((), ())), preferred_element_type=F32,
                                       precision=lax.Precision.HIGHEST)

    n = onehot.shape[1]
    return _pcall(body, name="band_bias", ins=[rel_bias_t, onehot], in_specs=[_whole(rel_bias_t), _whole(onehot)],
                  dep=dep, out_shape=jax.ShapeDtypeStruct((N_HEADS, n), F32), compiler_params=_params(0))


def _band_bias_grad(dbias_list, onehot, dep=None):
    n_in = len(dbias_list)

    def body(*refs):
        oh_ref, out_ref = refs[n_in], refs[n_in + 1]
        d = refs[0][...]
        for r in refs[1:n_in]:
            d = d + r[...]
        out_ref[...] = lax.dot_general(d, oh_ref[...], (NT, ((), ())), preferred_element_type=F32,
                                       precision=lax.Precision.HIGHEST)

    ins = [*dbias_list, onehot]
    return _pcall(body, name="band_bias_grad", ins=ins, in_specs=[_whole(t) for t in ins], dep=dep,
                  out_shape=jax.ShapeDtypeStruct((N_HEADS, N_BUCKETS), F32), compiler_params=_params(0))


KV_PAIRS = N_KV_HEADS // 2
PAIR_ROWS = 2 * Q_PER_KV * BLOCK
MASKED = float(np.finfo(np.float32).min) / 2


def _slot_cols(w):
    lead = w.shape[:-1]
    return w.reshape(*lead, KV_PAIRS, 2, Q_PER_KV, HEAD_DIM).swapaxes(-3, -2).reshape(*lead, N_HEADS * HEAD_DIM)


def _unslot_cols(w):
    lead = w.shape[:-1]
    return w.reshape(*lead, KV_PAIRS, Q_PER_KV, 2, HEAD_DIM).swapaxes(-3, -2).reshape(*lead, N_HEADS * HEAD_DIM)


def _slot_rows(blocks):
    n = blocks.shape[-1]
    return blocks.reshape(KV_PAIRS, 2, Q_PER_KV, HEAD_DIM, n).swapaxes(1, 2).reshape(blocks.shape)


def _unslot_rows(blocks):
    n = blocks.shape[-1]
    return blocks.reshape(KV_PAIRS, Q_PER_KV, 2, HEAD_DIM, n).swapaxes(1, 2).reshape(blocks.shape)


def _slot_bias(bias):
    qi = np.arange(BLOCK)[:, None]
    kj = np.arange(2 * BLOCK)[None, :]
    dist = qi + BLOCK - kj
    window = (dist >= 0) & (dist < BLOCK)
    b = bias.reshape(KV_PAIRS, 2, Q_PER_KV, BLOCK, 2 * BLOCK).swapaxes(1, 2).reshape(KV_PAIRS, PAIR_ROWS, 2 * BLOCK)
    tile = lambda mk: jnp.asarray(np.tile(mk, (2 * Q_PER_KV, 1)))[None]
    return jnp.stack([jnp.where(tile(window & (kj >= BLOCK)), b, MASKED), jnp.where(tile(window), b, MASKED)])


def _unslot_bias(db):
    return db.reshape(KV_PAIRS, Q_PER_KV, 2, BLOCK, 2 * BLOCK).swapaxes(1, 2).reshape(N_HEADS, -1)


def _pair_kv(kvc_ref, kvp_ref, kvp, dkv):
    lanes = slice(kvp * LANES, (kvp + 1) * LANES)
    vlanes = slice(dkv + kvp * LANES, dkv + (kvp + 1) * LANES)
    k2 = jnp.concatenate([kvp_ref[:, lanes], kvc_ref[:, lanes]], axis=0)
    v2 = jnp.concatenate([kvp_ref[:, vlanes], kvc_ref[:, vlanes]], axis=0)
    return k2, v2


def _head_operand(ref, grp, par, low, scale=None):
    xg = ref[:, grp * LANES:(grp + 1) * LANES]
    if scale is not None:
        xg = xg * scale
    zero = jnp.zeros_like(xg)
    return jnp.where(low, xg, zero) if par == 0 else jnp.where(low, zero, xg)


def _head_probs(qh, k2, bias_rows, sink):
    s = _dot(qh, k2, NT) + bias_rows
    m = jnp.maximum(jnp.max(s, axis=-1, keepdims=True), sink)
    e = jnp.exp(s - m)
    es = jnp.exp(sink - m)
    inv = 1.0 / (jnp.sum(e, axis=-1, keepdims=True) + es)
    return e * inv, es * inv


def _attn_specs(c):
    dq, dkv2 = c.DQ, 2 * c.DKV
    return [pl.BlockSpec((BLOCK, dq), lambda n: (n, 0)),
            pl.BlockSpec((BLOCK, dkv2), lambda n: (n, 0)),
            pl.BlockSpec((BLOCK, dkv2), lambda n: (jnp.maximum(n - 1, 0), 0))]


def _bias_spec():
    return pl.BlockSpec((None, KV_PAIRS, PAIR_ROWS, 2 * BLOCK), lambda n: (jnp.minimum(n, 1), 0, 0, 0))


def _low_lanes():
    return lax.broadcasted_iota(jnp.int32, (BLOCK, LANES), 1) < HEAD_DIM


def _attn_fwd(c, name, q, kv, bias, sinks, dep=None):
    S, dq = c.S, c.DQ

    def body(q_ref, kvc_ref, kvp_ref, bias_ref, sink_ref, o_ref):
        low = _low_lanes()
        for kvp in range(KV_PAIRS):
            k2, v2 = _pair_kv(kvc_ref, kvp_ref, kvp, c.DKV)
            for g in range(Q_PER_KV):
                grp = kvp * Q_PER_KV + g
                halves = []
                for par in range(2):
                    r = 2 * g + par
                    qh = _head_operand(q_ref, grp, par, low, scale=HEAD_DIM ** -0.5)
                    p, _ = _head_probs(qh, k2, bias_ref[kvp, r * BLOCK:(r + 1) * BLOCK, :],
                                       sink_ref[(2 * kvp + par) * Q_PER_KV + g])
                    halves.append(_dot(p.astype(BF16), v2, NN))
                o_ref[:, grp * LANES:(grp + 1) * LANES] = jnp.where(low, halves[0], halves[1]).astype(BF16)

    return _pcall(
        body, name=name, ins=[q, kv, kv, bias, sinks], dep=dep, grid=(c.nb,),
        in_specs=_attn_specs(c) + [_bias_spec(), SMEM],
        out_specs=pl.BlockSpec((BLOCK, dq), lambda n: (n, 0)),
        out_shape=jax.ShapeDtypeStruct((S, dq), BF16), compiler_params=_params(1))


def _final_loss(c, h, g, target, dep=None):
    S, D, tm = c.S, c.D, c.tm

    def body(h_ref, g_ref, t_ref, dh_ref, dg_ref, loss_ref):
        i = pl.program_id(0)
        h = h_ref[...]
        gg = g_ref[...]
        r = _rstd(h)
        xh = h * r
        err = xh * gg - t_ref[...]
        lp = jnp.sum(jnp.sum(err * err, axis=1, keepdims=True), axis=0, keepdims=True) * (0.5 / D)
        dx, dg = _rms_bwd(err * (1.0 / D), h, gg, 0.0)
        dh_ref[...] = dx

        @pl.when(i == 0)
        def _():
            dg_ref[...] = dg
            loss_ref[...] = jnp.broadcast_to(lp, loss_ref.shape)

        @pl.when(i > 0)
        def _():
            dg_ref[...] += dg
            loss_ref[...] += jnp.broadcast_to(lp, loss_ref.shape)

    row = pl.BlockSpec((tm, D), lambda i: (i, 0))
    return _pcall(
        body, name="final_loss", ins=[h, g, target], dep=dep, grid=(S // tm,),
        in_specs=[row, pl.BlockSpec((1, D), lambda i: (0, 0)), row],
        out_specs=[row, pl.BlockSpec((1, D), lambda i: (0, 0)), pl.BlockSpec((1, LANES), lambda i: (0, 0))],
        out_shape=[jax.ShapeDtypeStruct((S, D), F32), jax.ShapeDtypeStruct((1, D), F32),
                   jax.ShapeDtypeStruct((1, LANES), F32)],
        compiler_params=_params(1))


def _rms_bwd_epilogue(h_idx, g_idx, res_idx):
    def epilogue(dhn, in_refs, out_refs):
        dh, dg = _rms_bwd(dhn, in_refs[h_idx][...], in_refs[g_idx][...], in_refs[res_idx][...])
        out_refs[0][...] = dh
        i = pl.program_id(0)

        @pl.when(i == 0)
        def _():
            out_refs[1][...] = dg

        @pl.when(i > 0)
        def _():
            out_refs[1][...] += dg
    return epilogue


def _stream_outs(c, tm):
    S, D = c.S, c.D
    return ([jax.ShapeDtypeStruct((S, D), F32), jax.ShapeDtypeStruct((1, D), F32)],
            [pl.BlockSpec((tm, D), lambda i, j, k: (i, 0)), pl.BlockSpec((1, D), lambda i, j, k: (0, 0))])


def _row_specs(c, tm):
    D = c.D
    return [pl.BlockSpec((tm, D), lambda i, j, k: (i, 0)), pl.BlockSpec((1, D), lambda i, j, k: (0, 0)),
            pl.BlockSpec((tm, D), lambda i, j, k: (i, 0))]


def _bwd_rows_to_stream(c, name, dy_list, slab, blk_rows, blk_idx, n_in_cols, h, g, dres, dep=None):
    S, D, tm = c.S, c.D, c.tm
    nd = len(dy_list)

    def load_a(in_refs, out_refs):
        a = in_refs[0][...]
        for r in in_refs[1:nd]:
            a = a + r[...]
        return a.astype(BF16)

    shapes, specs = _stream_outs(c, tm)
    return _mm(
        name, [*dy_list, slab, h, g, dres],
        [pl.BlockSpec((tm, n_in_cols), lambda i, j, k: (i, 0))] * nd
        + [pl.BlockSpec((NDEV, blk_rows, n_in_cols), lambda i, j, k: (0, blk_idx, 0))] + _row_specs(c, tm),
        shapes, specs, grid=(S // tm, 1, 1), dims=NT, nk=1, acc_shape=None,
        load_a=load_a, load_b=_b_view(nd, NDEV * blk_rows), epilogue=_rms_bwd_epilogue(nd + 1, nd + 2, nd + 3),
        dep=dep)


def _bwd_cols_to_stream(c, name, dy, col, ci, h, g, dres, dep=None):
    S, D, cw, tm = c.S, c.D, c.cw, c.tmw
    K = NDEV * cw
    shapes, specs = _stream_outs(c, tm)
    return _mm(
        name, [dy, col, h, g, dres],
        [pl.BlockSpec((tm, K), lambda i, j, k: (i, 0)),
         _const((NDEV, None, cw, D), lambda i, j, k: (0, ci, 0, 0))] + _row_specs(c, tm),
        shapes, specs, grid=(S // tm, 1, 1), dims=NN, nk=1, acc_shape=None,
        load_a=_bf(0), load_b=_b_view(1, K), epilogue=_rms_bwd_epilogue(2, 3, 4), dep=dep)


def _bwd_rows_data(c, name, dy, slab, blk_rows, blk_idx, dep=None):
    S, D, tm = c.S, c.D, c.tm
    K = NDEV * blk_rows

    def epilogue(acc, in_refs, out_refs):
        out_refs[0][...] = acc.astype(BF16)

    return _mm(
        name, [dy, slab],
        [pl.BlockSpec((tm, D), lambda i, j, k: (i, 0)),
         pl.BlockSpec((NDEV, blk_rows, D), lambda i, j, k: (0, blk_idx, 0))],
        [jax.ShapeDtypeStruct((S, K), BF16)], [pl.BlockSpec((tm, K), lambda i, j, k: (i, 0))],
        grid=(S // tm, 1, 1), dims=NT, nk=1, acc_shape=None,
        load_a=_bf(0), load_b=_b_view(1, K), epilogue=epilogue, dep=dep)[0]


def _wgrad_rows(c, name, a, b_list, n_a, n_b, dep=None):
    S, tm = c.S, c.tm
    nb_in = len(b_list)
    blk_rows = n_a // NDEV

    def load_b(in_refs):
        b = in_refs[1][...]
        for r in in_refs[2:1 + nb_in]:
            b = b + r[...]
        return b.astype(BF16)

    def epilogue(acc, in_refs, out_refs):
        out_refs[0][...] = acc.reshape(NDEV, blk_rows, n_b).astype(BF16)

    return _mm(
        name, [a, *b_list],
        [pl.BlockSpec((tm, n_a), lambda i, j, k: (k, 0))] + [pl.BlockSpec((tm, n_b), lambda i, j, k: (k, 0))] * nb_in,
        [jax.ShapeDtypeStruct((NDEV, blk_rows, n_b), BF16)],
        [pl.BlockSpec((NDEV, blk_rows, n_b), lambda i, j, k: (0, 0, 0))],
        grid=(1, 1, S // tm), dims=TN, nk=S // tm, acc_shape=(n_a, n_b),
        load_a=_bf(0), load_b=load_b, epilogue=epilogue, dep=dep)[0]


def _wgrad_cols(c, name, a, b, dep=None):
    S, D, cw = c.S, c.D, c.cw

    def body(a_ref, b_ref, out_ref):
        out_ref[...] = _dot(a_ref[...], b_ref[...], TN).astype(BF16)

    return _pcall(
        body, name=name, ins=[a, b], dep=dep, grid=(NDEV,),
        in_specs=[_const((S, D), lambda j: (0, 0)), pl.BlockSpec((S, cw), lambda j: (0, j))],
        out_specs=pl.BlockSpec((None, D, cw), lambda j: (j, 0, 0)),
        out_shape=jax.ShapeDtypeStruct((NDEV, D, cw), BF16), compiler_params=_params(1))


def _ffn_bwd_data(c, name, dh, p, col, ci, rows, h, g, dep=None):
    S, D, fw, tm = c.S, c.D, c.fw, c.tmw
    F = NDEV * fw

    def body(dh_ref, p_ref, w1t_ref, w2_ref, h_ref, g_ref, da_ref, out_ref, dg_ref, dhb_ref):
        i = pl.program_id(0)
        dh = dh_ref[...]
        dhb = dh.astype(BF16)
        dhb_ref[...] = dhb
        for j in range(NDEV):
            cols = slice(j * fw, (j + 1) * fw)
            da_ref[:, cols] = (_dot(dhb, w2_ref[j], NT) * (2.0 * p_ref[:, cols].astype(F32))).astype(BF16)
        dx, dg = _rms_bwd(_dot(da_ref[...], w1t_ref[...].reshape(F, D), NN), h_ref[...], g_ref[...], dh)
        out_ref[...] = dx

        @pl.when(i == 0)
        def _():
            dg_ref[...] = dg

        @pl.when(i > 0)
        def _():
            dg_ref[...] += dg

    row = pl.BlockSpec((tm, D), lambda i: (i, 0))
    wide = pl.BlockSpec((tm, F), lambda i: (i, 0))
    return _pcall(
        body, name=name, ins=[dh, p, col, rows, h, g], dep=dep, grid=(S // tm,),
        in_specs=[row, wide, _const((NDEV, None, fw, D), lambda i: (0, ci, 0, 0)),
                  _const((NDEV, c.fr, D), lambda i: (0, 0, 0)),
                  row, pl.BlockSpec((1, D), lambda i: (0, 0))],
        out_specs=[wide, row, pl.BlockSpec((1, D), lambda i: (0, 0)), row],
        out_shape=[jax.ShapeDtypeStruct((S, F), BF16), jax.ShapeDtypeStruct((S, D), F32),
                   jax.ShapeDtypeStruct((1, D), F32), jax.ShapeDtypeStruct((S, D), BF16)],
        compiler_params=_params(1))


def _ffn_bwd_w(c, name, hn, da, p, dhb, dep=None):
    S, D, fw = c.S, c.D, c.fw

    def body(hn_ref, da_ref, p_ref, dhb_ref, dw1_ref, dw2_ref):
        dw1_ref[...] = _dot(hn_ref[...], da_ref[...], TN).astype(BF16)
        pf = p_ref[...].astype(F32)
        dw2_ref[...] = _dot((pf * pf).astype(BF16), dhb_ref[...], TN).astype(BF16)

    panel = pl.BlockSpec((S, fw), lambda j: (0, j))
    return _pcall(
        body, name=name, ins=[hn, da, p, dhb], dep=dep, grid=(NDEV,),
        in_specs=[_const((S, D), lambda j: (0, 0)), panel, panel, _const((S, D), lambda j: (0, 0))],
        out_specs=[pl.BlockSpec((None, D, fw), lambda j: (j, 0, 0)), pl.BlockSpec((None, c.fr, D), lambda j: (j, 0, 0))],
        out_shape=[jax.ShapeDtypeStruct((NDEV, D, fw), BF16), jax.ShapeDtypeStruct((NDEV, c.fr, D), BF16)],
        compiler_params=_params(1))


def _attn_bwd(c, name, q, kv, do, bias, sinks, dep=None):
    S, dq, dkv = c.S, c.DQ, c.DKV
    nb = c.nb
    scale = HEAD_DIM ** -0.5

    def body(q_ref, kvc_ref, kvp_ref, do_ref, bias_ref, sink_ref, dq_ref, dkv_ref, dbias_ref, dsink_ref, dsink_acc,
             ds_sc, p_sc, qm_sc, dom_sc):
        n = pl.program_id(0)

        @pl.when(n == 0)
        def _():
            dkv_ref[...] = jnp.zeros_like(dkv_ref)
            dbias_ref[...] = jnp.zeros_like(dbias_ref)
            dsink_acc[...] = jnp.zeros_like(dsink_acc)

        low = _low_lanes()
        rows_c = pl.ds(pl.multiple_of(n * BLOCK, BLOCK), BLOCK)
        rows_p = pl.ds(pl.multiple_of(jnp.maximum(n - 1, 0) * BLOCK, BLOCK), BLOCK)
        for kvp in range(KV_PAIRS):
            k2, v2 = _pair_kv(kvc_ref, kvp_ref, kvp, dkv)
            for g in range(Q_PER_KV):
                grp = kvp * Q_PER_KV + g
                halves = []
                for par in range(2):
                    rows = slice((2 * g + par) * BLOCK, (2 * g + par + 1) * BLOCK)
                    qh = _head_operand(q_ref, grp, par, low, scale=scale)
                    doh = _head_operand(do_ref, grp, par, low)
                    p, ps = _head_probs(qh, k2, bias_ref[kvp, rows, :], sink_ref[(2 * kvp + par) * Q_PER_KV + g])
                    dp = _dot(doh, v2, NT)
                    delta = jnp.sum(p * dp, axis=-1, keepdims=True)
                    ds = p * (dp - delta)
                    dbias_ref[kvp, rows, :] += ds
                    dsink_acc[rows, kvp:kvp + 1] += -(ps * delta)
                    ds16 = ds.astype(BF16)
                    halves.append(_dot(ds16, k2, NN) * scale)
                    ds_sc[rows, :] = ds16
                    p_sc[rows, :] = p.astype(BF16)
                    qm_sc[rows, :] = qh
                    dom_sc[rows, :] = doh
                dq_ref[:, grp * LANES:(grp + 1) * LANES] = jnp.where(low, halves[0], halves[1]).astype(BF16)
            dk2 = _dot(ds_sc[...], qm_sc[...], TN)
            dv2 = _dot(p_sc[...], dom_sc[...], TN)
            lanes = slice(kvp * LANES, (kvp + 1) * LANES)
            vlanes = slice(dkv + kvp * LANES, dkv + (kvp + 1) * LANES)
            dkv_ref[rows_p, lanes] += dk2[:BLOCK]
            dkv_ref[rows_c, lanes] += dk2[BLOCK:]
            dkv_ref[rows_p, vlanes] += dv2[:BLOCK]
            dkv_ref[rows_c, vlanes] += dv2[BLOCK:]

        @pl.when(n == nb - 1)
        def _():
            dsink_ref[...] = jnp.sum(dsink_acc[...].reshape(2 * Q_PER_KV, BLOCK, KV_PAIRS), axis=1)

    return _pcall(
        body, name=name, ins=[q, kv, kv, do, bias, sinks], dep=dep, grid=(nb,),
        in_specs=_attn_specs(c) + [pl.BlockSpec((BLOCK, dq), lambda n: (n, 0)), _bias_spec(), SMEM],
        out_specs=[pl.BlockSpec((BLOCK, dq), lambda n: (n, 0)), pl.BlockSpec((S, 2 * dkv), lambda n: (0, 0)),
                   pl.BlockSpec((KV_PAIRS, PAIR_ROWS, 2 * BLOCK), lambda n: (0, 0, 0)),
                   pl.BlockSpec((2 * Q_PER_KV, KV_PAIRS), lambda n: (0, 0))],
        out_shape=[jax.ShapeDtypeStruct((S, dq), BF16), jax.ShapeDtypeStruct((S, 2 * dkv), F32),
                   jax.ShapeDtypeStruct((KV_PAIRS, PAIR_ROWS, 2 * BLOCK), F32),
                   jax.ShapeDtypeStruct((2 * Q_PER_KV, KV_PAIRS), F32)],
        scratch_shapes=[pltpu.VMEM((PAIR_ROWS, KV_PAIRS), F32), pltpu.VMEM((PAIR_ROWS, 2 * BLOCK), BF16),
                        pltpu.VMEM((PAIR_ROWS, 2 * BLOCK), BF16), pltpu.VMEM((PAIR_ROWS, LANES), BF16),
                        pltpu.VMEM((PAIR_ROWS, LANES), BF16)],
        compiler_params=_params(1))


def _sgu_bwd(c, name, a, z, dgated, ln_g, wc, wc_t, b_t, dep=None):
    S, AW, gd, tm = c.S, c.AW, c.gd, c.tms

    def body(a_ref, z_ref, dg_ref, lng_ref, wc_ref, wct_ref, bt_ref, dz_ref, dws_ref, dbt_ref, dlng_ref, dvn_ref):
        i = pl.program_id(0)

        @pl.when(i == 0)
        def _():
            dws_ref[...] = jnp.zeros_like(dws_ref)
            dbt_ref[...] = jnp.zeros_like(dbt_ref)
            dlng_ref[...] = jnp.zeros_like(dlng_ref)

        lng = lng_ref[...]
        va = a_ref[:, AW:].astype(F32)
        xc = va - jnp.mean(va, axis=-1, keepdims=True)
        rstd = lax.rsqrt(jnp.mean(xc * xc, axis=-1, keepdims=True) + EPS)
        xh = xc * rstd
        vn = (xh * lng).astype(BF16)
        causal = _sgu_masks()
        for ch in range(tm // CHUNK):
            rows = slice(ch * CHUNK, (ch + 1) * CHUNK)
            for g in range(A_GROUPS):
                cols = slice(g * gd, (g + 1) * gd)
                blk = vn[rows, cols]
                mixed = _dot(wc_ref[g], blk, NN) + bt_ref[:, g:g + 1]
                dgb = dg_ref[rows, cols].astype(F32)
                dm = dgb * a_ref[rows, cols].astype(F32)
                dbt_ref[:, g:g + 1] += jnp.sum(dm, axis=1, keepdims=True)
                dm16 = dm.astype(BF16)
                dws_ref[g] += jnp.where(causal, _dot(dm16, blk, NT), 0.0)
                dvn_ref[rows, cols] = _dot(wct_ref[g], dm16, NN)
                dz_ref[rows, cols] = (dgb * mixed * _gelu_grad(z_ref[rows, cols].astype(F32))).astype(BF16)
        dvn = dvn_ref[...]
        dlng_ref[...] += jnp.sum(dvn * xh, axis=0, keepdims=True)
        dxh = dvn * lng
        dva = rstd * (dxh - jnp.mean(dxh, axis=-1, keepdims=True) - xh * jnp.mean(dxh * xh, axis=-1, keepdims=True))
        dz_ref[:, AW:] = (dva * _gelu_grad(z_ref[:, AW:].astype(F32))).astype(BF16)

    wide = pl.BlockSpec((tm, 2 * AW), lambda i: (i, 0))
    wsp = pl.BlockSpec((A_GROUPS, CHUNK, CHUNK), lambda i: (0, 0, 0))
    btsp = pl.BlockSpec((CHUNK, A_GROUPS), lambda i: (0, 0))
    return _pcall(
        body, name=name, ins=[a, z, dgated, ln_g, wc, wc_t, b_t], dep=dep, grid=(S // tm,),
        in_specs=[wide, wide, pl.BlockSpec((tm, AW), lambda i: (i, 0)), pl.BlockSpec((1, AW), lambda i: (0, 0)),
                  wsp, wsp, btsp],
        out_specs=[wide, wsp, btsp, pl.BlockSpec((1, AW), lambda i: (0, 0))],
        out_shape=[jax.ShapeDtypeStruct((S, 2 * AW), BF16), jax.ShapeDtypeStruct((A_GROUPS, CHUNK, CHUNK), F32),
                   jax.ShapeDtypeStruct((CHUNK, A_GROUPS), F32), jax.ShapeDtypeStruct((1, AW), F32)],
        scratch_shapes=[pltpu.VMEM((tm, AW), F32)], compiler_params=_params(1))


def _adamw(name, parts, part_block, part_index, w, m, v, tr, row_off=0, n_rows=None, prev=None, dep=None):
    R, C = w.shape
    n_rows = R if n_rows is None else n_rows
    assert n_rows % tr == 0 and row_off % tr == 0
    bc1 = 1.0 - ADAM_B1 ** ADAM_STEP
    bc2 = 1.0 - ADAM_B2 ** ADAM_STEP

    def body(p_ref, w_ref, m_ref, v_ref, *rest):
        g_ref, d_ref, nm_ref, nv_ref = rest[-4:]
        g = p_ref[0].astype(F32)
        for s in range(1, part_block[0]):
            g = g + p_ref[s].astype(F32)
        nm = ADAM_B1 * m_ref[...] + (1.0 - ADAM_B1) * g
        nv = ADAM_B2 * v_ref[...] + (1.0 - ADAM_B2) * (g * g)
        g_ref[...] = g
        nm_ref[...] = nm
        nv_ref[...] = nv
        d_ref[...] = -ADAM_LR * ((nm * (1.0 / bc1)) / (jnp.sqrt(nv * (1.0 / bc2)) + ADAM_EPS) + ADAM_WD * w_ref[...])

    ob = row_off // tr
    row = pl.BlockSpec((tr, C), lambda i: (ob + i, 0))
    out = jax.ShapeDtypeStruct((R, C), F32)
    chained = prev is not None
    return _pcall(
        body, name=name, ins=[parts, w, m, v] + (list(prev) if chained else []), dep=dep, grid=(n_rows // tr,),
        in_specs=[pl.BlockSpec(part_block, part_index), row, row, row] + ([ANY] * 4 if chained else []),
        out_specs=[row, row, row, row], out_shape=[out, out, out, out],
        input_output_aliases={4 + t: t for t in range(4)} if chained else {}, compiler_params=_params(1))


def _sum_parts(name, parts, dep=None):
    def body(p_ref, out_ref):
        g = p_ref[0]
        for s in range(1, parts.shape[0]):
            g = g + p_ref[s]
        out_ref[...] = g

    return _pcall(body, name=name, ins=[parts], in_specs=[_whole(parts)], dep=dep,
                  out_shape=jax.ShapeDtypeStruct(parts.shape[1:], F32), compiler_params=_params(0))


def _place():
    return lax.axis_index("x"), lax.axis_index("y"), lax.axis_index("c")


def _slot(px, py, pc):
    return 4 * px + 2 * py + pc


def _peer(k, x, y, c):
    return x ^ ((k >> 2) & 1), y ^ ((k >> 1) & 1), c ^ (k & 1)


SEND_PEERS = {"exchange": tuple(range(1, NDEV)), "gather": (1, 2, 4, 6), "forward": (2, 4, 6),
              "broadcast": tuple(range(1, NDEV))}


def _send_copies(mode, src_refs, land_refs, send_sems, recv_sems):
    x, y, c = _place()
    me = _slot(x, y, c)
    peers = SEND_PEERS[mode]
    copies = []
    for i, k in enumerate(peers):
        peer = _peer(k, x, y, c)
        for a, land in enumerate(land_refs):
            if mode == "exchange":
                src, dst, to = src_refs[a].at[_slot(*peer)], land.at[me], peer
            elif mode in ("gather", "broadcast"):
                src, dst, to = src_refs[a], land.at[me], peer
            else:
                src = dst = land.at[_slot(*peer)]
                to = (x, y, 1 - c)
            s = a * len(peers) + i
            copies.append(pltpu.make_async_remote_copy(src_ref=src, dst_ref=dst, send_sem=send_sems.at[s],
                                                       recv_sem=recv_sems.at[s], device_id=to, device_id_type=MESH))
    return copies


def _send_start_groups(name, groups, mode, dep=None):
    sizes = [(len(s), len(l)) for s, l in groups]
    flat = [t for s, l in groups for t in (*s, *l)]
    n_in, ng = len(flat), len(groups)

    def body(*refs):
        sems, token, at = refs[n_in:n_in + 2 * ng], refs[-1], 0
        for gi, (ns, nl) in enumerate(sizes):
            for cp in _send_copies(mode, refs[at:at + ns], refs[at + ns:at + ns + nl], sems[2 * gi], sems[2 * gi + 1]):
                cp.start()
            at += ns + nl
        token[...] = jnp.zeros_like(token)

    sem_shapes = [pltpu.SemaphoreType.DMA((nl * len(SEND_PEERS[mode]),)) for _, nl in sizes for _ in range(2)]
    if any(dep is t for t in flat):
        dep = None
    out = _pcall(
        body, name=name, ins=[pltpu.with_memory_space_constraint(t, pltpu.HBM) for t in flat],
        in_specs=[HBM] * n_in, dep=dep,
        out_shape=(*sem_shapes, *[pltpu.HBM(t.shape, t.dtype) for t in flat], jax.ShapeDtypeStruct((8, LANES), F32)),
        out_specs=(*[SEM] * (2 * ng), *[HBM] * n_in, pl.BlockSpec(memory_space=pltpu.VMEM)),
        input_output_aliases={i: 2 * ng + i for i in range(n_in)},
        compiler_params=pltpu.CompilerParams(has_side_effects=pltpu.SideEffectType.DATAFLOW_SIDE_EFFECTING))
    started, at = [], 2 * ng
    for gi, (ns, nl) in enumerate(sizes):
        started.append((out[-1], out[2 * gi], out[2 * gi + 1], list(out[at:at + ns]), list(out[at + ns:at + ns + nl])))
        at += ns + nl
    return started


def _send_start(name, srcs, lands, mode, dep=None):
    return _send_start_groups(name, [(srcs, lands)], mode, dep=dep)[0]


def _send_wait(name, started, mode, dep=None):
    _, send_sems, recv_sems, srcs, lands = started
    n_src, n = len(srcs), len(lands)

    def body(*refs):
        src_refs, land_refs = refs[:n_src], refs[n_src:n_src + n]
        ssem, rsem = refs[n_src + n], refs[n_src + n + 1]
        for cp in _send_copies(mode, src_refs, land_refs, ssem, rsem):
            cp.wait_send()
            cp.wait_recv()

    thru = [pltpu.HBM(t.shape, t.dtype) for t in [*srcs, *lands]]
    out = _pcall(
        body, name=name, ins=[*srcs, *lands, send_sems, recv_sems], in_specs=[HBM] * (n_src + n) + [SEM, SEM], dep=dep,
        out_shape=tuple(thru), out_specs=tuple([HBM] * (n_src + n)),
        input_output_aliases={i: i for i in range(n_src + n)},
        compiler_params=pltpu.CompilerParams(has_side_effects=pltpu.SideEffectType.DATAFLOW_SIDE_EFFECTING))
    return list(out[n_src:])


def _landing(block, me):
    zone = lax.empty((NDEV, *block.shape), block.dtype)
    return lax.dynamic_update_slice(zone, block[None], (me,) + (0,) * block.ndim)


def _rows128(t):
    flat = t.reshape(-1)
    n = flat.shape[0]
    rows = -(-n // (8 * LANES)) * 8
    return jnp.pad(flat, (0, rows * LANES - n)).reshape(rows, LANES)


def kernel(x, mix_norm_g, ffn_norm_g, a_w_in, a_ln_g, a_w_spatial, a_b_spatial, a_w_out, kv_norm_g, w_k, w_v, b_w_q, b_sinks, b_w_o, rel_bias, ffn_w1, ffn_w2, final_norm_g, loss_target, m_mix_norm_g, m_ffn_norm_g, m_a_w_in, m_a_ln_g, m_a_w_spatial, m_a_b_spatial, m_a_w_out, m_kv_norm_g, m_w_k, m_w_v, m_b_w_q, m_b_sinks, m_b_w_o, m_rel_bias, m_ffn_w1, m_ffn_w2, m_final_norm_g, v_mix_norm_g, v_ffn_norm_g, v_a_w_in, v_a_ln_g, v_a_w_spatial, v_a_b_spatial, v_a_w_out, v_kv_norm_g, v_w_k, v_w_v, v_b_w_q, v_b_sinks, v_b_w_o, v_rel_bias, v_ffn_w1, v_ffn_w2, v_final_norm_g):
    c = _config(x, a_w_in, a_w_out, w_k, b_w_q, b_w_o, ffn_w1, ffn_w2)
    S, D, LA, LB, LF = c.S, c.D, c.LA, c.LB, c.LF
    weights = dict(mix_norm_g=mix_norm_g, ffn_norm_g=ffn_norm_g, a_w_in=a_w_in, a_ln_g=a_ln_g, a_w_spatial=a_w_spatial,
                   a_b_spatial=a_b_spatial, a_w_out=a_w_out, kv_norm_g=kv_norm_g, w_k=w_k, w_v=w_v, b_w_q=b_w_q,
                   b_sinks=b_sinks, b_w_o=b_w_o, rel_bias=rel_bias, ffn_w1=ffn_w1, ffn_w2=ffn_w2,
                   final_norm_g=final_norm_g)
    m_in = dict(mix_norm_g=m_mix_norm_g, ffn_norm_g=m_ffn_norm_g, a_w_in=m_a_w_in, a_ln_g=m_a_ln_g,
                a_w_spatial=m_a_w_spatial, a_b_spatial=m_a_b_spatial, a_w_out=m_a_w_out, kv_norm_g=m_kv_norm_g,
                w_k=m_w_k, w_v=m_w_v, b_w_q=m_b_w_q, b_sinks=m_b_sinks, b_w_o=m_b_w_o, rel_bias=m_rel_bias,
                ffn_w1=m_ffn_w1, ffn_w2=m_ffn_w2, final_norm_g=m_final_norm_g)
    v_in = dict(mix_norm_g=v_mix_norm_g, ffn_norm_g=v_ffn_norm_g, a_w_in=v_a_w_in, a_ln_g=v_a_ln_g,
                a_w_spatial=v_a_w_spatial, a_b_spatial=v_a_b_spatial, a_w_out=v_a_w_out, kv_norm_g=v_kv_norm_g,
                w_k=v_w_k, w_v=v_w_v, b_w_q=v_b_w_q, b_sinks=v_b_sinks, b_w_o=v_b_w_o, rel_bias=v_rel_bias,
                ffn_w1=v_ffn_w1, ffn_w2=v_ffn_w2, final_norm_g=v_final_norm_g)
    names = list(weights)
    seq = _Seq()
    me = _slot(*_place())
    bf = lambda t: t.astype(BF16)

    tr = lambda t: bf(jnp.swapaxes(t, -1, -2))
    groups = []
    for l in range(LA):
        groups += [[tr(a_w_in[l])[None]] + ([a_ln_g] if l == 0 else []), [bf(a_w_out[l])],
                   [tr(ffn_w1[l])[None], bf(ffn_w2[l])]]
    gb = 3 * LA
    for l in range(LB):
        extra = [bf(jnp.concatenate([w_k, w_v], axis=1))] if l == 0 else []
        groups += [extra + [bf(b_w_q[l]), bf(b_w_o[l])], [tr(ffn_w1[LA + l])[None], bf(ffn_w2[LA + l])]]
    started = seq(_send_start_groups, "weights_start", [(grp, [_landing(t, me) for t in grp]) for grp in groups],
                  "gather")
    seq.last = started[0][0]
    forwarding = {}

    def forward(i):
        lands = seq(_send_wait, f"weights_wait{i}", started[i], "gather")
        forwarding[i] = seq(_send_start, f"weights_forward{i}", [], lands, "forward")

    def arrive(i):
        if i not in forwarding:
            forward(i)
        return seq(_send_wait, f"weights_arrive{i}", forwarding[i], "forward")

    causal = jnp.tril(jnp.ones((CHUNK, CHUNK), bool))
    wsp = jnp.where(causal[None, None], a_w_spatial, 0.0)
    wsp16 = wsp.astype(BF16)
    wsp16_t = jnp.swapaxes(wsp, -1, -2).astype(BF16)
    bsp_t = jnp.swapaxes(a_b_spatial, -1, -2)
    mix_g = mix_norm_g.reshape(-1, 1, D)
    ffn_g = ffn_norm_g.reshape(-1, 1, D)
    kv_g = kv_norm_g.reshape(1, D)
    fin_g = final_norm_g.reshape(1, D)
    onehot = _bucket_onehot()
    bias = _slot_bias(seq(_band_bias, rel_bias.T, onehot).reshape(N_HEADS, BLOCK, 2 * BLOCK))

    h = x.reshape(S, D)
    sav_a, sav_b, wts_a, wts_b = [], [], [], []
    for l in range(LA):
        got = arrive(3 * l)
        w_in = got[0]
        if l == 0:
            ln_g_full = jnp.transpose(got[1], (1, 0, 2)).reshape(LA, 1, c.AW)
        z, a, hn = seq(_a_in_fwd, c, f"a_in_fwd{l}", h, mix_g[l], w_in, 0)
        forward(3 * l + 1)
        gated = seq(_sgu_fwd, c, f"sgu_fwd{l}", a, ln_g_full[l], wsp16[l], bsp_t[l])
        (wout,) = arrive(3 * l + 1)
        if l > 0:
            forward(3 * l + 2)
        h1 = seq(_mm_res, c, f"a_out_fwd{l}", gated, wout, c.ar, 0, h)
        w1, rows = arrive(3 * l + 2)
        if l == LA - 1:
            forward(gb)
        p, h2, hnf = seq(_ffn_fwd, c, f"ffn_fwd{l}", h1, ffn_g[l], w1, 0, rows)
        sav_a.append((h, z, a, hn, gated, h1, p, hnf))
        wts_a.append((w_in, 0, w1, 0, rows, wout, 0))
        h = h2
    h_kv = h
    for l in range(LB):
        got = arrive(gb + 2 * l)
        if l == 0:
            wkv, got = got[0], got[1:]
            kv, hkv = seq(_rms_mm_rows, c, "kv_fwd", h, kv_g, wkv, c.kr, 0, 2 * c.DKV)
        wq, wo = _slot_cols(got[0]), _slot_rows(got[1])
        q, hn = seq(_rms_mm_rows, c, f"q_fwd{l}", h, mix_g[LA + l], wq, c.qr, 0, c.DQ)
        forward(gb + 2 * l + 1)
        o = seq(_attn_fwd, c, f"attn_fwd{l}", q, kv, bias, b_sinks[l])
        h1 = seq(_mm_res, c, f"o_fwd{l}", o, wo, c.orr, 0, h)
        w1, rows = arrive(gb + 2 * l + 1)
        if l + 1 < LB:
            forward(gb + 2 * l + 2)
        p, h2, hnf = seq(_ffn_fwd, c, f"ffn_fwd{LA + l}", h1, ffn_g[LA + l], w1, 0, rows)
        sav_b.append((h, q, hn, o, h1, p, hnf))
        wts_b.append((wq, wo, w1, rows))
        h = h2
    dh, d_fin_g, loss_row = seq(_final_loss, c, h, fin_g, loss_target.reshape(S, D))

    results = {}
    in_flight = []

    def update(k, parts, layer, col_blk=0):
        w = weights[k]
        rows_l, ncols = (w.shape[-2], w.shape[-1]) if w.ndim == 3 else w.shape
        flat = lambda t: t.reshape(-1, ncols)
        tr = min(256, rows_l)
        results[k] = seq(_adamw, f"adamw_{k}{layer}", parts, (NDEV, tr, ncols), lambda i: (0, i, col_blk),
                         flat(w), flat(m_in[k]), flat(v_in[k]), tr, row_off=layer * rows_l, n_rows=rows_l,
                         prev=results.get(k))

    def land(tag, entry):
        lands = seq(_send_wait, f"grads_wait_{tag}", entry[1], "exchange")
        for keys, parts in zip(entry[0], lands):
            for k, layer, col_blk in keys:
                update(k, parts, layer, col_blk)

    def send(tag, items):
        slabs = [t for _, t in items]
        own = [_landing(lax.dynamic_index_in_dim(t, me, 0, keepdims=False), me) for t in slabs]
        st = seq(_send_start, f"grads_start_{tag}", slabs, own, "exchange")
        in_flight.append((tag, ([k for k, _ in items], st)))
        while len(in_flight) > EXCHANGE_LAG:
            land(*in_flight.pop(0))

    d_mix_g, d_ffn_g = [None] * LF, [None] * LF
    dkv_list, dbias_list, dsink_list = [], [], [None] * LB

    def ffn_bwd(lf, dh, h1, p, hnf, w1, w1_i, rows):
        da, dh1, d_ffn_g[lf], dhb = seq(_ffn_bwd_data, c, f"ffn_bwd_data{lf}", dh, p, w1, w1_i, rows, h1, ffn_g[lf])
        dw1, dw2 = seq(_ffn_bwd_w, c, f"ffn_bwd_w{lf}", hnf, da, p, dhb)
        send(f"ffn{lf}", [([("ffn_w1", lf, 0)], dw1), ([("ffn_w2", lf, 0)], dw2)])
        return dh1

    for l in reversed(range(LB)):
        h0, q, hn, o, h1, p, hnf = sav_b[l]
        wq, wo, w1, rows = wts_b[l]
        dh1 = ffn_bwd(LA + l, dh, h1, p, hnf, w1, 0, rows)
        do = seq(_bwd_rows_data, c, f"o_bwd_data{l}", dh1, wo, c.orr, 0)
        dwo = _unslot_rows(seq(_wgrad_rows, c, f"o_bwd_w{l}", o, [dh1], c.DQ, D))
        dq, dkv, dbias, dsink = seq(_attn_bwd, c, f"attn_bwd{l}", q, kv, do, bias, b_sinks[l])
        dsink_list[l] = dsink.reshape(Q_PER_KV, 2, KV_PAIRS).transpose(2, 1, 0).reshape(1, N_HEADS)
        dkv_list.append(dkv)
        dbias_list.append(_unslot_bias(dbias))
        dwq = _unslot_cols(seq(_wgrad_rows, c, f"q_bwd_w{l}", hn, [dq], D, c.DQ))
        send(f"attn{l}", [([("b_w_o", l, 0)], dwo), ([("b_w_q", l, 0)], dwq)])
        dh, d_mix_g[LA + l] = seq(_bwd_rows_to_stream, c, f"q_bwd_data{l}", [dq], wq, c.qr, 0, c.DQ, h0,
                                  mix_g[LA + l], dh1)
    dwkv = seq(_wgrad_rows, c, "kv_bwd_w", hkv, dkv_list, D, 2 * c.DKV)
    send("kv", [([("w_k", 0, 0), ("w_v", 0, 1)], dwkv)])
    dh, d_kv_g = seq(_bwd_rows_to_stream, c, "kv_bwd_data", dkv_list, wkv, c.kr, 0, 2 * c.DKV, h_kv, kv_g, dh)
    d_rel_t = seq(_band_bias_grad, dbias_list, onehot)
    d_wsp, d_bsp, d_lng = [None] * LA, [None] * LA, [None] * LA
    for l in reversed(range(LA)):
        h0, z, a, hn, gated, h1, p, hnf = sav_a[l]
        w_in, in_i, w1, w1_i, rows, wout, wout_i = wts_a[l]
        dh1 = ffn_bwd(l, dh, h1, p, hnf, w1, w1_i, rows)
        dgated = seq(_bwd_rows_data, c, f"a_out_bwd_data{l}", dh1, wout, c.ar, wout_i)
        dwout = seq(_wgrad_rows, c, f"a_out_bwd_w{l}", gated, [dh1], c.AW, D)
        send(f"a_out{l}", [([("a_w_out", l, 0)], dwout)])
        dz, d_wsp[l], dbt, d_lng[l] = seq(_sgu_bwd, c, f"sgu_bwd{l}", a, z, dgated, ln_g_full[l], wsp16[l],
                                          wsp16_t[l], bsp_t[l])
        d_bsp[l] = dbt.T
        dwin = seq(_wgrad_cols, c, f"a_in_bwd_w{l}", hn, dz)
        send(f"a_in{l}", [([("a_w_in", l, 0)], dwin)])
        dh, d_mix_g[l] = seq(_bwd_cols_to_stream, c, f"a_in_bwd_data{l}", dz, w_in, in_i, h0, mix_g[l], dh1)
    grad_x = dh.reshape(1, S, D)

    small = {
        "mix_norm_g": jnp.concatenate(d_mix_g, axis=0), "ffn_norm_g": jnp.concatenate(d_ffn_g, axis=0),
        "a_w_spatial": jnp.stack(d_wsp), "a_b_spatial": jnp.stack(d_bsp), "kv_norm_g": d_kv_g,
        "b_sinks": jnp.concatenate(dsink_list, axis=0), "rel_bias": d_rel_t.T, "final_norm_g": d_fin_g,
    }
    small_names = list(small)
    packs = [_rows128(small[k]) for k in small_names] + [_rows128(jnp.concatenate(d_lng, axis=0)), _rows128(loss_row)]
    offs = [int(o) for o in np.cumsum([0] + [p.shape[0] for p in packs])]
    Rs = offs[-1] + (-offs[-1]) % (8 * NDEV)
    tail_rows = Rs - offs[len(small_names)]
    packed = jnp.concatenate(packs + [jnp.zeros((Rs - offs[-1], LANES), F32)], axis=0)
    slab = packed.reshape(NDEV, Rs // NDEV, LANES)
    st = seq(_send_start, "small_grads_start", [slab],
             [_landing(lax.dynamic_index_in_dim(slab, me, 0, keepdims=False), me)], "exchange")
    while len(in_flight) > 1:
        land(*in_flight.pop(0))
    (parts,) = seq(_send_wait, "small_grads_wait", st, "exchange")
    mine = seq(_sum_parts, "small_grads_sum", parts)
    st = seq(_send_start, "small_sums_start", [mine], [_landing(mine, me)], "broadcast")
    while in_flight:
        land(*in_flight.pop(0))
    (sums,) = seq(_send_wait, "small_sums_wait", st, "broadcast")
    small_all = sums.reshape(1, Rs, LANES)
    loss = sums.reshape(Rs, LANES)[offs[-2], 0]

    grads, deltas, new_m, new_v = {}, {}, {}, {}

    def put(k, outs, shape):
        grads[k], deltas[k], new_m[k], new_v[k] = (t.reshape(shape) for t in outs)

    def pack_state(d):
        return jnp.concatenate([_rows128(d[k]) for k in small_names] + [jnp.zeros((tail_rows, LANES), F32)], axis=0)

    outs = seq(_adamw, "adamw_small", small_all, (1, Rs, LANES), lambda i: (0, 0, 0),
               pack_state(weights), pack_state(m_in), pack_state(v_in), Rs)
    for n_, k in enumerate(small_names):
        shape = weights[k].shape
        size = int(np.prod(shape))
        put(k, [t[offs[n_]:offs[n_ + 1]].reshape(-1)[:size] for t in outs], shape)
    lng_sum = outs[0][offs[-3]:offs[-2]].reshape(-1)[:LA * c.AW].reshape(LA, c.AW)
    lng_mine = lax.dynamic_slice_in_dim(lng_sum, me * c.ar, c.ar, axis=1)
    lng_parts = jnp.concatenate([lng_mine[None], jnp.zeros((NDEV - 1, LA, c.ar), F32)], axis=0)
    put("a_ln_g", seq(_adamw, "adamw_ln_g", lng_parts, (NDEV, LA, c.ar), lambda i: (0, 0, 0),
                      a_ln_g, m_in["a_ln_g"], v_in["a_ln_g"], LA), a_ln_g.shape)
    for k in ("a_w_in", "ffn_w1", "ffn_w2", "a_w_out", "b_w_o", "b_w_q", "w_k", "w_v"):
        put(k, results[k], weights[k].shape)

    return (loss, grad_x, *[grads[k] for k in names], *[deltas[k] for k in names],
            *[new_m[k] for k in names], *[new_v[k] for k in names])
```

```python
import numpy as np
import math
import jax
import jax.numpy as jnp
from jax import lax
from jax.experimental import pallas as pl
from jax.experimental.pallas import tpu as pltpu

F32 = jnp.float32
BF16 = jnp.bfloat16

NDEV = 8
EPS = 1e-6
CHUNK = 128
A_GROUPS = 8
N_HEADS = 16
N_KV_HEADS = 4
Q_PER_KV = N_HEADS // N_KV_HEADS
HEAD_DIM = 64
BLOCK = 128
N_BUCKETS = 32
MAX_DISTANCE = 128
ADAM_LR, ADAM_B1, ADAM_B2, ADAM_EPS, ADAM_WD, ADAM_STEP = 0.001, 0.9, 0.999, 1e-08, 0.01, 10
LANES = 128
VMEM_LIMIT = 56 * 1024 * 1024
INV_SQRT2 = 0.7071067811865476
INV_SQRT_2PI = 0.3989422804014327
EXCHANGE_LAG = 4

HBM = pl.BlockSpec(memory_space=pltpu.HBM)
SMEM = pl.BlockSpec(memory_space=pltpu.SMEM)
ANY = pl.BlockSpec(memory_space=pl.ANY)
SEM = pl.BlockSpec(memory_space=pltpu.SEMAPHORE)
MESH = pl.DeviceIdType.MESH


def _params(n_grid):
    return pltpu.CompilerParams(dimension_semantics=("arbitrary",) * n_grid, vmem_limit_bytes=VMEM_LIMIT)


def _const(block, index_map):
    return pl.BlockSpec(block, index_map, pipeline_mode=pl.Buffered(1))


def _pcall(body, *, ins, in_specs, dep=None, **kw):
    n_in = len(ins)
    if dep is None or any(dep is t for t in ins):
        return pl.pallas_call(body, in_specs=list(in_specs), **kw)(*ins)

    def with_dep(*refs):
        body(*refs[:n_in], *refs[n_in + 1:])

    return pl.pallas_call(with_dep, in_specs=[*in_specs, ANY], **kw)(*ins, dep)


class _Seq:
    def __init__(self):
        self.last = None

    def __call__(self, fn, *args, **kw):
        out = fn(*args, dep=self.last, **kw)
        self.last = out[0] if isinstance(out, (tuple, list)) else out
        return out


def _rstd(h):
    return lax.rsqrt(jnp.mean(h * h, axis=-1, keepdims=True) + EPS)


def _rms_bwd(dhn, h, g, dres):
    r = _rstd(h)
    xh = h * r
    dg = jnp.sum(dhn * xh, axis=0, keepdims=True)
    dxh = dhn * g
    dx = r * (dxh - xh * jnp.mean(dxh * xh, axis=-1, keepdims=True))
    return dres + dx, dg


def _gelu(z):
    return 0.5 * z * (1.0 + lax.erf(z * INV_SQRT2))


def _gelu_grad(z):
    return 0.5 * (1.0 + lax.erf(z * INV_SQRT2)) + z * (jnp.exp(-0.5 * z * z) * INV_SQRT_2PI)


def _dot(a, b, dims):
    return lax.dot_general(a, b, (dims, ((), ())), preferred_element_type=F32)


NN = ((1,), (0,))
NT = ((1,), (1,))
TN = ((0,), (0,))


def _mm(name, ins, in_specs, out_shapes, out_specs, *, grid, dims, nk, acc_shape, load_a, load_b, epilogue,
        dep=None):
    n_in, n_out = len(ins), len(out_shapes)
    kax = len(grid) - 1

    def body(*refs):
        in_refs = refs[:n_in]
        out_refs = refs[n_in:n_in + n_out]
        a = load_a(in_refs, out_refs)
        b = load_b(in_refs)
        prod = _dot(a, b, dims)
        if nk == 1:
            epilogue(prod, in_refs, out_refs)
        else:
            acc = refs[n_in + n_out]
            k = pl.program_id(kax)

            @pl.when(k == 0)
            def _():
                acc[...] = prod

            @pl.when(k > 0)
            def _():
                acc[...] += prod

            @pl.when(k == nk - 1)
            def _():
                epilogue(acc[...], in_refs, out_refs)

    return _pcall(
        body, name=name, ins=ins, in_specs=in_specs, dep=dep, grid=grid, out_specs=out_specs, out_shape=out_shapes,
        scratch_shapes=[pltpu.VMEM(acc_shape, F32)] if nk > 1 else [], compiler_params=_params(len(grid)))


def _bf(ref_idx):
    return lambda in_refs, *_: in_refs[ref_idx][...].astype(BF16)


def _b_view(ref_idx, rows):
    def load(in_refs):
        b = in_refs[ref_idx][...]
        return b.reshape(rows, b.shape[-1])
    return load


class Cfg:
    pass


def _config(x, a_w_in, a_w_out, w_k, b_w_q, b_w_o, ffn_w1, ffn_w2):
    c = Cfg()
    c.S, c.D = x.shape[1], x.shape[2]
    c.LA, _, c.cw = a_w_in.shape
    c.AW2 = NDEV * c.cw
    c.AW = c.AW2 // 2
    c.gd = c.AW // A_GROUPS
    c.ar = a_w_out.shape[1]
    c.LF, _, c.fw = ffn_w1.shape
    c.fr = ffn_w2.shape[1]
    c.LB, c.qr, c.DQ = b_w_q.shape
    c.orr = b_w_o.shape[1]
    c.kr, c.DKV = w_k.shape
    c.tm = min(1024, c.S)
    c.tmw = min(512, c.S)
    c.tms = min(256, c.S)
    c.nb = c.S // BLOCK
    assert c.cw == c.fw == c.fr and c.AW == NDEV * c.ar and c.D == NDEV * c.qr == NDEV * c.kr
    assert c.DQ == NDEV * c.orr == N_HEADS * HEAD_DIM and c.DKV == N_KV_HEADS * HEAD_DIM
    assert c.S % c.tm == 0 and c.S % c.tmw == 0 and c.S % c.tms == 0 and c.tms % CHUNK == 0 and c.gd % LANES == 0
    assert c.LA >= 1 and c.LB >= 1 and c.LF == c.LA + c.LB
    return c


def _cached_rms(h_idx, g_idx, hn_out_idx, jax_axis=1):
    def load(in_refs, out_refs):
        hn_ref = out_refs[hn_out_idx]

        @pl.when(pl.program_id(jax_axis) == 0)
        def _():
            h = in_refs[h_idx][...]
            hn_ref[...] = (h * _rstd(h) * in_refs[g_idx][...]).astype(BF16)

        return hn_ref[...]
    return load


def _a_in_fwd(c, name, h, g, col, ci, dep=None):
    S, D, cw, tm = c.S, c.D, c.cw, c.tmw

    def body(h_ref, g_ref, w_ref, z_ref, a_ref, hn_ref):
        h = h_ref[...]
        hn = (h * _rstd(h) * g_ref[...]).astype(BF16)
        hn_ref[...] = hn
        for j in range(NDEV):
            cols = slice(j * cw, (j + 1) * cw)
            z = _dot(hn, w_ref[j], NT)
            z_ref[:, cols] = z.astype(BF16)
            a_ref[:, cols] = _gelu(z).astype(BF16)

    row = pl.BlockSpec((tm, D), lambda i: (i, 0))
    wide = pl.BlockSpec((tm, c.AW2), lambda i: (i, 0))
    return _pcall(
        body, name=name, ins=[h, g, col], dep=dep, grid=(S // tm,),
        in_specs=[row, pl.BlockSpec((1, D), lambda i: (0, 0)), _const((NDEV, None, cw, D), lambda i: (0, ci, 0, 0))],
        out_specs=[wide, wide, row],
        out_shape=[jax.ShapeDtypeStruct((S, c.AW2), BF16), jax.ShapeDtypeStruct((S, c.AW2), BF16),
                   jax.ShapeDtypeStruct((S, D), BF16)],
        compiler_params=_params(1))


def _rms_mm_rows(c, name, h, g, slab, blk_rows, blk_idx, n_out, dep=None):
    S, D, tm = c.S, c.D, c.tm

    def epilogue(acc, in_refs, out_refs):
        out_refs[0][...] = acc.astype(BF16)

    return _mm(
        name, [h, slab, g],
        [pl.BlockSpec((tm, D), lambda i, j, k: (i, 0)),
         pl.BlockSpec((NDEV, blk_rows, n_out), lambda i, j, k: (0, blk_idx, 0)),
         pl.BlockSpec((1, D), lambda i, j, k: (0, 0))],
        [jax.ShapeDtypeStruct((S, n_out), BF16), jax.ShapeDtypeStruct((S, D), BF16)],
        [pl.BlockSpec((tm, n_out), lambda i, j, k: (i, 0)), pl.BlockSpec((tm, D), lambda i, j, k: (i, 0))],
        grid=(S // tm, 1, 1), dims=NN, nk=1, acc_shape=None,
        load_a=_cached_rms(0, 2, 1), load_b=_b_view(1, NDEV * blk_rows), epilogue=epilogue, dep=dep)


def _mm_res(c, name, a, slab, blk_rows, blk_idx, res, dep=None):
    S, D, tm = c.S, c.D, c.tm
    K = NDEV * blk_rows

    def epilogue(acc, in_refs, out_refs):
        out_refs[0][...] = in_refs[2][...] + acc

    return _mm(
        name, [a, slab, res],
        [pl.BlockSpec((tm, K), lambda i, j, k: (i, 0)),
         pl.BlockSpec((NDEV, blk_rows, D), lambda i, j, k: (0, blk_idx, 0)),
         pl.BlockSpec((tm, D), lambda i, j, k: (i, 0))],
        [jax.ShapeDtypeStruct((S, D), F32)], [pl.BlockSpec((tm, D), lambda i, j, k: (i, 0))],
        grid=(S // tm, 1, 1), dims=NN, nk=1, acc_shape=None,
        load_a=_bf(0), load_b=_b_view(1, K), epilogue=epilogue, dep=dep)[0]


def _sgu_masks():
    ii = lax.broadcasted_iota(jnp.int32, (CHUNK, CHUNK), 0)
    jj = lax.broadcasted_iota(jnp.int32, (CHUNK, CHUNK), 1)
    return ii >= jj


def _sgu_fwd(c, name, a, ln_g, wc, b_t, dep=None):
    S, AW, gd, tm = c.S, c.AW, c.gd, c.tms

    def body(a_ref, lng_ref, wc_ref, bt_ref, out_ref):
        va = a_ref[:, AW:].astype(F32)
        xc = va - jnp.mean(va, axis=-1, keepdims=True)
        vn = (xc * lax.rsqrt(jnp.mean(xc * xc, axis=-1, keepdims=True) + EPS) * lng_ref[...]).astype(BF16)
        for ch in range(tm // CHUNK):
            rows = slice(ch * CHUNK, (ch + 1) * CHUNK)
            for g in range(A_GROUPS):
                cols = slice(g * gd, (g + 1) * gd)
                mixed = _dot(wc_ref[g], vn[rows, cols], NN) + bt_ref[:, g:g + 1]
                out_ref[rows, cols] = (a_ref[rows, cols].astype(F32) * mixed).astype(BF16)

    return _pcall(
        body, name=name, ins=[a, ln_g, wc, b_t], dep=dep, grid=(S // tm,),
        in_specs=[pl.BlockSpec((tm, 2 * AW), lambda i: (i, 0)), pl.BlockSpec((1, AW), lambda i: (0, 0)),
                  pl.BlockSpec((A_GROUPS, CHUNK, CHUNK), lambda i: (0, 0, 0)),
                  pl.BlockSpec((CHUNK, A_GROUPS), lambda i: (0, 0))],
        out_specs=pl.BlockSpec((tm, AW), lambda i: (i, 0)),
        out_shape=jax.ShapeDtypeStruct((S, AW), BF16), compiler_params=_params(1))


def _ffn_fwd(c, name, h, g, col, ci, rows, dep=None):
    S, D, fw, tm = c.S, c.D, c.fw, c.tmw
    F = NDEV * fw

    def body(h_ref, g_ref, w1_ref, w2_ref, p_ref, out_ref, hn_ref, r_ref):
        h = h_ref[...]
        hn = (h * _rstd(h) * g_ref[...]).astype(BF16)
        hn_ref[...] = hn
        for j in range(NDEV):
            cols = slice(j * fw, (j + 1) * fw)
            p = jnp.maximum(_dot(hn, w1_ref[j], NT), 0.0)
            p_ref[:, cols] = p.astype(BF16)
            r_ref[:, cols] = (p * p).astype(BF16)
        out_ref[...] = h + _dot(r_ref[...], w2_ref[...].reshape(F, D), NN)

    row = pl.BlockSpec((tm, D), lambda i: (i, 0))
    return _pcall(
        body, name=name, ins=[h, g, col, rows], dep=dep, grid=(S // tm,),
        in_specs=[row, pl.BlockSpec((1, D), lambda i: (0, 0)),
                  _const((NDEV, None, fw, D), lambda i: (0, ci, 0, 0)), _const((NDEV, c.fr, D), lambda i: (0, 0, 0))],
        out_specs=[pl.BlockSpec((tm, F), lambda i: (i, 0)), row, row],
        out_shape=[jax.ShapeDtypeStruct((S, F), BF16), jax.ShapeDtypeStruct((S, D), F32),
                   jax.ShapeDtypeStruct((S, D), BF16)],
        scratch_shapes=[pltpu.VMEM((tm, F), BF16)], compiler_params=_params(1))


def _bucket_table():
    qi = np.arange(BLOCK)[:, None]
    kj = np.arange(2 * BLOCK)[None, :]
    d = np.maximum(qi + BLOCK - kj, 0)
    max_exact = N_BUCKETS // 2
    ratio = np.log(np.maximum(d, 1).astype(np.float32) / np.float32(max_exact)) / np.float32(
        math.log(MAX_DISTANCE / max_exact))
    large = np.minimum(max_exact + (ratio.astype(np.float32) * np.float32(N_BUCKETS - max_exact)).astype(np.int32),
                       N_BUCKETS - 1)
    return np.where(d < max_exact, d, large).astype(np.int32)


def _bucket_onehot():
    b = jnp.asarray(_bucket_table().reshape(1, -1))
    return (b == lax.broadcasted_iota(jnp.int32, (N_BUCKETS, b.shape[1]), 0)).astype(F32)


def _whole(t):
    return pl.BlockSpec(t.shape, lambda: (0,) * t.ndim)


def _band_bias(rel_bias_t, onehot, dep=None):
    def body(r_ref, oh_ref, out_ref):
        out_ref[...] = lax.dot_general(r_ref[...], oh_ref[...], (NN, ((), ())), preferred_element_type=F32,
                                       precision=lax.Precision.HIGHEST)

    n = onehot.shape[1]
    return _pcall(body, name="band_bias", ins=[rel_bias_t, onehot], in_specs=[_whole(rel_bias_t), _whole(onehot)],
                  dep=dep, out_shape=jax.ShapeDtypeStruct((N_HEADS, n), F32), compiler_params=_params(0))


def _band_bias_grad(dbias_list, onehot, dep=None):
    n_in = len(dbias_list)

    def body(*refs):
        oh_ref, out_ref = refs[n_in], refs[n_in + 1]
        d = refs[0][...]
        for r in refs[1:n_in]:
            d = d + r[...]
        out_ref[...] = lax.dot_general(d, oh_ref[...], (NT, ((), ())), preferred_element_type=F32,
                                       precision=lax.Precision.HIGHEST)

    ins = [*dbias_list, onehot]
    return _pcall(body, name="band_bias_grad", ins=ins, in_specs=[_whole(t) for t in ins], dep=dep,
                  out_shape=jax.ShapeDtypeStruct((N_HEADS, N_BUCKETS), F32), compiler_params=_params(0))


KV_PAIRS = N_KV_HEADS // 2
PAIR_ROWS = 2 * Q_PER_KV * BLOCK
MASKED = float(np.finfo(np.float32).min) / 2


def _slot_cols(w):
    lead = w.shape[:-1]
    return w.reshape(*lead, KV_PAIRS, 2, Q_PER_KV, HEAD_DIM).swapaxes(-3, -2).reshape(*lead, N_HEADS * HEAD_DIM)


def _unslot_cols(w):
    lead = w.shape[:-1]
    return w.reshape(*lead, KV_PAIRS, Q_PER_KV, 2, HEAD_DIM).swapaxes(-3, -2).reshape(*lead, N_HEADS * HEAD_DIM)


def _slot_rows(blocks):
    n = blocks.shape[-1]
    return blocks.reshape(KV_PAIRS, 2, Q_PER_KV, HEAD_DIM, n).swapaxes(1, 2).reshape(blocks.shape)


def _unslot_rows(blocks):
    n = blocks.shape[-1]
    return blocks.reshape(KV_PAIRS, Q_PER_KV, 2, HEAD_DIM, n).swapaxes(1, 2).reshape(blocks.shape)


def _slot_bias(bias):
    qi = np.arange(BLOCK)[:, None]
    kj = np.arange(2 * BLOCK)[None, :]
    dist = qi + BLOCK - kj
    window = (dist >= 0) & (dist < BLOCK)
    b = bias.reshape(KV_PAIRS, 2, Q_PER_KV, BLOCK, 2 * BLOCK).swapaxes(1, 2).reshape(KV_PAIRS, PAIR_ROWS, 2 * BLOCK)
    tile = lambda mk: jnp.asarray(np.tile(mk, (2 * Q_PER_KV, 1)))[None]
    return jnp.stack([jnp.where(tile(window & (kj >= BLOCK)), b, MASKED), jnp.where(tile(window), b, MASKED)])


def _unslot_bias(db):
    return db.reshape(KV_PAIRS, Q_PER_KV, 2, BLOCK, 2 * BLOCK).swapaxes(1, 2).reshape(N_HEADS, -1)


def _pair_kv(kvc_ref, kvp_ref, kvp, dkv):
    lanes = slice(kvp * LANES, (kvp + 1) * LANES)
    vlanes = slice(dkv + kvp * LANES, dkv + (kvp + 1) * LANES)
    k2 = jnp.concatenate([kvp_ref[:, lanes], kvc_ref[:, lanes]], axis=0)
    v2 = jnp.concatenate([kvp_ref[:, vlanes], kvc_ref[:, vlanes]], axis=0)
    return k2, v2


def _head_operand(ref, grp, par, low, scale=None):
    xg = ref[:, grp * LANES:(grp + 1) * LANES]
    if scale is not None:
        xg = xg * scale
    zero = jnp.zeros_like(xg)
    return jnp.where(low, xg, zero) if par == 0 else jnp.where(low, zero, xg)


def _head_probs(qh, k2, bias_rows, sink):
    s = _dot(qh, k2, NT) + bias_rows
    m = jnp.maximum(jnp.max(s, axis=-1, keepdims=True), sink)
    e = jnp.exp(s - m)
    es = jnp.exp(sink - m)
    inv = 1.0 / (jnp.sum(e, axis=-1, keepdims=True) + es)
    return e * inv, es * inv


def _attn_specs(c):
    dq, dkv2 = c.DQ, 2 * c.DKV
    return [pl.BlockSpec((BLOCK, dq), lambda n: (n, 0)),
            pl.BlockSpec((BLOCK, dkv2), lambda n: (n, 0)),
            pl.BlockSpec((BLOCK, dkv2), lambda n: (jnp.maximum(n - 1, 0), 0))]


def _bias_spec():
    return pl.BlockSpec((None, KV_PAIRS, PAIR_ROWS, 2 * BLOCK), lambda n: (jnp.minimum(n, 1), 0, 0, 0))


def _low_lanes():
    return lax.broadcasted_iota(jnp.int32, (BLOCK, LANES), 1) < HEAD_DIM


def _attn_fwd(c, name, q, kv, bias, sinks, dep=None):
    S, dq = c.S, c.DQ

    def body(q_ref, kvc_ref, kvp_ref, bias_ref, sink_ref, o_ref):
        low = _low_lanes()
        for kvp in range(KV_PAIRS):
            k2, v2 = _pair_kv(kvc_ref, kvp_ref, kvp, c.DKV)
            for g in range(Q_PER_KV):
                grp = kvp * Q_PER_KV + g
                halves = []
                for par in range(2):
                    r = 2 * g + par
                    qh = _head_operand(q_ref, grp, par, low, scale=HEAD_DIM ** -0.5)
                    p, _ = _head_probs(qh, k2, bias_ref[kvp, r * BLOCK:(r + 1) * BLOCK, :],
                                       sink_ref[(2 * kvp + par) * Q_PER_KV + g])
                    halves.append(_dot(p.astype(BF16), v2, NN))
                o_ref[:, grp * LANES:(grp + 1) * LANES] = jnp.where(low, halves[0], halves[1]).astype(BF16)

    return _pcall(
        body, name=name, ins=[q, kv, kv, bias, sinks], dep=dep, grid=(c.nb,),
        in_specs=_attn_specs(c) + [_bias_spec(), SMEM],
        out_specs=pl.BlockSpec((BLOCK, dq), lambda n: (n, 0)),
        out_shape=jax.ShapeDtypeStruct((S, dq), BF16), compiler_params=_params(1))


def _final_loss(c, h, g, target, dep=None):
    S, D, tm = c.S, c.D, c.tm

    def body(h_ref, g_ref, t_ref, dh_ref, dg_ref, loss_ref):
        i = pl.program_id(0)
        h = h_ref[...]
        gg = g_ref[...]
        r = _rstd(h)
        xh = h * r
        err = xh * gg - t_ref[...]
        lp = jnp.sum(jnp.sum(err * err, axis=1, keepdims=True), axis=0, keepdims=True) * (0.5 / D)
        dx, dg = _rms_bwd(err * (1.0 / D), h, gg, 0.0)
        dh_ref[...] = dx

        @pl.when(i == 0)
        def _():
            dg_ref[...] = dg
            loss_ref[...] = jnp.broadcast_to(lp, loss_ref.shape)

        @pl.when(i > 0)
        def _():
            dg_ref[...] += dg
            loss_ref[...] += jnp.broadcast_to(lp, loss_ref.shape)

    row = pl.BlockSpec((tm, D), lambda i: (i, 0))
    return _pcall(
        body, name="final_loss", ins=[h, g, target], dep=dep, grid=(S // tm,),
        in_specs=[row, pl.BlockSpec((1, D), lambda i: (0, 0)), row],
        out_specs=[row, pl.BlockSpec((1, D), lambda i: (0, 0)), pl.BlockSpec((1, LANES), lambda i: (0, 0))],
        out_shape=[jax.ShapeDtypeStruct((S, D), F32), jax.ShapeDtypeStruct((1, D), F32),
                   jax.ShapeDtypeStruct((1, LANES), F32)],
        compiler_params=_params(1))


def _rms_bwd_epilogue(h_idx, g_idx, res_idx):
    def epilogue(dhn, in_refs, out_refs):
        dh, dg = _rms_bwd(dhn, in_refs[h_idx][...], in_refs[g_idx][...], in_refs[res_idx][...])
        out_refs[0][...] = dh
        i = pl.program_id(0)

        @pl.when(i == 0)
        def _():
            out_refs[1][...] = dg

        @pl.when(i > 0)
        def _():
            out_refs[1][...] += dg
    return epilogue


def _stream_outs(c, tm):
    S, D = c.S, c.D
    return ([jax.ShapeDtypeStruct((S, D), F32), jax.ShapeDtypeStruct((1, D), F32)],
            [pl.BlockSpec((tm, D), lambda i, j, k: (i, 0)), pl.BlockSpec((1, D), lambda i, j, k: (0, 0))])


def _row_specs(c, tm):
    D = c.D
    return [pl.BlockSpec((tm, D), lambda i, j, k: (i, 0)), pl.BlockSpec((1, D), lambda i, j, k: (0, 0)),
            pl.BlockSpec((tm, D), lambda i, j, k: (i, 0))]


def _bwd_rows_to_stream(c, name, dy_list, slab, blk_rows, blk_idx, n_in_cols, h, g, dres, dep=None):
    S, D, tm = c.S, c.D, c.tm
    nd = len(dy_list)

    def load_a(in_refs, out_refs):
        a = in_refs[0][...]
        for r in in_refs[1:nd]:
            a = a + r[...]
        return a.astype(BF16)

    shapes, specs = _stream_outs(c, tm)
    return _mm(
        name, [*dy_list, slab, h, g, dres],
        [pl.BlockSpec((tm, n_in_cols), lambda i, j, k: (i, 0))] * nd
        + [pl.BlockSpec((NDEV, blk_rows, n_in_cols), lambda i, j, k: (0, blk_idx, 0))] + _row_specs(c, tm),
        shapes, specs, grid=(S // tm, 1, 1), dims=NT, nk=1, acc_shape=None,
        load_a=load_a, load_b=_b_view(nd, NDEV * blk_rows), epilogue=_rms_bwd_epilogue(nd + 1, nd + 2, nd + 3),
        dep=dep)


def _bwd_cols_to_stream(c, name, dy, col, ci, h, g, dres, dep=None):
    S, D, cw, tm = c.S, c.D, c.cw, c.tmw
    K = NDEV * cw
    shapes, specs = _stream_outs(c, tm)
    return _mm(
        name, [dy, col, h, g, dres],
        [pl.BlockSpec((tm, K), lambda i, j, k: (i, 0)),
         _const((NDEV, None, cw, D), lambda i, j, k: (0, ci, 0, 0))] + _row_specs(c, tm),
        shapes, specs, grid=(S // tm, 1, 1), dims=NN, nk=1, acc_shape=None,
        load_a=_bf(0), load_b=_b_view(1, K), epilogue=_rms_bwd_epilogue(2, 3, 4), dep=dep)


def _bwd_rows_data(c, name, dy, slab, blk_rows, blk_idx, dep=None):
    S, D, tm = c.S, c.D, c.tm
    K = NDEV * blk_rows

    def epilogue(acc, in_refs, out_refs):
        out_refs[0][...] = acc.astype(BF16)

    return _mm(
        name, [dy, slab],
        [pl.BlockSpec((tm, D), lambda i, j, k: (i, 0)),
         pl.BlockSpec((NDEV, blk_rows, D), lambda i, j, k: (0, blk_idx, 0))],
        [jax.ShapeDtypeStruct((S, K), BF16)], [pl.BlockSpec((tm, K), lambda i, j, k: (i, 0))],
        grid=(S // tm, 1, 1), dims=NT, nk=1, acc_shape=None,
        load_a=_bf(0), load_b=_b_view(1, K), epilogue=epilogue, dep=dep)[0]


def _wgrad_rows(c, name, a, b_list, n_a, n_b, dep=None):
    S, tm = c.S, c.tm
    nb_in = len(b_list)
    blk_rows = n_a // NDEV

    def load_b(in_refs):
        b = in_refs[1][...]
        for r in in_refs[2:1 + nb_in]:
            b = b + r[...]
        return b.astype(BF16)

    def epilogue(acc, in_refs, out_refs):
        out_refs[0][...] = acc.reshape(NDEV, blk_rows, n_b).astype(BF16)

    return _mm(
        name, [a, *b_list],
        [pl.BlockSpec((tm, n_a), lambda i, j, k: (k, 0))] + [pl.BlockSpec((tm, n_b), lambda i, j, k: (k, 0))] * nb_in,
        [jax.ShapeDtypeStruct((NDEV, blk_rows, n_b), BF16)],
        [pl.BlockSpec((NDEV, blk_rows, n_b), lambda i, j, k: (0, 0, 0))],
        grid=(1, 1, S // tm), dims=TN, nk=S // tm, acc_shape=(n_a, n_b),
        load_a=_bf(0), load_b=load_b, epilogue=epilogue, dep=dep)[0]


def _wgrad_cols(c, name, a, b, dep=None):
    S, D, cw = c.S, c.D, c.cw

    def body(a_ref, b_ref, out_ref):
        out_ref[...] = _dot(a_ref[...], b_ref[...], TN).astype(BF16)

    return _pcall(
        body, name=name, ins=[a, b], dep=dep, grid=(NDEV,),
        in_specs=[_const((S, D), lambda j: (0, 0)), pl.BlockSpec((S, cw), lambda j: (0, j))],
        out_specs=pl.BlockSpec((None, D, cw), lambda j: (j, 0, 0)),
        out_shape=jax.ShapeDtypeStruct((NDEV, D, cw), BF16), compiler_params=_params(1))


def _ffn_bwd_data(c, name, dh, p, col, ci, rows, h, g, dep=None):
    S, D, fw, tm = c.S, c.D, c.fw, c.tmw
    F = NDEV * fw

    def body(dh_ref, p_ref, w1t_ref, w2_ref, h_ref, g_ref, da_ref, out_ref, dg_ref, dhb_ref):
        i = pl.program_id(0)
        dh = dh_ref[...]
        dhb = dh.astype(BF16)
        dhb_ref[...] = dhb
        for j in range(NDEV):
            cols = slice(j * fw, (j + 1) * fw)
            da_ref[:, cols] = (_dot(dhb, w2_ref[j], NT) * (2.0 * p_ref[:, cols].astype(F32))).astype(BF16)
        dx, dg = _rms_bwd(_dot(da_ref[...], w1t_ref[...].reshape(F, D), NN), h_ref[...], g_ref[...], dh)
        out_ref[...] = dx

        @pl.when(i == 0)
        def _():
            dg_ref[...] = dg

        @pl.when(i > 0)
        def _():
            dg_ref[...] += dg

    row = pl.BlockSpec((tm, D), lambda i: (i, 0))
    wide = pl.BlockSpec((tm, F), lambda i: (i, 0))
    return _pcall(
        body, name=name, ins=[dh, p, col, rows, h, g], dep=dep, grid=(S // tm,),
        in_specs=[row, wide, _const((NDEV, None, fw, D), lambda i: (0, ci, 0, 0)),
                  _const((NDEV, c.fr, D), lambda i: (0, 0, 0)),
                  row, pl.BlockSpec((1, D), lambda i: (0, 0))],
        out_specs=[wide, row, pl.BlockSpec((1, D), lambda i: (0, 0)), row],
        out_shape=[jax.ShapeDtypeStruct((S, F), BF16), jax.ShapeDtypeStruct((S, D), F32),
                   jax.ShapeDtypeStruct((1, D), F32), jax.ShapeDtypeStruct((S, D), BF16)],
        compiler_params=_params(1))


def _ffn_bwd_w(c, name, hn, da, p, dhb, dep=None):
    S, D, fw = c.S, c.D, c.fw

    def body(hn_ref, da_ref, p_ref, dhb_ref, dw1_ref, dw2_ref):
        dw1_ref[...] = _dot(hn_ref[...], da_ref[...], TN).astype(BF16)
        pf = p_ref[...].astype(F32)
        dw2_ref[...] = _dot((pf * pf).astype(BF16), dhb_ref[...], TN).astype(BF16)

    panel = pl.BlockSpec((S, fw), lambda j: (0, j))
    return _pcall(
        body, name=name, ins=[hn, da, p, dhb], dep=dep, grid=(NDEV,),
        in_specs=[_const((S, D), lambda j: (0, 0)), panel, panel, _const((S, D), lambda j: (0, 0))],
        out_specs=[pl.BlockSpec((None, D, fw), lambda j: (j, 0, 0)), pl.BlockSpec((None, c.fr, D), lambda j: (j, 0, 0))],
        out_shape=[jax.ShapeDtypeStruct((NDEV, D, fw), BF16), jax.ShapeDtypeStruct((NDEV, c.fr, D), BF16)],
        compiler_params=_params(1))


def _attn_bwd(c, name, q, kv, do, bias, sinks, dep=None):
    S, dq, dkv = c.S, c.DQ, c.DKV
    nb = c.nb
    scale = HEAD_DIM ** -0.5

    def body(q_ref, kvc_ref, kvp_ref, do_ref, bias_ref, sink_ref, dq_ref, dkv_ref, dbias_ref, dsink_ref, dsink_acc,
             ds_sc, p_sc, qm_sc, dom_sc):
        n = pl.program_id(0)

        @pl.when(n == 0)
        def _():
            dkv_ref[...] = jnp.zeros_like(dkv_ref)
            dbias_ref[...] = jnp.zeros_like(dbias_ref)
            dsink_acc[...] = jnp.zeros_like(dsink_acc)

        low = _low_lanes()
        rows_c = pl.ds(pl.multiple_of(n * BLOCK, BLOCK), BLOCK)
        rows_p = pl.ds(pl.multiple_of(jnp.maximum(n - 1, 0) * BLOCK, BLOCK), BLOCK)
        for kvp in range(KV_PAIRS):
            k2, v2 = _pair_kv(kvc_ref, kvp_ref, kvp, dkv)
            for g in range(Q_PER_KV):
                grp = kvp * Q_PER_KV + g
                halves = []
                for par in range(2):
                    rows = slice((2 * g + par) * BLOCK, (2 * g + par + 1) * BLOCK)
                    qh = _head_operand(q_ref, grp, par, low, scale=scale)
                    doh = _head_operand(do_ref, grp, par, low)
                    p, ps = _head_probs(qh, k2, bias_ref[kvp, rows, :], sink_ref[(2 * kvp + par) * Q_PER_KV + g])
                    dp = _dot(doh, v2, NT)
                    delta = jnp.sum(p * dp, axis=-1, keepdims=True)
                    ds = p * (dp - delta)
                    dbias_ref[kvp, rows, :] += ds
                    dsink_acc[rows, kvp:kvp + 1] += -(ps * delta)
                    ds16 = ds.astype(BF16)
                    halves.append(_dot(ds16, k2, NN) * scale)
                    ds_sc[rows, :] = ds16
                    p_sc[rows, :] = p.astype(BF16)
                    qm_sc[rows, :] = qh
                    dom_sc[rows, :] = doh
                dq_ref[:, grp * LANES:(grp + 1) * LANES] = jnp.where(low, halves[0], halves[1]).astype(BF16)
            dk2 = _dot(ds_sc[...], qm_sc[...], TN)
            dv2 = _dot(p_sc[...], dom_sc[...], TN)
            lanes = slice(kvp * LANES, (kvp + 1) * LANES)
            vlanes = slice(dkv + kvp * LANES, dkv + (kvp + 1) * LANES)
            dkv_ref[rows_p, lanes] += dk2[:BLOCK]
            dkv_ref[rows_c, lanes] += dk2[BLOCK:]
            dkv_ref[rows_p, vlanes] += dv2[:BLOCK]
            dkv_ref[rows_c, vlanes] += dv2[BLOCK:]

        @pl.when(n == nb - 1)
        def _():
            dsink_ref[...] = jnp.sum(dsink_acc[...].reshape(2 * Q_PER_KV, BLOCK, KV_PAIRS), axis=1)

    return _pcall(
        body, name=name, ins=[q, kv, kv, do, bias, sinks], dep=dep, grid=(nb,),
        in_specs=_attn_specs(c) + [pl.BlockSpec((BLOCK, dq), lambda n: (n, 0)), _bias_spec(), SMEM],
        out_specs=[pl.BlockSpec((BLOCK, dq), lambda n: (n, 0)), pl.BlockSpec((S, 2 * dkv), lambda n: (0, 0)),
                   pl.BlockSpec((KV_PAIRS, PAIR_ROWS, 2 * BLOCK), lambda n: (0, 0, 0)),
                   pl.BlockSpec((2 * Q_PER_KV, KV_PAIRS), lambda n: (0, 0))],
        out_shape=[jax.ShapeDtypeStruct((S, dq), BF16), jax.ShapeDtypeStruct((S, 2 * dkv), F32),
                   jax.ShapeDtypeStruct((KV_PAIRS, PAIR_ROWS, 2 * BLOCK), F32),
                   jax.ShapeDtypeStruct((2 * Q_PER_KV, KV_PAIRS), F32)],
        scratch_shapes=[pltpu.VMEM((PAIR_ROWS, KV_PAIRS), F32), pltpu.VMEM((PAIR_ROWS, 2 * BLOCK), BF16),
                        pltpu.VMEM((PAIR_ROWS, 2 * BLOCK), BF16), pltpu.VMEM((PAIR_ROWS, LANES), BF16),
                        pltpu.VMEM((PAIR_ROWS, LANES), BF16)],
        compiler_params=_params(1))


def _sgu_bwd(c, name, a, z, dgated, ln_g, wc, wc_t, b_t, dep=None):
    S, AW, gd, tm = c.S, c.AW, c.gd, c.tms

    def body(a_ref, z_ref, dg_ref, lng_ref, wc_ref, wct_ref, bt_ref, dz_ref, dws_ref, dbt_ref, dlng_ref, dvn_ref):
        i = pl.program_id(0)

        @pl.when(i == 0)
        def _():
            dws_ref[...] = jnp.zeros_like(dws_ref)
            dbt_ref[...] = jnp.zeros_like(dbt_ref)
            dlng_ref[...] = jnp.zeros_like(dlng_ref)

        lng = lng_ref[...]
        va = a_ref[:, AW:].astype(F32)
        xc = va - jnp.mean(va, axis=-1, keepdims=True)
        rstd = lax.rsqrt(jnp.mean(xc * xc, axis=-1, keepdims=True) + EPS)
        xh = xc * rstd
        vn = (xh * lng).astype(BF16)
        causal = _sgu_masks()
        for ch in range(tm // CHUNK):
            rows = slice(ch * CHUNK, (ch + 1) * CHUNK)
            for g in range(A_GROUPS):
                cols = slice(g * gd, (g + 1) * gd)
                blk = vn[rows, cols]
                mixed = _dot(wc_ref[g], blk, NN) + bt_ref[:, g:g + 1]
                dgb = dg_ref[rows, cols].astype(F32)
                dm = dgb * a_ref[rows, cols].astype(F32)
                dbt_ref[:, g:g + 1] += jnp.sum(dm, axis=1, keepdims=True)
                dm16 = dm.astype(BF16)
                dws_ref[g] += jnp.where(causal, _dot(dm16, blk, NT), 0.0)
                dvn_ref[rows, cols] = _dot(wct_ref[g], dm16, NN)
                dz_ref[rows, cols] = (dgb * mixed * _gelu_grad(z_ref[rows, cols].astype(F32))).astype(BF16)
        dvn = dvn_ref[...]
        dlng_ref[...] += jnp.sum(dvn * xh, axis=0, keepdims=True)
        dxh = dvn * lng
        dva = rstd * (dxh - jnp.mean(dxh, axis=-1, keepdims=True) - xh * jnp.mean(dxh * xh, axis=-1, keepdims=True))
        dz_ref[:, AW:] = (dva * _gelu_grad(z_ref[:, AW:].astype(F32))).astype(BF16)

    wide = pl.BlockSpec((tm, 2 * AW), lambda i: (i, 0))
    wsp = pl.BlockSpec((A_GROUPS, CHUNK, CHUNK), lambda i: (0, 0, 0))
    btsp = pl.BlockSpec((CHUNK, A_GROUPS), lambda i: (0, 0))
    return _pcall(
        body, name=name, ins=[a, z, dgated, ln_g, wc, wc_t, b_t], dep=dep, grid=(S // tm,),
        in_specs=[wide, wide, pl.BlockSpec((tm, AW), lambda i: (i, 0)), pl.BlockSpec((1, AW), lambda i: (0, 0)),
                  wsp, wsp, btsp],
        out_specs=[wide, wsp, btsp, pl.BlockSpec((1, AW), lambda i: (0, 0))],
        out_shape=[jax.ShapeDtypeStruct((S, 2 * AW), BF16), jax.ShapeDtypeStruct((A_GROUPS, CHUNK, CHUNK), F32),
                   jax.ShapeDtypeStruct((CHUNK, A_GROUPS), F32), jax.ShapeDtypeStruct((1, AW), F32)],
        scratch_shapes=[pltpu.VMEM((tm, AW), F32)], compiler_params=_params(1))


def _adamw(name, parts, part_block, part_index, w, m, v, tr, row_off=0, n_rows=None, prev=None, dep=None):
    R, C = w.shape
    n_rows = R if n_rows is None else n_rows
    assert n_rows % tr == 0 and row_off % tr == 0
    bc1 = 1.0 - ADAM_B1 ** ADAM_STEP
    bc2 = 1.0 - ADAM_B2 ** ADAM_STEP

    def body(p_ref, w_ref, m_ref, v_ref, *rest):
        g_ref, d_ref, nm_ref, nv_ref = rest[-4:]
        g = p_ref[0].astype(F32)
        for s in range(1, part_block[0]):
            g = g + p_ref[s].astype(F32)
        nm = ADAM_B1 * m_ref[...] + (1.0 - ADAM_B1) * g
        nv = ADAM_B2 * v_ref[...] + (1.0 - ADAM_B2) * (g * g)
        g_ref[...] = g
        nm_ref[...] = nm
        nv_ref[...] = nv
        d_ref[...] = -ADAM_LR * ((nm * (1.0 / bc1)) / (jnp.sqrt(nv * (1.0 / bc2)) + ADAM_EPS) + ADAM_WD * w_ref[...])

    ob = row_off // tr
    row = pl.BlockSpec((tr, C), lambda i: (ob + i, 0))
    out = jax.ShapeDtypeStruct((R, C), F32)
    chained = prev is not None
    return _pcall(
        body, name=name, ins=[parts, w, m, v] + (list(prev) if chained else []), dep=dep, grid=(n_rows // tr,),
        in_specs=[pl.BlockSpec(part_block, part_index), row, row, row] + ([ANY] * 4 if chained else []),
        out_specs=[row, row, row, row], out_shape=[out, out, out, out],
        input_output_aliases={4 + t: t for t in range(4)} if chained else {}, compiler_params=_params(1))


def _sum_parts(name, parts, dep=None):
    def body(p_ref, out_ref):
        g = p_ref[0]
        for s in range(1, parts.shape[0]):
            g = g + p_ref[s]
        out_ref[...] = g

    return _pcall(body, name=name, ins=[parts], in_specs=[_whole(parts)], dep=dep,
                  out_shape=jax.ShapeDtypeStruct(parts.shape[1:], F32), compiler_params=_params(0))


def _place():
    return lax.axis_index("x"), lax.axis_index("y"), lax.axis_index("c")


def _slot(px, py, pc):
    return 4 * px + 2 * py + pc


def _peer(k, x, y, c):
    return x ^ ((k >> 2) & 1), y ^ ((k >> 1) & 1), c ^ (k & 1)


SEND_PEERS = {"exchange": tuple(range(1, NDEV)), "gather": (1, 2, 4, 6), "forward": (2, 4, 6),
              "broadcast": tuple(range(1, NDEV))}


def _send_copies(mode, src_refs, land_refs, send_sems, recv_sems):
    x, y, c = _place()
    me = _slot(x, y, c)
    peers = SEND_PEERS[mode]
    copies = []
    for i, k in enumerate(peers):
        peer = _peer(k, x, y, c)
        for a, land in enumerate(land_refs):
            if mode == "exchange":
                src, dst, to = src_refs[a].at[_slot(*peer)], land.at[me], peer
            elif mode in ("gather", "broadcast"):
                src, dst, to = src_refs[a], land.at[me], peer
            else:
                src = dst = land.at[_slot(*peer)]
                to = (x, y, 1 - c)
            s = a * len(peers) + i
            copies.append(pltpu.make_async_remote_copy(src_ref=src, dst_ref=dst, send_sem=send_sems.at[s],
                                                       recv_sem=recv_sems.at[s], device_id=to, device_id_type=MESH))
    return copies


def _send_start_groups(name, groups, mode, dep=None):
    sizes = [(len(s), len(l)) for s, l in groups]
    flat = [t for s, l in groups for t in (*s, *l)]
    n_in, ng = len(flat), len(groups)

    def body(*refs):
        sems, token, at = refs[n_in:n_in + 2 * ng], refs[-1], 0
        for gi, (ns, nl) in enumerate(sizes):
            for cp in _send_copies(mode, refs[at:at + ns], refs[at + ns:at + ns + nl], sems[2 * gi], sems[2 * gi + 1]):
                cp.start()
            at += ns + nl
        token[...] = jnp.zeros_like(token)

    sem_shapes = [pltpu.SemaphoreType.DMA((nl * len(SEND_PEERS[mode]),)) for _, nl in sizes for _ in range(2)]
    if any(dep is t for t in flat):
        dep = None
    out = _pcall(
        body, name=name, ins=[pltpu.with_memory_space_constraint(t, pltpu.HBM) for t in flat],
        in_specs=[HBM] * n_in, dep=dep,
        out_shape=(*sem_shapes, *[pltpu.HBM(t.shape, t.dtype) for t in flat], jax.ShapeDtypeStruct((8, LANES), F32)),
        out_specs=(*[SEM] * (2 * ng), *[HBM] * n_in, pl.BlockSpec(memory_space=pltpu.VMEM)),
        input_output_aliases={i: 2 * ng + i for i in range(n_in)},
        compiler_params=pltpu.CompilerParams(has_side_effects=pltpu.SideEffectType.DATAFLOW_SIDE_EFFECTING))
    started, at = [], 2 * ng
    for gi, (ns, nl) in enumerate(sizes):
        started.append((out[-1], out[2 * gi], out[2 * gi + 1], list(out[at:at + ns]), list(out[at + ns:at + ns + nl])))
        at += ns + nl
    return started


def _send_start(name, srcs, lands, mode, dep=None):
    return _send_start_groups(name, [(srcs, lands)], mode, dep=dep)[0]


def _send_wait(name, started, mode, dep=None):
    _, send_sems, recv_sems, srcs, lands = started
    n_src, n = len(srcs), len(lands)

    def body(*refs):
        src_refs, land_refs = refs[:n_src], refs[n_src:n_src + n]
        ssem, rsem = refs[n_src + n], refs[n_src + n + 1]
        for cp in _send_copies(mode, src_refs, land_refs, ssem, rsem):
            cp.wait_send()
            cp.wait_recv()

    thru = [pltpu.HBM(t.shape, t.dtype) for t in [*srcs, *lands]]
    out = _pcall(
        body, name=name, ins=[*srcs, *lands, send_sems, recv_sems], in_specs=[HBM] * (n_src + n) + [SEM, SEM], dep=dep,
        out_shape=tuple(thru), out_specs=tuple([HBM] * (n_src + n)),
        input_output_aliases={i: i for i in range(n_src + n)},
        compiler_params=pltpu.CompilerParams(has_side_effects=pltpu.SideEffectType.DATAFLOW_SIDE_EFFECTING))
    return list(out[n_src:])


def _landing(block, me):
    zone = lax.empty((NDEV, *block.shape), block.dtype)
    return lax.dynamic_update_slice(zone, block[None], (me,) + (0,) * block.ndim)


def _rows128(t):
    flat = t.reshape(-1)
    n = flat.shape[0]
    rows = -(-n // (8 * LANES)) * 8
    return jnp.pad(flat, (0, rows * LANES - n)).reshape(rows, LANES)


def kernel(x, mix_norm_g, ffn_norm_g, a_w_in, a_ln_g, a_w_spatial, a_b_spatial, a_w_out, kv_norm_g, w_k, w_v, b_w_q, b_sinks, b_w_o, rel_bias, ffn_w1, ffn_w2, final_norm_g, loss_target, m_mix_norm_g, m_ffn_norm_g, m_a_w_in, m_a_ln_g, m_a_w_spatial, m_a_b_spatial, m_a_w_out, m_kv_norm_g, m_w_k, m_w_v, m_b_w_q, m_b_sinks, m_b_w_o, m_rel_bias, m_ffn_w1, m_ffn_w2, m_final_norm_g, v_mix_norm_g, v_ffn_norm_g, v_a_w_in, v_a_ln_g, v_a_w_spatial, v_a_b_spatial, v_a_w_out, v_kv_norm_g, v_w_k, v_w_v, v_b_w_q, v_b_sinks, v_b_w_o, v_rel_bias, v_ffn_w1, v_ffn_w2, v_final_norm_g):
    c = _config(x, a_w_in, a_w_out, w_k, b_w_q, b_w_o, ffn_w1, ffn_w2)
    S, D, LA, LB, LF = c.S, c.D, c.LA, c.LB, c.LF
    weights = dict(mix_norm_g=mix_norm_g, ffn_norm_g=ffn_norm_g, a_w_in=a_w_in, a_ln_g=a_ln_g, a_w_spatial=a_w_spatial,
                   a_b_spatial=a_b_spatial, a_w_out=a_w_out, kv_norm_g=kv_norm_g, w_k=w_k, w_v=w_v, b_w_q=b_w_q,
                   b_sinks=b_sinks, b_w_o=b_w_o, rel_bias=rel_bias, ffn_w1=ffn_w1, ffn_w2=ffn_w2,
                   final_norm_g=final_norm_g)
    m_in = dict(mix_norm_g=m_mix_norm_g, ffn_norm_g=m_ffn_norm_g, a_w_in=m_a_w_in, a_ln_g=m_a_ln_g,
                a_w_spatial=m_a_w_spatial, a_b_spatial=m_a_b_spatial, a_w_out=m_a_w_out, kv_norm_g=m_kv_norm_g,
                w_k=m_w_k, w_v=m_w_v, b_w_q=m_b_w_q, b_sinks=m_b_sinks, b_w_o=m_b_w_o, rel_bias=m_rel_bias,
                ffn_w1=m_ffn_w1, ffn_w2=m_ffn_w2, final_norm_g=m_final_norm_g)
    v_in = dict(mix_norm_g=v_mix_norm_g, ffn_norm_g=v_ffn_norm_g, a_w_in=v_a_w_in, a_ln_g=v_a_ln_g,
                a_w_spatial=v_a_w_spatial, a_b_spatial=v_a_b_spatial, a_w_out=v_a_w_out, kv_norm_g=v_kv_norm_g,
                w_k=v_w_k, w_v=v_w_v, b_w_q=v_b_w_q, b_sinks=v_b_sinks, b_w_o=v_b_w_o, rel_bias=v_rel_bias,
                ffn_w1=v_ffn_w1, ffn_w2=v_ffn_w2, final_norm_g=v_final_norm_g)
    names = list(weights)
    seq = _Seq()
    me = _slot(*_place())
    bf = lambda t: t.astype(BF16)

    tr = lambda t: bf(jnp.swapaxes(t, -1, -2))
    groups = []
    for l in range(LA):
        groups += [[tr(a_w_in[l])[None]] + ([a_ln_g] if l == 0 else []), [bf(a_w_out[l])],
                   [tr(ffn_w1[l])[None], bf(ffn_w2[l])]]
    gb = 3 * LA
    for l in range(LB):
        extra = [bf(jnp.concatenate([w_k, w_v], axis=1))] if l == 0 else []
        groups += [extra + [bf(b_w_q[l]), bf(b_w_o[l])], [tr(ffn_w1[LA + l])[None], bf(ffn_w2[LA + l])]]
    started, forwarding = [], {}

    def start(name, some):
        got = seq(_send_start_groups, name, [(grp, [_landing(t, me) for t in grp]) for grp in some], "gather")
        seq.last = got[0][0]
        started.extend(got)

    start("weights_start_first", groups[:1])

    def forward(i):
        lands = seq(_send_wait, f"weights_wait{i}", started[i], "gather")
        forwarding[i] = seq(_send_start, f"weights_forward{i}", [], lands, "forward")

    def arrive(i):
        if i not in forwarding:
            forward(i)
        return seq(_send_wait, f"weights_arrive{i}", forwarding[i], "forward")

    causal = jnp.tril(jnp.ones((CHUNK, CHUNK), bool))
    wsp = jnp.where(causal[None, None], a_w_spatial, 0.0)
    wsp16 = wsp.astype(BF16)
    wsp16_t = jnp.swapaxes(wsp, -1, -2).astype(BF16)
    bsp_t = jnp.swapaxes(a_b_spatial, -1, -2)
    mix_g = mix_norm_g.reshape(-1, 1, D)
    ffn_g = ffn_norm_g.reshape(-1, 1, D)
    kv_g = kv_norm_g.reshape(1, D)
    fin_g = final_norm_g.reshape(1, D)
    onehot = _bucket_onehot()
    bias = _slot_bias(seq(_band_bias, rel_bias.T, onehot).reshape(N_HEADS, BLOCK, 2 * BLOCK))

    h = x.reshape(S, D)
    sav_a, sav_b, wts_a, wts_b = [], [], [], []
    for l in range(LA):
        got = arrive(3 * l)
        w_in = got[0]
        if l == 0:
            start("weights_start_rest", groups[1:])
            ln_g_full = jnp.transpose(got[1], (1, 0, 2)).reshape(LA, 1, c.AW)
        z, a, hn = seq(_a_in_fwd, c, f"a_in_fwd{l}", h, mix_g[l], w_in, 0)
        forward(3 * l + 1)
        gated = seq(_sgu_fwd, c, f"sgu_fwd{l}", a, ln_g_full[l], wsp16[l], bsp_t[l])
        (wout,) = arrive(3 * l + 1)
        if l > 0:
            forward(3 * l + 2)
        h1 = seq(_mm_res, c, f"a_out_fwd{l}", gated, wout, c.ar, 0, h)
        w1, rows = arrive(3 * l + 2)
        if l == LA - 1:
            forward(gb)
        p, h2, hnf = seq(_ffn_fwd, c, f"ffn_fwd{l}", h1, ffn_g[l], w1, 0, rows)
        sav_a.append((h, z, a, hn, gated, h1, p, hnf))
        wts_a.append((w_in, 0, w1, 0, rows, wout, 0))
        h = h2
    h_kv = h
    for l in range(LB):
        got = arrive(gb + 2 * l)
        if l == 0:
            wkv, got = got[0], got[1:]
            kv, hkv = seq(_rms_mm_rows, c, "kv_fwd", h, kv_g, wkv, c.kr, 0, 2 * c.DKV)
        wq, wo = _slot_cols(got[0]), _slot_rows(got[1])
        q, hn = seq(_rms_mm_rows, c, f"q_fwd{l}", h, mix_g[LA + l], wq, c.qr, 0, c.DQ)
        forward(gb + 2 * l + 1)
        o = seq(_attn_fwd, c, f"attn_fwd{l}", q, kv, bias, b_sinks[l])
        h1 = seq(_mm_res, c, f"o_fwd{l}", o, wo, c.orr, 0, h)
        w1, rows = arrive(gb + 2 * l + 1)
        if l + 1 < LB:
            forward(gb + 2 * l + 2)
        p, h2, hnf = seq(_ffn_fwd, c, f"ffn_fwd{LA + l}", h1, ffn_g[LA + l], w1, 0, rows)
        sav_b.append((h, q, hn, o, h1, p, hnf))
        wts_b.append((wq, wo, w1, rows))
        h = h2
    dh, d_fin_g, loss_row = seq(_final_loss, c, h, fin_g, loss_target.reshape(S, D))

    results = {}
    in_flight = []

    def update(k, parts, layer, col_blk=0):
        w = weights[k]
        rows_l, ncols = (w.shape[-2], w.shape[-1]) if w.ndim == 3 else w.shape
        flat = lambda t: t.reshape(-1, ncols)
        tr = min(256, rows_l)
        results[k] = seq(_adamw, f"adamw_{k}{layer}", parts, (NDEV, tr, ncols), lambda i: (0, i, col_blk),
                         flat(w), flat(m_in[k]), flat(v_in[k]), tr, row_off=layer * rows_l, n_rows=rows_l,
                         prev=results.get(k))

    def land(tag, entry):
        lands = seq(_send_wait, f"grads_wait_{tag}", entry[1], "exchange")
        for keys, parts in zip(entry[0], lands):
            for k, layer, col_blk in keys:
                update(k, parts, layer, col_blk)

    def send(tag, items):
        slabs = [t for _, t in items]
        own = [_landing(lax.dynamic_index_in_dim(t, me, 0, keepdims=False), me) for t in slabs]
        st = seq(_send_start, f"grads_start_{tag}", slabs, own, "exchange")
        in_flight.append((tag, ([k for k, _ in items], st)))
        while len(in_flight) > EXCHANGE_LAG:
            land(*in_flight.pop(0))

    d_mix_g, d_ffn_g = [None] * LF, [None] * LF
    dkv_list, dbias_list, dsink_list = [], [], [None] * LB

    def ffn_bwd(lf, dh, h1, p, hnf, w1, w1_i, rows):
        da, dh1, d_ffn_g[lf], dhb = seq(_ffn_bwd_data, c, f"ffn_bwd_data{lf}", dh, p, w1, w1_i, rows, h1, ffn_g[lf])
        dw1, dw2 = seq(_ffn_bwd_w, c, f"ffn_bwd_w{lf}", hnf, da, p, dhb)
        send(f"ffn{lf}", [([("ffn_w1", lf, 0)], dw1), ([("ffn_w2", lf, 0)], dw2)])
        return dh1

    for l in reversed(range(LB)):
        h0, q, hn, o, h1, p, hnf = sav_b[l]
        wq, wo, w1, rows = wts_b[l]
        dh1 = ffn_bwd(LA + l, dh, h1, p, hnf, w1, 0, rows)
        do = seq(_bwd_rows_data, c, f"o_bwd_data{l}", dh1, wo, c.orr, 0)
        dwo = _unslot_rows(seq(_wgrad_rows, c, f"o_bwd_w{l}", o, [dh1], c.DQ, D))
        dq, dkv, dbias, dsink = seq(_attn_bwd, c, f"attn_bwd{l}", q, kv, do, bias, b_sinks[l])
        dsink_list[l] = dsink.reshape(Q_PER_KV, 2, KV_PAIRS).transpose(2, 1, 0).reshape(1, N_HEADS)
        dkv_list.append(dkv)
        dbias_list.append(_unslot_bias(dbias))
        dwq = _unslot_cols(seq(_wgrad_rows, c, f"q_bwd_w{l}", hn, [dq], D, c.DQ))
        send(f"attn{l}", [([("b_w_o", l, 0)], dwo), ([("b_w_q", l, 0)], dwq)])
        dh, d_mix_g[LA + l] = seq(_bwd_rows_to_stream, c, f"q_bwd_data{l}", [dq], wq, c.qr, 0, c.DQ, h0,
                                  mix_g[LA + l], dh1)
    dwkv = seq(_wgrad_rows, c, "kv_bwd_w", hkv, dkv_list, D, 2 * c.DKV)
    send("kv", [([("w_k", 0, 0), ("w_v", 0, 1)], dwkv)])
    dh, d_kv_g = seq(_bwd_rows_to_stream, c, "kv_bwd_data", dkv_list, wkv, c.kr, 0, 2 * c.DKV, h_kv, kv_g, dh)
    d_rel_t = seq(_band_bias_grad, dbias_list, onehot)
    d_wsp, d_bsp, d_lng = [None] * LA, [None] * LA, [None] * LA
    for l in reversed(range(LA)):
        h0, z, a, hn, gated, h1, p, hnf = sav_a[l]
        w_in, in_i, w1, w1_i, rows, wout, wout_i = wts_a[l]
        dh1 = ffn_bwd(l, dh, h1, p, hnf, w1, w1_i, rows)
        dgated = seq(_bwd_rows_data, c, f"a_out_bwd_data{l}", dh1, wout, c.ar, wout_i)
        dwout = seq(_wgrad_rows, c, f"a_out_bwd_w{l}", gated, [dh1], c.AW, D)
        send(f"a_out{l}", [([("a_w_out", l, 0)], dwout)])
        dz, d_wsp[l], dbt, d_lng[l] = seq(_sgu_bwd, c, f"sgu_bwd{l}", a, z, dgated, ln_g_full[l], wsp16[l],
                                          wsp16_t[l], bsp_t[l])
        d_bsp[l] = dbt.T
        dwin = seq(_wgrad_cols, c, f"a_in_bwd_w{l}", hn, dz)
        send(f"a_in{l}", [([("a_w_in", l, 0)], dwin)])
        dh, d_mix_g[l] = seq(_bwd_cols_to_stream, c, f"a_in_bwd_data{l}", dz, w_in, in_i, h0, mix_g[l], dh1)
    grad_x = dh.reshape(1, S, D)

    small = {
        "mix_norm_g": jnp.concatenate(d_mix_g, axis=0), "ffn_norm_g": jnp.concatenate(d_ffn_g, axis=0),
        "a_w_spatial": jnp.stack(d_wsp), "a_b_spatial": jnp.stack(d_bsp), "kv_norm_g": d_kv_g,
        "b_sinks": jnp.concatenate(dsink_list, axis=0), "rel_bias": d_rel_t.T, "final_norm_g": d_fin_g,
    }
    small_names = list(small)
    packs = [_rows128(small[k]) for k in small_names] + [_rows128(jnp.concatenate(d_lng, axis=0)), _rows128(loss_row)]
    offs = [int(o) for o in np.cumsum([0] + [p.shape[0] for p in packs])]
    Rs = offs[-1] + (-offs[-1]) % (8 * NDEV)
    tail_rows = Rs - offs[len(small_names)]
    packed = jnp.concatenate(packs + [jnp.zeros((Rs - offs[-1], LANES), F32)], axis=0)
    slab = packed.reshape(NDEV, Rs // NDEV, LANES)
    st = seq(_send_start, "small_grads_start", [slab],
             [_landing(lax.dynamic_index_in_dim(slab, me, 0, keepdims=False), me)], "exchange")
    while len(in_flight) > 1:
        land(*in_flight.pop(0))
    (parts,) = seq(_send_wait, "small_grads_wait", st, "exchange")
    mine = seq(_sum_parts, "small_grads_sum", parts)
    st = seq(_send_start, "small_sums_start", [mine], [_landing(mine, me)], "broadcast")
    while in_flight:
        land(*in_flight.pop(0))
    (sums,) = seq(_send_wait, "small_sums_wait", st, "broadcast")
    small_all = sums.reshape(1, Rs, LANES)
    loss = sums.reshape(Rs, LANES)[offs[-2], 0]

    grads, deltas, new_m, new_v = {}, {}, {}, {}

    def put(k, outs, shape):
        grads[k], deltas[k], new_m[k], new_v[k] = (t.reshape(shape) for t in outs)

    def pack_state(d):
        return jnp.concatenate([_rows128(d[k]) for k in small_names] + [jnp.zeros((tail_rows, LANES), F32)], axis=0)

    outs = seq(_adamw, "adamw_small", small_all, (1, Rs, LANES), lambda i: (0, 0, 0),
               pack_state(weights), pack_state(m_in), pack_state(v_in), Rs)
    for n_, k in enumerate(small_names):
        shape = weights[k].shape
        size = int(np.prod(shape))
        put(k, [t[offs[n_]:offs[n_ + 1]].reshape(-1)[:size] for t in outs], shape)
    lng_sum = outs[0][offs[-3]:offs[-2]].reshape(-1)[:LA * c.AW].reshape(LA, c.AW)
    lng_mine = lax.dynamic_slice_in_dim(lng_sum, me * c.ar, c.ar, axis=1)
    lng_parts = jnp.concatenate([lng_mine[None], jnp.zeros((NDEV - 1, LA, c.ar), F32)], axis=0)
    put("a_ln_g", seq(_adamw, "adamw_ln_g", lng_parts, (NDEV, LA, c.ar), lambda i: (0, 0, 0),
                      a_ln_g, m_in["a_ln_g"], v_in["a_ln_g"], LA), a_ln_g.shape)
    for k in ("a_w_in", "ffn_w1", "ffn_w2", "a_w_out", "b_w_o", "b_w_q", "w_k", "w_v"):
        put(k, results[k], weights[k].shape)

    return (loss, grad_x, *[grads[k] for k in names], *[deltas[k] for k in names],
            *[new_m[k] for k in names], *[new_v[k] for k in names])
```

```python
import numpy as np
import math
import jax
import jax.numpy as jnp
from jax import lax
from jax.experimental import pallas as pl
from jax.experimental.pallas import tpu as pltpu

F32 = jnp.float32
BF16 = jnp.bfloat16

NDEV = 8
EPS = 1e-6
CHUNK = 128
A_GROUPS = 8
N_HEADS = 16
N_KV_HEADS = 4
Q_PER_KV = N_HEADS // N_KV_HEADS
HEAD_DIM = 64
BLOCK = 128
N_BUCKETS = 32
MAX_DISTANCE = 128
ADAM_LR, ADAM_B1, ADAM_B2, ADAM_EPS, ADAM_WD, ADAM_STEP = 0.001, 0.9, 0.999, 1e-08, 0.01, 10
LANES = 128
VMEM_LIMIT = 56 * 1024 * 1024
INV_SQRT2 = 0.7071067811865476
INV_SQRT_2PI = 0.3989422804014327
EXCHANGE_LAG = 4

HBM = pl.BlockSpec(memory_space=pltpu.HBM)
SMEM = pl.BlockSpec(memory_space=pltpu.SMEM)
ANY = pl.BlockSpec(memory_space=pl.ANY)
SEM = pl.BlockSpec(memory_space=pltpu.SEMAPHORE)
MESH = pl.DeviceIdType.MESH


def _params(n_grid):
    return pltpu.CompilerParams(dimension_semantics=("arbitrary",) * n_grid, vmem_limit_bytes=VMEM_LIMIT)


def _const(block, index_map):
    return pl.BlockSpec(block, index_map, pipeline_mode=pl.Buffered(1))


def _pcall(body, *, ins, in_specs, dep=None, **kw):
    n_in = len(ins)
    if dep is None or any(dep is t for t in ins):
        return pl.pallas_call(body, in_specs=list(in_specs), **kw)(*ins)

    def with_dep(*refs):
        body(*refs[:n_in], *refs[n_in + 1:])

    return pl.pallas_call(with_dep, in_specs=[*in_specs, ANY], **kw)(*ins, dep)


class _Seq:
    def __init__(self):
        self.last = None

    def __call__(self, fn, *args, **kw):
        out = fn(*args, dep=self.last, **kw)
        self.last = out[0] if isinstance(out, (tuple, list)) else out
        return out


def _rstd(h):
    return lax.rsqrt(jnp.mean(h * h, axis=-1, keepdims=True) + EPS)


def _rms_bwd(dhn, h, g, dres):
    r = _rstd(h)
    xh = h * r
    dg = jnp.sum(dhn * xh, axis=0, keepdims=True)
    dxh = dhn * g
    dx = r * (dxh - xh * jnp.mean(dxh * xh, axis=-1, keepdims=True))
    return dres + dx, dg


def _gelu(z):
    return 0.5 * z * (1.0 + lax.erf(z * INV_SQRT2))


def _gelu_grad(z):
    return 0.5 * (1.0 + lax.erf(z * INV_SQRT2)) + z * (jnp.exp(-0.5 * z * z) * INV_SQRT_2PI)


def _dot(a, b, dims):
    return lax.dot_general(a, b, (dims, ((), ())), preferred_element_type=F32)


NN = ((1,), (0,))
NT = ((1,), (1,))
TN = ((0,), (0,))


def _mm(name, ins, in_specs, out_shapes, out_specs, *, grid, dims, nk, acc_shape, load_a, load_b, epilogue,
        dep=None):
    n_in, n_out = len(ins), len(out_shapes)
    kax = len(grid) - 1

    def body(*refs):
        in_refs = refs[:n_in]
        out_refs = refs[n_in:n_in + n_out]
        a = load_a(in_refs, out_refs)
        b = load_b(in_refs)
        prod = _dot(a, b, dims)
        if nk == 1:
            epilogue(prod, in_refs, out_refs)
        else:
            acc = refs[n_in + n_out]
            k = pl.program_id(kax)

            @pl.when(k == 0)
            def _():
                acc[...] = prod

            @pl.when(k > 0)
            def _():
                acc[...] += prod

            @pl.when(k == nk - 1)
            def _():
                epilogue(acc[...], in_refs, out_refs)

    return _pcall(
        body, name=name, ins=ins, in_specs=in_specs, dep=dep, grid=grid, out_specs=out_specs, out_shape=out_shapes,
        scratch_shapes=[pltpu.VMEM(acc_shape, F32)] if nk > 1 else [], compiler_params=_params(len(grid)))


def _bf(ref_idx):
    return lambda in_refs, *_: in_refs[ref_idx][...].astype(BF16)


def _b_view(ref_idx, rows):
    def load(in_refs):
        b = in_refs[ref_idx][...]
        return b.reshape(rows, b.shape[-1])
    return load


class Cfg:
    pass


def _config(x, a_w_in, a_w_out, w_k, b_w_q, b_w_o, ffn_w1, ffn_w2):
    c = Cfg()
    c.S, c.D = x.shape[1], x.shape[2]
    c.LA, _, c.cw = a_w_in.shape
    c.AW2 = NDEV * c.cw
    c.AW = c.AW2 // 2
    c.gd = c.AW // A_GROUPS
    c.ar = a_w_out.shape[1]
    c.LF, _, c.fw = ffn_w1.shape
    c.fr = ffn_w2.shape[1]
    c.LB, c.qr, c.DQ = b_w_q.shape
    c.orr = b_w_o.shape[1]
    c.kr, c.DKV = w_k.shape
    c.tm = min(1024, c.S)
    c.tmw = min(512, c.S)
    c.tms = min(256, c.S)
    c.nb = c.S // BLOCK
    assert c.cw == c.fw == c.fr and c.AW == NDEV * c.ar and c.D == NDEV * c.qr == NDEV * c.kr
    assert c.DQ == NDEV * c.orr == N_HEADS * HEAD_DIM and c.DKV == N_KV_HEADS * HEAD_DIM
    assert c.S % c.tm == 0 and c.S % c.tmw == 0 and c.S % c.tms == 0 and c.tms % CHUNK == 0 and c.gd % LANES == 0
    assert c.LA >= 1 and c.LB >= 1 and c.LF == c.LA + c.LB
    return c


def _cached_rms(h_idx, g_idx, hn_out_idx, jax_axis=1):
    def load(in_refs, out_refs):
        hn_ref = out_refs[hn_out_idx]

        @pl.when(pl.program_id(jax_axis) == 0)
        def _():
            h = in_refs[h_idx][...]
            hn_ref[...] = (h * _rstd(h) * in_refs[g_idx][...]).astype(BF16)

        return hn_ref[...]
    return load


def _a_in_fwd(c, name, h, g, col, ci, dep=None):
    S, D, cw, tm = c.S, c.D, c.cw, c.tmw

    def body(h_ref, g_ref, w_ref, z_ref, a_ref, hn_ref):
        h = h_ref[...]
        hn = (h * _rstd(h) * g_ref[...]).astype(BF16)
        hn_ref[...] = hn
        for j in range(NDEV):
            cols = slice(j * cw, (j + 1) * cw)
            z = _dot(hn, w_ref[j], NT)
            z_ref[:, cols] = z.astype(BF16)
            a_ref[:, cols] = _gelu(z).astype(BF16)

    row = pl.BlockSpec((tm, D), lambda i: (i, 0))
    wide = pl.BlockSpec((tm, c.AW2), lambda i: (i, 0))
    return _pcall(
        body, name=name, ins=[h, g, col], dep=dep, grid=(S // tm,),
        in_specs=[row, pl.BlockSpec((1, D), lambda i: (0, 0)), _const((NDEV, None, cw, D), lambda i: (0, ci, 0, 0))],
        out_specs=[wide, wide, row],
        out_shape=[jax.ShapeDtypeStruct((S, c.AW2), BF16), jax.ShapeDtypeStruct((S, c.AW2), BF16),
                   jax.ShapeDtypeStruct((S, D), BF16)],
        compiler_params=_params(1))


def _rms_mm_rows(c, name, h, g, slab, blk_rows, blk_idx, n_out, dep=None):
    S, D, tm = c.S, c.D, c.tm

    def epilogue(acc, in_refs, out_refs):
        out_refs[0][...] = acc.astype(BF16)

    return _mm(
        name, [h, slab, g],
        [pl.BlockSpec((tm, D), lambda i, j, k: (i, 0)),
         pl.BlockSpec((NDEV, blk_rows, n_out), lambda i, j, k: (0, blk_idx, 0)),
         pl.BlockSpec((1, D), lambda i, j, k: (0, 0))],
        [jax.ShapeDtypeStruct((S, n_out), BF16), jax.ShapeDtypeStruct((S, D), BF16)],
        [pl.BlockSpec((tm, n_out), lambda i, j, k: (i, 0)), pl.BlockSpec((tm, D), lambda i, j, k: (i, 0))],
        grid=(S // tm, 1, 1), dims=NN, nk=1, acc_shape=None,
        load_a=_cached_rms(0, 2, 1), load_b=_b_view(1, NDEV * blk_rows), epilogue=epilogue, dep=dep)


def _mm_res(c, name, a, slab, blk_rows, blk_idx, res, dep=None):
    S, D, tm = c.S, c.D, c.tm
    K = NDEV * blk_rows

    def epilogue(acc, in_refs, out_refs):
        out_refs[0][...] = in_refs[2][...] + acc

    return _mm(
        name, [a, slab, res],
        [pl.BlockSpec((tm, K), lambda i, j, k: (i, 0)),
         pl.BlockSpec((NDEV, blk_rows, D), lambda i, j, k: (0, blk_idx, 0)),
         pl.BlockSpec((tm, D), lambda i, j, k: (i, 0))],
        [jax.ShapeDtypeStruct((S, D), F32)], [pl.BlockSpec((tm, D), lambda i, j, k: (i, 0))],
        grid=(S // tm, 1, 1), dims=NN, nk=1, acc_shape=None,
        load_a=_bf(0), load_b=_b_view(1, K), epilogue=epilogue, dep=dep)[0]


def _sgu_masks():
    ii = lax.broadcasted_iota(jnp.int32, (CHUNK, CHUNK), 0)
    jj = lax.broadcasted_iota(jnp.int32, (CHUNK, CHUNK), 1)
    return ii >= jj


def _sgu_fwd(c, name, a, ln_g, wc, b_t, dep=None):
    S, AW, gd, tm = c.S, c.AW, c.gd, c.tms

    def body(a_ref, lng_ref, wc_ref, bt_ref, out_ref):
        va = a_ref[:, AW:].astype(F32)
        xc = va - jnp.mean(va, axis=-1, keepdims=True)
        vn = (xc * lax.rsqrt(jnp.mean(xc * xc, axis=-1, keepdims=True) + EPS) * lng_ref[...]).astype(BF16)
        for ch in range(tm // CHUNK):
            rows = slice(ch * CHUNK, (ch + 1) * CHUNK)
            for g in range(A_GROUPS):
                cols = slice(g * gd, (g + 1) * gd)
                mixed = _dot(wc_ref[g], vn[rows, cols], NN) + bt_ref[:, g:g + 1]
                out_ref[rows, cols] = (a_ref[rows, cols].astype(F32) * mixed).astype(BF16)

    return _pcall(
        body, name=name, ins=[a, ln_g, wc, b_t], dep=dep, grid=(S // tm,),
        in_specs=[pl.BlockSpec((tm, 2 * AW), lambda i: (i, 0)), pl.BlockSpec((1, AW), lambda i: (0, 0)),
                  pl.BlockSpec((A_GROUPS, CHUNK, CHUNK), lambda i: (0, 0, 0)),
                  pl.BlockSpec((CHUNK, A_GROUPS), lambda i: (0, 0))],
        out_specs=pl.BlockSpec((tm, AW), lambda i: (i, 0)),
        out_shape=jax.ShapeDtypeStruct((S, AW), BF16), compiler_params=_params(1))


def _ffn_fwd(c, name, h, g, col, ci, rows, dep=None):
    S, D, fw, tm = c.S, c.D, c.fw, c.tmw
    F = NDEV * fw

    def body(h_ref, g_ref, w1_ref, w2_ref, p_ref, out_ref, hn_ref, r_ref):
        h = h_ref[...]
        hn = (h * _rstd(h) * g_ref[...]).astype(BF16)
        hn_ref[...] = hn
        for j in range(NDEV):
            cols = slice(j * fw, (j + 1) * fw)
            p = jnp.maximum(_dot(hn, w1_ref[j], NT), 0.0)
            p_ref[:, cols] = p.astype(BF16)
            r_ref[:, cols] = (p * p).astype(BF16)
        out_ref[...] = h + _dot(r_ref[...], w2_ref[...].reshape(F, D), NN)

    row = pl.BlockSpec((tm, D), lambda i: (i, 0))
    return _pcall(
        body, name=name, ins=[h, g, col, rows], dep=dep, grid=(S // tm,),
        in_specs=[row, pl.BlockSpec((1, D), lambda i: (0, 0)),
                  _const((NDEV, None, fw, D), lambda i: (0, ci, 0, 0)), _const((NDEV, c.fr, D), lambda i: (0, 0, 0))],
        out_specs=[pl.BlockSpec((tm, F), lambda i: (i, 0)), row, row],
        out_shape=[jax.ShapeDtypeStruct((S, F), BF16), jax.ShapeDtypeStruct((S, D), F32),
                   jax.ShapeDtypeStruct((S, D), BF16)],
        scratch_shapes=[pltpu.VMEM((tm, F), BF16)], compiler_params=_params(1))


def _bucket_table():
    qi = np.arange(BLOCK)[:, None]
    kj = np.arange(2 * BLOCK)[None, :]
    d = np.maximum(qi + BLOCK - kj, 0)
    max_exact = N_BUCKETS // 2
    ratio = np.log(np.maximum(d, 1).astype(np.float32) / np.float32(max_exact)) / np.float32(
        math.log(MAX_DISTANCE / max_exact))
    large = np.minimum(max_exact + (ratio.astype(np.float32) * np.float32(N_BUCKETS - max_exact)).astype(np.int32),
                       N_BUCKETS - 1)
    return np.where(d < max_exact, d, large).astype(np.int32)


def _bucket_onehot():
    b = jnp.asarray(_bucket_table().reshape(1, -1))
    return (b == lax.broadcasted_iota(jnp.int32, (N_BUCKETS, b.shape[1]), 0)).astype(F32)


def _whole(t):
    return pl.BlockSpec(t.shape, lambda: (0,) * t.ndim)


def _band_bias(rel_bias_t, onehot, dep=None):
    def body(r_ref, oh_ref, out_ref):
        out_ref[...] = lax.dot_general(r_ref[...], oh_ref[...], (NN, ((), ())), preferred_element_type=F32,
                                       precision=lax.Precision.HIGHEST)

    n = onehot.shape[1]
    return _pcall(body, name="band_bias", ins=[rel_bias_t, onehot], in_specs=[_whole(rel_bias_t), _whole(onehot)],
                  dep=dep, out_shape=jax.ShapeDtypeStruct((N_HEADS, n), F32), compiler_params=_params(0))


def _band_bias_grad(dbias_list, onehot, dep=None):
    n_in = len(dbias_list)

    def body(*refs):
        oh_ref, out_ref = refs[n_in], refs[n_in + 1]
        d = refs[0][...]
        for r in refs[1:n_in]:
            d = d + r[...]
        out_ref[...] = lax.dot_general(d, oh_ref[...], (NT, ((), ())), preferred_element_type=F32,
                                       precision=lax.Precision.HIGHEST)

    ins = [*dbias_list, onehot]
    return _pcall(body, name="band_bias_grad", ins=ins, in_specs=[_whole(t) for t in ins], dep=dep,
                  out_shape=jax.ShapeDtypeStruct((N_HEADS, N_BUCKETS), F32), compiler_params=_params(0))


KV_PAIRS = N_KV_HEADS // 2
PAIR_ROWS = 2 * Q_PER_KV * BLOCK
MASKED = float(np.finfo(np.float32).min) / 2


def _slot_cols(w):
    lead = w.shape[:-1]
    return w.reshape(*lead, KV_PAIRS, 2, Q_PER_KV, HEAD_DIM).swapaxes(-3, -2).reshape(*lead, N_HEADS * HEAD_DIM)


def _unslot_cols(w):
    lead = w.shape[:-1]
    return w.reshape(*lead, KV_PAIRS, Q_PER_KV, 2, HEAD_DIM).swapaxes(-3, -2).reshape(*lead, N_HEADS * HEAD_DIM)


def _slot_rows(blocks):
    n = blocks.shape[-1]
    return blocks.reshape(KV_PAIRS, 2, Q_PER_KV, HEAD_DIM, n).swapaxes(1, 2).reshape(blocks.shape)


def _unslot_rows(blocks):
    n = blocks.shape[-1]
    return blocks.reshape(KV_PAIRS, Q_PER_KV, 2, HEAD_DIM, n).swapaxes(1, 2).reshape(blocks.shape)


def _slot_bias(bias):
    qi = np.arange(BLOCK)[:, None]
    kj = np.arange(2 * BLOCK)[None, :]
    dist = qi + BLOCK - kj
    window = (dist >= 0) & (dist < BLOCK)
    b = bias.reshape(KV_PAIRS, 2, Q_PER_KV, BLOCK, 2 * BLOCK).swapaxes(1, 2).reshape(KV_PAIRS, PAIR_ROWS, 2 * BLOCK)
    tile = lambda mk: jnp.asarray(np.tile(mk, (2 * Q_PER_KV, 1)))[None]
    return jnp.stack([jnp.where(tile(window & (kj >= BLOCK)), b, MASKED), jnp.where(tile(window), b, MASKED)])


def _unslot_bias(db):
    return db.reshape(KV_PAIRS, Q_PER_KV, 2, BLOCK, 2 * BLOCK).swapaxes(1, 2).reshape(N_HEADS, -1)


def _pair_kv(kvc_ref, kvp_ref, kvp, dkv):
    lanes = slice(kvp * LANES, (kvp + 1) * LANES)
    vlanes = slice(dkv + kvp * LANES, dkv + (kvp + 1) * LANES)
    k2 = jnp.concatenate([kvp_ref[:, lanes], kvc_ref[:, lanes]], axis=0)
    v2 = jnp.concatenate([kvp_ref[:, vlanes], kvc_ref[:, vlanes]], axis=0)
    return k2, v2


def _head_operand(ref, grp, par, low, scale=None):
    xg = ref[:, grp * LANES:(grp + 1) * LANES]
    if scale is not None:
        xg = xg * scale
    zero = jnp.zeros_like(xg)
    return jnp.where(low, xg, zero) if par == 0 else jnp.where(low, zero, xg)


def _head_probs(qh, k2, bias_rows, sink):
    s = _dot(qh, k2, NT) + bias_rows
    m = jnp.maximum(jnp.max(s, axis=-1, keepdims=True), sink)
    e = jnp.exp(s - m)
    es = jnp.exp(sink - m)
    inv = 1.0 / (jnp.sum(e, axis=-1, keepdims=True) + es)
    return e * inv, es * inv


def _attn_specs(c):
    dq, dkv2 = c.DQ, 2 * c.DKV
    return [pl.BlockSpec((BLOCK, dq), lambda n: (n, 0)),
            pl.BlockSpec((BLOCK, dkv2), lambda n: (n, 0)),
            pl.BlockSpec((BLOCK, dkv2), lambda n: (jnp.maximum(n - 1, 0), 0))]


def _bias_spec():
    return pl.BlockSpec((None, KV_PAIRS, PAIR_ROWS, 2 * BLOCK), lambda n: (jnp.minimum(n, 1), 0, 0, 0))


def _low_lanes():
    return lax.broadcasted_iota(jnp.int32, (BLOCK, LANES), 1) < HEAD_DIM


def _attn_fwd(c, name, q, kv, bias, sinks, dep=None):
    S, dq = c.S, c.DQ

    def body(q_ref, kvc_ref, kvp_ref, bias_ref, sink_ref, o_ref):
        low = _low_lanes()
        for kvp in range(KV_PAIRS):
            k2, v2 = _pair_kv(kvc_ref, kvp_ref, kvp, c.DKV)
            for g in range(Q_PER_KV):
                grp = kvp * Q_PER_KV + g
                halves = []
                for par in range(2):
                    r = 2 * g + par
                    qh = _head_operand(q_ref, grp, par, low, scale=HEAD_DIM ** -0.5)
                    p, _ = _head_probs(qh, k2, bias_ref[kvp, r * BLOCK:(r + 1) * BLOCK, :],
                                       sink_ref[(2 * kvp + par) * Q_PER_KV + g])
                    halves.append(_dot(p.astype(BF16), v2, NN))
                o_ref[:, grp * LANES:(grp + 1) * LANES] = jnp.where(low, halves[0], halves[1]).astype(BF16)

    return _pcall(
        body, name=name, ins=[q, kv, kv, bias, sinks], dep=dep, grid=(c.nb,),
        in_specs=_attn_specs(c) + [_bias_spec(), SMEM],
        out_specs=pl.BlockSpec((BLOCK, dq), lambda n: (n, 0)),
        out_shape=jax.ShapeDtypeStruct((S, dq), BF16), compiler_params=_params(1))


def _final_loss(c, h, g, target, dep=None):
    S, D, tm = c.S, c.D, c.tm

    def body(h_ref, g_ref, t_ref, dh_ref, dg_ref, loss_ref):
        i = pl.program_id(0)
        h = h_ref[...]
        gg = g_ref[...]
        r = _rstd(h)
        xh = h * r
        err = xh * gg - t_ref[...]
        lp = jnp.sum(jnp.sum(err * err, axis=1, keepdims=True), axis=0, keepdims=True) * (0.5 / D)
        dx, dg = _rms_bwd(err * (1.0 / D), h, gg, 0.0)
        dh_ref[...] = dx

        @pl.when(i == 0)
        def _():
            dg_ref[...] = dg
            loss_ref[...] = jnp.broadcast_to(lp, loss_ref.shape)

        @pl.when(i > 0)
        def _():
            dg_ref[...] += dg
            loss_ref[...] += jnp.broadcast_to(lp, loss_ref.shape)

    row = pl.BlockSpec((tm, D), lambda i: (i, 0))
    return _pcall(
        body, name="final_loss", ins=[h, g, target], dep=dep, grid=(S // tm,),
        in_specs=[row, pl.BlockSpec((1, D), lambda i: (0, 0)), row],
        out_specs=[row, pl.BlockSpec((1, D), lambda i: (0, 0)), pl.BlockSpec((1, LANES), lambda i: (0, 0))],
        out_shape=[jax.ShapeDtypeStruct((S, D), F32), jax.ShapeDtypeStruct((1, D), F32),
                   jax.ShapeDtypeStruct((1, LANES), F32)],
        compiler_params=_params(1))


def _rms_bwd_epilogue(h_idx, g_idx, res_idx):
    def epilogue(dhn, in_refs, out_refs):
        dh, dg = _rms_bwd(dhn, in_refs[h_idx][...], in_refs[g_idx][...], in_refs[res_idx][...])
        out_refs[0][...] = dh
        i = pl.program_id(0)

        @pl.when(i == 0)
        def _():
            out_refs[1][...] = dg

        @pl.when(i > 0)
        def _():
            out_refs[1][...] += dg
    return epilogue


def _stream_outs(c, tm):
    S, D = c.S, c.D
    return ([jax.ShapeDtypeStruct((S, D), F32), jax.ShapeDtypeStruct((1, D), F32)],
            [pl.BlockSpec((tm, D), lambda i, j, k: (i, 0)), pl.BlockSpec((1, D), lambda i, j, k: (0, 0))])


def _row_specs(c, tm):
    D = c.D
    return [pl.BlockSpec((tm, D), lambda i, j, k: (i, 0)), pl.BlockSpec((1, D), lambda i, j, k: (0, 0)),
            pl.BlockSpec((tm, D), lambda i, j, k: (i, 0))]


def _bwd_rows_to_stream(c, name, dy_list, slab, blk_rows, blk_idx, n_in_cols, h, g, dres, dep=None):
    S, D, tm = c.S, c.D, c.tm
    nd = len(dy_list)

    def load_a(in_refs, out_refs):
        a = in_refs[0][...]
        for r in in_refs[1:nd]:
            a = a + r[...]
        return a.astype(BF16)

    shapes, specs = _stream_outs(c, tm)
    return _mm(
        name, [*dy_list, slab, h, g, dres],
        [pl.BlockSpec((tm, n_in_cols), lambda i, j, k: (i, 0))] * nd
        + [pl.BlockSpec((NDEV, blk_rows, n_in_cols), lambda i, j, k: (0, blk_idx, 0))] + _row_specs(c, tm),
        shapes, specs, grid=(S // tm, 1, 1), dims=NT, nk=1, acc_shape=None,
        load_a=load_a, load_b=_b_view(nd, NDEV * blk_rows), epilogue=_rms_bwd_epilogue(nd + 1, nd + 2, nd + 3),
        dep=dep)


def _bwd_cols_to_stream(c, name, dy, col, ci, h, g, dres, dep=None):
    S, D, cw, tm = c.S, c.D, c.cw, c.tmw
    K = NDEV * cw
    shapes, specs = _stream_outs(c, tm)
    return _mm(
        name, [dy, col, h, g, dres],
        [pl.BlockSpec((tm, K), lambda i, j, k: (i, 0)),
         _const((NDEV, None, cw, D), lambda i, j, k: (0, ci, 0, 0))] + _row_specs(c, tm),
        shapes, specs, grid=(S // tm, 1, 1), dims=NN, nk=1, acc_shape=None,
        load_a=_bf(0), load_b=_b_view(1, K), epilogue=_rms_bwd_epilogue(2, 3, 4), dep=dep)


def _bwd_rows_data(c, name, dy, slab, blk_rows, blk_idx, dep=None):
    S, D, tm = c.S, c.D, c.tm
    K = NDEV * blk_rows

    def epilogue(acc, in_refs, out_refs):
        out_refs[0][...] = acc.astype(BF16)

    return _mm(
        name, [dy, slab],
        [pl.BlockSpec((tm, D), lambda i, j, k: (i, 0)),
         pl.BlockSpec((NDEV, blk_rows, D), lambda i, j, k: (0, blk_idx, 0))],
        [jax.ShapeDtypeStruct((S, K), BF16)], [pl.BlockSpec((tm, K), lambda i, j, k: (i, 0))],
        grid=(S // tm, 1, 1), dims=NT, nk=1, acc_shape=None,
        load_a=_bf(0), load_b=_b_view(1, K), epilogue=epilogue, dep=dep)[0]


def _wgrad_rows(c, name, a, b_list, n_a, n_b, dep=None):
    S, tm = c.S, c.tm
    nb_in = len(b_list)
    blk_rows = n_a // NDEV

    def load_b(in_refs):
        b = in_refs[1][...]
        for r in in_refs[2:1 + nb_in]:
            b = b + r[...]
        return b.astype(BF16)

    def epilogue(acc, in_refs, out_refs):
        out_refs[0][...] = acc.reshape(NDEV, blk_rows, n_b).astype(BF16)

    return _mm(
        name, [a, *b_list],
        [pl.BlockSpec((tm, n_a), lambda i, j, k: (k, 0))] + [pl.BlockSpec((tm, n_b), lambda i, j, k: (k, 0))] * nb_in,
        [jax.ShapeDtypeStruct((NDEV, blk_rows, n_b), BF16)],
        [pl.BlockSpec((NDEV, blk_rows, n_b), lambda i, j, k: (0, 0, 0))],
        grid=(1, 1, S // tm), dims=TN, nk=S // tm, acc_shape=(n_a, n_b),
        load_a=_bf(0), load_b=load_b, epilogue=epilogue, dep=dep)[0]


def _wgrad_cols(c, name, a, b, dep=None):
    S, D, cw = c.S, c.D, c.cw

    def body(a_ref, b_ref, out_ref):
        out_ref[...] = _dot(a_ref[...], b_ref[...], TN).astype(BF16)

    return _pcall(
        body, name=name, ins=[a, b], dep=dep, grid=(NDEV,),
        in_specs=[_const((S, D), lambda j: (0, 0)), pl.BlockSpec((S, cw), lambda j: (0, j))],
        out_specs=pl.BlockSpec((None, D, cw), lambda j: (j, 0, 0)),
        out_shape=jax.ShapeDtypeStruct((NDEV, D, cw), BF16), compiler_params=_params(1))


def _ffn_bwd_data(c, name, dh, p, col, ci, rows, h, g, dep=None):
    S, D, fw, tm = c.S, c.D, c.fw, c.tmw
    F = NDEV * fw

    def body(dh_ref, p_ref, w1t_ref, w2_ref, h_ref, g_ref, da_ref, out_ref, dg_ref, dhb_ref):
        i = pl.program_id(0)
        dh = dh_ref[...]
        dhb = dh.astype(BF16)
        dhb_ref[...] = dhb
        for j in range(NDEV):
            cols = slice(j * fw, (j + 1) * fw)
            da_ref[:, cols] = (_dot(dhb, w2_ref[j], NT) * (2.0 * p_ref[:, cols].astype(F32))).astype(BF16)
        dx, dg = _rms_bwd(_dot(da_ref[...], w1t_ref[...].reshape(F, D), NN), h_ref[...], g_ref[...], dh)
        out_ref[...] = dx

        @pl.when(i == 0)
        def _():
            dg_ref[...] = dg

        @pl.when(i > 0)
        def _():
            dg_ref[...] += dg

    row = pl.BlockSpec((tm, D), lambda i: (i, 0))
    wide = pl.BlockSpec((tm, F), lambda i: (i, 0))
    return _pcall(
        body, name=name, ins=[dh, p, col, rows, h, g], dep=dep, grid=(S // tm,),
        in_specs=[row, wide, _const((NDEV, None, fw, D), lambda i: (0, ci, 0, 0)),
                  _const((NDEV, c.fr, D), lambda i: (0, 0, 0)),
                  row, pl.BlockSpec((1, D), lambda i: (0, 0))],
        out_specs=[wide, row, pl.BlockSpec((1, D), lambda i: (0, 0)), row],
        out_shape=[jax.ShapeDtypeStruct((S, F), BF16), jax.ShapeDtypeStruct((S, D), F32),
                   jax.ShapeDtypeStruct((1, D), F32), jax.ShapeDtypeStruct((S, D), BF16)],
        compiler_params=_params(1))


def _ffn_bwd_w(c, name, hn, da, p, dhb, dep=None):
    S, D, fw = c.S, c.D, c.fw

    def body(hn_ref, da_ref, p_ref, dhb_ref, dw1_ref, dw2_ref):
        dw1_ref[...] = _dot(hn_ref[...], da_ref[...], TN).astype(BF16)
        pf = p_ref[...].astype(F32)
        dw2_ref[...] = _dot((pf * pf).astype(BF16), dhb_ref[...], TN).astype(BF16)

    panel = pl.BlockSpec((S, fw), lambda j: (0, j))
    return _pcall(
        body, name=name, ins=[hn, da, p, dhb], dep=dep, grid=(NDEV,),
        in_specs=[_const((S, D), lambda j: (0, 0)), panel, panel, _const((S, D), lambda j: (0, 0))],
        out_specs=[pl.BlockSpec((None, D, fw), lambda j: (j, 0, 0)), pl.BlockSpec((None, c.fr, D), lambda j: (j, 0, 0))],
        out_shape=[jax.ShapeDtypeStruct((NDEV, D, fw), BF16), jax.ShapeDtypeStruct((NDEV, c.fr, D), BF16)],
        compiler_params=_params(1))


def _attn_bwd(c, name, q, kv, do, bias, sinks, dep=None):
    S, dq, dkv = c.S, c.DQ, c.DKV
    nb = c.nb
    scale = HEAD_DIM ** -0.5

    def body(q_ref, kvc_ref, kvp_ref, do_ref, bias_ref, sink_ref, dq_ref, dkv_ref, dbias_ref, dsink_ref, dsink_acc,
             ds_sc, p_sc, qm_sc, dom_sc):
        n = pl.program_id(0)

        @pl.when(n == 0)
        def _():
            dkv_ref[...] = jnp.zeros_like(dkv_ref)
            dbias_ref[...] = jnp.zeros_like(dbias_ref)
            dsink_acc[...] = jnp.zeros_like(dsink_acc)

        low = _low_lanes()
        rows_c = pl.ds(pl.multiple_of(n * BLOCK, BLOCK), BLOCK)
        rows_p = pl.ds(pl.multiple_of(jnp.maximum(n - 1, 0) * BLOCK, BLOCK), BLOCK)
        for kvp in range(KV_PAIRS):
            k2, v2 = _pair_kv(kvc_ref, kvp_ref, kvp, dkv)
            for g in range(Q_PER_KV):
                grp = kvp * Q_PER_KV + g
                halves = []
                for par in range(2):
                    rows = slice((2 * g + par) * BLOCK, (2 * g + par + 1) * BLOCK)
                    qh = _head_operand(q_ref, grp, par, low, scale=scale)
                    doh = _head_operand(do_ref, grp, par, low)
                    p, ps = _head_probs(qh, k2, bias_ref[kvp, rows, :], sink_ref[(2 * kvp + par) * Q_PER_KV + g])
                    dp = _dot(doh, v2, NT)
                    delta = jnp.sum(p * dp, axis=-1, keepdims=True)
                    ds = p * (dp - delta)
                    dbias_ref[kvp, rows, :] += ds
                    dsink_acc[rows, kvp:kvp + 1] += -(ps * delta)
                    ds16 = ds.astype(BF16)
                    halves.append(_dot(ds16, k2, NN) * scale)
                    ds_sc[rows, :] = ds16
                    p_sc[rows, :] = p.astype(BF16)
                    qm_sc[rows, :] = qh
                    dom_sc[rows, :] = doh
                dq_ref[:, grp * LANES:(grp + 1) * LANES] = jnp.where(low, halves[0], halves[1]).astype(BF16)
            dk2 = _dot(ds_sc[...], qm_sc[...], TN)
            dv2 = _dot(p_sc[...], dom_sc[...], TN)
            lanes = slice(kvp * LANES, (kvp + 1) * LANES)
            vlanes = slice(dkv + kvp * LANES, dkv + (kvp + 1) * LANES)
            dkv_ref[rows_p, lanes] += dk2[:BLOCK]
            dkv_ref[rows_c, lanes] += dk2[BLOCK:]
            dkv_ref[rows_p, vlanes] += dv2[:BLOCK]
            dkv_ref[rows_c, vlanes] += dv2[BLOCK:]

        @pl.when(n == nb - 1)
        def _():
            dsink_ref[...] = jnp.sum(dsink_acc[...].reshape(2 * Q_PER_KV, BLOCK, KV_PAIRS), axis=1)

    return _pcall(
        body, name=name, ins=[q, kv, kv, do, bias, sinks], dep=dep, grid=(nb,),
        in_specs=_attn_specs(c) + [pl.BlockSpec((BLOCK, dq), lambda n: (n, 0)), _bias_spec(), SMEM],
        out_specs=[pl.BlockSpec((BLOCK, dq), lambda n: (n, 0)), pl.BlockSpec((S, 2 * dkv), lambda n: (0, 0)),
                   pl.BlockSpec((KV_PAIRS, PAIR_ROWS, 2 * BLOCK), lambda n: (0, 0, 0)),
                   pl.BlockSpec((2 * Q_PER_KV, KV_PAIRS), lambda n: (0, 0))],
        out_shape=[jax.ShapeDtypeStruct((S, dq), BF16), jax.ShapeDtypeStruct((S, 2 * dkv), F32),
                   jax.ShapeDtypeStruct((KV_PAIRS, PAIR_ROWS, 2 * BLOCK), F32),
                   jax.ShapeDtypeStruct((2 * Q_PER_KV, KV_PAIRS), F32)],
        scratch_shapes=[pltpu.VMEM((PAIR_ROWS, KV_PAIRS), F32), pltpu.VMEM((PAIR_ROWS, 2 * BLOCK), BF16),
                        pltpu.VMEM((PAIR_ROWS, 2 * BLOCK), BF16), pltpu.VMEM((PAIR_ROWS, LANES), BF16),
                        pltpu.VMEM((PAIR_ROWS, LANES), BF16)],
        compiler_params=_params(1))


def _sgu_bwd(c, name, a, z, dgated, ln_g, wc, wc_t, b_t, dep=None):
    S, AW, gd, tm = c.S, c.AW, c.gd, c.tms

    def body(a_ref, z_ref, dg_ref, lng_ref, wc_ref, wct_ref, bt_ref, dz_ref, dws_ref, dbt_ref, dlng_ref, dvn_ref):
        i = pl.program_id(0)

        @pl.when(i == 0)
        def _():
            dws_ref[...] = jnp.zeros_like(dws_ref)
            dbt_ref[...] = jnp.zeros_like(dbt_ref)
            dlng_ref[...] = jnp.zeros_like(dlng_ref)

        lng = lng_ref[...]
        va = a_ref[:, AW:].astype(F32)
        xc = va - jnp.mean(va, axis=-1, keepdims=True)
        rstd = lax.rsqrt(jnp.mean(xc * xc, axis=-1, keepdims=True) + EPS)
        xh = xc * rstd
        vn = (xh * lng).astype(BF16)
        causal = _sgu_masks()
        for ch in range(tm // CHUNK):
            rows = slice(ch * CHUNK, (ch + 1) * CHUNK)
            for g in range(A_GROUPS):
                cols = slice(g * gd, (g + 1) * gd)
                blk = vn[rows, cols]
                mixed = _dot(wc_ref[g], blk, NN) + bt_ref[:, g:g + 1]
                dgb = dg_ref[rows, cols].astype(F32)
                dm = dgb * a_ref[rows, cols].astype(F32)
                dbt_ref[:, g:g + 1] += jnp.sum(dm, axis=1, keepdims=True)
                dm16 = dm.astype(BF16)
                dws_ref[g] += jnp.where(causal, _dot(dm16, blk, NT), 0.0)
                dvn_ref[rows, cols] = _dot(wct_ref[g], dm16, NN)
                dz_ref[rows, cols] = (dgb * mixed * _gelu_grad(z_ref[rows, cols].astype(F32))).astype(BF16)
        dvn = dvn_ref[...]
        dlng_ref[...] += jnp.sum(dvn * xh, axis=0, keepdims=True)
        dxh = dvn * lng
        dva = rstd * (dxh - jnp.mean(dxh, axis=-1, keepdims=True) - xh * jnp.mean(dxh * xh, axis=-1, keepdims=True))
        dz_ref[:, AW:] = (dva * _gelu_grad(z_ref[:, AW:].astype(F32))).astype(BF16)

    wide = pl.BlockSpec((tm, 2 * AW), lambda i: (i, 0))
    wsp = pl.BlockSpec((A_GROUPS, CHUNK, CHUNK), lambda i: (0, 0, 0))
    btsp = pl.BlockSpec((CHUNK, A_GROUPS), lambda i: (0, 0))
    return _pcall(
        body, name=name, ins=[a, z, dgated, ln_g, wc, wc_t, b_t], dep=dep, grid=(S // tm,),
        in_specs=[wide, wide, pl.BlockSpec((tm, AW), lambda i: (i, 0)), pl.BlockSpec((1, AW), lambda i: (0, 0)),
                  wsp, wsp, btsp],
        out_specs=[wide, wsp, btsp, pl.BlockSpec((1, AW), lambda i: (0, 0))],
        out_shape=[jax.ShapeDtypeStruct((S, 2 * AW), BF16), jax.ShapeDtypeStruct((A_GROUPS, CHUNK, CHUNK), F32),
                   jax.ShapeDtypeStruct((CHUNK, A_GROUPS), F32), jax.ShapeDtypeStruct((1, AW), F32)],
        scratch_shapes=[pltpu.VMEM((tm, AW), F32)], compiler_params=_params(1))


def _adamw(name, parts, part_block, part_index, w, m, v, tr, row_off=0, n_rows=None, prev=None, dep=None):
    R, C = w.shape
    n_rows = R if n_rows is None else n_rows
    assert n_rows % tr == 0 and row_off % tr == 0
    bc1 = 1.0 - ADAM_B1 ** ADAM_STEP
    bc2 = 1.0 - ADAM_B2 ** ADAM_STEP

    def body(p_ref, w_ref, m_ref, v_ref, *rest):
        g_ref, d_ref, nm_ref, nv_ref = rest[-4:]
        g = p_ref[0].astype(F32)
        for s in range(1, part_block[0]):
            g = g + p_ref[s].astype(F32)
        nm = ADAM_B1 * m_ref[...] + (1.0 - ADAM_B1) * g
        nv = ADAM_B2 * v_ref[...] + (1.0 - ADAM_B2) * (g * g)
        g_ref[...] = g
        nm_ref[...] = nm
        nv_ref[...] = nv
        d_ref[...] = -ADAM_LR * ((nm * (1.0 / bc1)) / (jnp.sqrt(nv * (1.0 / bc2)) + ADAM_EPS) + ADAM_WD * w_ref[...])

    ob = row_off // tr
    row = pl.BlockSpec((tr, C), lambda i: (ob + i, 0))
    out = jax.ShapeDtypeStruct((R, C), F32)
    chained = prev is not None
    return _pcall(
        body, name=name, ins=[parts, w, m, v] + (list(prev) if chained else []), dep=dep, grid=(n_rows // tr,),
        in_specs=[pl.BlockSpec(part_block, part_index), row, row, row] + ([ANY] * 4 if chained else []),
        out_specs=[row, row, row, row], out_shape=[out, out, out, out],
        input_output_aliases={4 + t: t for t in range(4)} if chained else {}, compiler_params=_params(1))


def _sum_parts(name, parts, dep=None):
    def body(p_ref, out_ref):
        g = p_ref[0]
        for s in range(1, parts.shape[0]):
            g = g + p_ref[s]
        out_ref[...] = g

    return _pcall(body, name=name, ins=[parts], in_specs=[_whole(parts)], dep=dep,
                  out_shape=jax.ShapeDtypeStruct(parts.shape[1:], F32), compiler_params=_params(0))


def _place():
    return lax.axis_index("x"), lax.axis_index("y"), lax.axis_index("c")


def _slot(px, py, pc):
    return 4 * px + 2 * py + pc


def _peer(k, x, y, c):
    return x ^ ((k >> 2) & 1), y ^ ((k >> 1) & 1), c ^ (k & 1)


SEND_PEERS = {"exchange": tuple(range(1, NDEV)), "gather": (1, 2, 4, 6), "forward": (2, 4, 6),
              "broadcast": tuple(range(1, NDEV))}


def _n_sems(mode, n_lands):
    return n_lands * (len(SEND_PEERS[mode]) + (mode != "forward"))


def _send_copies(mode, src_refs, land_refs, send_sems, recv_sems):
    x, y, c = _place()
    me = _slot(x, y, c)
    peers = SEND_PEERS[mode]
    remote, local = [], []
    for i, k in enumerate(peers):
        peer = _peer(k, x, y, c)
        for a, land in enumerate(land_refs):
            if mode == "exchange":
                src, dst, to = src_refs[a].at[_slot(*peer)], land.at[me], peer
            elif mode in ("gather", "broadcast"):
                src, dst, to = src_refs[a], land.at[me], peer
            else:
                src = dst = land.at[_slot(*peer)]
                to = (x, y, 1 - c)
            s = a * len(peers) + i
            remote.append(pltpu.make_async_remote_copy(src_ref=src, dst_ref=dst, send_sem=send_sems.at[s],
                                                       recv_sem=recv_sems.at[s], device_id=to, device_id_type=MESH))
    if mode != "forward":
        for a, land in enumerate(land_refs):
            src = src_refs[a].at[me] if mode == "exchange" else src_refs[a]
            local.append(pltpu.make_async_copy(src, land.at[me], send_sems.at[len(land_refs) * len(peers) + a]))
    return remote, local


def _send_start_groups(name, groups, mode, dep=None):
    sizes = [(len(s), len(l)) for s, l in groups]
    flat = [t for s, l in groups for t in (*s, *l)]
    n_in, ng = len(flat), len(groups)

    def body(*refs):
        sems, token, at = refs[n_in:n_in + 2 * ng], refs[-1], 0
        for gi, (ns, nl) in enumerate(sizes):
            remote, local = _send_copies(mode, refs[at:at + ns], refs[at + ns:at + ns + nl], sems[2 * gi],
                                         sems[2 * gi + 1])
            for cp in remote + local:
                cp.start()
            at += ns + nl
        token[...] = jnp.zeros_like(token)

    sem_shapes = [pltpu.SemaphoreType.DMA((_n_sems(mode, nl),)) for _, nl in sizes for _ in range(2)]
    if any(dep is t for t in flat):
        dep = None
    out = _pcall(
        body, name=name, ins=[pltpu.with_memory_space_constraint(t, pltpu.HBM) for t in flat],
        in_specs=[HBM] * n_in, dep=dep,
        out_shape=(*sem_shapes, *[pltpu.HBM(t.shape, t.dtype) for t in flat], jax.ShapeDtypeStruct((8, LANES), F32)),
        out_specs=(*[SEM] * (2 * ng), *[HBM] * n_in, pl.BlockSpec(memory_space=pltpu.VMEM)),
        input_output_aliases={i: 2 * ng + i for i in range(n_in)},
        compiler_params=pltpu.CompilerParams(has_side_effects=pltpu.SideEffectType.DATAFLOW_SIDE_EFFECTING))
    started, at = [], 2 * ng
    for gi, (ns, nl) in enumerate(sizes):
        started.append((out[-1], out[2 * gi], out[2 * gi + 1], list(out[at:at + ns]), list(out[at + ns:at + ns + nl])))
        at += ns + nl
    return started


def _send_start(name, srcs, lands, mode, dep=None):
    return _send_start_groups(name, [(srcs, lands)], mode, dep=dep)[0]


def _send_wait(name, started, mode, dep=None):
    _, send_sems, recv_sems, srcs, lands = started
    n_src, n = len(srcs), len(lands)

    def body(*refs):
        src_refs, land_refs = refs[:n_src], refs[n_src:n_src + n]
        ssem, rsem = refs[n_src + n], refs[n_src + n + 1]
        remote, local = _send_copies(mode, src_refs, land_refs, ssem, rsem)
        for cp in remote:
            cp.wait_send()
            cp.wait_recv()
        for cp in local:
            cp.wait()

    thru = [pltpu.HBM(t.shape, t.dtype) for t in [*srcs, *lands]]
    out = _pcall(
        body, name=name, ins=[*srcs, *lands, send_sems, recv_sems], in_specs=[HBM] * (n_src + n) + [SEM, SEM], dep=dep,
        out_shape=tuple(thru), out_specs=tuple([HBM] * (n_src + n)),
        input_output_aliases={i: i for i in range(n_src + n)},
        compiler_params=pltpu.CompilerParams(has_side_effects=pltpu.SideEffectType.DATAFLOW_SIDE_EFFECTING))
    return list(out[n_src:])


def _landing(block):
    return lax.empty((NDEV, *block.shape), block.dtype)


def _rows128(t):
    flat = t.reshape(-1)
    n = flat.shape[0]
    rows = -(-n // (8 * LANES)) * 8
    return jnp.pad(flat, (0, rows * LANES - n)).reshape(rows, LANES)


def kernel(x, mix_norm_g, ffn_norm_g, a_w_in, a_ln_g, a_w_spatial, a_b_spatial, a_w_out, kv_norm_g, w_k, w_v, b_w_q, b_sinks, b_w_o, rel_bias, ffn_w1, ffn_w2, final_norm_g, loss_target, m_mix_norm_g, m_ffn_norm_g, m_a_w_in, m_a_ln_g, m_a_w_spatial, m_a_b_spatial, m_a_w_out, m_kv_norm_g, m_w_k, m_w_v, m_b_w_q, m_b_sinks, m_b_w_o, m_rel_bias, m_ffn_w1, m_ffn_w2, m_final_norm_g, v_mix_norm_g, v_ffn_norm_g, v_a_w_in, v_a_ln_g, v_a_w_spatial, v_a_b_spatial, v_a_w_out, v_kv_norm_g, v_w_k, v_w_v, v_b_w_q, v_b_sinks, v_b_w_o, v_rel_bias, v_ffn_w1, v_ffn_w2, v_final_norm_g):
    c = _config(x, a_w_in, a_w_out, w_k, b_w_q, b_w_o, ffn_w1, ffn_w2)
    S, D, LA, LB, LF = c.S, c.D, c.LA, c.LB, c.LF
    weights = dict(mix_norm_g=mix_norm_g, ffn_norm_g=ffn_norm_g, a_w_in=a_w_in, a_ln_g=a_ln_g, a_w_spatial=a_w_spatial,
                   a_b_spatial=a_b_spatial, a_w_out=a_w_out, kv_norm_g=kv_norm_g, w_k=w_k, w_v=w_v, b_w_q=b_w_q,
                   b_sinks=b_sinks, b_w_o=b_w_o, rel_bias=rel_bias, ffn_w1=ffn_w1, ffn_w2=ffn_w2,
                   final_norm_g=final_norm_g)
    m_in = dict(mix_norm_g=m_mix_norm_g, ffn_norm_g=m_ffn_norm_g, a_w_in=m_a_w_in, a_ln_g=m_a_ln_g,
                a_w_spatial=m_a_w_spatial, a_b_spatial=m_a_b_spatial, a_w_out=m_a_w_out, kv_norm_g=m_kv_norm_g,
                w_k=m_w_k, w_v=m_w_v, b_w_q=m_b_w_q, b_sinks=m_b_sinks, b_w_o=m_b_w_o, rel_bias=m_rel_bias,
                ffn_w1=m_ffn_w1, ffn_w2=m_ffn_w2, final_norm_g=m_final_norm_g)
    v_in = dict(mix_norm_g=v_mix_norm_g, ffn_norm_g=v_ffn_norm_g, a_w_in=v_a_w_in, a_ln_g=v_a_ln_g,
                a_w_spatial=v_a_w_spatial, a_b_spatial=v_a_b_spatial, a_w_out=v_a_w_out, kv_norm_g=v_kv_norm_g,
                w_k=v_w_k, w_v=v_w_v, b_w_q=v_b_w_q, b_sinks=v_b_sinks, b_w_o=v_b_w_o, rel_bias=v_rel_bias,
                ffn_w1=v_ffn_w1, ffn_w2=v_ffn_w2, final_norm_g=v_final_norm_g)
    names = list(weights)
    seq = _Seq()
    me = _slot(*_place())
    bf = lambda t: t.astype(BF16)

    tr = lambda t: bf(jnp.swapaxes(t, -1, -2))
    groups = []
    for l in range(LA):
        groups += [[tr(a_w_in[l])[None]] + ([a_ln_g] if l == 0 else []), [bf(a_w_out[l])],
                   [tr(ffn_w1[l])[None], bf(ffn_w2[l])]]
    gb = 3 * LA
    for l in range(LB):
        extra = [bf(jnp.concatenate([w_k, w_v], axis=1))] if l == 0 else []
        groups += [extra + [bf(b_w_q[l]), bf(b_w_o[l])], [tr(ffn_w1[LA + l])[None], bf(ffn_w2[LA + l])]]
    started = seq(_send_start_groups, "weights_start", [(grp, [_landing(t) for t in grp]) for grp in groups],
                  "gather")
    seq.last = started[0][0]
    forwarding = {}

    def forward(i):
        lands = seq(_send_wait, f"weights_wait{i}", started[i], "gather")
        forwarding[i] = seq(_send_start, f"weights_forward{i}", [], lands, "forward")

    def arrive(i):
        if i not in forwarding:
            forward(i)
        return seq(_send_wait, f"weights_arrive{i}", forwarding[i], "forward")

    causal = jnp.tril(jnp.ones((CHUNK, CHUNK), bool))
    wsp = jnp.where(causal[None, None], a_w_spatial, 0.0)
    wsp16 = wsp.astype(BF16)
    wsp16_t = jnp.swapaxes(wsp, -1, -2).astype(BF16)
    bsp_t = jnp.swapaxes(a_b_spatial, -1, -2)
    mix_g = mix_norm_g.reshape(-1, 1, D)
    ffn_g = ffn_norm_g.reshape(-1, 1, D)
    kv_g = kv_norm_g.reshape(1, D)
    fin_g = final_norm_g.reshape(1, D)
    onehot = _bucket_onehot()
    bias = _slot_bias(seq(_band_bias, rel_bias.T, onehot).reshape(N_HEADS, BLOCK, 2 * BLOCK))

    h = x.reshape(S, D)
    sav_a, sav_b, wts_a, wts_b = [], [], [], []
    for l in range(LA):
        got = arrive(3 * l)
        w_in = got[0]
        if l == 0:
            ln_g_full = jnp.transpose(got[1], (1, 0, 2)).reshape(LA, 1, c.AW)
        z, a, hn = seq(_a_in_fwd, c, f"a_in_fwd{l}", h, mix_g[l], w_in, 0)
        forward(3 * l + 1)
        gated = seq(_sgu_fwd, c, f"sgu_fwd{l}", a, ln_g_full[l], wsp16[l], bsp_t[l])
        (wout,) = arrive(3 * l + 1)
        if l > 0:
            forward(3 * l + 2)
        h1 = seq(_mm_res, c, f"a_out_fwd{l}", gated, wout, c.ar, 0, h)
        w1, rows = arrive(3 * l + 2)
        if l == LA - 1:
            forward(gb)
        p, h2, hnf = seq(_ffn_fwd, c, f"ffn_fwd{l}", h1, ffn_g[l], w1, 0, rows)
        sav_a.append((h, z, a, hn, gated, h1, p, hnf))
        wts_a.append((w_in, 0, w1, 0, rows, wout, 0))
        h = h2
    h_kv = h
    for l in range(LB):
        got = arrive(gb + 2 * l)
        if l == 0:
            wkv, got = got[0], got[1:]
            kv, hkv = seq(_rms_mm_rows, c, "kv_fwd", h, kv_g, wkv, c.kr, 0, 2 * c.DKV)
        wq, wo = _slot_cols(got[0]), _slot_rows(got[1])
        q, hn = seq(_rms_mm_rows, c, f"q_fwd{l}", h, mix_g[LA + l], wq, c.qr, 0, c.DQ)
        forward(gb + 2 * l + 1)
        o = seq(_attn_fwd, c, f"attn_fwd{l}", q, kv, bias, b_sinks[l])
        h1 = seq(_mm_res, c, f"o_fwd{l}", o, wo, c.orr, 0, h)
        w1, rows = arrive(gb + 2 * l + 1)
        if l + 1 < LB:
            forward(gb + 2 * l + 2)
        p, h2, hnf = seq(_ffn_fwd, c, f"ffn_fwd{LA + l}", h1, ffn_g[LA + l], w1, 0, rows)
        sav_b.append((h, q, hn, o, h1, p, hnf))
        wts_b.append((wq, wo, w1, rows))
        h = h2
    dh, d_fin_g, loss_row = seq(_final_loss, c, h, fin_g, loss_target.reshape(S, D))

    results = {}
    in_flight = []

    def update(k, parts, layer, col_blk=0):
        w = weights[k]
        rows_l, ncols = (w.shape[-2], w.shape[-1]) if w.ndim == 3 else w.shape
        flat = lambda t: t.reshape(-1, ncols)
        tr = min(256, rows_l)
        results[k] = seq(_adamw, f"adamw_{k}{layer}", parts, (NDEV, tr, ncols), lambda i: (0, i, col_blk),
                         flat(w), flat(m_in[k]), flat(v_in[k]), tr, row_off=layer * rows_l, n_rows=rows_l,
                         prev=results.get(k))

    def land(tag, entry):
        lands = seq(_send_wait, f"grads_wait_{tag}", entry[1], "exchange")
        for keys, parts in zip(entry[0], lands):
            for k, layer, col_blk in keys:
                update(k, parts, layer, col_blk)

    def send(tag, items):
        slabs = [t for _, t in items]
        own = [lax.empty(t.shape, t.dtype) for t in slabs]
        st = seq(_send_start, f"grads_start_{tag}", slabs, own, "exchange")
        in_flight.append((tag, ([k for k, _ in items], st)))
        while len(in_flight) > EXCHANGE_LAG:
            land(*in_flight.pop(0))

    d_mix_g, d_ffn_g = [None] * LF, [None] * LF
    dkv_list, dbias_list, dsink_list = [], [], [None] * LB

    def ffn_bwd(lf, dh, h1, p, hnf, w1, w1_i, rows):
        da, dh1, d_ffn_g[lf], dhb = seq(_ffn_bwd_data, c, f"ffn_bwd_data{lf}", dh, p, w1, w1_i, rows, h1, ffn_g[lf])
        dw1, dw2 = seq(_ffn_bwd_w, c, f"ffn_bwd_w{lf}", hnf, da, p, dhb)
        send(f"ffn{lf}", [([("ffn_w1", lf, 0)], dw1), ([("ffn_w2", lf, 0)], dw2)])
        return dh1

    for l in reversed(range(LB)):
        h0, q, hn, o, h1, p, hnf = sav_b[l]
        wq, wo, w1, rows = wts_b[l]
        dh1 = ffn_bwd(LA + l, dh, h1, p, hnf, w1, 0, rows)
        do = seq(_bwd_rows_data, c, f"o_bwd_data{l}", dh1, wo, c.orr, 0)
        dwo = _unslot_rows(seq(_wgrad_rows, c, f"o_bwd_w{l}", o, [dh1], c.DQ, D))
        dq, dkv, dbias, dsink = seq(_attn_bwd, c, f"attn_bwd{l}", q, kv, do, bias, b_sinks[l])
        dsink_list[l] = dsink.reshape(Q_PER_KV, 2, KV_PAIRS).transpose(2, 1, 0).reshape(1, N_HEADS)
        dkv_list.append(dkv)
        dbias_list.append(_unslot_bias(dbias))
        dwq = _unslot_cols(seq(_wgrad_rows, c, f"q_bwd_w{l}", hn, [dq], D, c.DQ))
        send(f"attn{l}", [([("b_w_o", l, 0)], dwo), ([("b_w_q", l, 0)], dwq)])
        dh, d_mix_g[LA + l] = seq(_bwd_rows_to_stream, c, f"q_bwd_data{l}", [dq], wq, c.qr, 0, c.DQ, h0,
                                  mix_g[LA + l], dh1)
    dwkv = seq(_wgrad_rows, c, "kv_bwd_w", hkv, dkv_list, D, 2 * c.DKV)
    send("kv", [([("w_k", 0, 0), ("w_v", 0, 1)], dwkv)])
    dh, d_kv_g = seq(_bwd_rows_to_stream, c, "kv_bwd_data", dkv_list, wkv, c.kr, 0, 2 * c.DKV, h_kv, kv_g, dh)
    d_rel_t = seq(_band_bias_grad, dbias_list, onehot)
    d_wsp, d_bsp, d_lng = [None] * LA, [None] * LA, [None] * LA
    for l in reversed(range(LA)):
        h0, z, a, hn, gated, h1, p, hnf = sav_a[l]
        w_in, in_i, w1, w1_i, rows, wout, wout_i = wts_a[l]
        dh1 = ffn_bwd(l, dh, h1, p, hnf, w1, w1_i, rows)
        dgated = seq(_bwd_rows_data, c, f"a_out_bwd_data{l}", dh1, wout, c.ar, wout_i)
        dwout = seq(_wgrad_rows, c, f"a_out_bwd_w{l}", gated, [dh1], c.AW, D)
        send(f"a_out{l}", [([("a_w_out", l, 0)], dwout)])
        dz, d_wsp[l], dbt, d_lng[l] = seq(_sgu_bwd, c, f"sgu_bwd{l}", a, z, dgated, ln_g_full[l], wsp16[l],
                                          wsp16_t[l], bsp_t[l])
        d_bsp[l] = dbt.T
        dwin = seq(_wgrad_cols, c, f"a_in_bwd_w{l}", hn, dz)
        send(f"a_in{l}", [([("a_w_in", l, 0)], dwin)])
        dh, d_mix_g[l] = seq(_bwd_cols_to_stream, c, f"a_in_bwd_data{l}", dz, w_in, in_i, h0, mix_g[l], dh1)
    grad_x = dh.reshape(1, S, D)

    small = {
        "mix_norm_g": jnp.concatenate(d_mix_g, axis=0), "ffn_norm_g": jnp.concatenate(d_ffn_g, axis=0),
        "a_w_spatial": jnp.stack(d_wsp), "a_b_spatial": jnp.stack(d_bsp), "kv_norm_g": d_kv_g,
        "b_sinks": jnp.concatenate(dsink_list, axis=0), "rel_bias": d_rel_t.T, "final_norm_g": d_fin_g,
    }
    small_names = list(small)
    packs = [_rows128(small[k]) for k in small_names] + [_rows128(jnp.concatenate(d_lng, axis=0)), _rows128(loss_row)]
    offs = [int(o) for o in np.cumsum([0] + [p.shape[0] for p in packs])]
    Rs = offs[-1] + (-offs[-1]) % (8 * NDEV)
    tail_rows = Rs - offs[len(small_names)]
    packed = jnp.concatenate(packs + [jnp.zeros((Rs - offs[-1], LANES), F32)], axis=0)
    slab = packed.reshape(NDEV, Rs // NDEV, LANES)
    st = seq(_send_start, "small_grads_start", [slab], [lax.empty(slab.shape, slab.dtype)], "exchange")
    while len(in_flight) > 1:
        land(*in_flight.pop(0))
    (parts,) = seq(_send_wait, "small_grads_wait", st, "exchange")
    mine = seq(_sum_parts, "small_grads_sum", parts)
    st = seq(_send_start, "small_sums_start", [mine], [_landing(mine)], "broadcast")
    while in_flight:
        land(*in_flight.pop(0))
    (sums,) = seq(_send_wait, "small_sums_wait", st, "broadcast")
    small_all = sums.reshape(1, Rs, LANES)
    loss = sums.reshape(Rs, LANES)[offs[-2], 0]

    grads, deltas, new_m, new_v = {}, {}, {}, {}

    def put(k, outs, shape):
        grads[k], deltas[k], new_m[k], new_v[k] = (t.reshape(shape) for t in outs)

    def pack_state(d):
        return jnp.concatenate([_rows128(d[k]) for k in small_names] + [jnp.zeros((tail_rows, LANES), F32)], axis=0)

    outs = seq(_adamw, "adamw_small", small_all, (1, Rs, LANES), lambda i: (0, 0, 0),
               pack_state(weights), pack_state(m_in), pack_state(v_in), Rs)
    for n_, k in enumerate(small_names):
        shape = weights[k].shape
        size = int(np.prod(shape))
        put(k, [t[offs[n_]:offs[n_ + 1]].reshape(-1)[:size] for t in outs], shape)
    lng_sum = outs[0][offs[-3]:offs[-2]].reshape(-1)[:LA * c.AW].reshape(LA, c.AW)
    lng_mine = lax.dynamic_slice_in_dim(lng_sum, me * c.ar, c.ar, axis=1)
    lng_parts = jnp.concatenate([lng_mine[None], jnp.zeros((NDEV - 1, LA, c.ar), F32)], axis=0)
    put("a_ln_g", seq(_adamw, "adamw_ln_g", lng_parts, (NDEV, LA, c.ar), lambda i: (0, 0, 0),
                      a_ln_g, m_in["a_ln_g"], v_in["a_ln_g"], LA), a_ln_g.shape)
    for k in ("a_w_in", "ffn_w1", "ffn_w2", "a_w_out", "b_w_o", "b_w_q", "w_k", "w_v"):
        put(k, results[k], weights[k].shape)

    return (loss, grad_x, *[grads[k] for k in names], *[deltas[k] for k in names],
            *[new_m[k] for k in names], *[new_v[k] for k in names])
```

```python
import numpy as np
import math
import jax
import jax.numpy as jnp
from jax import lax
from jax.experimental import pallas as pl
from jax.experimental.pallas import tpu as pltpu

F32 = jnp.float32
BF16 = jnp.bfloat16

NDEV = 8
EPS = 1e-6
CHUNK = 128
A_GROUPS = 8
N_HEADS = 16
N_KV_HEADS = 4
Q_PER_KV = N_HEADS // N_KV_HEADS
HEAD_DIM = 64
BLOCK = 128
N_BUCKETS = 32
MAX_DISTANCE = 128
ADAM_LR, ADAM_B1, ADAM_B2, ADAM_EPS, ADAM_WD, ADAM_STEP = 0.001, 0.9, 0.999, 1e-08, 0.01, 10
LANES = 128
VMEM_LIMIT = 56 * 1024 * 1024
INV_SQRT2 = 0.7071067811865476
INV_SQRT_2PI = 0.3989422804014327
EXCHANGE_LAG = 4

HBM = pl.BlockSpec(memory_space=pltpu.HBM)
SMEM = pl.BlockSpec(memory_space=pltpu.SMEM)
ANY = pl.BlockSpec(memory_space=pl.ANY)
SEM = pl.BlockSpec(memory_space=pltpu.SEMAPHORE)
MESH = pl.DeviceIdType.MESH


def _params(n_grid):
    return pltpu.CompilerParams(dimension_semantics=("arbitrary",) * n_grid, vmem_limit_bytes=VMEM_LIMIT)


def _const(block, index_map):
    return pl.BlockSpec(block, index_map, pipeline_mode=pl.Buffered(1))


def _pcall(body, *, ins, in_specs, dep=None, **kw):
    n_in = len(ins)
    if dep is None or any(dep is t for t in ins):
        return pl.pallas_call(body, in_specs=list(in_specs), **kw)(*ins)

    def with_dep(*refs):
        body(*refs[:n_in], *refs[n_in + 1:])

    return pl.pallas_call(with_dep, in_specs=[*in_specs, ANY], **kw)(*ins, dep)


class _Seq:
    def __init__(self):
        self.last = None

    def __call__(self, fn, *args, **kw):
        out = fn(*args, dep=self.last, **kw)
        self.last = out[0] if isinstance(out, (tuple, list)) else out
        return out


def _rstd(h):
    return lax.rsqrt(jnp.mean(h * h, axis=-1, keepdims=True) + EPS)


def _rms_bwd(dhn, h, g, dres):
    r = _rstd(h)
    xh = h * r
    dg = jnp.sum(dhn * xh, axis=0, keepdims=True)
    dxh = dhn * g
    dx = r * (dxh - xh * jnp.mean(dxh * xh, axis=-1, keepdims=True))
    return dres + dx, dg


def _gelu(z):
    return 0.5 * z * (1.0 + lax.erf(z * INV_SQRT2))


def _gelu_grad(z):
    return 0.5 * (1.0 + lax.erf(z * INV_SQRT2)) + z * (jnp.exp(-0.5 * z * z) * INV_SQRT_2PI)


def _dot(a, b, dims):
    return lax.dot_general(a, b, (dims, ((), ())), preferred_element_type=F32)


NN = ((1,), (0,))
NT = ((1,), (1,))
TN = ((0,), (0,))


def _mm(name, ins, in_specs, out_shapes, out_specs, *, grid, dims, nk, acc_shape, load_a, load_b, epilogue,
        dep=None):
    n_in, n_out = len(ins), len(out_shapes)
    kax = len(grid) - 1

    def body(*refs):
        in_refs = refs[:n_in]
        out_refs = refs[n_in:n_in + n_out]
        a = load_a(in_refs, out_refs)
        b = load_b(in_refs)
        prod = _dot(a, b, dims)
        if nk == 1:
            epilogue(prod, in_refs, out_refs)
        else:
            acc = refs[n_in + n_out]
            k = pl.program_id(kax)

            @pl.when(k == 0)
            def _():
                acc[...] = prod

            @pl.when(k > 0)
            def _():
                acc[...] += prod

            @pl.when(k == nk - 1)
            def _():
                epilogue(acc[...], in_refs, out_refs)

    return _pcall(
        body, name=name, ins=ins, in_specs=in_specs, dep=dep, grid=grid, out_specs=out_specs, out_shape=out_shapes,
        scratch_shapes=[pltpu.VMEM(acc_shape, F32)] if nk > 1 else [], compiler_params=_params(len(grid)))


def _bf(ref_idx):
    return lambda in_refs, *_: in_refs[ref_idx][...].astype(BF16)


def _b_view(ref_idx, rows):
    def load(in_refs):
        b = in_refs[ref_idx][...]
        return b.reshape(rows, b.shape[-1])
    return load


class Cfg:
    pass


def _config(x, a_w_in, a_w_out, w_k, b_w_q, b_w_o, ffn_w1, ffn_w2):
    c = Cfg()
    c.S, c.D = x.shape[1], x.shape[2]
    c.LA, _, c.cw = a_w_in.shape
    c.AW2 = NDEV * c.cw
    c.AW = c.AW2 // 2
    c.gd = c.AW // A_GROUPS
    c.ar = a_w_out.shape[1]
    c.LF, _, c.fw = ffn_w1.shape
    c.fr = ffn_w2.shape[1]
    c.LB, c.qr, c.DQ = b_w_q.shape
    c.orr = b_w_o.shape[1]
    c.kr, c.DKV = w_k.shape
    c.tm = min(1024, c.S)
    c.tmw = min(512, c.S)
    c.tms = min(256, c.S)
    c.nb = c.S // BLOCK
    assert c.cw == c.fw == c.fr and c.AW == NDEV * c.ar and c.D == NDEV * c.qr == NDEV * c.kr
    assert c.DQ == NDEV * c.orr == N_HEADS * HEAD_DIM and c.DKV == N_KV_HEADS * HEAD_DIM
    assert c.S % c.tm == 0 and c.S % c.tmw == 0 and c.S % c.tms == 0 and c.tms % CHUNK == 0 and c.gd % LANES == 0
    assert c.LA >= 1 and c.LB >= 1 and c.LF == c.LA + c.LB
    return c


def _cached_rms(h_idx, g_idx, hn_out_idx, jax_axis=1):
    def load(in_refs, out_refs):
        hn_ref = out_refs[hn_out_idx]

        @pl.when(pl.program_id(jax_axis) == 0)
        def _():
            h = in_refs[h_idx][...]
            hn_ref[...] = (h * _rstd(h) * in_refs[g_idx][...]).astype(BF16)

        return hn_ref[...]
    return load


def _a_in_fwd(c, name, h, g, col, ci, dep=None):
    S, D, cw, tm = c.S, c.D, c.cw, c.tmw

    def body(h_ref, g_ref, w_ref, z_ref, a_ref, hn_ref):
        h = h_ref[...]
        hn = (h * _rstd(h) * g_ref[...]).astype(BF16)
        hn_ref[...] = hn
        for j in range(NDEV):
            cols = slice(j * cw, (j + 1) * cw)
            z = _dot(hn, w_ref[j], NT)
            z_ref[:, cols] = z.astype(BF16)
            a_ref[:, cols] = _gelu(z).astype(BF16)

    row = pl.BlockSpec((tm, D), lambda i: (i, 0))
    wide = pl.BlockSpec((tm, c.AW2), lambda i: (i, 0))
    return _pcall(
        body, name=name, ins=[h, g, col], dep=dep, grid=(S // tm,),
        in_specs=[row, pl.BlockSpec((1, D), lambda i: (0, 0)), _const((NDEV, None, cw, D), lambda i: (0, ci, 0, 0))],
        out_specs=[wide, wide, row],
        out_shape=[jax.ShapeDtypeStruct((S, c.AW2), BF16), jax.ShapeDtypeStruct((S, c.AW2), BF16),
                   jax.ShapeDtypeStruct((S, D), BF16)],
        compiler_params=_params(1))


def _rms_mm_rows(c, name, h, g, slab, blk_rows, blk_idx, n_out, dep=None):
    S, D, tm = c.S, c.D, c.tm

    def epilogue(acc, in_refs, out_refs):
        out_refs[0][...] = acc.astype(BF16)

    return _mm(
        name, [h, slab, g],
        [pl.BlockSpec((tm, D), lambda i, j, k: (i, 0)),
         pl.BlockSpec((NDEV, blk_rows, n_out), lambda i, j, k: (0, blk_idx, 0)),
         pl.BlockSpec((1, D), lambda i, j, k: (0, 0))],
        [jax.ShapeDtypeStruct((S, n_out), BF16), jax.ShapeDtypeStruct((S, D), BF16)],
        [pl.BlockSpec((tm, n_out), lambda i, j, k: (i, 0)), pl.BlockSpec((tm, D), lambda i, j, k: (i, 0))],
        grid=(S // tm, 1, 1), dims=NN, nk=1, acc_shape=None,
        load_a=_cached_rms(0, 2, 1), load_b=_b_view(1, NDEV * blk_rows), epilogue=epilogue, dep=dep)


def _mm_res(c, name, a, slab, blk_rows, blk_idx, res, dep=None):
    S, D, tm = c.S, c.D, c.tm
    K = NDEV * blk_rows

    def epilogue(acc, in_refs, out_refs):
        out_refs[0][...] = in_refs[2][...] + acc

    return _mm(
        name, [a, slab, res],
        [pl.BlockSpec((tm, K), lambda i, j, k: (i, 0)),
         pl.BlockSpec((NDEV, blk_rows, D), lambda i, j, k: (0, blk_idx, 0)),
         pl.BlockSpec((tm, D), lambda i, j, k: (i, 0))],
        [jax.ShapeDtypeStruct((S, D), F32)], [pl.BlockSpec((tm, D), lambda i, j, k: (i, 0))],
        grid=(S // tm, 1, 1), dims=NN, nk=1, acc_shape=None,
        load_a=_bf(0), load_b=_b_view(1, K), epilogue=epilogue, dep=dep)[0]


def _sgu_masks():
    ii = lax.broadcasted_iota(jnp.int32, (CHUNK, CHUNK), 0)
    jj = lax.broadcasted_iota(jnp.int32, (CHUNK, CHUNK), 1)
    return ii >= jj


def _sgu_fwd(c, name, a, ln_g, wc, b_t, dep=None):
    S, AW, gd, tm = c.S, c.AW, c.gd, c.tms

    def body(a_ref, lng_ref, wc_ref, bt_ref, out_ref):
        va = a_ref[:, AW:].astype(F32)
        xc = va - jnp.mean(va, axis=-1, keepdims=True)
        vn = (xc * lax.rsqrt(jnp.mean(xc * xc, axis=-1, keepdims=True) + EPS) * lng_ref[...]).astype(BF16)
        for ch in range(tm // CHUNK):
            rows = slice(ch * CHUNK, (ch + 1) * CHUNK)
            for g in range(A_GROUPS):
                cols = slice(g * gd, (g + 1) * gd)
                mixed = _dot(wc_ref[g], vn[rows, cols], NN) + bt_ref[:, g:g + 1]
                out_ref[rows, cols] = (a_ref[rows, cols].astype(F32) * mixed).astype(BF16)

    return _pcall(
        body, name=name, ins=[a, ln_g, wc, b_t], dep=dep, grid=(S // tm,),
        in_specs=[pl.BlockSpec((tm, 2 * AW), lambda i: (i, 0)), pl.BlockSpec((1, AW), lambda i: (0, 0)),
                  pl.BlockSpec((A_GROUPS, CHUNK, CHUNK), lambda i: (0, 0, 0)),
                  pl.BlockSpec((CHUNK, A_GROUPS), lambda i: (0, 0))],
        out_specs=pl.BlockSpec((tm, AW), lambda i: (i, 0)),
        out_shape=jax.ShapeDtypeStruct((S, AW), BF16), compiler_params=_params(1))


def _ffn_fwd(c, name, h, g, col, ci, rows, dep=None):
    S, D, fw, tm = c.S, c.D, c.fw, c.tmw
    F = NDEV * fw

    def body(h_ref, g_ref, w1_ref, w2_ref, p_ref, out_ref, hn_ref, r_ref):
        h = h_ref[...]
        hn = (h * _rstd(h) * g_ref[...]).astype(BF16)
        hn_ref[...] = hn
        for j in range(NDEV):
            cols = slice(j * fw, (j + 1) * fw)
            p = jnp.maximum(_dot(hn, w1_ref[j], NT), 0.0)
            p_ref[:, cols] = p.astype(BF16)
            r_ref[:, cols] = (p * p).astype(BF16)
        out_ref[...] = h + _dot(r_ref[...], w2_ref[...].reshape(F, D), NN)

    row = pl.BlockSpec((tm, D), lambda i: (i, 0))
    return _pcall(
        body, name=name, ins=[h, g, col, rows], dep=dep, grid=(S // tm,),
        in_specs=[row, pl.BlockSpec((1, D), lambda i: (0, 0)),
                  _const((NDEV, None, fw, D), lambda i: (0, ci, 0, 0)), _const((NDEV, c.fr, D), lambda i: (0, 0, 0))],
        out_specs=[pl.BlockSpec((tm, F), lambda i: (i, 0)), row, row],
        out_shape=[jax.ShapeDtypeStruct((S, F), BF16), jax.ShapeDtypeStruct((S, D), F32),
                   jax.ShapeDtypeStruct((S, D), BF16)],
        scratch_shapes=[pltpu.VMEM((tm, F), BF16)], compiler_params=_params(1))


def _bucket_table():
    qi = np.arange(BLOCK)[:, None]
    kj = np.arange(2 * BLOCK)[None, :]
    d = np.maximum(qi + BLOCK - kj, 0)
    max_exact = N_BUCKETS // 2
    ratio = np.log(np.maximum(d, 1).astype(np.float32) / np.float32(max_exact)) / np.float32(
        math.log(MAX_DISTANCE / max_exact))
    large = np.minimum(max_exact + (ratio.astype(np.float32) * np.float32(N_BUCKETS - max_exact)).astype(np.int32),
                       N_BUCKETS - 1)
    return np.where(d < max_exact, d, large).astype(np.int32)


def _bucket_onehot():
    b = jnp.asarray(_bucket_table().reshape(1, -1))
    return (b == lax.broadcasted_iota(jnp.int32, (N_BUCKETS, b.shape[1]), 0)).astype(F32)


def _whole(t):
    return pl.BlockSpec(t.shape, lambda: (0,) * t.ndim)


def _band_bias(rel_bias_t, onehot, dep=None):
    def body(r_ref, oh_ref, out_ref):
        out_ref[...] = lax.dot_general(r_ref[...], oh_ref[...], (NN, ((), ())), preferred_element_type=F32,
                                       precision=lax.Precision.HIGHEST)

    n = onehot.shape[1]
    return _pcall(body, name="band_bias", ins=[rel_bias_t, onehot], in_specs=[_whole(rel_bias_t), _whole(onehot)],
                  dep=dep, out_shape=jax.ShapeDtypeStruct((N_HEADS, n), F32), compiler_params=_params(0))


def _band_bias_grad(dbias_list, onehot, dep=None):
    n_in = len(dbias_list)

    def body(*refs):
        oh_ref, out_ref = refs[n_in], refs[n_in + 1]
        d = refs[0][...]
        for r in refs[1:n_in]:
            d = d + r[...]
        out_ref[...] = lax.dot_general(d, oh_ref[...], (NT, ((), ())), preferred_element_type=F32,
                                       precision=lax.Precision.HIGHEST)

    ins = [*dbias_list, onehot]
    return _pcall(body, name="band_bias_grad", ins=ins, in_specs=[_whole(t) for t in ins], dep=dep,
                  out_shape=jax.ShapeDtypeStruct((N_HEADS, N_BUCKETS), F32), compiler_params=_params(0))


KV_PAIRS = N_KV_HEADS // 2
PAIR_ROWS = 2 * Q_PER_KV * BLOCK
MASKED = float(np.finfo(np.float32).min) / 2


def _slot_cols(w):
    lead = w.shape[:-1]
    return w.reshape(*lead, KV_PAIRS, 2, Q_PER_KV, HEAD_DIM).swapaxes(-3, -2).reshape(*lead, N_HEADS * HEAD_DIM)


def _unslot_cols(w):
    lead = w.shape[:-1]
    return w.reshape(*lead, KV_PAIRS, Q_PER_KV, 2, HEAD_DIM).swapaxes(-3, -2).reshape(*lead, N_HEADS * HEAD_DIM)


def _slot_rows(blocks):
    n = blocks.shape[-1]
    return blocks.reshape(KV_PAIRS, 2, Q_PER_KV, HEAD_DIM, n).swapaxes(1, 2).reshape(blocks.shape)


def _unslot_rows(blocks):
    n = blocks.shape[-1]
    return blocks.reshape(KV_PAIRS, Q_PER_KV, 2, HEAD_DIM, n).swapaxes(1, 2).reshape(blocks.shape)


def _slot_bias(bias):
    qi = np.arange(BLOCK)[:, None]
    kj = np.arange(2 * BLOCK)[None, :]
    dist = qi + BLOCK - kj
    window = (dist >= 0) & (dist < BLOCK)
    b = bias.reshape(KV_PAIRS, 2, Q_PER_KV, BLOCK, 2 * BLOCK).swapaxes(1, 2).reshape(KV_PAIRS, PAIR_ROWS, 2 * BLOCK)
    tile = lambda mk: jnp.asarray(np.tile(mk, (2 * Q_PER_KV, 1)))[None]
    return jnp.stack([jnp.where(tile(window & (kj >= BLOCK)), b, MASKED), jnp.where(tile(window), b, MASKED)])


def _unslot_bias(db):
    return db.reshape(KV_PAIRS, Q_PER_KV, 2, BLOCK, 2 * BLOCK).swapaxes(1, 2).reshape(N_HEADS, -1)


def _pair_kv(kvc_ref, kvp_ref, kvp, dkv):
    lanes = slice(kvp * LANES, (kvp + 1) * LANES)
    vlanes = slice(dkv + kvp * LANES, dkv + (kvp + 1) * LANES)
    k2 = jnp.concatenate([kvp_ref[:, lanes], kvc_ref[:, lanes]], axis=0)
    v2 = jnp.concatenate([kvp_ref[:, vlanes], kvc_ref[:, vlanes]], axis=0)
    return k2, v2


def _head_operand(ref, grp, par, low, scale=None):
    xg = ref[:, grp * LANES:(grp + 1) * LANES]
    if scale is not None:
        xg = xg * scale
    zero = jnp.zeros_like(xg)
    return jnp.where(low, xg, zero) if par == 0 else jnp.where(low, zero, xg)


def _head_probs(qh, k2, bias_rows, sink):
    s = _dot(qh, k2, NT) + bias_rows
    m = jnp.maximum(jnp.max(s, axis=-1, keepdims=True), sink)
    e = jnp.exp(s - m)
    es = jnp.exp(sink - m)
    inv = 1.0 / (jnp.sum(e, axis=-1, keepdims=True) + es)
    return e * inv, es * inv


def _attn_specs(c):
    dq, dkv2 = c.DQ, 2 * c.DKV
    return [pl.BlockSpec((BLOCK, dq), lambda n: (n, 0)),
            pl.BlockSpec((BLOCK, dkv2), lambda n: (n, 0)),
            pl.BlockSpec((BLOCK, dkv2), lambda n: (jnp.maximum(n - 1, 0), 0))]


def _bias_spec():
    return pl.BlockSpec((None, KV_PAIRS, PAIR_ROWS, 2 * BLOCK), lambda n: (jnp.minimum(n, 1), 0, 0, 0))


def _low_lanes():
    return lax.broadcasted_iota(jnp.int32, (BLOCK, LANES), 1) < HEAD_DIM


def _first_key():
    return lax.broadcasted_iota(jnp.int32, (BLOCK, 2 * BLOCK), 1) == 0


def _probs_spec():
    return pl.BlockSpec((None, KV_PAIRS, PAIR_ROWS, 2 * BLOCK), lambda n: (n, 0, 0, 0))


def _attn_fwd(c, name, q, kv, bias, sinks, dep=None):
    S, dq = c.S, c.DQ

    def body(q_ref, kvc_ref, kvp_ref, bias_ref, sink_ref, o_ref, probs_ref):
        low = _low_lanes()
        first = _first_key()
        for kvp in range(KV_PAIRS):
            k2, v2 = _pair_kv(kvc_ref, kvp_ref, kvp, c.DKV)
            for g in range(Q_PER_KV):
                grp = kvp * Q_PER_KV + g
                halves = []
                for par in range(2):
                    rows = slice((2 * g + par) * BLOCK, (2 * g + par + 1) * BLOCK)
                    qh = _head_operand(q_ref, grp, par, low, scale=HEAD_DIM ** -0.5)
                    p, ps = _head_probs(qh, k2, bias_ref[kvp, rows, :], sink_ref[(2 * kvp + par) * Q_PER_KV + g])
                    probs_ref[kvp, rows, :] = jnp.where(first, ps, p).astype(BF16)
                    halves.append(_dot(p.astype(BF16), v2, NN))
                o_ref[:, grp * LANES:(grp + 1) * LANES] = jnp.where(low, halves[0], halves[1]).astype(BF16)

    return _pcall(
        body, name=name, ins=[q, kv, kv, bias, sinks], dep=dep, grid=(c.nb,),
        in_specs=_attn_specs(c) + [_bias_spec(), SMEM],
        out_specs=[pl.BlockSpec((BLOCK, dq), lambda n: (n, 0)), _probs_spec()],
        out_shape=[jax.ShapeDtypeStruct((S, dq), BF16),
                   jax.ShapeDtypeStruct((c.nb, KV_PAIRS, PAIR_ROWS, 2 * BLOCK), BF16)],
        compiler_params=_params(1))


def _final_loss(c, h, g, target, dep=None):
    S, D, tm = c.S, c.D, c.tm

    def body(h_ref, g_ref, t_ref, dh_ref, dg_ref, loss_ref):
        i = pl.program_id(0)
        h = h_ref[...]
        gg = g_ref[...]
        r = _rstd(h)
        xh = h * r
        err = xh * gg - t_ref[...]
        lp = jnp.sum(jnp.sum(err * err, axis=1, keepdims=True), axis=0, keepdims=True) * (0.5 / D)
        dx, dg = _rms_bwd(err * (1.0 / D), h, gg, 0.0)
        dh_ref[...] = dx

        @pl.when(i == 0)
        def _():
            dg_ref[...] = dg
            loss_ref[...] = jnp.broadcast_to(lp, loss_ref.shape)

        @pl.when(i > 0)
        def _():
            dg_ref[...] += dg
            loss_ref[...] += jnp.broadcast_to(lp, loss_ref.shape)

    row = pl.BlockSpec((tm, D), lambda i: (i, 0))
    return _pcall(
        body, name="final_loss", ins=[h, g, target], dep=dep, grid=(S // tm,),
        in_specs=[row, pl.BlockSpec((1, D), lambda i: (0, 0)), row],
        out_specs=[row, pl.BlockSpec((1, D), lambda i: (0, 0)), pl.BlockSpec((1, LANES), lambda i: (0, 0))],
        out_shape=[jax.ShapeDtypeStruct((S, D), F32), jax.ShapeDtypeStruct((1, D), F32),
                   jax.ShapeDtypeStruct((1, LANES), F32)],
        compiler_params=_params(1))


def _rms_bwd_epilogue(h_idx, g_idx, res_idx):
    def epilogue(dhn, in_refs, out_refs):
        dh, dg = _rms_bwd(dhn, in_refs[h_idx][...], in_refs[g_idx][...], in_refs[res_idx][...])
        out_refs[0][...] = dh
        i = pl.program_id(0)

        @pl.when(i == 0)
        def _():
            out_refs[1][...] = dg

        @pl.when(i > 0)
        def _():
            out_refs[1][...] += dg
    return epilogue


def _stream_outs(c, tm):
    S, D = c.S, c.D
    return ([jax.ShapeDtypeStruct((S, D), F32), jax.ShapeDtypeStruct((1, D), F32)],
            [pl.BlockSpec((tm, D), lambda i, j, k: (i, 0)), pl.BlockSpec((1, D), lambda i, j, k: (0, 0))])


def _row_specs(c, tm):
    D = c.D
    return [pl.BlockSpec((tm, D), lambda i, j, k: (i, 0)), pl.BlockSpec((1, D), lambda i, j, k: (0, 0)),
            pl.BlockSpec((tm, D), lambda i, j, k: (i, 0))]


def _bwd_rows_to_stream(c, name, dy_list, slab, blk_rows, blk_idx, n_in_cols, h, g, dres, dep=None):
    S, D, tm = c.S, c.D, c.tm
    nd = len(dy_list)

    def load_a(in_refs, out_refs):
        a = in_refs[0][...]
        for r in in_refs[1:nd]:
            a = a + r[...]
        return a.astype(BF16)

    shapes, specs = _stream_outs(c, tm)
    return _mm(
        name, [*dy_list, slab, h, g, dres],
        [pl.BlockSpec((tm, n_in_cols), lambda i, j, k: (i, 0))] * nd
        + [pl.BlockSpec((NDEV, blk_rows, n_in_cols), lambda i, j, k: (0, blk_idx, 0))] + _row_specs(c, tm),
        shapes, specs, grid=(S // tm, 1, 1), dims=NT, nk=1, acc_shape=None,
        load_a=load_a, load_b=_b_view(nd, NDEV * blk_rows), epilogue=_rms_bwd_epilogue(nd + 1, nd + 2, nd + 3),
        dep=dep)


def _bwd_cols_to_stream(c, name, dy, col, ci, h, g, dres, dep=None):
    S, D, cw, tm = c.S, c.D, c.cw, c.tmw
    K = NDEV * cw
    shapes, specs = _stream_outs(c, tm)
    return _mm(
        name, [dy, col, h, g, dres],
        [pl.BlockSpec((tm, K), lambda i, j, k: (i, 0)),
         _const((NDEV, None, cw, D), lambda i, j, k: (0, ci, 0, 0))] + _row_specs(c, tm),
        shapes, specs, grid=(S // tm, 1, 1), dims=NN, nk=1, acc_shape=None,
        load_a=_bf(0), load_b=_b_view(1, K), epilogue=_rms_bwd_epilogue(2, 3, 4), dep=dep)


def _bwd_rows_data(c, name, dy, slab, blk_rows, blk_idx, dep=None):
    S, D, tm = c.S, c.D, c.tm
    K = NDEV * blk_rows

    def epilogue(acc, in_refs, out_refs):
        out_refs[0][...] = acc.astype(BF16)

    return _mm(
        name, [dy, slab],
        [pl.BlockSpec((tm, D), lambda i, j, k: (i, 0)),
         pl.BlockSpec((NDEV, blk_rows, D), lambda i, j, k: (0, blk_idx, 0))],
        [jax.ShapeDtypeStruct((S, K), BF16)], [pl.BlockSpec((tm, K), lambda i, j, k: (i, 0))],
        grid=(S // tm, 1, 1), dims=NT, nk=1, acc_shape=None,
        load_a=_bf(0), load_b=_b_view(1, K), epilogue=epilogue, dep=dep)[0]


def _wgrad_rows(c, name, a, b_list, n_a, n_b, dep=None):
    S, tm = c.S, c.tm
    nb_in = len(b_list)
    blk_rows = n_a // NDEV

    def load_b(in_refs):
        b = in_refs[1][...]
        for r in in_refs[2:1 + nb_in]:
            b = b + r[...]
        return b.astype(BF16)

    def epilogue(acc, in_refs, out_refs):
        out_refs[0][...] = acc.reshape(NDEV, blk_rows, n_b).astype(BF16)

    return _mm(
        name, [a, *b_list],
        [pl.BlockSpec((tm, n_a), lambda i, j, k: (k, 0))] + [pl.BlockSpec((tm, n_b), lambda i, j, k: (k, 0))] * nb_in,
        [jax.ShapeDtypeStruct((NDEV, blk_rows, n_b), BF16)],
        [pl.BlockSpec((NDEV, blk_rows, n_b), lambda i, j, k: (0, 0, 0))],
        grid=(1, 1, S // tm), dims=TN, nk=S // tm, acc_shape=(n_a, n_b),
        load_a=_bf(0), load_b=load_b, epilogue=epilogue, dep=dep)[0]


def _wgrad_cols(c, name, a, b, dep=None):
    S, D, cw = c.S, c.D, c.cw

    def body(a_ref, b_ref, out_ref):
        out_ref[...] = _dot(a_ref[...], b_ref[...], TN).astype(BF16)

    return _pcall(
        body, name=name, ins=[a, b], dep=dep, grid=(NDEV,),
        in_specs=[_const((S, D), lambda j: (0, 0)), pl.BlockSpec((S, cw), lambda j: (0, j))],
        out_specs=pl.BlockSpec((None, D, cw), lambda j: (j, 0, 0)),
        out_shape=jax.ShapeDtypeStruct((NDEV, D, cw), BF16), compiler_params=_params(1))


def _ffn_bwd_data(c, name, dh, p, col, ci, rows, h, g, dep=None):
    S, D, fw, tm = c.S, c.D, c.fw, c.tmw
    F = NDEV * fw

    def body(dh_ref, p_ref, w1t_ref, w2_ref, h_ref, g_ref, da_ref, out_ref, dg_ref, dhb_ref):
        i = pl.program_id(0)
        dh = dh_ref[...]
        dhb = dh.astype(BF16)
        dhb_ref[...] = dhb
        for j in range(NDEV):
            cols = slice(j * fw, (j + 1) * fw)
            da_ref[:, cols] = (_dot(dhb, w2_ref[j], NT) * (2.0 * p_ref[:, cols].astype(F32))).astype(BF16)
        dx, dg = _rms_bwd(_dot(da_ref[...], w1t_ref[...].reshape(F, D), NN), h_ref[...], g_ref[...], dh)
        out_ref[...] = dx

        @pl.when(i == 0)
        def _():
            dg_ref[...] = dg

        @pl.when(i > 0)
        def _():
            dg_ref[...] += dg

    row = pl.BlockSpec((tm, D), lambda i: (i, 0))
    wide = pl.BlockSpec((tm, F), lambda i: (i, 0))
    return _pcall(
        body, name=name, ins=[dh, p, col, rows, h, g], dep=dep, grid=(S // tm,),
        in_specs=[row, wide, _const((NDEV, None, fw, D), lambda i: (0, ci, 0, 0)),
                  _const((NDEV, c.fr, D), lambda i: (0, 0, 0)),
                  row, pl.BlockSpec((1, D), lambda i: (0, 0))],
        out_specs=[wide, row, pl.BlockSpec((1, D), lambda i: (0, 0)), row],
        out_shape=[jax.ShapeDtypeStruct((S, F), BF16), jax.ShapeDtypeStruct((S, D), F32),
                   jax.ShapeDtypeStruct((1, D), F32), jax.ShapeDtypeStruct((S, D), BF16)],
        compiler_params=_params(1))


def _ffn_bwd_w(c, name, hn, da, p, dhb, dep=None):
    S, D, fw = c.S, c.D, c.fw

    def body(hn_ref, da_ref, p_ref, dhb_ref, dw1_ref, dw2_ref):
        dw1_ref[...] = _dot(hn_ref[...], da_ref[...], TN).astype(BF16)
        pf = p_ref[...].astype(F32)
        dw2_ref[...] = _dot((pf * pf).astype(BF16), dhb_ref[...], TN).astype(BF16)

    panel = pl.BlockSpec((S, fw), lambda j: (0, j))
    return _pcall(
        body, name=name, ins=[hn, da, p, dhb], dep=dep, grid=(NDEV,),
        in_specs=[_const((S, D), lambda j: (0, 0)), panel, panel, _const((S, D), lambda j: (0, 0))],
        out_specs=[pl.BlockSpec((None, D, fw), lambda j: (j, 0, 0)), pl.BlockSpec((None, c.fr, D), lambda j: (j, 0, 0))],
        out_shape=[jax.ShapeDtypeStruct((NDEV, D, fw), BF16), jax.ShapeDtypeStruct((NDEV, c.fr, D), BF16)],
        compiler_params=_params(1))


def _attn_bwd(c, name, q, kv, do, probs, dep=None):
    S, dq, dkv = c.S, c.DQ, c.DKV
    nb = c.nb
    scale = HEAD_DIM ** -0.5

    def body(q_ref, kvc_ref, kvp_ref, do_ref, probs_ref, dq_ref, dkv_ref, dbias_ref, dsink_ref, dsink_acc,
             ds_sc, p_sc, qm_sc, dom_sc):
        n = pl.program_id(0)

        @pl.when(n == 0)
        def _():
            dkv_ref[...] = jnp.zeros_like(dkv_ref)
            dbias_ref[...] = jnp.zeros_like(dbias_ref)
            dsink_acc[...] = jnp.zeros_like(dsink_acc)

        low = _low_lanes()
        first = _first_key()
        rows_c = pl.ds(pl.multiple_of(n * BLOCK, BLOCK), BLOCK)
        rows_p = pl.ds(pl.multiple_of(jnp.maximum(n - 1, 0) * BLOCK, BLOCK), BLOCK)
        for kvp in range(KV_PAIRS):
            k2, v2 = _pair_kv(kvc_ref, kvp_ref, kvp, dkv)
            for g in range(Q_PER_KV):
                grp = kvp * Q_PER_KV + g
                halves = []
                for par in range(2):
                    rows = slice((2 * g + par) * BLOCK, (2 * g + par + 1) * BLOCK)
                    qh = _head_operand(q_ref, grp, par, low, scale=scale)
                    doh = _head_operand(do_ref, grp, par, low)
                    saved = probs_ref[kvp, rows, :]
                    ps = saved[:, 0:1].astype(F32)
                    p16 = jnp.where(first, jnp.zeros_like(saved), saved)
                    p = p16.astype(F32)
                    dp = _dot(doh, v2, NT)
                    delta = jnp.sum(p * dp, axis=-1, keepdims=True)
                    ds = p * (dp - delta)
                    dbias_ref[kvp, rows, :] += ds
                    dsink_acc[rows, kvp:kvp + 1] += -(ps * delta)
                    ds16 = ds.astype(BF16)
                    halves.append(_dot(ds16, k2, NN) * scale)
                    ds_sc[rows, :] = ds16
                    p_sc[rows, :] = p16
                    qm_sc[rows, :] = qh
                    dom_sc[rows, :] = doh
                dq_ref[:, grp * LANES:(grp + 1) * LANES] = jnp.where(low, halves[0], halves[1]).astype(BF16)
            dk2 = _dot(ds_sc[...], qm_sc[...], TN)
            dv2 = _dot(p_sc[...], dom_sc[...], TN)
            lanes = slice(kvp * LANES, (kvp + 1) * LANES)
            vlanes = slice(dkv + kvp * LANES, dkv + (kvp + 1) * LANES)
            dkv_ref[rows_p, lanes] += dk2[:BLOCK]
            dkv_ref[rows_c, lanes] += dk2[BLOCK:]
            dkv_ref[rows_p, vlanes] += dv2[:BLOCK]
            dkv_ref[rows_c, vlanes] += dv2[BLOCK:]

        @pl.when(n == nb - 1)
        def _():
            dsink_ref[...] = jnp.sum(dsink_acc[...].reshape(2 * Q_PER_KV, BLOCK, KV_PAIRS), axis=1)

    return _pcall(
        body, name=name, ins=[q, kv, kv, do, probs], dep=dep, grid=(nb,),
        in_specs=_attn_specs(c) + [pl.BlockSpec((BLOCK, dq), lambda n: (n, 0)), _probs_spec()],
        out_specs=[pl.BlockSpec((BLOCK, dq), lambda n: (n, 0)), pl.BlockSpec((S, 2 * dkv), lambda n: (0, 0)),
                   pl.BlockSpec((KV_PAIRS, PAIR_ROWS, 2 * BLOCK), lambda n: (0, 0, 0)),
                   pl.BlockSpec((2 * Q_PER_KV, KV_PAIRS), lambda n: (0, 0))],
        out_shape=[jax.ShapeDtypeStruct((S, dq), BF16), jax.ShapeDtypeStruct((S, 2 * dkv), F32),
                   jax.ShapeDtypeStruct((KV_PAIRS, PAIR_ROWS, 2 * BLOCK), F32),
                   jax.ShapeDtypeStruct((2 * Q_PER_KV, KV_PAIRS), F32)],
        scratch_shapes=[pltpu.VMEM((PAIR_ROWS, KV_PAIRS), F32), pltpu.VMEM((PAIR_ROWS, 2 * BLOCK), BF16),
                        pltpu.VMEM((PAIR_ROWS, 2 * BLOCK), BF16), pltpu.VMEM((PAIR_ROWS, LANES), BF16),
                        pltpu.VMEM((PAIR_ROWS, LANES), BF16)],
        compiler_params=_params(1))


def _sgu_bwd(c, name, a, z, dgated, ln_g, wc, wc_t, b_t, dep=None):
    S, AW, gd, tm = c.S, c.AW, c.gd, c.tms

    def body(a_ref, z_ref, dg_ref, lng_ref, wc_ref, wct_ref, bt_ref, dz_ref, dws_ref, dbt_ref, dlng_ref, dvn_ref):
        i = pl.program_id(0)

        @pl.when(i == 0)
        def _():
            dws_ref[...] = jnp.zeros_like(dws_ref)
            dbt_ref[...] = jnp.zeros_like(dbt_ref)
            dlng_ref[...] = jnp.zeros_like(dlng_ref)

        lng = lng_ref[...]
        va = a_ref[:, AW:].astype(F32)
        xc = va - jnp.mean(va, axis=-1, keepdims=True)
        rstd = lax.rsqrt(jnp.mean(xc * xc, axis=-1, keepdims=True) + EPS)
        xh = xc * rstd
        vn = (xh * lng).astype(BF16)
        causal = _sgu_masks()
        for ch in range(tm // CHUNK):
            rows = slice(ch * CHUNK, (ch + 1) * CHUNK)
            for g in range(A_GROUPS):
                cols = slice(g * gd, (g + 1) * gd)
                blk = vn[rows, cols]
                mixed = _dot(wc_ref[g], blk, NN) + bt_ref[:, g:g + 1]
                dgb = dg_ref[rows, cols].astype(F32)
                dm = dgb * a_ref[rows, cols].astype(F32)
                dbt_ref[:, g:g + 1] += jnp.sum(dm, axis=1, keepdims=True)
                dm16 = dm.astype(BF16)
                dws_ref[g] += jnp.where(causal, _dot(dm16, blk, NT), 0.0)
                dvn_ref[rows, cols] = _dot(wct_ref[g], dm16, NN)
                dz_ref[rows, cols] = (dgb * mixed * _gelu_grad(z_ref[rows, cols].astype(F32))).astype(BF16)
        dvn = dvn_ref[...]
        dlng_ref[...] += jnp.sum(dvn * xh, axis=0, keepdims=True)
        dxh = dvn * lng
        dva = rstd * (dxh - jnp.mean(dxh, axis=-1, keepdims=True) - xh * jnp.mean(dxh * xh, axis=-1, keepdims=True))
        dz_ref[:, AW:] = (dva * _gelu_grad(z_ref[:, AW:].astype(F32))).astype(BF16)

    wide = pl.BlockSpec((tm, 2 * AW), lambda i: (i, 0))
    wsp = pl.BlockSpec((A_GROUPS, CHUNK, CHUNK), lambda i: (0, 0, 0))
    btsp = pl.BlockSpec((CHUNK, A_GROUPS), lambda i: (0, 0))
    return _pcall(
        body, name=name, ins=[a, z, dgated, ln_g, wc, wc_t, b_t], dep=dep, grid=(S // tm,),
        in_specs=[wide, wide, pl.BlockSpec((tm, AW), lambda i: (i, 0)), pl.BlockSpec((1, AW), lambda i: (0, 0)),
                  wsp, wsp, btsp],
        out_specs=[wide, wsp, btsp, pl.BlockSpec((1, AW), lambda i: (0, 0))],
        out_shape=[jax.ShapeDtypeStruct((S, 2 * AW), BF16), jax.ShapeDtypeStruct((A_GROUPS, CHUNK, CHUNK), F32),
                   jax.ShapeDtypeStruct((CHUNK, A_GROUPS), F32), jax.ShapeDtypeStruct((1, AW), F32)],
        scratch_shapes=[pltpu.VMEM((tm, AW), F32)], compiler_params=_params(1))


def _adamw(name, parts, part_block, part_index, w, m, v, tr, row_off=0, n_rows=None, prev=None, dep=None):
    R, C = w.shape
    n_rows = R if n_rows is None else n_rows
    assert n_rows % tr == 0 and row_off % tr == 0
    bc1 = 1.0 - ADAM_B1 ** ADAM_STEP
    bc2 = 1.0 - ADAM_B2 ** ADAM_STEP

    def body(p_ref, w_ref, m_ref, v_ref, *rest):
        g_ref, d_ref, nm_ref, nv_ref = rest[-4:]
        g = p_ref[0].astype(F32)
        for s in range(1, part_block[0]):
            g = g + p_ref[s].astype(F32)
        nm = ADAM_B1 * m_ref[...] + (1.0 - ADAM_B1) * g
        nv = ADAM_B2 * v_ref[...] + (1.0 - ADAM_B2) * (g * g)
        g_ref[...] = g
        nm_ref[...] = nm
        nv_ref[...] = nv
        d_ref[...] = -ADAM_LR * ((nm * (1.0 / bc1)) / (jnp.sqrt(nv * (1.0 / bc2)) + ADAM_EPS) + ADAM_WD * w_ref[...])

    ob = row_off // tr
    row = pl.BlockSpec((tr, C), lambda i: (ob + i, 0))
    out = jax.ShapeDtypeStruct((R, C), F32)
    chained = prev is not None
    return _pcall(
        body, name=name, ins=[parts, w, m, v] + (list(prev) if chained else []), dep=dep, grid=(n_rows // tr,),
        in_specs=[pl.BlockSpec(part_block, part_index), row, row, row] + ([ANY] * 4 if chained else []),
        out_specs=[row, row, row, row], out_shape=[out, out, out, out],
        input_output_aliases={4 + t: t for t in range(4)} if chained else {}, compiler_params=_params(1))


def _sum_parts(name, parts, dep=None):
    def body(p_ref, out_ref):
        g = p_ref[0]
        for s in range(1, parts.shape[0]):
            g = g + p_ref[s]
        out_ref[...] = g

    return _pcall(body, name=name, ins=[parts], in_specs=[_whole(parts)], dep=dep,
                  out_shape=jax.ShapeDtypeStruct(parts.shape[1:], F32), compiler_params=_params(0))


def _place():
    return lax.axis_index("x"), lax.axis_index("y"), lax.axis_index("c")


def _slot(px, py, pc):
    return 4 * px + 2 * py + pc


def _peer(k, x, y, c):
    return x ^ ((k >> 2) & 1), y ^ ((k >> 1) & 1), c ^ (k & 1)


SEND_PEERS = {"exchange": tuple(range(1, NDEV)), "gather": (1, 2, 4, 6), "forward": (2, 4, 6),
              "broadcast": tuple(range(1, NDEV))}


def _n_sems(mode, n_lands):
    return n_lands * (len(SEND_PEERS[mode]) + (mode != "forward"))


def _send_copies(mode, src_refs, land_refs, send_sems, recv_sems):
    x, y, c = _place()
    me = _slot(x, y, c)
    peers = SEND_PEERS[mode]
    remote, local = [], []
    for i, k in enumerate(peers):
        peer = _peer(k, x, y, c)
        for a, land in enumerate(land_refs):
            if mode == "exchange":
                src, dst, to = src_refs[a].at[_slot(*peer)], land.at[me], peer
            elif mode in ("gather", "broadcast"):
                src, dst, to = src_refs[a], land.at[me], peer
            else:
                src = dst = land.at[_slot(*peer)]
                to = (x, y, 1 - c)
            s = a * len(peers) + i
            remote.append(pltpu.make_async_remote_copy(src_ref=src, dst_ref=dst, send_sem=send_sems.at[s],
                                                       recv_sem=recv_sems.at[s], device_id=to, device_id_type=MESH))
    if mode != "forward":
        for a, land in enumerate(land_refs):
            src = src_refs[a].at[me] if mode == "exchange" else src_refs[a]
            local.append(pltpu.make_async_copy(src, land.at[me], send_sems.at[len(land_refs) * len(peers) + a]))
    return remote, local


def _send_start_groups(name, groups, mode, dep=None):
    sizes = [(len(s), len(l)) for s, l in groups]
    flat = [t for s, l in groups for t in (*s, *l)]
    n_in, ng = len(flat), len(groups)

    def body(*refs):
        sems, token, at = refs[n_in:n_in + 2 * ng], refs[-1], 0
        for gi, (ns, nl) in enumerate(sizes):
            remote, local = _send_copies(mode, refs[at:at + ns], refs[at + ns:at + ns + nl], sems[2 * gi],
                                         sems[2 * gi + 1])
            for cp in remote + local:
                cp.start()
            at += ns + nl
        token[...] = jnp.zeros_like(token)

    sem_shapes = [pltpu.SemaphoreType.DMA((_n_sems(mode, nl),)) for _, nl in sizes for _ in range(2)]
    if any(dep is t for t in flat):
        dep = None
    out = _pcall(
        body, name=name, ins=[pltpu.with_memory_space_constraint(t, pltpu.HBM) for t in flat],
        in_specs=[HBM] * n_in, dep=dep,
        out_shape=(*sem_shapes, *[pltpu.HBM(t.shape, t.dtype) for t in flat], jax.ShapeDtypeStruct((8, LANES), F32)),
        out_specs=(*[SEM] * (2 * ng), *[HBM] * n_in, pl.BlockSpec(memory_space=pltpu.VMEM)),
        input_output_aliases={i: 2 * ng + i for i in range(n_in)},
        compiler_params=pltpu.CompilerParams(has_side_effects=pltpu.SideEffectType.DATAFLOW_SIDE_EFFECTING))
    started, at = [], 2 * ng
    for gi, (ns, nl) in enumerate(sizes):
        started.append((out[-1], out[2 * gi], out[2 * gi + 1], list(out[at:at + ns]), list(out[at + ns:at + ns + nl])))
        at += ns + nl
    return started


def _send_start(name, srcs, lands, mode, dep=None):
    return _send_start_groups(name, [(srcs, lands)], mode, dep=dep)[0]


def _send_wait(name, started, mode, dep=None):
    _, send_sems, recv_sems, srcs, lands = started
    n_src, n = len(srcs), len(lands)

    def body(*refs):
        src_refs, land_refs = refs[:n_src], refs[n_src:n_src + n]
        ssem, rsem = refs[n_src + n], refs[n_src + n + 1]
        remote, local = _send_copies(mode, src_refs, land_refs, ssem, rsem)
        for cp in remote:
            cp.wait_send()
            cp.wait_recv()
        for cp in local:
            cp.wait()

    thru = [pltpu.HBM(t.shape, t.dtype) for t in [*srcs, *lands]]
    out = _pcall(
        body, name=name, ins=[*srcs, *lands, send_sems, recv_sems], in_specs=[HBM] * (n_src + n) + [SEM, SEM], dep=dep,
        out_shape=tuple(thru), out_specs=tuple([HBM] * (n_src + n)),
        input_output_aliases={i: i for i in range(n_src + n)},
        compiler_params=pltpu.CompilerParams(has_side_effects=pltpu.SideEffectType.DATAFLOW_SIDE_EFFECTING))
    return list(out[n_src:])


def _landing(block):
    return lax.empty((NDEV, *block.shape), block.dtype)


def _rows128(t):
    flat = t.reshape(-1)
    n = flat.shape[0]
    rows = -(-n // (8 * LANES)) * 8
    return jnp.pad(flat, (0, rows * LANES - n)).reshape(rows, LANES)


def kernel(x, mix_norm_g, ffn_norm_g, a_w_in, a_ln_g, a_w_spatial, a_b_spatial, a_w_out, kv_norm_g, w_k, w_v, b_w_q, b_sinks, b_w_o, rel_bias, ffn_w1, ffn_w2, final_norm_g, loss_target, m_mix_norm_g, m_ffn_norm_g, m_a_w_in, m_a_ln_g, m_a_w_spatial, m_a_b_spatial, m_a_w_out, m_kv_norm_g, m_w_k, m_w_v, m_b_w_q, m_b_sinks, m_b_w_o, m_rel_bias, m_ffn_w1, m_ffn_w2, m_final_norm_g, v_mix_norm_g, v_ffn_norm_g, v_a_w_in, v_a_ln_g, v_a_w_spatial, v_a_b_spatial, v_a_w_out, v_kv_norm_g, v_w_k, v_w_v, v_b_w_q, v_b_sinks, v_b_w_o, v_rel_bias, v_ffn_w1, v_ffn_w2, v_final_norm_g):
    c = _config(x, a_w_in, a_w_out, w_k, b_w_q, b_w_o, ffn_w1, ffn_w2)
    S, D, LA, LB, LF = c.S, c.D, c.LA, c.LB, c.LF
    weights = dict(mix_norm_g=mix_norm_g, ffn_norm_g=ffn_norm_g, a_w_in=a_w_in, a_ln_g=a_ln_g, a_w_spatial=a_w_spatial,
                   a_b_spatial=a_b_spatial, a_w_out=a_w_out, kv_norm_g=kv_norm_g, w_k=w_k, w_v=w_v, b_w_q=b_w_q,
                   b_sinks=b_sinks, b_w_o=b_w_o, rel_bias=rel_bias, ffn_w1=ffn_w1, ffn_w2=ffn_w2,
                   final_norm_g=final_norm_g)
    m_in = dict(mix_norm_g=m_mix_norm_g, ffn_norm_g=m_ffn_norm_g, a_w_in=m_a_w_in, a_ln_g=m_a_ln_g,
                a_w_spatial=m_a_w_spatial, a_b_spatial=m_a_b_spatial, a_w_out=m_a_w_out, kv_norm_g=m_kv_norm_g,
                w_k=m_w_k, w_v=m_w_v, b_w_q=m_b_w_q, b_sinks=m_b_sinks, b_w_o=m_b_w_o, rel_bias=m_rel_bias,
                ffn_w1=m_ffn_w1, ffn_w2=m_ffn_w2, final_norm_g=m_final_norm_g)
    v_in = dict(mix_norm_g=v_mix_norm_g, ffn_norm_g=v_ffn_norm_g, a_w_in=v_a_w_in, a_ln_g=v_a_ln_g,
                a_w_spatial=v_a_w_spatial, a_b_spatial=v_a_b_spatial, a_w_out=v_a_w_out, kv_norm_g=v_kv_norm_g,
                w_k=v_w_k, w_v=v_w_v, b_w_q=v_b_w_q, b_sinks=v_b_sinks, b_w_o=v_b_w_o, rel_bias=v_rel_bias,
                ffn_w1=v_ffn_w1, ffn_w2=v_ffn_w2, final_norm_g=v_final_norm_g)
    names = list(weights)
    seq = _Seq()
    me = _slot(*_place())
    bf = lambda t: t.astype(BF16)

    tr = lambda t: bf(jnp.swapaxes(t, -1, -2))
    groups = []
    for l in range(LA):
        groups += [[tr(a_w_in[l])[None]] + ([a_ln_g] if l == 0 else []), [bf(a_w_out[l])],
                   [tr(ffn_w1[l])[None], bf(ffn_w2[l])]]
    gb = 3 * LA
    for l in range(LB):
        extra = [bf(jnp.concatenate([w_k, w_v], axis=1))] if l == 0 else []
        groups += [extra + [bf(b_w_q[l]), bf(b_w_o[l])], [tr(ffn_w1[LA + l])[None], bf(ffn_w2[LA + l])]]
    started = seq(_send_start_groups, "weights_start", [(grp, [_landing(t) for t in grp]) for grp in groups],
                  "gather")
    seq.last = started[0][0]
    forwarding = {}

    def forward(i):
        lands = seq(_send_wait, f"weights_wait{i}", started[i], "gather")
        forwarding[i] = seq(_send_start, f"weights_forward{i}", [], lands, "forward")

    def arrive(i):
        if i not in forwarding:
            forward(i)
        return seq(_send_wait, f"weights_arrive{i}", forwarding[i], "forward")

    causal = jnp.tril(jnp.ones((CHUNK, CHUNK), bool))
    wsp = jnp.where(causal[None, None], a_w_spatial, 0.0)
    wsp16 = wsp.astype(BF16)
    wsp16_t = jnp.swapaxes(wsp, -1, -2).astype(BF16)
    bsp_t = jnp.swapaxes(a_b_spatial, -1, -2)
    mix_g = mix_norm_g.reshape(-1, 1, D)
    ffn_g = ffn_norm_g.reshape(-1, 1, D)
    kv_g = kv_norm_g.reshape(1, D)
    fin_g = final_norm_g.reshape(1, D)
    onehot = _bucket_onehot()
    bias = _slot_bias(seq(_band_bias, rel_bias.T, onehot).reshape(N_HEADS, BLOCK, 2 * BLOCK))

    h = x.reshape(S, D)
    sav_a, sav_b, wts_a, wts_b = [], [], [], []
    for l in range(LA):
        got = arrive(3 * l)
        w_in = got[0]
        if l == 0:
            ln_g_full = jnp.transpose(got[1], (1, 0, 2)).reshape(LA, 1, c.AW)
        z, a, hn = seq(_a_in_fwd, c, f"a_in_fwd{l}", h, mix_g[l], w_in, 0)
        forward(3 * l + 1)
        gated = seq(_sgu_fwd, c, f"sgu_fwd{l}", a, ln_g_full[l], wsp16[l], bsp_t[l])
        (wout,) = arrive(3 * l + 1)
        if l > 0:
            forward(3 * l + 2)
        h1 = seq(_mm_res, c, f"a_out_fwd{l}", gated, wout, c.ar, 0, h)
        w1, rows = arrive(3 * l + 2)
        if l == LA - 1:
            forward(gb)
        p, h2, hnf = seq(_ffn_fwd, c, f"ffn_fwd{l}", h1, ffn_g[l], w1, 0, rows)
        sav_a.append((h, z, a, hn, gated, h1, p, hnf))
        wts_a.append((w_in, 0, w1, 0, rows, wout, 0))
        h = h2
    h_kv = h
    for l in range(LB):
        got = arrive(gb + 2 * l)
        if l == 0:
            wkv, got = got[0], got[1:]
            kv, hkv = seq(_rms_mm_rows, c, "kv_fwd", h, kv_g, wkv, c.kr, 0, 2 * c.DKV)
        wq, wo = _slot_cols(got[0]), _slot_rows(got[1])
        q, hn = seq(_rms_mm_rows, c, f"q_fwd{l}", h, mix_g[LA + l], wq, c.qr, 0, c.DQ)
        forward(gb + 2 * l + 1)
        o, probs = seq(_attn_fwd, c, f"attn_fwd{l}", q, kv, bias, b_sinks[l])
        h1 = seq(_mm_res, c, f"o_fwd{l}", o, wo, c.orr, 0, h)
        w1, rows = arrive(gb + 2 * l + 1)
        if l + 1 < LB:
            forward(gb + 2 * l + 2)
        p, h2, hnf = seq(_ffn_fwd, c, f"ffn_fwd{LA + l}", h1, ffn_g[LA + l], w1, 0, rows)
        sav_b.append((h, q, hn, o, probs, h1, p, hnf))
        wts_b.append((wq, wo, w1, rows))
        h = h2
    dh, d_fin_g, loss_row = seq(_final_loss, c, h, fin_g, loss_target.reshape(S, D))

    results = {}
    in_flight = []

    def update(k, parts, layer, col_blk=0):
        w = weights[k]
        rows_l, ncols = (w.shape[-2], w.shape[-1]) if w.ndim == 3 else w.shape
        flat = lambda t: t.reshape(-1, ncols)
        tr = min(256, rows_l)
        results[k] = seq(_adamw, f"adamw_{k}{layer}", parts, (NDEV, tr, ncols), lambda i: (0, i, col_blk),
                         flat(w), flat(m_in[k]), flat(v_in[k]), tr, row_off=layer * rows_l, n_rows=rows_l,
                         prev=results.get(k))

    def land(tag, entry):
        lands = seq(_send_wait, f"grads_wait_{tag}", entry[1], "exchange")
        for keys, parts in zip(entry[0], lands):
            for k, layer, col_blk in keys:
                update(k, parts, layer, col_blk)

    def send(tag, items):
        slabs = [t for _, t in items]
        own = [lax.empty(t.shape, t.dtype) for t in slabs]
        st = seq(_send_start, f"grads_start_{tag}", slabs, own, "exchange")
        in_flight.append((tag, ([k for k, _ in items], st)))
        while len(in_flight) > EXCHANGE_LAG:
            land(*in_flight.pop(0))

    d_mix_g, d_ffn_g = [None] * LF, [None] * LF
    dkv_list, dbias_list, dsink_list = [], [], [None] * LB

    def ffn_bwd(lf, dh, h1, p, hnf, w1, w1_i, rows):
        da, dh1, d_ffn_g[lf], dhb = seq(_ffn_bwd_data, c, f"ffn_bwd_data{lf}", dh, p, w1, w1_i, rows, h1, ffn_g[lf])
        dw1, dw2 = seq(_ffn_bwd_w, c, f"ffn_bwd_w{lf}", hnf, da, p, dhb)
        send(f"ffn{lf}", [([("ffn_w1", lf, 0)], dw1), ([("ffn_w2", lf, 0)], dw2)])
        return dh1

    for l in reversed(range(LB)):
        h0, q, hn, o, probs, h1, p, hnf = sav_b[l]
        wq, wo, w1, rows = wts_b[l]
        dh1 = ffn_bwd(LA + l, dh, h1, p, hnf, w1, 0, rows)
        do = seq(_bwd_rows_data, c, f"o_bwd_data{l}", dh1, wo, c.orr, 0)
        dwo = _unslot_rows(seq(_wgrad_rows, c, f"o_bwd_w{l}", o, [dh1], c.DQ, D))
        dq, dkv, dbias, dsink = seq(_attn_bwd, c, f"attn_bwd{l}", q, kv, do, probs)
        dsink_list[l] = dsink.reshape(Q_PER_KV, 2, KV_PAIRS).transpose(2, 1, 0).reshape(1, N_HEADS)
        dkv_list.append(dkv)
        dbias_list.append(_unslot_bias(dbias))
        dwq = _unslot_cols(seq(_wgrad_rows, c, f"q_bwd_w{l}", hn, [dq], D, c.DQ))
        send(f"attn{l}", [([("b_w_o", l, 0)], dwo), ([("b_w_q", l, 0)], dwq)])
        dh, d_mix_g[LA + l] = seq(_bwd_rows_to_stream, c, f"q_bwd_data{l}", [dq], wq, c.qr, 0, c.DQ, h0,
                                  mix_g[LA + l], dh1)
    dwkv = seq(_wgrad_rows, c, "kv_bwd_w", hkv, dkv_list, D, 2 * c.DKV)
    send("kv", [([("w_k", 0, 0), ("w_v", 0, 1)], dwkv)])
    dh, d_kv_g = seq(_bwd_rows_to_stream, c, "kv_bwd_data", dkv_list, wkv, c.kr, 0, 2 * c.DKV, h_kv, kv_g, dh)
    d_rel_t = seq(_band_bias_grad, dbias_list, onehot)
    d_wsp, d_bsp, d_lng = [None] * LA, [None] * LA, [None] * LA
    for l in reversed(range(LA)):
        h0, z, a, hn, gated, h1, p, hnf = sav_a[l]
        w_in, in_i, w1, w1_i, rows, wout, wout_i = wts_a[l]
        dh1 = ffn_bwd(l, dh, h1, p, hnf, w1, w1_i, rows)
        dgated = seq(_bwd_rows_data, c, f"a_out_bwd_data{l}", dh1, wout, c.ar, wout_i)
        dwout = seq(_wgrad_rows, c, f"a_out_bwd_w{l}", gated, [dh1], c.AW, D)
        send(f"a_out{l}", [([("a_w_out", l, 0)], dwout)])
        dz, d_wsp[l], dbt, d_lng[l] = seq(_sgu_bwd, c, f"sgu_bwd{l}", a, z, dgated, ln_g_full[l], wsp16[l],
                                          wsp16_t[l], bsp_t[l])
        d_bsp[l] = dbt.T
        dwin = seq(_wgrad_cols, c, f"a_in_bwd_w{l}", hn, dz)
        send(f"a_in{l}", [([("a_w_in", l, 0)], dwin)])
        dh, d_mix_g[l] = seq(_bwd_cols_to_stream, c, f"a_in_bwd_data{l}", dz, w_in, in_i, h0, mix_g[l], dh1)
    grad_x = dh.reshape(1, S, D)

    small = {
        "mix_norm_g": jnp.concatenate(d_mix_g, axis=0), "ffn_norm_g": jnp.concatenate(d_ffn_g, axis=0),
        "a_w_spatial": jnp.stack(d_wsp), "a_b_spatial": jnp.stack(d_bsp), "kv_norm_g": d_kv_g,
        "b_sinks": jnp.concatenate(dsink_list, axis=0), "rel_bias": d_rel_t.T, "final_norm_g": d_fin_g,
    }
    small_names = list(small)
    packs = [_rows128(small[k]) for k in small_names] + [_rows128(jnp.concatenate(d_lng, axis=0)), _rows128(loss_row)]
    offs = [int(o) for o in np.cumsum([0] + [p.shape[0] for p in packs])]
    Rs = offs[-1] + (-offs[-1]) % (8 * NDEV)
    tail_rows = Rs - offs[len(small_names)]
    packed = jnp.concatenate(packs + [jnp.zeros((Rs - offs[-1], LANES), F32)], axis=0)
    slab = packed.reshape(NDEV, Rs // NDEV, LANES)
    st = seq(_send_start, "small_grads_start", [slab], [lax.empty(slab.shape, slab.dtype)], "exchange")
    while len(in_flight) > 1:
        land(*in_flight.pop(0))
    (parts,) = seq(_send_wait, "small_grads_wait", st, "exchange")
    mine = seq(_sum_parts, "small_grads_sum", parts)
    st = seq(_send_start, "small_sums_start", [mine], [_landing(mine)], "broadcast")
    while in_flight:
        land(*in_flight.pop(0))
    (sums,) = seq(_send_wait, "small_sums_wait", st, "broadcast")
    small_all = sums.reshape(1, Rs, LANES)
    loss = sums.reshape(Rs, LANES)[offs[-2], 0]

    grads, deltas, new_m, new_v = {}, {}, {}, {}

    def put(k, outs, shape):
        grads[k], deltas[k], new_m[k], new_v[k] = (t.reshape(shape) for t in outs)

    def pack_state(d):
        return jnp.concatenate([_rows128(d[k]) for k in small_names] + [jnp.zeros((tail_rows, LANES), F32)], axis=0)

    outs = seq(_adamw, "adamw_small", small_all, (1, Rs, LANES), lambda i: (0, 0, 0),
               pack_state(weights), pack_state(m_in), pack_state(v_in), Rs)
    for n_, k in enumerate(small_names):
        shape = weights[k].shape
        size = int(np.prod(shape))
        put(k, [t[offs[n_]:offs[n_ + 1]].reshape(-1)[:size] for t in outs], shape)
    lng_sum = outs[0][offs[-3]:offs[-2]].reshape(-1)[:LA * c.AW].reshape(LA, c.AW)
    lng_mine = lax.dynamic_slice_in_dim(lng_sum, me * c.ar, c.ar, axis=1)
    lng_parts = jnp.concatenate([lng_mine[None], jnp.zeros((NDEV - 1, LA, c.ar), F32)], axis=0)
    put("a_ln_g", seq(_adamw, "adamw_ln_g", lng_parts, (NDEV, LA, c.ar), lambda i: (0, 0, 0),
                      a_ln_g, m_in["a_ln_g"], v_in["a_ln_g"], LA), a_ln_g.shape)
    for k in ("a_w_in", "ffn_w1", "ffn_w2", "a_w_out", "b_w_o", "b_w_q", "w_k", "w_v"):
        put(k, results[k], weights[k].shape)

    return (loss, grad_x, *[grads[k] for k in names], *[deltas[k] for k in names],
            *[new_m[k] for k in names], *[new_v[k] for k in names])
```

```python
import numpy as np
import math
import jax
import jax.numpy as jnp
from jax import lax
from jax.experimental import pallas as pl
from jax.experimental.pallas import tpu as pltpu

F32 = jnp.float32
BF16 = jnp.bfloat16

NDEV = 8
EPS = 1e-6
CHUNK = 128
A_GROUPS = 8
N_HEADS = 16
N_KV_HEADS = 4
Q_PER_KV = N_HEADS // N_KV_HEADS
HEAD_DIM = 64
BLOCK = 128
N_BUCKETS = 32
MAX_DISTANCE = 128
ADAM_LR, ADAM_B1, ADAM_B2, ADAM_EPS, ADAM_WD, ADAM_STEP = 0.001, 0.9, 0.999, 1e-08, 0.01, 10
LANES = 128
VMEM_LIMIT = 56 * 1024 * 1024
INV_SQRT2 = 0.7071067811865476
INV_SQRT_2PI = 0.3989422804014327
EXCHANGE_LAG = 4

HBM = pl.BlockSpec(memory_space=pltpu.HBM)
SMEM = pl.BlockSpec(memory_space=pltpu.SMEM)
ANY = pl.BlockSpec(memory_space=pl.ANY)
SEM = pl.BlockSpec(memory_space=pltpu.SEMAPHORE)
MESH = pl.DeviceIdType.MESH


def _params(n_grid):
    return pltpu.CompilerParams(dimension_semantics=("arbitrary",) * n_grid, vmem_limit_bytes=VMEM_LIMIT)


def _const(block, index_map):
    return pl.BlockSpec(block, index_map, pipeline_mode=pl.Buffered(1))


def _pcall(body, *, ins, in_specs, dep=None, **kw):
    n_in = len(ins)
    if dep is None or any(dep is t for t in ins):
        return pl.pallas_call(body, in_specs=list(in_specs), **kw)(*ins)

    def with_dep(*refs):
        body(*refs[:n_in], *refs[n_in + 1:])

    return pl.pallas_call(with_dep, in_specs=[*in_specs, ANY], **kw)(*ins, dep)


class _Seq:
    def __init__(self):
        self.last = None

    def __call__(self, fn, *args, **kw):
        out = fn(*args, dep=self.last, **kw)
        self.last = out[0] if isinstance(out, (tuple, list)) else out
        return out


def _rstd(h):
    return lax.rsqrt(jnp.mean(h * h, axis=-1, keepdims=True) + EPS)


def _rms_bwd(dhn, h, g, dres):
    r = _rstd(h)
    xh = h * r
    dg = jnp.sum(dhn * xh, axis=0, keepdims=True)
    dxh = dhn * g
    dx = r * (dxh - xh * jnp.mean(dxh * xh, axis=-1, keepdims=True))
    return dres + dx, dg


def _gelu(z):
    return 0.5 * z * (1.0 + lax.erf(z * INV_SQRT2))


def _gelu_grad(z):
    return 0.5 * (1.0 + lax.erf(z * INV_SQRT2)) + z * (jnp.exp(-0.5 * z * z) * INV_SQRT_2PI)


def _dot(a, b, dims):
    return lax.dot_general(a, b, (dims, ((), ())), preferred_element_type=F32)


NN = ((1,), (0,))
NT = ((1,), (1,))
TN = ((0,), (0,))


def _mm(name, ins, in_specs, out_shapes, out_specs, *, grid, dims, nk, acc_shape, load_a, load_b, epilogue,
        dep=None):
    n_in, n_out = len(ins), len(out_shapes)
    kax = len(grid) - 1

    def body(*refs):
        in_refs = refs[:n_in]
        out_refs = refs[n_in:n_in + n_out]
        a = load_a(in_refs, out_refs)
        b = load_b(in_refs)
        prod = _dot(a, b, dims)
        if nk == 1:
            epilogue(prod, in_refs, out_refs)
        else:
            acc = refs[n_in + n_out]
            k = pl.program_id(kax)

            @pl.when(k == 0)
            def _():
                acc[...] = prod

            @pl.when(k > 0)
            def _():
                acc[...] += prod

            @pl.when(k == nk - 1)
            def _():
                epilogue(acc[...], in_refs, out_refs)

    return _pcall(
        body, name=name, ins=ins, in_specs=in_specs, dep=dep, grid=grid, out_specs=out_specs, out_shape=out_shapes,
        scratch_shapes=[pltpu.VMEM(acc_shape, F32)] if nk > 1 else [], compiler_params=_params(len(grid)))


def _bf(ref_idx):
    return lambda in_refs, *_: in_refs[ref_idx][...].astype(BF16)


def _b_view(ref_idx, rows):
    def load(in_refs):
        b = in_refs[ref_idx][...]
        return b.reshape(rows, b.shape[-1])
    return load


class Cfg:
    pass


def _config(x, a_w_in, a_w_out, w_k, b_w_q, b_w_o, ffn_w1, ffn_w2):
    c = Cfg()
    c.S, c.D = x.shape[1], x.shape[2]
    c.LA, _, c.cw = a_w_in.shape
    c.AW2 = NDEV * c.cw
    c.AW = c.AW2 // 2
    c.gd = c.AW // A_GROUPS
    c.ar = a_w_out.shape[1]
    c.LF, _, c.fw = ffn_w1.shape
    c.fr = ffn_w2.shape[1]
    c.LB, c.qr, c.DQ = b_w_q.shape
    c.orr = b_w_o.shape[1]
    c.kr, c.DKV = w_k.shape
    c.tm = min(1024, c.S)
    c.tmw = min(512, c.S)
    c.tms = min(256, c.S)
    c.nb = c.S // BLOCK
    assert c.cw == c.fw == c.fr and c.AW == NDEV * c.ar and c.D == NDEV * c.qr == NDEV * c.kr
    assert c.DQ == NDEV * c.orr == N_HEADS * HEAD_DIM and c.DKV == N_KV_HEADS * HEAD_DIM
    assert c.S % c.tm == 0 and c.S % c.tmw == 0 and c.S % c.tms == 0 and c.tms % CHUNK == 0 and c.gd % LANES == 0
    assert c.LA >= 1 and c.LB >= 1 and c.LF == c.LA + c.LB
    return c


def _cached_rms(h_idx, g_idx, hn_out_idx, jax_axis=1):
    def load(in_refs, out_refs):
        hn_ref = out_refs[hn_out_idx]

        @pl.when(pl.program_id(jax_axis) == 0)
        def _():
            h = in_refs[h_idx][...]
            hn_ref[...] = (h * _rstd(h) * in_refs[g_idx][...]).astype(BF16)

        return hn_ref[...]
    return load


def _a_in_fwd(c, name, h, g, col, ci, dep=None):
    S, D, cw, tm = c.S, c.D, c.cw, c.tmw

    def body(h_ref, g_ref, w_ref, z_ref, a_ref, hn_ref):
        h = h_ref[...]
        hn = (h * _rstd(h) * g_ref[...]).astype(BF16)
        hn_ref[...] = hn
        for j in range(NDEV):
            cols = slice(j * cw, (j + 1) * cw)
            z = _dot(hn, w_ref[j], NT)
            z_ref[:, cols] = z.astype(BF16)
            a_ref[:, cols] = _gelu(z).astype(BF16)

    row = pl.BlockSpec((tm, D), lambda i: (i, 0))
    wide = pl.BlockSpec((tm, c.AW2), lambda i: (i, 0))
    return _pcall(
        body, name=name, ins=[h, g, col], dep=dep, grid=(S // tm,),
        in_specs=[row, pl.BlockSpec((1, D), lambda i: (0, 0)), _const((NDEV, None, cw, D), lambda i: (0, ci, 0, 0))],
        out_specs=[wide, wide, row],
        out_shape=[jax.ShapeDtypeStruct((S, c.AW2), BF16), jax.ShapeDtypeStruct((S, c.AW2), BF16),
                   jax.ShapeDtypeStruct((S, D), BF16)],
        compiler_params=_params(1))


def _rms_mm_rows(c, name, h, g, slab, blk_rows, blk_idx, n_out, dep=None):
    S, D, tm = c.S, c.D, c.tm

    def epilogue(acc, in_refs, out_refs):
        out_refs[0][...] = acc.astype(BF16)

    return _mm(
        name, [h, slab, g],
        [pl.BlockSpec((tm, D), lambda i, j, k: (i, 0)),
         pl.BlockSpec((NDEV, blk_rows, n_out), lambda i, j, k: (0, blk_idx, 0)),
         pl.BlockSpec((1, D), lambda i, j, k: (0, 0))],
        [jax.ShapeDtypeStruct((S, n_out), BF16), jax.ShapeDtypeStruct((S, D), BF16)],
        [pl.BlockSpec((tm, n_out), lambda i, j, k: (i, 0)), pl.BlockSpec((tm, D), lambda i, j, k: (i, 0))],
        grid=(S // tm, 1, 1), dims=NN, nk=1, acc_shape=None,
        load_a=_cached_rms(0, 2, 1), load_b=_b_view(1, NDEV * blk_rows), epilogue=epilogue, dep=dep)


def _mm_res(c, name, a, slab, blk_rows, blk_idx, res, dep=None):
    S, D, tm = c.S, c.D, c.tm
    K = NDEV * blk_rows

    def epilogue(acc, in_refs, out_refs):
        out_refs[0][...] = in_refs[2][...] + acc

    return _mm(
        name, [a, slab, res],
        [pl.BlockSpec((tm, K), lambda i, j, k: (i, 0)),
         pl.BlockSpec((NDEV, blk_rows, D), lambda i, j, k: (0, blk_idx, 0)),
         pl.BlockSpec((tm, D), lambda i, j, k: (i, 0))],
        [jax.ShapeDtypeStruct((S, D), F32)], [pl.BlockSpec((tm, D), lambda i, j, k: (i, 0))],
        grid=(S // tm, 1, 1), dims=NN, nk=1, acc_shape=None,
        load_a=_bf(0), load_b=_b_view(1, K), epilogue=epilogue, dep=dep)[0]


def _sgu_masks():
    ii = lax.broadcasted_iota(jnp.int32, (CHUNK, CHUNK), 0)
    jj = lax.broadcasted_iota(jnp.int32, (CHUNK, CHUNK), 1)
    return ii >= jj


def _sgu_fwd(c, name, a, ln_g, wc, b_t, dep=None):
    S, AW, gd, tm = c.S, c.AW, c.gd, c.tms

    def body(a_ref, lng_ref, wc_ref, bt_ref, out_ref):
        va = a_ref[:, AW:].astype(F32)
        xc = va - jnp.mean(va, axis=-1, keepdims=True)
        vn = (xc * lax.rsqrt(jnp.mean(xc * xc, axis=-1, keepdims=True) + EPS) * lng_ref[...]).astype(BF16)
        for ch in range(tm // CHUNK):
            rows = slice(ch * CHUNK, (ch + 1) * CHUNK)
            for g in range(A_GROUPS):
                cols = slice(g * gd, (g + 1) * gd)
                mixed = _dot(wc_ref[g], vn[rows, cols], NN) + bt_ref[:, g:g + 1]
                out_ref[rows, cols] = (a_ref[rows, cols].astype(F32) * mixed).astype(BF16)

    return _pcall(
        body, name=name, ins=[a, ln_g, wc, b_t], dep=dep, grid=(S // tm,),
        in_specs=[pl.BlockSpec((tm, 2 * AW), lambda i: (i, 0)), pl.BlockSpec((1, AW), lambda i: (0, 0)),
                  pl.BlockSpec((A_GROUPS, CHUNK, CHUNK), lambda i: (0, 0, 0)),
                  pl.BlockSpec((CHUNK, A_GROUPS), lambda i: (0, 0))],
        out_specs=pl.BlockSpec((tm, AW), lambda i: (i, 0)),
        out_shape=jax.ShapeDtypeStruct((S, AW), BF16), compiler_params=_params(1))


def _ffn_fwd(c, name, h, g, col, ci, rows, dep=None):
    S, D, fw, tm = c.S, c.D, c.fw, c.tmw
    F = NDEV * fw

    def body(h_ref, g_ref, w1_ref, w2_ref, p_ref, out_ref, hn_ref, r_ref):
        h = h_ref[...]
        hn = (h * _rstd(h) * g_ref[...]).astype(BF16)
        hn_ref[...] = hn
        for j in range(NDEV):
            cols = slice(j * fw, (j + 1) * fw)
            p = jnp.maximum(_dot(hn, w1_ref[j], NT), 0.0)
            p_ref[:, cols] = p.astype(BF16)
            r_ref[:, cols] = (p * p).astype(BF16)
        out_ref[...] = h + _dot(r_ref[...], w2_ref[...].reshape(F, D), NN)

    row = pl.BlockSpec((tm, D), lambda i: (i, 0))
    return _pcall(
        body, name=name, ins=[h, g, col, rows], dep=dep, grid=(S // tm,),
        in_specs=[row, pl.BlockSpec((1, D), lambda i: (0, 0)),
                  _const((NDEV, None, fw, D), lambda i: (0, ci, 0, 0)), _const((NDEV, c.fr, D), lambda i: (0, 0, 0))],
        out_specs=[pl.BlockSpec((tm, F), lambda i: (i, 0)), row, row],
        out_shape=[jax.ShapeDtypeStruct((S, F), BF16), jax.ShapeDtypeStruct((S, D), F32),
                   jax.ShapeDtypeStruct((S, D), BF16)],
        scratch_shapes=[pltpu.VMEM((tm, F), BF16)], compiler_params=_params(1))


def _bucket_table():
    qi = np.arange(BLOCK)[:, None]
    kj = np.arange(2 * BLOCK)[None, :]
    d = np.maximum(qi + BLOCK - kj, 0)
    max_exact = N_BUCKETS // 2
    ratio = np.log(np.maximum(d, 1).astype(np.float32) / np.float32(max_exact)) / np.float32(
        math.log(MAX_DISTANCE / max_exact))
    large = np.minimum(max_exact + (ratio.astype(np.float32) * np.float32(N_BUCKETS - max_exact)).astype(np.int32),
                       N_BUCKETS - 1)
    return np.where(d < max_exact, d, large).astype(np.int32)


def _bucket_onehot():
    b = jnp.asarray(_bucket_table().reshape(1, -1))
    return (b == lax.broadcasted_iota(jnp.int32, (N_BUCKETS, b.shape[1]), 0)).astype(F32)


def _whole(t):
    return pl.BlockSpec(t.shape, lambda: (0,) * t.ndim)


def _band_bias(rel_bias_t, onehot, dep=None):
    def body(r_ref, oh_ref, out_ref):
        out_ref[...] = lax.dot_general(r_ref[...], oh_ref[...], (NN, ((), ())), preferred_element_type=F32,
                                       precision=lax.Precision.HIGHEST)

    n = onehot.shape[1]
    return _pcall(body, name="band_bias", ins=[rel_bias_t, onehot], in_specs=[_whole(rel_bias_t), _whole(onehot)],
                  dep=dep, out_shape=jax.ShapeDtypeStruct((N_HEADS, n), F32), compiler_params=_params(0))


def _band_bias_grad(dbias_list, onehot, dep=None):
    n_in = len(dbias_list)

    def body(*refs):
        oh_ref, out_ref = refs[n_in], refs[n_in + 1]
        d = refs[0][...]
        for r in refs[1:n_in]:
            d = d + r[...]
        out_ref[...] = lax.dot_general(d, oh_ref[...], (NT, ((), ())), preferred_element_type=F32,
                                       precision=lax.Precision.HIGHEST)

    ins = [*dbias_list, onehot]
    return _pcall(body, name="band_bias_grad", ins=ins, in_specs=[_whole(t) for t in ins], dep=dep,
                  out_shape=jax.ShapeDtypeStruct((N_HEADS, N_BUCKETS), F32), compiler_params=_params(0))


KV_PAIRS = N_KV_HEADS // 2
PAIR_ROWS = 2 * Q_PER_KV * BLOCK
MASKED = float(np.finfo(np.float32).min) / 2


def _slot_cols(w):
    lead = w.shape[:-1]
    return w.reshape(*lead, KV_PAIRS, 2, Q_PER_KV, HEAD_DIM).swapaxes(-3, -2).reshape(*lead, N_HEADS * HEAD_DIM)


def _unslot_cols(w):
    lead = w.shape[:-1]
    return w.reshape(*lead, KV_PAIRS, Q_PER_KV, 2, HEAD_DIM).swapaxes(-3, -2).reshape(*lead, N_HEADS * HEAD_DIM)


def _slot_rows(blocks):
    n = blocks.shape[-1]
    return blocks.reshape(KV_PAIRS, 2, Q_PER_KV, HEAD_DIM, n).swapaxes(1, 2).reshape(blocks.shape)


def _unslot_rows(blocks):
    n = blocks.shape[-1]
    return blocks.reshape(KV_PAIRS, Q_PER_KV, 2, HEAD_DIM, n).swapaxes(1, 2).reshape(blocks.shape)


def _slot_bias(bias):
    qi = np.arange(BLOCK)[:, None]
    kj = np.arange(2 * BLOCK)[None, :]
    dist = qi + BLOCK - kj
    window = (dist >= 0) & (dist < BLOCK)
    b = bias.reshape(KV_PAIRS, 2, Q_PER_KV, BLOCK, 2 * BLOCK).swapaxes(1, 2).reshape(KV_PAIRS, PAIR_ROWS, 2 * BLOCK)
    tile = lambda mk: jnp.asarray(np.tile(mk, (2 * Q_PER_KV, 1)))[None]
    return jnp.stack([jnp.where(tile(window & (kj >= BLOCK)), b, MASKED), jnp.where(tile(window), b, MASKED)])


def _unslot_bias(db):
    return db.reshape(KV_PAIRS, Q_PER_KV, 2, BLOCK, 2 * BLOCK).swapaxes(1, 2).reshape(N_HEADS, -1)


def _pair_kv(kvc_ref, kvp_ref, kvp, dkv):
    lanes = slice(kvp * LANES, (kvp + 1) * LANES)
    vlanes = slice(dkv + kvp * LANES, dkv + (kvp + 1) * LANES)
    k2 = jnp.concatenate([kvp_ref[:, lanes], kvc_ref[:, lanes]], axis=0)
    v2 = jnp.concatenate([kvp_ref[:, vlanes], kvc_ref[:, vlanes]], axis=0)
    return k2, v2


def _head_operand(ref, grp, par, low, scale=None):
    xg = ref[:, grp * LANES:(grp + 1) * LANES]
    if scale is not None:
        xg = xg * scale
    zero = jnp.zeros_like(xg)
    return jnp.where(low, xg, zero) if par == 0 else jnp.where(low, zero, xg)


def _head_probs(qh, k2, bias_rows, sink):
    s = _dot(qh, k2, NT) + bias_rows
    m = jnp.maximum(jnp.max(s, axis=-1, keepdims=True), sink)
    e = jnp.exp(s - m)
    es = jnp.exp(sink - m)
    inv = 1.0 / (jnp.sum(e, axis=-1, keepdims=True) + es)
    return e * inv, es * inv


def _attn_specs(c):
    dq, dkv2 = c.DQ, 2 * c.DKV
    return [pl.BlockSpec((BLOCK, dq), lambda n: (n, 0)),
            pl.BlockSpec((BLOCK, dkv2), lambda n: (n, 0)),
            pl.BlockSpec((BLOCK, dkv2), lambda n: (jnp.maximum(n - 1, 0), 0))]


def _bias_spec():
    return pl.BlockSpec((None, KV_PAIRS, PAIR_ROWS, 2 * BLOCK), lambda n: (jnp.minimum(n, 1), 0, 0, 0))


def _low_lanes():
    return lax.broadcasted_iota(jnp.int32, (BLOCK, LANES), 1) < HEAD_DIM


def _first_key():
    return lax.broadcasted_iota(jnp.int32, (BLOCK, 2 * BLOCK), 1) == 0


def _probs_spec():
    return pl.BlockSpec((None, KV_PAIRS, PAIR_ROWS, 2 * BLOCK), lambda n: (n, 0, 0, 0))


def _attn_fwd(c, name, q, kv, bias, sinks, dep=None):
    S, dq = c.S, c.DQ

    def body(q_ref, kvc_ref, kvp_ref, bias_ref, sink_ref, o_ref, probs_ref):
        low = _low_lanes()
        first = _first_key()
        for kvp in range(KV_PAIRS):
            k2, v2 = _pair_kv(kvc_ref, kvp_ref, kvp, c.DKV)
            for g in range(Q_PER_KV):
                grp = kvp * Q_PER_KV + g
                halves = []
                for par in range(2):
                    rows = slice((2 * g + par) * BLOCK, (2 * g + par + 1) * BLOCK)
                    qh = _head_operand(q_ref, grp, par, low, scale=HEAD_DIM ** -0.5)
                    p, ps = _head_probs(qh, k2, bias_ref[kvp, rows, :], sink_ref[(2 * kvp + par) * Q_PER_KV + g])
                    probs_ref[kvp, rows, :] = jnp.where(first, ps, p).astype(BF16)
                    halves.append(_dot(p.astype(BF16), v2, NN))
                o_ref[:, grp * LANES:(grp + 1) * LANES] = jnp.where(low, halves[0], halves[1]).astype(BF16)

    return _pcall(
        body, name=name, ins=[q, kv, kv, bias, sinks], dep=dep, grid=(c.nb,),
        in_specs=_attn_specs(c) + [_bias_spec(), SMEM],
        out_specs=[pl.BlockSpec((BLOCK, dq), lambda n: (n, 0)), _probs_spec()],
        out_shape=[jax.ShapeDtypeStruct((S, dq), BF16),
                   jax.ShapeDtypeStruct((c.nb, KV_PAIRS, PAIR_ROWS, 2 * BLOCK), BF16)],
        compiler_params=_params(1))


def _final_loss(c, h, g, target, dep=None):
    S, D, tm = c.S, c.D, c.tm

    def body(h_ref, g_ref, t_ref, dh_ref, dg_ref, loss_ref):
        i = pl.program_id(0)
        h = h_ref[...]
        gg = g_ref[...]
        r = _rstd(h)
        xh = h * r
        err = xh * gg - t_ref[...]
        lp = jnp.sum(jnp.sum(err * err, axis=1, keepdims=True), axis=0, keepdims=True) * (0.5 / D)
        dx, dg = _rms_bwd(err * (1.0 / D), h, gg, 0.0)
        dh_ref[...] = dx

        @pl.when(i == 0)
        def _():
            dg_ref[...] = dg
            loss_ref[...] = jnp.broadcast_to(lp, loss_ref.shape)

        @pl.when(i > 0)
        def _():
            dg_ref[...] += dg
            loss_ref[...] += jnp.broadcast_to(lp, loss_ref.shape)

    row = pl.BlockSpec((tm, D), lambda i: (i, 0))
    return _pcall(
        body, name="final_loss", ins=[h, g, target], dep=dep, grid=(S // tm,),
        in_specs=[row, pl.BlockSpec((1, D), lambda i: (0, 0)), row],
        out_specs=[row, pl.BlockSpec((1, D), lambda i: (0, 0)), pl.BlockSpec((1, LANES), lambda i: (0, 0))],
        out_shape=[jax.ShapeDtypeStruct((S, D), F32), jax.ShapeDtypeStruct((1, D), F32),
                   jax.ShapeDtypeStruct((1, LANES), F32)],
        compiler_params=_params(1))


def _rms_bwd_epilogue(h_idx, g_idx, res_idx):
    def epilogue(dhn, in_refs, out_refs):
        dh, dg = _rms_bwd(dhn, in_refs[h_idx][...], in_refs[g_idx][...], in_refs[res_idx][...])
        out_refs[0][...] = dh
        i = pl.program_id(0)

        @pl.when(i == 0)
        def _():
            out_refs[1][...] = dg

        @pl.when(i > 0)
        def _():
            out_refs[1][...] += dg
    return epilogue


def _stream_outs(c, tm):
    S, D = c.S, c.D
    return ([jax.ShapeDtypeStruct((S, D), F32), jax.ShapeDtypeStruct((1, D), F32)],
            [pl.BlockSpec((tm, D), lambda i, j, k: (i, 0)), pl.BlockSpec((1, D), lambda i, j, k: (0, 0))])


def _row_specs(c, tm):
    D = c.D
    return [pl.BlockSpec((tm, D), lambda i, j, k: (i, 0)), pl.BlockSpec((1, D), lambda i, j, k: (0, 0)),
            pl.BlockSpec((tm, D), lambda i, j, k: (i, 0))]


def _bwd_rows_to_stream(c, name, dy_list, slab, blk_rows, blk_idx, n_in_cols, h, g, dres, dep=None):
    S, D, tm = c.S, c.D, c.tm
    nd = len(dy_list)

    def load_a(in_refs, out_refs):
        a = in_refs[0][...]
        for r in in_refs[1:nd]:
            a = a + r[...]
        return a.astype(BF16)

    shapes, specs = _stream_outs(c, tm)
    return _mm(
        name, [*dy_list, slab, h, g, dres],
        [pl.BlockSpec((tm, n_in_cols), lambda i, j, k: (i, 0))] * nd
        + [pl.BlockSpec((NDEV, blk_rows, n_in_cols), lambda i, j, k: (0, blk_idx, 0))] + _row_specs(c, tm),
        shapes, specs, grid=(S // tm, 1, 1), dims=NT, nk=1, acc_shape=None,
        load_a=load_a, load_b=_b_view(nd, NDEV * blk_rows), epilogue=_rms_bwd_epilogue(nd + 1, nd + 2, nd + 3),
        dep=dep)


def _bwd_cols_to_stream(c, name, dy, col, ci, h, g, dres, dep=None):
    S, D, cw, tm = c.S, c.D, c.cw, c.tmw
    K = NDEV * cw
    shapes, specs = _stream_outs(c, tm)
    return _mm(
        name, [dy, col, h, g, dres],
        [pl.BlockSpec((tm, K), lambda i, j, k: (i, 0)),
         _const((NDEV, None, cw, D), lambda i, j, k: (0, ci, 0, 0))] + _row_specs(c, tm),
        shapes, specs, grid=(S // tm, 1, 1), dims=NN, nk=1, acc_shape=None,
        load_a=_bf(0), load_b=_b_view(1, K), epilogue=_rms_bwd_epilogue(2, 3, 4), dep=dep)


def _bwd_rows_data(c, name, dy, slab, blk_rows, blk_idx, dep=None):
    S, D, tm = c.S, c.D, c.tm
    K = NDEV * blk_rows

    def epilogue(acc, in_refs, out_refs):
        out_refs[0][...] = acc.astype(BF16)

    return _mm(
        name, [dy, slab],
        [pl.BlockSpec((tm, D), lambda i, j, k: (i, 0)),
         pl.BlockSpec((NDEV, blk_rows, D), lambda i, j, k: (0, blk_idx, 0))],
        [jax.ShapeDtypeStruct((S, K), BF16)], [pl.BlockSpec((tm, K), lambda i, j, k: (i, 0))],
        grid=(S // tm, 1, 1), dims=NT, nk=1, acc_shape=None,
        load_a=_bf(0), load_b=_b_view(1, K), epilogue=epilogue, dep=dep)[0]


def _wgrad_rows(c, name, a, b_list, n_a, n_b, dep=None):
    S, tm = c.S, c.tm
    nb_in = len(b_list)
    blk_rows = n_a // NDEV

    def load_b(in_refs):
        b = in_refs[1][...]
        for r in in_refs[2:1 + nb_in]:
            b = b + r[...]
        return b.astype(BF16)

    def epilogue(acc, in_refs, out_refs):
        out_refs[0][...] = acc.reshape(NDEV, blk_rows, n_b).astype(BF16)

    return _mm(
        name, [a, *b_list],
        [pl.BlockSpec((tm, n_a), lambda i, j, k: (k, 0))] + [pl.BlockSpec((tm, n_b), lambda i, j, k: (k, 0))] * nb_in,
        [jax.ShapeDtypeStruct((NDEV, blk_rows, n_b), BF16)],
        [pl.BlockSpec((NDEV, blk_rows, n_b), lambda i, j, k: (0, 0, 0))],
        grid=(1, 1, S // tm), dims=TN, nk=S // tm, acc_shape=(n_a, n_b),
        load_a=_bf(0), load_b=load_b, epilogue=epilogue, dep=dep)[0]


def _wgrad_cols(c, name, a, b, dep=None):
    S, D, cw = c.S, c.D, c.cw

    def body(a_ref, b_ref, out_ref):
        out_ref[...] = _dot(a_ref[...], b_ref[...], TN).astype(BF16)

    return _pcall(
        body, name=name, ins=[a, b], dep=dep, grid=(NDEV,),
        in_specs=[_const((S, D), lambda j: (0, 0)), pl.BlockSpec((S, cw), lambda j: (0, j))],
        out_specs=pl.BlockSpec((None, D, cw), lambda j: (j, 0, 0)),
        out_shape=jax.ShapeDtypeStruct((NDEV, D, cw), BF16), compiler_params=_params(1))


def _ffn_bwd_data(c, name, dh, p, col, ci, rows, h, g, dep=None):
    S, D, fw, tm = c.S, c.D, c.fw, c.tmw
    F = NDEV * fw

    def body(dh_ref, p_ref, w1t_ref, w2_ref, h_ref, g_ref, da_ref, out_ref, dg_ref, dhb_ref):
        i = pl.program_id(0)
        dh = dh_ref[...]
        dhb = dh.astype(BF16)
        dhb_ref[...] = dhb
        for j in range(NDEV):
            cols = slice(j * fw, (j + 1) * fw)
            da_ref[:, cols] = (_dot(dhb, w2_ref[j], NT) * (2.0 * p_ref[:, cols].astype(F32))).astype(BF16)
        dx, dg = _rms_bwd(_dot(da_ref[...], w1t_ref[...].reshape(F, D), NN), h_ref[...], g_ref[...], dh)
        out_ref[...] = dx

        @pl.when(i == 0)
        def _():
            dg_ref[...] = dg

        @pl.when(i > 0)
        def _():
            dg_ref[...] += dg

    row = pl.BlockSpec((tm, D), lambda i: (i, 0))
    wide = pl.BlockSpec((tm, F), lambda i: (i, 0))
    return _pcall(
        body, name=name, ins=[dh, p, col, rows, h, g], dep=dep, grid=(S // tm,),
        in_specs=[row, wide, _const((NDEV, None, fw, D), lambda i: (0, ci, 0, 0)),
                  _const((NDEV, c.fr, D), lambda i: (0, 0, 0)),
                  row, pl.BlockSpec((1, D), lambda i: (0, 0))],
        out_specs=[wide, row, pl.BlockSpec((1, D), lambda i: (0, 0)), row],
        out_shape=[jax.ShapeDtypeStruct((S, F), BF16), jax.ShapeDtypeStruct((S, D), F32),
                   jax.ShapeDtypeStruct((1, D), F32), jax.ShapeDtypeStruct((S, D), BF16)],
        compiler_params=_params(1))


def _ffn_bwd_w(c, name, hn, da, p, dhb, dep=None):
    S, D, fw = c.S, c.D, c.fw

    def body(hn_ref, da_ref, p_ref, dhb_ref, dw1_ref, dw2_ref):
        dw1_ref[...] = _dot(hn_ref[...], da_ref[...], TN).astype(BF16)
        pf = p_ref[...].astype(F32)
        dw2_ref[...] = _dot((pf * pf).astype(BF16), dhb_ref[...], TN).astype(BF16)

    panel = pl.BlockSpec((S, fw), lambda j: (0, j))
    return _pcall(
        body, name=name, ins=[hn, da, p, dhb], dep=dep, grid=(NDEV,),
        in_specs=[_const((S, D), lambda j: (0, 0)), panel, panel, _const((S, D), lambda j: (0, 0))],
        out_specs=[pl.BlockSpec((None, D, fw), lambda j: (j, 0, 0)), pl.BlockSpec((None, c.fr, D), lambda j: (j, 0, 0))],
        out_shape=[jax.ShapeDtypeStruct((NDEV, D, fw), BF16), jax.ShapeDtypeStruct((NDEV, c.fr, D), BF16)],
        compiler_params=_params(1))


def _attn_bwd(c, name, q, kv, do, probs, dep=None):
    S, dq, dkv = c.S, c.DQ, c.DKV
    nb = c.nb
    scale = HEAD_DIM ** -0.5

    def body(q_ref, kvc_ref, kvp_ref, do_ref, probs_ref, dq_ref, dkv_ref, dbias_ref, dsink_ref, dsink_acc,
             ds_sc, p_sc, qm_sc, dom_sc):
        n = pl.program_id(0)

        @pl.when(n == 0)
        def _():
            dkv_ref[...] = jnp.zeros_like(dkv_ref)
            dbias_ref[...] = jnp.zeros_like(dbias_ref)
            dsink_acc[...] = jnp.zeros_like(dsink_acc)

        low = _low_lanes()
        first = _first_key()
        rows_c = pl.ds(pl.multiple_of(n * BLOCK, BLOCK), BLOCK)
        rows_p = pl.ds(pl.multiple_of(jnp.maximum(n - 1, 0) * BLOCK, BLOCK), BLOCK)
        for kvp in range(KV_PAIRS):
            k2, v2 = _pair_kv(kvc_ref, kvp_ref, kvp, dkv)
            for g in range(Q_PER_KV):
                grp = kvp * Q_PER_KV + g
                halves = []
                for par in range(2):
                    rows = slice((2 * g + par) * BLOCK, (2 * g + par + 1) * BLOCK)
                    qh = _head_operand(q_ref, grp, par, low, scale=scale)
                    doh = _head_operand(do_ref, grp, par, low)
                    saved = probs_ref[kvp, rows, :]
                    ps = saved[:, 0:1].astype(F32)
                    p16 = jnp.where(first, jnp.zeros_like(saved), saved)
                    p = p16.astype(F32)
                    dp = _dot(doh, v2, NT)
                    delta = jnp.sum(p * dp, axis=-1, keepdims=True)
                    ds = p * (dp - delta)
                    dbias_ref[kvp, rows, :] += ds
                    dsink_acc[rows, kvp:kvp + 1] += -(ps * delta)
                    ds16 = ds.astype(BF16)
                    halves.append(_dot(ds16, k2, NN) * scale)
                    ds_sc[rows, :] = ds16
                    p_sc[rows, :] = p16
                    qm_sc[rows, :] = qh
                    dom_sc[rows, :] = doh
                dq_ref[:, grp * LANES:(grp + 1) * LANES] = jnp.where(low, halves[0], halves[1]).astype(BF16)
            dk2 = _dot(ds_sc[...], qm_sc[...], TN)
            dv2 = _dot(p_sc[...], dom_sc[...], TN)
            lanes = slice(kvp * LANES, (kvp + 1) * LANES)
            vlanes = slice(dkv + kvp * LANES, dkv + (kvp + 1) * LANES)
            dkv_ref[rows_p, lanes] += dk2[:BLOCK]
            dkv_ref[rows_c, lanes] += dk2[BLOCK:]
            dkv_ref[rows_p, vlanes] += dv2[:BLOCK]
            dkv_ref[rows_c, vlanes] += dv2[BLOCK:]

        @pl.when(n == nb - 1)
        def _():
            dsink_ref[...] = jnp.sum(dsink_acc[...].reshape(2 * Q_PER_KV, BLOCK, KV_PAIRS), axis=1)

    return _pcall(
        body, name=name, ins=[q, kv, kv, do, probs], dep=dep, grid=(nb,),
        in_specs=_attn_specs(c) + [pl.BlockSpec((BLOCK, dq), lambda n: (n, 0)), _probs_spec()],
        out_specs=[pl.BlockSpec((BLOCK, dq), lambda n: (n, 0)), pl.BlockSpec((S, 2 * dkv), lambda n: (0, 0)),
                   pl.BlockSpec((KV_PAIRS, PAIR_ROWS, 2 * BLOCK), lambda n: (0, 0, 0)),
                   pl.BlockSpec((2 * Q_PER_KV, KV_PAIRS), lambda n: (0, 0))],
        out_shape=[jax.ShapeDtypeStruct((S, dq), BF16), jax.ShapeDtypeStruct((S, 2 * dkv), F32),
                   jax.ShapeDtypeStruct((KV_PAIRS, PAIR_ROWS, 2 * BLOCK), F32),
                   jax.ShapeDtypeStruct((2 * Q_PER_KV, KV_PAIRS), F32)],
        scratch_shapes=[pltpu.VMEM((PAIR_ROWS, KV_PAIRS), F32), pltpu.VMEM((PAIR_ROWS, 2 * BLOCK), BF16),
                        pltpu.VMEM((PAIR_ROWS, 2 * BLOCK), BF16), pltpu.VMEM((PAIR_ROWS, LANES), BF16),
                        pltpu.VMEM((PAIR_ROWS, LANES), BF16)],
        compiler_params=_params(1))


def _sgu_bwd(c, name, a, z, dgated, ln_g, wc, wc_t, b_t, dep=None):
    S, AW, gd, tm = c.S, c.AW, c.gd, c.tms

    def body(a_ref, z_ref, dg_ref, lng_ref, wc_ref, wct_ref, bt_ref, dz_ref, dws_ref, dbt_ref, dlng_ref, dvn_ref):
        i = pl.program_id(0)

        @pl.when(i == 0)
        def _():
            dws_ref[...] = jnp.zeros_like(dws_ref)
            dbt_ref[...] = jnp.zeros_like(dbt_ref)
            dlng_ref[...] = jnp.zeros_like(dlng_ref)

        lng = lng_ref[...]
        va = a_ref[:, AW:].astype(F32)
        xc = va - jnp.mean(va, axis=-1, keepdims=True)
        rstd = lax.rsqrt(jnp.mean(xc * xc, axis=-1, keepdims=True) + EPS)
        xh = xc * rstd
        vn = (xh * lng).astype(BF16)
        causal = _sgu_masks()
        for ch in range(tm // CHUNK):
            rows = slice(ch * CHUNK, (ch + 1) * CHUNK)
            for g in range(A_GROUPS):
                cols = slice(g * gd, (g + 1) * gd)
                blk = vn[rows, cols]
                mixed = _dot(wc_ref[g], blk, NN) + bt_ref[:, g:g + 1]
                dgb = dg_ref[rows, cols].astype(F32)
                dm = dgb * a_ref[rows, cols].astype(F32)
                dbt_ref[:, g:g + 1] += jnp.sum(dm, axis=1, keepdims=True)
                dm16 = dm.astype(BF16)
                dws_ref[g] += jnp.where(causal, _dot(dm16, blk, NT), 0.0)
                dvn_ref[rows, cols] = _dot(wct_ref[g], dm16, NN)
                dz_ref[rows, cols] = (dgb * mixed * _gelu_grad(z_ref[rows, cols].astype(F32))).astype(BF16)
        dvn = dvn_ref[...]
        dlng_ref[...] += jnp.sum(dvn * xh, axis=0, keepdims=True)
        dxh = dvn * lng
        dva = rstd * (dxh - jnp.mean(dxh, axis=-1, keepdims=True) - xh * jnp.mean(dxh * xh, axis=-1, keepdims=True))
        dz_ref[:, AW:] = (dva * _gelu_grad(z_ref[:, AW:].astype(F32))).astype(BF16)

    wide = pl.BlockSpec((tm, 2 * AW), lambda i: (i, 0))
    wsp = pl.BlockSpec((A_GROUPS, CHUNK, CHUNK), lambda i: (0, 0, 0))
    btsp = pl.BlockSpec((CHUNK, A_GROUPS), lambda i: (0, 0))
    return _pcall(
        body, name=name, ins=[a, z, dgated, ln_g, wc, wc_t, b_t], dep=dep, grid=(S // tm,),
        in_specs=[wide, wide, pl.BlockSpec((tm, AW), lambda i: (i, 0)), pl.BlockSpec((1, AW), lambda i: (0, 0)),
                  wsp, wsp, btsp],
        out_specs=[wide, wsp, btsp, pl.BlockSpec((1, AW), lambda i: (0, 0))],
        out_shape=[jax.ShapeDtypeStruct((S, 2 * AW), BF16), jax.ShapeDtypeStruct((A_GROUPS, CHUNK, CHUNK), F32),
                   jax.ShapeDtypeStruct((CHUNK, A_GROUPS), F32), jax.ShapeDtypeStruct((1, AW), F32)],
        scratch_shapes=[pltpu.VMEM((tm, AW), F32)], compiler_params=_params(1))


def _adamw(name, parts, part_block, part_index, w, m, v, tr, row_off=0, n_rows=None, prev=None, dep=None):
    R, C = w.shape
    n_rows = R if n_rows is None else n_rows
    assert n_rows % tr == 0 and row_off % tr == 0
    bc1 = 1.0 - ADAM_B1 ** ADAM_STEP
    bc2 = 1.0 - ADAM_B2 ** ADAM_STEP

    def body(p_ref, w_ref, m_ref, v_ref, *rest):
        g_ref, d_ref, nm_ref, nv_ref = rest[-4:]
        g = p_ref[0].astype(F32)
        for s in range(1, part_block[0]):
            g = g + p_ref[s].astype(F32)
        nm = ADAM_B1 * m_ref[...] + (1.0 - ADAM_B1) * g
        nv = ADAM_B2 * v_ref[...] + (1.0 - ADAM_B2) * (g * g)
        g_ref[...] = g
        nm_ref[...] = nm
        nv_ref[...] = nv
        d_ref[...] = -ADAM_LR * ((nm * (1.0 / bc1)) / (jnp.sqrt(nv * (1.0 / bc2)) + ADAM_EPS) + ADAM_WD * w_ref[...])

    ob = row_off // tr
    row = pl.BlockSpec((tr, C), lambda i: (ob + i, 0))
    out = jax.ShapeDtypeStruct((R, C), F32)
    chained = prev is not None
    return _pcall(
        body, name=name, ins=[parts, w, m, v] + (list(prev) if chained else []), dep=dep, grid=(n_rows // tr,),
        in_specs=[pl.BlockSpec(part_block, part_index), row, row, row] + ([ANY] * 4 if chained else []),
        out_specs=[row, row, row, row], out_shape=[out, out, out, out],
        input_output_aliases={4 + t: t for t in range(4)} if chained else {}, compiler_params=_params(1))


def _sum_parts(name, parts, dep=None):
    def body(p_ref, out_ref):
        g = p_ref[0]
        for s in range(1, parts.shape[0]):
            g = g + p_ref[s]
        out_ref[...] = g

    return _pcall(body, name=name, ins=[parts], in_specs=[_whole(parts)], dep=dep,
                  out_shape=jax.ShapeDtypeStruct(parts.shape[1:], F32), compiler_params=_params(0))


def _place():
    return lax.axis_index("x"), lax.axis_index("y"), lax.axis_index("c")


def _slot(px, py, pc):
    return 4 * px + 2 * py + pc


def _peer(k, x, y, c):
    return x ^ ((k >> 2) & 1), y ^ ((k >> 1) & 1), c ^ (k & 1)


SEND_PEERS = {"exchange": tuple(range(1, NDEV)), "gather": (1, 2, 4, 6), "forward": (2, 4, 6),
              "broadcast": tuple(range(1, NDEV))}


def _n_sems(mode, n_lands):
    return n_lands * (len(SEND_PEERS[mode]) + (mode != "forward"))


def _send_copies(mode, src_refs, land_refs, send_sems, recv_sems):
    x, y, c = _place()
    me = _slot(x, y, c)
    peers = SEND_PEERS[mode]
    remote, local = [], []
    for i, k in enumerate(peers):
        peer = _peer(k, x, y, c)
        for a, land in enumerate(land_refs):
            if mode == "exchange":
                src, dst, to = src_refs[a].at[_slot(*peer)], land.at[me], peer
            elif mode in ("gather", "broadcast"):
                src, dst, to = src_refs[a], land.at[me], peer
            else:
                src = dst = land.at[_slot(*peer)]
                to = (x, y, 1 - c)
            s = a * len(peers) + i
            remote.append(pltpu.make_async_remote_copy(src_ref=src, dst_ref=dst, send_sem=send_sems.at[s],
                                                       recv_sem=recv_sems.at[s], device_id=to, device_id_type=MESH))
    if mode != "forward":
        for a, land in enumerate(land_refs):
            src = src_refs[a].at[me] if mode == "exchange" else src_refs[a]
            local.append(pltpu.make_async_copy(src, land.at[me], send_sems.at[len(land_refs) * len(peers) + a]))
    return remote, local


def _send_start_groups(name, groups, mode, dep=None):
    sizes = [(len(s), len(l)) for s, l in groups]
    flat = [t for s, l in groups for t in (*s, *l)]
    n_in, ng = len(flat), len(groups)

    def body(*refs):
        sems, token, at = refs[n_in:n_in + 2 * ng], refs[-1], 0
        for gi, (ns, nl) in enumerate(sizes):
            remote, local = _send_copies(mode, refs[at:at + ns], refs[at + ns:at + ns + nl], sems[2 * gi],
                                         sems[2 * gi + 1])
            for cp in remote + local:
                cp.start()
            at += ns + nl
        token[...] = jnp.zeros_like(token)

    sem_shapes = [pltpu.SemaphoreType.DMA((_n_sems(mode, nl),)) for _, nl in sizes for _ in range(2)]
    if any(dep is t for t in flat):
        dep = None
    out = _pcall(
        body, name=name, ins=[pltpu.with_memory_space_constraint(t, pltpu.HBM) for t in flat],
        in_specs=[HBM] * n_in, dep=dep,
        out_shape=(*sem_shapes, *[pltpu.HBM(t.shape, t.dtype) for t in flat], jax.ShapeDtypeStruct((8, LANES), F32)),
        out_specs=(*[SEM] * (2 * ng), *[HBM] * n_in, pl.BlockSpec(memory_space=pltpu.VMEM)),
        input_output_aliases={i: 2 * ng + i for i in range(n_in)},
        compiler_params=pltpu.CompilerParams(has_side_effects=pltpu.SideEffectType.DATAFLOW_SIDE_EFFECTING))
    started, at = [], 2 * ng
    for gi, (ns, nl) in enumerate(sizes):
        started.append((out[-1], out[2 * gi], out[2 * gi + 1], list(out[at:at + ns]), list(out[at + ns:at + ns + nl])))
        at += ns + nl
    return started


def _send_start(name, srcs, lands, mode, dep=None):
    return _send_start_groups(name, [(srcs, lands)], mode, dep=dep)[0]


def _send_wait(name, started, mode, dep=None):
    _, send_sems, recv_sems, srcs, lands = started
    n_src, n = len(srcs), len(lands)

    def body(*refs):
        src_refs, land_refs = refs[:n_src], refs[n_src:n_src + n]
        ssem, rsem = refs[n_src + n], refs[n_src + n + 1]
        remote, local = _send_copies(mode, src_refs, land_refs, ssem, rsem)
        for cp in remote:
            cp.wait_send()
            cp.wait_recv()
        for cp in local:
            cp.wait()

    thru = [pltpu.HBM(t.shape, t.dtype) for t in [*srcs, *lands]]
    out = _pcall(
        body, name=name, ins=[*srcs, *lands, send_sems, recv_sems], in_specs=[HBM] * (n_src + n) + [SEM, SEM], dep=dep,
        out_shape=tuple(thru), out_specs=tuple([HBM] * (n_src + n)),
        input_output_aliases={i: i for i in range(n_src + n)},
        compiler_params=pltpu.CompilerParams(has_side_effects=pltpu.SideEffectType.DATAFLOW_SIDE_EFFECTING))
    return list(out[n_src:])


def _landing(block):
    return lax.empty((NDEV, *block.shape), block.dtype)


def _rows128(t):
    flat = t.reshape(-1)
    n = flat.shape[0]
    rows = -(-n // (8 * LANES)) * 8
    return jnp.pad(flat, (0, rows * LANES - n)).reshape(rows, LANES)


def kernel(x, mix_norm_g, ffn_norm_g, a_w_in, a_ln_g, a_w_spatial, a_b_spatial, a_w_out, kv_norm_g, w_k, w_v, b_w_q, b_sinks, b_w_o, rel_bias, ffn_w1, ffn_w2, final_norm_g, loss_target, m_mix_norm_g, m_ffn_norm_g, m_a_w_in, m_a_ln_g, m_a_w_spatial, m_a_b_spatial, m_a_w_out, m_kv_norm_g, m_w_k, m_w_v, m_b_w_q, m_b_sinks, m_b_w_o, m_rel_bias, m_ffn_w1, m_ffn_w2, m_final_norm_g, v_mix_norm_g, v_ffn_norm_g, v_a_w_in, v_a_ln_g, v_a_w_spatial, v_a_b_spatial, v_a_w_out, v_kv_norm_g, v_w_k, v_w_v, v_b_w_q, v_b_sinks, v_b_w_o, v_rel_bias, v_ffn_w1, v_ffn_w2, v_final_norm_g):
    c = _config(x, a_w_in, a_w_out, w_k, b_w_q, b_w_o, ffn_w1, ffn_w2)
    S, D, LA, LB, LF = c.S, c.D, c.LA, c.LB, c.LF
    weights = dict(mix_norm_g=mix_norm_g, ffn_norm_g=ffn_norm_g, a_w_in=a_w_in, a_ln_g=a_ln_g, a_w_spatial=a_w_spatial,
                   a_b_spatial=a_b_spatial, a_w_out=a_w_out, kv_norm_g=kv_norm_g, w_k=w_k, w_v=w_v, b_w_q=b_w_q,
                   b_sinks=b_sinks, b_w_o=b_w_o, rel_bias=rel_bias, ffn_w1=ffn_w1, ffn_w2=ffn_w2,
                   final_norm_g=final_norm_g)
    m_in = dict(mix_norm_g=m_mix_norm_g, ffn_norm_g=m_ffn_norm_g, a_w_in=m_a_w_in, a_ln_g=m_a_ln_g,
                a_w_spatial=m_a_w_spatial, a_b_spatial=m_a_b_spatial, a_w_out=m_a_w_out, kv_norm_g=m_kv_norm_g,
                w_k=m_w_k, w_v=m_w_v, b_w_q=m_b_w_q, b_sinks=m_b_sinks, b_w_o=m_b_w_o, rel_bias=m_rel_bias,
                ffn_w1=m_ffn_w1, ffn_w2=m_ffn_w2, final_norm_g=m_final_norm_g)
    v_in = dict(mix_norm_g=v_mix_norm_g, ffn_norm_g=v_ffn_norm_g, a_w_in=v_a_w_in, a_ln_g=v_a_ln_g,
                a_w_spatial=v_a_w_spatial, a_b_spatial=v_a_b_spatial, a_w_out=v_a_w_out, kv_norm_g=v_kv_norm_g,
                w_k=v_w_k, w_v=v_w_v, b_w_q=v_b_w_q, b_sinks=v_b_sinks, b_w_o=v_b_w_o, rel_bias=v_rel_bias,
                ffn_w1=v_ffn_w1, ffn_w2=v_ffn_w2, final_norm_g=v_final_norm_g)
    names = list(weights)
    seq = _Seq()
    me = _slot(*_place())
    bf = lambda t: t.astype(BF16)

    tr = lambda t: bf(jnp.swapaxes(t, -1, -2))

    def start(tag, some):
        got = seq(_send_start_groups, f"weights_start_{tag}", [(grp, [_landing(t) for t in grp]) for grp in some],
                  "gather")
        seq.last = got[0][0]
        return got

    started = start("first", [[tr(a_w_in[0])[None], a_ln_g]])
    _, a_w_in_, a_w_out_, ffn_w1_, ffn_w2_, w_k_, w_v_, b_w_q_, b_w_o_ = lax.optimization_barrier(
        (started[0][0], a_w_in, a_w_out, ffn_w1, ffn_w2, w_k, w_v, b_w_q, b_w_o))
    groups = []
    for l in range(LA):
        groups += [[tr(a_w_in_[l])[None]], [bf(a_w_out_[l])], [tr(ffn_w1_[l])[None], bf(ffn_w2_[l])]]
    gb = 3 * LA
    for l in range(LB):
        extra = [bf(jnp.concatenate([w_k_, w_v_], axis=1))] if l == 0 else []
        groups += [extra + [bf(b_w_q_[l]), bf(b_w_o_[l])], [tr(ffn_w1_[LA + l])[None], bf(ffn_w2_[LA + l])]]
    started += start("rest", groups[1:])
    forwarding = {}

    def forward(i):
        lands = seq(_send_wait, f"weights_wait{i}", started[i], "gather")
        forwarding[i] = seq(_send_start, f"weights_forward{i}", [], lands, "forward")

    def arrive(i):
        if i not in forwarding:
            forward(i)
        return seq(_send_wait, f"weights_arrive{i}", forwarding[i], "forward")

    causal = jnp.tril(jnp.ones((CHUNK, CHUNK), bool))
    wsp = jnp.where(causal[None, None], a_w_spatial, 0.0)
    wsp16 = wsp.astype(BF16)
    wsp16_t = jnp.swapaxes(wsp, -1, -2).astype(BF16)
    bsp_t = jnp.swapaxes(a_b_spatial, -1, -2)
    mix_g = mix_norm_g.reshape(-1, 1, D)
    ffn_g = ffn_norm_g.reshape(-1, 1, D)
    kv_g = kv_norm_g.reshape(1, D)
    fin_g = final_norm_g.reshape(1, D)
    onehot = _bucket_onehot()
    bias = _slot_bias(seq(_band_bias, rel_bias.T, onehot).reshape(N_HEADS, BLOCK, 2 * BLOCK))

    h = x.reshape(S, D)
    sav_a, sav_b, wts_a, wts_b = [], [], [], []
    for l in range(LA):
        got = arrive(3 * l)
        w_in = got[0]
        if l == 0:
            ln_g_full = jnp.transpose(got[1], (1, 0, 2)).reshape(LA, 1, c.AW)
        z, a, hn = seq(_a_in_fwd, c, f"a_in_fwd{l}", h, mix_g[l], w_in, 0)
        forward(3 * l + 1)
        gated = seq(_sgu_fwd, c, f"sgu_fwd{l}", a, ln_g_full[l], wsp16[l], bsp_t[l])
        (wout,) = arrive(3 * l + 1)
        if l > 0:
            forward(3 * l + 2)
        h1 = seq(_mm_res, c, f"a_out_fwd{l}", gated, wout, c.ar, 0, h)
        w1, rows = arrive(3 * l + 2)
        if l == LA - 1:
            forward(gb)
        p, h2, hnf = seq(_ffn_fwd, c, f"ffn_fwd{l}", h1, ffn_g[l], w1, 0, rows)
        sav_a.append((h, z, a, hn, gated, h1, p, hnf))
        wts_a.append((w_in, 0, w1, 0, rows, wout, 0))
        h = h2
    h_kv = h
    for l in range(LB):
        got = arrive(gb + 2 * l)
        if l == 0:
            wkv, got = got[0], got[1:]
            kv, hkv = seq(_rms_mm_rows, c, "kv_fwd", h, kv_g, wkv, c.kr, 0, 2 * c.DKV)
        wq, wo = _slot_cols(got[0]), _slot_rows(got[1])
        q, hn = seq(_rms_mm_rows, c, f"q_fwd{l}", h, mix_g[LA + l], wq, c.qr, 0, c.DQ)
        forward(gb + 2 * l + 1)
        o, probs = seq(_attn_fwd, c, f"attn_fwd{l}", q, kv, bias, b_sinks[l])
        h1 = seq(_mm_res, c, f"o_fwd{l}", o, wo, c.orr, 0, h)
        w1, rows = arrive(gb + 2 * l + 1)
        if l + 1 < LB:
            forward(gb + 2 * l + 2)
        p, h2, hnf = seq(_ffn_fwd, c, f"ffn_fwd{LA + l}", h1, ffn_g[LA + l], w1, 0, rows)
        sav_b.append((h, q, hn, o, probs, h1, p, hnf))
        wts_b.append((wq, wo, w1, rows))
        h = h2
    dh, d_fin_g, loss_row = seq(_final_loss, c, h, fin_g, loss_target.reshape(S, D))

    results = {}
    in_flight = []

    def update(k, parts, layer, col_blk=0):
        w = weights[k]
        rows_l, ncols = (w.shape[-2], w.shape[-1]) if w.ndim == 3 else w.shape
        flat = lambda t: t.reshape(-1, ncols)
        tr = min(256, rows_l)
        results[k] = seq(_adamw, f"adamw_{k}{layer}", parts, (NDEV, tr, ncols), lambda i: (0, i, col_blk),
                         flat(w), flat(m_in[k]), flat(v_in[k]), tr, row_off=layer * rows_l, n_rows=rows_l,
                         prev=results.get(k))

    def land(tag, entry):
        lands = seq(_send_wait, f"grads_wait_{tag}", entry[1], "exchange")
        for keys, parts in zip(entry[0], lands):
            for k, layer, col_blk in keys:
                update(k, parts, layer, col_blk)

    def send(tag, items):
        slabs = [t for _, t in items]
        own = [lax.empty(t.shape, t.dtype) for t in slabs]
        st = seq(_send_start, f"grads_start_{tag}", slabs, own, "exchange")
        in_flight.append((tag, ([k for k, _ in items], st)))
        while len(in_flight) > EXCHANGE_LAG:
            land(*in_flight.pop(0))

    d_mix_g, d_ffn_g = [None] * LF, [None] * LF
    dkv_list, dbias_list, dsink_list = [], [], [None] * LB

    def ffn_bwd(lf, dh, h1, p, hnf, w1, w1_i, rows):
        da, dh1, d_ffn_g[lf], dhb = seq(_ffn_bwd_data, c, f"ffn_bwd_data{lf}", dh, p, w1, w1_i, rows, h1, ffn_g[lf])
        dw1, dw2 = seq(_ffn_bwd_w, c, f"ffn_bwd_w{lf}", hnf, da, p, dhb)
        send(f"ffn{lf}", [([("ffn_w1", lf, 0)], dw1), ([("ffn_w2", lf, 0)], dw2)])
        return dh1

    for l in reversed(range(LB)):
        h0, q, hn, o, probs, h1, p, hnf = sav_b[l]
        wq, wo, w1, rows = wts_b[l]
        dh1 = ffn_bwd(LA + l, dh, h1, p, hnf, w1, 0, rows)
        do = seq(_bwd_rows_data, c, f"o_bwd_data{l}", dh1, wo, c.orr, 0)
        dwo = _unslot_rows(seq(_wgrad_rows, c, f"o_bwd_w{l}", o, [dh1], c.DQ, D))
        dq, dkv, dbias, dsink = seq(_attn_bwd, c, f"attn_bwd{l}", q, kv, do, probs)
        dsink_list[l] = dsink.reshape(Q_PER_KV, 2, KV_PAIRS).transpose(2, 1, 0).reshape(1, N_HEADS)
        dkv_list.append(dkv)
        dbias_list.append(_unslot_bias(dbias))
        dwq = _unslot_cols(seq(_wgrad_rows, c, f"q_bwd_w{l}", hn, [dq], D, c.DQ))
        send(f"attn{l}", [([("b_w_o", l, 0)], dwo), ([("b_w_q", l, 0)], dwq)])
        dh, d_mix_g[LA + l] = seq(_bwd_rows_to_stream, c, f"q_bwd_data{l}", [dq], wq, c.qr, 0, c.DQ, h0,
                                  mix_g[LA + l], dh1)
    dwkv = seq(_wgrad_rows, c, "kv_bwd_w", hkv, dkv_list, D, 2 * c.DKV)
    send("kv", [([("w_k", 0, 0), ("w_v", 0, 1)], dwkv)])
    dh, d_kv_g = seq(_bwd_rows_to_stream, c, "kv_bwd_data", dkv_list, wkv, c.kr, 0, 2 * c.DKV, h_kv, kv_g, dh)
    d_rel_t = seq(_band_bias_grad, dbias_list, onehot)
    d_wsp, d_bsp, d_lng = [None] * LA, [None] * LA, [None] * LA
    for l in reversed(range(LA)):
        h0, z, a, hn, gated, h1, p, hnf = sav_a[l]
        w_in, in_i, w1, w1_i, rows, wout, wout_i = wts_a[l]
        dh1 = ffn_bwd(l, dh, h1, p, hnf, w1, w1_i, rows)
        dgated = seq(_bwd_rows_data, c, f"a_out_bwd_data{l}", dh1, wout, c.ar, wout_i)
        dwout = seq(_wgrad_rows, c, f"a_out_bwd_w{l}", gated, [dh1], c.AW, D)
        send(f"a_out{l}", [([("a_w_out", l, 0)], dwout)])
        dz, d_wsp[l], dbt, d_lng[l] = seq(_sgu_bwd, c, f"sgu_bwd{l}", a, z, dgated, ln_g_full[l], wsp16[l],
                                          wsp16_t[l], bsp_t[l])
        d_bsp[l] = dbt.T
        dwin = seq(_wgrad_cols, c, f"a_in_bwd_w{l}", hn, dz)
        send(f"a_in{l}", [([("a_w_in", l, 0)], dwin)])
        dh, d_mix_g[l] = seq(_bwd_cols_to_stream, c, f"a_in_bwd_data{l}", dz, w_in, in_i, h0, mix_g[l], dh1)
    grad_x = dh.reshape(1, S, D)

    small = {
        "mix_norm_g": jnp.concatenate(d_mix_g, axis=0), "ffn_norm_g": jnp.concatenate(d_ffn_g, axis=0),
        "a_w_spatial": jnp.stack(d_wsp), "a_b_spatial": jnp.stack(d_bsp), "kv_norm_g": d_kv_g,
        "b_sinks": jnp.concatenate(dsink_list, axis=0), "rel_bias": d_rel_t.T, "final_norm_g": d_fin_g,
    }
    small_names = list(small)
    packs = [_rows128(small[k]) for k in small_names] + [_rows128(jnp.concatenate(d_lng, axis=0)), _rows128(loss_row)]
    offs = [int(o) for o in np.cumsum([0] + [p.shape[0] for p in packs])]
    Rs = offs[-1] + (-offs[-1]) % (8 * NDEV)
    tail_rows = Rs - offs[len(small_names)]
    packed = jnp.concatenate(packs + [jnp.zeros((Rs - offs[-1], LANES), F32)], axis=0)
    slab = packed.reshape(NDEV, Rs // NDEV, LANES)
    st = seq(_send_start, "small_grads_start", [slab], [lax.empty(slab.shape, slab.dtype)], "exchange")
    while len(in_flight) > 1:
        land(*in_flight.pop(0))
    (parts,) = seq(_send_wait, "small_grads_wait", st, "exchange")
    mine = seq(_sum_parts, "small_grads_sum", parts)
    st = seq(_send_start, "small_sums_start", [mine], [_landing(mine)], "broadcast")
    while in_flight:
        land(*in_flight.pop(0))
    (sums,) = seq(_send_wait, "small_sums_wait", st, "broadcast")
    small_all = sums.reshape(1, Rs, LANES)
    loss = sums.reshape(Rs, LANES)[offs[-2], 0]

    grads, deltas, new_m, new_v = {}, {}, {}, {}

    def put(k, outs, shape):
        grads[k], deltas[k], new_m[k], new_v[k] = (t.reshape(shape) for t in outs)

    def pack_state(d):
        return jnp.concatenate([_rows128(d[k]) for k in small_names] + [jnp.zeros((tail_rows, LANES), F32)], axis=0)

    outs = seq(_adamw, "adamw_small", small_all, (1, Rs, LANES), lambda i: (0, 0, 0),
               pack_state(weights), pack_state(m_in), pack_state(v_in), Rs)
    for n_, k in enumerate(small_names):
        shape = weights[k].shape
        size = int(np.prod(shape))
        put(k, [t[offs[n_]:offs[n_ + 1]].reshape(-1)[:size] for t in outs], shape)
    lng_sum = outs[0][offs[-3]:offs[-2]].reshape(-1)[:LA * c.AW].reshape(LA, c.AW)
    lng_mine = lax.dynamic_slice_in_dim(lng_sum, me * c.ar, c.ar, axis=1)
    lng_parts = jnp.concatenate([lng_mine[None], jnp.zeros((NDEV - 1, LA, c.ar), F32)], axis=0)
    put("a_ln_g", seq(_adamw, "adamw_ln_g", lng_parts, (NDEV, LA, c.ar), lambda i: (0, 0, 0),
                      a_ln_g, m_in["a_ln_g"], v_in["a_ln_g"], LA), a_ln_g.shape)
    for k in ("a_w_in", "ffn_w1", "ffn_w2", "a_w_out", "b_w_o", "b_w_q", "w_k", "w_v"):
        put(k, results[k], weights[k].shape)

    return (loss, grad_x, *[grads[k] for k in names], *[deltas[k] for k in names],
            *[new_m[k] for k in names], *[new_v[k] for k in names])
```

```python
import numpy as np
import math
import jax
import jax.numpy as jnp
from jax import lax
from jax.experimental import pallas as pl
from jax.experimental.pallas import tpu as pltpu

F32 = jnp.float32
BF16 = jnp.bfloat16

NDEV = 8
EPS = 1e-6
CHUNK = 128
A_GROUPS = 8
N_HEADS = 16
N_KV_HEADS = 4
Q_PER_KV = N_HEADS // N_KV_HEADS
HEAD_DIM = 64
BLOCK = 128
N_BUCKETS = 32
MAX_DISTANCE = 128
ADAM_LR, ADAM_B1, ADAM_B2, ADAM_EPS, ADAM_WD, ADAM_STEP = 0.001, 0.9, 0.999, 1e-08, 0.01, 10
LANES = 128
VMEM_LIMIT = 56 * 1024 * 1024
INV_SQRT2 = 0.7071067811865476
INV_SQRT_2PI = 0.3989422804014327
EXCHANGE_LAG = 4

HBM = pl.BlockSpec(memory_space=pltpu.HBM)
SMEM = pl.BlockSpec(memory_space=pltpu.SMEM)
ANY = pl.BlockSpec(memory_space=pl.ANY)
SEM = pl.BlockSpec(memory_space=pltpu.SEMAPHORE)
MESH = pl.DeviceIdType.MESH


def _params(n_grid):
    return pltpu.CompilerParams(dimension_semantics=("arbitrary",) * n_grid, vmem_limit_bytes=VMEM_LIMIT)


def _const(block, index_map):
    return pl.BlockSpec(block, index_map, pipeline_mode=pl.Buffered(1))


def _pcall(body, *, ins, in_specs, dep=None, **kw):
    n_in = len(ins)
    if dep is None or any(dep is t for t in ins):
        return pl.pallas_call(body, in_specs=list(in_specs), **kw)(*ins)

    def with_dep(*refs):
        body(*refs[:n_in], *refs[n_in + 1:])

    return pl.pallas_call(with_dep, in_specs=[*in_specs, ANY], **kw)(*ins, dep)


class _Seq:
    def __init__(self):
        self.last = None

    def __call__(self, fn, *args, **kw):
        out = fn(*args, dep=self.last, **kw)
        self.last = out[0] if isinstance(out, (tuple, list)) else out
        return out


def _rstd(h):
    return lax.rsqrt(jnp.mean(h * h, axis=-1, keepdims=True) + EPS)


def _rms_bwd(dhn, h, g, dres):
    r = _rstd(h)
    xh = h * r
    dg = jnp.sum(dhn * xh, axis=0, keepdims=True)
    dxh = dhn * g
    dx = r * (dxh - xh * jnp.mean(dxh * xh, axis=-1, keepdims=True))
    return dres + dx, dg


def _gelu(z):
    return 0.5 * z * (1.0 + lax.erf(z * INV_SQRT2))


def _gelu_grad(z):
    return 0.5 * (1.0 + lax.erf(z * INV_SQRT2)) + z * (jnp.exp(-0.5 * z * z) * INV_SQRT_2PI)


def _dot(a, b, dims):
    return lax.dot_general(a, b, (dims, ((), ())), preferred_element_type=F32)


NN = ((1,), (0,))
NT = ((1,), (1,))
TN = ((0,), (0,))


def _mm(name, ins, in_specs, out_shapes, out_specs, *, grid, dims, nk, acc_shape, load_a, load_b, epilogue,
        dep=None):
    n_in, n_out = len(ins), len(out_shapes)
    kax = len(grid) - 1

    def body(*refs):
        in_refs = refs[:n_in]
        out_refs = refs[n_in:n_in + n_out]
        a = load_a(in_refs, out_refs)
        b = load_b(in_refs)
        prod = _dot(a, b, dims)
        if nk == 1:
            epilogue(prod, in_refs, out_refs)
        else:
            acc = refs[n_in + n_out]
            k = pl.program_id(kax)

            @pl.when(k == 0)
            def _():
                acc[...] = prod

            @pl.when(k > 0)
            def _():
                acc[...] += prod

            @pl.when(k == nk - 1)
            def _():
                epilogue(acc[...], in_refs, out_refs)

    return _pcall(
        body, name=name, ins=ins, in_specs=in_specs, dep=dep, grid=grid, out_specs=out_specs, out_shape=out_shapes,
        scratch_shapes=[pltpu.VMEM(acc_shape, F32)] if nk > 1 else [], compiler_params=_params(len(grid)))


def _bf(ref_idx):
    return lambda in_refs, *_: in_refs[ref_idx][...].astype(BF16)


def _b_view(ref_idx, rows):
    def load(in_refs):
        b = in_refs[ref_idx][...]
        return b.reshape(rows, b.shape[-1])
    return load


class Cfg:
    pass


def _config(x, a_w_in, a_w_out, w_k, b_w_q, b_w_o, ffn_w1, ffn_w2):
    c = Cfg()
    c.S, c.D = x.shape[1], x.shape[2]
    c.LA, _, c.cw = a_w_in.shape
    c.AW2 = NDEV * c.cw
    c.AW = c.AW2 // 2
    c.gd = c.AW // A_GROUPS
    c.ar = a_w_out.shape[1]
    c.LF, _, c.fw = ffn_w1.shape
    c.fr = ffn_w2.shape[1]
    c.LB, c.qr, c.DQ = b_w_q.shape
    c.orr = b_w_o.shape[1]
    c.kr, c.DKV = w_k.shape
    c.tm = min(1024, c.S)
    c.tmw = min(512, c.S)
    c.tms = min(256, c.S)
    c.nb = c.S // BLOCK
    assert c.cw == c.fw == c.fr and c.AW == NDEV * c.ar and c.D == NDEV * c.qr == NDEV * c.kr
    assert c.DQ == NDEV * c.orr == N_HEADS * HEAD_DIM and c.DKV == N_KV_HEADS * HEAD_DIM
    assert c.S % c.tm == 0 and c.S % c.tmw == 0 and c.S % c.tms == 0 and c.tms % CHUNK == 0 and c.gd % LANES == 0
    assert c.LA >= 1 and c.LB >= 1 and c.LF == c.LA + c.LB
    return c


def _cached_rms(h_idx, g_idx, hn_out_idx, jax_axis=1):
    def load(in_refs, out_refs):
        hn_ref = out_refs[hn_out_idx]

        @pl.when(pl.program_id(jax_axis) == 0)
        def _():
            h = in_refs[h_idx][...]
            hn_ref[...] = (h * _rstd(h) * in_refs[g_idx][...]).astype(BF16)

        return hn_ref[...]
    return load


def _a_in_fwd(c, name, h, g, col, ci, dep=None):
    S, D, cw, tm = c.S, c.D, c.cw, c.tmw

    def body(h_ref, g_ref, w_ref, z_ref, a_ref, hn_ref):
        h = h_ref[...]
        hn = (h * _rstd(h) * g_ref[...]).astype(BF16)
        hn_ref[...] = hn
        for j in range(NDEV):
            cols = slice(j * cw, (j + 1) * cw)
            z = _dot(hn, w_ref[j], NT)
            z_ref[:, cols] = z.astype(BF16)
            a_ref[:, cols] = _gelu(z).astype(BF16)

    row = pl.BlockSpec((tm, D), lambda i: (i, 0))
    wide = pl.BlockSpec((tm, c.AW2), lambda i: (i, 0))
    return _pcall(
        body, name=name, ins=[h, g, col], dep=dep, grid=(S // tm,),
        in_specs=[row, pl.BlockSpec((1, D), lambda i: (0, 0)), _const((NDEV, None, cw, D), lambda i: (0, ci, 0, 0))],
        out_specs=[wide, wide, row],
        out_shape=[jax.ShapeDtypeStruct((S, c.AW2), BF16), jax.ShapeDtypeStruct((S, c.AW2), BF16),
                   jax.ShapeDtypeStruct((S, D), BF16)],
        compiler_params=_params(1))


def _rms_mm_rows(c, name, h, g, slab, blk_rows, blk_idx, n_out, dep=None):
    S, D, tm = c.S, c.D, c.tm

    def epilogue(acc, in_refs, out_refs):
        out_refs[0][...] = acc.astype(BF16)

    return _mm(
        name, [h, slab, g],
        [pl.BlockSpec((tm, D), lambda i, j, k: (i, 0)),
         pl.BlockSpec((NDEV, blk_rows, n_out), lambda i, j, k: (0, blk_idx, 0)),
         pl.BlockSpec((1, D), lambda i, j, k: (0, 0))],
        [jax.ShapeDtypeStruct((S, n_out), BF16), jax.ShapeDtypeStruct((S, D), BF16)],
        [pl.BlockSpec((tm, n_out), lambda i, j, k: (i, 0)), pl.BlockSpec((tm, D), lambda i, j, k: (i, 0))],
        grid=(S // tm, 1, 1), dims=NN, nk=1, acc_shape=None,
        load_a=_cached_rms(0, 2, 1), load_b=_b_view(1, NDEV * blk_rows), epilogue=epilogue, dep=dep)


def _mm_res(c, name, a, slab, blk_rows, blk_idx, res, dep=None):
    S, D, tm = c.S, c.D, c.tm
    K = NDEV * blk_rows

    def epilogue(acc, in_refs, out_refs):
        out_refs[0][...] = in_refs[2][...] + acc

    return _mm(
        name, [a, slab, res],
        [pl.BlockSpec((tm, K), lambda i, j, k: (i, 0)),
         pl.BlockSpec((NDEV, blk_rows, D), lambda i, j, k: (0, blk_idx, 0)),
         pl.BlockSpec((tm, D), lambda i, j, k: (i, 0))],
        [jax.ShapeDtypeStruct((S, D), F32)], [pl.BlockSpec((tm, D), lambda i, j, k: (i, 0))],
        grid=(S // tm, 1, 1), dims=NN, nk=1, acc_shape=None,
        load_a=_bf(0), load_b=_b_view(1, K), epilogue=epilogue, dep=dep)[0]


def _sgu_masks():
    ii = lax.broadcasted_iota(jnp.int32, (CHUNK, CHUNK), 0)
    jj = lax.broadcasted_iota(jnp.int32, (CHUNK, CHUNK), 1)
    return ii >= jj


def _sgu_fwd(c, name, a, ln_g, wc, b_t, dep=None):
    S, AW, gd, tm = c.S, c.AW, c.gd, c.tms

    def body(a_ref, lng_ref, wc_ref, bt_ref, out_ref):
        va = a_ref[:, AW:].astype(F32)
        xc = va - jnp.mean(va, axis=-1, keepdims=True)
        vn = (xc * lax.rsqrt(jnp.mean(xc * xc, axis=-1, keepdims=True) + EPS) * lng_ref[...]).astype(BF16)
        for ch in range(tm // CHUNK):
            rows = slice(ch * CHUNK, (ch + 1) * CHUNK)
            for g in range(A_GROUPS):
                cols = slice(g * gd, (g + 1) * gd)
                mixed = _dot(wc_ref[g], vn[rows, cols], NN) + bt_ref[:, g:g + 1]
                out_ref[rows, cols] = (a_ref[rows, cols].astype(F32) * mixed).astype(BF16)

    return _pcall(
        body, name=name, ins=[a, ln_g, wc, b_t], dep=dep, grid=(S // tm,),
        in_specs=[pl.BlockSpec((tm, 2 * AW), lambda i: (i, 0)), pl.BlockSpec((1, AW), lambda i: (0, 0)),
                  pl.BlockSpec((A_GROUPS, CHUNK, CHUNK), lambda i: (0, 0, 0)),
                  pl.BlockSpec((CHUNK, A_GROUPS), lambda i: (0, 0))],
        out_specs=pl.BlockSpec((tm, AW), lambda i: (i, 0)),
        out_shape=jax.ShapeDtypeStruct((S, AW), BF16), compiler_params=_params(1))


def _ffn_fwd(c, name, h, g, col, ci, rows, dep=None):
    S, D, fw, tm = c.S, c.D, c.fw, c.tmw
    F = NDEV * fw

    def body(h_ref, g_ref, w1_ref, w2_ref, p_ref, out_ref, hn_ref, r_ref):
        h = h_ref[...]
        hn = (h * _rstd(h) * g_ref[...]).astype(BF16)
        hn_ref[...] = hn
        for j in range(NDEV):
            cols = slice(j * fw, (j + 1) * fw)
            p = jnp.maximum(_dot(hn, w1_ref[j], NT), 0.0)
            p_ref[:, cols] = p.astype(BF16)
            r_ref[:, cols] = (p * p).astype(BF16)
        out_ref[...] = h + _dot(r_ref[...], w2_ref[...].reshape(F, D), NN)

    row = pl.BlockSpec((tm, D), lambda i: (i, 0))
    return _pcall(
        body, name=name, ins=[h, g, col, rows], dep=dep, grid=(S // tm,),
        in_specs=[row, pl.BlockSpec((1, D), lambda i: (0, 0)),
                  _const((NDEV, None, fw, D), lambda i: (0, ci, 0, 0)), _const((NDEV, c.fr, D), lambda i: (0, 0, 0))],
        out_specs=[pl.BlockSpec((tm, F), lambda i: (i, 0)), row, row],
        out_shape=[jax.ShapeDtypeStruct((S, F), BF16), jax.ShapeDtypeStruct((S, D), F32),
                   jax.ShapeDtypeStruct((S, D), BF16)],
        scratch_shapes=[pltpu.VMEM((tm, F), BF16)], compiler_params=_params(1))


def _bucket_table():
    qi = np.arange(BLOCK)[:, None]
    kj = np.arange(2 * BLOCK)[None, :]
    d = np.maximum(qi + BLOCK - kj, 0)
    max_exact = N_BUCKETS // 2
    ratio = np.log(np.maximum(d, 1).astype(np.float32) / np.float32(max_exact)) / np.float32(
        math.log(MAX_DISTANCE / max_exact))
    large = np.minimum(max_exact + (ratio.astype(np.float32) * np.float32(N_BUCKETS - max_exact)).astype(np.int32),
                       N_BUCKETS - 1)
    return np.where(d < max_exact, d, large).astype(np.int32)


def _bucket_onehot():
    b = jnp.asarray(_bucket_table().reshape(1, -1))
    return (b == lax.broadcasted_iota(jnp.int32, (N_BUCKETS, b.shape[1]), 0)).astype(F32)


def _whole(t):
    return pl.BlockSpec(t.shape, lambda: (0,) * t.ndim)


def _band_bias(rel_bias_t, onehot, dep=None):
    def body(r_ref, oh_ref, out_ref):
        out_ref[...] = lax.dot_general(r_ref[...], oh_ref[...], (NN, ((), ())), preferred_element_type=F32,
                                       precision=lax.Precision.HIGHEST)

    n = onehot.shape[1]
    return _pcall(body, name="band_bias", ins=[rel_bias_t, onehot], in_specs=[_whole(rel_bias_t), _whole(onehot)],
                  dep=dep, out_shape=jax.ShapeDtypeStruct((N_HEADS, n), F32), compiler_params=_params(0))


def _band_bias_grad(dbias_list, onehot, dep=None):
    n_in = len(dbias_list)

    def body(*refs):
        oh_ref, out_ref = refs[n_in], refs[n_in + 1]
        d = refs[0][...]
        for r in refs[1:n_in]:
            d = d + r[...]
        out_ref[...] = lax.dot_general(d, oh_ref[...], (NT, ((), ())), preferred_element_type=F32,
                                       precision=lax.Precision.HIGHEST)

    ins = [*dbias_list, onehot]
    return _pcall(body, name="band_bias_grad", ins=ins, in_specs=[_whole(t) for t in ins], dep=dep,
                  out_shape=jax.ShapeDtypeStruct((N_HEADS, N_BUCKETS), F32), compiler_params=_params(0))


KV_PAIRS = N_KV_HEADS // 2
PAIR_ROWS = 2 * Q_PER_KV * BLOCK
MASKED = float(np.finfo(np.float32).min) / 2


def _slot_cols(w):
    lead = w.shape[:-1]
    return w.reshape(*lead, KV_PAIRS, 2, Q_PER_KV, HEAD_DIM).swapaxes(-3, -2).reshape(*lead, N_HEADS * HEAD_DIM)


def _unslot_cols(w):
    lead = w.shape[:-1]
    return w.reshape(*lead, KV_PAIRS, Q_PER_KV, 2, HEAD_DIM).swapaxes(-3, -2).reshape(*lead, N_HEADS * HEAD_DIM)


def _slot_rows(blocks):
    n = blocks.shape[-1]
    return blocks.reshape(KV_PAIRS, 2, Q_PER_KV, HEAD_DIM, n).swapaxes(1, 2).reshape(blocks.shape)


def _unslot_rows(blocks):
    n = blocks.shape[-1]
    return blocks.reshape(KV_PAIRS, Q_PER_KV, 2, HEAD_DIM, n).swapaxes(1, 2).reshape(blocks.shape)


def _slot_bias(bias):
    qi = np.arange(BLOCK)[:, None]
    kj = np.arange(2 * BLOCK)[None, :]
    dist = qi + BLOCK - kj
    window = (dist >= 0) & (dist < BLOCK)
    b = bias.reshape(KV_PAIRS, 2, Q_PER_KV, BLOCK, 2 * BLOCK).swapaxes(1, 2).reshape(KV_PAIRS, PAIR_ROWS, 2 * BLOCK)
    tile = lambda mk: jnp.asarray(np.tile(mk, (2 * Q_PER_KV, 1)))[None]
    return jnp.stack([jnp.where(tile(window & (kj >= BLOCK)), b, MASKED), jnp.where(tile(window), b, MASKED)])


def _unslot_bias(db):
    return db.reshape(KV_PAIRS, Q_PER_KV, 2, BLOCK, 2 * BLOCK).swapaxes(1, 2).reshape(N_HEADS, -1)


def _pair_kv(kvc_ref, kvp_ref, kvp, dkv):
    lanes = slice(kvp * LANES, (kvp + 1) * LANES)
    vlanes = slice(dkv + kvp * LANES, dkv + (kvp + 1) * LANES)
    k2 = jnp.concatenate([kvp_ref[:, lanes], kvc_ref[:, lanes]], axis=0)
    v2 = jnp.concatenate([kvp_ref[:, vlanes], kvc_ref[:, vlanes]], axis=0)
    return k2, v2


def _head_operand(ref, grp, par, low, scale=None):
    xg = ref[:, grp * LANES:(grp + 1) * LANES]
    if scale is not None:
        xg = xg * scale
    zero = jnp.zeros_like(xg)
    return jnp.where(low, xg, zero) if par == 0 else jnp.where(low, zero, xg)


def _head_probs(qh, k2, bias_rows, sink):
    s = _dot(qh, k2, NT) + bias_rows
    m = jnp.maximum(jnp.max(s, axis=-1, keepdims=True), sink)
    e = jnp.exp(s - m)
    es = jnp.exp(sink - m)
    inv = 1.0 / (jnp.sum(e, axis=-1, keepdims=True) + es)
    return e * inv, es * inv


def _attn_specs(c):
    dq, dkv2 = c.DQ, 2 * c.DKV
    return [pl.BlockSpec((BLOCK, dq), lambda n: (n, 0)),
            pl.BlockSpec((BLOCK, dkv2), lambda n: (n, 0)),
            pl.BlockSpec((BLOCK, dkv2), lambda n: (jnp.maximum(n - 1, 0), 0))]


def _bias_spec():
    return pl.BlockSpec((None, KV_PAIRS, PAIR_ROWS, 2 * BLOCK), lambda n: (jnp.minimum(n, 1), 0, 0, 0))


def _low_lanes():
    return lax.broadcasted_iota(jnp.int32, (BLOCK, LANES), 1) < HEAD_DIM


def _first_key():
    return lax.broadcasted_iota(jnp.int32, (BLOCK, 2 * BLOCK), 1) == 0


def _probs_spec():
    return pl.BlockSpec((None, KV_PAIRS, PAIR_ROWS, 2 * BLOCK), lambda n: (n, 0, 0, 0))


def _attn_fwd(c, name, q, kv, bias, sinks, dep=None):
    S, dq = c.S, c.DQ

    def body(q_ref, kvc_ref, kvp_ref, bias_ref, sink_ref, o_ref, probs_ref):
        low = _low_lanes()
        first = _first_key()
        for kvp in range(KV_PAIRS):
            k2, v2 = _pair_kv(kvc_ref, kvp_ref, kvp, c.DKV)
            for g in range(Q_PER_KV):
                grp = kvp * Q_PER_KV + g
                halves = []
                for par in range(2):
                    rows = slice((2 * g + par) * BLOCK, (2 * g + par + 1) * BLOCK)
                    qh = _head_operand(q_ref, grp, par, low, scale=HEAD_DIM ** -0.5)
                    p, ps = _head_probs(qh, k2, bias_ref[kvp, rows, :], sink_ref[(2 * kvp + par) * Q_PER_KV + g])
                    probs_ref[kvp, rows, :] = jnp.where(first, ps, p).astype(BF16)
                    halves.append(_dot(p.astype(BF16), v2, NN))
                o_ref[:, grp * LANES:(grp + 1) * LANES] = jnp.where(low, halves[0], halves[1]).astype(BF16)

    return _pcall(
        body, name=name, ins=[q, kv, kv, bias, sinks], dep=dep, grid=(c.nb,),
        in_specs=_attn_specs(c) + [_bias_spec(), SMEM],
        out_specs=[pl.BlockSpec((BLOCK, dq), lambda n: (n, 0)), _probs_spec()],
        out_shape=[jax.ShapeDtypeStruct((S, dq), BF16),
                   jax.ShapeDtypeStruct((c.nb, KV_PAIRS, PAIR_ROWS, 2 * BLOCK), BF16)],
        compiler_params=_params(1))


def _final_loss(c, h, g, target, dep=None):
    S, D, tm = c.S, c.D, c.tm

    def body(h_ref, g_ref, t_ref, dh_ref, dg_ref, loss_ref):
        i = pl.program_id(0)
        h = h_ref[...]
        gg = g_ref[...]
        r = _rstd(h)
        xh = h * r
        err = xh * gg - t_ref[...]
        lp = jnp.sum(jnp.sum(err * err, axis=1, keepdims=True), axis=0, keepdims=True) * (0.5 / D)
        dx, dg = _rms_bwd(err * (1.0 / D), h, gg, 0.0)
        dh_ref[...] = dx

        @pl.when(i == 0)
        def _():
            dg_ref[...] = dg
            loss_ref[...] = jnp.broadcast_to(lp, loss_ref.shape)

        @pl.when(i > 0)
        def _():
            dg_ref[...] += dg
            loss_ref[...] += jnp.broadcast_to(lp, loss_ref.shape)

    row = pl.BlockSpec((tm, D), lambda i: (i, 0))
    return _pcall(
        body, name="final_loss", ins=[h, g, target], dep=dep, grid=(S // tm,),
        in_specs=[row, pl.BlockSpec((1, D), lambda i: (0, 0)), row],
        out_specs=[row, pl.BlockSpec((1, D), lambda i: (0, 0)), pl.BlockSpec((1, LANES), lambda i: (0, 0))],
        out_shape=[jax.ShapeDtypeStruct((S, D), F32), jax.ShapeDtypeStruct((1, D), F32),
                   jax.ShapeDtypeStruct((1, LANES), F32)],
        compiler_params=_params(1))


def _rms_bwd_epilogue(h_idx, g_idx, res_idx):
    def epilogue(dhn, in_refs, out_refs):
        dh, dg = _rms_bwd(dhn, in_refs[h_idx][...], in_refs[g_idx][...], in_refs[res_idx][...])
        out_refs[0][...] = dh
        i = pl.program_id(0)

        @pl.when(i == 0)
        def _():
            out_refs[1][...] = dg

        @pl.when(i > 0)
        def _():
            out_refs[1][...] += dg
    return epilogue


def _stream_outs(c, tm):
    S, D = c.S, c.D
    return ([jax.ShapeDtypeStruct((S, D), F32), jax.ShapeDtypeStruct((1, D), F32)],
            [pl.BlockSpec((tm, D), lambda i, j, k: (i, 0)), pl.BlockSpec((1, D), lambda i, j, k: (0, 0))])


def _row_specs(c, tm):
    D = c.D
    return [pl.BlockSpec((tm, D), lambda i, j, k: (i, 0)), pl.BlockSpec((1, D), lambda i, j, k: (0, 0)),
            pl.BlockSpec((tm, D), lambda i, j, k: (i, 0))]


def _bwd_rows_to_stream(c, name, dy_list, slab, blk_rows, blk_idx, n_in_cols, h, g, dres, dep=None):
    S, D, tm = c.S, c.D, c.tm
    nd = len(dy_list)

    def load_a(in_refs, out_refs):
        a = in_refs[0][...]
        for r in in_refs[1:nd]:
            a = a + r[...]
        return a.astype(BF16)

    shapes, specs = _stream_outs(c, tm)
    return _mm(
        name, [*dy_list, slab, h, g, dres],
        [pl.BlockSpec((tm, n_in_cols), lambda i, j, k: (i, 0))] * nd
        + [pl.BlockSpec((NDEV, blk_rows, n_in_cols), lambda i, j, k: (0, blk_idx, 0))] + _row_specs(c, tm),
        shapes, specs, grid=(S // tm, 1, 1), dims=NT, nk=1, acc_shape=None,
        load_a=load_a, load_b=_b_view(nd, NDEV * blk_rows), epilogue=_rms_bwd_epilogue(nd + 1, nd + 2, nd + 3),
        dep=dep)


def _bwd_cols_to_stream(c, name, dy, col, ci, h, g, dres, dep=None):
    S, D, cw, tm = c.S, c.D, c.cw, c.tmw
    K = NDEV * cw
    shapes, specs = _stream_outs(c, tm)
    return _mm(
        name, [dy, col, h, g, dres],
        [pl.BlockSpec((tm, K), lambda i, j, k: (i, 0)),
         _const((NDEV, None, cw, D), lambda i, j, k: (0, ci, 0, 0))] + _row_specs(c, tm),
        shapes, specs, grid=(S // tm, 1, 1), dims=NN, nk=1, acc_shape=None,
        load_a=_bf(0), load_b=_b_view(1, K), epilogue=_rms_bwd_epilogue(2, 3, 4), dep=dep)


def _bwd_rows_data(c, name, dy, slab, blk_rows, blk_idx, dep=None):
    S, D, tm = c.S, c.D, c.tm
    K = NDEV * blk_rows

    def epilogue(acc, in_refs, out_refs):
        out_refs[0][...] = acc.astype(BF16)

    return _mm(
        name, [dy, slab],
        [pl.BlockSpec((tm, D), lambda i, j, k: (i, 0)),
         pl.BlockSpec((NDEV, blk_rows, D), lambda i, j, k: (0, blk_idx, 0))],
        [jax.ShapeDtypeStruct((S, K), BF16)], [pl.BlockSpec((tm, K), lambda i, j, k: (i, 0))],
        grid=(S // tm, 1, 1), dims=NT, nk=1, acc_shape=None,
        load_a=_bf(0), load_b=_b_view(1, K), epilogue=epilogue, dep=dep)[0]


def _wgrad_rows(c, name, a, b_list, n_a, n_b, dep=None):
    S, tm = c.S, c.tm
    nb_in = len(b_list)
    blk_rows = n_a // NDEV

    def load_b(in_refs):
        b = in_refs[1][...]
        for r in in_refs[2:1 + nb_in]:
            b = b + r[...]
        return b.astype(BF16)

    def epilogue(acc, in_refs, out_refs):
        out_refs[0][...] = acc.reshape(NDEV, blk_rows, n_b).astype(BF16)

    return _mm(
        name, [a, *b_list],
        [pl.BlockSpec((tm, n_a), lambda i, j, k: (k, 0))] + [pl.BlockSpec((tm, n_b), lambda i, j, k: (k, 0))] * nb_in,
        [jax.ShapeDtypeStruct((NDEV, blk_rows, n_b), BF16)],
        [pl.BlockSpec((NDEV, blk_rows, n_b), lambda i, j, k: (0, 0, 0))],
        grid=(1, 1, S // tm), dims=TN, nk=S // tm, acc_shape=(n_a, n_b),
        load_a=_bf(0), load_b=load_b, epilogue=epilogue, dep=dep)[0]


def _wgrad_cols(c, name, a, b, dep=None):
    S, D, cw = c.S, c.D, c.cw

    def body(a_ref, b_ref, out_ref):
        out_ref[...] = _dot(a_ref[...], b_ref[...], TN).astype(BF16)

    return _pcall(
        body, name=name, ins=[a, b], dep=dep, grid=(NDEV,),
        in_specs=[_const((S, D), lambda j: (0, 0)), pl.BlockSpec((S, cw), lambda j: (0, j))],
        out_specs=pl.BlockSpec((None, D, cw), lambda j: (j, 0, 0)),
        out_shape=jax.ShapeDtypeStruct((NDEV, D, cw), BF16), compiler_params=_params(1))


def _ffn_bwd_data(c, name, dh, p, col, ci, rows, h, g, dep=None):
    S, D, fw, tm = c.S, c.D, c.fw, c.tmw
    F = NDEV * fw

    def body(dh_ref, p_ref, w1t_ref, w2_ref, h_ref, g_ref, da_ref, out_ref, dg_ref, dhb_ref):
        i = pl.program_id(0)
        dh = dh_ref[...]
        dhb = dh.astype(BF16)
        dhb_ref[...] = dhb
        for j in range(NDEV):
            cols = slice(j * fw, (j + 1) * fw)
            da_ref[:, cols] = (_dot(dhb, w2_ref[j], NT) * (2.0 * p_ref[:, cols].astype(F32))).astype(BF16)
        dx, dg = _rms_bwd(_dot(da_ref[...], w1t_ref[...].reshape(F, D), NN), h_ref[...], g_ref[...], dh)
        out_ref[...] = dx

        @pl.when(i == 0)
        def _():
            dg_ref[...] = dg

        @pl.when(i > 0)
        def _():
            dg_ref[...] += dg

    row = pl.BlockSpec((tm, D), lambda i: (i, 0))
    wide = pl.BlockSpec((tm, F), lambda i: (i, 0))
    return _pcall(
        body, name=name, ins=[dh, p, col, rows, h, g], dep=dep, grid=(S // tm,),
        in_specs=[row, wide, _const((NDEV, None, fw, D), lambda i: (0, ci, 0, 0)),
                  _const((NDEV, c.fr, D), lambda i: (0, 0, 0)),
                  row, pl.BlockSpec((1, D), lambda i: (0, 0))],
        out_specs=[wide, row, pl.BlockSpec((1, D), lambda i: (0, 0)), row],
        out_shape=[jax.ShapeDtypeStruct((S, F), BF16), jax.ShapeDtypeStruct((S, D), F32),
                   jax.ShapeDtypeStruct((1, D), F32), jax.ShapeDtypeStruct((S, D), BF16)],
        compiler_params=_params(1))


def _ffn_bwd_w(c, name, hn, da, p, dhb, dep=None):
    S, D, fw = c.S, c.D, c.fw

    def body(hn_ref, da_ref, p_ref, dhb_ref, dw1_ref, dw2_ref):
        dw1_ref[...] = _dot(hn_ref[...], da_ref[...], TN).astype(BF16)
        pf = p_ref[...].astype(F32)
        dw2_ref[...] = _dot((pf * pf).astype(BF16), dhb_ref[...], TN).astype(BF16)

    panel = pl.BlockSpec((S, fw), lambda j: (0, j))
    return _pcall(
        body, name=name, ins=[hn, da, p, dhb], dep=dep, grid=(NDEV,),
        in_specs=[_const((S, D), lambda j: (0, 0)), panel, panel, _const((S, D), lambda j: (0, 0))],
        out_specs=[pl.BlockSpec((None, D, fw), lambda j: (j, 0, 0)), pl.BlockSpec((None, c.fr, D), lambda j: (j, 0, 0))],
        out_shape=[jax.ShapeDtypeStruct((NDEV, D, fw), BF16), jax.ShapeDtypeStruct((NDEV, c.fr, D), BF16)],
        compiler_params=_params(1))


def _attn_bwd(c, name, q, kv, do, probs, dep=None):
    S, dq, dkv = c.S, c.DQ, c.DKV
    nb = c.nb
    scale = HEAD_DIM ** -0.5

    def body(q_ref, kvc_ref, kvp_ref, do_ref, probs_ref, dq_ref, dkv_ref, dbias_ref, dsink_ref, dsink_acc,
             ds_sc, p_sc, qm_sc, dom_sc):
        n = pl.program_id(0)

        @pl.when(n == 0)
        def _():
            dkv_ref[...] = jnp.zeros_like(dkv_ref)
            dbias_ref[...] = jnp.zeros_like(dbias_ref)
            dsink_acc[...] = jnp.zeros_like(dsink_acc)

        low = _low_lanes()
        first = _first_key()
        rows_c = pl.ds(pl.multiple_of(n * BLOCK, BLOCK), BLOCK)
        rows_p = pl.ds(pl.multiple_of(jnp.maximum(n - 1, 0) * BLOCK, BLOCK), BLOCK)
        for kvp in range(KV_PAIRS):
            k2, v2 = _pair_kv(kvc_ref, kvp_ref, kvp, dkv)
            for g in range(Q_PER_KV):
                grp = kvp * Q_PER_KV + g
                halves = []
                for par in range(2):
                    rows = slice((2 * g + par) * BLOCK, (2 * g + par + 1) * BLOCK)
                    qh = _head_operand(q_ref, grp, par, low, scale=scale)
                    doh = _head_operand(do_ref, grp, par, low)
                    saved = probs_ref[kvp, rows, :]
                    ps = saved[:, 0:1].astype(F32)
                    p16 = jnp.where(first, jnp.zeros_like(saved), saved)
                    p = p16.astype(F32)
                    dp = _dot(doh, v2, NT)
                    delta = jnp.sum(p * dp, axis=-1, keepdims=True)
                    ds = p * (dp - delta)
                    dbias_ref[kvp, rows, :] += ds
                    dsink_acc[rows, kvp:kvp + 1] += -(ps * delta)
                    ds16 = ds.astype(BF16)
                    halves.append(_dot(ds16, k2, NN) * scale)
                    ds_sc[rows, :] = ds16
                    p_sc[rows, :] = p16
                    qm_sc[rows, :] = qh
                    dom_sc[rows, :] = doh
                dq_ref[:, grp * LANES:(grp + 1) * LANES] = jnp.where(low, halves[0], halves[1]).astype(BF16)
            dk2 = _dot(ds_sc[...], qm_sc[...], TN)
            dv2 = _dot(p_sc[...], dom_sc[...], TN)
            lanes = slice(kvp * LANES, (kvp + 1) * LANES)
            vlanes = slice(dkv + kvp * LANES, dkv + (kvp + 1) * LANES)
            dkv_ref[rows_p, lanes] += dk2[:BLOCK]
            dkv_ref[rows_c, lanes] += dk2[BLOCK:]
            dkv_ref[rows_p, vlanes] += dv2[:BLOCK]
            dkv_ref[rows_c, vlanes] += dv2[BLOCK:]

        @pl.when(n == nb - 1)
        def _():
            dsink_ref[...] = jnp.sum(dsink_acc[...].reshape(2 * Q_PER_KV, BLOCK, KV_PAIRS), axis=1)

    return _pcall(
        body, name=name, ins=[q, kv, kv, do, probs], dep=dep, grid=(nb,),
        in_specs=_attn_specs(c) + [pl.BlockSpec((BLOCK, dq), lambda n: (n, 0)), _probs_spec()],
        out_specs=[pl.BlockSpec((BLOCK, dq), lambda n: (n, 0)), pl.BlockSpec((S, 2 * dkv), lambda n: (0, 0)),
                   pl.BlockSpec((KV_PAIRS, PAIR_ROWS, 2 * BLOCK), lambda n: (0, 0, 0)),
                   pl.BlockSpec((2 * Q_PER_KV, KV_PAIRS), lambda n: (0, 0))],
        out_shape=[jax.ShapeDtypeStruct((S, dq), BF16), jax.ShapeDtypeStruct((S, 2 * dkv), F32),
                   jax.ShapeDtypeStruct((KV_PAIRS, PAIR_ROWS, 2 * BLOCK), F32),
                   jax.ShapeDtypeStruct((2 * Q_PER_KV, KV_PAIRS), F32)],
        scratch_shapes=[pltpu.VMEM((PAIR_ROWS, KV_PAIRS), F32), pltpu.VMEM((PAIR_ROWS, 2 * BLOCK), BF16),
                        pltpu.VMEM((PAIR_ROWS, 2 * BLOCK), BF16), pltpu.VMEM((PAIR_ROWS, LANES), BF16),
                        pltpu.VMEM((PAIR_ROWS, LANES), BF16)],
        compiler_params=_params(1))


def _sgu_bwd(c, name, a, z, dgated, ln_g, wc, wc_t, b_t, dep=None):
    S, AW, gd, tm = c.S, c.AW, c.gd, c.tms

    def body(a_ref, z_ref, dg_ref, lng_ref, wc_ref, wct_ref, bt_ref, dz_ref, dws_ref, dbt_ref, dlng_ref, dvn_ref):
        i = pl.program_id(0)

        @pl.when(i == 0)
        def _():
            dws_ref[...] = jnp.zeros_like(dws_ref)
            dbt_ref[...] = jnp.zeros_like(dbt_ref)
            dlng_ref[...] = jnp.zeros_like(dlng_ref)

        lng = lng_ref[...]
        va = a_ref[:, AW:].astype(F32)
        xc = va - jnp.mean(va, axis=-1, keepdims=True)
        rstd = lax.rsqrt(jnp.mean(xc * xc, axis=-1, keepdims=True) + EPS)
        xh = xc * rstd
        vn = (xh * lng).astype(BF16)
        causal = _sgu_masks()
        for ch in range(tm // CHUNK):
            rows = slice(ch * CHUNK, (ch + 1) * CHUNK)
            for g in range(A_GROUPS):
                cols = slice(g * gd, (g + 1) * gd)
                blk = vn[rows, cols]
                mixed = _dot(wc_ref[g], blk, NN) + bt_ref[:, g:g + 1]
                dgb = dg_ref[rows, cols].astype(F32)
                dm = dgb * a_ref[rows, cols].astype(F32)
                dbt_ref[:, g:g + 1] += jnp.sum(dm, axis=1, keepdims=True)
                dm16 = dm.astype(BF16)
                dws_ref[g] += jnp.where(causal, _dot(dm16, blk, NT), 0.0)
                dvn_ref[rows, cols] = _dot(wct_ref[g], dm16, NN)
                dz_ref[rows, cols] = (dgb * mixed * _gelu_grad(z_ref[rows, cols].astype(F32))).astype(BF16)
        dvn = dvn_ref[...]
        dlng_ref[...] += jnp.sum(dvn * xh, axis=0, keepdims=True)
        dxh = dvn * lng
        dva = rstd * (dxh - jnp.mean(dxh, axis=-1, keepdims=True) - xh * jnp.mean(dxh * xh, axis=-1, keepdims=True))
        dz_ref[:, AW:] = (dva * _gelu_grad(z_ref[:, AW:].astype(F32))).astype(BF16)

    wide = pl.BlockSpec((tm, 2 * AW), lambda i: (i, 0))
    wsp = pl.BlockSpec((A_GROUPS, CHUNK, CHUNK), lambda i: (0, 0, 0))
    btsp = pl.BlockSpec((CHUNK, A_GROUPS), lambda i: (0, 0))
    return _pcall(
        body, name=name, ins=[a, z, dgated, ln_g, wc, wc_t, b_t], dep=dep, grid=(S // tm,),
        in_specs=[wide, wide, pl.BlockSpec((tm, AW), lambda i: (i, 0)), pl.BlockSpec((1, AW), lambda i: (0, 0)),
                  wsp, wsp, btsp],
        out_specs=[wide, wsp, btsp, pl.BlockSpec((1, AW), lambda i: (0, 0))],
        out_shape=[jax.ShapeDtypeStruct((S, 2 * AW), BF16), jax.ShapeDtypeStruct((A_GROUPS, CHUNK, CHUNK), F32),
                   jax.ShapeDtypeStruct((CHUNK, A_GROUPS), F32), jax.ShapeDtypeStruct((1, AW), F32)],
        scratch_shapes=[pltpu.VMEM((tm, AW), F32)], compiler_params=_params(1))


def _adam_update(g, w_ref, m_ref, v_ref, out_refs):
    g_ref, d_ref, nm_ref, nv_ref = out_refs
    bc1 = 1.0 - ADAM_B1 ** ADAM_STEP
    bc2 = 1.0 - ADAM_B2 ** ADAM_STEP
    nm = ADAM_B1 * m_ref[...] + (1.0 - ADAM_B1) * g
    nv = ADAM_B2 * v_ref[...] + (1.0 - ADAM_B2) * (g * g)
    g_ref[...] = g
    nm_ref[...] = nm
    nv_ref[...] = nv
    d_ref[...] = -ADAM_LR * ((nm * (1.0 / bc1)) / (jnp.sqrt(nv * (1.0 / bc2)) + ADAM_EPS) + ADAM_WD * w_ref[...])


def _adamw_packed(name, sums, offs, ws, ms, vs, dep=None):
    n = len(ws)

    def body(*refs):
        s_ref, w_refs, m_refs, v_refs = refs[0], refs[1:1 + n], refs[1 + n:1 + 2 * n], refs[1 + 2 * n:1 + 3 * n]
        outs = refs[1 + 3 * n:]
        for k in range(n):
            _adam_update(s_ref[offs[k]:offs[k + 1], :], w_refs[k], m_refs[k], v_refs[k], outs[4 * k:4 * k + 4])

    ins = [sums, *ws, *ms, *vs]
    shapes = [jax.ShapeDtypeStruct(w.shape, F32) for w in ws for _ in range(4)]
    return _pcall(body, name=name, ins=ins, in_specs=[_whole(t) for t in ins], dep=dep,
                  out_shape=shapes, out_specs=[_whole(t) for t in shapes], compiler_params=_params(0))


def _adamw(name, parts, part_block, part_index, w, m, v, tr, row_off=0, n_rows=None, prev=None, dep=None):
    R, C = w.shape
    n_rows = R if n_rows is None else n_rows
    assert n_rows % tr == 0 and row_off % tr == 0

    def body(p_ref, w_ref, m_ref, v_ref, *rest):
        g = p_ref[0].astype(F32)
        for s in range(1, part_block[0]):
            g = g + p_ref[s].astype(F32)
        _adam_update(g, w_ref, m_ref, v_ref, rest[-4:])

    ob = row_off // tr
    row = pl.BlockSpec((tr, C), lambda i: (ob + i, 0))
    out = jax.ShapeDtypeStruct((R, C), F32)
    chained = prev is not None
    return _pcall(
        body, name=name, ins=[parts, w, m, v] + (list(prev) if chained else []), dep=dep, grid=(n_rows // tr,),
        in_specs=[pl.BlockSpec(part_block, part_index), row, row, row] + ([ANY] * 4 if chained else []),
        out_specs=[row, row, row, row], out_shape=[out, out, out, out],
        input_output_aliases={4 + t: t for t in range(4)} if chained else {}, compiler_params=_params(1))


def _sum_parts(name, parts, dep=None):
    def body(p_ref, out_ref):
        g = p_ref[0]
        for s in range(1, parts.shape[0]):
            g = g + p_ref[s]
        out_ref[...] = g

    return _pcall(body, name=name, ins=[parts], in_specs=[_whole(parts)], dep=dep,
                  out_shape=jax.ShapeDtypeStruct(parts.shape[1:], F32), compiler_params=_params(0))


def _place():
    return lax.axis_index("x"), lax.axis_index("y"), lax.axis_index("c")


def _slot(px, py, pc):
    return 4 * px + 2 * py + pc


def _peer(k, x, y, c):
    return x ^ ((k >> 2) & 1), y ^ ((k >> 1) & 1), c ^ (k & 1)


SEND_PEERS = {"exchange": tuple(range(1, NDEV)), "gather": (1, 2, 4, 6), "forward": (2, 4, 6),
              "broadcast": tuple(range(1, NDEV))}


def _n_sems(mode, n_lands):
    return n_lands * (len(SEND_PEERS[mode]) + (mode != "forward"))


def _send_copies(mode, src_refs, land_refs, send_sems, recv_sems):
    x, y, c = _place()
    me = _slot(x, y, c)
    peers = SEND_PEERS[mode]
    remote, local = [], []
    for i, k in enumerate(peers):
        peer = _peer(k, x, y, c)
        for a, land in enumerate(land_refs):
            if mode == "exchange":
                src, dst, to = src_refs[a].at[_slot(*peer)], land.at[me], peer
            elif mode in ("gather", "broadcast"):
                src, dst, to = src_refs[a], land.at[me], peer
            else:
                src = dst = land.at[_slot(*peer)]
                to = (x, y, 1 - c)
            s = a * len(peers) + i
            remote.append(pltpu.make_async_remote_copy(src_ref=src, dst_ref=dst, send_sem=send_sems.at[s],
                                                       recv_sem=recv_sems.at[s], device_id=to, device_id_type=MESH))
    if mode != "forward":
        for a, land in enumerate(land_refs):
            src = src_refs[a].at[me] if mode == "exchange" else src_refs[a]
            local.append(pltpu.make_async_copy(src, land.at[me], send_sems.at[len(land_refs) * len(peers) + a]))
    return remote, local


def _send_start_groups(name, groups, mode, dep=None):
    sizes = [(len(s), len(l)) for s, l in groups]
    flat = [t for s, l in groups for t in (*s, *l)]
    n_in, ng = len(flat), len(groups)

    def body(*refs):
        sems, token, at = refs[n_in:n_in + 2 * ng], refs[-1], 0
        for gi, (ns, nl) in enumerate(sizes):
            remote, local = _send_copies(mode, refs[at:at + ns], refs[at + ns:at + ns + nl], sems[2 * gi],
                                         sems[2 * gi + 1])
            for cp in remote + local:
                cp.start()
            at += ns + nl
        token[...] = jnp.zeros_like(token)

    sem_shapes = [pltpu.SemaphoreType.DMA((_n_sems(mode, nl),)) for _, nl in sizes for _ in range(2)]
    if any(dep is t for t in flat):
        dep = None
    out = _pcall(
        body, name=name, ins=[pltpu.with_memory_space_constraint(t, pltpu.HBM) for t in flat],
        in_specs=[HBM] * n_in, dep=dep,
        out_shape=(*sem_shapes, *[pltpu.HBM(t.shape, t.dtype) for t in flat], jax.ShapeDtypeStruct((8, LANES), F32)),
        out_specs=(*[SEM] * (2 * ng), *[HBM] * n_in, pl.BlockSpec(memory_space=pltpu.VMEM)),
        input_output_aliases={i: 2 * ng + i for i in range(n_in)},
        compiler_params=pltpu.CompilerParams(has_side_effects=pltpu.SideEffectType.DATAFLOW_SIDE_EFFECTING))
    started, at = [], 2 * ng
    for gi, (ns, nl) in enumerate(sizes):
        started.append((out[-1], out[2 * gi], out[2 * gi + 1], list(out[at:at + ns]), list(out[at + ns:at + ns + nl])))
        at += ns + nl
    return started


def _send_start(name, srcs, lands, mode, dep=None):
    return _send_start_groups(name, [(srcs, lands)], mode, dep=dep)[0]


def _send_wait(name, started, mode, dep=None):
    _, send_sems, recv_sems, srcs, lands = started
    n_src, n = len(srcs), len(lands)

    def body(*refs):
        src_refs, land_refs = refs[:n_src], refs[n_src:n_src + n]
        ssem, rsem = refs[n_src + n], refs[n_src + n + 1]
        remote, local = _send_copies(mode, src_refs, land_refs, ssem, rsem)
        for cp in remote:
            cp.wait_send()
            cp.wait_recv()
        for cp in local:
            cp.wait()

    thru = [pltpu.HBM(t.shape, t.dtype) for t in [*srcs, *lands]]
    out = _pcall(
        body, name=name, ins=[*srcs, *lands, send_sems, recv_sems], in_specs=[HBM] * (n_src + n) + [SEM, SEM], dep=dep,
        out_shape=tuple(thru), out_specs=tuple([HBM] * (n_src + n)),
        input_output_aliases={i: i for i in range(n_src + n)},
        compiler_params=pltpu.CompilerParams(has_side_effects=pltpu.SideEffectType.DATAFLOW_SIDE_EFFECTING))
    return list(out[n_src:])


def _landing(block):
    return lax.empty((NDEV, *block.shape), block.dtype)


def _rows128(t):
    flat = t.reshape(-1)
    n = flat.shape[0]
    rows = -(-n // (8 * LANES)) * 8
    return jnp.pad(flat, (0, rows * LANES - n)).reshape(rows, LANES)


def kernel(x, mix_norm_g, ffn_norm_g, a_w_in, a_ln_g, a_w_spatial, a_b_spatial, a_w_out, kv_norm_g, w_k, w_v, b_w_q, b_sinks, b_w_o, rel_bias, ffn_w1, ffn_w2, final_norm_g, loss_target, m_mix_norm_g, m_ffn_norm_g, m_a_w_in, m_a_ln_g, m_a_w_spatial, m_a_b_spatial, m_a_w_out, m_kv_norm_g, m_w_k, m_w_v, m_b_w_q, m_b_sinks, m_b_w_o, m_rel_bias, m_ffn_w1, m_ffn_w2, m_final_norm_g, v_mix_norm_g, v_ffn_norm_g, v_a_w_in, v_a_ln_g, v_a_w_spatial, v_a_b_spatial, v_a_w_out, v_kv_norm_g, v_w_k, v_w_v, v_b_w_q, v_b_sinks, v_b_w_o, v_rel_bias, v_ffn_w1, v_ffn_w2, v_final_norm_g):
    c = _config(x, a_w_in, a_w_out, w_k, b_w_q, b_w_o, ffn_w1, ffn_w2)
    S, D, LA, LB, LF = c.S, c.D, c.LA, c.LB, c.LF
    weights = dict(mix_norm_g=mix_norm_g, ffn_norm_g=ffn_norm_g, a_w_in=a_w_in, a_ln_g=a_ln_g, a_w_spatial=a_w_spatial,
                   a_b_spatial=a_b_spatial, a_w_out=a_w_out, kv_norm_g=kv_norm_g, w_k=w_k, w_v=w_v, b_w_q=b_w_q,
                   b_sinks=b_sinks, b_w_o=b_w_o, rel_bias=rel_bias, ffn_w1=ffn_w1, ffn_w2=ffn_w2,
                   final_norm_g=final_norm_g)
    m_in = dict(mix_norm_g=m_mix_norm_g, ffn_norm_g=m_ffn_norm_g, a_w_in=m_a_w_in, a_ln_g=m_a_ln_g,
                a_w_spatial=m_a_w_spatial, a_b_spatial=m_a_b_spatial, a_w_out=m_a_w_out, kv_norm_g=m_kv_norm_g,
                w_k=m_w_k, w_v=m_w_v, b_w_q=m_b_w_q, b_sinks=m_b_sinks, b_w_o=m_b_w_o, rel_bias=m_rel_bias,
                ffn_w1=m_ffn_w1, ffn_w2=m_ffn_w2, final_norm_g=m_final_norm_g)
    v_in = dict(mix_norm_g=v_mix_norm_g, ffn_norm_g=v_ffn_norm_g, a_w_in=v_a_w_in, a_ln_g=v_a_ln_g,
                a_w_spatial=v_a_w_spatial, a_b_spatial=v_a_b_spatial, a_w_out=v_a_w_out, kv_norm_g=v_kv_norm_g,
                w_k=v_w_k, w_v=v_w_v, b_w_q=v_b_w_q, b_sinks=v_b_sinks, b_w_o=v_b_w_o, rel_bias=v_rel_bias,
                ffn_w1=v_ffn_w1, ffn_w2=v_ffn_w2, final_norm_g=v_final_norm_g)
    names = list(weights)
    seq = _Seq()
    me = _slot(*_place())
    bf = lambda t: t.astype(BF16)

    tr = lambda t: bf(jnp.swapaxes(t, -1, -2))

    def start(tag, some):
        got = seq(_send_start_groups, f"weights_start_{tag}", [(grp, [_landing(t) for t in grp]) for grp in some],
                  "gather")
        seq.last = got[0][0]
        return got

    started = start("first", [[tr(a_w_in[0])[None], a_ln_g]])
    _, a_w_in_, a_w_out_, ffn_w1_, ffn_w2_, w_k_, w_v_, b_w_q_, b_w_o_ = lax.optimization_barrier(
        (started[0][0], a_w_in, a_w_out, ffn_w1, ffn_w2, w_k, w_v, b_w_q, b_w_o))
    groups = []
    for l in range(LA):
        groups += [[tr(a_w_in_[l])[None]], [bf(a_w_out_[l])], [tr(ffn_w1_[l])[None], bf(ffn_w2_[l])]]
    gb = 3 * LA
    for l in range(LB):
        extra = [bf(jnp.concatenate([w_k_, w_v_], axis=1))] if l == 0 else []
        groups += [extra + [bf(b_w_q_[l]), bf(b_w_o_[l])], [tr(ffn_w1_[LA + l])[None], bf(ffn_w2_[LA + l])]]
    started += start("rest", groups[1:])
    forwarding = {}

    def forward(i):
        lands = seq(_send_wait, f"weights_wait{i}", started[i], "gather")
        forwarding[i] = seq(_send_start, f"weights_forward{i}", [], lands, "forward")

    def arrive(i):
        if i not in forwarding:
            forward(i)
        return seq(_send_wait, f"weights_arrive{i}", forwarding[i], "forward")

    causal = jnp.tril(jnp.ones((CHUNK, CHUNK), bool))
    wsp = jnp.where(causal[None, None], a_w_spatial, 0.0)
    wsp16 = wsp.astype(BF16)
    wsp16_t = jnp.swapaxes(wsp, -1, -2).astype(BF16)
    bsp_t = jnp.swapaxes(a_b_spatial, -1, -2)
    mix_g = mix_norm_g.reshape(-1, 1, D)
    ffn_g = ffn_norm_g.reshape(-1, 1, D)
    kv_g = kv_norm_g.reshape(1, D)
    fin_g = final_norm_g.reshape(1, D)
    onehot = _bucket_onehot()
    bias = _slot_bias(seq(_band_bias, rel_bias.T, onehot).reshape(N_HEADS, BLOCK, 2 * BLOCK))

    h = x.reshape(S, D)
    sav_a, sav_b, wts_a, wts_b = [], [], [], []
    for l in range(LA):
        got = arrive(3 * l)
        w_in = got[0]
        if l == 0:
            ln_g_full = jnp.transpose(got[1], (1, 0, 2)).reshape(LA, 1, c.AW)
        z, a, hn = seq(_a_in_fwd, c, f"a_in_fwd{l}", h, mix_g[l], w_in, 0)
        forward(3 * l + 1)
        gated = seq(_sgu_fwd, c, f"sgu_fwd{l}", a, ln_g_full[l], wsp16[l], bsp_t[l])
        (wout,) = arrive(3 * l + 1)
        if l > 0:
            forward(3 * l + 2)
        h1 = seq(_mm_res, c, f"a_out_fwd{l}", gated, wout, c.ar, 0, h)
        w1, rows = arrive(3 * l + 2)
        if l == LA - 1:
            forward(gb)
        p, h2, hnf = seq(_ffn_fwd, c, f"ffn_fwd{l}", h1, ffn_g[l], w1, 0, rows)
        sav_a.append((h, z, a, hn, gated, h1, p, hnf))
        wts_a.append((w_in, 0, w1, 0, rows, wout, 0))
        h = h2
    h_kv = h
    for l in range(LB):
        got = arrive(gb + 2 * l)
        if l == 0:
            wkv, got = got[0], got[1:]
            kv, hkv = seq(_rms_mm_rows, c, "kv_fwd", h, kv_g, wkv, c.kr, 0, 2 * c.DKV)
        wq, wo = _slot_cols(got[0]), _slot_rows(got[1])
        q, hn = seq(_rms_mm_rows, c, f"q_fwd{l}", h, mix_g[LA + l], wq, c.qr, 0, c.DQ)
        forward(gb + 2 * l + 1)
        o, probs = seq(_attn_fwd, c, f"attn_fwd{l}", q, kv, bias, b_sinks[l])
        h1 = seq(_mm_res, c, f"o_fwd{l}", o, wo, c.orr, 0, h)
        w1, rows = arrive(gb + 2 * l + 1)
        if l + 1 < LB:
            forward(gb + 2 * l + 2)
        p, h2, hnf = seq(_ffn_fwd, c, f"ffn_fwd{LA + l}", h1, ffn_g[LA + l], w1, 0, rows)
        sav_b.append((h, q, hn, o, probs, h1, p, hnf))
        wts_b.append((wq, wo, w1, rows))
        h = h2
    dh, d_fin_g, loss_row = seq(_final_loss, c, h, fin_g, loss_target.reshape(S, D))

    results = {}
    in_flight = []

    def update(k, parts, layer, col_blk=0):
        w = weights[k]
        rows_l, ncols = (w.shape[-2], w.shape[-1]) if w.ndim == 3 else w.shape
        flat = lambda t: t.reshape(-1, ncols)
        tr = min(256, rows_l)
        results[k] = seq(_adamw, f"adamw_{k}{layer}", parts, (NDEV, tr, ncols), lambda i: (0, i, col_blk),
                         flat(w), flat(m_in[k]), flat(v_in[k]), tr, row_off=layer * rows_l, n_rows=rows_l,
                         prev=results.get(k))

    def land(tag, entry):
        lands = seq(_send_wait, f"grads_wait_{tag}", entry[1], "exchange")
        for keys, parts in zip(entry[0], lands):
            for k, layer, col_blk in keys:
                update(k, parts, layer, col_blk)

    def send(tag, items):
        slabs = [t for _, t in items]
        own = [lax.empty(t.shape, t.dtype) for t in slabs]
        st = seq(_send_start, f"grads_start_{tag}", slabs, own, "exchange")
        in_flight.append((tag, ([k for k, _ in items], st)))
        while len(in_flight) > EXCHANGE_LAG:
            land(*in_flight.pop(0))

    d_mix_g, d_ffn_g = [None] * LF, [None] * LF
    dkv_list, dbias_list, dsink_list = [], [], [None] * LB

    def ffn_bwd(lf, dh, h1, p, hnf, w1, w1_i, rows):
        da, dh1, d_ffn_g[lf], dhb = seq(_ffn_bwd_data, c, f"ffn_bwd_data{lf}", dh, p, w1, w1_i, rows, h1, ffn_g[lf])
        dw1, dw2 = seq(_ffn_bwd_w, c, f"ffn_bwd_w{lf}", hnf, da, p, dhb)
        send(f"ffn{lf}", [([("ffn_w1", lf, 0)], dw1), ([("ffn_w2", lf, 0)], dw2)])
        return dh1

    for l in reversed(range(LB)):
        h0, q, hn, o, probs, h1, p, hnf = sav_b[l]
        wq, wo, w1, rows = wts_b[l]
        dh1 = ffn_bwd(LA + l, dh, h1, p, hnf, w1, 0, rows)
        do = seq(_bwd_rows_data, c, f"o_bwd_data{l}", dh1, wo, c.orr, 0)
        dwo = _unslot_rows(seq(_wgrad_rows, c, f"o_bwd_w{l}", o, [dh1], c.DQ, D))
        dq, dkv, dbias, dsink = seq(_attn_bwd, c, f"attn_bwd{l}", q, kv, do, probs)
        dsink_list[l] = dsink.reshape(Q_PER_KV, 2, KV_PAIRS).transpose(2, 1, 0).reshape(1, N_HEADS)
        dkv_list.append(dkv)
        dbias_list.append(_unslot_bias(dbias))
        dwq = _unslot_cols(seq(_wgrad_rows, c, f"q_bwd_w{l}", hn, [dq], D, c.DQ))
        send(f"attn{l}", [([("b_w_o", l, 0)], dwo), ([("b_w_q", l, 0)], dwq)])
        dh, d_mix_g[LA + l] = seq(_bwd_rows_to_stream, c, f"q_bwd_data{l}", [dq], wq, c.qr, 0, c.DQ, h0,
                                  mix_g[LA + l], dh1)
    dwkv = seq(_wgrad_rows, c, "kv_bwd_w", hkv, dkv_list, D, 2 * c.DKV)
    send("kv", [([("w_k", 0, 0), ("w_v", 0, 1)], dwkv)])
    dh, d_kv_g = seq(_bwd_rows_to_stream, c, "kv_bwd_data", dkv_list, wkv, c.kr, 0, 2 * c.DKV, h_kv, kv_g, dh)
    d_rel_t = seq(_band_bias_grad, dbias_list, onehot)
    d_wsp, d_bsp, d_lng = [None] * LA, [None] * LA, [None] * LA
    for l in reversed(range(LA)):
        h0, z, a, hn, gated, h1, p, hnf = sav_a[l]
        w_in, in_i, w1, w1_i, rows, wout, wout_i = wts_a[l]
        dh1 = ffn_bwd(l, dh, h1, p, hnf, w1, w1_i, rows)
        dgated = seq(_bwd_rows_data, c, f"a_out_bwd_data{l}", dh1, wout, c.ar, wout_i)
        dwout = seq(_wgrad_rows, c, f"a_out_bwd_w{l}", gated, [dh1], c.AW, D)
        send(f"a_out{l}", [([("a_w_out", l, 0)], dwout)])
        dz, d_wsp[l], dbt, d_lng[l] = seq(_sgu_bwd, c, f"sgu_bwd{l}", a, z, dgated, ln_g_full[l], wsp16[l],
                                          wsp16_t[l], bsp_t[l])
        d_bsp[l] = dbt.T
        dwin = seq(_wgrad_cols, c, f"a_in_bwd_w{l}", hn, dz)
        send(f"a_in{l}", [([("a_w_in", l, 0)], dwin)])
        dh, d_mix_g[l] = seq(_bwd_cols_to_stream, c, f"a_in_bwd_data{l}", dz, w_in, in_i, h0, mix_g[l], dh1)
    grad_x = dh.reshape(1, S, D)

    small = {
        "mix_norm_g": jnp.concatenate(d_mix_g, axis=0), "ffn_norm_g": jnp.concatenate(d_ffn_g, axis=0),
        "a_w_spatial": jnp.stack(d_wsp), "a_b_spatial": jnp.stack(d_bsp), "kv_norm_g": d_kv_g,
        "b_sinks": jnp.concatenate(dsink_list, axis=0), "rel_bias": d_rel_t.T, "final_norm_g": d_fin_g,
    }
    small_names = list(small)
    packs = [_rows128(small[k]) for k in small_names] + [_rows128(jnp.concatenate(d_lng, axis=0)), _rows128(loss_row)]
    offs = [int(o) for o in np.cumsum([0] + [p.shape[0] for p in packs])]
    Rs = offs[-1] + (-offs[-1]) % (8 * NDEV)
    packed = jnp.concatenate(packs + [jnp.zeros((Rs - offs[-1], LANES), F32)], axis=0)
    slab = packed.reshape(NDEV, Rs // NDEV, LANES)
    st = seq(_send_start, "small_grads_start", [slab], [lax.empty(slab.shape, slab.dtype)], "exchange")
    while len(in_flight) > 1:
        land(*in_flight.pop(0))
    (parts,) = seq(_send_wait, "small_grads_wait", st, "exchange")
    mine = seq(_sum_parts, "small_grads_sum", parts)
    st = seq(_send_start, "small_sums_start", [mine], [_landing(mine)], "broadcast")
    while in_flight:
        land(*in_flight.pop(0))
    (sums,) = seq(_send_wait, "small_sums_wait", st, "broadcast")
    sums = sums.reshape(Rs, LANES)
    loss = sums[offs[-2], 0]

    grads, deltas, new_m, new_v = {}, {}, {}, {}

    def put(k, outs, shape):
        grads[k], deltas[k], new_m[k], new_v[k] = (t.reshape(shape) for t in outs)

    outs = seq(_adamw_packed, "adamw_small", sums, offs, *[[_rows128(d[k]) for k in small_names]
                                                          for d in (weights, m_in, v_in)])
    for n_, k in enumerate(small_names):
        shape = weights[k].shape
        size = int(np.prod(shape))
        put(k, [t.reshape(-1)[:size] for t in outs[4 * n_:4 * n_ + 4]], shape)
    lng_sum = sums[offs[-3]:offs[-2]].reshape(-1)[:LA * c.AW].reshape(LA, c.AW)
    lng_mine = lax.dynamic_slice_in_dim(lng_sum, me * c.ar, c.ar, axis=1)
    lng_parts = jnp.concatenate([lng_mine[None], jnp.zeros((NDEV - 1, LA, c.ar), F32)], axis=0)
    put("a_ln_g", seq(_adamw, "adamw_ln_g", lng_parts, (NDEV, LA, c.ar), lambda i: (0, 0, 0),
                      a_ln_g, m_in["a_ln_g"], v_in["a_ln_g"], LA), a_ln_g.shape)
    for k in ("a_w_in", "ffn_w1", "ffn_w2", "a_w_out", "b_w_o", "b_w_q", "w_k", "w_v"):
        put(k, results[k], weights[k].shape)

    return (loss, grad_x, *[grads[k] for k in names], *[deltas[k] for k in names],
            *[new_m[k] for k in names], *[new_v[k] for k in names])
```

```python
import numpy as np
import math
import jax
import jax.numpy as jnp
from jax import lax
from jax.experimental import pallas as pl
from jax.experimental.pallas import tpu as pltpu

F32 = jnp.float32
BF16 = jnp.bfloat16

NDEV = 8
EPS = 1e-6
CHUNK = 128
A_GROUPS = 8
N_HEADS = 16
N_KV_HEADS = 4
Q_PER_KV = N_HEADS // N_KV_HEADS
HEAD_DIM = 64
BLOCK = 128
N_BUCKETS = 32
MAX_DISTANCE = 128
ADAM_LR, ADAM_B1, ADAM_B2, ADAM_EPS, ADAM_WD, ADAM_STEP = 0.001, 0.9, 0.999, 1e-08, 0.01, 10
LANES = 128
VMEM_LIMIT = 56 * 1024 * 1024
INV_SQRT2 = 0.7071067811865476
INV_SQRT_2PI = 0.3989422804014327
EXCHANGE_LAG = 4
SIBLING_COLLECTIVE_ID = 0

HBM = pl.BlockSpec(memory_space=pltpu.HBM)
SMEM = pl.BlockSpec(memory_space=pltpu.SMEM)
ANY = pl.BlockSpec(memory_space=pl.ANY)
SEM = pl.BlockSpec(memory_space=pltpu.SEMAPHORE)
MESH = pl.DeviceIdType.MESH


def _params(n_grid):
    return pltpu.CompilerParams(dimension_semantics=("arbitrary",) * n_grid, vmem_limit_bytes=VMEM_LIMIT)


def _const(block, index_map):
    return pl.BlockSpec(block, index_map, pipeline_mode=pl.Buffered(1))


def _pcall(body, *, ins, in_specs, dep=None, **kw):
    n_in = len(ins)
    if dep is None or any(dep is t for t in ins):
        return pl.pallas_call(body, in_specs=list(in_specs), **kw)(*ins)

    def with_dep(*refs):
        body(*refs[:n_in], *refs[n_in + 1:])

    return pl.pallas_call(with_dep, in_specs=[*in_specs, ANY], **kw)(*ins, dep)


class _Seq:
    def __init__(self):
        self.last = None

    def __call__(self, fn, *args, **kw):
        out = fn(*args, dep=self.last, **kw)
        self.last = out[0] if isinstance(out, (tuple, list)) else out
        return out


def _rstd(h):
    return lax.rsqrt(jnp.mean(h * h, axis=-1, keepdims=True) + EPS)


def _rms_bwd(dhn, h, g, dres):
    r = _rstd(h)
    xh = h * r
    dg = jnp.sum(dhn * xh, axis=0, keepdims=True)
    dxh = dhn * g
    dx = r * (dxh - xh * jnp.mean(dxh * xh, axis=-1, keepdims=True))
    return dres + dx, dg


def _gelu(z):
    return 0.5 * z * (1.0 + lax.erf(z * INV_SQRT2))


def _gelu_grad(z):
    return 0.5 * (1.0 + lax.erf(z * INV_SQRT2)) + z * (jnp.exp(-0.5 * z * z) * INV_SQRT_2PI)


def _dot(a, b, dims):
    return lax.dot_general(a, b, (dims, ((), ())), preferred_element_type=F32)


NN = ((1,), (0,))
NT = ((1,), (1,))
TN = ((0,), (0,))


def _mm(name, ins, in_specs, out_shapes, out_specs, *, grid, dims, nk, acc_shape, load_a, load_b, epilogue,
        dep=None):
    n_in, n_out = len(ins), len(out_shapes)
    kax = len(grid) - 1

    def body(*refs):
        in_refs = refs[:n_in]
        out_refs = refs[n_in:n_in + n_out]
        a = load_a(in_refs, out_refs)
        b = load_b(in_refs)
        prod = _dot(a, b, dims)
        if nk == 1:
            epilogue(prod, in_refs, out_refs)
        else:
            acc = refs[n_in + n_out]
            k = pl.program_id(kax)

            @pl.when(k == 0)
            def _():
                acc[...] = prod

            @pl.when(k > 0)
            def _():
                acc[...] += prod

            @pl.when(k == nk - 1)
            def _():
                epilogue(acc[...], in_refs, out_refs)

    return _pcall(
        body, name=name, ins=ins, in_specs=in_specs, dep=dep, grid=grid, out_specs=out_specs, out_shape=out_shapes,
        scratch_shapes=[pltpu.VMEM(acc_shape, F32)] if nk > 1 else [], compiler_params=_params(len(grid)))


def _bf(ref_idx):
    return lambda in_refs, *_: in_refs[ref_idx][...].astype(BF16)


def _b_view(ref_idx, rows):
    def load(in_refs):
        b = in_refs[ref_idx][...]
        return b.reshape(rows, b.shape[-1])
    return load


class Cfg:
    pass


def _config(x, a_w_in, a_w_out, w_k, b_w_q, b_w_o, ffn_w1, ffn_w2):
    c = Cfg()
    c.S, c.D = x.shape[1], x.shape[2]
    c.LA, _, c.cw = a_w_in.shape
    c.AW2 = NDEV * c.cw
    c.AW = c.AW2 // 2
    c.gd = c.AW // A_GROUPS
    c.ar = a_w_out.shape[1]
    c.LF, _, c.fw = ffn_w1.shape
    c.fr = ffn_w2.shape[1]
    c.LB, c.qr, c.DQ = b_w_q.shape
    c.orr = b_w_o.shape[1]
    c.kr, c.DKV = w_k.shape
    c.tm = min(1024, c.S)
    c.tmw = min(512, c.S)
    c.tms = min(256, c.S)
    c.nb = c.S // BLOCK
    assert c.cw == c.fw == c.fr and c.AW == NDEV * c.ar and c.D == NDEV * c.qr == NDEV * c.kr
    assert c.DQ == NDEV * c.orr == N_HEADS * HEAD_DIM and c.DKV == N_KV_HEADS * HEAD_DIM
    assert c.S % c.tm == 0 and c.S % c.tmw == 0 and c.S % c.tms == 0 and c.tms % CHUNK == 0 and c.gd % LANES == 0
    assert c.LA >= 1 and c.LB >= 1 and c.LF == c.LA + c.LB
    return c


def _cached_rms(h_idx, g_idx, hn_out_idx, jax_axis=1):
    def load(in_refs, out_refs):
        hn_ref = out_refs[hn_out_idx]

        @pl.when(pl.program_id(jax_axis) == 0)
        def _():
            h = in_refs[h_idx][...]
            hn_ref[...] = (h * _rstd(h) * in_refs[g_idx][...]).astype(BF16)

        return hn_ref[...]
    return load


def _a_in_fwd(c, name, h, g, col, ci, dep=None):
    S, D, cw, tm = c.S, c.D, c.cw, c.tmw

    def body(h_ref, g_ref, w_ref, z_ref, a_ref, hn_ref):
        h = h_ref[...]
        hn = (h * _rstd(h) * g_ref[...]).astype(BF16)
        hn_ref[...] = hn
        for j in range(NDEV):
            cols = slice(j * cw, (j + 1) * cw)
            z = _dot(hn, w_ref[j], NT)
            z_ref[:, cols] = z.astype(BF16)
            a_ref[:, cols] = _gelu(z).astype(BF16)

    row = pl.BlockSpec((tm, D), lambda i: (i, 0))
    wide = pl.BlockSpec((tm, c.AW2), lambda i: (i, 0))
    return _pcall(
        body, name=name, ins=[h, g, col], dep=dep, grid=(S // tm,),
        in_specs=[row, pl.BlockSpec((1, D), lambda i: (0, 0)), _const((NDEV, None, cw, D), lambda i: (0, ci, 0, 0))],
        out_specs=[wide, wide, row],
        out_shape=[jax.ShapeDtypeStruct((S, c.AW2), BF16), jax.ShapeDtypeStruct((S, c.AW2), BF16),
                   jax.ShapeDtypeStruct((S, D), BF16)],
        compiler_params=_params(1))


def _rms_mm_rows(c, name, h, g, slab, blk_rows, blk_idx, n_out, dep=None):
    S, D, tm = c.S, c.D, c.tm

    def epilogue(acc, in_refs, out_refs):
        out_refs[0][...] = acc.astype(BF16)

    return _mm(
        name, [h, slab, g],
        [pl.BlockSpec((tm, D), lambda i, j, k: (i, 0)),
         pl.BlockSpec((NDEV, blk_rows, n_out), lambda i, j, k: (0, blk_idx, 0)),
         pl.BlockSpec((1, D), lambda i, j, k: (0, 0))],
        [jax.ShapeDtypeStruct((S, n_out), BF16), jax.ShapeDtypeStruct((S, D), BF16)],
        [pl.BlockSpec((tm, n_out), lambda i, j, k: (i, 0)), pl.BlockSpec((tm, D), lambda i, j, k: (i, 0))],
        grid=(S // tm, 1, 1), dims=NN, nk=1, acc_shape=None,
        load_a=_cached_rms(0, 2, 1), load_b=_b_view(1, NDEV * blk_rows), epilogue=epilogue, dep=dep)


def _mm_res(c, name, a, slab, blk_rows, blk_idx, res, dep=None):
    S, D, tm = c.S, c.D, c.tm
    K = NDEV * blk_rows

    def epilogue(acc, in_refs, out_refs):
        out_refs[0][...] = in_refs[2][...] + acc

    return _mm(
        name, [a, slab, res],
        [pl.BlockSpec((tm, K), lambda i, j, k: (i, 0)),
         pl.BlockSpec((NDEV, blk_rows, D), lambda i, j, k: (0, blk_idx, 0)),
         pl.BlockSpec((tm, D), lambda i, j, k: (i, 0))],
        [jax.ShapeDtypeStruct((S, D), F32)], [pl.BlockSpec((tm, D), lambda i, j, k: (i, 0))],
        grid=(S // tm, 1, 1), dims=NN, nk=1, acc_shape=None,
        load_a=_bf(0), load_b=_b_view(1, K), epilogue=epilogue, dep=dep)[0]


def _sgu_masks():
    ii = lax.broadcasted_iota(jnp.int32, (CHUNK, CHUNK), 0)
    jj = lax.broadcasted_iota(jnp.int32, (CHUNK, CHUNK), 1)
    return ii >= jj


def _sgu_fwd(c, name, a, ln_g, wc, b_t, dep=None):
    S, AW, gd, tm = c.S, c.AW, c.gd, c.tms

    def body(a_ref, lng_ref, wc_ref, bt_ref, out_ref):
        va = a_ref[:, AW:].astype(F32)
        xc = va - jnp.mean(va, axis=-1, keepdims=True)
        vn = (xc * lax.rsqrt(jnp.mean(xc * xc, axis=-1, keepdims=True) + EPS) * lng_ref[...]).astype(BF16)
        for ch in range(tm // CHUNK):
            rows = slice(ch * CHUNK, (ch + 1) * CHUNK)
            for g in range(A_GROUPS):
                cols = slice(g * gd, (g + 1) * gd)
                mixed = _dot(wc_ref[g], vn[rows, cols], NN) + bt_ref[:, g:g + 1]
                out_ref[rows, cols] = (a_ref[rows, cols].astype(F32) * mixed).astype(BF16)

    return _pcall(
        body, name=name, ins=[a, ln_g, wc, b_t], dep=dep, grid=(S // tm,),
        in_specs=[pl.BlockSpec((tm, 2 * AW), lambda i: (i, 0)), pl.BlockSpec((1, AW), lambda i: (0, 0)),
                  pl.BlockSpec((A_GROUPS, CHUNK, CHUNK), lambda i: (0, 0, 0)),
                  pl.BlockSpec((CHUNK, A_GROUPS), lambda i: (0, 0))],
        out_specs=pl.BlockSpec((tm, AW), lambda i: (i, 0)),
        out_shape=jax.ShapeDtypeStruct((S, AW), BF16), compiler_params=_params(1))


def _ffn_fwd(c, name, h, g, col, ci, rows, dep=None):
    S, D, fw, tm = c.S, c.D, c.fw, c.tmw
    F = NDEV * fw

    def body(h_ref, g_ref, w1_ref, w2_ref, p_ref, out_ref, hn_ref, r_ref):
        h = h_ref[...]
        hn = (h * _rstd(h) * g_ref[...]).astype(BF16)
        hn_ref[...] = hn
        for j in range(NDEV):
            cols = slice(j * fw, (j + 1) * fw)
            p = jnp.maximum(_dot(hn, w1_ref[j], NT), 0.0)
            p_ref[:, cols] = p.astype(BF16)
            r_ref[:, cols] = (p * p).astype(BF16)
        out_ref[...] = h + _dot(r_ref[...], w2_ref[...].reshape(F, D), NN)

    row = pl.BlockSpec((tm, D), lambda i: (i, 0))
    return _pcall(
        body, name=name, ins=[h, g, col, rows], dep=dep, grid=(S // tm,),
        in_specs=[row, pl.BlockSpec((1, D), lambda i: (0, 0)),
                  _const((NDEV, None, fw, D), lambda i: (0, ci, 0, 0)), _const((NDEV, c.fr, D), lambda i: (0, 0, 0))],
        out_specs=[pl.BlockSpec((tm, F), lambda i: (i, 0)), row, row],
        out_shape=[jax.ShapeDtypeStruct((S, F), BF16), jax.ShapeDtypeStruct((S, D), F32),
                   jax.ShapeDtypeStruct((S, D), BF16)],
        scratch_shapes=[pltpu.VMEM((tm, F), BF16)], compiler_params=_params(1))


def _bucket_table():
    qi = np.arange(BLOCK)[:, None]
    kj = np.arange(2 * BLOCK)[None, :]
    d = np.maximum(qi + BLOCK - kj, 0)
    max_exact = N_BUCKETS // 2
    ratio = np.log(np.maximum(d, 1).astype(np.float32) / np.float32(max_exact)) / np.float32(
        math.log(MAX_DISTANCE / max_exact))
    large = np.minimum(max_exact + (ratio.astype(np.float32) * np.float32(N_BUCKETS - max_exact)).astype(np.int32),
                       N_BUCKETS - 1)
    return np.where(d < max_exact, d, large).astype(np.int32)


def _bucket_onehot():
    b = jnp.asarray(_bucket_table().reshape(1, -1))
    return (b == lax.broadcasted_iota(jnp.int32, (N_BUCKETS, b.shape[1]), 0)).astype(F32)


def _whole(t):
    return pl.BlockSpec(t.shape, lambda: (0,) * t.ndim)


def _band_bias(rel_bias_t, onehot, dep=None):
    def body(r_ref, oh_ref, out_ref):
        out_ref[...] = lax.dot_general(r_ref[...], oh_ref[...], (NN, ((), ())), preferred_element_type=F32,
                                       precision=lax.Precision.HIGHEST)

    n = onehot.shape[1]
    return _pcall(body, name="band_bias", ins=[rel_bias_t, onehot], in_specs=[_whole(rel_bias_t), _whole(onehot)],
                  dep=dep, out_shape=jax.ShapeDtypeStruct((N_HEADS, n), F32), compiler_params=_params(0))


def _band_bias_grad(dbias_list, onehot, dep=None):
    n_in = len(dbias_list)

    def body(*refs):
        oh_ref, out_ref = refs[n_in], refs[n_in + 1]
        d = refs[0][...]
        for r in refs[1:n_in]:
            d = d + r[...]
        out_ref[...] = lax.dot_general(d, oh_ref[...], (NT, ((), ())), preferred_element_type=F32,
                                       precision=lax.Precision.HIGHEST)

    ins = [*dbias_list, onehot]
    return _pcall(body, name="band_bias_grad", ins=ins, in_specs=[_whole(t) for t in ins], dep=dep,
                  out_shape=jax.ShapeDtypeStruct((N_HEADS, N_BUCKETS), F32), compiler_params=_params(0))


KV_PAIRS = N_KV_HEADS // 2
PAIR_ROWS = 2 * Q_PER_KV * BLOCK
MASKED = float(np.finfo(np.float32).min) / 2


def _slot_cols(w):
    lead = w.shape[:-1]
    return w.reshape(*lead, KV_PAIRS, 2, Q_PER_KV, HEAD_DIM).swapaxes(-3, -2).reshape(*lead, N_HEADS * HEAD_DIM)


def _unslot_cols(w):
    lead = w.shape[:-1]
    return w.reshape(*lead, KV_PAIRS, Q_PER_KV, 2, HEAD_DIM).swapaxes(-3, -2).reshape(*lead, N_HEADS * HEAD_DIM)


def _slot_rows(blocks):
    n = blocks.shape[-1]
    return blocks.reshape(KV_PAIRS, 2, Q_PER_KV, HEAD_DIM, n).swapaxes(1, 2).reshape(blocks.shape)


def _unslot_rows(blocks):
    n = blocks.shape[-1]
    return blocks.reshape(KV_PAIRS, Q_PER_KV, 2, HEAD_DIM, n).swapaxes(1, 2).reshape(blocks.shape)


def _slot_bias(bias):
    qi = np.arange(BLOCK)[:, None]
    kj = np.arange(2 * BLOCK)[None, :]
    dist = qi + BLOCK - kj
    window = (dist >= 0) & (dist < BLOCK)
    b = bias.reshape(KV_PAIRS, 2, Q_PER_KV, BLOCK, 2 * BLOCK).swapaxes(1, 2).reshape(KV_PAIRS, PAIR_ROWS, 2 * BLOCK)
    tile = lambda mk: jnp.asarray(np.tile(mk, (2 * Q_PER_KV, 1)))[None]
    return jnp.stack([jnp.where(tile(window & (kj >= BLOCK)), b, MASKED), jnp.where(tile(window), b, MASKED)])


def _unslot_bias(db):
    return db.reshape(KV_PAIRS, Q_PER_KV, 2, BLOCK, 2 * BLOCK).swapaxes(1, 2).reshape(N_HEADS, -1)


def _pair_kv(kvc_ref, kvp_ref, kvp, dkv):
    lanes = slice(kvp * LANES, (kvp + 1) * LANES)
    vlanes = slice(dkv + kvp * LANES, dkv + (kvp + 1) * LANES)
    k2 = jnp.concatenate([kvp_ref[:, lanes], kvc_ref[:, lanes]], axis=0)
    v2 = jnp.concatenate([kvp_ref[:, vlanes], kvc_ref[:, vlanes]], axis=0)
    return k2, v2


def _head_operand(ref, grp, par, low, scale=None):
    xg = ref[:, grp * LANES:(grp + 1) * LANES]
    if scale is not None:
        xg = xg * scale
    zero = jnp.zeros_like(xg)
    return jnp.where(low, xg, zero) if par == 0 else jnp.where(low, zero, xg)


def _head_probs(qh, k2, bias_rows, sink):
    s = _dot(qh, k2, NT) + bias_rows
    m = jnp.maximum(jnp.max(s, axis=-1, keepdims=True), sink)
    e = jnp.exp(s - m)
    es = jnp.exp(sink - m)
    inv = 1.0 / (jnp.sum(e, axis=-1, keepdims=True) + es)
    return e * inv, es * inv


def _attn_specs(c):
    dq, dkv2 = c.DQ, 2 * c.DKV
    return [pl.BlockSpec((BLOCK, dq), lambda n: (n, 0)),
            pl.BlockSpec((BLOCK, dkv2), lambda n: (n, 0)),
            pl.BlockSpec((BLOCK, dkv2), lambda n: (jnp.maximum(n - 1, 0), 0))]


def _bias_spec():
    return pl.BlockSpec((None, KV_PAIRS, PAIR_ROWS, 2 * BLOCK), lambda n: (jnp.minimum(n, 1), 0, 0, 0))


def _low_lanes():
    return lax.broadcasted_iota(jnp.int32, (BLOCK, LANES), 1) < HEAD_DIM


def _first_key():
    return lax.broadcasted_iota(jnp.int32, (BLOCK, 2 * BLOCK), 1) == 0


def _probs_spec():
    return pl.BlockSpec((None, KV_PAIRS, PAIR_ROWS, 2 * BLOCK), lambda n: (n, 0, 0, 0))


def _attn_fwd(c, name, q, kv, bias, sinks, dep=None):
    S, dq = c.S, c.DQ

    def body(q_ref, kvc_ref, kvp_ref, bias_ref, sink_ref, o_ref, probs_ref):
        low = _low_lanes()
        first = _first_key()
        for kvp in range(KV_PAIRS):
            k2, v2 = _pair_kv(kvc_ref, kvp_ref, kvp, c.DKV)
            for g in range(Q_PER_KV):
                grp = kvp * Q_PER_KV + g
                halves = []
                for par in range(2):
                    rows = slice((2 * g + par) * BLOCK, (2 * g + par + 1) * BLOCK)
                    qh = _head_operand(q_ref, grp, par, low, scale=HEAD_DIM ** -0.5)
                    p, ps = _head_probs(qh, k2, bias_ref[kvp, rows, :], sink_ref[(2 * kvp + par) * Q_PER_KV + g])
                    probs_ref[kvp, rows, :] = jnp.where(first, ps, p).astype(BF16)
                    halves.append(_dot(p.astype(BF16), v2, NN))
                o_ref[:, grp * LANES:(grp + 1) * LANES] = jnp.where(low, halves[0], halves[1]).astype(BF16)

    return _pcall(
        body, name=name, ins=[q, kv, kv, bias, sinks], dep=dep, grid=(c.nb,),
        in_specs=_attn_specs(c) + [_bias_spec(), SMEM],
        out_specs=[pl.BlockSpec((BLOCK, dq), lambda n: (n, 0)), _probs_spec()],
        out_shape=[jax.ShapeDtypeStruct((S, dq), BF16),
                   jax.ShapeDtypeStruct((c.nb, KV_PAIRS, PAIR_ROWS, 2 * BLOCK), BF16)],
        compiler_params=_params(1))


def _final_loss(c, h, g, target, dep=None):
    S, D, tm = c.S, c.D, c.tm

    def body(h_ref, g_ref, t_ref, dh_ref, dg_ref, loss_ref):
        i = pl.program_id(0)
        h = h_ref[...]
        gg = g_ref[...]
        r = _rstd(h)
        xh = h * r
        err = xh * gg - t_ref[...]
        lp = jnp.sum(jnp.sum(err * err, axis=1, keepdims=True), axis=0, keepdims=True) * (0.5 / D)
        dx, dg = _rms_bwd(err * (1.0 / D), h, gg, 0.0)
        dh_ref[...] = dx

        @pl.when(i == 0)
        def _():
            dg_ref[...] = dg
            loss_ref[...] = jnp.broadcast_to(lp, loss_ref.shape)

        @pl.when(i > 0)
        def _():
            dg_ref[...] += dg
            loss_ref[...] += jnp.broadcast_to(lp, loss_ref.shape)

    row = pl.BlockSpec((tm, D), lambda i: (i, 0))
    return _pcall(
        body, name="final_loss", ins=[h, g, target], dep=dep, grid=(S // tm,),
        in_specs=[row, pl.BlockSpec((1, D), lambda i: (0, 0)), row],
        out_specs=[row, pl.BlockSpec((1, D), lambda i: (0, 0)), pl.BlockSpec((1, LANES), lambda i: (0, 0))],
        out_shape=[jax.ShapeDtypeStruct((S, D), F32), jax.ShapeDtypeStruct((1, D), F32),
                   jax.ShapeDtypeStruct((1, LANES), F32)],
        compiler_params=_params(1))


def _rms_bwd_epilogue(h_idx, g_idx, res_idx):
    def epilogue(dhn, in_refs, out_refs):
        dh, dg = _rms_bwd(dhn, in_refs[h_idx][...], in_refs[g_idx][...], in_refs[res_idx][...])
        out_refs[0][...] = dh
        i = pl.program_id(0)

        @pl.when(i == 0)
        def _():
            out_refs[1][...] = dg

        @pl.when(i > 0)
        def _():
            out_refs[1][...] += dg
    return epilogue


def _stream_outs(c, tm):
    S, D = c.S, c.D
    return ([jax.ShapeDtypeStruct((S, D), F32), jax.ShapeDtypeStruct((1, D), F32)],
            [pl.BlockSpec((tm, D), lambda i, j, k: (i, 0)), pl.BlockSpec((1, D), lambda i, j, k: (0, 0))])


def _row_specs(c, tm):
    D = c.D
    return [pl.BlockSpec((tm, D), lambda i, j, k: (i, 0)), pl.BlockSpec((1, D), lambda i, j, k: (0, 0)),
            pl.BlockSpec((tm, D), lambda i, j, k: (i, 0))]


def _bwd_rows_to_stream(c, name, dy_list, slab, blk_rows, blk_idx, n_in_cols, h, g, dres, dep=None):
    S, D, tm = c.S, c.D, c.tm
    nd = len(dy_list)

    def load_a(in_refs, out_refs):
        a = in_refs[0][...]
        for r in in_refs[1:nd]:
            a = a + r[...]
        return a.astype(BF16)

    shapes, specs = _stream_outs(c, tm)
    return _mm(
        name, [*dy_list, slab, h, g, dres],
        [pl.BlockSpec((tm, n_in_cols), lambda i, j, k: (i, 0))] * nd
        + [pl.BlockSpec((NDEV, blk_rows, n_in_cols), lambda i, j, k: (0, blk_idx, 0))] + _row_specs(c, tm),
        shapes, specs, grid=(S // tm, 1, 1), dims=NT, nk=1, acc_shape=None,
        load_a=load_a, load_b=_b_view(nd, NDEV * blk_rows), epilogue=_rms_bwd_epilogue(nd + 1, nd + 2, nd + 3),
        dep=dep)


def _bwd_cols_to_stream(c, name, dy, col, ci, h, g, dres, dep=None):
    S, D, cw, tm = c.S, c.D, c.cw, c.tmw
    K = NDEV * cw
    shapes, specs = _stream_outs(c, tm)
    return _mm(
        name, [dy, col, h, g, dres],
        [pl.BlockSpec((tm, K), lambda i, j, k: (i, 0)),
         _const((NDEV, None, cw, D), lambda i, j, k: (0, ci, 0, 0))] + _row_specs(c, tm),
        shapes, specs, grid=(S // tm, 1, 1), dims=NN, nk=1, acc_shape=None,
        load_a=_bf(0), load_b=_b_view(1, K), epilogue=_rms_bwd_epilogue(2, 3, 4), dep=dep)


def _bwd_rows_data(c, name, dy, slab, blk_rows, blk_idx, dep=None):
    S, D, tm = c.S, c.D, c.tm
    K = NDEV * blk_rows

    def epilogue(acc, in_refs, out_refs):
        out_refs[0][...] = acc.astype(BF16)

    return _mm(
        name, [dy, slab],
        [pl.BlockSpec((tm, D), lambda i, j, k: (i, 0)),
         pl.BlockSpec((NDEV, blk_rows, D), lambda i, j, k: (0, blk_idx, 0))],
        [jax.ShapeDtypeStruct((S, K), BF16)], [pl.BlockSpec((tm, K), lambda i, j, k: (i, 0))],
        grid=(S // tm, 1, 1), dims=NT, nk=1, acc_shape=None,
        load_a=_bf(0), load_b=_b_view(1, K), epilogue=epilogue, dep=dep)[0]


def _wgrad_rows(c, name, a, b_list, n_a, n_b, dep=None):
    S, tm = c.S, c.tm
    nb_in = len(b_list)
    blk_rows = n_a // NDEV

    def load_b(in_refs):
        b = in_refs[1][...]
        for r in in_refs[2:1 + nb_in]:
            b = b + r[...]
        return b.astype(BF16)

    def epilogue(acc, in_refs, out_refs):
        out_refs[0][...] = acc.reshape(NDEV, blk_rows, n_b).astype(BF16)

    return _mm(
        name, [a, *b_list],
        [pl.BlockSpec((tm, n_a), lambda i, j, k: (k, 0))] + [pl.BlockSpec((tm, n_b), lambda i, j, k: (k, 0))] * nb_in,
        [jax.ShapeDtypeStruct((NDEV, blk_rows, n_b), BF16)],
        [pl.BlockSpec((NDEV, blk_rows, n_b), lambda i, j, k: (0, 0, 0))],
        grid=(1, 1, S // tm), dims=TN, nk=S // tm, acc_shape=(n_a, n_b),
        load_a=_bf(0), load_b=load_b, epilogue=epilogue, dep=dep)[0]


def _wgrad_cols(c, name, a, b, dep=None):
    S, D, cw = c.S, c.D, c.cw

    def body(a_ref, b_ref, out_ref):
        out_ref[...] = _dot(a_ref[...], b_ref[...], TN).astype(BF16)

    return _pcall(
        body, name=name, ins=[a, b], dep=dep, grid=(NDEV,),
        in_specs=[_const((S, D), lambda j: (0, 0)), pl.BlockSpec((S, cw), lambda j: (0, j))],
        out_specs=pl.BlockSpec((None, D, cw), lambda j: (j, 0, 0)),
        out_shape=jax.ShapeDtypeStruct((NDEV, D, cw), BF16), compiler_params=_params(1))


def _ffn_bwd_data(c, name, dh, p, col, ci, rows, h, g, dep=None):
    S, D, fw, tm = c.S, c.D, c.fw, c.tmw
    F = NDEV * fw

    def body(dh_ref, p_ref, w1t_ref, w2_ref, h_ref, g_ref, da_ref, out_ref, dg_ref, dhb_ref):
        i = pl.program_id(0)
        dh = dh_ref[...]
        dhb = dh.astype(BF16)
        dhb_ref[...] = dhb
        for j in range(NDEV):
            cols = slice(j * fw, (j + 1) * fw)
            da_ref[:, cols] = (_dot(dhb, w2_ref[j], NT) * (2.0 * p_ref[:, cols].astype(F32))).astype(BF16)
        dx, dg = _rms_bwd(_dot(da_ref[...], w1t_ref[...].reshape(F, D), NN), h_ref[...], g_ref[...], dh)
        out_ref[...] = dx

        @pl.when(i == 0)
        def _():
            dg_ref[...] = dg

        @pl.when(i > 0)
        def _():
            dg_ref[...] += dg

    row = pl.BlockSpec((tm, D), lambda i: (i, 0))
    wide = pl.BlockSpec((tm, F), lambda i: (i, 0))
    return _pcall(
        body, name=name, ins=[dh, p, col, rows, h, g], dep=dep, grid=(S // tm,),
        in_specs=[row, wide, _const((NDEV, None, fw, D), lambda i: (0, ci, 0, 0)),
                  _const((NDEV, c.fr, D), lambda i: (0, 0, 0)),
                  row, pl.BlockSpec((1, D), lambda i: (0, 0))],
        out_specs=[wide, row, pl.BlockSpec((1, D), lambda i: (0, 0)), row],
        out_shape=[jax.ShapeDtypeStruct((S, F), BF16), jax.ShapeDtypeStruct((S, D), F32),
                   jax.ShapeDtypeStruct((1, D), F32), jax.ShapeDtypeStruct((S, D), BF16)],
        compiler_params=_params(1))


def _ffn_bwd_w(c, name, hn, da, p, dhb, dep=None):
    S, D, fw = c.S, c.D, c.fw

    def body(hn_ref, da_ref, p_ref, dhb_ref, dw1_ref, dw2_ref):
        dw1_ref[...] = _dot(hn_ref[...], da_ref[...], TN).astype(BF16)
        pf = p_ref[...].astype(F32)
        dw2_ref[...] = _dot((pf * pf).astype(BF16), dhb_ref[...], TN).astype(BF16)

    panel = pl.BlockSpec((S, fw), lambda j: (0, j))
    return _pcall(
        body, name=name, ins=[hn, da, p, dhb], dep=dep, grid=(NDEV,),
        in_specs=[_const((S, D), lambda j: (0, 0)), panel, panel, _const((S, D), lambda j: (0, 0))],
        out_specs=[pl.BlockSpec((None, D, fw), lambda j: (j, 0, 0)), pl.BlockSpec((None, c.fr, D), lambda j: (j, 0, 0))],
        out_shape=[jax.ShapeDtypeStruct((NDEV, D, fw), BF16), jax.ShapeDtypeStruct((NDEV, c.fr, D), BF16)],
        compiler_params=_params(1))


def _attn_bwd(c, name, q, kv, do, probs, dep=None):
    S, dq, dkv = c.S, c.DQ, c.DKV
    nb = c.nb
    scale = HEAD_DIM ** -0.5

    def body(q_ref, kvc_ref, kvp_ref, do_ref, probs_ref, dq_ref, dkv_ref, dbias_ref, dsink_ref, dsink_acc,
             ds_sc, p_sc, qm_sc, dom_sc):
        n = pl.program_id(0)

        @pl.when(n == 0)
        def _():
            dkv_ref[...] = jnp.zeros_like(dkv_ref)
            dbias_ref[...] = jnp.zeros_like(dbias_ref)
            dsink_acc[...] = jnp.zeros_like(dsink_acc)

        low = _low_lanes()
        first = _first_key()
        rows_c = pl.ds(pl.multiple_of(n * BLOCK, BLOCK), BLOCK)
        rows_p = pl.ds(pl.multiple_of(jnp.maximum(n - 1, 0) * BLOCK, BLOCK), BLOCK)
        for kvp in range(KV_PAIRS):
            k2, v2 = _pair_kv(kvc_ref, kvp_ref, kvp, dkv)
            for g in range(Q_PER_KV):
                grp = kvp * Q_PER_KV + g
                halves = []
                for par in range(2):
                    rows = slice((2 * g + par) * BLOCK, (2 * g + par + 1) * BLOCK)
                    qh = _head_operand(q_ref, grp, par, low, scale=scale)
                    doh = _head_operand(do_ref, grp, par, low)
                    saved = probs_ref[kvp, rows, :]
                    ps = saved[:, 0:1].astype(F32)
                    p16 = jnp.where(first, jnp.zeros_like(saved), saved)
                    p = p16.astype(F32)
                    dp = _dot(doh, v2, NT)
                    delta = jnp.sum(p * dp, axis=-1, keepdims=True)
                    ds = p * (dp - delta)
                    dbias_ref[kvp, rows, :] += ds
                    dsink_acc[rows, kvp:kvp + 1] += -(ps * delta)
                    ds16 = ds.astype(BF16)
                    halves.append(_dot(ds16, k2, NN) * scale)
                    ds_sc[rows, :] = ds16
                    p_sc[rows, :] = p16
                    qm_sc[rows, :] = qh
                    dom_sc[rows, :] = doh
                dq_ref[:, grp * LANES:(grp + 1) * LANES] = jnp.where(low, halves[0], halves[1]).astype(BF16)
            dk2 = _dot(ds_sc[...], qm_sc[...], TN)
            dv2 = _dot(p_sc[...], dom_sc[...], TN)
            lanes = slice(kvp * LANES, (kvp + 1) * LANES)
            vlanes = slice(dkv + kvp * LANES, dkv + (kvp + 1) * LANES)
            dkv_ref[rows_p, lanes] += dk2[:BLOCK]
            dkv_ref[rows_c, lanes] += dk2[BLOCK:]
            dkv_ref[rows_p, vlanes] += dv2[:BLOCK]
            dkv_ref[rows_c, vlanes] += dv2[BLOCK:]

        @pl.when(n == nb - 1)
        def _():
            dsink_ref[...] = jnp.sum(dsink_acc[...].reshape(2 * Q_PER_KV, BLOCK, KV_PAIRS), axis=1)

    return _pcall(
        body, name=name, ins=[q, kv, kv, do, probs], dep=dep, grid=(nb,),
        in_specs=_attn_specs(c) + [pl.BlockSpec((BLOCK, dq), lambda n: (n, 0)), _probs_spec()],
        out_specs=[pl.BlockSpec((BLOCK, dq), lambda n: (n, 0)), pl.BlockSpec((S, 2 * dkv), lambda n: (0, 0)),
                   pl.BlockSpec((KV_PAIRS, PAIR_ROWS, 2 * BLOCK), lambda n: (0, 0, 0)),
                   pl.BlockSpec((2 * Q_PER_KV, KV_PAIRS), lambda n: (0, 0))],
        out_shape=[jax.ShapeDtypeStruct((S, dq), BF16), jax.ShapeDtypeStruct((S, 2 * dkv), F32),
                   jax.ShapeDtypeStruct((KV_PAIRS, PAIR_ROWS, 2 * BLOCK), F32),
                   jax.ShapeDtypeStruct((2 * Q_PER_KV, KV_PAIRS), F32)],
        scratch_shapes=[pltpu.VMEM((PAIR_ROWS, KV_PAIRS), F32), pltpu.VMEM((PAIR_ROWS, 2 * BLOCK), BF16),
                        pltpu.VMEM((PAIR_ROWS, 2 * BLOCK), BF16), pltpu.VMEM((PAIR_ROWS, LANES), BF16),
                        pltpu.VMEM((PAIR_ROWS, LANES), BF16)],
        compiler_params=_params(1))


def _sgu_bwd(c, name, a, z, dgated, ln_g, wc, wc_t, b_t, dep=None):
    S, AW, gd, tm = c.S, c.AW, c.gd, c.tms

    def body(a_ref, z_ref, dg_ref, lng_ref, wc_ref, wct_ref, bt_ref, dz_ref, dws_ref, dbt_ref, dlng_ref, dvn_ref):
        i = pl.program_id(0)

        @pl.when(i == 0)
        def _():
            dws_ref[...] = jnp.zeros_like(dws_ref)
            dbt_ref[...] = jnp.zeros_like(dbt_ref)
            dlng_ref[...] = jnp.zeros_like(dlng_ref)

        lng = lng_ref[...]
        va = a_ref[:, AW:].astype(F32)
        xc = va - jnp.mean(va, axis=-1, keepdims=True)
        rstd = lax.rsqrt(jnp.mean(xc * xc, axis=-1, keepdims=True) + EPS)
        xh = xc * rstd
        vn = (xh * lng).astype(BF16)
        causal = _sgu_masks()
        for ch in range(tm // CHUNK):
            rows = slice(ch * CHUNK, (ch + 1) * CHUNK)
            for g in range(A_GROUPS):
                cols = slice(g * gd, (g + 1) * gd)
                blk = vn[rows, cols]
                mixed = _dot(wc_ref[g], blk, NN) + bt_ref[:, g:g + 1]
                dgb = dg_ref[rows, cols].astype(F32)
                dm = dgb * a_ref[rows, cols].astype(F32)
                dbt_ref[:, g:g + 1] += jnp.sum(dm, axis=1, keepdims=True)
                dm16 = dm.astype(BF16)
                dws_ref[g] += jnp.where(causal, _dot(dm16, blk, NT), 0.0)
                dvn_ref[rows, cols] = _dot(wct_ref[g], dm16, NN)
                dz_ref[rows, cols] = (dgb * mixed * _gelu_grad(z_ref[rows, cols].astype(F32))).astype(BF16)
        dvn = dvn_ref[...]
        dlng_ref[...] += jnp.sum(dvn * xh, axis=0, keepdims=True)
        dxh = dvn * lng
        dva = rstd * (dxh - jnp.mean(dxh, axis=-1, keepdims=True) - xh * jnp.mean(dxh * xh, axis=-1, keepdims=True))
        dz_ref[:, AW:] = (dva * _gelu_grad(z_ref[:, AW:].astype(F32))).astype(BF16)

    wide = pl.BlockSpec((tm, 2 * AW), lambda i: (i, 0))
    wsp = pl.BlockSpec((A_GROUPS, CHUNK, CHUNK), lambda i: (0, 0, 0))
    btsp = pl.BlockSpec((CHUNK, A_GROUPS), lambda i: (0, 0))
    return _pcall(
        body, name=name, ins=[a, z, dgated, ln_g, wc, wc_t, b_t], dep=dep, grid=(S // tm,),
        in_specs=[wide, wide, pl.BlockSpec((tm, AW), lambda i: (i, 0)), pl.BlockSpec((1, AW), lambda i: (0, 0)),
                  wsp, wsp, btsp],
        out_specs=[wide, wsp, btsp, pl.BlockSpec((1, AW), lambda i: (0, 0))],
        out_shape=[jax.ShapeDtypeStruct((S, 2 * AW), BF16), jax.ShapeDtypeStruct((A_GROUPS, CHUNK, CHUNK), F32),
                   jax.ShapeDtypeStruct((CHUNK, A_GROUPS), F32), jax.ShapeDtypeStruct((1, AW), F32)],
        scratch_shapes=[pltpu.VMEM((tm, AW), F32)], compiler_params=_params(1))


def _adam_update(g, w_ref, m_ref, v_ref, out_refs):
    g_ref, d_ref, nm_ref, nv_ref = out_refs
    bc1 = 1.0 - ADAM_B1 ** ADAM_STEP
    bc2 = 1.0 - ADAM_B2 ** ADAM_STEP
    nm = ADAM_B1 * m_ref[...] + (1.0 - ADAM_B1) * g
    nv = ADAM_B2 * v_ref[...] + (1.0 - ADAM_B2) * (g * g)
    g_ref[...] = g
    nm_ref[...] = nm
    nv_ref[...] = nv
    d_ref[...] = -ADAM_LR * ((nm * (1.0 / bc1)) / (jnp.sqrt(nv * (1.0 / bc2)) + ADAM_EPS) + ADAM_WD * w_ref[...])


def _adamw_packed(name, sums, offs, ws, ms, vs, dep=None):
    n = len(ws)

    def body(*refs):
        s_ref, w_refs, m_refs, v_refs = refs[0], refs[1:1 + n], refs[1 + n:1 + 2 * n], refs[1 + 2 * n:1 + 3 * n]
        outs = refs[1 + 3 * n:]
        for k in range(n):
            _adam_update(s_ref[offs[k]:offs[k + 1], :], w_refs[k], m_refs[k], v_refs[k], outs[4 * k:4 * k + 4])

    ins = [sums, *ws, *ms, *vs]
    shapes = [jax.ShapeDtypeStruct(w.shape, F32) for w in ws for _ in range(4)]
    return _pcall(body, name=name, ins=ins, in_specs=[_whole(t) for t in ins], dep=dep,
                  out_shape=shapes, out_specs=[_whole(t) for t in shapes], compiler_params=_params(0))


def _adamw(name, parts, part_block, part_index, w, m, v, tr, row_off=0, n_rows=None, prev=None, dep=None):
    R, C = w.shape
    n_rows = R if n_rows is None else n_rows
    assert n_rows % tr == 0 and row_off % tr == 0

    def body(p_ref, w_ref, m_ref, v_ref, *rest):
        g = p_ref[0].astype(F32)
        for s in range(1, part_block[0]):
            g = g + p_ref[s].astype(F32)
        _adam_update(g, w_ref, m_ref, v_ref, rest[-4:])

    ob = row_off // tr
    row = pl.BlockSpec((tr, C), lambda i: (ob + i, 0))
    out = jax.ShapeDtypeStruct((R, C), F32)
    chained = prev is not None
    return _pcall(
        body, name=name, ins=[parts, w, m, v] + (list(prev) if chained else []), dep=dep, grid=(n_rows // tr,),
        in_specs=[pl.BlockSpec(part_block, part_index), row, row, row] + ([ANY] * 4 if chained else []),
        out_specs=[row, row, row, row], out_shape=[out, out, out, out],
        input_output_aliases={4 + t: t for t in range(4)} if chained else {}, compiler_params=_params(1))


def _sum_parts(name, parts, dep=None):
    def body(p_ref, out_ref):
        g = p_ref[0]
        for s in range(1, parts.shape[0]):
            g = g + p_ref[s]
        out_ref[...] = g

    return _pcall(body, name=name, ins=[parts], in_specs=[_whole(parts)], dep=dep,
                  out_shape=jax.ShapeDtypeStruct(parts.shape[1:], F32), compiler_params=_params(0))


def _place():
    return lax.axis_index("x"), lax.axis_index("y"), lax.axis_index("c")


def _slot(px, py, pc):
    return 4 * px + 2 * py + pc


def _peer(k, x, y, c):
    return x ^ ((k >> 2) & 1), y ^ ((k >> 1) & 1), c ^ (k & 1)


SEND_PEERS = {"exchange": tuple(range(1, NDEV)), "gather": (1, 2, 4, 6), "forward": (2, 4, 6),
              "broadcast": tuple(range(1, NDEV))}


def _n_sems(mode, n_lands):
    return n_lands * (len(SEND_PEERS[mode]) + (mode != "forward"))


def _send_copies(mode, src_refs, land_refs, send_sems, recv_sems):
    x, y, c = _place()
    me = _slot(x, y, c)
    peers = SEND_PEERS[mode]
    remote, local = [], []
    for i, k in enumerate(peers):
        peer = _peer(k, x, y, c)
        for a, land in enumerate(land_refs):
            if mode == "exchange":
                src, dst, to = src_refs[a].at[_slot(*peer)], land.at[me], peer
            elif mode in ("gather", "broadcast"):
                src, dst, to = src_refs[a], land.at[me], peer
            else:
                src = dst = land.at[_slot(*peer)]
                to = (x, y, 1 - c)
            s = a * len(peers) + i
            remote.append(pltpu.make_async_remote_copy(src_ref=src, dst_ref=dst, send_sem=send_sems.at[s],
                                                       recv_sem=recv_sems.at[s], device_id=to, device_id_type=MESH))
    if mode != "forward":
        for a, land in enumerate(land_refs):
            src = src_refs[a].at[me] if mode == "exchange" else src_refs[a]
            local.append(pltpu.make_async_copy(src, land.at[me], send_sems.at[len(land_refs) * len(peers) + a]))
    return remote, local


def _send_start_groups(name, groups, mode, dep=None):
    sizes = [(len(s), len(l)) for s, l in groups]
    flat = [t for s, l in groups for t in (*s, *l)]
    n_in, ng = len(flat), len(groups)

    def body(*refs):
        sems, token, at = refs[n_in:n_in + 2 * ng], refs[-1], 0
        if mode == "forward":
            x, y, c = _place()
            barrier = pltpu.get_barrier_semaphore()
            pl.semaphore_signal(barrier, inc=1, device_id=(x, y, 1 - c), device_id_type=MESH)
            pl.semaphore_wait(barrier, 1)
        for gi, (ns, nl) in enumerate(sizes):
            remote, local = _send_copies(mode, refs[at:at + ns], refs[at + ns:at + ns + nl], sems[2 * gi],
                                         sems[2 * gi + 1])
            for cp in remote + local:
                cp.start()
            at += ns + nl
        token[...] = jnp.zeros_like(token)

    sem_shapes = [pltpu.SemaphoreType.DMA((_n_sems(mode, nl),)) for _, nl in sizes for _ in range(2)]
    if any(dep is t for t in flat):
        dep = None
    out = _pcall(
        body, name=name, ins=[pltpu.with_memory_space_constraint(t, pltpu.HBM) for t in flat],
        in_specs=[HBM] * n_in, dep=dep,
        out_shape=(*sem_shapes, *[pltpu.HBM(t.shape, t.dtype) for t in flat], jax.ShapeDtypeStruct((8, LANES), F32)),
        out_specs=(*[SEM] * (2 * ng), *[HBM] * n_in, pl.BlockSpec(memory_space=pltpu.VMEM)),
        input_output_aliases={i: 2 * ng + i for i in range(n_in)},
        compiler_params=pltpu.CompilerParams(has_side_effects=pltpu.SideEffectType.DATAFLOW_SIDE_EFFECTING,
                                             collective_id=SIBLING_COLLECTIVE_ID if mode == "forward" else None))
    started, at = [], 2 * ng
    for gi, (ns, nl) in enumerate(sizes):
        started.append((out[-1], out[2 * gi], out[2 * gi + 1], list(out[at:at + ns]), list(out[at + ns:at + ns + nl])))
        at += ns + nl
    return started


def _send_start(name, srcs, lands, mode, dep=None):
    return _send_start_groups(name, [(srcs, lands)], mode, dep=dep)[0]


def _send_wait(name, started, mode, dep=None):
    _, send_sems, recv_sems, srcs, lands = started
    n_src, n = len(srcs), len(lands)

    def body(*refs):
        src_refs, land_refs = refs[:n_src], refs[n_src:n_src + n]
        ssem, rsem = refs[n_src + n], refs[n_src + n + 1]
        remote, local = _send_copies(mode, src_refs, land_refs, ssem, rsem)
        for cp in remote:
            cp.wait_send()
            cp.wait_recv()
        for cp in local:
            cp.wait()

    thru = [pltpu.HBM(t.shape, t.dtype) for t in [*srcs, *lands]]
    out = _pcall(
        body, name=name, ins=[*srcs, *lands, send_sems, recv_sems], in_specs=[HBM] * (n_src + n) + [SEM, SEM], dep=dep,
        out_shape=tuple(thru), out_specs=tuple([HBM] * (n_src + n)),
        input_output_aliases={i: i for i in range(n_src + n)},
        compiler_params=pltpu.CompilerParams(has_side_effects=pltpu.SideEffectType.DATAFLOW_SIDE_EFFECTING))
    return list(out[n_src:])


def _landing(block):
    return lax.empty((NDEV, *block.shape), block.dtype)


def _rows128(t):
    flat = t.reshape(-1)
    n = flat.shape[0]
    rows = -(-n // (8 * LANES)) * 8
    return jnp.pad(flat, (0, rows * LANES - n)).reshape(rows, LANES)


def kernel(x, mix_norm_g, ffn_norm_g, a_w_in, a_ln_g, a_w_spatial, a_b_spatial, a_w_out, kv_norm_g, w_k, w_v, b_w_q, b_sinks, b_w_o, rel_bias, ffn_w1, ffn_w2, final_norm_g, loss_target, m_mix_norm_g, m_ffn_norm_g, m_a_w_in, m_a_ln_g, m_a_w_spatial, m_a_b_spatial, m_a_w_out, m_kv_norm_g, m_w_k, m_w_v, m_b_w_q, m_b_sinks, m_b_w_o, m_rel_bias, m_ffn_w1, m_ffn_w2, m_final_norm_g, v_mix_norm_g, v_ffn_norm_g, v_a_w_in, v_a_ln_g, v_a_w_spatial, v_a_b_spatial, v_a_w_out, v_kv_norm_g, v_w_k, v_w_v, v_b_w_q, v_b_sinks, v_b_w_o, v_rel_bias, v_ffn_w1, v_ffn_w2, v_final_norm_g):
    c = _config(x, a_w_in, a_w_out, w_k, b_w_q, b_w_o, ffn_w1, ffn_w2)
    S, D, LA, LB, LF = c.S, c.D, c.LA, c.LB, c.LF
    weights = dict(mix_norm_g=mix_norm_g, ffn_norm_g=ffn_norm_g, a_w_in=a_w_in, a_ln_g=a_ln_g, a_w_spatial=a_w_spatial,
                   a_b_spatial=a_b_spatial, a_w_out=a_w_out, kv_norm_g=kv_norm_g, w_k=w_k, w_v=w_v, b_w_q=b_w_q,
                   b_sinks=b_sinks, b_w_o=b_w_o, rel_bias=rel_bias, ffn_w1=ffn_w1, ffn_w2=ffn_w2,
                   final_norm_g=final_norm_g)
    m_in = dict(mix_norm_g=m_mix_norm_g, ffn_norm_g=m_ffn_norm_g, a_w_in=m_a_w_in, a_ln_g=m_a_ln_g,
                a_w_spatial=m_a_w_spatial, a_b_spatial=m_a_b_spatial, a_w_out=m_a_w_out, kv_norm_g=m_kv_norm_g,
                w_k=m_w_k, w_v=m_w_v, b_w_q=m_b_w_q, b_sinks=m_b_sinks, b_w_o=m_b_w_o, rel_bias=m_rel_bias,
                ffn_w1=m_ffn_w1, ffn_w2=m_ffn_w2, final_norm_g=m_final_norm_g)
    v_in = dict(mix_norm_g=v_mix_norm_g, ffn_norm_g=v_ffn_norm_g, a_w_in=v_a_w_in, a_ln_g=v_a_ln_g,
                a_w_spatial=v_a_w_spatial, a_b_spatial=v_a_b_spatial, a_w_out=v_a_w_out, kv_norm_g=v_kv_norm_g,
                w_k=v_w_k, w_v=v_w_v, b_w_q=v_b_w_q, b_sinks=v_b_sinks, b_w_o=v_b_w_o, rel_bias=v_rel_bias,
                ffn_w1=v_ffn_w1, ffn_w2=v_ffn_w2, final_norm_g=v_final_norm_g)
    names = list(weights)
    seq = _Seq()
    me = _slot(*_place())
    bf = lambda t: t.astype(BF16)

    tr = lambda t: bf(jnp.swapaxes(t, -1, -2))

    def start(tag, some):
        got = seq(_send_start_groups, f"weights_start_{tag}", [(grp, [_landing(t) for t in grp]) for grp in some],
                  "gather")
        seq.last = got[0][0]
        return got

    started = start("first", [[tr(a_w_in[0])[None], a_ln_g]])
    _, a_w_in_, a_w_out_, ffn_w1_, ffn_w2_, w_k_, w_v_, b_w_q_, b_w_o_ = lax.optimization_barrier(
        (started[0][0], a_w_in, a_w_out, ffn_w1, ffn_w2, w_k, w_v, b_w_q, b_w_o))
    groups = []
    for l in range(LA):
        groups += [[tr(a_w_in_[l])[None]], [bf(a_w_out_[l])], [tr(ffn_w1_[l])[None], bf(ffn_w2_[l])]]
    gb = 3 * LA
    for l in range(LB):
        extra = [bf(jnp.concatenate([w_k_, w_v_], axis=1))] if l == 0 else []
        groups += [extra + [bf(b_w_q_[l]), bf(b_w_o_[l])], [tr(ffn_w1_[LA + l])[None], bf(ffn_w2_[LA + l])]]
    started += start("rest", groups[1:])
    forwarding = {}

    def forward(i):
        lands = seq(_send_wait, f"weights_wait{i}", started[i], "gather")
        forwarding[i] = seq(_send_start, f"weights_forward{i}", [], lands, "forward")

    def arrive(i):
        if i not in forwarding:
            forward(i)
        return seq(_send_wait, f"weights_arrive{i}", forwarding[i], "forward")

    causal = jnp.tril(jnp.ones((CHUNK, CHUNK), bool))
    wsp = jnp.where(causal[None, None], a_w_spatial, 0.0)
    wsp16 = wsp.astype(BF16)
    wsp16_t = jnp.swapaxes(wsp, -1, -2).astype(BF16)
    bsp_t = jnp.swapaxes(a_b_spatial, -1, -2)
    mix_g = mix_norm_g.reshape(-1, 1, D)
    ffn_g = ffn_norm_g.reshape(-1, 1, D)
    kv_g = kv_norm_g.reshape(1, D)
    fin_g = final_norm_g.reshape(1, D)
    onehot = _bucket_onehot()
    bias = _slot_bias(seq(_band_bias, rel_bias.T, onehot).reshape(N_HEADS, BLOCK, 2 * BLOCK))

    h = x.reshape(S, D)
    sav_a, sav_b, wts_a, wts_b = [], [], [], []
    for l in range(LA):
        got = arrive(3 * l)
        w_in = got[0]
        if l == 0:
            ln_g_full = jnp.transpose(got[1], (1, 0, 2)).reshape(LA, 1, c.AW)
        z, a, hn = seq(_a_in_fwd, c, f"a_in_fwd{l}", h, mix_g[l], w_in, 0)
        forward(3 * l + 1)
        gated = seq(_sgu_fwd, c, f"sgu_fwd{l}", a, ln_g_full[l], wsp16[l], bsp_t[l])
        (wout,) = arrive(3 * l + 1)
        if l > 0:
            forward(3 * l + 2)
        h1 = seq(_mm_res, c, f"a_out_fwd{l}", gated, wout, c.ar, 0, h)
        w1, rows = arrive(3 * l + 2)
        if l == LA - 1:
            forward(gb)
        p, h2, hnf = seq(_ffn_fwd, c, f"ffn_fwd{l}", h1, ffn_g[l], w1, 0, rows)
        sav_a.append((h, z, a, hn, gated, h1, p, hnf))
        wts_a.append((w_in, 0, w1, 0, rows, wout, 0))
        h = h2
    h_kv = h
    for l in range(LB):
        got = arrive(gb + 2 * l)
        if l == 0:
            wkv, got = got[0], got[1:]
            kv, hkv = seq(_rms_mm_rows, c, "kv_fwd", h, kv_g, wkv, c.kr, 0, 2 * c.DKV)
        wq, wo = _slot_cols(got[0]), _slot_rows(got[1])
        q, hn = seq(_rms_mm_rows, c, f"q_fwd{l}", h, mix_g[LA + l], wq, c.qr, 0, c.DQ)
        forward(gb + 2 * l + 1)
        o, probs = seq(_attn_fwd, c, f"attn_fwd{l}", q, kv, bias, b_sinks[l])
        h1 = seq(_mm_res, c, f"o_fwd{l}", o, wo, c.orr, 0, h)
        w1, rows = arrive(gb + 2 * l + 1)
        if l + 1 < LB:
            forward(gb + 2 * l + 2)
        p, h2, hnf = seq(_ffn_fwd, c, f"ffn_fwd{LA + l}", h1, ffn_g[LA + l], w1, 0, rows)
        sav_b.append((h, q, hn, o, probs, h1, p, hnf))
        wts_b.append((wq, wo, w1, rows))
        h = h2
    dh, d_fin_g, loss_row = seq(_final_loss, c, h, fin_g, loss_target.reshape(S, D))

    results = {}
    in_flight = []

    def update(k, parts, layer, col_blk=0):
        w = weights[k]
        rows_l, ncols = (w.shape[-2], w.shape[-1]) if w.ndim == 3 else w.shape
        flat = lambda t: t.reshape(-1, ncols)
        tr = min(256, rows_l)
        results[k] = seq(_adamw, f"adamw_{k}{layer}", parts, (NDEV, tr, ncols), lambda i: (0, i, col_blk),
                         flat(w), flat(m_in[k]), flat(v_in[k]), tr, row_off=layer * rows_l, n_rows=rows_l,
                         prev=results.get(k))

    def land(tag, entry):
        lands = seq(_send_wait, f"grads_wait_{tag}", entry[1], "exchange")
        for keys, parts in zip(entry[0], lands):
            for k, layer, col_blk in keys:
                update(k, parts, layer, col_blk)

    def send(tag, items):
        slabs = [t for _, t in items]
        own = [lax.empty(t.shape, t.dtype) for t in slabs]
        st = seq(_send_start, f"grads_start_{tag}", slabs, own, "exchange")
        in_flight.append((tag, ([k for k, _ in items], st)))
        while len(in_flight) > EXCHANGE_LAG:
            land(*in_flight.pop(0))

    d_mix_g, d_ffn_g = [None] * LF, [None] * LF
    dkv_list, dbias_list, dsink_list = [], [], [None] * LB

    def ffn_bwd(lf, dh, h1, p, hnf, w1, w1_i, rows):
        da, dh1, d_ffn_g[lf], dhb = seq(_ffn_bwd_data, c, f"ffn_bwd_data{lf}", dh, p, w1, w1_i, rows, h1, ffn_g[lf])
        dw1, dw2 = seq(_ffn_bwd_w, c, f"ffn_bwd_w{lf}", hnf, da, p, dhb)
        send(f"ffn{lf}", [([("ffn_w1", lf, 0)], dw1), ([("ffn_w2", lf, 0)], dw2)])
        return dh1

    for l in reversed(range(LB)):
        h0, q, hn, o, probs, h1, p, hnf = sav_b[l]
        wq, wo, w1, rows = wts_b[l]
        dh1 = ffn_bwd(LA + l, dh, h1, p, hnf, w1, 0, rows)
        do = seq(_bwd_rows_data, c, f"o_bwd_data{l}", dh1, wo, c.orr, 0)
        dwo = _unslot_rows(seq(_wgrad_rows, c, f"o_bwd_w{l}", o, [dh1], c.DQ, D))
        dq, dkv, dbias, dsink = seq(_attn_bwd, c, f"attn_bwd{l}", q, kv, do, probs)
        dsink_list[l] = dsink.reshape(Q_PER_KV, 2, KV_PAIRS).transpose(2, 1, 0).reshape(1, N_HEADS)
        dkv_list.append(dkv)
        dbias_list.append(_unslot_bias(dbias))
        dwq = _unslot_cols(seq(_wgrad_rows, c, f"q_bwd_w{l}", hn, [dq], D, c.DQ))
        send(f"attn{l}", [([("b_w_o", l, 0)], dwo), ([("b_w_q", l, 0)], dwq)])
        dh, d_mix_g[LA + l] = seq(_bwd_rows_to_stream, c, f"q_bwd_data{l}", [dq], wq, c.qr, 0, c.DQ, h0,
                                  mix_g[LA + l], dh1)
    dwkv = seq(_wgrad_rows, c, "kv_bwd_w", hkv, dkv_list, D, 2 * c.DKV)
    send("kv", [([("w_k", 0, 0), ("w_v", 0, 1)], dwkv)])
    dh, d_kv_g = seq(_bwd_rows_to_stream, c, "kv_bwd_data", dkv_list, wkv, c.kr, 0, 2 * c.DKV, h_kv, kv_g, dh)
    d_rel_t = seq(_band_bias_grad, dbias_list, onehot)
    d_wsp, d_bsp, d_lng = [None] * LA, [None] * LA, [None] * LA
    for l in reversed(range(LA)):
        h0, z, a, hn, gated, h1, p, hnf = sav_a[l]
        w_in, in_i, w1, w1_i, rows, wout, wout_i = wts_a[l]
        dh1 = ffn_bwd(l, dh, h1, p, hnf, w1, w1_i, rows)
        dgated = seq(_bwd_rows_data, c, f"a_out_bwd_data{l}", dh1, wout, c.ar, wout_i)
        dwout = seq(_wgrad_rows, c, f"a_out_bwd_w{l}", gated, [dh1], c.AW, D)
        send(f"a_out{l}", [([("a_w_out", l, 0)], dwout)])
        dz, d_wsp[l], dbt, d_lng[l] = seq(_sgu_bwd, c, f"sgu_bwd{l}", a, z, dgated, ln_g_full[l], wsp16[l],
                                          wsp16_t[l], bsp_t[l])
        d_bsp[l] = dbt.T
        dwin = seq(_wgrad_cols, c, f"a_in_bwd_w{l}", hn, dz)
        send(f"a_in{l}", [([("a_w_in", l, 0)], dwin)])
        dh, d_mix_g[l] = seq(_bwd_cols_to_stream, c, f"a_in_bwd_data{l}", dz, w_in, in_i, h0, mix_g[l], dh1)
    grad_x = dh.reshape(1, S, D)

    small = {
        "mix_norm_g": jnp.concatenate(d_mix_g, axis=0), "ffn_norm_g": jnp.concatenate(d_ffn_g, axis=0),
        "a_w_spatial": jnp.stack(d_wsp), "a_b_spatial": jnp.stack(d_bsp), "kv_norm_g": d_kv_g,
        "b_sinks": jnp.concatenate(dsink_list, axis=0), "rel_bias": d_rel_t.T, "final_norm_g": d_fin_g,
    }
    small_names = list(small)
    packs = [_rows128(small[k]) for k in small_names] + [_rows128(jnp.concatenate(d_lng, axis=0)), _rows128(loss_row)]
    offs = [int(o) for o in np.cumsum([0] + [p.shape[0] for p in packs])]
    Rs = offs[-1] + (-offs[-1]) % (8 * NDEV)
    packed = jnp.concatenate(packs + [jnp.zeros((Rs - offs[-1], LANES), F32)], axis=0)
    slab = packed.reshape(NDEV, Rs // NDEV, LANES)
    st = seq(_send_start, "small_grads_start", [slab], [lax.empty(slab.shape, slab.dtype)], "exchange")
    while len(in_flight) > 1:
        land(*in_flight.pop(0))
    (parts,) = seq(_send_wait, "small_grads_wait", st, "exchange")
    mine = seq(_sum_parts, "small_grads_sum", parts)
    st = seq(_send_start, "small_sums_start", [mine], [_landing(mine)], "broadcast")
    while in_flight:
        land(*in_flight.pop(0))
    (sums,) = seq(_send_wait, "small_sums_wait", st, "broadcast")
    sums = sums.reshape(Rs, LANES)
    loss = sums[offs[-2], 0]

    grads, deltas, new_m, new_v = {}, {}, {}, {}

    def put(k, outs, shape):
        grads[k], deltas[k], new_m[k], new_v[k] = (t.reshape(shape) for t in outs)

    outs = seq(_adamw_packed, "adamw_small", sums, offs, *[[_rows128(d[k]) for k in small_names]
                                                          for d in (weights, m_in, v_in)])
    for n_, k in enumerate(small_names):
        shape = weights[k].shape
        size = int(np.prod(shape))
        put(k, [t.reshape(-1)[:size] for t in outs[4 * n_:4 * n_ + 4]], shape)
    lng_sum = sums[offs[-3]:offs[-2]].reshape(-1)[:LA * c.AW].reshape(LA, c.AW)
    lng_mine = lax.dynamic_slice_in_dim(lng_sum, me * c.ar, c.ar, axis=1)
    lng_parts = jnp.concatenate([lng_mine[None], jnp.zeros((NDEV - 1, LA, c.ar), F32)], axis=0)
    put("a_ln_g", seq(_adamw, "adamw_ln_g", lng_parts, (NDEV, LA, c.ar), lambda i: (0, 0, 0),
                      a_ln_g, m_in["a_ln_g"], v_in["a_ln_g"], LA), a_ln_g.shape)
    for k in ("a_w_in", "ffn_w1", "ffn_w2", "a_w_out", "b_w_o", "b_w_q", "w_k", "w_v"):
        put(k, results[k], weights[k].shape)

    return (loss, grad_x, *[grads[k] for k in names], *[deltas[k] for k in names],
            *[new_m[k] for k in names], *[new_v[k] for k in names])
```

```python
import numpy as np
import math
import jax
import jax.numpy as jnp
from jax import lax
from jax.experimental import pallas as pl
from jax.experimental.pallas import tpu as pltpu

F32 = jnp.float32
BF16 = jnp.bfloat16

NDEV = 8
EPS = 1e-6
CHUNK = 128
A_GROUPS = 8
N_HEADS = 16
N_KV_HEADS = 4
Q_PER_KV = N_HEADS // N_KV_HEADS
HEAD_DIM = 64
BLOCK = 128
N_BUCKETS = 32
MAX_DISTANCE = 128
ADAM_LR, ADAM_B1, ADAM_B2, ADAM_EPS, ADAM_WD, ADAM_STEP = 0.001, 0.9, 0.999, 1e-08, 0.01, 10
LANES = 128
VMEM_LIMIT = 56 * 1024 * 1024
INV_SQRT2 = 0.7071067811865476
INV_SQRT_2PI = 0.3989422804014327
EXCHANGE_LAG = 4
SIBLING_COLLECTIVE_ID = 0

HBM = pl.BlockSpec(memory_space=pltpu.HBM)
SMEM = pl.BlockSpec(memory_space=pltpu.SMEM)
ANY = pl.BlockSpec(memory_space=pl.ANY)
SEM = pl.BlockSpec(memory_space=pltpu.SEMAPHORE)
MESH = pl.DeviceIdType.MESH


def _params(n_grid):
    return pltpu.CompilerParams(dimension_semantics=("arbitrary",) * n_grid, vmem_limit_bytes=VMEM_LIMIT)


def _const(block, index_map):
    return pl.BlockSpec(block, index_map, pipeline_mode=pl.Buffered(1))


def _pcall(body, *, ins, in_specs, dep=None, **kw):
    n_in = len(ins)
    if dep is None or any(dep is t for t in ins):
        return pl.pallas_call(body, in_specs=list(in_specs), **kw)(*ins)

    def with_dep(*refs):
        body(*refs[:n_in], *refs[n_in + 1:])

    return pl.pallas_call(with_dep, in_specs=[*in_specs, ANY], **kw)(*ins, dep)


class _Seq:
    def __init__(self):
        self.last = None

    def __call__(self, fn, *args, **kw):
        out = fn(*args, dep=self.last, **kw)
        self.last = out[0] if isinstance(out, (tuple, list)) else out
        return out


def _rstd(h):
    return lax.rsqrt(jnp.mean(h * h, axis=-1, keepdims=True) + EPS)


def _rms_bwd(dhn, h, g, dres):
    r = _rstd(h)
    xh = h * r
    dg = jnp.sum(dhn * xh, axis=0, keepdims=True)
    dxh = dhn * g
    dx = r * (dxh - xh * jnp.mean(dxh * xh, axis=-1, keepdims=True))
    return dres + dx, dg


def _gelu(z):
    return 0.5 * z * (1.0 + lax.erf(z * INV_SQRT2))


def _gelu_grad(z):
    return 0.5 * (1.0 + lax.erf(z * INV_SQRT2)) + z * (jnp.exp(-0.5 * z * z) * INV_SQRT_2PI)


def _dot(a, b, dims):
    return lax.dot_general(a, b, (dims, ((), ())), preferred_element_type=F32)


NN = ((1,), (0,))
NT = ((1,), (1,))
TN = ((0,), (0,))


def _mm(name, ins, in_specs, out_shapes, out_specs, *, grid, dims, nk, acc_shape, load_a, load_b, epilogue,
        dep=None):
    n_in, n_out = len(ins), len(out_shapes)
    kax = len(grid) - 1

    def body(*refs):
        in_refs = refs[:n_in]
        out_refs = refs[n_in:n_in + n_out]
        a = load_a(in_refs, out_refs)
        b = load_b(in_refs)
        prod = _dot(a, b, dims)
        if nk == 1:
            epilogue(prod, in_refs, out_refs)
        else:
            acc = refs[n_in + n_out]
            k = pl.program_id(kax)

            @pl.when(k == 0)
            def _():
                acc[...] = prod

            @pl.when(k > 0)
            def _():
                acc[...] += prod

            @pl.when(k == nk - 1)
            def _():
                epilogue(acc[...], in_refs, out_refs)

    return _pcall(
        body, name=name, ins=ins, in_specs=in_specs, dep=dep, grid=grid, out_specs=out_specs, out_shape=out_shapes,
        scratch_shapes=[pltpu.VMEM(acc_shape, F32)] if nk > 1 else [], compiler_params=_params(len(grid)))


def _bf(ref_idx):
    return lambda in_refs, *_: in_refs[ref_idx][...].astype(BF16)


def _b_view(ref_idx, rows):
    def load(in_refs):
        b = in_refs[ref_idx][...]
        return b.reshape(rows, b.shape[-1])
    return load


class Cfg:
    pass


def _config(x, a_w_in, a_w_out, w_k, b_w_q, b_w_o, ffn_w1, ffn_w2):
    c = Cfg()
    c.S, c.D = x.shape[1], x.shape[2]
    c.LA, _, c.cw = a_w_in.shape
    c.AW2 = NDEV * c.cw
    c.AW = c.AW2 // 2
    c.gd = c.AW // A_GROUPS
    c.ar = a_w_out.shape[1]
    c.LF, _, c.fw = ffn_w1.shape
    c.fr = ffn_w2.shape[1]
    c.LB, c.qr, c.DQ = b_w_q.shape
    c.orr = b_w_o.shape[1]
    c.kr, c.DKV = w_k.shape
    c.tm = min(1024, c.S)
    c.tmw = min(512, c.S)
    c.tms = min(256, c.S)
    c.nb = c.S // BLOCK
    assert c.cw == c.fw == c.fr and c.AW == NDEV * c.ar and c.D == NDEV * c.qr == NDEV * c.kr
    assert c.DQ == NDEV * c.orr == N_HEADS * HEAD_DIM and c.DKV == N_KV_HEADS * HEAD_DIM
    assert c.S % c.tm == 0 and c.S % c.tmw == 0 and c.S % c.tms == 0 and c.tms % CHUNK == 0 and c.gd % LANES == 0
    assert c.LA >= 1 and c.LB >= 1 and c.LF == c.LA + c.LB
    return c


def _cached_rms(h_idx, g_idx, hn_out_idx, jax_axis=1):
    def load(in_refs, out_refs):
        hn_ref = out_refs[hn_out_idx]

        @pl.when(pl.program_id(jax_axis) == 0)
        def _():
            h = in_refs[h_idx][...]
            hn_ref[...] = (h * _rstd(h) * in_refs[g_idx][...]).astype(BF16)

        return hn_ref[...]
    return load


def _a_in_fwd(c, name, h, g, col, ci, dep=None):
    S, D, cw, tm = c.S, c.D, c.cw, c.tmw

    def body(h_ref, g_ref, w_ref, z_ref, a_ref, hn_ref):
        h = h_ref[...]
        hn = (h * _rstd(h) * g_ref[...]).astype(BF16)
        hn_ref[...] = hn
        for j in range(NDEV):
            cols = slice(j * cw, (j + 1) * cw)
            z = _dot(hn, w_ref[j], NT)
            z_ref[:, cols] = z.astype(BF16)
            a_ref[:, cols] = _gelu(z).astype(BF16)

    row = pl.BlockSpec((tm, D), lambda i: (i, 0))
    wide = pl.BlockSpec((tm, c.AW2), lambda i: (i, 0))
    return _pcall(
        body, name=name, ins=[h, g, col], dep=dep, grid=(S // tm,),
        in_specs=[row, pl.BlockSpec((1, D), lambda i: (0, 0)), _const((NDEV, None, cw, D), lambda i: (0, ci, 0, 0))],
        out_specs=[wide, wide, row],
        out_shape=[jax.ShapeDtypeStruct((S, c.AW2), BF16), jax.ShapeDtypeStruct((S, c.AW2), BF16),
                   jax.ShapeDtypeStruct((S, D), BF16)],
        compiler_params=_params(1))


def _rms_mm_rows(c, name, h, g, slab, blk_rows, blk_idx, n_out, dep=None):
    S, D, tm = c.S, c.D, c.tm

    def epilogue(acc, in_refs, out_refs):
        out_refs[0][...] = acc.astype(BF16)

    return _mm(
        name, [h, slab, g],
        [pl.BlockSpec((tm, D), lambda i, j, k: (i, 0)),
         pl.BlockSpec((NDEV, blk_rows, n_out), lambda i, j, k: (0, blk_idx, 0)),
         pl.BlockSpec((1, D), lambda i, j, k: (0, 0))],
        [jax.ShapeDtypeStruct((S, n_out), BF16), jax.ShapeDtypeStruct((S, D), BF16)],
        [pl.BlockSpec((tm, n_out), lambda i, j, k: (i, 0)), pl.BlockSpec((tm, D), lambda i, j, k: (i, 0))],
        grid=(S // tm, 1, 1), dims=NN, nk=1, acc_shape=None,
        load_a=_cached_rms(0, 2, 1), load_b=_b_view(1, NDEV * blk_rows), epilogue=epilogue, dep=dep)


def _mm_res(c, name, a, slab, blk_rows, blk_idx, res, dep=None):
    S, D, tm = c.S, c.D, c.tm
    K = NDEV * blk_rows

    def epilogue(acc, in_refs, out_refs):
        out_refs[0][...] = in_refs[2][...] + acc

    return _mm(
        name, [a, slab, res],
        [pl.BlockSpec((tm, K), lambda i, j, k: (i, 0)),
         pl.BlockSpec((NDEV, blk_rows, D), lambda i, j, k: (0, blk_idx, 0)),
         pl.BlockSpec((tm, D), lambda i, j, k: (i, 0))],
        [jax.ShapeDtypeStruct((S, D), F32)], [pl.BlockSpec((tm, D), lambda i, j, k: (i, 0))],
        grid=(S // tm, 1, 1), dims=NN, nk=1, acc_shape=None,
        load_a=_bf(0), load_b=_b_view(1, K), epilogue=epilogue, dep=dep)[0]


def _sgu_masks():
    ii = lax.broadcasted_iota(jnp.int32, (CHUNK, CHUNK), 0)
    jj = lax.broadcasted_iota(jnp.int32, (CHUNK, CHUNK), 1)
    return ii >= jj


def _sgu_fwd(c, name, a, ln_g, wc, b_t, dep=None):
    S, AW, gd, tm = c.S, c.AW, c.gd, c.tms

    def body(a_ref, lng_ref, wc_ref, bt_ref, out_ref):
        va = a_ref[:, AW:].astype(F32)
        xc = va - jnp.mean(va, axis=-1, keepdims=True)
        vn = (xc * lax.rsqrt(jnp.mean(xc * xc, axis=-1, keepdims=True) + EPS) * lng_ref[...]).astype(BF16)
        for ch in range(tm // CHUNK):
            rows = slice(ch * CHUNK, (ch + 1) * CHUNK)
            for g in range(A_GROUPS):
                cols = slice(g * gd, (g + 1) * gd)
                mixed = _dot(wc_ref[g], vn[rows, cols], NN) + bt_ref[:, g:g + 1]
                out_ref[rows, cols] = (a_ref[rows, cols].astype(F32) * mixed).astype(BF16)

    return _pcall(
        body, name=name, ins=[a, ln_g, wc, b_t], dep=dep, grid=(S // tm,),
        in_specs=[pl.BlockSpec((tm, 2 * AW), lambda i: (i, 0)), pl.BlockSpec((1, AW), lambda i: (0, 0)),
                  pl.BlockSpec((A_GROUPS, CHUNK, CHUNK), lambda i: (0, 0, 0)),
                  pl.BlockSpec((CHUNK, A_GROUPS), lambda i: (0, 0))],
        out_specs=pl.BlockSpec((tm, AW), lambda i: (i, 0)),
        out_shape=jax.ShapeDtypeStruct((S, AW), BF16), compiler_params=_params(1))


def _ffn_fwd(c, name, h, g, col, ci, rows, dep=None):
    S, D, fw, tm = c.S, c.D, c.fw, c.tmw
    F = NDEV * fw

    def body(h_ref, g_ref, w1_ref, w2_ref, p_ref, out_ref, hn_ref, r_ref):
        h = h_ref[...]
        hn = (h * _rstd(h) * g_ref[...]).astype(BF16)
        hn_ref[...] = hn
        for j in range(NDEV):
            cols = slice(j * fw, (j + 1) * fw)
            p = jnp.maximum(_dot(hn, w1_ref[j], NT), 0.0)
            p_ref[:, cols] = p.astype(BF16)
            r_ref[:, cols] = (p * p).astype(BF16)
        out_ref[...] = h + _dot(r_ref[...], w2_ref[...].reshape(F, D), NN)

    row = pl.BlockSpec((tm, D), lambda i: (i, 0))
    return _pcall(
        body, name=name, ins=[h, g, col, rows], dep=dep, grid=(S // tm,),
        in_specs=[row, pl.BlockSpec((1, D), lambda i: (0, 0)),
                  _const((NDEV, None, fw, D), lambda i: (0, ci, 0, 0)), _const((NDEV, c.fr, D), lambda i: (0, 0, 0))],
        out_specs=[pl.BlockSpec((tm, F), lambda i: (i, 0)), row, row],
        out_shape=[jax.ShapeDtypeStruct((S, F), BF16), jax.ShapeDtypeStruct((S, D), F32),
                   jax.ShapeDtypeStruct((S, D), BF16)],
        scratch_shapes=[pltpu.VMEM((tm, F), BF16)], compiler_params=_params(1))


def _bucket_table():
    qi = np.arange(BLOCK)[:, None]
    kj = np.arange(2 * BLOCK)[None, :]
    d = np.maximum(qi + BLOCK - kj, 0)
    max_exact = N_BUCKETS // 2
    ratio = np.log(np.maximum(d, 1).astype(np.float32) / np.float32(max_exact)) / np.float32(
        math.log(MAX_DISTANCE / max_exact))
    large = np.minimum(max_exact + (ratio.astype(np.float32) * np.float32(N_BUCKETS - max_exact)).astype(np.int32),
                       N_BUCKETS - 1)
    return np.where(d < max_exact, d, large).astype(np.int32)


def _bucket_onehot():
    b = jnp.asarray(_bucket_table().reshape(1, -1))
    return (b == lax.broadcasted_iota(jnp.int32, (N_BUCKETS, b.shape[1]), 0)).astype(F32)


def _whole(t):
    return pl.BlockSpec(t.shape, lambda: (0,) * t.ndim)


def _band_bias(rel_bias_t, onehot, dep=None):
    def body(r_ref, oh_ref, out_ref):
        out_ref[...] = lax.dot_general(r_ref[...], oh_ref[...], (NN, ((), ())), preferred_element_type=F32,
                                       precision=lax.Precision.HIGHEST)

    n = onehot.shape[1]
    return _pcall(body, name="band_bias", ins=[rel_bias_t, onehot], in_specs=[_whole(rel_bias_t), _whole(onehot)],
                  dep=dep, out_shape=jax.ShapeDtypeStruct((N_HEADS, n), F32), compiler_params=_params(0))


def _band_bias_grad(dbias_list, onehot, dep=None):
    n_in = len(dbias_list)

    def body(*refs):
        oh_ref, out_ref = refs[n_in], refs[n_in + 1]
        d = refs[0][...]
        for r in refs[1:n_in]:
            d = d + r[...]
        out_ref[...] = lax.dot_general(d, oh_ref[...], (NT, ((), ())), preferred_element_type=F32,
                                       precision=lax.Precision.HIGHEST)

    ins = [*dbias_list, onehot]
    return _pcall(body, name="band_bias_grad", ins=ins, in_specs=[_whole(t) for t in ins], dep=dep,
                  out_shape=jax.ShapeDtypeStruct((N_HEADS, N_BUCKETS), F32), compiler_params=_params(0))


KV_PAIRS = N_KV_HEADS // 2
PAIR_ROWS = 2 * Q_PER_KV * BLOCK
MASKED = float(np.finfo(np.float32).min) / 2


def _slot_cols(w):
    lead = w.shape[:-1]
    return w.reshape(*lead, KV_PAIRS, 2, Q_PER_KV, HEAD_DIM).swapaxes(-3, -2).reshape(*lead, N_HEADS * HEAD_DIM)


def _unslot_cols(w):
    lead = w.shape[:-1]
    return w.reshape(*lead, KV_PAIRS, Q_PER_KV, 2, HEAD_DIM).swapaxes(-3, -2).reshape(*lead, N_HEADS * HEAD_DIM)


def _slot_rows(blocks):
    n = blocks.shape[-1]
    return blocks.reshape(KV_PAIRS, 2, Q_PER_KV, HEAD_DIM, n).swapaxes(1, 2).reshape(blocks.shape)


def _unslot_rows(blocks):
    n = blocks.shape[-1]
    return blocks.reshape(KV_PAIRS, Q_PER_KV, 2, HEAD_DIM, n).swapaxes(1, 2).reshape(blocks.shape)


def _slot_bias(bias):
    qi = np.arange(BLOCK)[:, None]
    kj = np.arange(2 * BLOCK)[None, :]
    dist = qi + BLOCK - kj
    window = (dist >= 0) & (dist < BLOCK)
    b = bias.reshape(KV_PAIRS, 2, Q_PER_KV, BLOCK, 2 * BLOCK).swapaxes(1, 2).reshape(KV_PAIRS, PAIR_ROWS, 2 * BLOCK)
    tile = lambda mk: jnp.asarray(np.tile(mk, (2 * Q_PER_KV, 1)))[None]
    return jnp.stack([jnp.where(tile(window & (kj >= BLOCK)), b, MASKED), jnp.where(tile(window), b, MASKED)])


def _unslot_bias(db):
    return db.reshape(KV_PAIRS, Q_PER_KV, 2, BLOCK, 2 * BLOCK).swapaxes(1, 2).reshape(N_HEADS, -1)


def _pair_kv(kvc_ref, kvp_ref, kvp, dkv):
    lanes = slice(kvp * LANES, (kvp + 1) * LANES)
    vlanes = slice(dkv + kvp * LANES, dkv + (kvp + 1) * LANES)
    k2 = jnp.concatenate([kvp_ref[:, lanes], kvc_ref[:, lanes]], axis=0)
    v2 = jnp.concatenate([kvp_ref[:, vlanes], kvc_ref[:, vlanes]], axis=0)
    return k2, v2


def _head_operand(ref, grp, par, low, scale=None):
    xg = ref[:, grp * LANES:(grp + 1) * LANES]
    if scale is not None:
        xg = xg * scale
    zero = jnp.zeros_like(xg)
    return jnp.where(low, xg, zero) if par == 0 else jnp.where(low, zero, xg)


def _head_probs(qh, k2, bias_rows, sink):
    s = _dot(qh, k2, NT) + bias_rows
    m = jnp.maximum(jnp.max(s, axis=-1, keepdims=True), sink)
    e = jnp.exp(s - m)
    es = jnp.exp(sink - m)
    inv = 1.0 / (jnp.sum(e, axis=-1, keepdims=True) + es)
    return e * inv, es * inv


def _attn_specs(c):
    dq, dkv2 = c.DQ, 2 * c.DKV
    return [pl.BlockSpec((BLOCK, dq), lambda n: (n, 0)),
            pl.BlockSpec((BLOCK, dkv2), lambda n: (n, 0)),
            pl.BlockSpec((BLOCK, dkv2), lambda n: (jnp.maximum(n - 1, 0), 0))]


def _bias_spec():
    return pl.BlockSpec((None, KV_PAIRS, PAIR_ROWS, 2 * BLOCK), lambda n: (jnp.minimum(n, 1), 0, 0, 0))


def _low_lanes():
    return lax.broadcasted_iota(jnp.int32, (BLOCK, LANES), 1) < HEAD_DIM


def _first_key():
    return lax.broadcasted_iota(jnp.int32, (BLOCK, 2 * BLOCK), 1) == 0


def _probs_spec():
    return pl.BlockSpec((None, KV_PAIRS, PAIR_ROWS, 2 * BLOCK), lambda n: (n, 0, 0, 0))


def _attn_fwd(c, name, q, kv, bias, sinks, dep=None):
    S, dq = c.S, c.DQ

    def body(q_ref, kvc_ref, kvp_ref, bias_ref, sink_ref, o_ref, probs_ref):
        low = _low_lanes()
        first = _first_key()
        for kvp in range(KV_PAIRS):
            k2, v2 = _pair_kv(kvc_ref, kvp_ref, kvp, c.DKV)
            for g in range(Q_PER_KV):
                grp = kvp * Q_PER_KV + g
                halves = []
                for par in range(2):
                    rows = slice((2 * g + par) * BLOCK, (2 * g + par + 1) * BLOCK)
                    qh = _head_operand(q_ref, grp, par, low, scale=HEAD_DIM ** -0.5)
                    p, ps = _head_probs(qh, k2, bias_ref[kvp, rows, :], sink_ref[(2 * kvp + par) * Q_PER_KV + g])
                    probs_ref[kvp, rows, :] = jnp.where(first, ps, p).astype(BF16)
                    halves.append(_dot(p.astype(BF16), v2, NN))
                o_ref[:, grp * LANES:(grp + 1) * LANES] = jnp.where(low, halves[0], halves[1]).astype(BF16)

    return _pcall(
        body, name=name, ins=[q, kv, kv, bias, sinks], dep=dep, grid=(c.nb,),
        in_specs=_attn_specs(c) + [_bias_spec(), SMEM],
        out_specs=[pl.BlockSpec((BLOCK, dq), lambda n: (n, 0)), _probs_spec()],
        out_shape=[jax.ShapeDtypeStruct((S, dq), BF16),
                   jax.ShapeDtypeStruct((c.nb, KV_PAIRS, PAIR_ROWS, 2 * BLOCK), BF16)],
        compiler_params=_params(1))


def _final_loss(c, h, g, target, dep=None):
    S, D, tm = c.S, c.D, c.tm

    def body(h_ref, g_ref, t_ref, dh_ref, dg_ref, loss_ref):
        i = pl.program_id(0)
        h = h_ref[...]
        gg = g_ref[...]
        r = _rstd(h)
        xh = h * r
        err = xh * gg - t_ref[...]
        lp = jnp.sum(jnp.sum(err * err, axis=1, keepdims=True), axis=0, keepdims=True) * (0.5 / D)
        dx, dg = _rms_bwd(err * (1.0 / D), h, gg, 0.0)
        dh_ref[...] = dx

        @pl.when(i == 0)
        def _():
            dg_ref[...] = dg
            loss_ref[...] = jnp.broadcast_to(lp, loss_ref.shape)

        @pl.when(i > 0)
        def _():
            dg_ref[...] += dg
            loss_ref[...] += jnp.broadcast_to(lp, loss_ref.shape)

    row = pl.BlockSpec((tm, D), lambda i: (i, 0))
    return _pcall(
        body, name="final_loss", ins=[h, g, target], dep=dep, grid=(S // tm,),
        in_specs=[row, pl.BlockSpec((1, D), lambda i: (0, 0)), row],
        out_specs=[row, pl.BlockSpec((1, D), lambda i: (0, 0)), pl.BlockSpec((1, LANES), lambda i: (0, 0))],
        out_shape=[jax.ShapeDtypeStruct((S, D), F32), jax.ShapeDtypeStruct((1, D), F32),
                   jax.ShapeDtypeStruct((1, LANES), F32)],
        compiler_params=_params(1))


def _rms_bwd_epilogue(h_idx, g_idx, res_idx):
    def epilogue(dhn, in_refs, out_refs):
        dh, dg = _rms_bwd(dhn, in_refs[h_idx][...], in_refs[g_idx][...], in_refs[res_idx][...])
        out_refs[0][...] = dh
        i = pl.program_id(0)

        @pl.when(i == 0)
        def _():
            out_refs[1][...] = dg

        @pl.when(i > 0)
        def _():
            out_refs[1][...] += dg
    return epilogue


def _stream_outs(c, tm):
    S, D = c.S, c.D
    return ([jax.ShapeDtypeStruct((S, D), F32), jax.ShapeDtypeStruct((1, D), F32)],
            [pl.BlockSpec((tm, D), lambda i, j, k: (i, 0)), pl.BlockSpec((1, D), lambda i, j, k: (0, 0))])


def _row_specs(c, tm):
    D = c.D
    return [pl.BlockSpec((tm, D), lambda i, j, k: (i, 0)), pl.BlockSpec((1, D), lambda i, j, k: (0, 0)),
            pl.BlockSpec((tm, D), lambda i, j, k: (i, 0))]


def _bwd_rows_to_stream(c, name, dy_list, slab, blk_rows, blk_idx, n_in_cols, h, g, dres, dep=None):
    S, D, tm = c.S, c.D, c.tm
    nd = len(dy_list)

    def load_a(in_refs, out_refs):
        a = in_refs[0][...]
        for r in in_refs[1:nd]:
            a = a + r[...]
        return a.astype(BF16)

    shapes, specs = _stream_outs(c, tm)
    return _mm(
        name, [*dy_list, slab, h, g, dres],
        [pl.BlockSpec((tm, n_in_cols), lambda i, j, k: (i, 0))] * nd
        + [pl.BlockSpec((NDEV, blk_rows, n_in_cols), lambda i, j, k: (0, blk_idx, 0))] + _row_specs(c, tm),
        shapes, specs, grid=(S // tm, 1, 1), dims=NT, nk=1, acc_shape=None,
        load_a=load_a, load_b=_b_view(nd, NDEV * blk_rows), epilogue=_rms_bwd_epilogue(nd + 1, nd + 2, nd + 3),
        dep=dep)


def _bwd_cols_to_stream(c, name, dy, col, ci, h, g, dres, dep=None):
    S, D, cw, tm = c.S, c.D, c.cw, c.tmw
    K = NDEV * cw
    shapes, specs = _stream_outs(c, tm)
    return _mm(
        name, [dy, col, h, g, dres],
        [pl.BlockSpec((tm, K), lambda i, j, k: (i, 0)),
         _const((NDEV, None, cw, D), lambda i, j, k: (0, ci, 0, 0))] + _row_specs(c, tm),
        shapes, specs, grid=(S // tm, 1, 1), dims=NN, nk=1, acc_shape=None,
        load_a=_bf(0), load_b=_b_view(1, K), epilogue=_rms_bwd_epilogue(2, 3, 4), dep=dep)


def _bwd_rows_data(c, name, dy, slab, blk_rows, blk_idx, dep=None):
    S, D, tm = c.S, c.D, c.tm
    K = NDEV * blk_rows

    def epilogue(acc, in_refs, out_refs):
        out_refs[0][...] = acc.astype(BF16)

    return _mm(
        name, [dy, slab],
        [pl.BlockSpec((tm, D), lambda i, j, k: (i, 0)),
         pl.BlockSpec((NDEV, blk_rows, D), lambda i, j, k: (0, blk_idx, 0))],
        [jax.ShapeDtypeStruct((S, K), BF16)], [pl.BlockSpec((tm, K), lambda i, j, k: (i, 0))],
        grid=(S // tm, 1, 1), dims=NT, nk=1, acc_shape=None,
        load_a=_bf(0), load_b=_b_view(1, K), epilogue=epilogue, dep=dep)[0]


def _wgrad_rows(c, name, a, b_list, n_a, n_b, dep=None):
    S, tm = c.S, c.tm
    nb_in = len(b_list)
    blk_rows = n_a // NDEV

    def load_b(in_refs):
        b = in_refs[1][...]
        for r in in_refs[2:1 + nb_in]:
            b = b + r[...]
        return b.astype(BF16)

    def epilogue(acc, in_refs, out_refs):
        out_refs[0][...] = acc.reshape(NDEV, blk_rows, n_b).astype(BF16)

    return _mm(
        name, [a, *b_list],
        [pl.BlockSpec((tm, n_a), lambda i, j, k: (k, 0))] + [pl.BlockSpec((tm, n_b), lambda i, j, k: (k, 0))] * nb_in,
        [jax.ShapeDtypeStruct((NDEV, blk_rows, n_b), BF16)],
        [pl.BlockSpec((NDEV, blk_rows, n_b), lambda i, j, k: (0, 0, 0))],
        grid=(1, 1, S // tm), dims=TN, nk=S // tm, acc_shape=(n_a, n_b),
        load_a=_bf(0), load_b=load_b, epilogue=epilogue, dep=dep)[0]


def _wgrad_cols(c, name, a, b, dep=None):
    S, D, cw = c.S, c.D, c.cw

    def body(a_ref, b_ref, out_ref):
        out_ref[...] = _dot(a_ref[...], b_ref[...], TN).astype(BF16)

    return _pcall(
        body, name=name, ins=[a, b], dep=dep, grid=(NDEV,),
        in_specs=[_const((S, D), lambda j: (0, 0)), pl.BlockSpec((S, cw), lambda j: (0, j))],
        out_specs=pl.BlockSpec((None, D, cw), lambda j: (j, 0, 0)),
        out_shape=jax.ShapeDtypeStruct((NDEV, D, cw), BF16), compiler_params=_params(1))


def _ffn_bwd_data(c, name, dh, p, col, ci, rows, h, g, dep=None):
    S, D, fw, tm = c.S, c.D, c.fw, c.tmw
    F = NDEV * fw

    def body(dh_ref, p_ref, w1t_ref, w2_ref, h_ref, g_ref, da_ref, out_ref, dg_ref, dhb_ref):
        i = pl.program_id(0)
        dh = dh_ref[...]
        dhb = dh.astype(BF16)
        dhb_ref[...] = dhb
        for j in range(NDEV):
            cols = slice(j * fw, (j + 1) * fw)
            da_ref[:, cols] = (_dot(dhb, w2_ref[j], NT) * (2.0 * p_ref[:, cols].astype(F32))).astype(BF16)
        dx, dg = _rms_bwd(_dot(da_ref[...], w1t_ref[...].reshape(F, D), NN), h_ref[...], g_ref[...], dh)
        out_ref[...] = dx

        @pl.when(i == 0)
        def _():
            dg_ref[...] = dg

        @pl.when(i > 0)
        def _():
            dg_ref[...] += dg

    row = pl.BlockSpec((tm, D), lambda i: (i, 0))
    wide = pl.BlockSpec((tm, F), lambda i: (i, 0))
    return _pcall(
        body, name=name, ins=[dh, p, col, rows, h, g], dep=dep, grid=(S // tm,),
        in_specs=[row, wide, _const((NDEV, None, fw, D), lambda i: (0, ci, 0, 0)),
                  _const((NDEV, c.fr, D), lambda i: (0, 0, 0)),
                  row, pl.BlockSpec((1, D), lambda i: (0, 0))],
        out_specs=[wide, row, pl.BlockSpec((1, D), lambda i: (0, 0)), row],
        out_shape=[jax.ShapeDtypeStruct((S, F), BF16), jax.ShapeDtypeStruct((S, D), F32),
                   jax.ShapeDtypeStruct((1, D), F32), jax.ShapeDtypeStruct((S, D), BF16)],
        compiler_params=_params(1))


def _ffn_bwd_w(c, name, hn, da, p, dhb, dep=None):
    S, D, fw = c.S, c.D, c.fw

    def body(hn_ref, da_ref, p_ref, dhb_ref, dw1_ref, dw2_ref):
        dw1_ref[...] = _dot(hn_ref[...], da_ref[...], TN).astype(BF16)
        pf = p_ref[...].astype(F32)
        dw2_ref[...] = _dot((pf * pf).astype(BF16), dhb_ref[...], TN).astype(BF16)

    panel = pl.BlockSpec((S, fw), lambda j: (0, j))
    return _pcall(
        body, name=name, ins=[hn, da, p, dhb], dep=dep, grid=(NDEV,),
        in_specs=[_const((S, D), lambda j: (0, 0)), panel, panel, _const((S, D), lambda j: (0, 0))],
        out_specs=[pl.BlockSpec((None, D, fw), lambda j: (j, 0, 0)), pl.BlockSpec((None, c.fr, D), lambda j: (j, 0, 0))],
        out_shape=[jax.ShapeDtypeStruct((NDEV, D, fw), BF16), jax.ShapeDtypeStruct((NDEV, c.fr, D), BF16)],
        compiler_params=_params(1))


def _attn_bwd(c, name, q, kv, do, probs, dep=None):
    S, dq, dkv = c.S, c.DQ, c.DKV
    nb = c.nb
    scale = HEAD_DIM ** -0.5

    def body(q_ref, kvc_ref, kvp_ref, do_ref, probs_ref, dq_ref, dkv_ref, dbias_ref, dsink_ref, dsink_acc,
             ds_sc, p_sc, qm_sc, dom_sc):
        n = pl.program_id(0)

        @pl.when(n == 0)
        def _():
            dkv_ref[...] = jnp.zeros_like(dkv_ref)
            dbias_ref[...] = jnp.zeros_like(dbias_ref)
            dsink_acc[...] = jnp.zeros_like(dsink_acc)

        low = _low_lanes()
        first = _first_key()
        rows_c = pl.ds(pl.multiple_of(n * BLOCK, BLOCK), BLOCK)
        rows_p = pl.ds(pl.multiple_of(jnp.maximum(n - 1, 0) * BLOCK, BLOCK), BLOCK)
        for kvp in range(KV_PAIRS):
            k2, v2 = _pair_kv(kvc_ref, kvp_ref, kvp, dkv)
            for g in range(Q_PER_KV):
                grp = kvp * Q_PER_KV + g
                halves = []
                for par in range(2):
                    rows = slice((2 * g + par) * BLOCK, (2 * g + par + 1) * BLOCK)
                    qh = _head_operand(q_ref, grp, par, low, scale=scale)
                    doh = _head_operand(do_ref, grp, par, low)
                    saved = probs_ref[kvp, rows, :]
                    ps = saved[:, 0:1].astype(F32)
                    p16 = jnp.where(first, jnp.zeros_like(saved), saved)
                    p = p16.astype(F32)
                    dp = _dot(doh, v2, NT)
                    delta = jnp.sum(p * dp, axis=-1, keepdims=True)
                    ds = p * (dp - delta)
                    dbias_ref[kvp, rows, :] += ds
                    dsink_acc[rows, kvp:kvp + 1] += -(ps * delta)
                    ds16 = ds.astype(BF16)
                    halves.append(_dot(ds16, k2, NN) * scale)
                    ds_sc[rows, :] = ds16
                    p_sc[rows, :] = p16
                    qm_sc[rows, :] = qh
                    dom_sc[rows, :] = doh
                dq_ref[:, grp * LANES:(grp + 1) * LANES] = jnp.where(low, halves[0], halves[1]).astype(BF16)
            dk2 = _dot(ds_sc[...], qm_sc[...], TN)
            dv2 = _dot(p_sc[...], dom_sc[...], TN)
            lanes = slice(kvp * LANES, (kvp + 1) * LANES)
            vlanes = slice(dkv + kvp * LANES, dkv + (kvp + 1) * LANES)
            dkv_ref[rows_p, lanes] += dk2[:BLOCK]
            dkv_ref[rows_c, lanes] += dk2[BLOCK:]
            dkv_ref[rows_p, vlanes] += dv2[:BLOCK]
            dkv_ref[rows_c, vlanes] += dv2[BLOCK:]

        @pl.when(n == nb - 1)
        def _():
            dsink_ref[...] = jnp.sum(dsink_acc[...].reshape(2 * Q_PER_KV, BLOCK, KV_PAIRS), axis=1)

    return _pcall(
        body, name=name, ins=[q, kv, kv, do, probs], dep=dep, grid=(nb,),
        in_specs=_attn_specs(c) + [pl.BlockSpec((BLOCK, dq), lambda n: (n, 0)), _probs_spec()],
        out_specs=[pl.BlockSpec((BLOCK, dq), lambda n: (n, 0)), pl.BlockSpec((S, 2 * dkv), lambda n: (0, 0)),
                   pl.BlockSpec((KV_PAIRS, PAIR_ROWS, 2 * BLOCK), lambda n: (0, 0, 0)),
                   pl.BlockSpec((2 * Q_PER_KV, KV_PAIRS), lambda n: (0, 0))],
        out_shape=[jax.ShapeDtypeStruct((S, dq), BF16), jax.ShapeDtypeStruct((S, 2 * dkv), F32),
                   jax.ShapeDtypeStruct((KV_PAIRS, PAIR_ROWS, 2 * BLOCK), F32),
                   jax.ShapeDtypeStruct((2 * Q_PER_KV, KV_PAIRS), F32)],
        scratch_shapes=[pltpu.VMEM((PAIR_ROWS, KV_PAIRS), F32), pltpu.VMEM((PAIR_ROWS, 2 * BLOCK), BF16),
                        pltpu.VMEM((PAIR_ROWS, 2 * BLOCK), BF16), pltpu.VMEM((PAIR_ROWS, LANES), BF16),
                        pltpu.VMEM((PAIR_ROWS, LANES), BF16)],
        compiler_params=_params(1))


def _sgu_bwd(c, name, a, z, dgated, ln_g, wc, wc_t, b_t, dep=None):
    S, AW, gd, tm = c.S, c.AW, c.gd, c.tms

    def body(a_ref, z_ref, dg_ref, lng_ref, wc_ref, wct_ref, bt_ref, dz_ref, dws_ref, dbt_ref, dlng_ref, dvn_ref):
        i = pl.program_id(0)

        @pl.when(i == 0)
        def _():
            dws_ref[...] = jnp.zeros_like(dws_ref)
            dbt_ref[...] = jnp.zeros_like(dbt_ref)
            dlng_ref[...] = jnp.zeros_like(dlng_ref)

        lng = lng_ref[...]
        va = a_ref[:, AW:].astype(F32)
        xc = va - jnp.mean(va, axis=-1, keepdims=True)
        rstd = lax.rsqrt(jnp.mean(xc * xc, axis=-1, keepdims=True) + EPS)
        xh = xc * rstd
        vn = (xh * lng).astype(BF16)
        causal = _sgu_masks()
        for ch in range(tm // CHUNK):
            rows = slice(ch * CHUNK, (ch + 1) * CHUNK)
            for g in range(A_GROUPS):
                cols = slice(g * gd, (g + 1) * gd)
                blk = vn[rows, cols]
                mixed = _dot(wc_ref[g], blk, NN) + bt_ref[:, g:g + 1]
                dgb = dg_ref[rows, cols].astype(F32)
                dm = dgb * a_ref[rows, cols].astype(F32)
                dbt_ref[:, g:g + 1] += jnp.sum(dm, axis=1, keepdims=True)
                dm16 = dm.astype(BF16)
                dws_ref[g] += jnp.where(causal, _dot(dm16, blk, NT), 0.0)
                dvn_ref[rows, cols] = _dot(wct_ref[g], dm16, NN)
                dz_ref[rows, cols] = (dgb * mixed * _gelu_grad(z_ref[rows, cols].astype(F32))).astype(BF16)
        dvn = dvn_ref[...]
        dlng_ref[...] += jnp.sum(dvn * xh, axis=0, keepdims=True)
        dxh = dvn * lng
        dva = rstd * (dxh - jnp.mean(dxh, axis=-1, keepdims=True) - xh * jnp.mean(dxh * xh, axis=-1, keepdims=True))
        dz_ref[:, AW:] = (dva * _gelu_grad(z_ref[:, AW:].astype(F32))).astype(BF16)

    wide = pl.BlockSpec((tm, 2 * AW), lambda i: (i, 0))
    wsp = pl.BlockSpec((A_GROUPS, CHUNK, CHUNK), lambda i: (0, 0, 0))
    btsp = pl.BlockSpec((CHUNK, A_GROUPS), lambda i: (0, 0))
    return _pcall(
        body, name=name, ins=[a, z, dgated, ln_g, wc, wc_t, b_t], dep=dep, grid=(S // tm,),
        in_specs=[wide, wide, pl.BlockSpec((tm, AW), lambda i: (i, 0)), pl.BlockSpec((1, AW), lambda i: (0, 0)),
                  wsp, wsp, btsp],
        out_specs=[wide, wsp, btsp, pl.BlockSpec((1, AW), lambda i: (0, 0))],
        out_shape=[jax.ShapeDtypeStruct((S, 2 * AW), BF16), jax.ShapeDtypeStruct((A_GROUPS, CHUNK, CHUNK), F32),
                   jax.ShapeDtypeStruct((CHUNK, A_GROUPS), F32), jax.ShapeDtypeStruct((1, AW), F32)],
        scratch_shapes=[pltpu.VMEM((tm, AW), F32)], compiler_params=_params(1))


def _adam_update(g, w_ref, m_ref, v_ref, out_refs):
    g_ref, d_ref, nm_ref, nv_ref = out_refs
    bc1 = 1.0 - ADAM_B1 ** ADAM_STEP
    bc2 = 1.0 - ADAM_B2 ** ADAM_STEP
    nm = ADAM_B1 * m_ref[...] + (1.0 - ADAM_B1) * g
    nv = ADAM_B2 * v_ref[...] + (1.0 - ADAM_B2) * (g * g)
    g_ref[...] = g
    nm_ref[...] = nm
    nv_ref[...] = nv
    d_ref[...] = -ADAM_LR * ((nm * (1.0 / bc1)) / (jnp.sqrt(nv * (1.0 / bc2)) + ADAM_EPS) + ADAM_WD * w_ref[...])


def _adamw_packed(name, sums, offs, ws, ms, vs, dep=None):
    n = len(ws)

    def body(*refs):
        s_ref, w_refs, m_refs, v_refs = refs[0], refs[1:1 + n], refs[1 + n:1 + 2 * n], refs[1 + 2 * n:1 + 3 * n]
        outs = refs[1 + 3 * n:]
        for k in range(n):
            _adam_update(s_ref[offs[k]:offs[k + 1], :], w_refs[k], m_refs[k], v_refs[k], outs[4 * k:4 * k + 4])

    ins = [sums, *ws, *ms, *vs]
    shapes = [jax.ShapeDtypeStruct(w.shape, F32) for w in ws for _ in range(4)]
    return _pcall(body, name=name, ins=ins, in_specs=[_whole(t) for t in ins], dep=dep,
                  out_shape=shapes, out_specs=[_whole(t) for t in shapes], compiler_params=_params(0))


def _adamw(name, parts, part_block, part_index, w, m, v, tr, row_off=0, n_rows=None, prev=None, dep=None):
    R, C = w.shape
    n_rows = R if n_rows is None else n_rows
    assert n_rows % tr == 0 and row_off % tr == 0

    def body(p_ref, w_ref, m_ref, v_ref, *rest):
        g = p_ref[0].astype(F32)
        for s in range(1, part_block[0]):
            g = g + p_ref[s].astype(F32)
        _adam_update(g, w_ref, m_ref, v_ref, rest[-4:])

    ob = row_off // tr
    row = pl.BlockSpec((tr, C), lambda i: (ob + i, 0))
    out = jax.ShapeDtypeStruct((R, C), F32)
    chained = prev is not None
    return _pcall(
        body, name=name, ins=[parts, w, m, v] + (list(prev) if chained else []), dep=dep, grid=(n_rows // tr,),
        in_specs=[pl.BlockSpec(part_block, part_index), row, row, row] + ([ANY] * 4 if chained else []),
        out_specs=[row, row, row, row], out_shape=[out, out, out, out],
        input_output_aliases={4 + t: t for t in range(4)} if chained else {}, compiler_params=_params(1))


def _sum_parts(name, parts, dep=None):
    def body(p_ref, out_ref):
        g = p_ref[0]
        for s in range(1, parts.shape[0]):
            g = g + p_ref[s]
        out_ref[...] = g

    return _pcall(body, name=name, ins=[parts], in_specs=[_whole(parts)], dep=dep,
                  out_shape=jax.ShapeDtypeStruct(parts.shape[1:], F32), compiler_params=_params(0))


def _place():
    return lax.axis_index("x"), lax.axis_index("y"), lax.axis_index("c")


def _slot(px, py, pc):
    return 4 * px + 2 * py + pc


def _peer(k, x, y, c):
    return x ^ ((k >> 2) & 1), y ^ ((k >> 1) & 1), c ^ (k & 1)


SEND_PEERS = {"exchange": tuple(range(1, NDEV)), "gather": (1, 2, 4, 6), "forward": (2, 4, 6),
              "broadcast": tuple(range(1, NDEV))}


def _n_sems(mode, n_lands):
    return n_lands * (len(SEND_PEERS[mode]) + (mode != "forward"))


def _send_copies(mode, src_refs, land_refs, send_sems, recv_sems):
    x, y, c = _place()
    me = _slot(x, y, c)
    peers = SEND_PEERS[mode]
    remote, local = [], []
    for i, k in enumerate(peers):
        peer = _peer(k, x, y, c)
        for a, land in enumerate(land_refs):
            if mode == "exchange":
                src, dst, to = src_refs[a].at[_slot(*peer)], land.at[me], peer
            elif mode in ("gather", "broadcast"):
                src, dst, to = src_refs[a], land.at[me], peer
            else:
                src = dst = land.at[_slot(*peer)]
                to = (x, y, 1 - c)
            s = a * len(peers) + i
            remote.append(pltpu.make_async_remote_copy(src_ref=src, dst_ref=dst, send_sem=send_sems.at[s],
                                                       recv_sem=recv_sems.at[s], device_id=to, device_id_type=MESH))
    if mode != "forward":
        for a, land in enumerate(land_refs):
            src = src_refs[a].at[me] if mode == "exchange" else src_refs[a]
            local.append(pltpu.make_async_copy(src, land.at[me], send_sems.at[len(land_refs) * len(peers) + a]))
    return remote, local


def _send_start_groups(name, groups, mode, collective_id=None, dep=None):
    sizes = [(len(s), len(l)) for s, l in groups]
    flat = [t for s, l in groups for t in (*s, *l)]
    n_in, ng = len(flat), len(groups)

    def body(*refs):
        sems, token, at = refs[n_in:n_in + 2 * ng], refs[-1], 0
        if collective_id is not None:
            x, y, c = _place()
            barrier = pltpu.get_barrier_semaphore()
            shake = (1,) if mode == "forward" else SEND_PEERS[mode]
            for k in shake:
                pl.semaphore_signal(barrier, inc=1, device_id=_peer(k, x, y, c), device_id_type=MESH)
            pl.semaphore_wait(barrier, len(shake))
        for gi, (ns, nl) in enumerate(sizes):
            remote, local = _send_copies(mode, refs[at:at + ns], refs[at + ns:at + ns + nl], sems[2 * gi],
                                         sems[2 * gi + 1])
            for cp in remote + local:
                cp.start()
            at += ns + nl
        token[...] = jnp.zeros_like(token)

    sem_shapes = [pltpu.SemaphoreType.DMA((_n_sems(mode, nl),)) for _, nl in sizes for _ in range(2)]
    if any(dep is t for t in flat):
        dep = None
    out = _pcall(
        body, name=name, ins=[pltpu.with_memory_space_constraint(t, pltpu.HBM) for t in flat],
        in_specs=[HBM] * n_in, dep=dep,
        out_shape=(*sem_shapes, *[pltpu.HBM(t.shape, t.dtype) for t in flat], jax.ShapeDtypeStruct((8, LANES), F32)),
        out_specs=(*[SEM] * (2 * ng), *[HBM] * n_in, pl.BlockSpec(memory_space=pltpu.VMEM)),
        input_output_aliases={i: 2 * ng + i for i in range(n_in)},
        compiler_params=pltpu.CompilerParams(has_side_effects=pltpu.SideEffectType.DATAFLOW_SIDE_EFFECTING,
                                             collective_id=collective_id))
    started, at = [], 2 * ng
    for gi, (ns, nl) in enumerate(sizes):
        started.append((out[-1], out[2 * gi], out[2 * gi + 1], list(out[at:at + ns]), list(out[at + ns:at + ns + nl])))
        at += ns + nl
    return started


def _send_start(name, srcs, lands, mode, collective_id=None, dep=None):
    if mode == "forward":
        collective_id = SIBLING_COLLECTIVE_ID
    return _send_start_groups(name, [(srcs, lands)], mode, collective_id=collective_id, dep=dep)[0]


def _send_wait(name, started, mode, dep=None):
    _, send_sems, recv_sems, srcs, lands = started
    n_src, n = len(srcs), len(lands)

    def body(*refs):
        src_refs, land_refs = refs[:n_src], refs[n_src:n_src + n]
        ssem, rsem = refs[n_src + n], refs[n_src + n + 1]
        remote, local = _send_copies(mode, src_refs, land_refs, ssem, rsem)
        for cp in remote:
            cp.wait_send()
            cp.wait_recv()
        for cp in local:
            cp.wait()

    thru = [pltpu.HBM(t.shape, t.dtype) for t in [*srcs, *lands]]
    out = _pcall(
        body, name=name, ins=[*srcs, *lands, send_sems, recv_sems], in_specs=[HBM] * (n_src + n) + [SEM, SEM], dep=dep,
        out_shape=tuple(thru), out_specs=tuple([HBM] * (n_src + n)),
        input_output_aliases={i: i for i in range(n_src + n)},
        compiler_params=pltpu.CompilerParams(has_side_effects=pltpu.SideEffectType.DATAFLOW_SIDE_EFFECTING))
    return list(out[n_src:])


def _landing(block):
    return lax.empty((NDEV, *block.shape), block.dtype)


def _rows128(t):
    flat = t.reshape(-1)
    n = flat.shape[0]
    rows = -(-n // (8 * LANES)) * 8
    return jnp.pad(flat, (0, rows * LANES - n)).reshape(rows, LANES)


def kernel(x, mix_norm_g, ffn_norm_g, a_w_in, a_ln_g, a_w_spatial, a_b_spatial, a_w_out, kv_norm_g, w_k, w_v, b_w_q, b_sinks, b_w_o, rel_bias, ffn_w1, ffn_w2, final_norm_g, loss_target, m_mix_norm_g, m_ffn_norm_g, m_a_w_in, m_a_ln_g, m_a_w_spatial, m_a_b_spatial, m_a_w_out, m_kv_norm_g, m_w_k, m_w_v, m_b_w_q, m_b_sinks, m_b_w_o, m_rel_bias, m_ffn_w1, m_ffn_w2, m_final_norm_g, v_mix_norm_g, v_ffn_norm_g, v_a_w_in, v_a_ln_g, v_a_w_spatial, v_a_b_spatial, v_a_w_out, v_kv_norm_g, v_w_k, v_w_v, v_b_w_q, v_b_sinks, v_b_w_o, v_rel_bias, v_ffn_w1, v_ffn_w2, v_final_norm_g):
    c = _config(x, a_w_in, a_w_out, w_k, b_w_q, b_w_o, ffn_w1, ffn_w2)
    S, D, LA, LB, LF = c.S, c.D, c.LA, c.LB, c.LF
    weights = dict(mix_norm_g=mix_norm_g, ffn_norm_g=ffn_norm_g, a_w_in=a_w_in, a_ln_g=a_ln_g, a_w_spatial=a_w_spatial,
                   a_b_spatial=a_b_spatial, a_w_out=a_w_out, kv_norm_g=kv_norm_g, w_k=w_k, w_v=w_v, b_w_q=b_w_q,
                   b_sinks=b_sinks, b_w_o=b_w_o, rel_bias=rel_bias, ffn_w1=ffn_w1, ffn_w2=ffn_w2,
                   final_norm_g=final_norm_g)
    m_in = dict(mix_norm_g=m_mix_norm_g, ffn_norm_g=m_ffn_norm_g, a_w_in=m_a_w_in, a_ln_g=m_a_ln_g,
                a_w_spatial=m_a_w_spatial, a_b_spatial=m_a_b_spatial, a_w_out=m_a_w_out, kv_norm_g=m_kv_norm_g,
                w_k=m_w_k, w_v=m_w_v, b_w_q=m_b_w_q, b_sinks=m_b_sinks, b_w_o=m_b_w_o, rel_bias=m_rel_bias,
                ffn_w1=m_ffn_w1, ffn_w2=m_ffn_w2, final_norm_g=m_final_norm_g)
    v_in = dict(mix_norm_g=v_mix_norm_g, ffn_norm_g=v_ffn_norm_g, a_w_in=v_a_w_in, a_ln_g=v_a_ln_g,
                a_w_spatial=v_a_w_spatial, a_b_spatial=v_a_b_spatial, a_w_out=v_a_w_out, kv_norm_g=v_kv_norm_g,
                w_k=v_w_k, w_v=v_w_v, b_w_q=v_b_w_q, b_sinks=v_b_sinks, b_w_o=v_b_w_o, rel_bias=v_rel_bias,
                ffn_w1=v_ffn_w1, ffn_w2=v_ffn_w2, final_norm_g=v_final_norm_g)
    names = list(weights)
    seq = _Seq()
    me = _slot(*_place())
    bf = lambda t: t.astype(BF16)

    tr = lambda t: bf(jnp.swapaxes(t, -1, -2))

    def start(tag, some):
        got = seq(_send_start_groups, f"weights_start_{tag}", [(grp, [_landing(t) for t in grp]) for grp in some],
                  "gather")
        seq.last = got[0][0]
        return got

    started = start("first", [[tr(a_w_in[0])[None], a_ln_g]])
    _, a_w_in_, a_w_out_, ffn_w1_, ffn_w2_, w_k_, w_v_, b_w_q_, b_w_o_ = lax.optimization_barrier(
        (started[0][0], a_w_in, a_w_out, ffn_w1, ffn_w2, w_k, w_v, b_w_q, b_w_o))
    groups = []
    for l in range(LA):
        groups += [[tr(a_w_in_[l])[None]], [bf(a_w_out_[l])], [tr(ffn_w1_[l])[None], bf(ffn_w2_[l])]]
    gb = 3 * LA
    for l in range(LB):
        extra = [bf(jnp.concatenate([w_k_, w_v_], axis=1))] if l == 0 else []
        groups += [extra + [bf(b_w_q_[l]), bf(b_w_o_[l])], [tr(ffn_w1_[LA + l])[None], bf(ffn_w2_[LA + l])]]
    started += start("rest", groups[1:])
    forwarding = {}

    def forward(i):
        lands = seq(_send_wait, f"weights_wait{i}", started[i], "gather")
        forwarding[i] = seq(_send_start, f"weights_forward{i}", [], lands, "forward")

    def arrive(i):
        if i not in forwarding:
            forward(i)
        return seq(_send_wait, f"weights_arrive{i}", forwarding[i], "forward")

    causal = jnp.tril(jnp.ones((CHUNK, CHUNK), bool))
    wsp = jnp.where(causal[None, None], a_w_spatial, 0.0)
    wsp16 = wsp.astype(BF16)
    wsp16_t = jnp.swapaxes(wsp, -1, -2).astype(BF16)
    bsp_t = jnp.swapaxes(a_b_spatial, -1, -2)
    mix_g = mix_norm_g.reshape(-1, 1, D)
    ffn_g = ffn_norm_g.reshape(-1, 1, D)
    kv_g = kv_norm_g.reshape(1, D)
    fin_g = final_norm_g.reshape(1, D)
    onehot = _bucket_onehot()
    bias = _slot_bias(seq(_band_bias, rel_bias.T, onehot).reshape(N_HEADS, BLOCK, 2 * BLOCK))

    h = x.reshape(S, D)
    sav_a, sav_b, wts_a, wts_b = [], [], [], []
    for l in range(LA):
        got = arrive(3 * l)
        w_in = got[0]
        if l == 0:
            ln_g_full = jnp.transpose(got[1], (1, 0, 2)).reshape(LA, 1, c.AW)
        z, a, hn = seq(_a_in_fwd, c, f"a_in_fwd{l}", h, mix_g[l], w_in, 0)
        forward(3 * l + 1)
        gated = seq(_sgu_fwd, c, f"sgu_fwd{l}", a, ln_g_full[l], wsp16[l], bsp_t[l])
        (wout,) = arrive(3 * l + 1)
        if l > 0:
            forward(3 * l + 2)
        h1 = seq(_mm_res, c, f"a_out_fwd{l}", gated, wout, c.ar, 0, h)
        w1, rows = arrive(3 * l + 2)
        if l == LA - 1:
            forward(gb)
        p, h2, hnf = seq(_ffn_fwd, c, f"ffn_fwd{l}", h1, ffn_g[l], w1, 0, rows)
        sav_a.append((h, z, a, hn, gated, h1, p, hnf))
        wts_a.append((w_in, 0, w1, 0, rows, wout, 0))
        h = h2
    h_kv = h
    for l in range(LB):
        got = arrive(gb + 2 * l)
        if l == 0:
            wkv, got = got[0], got[1:]
            kv, hkv = seq(_rms_mm_rows, c, "kv_fwd", h, kv_g, wkv, c.kr, 0, 2 * c.DKV)
        wq, wo = _slot_cols(got[0]), _slot_rows(got[1])
        q, hn = seq(_rms_mm_rows, c, f"q_fwd{l}", h, mix_g[LA + l], wq, c.qr, 0, c.DQ)
        forward(gb + 2 * l + 1)
        o, probs = seq(_attn_fwd, c, f"attn_fwd{l}", q, kv, bias, b_sinks[l])
        h1 = seq(_mm_res, c, f"o_fwd{l}", o, wo, c.orr, 0, h)
        w1, rows = arrive(gb + 2 * l + 1)
        if l + 1 < LB:
            forward(gb + 2 * l + 2)
        p, h2, hnf = seq(_ffn_fwd, c, f"ffn_fwd{LA + l}", h1, ffn_g[LA + l], w1, 0, rows)
        sav_b.append((h, q, hn, o, probs, h1, p, hnf))
        wts_b.append((wq, wo, w1, rows))
        h = h2
    dh, d_fin_g, loss_row = seq(_final_loss, c, h, fin_g, loss_target.reshape(S, D))

    results = {}
    in_flight, exchanges = [], []

    def update(k, parts, layer, col_blk=0):
        w = weights[k]
        rows_l, ncols = (w.shape[-2], w.shape[-1]) if w.ndim == 3 else w.shape
        flat = lambda t: t.reshape(-1, ncols)
        tr = min(256, rows_l)
        results[k] = seq(_adamw, f"adamw_{k}{layer}", parts, (NDEV, tr, ncols), lambda i: (0, i, col_blk),
                         flat(w), flat(m_in[k]), flat(v_in[k]), tr, row_off=layer * rows_l, n_rows=rows_l,
                         prev=results.get(k))

    def land(tag, entry):
        lands = seq(_send_wait, f"grads_wait_{tag}", entry[1], "exchange")
        for keys, parts in zip(entry[0], lands):
            for k, layer, col_blk in keys:
                update(k, parts, layer, col_blk)

    def send(tag, items):
        slabs = [t for _, t in items]
        own = [lax.empty(t.shape, t.dtype) for t in slabs]
        exchanges.append(tag)
        st = seq(_send_start, f"grads_start_{tag}", slabs, own, "exchange",
                 collective_id=SIBLING_COLLECTIVE_ID + len(exchanges))
        in_flight.append((tag, ([k for k, _ in items], st)))
        while len(in_flight) > EXCHANGE_LAG:
            land(*in_flight.pop(0))

    d_mix_g, d_ffn_g = [None] * LF, [None] * LF
    dkv_list, dbias_list, dsink_list = [], [], [None] * LB

    def ffn_bwd(lf, dh, h1, p, hnf, w1, w1_i, rows):
        da, dh1, d_ffn_g[lf], dhb = seq(_ffn_bwd_data, c, f"ffn_bwd_data{lf}", dh, p, w1, w1_i, rows, h1, ffn_g[lf])
        dw1, dw2 = seq(_ffn_bwd_w, c, f"ffn_bwd_w{lf}", hnf, da, p, dhb)
        send(f"ffn{lf}", [([("ffn_w1", lf, 0)], dw1), ([("ffn_w2", lf, 0)], dw2)])
        return dh1

    for l in reversed(range(LB)):
        h0, q, hn, o, probs, h1, p, hnf = sav_b[l]
        wq, wo, w1, rows = wts_b[l]
        dh1 = ffn_bwd(LA + l, dh, h1, p, hnf, w1, 0, rows)
        do = seq(_bwd_rows_data, c, f"o_bwd_data{l}", dh1, wo, c.orr, 0)
        dwo = _unslot_rows(seq(_wgrad_rows, c, f"o_bwd_w{l}", o, [dh1], c.DQ, D))
        dq, dkv, dbias, dsink = seq(_attn_bwd, c, f"attn_bwd{l}", q, kv, do, probs)
        dsink_list[l] = dsink.reshape(Q_PER_KV, 2, KV_PAIRS).transpose(2, 1, 0).reshape(1, N_HEADS)
        dkv_list.append(dkv)
        dbias_list.append(_unslot_bias(dbias))
        dwq = _unslot_cols(seq(_wgrad_rows, c, f"q_bwd_w{l}", hn, [dq], D, c.DQ))
        send(f"attn{l}", [([("b_w_o", l, 0)], dwo), ([("b_w_q", l, 0)], dwq)])
        dh, d_mix_g[LA + l] = seq(_bwd_rows_to_stream, c, f"q_bwd_data{l}", [dq], wq, c.qr, 0, c.DQ, h0,
                                  mix_g[LA + l], dh1)
    dwkv = seq(_wgrad_rows, c, "kv_bwd_w", hkv, dkv_list, D, 2 * c.DKV)
    send("kv", [([("w_k", 0, 0), ("w_v", 0, 1)], dwkv)])
    dh, d_kv_g = seq(_bwd_rows_to_stream, c, "kv_bwd_data", dkv_list, wkv, c.kr, 0, 2 * c.DKV, h_kv, kv_g, dh)
    d_rel_t = seq(_band_bias_grad, dbias_list, onehot)
    d_wsp, d_bsp, d_lng = [None] * LA, [None] * LA, [None] * LA
    for l in reversed(range(LA)):
        h0, z, a, hn, gated, h1, p, hnf = sav_a[l]
        w_in, in_i, w1, w1_i, rows, wout, wout_i = wts_a[l]
        dh1 = ffn_bwd(l, dh, h1, p, hnf, w1, w1_i, rows)
        dgated = seq(_bwd_rows_data, c, f"a_out_bwd_data{l}", dh1, wout, c.ar, wout_i)
        dwout = seq(_wgrad_rows, c, f"a_out_bwd_w{l}", gated, [dh1], c.AW, D)
        send(f"a_out{l}", [([("a_w_out", l, 0)], dwout)])
        dz, d_wsp[l], dbt, d_lng[l] = seq(_sgu_bwd, c, f"sgu_bwd{l}", a, z, dgated, ln_g_full[l], wsp16[l],
                                          wsp16_t[l], bsp_t[l])
        d_bsp[l] = dbt.T
        dwin = seq(_wgrad_cols, c, f"a_in_bwd_w{l}", hn, dz)
        send(f"a_in{l}", [([("a_w_in", l, 0)], dwin)])
        dh, d_mix_g[l] = seq(_bwd_cols_to_stream, c, f"a_in_bwd_data{l}", dz, w_in, in_i, h0, mix_g[l], dh1)
    grad_x = dh.reshape(1, S, D)

    small = {
        "mix_norm_g": jnp.concatenate(d_mix_g, axis=0), "ffn_norm_g": jnp.concatenate(d_ffn_g, axis=0),
        "a_w_spatial": jnp.stack(d_wsp), "a_b_spatial": jnp.stack(d_bsp), "kv_norm_g": d_kv_g,
        "b_sinks": jnp.concatenate(dsink_list, axis=0), "rel_bias": d_rel_t.T, "final_norm_g": d_fin_g,
    }
    small_names = list(small)
    packs = [_rows128(small[k]) for k in small_names] + [_rows128(jnp.concatenate(d_lng, axis=0)), _rows128(loss_row)]
    offs = [int(o) for o in np.cumsum([0] + [p.shape[0] for p in packs])]
    Rs = offs[-1] + (-offs[-1]) % (8 * NDEV)
    packed = jnp.concatenate(packs + [jnp.zeros((Rs - offs[-1], LANES), F32)], axis=0)
    slab = packed.reshape(NDEV, Rs // NDEV, LANES)
    st = seq(_send_start, "small_grads_start", [slab], [lax.empty(slab.shape, slab.dtype)], "exchange")
    while len(in_flight) > 1:
        land(*in_flight.pop(0))
    (parts,) = seq(_send_wait, "small_grads_wait", st, "exchange")
    mine = seq(_sum_parts, "small_grads_sum", parts)
    st = seq(_send_start, "small_sums_start", [mine], [_landing(mine)], "broadcast")
    while in_flight:
        land(*in_flight.pop(0))
    (sums,) = seq(_send_wait, "small_sums_wait", st, "broadcast")
    sums = sums.reshape(Rs, LANES)
    loss = sums[offs[-2], 0]

    grads, deltas, new_m, new_v = {}, {}, {}, {}

    def put(k, outs, shape):
        grads[k], deltas[k], new_m[k], new_v[k] = (t.reshape(shape) for t in outs)

    outs = seq(_adamw_packed, "adamw_small", sums, offs, *[[_rows128(d[k]) for k in small_names]
                                                          for d in (weights, m_in, v_in)])
    for n_, k in enumerate(small_names):
        shape = weights[k].shape
        size = int(np.prod(shape))
        put(k, [t.reshape(-1)[:size] for t in outs[4 * n_:4 * n_ + 4]], shape)
    lng_sum = sums[offs[-3]:offs[-2]].reshape(-1)[:LA * c.AW].reshape(LA, c.AW)
    lng_mine = lax.dynamic_slice_in_dim(lng_sum, me * c.ar, c.ar, axis=1)
    lng_parts = jnp.concatenate([lng_mine[None], jnp.zeros((NDEV - 1, LA, c.ar), F32)], axis=0)
    put("a_ln_g", seq(_adamw, "adamw_ln_g", lng_parts, (NDEV, LA, c.ar), lambda i: (0, 0, 0),
                      a_ln_g, m_in["a_ln_g"], v_in["a_ln_g"], LA), a_ln_g.shape)
    for k in ("a_w_in", "ffn_w1", "ffn_w2", "a_w_out", "b_w_o", "b_w_q", "w_k", "w_v"):
        put(k, results[k], weights[k].shape)

    return (loss, grad_x, *[grads[k] for k in names], *[deltas[k] for k in names],
            *[new_m[k] for k in names], *[new_v[k] for k in names])
```

```python
import numpy as np
import math
import jax
import jax.numpy as jnp
from jax import lax
from jax.experimental import pallas as pl
from jax.experimental.pallas import tpu as pltpu

F32 = jnp.float32
BF16 = jnp.bfloat16

NDEV = 8
EPS = 1e-6
CHUNK = 128
A_GROUPS = 8
N_HEADS = 16
N_KV_HEADS = 4
Q_PER_KV = N_HEADS // N_KV_HEADS
HEAD_DIM = 64
BLOCK = 128
N_BUCKETS = 32
MAX_DISTANCE = 128
ADAM_LR, ADAM_B1, ADAM_B2, ADAM_EPS, ADAM_WD, ADAM_STEP = 0.001, 0.9, 0.999, 1e-08, 0.01, 10
LANES = 128
VMEM_LIMIT = 56 * 1024 * 1024
INV_SQRT2 = 0.7071067811865476
INV_SQRT_2PI = 0.3989422804014327
EXCHANGE_LAG = 4
SIBLING_COLLECTIVE_ID = 0

HBM = pl.BlockSpec(memory_space=pltpu.HBM)
SMEM = pl.BlockSpec(memory_space=pltpu.SMEM)
ANY = pl.BlockSpec(memory_space=pl.ANY)
SEM = pl.BlockSpec(memory_space=pltpu.SEMAPHORE)
MESH = pl.DeviceIdType.MESH


def _params(n_grid):
    return pltpu.CompilerParams(dimension_semantics=("arbitrary",) * n_grid, vmem_limit_bytes=VMEM_LIMIT)


def _const(block, index_map):
    return pl.BlockSpec(block, index_map, pipeline_mode=pl.Buffered(1))


def _pcall(body, *, ins, in_specs, dep=None, **kw):
    n_in = len(ins)
    if dep is None or any(dep is t for t in ins):
        return pl.pallas_call(body, in_specs=list(in_specs), **kw)(*ins)

    def with_dep(*refs):
        body(*refs[:n_in], *refs[n_in + 1:])

    return pl.pallas_call(with_dep, in_specs=[*in_specs, ANY], **kw)(*ins, dep)


class _Seq:
    def __init__(self):
        self.last = None

    def __call__(self, fn, *args, **kw):
        out = fn(*args, dep=self.last, **kw)
        self.last = out[0] if isinstance(out, (tuple, list)) else out
        return out


def _rstd(h):
    return lax.rsqrt(jnp.mean(h * h, axis=-1, keepdims=True) + EPS)


def _rms_bwd(dhn, h, g, dres):
    r = _rstd(h)
    xh = h * r
    dg = jnp.sum(dhn * xh, axis=0, keepdims=True)
    dxh = dhn * g
    dx = r * (dxh - xh * jnp.mean(dxh * xh, axis=-1, keepdims=True))
    return dres + dx, dg


def _gelu(z):
    return 0.5 * z * (1.0 + lax.erf(z * INV_SQRT2))


def _gelu_grad(z):
    return 0.5 * (1.0 + lax.erf(z * INV_SQRT2)) + z * (jnp.exp(-0.5 * z * z) * INV_SQRT_2PI)


def _dot(a, b, dims):
    return lax.dot_general(a, b, (dims, ((), ())), preferred_element_type=F32)


NN = ((1,), (0,))
NT = ((1,), (1,))
TN = ((0,), (0,))


def _mm(name, ins, in_specs, out_shapes, out_specs, *, grid, dims, nk, acc_shape, load_a, load_b, epilogue,
        dep=None):
    n_in, n_out = len(ins), len(out_shapes)
    kax = len(grid) - 1

    def body(*refs):
        in_refs = refs[:n_in]
        out_refs = refs[n_in:n_in + n_out]
        a = load_a(in_refs, out_refs)
        b = load_b(in_refs)
        prod = _dot(a, b, dims)
        if nk == 1:
            epilogue(prod, in_refs, out_refs)
        else:
            acc = refs[n_in + n_out]
            k = pl.program_id(kax)

            @pl.when(k == 0)
            def _():
                acc[...] = prod

            @pl.when(k > 0)
            def _():
                acc[...] += prod

            @pl.when(k == nk - 1)
            def _():
                epilogue(acc[...], in_refs, out_refs)

    return _pcall(
        body, name=name, ins=ins, in_specs=in_specs, dep=dep, grid=grid, out_specs=out_specs, out_shape=out_shapes,
        scratch_shapes=[pltpu.VMEM(acc_shape, F32)] if nk > 1 else [], compiler_params=_params(len(grid)))


def _bf(ref_idx):
    return lambda in_refs, *_: in_refs[ref_idx][...].astype(BF16)


def _b_view(ref_idx, rows):
    def load(in_refs):
        b = in_refs[ref_idx][...]
        return b.reshape(rows, b.shape[-1])
    return load


class Cfg:
    pass


def _config(x, a_w_in, a_w_out, w_k, b_w_q, b_w_o, ffn_w1, ffn_w2):
    c = Cfg()
    c.S, c.D = x.shape[1], x.shape[2]
    c.LA, _, c.cw = a_w_in.shape
    c.AW2 = NDEV * c.cw
    c.AW = c.AW2 // 2
    c.gd = c.AW // A_GROUPS
    c.ar = a_w_out.shape[1]
    c.LF, _, c.fw = ffn_w1.shape
    c.fr = ffn_w2.shape[1]
    c.LB, c.qr, c.DQ = b_w_q.shape
    c.orr = b_w_o.shape[1]
    c.kr, c.DKV = w_k.shape
    c.tm = min(1024, c.S)
    c.tmw = min(512, c.S)
    c.tms = min(256, c.S)
    c.nb = c.S // BLOCK
    assert c.cw == c.fw == c.fr and c.AW == NDEV * c.ar and c.D == NDEV * c.qr == NDEV * c.kr
    assert c.DQ == NDEV * c.orr == N_HEADS * HEAD_DIM and c.DKV == N_KV_HEADS * HEAD_DIM
    assert c.S % c.tm == 0 and c.S % c.tmw == 0 and c.S % c.tms == 0 and c.tms % CHUNK == 0 and c.gd % LANES == 0
    assert c.LA >= 1 and c.LB >= 1 and c.LF == c.LA + c.LB
    return c


def _cached_rms(h_idx, g_idx, hn_out_idx, jax_axis=1):
    def load(in_refs, out_refs):
        hn_ref = out_refs[hn_out_idx]

        @pl.when(pl.program_id(jax_axis) == 0)
        def _():
            h = in_refs[h_idx][...]
            hn_ref[...] = (h * _rstd(h) * in_refs[g_idx][...]).astype(BF16)

        return hn_ref[...]
    return load


def _a_in_fwd(c, name, h, g, col, ci, dep=None):
    S, D, cw, tm = c.S, c.D, c.cw, c.tmw

    def body(h_ref, g_ref, w_ref, z_ref, a_ref, hn_ref):
        h = h_ref[...]
        hn = (h * _rstd(h) * g_ref[...]).astype(BF16)
        hn_ref[...] = hn
        for j in range(NDEV):
            cols = slice(j * cw, (j + 1) * cw)
            z = _dot(hn, w_ref[j], NT)
            z_ref[:, cols] = z.astype(BF16)
            a_ref[:, cols] = _gelu(z).astype(BF16)

    row = pl.BlockSpec((tm, D), lambda i: (i, 0))
    wide = pl.BlockSpec((tm, c.AW2), lambda i: (i, 0))
    return _pcall(
        body, name=name, ins=[h, g, col], dep=dep, grid=(S // tm,),
        in_specs=[row, pl.BlockSpec((1, D), lambda i: (0, 0)), _const((NDEV, None, cw, D), lambda i: (0, ci, 0, 0))],
        out_specs=[wide, wide, row],
        out_shape=[jax.ShapeDtypeStruct((S, c.AW2), BF16), jax.ShapeDtypeStruct((S, c.AW2), BF16),
                   jax.ShapeDtypeStruct((S, D), BF16)],
        compiler_params=_params(1))


def _rms_mm_rows(c, name, h, g, slab, blk_rows, blk_idx, n_out, dep=None):
    S, D, tm = c.S, c.D, c.tm

    def epilogue(acc, in_refs, out_refs):
        out_refs[0][...] = acc.astype(BF16)

    return _mm(
        name, [h, slab, g],
        [pl.BlockSpec((tm, D), lambda i, j, k: (i, 0)),
         pl.BlockSpec((NDEV, blk_rows, n_out), lambda i, j, k: (0, blk_idx, 0)),
         pl.BlockSpec((1, D), lambda i, j, k: (0, 0))],
        [jax.ShapeDtypeStruct((S, n_out), BF16), jax.ShapeDtypeStruct((S, D), BF16)],
        [pl.BlockSpec((tm, n_out), lambda i, j, k: (i, 0)), pl.BlockSpec((tm, D), lambda i, j, k: (i, 0))],
        grid=(S // tm, 1, 1), dims=NN, nk=1, acc_shape=None,
        load_a=_cached_rms(0, 2, 1), load_b=_b_view(1, NDEV * blk_rows), epilogue=epilogue, dep=dep)


def _mm_res(c, name, a, slab, blk_rows, blk_idx, res, dep=None):
    S, D, tm = c.S, c.D, c.tm
    K = NDEV * blk_rows

    def epilogue(acc, in_refs, out_refs):
        out_refs[0][...] = in_refs[2][...] + acc

    return _mm(
        name, [a, slab, res],
        [pl.BlockSpec((tm, K), lambda i, j, k: (i, 0)),
         pl.BlockSpec((NDEV, blk_rows, D), lambda i, j, k: (0, blk_idx, 0)),
         pl.BlockSpec((tm, D), lambda i, j, k: (i, 0))],
        [jax.ShapeDtypeStruct((S, D), F32)], [pl.BlockSpec((tm, D), lambda i, j, k: (i, 0))],
        grid=(S // tm, 1, 1), dims=NN, nk=1, acc_shape=None,
        load_a=_bf(0), load_b=_b_view(1, K), epilogue=epilogue, dep=dep)[0]


def _sgu_masks():
    ii = lax.broadcasted_iota(jnp.int32, (CHUNK, CHUNK), 0)
    jj = lax.broadcasted_iota(jnp.int32, (CHUNK, CHUNK), 1)
    return ii >= jj


def _sgu_fwd(c, name, a, ln_g, wc, b_t, dep=None):
    S, AW, gd, tm = c.S, c.AW, c.gd, c.tms

    def body(a_ref, lng_ref, wc_ref, bt_ref, out_ref):
        va = a_ref[:, AW:].astype(F32)
        xc = va - jnp.mean(va, axis=-1, keepdims=True)
        vn = (xc * lax.rsqrt(jnp.mean(xc * xc, axis=-1, keepdims=True) + EPS) * lng_ref[...]).astype(BF16)
        for ch in range(tm // CHUNK):
            rows = slice(ch * CHUNK, (ch + 1) * CHUNK)
            for g in range(A_GROUPS):
                cols = slice(g * gd, (g + 1) * gd)
                mixed = _dot(wc_ref[g], vn[rows, cols], NN) + bt_ref[:, g:g + 1]
                out_ref[rows, cols] = (a_ref[rows, cols].astype(F32) * mixed).astype(BF16)

    return _pcall(
        body, name=name, ins=[a, ln_g, wc, b_t], dep=dep, grid=(S // tm,),
        in_specs=[pl.BlockSpec((tm, 2 * AW), lambda i: (i, 0)), pl.BlockSpec((1, AW), lambda i: (0, 0)),
                  pl.BlockSpec((A_GROUPS, CHUNK, CHUNK), lambda i: (0, 0, 0)),
                  pl.BlockSpec((CHUNK, A_GROUPS), lambda i: (0, 0))],
        out_specs=pl.BlockSpec((tm, AW), lambda i: (i, 0)),
        out_shape=jax.ShapeDtypeStruct((S, AW), BF16), compiler_params=_params(1))


def _ffn_fwd(c, name, h, g, col, ci, rows, dep=None):
    S, D, fw, tm = c.S, c.D, c.fw, c.tmw
    F = NDEV * fw

    def body(h_ref, g_ref, w1_ref, w2_ref, p_ref, out_ref, hn_ref, r_ref):
        h = h_ref[...]
        hn = (h * _rstd(h) * g_ref[...]).astype(BF16)
        hn_ref[...] = hn
        for j in range(NDEV):
            cols = slice(j * fw, (j + 1) * fw)
            p = jnp.maximum(_dot(hn, w1_ref[j], NT), 0.0)
            p_ref[:, cols] = p.astype(BF16)
            r_ref[:, cols] = (p * p).astype(BF16)
        out_ref[...] = h + _dot(r_ref[...], w2_ref[...].reshape(F, D), NN)

    row = pl.BlockSpec((tm, D), lambda i: (i, 0))
    return _pcall(
        body, name=name, ins=[h, g, col, rows], dep=dep, grid=(S // tm,),
        in_specs=[row, pl.BlockSpec((1, D), lambda i: (0, 0)),
                  _const((NDEV, None, fw, D), lambda i: (0, ci, 0, 0)), _const((NDEV, c.fr, D), lambda i: (0, 0, 0))],
        out_specs=[pl.BlockSpec((tm, F), lambda i: (i, 0)), row, row],
        out_shape=[jax.ShapeDtypeStruct((S, F), BF16), jax.ShapeDtypeStruct((S, D), F32),
                   jax.ShapeDtypeStruct((S, D), BF16)],
        scratch_shapes=[pltpu.VMEM((tm, F), BF16)], compiler_params=_params(1))


def _bucket_table():
    qi = np.arange(BLOCK)[:, None]
    kj = np.arange(2 * BLOCK)[None, :]
    d = np.maximum(qi + BLOCK - kj, 0)
    max_exact = N_BUCKETS // 2
    ratio = np.log(np.maximum(d, 1).astype(np.float32) / np.float32(max_exact)) / np.float32(
        math.log(MAX_DISTANCE / max_exact))
    large = np.minimum(max_exact + (ratio.astype(np.float32) * np.float32(N_BUCKETS - max_exact)).astype(np.int32),
                       N_BUCKETS - 1)
    return np.where(d < max_exact, d, large).astype(np.int32)


def _bucket_onehot():
    b = jnp.asarray(_bucket_table().reshape(1, -1))
    return (b == lax.broadcasted_iota(jnp.int32, (N_BUCKETS, b.shape[1]), 0)).astype(F32)


def _whole(t):
    return pl.BlockSpec(t.shape, lambda: (0,) * t.ndim)


def _band_bias(rel_bias_t, onehot, dep=None):
    def body(r_ref, oh_ref, out_ref):
        out_ref[...] = lax.dot_general(r_ref[...], oh_ref[...], (NN, ((), ())), preferred_element_type=F32,
                                       precision=lax.Precision.HIGHEST)

    n = onehot.shape[1]
    return _pcall(body, name="band_bias", ins=[rel_bias_t, onehot], in_specs=[_whole(rel_bias_t), _whole(onehot)],
                  dep=dep, out_shape=jax.ShapeDtypeStruct((N_HEADS, n), F32), compiler_params=_params(0))


def _band_bias_grad(dbias_list, onehot, dep=None):
    n_in = len(dbias_list)

    def body(*refs):
        oh_ref, out_ref = refs[n_in], refs[n_in + 1]
        d = refs[0][...]
        for r in refs[1:n_in]:
            d = d + r[...]
        out_ref[...] = lax.dot_general(d, oh_ref[...], (NT, ((), ())), preferred_element_type=F32,
                                       precision=lax.Precision.HIGHEST)

    ins = [*dbias_list, onehot]
    return _pcall(body, name="band_bias_grad", ins=ins, in_specs=[_whole(t) for t in ins], dep=dep,
                  out_shape=jax.ShapeDtypeStruct((N_HEADS, N_BUCKETS), F32), compiler_params=_params(0))


KV_PAIRS = N_KV_HEADS // 2
PAIR_ROWS = 2 * Q_PER_KV * BLOCK
MASKED = float(np.finfo(np.float32).min) / 2


def _slot_cols(w):
    lead = w.shape[:-1]
    return w.reshape(*lead, KV_PAIRS, 2, Q_PER_KV, HEAD_DIM).swapaxes(-3, -2).reshape(*lead, N_HEADS * HEAD_DIM)


def _slot_rows(blocks):
    n = blocks.shape[-1]
    return blocks.reshape(KV_PAIRS, 2, Q_PER_KV, HEAD_DIM, n).swapaxes(1, 2).reshape(blocks.shape)


def _slot_bias(bias):
    qi = np.arange(BLOCK)[:, None]
    kj = np.arange(2 * BLOCK)[None, :]
    dist = qi + BLOCK - kj
    window = (dist >= 0) & (dist < BLOCK)
    b = bias.reshape(KV_PAIRS, 2, Q_PER_KV, BLOCK, 2 * BLOCK).swapaxes(1, 2).reshape(KV_PAIRS, PAIR_ROWS, 2 * BLOCK)
    tile = lambda mk: jnp.asarray(np.tile(mk, (2 * Q_PER_KV, 1)))[None]
    return jnp.stack([jnp.where(tile(window & (kj >= BLOCK)), b, MASKED), jnp.where(tile(window), b, MASKED)])


def _unslot_bias(db):
    return db.reshape(KV_PAIRS, Q_PER_KV, 2, BLOCK, 2 * BLOCK).swapaxes(1, 2).reshape(N_HEADS, -1)


def _pair_kv(kvc_ref, kvp_ref, kvp, dkv):
    lanes = slice(kvp * LANES, (kvp + 1) * LANES)
    vlanes = slice(dkv + kvp * LANES, dkv + (kvp + 1) * LANES)
    k2 = jnp.concatenate([kvp_ref[:, lanes], kvc_ref[:, lanes]], axis=0)
    v2 = jnp.concatenate([kvp_ref[:, vlanes], kvc_ref[:, vlanes]], axis=0)
    return k2, v2


def _head_operand(ref, grp, par, low, scale=None):
    xg = ref[:, grp * LANES:(grp + 1) * LANES]
    if scale is not None:
        xg = xg * scale
    zero = jnp.zeros_like(xg)
    return jnp.where(low, xg, zero) if par == 0 else jnp.where(low, zero, xg)


def _head_probs(qh, k2, bias_rows, sink):
    s = _dot(qh, k2, NT) + bias_rows
    m = jnp.maximum(jnp.max(s, axis=-1, keepdims=True), sink)
    e = jnp.exp(s - m)
    es = jnp.exp(sink - m)
    inv = 1.0 / (jnp.sum(e, axis=-1, keepdims=True) + es)
    return e * inv, es * inv


def _attn_specs(c):
    dq, dkv2 = c.DQ, 2 * c.DKV
    return [pl.BlockSpec((BLOCK, dq), lambda n: (n, 0)),
            pl.BlockSpec((BLOCK, dkv2), lambda n: (n, 0)),
            pl.BlockSpec((BLOCK, dkv2), lambda n: (jnp.maximum(n - 1, 0), 0))]


def _bias_spec():
    return pl.BlockSpec((None, KV_PAIRS, PAIR_ROWS, 2 * BLOCK), lambda n: (jnp.minimum(n, 1), 0, 0, 0))


def _low_lanes():
    return lax.broadcasted_iota(jnp.int32, (BLOCK, LANES), 1) < HEAD_DIM


def _first_key():
    return lax.broadcasted_iota(jnp.int32, (BLOCK, 2 * BLOCK), 1) == 0


def _probs_spec():
    return pl.BlockSpec((None, KV_PAIRS, PAIR_ROWS, 2 * BLOCK), lambda n: (n, 0, 0, 0))


def _attn_fwd(c, name, q, kv, bias, sinks, dep=None):
    S, dq = c.S, c.DQ

    def body(q_ref, kvc_ref, kvp_ref, bias_ref, sink_ref, o_ref, probs_ref):
        low = _low_lanes()
        first = _first_key()
        for kvp in range(KV_PAIRS):
            k2, v2 = _pair_kv(kvc_ref, kvp_ref, kvp, c.DKV)
            for g in range(Q_PER_KV):
                grp = kvp * Q_PER_KV + g
                halves = []
                for par in range(2):
                    rows = slice((2 * g + par) * BLOCK, (2 * g + par + 1) * BLOCK)
                    qh = _head_operand(q_ref, grp, par, low, scale=HEAD_DIM ** -0.5)
                    p, ps = _head_probs(qh, k2, bias_ref[kvp, rows, :], sink_ref[(2 * kvp + par) * Q_PER_KV + g])
                    probs_ref[kvp, rows, :] = jnp.where(first, ps, p).astype(BF16)
                    halves.append(_dot(p.astype(BF16), v2, NN))
                o_ref[:, grp * LANES:(grp + 1) * LANES] = jnp.where(low, halves[0], halves[1]).astype(BF16)

    return _pcall(
        body, name=name, ins=[q, kv, kv, bias, sinks], dep=dep, grid=(c.nb,),
        in_specs=_attn_specs(c) + [_bias_spec(), SMEM],
        out_specs=[pl.BlockSpec((BLOCK, dq), lambda n: (n, 0)), _probs_spec()],
        out_shape=[jax.ShapeDtypeStruct((S, dq), BF16),
                   jax.ShapeDtypeStruct((c.nb, KV_PAIRS, PAIR_ROWS, 2 * BLOCK), BF16)],
        compiler_params=_params(1))


def _final_loss(c, h, g, target, dep=None):
    S, D, tm = c.S, c.D, c.tm

    def body(h_ref, g_ref, t_ref, dh_ref, dg_ref, loss_ref):
        i = pl.program_id(0)
        h = h_ref[...]
        gg = g_ref[...]
        r = _rstd(h)
        xh = h * r
        err = xh * gg - t_ref[...]
        lp = jnp.sum(jnp.sum(err * err, axis=1, keepdims=True), axis=0, keepdims=True) * (0.5 / D)
        dx, dg = _rms_bwd(err * (1.0 / D), h, gg, 0.0)
        dh_ref[...] = dx

        @pl.when(i == 0)
        def _():
            dg_ref[...] = dg
            loss_ref[...] = jnp.broadcast_to(lp, loss_ref.shape)

        @pl.when(i > 0)
        def _():
            dg_ref[...] += dg
            loss_ref[...] += jnp.broadcast_to(lp, loss_ref.shape)

    row = pl.BlockSpec((tm, D), lambda i: (i, 0))
    return _pcall(
        body, name="final_loss", ins=[h, g, target], dep=dep, grid=(S // tm,),
        in_specs=[row, pl.BlockSpec((1, D), lambda i: (0, 0)), row],
        out_specs=[row, pl.BlockSpec((1, D), lambda i: (0, 0)), pl.BlockSpec((1, LANES), lambda i: (0, 0))],
        out_shape=[jax.ShapeDtypeStruct((S, D), F32), jax.ShapeDtypeStruct((1, D), F32),
                   jax.ShapeDtypeStruct((1, LANES), F32)],
        compiler_params=_params(1))


def _rms_bwd_epilogue(h_idx, g_idx, res_idx):
    def epilogue(dhn, in_refs, out_refs):
        dh, dg = _rms_bwd(dhn, in_refs[h_idx][...], in_refs[g_idx][...], in_refs[res_idx][...])
        out_refs[0][...] = dh
        i = pl.program_id(0)

        @pl.when(i == 0)
        def _():
            out_refs[1][...] = dg

        @pl.when(i > 0)
        def _():
            out_refs[1][...] += dg
    return epilogue


def _stream_outs(c, tm):
    S, D = c.S, c.D
    return ([jax.ShapeDtypeStruct((S, D), F32), jax.ShapeDtypeStruct((1, D), F32)],
            [pl.BlockSpec((tm, D), lambda i, j, k: (i, 0)), pl.BlockSpec((1, D), lambda i, j, k: (0, 0))])


def _row_specs(c, tm):
    D = c.D
    return [pl.BlockSpec((tm, D), lambda i, j, k: (i, 0)), pl.BlockSpec((1, D), lambda i, j, k: (0, 0)),
            pl.BlockSpec((tm, D), lambda i, j, k: (i, 0))]


def _bwd_rows_to_stream(c, name, dy_list, slab, blk_rows, blk_idx, n_in_cols, h, g, dres, dep=None):
    S, D, tm = c.S, c.D, c.tm
    nd = len(dy_list)

    def load_a(in_refs, out_refs):
        a = in_refs[0][...]
        for r in in_refs[1:nd]:
            a = a + r[...]
        return a.astype(BF16)

    shapes, specs = _stream_outs(c, tm)
    return _mm(
        name, [*dy_list, slab, h, g, dres],
        [pl.BlockSpec((tm, n_in_cols), lambda i, j, k: (i, 0))] * nd
        + [pl.BlockSpec((NDEV, blk_rows, n_in_cols), lambda i, j, k: (0, blk_idx, 0))] + _row_specs(c, tm),
        shapes, specs, grid=(S // tm, 1, 1), dims=NT, nk=1, acc_shape=None,
        load_a=load_a, load_b=_b_view(nd, NDEV * blk_rows), epilogue=_rms_bwd_epilogue(nd + 1, nd + 2, nd + 3),
        dep=dep)


def _bwd_cols_to_stream(c, name, dy, col, ci, h, g, dres, dep=None):
    S, D, cw, tm = c.S, c.D, c.cw, c.tmw
    K = NDEV * cw
    shapes, specs = _stream_outs(c, tm)
    return _mm(
        name, [dy, col, h, g, dres],
        [pl.BlockSpec((tm, K), lambda i, j, k: (i, 0)),
         _const((NDEV, None, cw, D), lambda i, j, k: (0, ci, 0, 0))] + _row_specs(c, tm),
        shapes, specs, grid=(S // tm, 1, 1), dims=NN, nk=1, acc_shape=None,
        load_a=_bf(0), load_b=_b_view(1, K), epilogue=_rms_bwd_epilogue(2, 3, 4), dep=dep)


def _bwd_rows_data(c, name, dy, slab, blk_rows, blk_idx, dep=None):
    S, D, tm = c.S, c.D, c.tm
    K = NDEV * blk_rows

    def epilogue(acc, in_refs, out_refs):
        out_refs[0][...] = acc.astype(BF16)

    return _mm(
        name, [dy, slab],
        [pl.BlockSpec((tm, D), lambda i, j, k: (i, 0)),
         pl.BlockSpec((NDEV, blk_rows, D), lambda i, j, k: (0, blk_idx, 0))],
        [jax.ShapeDtypeStruct((S, K), BF16)], [pl.BlockSpec((tm, K), lambda i, j, k: (i, 0))],
        grid=(S // tm, 1, 1), dims=NT, nk=1, acc_shape=None,
        load_a=_bf(0), load_b=_b_view(1, K), epilogue=epilogue, dep=dep)[0]


def _wgrad_rows(c, name, a, b_list, n_a, n_b, unslot=None, dep=None):
    S, tm = c.S, c.tm
    nb_in = len(b_list)
    blk_rows = n_a // NDEV

    def load_b(in_refs):
        b = in_refs[1][...]
        for r in in_refs[2:1 + nb_in]:
            b = b + r[...]
        return b.astype(BF16)

    def epilogue(acc, in_refs, out_refs):
        out = out_refs[0]
        if unslot is None:
            out[...] = acc.reshape(NDEV, blk_rows, n_b).astype(BF16)
            return
        per_blk = blk_rows // HEAD_DIM
        for h in range(N_HEADS):
            s = (h // (2 * Q_PER_KV)) * 2 * Q_PER_KV + (h % Q_PER_KV) * 2 + (h // Q_PER_KV) % 2
            if unslot == "rows":
                out[h // per_blk, (h % per_blk) * HEAD_DIM:(h % per_blk + 1) * HEAD_DIM, :] = (
                    acc[s * HEAD_DIM:(s + 1) * HEAD_DIM, :].astype(BF16))
            else:
                out[:, :, h * HEAD_DIM:(h + 1) * HEAD_DIM] = (
                    acc[:, s * HEAD_DIM:(s + 1) * HEAD_DIM].reshape(NDEV, blk_rows, HEAD_DIM).astype(BF16))

    return _mm(
        name, [a, *b_list],
        [pl.BlockSpec((tm, n_a), lambda i, j, k: (k, 0))] + [pl.BlockSpec((tm, n_b), lambda i, j, k: (k, 0))] * nb_in,
        [jax.ShapeDtypeStruct((NDEV, blk_rows, n_b), BF16)],
        [pl.BlockSpec((NDEV, blk_rows, n_b), lambda i, j, k: (0, 0, 0))],
        grid=(1, 1, S // tm), dims=TN, nk=S // tm, acc_shape=(n_a, n_b),
        load_a=_bf(0), load_b=load_b, epilogue=epilogue, dep=dep)[0]


def _wgrad_cols(c, name, a, b, dep=None):
    S, D, cw = c.S, c.D, c.cw

    def body(a_ref, b_ref, out_ref):
        out_ref[...] = _dot(a_ref[...], b_ref[...], TN).astype(BF16)

    return _pcall(
        body, name=name, ins=[a, b], dep=dep, grid=(NDEV,),
        in_specs=[_const((S, D), lambda j: (0, 0)), pl.BlockSpec((S, cw), lambda j: (0, j))],
        out_specs=pl.BlockSpec((None, D, cw), lambda j: (j, 0, 0)),
        out_shape=jax.ShapeDtypeStruct((NDEV, D, cw), BF16), compiler_params=_params(1))


def _ffn_bwd_data(c, name, dh, p, col, ci, rows, h, g, dep=None):
    S, D, fw, tm = c.S, c.D, c.fw, c.tmw
    F = NDEV * fw

    def body(dh_ref, p_ref, w1t_ref, w2_ref, h_ref, g_ref, da_ref, out_ref, dg_ref, dhb_ref):
        i = pl.program_id(0)
        dh = dh_ref[...]
        dhb = dh.astype(BF16)
        dhb_ref[...] = dhb
        for j in range(NDEV):
            cols = slice(j * fw, (j + 1) * fw)
            da_ref[:, cols] = (_dot(dhb, w2_ref[j], NT) * (2.0 * p_ref[:, cols].astype(F32))).astype(BF16)
        dx, dg = _rms_bwd(_dot(da_ref[...], w1t_ref[...].reshape(F, D), NN), h_ref[...], g_ref[...], dh)
        out_ref[...] = dx

        @pl.when(i == 0)
        def _():
            dg_ref[...] = dg

        @pl.when(i > 0)
        def _():
            dg_ref[...] += dg

    row = pl.BlockSpec((tm, D), lambda i: (i, 0))
    wide = pl.BlockSpec((tm, F), lambda i: (i, 0))
    return _pcall(
        body, name=name, ins=[dh, p, col, rows, h, g], dep=dep, grid=(S // tm,),
        in_specs=[row, wide, _const((NDEV, None, fw, D), lambda i: (0, ci, 0, 0)),
                  _const((NDEV, c.fr, D), lambda i: (0, 0, 0)),
                  row, pl.BlockSpec((1, D), lambda i: (0, 0))],
        out_specs=[wide, row, pl.BlockSpec((1, D), lambda i: (0, 0)), row],
        out_shape=[jax.ShapeDtypeStruct((S, F), BF16), jax.ShapeDtypeStruct((S, D), F32),
                   jax.ShapeDtypeStruct((1, D), F32), jax.ShapeDtypeStruct((S, D), BF16)],
        compiler_params=_params(1))


def _ffn_bwd_w(c, name, hn, da, p, dhb, dep=None):
    S, D, fw = c.S, c.D, c.fw

    def body(hn_ref, da_ref, p_ref, dhb_ref, dw1_ref, dw2_ref):
        dw1_ref[...] = _dot(hn_ref[...], da_ref[...], TN).astype(BF16)
        pf = p_ref[...].astype(F32)
        dw2_ref[...] = _dot((pf * pf).astype(BF16), dhb_ref[...], TN).astype(BF16)

    panel = pl.BlockSpec((S, fw), lambda j: (0, j))
    return _pcall(
        body, name=name, ins=[hn, da, p, dhb], dep=dep, grid=(NDEV,),
        in_specs=[_const((S, D), lambda j: (0, 0)), panel, panel, _const((S, D), lambda j: (0, 0))],
        out_specs=[pl.BlockSpec((None, D, fw), lambda j: (j, 0, 0)), pl.BlockSpec((None, c.fr, D), lambda j: (j, 0, 0))],
        out_shape=[jax.ShapeDtypeStruct((NDEV, D, fw), BF16), jax.ShapeDtypeStruct((NDEV, c.fr, D), BF16)],
        compiler_params=_params(1))


def _attn_bwd(c, name, q, kv, do, probs, dep=None):
    S, dq, dkv = c.S, c.DQ, c.DKV
    nb = c.nb
    scale = HEAD_DIM ** -0.5

    def body(q_ref, kvc_ref, kvp_ref, do_ref, probs_ref, dq_ref, dkv_ref, dbias_ref, dsink_ref, dsink_acc,
             ds_sc, p_sc, qm_sc, dom_sc):
        n = pl.program_id(0)

        @pl.when(n == 0)
        def _():
            dkv_ref[...] = jnp.zeros_like(dkv_ref)
            dbias_ref[...] = jnp.zeros_like(dbias_ref)
            dsink_acc[...] = jnp.zeros_like(dsink_acc)

        low = _low_lanes()
        first = _first_key()
        rows_c = pl.ds(pl.multiple_of(n * BLOCK, BLOCK), BLOCK)
        rows_p = pl.ds(pl.multiple_of(jnp.maximum(n - 1, 0) * BLOCK, BLOCK), BLOCK)
        for kvp in range(KV_PAIRS):
            k2, v2 = _pair_kv(kvc_ref, kvp_ref, kvp, dkv)
            for g in range(Q_PER_KV):
                grp = kvp * Q_PER_KV + g
                halves = []
                for par in range(2):
                    rows = slice((2 * g + par) * BLOCK, (2 * g + par + 1) * BLOCK)
                    qh = _head_operand(q_ref, grp, par, low, scale=scale)
                    doh = _head_operand(do_ref, grp, par, low)
                    saved = probs_ref[kvp, rows, :]
                    ps = saved[:, 0:1].astype(F32)
                    p16 = jnp.where(first, jnp.zeros_like(saved), saved)
                    p = p16.astype(F32)
                    dp = _dot(doh, v2, NT)
                    delta = jnp.sum(p * dp, axis=-1, keepdims=True)
                    ds = p * (dp - delta)
                    dbias_ref[kvp, rows, :] += ds
                    dsink_acc[rows, kvp:kvp + 1] += -(ps * delta)
                    ds16 = ds.astype(BF16)
                    halves.append(_dot(ds16, k2, NN) * scale)
                    ds_sc[rows, :] = ds16
                    p_sc[rows, :] = p16
                    qm_sc[rows, :] = qh
                    dom_sc[rows, :] = doh
                dq_ref[:, grp * LANES:(grp + 1) * LANES] = jnp.where(low, halves[0], halves[1]).astype(BF16)
            dk2 = _dot(ds_sc[...], qm_sc[...], TN)
            dv2 = _dot(p_sc[...], dom_sc[...], TN)
            lanes = slice(kvp * LANES, (kvp + 1) * LANES)
            vlanes = slice(dkv + kvp * LANES, dkv + (kvp + 1) * LANES)
            dkv_ref[rows_p, lanes] += dk2[:BLOCK]
            dkv_ref[rows_c, lanes] += dk2[BLOCK:]
            dkv_ref[rows_p, vlanes] += dv2[:BLOCK]
            dkv_ref[rows_c, vlanes] += dv2[BLOCK:]

        @pl.when(n == nb - 1)
        def _():
            dsink_ref[...] = jnp.sum(dsink_acc[...].reshape(2 * Q_PER_KV, BLOCK, KV_PAIRS), axis=1)

    return _pcall(
        body, name=name, ins=[q, kv, kv, do, probs], dep=dep, grid=(nb,),
        in_specs=_attn_specs(c) + [pl.BlockSpec((BLOCK, dq), lambda n: (n, 0)), _probs_spec()],
        out_specs=[pl.BlockSpec((BLOCK, dq), lambda n: (n, 0)), pl.BlockSpec((S, 2 * dkv), lambda n: (0, 0)),
                   pl.BlockSpec((KV_PAIRS, PAIR_ROWS, 2 * BLOCK), lambda n: (0, 0, 0)),
                   pl.BlockSpec((2 * Q_PER_KV, KV_PAIRS), lambda n: (0, 0))],
        out_shape=[jax.ShapeDtypeStruct((S, dq), BF16), jax.ShapeDtypeStruct((S, 2 * dkv), F32),
                   jax.ShapeDtypeStruct((KV_PAIRS, PAIR_ROWS, 2 * BLOCK), F32),
                   jax.ShapeDtypeStruct((2 * Q_PER_KV, KV_PAIRS), F32)],
        scratch_shapes=[pltpu.VMEM((PAIR_ROWS, KV_PAIRS), F32), pltpu.VMEM((PAIR_ROWS, 2 * BLOCK), BF16),
                        pltpu.VMEM((PAIR_ROWS, 2 * BLOCK), BF16), pltpu.VMEM((PAIR_ROWS, LANES), BF16),
                        pltpu.VMEM((PAIR_ROWS, LANES), BF16)],
        compiler_params=_params(1))


def _sgu_bwd(c, name, a, z, dgated, ln_g, wc, wc_t, b_t, dep=None):
    S, AW, gd, tm = c.S, c.AW, c.gd, c.tms

    def body(a_ref, z_ref, dg_ref, lng_ref, wc_ref, wct_ref, bt_ref, dz_ref, dws_ref, dbt_ref, dlng_ref, dvn_ref):
        i = pl.program_id(0)

        @pl.when(i == 0)
        def _():
            dws_ref[...] = jnp.zeros_like(dws_ref)
            dbt_ref[...] = jnp.zeros_like(dbt_ref)
            dlng_ref[...] = jnp.zeros_like(dlng_ref)

        lng = lng_ref[...]
        va = a_ref[:, AW:].astype(F32)
        xc = va - jnp.mean(va, axis=-1, keepdims=True)
        rstd = lax.rsqrt(jnp.mean(xc * xc, axis=-1, keepdims=True) + EPS)
        xh = xc * rstd
        vn = (xh * lng).astype(BF16)
        causal = _sgu_masks()
        for ch in range(tm // CHUNK):
            rows = slice(ch * CHUNK, (ch + 1) * CHUNK)
            for g in range(A_GROUPS):
                cols = slice(g * gd, (g + 1) * gd)
                blk = vn[rows, cols]
                mixed = _dot(wc_ref[g], blk, NN) + bt_ref[:, g:g + 1]
                dgb = dg_ref[rows, cols].astype(F32)
                dm = dgb * a_ref[rows, cols].astype(F32)
                dbt_ref[:, g:g + 1] += jnp.sum(dm, axis=1, keepdims=True)
                dm16 = dm.astype(BF16)
                dws_ref[g] += jnp.where(causal, _dot(dm16, blk, NT), 0.0)
                dvn_ref[rows, cols] = _dot(wct_ref[g], dm16, NN)
                dz_ref[rows, cols] = (dgb * mixed * _gelu_grad(z_ref[rows, cols].astype(F32))).astype(BF16)
        dvn = dvn_ref[...]
        dlng_ref[...] += jnp.sum(dvn * xh, axis=0, keepdims=True)
        dxh = dvn * lng
        dva = rstd * (dxh - jnp.mean(dxh, axis=-1, keepdims=True) - xh * jnp.mean(dxh * xh, axis=-1, keepdims=True))
        dz_ref[:, AW:] = (dva * _gelu_grad(z_ref[:, AW:].astype(F32))).astype(BF16)

    wide = pl.BlockSpec((tm, 2 * AW), lambda i: (i, 0))
    wsp = pl.BlockSpec((A_GROUPS, CHUNK, CHUNK), lambda i: (0, 0, 0))
    btsp = pl.BlockSpec((CHUNK, A_GROUPS), lambda i: (0, 0))
    return _pcall(
        body, name=name, ins=[a, z, dgated, ln_g, wc, wc_t, b_t], dep=dep, grid=(S // tm,),
        in_specs=[wide, wide, pl.BlockSpec((tm, AW), lambda i: (i, 0)), pl.BlockSpec((1, AW), lambda i: (0, 0)),
                  wsp, wsp, btsp],
        out_specs=[wide, wsp, btsp, pl.BlockSpec((1, AW), lambda i: (0, 0))],
        out_shape=[jax.ShapeDtypeStruct((S, 2 * AW), BF16), jax.ShapeDtypeStruct((A_GROUPS, CHUNK, CHUNK), F32),
                   jax.ShapeDtypeStruct((CHUNK, A_GROUPS), F32), jax.ShapeDtypeStruct((1, AW), F32)],
        scratch_shapes=[pltpu.VMEM((tm, AW), F32)], compiler_params=_params(1))


def _adam_update(g, w_ref, m_ref, v_ref, out_refs):
    g_ref, d_ref, nm_ref, nv_ref = out_refs
    bc1 = 1.0 - ADAM_B1 ** ADAM_STEP
    bc2 = 1.0 - ADAM_B2 ** ADAM_STEP
    nm = ADAM_B1 * m_ref[...] + (1.0 - ADAM_B1) * g
    nv = ADAM_B2 * v_ref[...] + (1.0 - ADAM_B2) * (g * g)
    g_ref[...] = g
    nm_ref[...] = nm
    nv_ref[...] = nv
    d_ref[...] = -ADAM_LR * ((nm * (1.0 / bc1)) / (jnp.sqrt(nv * (1.0 / bc2)) + ADAM_EPS) + ADAM_WD * w_ref[...])


def _adamw_packed(name, sums, offs, ws, ms, vs, dep=None):
    n = len(ws)

    def body(*refs):
        s_ref, w_refs, m_refs, v_refs = refs[0], refs[1:1 + n], refs[1 + n:1 + 2 * n], refs[1 + 2 * n:1 + 3 * n]
        outs = refs[1 + 3 * n:]
        for k in range(n):
            _adam_update(s_ref[offs[k]:offs[k + 1], :], w_refs[k], m_refs[k], v_refs[k], outs[4 * k:4 * k + 4])

    ins = [sums, *ws, *ms, *vs]
    shapes = [jax.ShapeDtypeStruct(w.shape, F32) for w in ws for _ in range(4)]
    return _pcall(body, name=name, ins=ins, in_specs=[_whole(t) for t in ins], dep=dep,
                  out_shape=shapes, out_specs=[_whole(t) for t in shapes], compiler_params=_params(0))


def _adamw(name, parts, part_block, part_index, w, m, v, tr, row_off=0, n_rows=None, prev=None, dep=None):
    R, C = w.shape
    n_rows = R if n_rows is None else n_rows
    assert n_rows % tr == 0 and row_off % tr == 0

    def body(p_ref, w_ref, m_ref, v_ref, *rest):
        g = p_ref[0].astype(F32)
        for s in range(1, part_block[0]):
            g = g + p_ref[s].astype(F32)
        _adam_update(g, w_ref, m_ref, v_ref, rest[-4:])

    ob = row_off // tr
    row = pl.BlockSpec((tr, C), lambda i: (ob + i, 0))
    out = jax.ShapeDtypeStruct((R, C), F32)
    chained = prev is not None
    return _pcall(
        body, name=name, ins=[parts, w, m, v] + (list(prev) if chained else []), dep=dep, grid=(n_rows // tr,),
        in_specs=[pl.BlockSpec(part_block, part_index), row, row, row] + ([ANY] * 4 if chained else []),
        out_specs=[row, row, row, row], out_shape=[out, out, out, out],
        input_output_aliases={4 + t: t for t in range(4)} if chained else {}, compiler_params=_params(1))


def _sum_parts(name, parts, dep=None):
    def body(p_ref, out_ref):
        g = p_ref[0]
        for s in range(1, parts.shape[0]):
            g = g + p_ref[s]
        out_ref[...] = g

    return _pcall(body, name=name, ins=[parts], in_specs=[_whole(parts)], dep=dep,
                  out_shape=jax.ShapeDtypeStruct(parts.shape[1:], F32), compiler_params=_params(0))


def _place():
    return lax.axis_index("x"), lax.axis_index("y"), lax.axis_index("c")


def _slot(px, py, pc):
    return 4 * px + 2 * py + pc


def _peer(k, x, y, c):
    return x ^ ((k >> 2) & 1), y ^ ((k >> 1) & 1), c ^ (k & 1)


SEND_PEERS = {"exchange": tuple(range(1, NDEV)), "gather": (1, 2, 4, 6), "forward": (2, 4, 6),
              "broadcast": tuple(range(1, NDEV))}


def _n_sems(mode, n_lands):
    return n_lands * (len(SEND_PEERS[mode]) + (mode != "forward"))


def _send_copies(mode, src_refs, land_refs, send_sems, recv_sems):
    x, y, c = _place()
    me = _slot(x, y, c)
    peers = SEND_PEERS[mode]
    remote, local = [], []
    for i, k in enumerate(peers):
        peer = _peer(k, x, y, c)
        for a, land in enumerate(land_refs):
            if mode == "exchange":
                src, dst, to = src_refs[a].at[_slot(*peer)], land.at[me], peer
            elif mode in ("gather", "broadcast"):
                src, dst, to = src_refs[a], land.at[me], peer
            else:
                src = dst = land.at[_slot(*peer)]
                to = (x, y, 1 - c)
            s = a * len(peers) + i
            remote.append(pltpu.make_async_remote_copy(src_ref=src, dst_ref=dst, send_sem=send_sems.at[s],
                                                       recv_sem=recv_sems.at[s], device_id=to, device_id_type=MESH))
    if mode != "forward":
        for a, land in enumerate(land_refs):
            src = src_refs[a].at[me] if mode == "exchange" else src_refs[a]
            local.append(pltpu.make_async_copy(src, land.at[me], send_sems.at[len(land_refs) * len(peers) + a]))
    return remote, local


def _send_start_groups(name, groups, mode, collective_id=None, dep=None):
    sizes = [(len(s), len(l)) for s, l in groups]
    flat = [t for s, l in groups for t in (*s, *l)]
    n_in, ng = len(flat), len(groups)

    def body(*refs):
        sems, token, at = refs[n_in:n_in + 2 * ng], refs[-1], 0
        if collective_id is not None:
            x, y, c = _place()
            barrier = pltpu.get_barrier_semaphore()
            shake = (1,) if mode == "forward" else SEND_PEERS[mode]
            for k in shake:
                pl.semaphore_signal(barrier, inc=1, device_id=_peer(k, x, y, c), device_id_type=MESH)
            pl.semaphore_wait(barrier, len(shake))
        for gi, (ns, nl) in enumerate(sizes):
            remote, local = _send_copies(mode, refs[at:at + ns], refs[at + ns:at + ns + nl], sems[2 * gi],
                                         sems[2 * gi + 1])
            for cp in remote + local:
                cp.start()
            at += ns + nl
        token[...] = jnp.zeros_like(token)

    sem_shapes = [pltpu.SemaphoreType.DMA((_n_sems(mode, nl),)) for _, nl in sizes for _ in range(2)]
    if any(dep is t for t in flat):
        dep = None
    out = _pcall(
        body, name=name, ins=[pltpu.with_memory_space_constraint(t, pltpu.HBM) for t in flat],
        in_specs=[HBM] * n_in, dep=dep,
        out_shape=(*sem_shapes, *[pltpu.HBM(t.shape, t.dtype) for t in flat], jax.ShapeDtypeStruct((8, LANES), F32)),
        out_specs=(*[SEM] * (2 * ng), *[HBM] * n_in, pl.BlockSpec(memory_space=pltpu.VMEM)),
        input_output_aliases={i: 2 * ng + i for i in range(n_in)},
        compiler_params=pltpu.CompilerParams(has_side_effects=pltpu.SideEffectType.DATAFLOW_SIDE_EFFECTING,
                                             collective_id=collective_id))
    started, at = [], 2 * ng
    for gi, (ns, nl) in enumerate(sizes):
        started.append((out[-1], out[2 * gi], out[2 * gi + 1], list(out[at:at + ns]), list(out[at + ns:at + ns + nl])))
        at += ns + nl
    return started


def _send_start(name, srcs, lands, mode, collective_id=None, dep=None):
    if mode == "forward":
        collective_id = SIBLING_COLLECTIVE_ID
    return _send_start_groups(name, [(srcs, lands)], mode, collective_id=collective_id, dep=dep)[0]


def _send_wait(name, started, mode, dep=None):
    _, send_sems, recv_sems, srcs, lands = started
    n_src, n = len(srcs), len(lands)

    def body(*refs):
        src_refs, land_refs = refs[:n_src], refs[n_src:n_src + n]
        ssem, rsem = refs[n_src + n], refs[n_src + n + 1]
        remote, local = _send_copies(mode, src_refs, land_refs, ssem, rsem)
        for cp in remote:
            cp.wait_send()
            cp.wait_recv()
        for cp in local:
            cp.wait()

    thru = [pltpu.HBM(t.shape, t.dtype) for t in [*srcs, *lands]]
    out = _pcall(
        body, name=name, ins=[*srcs, *lands, send_sems, recv_sems], in_specs=[HBM] * (n_src + n) + [SEM, SEM], dep=dep,
        out_shape=tuple(thru), out_specs=tuple([HBM] * (n_src + n)),
        input_output_aliases={i: i for i in range(n_src + n)},
        compiler_params=pltpu.CompilerParams(has_side_effects=pltpu.SideEffectType.DATAFLOW_SIDE_EFFECTING))
    return list(out[n_src:])


def _landing(block):
    return lax.empty((NDEV, *block.shape), block.dtype)


def _rows128(t):
    flat = t.reshape(-1)
    n = flat.shape[0]
    rows = -(-n // (8 * LANES)) * 8
    return jnp.pad(flat, (0, rows * LANES - n)).reshape(rows, LANES)


def kernel(x, mix_norm_g, ffn_norm_g, a_w_in, a_ln_g, a_w_spatial, a_b_spatial, a_w_out, kv_norm_g, w_k, w_v, b_w_q, b_sinks, b_w_o, rel_bias, ffn_w1, ffn_w2, final_norm_g, loss_target, m_mix_norm_g, m_ffn_norm_g, m_a_w_in, m_a_ln_g, m_a_w_spatial, m_a_b_spatial, m_a_w_out, m_kv_norm_g, m_w_k, m_w_v, m_b_w_q, m_b_sinks, m_b_w_o, m_rel_bias, m_ffn_w1, m_ffn_w2, m_final_norm_g, v_mix_norm_g, v_ffn_norm_g, v_a_w_in, v_a_ln_g, v_a_w_spatial, v_a_b_spatial, v_a_w_out, v_kv_norm_g, v_w_k, v_w_v, v_b_w_q, v_b_sinks, v_b_w_o, v_rel_bias, v_ffn_w1, v_ffn_w2, v_final_norm_g):
    c = _config(x, a_w_in, a_w_out, w_k, b_w_q, b_w_o, ffn_w1, ffn_w2)
    S, D, LA, LB, LF = c.S, c.D, c.LA, c.LB, c.LF
    weights = dict(mix_norm_g=mix_norm_g, ffn_norm_g=ffn_norm_g, a_w_in=a_w_in, a_ln_g=a_ln_g, a_w_spatial=a_w_spatial,
                   a_b_spatial=a_b_spatial, a_w_out=a_w_out, kv_norm_g=kv_norm_g, w_k=w_k, w_v=w_v, b_w_q=b_w_q,
                   b_sinks=b_sinks, b_w_o=b_w_o, rel_bias=rel_bias, ffn_w1=ffn_w1, ffn_w2=ffn_w2,
                   final_norm_g=final_norm_g)
    m_in = dict(mix_norm_g=m_mix_norm_g, ffn_norm_g=m_ffn_norm_g, a_w_in=m_a_w_in, a_ln_g=m_a_ln_g,
                a_w_spatial=m_a_w_spatial, a_b_spatial=m_a_b_spatial, a_w_out=m_a_w_out, kv_norm_g=m_kv_norm_g,
                w_k=m_w_k, w_v=m_w_v, b_w_q=m_b_w_q, b_sinks=m_b_sinks, b_w_o=m_b_w_o, rel_bias=m_rel_bias,
                ffn_w1=m_ffn_w1, ffn_w2=m_ffn_w2, final_norm_g=m_final_norm_g)
    v_in = dict(mix_norm_g=v_mix_norm_g, ffn_norm_g=v_ffn_norm_g, a_w_in=v_a_w_in, a_ln_g=v_a_ln_g,
                a_w_spatial=v_a_w_spatial, a_b_spatial=v_a_b_spatial, a_w_out=v_a_w_out, kv_norm_g=v_kv_norm_g,
                w_k=v_w_k, w_v=v_w_v, b_w_q=v_b_w_q, b_sinks=v_b_sinks, b_w_o=v_b_w_o, rel_bias=v_rel_bias,
                ffn_w1=v_ffn_w1, ffn_w2=v_ffn_w2, final_norm_g=v_final_norm_g)
    names = list(weights)
    seq = _Seq()
    me = _slot(*_place())
    bf = lambda t: t.astype(BF16)

    tr = lambda t: bf(jnp.swapaxes(t, -1, -2))

    def start(tag, some):
        got = seq(_send_start_groups, f"weights_start_{tag}", [(grp, [_landing(t) for t in grp]) for grp in some],
                  "gather")
        seq.last = got[0][0]
        return got

    started = start("first", [[tr(a_w_in[0])[None], a_ln_g]])
    _, a_w_in_, a_w_out_, ffn_w1_, ffn_w2_, w_k_, w_v_, b_w_q_, b_w_o_ = lax.optimization_barrier(
        (started[0][0], a_w_in, a_w_out, ffn_w1, ffn_w2, w_k, w_v, b_w_q, b_w_o))
    groups = []
    for l in range(LA):
        groups += [[tr(a_w_in_[l])[None]], [bf(a_w_out_[l])], [tr(ffn_w1_[l])[None], bf(ffn_w2_[l])]]
    gb = 3 * LA
    for l in range(LB):
        extra = [bf(jnp.concatenate([w_k_, w_v_], axis=1))] if l == 0 else []
        groups += [extra + [bf(b_w_q_[l]), bf(b_w_o_[l])], [tr(ffn_w1_[LA + l])[None], bf(ffn_w2_[LA + l])]]
    started += start("rest", groups[1:])
    forwarding = {}

    def forward(i):
        lands = seq(_send_wait, f"weights_wait{i}", started[i], "gather")
        forwarding[i] = seq(_send_start, f"weights_forward{i}", [], lands, "forward")

    def arrive(i):
        if i not in forwarding:
            forward(i)
        return seq(_send_wait, f"weights_arrive{i}", forwarding[i], "forward")

    causal = jnp.tril(jnp.ones((CHUNK, CHUNK), bool))
    wsp = jnp.where(causal[None, None], a_w_spatial, 0.0)
    wsp16 = wsp.astype(BF16)
    wsp16_t = jnp.swapaxes(wsp, -1, -2).astype(BF16)
    bsp_t = jnp.swapaxes(a_b_spatial, -1, -2)
    mix_g = mix_norm_g.reshape(-1, 1, D)
    ffn_g = ffn_norm_g.reshape(-1, 1, D)
    kv_g = kv_norm_g.reshape(1, D)
    fin_g = final_norm_g.reshape(1, D)
    onehot = _bucket_onehot()
    bias = _slot_bias(seq(_band_bias, rel_bias.T, onehot).reshape(N_HEADS, BLOCK, 2 * BLOCK))

    h = x.reshape(S, D)
    sav_a, sav_b, wts_a, wts_b = [], [], [], []
    for l in range(LA):
        got = arrive(3 * l)
        w_in = got[0]
        if l == 0:
            ln_g_full = jnp.transpose(got[1], (1, 0, 2)).reshape(LA, 1, c.AW)
        z, a, hn = seq(_a_in_fwd, c, f"a_in_fwd{l}", h, mix_g[l], w_in, 0)
        forward(3 * l + 1)
        gated = seq(_sgu_fwd, c, f"sgu_fwd{l}", a, ln_g_full[l], wsp16[l], bsp_t[l])
        (wout,) = arrive(3 * l + 1)
        if l > 0:
            forward(3 * l + 2)
        h1 = seq(_mm_res, c, f"a_out_fwd{l}", gated, wout, c.ar, 0, h)
        w1, rows = arrive(3 * l + 2)
        if l == LA - 1:
            forward(gb)
        p, h2, hnf = seq(_ffn_fwd, c, f"ffn_fwd{l}", h1, ffn_g[l], w1, 0, rows)
        sav_a.append((h, z, a, hn, gated, h1, p, hnf))
        wts_a.append((w_in, 0, w1, 0, rows, wout, 0))
        h = h2
    h_kv = h
    for l in range(LB):
        got = arrive(gb + 2 * l)
        if l == 0:
            wkv, got = got[0], got[1:]
            kv, hkv = seq(_rms_mm_rows, c, "kv_fwd", h, kv_g, wkv, c.kr, 0, 2 * c.DKV)
        wq, wo = _slot_cols(got[0]), _slot_rows(got[1])
        q, hn = seq(_rms_mm_rows, c, f"q_fwd{l}", h, mix_g[LA + l], wq, c.qr, 0, c.DQ)
        forward(gb + 2 * l + 1)
        o, probs = seq(_attn_fwd, c, f"attn_fwd{l}", q, kv, bias, b_sinks[l])
        h1 = seq(_mm_res, c, f"o_fwd{l}", o, wo, c.orr, 0, h)
        w1, rows = arrive(gb + 2 * l + 1)
        if l + 1 < LB:
            forward(gb + 2 * l + 2)
        p, h2, hnf = seq(_ffn_fwd, c, f"ffn_fwd{LA + l}", h1, ffn_g[LA + l], w1, 0, rows)
        sav_b.append((h, q, hn, o, probs, h1, p, hnf))
        wts_b.append((wq, wo, w1, rows))
        h = h2
    dh, d_fin_g, loss_row = seq(_final_loss, c, h, fin_g, loss_target.reshape(S, D))

    results = {}
    in_flight, exchanges = [], []

    def update(k, parts, layer, col_blk=0):
        w = weights[k]
        rows_l, ncols = (w.shape[-2], w.shape[-1]) if w.ndim == 3 else w.shape
        flat = lambda t: t.reshape(-1, ncols)
        tr = min(256, rows_l)
        results[k] = seq(_adamw, f"adamw_{k}{layer}", parts, (NDEV, tr, ncols), lambda i: (0, i, col_blk),
                         flat(w), flat(m_in[k]), flat(v_in[k]), tr, row_off=layer * rows_l, n_rows=rows_l,
                         prev=results.get(k))

    def land(tag, entry):
        lands = seq(_send_wait, f"grads_wait_{tag}", entry[1], "exchange")
        for keys, parts in zip(entry[0], lands):
            for k, layer, col_blk in keys:
                update(k, parts, layer, col_blk)

    def send(tag, items):
        slabs = [t for _, t in items]
        own = [lax.empty(t.shape, t.dtype) for t in slabs]
        exchanges.append(tag)
        st = seq(_send_start, f"grads_start_{tag}", slabs, own, "exchange",
                 collective_id=SIBLING_COLLECTIVE_ID + len(exchanges))
        in_flight.append((tag, ([k for k, _ in items], st)))
        while len(in_flight) > EXCHANGE_LAG:
            land(*in_flight.pop(0))

    d_mix_g, d_ffn_g = [None] * LF, [None] * LF
    dkv_list, dbias_list, dsink_list = [], [], [None] * LB

    def ffn_bwd(lf, dh, h1, p, hnf, w1, w1_i, rows):
        da, dh1, d_ffn_g[lf], dhb = seq(_ffn_bwd_data, c, f"ffn_bwd_data{lf}", dh, p, w1, w1_i, rows, h1, ffn_g[lf])
        dw1, dw2 = seq(_ffn_bwd_w, c, f"ffn_bwd_w{lf}", hnf, da, p, dhb)
        send(f"ffn{lf}", [([("ffn_w1", lf, 0)], dw1), ([("ffn_w2", lf, 0)], dw2)])
        return dh1

    for l in reversed(range(LB)):
        h0, q, hn, o, probs, h1, p, hnf = sav_b[l]
        wq, wo, w1, rows = wts_b[l]
        dh1 = ffn_bwd(LA + l, dh, h1, p, hnf, w1, 0, rows)
        do = seq(_bwd_rows_data, c, f"o_bwd_data{l}", dh1, wo, c.orr, 0)
        dwo = seq(_wgrad_rows, c, f"o_bwd_w{l}", o, [dh1], c.DQ, D, unslot="rows")
        dq, dkv, dbias, dsink = seq(_attn_bwd, c, f"attn_bwd{l}", q, kv, do, probs)
        dsink_list[l] = dsink.reshape(Q_PER_KV, 2, KV_PAIRS).transpose(2, 1, 0).reshape(1, N_HEADS)
        dkv_list.append(dkv)
        dbias_list.append(_unslot_bias(dbias))
        dwq = seq(_wgrad_rows, c, f"q_bwd_w{l}", hn, [dq], D, c.DQ, unslot="cols")
        send(f"attn{l}", [([("b_w_o", l, 0)], dwo), ([("b_w_q", l, 0)], dwq)])
        dh, d_mix_g[LA + l] = seq(_bwd_rows_to_stream, c, f"q_bwd_data{l}", [dq], wq, c.qr, 0, c.DQ, h0,
                                  mix_g[LA + l], dh1)
    dwkv = seq(_wgrad_rows, c, "kv_bwd_w", hkv, dkv_list, D, 2 * c.DKV)
    send("kv", [([("w_k", 0, 0), ("w_v", 0, 1)], dwkv)])
    dh, d_kv_g = seq(_bwd_rows_to_stream, c, "kv_bwd_data", dkv_list, wkv, c.kr, 0, 2 * c.DKV, h_kv, kv_g, dh)
    d_rel_t = seq(_band_bias_grad, dbias_list, onehot)
    d_wsp, d_bsp, d_lng = [None] * LA, [None] * LA, [None] * LA
    for l in reversed(range(LA)):
        h0, z, a, hn, gated, h1, p, hnf = sav_a[l]
        w_in, in_i, w1, w1_i, rows, wout, wout_i = wts_a[l]
        dh1 = ffn_bwd(l, dh, h1, p, hnf, w1, w1_i, rows)
        dgated = seq(_bwd_rows_data, c, f"a_out_bwd_data{l}", dh1, wout, c.ar, wout_i)
        dwout = seq(_wgrad_rows, c, f"a_out_bwd_w{l}", gated, [dh1], c.AW, D)
        send(f"a_out{l}", [([("a_w_out", l, 0)], dwout)])
        dz, d_wsp[l], dbt, d_lng[l] = seq(_sgu_bwd, c, f"sgu_bwd{l}", a, z, dgated, ln_g_full[l], wsp16[l],
                                          wsp16_t[l], bsp_t[l])
        d_bsp[l] = dbt.T
        dwin = seq(_wgrad_cols, c, f"a_in_bwd_w{l}", hn, dz)
        send(f"a_in{l}", [([("a_w_in", l, 0)], dwin)])
        dh, d_mix_g[l] = seq(_bwd_cols_to_stream, c, f"a_in_bwd_data{l}", dz, w_in, in_i, h0, mix_g[l], dh1)
    grad_x = dh.reshape(1, S, D)

    small = {
        "mix_norm_g": jnp.concatenate(d_mix_g, axis=0), "ffn_norm_g": jnp.concatenate(d_ffn_g, axis=0),
        "a_w_spatial": jnp.stack(d_wsp), "a_b_spatial": jnp.stack(d_bsp), "kv_norm_g": d_kv_g,
        "b_sinks": jnp.concatenate(dsink_list, axis=0), "rel_bias": d_rel_t.T, "final_norm_g": d_fin_g,
    }
    small_names = list(small)
    packs = [_rows128(small[k]) for k in small_names] + [_rows128(jnp.concatenate(d_lng, axis=0)), _rows128(loss_row)]
    offs = [int(o) for o in np.cumsum([0] + [p.shape[0] for p in packs])]
    Rs = offs[-1] + (-offs[-1]) % (8 * NDEV)
    packed = jnp.concatenate(packs + [jnp.zeros((Rs - offs[-1], LANES), F32)], axis=0)
    slab = packed.reshape(NDEV, Rs // NDEV, LANES)
    st = seq(_send_start, "small_grads_start", [slab], [lax.empty(slab.shape, slab.dtype)], "exchange")
    while len(in_flight) > 1:
        land(*in_flight.pop(0))
    (parts,) = seq(_send_wait, "small_grads_wait", st, "exchange")
    mine = seq(_sum_parts, "small_grads_sum", parts)
    st = seq(_send_start, "small_sums_start", [mine], [_landing(mine)], "broadcast")
    while in_flight:
        land(*in_flight.pop(0))
    (sums,) = seq(_send_wait, "small_sums_wait", st, "broadcast")
    sums = sums.reshape(Rs, LANES)
    loss = sums[offs[-2], 0]

    grads, deltas, new_m, new_v = {}, {}, {}, {}

    def put(k, outs, shape):
        grads[k], deltas[k], new_m[k], new_v[k] = (t.reshape(shape) for t in outs)

    outs = seq(_adamw_packed, "adamw_small", sums, offs, *[[_rows128(d[k]) for k in small_names]
                                                          for d in (weights, m_in, v_in)])
    for n_, k in enumerate(small_names):
        shape = weights[k].shape
        size = int(np.prod(shape))
        put(k, [t.reshape(-1)[:size] for t in outs[4 * n_:4 * n_ + 4]], shape)
    lng_sum = sums[offs[-3]:offs[-2]].reshape(-1)[:LA * c.AW].reshape(LA, c.AW)
    lng_mine = lax.dynamic_slice_in_dim(lng_sum, me * c.ar, c.ar, axis=1)
    lng_parts = jnp.concatenate([lng_mine[None], jnp.zeros((NDEV - 1, LA, c.ar), F32)], axis=0)
    put("a_ln_g", seq(_adamw, "adamw_ln_g", lng_parts, (NDEV, LA, c.ar), lambda i: (0, 0, 0),
                      a_ln_g, m_in["a_ln_g"], v_in["a_ln_g"], LA), a_ln_g.shape)
    for k in ("a_w_in", "ffn_w1", "ffn_w2", "a_w_out", "b_w_o", "b_w_q", "w_k", "w_v"):
        put(k, results[k], weights[k].shape)

    return (loss, grad_x, *[grads[k] for k in names], *[deltas[k] for k in names],
            *[new_m[k] for k in names], *[new_v[k] for k in names])
```

```python
import numpy as np
import math
import jax
import jax.numpy as jnp
from jax import lax
from jax.experimental import pallas as pl
from jax.experimental.pallas import tpu as pltpu

F32 = jnp.float32
BF16 = jnp.bfloat16

NDEV = 8
EPS = 1e-6
CHUNK = 128
A_GROUPS = 8
N_HEADS = 16
N_KV_HEADS = 4
Q_PER_KV = N_HEADS // N_KV_HEADS
HEAD_DIM = 64
BLOCK = 128
N_BUCKETS = 32
MAX_DISTANCE = 128
ADAM_LR, ADAM_B1, ADAM_B2, ADAM_EPS, ADAM_WD, ADAM_STEP = 0.001, 0.9, 0.999, 1e-08, 0.01, 10
LANES = 128
VMEM_LIMIT = 56 * 1024 * 1024
INV_SQRT2 = 0.7071067811865476
INV_SQRT_2PI = 0.3989422804014327
EXCHANGE_LAG = 4
SIBLING_COLLECTIVE_ID = 0

HBM = pl.BlockSpec(memory_space=pltpu.HBM)
SMEM = pl.BlockSpec(memory_space=pltpu.SMEM)
ANY = pl.BlockSpec(memory_space=pl.ANY)
SEM = pl.BlockSpec(memory_space=pltpu.SEMAPHORE)
MESH = pl.DeviceIdType.MESH


def _params(n_grid):
    return pltpu.CompilerParams(dimension_semantics=("arbitrary",) * n_grid, vmem_limit_bytes=VMEM_LIMIT)


def _const(block, index_map):
    return pl.BlockSpec(block, index_map, pipeline_mode=pl.Buffered(1))


def _pcall(body, *, ins, in_specs, dep=None, **kw):
    n_in = len(ins)
    if dep is None or any(dep is t for t in ins):
        return pl.pallas_call(body, in_specs=list(in_specs), **kw)(*ins)

    def with_dep(*refs):
        body(*refs[:n_in], *refs[n_in + 1:])

    return pl.pallas_call(with_dep, in_specs=[*in_specs, ANY], **kw)(*ins, dep)


class _Seq:
    def __init__(self):
        self.last = None

    def __call__(self, fn, *args, **kw):
        out = fn(*args, dep=self.last, **kw)
        self.last = out[0] if isinstance(out, (tuple, list)) else out
        return out


def _rstd(h):
    return lax.rsqrt(jnp.mean(h * h, axis=-1, keepdims=True) + EPS)


def _rms_bwd(dhn, h, g, dres):
    r = _rstd(h)
    xh = h * r
    dg = jnp.sum(dhn * xh, axis=0, keepdims=True)
    dxh = dhn * g
    dx = r * (dxh - xh * jnp.mean(dxh * xh, axis=-1, keepdims=True))
    return dres + dx, dg


def _gelu(z):
    return 0.5 * z * (1.0 + lax.erf(z * INV_SQRT2))


def _gelu_grad(z):
    return 0.5 * (1.0 + lax.erf(z * INV_SQRT2)) + z * (jnp.exp(-0.5 * z * z) * INV_SQRT_2PI)


def _dot(a, b, dims):
    return lax.dot_general(a, b, (dims, ((), ())), preferred_element_type=F32)


NN = ((1,), (0,))
NT = ((1,), (1,))
TN = ((0,), (0,))


def _mm(name, ins, in_specs, out_shapes, out_specs, *, grid, dims, nk, acc_shape, load_a, load_b, epilogue,
        dep=None):
    n_in, n_out = len(ins), len(out_shapes)
    kax = len(grid) - 1

    def body(*refs):
        in_refs = refs[:n_in]
        out_refs = refs[n_in:n_in + n_out]
        a = load_a(in_refs, out_refs)
        b = load_b(in_refs)
        prod = _dot(a, b, dims)
        if nk == 1:
            epilogue(prod, in_refs, out_refs)
        else:
            acc = refs[n_in + n_out]
            k = pl.program_id(kax)

            @pl.when(k == 0)
            def _():
                acc[...] = prod

            @pl.when(k > 0)
            def _():
                acc[...] += prod

            @pl.when(k == nk - 1)
            def _():
                epilogue(acc[...], in_refs, out_refs)

    return _pcall(
        body, name=name, ins=ins, in_specs=in_specs, dep=dep, grid=grid, out_specs=out_specs, out_shape=out_shapes,
        scratch_shapes=[pltpu.VMEM(acc_shape, F32)] if nk > 1 else [], compiler_params=_params(len(grid)))


def _bf(ref_idx):
    return lambda in_refs, *_: in_refs[ref_idx][...].astype(BF16)


def _b_view(ref_idx, rows):
    def load(in_refs):
        b = in_refs[ref_idx][...]
        return b.reshape(rows, b.shape[-1])
    return load


class Cfg:
    pass


def _config(x, a_w_in, a_w_out, w_k, b_w_q, b_w_o, ffn_w1, ffn_w2):
    c = Cfg()
    c.S, c.D = x.shape[1], x.shape[2]
    c.LA, _, c.cw = a_w_in.shape
    c.AW2 = NDEV * c.cw
    c.AW = c.AW2 // 2
    c.gd = c.AW // A_GROUPS
    c.ar = a_w_out.shape[1]
    c.LF, _, c.fw = ffn_w1.shape
    c.fr = ffn_w2.shape[1]
    c.LB, c.qr, c.DQ = b_w_q.shape
    c.orr = b_w_o.shape[1]
    c.kr, c.DKV = w_k.shape
    c.tm = min(1024, c.S)
    c.tmw = min(512, c.S)
    c.tms = min(512, c.S)
    c.nb = c.S // BLOCK
    assert c.cw == c.fw == c.fr and c.AW == NDEV * c.ar and c.D == NDEV * c.qr == NDEV * c.kr
    assert c.DQ == NDEV * c.orr == N_HEADS * HEAD_DIM and c.DKV == N_KV_HEADS * HEAD_DIM
    assert c.S % c.tm == 0 and c.S % c.tmw == 0 and c.S % c.tms == 0 and c.tms % CHUNK == 0 and c.gd % LANES == 0
    assert c.LA >= 1 and c.LB >= 1 and c.LF == c.LA + c.LB
    return c


def _cached_rms(h_idx, g_idx, hn_out_idx, jax_axis=1):
    def load(in_refs, out_refs):
        hn_ref = out_refs[hn_out_idx]

        @pl.when(pl.program_id(jax_axis) == 0)
        def _():
            h = in_refs[h_idx][...]
            hn_ref[...] = (h * _rstd(h) * in_refs[g_idx][...]).astype(BF16)

        return hn_ref[...]
    return load


def _a_in_fwd(c, name, h, g, col, ci, dep=None):
    S, D, cw, tm = c.S, c.D, c.cw, c.tmw

    def body(h_ref, g_ref, w_ref, z_ref, a_ref, hn_ref):
        h = h_ref[...]
        hn = (h * _rstd(h) * g_ref[...]).astype(BF16)
        hn_ref[...] = hn
        for j in range(NDEV):
            cols = slice(j * cw, (j + 1) * cw)
            z = _dot(hn, w_ref[j], NT)
            z_ref[:, cols] = z.astype(BF16)
            a_ref[:, cols] = _gelu(z).astype(BF16)

    row = pl.BlockSpec((tm, D), lambda i: (i, 0))
    wide = pl.BlockSpec((tm, c.AW2), lambda i: (i, 0))
    return _pcall(
        body, name=name, ins=[h, g, col], dep=dep, grid=(S // tm,),
        in_specs=[row, pl.BlockSpec((1, D), lambda i: (0, 0)), _const((NDEV, None, cw, D), lambda i: (0, ci, 0, 0))],
        out_specs=[wide, wide, row],
        out_shape=[jax.ShapeDtypeStruct((S, c.AW2), BF16), jax.ShapeDtypeStruct((S, c.AW2), BF16),
                   jax.ShapeDtypeStruct((S, D), BF16)],
        compiler_params=_params(1))


def _rms_mm_rows(c, name, h, g, slab, blk_rows, blk_idx, n_out, dep=None):
    S, D, tm = c.S, c.D, c.tm

    def epilogue(acc, in_refs, out_refs):
        out_refs[0][...] = acc.astype(BF16)

    return _mm(
        name, [h, slab, g],
        [pl.BlockSpec((tm, D), lambda i, j, k: (i, 0)),
         pl.BlockSpec((NDEV, blk_rows, n_out), lambda i, j, k: (0, blk_idx, 0)),
         pl.BlockSpec((1, D), lambda i, j, k: (0, 0))],
        [jax.ShapeDtypeStruct((S, n_out), BF16), jax.ShapeDtypeStruct((S, D), BF16)],
        [pl.BlockSpec((tm, n_out), lambda i, j, k: (i, 0)), pl.BlockSpec((tm, D), lambda i, j, k: (i, 0))],
        grid=(S // tm, 1, 1), dims=NN, nk=1, acc_shape=None,
        load_a=_cached_rms(0, 2, 1), load_b=_b_view(1, NDEV * blk_rows), epilogue=epilogue, dep=dep)


def _mm_res(c, name, a, slab, blk_rows, blk_idx, res, dep=None):
    S, D, tm = c.S, c.D, c.tm
    K = NDEV * blk_rows

    def epilogue(acc, in_refs, out_refs):
        out_refs[0][...] = in_refs[2][...] + acc

    return _mm(
        name, [a, slab, res],
        [pl.BlockSpec((tm, K), lambda i, j, k: (i, 0)),
         pl.BlockSpec((NDEV, blk_rows, D), lambda i, j, k: (0, blk_idx, 0)),
         pl.BlockSpec((tm, D), lambda i, j, k: (i, 0))],
        [jax.ShapeDtypeStruct((S, D), F32)], [pl.BlockSpec((tm, D), lambda i, j, k: (i, 0))],
        grid=(S // tm, 1, 1), dims=NN, nk=1, acc_shape=None,
        load_a=_bf(0), load_b=_b_view(1, K), epilogue=epilogue, dep=dep)[0]


def _sgu_masks():
    ii = lax.broadcasted_iota(jnp.int32, (CHUNK, CHUNK), 0)
    jj = lax.broadcasted_iota(jnp.int32, (CHUNK, CHUNK), 1)
    return ii >= jj


def _sgu_fwd(c, name, a, ln_g, wc, b_t, dep=None):
    S, AW, gd, tm = c.S, c.AW, c.gd, c.tms

    def body(a_ref, lng_ref, wc_ref, bt_ref, out_ref):
        va = a_ref[:, AW:].astype(F32)
        xc = va - jnp.mean(va, axis=-1, keepdims=True)
        vn = (xc * lax.rsqrt(jnp.mean(xc * xc, axis=-1, keepdims=True) + EPS) * lng_ref[...]).astype(BF16)
        for ch in range(tm // CHUNK):
            rows = slice(ch * CHUNK, (ch + 1) * CHUNK)
            for g in range(A_GROUPS):
                cols = slice(g * gd, (g + 1) * gd)
                mixed = _dot(wc_ref[g], vn[rows, cols], NN) + bt_ref[:, g:g + 1]
                out_ref[rows, cols] = (a_ref[rows, cols].astype(F32) * mixed).astype(BF16)

    return _pcall(
        body, name=name, ins=[a, ln_g, wc, b_t], dep=dep, grid=(S // tm,),
        in_specs=[pl.BlockSpec((tm, 2 * AW), lambda i: (i, 0)), pl.BlockSpec((1, AW), lambda i: (0, 0)),
                  pl.BlockSpec((A_GROUPS, CHUNK, CHUNK), lambda i: (0, 0, 0)),
                  pl.BlockSpec((CHUNK, A_GROUPS), lambda i: (0, 0))],
        out_specs=pl.BlockSpec((tm, AW), lambda i: (i, 0)),
        out_shape=jax.ShapeDtypeStruct((S, AW), BF16), compiler_params=_params(1))


def _ffn_fwd(c, name, h, g, col, ci, rows, dep=None):
    S, D, fw, tm = c.S, c.D, c.fw, c.tmw
    F = NDEV * fw

    def body(h_ref, g_ref, w1_ref, w2_ref, p_ref, out_ref, hn_ref, r_ref):
        h = h_ref[...]
        hn = (h * _rstd(h) * g_ref[...]).astype(BF16)
        hn_ref[...] = hn
        for j in range(NDEV):
            cols = slice(j * fw, (j + 1) * fw)
            p = jnp.maximum(_dot(hn, w1_ref[j], NT), 0.0)
            p_ref[:, cols] = p.astype(BF16)
            r_ref[:, cols] = (p * p).astype(BF16)
        out_ref[...] = h + _dot(r_ref[...], w2_ref[...].reshape(F, D), NN)

    row = pl.BlockSpec((tm, D), lambda i: (i, 0))
    return _pcall(
        body, name=name, ins=[h, g, col, rows], dep=dep, grid=(S // tm,),
        in_specs=[row, pl.BlockSpec((1, D), lambda i: (0, 0)),
                  _const((NDEV, None, fw, D), lambda i: (0, ci, 0, 0)), _const((NDEV, c.fr, D), lambda i: (0, 0, 0))],
        out_specs=[pl.BlockSpec((tm, F), lambda i: (i, 0)), row, row],
        out_shape=[jax.ShapeDtypeStruct((S, F), BF16), jax.ShapeDtypeStruct((S, D), F32),
                   jax.ShapeDtypeStruct((S, D), BF16)],
        scratch_shapes=[pltpu.VMEM((tm, F), BF16)], compiler_params=_params(1))


def _bucket_table():
    qi = np.arange(BLOCK)[:, None]
    kj = np.arange(2 * BLOCK)[None, :]
    d = np.maximum(qi + BLOCK - kj, 0)
    max_exact = N_BUCKETS // 2
    ratio = np.log(np.maximum(d, 1).astype(np.float32) / np.float32(max_exact)) / np.float32(
        math.log(MAX_DISTANCE / max_exact))
    large = np.minimum(max_exact + (ratio.astype(np.float32) * np.float32(N_BUCKETS - max_exact)).astype(np.int32),
                       N_BUCKETS - 1)
    return np.where(d < max_exact, d, large).astype(np.int32)


def _bucket_onehot():
    b = jnp.asarray(_bucket_table().reshape(1, -1))
    return (b == lax.broadcasted_iota(jnp.int32, (N_BUCKETS, b.shape[1]), 0)).astype(F32)


def _whole(t):
    return pl.BlockSpec(t.shape, lambda: (0,) * t.ndim)


def _band_bias(rel_bias_t, onehot, dep=None):
    def body(r_ref, oh_ref, out_ref):
        out_ref[...] = lax.dot_general(r_ref[...], oh_ref[...], (NN, ((), ())), preferred_element_type=F32,
                                       precision=lax.Precision.HIGHEST)

    n = onehot.shape[1]
    return _pcall(body, name="band_bias", ins=[rel_bias_t, onehot], in_specs=[_whole(rel_bias_t), _whole(onehot)],
                  dep=dep, out_shape=jax.ShapeDtypeStruct((N_HEADS, n), F32), compiler_params=_params(0))


def _band_bias_grad(dbias_list, onehot, dep=None):
    n_in = len(dbias_list)

    def body(*refs):
        oh_ref, out_ref = refs[n_in], refs[n_in + 1]
        d = refs[0][...]
        for r in refs[1:n_in]:
            d = d + r[...]
        out_ref[...] = lax.dot_general(d, oh_ref[...], (NT, ((), ())), preferred_element_type=F32,
                                       precision=lax.Precision.HIGHEST)

    ins = [*dbias_list, onehot]
    return _pcall(body, name="band_bias_grad", ins=ins, in_specs=[_whole(t) for t in ins], dep=dep,
                  out_shape=jax.ShapeDtypeStruct((N_HEADS, N_BUCKETS), F32), compiler_params=_params(0))


KV_PAIRS = N_KV_HEADS // 2
PAIR_ROWS = 2 * Q_PER_KV * BLOCK
MASKED = float(np.finfo(np.float32).min) / 2


def _slot_cols(w):
    lead = w.shape[:-1]
    return w.reshape(*lead, KV_PAIRS, 2, Q_PER_KV, HEAD_DIM).swapaxes(-3, -2).reshape(*lead, N_HEADS * HEAD_DIM)


def _slot_rows(blocks):
    n = blocks.shape[-1]
    return blocks.reshape(KV_PAIRS, 2, Q_PER_KV, HEAD_DIM, n).swapaxes(1, 2).reshape(blocks.shape)


def _slot_bias(bias):
    qi = np.arange(BLOCK)[:, None]
    kj = np.arange(2 * BLOCK)[None, :]
    dist = qi + BLOCK - kj
    window = (dist >= 0) & (dist < BLOCK)
    b = bias.reshape(KV_PAIRS, 2, Q_PER_KV, BLOCK, 2 * BLOCK).swapaxes(1, 2).reshape(KV_PAIRS, PAIR_ROWS, 2 * BLOCK)
    tile = lambda mk: jnp.asarray(np.tile(mk, (2 * Q_PER_KV, 1)))[None]
    return jnp.stack([jnp.where(tile(window & (kj >= BLOCK)), b, MASKED), jnp.where(tile(window), b, MASKED)])


def _pair_kv(kvc_ref, kvp_ref, kvp, dkv):
    lanes = slice(kvp * LANES, (kvp + 1) * LANES)
    vlanes = slice(dkv + kvp * LANES, dkv + (kvp + 1) * LANES)
    k2 = jnp.concatenate([kvp_ref[:, lanes], kvc_ref[:, lanes]], axis=0)
    v2 = jnp.concatenate([kvp_ref[:, vlanes], kvc_ref[:, vlanes]], axis=0)
    return k2, v2


def _head_operand(ref, grp, par, low, scale=None):
    xg = ref[:, grp * LANES:(grp + 1) * LANES]
    if scale is not None:
        xg = xg * scale
    zero = jnp.zeros_like(xg)
    return jnp.where(low, xg, zero) if par == 0 else jnp.where(low, zero, xg)


def _head_probs(qh, k2, bias_rows, sink):
    s = _dot(qh, k2, NT) + bias_rows
    m = jnp.maximum(jnp.max(s, axis=-1, keepdims=True), sink)
    e = jnp.exp(s - m)
    es = jnp.exp(sink - m)
    inv = 1.0 / (jnp.sum(e, axis=-1, keepdims=True) + es)
    return e * inv, es * inv


def _attn_specs(c):
    dq, dkv2 = c.DQ, 2 * c.DKV
    return [pl.BlockSpec((BLOCK, dq), lambda n: (n, 0)),
            pl.BlockSpec((BLOCK, dkv2), lambda n: (n, 0)),
            pl.BlockSpec((BLOCK, dkv2), lambda n: (jnp.maximum(n - 1, 0), 0))]


def _bias_spec():
    return pl.BlockSpec((None, KV_PAIRS, PAIR_ROWS, 2 * BLOCK), lambda n: (jnp.minimum(n, 1), 0, 0, 0))


def _low_lanes():
    return lax.broadcasted_iota(jnp.int32, (BLOCK, LANES), 1) < HEAD_DIM


def _first_key():
    return lax.broadcasted_iota(jnp.int32, (BLOCK, 2 * BLOCK), 1) == 0


def _probs_spec():
    return pl.BlockSpec((None, KV_PAIRS, PAIR_ROWS, 2 * BLOCK), lambda n: (n, 0, 0, 0))


def _attn_fwd(c, name, q, kv, bias, sinks, dep=None):
    S, dq = c.S, c.DQ

    def body(q_ref, kvc_ref, kvp_ref, bias_ref, sink_ref, o_ref, probs_ref):
        low = _low_lanes()
        first = _first_key()
        for kvp in range(KV_PAIRS):
            k2, v2 = _pair_kv(kvc_ref, kvp_ref, kvp, c.DKV)
            for g in range(Q_PER_KV):
                grp = kvp * Q_PER_KV + g
                halves = []
                for par in range(2):
                    rows = slice((2 * g + par) * BLOCK, (2 * g + par + 1) * BLOCK)
                    qh = _head_operand(q_ref, grp, par, low, scale=HEAD_DIM ** -0.5)
                    p, ps = _head_probs(qh, k2, bias_ref[kvp, rows, :], sink_ref[(2 * kvp + par) * Q_PER_KV + g])
                    probs_ref[kvp, rows, :] = jnp.where(first, ps, p).astype(BF16)
                    halves.append(_dot(p.astype(BF16), v2, NN))
                o_ref[:, grp * LANES:(grp + 1) * LANES] = jnp.where(low, halves[0], halves[1]).astype(BF16)

    return _pcall(
        body, name=name, ins=[q, kv, kv, bias, sinks], dep=dep, grid=(c.nb,),
        in_specs=_attn_specs(c) + [_bias_spec(), SMEM],
        out_specs=[pl.BlockSpec((BLOCK, dq), lambda n: (n, 0)), _probs_spec()],
        out_shape=[jax.ShapeDtypeStruct((S, dq), BF16),
                   jax.ShapeDtypeStruct((c.nb, KV_PAIRS, PAIR_ROWS, 2 * BLOCK), BF16)],
        compiler_params=_params(1))


def _final_loss(c, h, g, target, dep=None):
    S, D, tm = c.S, c.D, c.tm

    def body(h_ref, g_ref, t_ref, dh_ref, dg_ref, loss_ref):
        i = pl.program_id(0)
        h = h_ref[...]
        gg = g_ref[...]
        r = _rstd(h)
        xh = h * r
        err = xh * gg - t_ref[...]
        lp = jnp.sum(jnp.sum(err * err, axis=1, keepdims=True), axis=0, keepdims=True) * (0.5 / D)
        dx, dg = _rms_bwd(err * (1.0 / D), h, gg, 0.0)
        dh_ref[...] = dx

        @pl.when(i == 0)
        def _():
            dg_ref[...] = dg
            loss_ref[...] = jnp.broadcast_to(lp, loss_ref.shape)

        @pl.when(i > 0)
        def _():
            dg_ref[...] += dg
            loss_ref[...] += jnp.broadcast_to(lp, loss_ref.shape)

    row = pl.BlockSpec((tm, D), lambda i: (i, 0))
    return _pcall(
        body, name="final_loss", ins=[h, g, target], dep=dep, grid=(S // tm,),
        in_specs=[row, pl.BlockSpec((1, D), lambda i: (0, 0)), row],
        out_specs=[row, pl.BlockSpec((1, D), lambda i: (0, 0)), pl.BlockSpec((1, LANES), lambda i: (0, 0))],
        out_shape=[jax.ShapeDtypeStruct((S, D), F32), jax.ShapeDtypeStruct((1, D), F32),
                   jax.ShapeDtypeStruct((1, LANES), F32)],
        compiler_params=_params(1))


def _rms_bwd_epilogue(h_idx, g_idx, res_idx):
    def epilogue(dhn, in_refs, out_refs):
        dh, dg = _rms_bwd(dhn, in_refs[h_idx][...], in_refs[g_idx][...], in_refs[res_idx][...])
        out_refs[0][...] = dh
        i = pl.program_id(0)

        @pl.when(i == 0)
        def _():
            out_refs[1][...] = dg

        @pl.when(i > 0)
        def _():
            out_refs[1][...] += dg
    return epilogue


def _stream_outs(c, tm):
    S, D = c.S, c.D
    return ([jax.ShapeDtypeStruct((S, D), F32), jax.ShapeDtypeStruct((1, D), F32)],
            [pl.BlockSpec((tm, D), lambda i, j, k: (i, 0)), pl.BlockSpec((1, D), lambda i, j, k: (0, 0))])


def _row_specs(c, tm):
    D = c.D
    return [pl.BlockSpec((tm, D), lambda i, j, k: (i, 0)), pl.BlockSpec((1, D), lambda i, j, k: (0, 0)),
            pl.BlockSpec((tm, D), lambda i, j, k: (i, 0))]


def _bwd_rows_to_stream(c, name, dy_list, slab, blk_rows, blk_idx, n_in_cols, h, g, dres, dep=None):
    S, D, tm = c.S, c.D, c.tm
    nd = len(dy_list)

    def load_a(in_refs, out_refs):
        a = in_refs[0][...]
        for r in in_refs[1:nd]:
            a = a + r[...]
        return a.astype(BF16)

    shapes, specs = _stream_outs(c, tm)
    return _mm(
        name, [*dy_list, slab, h, g, dres],
        [pl.BlockSpec((tm, n_in_cols), lambda i, j, k: (i, 0))] * nd
        + [pl.BlockSpec((NDEV, blk_rows, n_in_cols), lambda i, j, k: (0, blk_idx, 0))] + _row_specs(c, tm),
        shapes, specs, grid=(S // tm, 1, 1), dims=NT, nk=1, acc_shape=None,
        load_a=load_a, load_b=_b_view(nd, NDEV * blk_rows), epilogue=_rms_bwd_epilogue(nd + 1, nd + 2, nd + 3),
        dep=dep)


def _bwd_cols_to_stream(c, name, dy, col, ci, h, g, dres, dep=None):
    S, D, cw, tm = c.S, c.D, c.cw, c.tmw
    K = NDEV * cw
    shapes, specs = _stream_outs(c, tm)
    return _mm(
        name, [dy, col, h, g, dres],
        [pl.BlockSpec((tm, K), lambda i, j, k: (i, 0)),
         _const((NDEV, None, cw, D), lambda i, j, k: (0, ci, 0, 0))] + _row_specs(c, tm),
        shapes, specs, grid=(S // tm, 1, 1), dims=NN, nk=1, acc_shape=None,
        load_a=_bf(0), load_b=_b_view(1, K), epilogue=_rms_bwd_epilogue(2, 3, 4), dep=dep)


def _bwd_rows_data(c, name, dy, slab, blk_rows, blk_idx, dep=None):
    S, D, tm = c.S, c.D, c.tm
    K = NDEV * blk_rows

    def epilogue(acc, in_refs, out_refs):
        out_refs[0][...] = acc.astype(BF16)

    return _mm(
        name, [dy, slab],
        [pl.BlockSpec((tm, D), lambda i, j, k: (i, 0)),
         pl.BlockSpec((NDEV, blk_rows, D), lambda i, j, k: (0, blk_idx, 0))],
        [jax.ShapeDtypeStruct((S, K), BF16)], [pl.BlockSpec((tm, K), lambda i, j, k: (i, 0))],
        grid=(S // tm, 1, 1), dims=NT, nk=1, acc_shape=None,
        load_a=_bf(0), load_b=_b_view(1, K), epilogue=epilogue, dep=dep)[0]


def _wgrad_rows(c, name, a, b_list, n_a, n_b, unslot=None, dep=None):
    S, tm = c.S, c.tm
    nb_in = len(b_list)
    blk_rows = n_a // NDEV

    def load_b(in_refs):
        b = in_refs[1][...]
        for r in in_refs[2:1 + nb_in]:
            b = b + r[...]
        return b.astype(BF16)

    def epilogue(acc, in_refs, out_refs):
        out = out_refs[0]
        if unslot is None:
            out[...] = acc.reshape(NDEV, blk_rows, n_b).astype(BF16)
            return
        per_blk = blk_rows // HEAD_DIM
        for h in range(N_HEADS):
            s = (h // (2 * Q_PER_KV)) * 2 * Q_PER_KV + (h % Q_PER_KV) * 2 + (h // Q_PER_KV) % 2
            if unslot == "rows":
                out[h // per_blk, (h % per_blk) * HEAD_DIM:(h % per_blk + 1) * HEAD_DIM, :] = (
                    acc[s * HEAD_DIM:(s + 1) * HEAD_DIM, :].astype(BF16))
            else:
                out[:, :, h * HEAD_DIM:(h + 1) * HEAD_DIM] = (
                    acc[:, s * HEAD_DIM:(s + 1) * HEAD_DIM].reshape(NDEV, blk_rows, HEAD_DIM).astype(BF16))

    return _mm(
        name, [a, *b_list],
        [pl.BlockSpec((tm, n_a), lambda i, j, k: (k, 0))] + [pl.BlockSpec((tm, n_b), lambda i, j, k: (k, 0))] * nb_in,
        [jax.ShapeDtypeStruct((NDEV, blk_rows, n_b), BF16)],
        [pl.BlockSpec((NDEV, blk_rows, n_b), lambda i, j, k: (0, 0, 0))],
        grid=(1, 1, S // tm), dims=TN, nk=S // tm, acc_shape=(n_a, n_b),
        load_a=_bf(0), load_b=load_b, epilogue=epilogue, dep=dep)[0]


def _wgrad_cols(c, name, a, b, dep=None):
    S, D, cw = c.S, c.D, c.cw

    def body(a_ref, b_ref, out_ref):
        out_ref[...] = _dot(a_ref[...], b_ref[...], TN).astype(BF16)

    return _pcall(
        body, name=name, ins=[a, b], dep=dep, grid=(NDEV,),
        in_specs=[_const((S, D), lambda j: (0, 0)), pl.BlockSpec((S, cw), lambda j: (0, j))],
        out_specs=pl.BlockSpec((None, D, cw), lambda j: (j, 0, 0)),
        out_shape=jax.ShapeDtypeStruct((NDEV, D, cw), BF16), compiler_params=_params(1))


def _ffn_bwd_data(c, name, dh, p, col, ci, rows, h, g, dep=None):
    S, D, fw, tm = c.S, c.D, c.fw, c.tmw
    F = NDEV * fw

    def body(dh_ref, p_ref, w1t_ref, w2_ref, h_ref, g_ref, da_ref, out_ref, dg_ref, dhb_ref):
        i = pl.program_id(0)
        dh = dh_ref[...]
        dhb = dh.astype(BF16)
        dhb_ref[...] = dhb
        for j in range(NDEV):
            cols = slice(j * fw, (j + 1) * fw)
            da_ref[:, cols] = (_dot(dhb, w2_ref[j], NT) * (2.0 * p_ref[:, cols].astype(F32))).astype(BF16)
        dx, dg = _rms_bwd(_dot(da_ref[...], w1t_ref[...].reshape(F, D), NN), h_ref[...], g_ref[...], dh)
        out_ref[...] = dx

        @pl.when(i == 0)
        def _():
            dg_ref[...] = dg

        @pl.when(i > 0)
        def _():
            dg_ref[...] += dg

    row = pl.BlockSpec((tm, D), lambda i: (i, 0))
    wide = pl.BlockSpec((tm, F), lambda i: (i, 0))
    return _pcall(
        body, name=name, ins=[dh, p, col, rows, h, g], dep=dep, grid=(S // tm,),
        in_specs=[row, wide, _const((NDEV, None, fw, D), lambda i: (0, ci, 0, 0)),
                  _const((NDEV, c.fr, D), lambda i: (0, 0, 0)),
                  row, pl.BlockSpec((1, D), lambda i: (0, 0))],
        out_specs=[wide, row, pl.BlockSpec((1, D), lambda i: (0, 0)), row],
        out_shape=[jax.ShapeDtypeStruct((S, F), BF16), jax.ShapeDtypeStruct((S, D), F32),
                   jax.ShapeDtypeStruct((1, D), F32), jax.ShapeDtypeStruct((S, D), BF16)],
        compiler_params=_params(1))


def _ffn_bwd_w(c, name, hn, da, p, dhb, dep=None):
    S, D, fw = c.S, c.D, c.fw

    def body(hn_ref, da_ref, p_ref, dhb_ref, dw1_ref, dw2_ref):
        dw1_ref[...] = _dot(hn_ref[...], da_ref[...], TN).astype(BF16)
        pf = p_ref[...].astype(F32)
        dw2_ref[...] = _dot((pf * pf).astype(BF16), dhb_ref[...], TN).astype(BF16)

    panel = pl.BlockSpec((S, fw), lambda j: (0, j))
    return _pcall(
        body, name=name, ins=[hn, da, p, dhb], dep=dep, grid=(NDEV,),
        in_specs=[_const((S, D), lambda j: (0, 0)), panel, panel, _const((S, D), lambda j: (0, 0))],
        out_specs=[pl.BlockSpec((None, D, fw), lambda j: (j, 0, 0)), pl.BlockSpec((None, c.fr, D), lambda j: (j, 0, 0))],
        out_shape=[jax.ShapeDtypeStruct((NDEV, D, fw), BF16), jax.ShapeDtypeStruct((NDEV, c.fr, D), BF16)],
        compiler_params=_params(1))


def _attn_bwd(c, name, q, kv, do, probs, dep=None):
    S, dq, dkv = c.S, c.DQ, c.DKV
    nb = c.nb
    scale = HEAD_DIM ** -0.5

    def body(q_ref, kvc_ref, kvp_ref, do_ref, probs_ref, dq_ref, dkv_ref, dbias_ref, dsink_ref, dsink_acc,
             ds_sc, p_sc, qm_sc, dom_sc):
        n = pl.program_id(0)

        @pl.when(n == 0)
        def _():
            dkv_ref[...] = jnp.zeros_like(dkv_ref)
            dbias_ref[...] = jnp.zeros_like(dbias_ref)
            dsink_acc[...] = jnp.zeros_like(dsink_acc)

        low = _low_lanes()
        first = _first_key()
        rows_c = pl.ds(pl.multiple_of(n * BLOCK, BLOCK), BLOCK)
        rows_p = pl.ds(pl.multiple_of(jnp.maximum(n - 1, 0) * BLOCK, BLOCK), BLOCK)
        for kvp in range(KV_PAIRS):
            k2, v2 = _pair_kv(kvc_ref, kvp_ref, kvp, dkv)
            for g in range(Q_PER_KV):
                grp = kvp * Q_PER_KV + g
                halves = []
                for par in range(2):
                    rows = slice((2 * g + par) * BLOCK, (2 * g + par + 1) * BLOCK)
                    qh = _head_operand(q_ref, grp, par, low, scale=scale)
                    doh = _head_operand(do_ref, grp, par, low)
                    saved = probs_ref[kvp, rows, :]
                    ps = saved[:, 0:1].astype(F32)
                    p16 = jnp.where(first, jnp.zeros_like(saved), saved)
                    p = p16.astype(F32)
                    dp = _dot(doh, v2, NT)
                    delta = jnp.sum(p * dp, axis=-1, keepdims=True)
                    ds = p * (dp - delta)
                    dbias_ref[kvp, rows, :] += ds
                    dsink_acc[rows, kvp:kvp + 1] += -(ps * delta)
                    ds16 = ds.astype(BF16)
                    halves.append(_dot(ds16, k2, NN) * scale)
                    ds_sc[rows, :] = ds16
                    p_sc[rows, :] = p16
                    qm_sc[rows, :] = qh
                    dom_sc[rows, :] = doh
                dq_ref[:, grp * LANES:(grp + 1) * LANES] = jnp.where(low, halves[0], halves[1]).astype(BF16)
            dk2 = _dot(ds_sc[...], qm_sc[...], TN)
            dv2 = _dot(p_sc[...], dom_sc[...], TN)
            lanes = slice(kvp * LANES, (kvp + 1) * LANES)
            vlanes = slice(dkv + kvp * LANES, dkv + (kvp + 1) * LANES)
            dkv_ref[rows_p, lanes] += dk2[:BLOCK]
            dkv_ref[rows_c, lanes] += dk2[BLOCK:]
            dkv_ref[rows_p, vlanes] += dv2[:BLOCK]
            dkv_ref[rows_c, vlanes] += dv2[BLOCK:]

        @pl.when(n == nb - 1)
        def _():
            dsink_ref[...] = jnp.sum(dsink_acc[...].reshape(2 * Q_PER_KV, BLOCK, KV_PAIRS), axis=1)

    return _pcall(
        body, name=name, ins=[q, kv, kv, do, probs], dep=dep, grid=(nb,),
        in_specs=_attn_specs(c) + [pl.BlockSpec((BLOCK, dq), lambda n: (n, 0)), _probs_spec()],
        out_specs=[pl.BlockSpec((BLOCK, dq), lambda n: (n, 0)), pl.BlockSpec((S, 2 * dkv), lambda n: (0, 0)),
                   pl.BlockSpec((KV_PAIRS, PAIR_ROWS, 2 * BLOCK), lambda n: (0, 0, 0)),
                   pl.BlockSpec((2 * Q_PER_KV, KV_PAIRS), lambda n: (0, 0))],
        out_shape=[jax.ShapeDtypeStruct((S, dq), BF16), jax.ShapeDtypeStruct((S, 2 * dkv), F32),
                   jax.ShapeDtypeStruct((KV_PAIRS, PAIR_ROWS, 2 * BLOCK), F32),
                   jax.ShapeDtypeStruct((2 * Q_PER_KV, KV_PAIRS), F32)],
        scratch_shapes=[pltpu.VMEM((PAIR_ROWS, KV_PAIRS), F32), pltpu.VMEM((PAIR_ROWS, 2 * BLOCK), BF16),
                        pltpu.VMEM((PAIR_ROWS, 2 * BLOCK), BF16), pltpu.VMEM((PAIR_ROWS, LANES), BF16),
                        pltpu.VMEM((PAIR_ROWS, LANES), BF16)],
        compiler_params=_params(1))


def _sgu_bwd(c, name, a, z, dgated, ln_g, wc, wc_t, b_t, dep=None):
    S, AW, gd, tm = c.S, c.AW, c.gd, c.tms

    def body(a_ref, z_ref, dg_ref, lng_ref, wc_ref, wct_ref, bt_ref, dz_ref, dws_ref, dbt_ref, dlng_ref, dvn_ref):
        i = pl.program_id(0)

        @pl.when(i == 0)
        def _():
            dws_ref[...] = jnp.zeros_like(dws_ref)
            dbt_ref[...] = jnp.zeros_like(dbt_ref)
            dlng_ref[...] = jnp.zeros_like(dlng_ref)

        lng = lng_ref[...]
        va = a_ref[:, AW:].astype(F32)
        xc = va - jnp.mean(va, axis=-1, keepdims=True)
        rstd = lax.rsqrt(jnp.mean(xc * xc, axis=-1, keepdims=True) + EPS)
        xh = xc * rstd
        vn = (xh * lng).astype(BF16)
        causal = _sgu_masks()
        for ch in range(tm // CHUNK):
            rows = slice(ch * CHUNK, (ch + 1) * CHUNK)
            for g in range(A_GROUPS):
                cols = slice(g * gd, (g + 1) * gd)
                blk = vn[rows, cols]
                mixed = _dot(wc_ref[g], blk, NN) + bt_ref[:, g:g + 1]
                dgb = dg_ref[rows, cols].astype(F32)
                dm = dgb * a_ref[rows, cols].astype(F32)
                dbt_ref[:, g:g + 1] += jnp.sum(dm, axis=1, keepdims=True)
                dm16 = dm.astype(BF16)
                dws_ref[g] += jnp.where(causal, _dot(dm16, blk, NT), 0.0)
                dvn_ref[rows, cols] = _dot(wct_ref[g], dm16, NN)
                dz_ref[rows, cols] = (dgb * mixed * _gelu_grad(z_ref[rows, cols].astype(F32))).astype(BF16)
        dvn = dvn_ref[...]
        dlng_ref[...] += jnp.sum(dvn * xh, axis=0, keepdims=True)
        dxh = dvn * lng
        dva = rstd * (dxh - jnp.mean(dxh, axis=-1, keepdims=True) - xh * jnp.mean(dxh * xh, axis=-1, keepdims=True))
        dz_ref[:, AW:] = (dva * _gelu_grad(z_ref[:, AW:].astype(F32))).astype(BF16)

    wide = pl.BlockSpec((tm, 2 * AW), lambda i: (i, 0))
    wsp = pl.BlockSpec((A_GROUPS, CHUNK, CHUNK), lambda i: (0, 0, 0))
    btsp = pl.BlockSpec((CHUNK, A_GROUPS), lambda i: (0, 0))
    return _pcall(
        body, name=name, ins=[a, z, dgated, ln_g, wc, wc_t, b_t], dep=dep, grid=(S // tm,),
        in_specs=[wide, wide, pl.BlockSpec((tm, AW), lambda i: (i, 0)), pl.BlockSpec((1, AW), lambda i: (0, 0)),
                  wsp, wsp, btsp],
        out_specs=[wide, wsp, btsp, pl.BlockSpec((1, AW), lambda i: (0, 0))],
        out_shape=[jax.ShapeDtypeStruct((S, 2 * AW), BF16), jax.ShapeDtypeStruct((A_GROUPS, CHUNK, CHUNK), F32),
                   jax.ShapeDtypeStruct((CHUNK, A_GROUPS), F32), jax.ShapeDtypeStruct((1, AW), F32)],
        scratch_shapes=[pltpu.VMEM((tm, AW), F32)], compiler_params=_params(1))


def _adam_update(g, w_ref, m_ref, v_ref, out_refs):
    g_ref, d_ref, nm_ref, nv_ref = out_refs
    bc1 = 1.0 - ADAM_B1 ** ADAM_STEP
    bc2 = 1.0 - ADAM_B2 ** ADAM_STEP
    nm = ADAM_B1 * m_ref[...] + (1.0 - ADAM_B1) * g
    nv = ADAM_B2 * v_ref[...] + (1.0 - ADAM_B2) * (g * g)
    g_ref[...] = g
    nm_ref[...] = nm
    nv_ref[...] = nv
    d_ref[...] = -ADAM_LR * ((nm * (1.0 / bc1)) / (jnp.sqrt(nv * (1.0 / bc2)) + ADAM_EPS) + ADAM_WD * w_ref[...])


def _adamw_packed(name, sums, offs, ws, ms, vs, dep=None):
    n = len(ws)

    def body(*refs):
        s_ref, w_refs, m_refs, v_refs = refs[0], refs[1:1 + n], refs[1 + n:1 + 2 * n], refs[1 + 2 * n:1 + 3 * n]
        outs = refs[1 + 3 * n:]
        for k in range(n):
            _adam_update(s_ref[offs[k]:offs[k + 1], :], w_refs[k], m_refs[k], v_refs[k], outs[4 * k:4 * k + 4])

    ins = [sums, *ws, *ms, *vs]
    shapes = [jax.ShapeDtypeStruct(w.shape, F32) for w in ws for _ in range(4)]
    return _pcall(body, name=name, ins=ins, in_specs=[_whole(t) for t in ins], dep=dep,
                  out_shape=shapes, out_specs=[_whole(t) for t in shapes], compiler_params=_params(0))


def _adamw(name, parts, part_block, part_index, w, m, v, tr, row_off=0, n_rows=None, prev=None, dep=None):
    R, C = w.shape
    n_rows = R if n_rows is None else n_rows
    assert n_rows % tr == 0 and row_off % tr == 0

    def body(p_ref, w_ref, m_ref, v_ref, *rest):
        g = p_ref[0].astype(F32)
        for s in range(1, part_block[0]):
            g = g + p_ref[s].astype(F32)
        _adam_update(g, w_ref, m_ref, v_ref, rest[-4:])

    ob = row_off // tr
    row = pl.BlockSpec((tr, C), lambda i: (ob + i, 0))
    out = jax.ShapeDtypeStruct((R, C), F32)
    chained = prev is not None
    return _pcall(
        body, name=name, ins=[parts, w, m, v] + (list(prev) if chained else []), dep=dep, grid=(n_rows // tr,),
        in_specs=[pl.BlockSpec(part_block, part_index), row, row, row] + ([ANY] * 4 if chained else []),
        out_specs=[row, row, row, row], out_shape=[out, out, out, out],
        input_output_aliases={4 + t: t for t in range(4)} if chained else {}, compiler_params=_params(1))


def _sum_parts(name, parts, dep=None):
    def body(p_ref, out_ref):
        g = p_ref[0]
        for s in range(1, parts.shape[0]):
            g = g + p_ref[s]
        out_ref[...] = g

    return _pcall(body, name=name, ins=[parts], in_specs=[_whole(parts)], dep=dep,
                  out_shape=jax.ShapeDtypeStruct(parts.shape[1:], F32), compiler_params=_params(0))


def _place():
    return lax.axis_index("x"), lax.axis_index("y"), lax.axis_index("c")


def _slot(px, py, pc):
    return 4 * px + 2 * py + pc


def _peer(k, x, y, c):
    return x ^ ((k >> 2) & 1), y ^ ((k >> 1) & 1), c ^ (k & 1)


SEND_PEERS = {"exchange": tuple(range(1, NDEV)), "gather": (1, 2, 4, 6), "forward": (2, 4, 6),
              "broadcast": tuple(range(1, NDEV))}


def _n_sems(mode, n_lands):
    return n_lands * (len(SEND_PEERS[mode]) + (mode != "forward"))


def _send_copies(mode, src_refs, land_refs, send_sems, recv_sems):
    x, y, c = _place()
    me = _slot(x, y, c)
    peers = SEND_PEERS[mode]
    remote, local = [], []
    for i, k in enumerate(peers):
        peer = _peer(k, x, y, c)
        for a, land in enumerate(land_refs):
            if mode == "exchange":
                src, dst, to = src_refs[a].at[_slot(*peer)], land.at[me], peer
            elif mode in ("gather", "broadcast"):
                src, dst, to = src_refs[a], land.at[me], peer
            else:
                src = dst = land.at[_slot(*peer)]
                to = (x, y, 1 - c)
            s = a * len(peers) + i
            remote.append(pltpu.make_async_remote_copy(src_ref=src, dst_ref=dst, send_sem=send_sems.at[s],
                                                       recv_sem=recv_sems.at[s], device_id=to, device_id_type=MESH))
    if mode != "forward":
        for a, land in enumerate(land_refs):
            src = src_refs[a].at[me] if mode == "exchange" else src_refs[a]
            local.append(pltpu.make_async_copy(src, land.at[me], send_sems.at[len(land_refs) * len(peers) + a]))
    return remote, local


def _send_start_groups(name, groups, mode, collective_id=None, dep=None):
    sizes = [(len(s), len(l)) for s, l in groups]
    flat = [t for s, l in groups for t in (*s, *l)]
    n_in, ng = len(flat), len(groups)

    def body(*refs):
        sems, token, at = refs[n_in:n_in + 2 * ng], refs[-1], 0
        if collective_id is not None:
            x, y, c = _place()
            barrier = pltpu.get_barrier_semaphore()
            shake = (1,) if mode == "forward" else SEND_PEERS[mode]
            for k in shake:
                pl.semaphore_signal(barrier, inc=1, device_id=_peer(k, x, y, c), device_id_type=MESH)
            pl.semaphore_wait(barrier, len(shake))
        for gi, (ns, nl) in enumerate(sizes):
            remote, local = _send_copies(mode, refs[at:at + ns], refs[at + ns:at + ns + nl], sems[2 * gi],
                                         sems[2 * gi + 1])
            for cp in remote + local:
                cp.start()
            at += ns + nl
        token[...] = jnp.zeros_like(token)

    sem_shapes = [pltpu.SemaphoreType.DMA((_n_sems(mode, nl),)) for _, nl in sizes for _ in range(2)]
    if any(dep is t for t in flat):
        dep = None
    out = _pcall(
        body, name=name, ins=[pltpu.with_memory_space_constraint(t, pltpu.HBM) for t in flat],
        in_specs=[HBM] * n_in, dep=dep,
        out_shape=(*sem_shapes, *[pltpu.HBM(t.shape, t.dtype) for t in flat], jax.ShapeDtypeStruct((8, LANES), F32)),
        out_specs=(*[SEM] * (2 * ng), *[HBM] * n_in, pl.BlockSpec(memory_space=pltpu.VMEM)),
        input_output_aliases={i: 2 * ng + i for i in range(n_in)},
        compiler_params=pltpu.CompilerParams(has_side_effects=pltpu.SideEffectType.DATAFLOW_SIDE_EFFECTING,
                                             collective_id=collective_id))
    started, at = [], 2 * ng
    for gi, (ns, nl) in enumerate(sizes):
        started.append((out[-1], out[2 * gi], out[2 * gi + 1], list(out[at:at + ns]), list(out[at + ns:at + ns + nl])))
        at += ns + nl
    return started


def _send_start(name, srcs, lands, mode, collective_id=None, dep=None):
    if mode == "forward":
        collective_id = SIBLING_COLLECTIVE_ID
    return _send_start_groups(name, [(srcs, lands)], mode, collective_id=collective_id, dep=dep)[0]


def _send_wait(name, started, mode, dep=None):
    _, send_sems, recv_sems, srcs, lands = started
    n_src, n = len(srcs), len(lands)

    def body(*refs):
        src_refs, land_refs = refs[:n_src], refs[n_src:n_src + n]
        ssem, rsem = refs[n_src + n], refs[n_src + n + 1]
        remote, local = _send_copies(mode, src_refs, land_refs, ssem, rsem)
        for cp in remote:
            cp.wait_send()
            cp.wait_recv()
        for cp in local:
            cp.wait()

    thru = [pltpu.HBM(t.shape, t.dtype) for t in [*srcs, *lands]]
    out = _pcall(
        body, name=name, ins=[*srcs, *lands, send_sems, recv_sems], in_specs=[HBM] * (n_src + n) + [SEM, SEM], dep=dep,
        out_shape=tuple(thru), out_specs=tuple([HBM] * (n_src + n)),
        input_output_aliases={i: i for i in range(n_src + n)},
        compiler_params=pltpu.CompilerParams(has_side_effects=pltpu.SideEffectType.DATAFLOW_SIDE_EFFECTING))
    return list(out[n_src:])


def _landing(block):
    return lax.empty((NDEV, *block.shape), block.dtype)


def _rows128(t):
    flat = t.reshape(-1)
    n = flat.shape[0]
    rows = -(-n // (8 * LANES)) * 8
    return jnp.pad(flat, (0, rows * LANES - n)).reshape(rows, LANES)


def kernel(x, mix_norm_g, ffn_norm_g, a_w_in, a_ln_g, a_w_spatial, a_b_spatial, a_w_out, kv_norm_g, w_k, w_v, b_w_q, b_sinks, b_w_o, rel_bias, ffn_w1, ffn_w2, final_norm_g, loss_target, m_mix_norm_g, m_ffn_norm_g, m_a_w_in, m_a_ln_g, m_a_w_spatial, m_a_b_spatial, m_a_w_out, m_kv_norm_g, m_w_k, m_w_v, m_b_w_q, m_b_sinks, m_b_w_o, m_rel_bias, m_ffn_w1, m_ffn_w2, m_final_norm_g, v_mix_norm_g, v_ffn_norm_g, v_a_w_in, v_a_ln_g, v_a_w_spatial, v_a_b_spatial, v_a_w_out, v_kv_norm_g, v_w_k, v_w_v, v_b_w_q, v_b_sinks, v_b_w_o, v_rel_bias, v_ffn_w1, v_ffn_w2, v_final_norm_g):
    c = _config(x, a_w_in, a_w_out, w_k, b_w_q, b_w_o, ffn_w1, ffn_w2)
    S, D, LA, LB, LF = c.S, c.D, c.LA, c.LB, c.LF
    weights = dict(mix_norm_g=mix_norm_g, ffn_norm_g=ffn_norm_g, a_w_in=a_w_in, a_ln_g=a_ln_g, a_w_spatial=a_w_spatial,
                   a_b_spatial=a_b_spatial, a_w_out=a_w_out, kv_norm_g=kv_norm_g, w_k=w_k, w_v=w_v, b_w_q=b_w_q,
                   b_sinks=b_sinks, b_w_o=b_w_o, rel_bias=rel_bias, ffn_w1=ffn_w1, ffn_w2=ffn_w2,
                   final_norm_g=final_norm_g)
    m_in = dict(mix_norm_g=m_mix_norm_g, ffn_norm_g=m_ffn_norm_g, a_w_in=m_a_w_in, a_ln_g=m_a_ln_g,
                a_w_spatial=m_a_w_spatial, a_b_spatial=m_a_b_spatial, a_w_out=m_a_w_out, kv_norm_g=m_kv_norm_g,
                w_k=m_w_k, w_v=m_w_v, b_w_q=m_b_w_q, b_sinks=m_b_sinks, b_w_o=m_b_w_o, rel_bias=m_rel_bias,
                ffn_w1=m_ffn_w1, ffn_w2=m_ffn_w2, final_norm_g=m_final_norm_g)
    v_in = dict(mix_norm_g=v_mix_norm_g, ffn_norm_g=v_ffn_norm_g, a_w_in=v_a_w_in, a_ln_g=v_a_ln_g,
                a_w_spatial=v_a_w_spatial, a_b_spatial=v_a_b_spatial, a_w_out=v_a_w_out, kv_norm_g=v_kv_norm_g,
                w_k=v_w_k, w_v=v_w_v, b_w_q=v_b_w_q, b_sinks=v_b_sinks, b_w_o=v_b_w_o, rel_bias=v_rel_bias,
                ffn_w1=v_ffn_w1, ffn_w2=v_ffn_w2, final_norm_g=v_final_norm_g)
    names = list(weights)
    seq = _Seq()
    me = _slot(*_place())
    bf = lambda t: t.astype(BF16)

    tr = lambda t: bf(jnp.swapaxes(t, -1, -2))

    def start(tag, some):
        got = seq(_send_start_groups, f"weights_start_{tag}", [(grp, [_landing(t) for t in grp]) for grp in some],
                  "gather")
        seq.last = got[0][0]
        return got

    started = start("first", [[tr(a_w_in[0])[None], a_ln_g]])
    _, a_w_in_, a_w_out_, ffn_w1_, ffn_w2_, w_k_, w_v_, b_w_q_, b_w_o_ = lax.optimization_barrier(
        (started[0][0], a_w_in, a_w_out, ffn_w1, ffn_w2, w_k, w_v, b_w_q, b_w_o))
    groups = []
    for l in range(LA):
        groups += [[tr(a_w_in_[l])[None]], [bf(a_w_out_[l])], [tr(ffn_w1_[l])[None], bf(ffn_w2_[l])]]
    gb = 3 * LA
    for l in range(LB):
        extra = [bf(jnp.concatenate([w_k_, w_v_], axis=1))] if l == 0 else []
        groups += [extra + [bf(_slot_cols(b_w_q_[l])), bf(b_w_o_[l])], [tr(ffn_w1_[LA + l])[None], bf(ffn_w2_[LA + l])]]
    started += start("rest", groups[1:])
    forwarding = {}

    def forward(i):
        lands = seq(_send_wait, f"weights_wait{i}", started[i], "gather")
        forwarding[i] = seq(_send_start, f"weights_forward{i}", [], lands, "forward")

    def arrive(i):
        if i not in forwarding:
            forward(i)
        return seq(_send_wait, f"weights_arrive{i}", forwarding[i], "forward")

    causal = jnp.tril(jnp.ones((CHUNK, CHUNK), bool))
    wsp = jnp.where(causal[None, None], a_w_spatial, 0.0)
    wsp16 = wsp.astype(BF16)
    wsp16_t = jnp.swapaxes(wsp, -1, -2).astype(BF16)
    bsp_t = jnp.swapaxes(a_b_spatial, -1, -2)
    mix_g = mix_norm_g.reshape(-1, 1, D)
    ffn_g = ffn_norm_g.reshape(-1, 1, D)
    kv_g = kv_norm_g.reshape(1, D)
    fin_g = final_norm_g.reshape(1, D)
    onehot = _bucket_onehot()
    bias = _slot_bias(seq(_band_bias, rel_bias.T, onehot).reshape(N_HEADS, BLOCK, 2 * BLOCK))

    h = x.reshape(S, D)
    sav_a, sav_b, wts_a, wts_b = [], [], [], []
    for l in range(LA):
        got = arrive(3 * l)
        w_in = got[0]
        if l == 0:
            ln_g_full = jnp.transpose(got[1], (1, 0, 2)).reshape(LA, 1, c.AW)
        z, a, hn = seq(_a_in_fwd, c, f"a_in_fwd{l}", h, mix_g[l], w_in, 0)
        forward(3 * l + 1)
        gated = seq(_sgu_fwd, c, f"sgu_fwd{l}", a, ln_g_full[l], wsp16[l], bsp_t[l])
        (wout,) = arrive(3 * l + 1)
        if l > 0:
            forward(3 * l + 2)
        h1 = seq(_mm_res, c, f"a_out_fwd{l}", gated, wout, c.ar, 0, h)
        w1, rows = arrive(3 * l + 2)
        if l == LA - 1:
            forward(gb)
        p, h2, hnf = seq(_ffn_fwd, c, f"ffn_fwd{l}", h1, ffn_g[l], w1, 0, rows)
        sav_a.append((h, z, a, hn, gated, h1, p, hnf))
        wts_a.append((w_in, 0, w1, 0, rows, wout, 0))
        h = h2
    h_kv = h
    for l in range(LB):
        got = arrive(gb + 2 * l)
        if l == 0:
            wkv, got = got[0], got[1:]
            kv, hkv = seq(_rms_mm_rows, c, "kv_fwd", h, kv_g, wkv, c.kr, 0, 2 * c.DKV)
        wq, wo = got[0], _slot_rows(got[1])
        q, hn = seq(_rms_mm_rows, c, f"q_fwd{l}", h, mix_g[LA + l], wq, c.qr, 0, c.DQ)
        forward(gb + 2 * l + 1)
        o, probs = seq(_attn_fwd, c, f"attn_fwd{l}", q, kv, bias, b_sinks[l])
        h1 = seq(_mm_res, c, f"o_fwd{l}", o, wo, c.orr, 0, h)
        w1, rows = arrive(gb + 2 * l + 1)
        if l + 1 < LB:
            forward(gb + 2 * l + 2)
        p, h2, hnf = seq(_ffn_fwd, c, f"ffn_fwd{LA + l}", h1, ffn_g[LA + l], w1, 0, rows)
        sav_b.append((h, q, hn, o, probs, h1, p, hnf))
        wts_b.append((wq, wo, w1, rows))
        h = h2
    dh, d_fin_g, loss_row = seq(_final_loss, c, h, fin_g, loss_target.reshape(S, D))

    results = {}
    in_flight, exchanges = [], []

    def update(k, parts, layer, col_blk=0):
        w = weights[k]
        rows_l, ncols = (w.shape[-2], w.shape[-1]) if w.ndim == 3 else w.shape
        flat = lambda t: t.reshape(-1, ncols)
        tr = min(256, rows_l)
        results[k] = seq(_adamw, f"adamw_{k}{layer}", parts, (NDEV, tr, ncols), lambda i: (0, i, col_blk),
                         flat(w), flat(m_in[k]), flat(v_in[k]), tr, row_off=layer * rows_l, n_rows=rows_l,
                         prev=results.get(k))

    def land(tag, entry):
        lands = seq(_send_wait, f"grads_wait_{tag}", entry[1], "exchange")
        for keys, parts in zip(entry[0], lands):
            for k, layer, col_blk in keys:
                update(k, parts, layer, col_blk)

    def send(tag, items):
        slabs = [t for _, t in items]
        own = [lax.empty(t.shape, t.dtype) for t in slabs]
        exchanges.append(tag)
        st = seq(_send_start, f"grads_start_{tag}", slabs, own, "exchange",
                 collective_id=SIBLING_COLLECTIVE_ID + len(exchanges))
        in_flight.append((tag, ([k for k, _ in items], st)))
        while len(in_flight) > EXCHANGE_LAG:
            land(*in_flight.pop(0))

    d_mix_g, d_ffn_g = [None] * LF, [None] * LF
    dkv_list, dbias_list, dsink_list = [], [], [None] * LB

    def ffn_bwd(lf, dh, h1, p, hnf, w1, w1_i, rows):
        da, dh1, d_ffn_g[lf], dhb = seq(_ffn_bwd_data, c, f"ffn_bwd_data{lf}", dh, p, w1, w1_i, rows, h1, ffn_g[lf])
        dw1, dw2 = seq(_ffn_bwd_w, c, f"ffn_bwd_w{lf}", hnf, da, p, dhb)
        send(f"ffn{lf}", [([("ffn_w1", lf, 0)], dw1), ([("ffn_w2", lf, 0)], dw2)])
        return dh1

    for l in reversed(range(LB)):
        h0, q, hn, o, probs, h1, p, hnf = sav_b[l]
        wq, wo, w1, rows = wts_b[l]
        dh1 = ffn_bwd(LA + l, dh, h1, p, hnf, w1, 0, rows)
        do = seq(_bwd_rows_data, c, f"o_bwd_data{l}", dh1, wo, c.orr, 0)
        dwo = seq(_wgrad_rows, c, f"o_bwd_w{l}", o, [dh1], c.DQ, D, unslot="rows")
        dq, dkv, dbias, dsink = seq(_attn_bwd, c, f"attn_bwd{l}", q, kv, do, probs)
        dsink_list[l] = dsink.reshape(Q_PER_KV, 2, KV_PAIRS).transpose(2, 1, 0).reshape(1, N_HEADS)
        dkv_list.append(dkv)
        dbias_list.append(dbias.reshape(N_HEADS, -1))
        dwq = seq(_wgrad_rows, c, f"q_bwd_w{l}", hn, [dq], D, c.DQ, unslot="cols")
        send(f"attn{l}", [([("b_w_o", l, 0)], dwo), ([("b_w_q", l, 0)], dwq)])
        dh, d_mix_g[LA + l] = seq(_bwd_rows_to_stream, c, f"q_bwd_data{l}", [dq], wq, c.qr, 0, c.DQ, h0,
                                  mix_g[LA + l], dh1)
    dwkv = seq(_wgrad_rows, c, "kv_bwd_w", hkv, dkv_list, D, 2 * c.DKV)
    send("kv", [([("w_k", 0, 0), ("w_v", 0, 1)], dwkv)])
    dh, d_kv_g = seq(_bwd_rows_to_stream, c, "kv_bwd_data", dkv_list, wkv, c.kr, 0, 2 * c.DKV, h_kv, kv_g, dh)
    d_rel_t = seq(_band_bias_grad, dbias_list, onehot)
    d_rel_t = d_rel_t.reshape(KV_PAIRS, Q_PER_KV, 2, N_BUCKETS).swapaxes(1, 2).reshape(N_HEADS, N_BUCKETS)
    d_wsp, d_bsp, d_lng = [None] * LA, [None] * LA, [None] * LA
    for l in reversed(range(LA)):
        h0, z, a, hn, gated, h1, p, hnf = sav_a[l]
        w_in, in_i, w1, w1_i, rows, wout, wout_i = wts_a[l]
        dh1 = ffn_bwd(l, dh, h1, p, hnf, w1, w1_i, rows)
        dgated = seq(_bwd_rows_data, c, f"a_out_bwd_data{l}", dh1, wout, c.ar, wout_i)
        dwout = seq(_wgrad_rows, c, f"a_out_bwd_w{l}", gated, [dh1], c.AW, D)
        send(f"a_out{l}", [([("a_w_out", l, 0)], dwout)])
        dz, d_wsp[l], dbt, d_lng[l] = seq(_sgu_bwd, c, f"sgu_bwd{l}", a, z, dgated, ln_g_full[l], wsp16[l],
                                          wsp16_t[l], bsp_t[l])
        d_bsp[l] = dbt.T
        dwin = seq(_wgrad_cols, c, f"a_in_bwd_w{l}", hn, dz)
        send(f"a_in{l}", [([("a_w_in", l, 0)], dwin)])
        dh, d_mix_g[l] = seq(_bwd_cols_to_stream, c, f"a_in_bwd_data{l}", dz, w_in, in_i, h0, mix_g[l], dh1)
    grad_x = dh.reshape(1, S, D)

    small = {
        "mix_norm_g": jnp.concatenate(d_mix_g, axis=0), "ffn_norm_g": jnp.concatenate(d_ffn_g, axis=0),
        "a_w_spatial": jnp.stack(d_wsp), "a_b_spatial": jnp.stack(d_bsp), "kv_norm_g": d_kv_g,
        "b_sinks": jnp.concatenate(dsink_list, axis=0), "rel_bias": d_rel_t.T, "final_norm_g": d_fin_g,
    }
    small_names = list(small)
    packs = [_rows128(small[k]) for k in small_names] + [_rows128(jnp.concatenate(d_lng, axis=0)), _rows128(loss_row)]
    offs = [int(o) for o in np.cumsum([0] + [p.shape[0] for p in packs])]
    Rs = offs[-1] + (-offs[-1]) % (8 * NDEV)
    packed = jnp.concatenate(packs + [jnp.zeros((Rs - offs[-1], LANES), F32)], axis=0)
    slab = packed.reshape(NDEV, Rs // NDEV, LANES)
    st = seq(_send_start, "small_grads_start", [slab], [lax.empty(slab.shape, slab.dtype)], "exchange")
    while len(in_flight) > 1:
        land(*in_flight.pop(0))
    (parts,) = seq(_send_wait, "small_grads_wait", st, "exchange")
    mine = seq(_sum_parts, "small_grads_sum", parts)
    st = seq(_send_start, "small_sums_start", [mine], [_landing(mine)], "broadcast")
    while in_flight:
        land(*in_flight.pop(0))
    (sums,) = seq(_send_wait, "small_sums_wait", st, "broadcast")
    sums = sums.reshape(Rs, LANES)
    loss = sums[offs[-2], 0]

    grads, deltas, new_m, new_v = {}, {}, {}, {}

    def put(k, outs, shape):
        grads[k], deltas[k], new_m[k], new_v[k] = (t.reshape(shape) for t in outs)

    outs = seq(_adamw_packed, "adamw_small", sums, offs, *[[_rows128(d[k]) for k in small_names]
                                                          for d in (weights, m_in, v_in)])
    for n_, k in enumerate(small_names):
        shape = weights[k].shape
        size = int(np.prod(shape))
        put(k, [t.reshape(-1)[:size] for t in outs[4 * n_:4 * n_ + 4]], shape)
    lng_sum = sums[offs[-3]:offs[-2]].reshape(-1)[:LA * c.AW].reshape(LA, c.AW)
    lng_mine = lax.dynamic_slice_in_dim(lng_sum, me * c.ar, c.ar, axis=1)
    lng_parts = jnp.concatenate([lng_mine[None], jnp.zeros((NDEV - 1, LA, c.ar), F32)], axis=0)
    put("a_ln_g", seq(_adamw, "adamw_ln_g", lng_parts, (NDEV, LA, c.ar), lambda i: (0, 0, 0),
                      a_ln_g, m_in["a_ln_g"], v_in["a_ln_g"], LA), a_ln_g.shape)
    for k in ("a_w_in", "ffn_w1", "ffn_w2", "a_w_out", "b_w_o", "b_w_q", "w_k", "w_v"):
        put(k, results[k], weights[k].shape)

    return (loss, grad_x, *[grads[k] for k in names], *[deltas[k] for k in names],
            *[new_m[k] for k in names], *[new_v[k] for k in names])
```

```python
import numpy as np
import math
import jax
import jax.numpy as jnp
from jax import lax
from jax.experimental import pallas as pl
from jax.experimental.pallas import tpu as pltpu

F32 = jnp.float32
BF16 = jnp.bfloat16

NDEV = 8
EPS = 1e-6
CHUNK = 128
A_GROUPS = 8
N_HEADS = 16
N_KV_HEADS = 4
Q_PER_KV = N_HEADS // N_KV_HEADS
HEAD_DIM = 64
BLOCK = 128
N_BUCKETS = 32
MAX_DISTANCE = 128
ADAM_LR, ADAM_B1, ADAM_B2, ADAM_EPS, ADAM_WD, ADAM_STEP = 0.001, 0.9, 0.999, 1e-08, 0.01, 10
LANES = 128
VMEM_LIMIT = 56 * 1024 * 1024
INV_SQRT2 = 0.7071067811865476
INV_SQRT_2PI = 0.3989422804014327
EXCHANGE_LAG = 4
SIBLING_COLLECTIVE_ID = 0

HBM = pl.BlockSpec(memory_space=pltpu.HBM)
SMEM = pl.BlockSpec(memory_space=pltpu.SMEM)
ANY = pl.BlockSpec(memory_space=pl.ANY)
SEM = pl.BlockSpec(memory_space=pltpu.SEMAPHORE)
MESH = pl.DeviceIdType.MESH


def _params(n_grid):
    return pltpu.CompilerParams(dimension_semantics=("arbitrary",) * n_grid, vmem_limit_bytes=VMEM_LIMIT)


def _const(block, index_map):
    return pl.BlockSpec(block, index_map, pipeline_mode=pl.Buffered(1))


def _pcall(body, *, ins, in_specs, dep=None, **kw):
    n_in = len(ins)
    if dep is None or any(dep is t for t in ins):
        return pl.pallas_call(body, in_specs=list(in_specs), **kw)(*ins)

    def with_dep(*refs):
        body(*refs[:n_in], *refs[n_in + 1:])

    return pl.pallas_call(with_dep, in_specs=[*in_specs, ANY], **kw)(*ins, dep)


class _Seq:
    def __init__(self):
        self.last = None

    def __call__(self, fn, *args, **kw):
        out = fn(*args, dep=self.last, **kw)
        self.last = out[0] if isinstance(out, (tuple, list)) else out
        return out


def _rstd(h):
    return lax.rsqrt(jnp.mean(h * h, axis=-1, keepdims=True) + EPS)


def _rms_bwd(dhn, h, g, dres):
    r = _rstd(h)
    xh = h * r
    dg = jnp.sum(dhn * xh, axis=0, keepdims=True)
    dxh = dhn * g
    dx = r * (dxh - xh * jnp.mean(dxh * xh, axis=-1, keepdims=True))
    return dres + dx, dg


def _gelu(z):
    return 0.5 * z * (1.0 + lax.erf(z * INV_SQRT2))


def _gelu_grad(z):
    return 0.5 * (1.0 + lax.erf(z * INV_SQRT2)) + z * (jnp.exp(-0.5 * z * z) * INV_SQRT_2PI)


def _dot(a, b, dims):
    return lax.dot_general(a, b, (dims, ((), ())), preferred_element_type=F32)


NN = ((1,), (0,))
NT = ((1,), (1,))
TN = ((0,), (0,))


def _mm(name, ins, in_specs, out_shapes, out_specs, *, grid, dims, nk, acc_shape, load_a, load_b, epilogue,
        dep=None):
    n_in, n_out = len(ins), len(out_shapes)
    kax = len(grid) - 1

    def body(*refs):
        in_refs = refs[:n_in]
        out_refs = refs[n_in:n_in + n_out]
        a = load_a(in_refs, out_refs)
        b = load_b(in_refs)
        prod = _dot(a, b, dims)
        if nk == 1:
            epilogue(prod, in_refs, out_refs)
        else:
            acc = refs[n_in + n_out]
            k = pl.program_id(kax)

            @pl.when(k == 0)
            def _():
                acc[...] = prod

            @pl.when(k > 0)
            def _():
                acc[...] += prod

            @pl.when(k == nk - 1)
            def _():
                epilogue(acc[...], in_refs, out_refs)

    return _pcall(
        body, name=name, ins=ins, in_specs=in_specs, dep=dep, grid=grid, out_specs=out_specs, out_shape=out_shapes,
        scratch_shapes=[pltpu.VMEM(acc_shape, F32)] if nk > 1 else [], compiler_params=_params(len(grid)))


def _bf(ref_idx):
    return lambda in_refs, *_: in_refs[ref_idx][...].astype(BF16)


def _b_view(ref_idx, rows):
    def load(in_refs):
        b = in_refs[ref_idx][...]
        return b.reshape(rows, b.shape[-1])
    return load


class Cfg:
    pass


def _config(x, a_w_in, a_w_out, w_k, b_w_q, b_w_o, ffn_w1, ffn_w2):
    c = Cfg()
    c.S, c.D = x.shape[1], x.shape[2]
    c.LA, _, c.cw = a_w_in.shape
    c.AW2 = NDEV * c.cw
    c.AW = c.AW2 // 2
    c.gd = c.AW // A_GROUPS
    c.ar = a_w_out.shape[1]
    c.LF, _, c.fw = ffn_w1.shape
    c.fr = ffn_w2.shape[1]
    c.LB, c.qr, c.DQ = b_w_q.shape
    c.orr = b_w_o.shape[1]
    c.kr, c.DKV = w_k.shape
    c.tm = min(1024, c.S)
    c.tmw = min(512, c.S)
    c.tms = min(512, c.S)
    c.nb = c.S // BLOCK
    assert c.cw == c.fw == c.fr and c.AW == NDEV * c.ar and c.D == NDEV * c.qr == NDEV * c.kr
    assert c.DQ == NDEV * c.orr == N_HEADS * HEAD_DIM and c.DKV == N_KV_HEADS * HEAD_DIM
    assert c.S % c.tm == 0 and c.S % c.tmw == 0 and c.S % c.tms == 0 and c.tms % CHUNK == 0 and c.gd % LANES == 0
    assert c.LA >= 1 and c.LB >= 1 and c.LF == c.LA + c.LB
    return c


def _cached_rms(h_idx, g_idx, hn_out_idx, jax_axis=1):
    def load(in_refs, out_refs):
        hn_ref = out_refs[hn_out_idx]

        @pl.when(pl.program_id(jax_axis) == 0)
        def _():
            h = in_refs[h_idx][...]
            hn_ref[...] = (h * _rstd(h) * in_refs[g_idx][...]).astype(BF16)

        return hn_ref[...]
    return load


def _a_in_fwd(c, name, h, g, col, ci, dep=None):
    S, D, cw, tm = c.S, c.D, c.cw, c.tmw

    def body(h_ref, g_ref, w_ref, z_ref, a_ref, hn_ref):
        h = h_ref[...]
        hn = (h * _rstd(h) * g_ref[...]).astype(BF16)
        hn_ref[...] = hn
        for j in range(NDEV):
            cols = slice(j * cw, (j + 1) * cw)
            z = _dot(hn, w_ref[j], NT)
            z_ref[:, cols] = z.astype(BF16)
            a_ref[:, cols] = _gelu(z).astype(BF16)

    row = pl.BlockSpec((tm, D), lambda i: (i, 0))
    wide = pl.BlockSpec((tm, c.AW2), lambda i: (i, 0))
    return _pcall(
        body, name=name, ins=[h, g, col], dep=dep, grid=(S // tm,),
        in_specs=[row, pl.BlockSpec((1, D), lambda i: (0, 0)), _const((NDEV, None, cw, D), lambda i: (0, ci, 0, 0))],
        out_specs=[wide, wide, row],
        out_shape=[jax.ShapeDtypeStruct((S, c.AW2), BF16), jax.ShapeDtypeStruct((S, c.AW2), BF16),
                   jax.ShapeDtypeStruct((S, D), BF16)],
        compiler_params=_params(1))


def _rms_mm_rows(c, name, h, g, slab, blk_rows, blk_idx, n_out, dep=None):
    S, D, tm = c.S, c.D, c.tm

    def epilogue(acc, in_refs, out_refs):
        out_refs[0][...] = acc.astype(BF16)

    return _mm(
        name, [h, slab, g],
        [pl.BlockSpec((tm, D), lambda i, j, k: (i, 0)),
         pl.BlockSpec((NDEV, blk_rows, n_out), lambda i, j, k: (0, blk_idx, 0)),
         pl.BlockSpec((1, D), lambda i, j, k: (0, 0))],
        [jax.ShapeDtypeStruct((S, n_out), BF16), jax.ShapeDtypeStruct((S, D), BF16)],
        [pl.BlockSpec((tm, n_out), lambda i, j, k: (i, 0)), pl.BlockSpec((tm, D), lambda i, j, k: (i, 0))],
        grid=(S // tm, 1, 1), dims=NN, nk=1, acc_shape=None,
        load_a=_cached_rms(0, 2, 1), load_b=_b_view(1, NDEV * blk_rows), epilogue=epilogue, dep=dep)


def _mm_res(c, name, a, slab, blk_rows, blk_idx, res, dep=None):
    S, D, tm = c.S, c.D, c.tm
    K = NDEV * blk_rows

    def epilogue(acc, in_refs, out_refs):
        out_refs[0][...] = in_refs[2][...] + acc

    return _mm(
        name, [a, slab, res],
        [pl.BlockSpec((tm, K), lambda i, j, k: (i, 0)),
         pl.BlockSpec((NDEV, blk_rows, D), lambda i, j, k: (0, blk_idx, 0)),
         pl.BlockSpec((tm, D), lambda i, j, k: (i, 0))],
        [jax.ShapeDtypeStruct((S, D), F32)], [pl.BlockSpec((tm, D), lambda i, j, k: (i, 0))],
        grid=(S // tm, 1, 1), dims=NN, nk=1, acc_shape=None,
        load_a=_bf(0), load_b=_b_view(1, K), epilogue=epilogue, dep=dep)[0]


def _sgu_masks():
    ii = lax.broadcasted_iota(jnp.int32, (CHUNK, CHUNK), 0)
    jj = lax.broadcasted_iota(jnp.int32, (CHUNK, CHUNK), 1)
    return ii >= jj


def _sgu_fwd(c, name, a, ln_g, wc, b_t, dep=None):
    S, AW, gd, tm = c.S, c.AW, c.gd, c.tms

    def body(a_ref, lng_ref, wc_ref, bt_ref, out_ref):
        va = a_ref[:, AW:].astype(F32)
        xc = va - jnp.mean(va, axis=-1, keepdims=True)
        vn = (xc * lax.rsqrt(jnp.mean(xc * xc, axis=-1, keepdims=True) + EPS) * lng_ref[...]).astype(BF16)
        for ch in range(tm // CHUNK):
            rows = slice(ch * CHUNK, (ch + 1) * CHUNK)
            for g in range(A_GROUPS):
                cols = slice(g * gd, (g + 1) * gd)
                mixed = _dot(wc_ref[g], vn[rows, cols], NN) + bt_ref[:, g:g + 1]
                out_ref[rows, cols] = (a_ref[rows, cols].astype(F32) * mixed).astype(BF16)

    return _pcall(
        body, name=name, ins=[a, ln_g, wc, b_t], dep=dep, grid=(S // tm,),
        in_specs=[pl.BlockSpec((tm, 2 * AW), lambda i: (i, 0)), pl.BlockSpec((1, AW), lambda i: (0, 0)),
                  pl.BlockSpec((A_GROUPS, CHUNK, CHUNK), lambda i: (0, 0, 0)),
                  pl.BlockSpec((CHUNK, A_GROUPS), lambda i: (0, 0))],
        out_specs=pl.BlockSpec((tm, AW), lambda i: (i, 0)),
        out_shape=jax.ShapeDtypeStruct((S, AW), BF16), compiler_params=_params(1))


def _ffn_fwd(c, name, h, g, col, ci, rows, dep=None):
    S, D, fw, tm = c.S, c.D, c.fw, c.tmw
    F = NDEV * fw

    def body(h_ref, g_ref, w1_ref, w2_ref, p_ref, out_ref, hn_ref, r_ref):
        h = h_ref[...]
        hn = (h * _rstd(h) * g_ref[...]).astype(BF16)
        hn_ref[...] = hn
        for j in range(NDEV):
            cols = slice(j * fw, (j + 1) * fw)
            p = jnp.maximum(_dot(hn, w1_ref[j], NT), 0.0)
            p_ref[:, cols] = p.astype(BF16)
            r_ref[:, cols] = (p * p).astype(BF16)
        out_ref[...] = h + _dot(r_ref[...], w2_ref[...].reshape(F, D), NN)

    row = pl.BlockSpec((tm, D), lambda i: (i, 0))
    return _pcall(
        body, name=name, ins=[h, g, col, rows], dep=dep, grid=(S // tm,),
        in_specs=[row, pl.BlockSpec((1, D), lambda i: (0, 0)),
                  _const((NDEV, None, fw, D), lambda i: (0, ci, 0, 0)), _const((NDEV, c.fr, D), lambda i: (0, 0, 0))],
        out_specs=[pl.BlockSpec((tm, F), lambda i: (i, 0)), row, row],
        out_shape=[jax.ShapeDtypeStruct((S, F), BF16), jax.ShapeDtypeStruct((S, D), F32),
                   jax.ShapeDtypeStruct((S, D), BF16)],
        scratch_shapes=[pltpu.VMEM((tm, F), BF16)], compiler_params=_params(1))


def _bucket_table():
    qi = np.arange(BLOCK)[:, None]
    kj = np.arange(2 * BLOCK)[None, :]
    d = np.maximum(qi + BLOCK - kj, 0)
    max_exact = N_BUCKETS // 2
    ratio = np.log(np.maximum(d, 1).astype(np.float32) / np.float32(max_exact)) / np.float32(
        math.log(MAX_DISTANCE / max_exact))
    large = np.minimum(max_exact + (ratio.astype(np.float32) * np.float32(N_BUCKETS - max_exact)).astype(np.int32),
                       N_BUCKETS - 1)
    return np.where(d < max_exact, d, large).astype(np.int32)


def _bucket_onehot():
    b = jnp.asarray(_bucket_table().reshape(1, -1))
    return (b == lax.broadcasted_iota(jnp.int32, (N_BUCKETS, b.shape[1]), 0)).astype(F32)


def _whole(t):
    return pl.BlockSpec(t.shape, lambda: (0,) * t.ndim)


def _band_bias(rel_bias_t, onehot, dep=None):
    def body(r_ref, oh_ref, out_ref):
        out_ref[...] = lax.dot_general(r_ref[...], oh_ref[...], (NN, ((), ())), preferred_element_type=F32,
                                       precision=lax.Precision.HIGHEST)

    n = onehot.shape[1]
    return _pcall(body, name="band_bias", ins=[rel_bias_t, onehot], in_specs=[_whole(rel_bias_t), _whole(onehot)],
                  dep=dep, out_shape=jax.ShapeDtypeStruct((N_HEADS, n), F32), compiler_params=_params(0))


def _band_bias_grad(dbias_list, onehot, dep=None):
    n_in = len(dbias_list)

    def body(*refs):
        oh_ref, out_ref = refs[n_in], refs[n_in + 1]
        d = refs[0][...]
        for r in refs[1:n_in]:
            d = d + r[...]
        out_ref[...] = lax.dot_general(d, oh_ref[...], (NT, ((), ())), preferred_element_type=F32,
                                       precision=lax.Precision.HIGHEST)

    ins = [*dbias_list, onehot]
    return _pcall(body, name="band_bias_grad", ins=ins, in_specs=[_whole(t) for t in ins], dep=dep,
                  out_shape=jax.ShapeDtypeStruct((N_HEADS, N_BUCKETS), F32), compiler_params=_params(0))


KV_PAIRS = N_KV_HEADS // 2
PAIR_ROWS = 2 * Q_PER_KV * BLOCK
MASKED = float(np.finfo(np.float32).min) / 2


def _slot_cols(w):
    lead = w.shape[:-1]
    return w.reshape(*lead, KV_PAIRS, 2, Q_PER_KV, HEAD_DIM).swapaxes(-3, -2).reshape(*lead, N_HEADS * HEAD_DIM)


def _slot_rows(blocks):
    n = blocks.shape[-1]
    return blocks.reshape(KV_PAIRS, 2, Q_PER_KV, HEAD_DIM, n).swapaxes(1, 2).reshape(blocks.shape)


def _slot_bias(bias):
    qi = np.arange(BLOCK)[:, None]
    kj = np.arange(2 * BLOCK)[None, :]
    dist = qi + BLOCK - kj
    window = (dist >= 0) & (dist < BLOCK)
    b = bias.reshape(KV_PAIRS, 2, Q_PER_KV, BLOCK, 2 * BLOCK).swapaxes(1, 2).reshape(KV_PAIRS, PAIR_ROWS, 2 * BLOCK)
    tile = lambda mk: jnp.asarray(np.tile(mk, (2 * Q_PER_KV, 1)))[None]
    return jnp.stack([jnp.where(tile(window & (kj >= BLOCK)), b, MASKED), jnp.where(tile(window), b, MASKED)])


def _pair_kv(kvc_ref, kvp_ref, kvp, dkv):
    lanes = slice(kvp * LANES, (kvp + 1) * LANES)
    vlanes = slice(dkv + kvp * LANES, dkv + (kvp + 1) * LANES)
    k2 = jnp.concatenate([kvp_ref[:, lanes], kvc_ref[:, lanes]], axis=0)
    v2 = jnp.concatenate([kvp_ref[:, vlanes], kvc_ref[:, vlanes]], axis=0)
    return k2, v2


def _head_operand(ref, grp, par, low, scale=None):
    xg = ref[:, grp * LANES:(grp + 1) * LANES]
    if scale is not None:
        xg = xg * scale
    zero = jnp.zeros_like(xg)
    return jnp.where(low, xg, zero) if par == 0 else jnp.where(low, zero, xg)


def _head_probs(qh, k2, bias_rows, sink):
    s = _dot(qh, k2, NT) + bias_rows
    m = jnp.maximum(jnp.max(s, axis=-1, keepdims=True), sink)
    e = jnp.exp(s - m)
    es = jnp.exp(sink - m)
    inv = 1.0 / (jnp.sum(e, axis=-1, keepdims=True) + es)
    return e * inv, es * inv


def _attn_specs(c):
    dq, dkv2 = c.DQ, 2 * c.DKV
    return [pl.BlockSpec((BLOCK, dq), lambda n: (n, 0)),
            pl.BlockSpec((BLOCK, dkv2), lambda n: (n, 0)),
            pl.BlockSpec((BLOCK, dkv2), lambda n: (jnp.maximum(n - 1, 0), 0))]


def _bias_spec():
    return pl.BlockSpec((None, KV_PAIRS, PAIR_ROWS, 2 * BLOCK), lambda n: (jnp.minimum(n, 1), 0, 0, 0))


def _low_lanes():
    return lax.broadcasted_iota(jnp.int32, (BLOCK, LANES), 1) < HEAD_DIM


def _first_key():
    return lax.broadcasted_iota(jnp.int32, (BLOCK, 2 * BLOCK), 1) == 0


def _probs_spec():
    return pl.BlockSpec((None, KV_PAIRS, PAIR_ROWS, 2 * BLOCK), lambda n: (n, 0, 0, 0))


def _attn_fwd(c, name, q, kv, bias, sinks, dep=None):
    S, dq = c.S, c.DQ

    def body(q_ref, kvc_ref, kvp_ref, bias_ref, sink_ref, o_ref, probs_ref):
        low = _low_lanes()
        first = _first_key()
        for kvp in range(KV_PAIRS):
            k2, v2 = _pair_kv(kvc_ref, kvp_ref, kvp, c.DKV)
            for g in range(Q_PER_KV):
                grp = kvp * Q_PER_KV + g
                halves = []
                for par in range(2):
                    rows = slice((2 * g + par) * BLOCK, (2 * g + par + 1) * BLOCK)
                    qh = _head_operand(q_ref, grp, par, low, scale=HEAD_DIM ** -0.5)
                    p, ps = _head_probs(qh, k2, bias_ref[kvp, rows, :], sink_ref[(2 * kvp + par) * Q_PER_KV + g])
                    probs_ref[kvp, rows, :] = jnp.where(first, ps, p).astype(BF16)
                    halves.append(_dot(p.astype(BF16), v2, NN))
                o_ref[:, grp * LANES:(grp + 1) * LANES] = jnp.where(low, halves[0], halves[1]).astype(BF16)

    return _pcall(
        body, name=name, ins=[q, kv, kv, bias, sinks], dep=dep, grid=(c.nb,),
        in_specs=_attn_specs(c) + [_bias_spec(), SMEM],
        out_specs=[pl.BlockSpec((BLOCK, dq), lambda n: (n, 0)), _probs_spec()],
        out_shape=[jax.ShapeDtypeStruct((S, dq), BF16),
                   jax.ShapeDtypeStruct((c.nb, KV_PAIRS, PAIR_ROWS, 2 * BLOCK), BF16)],
        compiler_params=_params(1))


def _final_loss(c, h, g, target, dep=None):
    S, D, tm = c.S, c.D, c.tm

    def body(h_ref, g_ref, t_ref, dh_ref, dg_ref, loss_ref):
        i = pl.program_id(0)
        h = h_ref[...]
        gg = g_ref[...]
        r = _rstd(h)
        xh = h * r
        err = xh * gg - t_ref[...]
        lp = jnp.sum(jnp.sum(err * err, axis=1, keepdims=True), axis=0, keepdims=True) * (0.5 / D)
        dx, dg = _rms_bwd(err * (1.0 / D), h, gg, 0.0)
        dh_ref[...] = dx

        @pl.when(i == 0)
        def _():
            dg_ref[...] = dg
            loss_ref[...] = jnp.broadcast_to(lp, loss_ref.shape)

        @pl.when(i > 0)
        def _():
            dg_ref[...] += dg
            loss_ref[...] += jnp.broadcast_to(lp, loss_ref.shape)

    row = pl.BlockSpec((tm, D), lambda i: (i, 0))
    return _pcall(
        body, name="final_loss", ins=[h, g, target], dep=dep, grid=(S // tm,),
        in_specs=[row, pl.BlockSpec((1, D), lambda i: (0, 0)), row],
        out_specs=[row, pl.BlockSpec((1, D), lambda i: (0, 0)), pl.BlockSpec((1, LANES), lambda i: (0, 0))],
        out_shape=[jax.ShapeDtypeStruct((S, D), F32), jax.ShapeDtypeStruct((1, D), F32),
                   jax.ShapeDtypeStruct((1, LANES), F32)],
        compiler_params=_params(1))


def _rms_bwd_epilogue(h_idx, g_idx, res_idx):
    def epilogue(dhn, in_refs, out_refs):
        dh, dg = _rms_bwd(dhn, in_refs[h_idx][...], in_refs[g_idx][...], in_refs[res_idx][...])
        out_refs[0][...] = dh
        i = pl.program_id(0)

        @pl.when(i == 0)
        def _():
            out_refs[1][...] = dg

        @pl.when(i > 0)
        def _():
            out_refs[1][...] += dg
    return epilogue


def _stream_outs(c, tm):
    S, D = c.S, c.D
    return ([jax.ShapeDtypeStruct((S, D), F32), jax.ShapeDtypeStruct((1, D), F32)],
            [pl.BlockSpec((tm, D), lambda i, j, k: (i, 0)), pl.BlockSpec((1, D), lambda i, j, k: (0, 0))])


def _row_specs(c, tm):
    D = c.D
    return [pl.BlockSpec((tm, D), lambda i, j, k: (i, 0)), pl.BlockSpec((1, D), lambda i, j, k: (0, 0)),
            pl.BlockSpec((tm, D), lambda i, j, k: (i, 0))]


def _bwd_rows_to_stream(c, name, dy_list, slab, blk_rows, blk_idx, n_in_cols, h, g, dres, dep=None):
    S, D, tm = c.S, c.D, c.tm
    nd = len(dy_list)

    def load_a(in_refs, out_refs):
        a = in_refs[0][...]
        for r in in_refs[1:nd]:
            a = a + r[...]
        return a.astype(BF16)

    shapes, specs = _stream_outs(c, tm)
    return _mm(
        name, [*dy_list, slab, h, g, dres],
        [pl.BlockSpec((tm, n_in_cols), lambda i, j, k: (i, 0))] * nd
        + [pl.BlockSpec((NDEV, blk_rows, n_in_cols), lambda i, j, k: (0, blk_idx, 0))] + _row_specs(c, tm),
        shapes, specs, grid=(S // tm, 1, 1), dims=NT, nk=1, acc_shape=None,
        load_a=load_a, load_b=_b_view(nd, NDEV * blk_rows), epilogue=_rms_bwd_epilogue(nd + 1, nd + 2, nd + 3),
        dep=dep)


def _bwd_q_kv_to_stream(c, name, dq, wq, dkv_list, wkv, h, g_q, g_kv, dres, dep=None):
    S, D, tm = c.S, c.D, c.tm
    nkv = len(dkv_list)

    def body(dq_ref, wq_ref, *rest):
        dkv_refs = rest[:nkv]
        wkv_ref, h_ref, gq_ref, gkv_ref, dres_ref, out_ref, dgq_ref, dgkv_ref = rest[nkv:]
        i = pl.program_id(0)
        dhn_q = _dot(dq_ref[...], wq_ref[...].reshape(D, c.DQ), NT)
        dkv = dkv_refs[0][...]
        for r in dkv_refs[1:]:
            dkv = dkv + r[...]
        dhn_kv = _dot(dkv.astype(BF16), wkv_ref[...].reshape(D, 2 * c.DKV), NT)
        h = h_ref[...]
        r = _rstd(h)
        xh = h * r
        dxh = dhn_q * gq_ref[...] + dhn_kv * gkv_ref[...]
        out_ref[...] = dres_ref[...] + r * (dxh - xh * jnp.mean(dxh * xh, axis=-1, keepdims=True))
        dgq = jnp.sum(dhn_q * xh, axis=0, keepdims=True)
        dgkv = jnp.sum(dhn_kv * xh, axis=0, keepdims=True)

        @pl.when(i == 0)
        def _():
            dgq_ref[...] = dgq
            dgkv_ref[...] = dgkv

        @pl.when(i > 0)
        def _():
            dgq_ref[...] += dgq
            dgkv_ref[...] += dgkv

    row = pl.BlockSpec((tm, D), lambda i: (i, 0))
    gain = pl.BlockSpec((1, D), lambda i: (0, 0))
    return _pcall(
        body, name=name, ins=[dq, wq, *dkv_list, wkv, h, g_q, g_kv, dres], dep=dep, grid=(S // tm,),
        in_specs=[pl.BlockSpec((tm, c.DQ), lambda i: (i, 0)), pl.BlockSpec((NDEV, c.qr, c.DQ), lambda i: (0, 0, 0))]
        + [pl.BlockSpec((tm, 2 * c.DKV), lambda i: (i, 0))] * nkv
        + [pl.BlockSpec((NDEV, c.kr, 2 * c.DKV), lambda i: (0, 0, 0)), row, gain, gain, row],
        out_specs=[row, gain, gain],
        out_shape=[jax.ShapeDtypeStruct((S, D), F32), jax.ShapeDtypeStruct((1, D), F32),
                   jax.ShapeDtypeStruct((1, D), F32)],
        compiler_params=_params(1))


def _bwd_cols_to_stream(c, name, dy, col, ci, h, g, dres, dep=None):
    S, D, cw, tm = c.S, c.D, c.cw, c.tmw
    K = NDEV * cw
    shapes, specs = _stream_outs(c, tm)
    return _mm(
        name, [dy, col, h, g, dres],
        [pl.BlockSpec((tm, K), lambda i, j, k: (i, 0)),
         _const((NDEV, None, cw, D), lambda i, j, k: (0, ci, 0, 0))] + _row_specs(c, tm),
        shapes, specs, grid=(S // tm, 1, 1), dims=NN, nk=1, acc_shape=None,
        load_a=_bf(0), load_b=_b_view(1, K), epilogue=_rms_bwd_epilogue(2, 3, 4), dep=dep)


def _bwd_rows_data(c, name, dy, slab, blk_rows, blk_idx, dep=None):
    S, D, tm = c.S, c.D, c.tm
    K = NDEV * blk_rows

    def epilogue(acc, in_refs, out_refs):
        out_refs[0][...] = acc.astype(BF16)

    return _mm(
        name, [dy, slab],
        [pl.BlockSpec((tm, D), lambda i, j, k: (i, 0)),
         pl.BlockSpec((NDEV, blk_rows, D), lambda i, j, k: (0, blk_idx, 0))],
        [jax.ShapeDtypeStruct((S, K), BF16)], [pl.BlockSpec((tm, K), lambda i, j, k: (i, 0))],
        grid=(S // tm, 1, 1), dims=NT, nk=1, acc_shape=None,
        load_a=_bf(0), load_b=_b_view(1, K), epilogue=epilogue, dep=dep)[0]


def _wgrad_rows(c, name, a, b_list, n_a, n_b, unslot=None, dep=None):
    S, tm = c.S, c.tm
    nb_in = len(b_list)
    blk_rows = n_a // NDEV

    def load_b(in_refs):
        b = in_refs[1][...]
        for r in in_refs[2:1 + nb_in]:
            b = b + r[...]
        return b.astype(BF16)

    def epilogue(acc, in_refs, out_refs):
        out = out_refs[0]
        if unslot is None:
            out[...] = acc.reshape(NDEV, blk_rows, n_b).astype(BF16)
            return
        per_blk = blk_rows // HEAD_DIM
        for h in range(N_HEADS):
            s = (h // (2 * Q_PER_KV)) * 2 * Q_PER_KV + (h % Q_PER_KV) * 2 + (h // Q_PER_KV) % 2
            if unslot == "rows":
                out[h // per_blk, (h % per_blk) * HEAD_DIM:(h % per_blk + 1) * HEAD_DIM, :] = (
                    acc[s * HEAD_DIM:(s + 1) * HEAD_DIM, :].astype(BF16))
            else:
                out[:, :, h * HEAD_DIM:(h + 1) * HEAD_DIM] = (
                    acc[:, s * HEAD_DIM:(s + 1) * HEAD_DIM].reshape(NDEV, blk_rows, HEAD_DIM).astype(BF16))

    return _mm(
        name, [a, *b_list],
        [pl.BlockSpec((tm, n_a), lambda i, j, k: (k, 0))] + [pl.BlockSpec((tm, n_b), lambda i, j, k: (k, 0))] * nb_in,
        [jax.ShapeDtypeStruct((NDEV, blk_rows, n_b), BF16)],
        [pl.BlockSpec((NDEV, blk_rows, n_b), lambda i, j, k: (0, 0, 0))],
        grid=(1, 1, S // tm), dims=TN, nk=S // tm, acc_shape=(n_a, n_b),
        load_a=_bf(0), load_b=load_b, epilogue=epilogue, dep=dep)[0]


def _wgrad_cols(c, name, a, b, dep=None):
    S, D, cw = c.S, c.D, c.cw

    def body(a_ref, b_ref, out_ref):
        out_ref[...] = _dot(a_ref[...], b_ref[...], TN).astype(BF16)

    return _pcall(
        body, name=name, ins=[a, b], dep=dep, grid=(NDEV,),
        in_specs=[_const((S, D), lambda j: (0, 0)), pl.BlockSpec((S, cw), lambda j: (0, j))],
        out_specs=pl.BlockSpec((None, D, cw), lambda j: (j, 0, 0)),
        out_shape=jax.ShapeDtypeStruct((NDEV, D, cw), BF16), compiler_params=_params(1))


def _ffn_bwd_data(c, name, dh, p, col, ci, rows, h, g, dep=None):
    S, D, fw, tm = c.S, c.D, c.fw, c.tmw
    F = NDEV * fw

    def body(dh_ref, p_ref, w1t_ref, w2_ref, h_ref, g_ref, da_ref, out_ref, dg_ref, dhb_ref):
        i = pl.program_id(0)
        dh = dh_ref[...]
        dhb = dh.astype(BF16)
        dhb_ref[...] = dhb
        for j in range(NDEV):
            cols = slice(j * fw, (j + 1) * fw)
            da_ref[:, cols] = (_dot(dhb, w2_ref[j], NT) * (2.0 * p_ref[:, cols].astype(F32))).astype(BF16)
        dx, dg = _rms_bwd(_dot(da_ref[...], w1t_ref[...].reshape(F, D), NN), h_ref[...], g_ref[...], dh)
        out_ref[...] = dx

        @pl.when(i == 0)
        def _():
            dg_ref[...] = dg

        @pl.when(i > 0)
        def _():
            dg_ref[...] += dg

    row = pl.BlockSpec((tm, D), lambda i: (i, 0))
    wide = pl.BlockSpec((tm, F), lambda i: (i, 0))
    return _pcall(
        body, name=name, ins=[dh, p, col, rows, h, g], dep=dep, grid=(S // tm,),
        in_specs=[row, wide, _const((NDEV, None, fw, D), lambda i: (0, ci, 0, 0)),
                  _const((NDEV, c.fr, D), lambda i: (0, 0, 0)),
                  row, pl.BlockSpec((1, D), lambda i: (0, 0))],
        out_specs=[wide, row, pl.BlockSpec((1, D), lambda i: (0, 0)), row],
        out_shape=[jax.ShapeDtypeStruct((S, F), BF16), jax.ShapeDtypeStruct((S, D), F32),
                   jax.ShapeDtypeStruct((1, D), F32), jax.ShapeDtypeStruct((S, D), BF16)],
        compiler_params=_params(1))


def _ffn_bwd_w(c, name, hn, da, p, dhb, dep=None):
    S, D, fw = c.S, c.D, c.fw

    def body(hn_ref, da_ref, p_ref, dhb_ref, dw1_ref, dw2_ref):
        dw1_ref[...] = _dot(hn_ref[...], da_ref[...], TN).astype(BF16)
        pf = p_ref[...].astype(F32)
        dw2_ref[...] = _dot((pf * pf).astype(BF16), dhb_ref[...], TN).astype(BF16)

    panel = pl.BlockSpec((S, fw), lambda j: (0, j))
    return _pcall(
        body, name=name, ins=[hn, da, p, dhb], dep=dep, grid=(NDEV,),
        in_specs=[_const((S, D), lambda j: (0, 0)), panel, panel, _const((S, D), lambda j: (0, 0))],
        out_specs=[pl.BlockSpec((None, D, fw), lambda j: (j, 0, 0)), pl.BlockSpec((None, c.fr, D), lambda j: (j, 0, 0))],
        out_shape=[jax.ShapeDtypeStruct((NDEV, D, fw), BF16), jax.ShapeDtypeStruct((NDEV, c.fr, D), BF16)],
        compiler_params=_params(1))


def _attn_bwd(c, name, q, kv, do, probs, dep=None):
    S, dq, dkv = c.S, c.DQ, c.DKV
    nb = c.nb
    scale = HEAD_DIM ** -0.5

    def body(q_ref, kvc_ref, kvp_ref, do_ref, probs_ref, dq_ref, dkv_ref, dbias_ref, dsink_ref, dsink_acc,
             ds_sc, p_sc, qm_sc, dom_sc):
        n = pl.program_id(0)

        @pl.when(n == 0)
        def _():
            dkv_ref[...] = jnp.zeros_like(dkv_ref)
            dbias_ref[...] = jnp.zeros_like(dbias_ref)
            dsink_acc[...] = jnp.zeros_like(dsink_acc)

        low = _low_lanes()
        first = _first_key()
        rows_c = pl.ds(pl.multiple_of(n * BLOCK, BLOCK), BLOCK)
        rows_p = pl.ds(pl.multiple_of(jnp.maximum(n - 1, 0) * BLOCK, BLOCK), BLOCK)
        for kvp in range(KV_PAIRS):
            k2, v2 = _pair_kv(kvc_ref, kvp_ref, kvp, dkv)
            for g in range(Q_PER_KV):
                grp = kvp * Q_PER_KV + g
                halves = []
                for par in range(2):
                    rows = slice((2 * g + par) * BLOCK, (2 * g + par + 1) * BLOCK)
                    qh = _head_operand(q_ref, grp, par, low, scale=scale)
                    doh = _head_operand(do_ref, grp, par, low)
                    saved = probs_ref[kvp, rows, :]
                    ps = saved[:, 0:1].astype(F32)
                    p16 = jnp.where(first, jnp.zeros_like(saved), saved)
                    p = p16.astype(F32)
                    dp = _dot(doh, v2, NT)
                    delta = jnp.sum(p * dp, axis=-1, keepdims=True)
                    ds = p * (dp - delta)
                    dbias_ref[kvp, rows, :] += ds
                    dsink_acc[rows, kvp:kvp + 1] += -(ps * delta)
                    ds16 = ds.astype(BF16)
                    halves.append(_dot(ds16, k2, NN) * scale)
                    ds_sc[rows, :] = ds16
                    p_sc[rows, :] = p16
                    qm_sc[rows, :] = qh
                    dom_sc[rows, :] = doh
                dq_ref[:, grp * LANES:(grp + 1) * LANES] = jnp.where(low, halves[0], halves[1]).astype(BF16)
            dk2 = _dot(ds_sc[...], qm_sc[...], TN)
            dv2 = _dot(p_sc[...], dom_sc[...], TN)
            lanes = slice(kvp * LANES, (kvp + 1) * LANES)
            vlanes = slice(dkv + kvp * LANES, dkv + (kvp + 1) * LANES)
            dkv_ref[rows_p, lanes] += dk2[:BLOCK]
            dkv_ref[rows_c, lanes] += dk2[BLOCK:]
            dkv_ref[rows_p, vlanes] += dv2[:BLOCK]
            dkv_ref[rows_c, vlanes] += dv2[BLOCK:]

        @pl.when(n == nb - 1)
        def _():
            dsink_ref[...] = jnp.sum(dsink_acc[...].reshape(2 * Q_PER_KV, BLOCK, KV_PAIRS), axis=1)

    return _pcall(
        body, name=name, ins=[q, kv, kv, do, probs], dep=dep, grid=(nb,),
        in_specs=_attn_specs(c) + [pl.BlockSpec((BLOCK, dq), lambda n: (n, 0)), _probs_spec()],
        out_specs=[pl.BlockSpec((BLOCK, dq), lambda n: (n, 0)), pl.BlockSpec((S, 2 * dkv), lambda n: (0, 0)),
                   pl.BlockSpec((KV_PAIRS, PAIR_ROWS, 2 * BLOCK), lambda n: (0, 0, 0)),
                   pl.BlockSpec((2 * Q_PER_KV, KV_PAIRS), lambda n: (0, 0))],
        out_shape=[jax.ShapeDtypeStruct((S, dq), BF16), jax.ShapeDtypeStruct((S, 2 * dkv), F32),
                   jax.ShapeDtypeStruct((KV_PAIRS, PAIR_ROWS, 2 * BLOCK), F32),
                   jax.ShapeDtypeStruct((2 * Q_PER_KV, KV_PAIRS), F32)],
        scratch_shapes=[pltpu.VMEM((PAIR_ROWS, KV_PAIRS), F32), pltpu.VMEM((PAIR_ROWS, 2 * BLOCK), BF16),
                        pltpu.VMEM((PAIR_ROWS, 2 * BLOCK), BF16), pltpu.VMEM((PAIR_ROWS, LANES), BF16),
                        pltpu.VMEM((PAIR_ROWS, LANES), BF16)],
        compiler_params=_params(1))


def _sgu_bwd(c, name, a, z, dgated, ln_g, wc, wc_t, b_t, dep=None):
    S, AW, gd, tm = c.S, c.AW, c.gd, c.tms

    def body(a_ref, z_ref, dg_ref, lng_ref, wc_ref, wct_ref, bt_ref, dz_ref, dws_ref, dbt_ref, dlng_ref, dvn_ref):
        i = pl.program_id(0)

        @pl.when(i == 0)
        def _():
            dws_ref[...] = jnp.zeros_like(dws_ref)
            dbt_ref[...] = jnp.zeros_like(dbt_ref)
            dlng_ref[...] = jnp.zeros_like(dlng_ref)

        lng = lng_ref[...]
        va = a_ref[:, AW:].astype(F32)
        xc = va - jnp.mean(va, axis=-1, keepdims=True)
        rstd = lax.rsqrt(jnp.mean(xc * xc, axis=-1, keepdims=True) + EPS)
        xh = xc * rstd
        vn = (xh * lng).astype(BF16)
        causal = _sgu_masks()
        for ch in range(tm // CHUNK):
            rows = slice(ch * CHUNK, (ch + 1) * CHUNK)
            for g in range(A_GROUPS):
                cols = slice(g * gd, (g + 1) * gd)
                blk = vn[rows, cols]
                mixed = _dot(wc_ref[g], blk, NN) + bt_ref[:, g:g + 1]
                dgb = dg_ref[rows, cols].astype(F32)
                dm = dgb * a_ref[rows, cols].astype(F32)
                dbt_ref[:, g:g + 1] += jnp.sum(dm, axis=1, keepdims=True)
                dm16 = dm.astype(BF16)
                dws_ref[g] += jnp.where(causal, _dot(dm16, blk, NT), 0.0)
                dvn_ref[rows, cols] = _dot(wct_ref[g], dm16, NN)
                dz_ref[rows, cols] = (dgb * mixed * _gelu_grad(z_ref[rows, cols].astype(F32))).astype(BF16)
        dvn = dvn_ref[...]
        dlng_ref[...] += jnp.sum(dvn * xh, axis=0, keepdims=True)
        dxh = dvn * lng
        dva = rstd * (dxh - jnp.mean(dxh, axis=-1, keepdims=True) - xh * jnp.mean(dxh * xh, axis=-1, keepdims=True))
        dz_ref[:, AW:] = (dva * _gelu_grad(z_ref[:, AW:].astype(F32))).astype(BF16)

    wide = pl.BlockSpec((tm, 2 * AW), lambda i: (i, 0))
    wsp = pl.BlockSpec((A_GROUPS, CHUNK, CHUNK), lambda i: (0, 0, 0))
    btsp = pl.BlockSpec((CHUNK, A_GROUPS), lambda i: (0, 0))
    return _pcall(
        body, name=name, ins=[a, z, dgated, ln_g, wc, wc_t, b_t], dep=dep, grid=(S // tm,),
        in_specs=[wide, wide, pl.BlockSpec((tm, AW), lambda i: (i, 0)), pl.BlockSpec((1, AW), lambda i: (0, 0)),
                  wsp, wsp, btsp],
        out_specs=[wide, wsp, btsp, pl.BlockSpec((1, AW), lambda i: (0, 0))],
        out_shape=[jax.ShapeDtypeStruct((S, 2 * AW), BF16), jax.ShapeDtypeStruct((A_GROUPS, CHUNK, CHUNK), F32),
                   jax.ShapeDtypeStruct((CHUNK, A_GROUPS), F32), jax.ShapeDtypeStruct((1, AW), F32)],
        scratch_shapes=[pltpu.VMEM((tm, AW), F32)], compiler_params=_params(1))


def _adam_update(g, w_ref, m_ref, v_ref, out_refs):
    g_ref, d_ref, nm_ref, nv_ref = out_refs
    bc1 = 1.0 - ADAM_B1 ** ADAM_STEP
    bc2 = 1.0 - ADAM_B2 ** ADAM_STEP
    nm = ADAM_B1 * m_ref[...] + (1.0 - ADAM_B1) * g
    nv = ADAM_B2 * v_ref[...] + (1.0 - ADAM_B2) * (g * g)
    g_ref[...] = g
    nm_ref[...] = nm
    nv_ref[...] = nv
    d_ref[...] = -ADAM_LR * ((nm * (1.0 / bc1)) / (jnp.sqrt(nv * (1.0 / bc2)) + ADAM_EPS) + ADAM_WD * w_ref[...])


def _adamw_packed(name, sums, offs, ws, ms, vs, dep=None):
    n = len(ws)

    def body(*refs):
        s_ref, w_refs, m_refs, v_refs = refs[0], refs[1:1 + n], refs[1 + n:1 + 2 * n], refs[1 + 2 * n:1 + 3 * n]
        outs = refs[1 + 3 * n:]
        for k in range(n):
            _adam_update(s_ref[offs[k]:offs[k + 1], :], w_refs[k], m_refs[k], v_refs[k], outs[4 * k:4 * k + 4])

    ins = [sums, *ws, *ms, *vs]
    shapes = [jax.ShapeDtypeStruct(w.shape, F32) for w in ws for _ in range(4)]
    return _pcall(body, name=name, ins=ins, in_specs=[_whole(t) for t in ins], dep=dep,
                  out_shape=shapes, out_specs=[_whole(t) for t in shapes], compiler_params=_params(0))


def _adamw(name, parts, part_block, part_index, w, m, v, tr, row_off=0, n_rows=None, prev=None, dep=None):
    R, C = w.shape
    n_rows = R if n_rows is None else n_rows
    assert n_rows % tr == 0 and row_off % tr == 0

    def body(p_ref, w_ref, m_ref, v_ref, *rest):
        g = p_ref[0].astype(F32)
        for s in range(1, part_block[0]):
            g = g + p_ref[s].astype(F32)
        _adam_update(g, w_ref, m_ref, v_ref, rest[-4:])

    ob = row_off // tr
    row = pl.BlockSpec((tr, C), lambda i: (ob + i, 0))
    out = jax.ShapeDtypeStruct((R, C), F32)
    chained = prev is not None
    return _pcall(
        body, name=name, ins=[parts, w, m, v] + (list(prev) if chained else []), dep=dep, grid=(n_rows // tr,),
        in_specs=[pl.BlockSpec(part_block, part_index), row, row, row] + ([ANY] * 4 if chained else []),
        out_specs=[row, row, row, row], out_shape=[out, out, out, out],
        input_output_aliases={4 + t: t for t in range(4)} if chained else {}, compiler_params=_params(1))


def _sum_parts(name, parts, dep=None):
    def body(p_ref, out_ref):
        g = p_ref[0]
        for s in range(1, parts.shape[0]):
            g = g + p_ref[s]
        out_ref[...] = g

    return _pcall(body, name=name, ins=[parts], in_specs=[_whole(parts)], dep=dep,
                  out_shape=jax.ShapeDtypeStruct(parts.shape[1:], F32), compiler_params=_params(0))


def _place():
    return lax.axis_index("x"), lax.axis_index("y"), lax.axis_index("c")


def _slot(px, py, pc):
    return 4 * px + 2 * py + pc


def _peer(k, x, y, c):
    return x ^ ((k >> 2) & 1), y ^ ((k >> 1) & 1), c ^ (k & 1)


SEND_PEERS = {"exchange": tuple(range(1, NDEV)), "gather": (1, 2, 4, 6), "forward": (2, 4, 6),
              "broadcast": tuple(range(1, NDEV))}


def _n_sems(mode, n_lands):
    return n_lands * (len(SEND_PEERS[mode]) + (mode != "forward"))


def _send_copies(mode, src_refs, land_refs, send_sems, recv_sems):
    x, y, c = _place()
    me = _slot(x, y, c)
    peers = SEND_PEERS[mode]
    remote, local = [], []
    for i, k in enumerate(peers):
        peer = _peer(k, x, y, c)
        for a, land in enumerate(land_refs):
            if mode == "exchange":
                src, dst, to = src_refs[a].at[_slot(*peer)], land.at[me], peer
            elif mode in ("gather", "broadcast"):
                src, dst, to = src_refs[a], land.at[me], peer
            else:
                src = dst = land.at[_slot(*peer)]
                to = (x, y, 1 - c)
            s = a * len(peers) + i
            remote.append(pltpu.make_async_remote_copy(src_ref=src, dst_ref=dst, send_sem=send_sems.at[s],
                                                       recv_sem=recv_sems.at[s], device_id=to, device_id_type=MESH))
    if mode != "forward":
        for a, land in enumerate(land_refs):
            src = src_refs[a].at[me] if mode == "exchange" else src_refs[a]
            local.append(pltpu.make_async_copy(src, land.at[me], send_sems.at[len(land_refs) * len(peers) + a]))
    return remote, local


def _send_start_groups(name, groups, mode, collective_id=None, dep=None):
    sizes = [(len(s), len(l)) for s, l in groups]
    flat = [t for s, l in groups for t in (*s, *l)]
    n_in, ng = len(flat), len(groups)

    def body(*refs):
        sems, token, at = refs[n_in:n_in + 2 * ng], refs[-1], 0
        if collective_id is not None:
            x, y, c = _place()
            barrier = pltpu.get_barrier_semaphore()
            shake = (1,) if mode == "forward" else SEND_PEERS[mode]
            for k in shake:
                pl.semaphore_signal(barrier, inc=1, device_id=_peer(k, x, y, c), device_id_type=MESH)
            pl.semaphore_wait(barrier, len(shake))
        for gi, (ns, nl) in enumerate(sizes):
            remote, local = _send_copies(mode, refs[at:at + ns], refs[at + ns:at + ns + nl], sems[2 * gi],
                                         sems[2 * gi + 1])
            for cp in remote + local:
                cp.start()
            at += ns + nl
        token[...] = jnp.zeros_like(token)

    sem_shapes = [pltpu.SemaphoreType.DMA((_n_sems(mode, nl),)) for _, nl in sizes for _ in range(2)]
    if any(dep is t for t in flat):
        dep = None
    out = _pcall(
        body, name=name, ins=[pltpu.with_memory_space_constraint(t, pltpu.HBM) for t in flat],
        in_specs=[HBM] * n_in, dep=dep,
        out_shape=(*sem_shapes, *[pltpu.HBM(t.shape, t.dtype) for t in flat], jax.ShapeDtypeStruct((8, LANES), F32)),
        out_specs=(*[SEM] * (2 * ng), *[HBM] * n_in, pl.BlockSpec(memory_space=pltpu.VMEM)),
        input_output_aliases={i: 2 * ng + i for i in range(n_in)},
        compiler_params=pltpu.CompilerParams(has_side_effects=pltpu.SideEffectType.DATAFLOW_SIDE_EFFECTING,
                                             collective_id=collective_id))
    started, at = [], 2 * ng
    for gi, (ns, nl) in enumerate(sizes):
        started.append((out[-1], out[2 * gi], out[2 * gi + 1], list(out[at:at + ns]), list(out[at + ns:at + ns + nl])))
        at += ns + nl
    return started


def _send_start(name, srcs, lands, mode, collective_id=None, dep=None):
    if mode == "forward":
        collective_id = SIBLING_COLLECTIVE_ID
    return _send_start_groups(name, [(srcs, lands)], mode, collective_id=collective_id, dep=dep)[0]


def _send_wait(name, started, mode, dep=None):
    _, send_sems, recv_sems, srcs, lands = started
    n_src, n = len(srcs), len(lands)

    def body(*refs):
        src_refs, land_refs = refs[:n_src], refs[n_src:n_src + n]
        ssem, rsem = refs[n_src + n], refs[n_src + n + 1]
        remote, local = _send_copies(mode, src_refs, land_refs, ssem, rsem)
        for cp in remote:
            cp.wait_send()
            cp.wait_recv()
        for cp in local:
            cp.wait()

    thru = [pltpu.HBM(t.shape, t.dtype) for t in [*srcs, *lands]]
    out = _pcall(
        body, name=name, ins=[*srcs, *lands, send_sems, recv_sems], in_specs=[HBM] * (n_src + n) + [SEM, SEM], dep=dep,
        out_shape=tuple(thru), out_specs=tuple([HBM] * (n_src + n)),
        input_output_aliases={i: i for i in range(n_src + n)},
        compiler_params=pltpu.CompilerParams(has_side_effects=pltpu.SideEffectType.DATAFLOW_SIDE_EFFECTING))
    return list(out[n_src:])


def _landing(block):
    return lax.empty((NDEV, *block.shape), block.dtype)


def _rows128(t):
    flat = t.reshape(-1)
    n = flat.shape[0]
    rows = -(-n // (8 * LANES)) * 8
    return jnp.pad(flat, (0, rows * LANES - n)).reshape(rows, LANES)


def kernel(x, mix_norm_g, ffn_norm_g, a_w_in, a_ln_g, a_w_spatial, a_b_spatial, a_w_out, kv_norm_g, w_k, w_v, b_w_q, b_sinks, b_w_o, rel_bias, ffn_w1, ffn_w2, final_norm_g, loss_target, m_mix_norm_g, m_ffn_norm_g, m_a_w_in, m_a_ln_g, m_a_w_spatial, m_a_b_spatial, m_a_w_out, m_kv_norm_g, m_w_k, m_w_v, m_b_w_q, m_b_sinks, m_b_w_o, m_rel_bias, m_ffn_w1, m_ffn_w2, m_final_norm_g, v_mix_norm_g, v_ffn_norm_g, v_a_w_in, v_a_ln_g, v_a_w_spatial, v_a_b_spatial, v_a_w_out, v_kv_norm_g, v_w_k, v_w_v, v_b_w_q, v_b_sinks, v_b_w_o, v_rel_bias, v_ffn_w1, v_ffn_w2, v_final_norm_g):
    c = _config(x, a_w_in, a_w_out, w_k, b_w_q, b_w_o, ffn_w1, ffn_w2)
    S, D, LA, LB, LF = c.S, c.D, c.LA, c.LB, c.LF
    weights = dict(mix_norm_g=mix_norm_g, ffn_norm_g=ffn_norm_g, a_w_in=a_w_in, a_ln_g=a_ln_g, a_w_spatial=a_w_spatial,
                   a_b_spatial=a_b_spatial, a_w_out=a_w_out, kv_norm_g=kv_norm_g, w_k=w_k, w_v=w_v, b_w_q=b_w_q,
                   b_sinks=b_sinks, b_w_o=b_w_o, rel_bias=rel_bias, ffn_w1=ffn_w1, ffn_w2=ffn_w2,
                   final_norm_g=final_norm_g)
    m_in = dict(mix_norm_g=m_mix_norm_g, ffn_norm_g=m_ffn_norm_g, a_w_in=m_a_w_in, a_ln_g=m_a_ln_g,
                a_w_spatial=m_a_w_spatial, a_b_spatial=m_a_b_spatial, a_w_out=m_a_w_out, kv_norm_g=m_kv_norm_g,
                w_k=m_w_k, w_v=m_w_v, b_w_q=m_b_w_q, b_sinks=m_b_sinks, b_w_o=m_b_w_o, rel_bias=m_rel_bias,
                ffn_w1=m_ffn_w1, ffn_w2=m_ffn_w2, final_norm_g=m_final_norm_g)
    v_in = dict(mix_norm_g=v_mix_norm_g, ffn_norm_g=v_ffn_norm_g, a_w_in=v_a_w_in, a_ln_g=v_a_ln_g,
                a_w_spatial=v_a_w_spatial, a_b_spatial=v_a_b_spatial, a_w_out=v_a_w_out, kv_norm_g=v_kv_norm_g,
                w_k=v_w_k, w_v=v_w_v, b_w_q=v_b_w_q, b_sinks=v_b_sinks, b_w_o=v_b_w_o, rel_bias=v_rel_bias,
                ffn_w1=v_ffn_w1, ffn_w2=v_ffn_w2, final_norm_g=v_final_norm_g)
    names = list(weights)
    seq = _Seq()
    me = _slot(*_place())
    bf = lambda t: t.astype(BF16)

    tr = lambda t: bf(jnp.swapaxes(t, -1, -2))

    def start(tag, some):
        got = seq(_send_start_groups, f"weights_start_{tag}", [(grp, [_landing(t) for t in grp]) for grp in some],
                  "gather")
        seq.last = got[0][0]
        return got

    started = start("first", [[tr(a_w_in[0])[None], a_ln_g]])
    _, a_w_in_, a_w_out_, ffn_w1_, ffn_w2_, w_k_, w_v_, b_w_q_, b_w_o_ = lax.optimization_barrier(
        (started[0][0], a_w_in, a_w_out, ffn_w1, ffn_w2, w_k, w_v, b_w_q, b_w_o))
    groups = []
    for l in range(LA):
        groups += [[tr(a_w_in_[l])[None]], [bf(a_w_out_[l])], [tr(ffn_w1_[l])[None], bf(ffn_w2_[l])]]
    gb = 3 * LA
    for l in range(LB):
        extra = [bf(jnp.concatenate([w_k_, w_v_], axis=1))] if l == 0 else []
        groups += [extra + [bf(_slot_cols(b_w_q_[l])), bf(b_w_o_[l])], [tr(ffn_w1_[LA + l])[None], bf(ffn_w2_[LA + l])]]
    started += start("rest", groups[1:])
    forwarding = {}

    def forward(i):
        lands = seq(_send_wait, f"weights_wait{i}", started[i], "gather")
        forwarding[i] = seq(_send_start, f"weights_forward{i}", [], lands, "forward")

    def arrive(i):
        if i not in forwarding:
            forward(i)
        return seq(_send_wait, f"weights_arrive{i}", forwarding[i], "forward")

    causal = jnp.tril(jnp.ones((CHUNK, CHUNK), bool))
    wsp = jnp.where(causal[None, None], a_w_spatial, 0.0)
    wsp16 = wsp.astype(BF16)
    wsp16_t = jnp.swapaxes(wsp, -1, -2).astype(BF16)
    bsp_t = jnp.swapaxes(a_b_spatial, -1, -2)
    mix_g = mix_norm_g.reshape(-1, 1, D)
    ffn_g = ffn_norm_g.reshape(-1, 1, D)
    kv_g = kv_norm_g.reshape(1, D)
    fin_g = final_norm_g.reshape(1, D)
    onehot = _bucket_onehot()
    bias = _slot_bias(seq(_band_bias, rel_bias.T, onehot).reshape(N_HEADS, BLOCK, 2 * BLOCK))

    h = x.reshape(S, D)
    sav_a, sav_b, wts_a, wts_b = [], [], [], []
    for l in range(LA):
        got = arrive(3 * l)
        w_in = got[0]
        if l == 0:
            ln_g_full = jnp.transpose(got[1], (1, 0, 2)).reshape(LA, 1, c.AW)
        z, a, hn = seq(_a_in_fwd, c, f"a_in_fwd{l}", h, mix_g[l], w_in, 0)
        forward(3 * l + 1)
        gated = seq(_sgu_fwd, c, f"sgu_fwd{l}", a, ln_g_full[l], wsp16[l], bsp_t[l])
        (wout,) = arrive(3 * l + 1)
        if l > 0:
            forward(3 * l + 2)
        h1 = seq(_mm_res, c, f"a_out_fwd{l}", gated, wout, c.ar, 0, h)
        w1, rows = arrive(3 * l + 2)
        if l == LA - 1:
            forward(gb)
        p, h2, hnf = seq(_ffn_fwd, c, f"ffn_fwd{l}", h1, ffn_g[l], w1, 0, rows)
        sav_a.append((h, z, a, hn, gated, h1, p, hnf))
        wts_a.append((w_in, 0, w1, 0, rows, wout, 0))
        h = h2
    h_kv = h
    for l in range(LB):
        got = arrive(gb + 2 * l)
        if l == 0:
            wkv, got = got[0], got[1:]
            kv, hkv = seq(_rms_mm_rows, c, "kv_fwd", h, kv_g, wkv, c.kr, 0, 2 * c.DKV)
        wq, wo = got[0], _slot_rows(got[1])
        q, hn = seq(_rms_mm_rows, c, f"q_fwd{l}", h, mix_g[LA + l], wq, c.qr, 0, c.DQ)
        forward(gb + 2 * l + 1)
        o, probs = seq(_attn_fwd, c, f"attn_fwd{l}", q, kv, bias, b_sinks[l])
        h1 = seq(_mm_res, c, f"o_fwd{l}", o, wo, c.orr, 0, h)
        w1, rows = arrive(gb + 2 * l + 1)
        if l + 1 < LB:
            forward(gb + 2 * l + 2)
        p, h2, hnf = seq(_ffn_fwd, c, f"ffn_fwd{LA + l}", h1, ffn_g[LA + l], w1, 0, rows)
        sav_b.append((h, q, hn, o, probs, h1, p, hnf))
        wts_b.append((wq, wo, w1, rows))
        h = h2
    dh, d_fin_g, loss_row = seq(_final_loss, c, h, fin_g, loss_target.reshape(S, D))

    results = {}
    in_flight, exchanges = [], []

    def update(k, parts, layer, col_blk=0):
        w = weights[k]
        rows_l, ncols = (w.shape[-2], w.shape[-1]) if w.ndim == 3 else w.shape
        flat = lambda t: t.reshape(-1, ncols)
        tr = min(256, rows_l)
        results[k] = seq(_adamw, f"adamw_{k}{layer}", parts, (NDEV, tr, ncols), lambda i: (0, i, col_blk),
                         flat(w), flat(m_in[k]), flat(v_in[k]), tr, row_off=layer * rows_l, n_rows=rows_l,
                         prev=results.get(k))

    def land(tag, entry):
        lands = seq(_send_wait, f"grads_wait_{tag}", entry[1], "exchange")
        for keys, parts in zip(entry[0], lands):
            for k, layer, col_blk in keys:
                update(k, parts, layer, col_blk)

    def send(tag, items):
        slabs = [t for _, t in items]
        own = [lax.empty(t.shape, t.dtype) for t in slabs]
        exchanges.append(tag)
        st = seq(_send_start, f"grads_start_{tag}", slabs, own, "exchange",
                 collective_id=SIBLING_COLLECTIVE_ID + len(exchanges))
        in_flight.append((tag, ([k for k, _ in items], st)))
        while len(in_flight) > EXCHANGE_LAG:
            land(*in_flight.pop(0))

    d_mix_g, d_ffn_g = [None] * LF, [None] * LF
    dkv_list, dbias_list, dsink_list = [], [], [None] * LB

    def ffn_bwd(lf, dh, h1, p, hnf, w1, w1_i, rows):
        da, dh1, d_ffn_g[lf], dhb = seq(_ffn_bwd_data, c, f"ffn_bwd_data{lf}", dh, p, w1, w1_i, rows, h1, ffn_g[lf])
        dw1, dw2 = seq(_ffn_bwd_w, c, f"ffn_bwd_w{lf}", hnf, da, p, dhb)
        send(f"ffn{lf}", [([("ffn_w1", lf, 0)], dw1), ([("ffn_w2", lf, 0)], dw2)])
        return dh1

    for l in reversed(range(LB)):
        h0, q, hn, o, probs, h1, p, hnf = sav_b[l]
        wq, wo, w1, rows = wts_b[l]
        dh1 = ffn_bwd(LA + l, dh, h1, p, hnf, w1, 0, rows)
        do = seq(_bwd_rows_data, c, f"o_bwd_data{l}", dh1, wo, c.orr, 0)
        dwo = seq(_wgrad_rows, c, f"o_bwd_w{l}", o, [dh1], c.DQ, D, unslot="rows")
        dq, dkv, dbias, dsink = seq(_attn_bwd, c, f"attn_bwd{l}", q, kv, do, probs)
        dsink_list[l] = dsink.reshape(Q_PER_KV, 2, KV_PAIRS).transpose(2, 1, 0).reshape(1, N_HEADS)
        dkv_list.append(dkv)
        dbias_list.append(dbias.reshape(N_HEADS, -1))
        dwq = seq(_wgrad_rows, c, f"q_bwd_w{l}", hn, [dq], D, c.DQ, unslot="cols")
        send(f"attn{l}", [([("b_w_o", l, 0)], dwo), ([("b_w_q", l, 0)], dwq)])
        if l > 0:
            dh, d_mix_g[LA + l] = seq(_bwd_rows_to_stream, c, f"q_bwd_data{l}", [dq], wq, c.qr, 0, c.DQ, h0,
                                      mix_g[LA + l], dh1)
    dwkv = seq(_wgrad_rows, c, "kv_bwd_w", hkv, dkv_list, D, 2 * c.DKV)
    send("kv", [([("w_k", 0, 0), ("w_v", 0, 1)], dwkv)])
    dh, d_mix_g[LA], d_kv_g = seq(_bwd_q_kv_to_stream, c, "q_kv_bwd_data", dq, wq, dkv_list, wkv, h_kv, mix_g[LA],
                                  kv_g, dh1)
    d_rel_t = seq(_band_bias_grad, dbias_list, onehot)
    d_rel_t = d_rel_t.reshape(KV_PAIRS, Q_PER_KV, 2, N_BUCKETS).swapaxes(1, 2).reshape(N_HEADS, N_BUCKETS)
    d_wsp, d_bsp, d_lng = [None] * LA, [None] * LA, [None] * LA
    for l in reversed(range(LA)):
        h0, z, a, hn, gated, h1, p, hnf = sav_a[l]
        w_in, in_i, w1, w1_i, rows, wout, wout_i = wts_a[l]
        dh1 = ffn_bwd(l, dh, h1, p, hnf, w1, w1_i, rows)
        dgated = seq(_bwd_rows_data, c, f"a_out_bwd_data{l}", dh1, wout, c.ar, wout_i)
        dwout = seq(_wgrad_rows, c, f"a_out_bwd_w{l}", gated, [dh1], c.AW, D)
        send(f"a_out{l}", [([("a_w_out", l, 0)], dwout)])
        dz, d_wsp[l], dbt, d_lng[l] = seq(_sgu_bwd, c, f"sgu_bwd{l}", a, z, dgated, ln_g_full[l], wsp16[l],
                                          wsp16_t[l], bsp_t[l])
        d_bsp[l] = dbt.T
        dwin = seq(_wgrad_cols, c, f"a_in_bwd_w{l}", hn, dz)
        send(f"a_in{l}", [([("a_w_in", l, 0)], dwin)])
        dh, d_mix_g[l] = seq(_bwd_cols_to_stream, c, f"a_in_bwd_data{l}", dz, w_in, in_i, h0, mix_g[l], dh1)
    grad_x = dh.reshape(1, S, D)

    small = {
        "mix_norm_g": jnp.concatenate(d_mix_g, axis=0), "ffn_norm_g": jnp.concatenate(d_ffn_g, axis=0),
        "a_w_spatial": jnp.stack(d_wsp), "a_b_spatial": jnp.stack(d_bsp), "kv_norm_g": d_kv_g,
        "b_sinks": jnp.concatenate(dsink_list, axis=0), "rel_bias": d_rel_t.T, "final_norm_g": d_fin_g,
    }
    small_names = list(small)
    packs = [_rows128(small[k]) for k in small_names] + [_rows128(jnp.concatenate(d_lng, axis=0)), _rows128(loss_row)]
    offs = [int(o) for o in np.cumsum([0] + [p.shape[0] for p in packs])]
    Rs = offs[-1] + (-offs[-1]) % (8 * NDEV)
    packed = jnp.concatenate(packs + [jnp.zeros((Rs - offs[-1], LANES), F32)], axis=0)
    slab = packed.reshape(NDEV, Rs // NDEV, LANES)
    st = seq(_send_start, "small_grads_start", [slab], [lax.empty(slab.shape, slab.dtype)], "exchange")
    while len(in_flight) > 1:
        land(*in_flight.pop(0))
    (parts,) = seq(_send_wait, "small_grads_wait", st, "exchange")
    mine = seq(_sum_parts, "small_grads_sum", parts)
    st = seq(_send_start, "small_sums_start", [mine], [_landing(mine)], "broadcast")
    while in_flight:
        land(*in_flight.pop(0))
    (sums,) = seq(_send_wait, "small_sums_wait", st, "broadcast")
    sums = sums.reshape(Rs, LANES)
    loss = sums[offs[-2], 0]

    grads, deltas, new_m, new_v = {}, {}, {}, {}

    def put(k, outs, shape):
        grads[k], deltas[k], new_m[k], new_v[k] = (t.reshape(shape) for t in outs)

    outs = seq(_adamw_packed, "adamw_small", sums, offs, *[[_rows128(d[k]) for k in small_names]
                                                          for d in (weights, m_in, v_in)])
    for n_, k in enumerate(small_names):
        shape = weights[k].shape
        size = int(np.prod(shape))
        put(k, [t.reshape(-1)[:size] for t in outs[4 * n_:4 * n_ + 4]], shape)
    lng_sum = sums[offs[-3]:offs[-2]].reshape(-1)[:LA * c.AW].reshape(LA, c.AW)
    lng_mine = lax.dynamic_slice_in_dim(lng_sum, me * c.ar, c.ar, axis=1)
    lng_parts = jnp.concatenate([lng_mine[None], jnp.zeros((NDEV - 1, LA, c.ar), F32)], axis=0)
    put("a_ln_g", seq(_adamw, "adamw_ln_g", lng_parts, (NDEV, LA, c.ar), lambda i: (0, 0, 0),
                      a_ln_g, m_in["a_ln_g"], v_in["a_ln_g"], LA), a_ln_g.shape)
    for k in ("a_w_in", "ffn_w1", "ffn_w2", "a_w_out", "b_w_o", "b_w_q", "w_k", "w_v"):
        put(k, results[k], weights[k].shape)

    return (loss, grad_x, *[grads[k] for k in names], *[deltas[k] for k in names],
            *[new_m[k] for k in names], *[new_v[k] for k in names])
```

```python
import numpy as np
import math
import jax
import jax.numpy as jnp
from jax import lax
from jax.experimental import pallas as pl
from jax.experimental.pallas import tpu as pltpu

F32 = jnp.float32
BF16 = jnp.bfloat16

NDEV = 8
EPS = 1e-6
CHUNK = 128
A_GROUPS = 8
N_HEADS = 16
N_KV_HEADS = 4
Q_PER_KV = N_HEADS // N_KV_HEADS
HEAD_DIM = 64
BLOCK = 128
N_BUCKETS = 32
MAX_DISTANCE = 128
ADAM_LR, ADAM_B1, ADAM_B2, ADAM_EPS, ADAM_WD, ADAM_STEP = 0.001, 0.9, 0.999, 1e-08, 0.01, 10
LANES = 128
VMEM_LIMIT = 56 * 1024 * 1024
INV_SQRT2 = 0.7071067811865476
INV_SQRT_2PI = 0.3989422804014327
EXCHANGE_LAG = 4
SIBLING_COLLECTIVE_ID = 0

HBM = pl.BlockSpec(memory_space=pltpu.HBM)
SMEM = pl.BlockSpec(memory_space=pltpu.SMEM)
ANY = pl.BlockSpec(memory_space=pl.ANY)
SEM = pl.BlockSpec(memory_space=pltpu.SEMAPHORE)
MESH = pl.DeviceIdType.MESH


def _params(n_grid):
    return pltpu.CompilerParams(dimension_semantics=("arbitrary",) * n_grid, vmem_limit_bytes=VMEM_LIMIT)


def _const(block, index_map):
    return pl.BlockSpec(block, index_map, pipeline_mode=pl.Buffered(1))


def _pcall(body, *, ins, in_specs, dep=None, **kw):
    n_in = len(ins)
    if dep is None or any(dep is t for t in ins):
        return pl.pallas_call(body, in_specs=list(in_specs), **kw)(*ins)

    def with_dep(*refs):
        body(*refs[:n_in], *refs[n_in + 1:])

    return pl.pallas_call(with_dep, in_specs=[*in_specs, ANY], **kw)(*ins, dep)


class _Seq:
    def __init__(self):
        self.last = None

    def __call__(self, fn, *args, **kw):
        out = fn(*args, dep=self.last, **kw)
        self.last = out[0] if isinstance(out, (tuple, list)) else out
        return out


def _rstd(h):
    return lax.rsqrt(jnp.mean(h * h, axis=-1, keepdims=True) + EPS)


def _rms_bwd(dhn, h, g, dres):
    r = _rstd(h)
    xh = h * r
    dg = jnp.sum(dhn * xh, axis=0, keepdims=True)
    dxh = dhn * g
    dx = r * (dxh - xh * jnp.mean(dxh * xh, axis=-1, keepdims=True))
    return dres + dx, dg


def _gelu(z):
    return 0.5 * z * (1.0 + lax.erf(z * INV_SQRT2))


def _gelu_grad(z):
    return 0.5 * (1.0 + lax.erf(z * INV_SQRT2)) + z * (jnp.exp(-0.5 * z * z) * INV_SQRT_2PI)


def _dot(a, b, dims):
    return lax.dot_general(a, b, (dims, ((), ())), preferred_element_type=F32)


NN = ((1,), (0,))
NT = ((1,), (1,))
TN = ((0,), (0,))


def _mm(name, ins, in_specs, out_shapes, out_specs, *, grid, dims, nk, acc_shape, load_a, load_b, epilogue,
        dep=None):
    n_in, n_out = len(ins), len(out_shapes)
    kax = len(grid) - 1

    def body(*refs):
        in_refs = refs[:n_in]
        out_refs = refs[n_in:n_in + n_out]
        a = load_a(in_refs, out_refs)
        b = load_b(in_refs)
        prod = _dot(a, b, dims)
        if nk == 1:
            epilogue(prod, in_refs, out_refs)
        else:
            acc = refs[n_in + n_out]
            k = pl.program_id(kax)

            @pl.when(k == 0)
            def _():
                acc[...] = prod

            @pl.when(k > 0)
            def _():
                acc[...] += prod

            @pl.when(k == nk - 1)
            def _():
                epilogue(acc[...], in_refs, out_refs)

    return _pcall(
        body, name=name, ins=ins, in_specs=in_specs, dep=dep, grid=grid, out_specs=out_specs, out_shape=out_shapes,
        scratch_shapes=[pltpu.VMEM(acc_shape, F32)] if nk > 1 else [], compiler_params=_params(len(grid)))


def _bf(ref_idx):
    return lambda in_refs, *_: in_refs[ref_idx][...].astype(BF16)


def _b_view(ref_idx, rows):
    def load(in_refs):
        b = in_refs[ref_idx][...]
        return b.reshape(rows, b.shape[-1])
    return load


class Cfg:
    pass


def _config(x, a_w_in, a_w_out, w_k, b_w_q, b_w_o, ffn_w1, ffn_w2):
    c = Cfg()
    c.S, c.D = x.shape[1], x.shape[2]
    c.LA, _, c.cw = a_w_in.shape
    c.AW2 = NDEV * c.cw
    c.AW = c.AW2 // 2
    c.gd = c.AW // A_GROUPS
    c.ar = a_w_out.shape[1]
    c.LF, _, c.fw = ffn_w1.shape
    c.fr = ffn_w2.shape[1]
    c.LB, c.qr, c.DQ = b_w_q.shape
    c.orr = b_w_o.shape[1]
    c.kr, c.DKV = w_k.shape
    c.tm = min(1024, c.S)
    c.tmw = min(512, c.S)
    c.tms = min(512, c.S)
    c.nb = c.S // BLOCK
    assert c.cw == c.fw == c.fr and c.AW == NDEV * c.ar and c.D == NDEV * c.qr == NDEV * c.kr
    assert c.DQ == NDEV * c.orr == N_HEADS * HEAD_DIM and c.DKV == N_KV_HEADS * HEAD_DIM
    assert c.S % c.tm == 0 and c.S % c.tmw == 0 and c.S % c.tms == 0 and c.tms % CHUNK == 0 and c.gd % LANES == 0
    assert c.LA >= 1 and c.LB >= 1 and c.LF == c.LA + c.LB
    return c


def _cached_rms(h_idx, g_idx, hn_out_idx, jax_axis=1):
    def load(in_refs, out_refs):
        hn_ref = out_refs[hn_out_idx]

        @pl.when(pl.program_id(jax_axis) == 0)
        def _():
            h = in_refs[h_idx][...]
            hn_ref[...] = (h * _rstd(h) * in_refs[g_idx][...]).astype(BF16)

        return hn_ref[...]
    return load


def _a_in_fwd(c, name, h, g, col, ci, dep=None):
    S, D, cw, tm = c.S, c.D, c.cw, c.tmw

    def body(h_ref, g_ref, w_ref, z_ref, a_ref, hn_ref):
        h = h_ref[...]
        hn = (h * _rstd(h) * g_ref[...]).astype(BF16)
        hn_ref[...] = hn
        for j in range(NDEV):
            cols = slice(j * cw, (j + 1) * cw)
            z = _dot(hn, w_ref[j], NT)
            z_ref[:, cols] = z.astype(BF16)
            a_ref[:, cols] = _gelu(z).astype(BF16)

    row = pl.BlockSpec((tm, D), lambda i: (i, 0))
    wide = pl.BlockSpec((tm, c.AW2), lambda i: (i, 0))
    return _pcall(
        body, name=name, ins=[h, g, col], dep=dep, grid=(S // tm,),
        in_specs=[row, pl.BlockSpec((1, D), lambda i: (0, 0)), _const((NDEV, None, cw, D), lambda i: (0, ci, 0, 0))],
        out_specs=[wide, wide, row],
        out_shape=[jax.ShapeDtypeStruct((S, c.AW2), BF16), jax.ShapeDtypeStruct((S, c.AW2), BF16),
                   jax.ShapeDtypeStruct((S, D), BF16)],
        compiler_params=_params(1))


def _rms_mm_rows(c, name, h, g, slab, blk_rows, blk_idx, n_out, dep=None):
    S, D, tm = c.S, c.D, c.tm

    def epilogue(acc, in_refs, out_refs):
        out_refs[0][...] = acc.astype(BF16)

    return _mm(
        name, [h, slab, g],
        [pl.BlockSpec((tm, D), lambda i, j, k: (i, 0)),
         pl.BlockSpec((NDEV, blk_rows, n_out), lambda i, j, k: (0, blk_idx, 0)),
         pl.BlockSpec((1, D), lambda i, j, k: (0, 0))],
        [jax.ShapeDtypeStruct((S, n_out), BF16), jax.ShapeDtypeStruct((S, D), BF16)],
        [pl.BlockSpec((tm, n_out), lambda i, j, k: (i, 0)), pl.BlockSpec((tm, D), lambda i, j, k: (i, 0))],
        grid=(S // tm, 1, 1), dims=NN, nk=1, acc_shape=None,
        load_a=_cached_rms(0, 2, 1), load_b=_b_view(1, NDEV * blk_rows), epilogue=epilogue, dep=dep)


def _rms2_mm_rows(c, name, h, projections, dep=None):
    S, D, tm = c.S, c.D, c.tm
    n_p = len(projections)

    def body(h_ref, *refs):
        ins, outs = refs[:2 * n_p], refs[2 * n_p:]
        h = h_ref[...]
        xh = h * _rstd(h)
        for k in range(n_p):
            hn = (xh * ins[2 * k][...]).astype(BF16)
            outs[2 * k + 1][...] = hn
            w = ins[2 * k + 1][...]
            outs[2 * k][...] = _dot(hn, w.reshape(D, w.shape[-1]), NN).astype(BF16)

    row = pl.BlockSpec((tm, D), lambda i: (i, 0))
    ins, in_specs, shapes, specs = [h], [row], [], []
    for g, slab in projections:
        n = slab.shape[-1]
        ins += [g, slab]
        in_specs += [pl.BlockSpec((1, D), lambda i: (0, 0)), pl.BlockSpec(slab.shape, lambda i: (0, 0, 0))]
        shapes += [jax.ShapeDtypeStruct((S, n), BF16), jax.ShapeDtypeStruct((S, D), BF16)]
        specs += [pl.BlockSpec((tm, n), lambda i: (i, 0)), row]
    return _pcall(body, name=name, ins=ins, in_specs=in_specs, dep=dep, grid=(S // tm,), out_specs=specs,
                  out_shape=shapes, compiler_params=_params(1))


def _mm_res(c, name, a, slab, blk_rows, blk_idx, res, dep=None):
    S, D, tm = c.S, c.D, c.tm
    K = NDEV * blk_rows

    def epilogue(acc, in_refs, out_refs):
        out_refs[0][...] = in_refs[2][...] + acc

    return _mm(
        name, [a, slab, res],
        [pl.BlockSpec((tm, K), lambda i, j, k: (i, 0)),
         pl.BlockSpec((NDEV, blk_rows, D), lambda i, j, k: (0, blk_idx, 0)),
         pl.BlockSpec((tm, D), lambda i, j, k: (i, 0))],
        [jax.ShapeDtypeStruct((S, D), F32)], [pl.BlockSpec((tm, D), lambda i, j, k: (i, 0))],
        grid=(S // tm, 1, 1), dims=NN, nk=1, acc_shape=None,
        load_a=_bf(0), load_b=_b_view(1, K), epilogue=epilogue, dep=dep)[0]


def _sgu_masks():
    ii = lax.broadcasted_iota(jnp.int32, (CHUNK, CHUNK), 0)
    jj = lax.broadcasted_iota(jnp.int32, (CHUNK, CHUNK), 1)
    return ii >= jj


def _sgu_fwd(c, name, a, ln_g, wc, b_t, dep=None):
    S, AW, gd, tm = c.S, c.AW, c.gd, c.tms

    def body(a_ref, lng_ref, wc_ref, bt_ref, out_ref):
        va = a_ref[:, AW:].astype(F32)
        xc = va - jnp.mean(va, axis=-1, keepdims=True)
        vn = (xc * lax.rsqrt(jnp.mean(xc * xc, axis=-1, keepdims=True) + EPS) * lng_ref[...]).astype(BF16)
        for ch in range(tm // CHUNK):
            rows = slice(ch * CHUNK, (ch + 1) * CHUNK)
            for g in range(A_GROUPS):
                cols = slice(g * gd, (g + 1) * gd)
                mixed = _dot(wc_ref[g], vn[rows, cols], NN) + bt_ref[:, g:g + 1]
                out_ref[rows, cols] = (a_ref[rows, cols].astype(F32) * mixed).astype(BF16)

    return _pcall(
        body, name=name, ins=[a, ln_g, wc, b_t], dep=dep, grid=(S // tm,),
        in_specs=[pl.BlockSpec((tm, 2 * AW), lambda i: (i, 0)), pl.BlockSpec((1, AW), lambda i: (0, 0)),
                  pl.BlockSpec((A_GROUPS, CHUNK, CHUNK), lambda i: (0, 0, 0)),
                  pl.BlockSpec((CHUNK, A_GROUPS), lambda i: (0, 0))],
        out_specs=pl.BlockSpec((tm, AW), lambda i: (i, 0)),
        out_shape=jax.ShapeDtypeStruct((S, AW), BF16), compiler_params=_params(1))


def _ffn_fwd(c, name, h, g, col, ci, rows, dep=None):
    S, D, fw, tm = c.S, c.D, c.fw, c.tmw
    F = NDEV * fw

    def body(h_ref, g_ref, w1_ref, w2_ref, p_ref, out_ref, hn_ref, r_ref):
        h = h_ref[...]
        hn = (h * _rstd(h) * g_ref[...]).astype(BF16)
        hn_ref[...] = hn
        for j in range(NDEV):
            cols = slice(j * fw, (j + 1) * fw)
            p = jnp.maximum(_dot(hn, w1_ref[j], NT), 0.0)
            p_ref[:, cols] = p.astype(BF16)
            r_ref[:, cols] = (p * p).astype(BF16)
        out_ref[...] = h + _dot(r_ref[...], w2_ref[...].reshape(F, D), NN)

    row = pl.BlockSpec((tm, D), lambda i: (i, 0))
    return _pcall(
        body, name=name, ins=[h, g, col, rows], dep=dep, grid=(S // tm,),
        in_specs=[row, pl.BlockSpec((1, D), lambda i: (0, 0)),
                  _const((NDEV, None, fw, D), lambda i: (0, ci, 0, 0)), _const((NDEV, c.fr, D), lambda i: (0, 0, 0))],
        out_specs=[pl.BlockSpec((tm, F), lambda i: (i, 0)), row, row],
        out_shape=[jax.ShapeDtypeStruct((S, F), BF16), jax.ShapeDtypeStruct((S, D), F32),
                   jax.ShapeDtypeStruct((S, D), BF16)],
        scratch_shapes=[pltpu.VMEM((tm, F), BF16)], compiler_params=_params(1))


def _bucket_table():
    qi = np.arange(BLOCK)[:, None]
    kj = np.arange(2 * BLOCK)[None, :]
    d = np.maximum(qi + BLOCK - kj, 0)
    max_exact = N_BUCKETS // 2
    ratio = np.log(np.maximum(d, 1).astype(np.float32) / np.float32(max_exact)) / np.float32(
        math.log(MAX_DISTANCE / max_exact))
    large = np.minimum(max_exact + (ratio.astype(np.float32) * np.float32(N_BUCKETS - max_exact)).astype(np.int32),
                       N_BUCKETS - 1)
    return np.where(d < max_exact, d, large).astype(np.int32)


def _bucket_onehot():
    b = jnp.asarray(_bucket_table().reshape(1, -1))
    return (b == lax.broadcasted_iota(jnp.int32, (N_BUCKETS, b.shape[1]), 0)).astype(F32)


def _whole(t):
    return pl.BlockSpec(t.shape, lambda: (0,) * t.ndim)


def _band_bias(rel_bias_t, onehot, dep=None):
    def body(r_ref, oh_ref, out_ref):
        out_ref[...] = lax.dot_general(r_ref[...], oh_ref[...], (NN, ((), ())), preferred_element_type=F32,
                                       precision=lax.Precision.HIGHEST)

    n = onehot.shape[1]
    return _pcall(body, name="band_bias", ins=[rel_bias_t, onehot], in_specs=[_whole(rel_bias_t), _whole(onehot)],
                  dep=dep, out_shape=jax.ShapeDtypeStruct((N_HEADS, n), F32), compiler_params=_params(0))


def _band_bias_grad(dbias_list, onehot, dep=None):
    n_in = len(dbias_list)

    def body(*refs):
        oh_ref, out_ref = refs[n_in], refs[n_in + 1]
        d = refs[0][...]
        for r in refs[1:n_in]:
            d = d + r[...]
        out_ref[...] = lax.dot_general(d, oh_ref[...], (NT, ((), ())), preferred_element_type=F32,
                                       precision=lax.Precision.HIGHEST)

    ins = [*dbias_list, onehot]
    return _pcall(body, name="band_bias_grad", ins=ins, in_specs=[_whole(t) for t in ins], dep=dep,
                  out_shape=jax.ShapeDtypeStruct((N_HEADS, N_BUCKETS), F32), compiler_params=_params(0))


KV_PAIRS = N_KV_HEADS // 2
PAIR_ROWS = 2 * Q_PER_KV * BLOCK
MASKED = float(np.finfo(np.float32).min) / 2


def _slot_cols(w):
    lead = w.shape[:-1]
    return w.reshape(*lead, KV_PAIRS, 2, Q_PER_KV, HEAD_DIM).swapaxes(-3, -2).reshape(*lead, N_HEADS * HEAD_DIM)


def _slot_rows(blocks):
    n = blocks.shape[-1]
    return blocks.reshape(KV_PAIRS, 2, Q_PER_KV, HEAD_DIM, n).swapaxes(1, 2).reshape(blocks.shape)


def _slot_bias(bias):
    qi = np.arange(BLOCK)[:, None]
    kj = np.arange(2 * BLOCK)[None, :]
    dist = qi + BLOCK - kj
    window = (dist >= 0) & (dist < BLOCK)
    b = bias.reshape(KV_PAIRS, 2, Q_PER_KV, BLOCK, 2 * BLOCK).swapaxes(1, 2).reshape(KV_PAIRS, PAIR_ROWS, 2 * BLOCK)
    tile = lambda mk: jnp.asarray(np.tile(mk, (2 * Q_PER_KV, 1)))[None]
    return jnp.stack([jnp.where(tile(window & (kj >= BLOCK)), b, MASKED), jnp.where(tile(window), b, MASKED)])


def _pair_kv(kvc_ref, kvp_ref, kvp, dkv):
    lanes = slice(kvp * LANES, (kvp + 1) * LANES)
    vlanes = slice(dkv + kvp * LANES, dkv + (kvp + 1) * LANES)
    k2 = jnp.concatenate([kvp_ref[:, lanes], kvc_ref[:, lanes]], axis=0)
    v2 = jnp.concatenate([kvp_ref[:, vlanes], kvc_ref[:, vlanes]], axis=0)
    return k2, v2


def _head_operand(ref, grp, par, low, scale=None):
    xg = ref[:, grp * LANES:(grp + 1) * LANES]
    if scale is not None:
        xg = xg * scale
    zero = jnp.zeros_like(xg)
    return jnp.where(low, xg, zero) if par == 0 else jnp.where(low, zero, xg)


def _head_probs(qh, k2, bias_rows, sink):
    s = _dot(qh, k2, NT) + bias_rows
    m = jnp.maximum(jnp.max(s, axis=-1, keepdims=True), sink)
    e = jnp.exp(s - m)
    es = jnp.exp(sink - m)
    inv = 1.0 / (jnp.sum(e, axis=-1, keepdims=True) + es)
    return e * inv, es * inv


def _attn_specs(c):
    dq, dkv2 = c.DQ, 2 * c.DKV
    return [pl.BlockSpec((BLOCK, dq), lambda n: (n, 0)),
            pl.BlockSpec((BLOCK, dkv2), lambda n: (n, 0)),
            pl.BlockSpec((BLOCK, dkv2), lambda n: (jnp.maximum(n - 1, 0), 0))]


def _bias_spec():
    return pl.BlockSpec((None, KV_PAIRS, PAIR_ROWS, 2 * BLOCK), lambda n: (jnp.minimum(n, 1), 0, 0, 0))


def _low_lanes():
    return lax.broadcasted_iota(jnp.int32, (BLOCK, LANES), 1) < HEAD_DIM


def _first_key():
    return lax.broadcasted_iota(jnp.int32, (BLOCK, 2 * BLOCK), 1) == 0


def _probs_spec():
    return pl.BlockSpec((None, KV_PAIRS, PAIR_ROWS, 2 * BLOCK), lambda n: (n, 0, 0, 0))


def _attn_fwd(c, name, q, kv, bias, sinks, dep=None):
    S, dq = c.S, c.DQ

    def body(q_ref, kvc_ref, kvp_ref, bias_ref, sink_ref, o_ref, probs_ref):
        low = _low_lanes()
        first = _first_key()
        for kvp in range(KV_PAIRS):
            k2, v2 = _pair_kv(kvc_ref, kvp_ref, kvp, c.DKV)
            for g in range(Q_PER_KV):
                grp = kvp * Q_PER_KV + g
                halves = []
                for par in range(2):
                    rows = slice((2 * g + par) * BLOCK, (2 * g + par + 1) * BLOCK)
                    qh = _head_operand(q_ref, grp, par, low, scale=HEAD_DIM ** -0.5)
                    p, ps = _head_probs(qh, k2, bias_ref[kvp, rows, :], sink_ref[(2 * kvp + par) * Q_PER_KV + g])
                    probs_ref[kvp, rows, :] = jnp.where(first, ps, p).astype(BF16)
                    halves.append(_dot(p.astype(BF16), v2, NN))
                o_ref[:, grp * LANES:(grp + 1) * LANES] = jnp.where(low, halves[0], halves[1]).astype(BF16)

    return _pcall(
        body, name=name, ins=[q, kv, kv, bias, sinks], dep=dep, grid=(c.nb,),
        in_specs=_attn_specs(c) + [_bias_spec(), SMEM],
        out_specs=[pl.BlockSpec((BLOCK, dq), lambda n: (n, 0)), _probs_spec()],
        out_shape=[jax.ShapeDtypeStruct((S, dq), BF16),
                   jax.ShapeDtypeStruct((c.nb, KV_PAIRS, PAIR_ROWS, 2 * BLOCK), BF16)],
        compiler_params=_params(1))


def _final_loss(c, h, g, target, dep=None):
    S, D, tm = c.S, c.D, c.tm

    def body(h_ref, g_ref, t_ref, dh_ref, dg_ref, loss_ref):
        i = pl.program_id(0)
        h = h_ref[...]
        gg = g_ref[...]
        r = _rstd(h)
        xh = h * r
        err = xh * gg - t_ref[...]
        lp = jnp.sum(jnp.sum(err * err, axis=1, keepdims=True), axis=0, keepdims=True) * (0.5 / D)
        dx, dg = _rms_bwd(err * (1.0 / D), h, gg, 0.0)
        dh_ref[...] = dx

        @pl.when(i == 0)
        def _():
            dg_ref[...] = dg
            loss_ref[...] = jnp.broadcast_to(lp, loss_ref.shape)

        @pl.when(i > 0)
        def _():
            dg_ref[...] += dg
            loss_ref[...] += jnp.broadcast_to(lp, loss_ref.shape)

    row = pl.BlockSpec((tm, D), lambda i: (i, 0))
    return _pcall(
        body, name="final_loss", ins=[h, g, target], dep=dep, grid=(S // tm,),
        in_specs=[row, pl.BlockSpec((1, D), lambda i: (0, 0)), row],
        out_specs=[row, pl.BlockSpec((1, D), lambda i: (0, 0)), pl.BlockSpec((1, LANES), lambda i: (0, 0))],
        out_shape=[jax.ShapeDtypeStruct((S, D), F32), jax.ShapeDtypeStruct((1, D), F32),
                   jax.ShapeDtypeStruct((1, LANES), F32)],
        compiler_params=_params(1))


def _rms_bwd_epilogue(h_idx, g_idx, res_idx):
    def epilogue(dhn, in_refs, out_refs):
        dh, dg = _rms_bwd(dhn, in_refs[h_idx][...], in_refs[g_idx][...], in_refs[res_idx][...])
        out_refs[0][...] = dh
        i = pl.program_id(0)

        @pl.when(i == 0)
        def _():
            out_refs[1][...] = dg

        @pl.when(i > 0)
        def _():
            out_refs[1][...] += dg
    return epilogue


def _stream_outs(c, tm):
    S, D = c.S, c.D
    return ([jax.ShapeDtypeStruct((S, D), F32), jax.ShapeDtypeStruct((1, D), F32)],
            [pl.BlockSpec((tm, D), lambda i, j, k: (i, 0)), pl.BlockSpec((1, D), lambda i, j, k: (0, 0))])


def _row_specs(c, tm):
    D = c.D
    return [pl.BlockSpec((tm, D), lambda i, j, k: (i, 0)), pl.BlockSpec((1, D), lambda i, j, k: (0, 0)),
            pl.BlockSpec((tm, D), lambda i, j, k: (i, 0))]


def _bwd_rows_to_stream(c, name, dy_list, slab, blk_rows, blk_idx, n_in_cols, h, g, dres, dep=None):
    S, D, tm = c.S, c.D, c.tm
    nd = len(dy_list)

    def load_a(in_refs, out_refs):
        a = in_refs[0][...]
        for r in in_refs[1:nd]:
            a = a + r[...]
        return a.astype(BF16)

    shapes, specs = _stream_outs(c, tm)
    return _mm(
        name, [*dy_list, slab, h, g, dres],
        [pl.BlockSpec((tm, n_in_cols), lambda i, j, k: (i, 0))] * nd
        + [pl.BlockSpec((NDEV, blk_rows, n_in_cols), lambda i, j, k: (0, blk_idx, 0))] + _row_specs(c, tm),
        shapes, specs, grid=(S // tm, 1, 1), dims=NT, nk=1, acc_shape=None,
        load_a=load_a, load_b=_b_view(nd, NDEV * blk_rows), epilogue=_rms_bwd_epilogue(nd + 1, nd + 2, nd + 3),
        dep=dep)


def _bwd_q_kv_to_stream(c, name, dq, wq, dkv_list, wkv, h, g_q, g_kv, dres, dep=None):
    S, D, tm = c.S, c.D, c.tm
    nkv = len(dkv_list)

    def body(dq_ref, wq_ref, *rest):
        dkv_refs = rest[:nkv]
        wkv_ref, h_ref, gq_ref, gkv_ref, dres_ref, out_ref, dgq_ref, dgkv_ref = rest[nkv:]
        i = pl.program_id(0)
        dhn_q = _dot(dq_ref[...], wq_ref[...].reshape(D, c.DQ), NT)
        dkv = dkv_refs[0][...]
        for r in dkv_refs[1:]:
            dkv = dkv + r[...]
        dhn_kv = _dot(dkv.astype(BF16), wkv_ref[...].reshape(D, 2 * c.DKV), NT)
        h = h_ref[...]
        r = _rstd(h)
        xh = h * r
        dxh = dhn_q * gq_ref[...] + dhn_kv * gkv_ref[...]
        out_ref[...] = dres_ref[...] + r * (dxh - xh * jnp.mean(dxh * xh, axis=-1, keepdims=True))
        dgq = jnp.sum(dhn_q * xh, axis=0, keepdims=True)
        dgkv = jnp.sum(dhn_kv * xh, axis=0, keepdims=True)

        @pl.when(i == 0)
        def _():
            dgq_ref[...] = dgq
            dgkv_ref[...] = dgkv

        @pl.when(i > 0)
        def _():
            dgq_ref[...] += dgq
            dgkv_ref[...] += dgkv

    row = pl.BlockSpec((tm, D), lambda i: (i, 0))
    gain = pl.BlockSpec((1, D), lambda i: (0, 0))
    return _pcall(
        body, name=name, ins=[dq, wq, *dkv_list, wkv, h, g_q, g_kv, dres], dep=dep, grid=(S // tm,),
        in_specs=[pl.BlockSpec((tm, c.DQ), lambda i: (i, 0)), pl.BlockSpec((NDEV, c.qr, c.DQ), lambda i: (0, 0, 0))]
        + [pl.BlockSpec((tm, 2 * c.DKV), lambda i: (i, 0))] * nkv
        + [pl.BlockSpec((NDEV, c.kr, 2 * c.DKV), lambda i: (0, 0, 0)), row, gain, gain, row],
        out_specs=[row, gain, gain],
        out_shape=[jax.ShapeDtypeStruct((S, D), F32), jax.ShapeDtypeStruct((1, D), F32),
                   jax.ShapeDtypeStruct((1, D), F32)],
        compiler_params=_params(1))


def _bwd_cols_to_stream(c, name, dy, col, ci, h, g, dres, dep=None):
    S, D, cw, tm = c.S, c.D, c.cw, c.tmw
    K = NDEV * cw
    shapes, specs = _stream_outs(c, tm)
    return _mm(
        name, [dy, col, h, g, dres],
        [pl.BlockSpec((tm, K), lambda i, j, k: (i, 0)),
         _const((NDEV, None, cw, D), lambda i, j, k: (0, ci, 0, 0))] + _row_specs(c, tm),
        shapes, specs, grid=(S // tm, 1, 1), dims=NN, nk=1, acc_shape=None,
        load_a=_bf(0), load_b=_b_view(1, K), epilogue=_rms_bwd_epilogue(2, 3, 4), dep=dep)


def _bwd_rows_data(c, name, dy, slab, blk_rows, blk_idx, dep=None):
    S, D, tm = c.S, c.D, c.tm
    K = NDEV * blk_rows

    def epilogue(acc, in_refs, out_refs):
        out_refs[0][...] = acc.astype(BF16)

    return _mm(
        name, [dy, slab],
        [pl.BlockSpec((tm, D), lambda i, j, k: (i, 0)),
         pl.BlockSpec((NDEV, blk_rows, D), lambda i, j, k: (0, blk_idx, 0))],
        [jax.ShapeDtypeStruct((S, K), BF16)], [pl.BlockSpec((tm, K), lambda i, j, k: (i, 0))],
        grid=(S // tm, 1, 1), dims=NT, nk=1, acc_shape=None,
        load_a=_bf(0), load_b=_b_view(1, K), epilogue=epilogue, dep=dep)[0]


def _wgrad_rows(c, name, a, b_list, n_a, n_b, unslot=None, dep=None):
    S, tm = c.S, c.tm
    nb_in = len(b_list)
    blk_rows = n_a // NDEV

    def load_b(in_refs):
        b = in_refs[1][...]
        for r in in_refs[2:1 + nb_in]:
            b = b + r[...]
        return b.astype(BF16)

    def epilogue(acc, in_refs, out_refs):
        out = out_refs[0]
        if unslot is None:
            out[...] = acc.reshape(NDEV, blk_rows, n_b).astype(BF16)
            return
        per_blk = blk_rows // HEAD_DIM
        for h in range(N_HEADS):
            s = (h // (2 * Q_PER_KV)) * 2 * Q_PER_KV + (h % Q_PER_KV) * 2 + (h // Q_PER_KV) % 2
            if unslot == "rows":
                out[h // per_blk, (h % per_blk) * HEAD_DIM:(h % per_blk + 1) * HEAD_DIM, :] = (
                    acc[s * HEAD_DIM:(s + 1) * HEAD_DIM, :].astype(BF16))
            else:
                out[:, :, h * HEAD_DIM:(h + 1) * HEAD_DIM] = (
                    acc[:, s * HEAD_DIM:(s + 1) * HEAD_DIM].reshape(NDEV, blk_rows, HEAD_DIM).astype(BF16))

    return _mm(
        name, [a, *b_list],
        [pl.BlockSpec((tm, n_a), lambda i, j, k: (k, 0))] + [pl.BlockSpec((tm, n_b), lambda i, j, k: (k, 0))] * nb_in,
        [jax.ShapeDtypeStruct((NDEV, blk_rows, n_b), BF16)],
        [pl.BlockSpec((NDEV, blk_rows, n_b), lambda i, j, k: (0, 0, 0))],
        grid=(1, 1, S // tm), dims=TN, nk=S // tm, acc_shape=(n_a, n_b),
        load_a=_bf(0), load_b=load_b, epilogue=epilogue, dep=dep)[0]


def _wgrad_cols(c, name, a, b, dep=None):
    S, D, cw = c.S, c.D, c.cw

    def body(a_ref, b_ref, out_ref):
        out_ref[...] = _dot(a_ref[...], b_ref[...], TN).astype(BF16)

    return _pcall(
        body, name=name, ins=[a, b], dep=dep, grid=(NDEV,),
        in_specs=[_const((S, D), lambda j: (0, 0)), pl.BlockSpec((S, cw), lambda j: (0, j))],
        out_specs=pl.BlockSpec((None, D, cw), lambda j: (j, 0, 0)),
        out_shape=jax.ShapeDtypeStruct((NDEV, D, cw), BF16), compiler_params=_params(1))


def _ffn_bwd_data(c, name, dh, p, col, ci, rows, h, g, dep=None):
    S, D, fw, tm = c.S, c.D, c.fw, c.tmw
    F = NDEV * fw

    def body(dh_ref, p_ref, w1t_ref, w2_ref, h_ref, g_ref, da_ref, out_ref, dg_ref, dhb_ref):
        i = pl.program_id(0)
        dh = dh_ref[...]
        dhb = dh.astype(BF16)
        dhb_ref[...] = dhb
        for j in range(NDEV):
            cols = slice(j * fw, (j + 1) * fw)
            da_ref[:, cols] = (_dot(dhb, w2_ref[j], NT) * (2.0 * p_ref[:, cols].astype(F32))).astype(BF16)
        dx, dg = _rms_bwd(_dot(da_ref[...], w1t_ref[...].reshape(F, D), NN), h_ref[...], g_ref[...], dh)
        out_ref[...] = dx

        @pl.when(i == 0)
        def _():
            dg_ref[...] = dg

        @pl.when(i > 0)
        def _():
            dg_ref[...] += dg

    row = pl.BlockSpec((tm, D), lambda i: (i, 0))
    wide = pl.BlockSpec((tm, F), lambda i: (i, 0))
    return _pcall(
        body, name=name, ins=[dh, p, col, rows, h, g], dep=dep, grid=(S // tm,),
        in_specs=[row, wide, _const((NDEV, None, fw, D), lambda i: (0, ci, 0, 0)),
                  _const((NDEV, c.fr, D), lambda i: (0, 0, 0)),
                  row, pl.BlockSpec((1, D), lambda i: (0, 0))],
        out_specs=[wide, row, pl.BlockSpec((1, D), lambda i: (0, 0)), row],
        out_shape=[jax.ShapeDtypeStruct((S, F), BF16), jax.ShapeDtypeStruct((S, D), F32),
                   jax.ShapeDtypeStruct((1, D), F32), jax.ShapeDtypeStruct((S, D), BF16)],
        compiler_params=_params(1))


def _ffn_bwd_w(c, name, hn, da, p, dhb, dep=None):
    S, D, fw = c.S, c.D, c.fw

    def body(hn_ref, da_ref, p_ref, dhb_ref, dw1_ref, dw2_ref):
        dw1_ref[...] = _dot(hn_ref[...], da_ref[...], TN).astype(BF16)
        pf = p_ref[...].astype(F32)
        dw2_ref[...] = _dot((pf * pf).astype(BF16), dhb_ref[...], TN).astype(BF16)

    panel = pl.BlockSpec((S, fw), lambda j: (0, j))
    return _pcall(
        body, name=name, ins=[hn, da, p, dhb], dep=dep, grid=(NDEV,),
        in_specs=[_const((S, D), lambda j: (0, 0)), panel, panel, _const((S, D), lambda j: (0, 0))],
        out_specs=[pl.BlockSpec((None, D, fw), lambda j: (j, 0, 0)), pl.BlockSpec((None, c.fr, D), lambda j: (j, 0, 0))],
        out_shape=[jax.ShapeDtypeStruct((NDEV, D, fw), BF16), jax.ShapeDtypeStruct((NDEV, c.fr, D), BF16)],
        compiler_params=_params(1))


def _attn_bwd(c, name, q, kv, do, probs, dep=None):
    S, dq, dkv = c.S, c.DQ, c.DKV
    nb = c.nb
    scale = HEAD_DIM ** -0.5

    def body(q_ref, kvc_ref, kvp_ref, do_ref, probs_ref, dq_ref, dkv_ref, dbias_ref, dsink_ref, dsink_acc,
             ds_sc, p_sc, qm_sc, dom_sc):
        n = pl.program_id(0)

        @pl.when(n == 0)
        def _():
            dkv_ref[...] = jnp.zeros_like(dkv_ref)
            dbias_ref[...] = jnp.zeros_like(dbias_ref)
            dsink_acc[...] = jnp.zeros_like(dsink_acc)

        low = _low_lanes()
        first = _first_key()
        rows_c = pl.ds(pl.multiple_of(n * BLOCK, BLOCK), BLOCK)
        rows_p = pl.ds(pl.multiple_of(jnp.maximum(n - 1, 0) * BLOCK, BLOCK), BLOCK)
        for kvp in range(KV_PAIRS):
            k2, v2 = _pair_kv(kvc_ref, kvp_ref, kvp, dkv)
            for g in range(Q_PER_KV):
                grp = kvp * Q_PER_KV + g
                halves = []
                for par in range(2):
                    rows = slice((2 * g + par) * BLOCK, (2 * g + par + 1) * BLOCK)
                    qh = _head_operand(q_ref, grp, par, low, scale=scale)
                    doh = _head_operand(do_ref, grp, par, low)
                    saved = probs_ref[kvp, rows, :]
                    ps = saved[:, 0:1].astype(F32)
                    p16 = jnp.where(first, jnp.zeros_like(saved), saved)
                    p = p16.astype(F32)
                    dp = _dot(doh, v2, NT)
                    delta = jnp.sum(p * dp, axis=-1, keepdims=True)
                    ds = p * (dp - delta)
                    dbias_ref[kvp, rows, :] += ds
                    dsink_acc[rows, kvp:kvp + 1] += -(ps * delta)
                    ds16 = ds.astype(BF16)
                    halves.append(_dot(ds16, k2, NN) * scale)
                    ds_sc[rows, :] = ds16
                    p_sc[rows, :] = p16
                    qm_sc[rows, :] = qh
                    dom_sc[rows, :] = doh
                dq_ref[:, grp * LANES:(grp + 1) * LANES] = jnp.where(low, halves[0], halves[1]).astype(BF16)
            dk2 = _dot(ds_sc[...], qm_sc[...], TN)
            dv2 = _dot(p_sc[...], dom_sc[...], TN)
            lanes = slice(kvp * LANES, (kvp + 1) * LANES)
            vlanes = slice(dkv + kvp * LANES, dkv + (kvp + 1) * LANES)
            dkv_ref[rows_p, lanes] += dk2[:BLOCK]
            dkv_ref[rows_c, lanes] += dk2[BLOCK:]
            dkv_ref[rows_p, vlanes] += dv2[:BLOCK]
            dkv_ref[rows_c, vlanes] += dv2[BLOCK:]

        @pl.when(n == nb - 1)
        def _():
            dsink_ref[...] = jnp.sum(dsink_acc[...].reshape(2 * Q_PER_KV, BLOCK, KV_PAIRS), axis=1)

    return _pcall(
        body, name=name, ins=[q, kv, kv, do, probs], dep=dep, grid=(nb,),
        in_specs=_attn_specs(c) + [pl.BlockSpec((BLOCK, dq), lambda n: (n, 0)), _probs_spec()],
        out_specs=[pl.BlockSpec((BLOCK, dq), lambda n: (n, 0)), pl.BlockSpec((S, 2 * dkv), lambda n: (0, 0)),
                   pl.BlockSpec((KV_PAIRS, PAIR_ROWS, 2 * BLOCK), lambda n: (0, 0, 0)),
                   pl.BlockSpec((2 * Q_PER_KV, KV_PAIRS), lambda n: (0, 0))],
        out_shape=[jax.ShapeDtypeStruct((S, dq), BF16), jax.ShapeDtypeStruct((S, 2 * dkv), F32),
                   jax.ShapeDtypeStruct((KV_PAIRS, PAIR_ROWS, 2 * BLOCK), F32),
                   jax.ShapeDtypeStruct((2 * Q_PER_KV, KV_PAIRS), F32)],
        scratch_shapes=[pltpu.VMEM((PAIR_ROWS, KV_PAIRS), F32), pltpu.VMEM((PAIR_ROWS, 2 * BLOCK), BF16),
                        pltpu.VMEM((PAIR_ROWS, 2 * BLOCK), BF16), pltpu.VMEM((PAIR_ROWS, LANES), BF16),
                        pltpu.VMEM((PAIR_ROWS, LANES), BF16)],
        compiler_params=_params(1))


def _sgu_bwd(c, name, a, z, dgated, ln_g, wc, wc_t, b_t, dep=None):
    S, AW, gd, tm = c.S, c.AW, c.gd, c.tms

    def body(a_ref, z_ref, dg_ref, lng_ref, wc_ref, wct_ref, bt_ref, dz_ref, dws_ref, dbt_ref, dlng_ref, dvn_ref):
        i = pl.program_id(0)

        @pl.when(i == 0)
        def _():
            dws_ref[...] = jnp.zeros_like(dws_ref)
            dbt_ref[...] = jnp.zeros_like(dbt_ref)
            dlng_ref[...] = jnp.zeros_like(dlng_ref)

        lng = lng_ref[...]
        va = a_ref[:, AW:].astype(F32)
        xc = va - jnp.mean(va, axis=-1, keepdims=True)
        rstd = lax.rsqrt(jnp.mean(xc * xc, axis=-1, keepdims=True) + EPS)
        xh = xc * rstd
        vn = (xh * lng).astype(BF16)
        causal = _sgu_masks()
        for ch in range(tm // CHUNK):
            rows = slice(ch * CHUNK, (ch + 1) * CHUNK)
            for g in range(A_GROUPS):
                cols = slice(g * gd, (g + 1) * gd)
                blk = vn[rows, cols]
                mixed = _dot(wc_ref[g], blk, NN) + bt_ref[:, g:g + 1]
                dgb = dg_ref[rows, cols].astype(F32)
                dm = dgb * a_ref[rows, cols].astype(F32)
                dbt_ref[:, g:g + 1] += jnp.sum(dm, axis=1, keepdims=True)
                dm16 = dm.astype(BF16)
                dws_ref[g] += jnp.where(causal, _dot(dm16, blk, NT), 0.0)
                dvn_ref[rows, cols] = _dot(wct_ref[g], dm16, NN)
                dz_ref[rows, cols] = (dgb * mixed * _gelu_grad(z_ref[rows, cols].astype(F32))).astype(BF16)
        dvn = dvn_ref[...]
        dlng_ref[...] += jnp.sum(dvn * xh, axis=0, keepdims=True)
        dxh = dvn * lng
        dva = rstd * (dxh - jnp.mean(dxh, axis=-1, keepdims=True) - xh * jnp.mean(dxh * xh, axis=-1, keepdims=True))
        dz_ref[:, AW:] = (dva * _gelu_grad(z_ref[:, AW:].astype(F32))).astype(BF16)

    wide = pl.BlockSpec((tm, 2 * AW), lambda i: (i, 0))
    wsp = pl.BlockSpec((A_GROUPS, CHUNK, CHUNK), lambda i: (0, 0, 0))
    btsp = pl.BlockSpec((CHUNK, A_GROUPS), lambda i: (0, 0))
    return _pcall(
        body, name=name, ins=[a, z, dgated, ln_g, wc, wc_t, b_t], dep=dep, grid=(S // tm,),
        in_specs=[wide, wide, pl.BlockSpec((tm, AW), lambda i: (i, 0)), pl.BlockSpec((1, AW), lambda i: (0, 0)),
                  wsp, wsp, btsp],
        out_specs=[wide, wsp, btsp, pl.BlockSpec((1, AW), lambda i: (0, 0))],
        out_shape=[jax.ShapeDtypeStruct((S, 2 * AW), BF16), jax.ShapeDtypeStruct((A_GROUPS, CHUNK, CHUNK), F32),
                   jax.ShapeDtypeStruct((CHUNK, A_GROUPS), F32), jax.ShapeDtypeStruct((1, AW), F32)],
        scratch_shapes=[pltpu.VMEM((tm, AW), F32)], compiler_params=_params(1))


def _adam_update(g, w_ref, m_ref, v_ref, out_refs):
    g_ref, d_ref, nm_ref, nv_ref = out_refs
    bc1 = 1.0 - ADAM_B1 ** ADAM_STEP
    bc2 = 1.0 - ADAM_B2 ** ADAM_STEP
    nm = ADAM_B1 * m_ref[...] + (1.0 - ADAM_B1) * g
    nv = ADAM_B2 * v_ref[...] + (1.0 - ADAM_B2) * (g * g)
    g_ref[...] = g
    nm_ref[...] = nm
    nv_ref[...] = nv
    d_ref[...] = -ADAM_LR * ((nm * (1.0 / bc1)) / (jnp.sqrt(nv * (1.0 / bc2)) + ADAM_EPS) + ADAM_WD * w_ref[...])


def _adamw_packed(name, sums, offs, ws, ms, vs, dep=None):
    n = len(ws)

    def body(*refs):
        s_ref, w_refs, m_refs, v_refs = refs[0], refs[1:1 + n], refs[1 + n:1 + 2 * n], refs[1 + 2 * n:1 + 3 * n]
        outs = refs[1 + 3 * n:]
        for k in range(n):
            _adam_update(s_ref[offs[k]:offs[k + 1], :], w_refs[k], m_refs[k], v_refs[k], outs[4 * k:4 * k + 4])

    ins = [sums, *ws, *ms, *vs]
    shapes = [jax.ShapeDtypeStruct(w.shape, F32) for w in ws for _ in range(4)]
    return _pcall(body, name=name, ins=ins, in_specs=[_whole(t) for t in ins], dep=dep,
                  out_shape=shapes, out_specs=[_whole(t) for t in shapes], compiler_params=_params(0))


def _adamw(name, parts, part_block, part_index, w, m, v, tr, row_off=0, n_rows=None, prev=None, dep=None):
    R, C = w.shape
    n_rows = R if n_rows is None else n_rows
    assert n_rows % tr == 0 and row_off % tr == 0

    def body(p_ref, w_ref, m_ref, v_ref, *rest):
        g = p_ref[0].astype(F32)
        for s in range(1, part_block[0]):
            g = g + p_ref[s].astype(F32)
        _adam_update(g, w_ref, m_ref, v_ref, rest[-4:])

    ob = row_off // tr
    row = pl.BlockSpec((tr, C), lambda i: (ob + i, 0))
    out = jax.ShapeDtypeStruct((R, C), F32)
    chained = prev is not None
    return _pcall(
        body, name=name, ins=[parts, w, m, v] + (list(prev) if chained else []), dep=dep, grid=(n_rows // tr,),
        in_specs=[pl.BlockSpec(part_block, part_index), row, row, row] + ([ANY] * 4 if chained else []),
        out_specs=[row, row, row, row], out_shape=[out, out, out, out],
        input_output_aliases={4 + t: t for t in range(4)} if chained else {}, compiler_params=_params(1))


def _sum_parts(name, parts, dep=None):
    def body(p_ref, out_ref):
        g = p_ref[0]
        for s in range(1, parts.shape[0]):
            g = g + p_ref[s]
        out_ref[...] = g

    return _pcall(body, name=name, ins=[parts], in_specs=[_whole(parts)], dep=dep,
                  out_shape=jax.ShapeDtypeStruct(parts.shape[1:], F32), compiler_params=_params(0))


def _place():
    return lax.axis_index("x"), lax.axis_index("y"), lax.axis_index("c")


def _slot(px, py, pc):
    return 4 * px + 2 * py + pc


def _peer(k, x, y, c):
    return x ^ ((k >> 2) & 1), y ^ ((k >> 1) & 1), c ^ (k & 1)


SEND_PEERS = {"exchange": tuple(range(1, NDEV)), "gather": (1, 2, 4, 6), "forward": (2, 4, 6),
              "broadcast": tuple(range(1, NDEV))}


def _n_sems(mode, n_lands):
    return n_lands * (len(SEND_PEERS[mode]) + (mode != "forward"))


def _send_copies(mode, src_refs, land_refs, send_sems, recv_sems):
    x, y, c = _place()
    me = _slot(x, y, c)
    peers = SEND_PEERS[mode]
    remote, local = [], []
    for i, k in enumerate(peers):
        peer = _peer(k, x, y, c)
        for a, land in enumerate(land_refs):
            if mode == "exchange":
                src, dst, to = src_refs[a].at[_slot(*peer)], land.at[me], peer
            elif mode in ("gather", "broadcast"):
                src, dst, to = src_refs[a], land.at[me], peer
            else:
                src = dst = land.at[_slot(*peer)]
                to = (x, y, 1 - c)
            s = a * len(peers) + i
            remote.append(pltpu.make_async_remote_copy(src_ref=src, dst_ref=dst, send_sem=send_sems.at[s],
                                                       recv_sem=recv_sems.at[s], device_id=to, device_id_type=MESH))
    if mode != "forward":
        for a, land in enumerate(land_refs):
            src = src_refs[a].at[me] if mode == "exchange" else src_refs[a]
            local.append(pltpu.make_async_copy(src, land.at[me], send_sems.at[len(land_refs) * len(peers) + a]))
    return remote, local


def _send_start_groups(name, groups, mode, collective_id=None, dep=None):
    sizes = [(len(s), len(l)) for s, l in groups]
    flat = [t for s, l in groups for t in (*s, *l)]
    n_in, ng = len(flat), len(groups)

    def body(*refs):
        sems, token, at = refs[n_in:n_in + 2 * ng], refs[-1], 0
        if collective_id is not None:
            x, y, c = _place()
            barrier = pltpu.get_barrier_semaphore()
            shake = (1,) if mode == "forward" else SEND_PEERS[mode]
            for k in shake:
                pl.semaphore_signal(barrier, inc=1, device_id=_peer(k, x, y, c), device_id_type=MESH)
            pl.semaphore_wait(barrier, len(shake))
        for gi, (ns, nl) in enumerate(sizes):
            remote, local = _send_copies(mode, refs[at:at + ns], refs[at + ns:at + ns + nl], sems[2 * gi],
                                         sems[2 * gi + 1])
            for cp in remote + local:
                cp.start()
            at += ns + nl
        token[...] = jnp.zeros_like(token)

    sem_shapes = [pltpu.SemaphoreType.DMA((_n_sems(mode, nl),)) for _, nl in sizes for _ in range(2)]
    if any(dep is t for t in flat):
        dep = None
    out = _pcall(
        body, name=name, ins=[pltpu.with_memory_space_constraint(t, pltpu.HBM) for t in flat],
        in_specs=[HBM] * n_in, dep=dep,
        out_shape=(*sem_shapes, *[pltpu.HBM(t.shape, t.dtype) for t in flat], jax.ShapeDtypeStruct((8, LANES), F32)),
        out_specs=(*[SEM] * (2 * ng), *[HBM] * n_in, pl.BlockSpec(memory_space=pltpu.VMEM)),
        input_output_aliases={i: 2 * ng + i for i in range(n_in)},
        compiler_params=pltpu.CompilerParams(has_side_effects=pltpu.SideEffectType.DATAFLOW_SIDE_EFFECTING,
                                             collective_id=collective_id))
    started, at = [], 2 * ng
    for gi, (ns, nl) in enumerate(sizes):
        started.append((out[-1], out[2 * gi], out[2 * gi + 1], list(out[at:at + ns]), list(out[at + ns:at + ns + nl])))
        at += ns + nl
    return started


def _send_start(name, srcs, lands, mode, collective_id=None, dep=None):
    if mode == "forward":
        collective_id = SIBLING_COLLECTIVE_ID
    return _send_start_groups(name, [(srcs, lands)], mode, collective_id=collective_id, dep=dep)[0]


def _send_wait(name, started, mode, dep=None):
    _, send_sems, recv_sems, srcs, lands = started
    n_src, n = len(srcs), len(lands)

    def body(*refs):
        src_refs, land_refs = refs[:n_src], refs[n_src:n_src + n]
        ssem, rsem = refs[n_src + n], refs[n_src + n + 1]
        remote, local = _send_copies(mode, src_refs, land_refs, ssem, rsem)
        for cp in remote:
            cp.wait_send()
            cp.wait_recv()
        for cp in local:
            cp.wait()

    thru = [pltpu.HBM(t.shape, t.dtype) for t in [*srcs, *lands]]
    out = _pcall(
        body, name=name, ins=[*srcs, *lands, send_sems, recv_sems], in_specs=[HBM] * (n_src + n) + [SEM, SEM], dep=dep,
        out_shape=tuple(thru), out_specs=tuple([HBM] * (n_src + n)),
        input_output_aliases={i: i for i in range(n_src + n)},
        compiler_params=pltpu.CompilerParams(has_side_effects=pltpu.SideEffectType.DATAFLOW_SIDE_EFFECTING))
    return list(out[n_src:])


def _landing(block):
    return lax.empty((NDEV, *block.shape), block.dtype)


def _rows128(t):
    flat = t.reshape(-1)
    n = flat.shape[0]
    rows = -(-n // (8 * LANES)) * 8
    return jnp.pad(flat, (0, rows * LANES - n)).reshape(rows, LANES)


def kernel(x, mix_norm_g, ffn_norm_g, a_w_in, a_ln_g, a_w_spatial, a_b_spatial, a_w_out, kv_norm_g, w_k, w_v, b_w_q, b_sinks, b_w_o, rel_bias, ffn_w1, ffn_w2, final_norm_g, loss_target, m_mix_norm_g, m_ffn_norm_g, m_a_w_in, m_a_ln_g, m_a_w_spatial, m_a_b_spatial, m_a_w_out, m_kv_norm_g, m_w_k, m_w_v, m_b_w_q, m_b_sinks, m_b_w_o, m_rel_bias, m_ffn_w1, m_ffn_w2, m_final_norm_g, v_mix_norm_g, v_ffn_norm_g, v_a_w_in, v_a_ln_g, v_a_w_spatial, v_a_b_spatial, v_a_w_out, v_kv_norm_g, v_w_k, v_w_v, v_b_w_q, v_b_sinks, v_b_w_o, v_rel_bias, v_ffn_w1, v_ffn_w2, v_final_norm_g):
    c = _config(x, a_w_in, a_w_out, w_k, b_w_q, b_w_o, ffn_w1, ffn_w2)
    S, D, LA, LB, LF = c.S, c.D, c.LA, c.LB, c.LF
    weights = dict(mix_norm_g=mix_norm_g, ffn_norm_g=ffn_norm_g, a_w_in=a_w_in, a_ln_g=a_ln_g, a_w_spatial=a_w_spatial,
                   a_b_spatial=a_b_spatial, a_w_out=a_w_out, kv_norm_g=kv_norm_g, w_k=w_k, w_v=w_v, b_w_q=b_w_q,
                   b_sinks=b_sinks, b_w_o=b_w_o, rel_bias=rel_bias, ffn_w1=ffn_w1, ffn_w2=ffn_w2,
                   final_norm_g=final_norm_g)
    m_in = dict(mix_norm_g=m_mix_norm_g, ffn_norm_g=m_ffn_norm_g, a_w_in=m_a_w_in, a_ln_g=m_a_ln_g,
                a_w_spatial=m_a_w_spatial, a_b_spatial=m_a_b_spatial, a_w_out=m_a_w_out, kv_norm_g=m_kv_norm_g,
                w_k=m_w_k, w_v=m_w_v, b_w_q=m_b_w_q, b_sinks=m_b_sinks, b_w_o=m_b_w_o, rel_bias=m_rel_bias,
                ffn_w1=m_ffn_w1, ffn_w2=m_ffn_w2, final_norm_g=m_final_norm_g)
    v_in = dict(mix_norm_g=v_mix_norm_g, ffn_norm_g=v_ffn_norm_g, a_w_in=v_a_w_in, a_ln_g=v_a_ln_g,
                a_w_spatial=v_a_w_spatial, a_b_spatial=v_a_b_spatial, a_w_out=v_a_w_out, kv_norm_g=v_kv_norm_g,
                w_k=v_w_k, w_v=v_w_v, b_w_q=v_b_w_q, b_sinks=v_b_sinks, b_w_o=v_b_w_o, rel_bias=v_rel_bias,
                ffn_w1=v_ffn_w1, ffn_w2=v_ffn_w2, final_norm_g=v_final_norm_g)
    names = list(weights)
    seq = _Seq()
    me = _slot(*_place())
    bf = lambda t: t.astype(BF16)

    tr = lambda t: bf(jnp.swapaxes(t, -1, -2))

    def start(tag, some):
        got = seq(_send_start_groups, f"weights_start_{tag}", [(grp, [_landing(t) for t in grp]) for grp in some],
                  "gather")
        seq.last = got[0][0]
        return got

    started = start("first", [[tr(a_w_in[0])[None], a_ln_g]])
    _, a_w_in_, a_w_out_, ffn_w1_, ffn_w2_, w_k_, w_v_, b_w_q_, b_w_o_ = lax.optimization_barrier(
        (started[0][0], a_w_in, a_w_out, ffn_w1, ffn_w2, w_k, w_v, b_w_q, b_w_o))
    groups = []
    for l in range(LA):
        groups += [[tr(a_w_in_[l])[None]], [bf(a_w_out_[l])], [tr(ffn_w1_[l])[None], bf(ffn_w2_[l])]]
    gb = 3 * LA
    for l in range(LB):
        extra = [bf(jnp.concatenate([w_k_, w_v_], axis=1))] if l == 0 else []
        groups += [extra + [bf(_slot_cols(b_w_q_[l])), bf(b_w_o_[l])], [tr(ffn_w1_[LA + l])[None], bf(ffn_w2_[LA + l])]]
    started += start("rest", groups[1:])
    forwarding = {}

    def forward(i):
        lands = seq(_send_wait, f"weights_wait{i}", started[i], "gather")
        forwarding[i] = seq(_send_start, f"weights_forward{i}", [], lands, "forward")

    def arrive(i):
        if i not in forwarding:
            forward(i)
        return seq(_send_wait, f"weights_arrive{i}", forwarding[i], "forward")

    causal = jnp.tril(jnp.ones((CHUNK, CHUNK), bool))
    wsp = jnp.where(causal[None, None], a_w_spatial, 0.0)
    wsp16 = wsp.astype(BF16)
    wsp16_t = jnp.swapaxes(wsp, -1, -2).astype(BF16)
    bsp_t = jnp.swapaxes(a_b_spatial, -1, -2)
    mix_g = mix_norm_g.reshape(-1, 1, D)
    ffn_g = ffn_norm_g.reshape(-1, 1, D)
    kv_g = kv_norm_g.reshape(1, D)
    fin_g = final_norm_g.reshape(1, D)
    onehot = _bucket_onehot()
    bias = _slot_bias(seq(_band_bias, rel_bias.T, onehot).reshape(N_HEADS, BLOCK, 2 * BLOCK))

    h = x.reshape(S, D)
    sav_a, sav_b, wts_a, wts_b = [], [], [], []
    for l in range(LA):
        got = arrive(3 * l)
        w_in = got[0]
        if l == 0:
            ln_g_full = jnp.transpose(got[1], (1, 0, 2)).reshape(LA, 1, c.AW)
        z, a, hn = seq(_a_in_fwd, c, f"a_in_fwd{l}", h, mix_g[l], w_in, 0)
        forward(3 * l + 1)
        gated = seq(_sgu_fwd, c, f"sgu_fwd{l}", a, ln_g_full[l], wsp16[l], bsp_t[l])
        (wout,) = arrive(3 * l + 1)
        if l > 0:
            forward(3 * l + 2)
        h1 = seq(_mm_res, c, f"a_out_fwd{l}", gated, wout, c.ar, 0, h)
        w1, rows = arrive(3 * l + 2)
        if l == LA - 1:
            forward(gb)
        p, h2, hnf = seq(_ffn_fwd, c, f"ffn_fwd{l}", h1, ffn_g[l], w1, 0, rows)
        sav_a.append((h, z, a, hn, gated, h1, p, hnf))
        wts_a.append((w_in, 0, w1, 0, rows, wout, 0))
        h = h2
    h_kv = h
    for l in range(LB):
        got = arrive(gb + 2 * l)
        if l == 0:
            wkv, got = got[0], got[1:]
        wq, wo = got[0], _slot_rows(got[1])
        if l == 0:
            kv, hkv, q, hn = seq(_rms2_mm_rows, c, "kv_q_fwd", h, [(kv_g, wkv), (mix_g[LA], wq)])
        else:
            q, hn = seq(_rms_mm_rows, c, f"q_fwd{l}", h, mix_g[LA + l], wq, c.qr, 0, c.DQ)
        forward(gb + 2 * l + 1)
        o, probs = seq(_attn_fwd, c, f"attn_fwd{l}", q, kv, bias, b_sinks[l])
        h1 = seq(_mm_res, c, f"o_fwd{l}", o, wo, c.orr, 0, h)
        w1, rows = arrive(gb + 2 * l + 1)
        if l + 1 < LB:
            forward(gb + 2 * l + 2)
        p, h2, hnf = seq(_ffn_fwd, c, f"ffn_fwd{LA + l}", h1, ffn_g[LA + l], w1, 0, rows)
        sav_b.append((h, q, hn, o, probs, h1, p, hnf))
        wts_b.append((wq, wo, w1, rows))
        h = h2
    dh, d_fin_g, loss_row = seq(_final_loss, c, h, fin_g, loss_target.reshape(S, D))

    results = {}
    in_flight, exchanges = [], []

    def update(k, parts, layer, col_blk=0):
        w = weights[k]
        rows_l, ncols = (w.shape[-2], w.shape[-1]) if w.ndim == 3 else w.shape
        flat = lambda t: t.reshape(-1, ncols)
        tr = min(256, rows_l)
        results[k] = seq(_adamw, f"adamw_{k}{layer}", parts, (NDEV, tr, ncols), lambda i: (0, i, col_blk),
                         flat(w), flat(m_in[k]), flat(v_in[k]), tr, row_off=layer * rows_l, n_rows=rows_l,
                         prev=results.get(k))

    def land(tag, entry):
        lands = seq(_send_wait, f"grads_wait_{tag}", entry[1], "exchange")
        for keys, parts in zip(entry[0], lands):
            for k, layer, col_blk in keys:
                update(k, parts, layer, col_blk)

    def send(tag, items):
        slabs = [t for _, t in items]
        own = [lax.empty(t.shape, t.dtype) for t in slabs]
        exchanges.append(tag)
        st = seq(_send_start, f"grads_start_{tag}", slabs, own, "exchange",
                 collective_id=SIBLING_COLLECTIVE_ID + len(exchanges))
        in_flight.append((tag, ([k for k, _ in items], st)))
        while len(in_flight) > EXCHANGE_LAG:
            land(*in_flight.pop(0))

    d_mix_g, d_ffn_g = [None] * LF, [None] * LF
    dkv_list, dbias_list, dsink_list = [], [], [None] * LB

    def ffn_bwd(lf, dh, h1, p, hnf, w1, w1_i, rows):
        da, dh1, d_ffn_g[lf], dhb = seq(_ffn_bwd_data, c, f"ffn_bwd_data{lf}", dh, p, w1, w1_i, rows, h1, ffn_g[lf])
        dw1, dw2 = seq(_ffn_bwd_w, c, f"ffn_bwd_w{lf}", hnf, da, p, dhb)
        send(f"ffn{lf}", [([("ffn_w1", lf, 0)], dw1), ([("ffn_w2", lf, 0)], dw2)])
        return dh1

    for l in reversed(range(LB)):
        h0, q, hn, o, probs, h1, p, hnf = sav_b[l]
        wq, wo, w1, rows = wts_b[l]
        dh1 = ffn_bwd(LA + l, dh, h1, p, hnf, w1, 0, rows)
        do = seq(_bwd_rows_data, c, f"o_bwd_data{l}", dh1, wo, c.orr, 0)
        dwo = seq(_wgrad_rows, c, f"o_bwd_w{l}", o, [dh1], c.DQ, D, unslot="rows")
        dq, dkv, dbias, dsink = seq(_attn_bwd, c, f"attn_bwd{l}", q, kv, do, probs)
        dsink_list[l] = dsink.reshape(Q_PER_KV, 2, KV_PAIRS).transpose(2, 1, 0).reshape(1, N_HEADS)
        dkv_list.append(dkv)
        dbias_list.append(dbias.reshape(N_HEADS, -1))
        dwq = seq(_wgrad_rows, c, f"q_bwd_w{l}", hn, [dq], D, c.DQ, unslot="cols")
        send(f"attn{l}", [([("b_w_o", l, 0)], dwo), ([("b_w_q", l, 0)], dwq)])
        if l > 0:
            dh, d_mix_g[LA + l] = seq(_bwd_rows_to_stream, c, f"q_bwd_data{l}", [dq], wq, c.qr, 0, c.DQ, h0,
                                      mix_g[LA + l], dh1)
    dwkv = seq(_wgrad_rows, c, "kv_bwd_w", hkv, dkv_list, D, 2 * c.DKV)
    send("kv", [([("w_k", 0, 0), ("w_v", 0, 1)], dwkv)])
    dh, d_mix_g[LA], d_kv_g = seq(_bwd_q_kv_to_stream, c, "q_kv_bwd_data", dq, wq, dkv_list, wkv, h_kv, mix_g[LA],
                                  kv_g, dh1)
    d_rel_t = seq(_band_bias_grad, dbias_list, onehot)
    d_rel_t = d_rel_t.reshape(KV_PAIRS, Q_PER_KV, 2, N_BUCKETS).swapaxes(1, 2).reshape(N_HEADS, N_BUCKETS)
    d_wsp, d_bsp, d_lng = [None] * LA, [None] * LA, [None] * LA
    for l in reversed(range(LA)):
        h0, z, a, hn, gated, h1, p, hnf = sav_a[l]
        w_in, in_i, w1, w1_i, rows, wout, wout_i = wts_a[l]
        dh1 = ffn_bwd(l, dh, h1, p, hnf, w1, w1_i, rows)
        dgated = seq(_bwd_rows_data, c, f"a_out_bwd_data{l}", dh1, wout, c.ar, wout_i)
        dwout = seq(_wgrad_rows, c, f"a_out_bwd_w{l}", gated, [dh1], c.AW, D)
        send(f"a_out{l}", [([("a_w_out", l, 0)], dwout)])
        dz, d_wsp[l], dbt, d_lng[l] = seq(_sgu_bwd, c, f"sgu_bwd{l}", a, z, dgated, ln_g_full[l], wsp16[l],
                                          wsp16_t[l], bsp_t[l])
        d_bsp[l] = dbt.T
        dwin = seq(_wgrad_cols, c, f"a_in_bwd_w{l}", hn, dz)
        send(f"a_in{l}", [([("a_w_in", l, 0)], dwin)])
        dh, d_mix_g[l] = seq(_bwd_cols_to_stream, c, f"a_in_bwd_data{l}", dz, w_in, in_i, h0, mix_g[l], dh1)
    grad_x = dh.reshape(1, S, D)

    small = {
        "mix_norm_g": jnp.concatenate(d_mix_g, axis=0), "ffn_norm_g": jnp.concatenate(d_ffn_g, axis=0),
        "a_w_spatial": jnp.stack(d_wsp), "a_b_spatial": jnp.stack(d_bsp), "kv_norm_g": d_kv_g,
        "b_sinks": jnp.concatenate(dsink_list, axis=0), "rel_bias": d_rel_t.T, "final_norm_g": d_fin_g,
    }
    small_names = list(small)
    packs = [_rows128(small[k]) for k in small_names] + [_rows128(jnp.concatenate(d_lng, axis=0)), _rows128(loss_row)]
    offs = [int(o) for o in np.cumsum([0] + [p.shape[0] for p in packs])]
    Rs = offs[-1] + (-offs[-1]) % (8 * NDEV)
    packed = jnp.concatenate(packs + [jnp.zeros((Rs - offs[-1], LANES), F32)], axis=0)
    slab = packed.reshape(NDEV, Rs // NDEV, LANES)
    st = seq(_send_start, "small_grads_start", [slab], [lax.empty(slab.shape, slab.dtype)], "exchange")
    while len(in_flight) > 1:
        land(*in_flight.pop(0))
    (parts,) = seq(_send_wait, "small_grads_wait", st, "exchange")
    mine = seq(_sum_parts, "small_grads_sum", parts)
    st = seq(_send_start, "small_sums_start", [mine], [_landing(mine)], "broadcast")
    while in_flight:
        land(*in_flight.pop(0))
    (sums,) = seq(_send_wait, "small_sums_wait", st, "broadcast")
    sums = sums.reshape(Rs, LANES)
    loss = sums[offs[-2], 0]

    grads, deltas, new_m, new_v = {}, {}, {}, {}

    def put(k, outs, shape):
        grads[k], deltas[k], new_m[k], new_v[k] = (t.reshape(shape) for t in outs)

    outs = seq(_adamw_packed, "adamw_small", sums, offs, *[[_rows128(d[k]) for k in small_names]
                                                          for d in (weights, m_in, v_in)])
    for n_, k in enumerate(small_names):
        shape = weights[k].shape
        size = int(np.prod(shape))
        put(k, [t.reshape(-1)[:size] for t in outs[4 * n_:4 * n_ + 4]], shape)
    lng_sum = sums[offs[-3]:offs[-2]].reshape(-1)[:LA * c.AW].reshape(LA, c.AW)
    lng_mine = lax.dynamic_slice_in_dim(lng_sum, me * c.ar, c.ar, axis=1)
    lng_parts = jnp.concatenate([lng_mine[None], jnp.zeros((NDEV - 1, LA, c.ar), F32)], axis=0)
    put("a_ln_g", seq(_adamw, "adamw_ln_g", lng_parts, (NDEV, LA, c.ar), lambda i: (0, 0, 0),
                      a_ln_g, m_in["a_ln_g"], v_in["a_ln_g"], LA), a_ln_g.shape)
    for k in ("a_w_in", "ffn_w1", "ffn_w2", "a_w_out", "b_w_o", "b_w_q", "w_k", "w_v"):
        put(k, results[k], weights[k].shape)

    return (loss, grad_x, *[grads[k] for k in names], *[deltas[k] for k in names],
            *[new_m[k] for k in names], *[new_v[k] for k in names])
```

```python
import numpy as np
import math
import jax
import jax.numpy as jnp
from jax import lax
from jax.experimental import pallas as pl
from jax.experimental.pallas import tpu as pltpu

F32 = jnp.float32
BF16 = jnp.bfloat16

NDEV = 8
EPS = 1e-6
CHUNK = 128
A_GROUPS = 8
N_HEADS = 16
N_KV_HEADS = 4
Q_PER_KV = N_HEADS // N_KV_HEADS
HEAD_DIM = 64
BLOCK = 128
N_BUCKETS = 32
MAX_DISTANCE = 128
ADAM_LR, ADAM_B1, ADAM_B2, ADAM_EPS, ADAM_WD, ADAM_STEP = 0.001, 0.9, 0.999, 1e-08, 0.01, 10
LANES = 128
VMEM_LIMIT = 56 * 1024 * 1024
INV_SQRT2 = 0.7071067811865476
INV_SQRT_2PI = 0.3989422804014327
EXCHANGE_LAG = 4
SIBLING_COLLECTIVE_ID = 0

HBM = pl.BlockSpec(memory_space=pltpu.HBM)
SMEM = pl.BlockSpec(memory_space=pltpu.SMEM)
ANY = pl.BlockSpec(memory_space=pl.ANY)
SEM = pl.BlockSpec(memory_space=pltpu.SEMAPHORE)
MESH = pl.DeviceIdType.MESH


def _params(n_grid):
    return pltpu.CompilerParams(dimension_semantics=("arbitrary",) * n_grid, vmem_limit_bytes=VMEM_LIMIT)


def _const(block, index_map):
    return pl.BlockSpec(block, index_map, pipeline_mode=pl.Buffered(1))


def _pcall(body, *, ins, in_specs, dep=None, **kw):
    n_in = len(ins)
    if dep is None or any(dep is t for t in ins):
        return pl.pallas_call(body, in_specs=list(in_specs), **kw)(*ins)

    def with_dep(*refs):
        body(*refs[:n_in], *refs[n_in + 1:])

    return pl.pallas_call(with_dep, in_specs=[*in_specs, ANY], **kw)(*ins, dep)


class _Seq:
    def __init__(self):
        self.last = None

    def __call__(self, fn, *args, **kw):
        out = fn(*args, dep=self.last, **kw)
        self.last = out[0] if isinstance(out, (tuple, list)) else out
        return out


def _rstd(h):
    return lax.rsqrt(jnp.mean(h * h, axis=-1, keepdims=True) + EPS)


def _rms_bwd(dhn, h, g, dres):
    r = _rstd(h)
    xh = h * r
    dg = jnp.sum(dhn * xh, axis=0, keepdims=True)
    dxh = dhn * g
    dx = r * (dxh - xh * jnp.mean(dxh * xh, axis=-1, keepdims=True))
    return dres + dx, dg


def _gelu(z):
    return 0.5 * z * (1.0 + lax.erf(z * INV_SQRT2))


def _gelu_grad(z):
    return 0.5 * (1.0 + lax.erf(z * INV_SQRT2)) + z * (jnp.exp(-0.5 * z * z) * INV_SQRT_2PI)


def _dot(a, b, dims):
    return lax.dot_general(a, b, (dims, ((), ())), preferred_element_type=F32)


NN = ((1,), (0,))
NT = ((1,), (1,))
TN = ((0,), (0,))


def _mm(name, ins, in_specs, out_shapes, out_specs, *, grid, dims, nk, acc_shape, load_a, load_b, epilogue,
        dep=None):
    n_in, n_out = len(ins), len(out_shapes)
    kax = len(grid) - 1

    def body(*refs):
        in_refs = refs[:n_in]
        out_refs = refs[n_in:n_in + n_out]
        a = load_a(in_refs, out_refs)
        b = load_b(in_refs)
        prod = _dot(a, b, dims)
        if nk == 1:
            epilogue(prod, in_refs, out_refs)
        else:
            acc = refs[n_in + n_out]
            k = pl.program_id(kax)

            @pl.when(k == 0)
            def _():
                acc[...] = prod

            @pl.when(k > 0)
            def _():
                acc[...] += prod

            @pl.when(k == nk - 1)
            def _():
                epilogue(acc[...], in_refs, out_refs)

    return _pcall(
        body, name=name, ins=ins, in_specs=in_specs, dep=dep, grid=grid, out_specs=out_specs, out_shape=out_shapes,
        scratch_shapes=[pltpu.VMEM(acc_shape, F32)] if nk > 1 else [], compiler_params=_params(len(grid)))


def _bf(ref_idx):
    return lambda in_refs, *_: in_refs[ref_idx][...].astype(BF16)


def _b_view(ref_idx, rows):
    def load(in_refs):
        b = in_refs[ref_idx][...]
        return b.reshape(rows, b.shape[-1])
    return load


class Cfg:
    pass


def _config(x, a_w_in, a_w_out, w_k, b_w_q, b_w_o, ffn_w1, ffn_w2):
    c = Cfg()
    c.S, c.D = x.shape[1], x.shape[2]
    c.LA, _, c.cw = a_w_in.shape
    c.AW2 = NDEV * c.cw
    c.AW = c.AW2 // 2
    c.gd = c.AW // A_GROUPS
    c.ar = a_w_out.shape[1]
    c.LF, _, c.fw = ffn_w1.shape
    c.fr = ffn_w2.shape[1]
    c.LB, c.qr, c.DQ = b_w_q.shape
    c.orr = b_w_o.shape[1]
    c.kr, c.DKV = w_k.shape
    c.tm = min(1024, c.S)
    c.tmw = min(512, c.S)
    c.tms = min(512, c.S)
    c.nb = c.S // BLOCK
    assert c.cw == c.fw == c.fr and c.AW == NDEV * c.ar and c.D == NDEV * c.qr == NDEV * c.kr
    assert c.DQ == NDEV * c.orr == N_HEADS * HEAD_DIM and c.DKV == N_KV_HEADS * HEAD_DIM
    assert c.S % c.tm == 0 and c.S % c.tmw == 0 and c.S % c.tms == 0 and c.tms % CHUNK == 0 and c.gd % LANES == 0
    assert c.LA >= 1 and c.LB >= 1 and c.LF == c.LA + c.LB
    return c


def _cached_rms(h_idx, g_idx, hn_out_idx, jax_axis=1):
    def load(in_refs, out_refs):
        hn_ref = out_refs[hn_out_idx]

        @pl.when(pl.program_id(jax_axis) == 0)
        def _():
            h = in_refs[h_idx][...]
            hn_ref[...] = (h * _rstd(h) * in_refs[g_idx][...]).astype(BF16)

        return hn_ref[...]
    return load


def _a_in_fwd(c, name, h, g, col, ci, dep=None):
    S, D, cw, tm = c.S, c.D, c.cw, c.tmw

    def body(h_ref, g_ref, w_ref, z_ref, a_ref, hn_ref):
        h = h_ref[...]
        hn = (h * _rstd(h) * g_ref[...]).astype(BF16)
        hn_ref[...] = hn
        for j in range(NDEV):
            cols = slice(j * cw, (j + 1) * cw)
            z = _dot(hn, w_ref[j], NT)
            z_ref[:, cols] = z.astype(BF16)
            a_ref[:, cols] = _gelu(z).astype(BF16)

    row = pl.BlockSpec((tm, D), lambda i: (i, 0))
    wide = pl.BlockSpec((tm, c.AW2), lambda i: (i, 0))
    return _pcall(
        body, name=name, ins=[h, g, col], dep=dep, grid=(S // tm,),
        in_specs=[row, pl.BlockSpec((1, D), lambda i: (0, 0)), _const((NDEV, None, cw, D), lambda i: (0, ci, 0, 0))],
        out_specs=[wide, wide, row],
        out_shape=[jax.ShapeDtypeStruct((S, c.AW2), BF16), jax.ShapeDtypeStruct((S, c.AW2), BF16),
                   jax.ShapeDtypeStruct((S, D), BF16)],
        compiler_params=_params(1))


def _rms_mm_rows(c, name, h, g, slab, blk_rows, blk_idx, n_out, dep=None):
    S, D, tm = c.S, c.D, c.tm

    def epilogue(acc, in_refs, out_refs):
        out_refs[0][...] = acc.astype(BF16)

    return _mm(
        name, [h, slab, g],
        [pl.BlockSpec((tm, D), lambda i, j, k: (i, 0)),
         pl.BlockSpec((NDEV, blk_rows, n_out), lambda i, j, k: (0, blk_idx, 0)),
         pl.BlockSpec((1, D), lambda i, j, k: (0, 0))],
        [jax.ShapeDtypeStruct((S, n_out), BF16), jax.ShapeDtypeStruct((S, D), BF16)],
        [pl.BlockSpec((tm, n_out), lambda i, j, k: (i, 0)), pl.BlockSpec((tm, D), lambda i, j, k: (i, 0))],
        grid=(S // tm, 1, 1), dims=NN, nk=1, acc_shape=None,
        load_a=_cached_rms(0, 2, 1), load_b=_b_view(1, NDEV * blk_rows), epilogue=epilogue, dep=dep)


def _rms2_mm_rows(c, name, h, projections, dep=None):
    S, D, tm = c.S, c.D, c.tm
    n_p = len(projections)

    def body(h_ref, *refs):
        ins, outs = refs[:2 * n_p], refs[2 * n_p:]
        h = h_ref[...]
        xh = h * _rstd(h)
        for k in range(n_p):
            hn = (xh * ins[2 * k][...]).astype(BF16)
            outs[2 * k + 1][...] = hn
            w = ins[2 * k + 1][...]
            outs[2 * k][...] = _dot(hn, w.reshape(D, w.shape[-1]), NN).astype(BF16)

    row = pl.BlockSpec((tm, D), lambda i: (i, 0))
    ins, in_specs, shapes, specs = [h], [row], [], []
    for g, slab in projections:
        n = slab.shape[-1]
        ins += [g, slab]
        in_specs += [pl.BlockSpec((1, D), lambda i: (0, 0)), pl.BlockSpec(slab.shape, lambda i: (0, 0, 0))]
        shapes += [jax.ShapeDtypeStruct((S, n), BF16), jax.ShapeDtypeStruct((S, D), BF16)]
        specs += [pl.BlockSpec((tm, n), lambda i: (i, 0)), row]
    return _pcall(body, name=name, ins=ins, in_specs=in_specs, dep=dep, grid=(S // tm,), out_specs=specs,
                  out_shape=shapes, compiler_params=_params(1))


def _mm_res(c, name, a, slab, blk_rows, blk_idx, res, dep=None):
    S, D, tm = c.S, c.D, c.tm
    K = NDEV * blk_rows

    def epilogue(acc, in_refs, out_refs):
        out_refs[0][...] = in_refs[2][...] + acc

    return _mm(
        name, [a, slab, res],
        [pl.BlockSpec((tm, K), lambda i, j, k: (i, 0)),
         pl.BlockSpec((NDEV, blk_rows, D), lambda i, j, k: (0, blk_idx, 0)),
         pl.BlockSpec((tm, D), lambda i, j, k: (i, 0))],
        [jax.ShapeDtypeStruct((S, D), F32)], [pl.BlockSpec((tm, D), lambda i, j, k: (i, 0))],
        grid=(S // tm, 1, 1), dims=NN, nk=1, acc_shape=None,
        load_a=_bf(0), load_b=_b_view(1, K), epilogue=epilogue, dep=dep)[0]


def _sgu_masks():
    ii = lax.broadcasted_iota(jnp.int32, (CHUNK, CHUNK), 0)
    jj = lax.broadcasted_iota(jnp.int32, (CHUNK, CHUNK), 1)
    return ii >= jj


def _sgu_fwd(c, name, a, ln_g, wc, b_t, dep=None):
    S, AW, gd, tm = c.S, c.AW, c.gd, c.tms

    def body(a_ref, lng_ref, wc_ref, bt_ref, out_ref):
        va = a_ref[:, AW:].astype(F32)
        xc = va - jnp.mean(va, axis=-1, keepdims=True)
        vn = (xc * lax.rsqrt(jnp.mean(xc * xc, axis=-1, keepdims=True) + EPS) * lng_ref[...]).astype(BF16)
        for ch in range(tm // CHUNK):
            rows = slice(ch * CHUNK, (ch + 1) * CHUNK)
            for g in range(A_GROUPS):
                cols = slice(g * gd, (g + 1) * gd)
                mixed = _dot(wc_ref[g], vn[rows, cols], NN) + bt_ref[:, g:g + 1]
                out_ref[rows, cols] = (a_ref[rows, cols].astype(F32) * mixed).astype(BF16)

    return _pcall(
        body, name=name, ins=[a, ln_g, wc, b_t], dep=dep, grid=(S // tm,),
        in_specs=[pl.BlockSpec((tm, 2 * AW), lambda i: (i, 0)), pl.BlockSpec((1, AW), lambda i: (0, 0)),
                  pl.BlockSpec((A_GROUPS, CHUNK, CHUNK), lambda i: (0, 0, 0)),
                  pl.BlockSpec((CHUNK, A_GROUPS), lambda i: (0, 0))],
        out_specs=pl.BlockSpec((tm, AW), lambda i: (i, 0)),
        out_shape=jax.ShapeDtypeStruct((S, AW), BF16), compiler_params=_params(1))


def _ffn_fwd(c, name, h, g, col, ci, rows, dep=None):
    S, D, fw, tm = c.S, c.D, c.fw, c.tmw
    F = NDEV * fw

    def body(h_ref, g_ref, w1_ref, w2_ref, p_ref, out_ref, hn_ref, r_ref):
        h = h_ref[...]
        hn = (h * _rstd(h) * g_ref[...]).astype(BF16)
        hn_ref[...] = hn
        for j in range(NDEV):
            cols = slice(j * fw, (j + 1) * fw)
            p = jnp.maximum(_dot(hn, w1_ref[j], NT), 0.0)
            p_ref[:, cols] = p.astype(BF16)
            r_ref[:, cols] = (p * p).astype(BF16)
        out_ref[...] = h + _dot(r_ref[...], w2_ref[...].reshape(F, D), NN)

    row = pl.BlockSpec((tm, D), lambda i: (i, 0))
    return _pcall(
        body, name=name, ins=[h, g, col, rows], dep=dep, grid=(S // tm,),
        in_specs=[row, pl.BlockSpec((1, D), lambda i: (0, 0)),
                  _const((NDEV, None, fw, D), lambda i: (0, ci, 0, 0)), _const((NDEV, c.fr, D), lambda i: (0, 0, 0))],
        out_specs=[pl.BlockSpec((tm, F), lambda i: (i, 0)), row, row],
        out_shape=[jax.ShapeDtypeStruct((S, F), BF16), jax.ShapeDtypeStruct((S, D), F32),
                   jax.ShapeDtypeStruct((S, D), BF16)],
        scratch_shapes=[pltpu.VMEM((tm, F), BF16)], compiler_params=_params(1))


def _bucket_table():
    qi = np.arange(BLOCK)[:, None]
    kj = np.arange(2 * BLOCK)[None, :]
    d = np.maximum(qi + BLOCK - kj, 0)
    max_exact = N_BUCKETS // 2
    ratio = np.log(np.maximum(d, 1).astype(np.float32) / np.float32(max_exact)) / np.float32(
        math.log(MAX_DISTANCE / max_exact))
    large = np.minimum(max_exact + (ratio.astype(np.float32) * np.float32(N_BUCKETS - max_exact)).astype(np.int32),
                       N_BUCKETS - 1)
    return np.where(d < max_exact, d, large).astype(np.int32)


def _bucket_onehot():
    b = jnp.asarray(_bucket_table().reshape(1, -1))
    return (b == lax.broadcasted_iota(jnp.int32, (N_BUCKETS, b.shape[1]), 0)).astype(F32)


def _whole(t):
    return pl.BlockSpec(t.shape, lambda: (0,) * t.ndim)


def _band_bias(rel_bias_t, onehot, dep=None):
    def body(r_ref, oh_ref, out_ref):
        out_ref[...] = lax.dot_general(r_ref[...], oh_ref[...], (NN, ((), ())), preferred_element_type=F32,
                                       precision=lax.Precision.HIGHEST)

    n = onehot.shape[1]
    return _pcall(body, name="band_bias", ins=[rel_bias_t, onehot], in_specs=[_whole(rel_bias_t), _whole(onehot)],
                  dep=dep, out_shape=jax.ShapeDtypeStruct((N_HEADS, n), F32), compiler_params=_params(0))


def _band_bias_grad(dbias_list, onehot, dep=None):
    n_in = len(dbias_list)

    def body(*refs):
        oh_ref, out_ref = refs[n_in], refs[n_in + 1]
        d = refs[0][...]
        for r in refs[1:n_in]:
            d = d + r[...]
        out_ref[...] = lax.dot_general(d, oh_ref[...], (NT, ((), ())), preferred_element_type=F32,
                                       precision=lax.Precision.HIGHEST)

    ins = [*dbias_list, onehot]
    return _pcall(body, name="band_bias_grad", ins=ins, in_specs=[_whole(t) for t in ins], dep=dep,
                  out_shape=jax.ShapeDtypeStruct((N_HEADS, N_BUCKETS), F32), compiler_params=_params(0))


KV_PAIRS = N_KV_HEADS // 2
PAIR_ROWS = 2 * Q_PER_KV * BLOCK
MASKED = float(np.finfo(np.float32).min) / 2


def _slot_cols(w):
    lead = w.shape[:-1]
    return w.reshape(*lead, KV_PAIRS, 2, Q_PER_KV, HEAD_DIM).swapaxes(-3, -2).reshape(*lead, N_HEADS * HEAD_DIM)


def _slot_rows(blocks):
    n = blocks.shape[-1]
    return blocks.reshape(KV_PAIRS, 2, Q_PER_KV, HEAD_DIM, n).swapaxes(1, 2).reshape(blocks.shape)


def _slot_bias(bias):
    qi = np.arange(BLOCK)[:, None]
    kj = np.arange(2 * BLOCK)[None, :]
    dist = qi + BLOCK - kj
    window = (dist >= 0) & (dist < BLOCK)
    b = bias.reshape(KV_PAIRS, 2, Q_PER_KV, BLOCK, 2 * BLOCK).swapaxes(1, 2).reshape(KV_PAIRS, PAIR_ROWS, 2 * BLOCK)
    tile = lambda mk: jnp.asarray(np.tile(mk, (2 * Q_PER_KV, 1)))[None]
    return jnp.stack([jnp.where(tile(window & (kj >= BLOCK)), b, MASKED), jnp.where(tile(window), b, MASKED)])


def _pair_kv(kvc_ref, kvp_ref, kvp, dkv):
    lanes = slice(kvp * LANES, (kvp + 1) * LANES)
    vlanes = slice(dkv + kvp * LANES, dkv + (kvp + 1) * LANES)
    k2 = jnp.concatenate([kvp_ref[:, lanes], kvc_ref[:, lanes]], axis=0)
    v2 = jnp.concatenate([kvp_ref[:, vlanes], kvc_ref[:, vlanes]], axis=0)
    return k2, v2


def _head_operand(ref, grp, par, low, scale=None):
    xg = ref[:, grp * LANES:(grp + 1) * LANES]
    if scale is not None:
        xg = xg * scale
    zero = jnp.zeros_like(xg)
    return jnp.where(low, xg, zero) if par == 0 else jnp.where(low, zero, xg)


def _head_probs(qh, k2, bias_rows, sink):
    s = _dot(qh, k2, NT) + bias_rows
    m = jnp.maximum(jnp.max(s, axis=-1, keepdims=True), sink)
    e = jnp.exp(s - m)
    es = jnp.exp(sink - m)
    inv = 1.0 / (jnp.sum(e, axis=-1, keepdims=True) + es)
    return e * inv, es * inv


def _attn_specs(c):
    dq, dkv2 = c.DQ, 2 * c.DKV
    return [pl.BlockSpec((BLOCK, dq), lambda n: (n, 0)),
            pl.BlockSpec((BLOCK, dkv2), lambda n: (n, 0)),
            pl.BlockSpec((BLOCK, dkv2), lambda n: (jnp.maximum(n - 1, 0), 0))]


def _bias_spec():
    return pl.BlockSpec((None, KV_PAIRS, PAIR_ROWS, 2 * BLOCK), lambda n: (jnp.minimum(n, 1), 0, 0, 0))


def _low_lanes():
    return lax.broadcasted_iota(jnp.int32, (BLOCK, LANES), 1) < HEAD_DIM


def _first_key():
    return lax.broadcasted_iota(jnp.int32, (BLOCK, 2 * BLOCK), 1) == 0


def _probs_spec():
    return pl.BlockSpec((None, KV_PAIRS, PAIR_ROWS, 2 * BLOCK), lambda n: (n, 0, 0, 0))


def _attn_fwd(c, name, q, kv, bias, sinks, dep=None):
    S, dq = c.S, c.DQ

    def body(q_ref, kvc_ref, kvp_ref, bias_ref, sink_ref, o_ref, probs_ref):
        low = _low_lanes()
        first = _first_key()
        for kvp in range(KV_PAIRS):
            k2, v2 = _pair_kv(kvc_ref, kvp_ref, kvp, c.DKV)
            for g in range(Q_PER_KV):
                grp = kvp * Q_PER_KV + g
                halves = []
                for par in range(2):
                    rows = slice((2 * g + par) * BLOCK, (2 * g + par + 1) * BLOCK)
                    qh = _head_operand(q_ref, grp, par, low, scale=HEAD_DIM ** -0.5)
                    p, ps = _head_probs(qh, k2, bias_ref[kvp, rows, :], sink_ref[(2 * kvp + par) * Q_PER_KV + g])
                    probs_ref[kvp, rows, :] = jnp.where(first, ps, p).astype(BF16)
                    halves.append(_dot(p.astype(BF16), v2, NN))
                o_ref[:, grp * LANES:(grp + 1) * LANES] = jnp.where(low, halves[0], halves[1]).astype(BF16)

    return _pcall(
        body, name=name, ins=[q, kv, kv, bias, sinks], dep=dep, grid=(c.nb,),
        in_specs=_attn_specs(c) + [_bias_spec(), SMEM],
        out_specs=[pl.BlockSpec((BLOCK, dq), lambda n: (n, 0)), _probs_spec()],
        out_shape=[jax.ShapeDtypeStruct((S, dq), BF16),
                   jax.ShapeDtypeStruct((c.nb, KV_PAIRS, PAIR_ROWS, 2 * BLOCK), BF16)],
        compiler_params=_params(1))


def _final_loss(c, h, g, target, dep=None):
    S, D, tm = c.S, c.D, c.tm

    def body(h_ref, g_ref, t_ref, dh_ref, dg_ref, loss_ref):
        i = pl.program_id(0)
        h = h_ref[...]
        gg = g_ref[...]
        r = _rstd(h)
        xh = h * r
        err = xh * gg - t_ref[...]
        lp = jnp.sum(jnp.sum(err * err, axis=1, keepdims=True), axis=0, keepdims=True) * (0.5 / D)
        dx, dg = _rms_bwd(err * (1.0 / D), h, gg, 0.0)
        dh_ref[...] = dx

        @pl.when(i == 0)
        def _():
            dg_ref[...] = dg
            loss_ref[...] = jnp.broadcast_to(lp, loss_ref.shape)

        @pl.when(i > 0)
        def _():
            dg_ref[...] += dg
            loss_ref[...] += jnp.broadcast_to(lp, loss_ref.shape)

    row = pl.BlockSpec((tm, D), lambda i: (i, 0))
    return _pcall(
        body, name="final_loss", ins=[h, g, target], dep=dep, grid=(S // tm,),
        in_specs=[row, pl.BlockSpec((1, D), lambda i: (0, 0)), row],
        out_specs=[row, pl.BlockSpec((1, D), lambda i: (0, 0)), pl.BlockSpec((1, LANES), lambda i: (0, 0))],
        out_shape=[jax.ShapeDtypeStruct((S, D), F32), jax.ShapeDtypeStruct((1, D), F32),
                   jax.ShapeDtypeStruct((1, LANES), F32)],
        compiler_params=_params(1))


def _rms_bwd_epilogue(h_idx, g_idx, res_idx):
    def epilogue(dhn, in_refs, out_refs):
        dh, dg = _rms_bwd(dhn, in_refs[h_idx][...], in_refs[g_idx][...], in_refs[res_idx][...])
        out_refs[0][...] = dh
        i = pl.program_id(0)

        @pl.when(i == 0)
        def _():
            out_refs[1][...] = dg

        @pl.when(i > 0)
        def _():
            out_refs[1][...] += dg
    return epilogue


def _stream_outs(c, tm):
    S, D = c.S, c.D
    return ([jax.ShapeDtypeStruct((S, D), F32), jax.ShapeDtypeStruct((1, D), F32)],
            [pl.BlockSpec((tm, D), lambda i, j, k: (i, 0)), pl.BlockSpec((1, D), lambda i, j, k: (0, 0))])


def _row_specs(c, tm):
    D = c.D
    return [pl.BlockSpec((tm, D), lambda i, j, k: (i, 0)), pl.BlockSpec((1, D), lambda i, j, k: (0, 0)),
            pl.BlockSpec((tm, D), lambda i, j, k: (i, 0))]


def _bwd_rows_to_stream(c, name, dy_list, slab, blk_rows, blk_idx, n_in_cols, h, g, dres, dep=None):
    S, D, tm = c.S, c.D, c.tm
    nd = len(dy_list)

    def load_a(in_refs, out_refs):
        a = in_refs[0][...]
        for r in in_refs[1:nd]:
            a = a + r[...]
        return a.astype(BF16)

    shapes, specs = _stream_outs(c, tm)
    return _mm(
        name, [*dy_list, slab, h, g, dres],
        [pl.BlockSpec((tm, n_in_cols), lambda i, j, k: (i, 0))] * nd
        + [pl.BlockSpec((NDEV, blk_rows, n_in_cols), lambda i, j, k: (0, blk_idx, 0))] + _row_specs(c, tm),
        shapes, specs, grid=(S // tm, 1, 1), dims=NT, nk=1, acc_shape=None,
        load_a=load_a, load_b=_b_view(nd, NDEV * blk_rows), epilogue=_rms_bwd_epilogue(nd + 1, nd + 2, nd + 3),
        dep=dep)


def _bwd_q_kv_to_stream(c, name, dq, wq, dkv_list, wkv, h, g_q, g_kv, dres, dep=None):
    S, D, tm = c.S, c.D, c.tm
    nkv = len(dkv_list)

    def body(dq_ref, wq_ref, *rest):
        dkv_refs = rest[:nkv]
        wkv_ref, h_ref, gq_ref, gkv_ref, dres_ref, out_ref, dgq_ref, dgkv_ref = rest[nkv:]
        i = pl.program_id(0)
        dhn_q = _dot(dq_ref[...], wq_ref[...].reshape(D, c.DQ), NT)
        dkv = dkv_refs[0][...]
        for r in dkv_refs[1:]:
            dkv = dkv + r[...]
        dhn_kv = _dot(dkv.astype(BF16), wkv_ref[...].reshape(D, 2 * c.DKV), NT)
        h = h_ref[...]
        r = _rstd(h)
        xh = h * r
        dxh = dhn_q * gq_ref[...] + dhn_kv * gkv_ref[...]
        out_ref[...] = dres_ref[...] + r * (dxh - xh * jnp.mean(dxh * xh, axis=-1, keepdims=True))
        dgq = jnp.sum(dhn_q * xh, axis=0, keepdims=True)
        dgkv = jnp.sum(dhn_kv * xh, axis=0, keepdims=True)

        @pl.when(i == 0)
        def _():
            dgq_ref[...] = dgq
            dgkv_ref[...] = dgkv

        @pl.when(i > 0)
        def _():
            dgq_ref[...] += dgq
            dgkv_ref[...] += dgkv

    row = pl.BlockSpec((tm, D), lambda i: (i, 0))
    gain = pl.BlockSpec((1, D), lambda i: (0, 0))
    return _pcall(
        body, name=name, ins=[dq, wq, *dkv_list, wkv, h, g_q, g_kv, dres], dep=dep, grid=(S // tm,),
        in_specs=[pl.BlockSpec((tm, c.DQ), lambda i: (i, 0)), pl.BlockSpec((NDEV, c.qr, c.DQ), lambda i: (0, 0, 0))]
        + [pl.BlockSpec((tm, 2 * c.DKV), lambda i: (i, 0))] * nkv
        + [pl.BlockSpec((NDEV, c.kr, 2 * c.DKV), lambda i: (0, 0, 0)), row, gain, gain, row],
        out_specs=[row, gain, gain],
        out_shape=[jax.ShapeDtypeStruct((S, D), F32), jax.ShapeDtypeStruct((1, D), F32),
                   jax.ShapeDtypeStruct((1, D), F32)],
        compiler_params=_params(1))


def _bwd_cols_to_stream(c, name, dy, col, ci, h, g, dres, dep=None):
    S, D, cw, tm = c.S, c.D, c.cw, c.tmw
    K = NDEV * cw
    shapes, specs = _stream_outs(c, tm)
    return _mm(
        name, [dy, col, h, g, dres],
        [pl.BlockSpec((tm, K), lambda i, j, k: (i, 0)),
         _const((NDEV, None, cw, D), lambda i, j, k: (0, ci, 0, 0))] + _row_specs(c, tm),
        shapes, specs, grid=(S // tm, 1, 1), dims=NN, nk=1, acc_shape=None,
        load_a=_bf(0), load_b=_b_view(1, K), epilogue=_rms_bwd_epilogue(2, 3, 4), dep=dep)


def _bwd_rows_data(c, name, dy, slab, blk_rows, blk_idx, dep=None):
    S, D, tm = c.S, c.D, c.tm
    K = NDEV * blk_rows

    def epilogue(acc, in_refs, out_refs):
        out_refs[0][...] = acc.astype(BF16)

    return _mm(
        name, [dy, slab],
        [pl.BlockSpec((tm, D), lambda i, j, k: (i, 0)),
         pl.BlockSpec((NDEV, blk_rows, D), lambda i, j, k: (0, blk_idx, 0))],
        [jax.ShapeDtypeStruct((S, K), BF16)], [pl.BlockSpec((tm, K), lambda i, j, k: (i, 0))],
        grid=(S // tm, 1, 1), dims=NT, nk=1, acc_shape=None,
        load_a=_bf(0), load_b=_b_view(1, K), epilogue=epilogue, dep=dep)[0]


def _owner_blocks(out, acc, blk_rows, n_b, unslot):
    if unslot is None:
        out[...] = acc.reshape(NDEV, blk_rows, n_b).astype(BF16)
        return
    per_blk = blk_rows // HEAD_DIM
    for h in range(N_HEADS):
        s = (h // (2 * Q_PER_KV)) * 2 * Q_PER_KV + (h % Q_PER_KV) * 2 + (h // Q_PER_KV) % 2
        if unslot == "rows":
            out[h // per_blk, (h % per_blk) * HEAD_DIM:(h % per_blk + 1) * HEAD_DIM, :] = (
                acc[s * HEAD_DIM:(s + 1) * HEAD_DIM, :].astype(BF16))
        else:
            out[:, :, h * HEAD_DIM:(h + 1) * HEAD_DIM] = (
                acc[:, s * HEAD_DIM:(s + 1) * HEAD_DIM].reshape(NDEV, blk_rows, HEAD_DIM).astype(BF16))


def _bwd_rows_data_w(c, name, dy, slab, x, unslot=None, dep=None):
    S, D, tm = c.S, c.D, c.tm
    blk_rows = slab.shape[1]
    K = NDEV * blk_rows
    nk = S // tm

    def body(dy_ref, w_ref, x_ref, dx_ref, dw_ref, acc):
        k = pl.program_id(0)
        dyb = dy_ref[...].astype(BF16)
        dx_ref[...] = _dot(dyb, w_ref[...].reshape(K, D), NT).astype(BF16)
        part = _dot(x_ref[...], dyb, TN)

        @pl.when(k == 0)
        def _():
            acc[...] = part

        @pl.when(k > 0)
        def _():
            acc[...] += part

        @pl.when(k == nk - 1)
        def _():
            _owner_blocks(dw_ref, acc[...], blk_rows, D, unslot)

    return _pcall(
        body, name=name, ins=[dy, slab, x], dep=dep, grid=(nk,),
        in_specs=[pl.BlockSpec((tm, D), lambda k: (k, 0)), pl.BlockSpec((NDEV, blk_rows, D), lambda k: (0, 0, 0)),
                  pl.BlockSpec((tm, K), lambda k: (k, 0))],
        out_specs=[pl.BlockSpec((tm, K), lambda k: (k, 0)), pl.BlockSpec((NDEV, blk_rows, D), lambda k: (0, 0, 0))],
        out_shape=[jax.ShapeDtypeStruct((S, K), BF16), jax.ShapeDtypeStruct((NDEV, blk_rows, D), BF16)],
        scratch_shapes=[pltpu.VMEM((K, D), F32)], compiler_params=_params(1))


def _wgrad_rows(c, name, a, b_list, n_a, n_b, unslot=None, dep=None):
    S, tm = c.S, c.tm
    nb_in = len(b_list)
    blk_rows = n_a // NDEV

    def load_b(in_refs):
        b = in_refs[1][...]
        for r in in_refs[2:1 + nb_in]:
            b = b + r[...]
        return b.astype(BF16)

    def epilogue(acc, in_refs, out_refs):
        _owner_blocks(out_refs[0], acc, blk_rows, n_b, unslot)

    return _mm(
        name, [a, *b_list],
        [pl.BlockSpec((tm, n_a), lambda i, j, k: (k, 0))] + [pl.BlockSpec((tm, n_b), lambda i, j, k: (k, 0))] * nb_in,
        [jax.ShapeDtypeStruct((NDEV, blk_rows, n_b), BF16)],
        [pl.BlockSpec((NDEV, blk_rows, n_b), lambda i, j, k: (0, 0, 0))],
        grid=(1, 1, S // tm), dims=TN, nk=S // tm, acc_shape=(n_a, n_b),
        load_a=_bf(0), load_b=load_b, epilogue=epilogue, dep=dep)[0]


def _wgrad_cols(c, name, a, b, dep=None):
    S, D, cw = c.S, c.D, c.cw

    def body(a_ref, b_ref, out_ref):
        out_ref[...] = _dot(a_ref[...], b_ref[...], TN).astype(BF16)

    return _pcall(
        body, name=name, ins=[a, b], dep=dep, grid=(NDEV,),
        in_specs=[_const((S, D), lambda j: (0, 0)), pl.BlockSpec((S, cw), lambda j: (0, j))],
        out_specs=pl.BlockSpec((None, D, cw), lambda j: (j, 0, 0)),
        out_shape=jax.ShapeDtypeStruct((NDEV, D, cw), BF16), compiler_params=_params(1))


def _ffn_bwd_data(c, name, dh, p, col, ci, rows, h, g, dep=None):
    S, D, fw, tm = c.S, c.D, c.fw, c.tmw
    F = NDEV * fw

    def body(dh_ref, p_ref, w1t_ref, w2_ref, h_ref, g_ref, da_ref, out_ref, dg_ref, dhb_ref):
        i = pl.program_id(0)
        dh = dh_ref[...]
        dhb = dh.astype(BF16)
        dhb_ref[...] = dhb
        for j in range(NDEV):
            cols = slice(j * fw, (j + 1) * fw)
            da_ref[:, cols] = (_dot(dhb, w2_ref[j], NT) * (2.0 * p_ref[:, cols].astype(F32))).astype(BF16)
        dx, dg = _rms_bwd(_dot(da_ref[...], w1t_ref[...].reshape(F, D), NN), h_ref[...], g_ref[...], dh)
        out_ref[...] = dx

        @pl.when(i == 0)
        def _():
            dg_ref[...] = dg

        @pl.when(i > 0)
        def _():
            dg_ref[...] += dg

    row = pl.BlockSpec((tm, D), lambda i: (i, 0))
    wide = pl.BlockSpec((tm, F), lambda i: (i, 0))
    return _pcall(
        body, name=name, ins=[dh, p, col, rows, h, g], dep=dep, grid=(S // tm,),
        in_specs=[row, wide, _const((NDEV, None, fw, D), lambda i: (0, ci, 0, 0)),
                  _const((NDEV, c.fr, D), lambda i: (0, 0, 0)),
                  row, pl.BlockSpec((1, D), lambda i: (0, 0))],
        out_specs=[wide, row, pl.BlockSpec((1, D), lambda i: (0, 0)), row],
        out_shape=[jax.ShapeDtypeStruct((S, F), BF16), jax.ShapeDtypeStruct((S, D), F32),
                   jax.ShapeDtypeStruct((1, D), F32), jax.ShapeDtypeStruct((S, D), BF16)],
        compiler_params=_params(1))


def _ffn_bwd_w(c, name, hn, da, p, dhb, dep=None):
    S, D, fw = c.S, c.D, c.fw

    def body(hn_ref, da_ref, p_ref, dhb_ref, dw1_ref, dw2_ref):
        dw1_ref[...] = _dot(hn_ref[...], da_ref[...], TN).astype(BF16)
        pf = p_ref[...].astype(F32)
        dw2_ref[...] = _dot((pf * pf).astype(BF16), dhb_ref[...], TN).astype(BF16)

    panel = pl.BlockSpec((S, fw), lambda j: (0, j))
    return _pcall(
        body, name=name, ins=[hn, da, p, dhb], dep=dep, grid=(NDEV,),
        in_specs=[_const((S, D), lambda j: (0, 0)), panel, panel, _const((S, D), lambda j: (0, 0))],
        out_specs=[pl.BlockSpec((None, D, fw), lambda j: (j, 0, 0)), pl.BlockSpec((None, c.fr, D), lambda j: (j, 0, 0))],
        out_shape=[jax.ShapeDtypeStruct((NDEV, D, fw), BF16), jax.ShapeDtypeStruct((NDEV, c.fr, D), BF16)],
        compiler_params=_params(1))


def _attn_bwd(c, name, q, kv, do, probs, dep=None):
    S, dq, dkv = c.S, c.DQ, c.DKV
    nb = c.nb
    scale = HEAD_DIM ** -0.5

    def body(q_ref, kvc_ref, kvp_ref, do_ref, probs_ref, dq_ref, dkv_ref, dbias_ref, dsink_ref, dsink_acc,
             ds_sc, p_sc, qm_sc, dom_sc):
        n = pl.program_id(0)

        @pl.when(n == 0)
        def _():
            dkv_ref[...] = jnp.zeros_like(dkv_ref)
            dbias_ref[...] = jnp.zeros_like(dbias_ref)
            dsink_acc[...] = jnp.zeros_like(dsink_acc)

        low = _low_lanes()
        first = _first_key()
        rows_c = pl.ds(pl.multiple_of(n * BLOCK, BLOCK), BLOCK)
        rows_p = pl.ds(pl.multiple_of(jnp.maximum(n - 1, 0) * BLOCK, BLOCK), BLOCK)
        for kvp in range(KV_PAIRS):
            k2, v2 = _pair_kv(kvc_ref, kvp_ref, kvp, dkv)
            for g in range(Q_PER_KV):
                grp = kvp * Q_PER_KV + g
                halves = []
                for par in range(2):
                    rows = slice((2 * g + par) * BLOCK, (2 * g + par + 1) * BLOCK)
                    qh = _head_operand(q_ref, grp, par, low, scale=scale)
                    doh = _head_operand(do_ref, grp, par, low)
                    saved = probs_ref[kvp, rows, :]
                    ps = saved[:, 0:1].astype(F32)
                    p16 = jnp.where(first, jnp.zeros_like(saved), saved)
                    p = p16.astype(F32)
                    dp = _dot(doh, v2, NT)
                    delta = jnp.sum(p * dp, axis=-1, keepdims=True)
                    ds = p * (dp - delta)
                    dbias_ref[kvp, rows, :] += ds
                    dsink_acc[rows, kvp:kvp + 1] += -(ps * delta)
                    ds16 = ds.astype(BF16)
                    halves.append(_dot(ds16, k2, NN) * scale)
                    ds_sc[rows, :] = ds16
                    p_sc[rows, :] = p16
                    qm_sc[rows, :] = qh
                    dom_sc[rows, :] = doh
                dq_ref[:, grp * LANES:(grp + 1) * LANES] = jnp.where(low, halves[0], halves[1]).astype(BF16)
            dk2 = _dot(ds_sc[...], qm_sc[...], TN)
            dv2 = _dot(p_sc[...], dom_sc[...], TN)
            lanes = slice(kvp * LANES, (kvp + 1) * LANES)
            vlanes = slice(dkv + kvp * LANES, dkv + (kvp + 1) * LANES)
            dkv_ref[rows_p, lanes] += dk2[:BLOCK]
            dkv_ref[rows_c, lanes] += dk2[BLOCK:]
            dkv_ref[rows_p, vlanes] += dv2[:BLOCK]
            dkv_ref[rows_c, vlanes] += dv2[BLOCK:]

        @pl.when(n == nb - 1)
        def _():
            dsink_ref[...] = jnp.sum(dsink_acc[...].reshape(2 * Q_PER_KV, BLOCK, KV_PAIRS), axis=1)

    return _pcall(
        body, name=name, ins=[q, kv, kv, do, probs], dep=dep, grid=(nb,),
        in_specs=_attn_specs(c) + [pl.BlockSpec((BLOCK, dq), lambda n: (n, 0)), _probs_spec()],
        out_specs=[pl.BlockSpec((BLOCK, dq), lambda n: (n, 0)), pl.BlockSpec((S, 2 * dkv), lambda n: (0, 0)),
                   pl.BlockSpec((KV_PAIRS, PAIR_ROWS, 2 * BLOCK), lambda n: (0, 0, 0)),
                   pl.BlockSpec((2 * Q_PER_KV, KV_PAIRS), lambda n: (0, 0))],
        out_shape=[jax.ShapeDtypeStruct((S, dq), BF16), jax.ShapeDtypeStruct((S, 2 * dkv), F32),
                   jax.ShapeDtypeStruct((KV_PAIRS, PAIR_ROWS, 2 * BLOCK), F32),
                   jax.ShapeDtypeStruct((2 * Q_PER_KV, KV_PAIRS), F32)],
        scratch_shapes=[pltpu.VMEM((PAIR_ROWS, KV_PAIRS), F32), pltpu.VMEM((PAIR_ROWS, 2 * BLOCK), BF16),
                        pltpu.VMEM((PAIR_ROWS, 2 * BLOCK), BF16), pltpu.VMEM((PAIR_ROWS, LANES), BF16),
                        pltpu.VMEM((PAIR_ROWS, LANES), BF16)],
        compiler_params=_params(1))


def _sgu_bwd(c, name, a, z, dgated, ln_g, wc, wc_t, b_t, dep=None):
    S, AW, gd, tm = c.S, c.AW, c.gd, c.tms

    def body(a_ref, z_ref, dg_ref, lng_ref, wc_ref, wct_ref, bt_ref, dz_ref, dws_ref, dbt_ref, dlng_ref, dvn_ref):
        i = pl.program_id(0)

        @pl.when(i == 0)
        def _():
            dws_ref[...] = jnp.zeros_like(dws_ref)
            dbt_ref[...] = jnp.zeros_like(dbt_ref)
            dlng_ref[...] = jnp.zeros_like(dlng_ref)

        lng = lng_ref[...]
        va = a_ref[:, AW:].astype(F32)
        xc = va - jnp.mean(va, axis=-1, keepdims=True)
        rstd = lax.rsqrt(jnp.mean(xc * xc, axis=-1, keepdims=True) + EPS)
        xh = xc * rstd
        vn = (xh * lng).astype(BF16)
        causal = _sgu_masks()
        for ch in range(tm // CHUNK):
            rows = slice(ch * CHUNK, (ch + 1) * CHUNK)
            for g in range(A_GROUPS):
                cols = slice(g * gd, (g + 1) * gd)
                blk = vn[rows, cols]
                mixed = _dot(wc_ref[g], blk, NN) + bt_ref[:, g:g + 1]
                dgb = dg_ref[rows, cols].astype(F32)
                dm = dgb * a_ref[rows, cols].astype(F32)
                dbt_ref[:, g:g + 1] += jnp.sum(dm, axis=1, keepdims=True)
                dm16 = dm.astype(BF16)
                dws_ref[g] += jnp.where(causal, _dot(dm16, blk, NT), 0.0)
                dvn_ref[rows, cols] = _dot(wct_ref[g], dm16, NN)
                dz_ref[rows, cols] = (dgb * mixed * _gelu_grad(z_ref[rows, cols].astype(F32))).astype(BF16)
        dvn = dvn_ref[...]
        dlng_ref[...] += jnp.sum(dvn * xh, axis=0, keepdims=True)
        dxh = dvn * lng
        dva = rstd * (dxh - jnp.mean(dxh, axis=-1, keepdims=True) - xh * jnp.mean(dxh * xh, axis=-1, keepdims=True))
        dz_ref[:, AW:] = (dva * _gelu_grad(z_ref[:, AW:].astype(F32))).astype(BF16)

    wide = pl.BlockSpec((tm, 2 * AW), lambda i: (i, 0))
    wsp = pl.BlockSpec((A_GROUPS, CHUNK, CHUNK), lambda i: (0, 0, 0))
    btsp = pl.BlockSpec((CHUNK, A_GROUPS), lambda i: (0, 0))
    return _pcall(
        body, name=name, ins=[a, z, dgated, ln_g, wc, wc_t, b_t], dep=dep, grid=(S // tm,),
        in_specs=[wide, wide, pl.BlockSpec((tm, AW), lambda i: (i, 0)), pl.BlockSpec((1, AW), lambda i: (0, 0)),
                  wsp, wsp, btsp],
        out_specs=[wide, wsp, btsp, pl.BlockSpec((1, AW), lambda i: (0, 0))],
        out_shape=[jax.ShapeDtypeStruct((S, 2 * AW), BF16), jax.ShapeDtypeStruct((A_GROUPS, CHUNK, CHUNK), F32),
                   jax.ShapeDtypeStruct((CHUNK, A_GROUPS), F32), jax.ShapeDtypeStruct((1, AW), F32)],
        scratch_shapes=[pltpu.VMEM((tm, AW), F32)], compiler_params=_params(1))


def _adam_update(g, w_ref, m_ref, v_ref, out_refs):
    g_ref, d_ref, nm_ref, nv_ref = out_refs
    bc1 = 1.0 - ADAM_B1 ** ADAM_STEP
    bc2 = 1.0 - ADAM_B2 ** ADAM_STEP
    nm = ADAM_B1 * m_ref[...] + (1.0 - ADAM_B1) * g
    nv = ADAM_B2 * v_ref[...] + (1.0 - ADAM_B2) * (g * g)
    g_ref[...] = g
    nm_ref[...] = nm
    nv_ref[...] = nv
    d_ref[...] = -ADAM_LR * ((nm * (1.0 / bc1)) / (jnp.sqrt(nv * (1.0 / bc2)) + ADAM_EPS) + ADAM_WD * w_ref[...])


def _adamw_packed(name, sums, offs, ws, ms, vs, dep=None):
    n = len(ws)

    def body(*refs):
        s_ref, w_refs, m_refs, v_refs = refs[0], refs[1:1 + n], refs[1 + n:1 + 2 * n], refs[1 + 2 * n:1 + 3 * n]
        outs = refs[1 + 3 * n:]
        for k in range(n):
            _adam_update(s_ref[offs[k]:offs[k + 1], :], w_refs[k], m_refs[k], v_refs[k], outs[4 * k:4 * k + 4])

    ins = [sums, *ws, *ms, *vs]
    shapes = [jax.ShapeDtypeStruct(w.shape, F32) for w in ws for _ in range(4)]
    return _pcall(body, name=name, ins=ins, in_specs=[_whole(t) for t in ins], dep=dep,
                  out_shape=shapes, out_specs=[_whole(t) for t in shapes], compiler_params=_params(0))


def _adamw(name, parts, part_block, part_index, w, m, v, tr, row_off=0, n_rows=None, prev=None, dep=None):
    R, C = w.shape
    n_rows = R if n_rows is None else n_rows
    assert n_rows % tr == 0 and row_off % tr == 0

    def body(p_ref, w_ref, m_ref, v_ref, *rest):
        g = p_ref[0].astype(F32)
        for s in range(1, part_block[0]):
            g = g + p_ref[s].astype(F32)
        _adam_update(g, w_ref, m_ref, v_ref, rest[-4:])

    ob = row_off // tr
    row = pl.BlockSpec((tr, C), lambda i: (ob + i, 0))
    out = jax.ShapeDtypeStruct((R, C), F32)
    chained = prev is not None
    return _pcall(
        body, name=name, ins=[parts, w, m, v] + (list(prev) if chained else []), dep=dep, grid=(n_rows // tr,),
        in_specs=[pl.BlockSpec(part_block, part_index), row, row, row] + ([ANY] * 4 if chained else []),
        out_specs=[row, row, row, row], out_shape=[out, out, out, out],
        input_output_aliases={4 + t: t for t in range(4)} if chained else {}, compiler_params=_params(1))


def _sum_parts(name, parts, dep=None):
    def body(p_ref, out_ref):
        g = p_ref[0]
        for s in range(1, parts.shape[0]):
            g = g + p_ref[s]
        out_ref[...] = g

    return _pcall(body, name=name, ins=[parts], in_specs=[_whole(parts)], dep=dep,
                  out_shape=jax.ShapeDtypeStruct(parts.shape[1:], F32), compiler_params=_params(0))


def _place():
    return lax.axis_index("x"), lax.axis_index("y"), lax.axis_index("c")


def _slot(px, py, pc):
    return 4 * px + 2 * py + pc


def _peer(k, x, y, c):
    return x ^ ((k >> 2) & 1), y ^ ((k >> 1) & 1), c ^ (k & 1)


SEND_PEERS = {"exchange": tuple(range(1, NDEV)), "gather": (1, 2, 4, 6), "forward": (2, 4, 6),
              "broadcast": tuple(range(1, NDEV))}


def _n_sems(mode, n_lands):
    return n_lands * (len(SEND_PEERS[mode]) + (mode != "forward"))


def _send_copies(mode, src_refs, land_refs, send_sems, recv_sems):
    x, y, c = _place()
    me = _slot(x, y, c)
    peers = SEND_PEERS[mode]
    remote, local = [], []
    for i, k in enumerate(peers):
        peer = _peer(k, x, y, c)
        for a, land in enumerate(land_refs):
            if mode == "exchange":
                src, dst, to = src_refs[a].at[_slot(*peer)], land.at[me], peer
            elif mode in ("gather", "broadcast"):
                src, dst, to = src_refs[a], land.at[me], peer
            else:
                src = dst = land.at[_slot(*peer)]
                to = (x, y, 1 - c)
            s = a * len(peers) + i
            remote.append(pltpu.make_async_remote_copy(src_ref=src, dst_ref=dst, send_sem=send_sems.at[s],
                                                       recv_sem=recv_sems.at[s], device_id=to, device_id_type=MESH))
    if mode != "forward":
        for a, land in enumerate(land_refs):
            src = src_refs[a].at[me] if mode == "exchange" else src_refs[a]
            local.append(pltpu.make_async_copy(src, land.at[me], send_sems.at[len(land_refs) * len(peers) + a]))
    return remote, local


def _send_start_groups(name, groups, mode, collective_id=None, dep=None):
    sizes = [(len(s), len(l)) for s, l in groups]
    flat = [t for s, l in groups for t in (*s, *l)]
    n_in, ng = len(flat), len(groups)

    def body(*refs):
        sems, token, at = refs[n_in:n_in + 2 * ng], refs[-1], 0
        if collective_id is not None:
            x, y, c = _place()
            barrier = pltpu.get_barrier_semaphore()
            shake = (1,) if mode == "forward" else SEND_PEERS[mode]
            for k in shake:
                pl.semaphore_signal(barrier, inc=1, device_id=_peer(k, x, y, c), device_id_type=MESH)
            pl.semaphore_wait(barrier, len(shake))
        for gi, (ns, nl) in enumerate(sizes):
            remote, local = _send_copies(mode, refs[at:at + ns], refs[at + ns:at + ns + nl], sems[2 * gi],
                                         sems[2 * gi + 1])
            for cp in remote + local:
                cp.start()
            at += ns + nl
        token[...] = jnp.zeros_like(token)

    sem_shapes = [pltpu.SemaphoreType.DMA((_n_sems(mode, nl),)) for _, nl in sizes for _ in range(2)]
    if any(dep is t for t in flat):
        dep = None
    out = _pcall(
        body, name=name, ins=[pltpu.with_memory_space_constraint(t, pltpu.HBM) for t in flat],
        in_specs=[HBM] * n_in, dep=dep,
        out_shape=(*sem_shapes, *[pltpu.HBM(t.shape, t.dtype) for t in flat], jax.ShapeDtypeStruct((8, LANES), F32)),
        out_specs=(*[SEM] * (2 * ng), *[HBM] * n_in, pl.BlockSpec(memory_space=pltpu.VMEM)),
        input_output_aliases={i: 2 * ng + i for i in range(n_in)},
        compiler_params=pltpu.CompilerParams(has_side_effects=pltpu.SideEffectType.DATAFLOW_SIDE_EFFECTING,
                                             collective_id=collective_id))
    started, at = [], 2 * ng
    for gi, (ns, nl) in enumerate(sizes):
        started.append((out[-1], out[2 * gi], out[2 * gi + 1], list(out[at:at + ns]), list(out[at + ns:at + ns + nl])))
        at += ns + nl
    return started


def _send_start(name, srcs, lands, mode, collective_id=None, dep=None):
    if mode == "forward":
        collective_id = SIBLING_COLLECTIVE_ID
    return _send_start_groups(name, [(srcs, lands)], mode, collective_id=collective_id, dep=dep)[0]


def _send_wait(name, started, mode, dep=None):
    _, send_sems, recv_sems, srcs, lands = started
    n_src, n = len(srcs), len(lands)

    def body(*refs):
        src_refs, land_refs = refs[:n_src], refs[n_src:n_src + n]
        ssem, rsem = refs[n_src + n], refs[n_src + n + 1]
        remote, local = _send_copies(mode, src_refs, land_refs, ssem, rsem)
        for cp in remote:
            cp.wait_send()
            cp.wait_recv()
        for cp in local:
            cp.wait()

    thru = [pltpu.HBM(t.shape, t.dtype) for t in [*srcs, *lands]]
    out = _pcall(
        body, name=name, ins=[*srcs, *lands, send_sems, recv_sems], in_specs=[HBM] * (n_src + n) + [SEM, SEM], dep=dep,
        out_shape=tuple(thru), out_specs=tuple([HBM] * (n_src + n)),
        input_output_aliases={i: i for i in range(n_src + n)},
        compiler_params=pltpu.CompilerParams(has_side_effects=pltpu.SideEffectType.DATAFLOW_SIDE_EFFECTING))
    return list(out[n_src:])


def _landing(block):
    return lax.empty((NDEV, *block.shape), block.dtype)


def _rows128(t):
    flat = t.reshape(-1)
    n = flat.shape[0]
    rows = -(-n // (8 * LANES)) * 8
    return jnp.pad(flat, (0, rows * LANES - n)).reshape(rows, LANES)


def kernel(x, mix_norm_g, ffn_norm_g, a_w_in, a_ln_g, a_w_spatial, a_b_spatial, a_w_out, kv_norm_g, w_k, w_v, b_w_q, b_sinks, b_w_o, rel_bias, ffn_w1, ffn_w2, final_norm_g, loss_target, m_mix_norm_g, m_ffn_norm_g, m_a_w_in, m_a_ln_g, m_a_w_spatial, m_a_b_spatial, m_a_w_out, m_kv_norm_g, m_w_k, m_w_v, m_b_w_q, m_b_sinks, m_b_w_o, m_rel_bias, m_ffn_w1, m_ffn_w2, m_final_norm_g, v_mix_norm_g, v_ffn_norm_g, v_a_w_in, v_a_ln_g, v_a_w_spatial, v_a_b_spatial, v_a_w_out, v_kv_norm_g, v_w_k, v_w_v, v_b_w_q, v_b_sinks, v_b_w_o, v_rel_bias, v_ffn_w1, v_ffn_w2, v_final_norm_g):
    c = _config(x, a_w_in, a_w_out, w_k, b_w_q, b_w_o, ffn_w1, ffn_w2)
    S, D, LA, LB, LF = c.S, c.D, c.LA, c.LB, c.LF
    weights = dict(mix_norm_g=mix_norm_g, ffn_norm_g=ffn_norm_g, a_w_in=a_w_in, a_ln_g=a_ln_g, a_w_spatial=a_w_spatial,
                   a_b_spatial=a_b_spatial, a_w_out=a_w_out, kv_norm_g=kv_norm_g, w_k=w_k, w_v=w_v, b_w_q=b_w_q,
                   b_sinks=b_sinks, b_w_o=b_w_o, rel_bias=rel_bias, ffn_w1=ffn_w1, ffn_w2=ffn_w2,
                   final_norm_g=final_norm_g)
    m_in = dict(mix_norm_g=m_mix_norm_g, ffn_norm_g=m_ffn_norm_g, a_w_in=m_a_w_in, a_ln_g=m_a_ln_g,
                a_w_spatial=m_a_w_spatial, a_b_spatial=m_a_b_spatial, a_w_out=m_a_w_out, kv_norm_g=m_kv_norm_g,
                w_k=m_w_k, w_v=m_w_v, b_w_q=m_b_w_q, b_sinks=m_b_sinks, b_w_o=m_b_w_o, rel_bias=m_rel_bias,
                ffn_w1=m_ffn_w1, ffn_w2=m_ffn_w2, final_norm_g=m_final_norm_g)
    v_in = dict(mix_norm_g=v_mix_norm_g, ffn_norm_g=v_ffn_norm_g, a_w_in=v_a_w_in, a_ln_g=v_a_ln_g,
                a_w_spatial=v_a_w_spatial, a_b_spatial=v_a_b_spatial, a_w_out=v_a_w_out, kv_norm_g=v_kv_norm_g,
                w_k=v_w_k, w_v=v_w_v, b_w_q=v_b_w_q, b_sinks=v_b_sinks, b_w_o=v_b_w_o, rel_bias=v_rel_bias,
                ffn_w1=v_ffn_w1, ffn_w2=v_ffn_w2, final_norm_g=v_final_norm_g)
    names = list(weights)
    seq = _Seq()
    me = _slot(*_place())
    bf = lambda t: t.astype(BF16)

    tr = lambda t: bf(jnp.swapaxes(t, -1, -2))

    def start(tag, some):
        got = seq(_send_start_groups, f"weights_start_{tag}", [(grp, [_landing(t) for t in grp]) for grp in some],
                  "gather")
        seq.last = got[0][0]
        return got

    started = start("first", [[tr(a_w_in[0])[None], a_ln_g]])
    _, a_w_in_, a_w_out_, ffn_w1_, ffn_w2_, w_k_, w_v_, b_w_q_, b_w_o_ = lax.optimization_barrier(
        (started[0][0], a_w_in, a_w_out, ffn_w1, ffn_w2, w_k, w_v, b_w_q, b_w_o))
    groups = []
    for l in range(LA):
        groups += [[tr(a_w_in_[l])[None]], [bf(a_w_out_[l])], [tr(ffn_w1_[l])[None], bf(ffn_w2_[l])]]
    gb = 3 * LA
    for l in range(LB):
        extra = [bf(jnp.concatenate([w_k_, w_v_], axis=1))] if l == 0 else []
        groups += [extra + [bf(_slot_cols(b_w_q_[l])), bf(b_w_o_[l])], [tr(ffn_w1_[LA + l])[None], bf(ffn_w2_[LA + l])]]
    started += start("rest", groups[1:])
    forwarding = {}

    def forward(i):
        lands = seq(_send_wait, f"weights_wait{i}", started[i], "gather")
        forwarding[i] = seq(_send_start, f"weights_forward{i}", [], lands, "forward")

    def arrive(i):
        if i not in forwarding:
            forward(i)
        return seq(_send_wait, f"weights_arrive{i}", forwarding[i], "forward")

    causal = jnp.tril(jnp.ones((CHUNK, CHUNK), bool))
    wsp = jnp.where(causal[None, None], a_w_spatial, 0.0)
    wsp16 = wsp.astype(BF16)
    wsp16_t = jnp.swapaxes(wsp, -1, -2).astype(BF16)
    bsp_t = jnp.swapaxes(a_b_spatial, -1, -2)
    mix_g = mix_norm_g.reshape(-1, 1, D)
    ffn_g = ffn_norm_g.reshape(-1, 1, D)
    kv_g = kv_norm_g.reshape(1, D)
    fin_g = final_norm_g.reshape(1, D)
    onehot = _bucket_onehot()
    bias = _slot_bias(seq(_band_bias, rel_bias.T, onehot).reshape(N_HEADS, BLOCK, 2 * BLOCK))

    h = x.reshape(S, D)
    sav_a, sav_b, wts_a, wts_b = [], [], [], []
    for l in range(LA):
        got = arrive(3 * l)
        w_in = got[0]
        if l == 0:
            ln_g_full = jnp.transpose(got[1], (1, 0, 2)).reshape(LA, 1, c.AW)
        z, a, hn = seq(_a_in_fwd, c, f"a_in_fwd{l}", h, mix_g[l], w_in, 0)
        forward(3 * l + 1)
        gated = seq(_sgu_fwd, c, f"sgu_fwd{l}", a, ln_g_full[l], wsp16[l], bsp_t[l])
        (wout,) = arrive(3 * l + 1)
        if l > 0:
            forward(3 * l + 2)
        h1 = seq(_mm_res, c, f"a_out_fwd{l}", gated, wout, c.ar, 0, h)
        w1, rows = arrive(3 * l + 2)
        if l == LA - 1:
            forward(gb)
        p, h2, hnf = seq(_ffn_fwd, c, f"ffn_fwd{l}", h1, ffn_g[l], w1, 0, rows)
        sav_a.append((h, z, a, hn, gated, h1, p, hnf))
        wts_a.append((w_in, 0, w1, 0, rows, wout, 0))
        h = h2
    h_kv = h
    for l in range(LB):
        got = arrive(gb + 2 * l)
        if l == 0:
            wkv, got = got[0], got[1:]
        wq, wo = got[0], _slot_rows(got[1])
        if l == 0:
            kv, hkv, q, hn = seq(_rms2_mm_rows, c, "kv_q_fwd", h, [(kv_g, wkv), (mix_g[LA], wq)])
        else:
            q, hn = seq(_rms_mm_rows, c, f"q_fwd{l}", h, mix_g[LA + l], wq, c.qr, 0, c.DQ)
        forward(gb + 2 * l + 1)
        o, probs = seq(_attn_fwd, c, f"attn_fwd{l}", q, kv, bias, b_sinks[l])
        h1 = seq(_mm_res, c, f"o_fwd{l}", o, wo, c.orr, 0, h)
        w1, rows = arrive(gb + 2 * l + 1)
        if l + 1 < LB:
            forward(gb + 2 * l + 2)
        p, h2, hnf = seq(_ffn_fwd, c, f"ffn_fwd{LA + l}", h1, ffn_g[LA + l], w1, 0, rows)
        sav_b.append((h, q, hn, o, probs, h1, p, hnf))
        wts_b.append((wq, wo, w1, rows))
        h = h2
    dh, d_fin_g, loss_row = seq(_final_loss, c, h, fin_g, loss_target.reshape(S, D))

    results = {}
    in_flight, exchanges = [], []

    def update(k, parts, layer, col_blk=0):
        w = weights[k]
        rows_l, ncols = (w.shape[-2], w.shape[-1]) if w.ndim == 3 else w.shape
        flat = lambda t: t.reshape(-1, ncols)
        tr = min(256, rows_l)
        results[k] = seq(_adamw, f"adamw_{k}{layer}", parts, (NDEV, tr, ncols), lambda i: (0, i, col_blk),
                         flat(w), flat(m_in[k]), flat(v_in[k]), tr, row_off=layer * rows_l, n_rows=rows_l,
                         prev=results.get(k))

    def land(tag, entry):
        lands = seq(_send_wait, f"grads_wait_{tag}", entry[1], "exchange")
        for keys, parts in zip(entry[0], lands):
            for k, layer, col_blk in keys:
                update(k, parts, layer, col_blk)

    def send(tag, items):
        slabs = [t for _, t in items]
        own = [lax.empty(t.shape, t.dtype) for t in slabs]
        exchanges.append(tag)
        st = seq(_send_start, f"grads_start_{tag}", slabs, own, "exchange",
                 collective_id=SIBLING_COLLECTIVE_ID + len(exchanges))
        in_flight.append((tag, ([k for k, _ in items], st)))
        while len(in_flight) > EXCHANGE_LAG:
            land(*in_flight.pop(0))

    d_mix_g, d_ffn_g = [None] * LF, [None] * LF
    dkv_list, dbias_list, dsink_list = [], [], [None] * LB

    def ffn_bwd(lf, dh, h1, p, hnf, w1, w1_i, rows):
        da, dh1, d_ffn_g[lf], dhb = seq(_ffn_bwd_data, c, f"ffn_bwd_data{lf}", dh, p, w1, w1_i, rows, h1, ffn_g[lf])
        dw1, dw2 = seq(_ffn_bwd_w, c, f"ffn_bwd_w{lf}", hnf, da, p, dhb)
        send(f"ffn{lf}", [([("ffn_w1", lf, 0)], dw1), ([("ffn_w2", lf, 0)], dw2)])
        return dh1

    for l in reversed(range(LB)):
        h0, q, hn, o, probs, h1, p, hnf = sav_b[l]
        wq, wo, w1, rows = wts_b[l]
        dh1 = ffn_bwd(LA + l, dh, h1, p, hnf, w1, 0, rows)
        do, dwo = seq(_bwd_rows_data_w, c, f"o_bwd{l}", dh1, wo, o, unslot="rows")
        dq, dkv, dbias, dsink = seq(_attn_bwd, c, f"attn_bwd{l}", q, kv, do, probs)
        dsink_list[l] = dsink.reshape(Q_PER_KV, 2, KV_PAIRS).transpose(2, 1, 0).reshape(1, N_HEADS)
        dkv_list.append(dkv)
        dbias_list.append(dbias.reshape(N_HEADS, -1))
        dwq = seq(_wgrad_rows, c, f"q_bwd_w{l}", hn, [dq], D, c.DQ, unslot="cols")
        send(f"attn{l}", [([("b_w_o", l, 0)], dwo), ([("b_w_q", l, 0)], dwq)])
        if l > 0:
            dh, d_mix_g[LA + l] = seq(_bwd_rows_to_stream, c, f"q_bwd_data{l}", [dq], wq, c.qr, 0, c.DQ, h0,
                                      mix_g[LA + l], dh1)
    dwkv = seq(_wgrad_rows, c, "kv_bwd_w", hkv, dkv_list, D, 2 * c.DKV)
    send("kv", [([("w_k", 0, 0), ("w_v", 0, 1)], dwkv)])
    dh, d_mix_g[LA], d_kv_g = seq(_bwd_q_kv_to_stream, c, "q_kv_bwd_data", dq, wq, dkv_list, wkv, h_kv, mix_g[LA],
                                  kv_g, dh1)
    d_rel_t = seq(_band_bias_grad, dbias_list, onehot)
    d_rel_t = d_rel_t.reshape(KV_PAIRS, Q_PER_KV, 2, N_BUCKETS).swapaxes(1, 2).reshape(N_HEADS, N_BUCKETS)
    d_wsp, d_bsp, d_lng = [None] * LA, [None] * LA, [None] * LA
    for l in reversed(range(LA)):
        h0, z, a, hn, gated, h1, p, hnf = sav_a[l]
        w_in, in_i, w1, w1_i, rows, wout, wout_i = wts_a[l]
        dh1 = ffn_bwd(l, dh, h1, p, hnf, w1, w1_i, rows)
        dgated = seq(_bwd_rows_data, c, f"a_out_bwd_data{l}", dh1, wout, c.ar, wout_i)
        dwout = seq(_wgrad_rows, c, f"a_out_bwd_w{l}", gated, [dh1], c.AW, D)
        send(f"a_out{l}", [([("a_w_out", l, 0)], dwout)])
        dz, d_wsp[l], dbt, d_lng[l] = seq(_sgu_bwd, c, f"sgu_bwd{l}", a, z, dgated, ln_g_full[l], wsp16[l],
                                          wsp16_t[l], bsp_t[l])
        d_bsp[l] = dbt.T
        dwin = seq(_wgrad_cols, c, f"a_in_bwd_w{l}", hn, dz)
        send(f"a_in{l}", [([("a_w_in", l, 0)], dwin)])
        dh, d_mix_g[l] = seq(_bwd_cols_to_stream, c, f"a_in_bwd_data{l}", dz, w_in, in_i, h0, mix_g[l], dh1)
    grad_x = dh.reshape(1, S, D)

    small = {
        "mix_norm_g": jnp.concatenate(d_mix_g, axis=0), "ffn_norm_g": jnp.concatenate(d_ffn_g, axis=0),
        "a_w_spatial": jnp.stack(d_wsp), "a_b_spatial": jnp.stack(d_bsp), "kv_norm_g": d_kv_g,
        "b_sinks": jnp.concatenate(dsink_list, axis=0), "rel_bias": d_rel_t.T, "final_norm_g": d_fin_g,
    }
    small_names = list(small)
    packs = [_rows128(small[k]) for k in small_names] + [_rows128(jnp.concatenate(d_lng, axis=0)), _rows128(loss_row)]
    offs = [int(o) for o in np.cumsum([0] + [p.shape[0] for p in packs])]
    Rs = offs[-1] + (-offs[-1]) % (8 * NDEV)
    packed = jnp.concatenate(packs + [jnp.zeros((Rs - offs[-1], LANES), F32)], axis=0)
    slab = packed.reshape(NDEV, Rs // NDEV, LANES)
    st = seq(_send_start, "small_grads_start", [slab], [lax.empty(slab.shape, slab.dtype)], "exchange")
    while len(in_flight) > 1:
        land(*in_flight.pop(0))
    (parts,) = seq(_send_wait, "small_grads_wait", st, "exchange")
    mine = seq(_sum_parts, "small_grads_sum", parts)
    st = seq(_send_start, "small_sums_start", [mine], [_landing(mine)], "broadcast")
    while in_flight:
        land(*in_flight.pop(0))
    (sums,) = seq(_send_wait, "small_sums_wait", st, "broadcast")
    sums = sums.reshape(Rs, LANES)
    loss = sums[offs[-2], 0]

    grads, deltas, new_m, new_v = {}, {}, {}, {}

    def put(k, outs, shape):
        grads[k], deltas[k], new_m[k], new_v[k] = (t.reshape(shape) for t in outs)

    outs = seq(_adamw_packed, "adamw_small", sums, offs, *[[_rows128(d[k]) for k in small_names]
                                                          for d in (weights, m_in, v_in)])
    for n_, k in enumerate(small_names):
        shape = weights[k].shape
        size = int(np.prod(shape))
        put(k, [t.reshape(-1)[:size] for t in outs[4 * n_:4 * n_ + 4]], shape)
    lng_sum = sums[offs[-3]:offs[-2]].reshape(-1)[:LA * c.AW].reshape(LA, c.AW)
    lng_mine = lax.dynamic_slice_in_dim(lng_sum, me * c.ar, c.ar, axis=1)
    lng_parts = jnp.concatenate([lng_mine[None], jnp.zeros((NDEV - 1, LA, c.ar), F32)], axis=0)
    put("a_ln_g", seq(_adamw, "adamw_ln_g", lng_parts, (NDEV, LA, c.ar), lambda i: (0, 0, 0),
                      a_ln_g, m_in["a_ln_g"], v_in["a_ln_g"], LA), a_ln_g.shape)
    for k in ("a_w_in", "ffn_w1", "ffn_w2", "a_w_out", "b_w_o", "b_w_q", "w_k", "w_v"):
        put(k, results[k], weights[k].shape)

    return (loss, grad_x, *[grads[k] for k in names], *[deltas[k] for k in names],
            *[new_m[k] for k in names], *[new_v[k] for k in names])
```

```python
import numpy as np
import math
import jax
import jax.numpy as jnp
from jax import lax
from jax.experimental import pallas as pl
from jax.experimental.pallas import tpu as pltpu

F32 = jnp.float32
BF16 = jnp.bfloat16

NDEV = 8
EPS = 1e-6
CHUNK = 128
A_GROUPS = 8
N_HEADS = 16
N_KV_HEADS = 4
Q_PER_KV = N_HEADS // N_KV_HEADS
HEAD_DIM = 64
BLOCK = 128
N_BUCKETS = 32
MAX_DISTANCE = 128
ADAM_LR, ADAM_B1, ADAM_B2, ADAM_EPS, ADAM_WD, ADAM_STEP = 0.001, 0.9, 0.999, 1e-08, 0.01, 10
LANES = 128
VMEM_LIMIT = 56 * 1024 * 1024
INV_SQRT2 = 0.7071067811865476
INV_SQRT_2PI = 0.3989422804014327
EXCHANGE_LAG = 6
SIBLING_COLLECTIVE_ID = 0

HBM = pl.BlockSpec(memory_space=pltpu.HBM)
SMEM = pl.BlockSpec(memory_space=pltpu.SMEM)
ANY = pl.BlockSpec(memory_space=pl.ANY)
SEM = pl.BlockSpec(memory_space=pltpu.SEMAPHORE)
MESH = pl.DeviceIdType.MESH


def _params(n_grid):
    return pltpu.CompilerParams(dimension_semantics=("arbitrary",) * n_grid, vmem_limit_bytes=VMEM_LIMIT)


def _const(block, index_map):
    return pl.BlockSpec(block, index_map, pipeline_mode=pl.Buffered(1))


def _pcall(body, *, ins, in_specs, dep=None, **kw):
    n_in = len(ins)
    if dep is None or any(dep is t for t in ins):
        return pl.pallas_call(body, in_specs=list(in_specs), **kw)(*ins)

    def with_dep(*refs):
        body(*refs[:n_in], *refs[n_in + 1:])

    return pl.pallas_call(with_dep, in_specs=[*in_specs, ANY], **kw)(*ins, dep)


class _Seq:
    def __init__(self):
        self.last = None

    def __call__(self, fn, *args, **kw):
        out = fn(*args, dep=self.last, **kw)
        self.last = out[0] if isinstance(out, (tuple, list)) else out
        return out


def _rstd(h):
    return lax.rsqrt(jnp.mean(h * h, axis=-1, keepdims=True) + EPS)


def _rms_bwd(dhn, h, g, dres):
    r = _rstd(h)
    xh = h * r
    dg = jnp.sum(dhn * xh, axis=0, keepdims=True)
    dxh = dhn * g
    dx = r * (dxh - xh * jnp.mean(dxh * xh, axis=-1, keepdims=True))
    return dres + dx, dg


def _gelu(z):
    return 0.5 * z * (1.0 + lax.erf(z * INV_SQRT2))


def _gelu_grad(z):
    return 0.5 * (1.0 + lax.erf(z * INV_SQRT2)) + z * (jnp.exp(-0.5 * z * z) * INV_SQRT_2PI)


def _dot(a, b, dims):
    return lax.dot_general(a, b, (dims, ((), ())), preferred_element_type=F32)


NN = ((1,), (0,))
NT = ((1,), (1,))
TN = ((0,), (0,))


def _mm(name, ins, in_specs, out_shapes, out_specs, *, grid, dims, nk, acc_shape, load_a, load_b, epilogue,
        dep=None):
    n_in, n_out = len(ins), len(out_shapes)
    kax = len(grid) - 1

    def body(*refs):
        in_refs = refs[:n_in]
        out_refs = refs[n_in:n_in + n_out]
        a = load_a(in_refs, out_refs)
        b = load_b(in_refs)
        prod = _dot(a, b, dims)
        if nk == 1:
            epilogue(prod, in_refs, out_refs)
        else:
            acc = refs[n_in + n_out]
            k = pl.program_id(kax)

            @pl.when(k == 0)
            def _():
                acc[...] = prod

            @pl.when(k > 0)
            def _():
                acc[...] += prod

            @pl.when(k == nk - 1)
            def _():
                epilogue(acc[...], in_refs, out_refs)

    return _pcall(
        body, name=name, ins=ins, in_specs=in_specs, dep=dep, grid=grid, out_specs=out_specs, out_shape=out_shapes,
        scratch_shapes=[pltpu.VMEM(acc_shape, F32)] if nk > 1 else [], compiler_params=_params(len(grid)))


def _bf(ref_idx):
    return lambda in_refs, *_: in_refs[ref_idx][...].astype(BF16)


def _b_view(ref_idx, rows):
    def load(in_refs):
        b = in_refs[ref_idx][...]
        return b.reshape(rows, b.shape[-1])
    return load


class Cfg:
    pass


def _config(x, a_w_in, a_w_out, w_k, b_w_q, b_w_o, ffn_w1, ffn_w2):
    c = Cfg()
    c.S, c.D = x.shape[1], x.shape[2]
    c.LA, _, c.cw = a_w_in.shape
    c.AW2 = NDEV * c.cw
    c.AW = c.AW2 // 2
    c.gd = c.AW // A_GROUPS
    c.ar = a_w_out.shape[1]
    c.LF, _, c.fw = ffn_w1.shape
    c.fr = ffn_w2.shape[1]
    c.LB, c.qr, c.DQ = b_w_q.shape
    c.orr = b_w_o.shape[1]
    c.kr, c.DKV = w_k.shape
    c.tm = min(1024, c.S)
    c.tmw = min(512, c.S)
    c.tms = min(512, c.S)
    c.nb = c.S // BLOCK
    assert c.cw == c.fw == c.fr and c.AW == NDEV * c.ar and c.D == NDEV * c.qr == NDEV * c.kr
    assert c.DQ == NDEV * c.orr == N_HEADS * HEAD_DIM and c.DKV == N_KV_HEADS * HEAD_DIM
    assert c.S % c.tm == 0 and c.S % c.tmw == 0 and c.S % c.tms == 0 and c.tms % CHUNK == 0 and c.gd % LANES == 0
    assert c.LA >= 1 and c.LB >= 1 and c.LF == c.LA + c.LB
    return c


def _cached_rms(h_idx, g_idx, hn_out_idx, jax_axis=1):
    def load(in_refs, out_refs):
        hn_ref = out_refs[hn_out_idx]

        @pl.when(pl.program_id(jax_axis) == 0)
        def _():
            h = in_refs[h_idx][...]
            hn_ref[...] = (h * _rstd(h) * in_refs[g_idx][...]).astype(BF16)

        return hn_ref[...]
    return load


def _a_in_fwd(c, name, h, g, col, ci, dep=None):
    S, D, cw, tm = c.S, c.D, c.cw, c.tmw

    def body(h_ref, g_ref, w_ref, z_ref, a_ref, hn_ref):
        h = h_ref[...]
        hn = (h * _rstd(h) * g_ref[...]).astype(BF16)
        hn_ref[...] = hn
        for j in range(NDEV):
            cols = slice(j * cw, (j + 1) * cw)
            z = _dot(hn, w_ref[j], NT)
            z_ref[:, cols] = z.astype(BF16)
            a_ref[:, cols] = _gelu(z).astype(BF16)

    row = pl.BlockSpec((tm, D), lambda i: (i, 0))
    wide = pl.BlockSpec((tm, c.AW2), lambda i: (i, 0))
    return _pcall(
        body, name=name, ins=[h, g, col], dep=dep, grid=(S // tm,),
        in_specs=[row, pl.BlockSpec((1, D), lambda i: (0, 0)), _const((NDEV, None, cw, D), lambda i: (0, ci, 0, 0))],
        out_specs=[wide, wide, row],
        out_shape=[jax.ShapeDtypeStruct((S, c.AW2), BF16), jax.ShapeDtypeStruct((S, c.AW2), BF16),
                   jax.ShapeDtypeStruct((S, D), BF16)],
        compiler_params=_params(1))


def _rms_mm_rows(c, name, h, g, slab, blk_rows, blk_idx, n_out, dep=None):
    S, D, tm = c.S, c.D, c.tm

    def epilogue(acc, in_refs, out_refs):
        out_refs[0][...] = acc.astype(BF16)

    return _mm(
        name, [h, slab, g],
        [pl.BlockSpec((tm, D), lambda i, j, k: (i, 0)),
         pl.BlockSpec((NDEV, blk_rows, n_out), lambda i, j, k: (0, blk_idx, 0)),
         pl.BlockSpec((1, D), lambda i, j, k: (0, 0))],
        [jax.ShapeDtypeStruct((S, n_out), BF16), jax.ShapeDtypeStruct((S, D), BF16)],
        [pl.BlockSpec((tm, n_out), lambda i, j, k: (i, 0)), pl.BlockSpec((tm, D), lambda i, j, k: (i, 0))],
        grid=(S // tm, 1, 1), dims=NN, nk=1, acc_shape=None,
        load_a=_cached_rms(0, 2, 1), load_b=_b_view(1, NDEV * blk_rows), epilogue=epilogue, dep=dep)


def _rms2_mm_rows(c, name, h, projections, dep=None):
    S, D, tm = c.S, c.D, c.tm
    n_p = len(projections)

    def body(h_ref, *refs):
        ins, outs = refs[:2 * n_p], refs[2 * n_p:]
        h = h_ref[...]
        xh = h * _rstd(h)
        for k in range(n_p):
            hn = (xh * ins[2 * k][...]).astype(BF16)
            outs[2 * k + 1][...] = hn
            w = ins[2 * k + 1][...]
            outs[2 * k][...] = _dot(hn, w.reshape(D, w.shape[-1]), NN).astype(BF16)

    row = pl.BlockSpec((tm, D), lambda i: (i, 0))
    ins, in_specs, shapes, specs = [h], [row], [], []
    for g, slab in projections:
        n = slab.shape[-1]
        ins += [g, slab]
        in_specs += [pl.BlockSpec((1, D), lambda i: (0, 0)), pl.BlockSpec(slab.shape, lambda i: (0, 0, 0))]
        shapes += [jax.ShapeDtypeStruct((S, n), BF16), jax.ShapeDtypeStruct((S, D), BF16)]
        specs += [pl.BlockSpec((tm, n), lambda i: (i, 0)), row]
    return _pcall(body, name=name, ins=ins, in_specs=in_specs, dep=dep, grid=(S // tm,), out_specs=specs,
                  out_shape=shapes, compiler_params=_params(1))


def _mm_res(c, name, a, slab, blk_rows, blk_idx, res, dep=None):
    S, D, tm = c.S, c.D, c.tm
    K = NDEV * blk_rows

    def epilogue(acc, in_refs, out_refs):
        out_refs[0][...] = in_refs[2][...] + acc

    return _mm(
        name, [a, slab, res],
        [pl.BlockSpec((tm, K), lambda i, j, k: (i, 0)),
         pl.BlockSpec((NDEV, blk_rows, D), lambda i, j, k: (0, blk_idx, 0)),
         pl.BlockSpec((tm, D), lambda i, j, k: (i, 0))],
        [jax.ShapeDtypeStruct((S, D), F32)], [pl.BlockSpec((tm, D), lambda i, j, k: (i, 0))],
        grid=(S // tm, 1, 1), dims=NN, nk=1, acc_shape=None,
        load_a=_bf(0), load_b=_b_view(1, K), epilogue=epilogue, dep=dep)[0]


def _sgu_masks():
    ii = lax.broadcasted_iota(jnp.int32, (CHUNK, CHUNK), 0)
    jj = lax.broadcasted_iota(jnp.int32, (CHUNK, CHUNK), 1)
    return ii >= jj


def _sgu_fwd(c, name, a, ln_g, wc, b_t, dep=None):
    S, AW, gd, tm = c.S, c.AW, c.gd, c.tms

    def body(a_ref, lng_ref, wc_ref, bt_ref, out_ref):
        va = a_ref[:, AW:].astype(F32)
        xc = va - jnp.mean(va, axis=-1, keepdims=True)
        vn = (xc * lax.rsqrt(jnp.mean(xc * xc, axis=-1, keepdims=True) + EPS) * lng_ref[...]).astype(BF16)
        for ch in range(tm // CHUNK):
            rows = slice(ch * CHUNK, (ch + 1) * CHUNK)
            for g in range(A_GROUPS):
                cols = slice(g * gd, (g + 1) * gd)
                mixed = _dot(wc_ref[g], vn[rows, cols], NN) + bt_ref[:, g:g + 1]
                out_ref[rows, cols] = (a_ref[rows, cols].astype(F32) * mixed).astype(BF16)

    return _pcall(
        body, name=name, ins=[a, ln_g, wc, b_t], dep=dep, grid=(S // tm,),
        in_specs=[pl.BlockSpec((tm, 2 * AW), lambda i: (i, 0)), pl.BlockSpec((1, AW), lambda i: (0, 0)),
                  pl.BlockSpec((A_GROUPS, CHUNK, CHUNK), lambda i: (0, 0, 0)),
                  pl.BlockSpec((CHUNK, A_GROUPS), lambda i: (0, 0))],
        out_specs=pl.BlockSpec((tm, AW), lambda i: (i, 0)),
        out_shape=jax.ShapeDtypeStruct((S, AW), BF16), compiler_params=_params(1))


def _ffn_fwd(c, name, h, g, col, ci, rows, dep=None):
    S, D, fw, tm = c.S, c.D, c.fw, c.tmw
    F = NDEV * fw

    def body(h_ref, g_ref, w1_ref, w2_ref, p_ref, out_ref, hn_ref, r_ref):
        h = h_ref[...]
        hn = (h * _rstd(h) * g_ref[...]).astype(BF16)
        hn_ref[...] = hn
        for j in range(NDEV):
            cols = slice(j * fw, (j + 1) * fw)
            p = jnp.maximum(_dot(hn, w1_ref[j], NT), 0.0)
            p_ref[:, cols] = p.astype(BF16)
            r_ref[:, cols] = (p * p).astype(BF16)
        out_ref[...] = h + _dot(r_ref[...], w2_ref[...].reshape(F, D), NN)

    row = pl.BlockSpec((tm, D), lambda i: (i, 0))
    return _pcall(
        body, name=name, ins=[h, g, col, rows], dep=dep, grid=(S // tm,),
        in_specs=[row, pl.BlockSpec((1, D), lambda i: (0, 0)),
                  _const((NDEV, None, fw, D), lambda i: (0, ci, 0, 0)), _const((NDEV, c.fr, D), lambda i: (0, 0, 0))],
        out_specs=[pl.BlockSpec((tm, F), lambda i: (i, 0)), row, row],
        out_shape=[jax.ShapeDtypeStruct((S, F), BF16), jax.ShapeDtypeStruct((S, D), F32),
                   jax.ShapeDtypeStruct((S, D), BF16)],
        scratch_shapes=[pltpu.VMEM((tm, F), BF16)], compiler_params=_params(1))


def _bucket_table():
    qi = np.arange(BLOCK)[:, None]
    kj = np.arange(2 * BLOCK)[None, :]
    d = np.maximum(qi + BLOCK - kj, 0)
    max_exact = N_BUCKETS // 2
    ratio = np.log(np.maximum(d, 1).astype(np.float32) / np.float32(max_exact)) / np.float32(
        math.log(MAX_DISTANCE / max_exact))
    large = np.minimum(max_exact + (ratio.astype(np.float32) * np.float32(N_BUCKETS - max_exact)).astype(np.int32),
                       N_BUCKETS - 1)
    return np.where(d < max_exact, d, large).astype(np.int32)


def _bucket_onehot():
    b = jnp.asarray(_bucket_table().reshape(1, -1))
    return (b == lax.broadcasted_iota(jnp.int32, (N_BUCKETS, b.shape[1]), 0)).astype(F32)


def _whole(t):
    return pl.BlockSpec(t.shape, lambda: (0,) * t.ndim)


def _band_bias(rel_bias_t, onehot, dep=None):
    def body(r_ref, oh_ref, out_ref):
        out_ref[...] = lax.dot_general(r_ref[...], oh_ref[...], (NN, ((), ())), preferred_element_type=F32,
                                       precision=lax.Precision.HIGHEST)

    n = onehot.shape[1]
    return _pcall(body, name="band_bias", ins=[rel_bias_t, onehot], in_specs=[_whole(rel_bias_t), _whole(onehot)],
                  dep=dep, out_shape=jax.ShapeDtypeStruct((N_HEADS, n), F32), compiler_params=_params(0))


def _band_bias_grad(dbias_list, onehot, dep=None):
    n_in = len(dbias_list)

    def body(*refs):
        oh_ref, out_ref = refs[n_in], refs[n_in + 1]
        d = refs[0][...]
        for r in refs[1:n_in]:
            d = d + r[...]
        out_ref[...] = lax.dot_general(d, oh_ref[...], (NT, ((), ())), preferred_element_type=F32,
                                       precision=lax.Precision.HIGHEST)

    ins = [*dbias_list, onehot]
    return _pcall(body, name="band_bias_grad", ins=ins, in_specs=[_whole(t) for t in ins], dep=dep,
                  out_shape=jax.ShapeDtypeStruct((N_HEADS, N_BUCKETS), F32), compiler_params=_params(0))


KV_PAIRS = N_KV_HEADS // 2
PAIR_ROWS = 2 * Q_PER_KV * BLOCK
MASKED = float(np.finfo(np.float32).min) / 2


def _slot_cols(w):
    lead = w.shape[:-1]
    return w.reshape(*lead, KV_PAIRS, 2, Q_PER_KV, HEAD_DIM).swapaxes(-3, -2).reshape(*lead, N_HEADS * HEAD_DIM)


def _slot_rows(blocks):
    n = blocks.shape[-1]
    return blocks.reshape(KV_PAIRS, 2, Q_PER_KV, HEAD_DIM, n).swapaxes(1, 2).reshape(blocks.shape)


def _slot_bias(bias):
    qi = np.arange(BLOCK)[:, None]
    kj = np.arange(2 * BLOCK)[None, :]
    dist = qi + BLOCK - kj
    window = (dist >= 0) & (dist < BLOCK)
    b = bias.reshape(KV_PAIRS, 2, Q_PER_KV, BLOCK, 2 * BLOCK).swapaxes(1, 2).reshape(KV_PAIRS, PAIR_ROWS, 2 * BLOCK)
    tile = lambda mk: jnp.asarray(np.tile(mk, (2 * Q_PER_KV, 1)))[None]
    return jnp.stack([jnp.where(tile(window & (kj >= BLOCK)), b, MASKED), jnp.where(tile(window), b, MASKED)])


def _pair_kv(kvc_ref, kvp_ref, kvp, dkv):
    lanes = slice(kvp * LANES, (kvp + 1) * LANES)
    vlanes = slice(dkv + kvp * LANES, dkv + (kvp + 1) * LANES)
    k2 = jnp.concatenate([kvp_ref[:, lanes], kvc_ref[:, lanes]], axis=0)
    v2 = jnp.concatenate([kvp_ref[:, vlanes], kvc_ref[:, vlanes]], axis=0)
    return k2, v2


def _head_operand(ref, grp, par, low, scale=None):
    xg = ref[:, grp * LANES:(grp + 1) * LANES]
    if scale is not None:
        xg = xg * scale
    zero = jnp.zeros_like(xg)
    return jnp.where(low, xg, zero) if par == 0 else jnp.where(low, zero, xg)


def _head_probs(qh, k2, bias_rows, sink):
    s = _dot(qh, k2, NT) + bias_rows
    m = jnp.maximum(jnp.max(s, axis=-1, keepdims=True), sink)
    e = jnp.exp(s - m)
    es = jnp.exp(sink - m)
    inv = 1.0 / (jnp.sum(e, axis=-1, keepdims=True) + es)
    return e * inv, es * inv


def _attn_specs(c):
    dq, dkv2 = c.DQ, 2 * c.DKV
    return [pl.BlockSpec((BLOCK, dq), lambda n: (n, 0)),
            pl.BlockSpec((BLOCK, dkv2), lambda n: (n, 0)),
            pl.BlockSpec((BLOCK, dkv2), lambda n: (jnp.maximum(n - 1, 0), 0))]


def _bias_spec():
    return pl.BlockSpec((None, KV_PAIRS, PAIR_ROWS, 2 * BLOCK), lambda n: (jnp.minimum(n, 1), 0, 0, 0))


def _low_lanes():
    return lax.broadcasted_iota(jnp.int32, (BLOCK, LANES), 1) < HEAD_DIM


def _first_key():
    return lax.broadcasted_iota(jnp.int32, (BLOCK, 2 * BLOCK), 1) == 0


def _probs_spec():
    return pl.BlockSpec((None, KV_PAIRS, PAIR_ROWS, 2 * BLOCK), lambda n: (n, 0, 0, 0))


def _attn_fwd(c, name, q, kv, bias, sinks, dep=None):
    S, dq = c.S, c.DQ

    def body(q_ref, kvc_ref, kvp_ref, bias_ref, sink_ref, o_ref, probs_ref):
        low = _low_lanes()
        first = _first_key()
        for kvp in range(KV_PAIRS):
            k2, v2 = _pair_kv(kvc_ref, kvp_ref, kvp, c.DKV)
            for g in range(Q_PER_KV):
                grp = kvp * Q_PER_KV + g
                halves = []
                for par in range(2):
                    rows = slice((2 * g + par) * BLOCK, (2 * g + par + 1) * BLOCK)
                    qh = _head_operand(q_ref, grp, par, low, scale=HEAD_DIM ** -0.5)
                    p, ps = _head_probs(qh, k2, bias_ref[kvp, rows, :], sink_ref[(2 * kvp + par) * Q_PER_KV + g])
                    probs_ref[kvp, rows, :] = jnp.where(first, ps, p).astype(BF16)
                    halves.append(_dot(p.astype(BF16), v2, NN))
                o_ref[:, grp * LANES:(grp + 1) * LANES] = jnp.where(low, halves[0], halves[1]).astype(BF16)

    return _pcall(
        body, name=name, ins=[q, kv, kv, bias, sinks], dep=dep, grid=(c.nb,),
        in_specs=_attn_specs(c) + [_bias_spec(), SMEM],
        out_specs=[pl.BlockSpec((BLOCK, dq), lambda n: (n, 0)), _probs_spec()],
        out_shape=[jax.ShapeDtypeStruct((S, dq), BF16),
                   jax.ShapeDtypeStruct((c.nb, KV_PAIRS, PAIR_ROWS, 2 * BLOCK), BF16)],
        compiler_params=_params(1))


def _final_loss(c, h, g, target, dep=None):
    S, D, tm = c.S, c.D, c.tm

    def body(h_ref, g_ref, t_ref, dh_ref, dg_ref, loss_ref):
        i = pl.program_id(0)
        h = h_ref[...]
        gg = g_ref[...]
        r = _rstd(h)
        xh = h * r
        err = xh * gg - t_ref[...]
        lp = jnp.sum(jnp.sum(err * err, axis=1, keepdims=True), axis=0, keepdims=True) * (0.5 / D)
        dx, dg = _rms_bwd(err * (1.0 / D), h, gg, 0.0)
        dh_ref[...] = dx

        @pl.when(i == 0)
        def _():
            dg_ref[...] = dg
            loss_ref[...] = jnp.broadcast_to(lp, loss_ref.shape)

        @pl.when(i > 0)
        def _():
            dg_ref[...] += dg
            loss_ref[...] += jnp.broadcast_to(lp, loss_ref.shape)

    row = pl.BlockSpec((tm, D), lambda i: (i, 0))
    return _pcall(
        body, name="final_loss", ins=[h, g, target], dep=dep, grid=(S // tm,),
        in_specs=[row, pl.BlockSpec((1, D), lambda i: (0, 0)), row],
        out_specs=[row, pl.BlockSpec((1, D), lambda i: (0, 0)), pl.BlockSpec((1, LANES), lambda i: (0, 0))],
        out_shape=[jax.ShapeDtypeStruct((S, D), F32), jax.ShapeDtypeStruct((1, D), F32),
                   jax.ShapeDtypeStruct((1, LANES), F32)],
        compiler_params=_params(1))


def _rms_bwd_epilogue(h_idx, g_idx, res_idx):
    def epilogue(dhn, in_refs, out_refs):
        dh, dg = _rms_bwd(dhn, in_refs[h_idx][...], in_refs[g_idx][...], in_refs[res_idx][...])
        out_refs[0][...] = dh
        i = pl.program_id(0)

        @pl.when(i == 0)
        def _():
            out_refs[1][...] = dg

        @pl.when(i > 0)
        def _():
            out_refs[1][...] += dg
    return epilogue


def _stream_outs(c, tm):
    S, D = c.S, c.D
    return ([jax.ShapeDtypeStruct((S, D), F32), jax.ShapeDtypeStruct((1, D), F32)],
            [pl.BlockSpec((tm, D), lambda i, j, k: (i, 0)), pl.BlockSpec((1, D), lambda i, j, k: (0, 0))])


def _row_specs(c, tm):
    D = c.D
    return [pl.BlockSpec((tm, D), lambda i, j, k: (i, 0)), pl.BlockSpec((1, D), lambda i, j, k: (0, 0)),
            pl.BlockSpec((tm, D), lambda i, j, k: (i, 0))]


def _bwd_rows_to_stream(c, name, dy_list, slab, blk_rows, blk_idx, n_in_cols, h, g, dres, dep=None):
    S, D, tm = c.S, c.D, c.tm
    nd = len(dy_list)

    def load_a(in_refs, out_refs):
        a = in_refs[0][...]
        for r in in_refs[1:nd]:
            a = a + r[...]
        return a.astype(BF16)

    shapes, specs = _stream_outs(c, tm)
    return _mm(
        name, [*dy_list, slab, h, g, dres],
        [pl.BlockSpec((tm, n_in_cols), lambda i, j, k: (i, 0))] * nd
        + [pl.BlockSpec((NDEV, blk_rows, n_in_cols), lambda i, j, k: (0, blk_idx, 0))] + _row_specs(c, tm),
        shapes, specs, grid=(S // tm, 1, 1), dims=NT, nk=1, acc_shape=None,
        load_a=load_a, load_b=_b_view(nd, NDEV * blk_rows), epilogue=_rms_bwd_epilogue(nd + 1, nd + 2, nd + 3),
        dep=dep)


def _bwd_q_kv_to_stream(c, name, dq, wq, dkv_list, wkv, h, g_q, g_kv, dres, dep=None):
    S, D, tm = c.S, c.D, c.tm
    nkv = len(dkv_list)

    def body(dq_ref, wq_ref, *rest):
        dkv_refs = rest[:nkv]
        wkv_ref, h_ref, gq_ref, gkv_ref, dres_ref, out_ref, dgq_ref, dgkv_ref = rest[nkv:]
        i = pl.program_id(0)
        dhn_q = _dot(dq_ref[...], wq_ref[...].reshape(D, c.DQ), NT)
        dkv = dkv_refs[0][...]
        for r in dkv_refs[1:]:
            dkv = dkv + r[...]
        dhn_kv = _dot(dkv.astype(BF16), wkv_ref[...].reshape(D, 2 * c.DKV), NT)
        h = h_ref[...]
        r = _rstd(h)
        xh = h * r
        dxh = dhn_q * gq_ref[...] + dhn_kv * gkv_ref[...]
        out_ref[...] = dres_ref[...] + r * (dxh - xh * jnp.mean(dxh * xh, axis=-1, keepdims=True))
        dgq = jnp.sum(dhn_q * xh, axis=0, keepdims=True)
        dgkv = jnp.sum(dhn_kv * xh, axis=0, keepdims=True)

        @pl.when(i == 0)
        def _():
            dgq_ref[...] = dgq
            dgkv_ref[...] = dgkv

        @pl.when(i > 0)
        def _():
            dgq_ref[...] += dgq
            dgkv_ref[...] += dgkv

    row = pl.BlockSpec((tm, D), lambda i: (i, 0))
    gain = pl.BlockSpec((1, D), lambda i: (0, 0))
    return _pcall(
        body, name=name, ins=[dq, wq, *dkv_list, wkv, h, g_q, g_kv, dres], dep=dep, grid=(S // tm,),
        in_specs=[pl.BlockSpec((tm, c.DQ), lambda i: (i, 0)), pl.BlockSpec((NDEV, c.qr, c.DQ), lambda i: (0, 0, 0))]
        + [pl.BlockSpec((tm, 2 * c.DKV), lambda i: (i, 0))] * nkv
        + [pl.BlockSpec((NDEV, c.kr, 2 * c.DKV), lambda i: (0, 0, 0)), row, gain, gain, row],
        out_specs=[row, gain, gain],
        out_shape=[jax.ShapeDtypeStruct((S, D), F32), jax.ShapeDtypeStruct((1, D), F32),
                   jax.ShapeDtypeStruct((1, D), F32)],
        compiler_params=_params(1))


def _bwd_cols_to_stream(c, name, dy, col, ci, h, g, dres, dep=None):
    S, D, cw, tm = c.S, c.D, c.cw, c.tmw
    K = NDEV * cw
    shapes, specs = _stream_outs(c, tm)
    return _mm(
        name, [dy, col, h, g, dres],
        [pl.BlockSpec((tm, K), lambda i, j, k: (i, 0)),
         _const((NDEV, None, cw, D), lambda i, j, k: (0, ci, 0, 0))] + _row_specs(c, tm),
        shapes, specs, grid=(S // tm, 1, 1), dims=NN, nk=1, acc_shape=None,
        load_a=_bf(0), load_b=_b_view(1, K), epilogue=_rms_bwd_epilogue(2, 3, 4), dep=dep)


def _bwd_rows_data(c, name, dy, slab, blk_rows, blk_idx, dep=None):
    S, D, tm = c.S, c.D, c.tm
    K = NDEV * blk_rows

    def epilogue(acc, in_refs, out_refs):
        out_refs[0][...] = acc.astype(BF16)

    return _mm(
        name, [dy, slab],
        [pl.BlockSpec((tm, D), lambda i, j, k: (i, 0)),
         pl.BlockSpec((NDEV, blk_rows, D), lambda i, j, k: (0, blk_idx, 0))],
        [jax.ShapeDtypeStruct((S, K), BF16)], [pl.BlockSpec((tm, K), lambda i, j, k: (i, 0))],
        grid=(S // tm, 1, 1), dims=NT, nk=1, acc_shape=None,
        load_a=_bf(0), load_b=_b_view(1, K), epilogue=epilogue, dep=dep)[0]


def _owner_blocks(out, acc, blk_rows, n_b, unslot):
    if unslot is None:
        out[...] = acc.reshape(NDEV, blk_rows, n_b).astype(BF16)
        return
    per_blk = blk_rows // HEAD_DIM
    for h in range(N_HEADS):
        s = (h // (2 * Q_PER_KV)) * 2 * Q_PER_KV + (h % Q_PER_KV) * 2 + (h // Q_PER_KV) % 2
        if unslot == "rows":
            out[h // per_blk, (h % per_blk) * HEAD_DIM:(h % per_blk + 1) * HEAD_DIM, :] = (
                acc[s * HEAD_DIM:(s + 1) * HEAD_DIM, :].astype(BF16))
        else:
            out[:, :, h * HEAD_DIM:(h + 1) * HEAD_DIM] = (
                acc[:, s * HEAD_DIM:(s + 1) * HEAD_DIM].reshape(NDEV, blk_rows, HEAD_DIM).astype(BF16))


def _bwd_rows_data_w(c, name, dy, slab, x, unslot=None, dep=None):
    S, D, tm = c.S, c.D, c.tm
    blk_rows = slab.shape[1]
    K = NDEV * blk_rows
    nk = S // tm

    def body(dy_ref, w_ref, x_ref, dx_ref, dw_ref, acc):
        k = pl.program_id(0)
        dyb = dy_ref[...].astype(BF16)
        dx_ref[...] = _dot(dyb, w_ref[...].reshape(K, D), NT).astype(BF16)
        part = _dot(x_ref[...], dyb, TN)

        @pl.when(k == 0)
        def _():
            acc[...] = part

        @pl.when(k > 0)
        def _():
            acc[...] += part

        @pl.when(k == nk - 1)
        def _():
            _owner_blocks(dw_ref, acc[...], blk_rows, D, unslot)

    return _pcall(
        body, name=name, ins=[dy, slab, x], dep=dep, grid=(nk,),
        in_specs=[pl.BlockSpec((tm, D), lambda k: (k, 0)), pl.BlockSpec((NDEV, blk_rows, D), lambda k: (0, 0, 0)),
                  pl.BlockSpec((tm, K), lambda k: (k, 0))],
        out_specs=[pl.BlockSpec((tm, K), lambda k: (k, 0)), pl.BlockSpec((NDEV, blk_rows, D), lambda k: (0, 0, 0))],
        out_shape=[jax.ShapeDtypeStruct((S, K), BF16), jax.ShapeDtypeStruct((NDEV, blk_rows, D), BF16)],
        scratch_shapes=[pltpu.VMEM((K, D), F32)], compiler_params=_params(1))


def _wgrad_rows(c, name, a, b_list, n_a, n_b, unslot=None, dep=None):
    S, tm = c.S, c.tm
    nb_in = len(b_list)
    blk_rows = n_a // NDEV

    def load_b(in_refs):
        b = in_refs[1][...]
        for r in in_refs[2:1 + nb_in]:
            b = b + r[...]
        return b.astype(BF16)

    def epilogue(acc, in_refs, out_refs):
        _owner_blocks(out_refs[0], acc, blk_rows, n_b, unslot)

    return _mm(
        name, [a, *b_list],
        [pl.BlockSpec((tm, n_a), lambda i, j, k: (k, 0))] + [pl.BlockSpec((tm, n_b), lambda i, j, k: (k, 0))] * nb_in,
        [jax.ShapeDtypeStruct((NDEV, blk_rows, n_b), BF16)],
        [pl.BlockSpec((NDEV, blk_rows, n_b), lambda i, j, k: (0, 0, 0))],
        grid=(1, 1, S // tm), dims=TN, nk=S // tm, acc_shape=(n_a, n_b),
        load_a=_bf(0), load_b=load_b, epilogue=epilogue, dep=dep)[0]


def _wgrad_cols(c, name, a, b, dep=None):
    S, D, cw = c.S, c.D, c.cw

    def body(a_ref, b_ref, out_ref):
        out_ref[...] = _dot(a_ref[...], b_ref[...], TN).astype(BF16)

    return _pcall(
        body, name=name, ins=[a, b], dep=dep, grid=(NDEV,),
        in_specs=[_const((S, D), lambda j: (0, 0)), pl.BlockSpec((S, cw), lambda j: (0, j))],
        out_specs=pl.BlockSpec((None, D, cw), lambda j: (j, 0, 0)),
        out_shape=jax.ShapeDtypeStruct((NDEV, D, cw), BF16), compiler_params=_params(1))


def _ffn_bwd_data(c, name, dh, p, col, ci, rows, h, g, dep=None):
    S, D, fw, tm = c.S, c.D, c.fw, c.tmw
    F = NDEV * fw

    def body(dh_ref, p_ref, w1t_ref, w2_ref, h_ref, g_ref, da_ref, out_ref, dg_ref, dhb_ref):
        i = pl.program_id(0)
        dh = dh_ref[...]
        dhb = dh.astype(BF16)
        dhb_ref[...] = dhb
        for j in range(NDEV):
            cols = slice(j * fw, (j + 1) * fw)
            da_ref[:, cols] = (_dot(dhb, w2_ref[j], NT) * (2.0 * p_ref[:, cols].astype(F32))).astype(BF16)
        dx, dg = _rms_bwd(_dot(da_ref[...], w1t_ref[...].reshape(F, D), NN), h_ref[...], g_ref[...], dh)
        out_ref[...] = dx

        @pl.when(i == 0)
        def _():
            dg_ref[...] = dg

        @pl.when(i > 0)
        def _():
            dg_ref[...] += dg

    row = pl.BlockSpec((tm, D), lambda i: (i, 0))
    wide = pl.BlockSpec((tm, F), lambda i: (i, 0))
    return _pcall(
        body, name=name, ins=[dh, p, col, rows, h, g], dep=dep, grid=(S // tm,),
        in_specs=[row, wide, _const((NDEV, None, fw, D), lambda i: (0, ci, 0, 0)),
                  _const((NDEV, c.fr, D), lambda i: (0, 0, 0)),
                  row, pl.BlockSpec((1, D), lambda i: (0, 0))],
        out_specs=[wide, row, pl.BlockSpec((1, D), lambda i: (0, 0)), row],
        out_shape=[jax.ShapeDtypeStruct((S, F), BF16), jax.ShapeDtypeStruct((S, D), F32),
                   jax.ShapeDtypeStruct((1, D), F32), jax.ShapeDtypeStruct((S, D), BF16)],
        compiler_params=_params(1))


def _ffn_bwd_w(c, name, hn, da, p, dhb, dep=None):
    S, D, fw = c.S, c.D, c.fw

    def body(hn_ref, da_ref, p_ref, dhb_ref, dw1_ref, dw2_ref):
        dw1_ref[...] = _dot(hn_ref[...], da_ref[...], TN).astype(BF16)
        pf = p_ref[...].astype(F32)
        dw2_ref[...] = _dot((pf * pf).astype(BF16), dhb_ref[...], TN).astype(BF16)

    panel = pl.BlockSpec((S, fw), lambda j: (0, j))
    return _pcall(
        body, name=name, ins=[hn, da, p, dhb], dep=dep, grid=(NDEV,),
        in_specs=[_const((S, D), lambda j: (0, 0)), panel, panel, _const((S, D), lambda j: (0, 0))],
        out_specs=[pl.BlockSpec((None, D, fw), lambda j: (j, 0, 0)), pl.BlockSpec((None, c.fr, D), lambda j: (j, 0, 0))],
        out_shape=[jax.ShapeDtypeStruct((NDEV, D, fw), BF16), jax.ShapeDtypeStruct((NDEV, c.fr, D), BF16)],
        compiler_params=_params(1))


def _attn_bwd(c, name, q, kv, do, probs, dep=None):
    S, dq, dkv = c.S, c.DQ, c.DKV
    nb = c.nb
    scale = HEAD_DIM ** -0.5

    def body(q_ref, kvc_ref, kvp_ref, do_ref, probs_ref, dq_ref, dkv_ref, dbias_ref, dsink_ref, dsink_acc,
             ds_sc, p_sc, qm_sc, dom_sc):
        n = pl.program_id(0)

        @pl.when(n == 0)
        def _():
            dkv_ref[...] = jnp.zeros_like(dkv_ref)
            dbias_ref[...] = jnp.zeros_like(dbias_ref)
            dsink_acc[...] = jnp.zeros_like(dsink_acc)

        low = _low_lanes()
        first = _first_key()
        rows_c = pl.ds(pl.multiple_of(n * BLOCK, BLOCK), BLOCK)
        rows_p = pl.ds(pl.multiple_of(jnp.maximum(n - 1, 0) * BLOCK, BLOCK), BLOCK)
        for kvp in range(KV_PAIRS):
            k2, v2 = _pair_kv(kvc_ref, kvp_ref, kvp, dkv)
            for g in range(Q_PER_KV):
                grp = kvp * Q_PER_KV + g
                halves = []
                for par in range(2):
                    rows = slice((2 * g + par) * BLOCK, (2 * g + par + 1) * BLOCK)
                    qh = _head_operand(q_ref, grp, par, low, scale=scale)
                    doh = _head_operand(do_ref, grp, par, low)
                    saved = probs_ref[kvp, rows, :]
                    ps = saved[:, 0:1].astype(F32)
                    p16 = jnp.where(first, jnp.zeros_like(saved), saved)
                    p = p16.astype(F32)
                    dp = _dot(doh, v2, NT)
                    delta = jnp.sum(p * dp, axis=-1, keepdims=True)
                    ds = p * (dp - delta)
                    dbias_ref[kvp, rows, :] += ds
                    dsink_acc[rows, kvp:kvp + 1] += -(ps * delta)
                    ds16 = ds.astype(BF16)
                    halves.append(_dot(ds16, k2, NN) * scale)
                    ds_sc[rows, :] = ds16
                    p_sc[rows, :] = p16
                    qm_sc[rows, :] = qh
                    dom_sc[rows, :] = doh
                dq_ref[:, grp * LANES:(grp + 1) * LANES] = jnp.where(low, halves[0], halves[1]).astype(BF16)
            dk2 = _dot(ds_sc[...], qm_sc[...], TN)
            dv2 = _dot(p_sc[...], dom_sc[...], TN)
            lanes = slice(kvp * LANES, (kvp + 1) * LANES)
            vlanes = slice(dkv + kvp * LANES, dkv + (kvp + 1) * LANES)
            dkv_ref[rows_p, lanes] += dk2[:BLOCK]
            dkv_ref[rows_c, lanes] += dk2[BLOCK:]
            dkv_ref[rows_p, vlanes] += dv2[:BLOCK]
            dkv_ref[rows_c, vlanes] += dv2[BLOCK:]

        @pl.when(n == nb - 1)
        def _():
            dsink_ref[...] = jnp.sum(dsink_acc[...].reshape(2 * Q_PER_KV, BLOCK, KV_PAIRS), axis=1)

    return _pcall(
        body, name=name, ins=[q, kv, kv, do, probs], dep=dep, grid=(nb,),
        in_specs=_attn_specs(c) + [pl.BlockSpec((BLOCK, dq), lambda n: (n, 0)), _probs_spec()],
        out_specs=[pl.BlockSpec((BLOCK, dq), lambda n: (n, 0)), pl.BlockSpec((S, 2 * dkv), lambda n: (0, 0)),
                   pl.BlockSpec((KV_PAIRS, PAIR_ROWS, 2 * BLOCK), lambda n: (0, 0, 0)),
                   pl.BlockSpec((2 * Q_PER_KV, KV_PAIRS), lambda n: (0, 0))],
        out_shape=[jax.ShapeDtypeStruct((S, dq), BF16), jax.ShapeDtypeStruct((S, 2 * dkv), F32),
                   jax.ShapeDtypeStruct((KV_PAIRS, PAIR_ROWS, 2 * BLOCK), F32),
                   jax.ShapeDtypeStruct((2 * Q_PER_KV, KV_PAIRS), F32)],
        scratch_shapes=[pltpu.VMEM((PAIR_ROWS, KV_PAIRS), F32), pltpu.VMEM((PAIR_ROWS, 2 * BLOCK), BF16),
                        pltpu.VMEM((PAIR_ROWS, 2 * BLOCK), BF16), pltpu.VMEM((PAIR_ROWS, LANES), BF16),
                        pltpu.VMEM((PAIR_ROWS, LANES), BF16)],
        compiler_params=_params(1))


def _sgu_bwd(c, name, a, z, dgated, ln_g, wc, wc_t, b_t, dep=None):
    S, AW, gd, tm = c.S, c.AW, c.gd, c.tms

    def body(a_ref, z_ref, dg_ref, lng_ref, wc_ref, wct_ref, bt_ref, dz_ref, dws_ref, dbt_ref, dlng_ref, dvn_ref):
        i = pl.program_id(0)

        @pl.when(i == 0)
        def _():
            dws_ref[...] = jnp.zeros_like(dws_ref)
            dbt_ref[...] = jnp.zeros_like(dbt_ref)
            dlng_ref[...] = jnp.zeros_like(dlng_ref)

        lng = lng_ref[...]
        va = a_ref[:, AW:].astype(F32)
        xc = va - jnp.mean(va, axis=-1, keepdims=True)
        rstd = lax.rsqrt(jnp.mean(xc * xc, axis=-1, keepdims=True) + EPS)
        xh = xc * rstd
        vn = (xh * lng).astype(BF16)
        causal = _sgu_masks()
        for ch in range(tm // CHUNK):
            rows = slice(ch * CHUNK, (ch + 1) * CHUNK)
            for g in range(A_GROUPS):
                cols = slice(g * gd, (g + 1) * gd)
                blk = vn[rows, cols]
                mixed = _dot(wc_ref[g], blk, NN) + bt_ref[:, g:g + 1]
                dgb = dg_ref[rows, cols].astype(F32)
                dm = dgb * a_ref[rows, cols].astype(F32)
                dbt_ref[:, g:g + 1] += jnp.sum(dm, axis=1, keepdims=True)
                dm16 = dm.astype(BF16)
                dws_ref[g] += jnp.where(causal, _dot(dm16, blk, NT), 0.0)
                dvn_ref[rows, cols] = _dot(wct_ref[g], dm16, NN)
                dz_ref[rows, cols] = (dgb * mixed * _gelu_grad(z_ref[rows, cols].astype(F32))).astype(BF16)
        dvn = dvn_ref[...]
        dlng_ref[...] += jnp.sum(dvn * xh, axis=0, keepdims=True)
        dxh = dvn * lng
        dva = rstd * (dxh - jnp.mean(dxh, axis=-1, keepdims=True) - xh * jnp.mean(dxh * xh, axis=-1, keepdims=True))
        dz_ref[:, AW:] = (dva * _gelu_grad(z_ref[:, AW:].astype(F32))).astype(BF16)

    wide = pl.BlockSpec((tm, 2 * AW), lambda i: (i, 0))
    wsp = pl.BlockSpec((A_GROUPS, CHUNK, CHUNK), lambda i: (0, 0, 0))
    btsp = pl.BlockSpec((CHUNK, A_GROUPS), lambda i: (0, 0))
    return _pcall(
        body, name=name, ins=[a, z, dgated, ln_g, wc, wc_t, b_t], dep=dep, grid=(S // tm,),
        in_specs=[wide, wide, pl.BlockSpec((tm, AW), lambda i: (i, 0)), pl.BlockSpec((1, AW), lambda i: (0, 0)),
                  wsp, wsp, btsp],
        out_specs=[wide, wsp, btsp, pl.BlockSpec((1, AW), lambda i: (0, 0))],
        out_shape=[jax.ShapeDtypeStruct((S, 2 * AW), BF16), jax.ShapeDtypeStruct((A_GROUPS, CHUNK, CHUNK), F32),
                   jax.ShapeDtypeStruct((CHUNK, A_GROUPS), F32), jax.ShapeDtypeStruct((1, AW), F32)],
        scratch_shapes=[pltpu.VMEM((tm, AW), F32)], compiler_params=_params(1))


def _adam_update(g, w_ref, m_ref, v_ref, out_refs):
    g_ref, d_ref, nm_ref, nv_ref = out_refs
    bc1 = 1.0 - ADAM_B1 ** ADAM_STEP
    bc2 = 1.0 - ADAM_B2 ** ADAM_STEP
    nm = ADAM_B1 * m_ref[...] + (1.0 - ADAM_B1) * g
    nv = ADAM_B2 * v_ref[...] + (1.0 - ADAM_B2) * (g * g)
    g_ref[...] = g
    nm_ref[...] = nm
    nv_ref[...] = nv
    d_ref[...] = -ADAM_LR * ((nm * (1.0 / bc1)) / (jnp.sqrt(nv * (1.0 / bc2)) + ADAM_EPS) + ADAM_WD * w_ref[...])


def _adamw_packed(name, sums, offs, ws, ms, vs, dep=None):
    n = len(ws)

    def body(*refs):
        s_ref, w_refs, m_refs, v_refs = refs[0], refs[1:1 + n], refs[1 + n:1 + 2 * n], refs[1 + 2 * n:1 + 3 * n]
        outs = refs[1 + 3 * n:]
        for k in range(n):
            _adam_update(s_ref[offs[k]:offs[k + 1], :], w_refs[k], m_refs[k], v_refs[k], outs[4 * k:4 * k + 4])

    ins = [sums, *ws, *ms, *vs]
    shapes = [jax.ShapeDtypeStruct(w.shape, F32) for w in ws for _ in range(4)]
    return _pcall(body, name=name, ins=ins, in_specs=[_whole(t) for t in ins], dep=dep,
                  out_shape=shapes, out_specs=[_whole(t) for t in shapes], compiler_params=_params(0))


def _adamw(name, parts, part_block, part_index, w, m, v, tr, row_off=0, n_rows=None, prev=None, dep=None):
    R, C = w.shape
    n_rows = R if n_rows is None else n_rows
    assert n_rows % tr == 0 and row_off % tr == 0

    def body(p_ref, w_ref, m_ref, v_ref, *rest):
        g = p_ref[0].astype(F32)
        for s in range(1, part_block[0]):
            g = g + p_ref[s].astype(F32)
        _adam_update(g, w_ref, m_ref, v_ref, rest[-4:])

    ob = row_off // tr
    row = pl.BlockSpec((tr, C), lambda i: (ob + i, 0))
    out = jax.ShapeDtypeStruct((R, C), F32)
    chained = prev is not None
    return _pcall(
        body, name=name, ins=[parts, w, m, v] + (list(prev) if chained else []), dep=dep, grid=(n_rows // tr,),
        in_specs=[pl.BlockSpec(part_block, part_index), row, row, row] + ([ANY] * 4 if chained else []),
        out_specs=[row, row, row, row], out_shape=[out, out, out, out],
        input_output_aliases={4 + t: t for t in range(4)} if chained else {}, compiler_params=_params(1))


def _sum_parts(name, parts, dep=None):
    def body(p_ref, out_ref):
        g = p_ref[0]
        for s in range(1, parts.shape[0]):
            g = g + p_ref[s]
        out_ref[...] = g

    return _pcall(body, name=name, ins=[parts], in_specs=[_whole(parts)], dep=dep,
                  out_shape=jax.ShapeDtypeStruct(parts.shape[1:], F32), compiler_params=_params(0))


def _place():
    return lax.axis_index("x"), lax.axis_index("y"), lax.axis_index("c")


def _slot(px, py, pc):
    return 4 * px + 2 * py + pc


def _peer(k, x, y, c):
    return x ^ ((k >> 2) & 1), y ^ ((k >> 1) & 1), c ^ (k & 1)


SEND_PEERS = {"exchange": tuple(range(1, NDEV)), "gather": (1, 2, 4, 6), "forward": (2, 4, 6),
              "broadcast": tuple(range(1, NDEV))}


def _n_sems(mode, n_lands):
    return n_lands * (len(SEND_PEERS[mode]) + (mode != "forward"))


def _send_copies(mode, src_refs, land_refs, send_sems, recv_sems):
    x, y, c = _place()
    me = _slot(x, y, c)
    peers = SEND_PEERS[mode]
    remote, local = [], []
    for i, k in enumerate(peers):
        peer = _peer(k, x, y, c)
        for a, land in enumerate(land_refs):
            if mode == "exchange":
                src, dst, to = src_refs[a].at[_slot(*peer)], land.at[me], peer
            elif mode in ("gather", "broadcast"):
                src, dst, to = src_refs[a], land.at[me], peer
            else:
                src = dst = land.at[_slot(*peer)]
                to = (x, y, 1 - c)
            s = a * len(peers) + i
            remote.append(pltpu.make_async_remote_copy(src_ref=src, dst_ref=dst, send_sem=send_sems.at[s],
                                                       recv_sem=recv_sems.at[s], device_id=to, device_id_type=MESH))
    if mode != "forward":
        for a, land in enumerate(land_refs):
            src = src_refs[a].at[me] if mode == "exchange" else src_refs[a]
            local.append(pltpu.make_async_copy(src, land.at[me], send_sems.at[len(land_refs) * len(peers) + a]))
    return remote, local


def _send_start_groups(name, groups, mode, collective_id=None, dep=None):
    sizes = [(len(s), len(l)) for s, l in groups]
    flat = [t for s, l in groups for t in (*s, *l)]
    n_in, ng = len(flat), len(groups)

    def body(*refs):
        sems, token, at = refs[n_in:n_in + 2 * ng], refs[-1], 0
        if collective_id is not None:
            x, y, c = _place()
            barrier = pltpu.get_barrier_semaphore()
            shake = (1,) if mode == "forward" else SEND_PEERS[mode]
            for k in shake:
                pl.semaphore_signal(barrier, inc=1, device_id=_peer(k, x, y, c), device_id_type=MESH)
            pl.semaphore_wait(barrier, len(shake))
        for gi, (ns, nl) in enumerate(sizes):
            remote, local = _send_copies(mode, refs[at:at + ns], refs[at + ns:at + ns + nl], sems[2 * gi],
                                         sems[2 * gi + 1])
            for cp in remote + local:
                cp.start()
            at += ns + nl
        token[...] = jnp.zeros_like(token)

    sem_shapes = [pltpu.SemaphoreType.DMA((_n_sems(mode, nl),)) for _, nl in sizes for _ in range(2)]
    if any(dep is t for t in flat):
        dep = None
    out = _pcall(
        body, name=name, ins=[pltpu.with_memory_space_constraint(t, pltpu.HBM) for t in flat],
        in_specs=[HBM] * n_in, dep=dep,
        out_shape=(*sem_shapes, *[pltpu.HBM(t.shape, t.dtype) for t in flat], jax.ShapeDtypeStruct((8, LANES), F32)),
        out_specs=(*[SEM] * (2 * ng), *[HBM] * n_in, pl.BlockSpec(memory_space=pltpu.VMEM)),
        input_output_aliases={i: 2 * ng + i for i in range(n_in)},
        compiler_params=pltpu.CompilerParams(has_side_effects=pltpu.SideEffectType.DATAFLOW_SIDE_EFFECTING,
                                             collective_id=collective_id))
    started, at = [], 2 * ng
    for gi, (ns, nl) in enumerate(sizes):
        started.append((out[-1], out[2 * gi], out[2 * gi + 1], list(out[at:at + ns]), list(out[at + ns:at + ns + nl])))
        at += ns + nl
    return started


def _send_start(name, srcs, lands, mode, collective_id=None, dep=None):
    if mode == "forward":
        collective_id = SIBLING_COLLECTIVE_ID
    return _send_start_groups(name, [(srcs, lands)], mode, collective_id=collective_id, dep=dep)[0]


def _send_wait(name, started, mode, dep=None):
    _, send_sems, recv_sems, srcs, lands = started
    n_src, n = len(srcs), len(lands)

    def body(*refs):
        src_refs, land_refs = refs[:n_src], refs[n_src:n_src + n]
        ssem, rsem = refs[n_src + n], refs[n_src + n + 1]
        remote, local = _send_copies(mode, src_refs, land_refs, ssem, rsem)
        for cp in remote:
            cp.wait_send()
            cp.wait_recv()
        for cp in local:
            cp.wait()

    thru = [pltpu.HBM(t.shape, t.dtype) for t in [*srcs, *lands]]
    out = _pcall(
        body, name=name, ins=[*srcs, *lands, send_sems, recv_sems], in_specs=[HBM] * (n_src + n) + [SEM, SEM], dep=dep,
        out_shape=tuple(thru), out_specs=tuple([HBM] * (n_src + n)),
        input_output_aliases={i: i for i in range(n_src + n)},
        compiler_params=pltpu.CompilerParams(has_side_effects=pltpu.SideEffectType.DATAFLOW_SIDE_EFFECTING))
    return list(out[n_src:])


def _landing(block):
    return lax.empty((NDEV, *block.shape), block.dtype)


def _rows128(t):
    flat = t.reshape(-1)
    n = flat.shape[0]
    rows = -(-n // (8 * LANES)) * 8
    return jnp.pad(flat, (0, rows * LANES - n)).reshape(rows, LANES)


def kernel(x, mix_norm_g, ffn_norm_g, a_w_in, a_ln_g, a_w_spatial, a_b_spatial, a_w_out, kv_norm_g, w_k, w_v, b_w_q, b_sinks, b_w_o, rel_bias, ffn_w1, ffn_w2, final_norm_g, loss_target, m_mix_norm_g, m_ffn_norm_g, m_a_w_in, m_a_ln_g, m_a_w_spatial, m_a_b_spatial, m_a_w_out, m_kv_norm_g, m_w_k, m_w_v, m_b_w_q, m_b_sinks, m_b_w_o, m_rel_bias, m_ffn_w1, m_ffn_w2, m_final_norm_g, v_mix_norm_g, v_ffn_norm_g, v_a_w_in, v_a_ln_g, v_a_w_spatial, v_a_b_spatial, v_a_w_out, v_kv_norm_g, v_w_k, v_w_v, v_b_w_q, v_b_sinks, v_b_w_o, v_rel_bias, v_ffn_w1, v_ffn_w2, v_final_norm_g):
    c = _config(x, a_w_in, a_w_out, w_k, b_w_q, b_w_o, ffn_w1, ffn_w2)
    S, D, LA, LB, LF = c.S, c.D, c.LA, c.LB, c.LF
    weights = dict(mix_norm_g=mix_norm_g, ffn_norm_g=ffn_norm_g, a_w_in=a_w_in, a_ln_g=a_ln_g, a_w_spatial=a_w_spatial,
                   a_b_spatial=a_b_spatial, a_w_out=a_w_out, kv_norm_g=kv_norm_g, w_k=w_k, w_v=w_v, b_w_q=b_w_q,
                   b_sinks=b_sinks, b_w_o=b_w_o, rel_bias=rel_bias, ffn_w1=ffn_w1, ffn_w2=ffn_w2,
                   final_norm_g=final_norm_g)
    m_in = dict(mix_norm_g=m_mix_norm_g, ffn_norm_g=m_ffn_norm_g, a_w_in=m_a_w_in, a_ln_g=m_a_ln_g,
                a_w_spatial=m_a_w_spatial, a_b_spatial=m_a_b_spatial, a_w_out=m_a_w_out, kv_norm_g=m_kv_norm_g,
                w_k=m_w_k, w_v=m_w_v, b_w_q=m_b_w_q, b_sinks=m_b_sinks, b_w_o=m_b_w_o, rel_bias=m_rel_bias,
                ffn_w1=m_ffn_w1, ffn_w2=m_ffn_w2, final_norm_g=m_final_norm_g)
    v_in = dict(mix_norm_g=v_mix_norm_g, ffn_norm_g=v_ffn_norm_g, a_w_in=v_a_w_in, a_ln_g=v_a_ln_g,
                a_w_spatial=v_a_w_spatial, a_b_spatial=v_a_b_spatial, a_w_out=v_a_w_out, kv_norm_g=v_kv_norm_g,
                w_k=v_w_k, w_v=v_w_v, b_w_q=v_b_w_q, b_sinks=v_b_sinks, b_w_o=v_b_w_o, rel_bias=v_rel_bias,
                ffn_w1=v_ffn_w1, ffn_w2=v_ffn_w2, final_norm_g=v_final_norm_g)
    names = list(weights)
    seq = _Seq()
    me = _slot(*_place())
    bf = lambda t: t.astype(BF16)

    tr = lambda t: bf(jnp.swapaxes(t, -1, -2))

    def start(tag, some):
        got = seq(_send_start_groups, f"weights_start_{tag}", [(grp, [_landing(t) for t in grp]) for grp in some],
                  "gather")
        seq.last = got[0][0]
        return got

    started = start("first", [[tr(a_w_in[0])[None], a_ln_g]])
    _, a_w_in_, a_w_out_, ffn_w1_, ffn_w2_, w_k_, w_v_, b_w_q_, b_w_o_ = lax.optimization_barrier(
        (started[0][0], a_w_in, a_w_out, ffn_w1, ffn_w2, w_k, w_v, b_w_q, b_w_o))
    groups = []
    for l in range(LA):
        groups += [[tr(a_w_in_[l])[None]], [bf(a_w_out_[l])], [tr(ffn_w1_[l])[None], bf(ffn_w2_[l])]]
    gb = 3 * LA
    for l in range(LB):
        extra = [bf(jnp.concatenate([w_k_, w_v_], axis=1))] if l == 0 else []
        groups += [extra + [bf(_slot_cols(b_w_q_[l])), bf(b_w_o_[l])], [tr(ffn_w1_[LA + l])[None], bf(ffn_w2_[LA + l])]]
    started += start("rest", groups[1:])
    forwarding = {}

    def forward(i):
        lands = seq(_send_wait, f"weights_wait{i}", started[i], "gather")
        forwarding[i] = seq(_send_start, f"weights_forward{i}", [], lands, "forward")

    def arrive(i):
        if i not in forwarding:
            forward(i)
        return seq(_send_wait, f"weights_arrive{i}", forwarding[i], "forward")

    causal = jnp.tril(jnp.ones((CHUNK, CHUNK), bool))
    wsp = jnp.where(causal[None, None], a_w_spatial, 0.0)
    wsp16 = wsp.astype(BF16)
    wsp16_t = jnp.swapaxes(wsp, -1, -2).astype(BF16)
    bsp_t = jnp.swapaxes(a_b_spatial, -1, -2)
    mix_g = mix_norm_g.reshape(-1, 1, D)
    ffn_g = ffn_norm_g.reshape(-1, 1, D)
    kv_g = kv_norm_g.reshape(1, D)
    fin_g = final_norm_g.reshape(1, D)
    onehot = _bucket_onehot()
    bias = _slot_bias(seq(_band_bias, rel_bias.T, onehot).reshape(N_HEADS, BLOCK, 2 * BLOCK))

    h = x.reshape(S, D)
    sav_a, sav_b, wts_a, wts_b = [], [], [], []
    for l in range(LA):
        got = arrive(3 * l)
        w_in = got[0]
        if l == 0:
            ln_g_full = jnp.transpose(got[1], (1, 0, 2)).reshape(LA, 1, c.AW)
        z, a, hn = seq(_a_in_fwd, c, f"a_in_fwd{l}", h, mix_g[l], w_in, 0)
        forward(3 * l + 1)
        gated = seq(_sgu_fwd, c, f"sgu_fwd{l}", a, ln_g_full[l], wsp16[l], bsp_t[l])
        (wout,) = arrive(3 * l + 1)
        if l > 0:
            forward(3 * l + 2)
        h1 = seq(_mm_res, c, f"a_out_fwd{l}", gated, wout, c.ar, 0, h)
        w1, rows = arrive(3 * l + 2)
        if l == LA - 1:
            forward(gb)
        p, h2, hnf = seq(_ffn_fwd, c, f"ffn_fwd{l}", h1, ffn_g[l], w1, 0, rows)
        sav_a.append((h, z, a, hn, gated, h1, p, hnf))
        wts_a.append((w_in, 0, w1, 0, rows, wout, 0))
        h = h2
    h_kv = h
    for l in range(LB):
        got = arrive(gb + 2 * l)
        if l == 0:
            wkv, got = got[0], got[1:]
        wq, wo = got[0], _slot_rows(got[1])
        if l == 0:
            kv, hkv, q, hn = seq(_rms2_mm_rows, c, "kv_q_fwd", h, [(kv_g, wkv), (mix_g[LA], wq)])
        else:
            q, hn = seq(_rms_mm_rows, c, f"q_fwd{l}", h, mix_g[LA + l], wq, c.qr, 0, c.DQ)
        forward(gb + 2 * l + 1)
        o, probs = seq(_attn_fwd, c, f"attn_fwd{l}", q, kv, bias, b_sinks[l])
        h1 = seq(_mm_res, c, f"o_fwd{l}", o, wo, c.orr, 0, h)
        w1, rows = arrive(gb + 2 * l + 1)
        if l + 1 < LB:
            forward(gb + 2 * l + 2)
        p, h2, hnf = seq(_ffn_fwd, c, f"ffn_fwd{LA + l}", h1, ffn_g[LA + l], w1, 0, rows)
        sav_b.append((h, q, hn, o, probs, h1, p, hnf))
        wts_b.append((wq, wo, w1, rows))
        h = h2
    dh, d_fin_g, loss_row = seq(_final_loss, c, h, fin_g, loss_target.reshape(S, D))

    results = {}
    in_flight, exchanges = [], []

    def update(k, parts, layer, col_blk=0):
        w = weights[k]
        rows_l, ncols = (w.shape[-2], w.shape[-1]) if w.ndim == 3 else w.shape
        flat = lambda t: t.reshape(-1, ncols)
        tr = min(256, rows_l)
        results[k] = seq(_adamw, f"adamw_{k}{layer}", parts, (NDEV, tr, ncols), lambda i: (0, i, col_blk),
                         flat(w), flat(m_in[k]), flat(v_in[k]), tr, row_off=layer * rows_l, n_rows=rows_l,
                         prev=results.get(k))

    def land(tag, entry):
        lands = seq(_send_wait, f"grads_wait_{tag}", entry[1], "exchange")
        for keys, parts in zip(entry[0], lands):
            for k, layer, col_blk in keys:
                update(k, parts, layer, col_blk)

    def send(tag, items):
        slabs = [t for _, t in items]
        own = [lax.empty(t.shape, t.dtype) for t in slabs]
        exchanges.append(tag)
        st = seq(_send_start, f"grads_start_{tag}", slabs, own, "exchange",
                 collective_id=SIBLING_COLLECTIVE_ID + len(exchanges))
        in_flight.append((tag, ([k for k, _ in items], st)))
        while len(in_flight) > EXCHANGE_LAG:
            land(*in_flight.pop(0))

    d_mix_g, d_ffn_g = [None] * LF, [None] * LF
    dkv_list, dbias_list, dsink_list = [], [], [None] * LB

    def ffn_bwd(lf, dh, h1, p, hnf, w1, w1_i, rows):
        da, dh1, d_ffn_g[lf], dhb = seq(_ffn_bwd_data, c, f"ffn_bwd_data{lf}", dh, p, w1, w1_i, rows, h1, ffn_g[lf])
        dw1, dw2 = seq(_ffn_bwd_w, c, f"ffn_bwd_w{lf}", hnf, da, p, dhb)
        send(f"ffn{lf}", [([("ffn_w1", lf, 0)], dw1), ([("ffn_w2", lf, 0)], dw2)])
        return dh1

    for l in reversed(range(LB)):
        h0, q, hn, o, probs, h1, p, hnf = sav_b[l]
        wq, wo, w1, rows = wts_b[l]
        dh1 = ffn_bwd(LA + l, dh, h1, p, hnf, w1, 0, rows)
        do, dwo = seq(_bwd_rows_data_w, c, f"o_bwd{l}", dh1, wo, o, unslot="rows")
        dq, dkv, dbias, dsink = seq(_attn_bwd, c, f"attn_bwd{l}", q, kv, do, probs)
        dsink_list[l] = dsink.reshape(Q_PER_KV, 2, KV_PAIRS).transpose(2, 1, 0).reshape(1, N_HEADS)
        dkv_list.append(dkv)
        dbias_list.append(dbias.reshape(N_HEADS, -1))
        dwq = seq(_wgrad_rows, c, f"q_bwd_w{l}", hn, [dq], D, c.DQ, unslot="cols")
        send(f"attn{l}", [([("b_w_o", l, 0)], dwo), ([("b_w_q", l, 0)], dwq)])
        if l > 0:
            dh, d_mix_g[LA + l] = seq(_bwd_rows_to_stream, c, f"q_bwd_data{l}", [dq], wq, c.qr, 0, c.DQ, h0,
                                      mix_g[LA + l], dh1)
    dwkv = seq(_wgrad_rows, c, "kv_bwd_w", hkv, dkv_list, D, 2 * c.DKV)
    send("kv", [([("w_k", 0, 0), ("w_v", 0, 1)], dwkv)])
    dh, d_mix_g[LA], d_kv_g = seq(_bwd_q_kv_to_stream, c, "q_kv_bwd_data", dq, wq, dkv_list, wkv, h_kv, mix_g[LA],
                                  kv_g, dh1)
    d_rel_t = seq(_band_bias_grad, dbias_list, onehot)
    d_rel_t = d_rel_t.reshape(KV_PAIRS, Q_PER_KV, 2, N_BUCKETS).swapaxes(1, 2).reshape(N_HEADS, N_BUCKETS)
    d_wsp, d_bsp, d_lng = [None] * LA, [None] * LA, [None] * LA
    for l in reversed(range(LA)):
        h0, z, a, hn, gated, h1, p, hnf = sav_a[l]
        w_in, in_i, w1, w1_i, rows, wout, wout_i = wts_a[l]
        dh1 = ffn_bwd(l, dh, h1, p, hnf, w1, w1_i, rows)
        dgated = seq(_bwd_rows_data, c, f"a_out_bwd_data{l}", dh1, wout, c.ar, wout_i)
        dwout = seq(_wgrad_rows, c, f"a_out_bwd_w{l}", gated, [dh1], c.AW, D)
        send(f"a_out{l}", [([("a_w_out", l, 0)], dwout)])
        dz, d_wsp[l], dbt, d_lng[l] = seq(_sgu_bwd, c, f"sgu_bwd{l}", a, z, dgated, ln_g_full[l], wsp16[l],
                                          wsp16_t[l], bsp_t[l])
        d_bsp[l] = dbt.T
        dwin = seq(_wgrad_cols, c, f"a_in_bwd_w{l}", hn, dz)
        send(f"a_in{l}", [([("a_w_in", l, 0)], dwin)])
        dh, d_mix_g[l] = seq(_bwd_cols_to_stream, c, f"a_in_bwd_data{l}", dz, w_in, in_i, h0, mix_g[l], dh1)
    grad_x = dh.reshape(1, S, D)

    small = {
        "mix_norm_g": jnp.concatenate(d_mix_g, axis=0), "ffn_norm_g": jnp.concatenate(d_ffn_g, axis=0),
        "a_w_spatial": jnp.stack(d_wsp), "a_b_spatial": jnp.stack(d_bsp), "kv_norm_g": d_kv_g,
        "b_sinks": jnp.concatenate(dsink_list, axis=0), "rel_bias": d_rel_t.T, "final_norm_g": d_fin_g,
    }
    small_names = list(small)
    packs = [_rows128(small[k]) for k in small_names] + [_rows128(jnp.concatenate(d_lng, axis=0)), _rows128(loss_row)]
    offs = [int(o) for o in np.cumsum([0] + [p.shape[0] for p in packs])]
    Rs = offs[-1] + (-offs[-1]) % (8 * NDEV)
    packed = jnp.concatenate(packs + [jnp.zeros((Rs - offs[-1], LANES), F32)], axis=0)
    slab = packed.reshape(NDEV, Rs // NDEV, LANES)
    st = seq(_send_start, "small_grads_start", [slab], [lax.empty(slab.shape, slab.dtype)], "exchange")
    while len(in_flight) > 1:
        land(*in_flight.pop(0))
    (parts,) = seq(_send_wait, "small_grads_wait", st, "exchange")
    mine = seq(_sum_parts, "small_grads_sum", parts)
    st = seq(_send_start, "small_sums_start", [mine], [_landing(mine)], "broadcast")
    while in_flight:
        land(*in_flight.pop(0))
    (sums,) = seq(_send_wait, "small_sums_wait", st, "broadcast")
    sums = sums.reshape(Rs, LANES)
    loss = sums[offs[-2], 0]

    grads, deltas, new_m, new_v = {}, {}, {}, {}

    def put(k, outs, shape):
        grads[k], deltas[k], new_m[k], new_v[k] = (t.reshape(shape) for t in outs)

    outs = seq(_adamw_packed, "adamw_small", sums, offs, *[[_rows128(d[k]) for k in small_names]
                                                          for d in (weights, m_in, v_in)])
    for n_, k in enumerate(small_names):
        shape = weights[k].shape
        size = int(np.prod(shape))
        put(k, [t.reshape(-1)[:size] for t in outs[4 * n_:4 * n_ + 4]], shape)
    lng_sum = sums[offs[-3]:offs[-2]].reshape(-1)[:LA * c.AW].reshape(LA, c.AW)
    lng_mine = lax.dynamic_slice_in_dim(lng_sum, me * c.ar, c.ar, axis=1)
    lng_parts = jnp.concatenate([lng_mine[None], jnp.zeros((NDEV - 1, LA, c.ar), F32)], axis=0)
    put("a_ln_g", seq(_adamw, "adamw_ln_g", lng_parts, (NDEV, LA, c.ar), lambda i: (0, 0, 0),
                      a_ln_g, m_in["a_ln_g"], v_in["a_ln_g"], LA), a_ln_g.shape)
    for k in ("a_w_in", "ffn_w1", "ffn_w2", "a_w_out", "b_w_o", "b_w_q", "w_k", "w_v"):
        put(k, results[k], weights[k].shape)

    return (loss, grad_x, *[grads[k] for k in names], *[deltas[k] for k in names],
            *[new_m[k] for k in names], *[new_v[k] for k in names])
```

```python
import numpy as np
import math
import jax
import jax.numpy as jnp
from jax import lax
from jax.experimental import pallas as pl
from jax.experimental.pallas import tpu as pltpu

F32 = jnp.float32
BF16 = jnp.bfloat16

NDEV = 8
EPS = 1e-6
CHUNK = 128
A_GROUPS = 8
N_HEADS = 16
N_KV_HEADS = 4
Q_PER_KV = N_HEADS // N_KV_HEADS
HEAD_DIM = 64
BLOCK = 128
N_BUCKETS = 32
MAX_DISTANCE = 128
ADAM_LR, ADAM_B1, ADAM_B2, ADAM_EPS, ADAM_WD, ADAM_STEP = 0.001, 0.9, 0.999, 1e-08, 0.01, 10
LANES = 128
VMEM_LIMIT = 56 * 1024 * 1024
INV_SQRT2 = 0.7071067811865476
INV_SQRT_2PI = 0.3989422804014327
EXCHANGE_LAG = 7
SIBLING_COLLECTIVE_ID = 0

HBM = pl.BlockSpec(memory_space=pltpu.HBM)
SMEM = pl.BlockSpec(memory_space=pltpu.SMEM)
ANY = pl.BlockSpec(memory_space=pl.ANY)
SEM = pl.BlockSpec(memory_space=pltpu.SEMAPHORE)
MESH = pl.DeviceIdType.MESH


def _params(n_grid):
    return pltpu.CompilerParams(dimension_semantics=("arbitrary",) * n_grid, vmem_limit_bytes=VMEM_LIMIT)


def _const(block, index_map):
    return pl.BlockSpec(block, index_map, pipeline_mode=pl.Buffered(1))


def _pcall(body, *, ins, in_specs, dep=None, **kw):
    n_in = len(ins)
    if dep is None or any(dep is t for t in ins):
        return pl.pallas_call(body, in_specs=list(in_specs), **kw)(*ins)

    def with_dep(*refs):
        body(*refs[:n_in], *refs[n_in + 1:])

    return pl.pallas_call(with_dep, in_specs=[*in_specs, ANY], **kw)(*ins, dep)


class _Seq:
    def __init__(self):
        self.last = None

    def __call__(self, fn, *args, **kw):
        out = fn(*args, dep=self.last, **kw)
        self.last = out[0] if isinstance(out, (tuple, list)) else out
        return out


def _rstd(h):
    return lax.rsqrt(jnp.mean(h * h, axis=-1, keepdims=True) + EPS)


def _rms_bwd(dhn, h, g, dres):
    r = _rstd(h)
    xh = h * r
    dg = jnp.sum(dhn * xh, axis=0, keepdims=True)
    dxh = dhn * g
    dx = r * (dxh - xh * jnp.mean(dxh * xh, axis=-1, keepdims=True))
    return dres + dx, dg


def _gelu(z):
    return 0.5 * z * (1.0 + lax.erf(z * INV_SQRT2))


def _gelu_grad(z):
    return 0.5 * (1.0 + lax.erf(z * INV_SQRT2)) + z * (jnp.exp(-0.5 * z * z) * INV_SQRT_2PI)


def _dot(a, b, dims):
    return lax.dot_general(a, b, (dims, ((), ())), preferred_element_type=F32)


NN = ((1,), (0,))
NT = ((1,), (1,))
TN = ((0,), (0,))


def _mm(name, ins, in_specs, out_shapes, out_specs, *, grid, dims, nk, acc_shape, load_a, load_b, epilogue,
        dep=None):
    n_in, n_out = len(ins), len(out_shapes)
    kax = len(grid) - 1

    def body(*refs):
        in_refs = refs[:n_in]
        out_refs = refs[n_in:n_in + n_out]
        a = load_a(in_refs, out_refs)
        b = load_b(in_refs)
        prod = _dot(a, b, dims)
        if nk == 1:
            epilogue(prod, in_refs, out_refs)
        else:
            acc = refs[n_in + n_out]
            k = pl.program_id(kax)

            @pl.when(k == 0)
            def _():
                acc[...] = prod

            @pl.when(k > 0)
            def _():
                acc[...] += prod

            @pl.when(k == nk - 1)
            def _():
                epilogue(acc[...], in_refs, out_refs)

    return _pcall(
        body, name=name, ins=ins, in_specs=in_specs, dep=dep, grid=grid, out_specs=out_specs, out_shape=out_shapes,
        scratch_shapes=[pltpu.VMEM(acc_shape, F32)] if nk > 1 else [], compiler_params=_params(len(grid)))


def _bf(ref_idx):
    return lambda in_refs, *_: in_refs[ref_idx][...].astype(BF16)


def _b_view(ref_idx, rows):
    def load(in_refs):
        b = in_refs[ref_idx][...]
        return b.reshape(rows, b.shape[-1])
    return load


class Cfg:
    pass


def _config(x, a_w_in, a_w_out, w_k, b_w_q, b_w_o, ffn_w1, ffn_w2):
    c = Cfg()
    c.S, c.D = x.shape[1], x.shape[2]
    c.LA, _, c.cw = a_w_in.shape
    c.AW2 = NDEV * c.cw
    c.AW = c.AW2 // 2
    c.gd = c.AW // A_GROUPS
    c.ar = a_w_out.shape[1]
    c.LF, _, c.fw = ffn_w1.shape
    c.fr = ffn_w2.shape[1]
    c.LB, c.qr, c.DQ = b_w_q.shape
    c.orr = b_w_o.shape[1]
    c.kr, c.DKV = w_k.shape
    c.tm = min(1024, c.S)
    c.tmw = min(512, c.S)
    c.tms = min(512, c.S)
    c.nb = c.S // BLOCK
    assert c.cw == c.fw == c.fr and c.AW == NDEV * c.ar and c.D == NDEV * c.qr == NDEV * c.kr
    assert c.DQ == NDEV * c.orr == N_HEADS * HEAD_DIM and c.DKV == N_KV_HEADS * HEAD_DIM
    assert c.S % c.tm == 0 and c.S % c.tmw == 0 and c.S % c.tms == 0 and c.tms % CHUNK == 0 and c.gd % LANES == 0
    assert c.LA >= 1 and c.LB >= 1 and c.LF == c.LA + c.LB
    return c


def _cached_rms(h_idx, g_idx, hn_out_idx, jax_axis=1):
    def load(in_refs, out_refs):
        hn_ref = out_refs[hn_out_idx]

        @pl.when(pl.program_id(jax_axis) == 0)
        def _():
            h = in_refs[h_idx][...]
            hn_ref[...] = (h * _rstd(h) * in_refs[g_idx][...]).astype(BF16)

        return hn_ref[...]
    return load


def _a_in_fwd(c, name, h, g, col, ci, dep=None):
    S, D, cw, tm = c.S, c.D, c.cw, c.tmw

    def body(h_ref, g_ref, w_ref, z_ref, a_ref, hn_ref):
        h = h_ref[...]
        hn = (h * _rstd(h) * g_ref[...]).astype(BF16)
        hn_ref[...] = hn
        for j in range(NDEV):
            cols = slice(j * cw, (j + 1) * cw)
            z = _dot(hn, w_ref[j], NT)
            z_ref[:, cols] = z.astype(BF16)
            a_ref[:, cols] = _gelu(z).astype(BF16)

    row = pl.BlockSpec((tm, D), lambda i: (i, 0))
    wide = pl.BlockSpec((tm, c.AW2), lambda i: (i, 0))
    return _pcall(
        body, name=name, ins=[h, g, col], dep=dep, grid=(S // tm,),
        in_specs=[row, pl.BlockSpec((1, D), lambda i: (0, 0)), _const((NDEV, None, cw, D), lambda i: (0, ci, 0, 0))],
        out_specs=[wide, wide, row],
        out_shape=[jax.ShapeDtypeStruct((S, c.AW2), BF16), jax.ShapeDtypeStruct((S, c.AW2), BF16),
                   jax.ShapeDtypeStruct((S, D), BF16)],
        compiler_params=_params(1))


def _rms_mm_rows(c, name, h, g, slab, blk_rows, blk_idx, n_out, dep=None):
    S, D, tm = c.S, c.D, c.tm

    def epilogue(acc, in_refs, out_refs):
        out_refs[0][...] = acc.astype(BF16)

    return _mm(
        name, [h, slab, g],
        [pl.BlockSpec((tm, D), lambda i, j, k: (i, 0)),
         pl.BlockSpec((NDEV, blk_rows, n_out), lambda i, j, k: (0, blk_idx, 0)),
         pl.BlockSpec((1, D), lambda i, j, k: (0, 0))],
        [jax.ShapeDtypeStruct((S, n_out), BF16), jax.ShapeDtypeStruct((S, D), BF16)],
        [pl.BlockSpec((tm, n_out), lambda i, j, k: (i, 0)), pl.BlockSpec((tm, D), lambda i, j, k: (i, 0))],
        grid=(S // tm, 1, 1), dims=NN, nk=1, acc_shape=None,
        load_a=_cached_rms(0, 2, 1), load_b=_b_view(1, NDEV * blk_rows), epilogue=epilogue, dep=dep)


def _rms2_mm_rows(c, name, h, projections, dep=None):
    S, D, tm = c.S, c.D, c.tm
    n_p = len(projections)

    def body(h_ref, *refs):
        ins, outs = refs[:2 * n_p], refs[2 * n_p:]
        h = h_ref[...]
        xh = h * _rstd(h)
        for k in range(n_p):
            hn = (xh * ins[2 * k][...]).astype(BF16)
            outs[2 * k + 1][...] = hn
            w = ins[2 * k + 1][...]
            outs[2 * k][...] = _dot(hn, w.reshape(D, w.shape[-1]), NN).astype(BF16)

    row = pl.BlockSpec((tm, D), lambda i: (i, 0))
    ins, in_specs, shapes, specs = [h], [row], [], []
    for g, slab in projections:
        n = slab.shape[-1]
        ins += [g, slab]
        in_specs += [pl.BlockSpec((1, D), lambda i: (0, 0)), pl.BlockSpec(slab.shape, lambda i: (0, 0, 0))]
        shapes += [jax.ShapeDtypeStruct((S, n), BF16), jax.ShapeDtypeStruct((S, D), BF16)]
        specs += [pl.BlockSpec((tm, n), lambda i: (i, 0)), row]
    return _pcall(body, name=name, ins=ins, in_specs=in_specs, dep=dep, grid=(S // tm,), out_specs=specs,
                  out_shape=shapes, compiler_params=_params(1))


def _mm_res(c, name, a, slab, blk_rows, blk_idx, res, dep=None):
    S, D, tm = c.S, c.D, c.tm
    K = NDEV * blk_rows

    def epilogue(acc, in_refs, out_refs):
        out_refs[0][...] = in_refs[2][...] + acc

    return _mm(
        name, [a, slab, res],
        [pl.BlockSpec((tm, K), lambda i, j, k: (i, 0)),
         pl.BlockSpec((NDEV, blk_rows, D), lambda i, j, k: (0, blk_idx, 0)),
         pl.BlockSpec((tm, D), lambda i, j, k: (i, 0))],
        [jax.ShapeDtypeStruct((S, D), F32)], [pl.BlockSpec((tm, D), lambda i, j, k: (i, 0))],
        grid=(S // tm, 1, 1), dims=NN, nk=1, acc_shape=None,
        load_a=_bf(0), load_b=_b_view(1, K), epilogue=epilogue, dep=dep)[0]


def _sgu_masks():
    ii = lax.broadcasted_iota(jnp.int32, (CHUNK, CHUNK), 0)
    jj = lax.broadcasted_iota(jnp.int32, (CHUNK, CHUNK), 1)
    return ii >= jj


def _sgu_fwd(c, name, a, ln_g, wc, b_t, dep=None):
    S, AW, gd, tm = c.S, c.AW, c.gd, c.tms

    def body(a_ref, lng_ref, wc_ref, bt_ref, out_ref):
        va = a_ref[:, AW:].astype(F32)
        xc = va - jnp.mean(va, axis=-1, keepdims=True)
        vn = (xc * lax.rsqrt(jnp.mean(xc * xc, axis=-1, keepdims=True) + EPS) * lng_ref[...]).astype(BF16)
        for ch in range(tm // CHUNK):
            rows = slice(ch * CHUNK, (ch + 1) * CHUNK)
            for g in range(A_GROUPS):
                cols = slice(g * gd, (g + 1) * gd)
                mixed = _dot(wc_ref[g], vn[rows, cols], NN) + bt_ref[:, g:g + 1]
                out_ref[rows, cols] = (a_ref[rows, cols].astype(F32) * mixed).astype(BF16)

    return _pcall(
        body, name=name, ins=[a, ln_g, wc, b_t], dep=dep, grid=(S // tm,),
        in_specs=[pl.BlockSpec((tm, 2 * AW), lambda i: (i, 0)), pl.BlockSpec((1, AW), lambda i: (0, 0)),
                  pl.BlockSpec((A_GROUPS, CHUNK, CHUNK), lambda i: (0, 0, 0)),
                  pl.BlockSpec((CHUNK, A_GROUPS), lambda i: (0, 0))],
        out_specs=pl.BlockSpec((tm, AW), lambda i: (i, 0)),
        out_shape=jax.ShapeDtypeStruct((S, AW), BF16), compiler_params=_params(1))


def _ffn_fwd(c, name, h, g, col, ci, rows, dep=None):
    S, D, fw, tm = c.S, c.D, c.fw, c.tmw
    F = NDEV * fw

    def body(h_ref, g_ref, w1_ref, w2_ref, p_ref, out_ref, hn_ref, r_ref):
        h = h_ref[...]
        hn = (h * _rstd(h) * g_ref[...]).astype(BF16)
        hn_ref[...] = hn
        for j in range(NDEV):
            cols = slice(j * fw, (j + 1) * fw)
            p = jnp.maximum(_dot(hn, w1_ref[j], NT), 0.0)
            p_ref[:, cols] = p.astype(BF16)
            r_ref[:, cols] = (p * p).astype(BF16)
        out_ref[...] = h + _dot(r_ref[...], w2_ref[...].reshape(F, D), NN)

    row = pl.BlockSpec((tm, D), lambda i: (i, 0))
    return _pcall(
        body, name=name, ins=[h, g, col, rows], dep=dep, grid=(S // tm,),
        in_specs=[row, pl.BlockSpec((1, D), lambda i: (0, 0)),
                  _const((NDEV, None, fw, D), lambda i: (0, ci, 0, 0)), _const((NDEV, c.fr, D), lambda i: (0, 0, 0))],
        out_specs=[pl.BlockSpec((tm, F), lambda i: (i, 0)), row, row],
        out_shape=[jax.ShapeDtypeStruct((S, F), BF16), jax.ShapeDtypeStruct((S, D), F32),
                   jax.ShapeDtypeStruct((S, D), BF16)],
        scratch_shapes=[pltpu.VMEM((tm, F), BF16)], compiler_params=_params(1))


def _bucket_table():
    qi = np.arange(BLOCK)[:, None]
    kj = np.arange(2 * BLOCK)[None, :]
    d = np.maximum(qi + BLOCK - kj, 0)
    max_exact = N_BUCKETS // 2
    ratio = np.log(np.maximum(d, 1).astype(np.float32) / np.float32(max_exact)) / np.float32(
        math.log(MAX_DISTANCE / max_exact))
    large = np.minimum(max_exact + (ratio.astype(np.float32) * np.float32(N_BUCKETS - max_exact)).astype(np.int32),
                       N_BUCKETS - 1)
    return np.where(d < max_exact, d, large).astype(np.int32)


def _bucket_onehot():
    b = jnp.asarray(_bucket_table().reshape(1, -1))
    return (b == lax.broadcasted_iota(jnp.int32, (N_BUCKETS, b.shape[1]), 0)).astype(F32)


def _whole(t):
    return pl.BlockSpec(t.shape, lambda: (0,) * t.ndim)


def _band_bias(rel_bias_t, onehot, dep=None):
    def body(r_ref, oh_ref, out_ref):
        out_ref[...] = lax.dot_general(r_ref[...], oh_ref[...], (NN, ((), ())), preferred_element_type=F32,
                                       precision=lax.Precision.HIGHEST)

    n = onehot.shape[1]
    return _pcall(body, name="band_bias", ins=[rel_bias_t, onehot], in_specs=[_whole(rel_bias_t), _whole(onehot)],
                  dep=dep, out_shape=jax.ShapeDtypeStruct((N_HEADS, n), F32), compiler_params=_params(0))


def _band_bias_grad(dbias_list, onehot, dep=None):
    n_in = len(dbias_list)

    def body(*refs):
        oh_ref, out_ref = refs[n_in], refs[n_in + 1]
        d = refs[0][...]
        for r in refs[1:n_in]:
            d = d + r[...]
        out_ref[...] = lax.dot_general(d, oh_ref[...], (NT, ((), ())), preferred_element_type=F32,
                                       precision=lax.Precision.HIGHEST)

    ins = [*dbias_list, onehot]
    return _pcall(body, name="band_bias_grad", ins=ins, in_specs=[_whole(t) for t in ins], dep=dep,
                  out_shape=jax.ShapeDtypeStruct((N_HEADS, N_BUCKETS), F32), compiler_params=_params(0))


KV_PAIRS = N_KV_HEADS // 2
PAIR_ROWS = 2 * Q_PER_KV * BLOCK
MASKED = float(np.finfo(np.float32).min) / 2


def _slot_cols(w):
    lead = w.shape[:-1]
    return w.reshape(*lead, KV_PAIRS, 2, Q_PER_KV, HEAD_DIM).swapaxes(-3, -2).reshape(*lead, N_HEADS * HEAD_DIM)


def _slot_rows(blocks):
    n = blocks.shape[-1]
    return blocks.reshape(KV_PAIRS, 2, Q_PER_KV, HEAD_DIM, n).swapaxes(1, 2).reshape(blocks.shape)


def _slot_bias(bias):
    qi = np.arange(BLOCK)[:, None]
    kj = np.arange(2 * BLOCK)[None, :]
    dist = qi + BLOCK - kj
    window = (dist >= 0) & (dist < BLOCK)
    b = bias.reshape(KV_PAIRS, 2, Q_PER_KV, BLOCK, 2 * BLOCK).swapaxes(1, 2).reshape(KV_PAIRS, PAIR_ROWS, 2 * BLOCK)
    tile = lambda mk: jnp.asarray(np.tile(mk, (2 * Q_PER_KV, 1)))[None]
    return jnp.stack([jnp.where(tile(window & (kj >= BLOCK)), b, MASKED), jnp.where(tile(window), b, MASKED)])


def _pair_kv(kvc_ref, kvp_ref, kvp, dkv):
    lanes = slice(kvp * LANES, (kvp + 1) * LANES)
    vlanes = slice(dkv + kvp * LANES, dkv + (kvp + 1) * LANES)
    k2 = jnp.concatenate([kvp_ref[:, lanes], kvc_ref[:, lanes]], axis=0)
    v2 = jnp.concatenate([kvp_ref[:, vlanes], kvc_ref[:, vlanes]], axis=0)
    return k2, v2


def _head_operand(ref, grp, par, low, scale=None):
    xg = ref[:, grp * LANES:(grp + 1) * LANES]
    if scale is not None:
        xg = xg * scale
    zero = jnp.zeros_like(xg)
    return jnp.where(low, xg, zero) if par == 0 else jnp.where(low, zero, xg)


def _head_probs(qh, k2, bias_rows, sink):
    s = _dot(qh, k2, NT) + bias_rows
    m = jnp.maximum(jnp.max(s, axis=-1, keepdims=True), sink)
    e = jnp.exp(s - m)
    es = jnp.exp(sink - m)
    inv = 1.0 / (jnp.sum(e, axis=-1, keepdims=True) + es)
    return e * inv, es * inv


def _attn_specs(c):
    dq, dkv2 = c.DQ, 2 * c.DKV
    return [pl.BlockSpec((BLOCK, dq), lambda n: (n, 0)),
            pl.BlockSpec((BLOCK, dkv2), lambda n: (n, 0)),
            pl.BlockSpec((BLOCK, dkv2), lambda n: (jnp.maximum(n - 1, 0), 0))]


def _bias_spec():
    return pl.BlockSpec((None, KV_PAIRS, PAIR_ROWS, 2 * BLOCK), lambda n: (jnp.minimum(n, 1), 0, 0, 0))


def _low_lanes():
    return lax.broadcasted_iota(jnp.int32, (BLOCK, LANES), 1) < HEAD_DIM


def _first_key():
    return lax.broadcasted_iota(jnp.int32, (BLOCK, 2 * BLOCK), 1) == 0


def _probs_spec():
    return pl.BlockSpec((None, KV_PAIRS, PAIR_ROWS, 2 * BLOCK), lambda n: (n, 0, 0, 0))


def _attn_fwd(c, name, q, kv, bias, sinks, dep=None):
    S, dq = c.S, c.DQ

    def body(q_ref, kvc_ref, kvp_ref, bias_ref, sink_ref, o_ref, probs_ref):
        low = _low_lanes()
        first = _first_key()
        for kvp in range(KV_PAIRS):
            k2, v2 = _pair_kv(kvc_ref, kvp_ref, kvp, c.DKV)
            for g in range(Q_PER_KV):
                grp = kvp * Q_PER_KV + g
                halves = []
                for par in range(2):
                    rows = slice((2 * g + par) * BLOCK, (2 * g + par + 1) * BLOCK)
                    qh = _head_operand(q_ref, grp, par, low, scale=HEAD_DIM ** -0.5)
                    p, ps = _head_probs(qh, k2, bias_ref[kvp, rows, :], sink_ref[(2 * kvp + par) * Q_PER_KV + g])
                    probs_ref[kvp, rows, :] = jnp.where(first, ps, p).astype(BF16)
                    halves.append(_dot(p.astype(BF16), v2, NN))
                o_ref[:, grp * LANES:(grp + 1) * LANES] = jnp.where(low, halves[0], halves[1]).astype(BF16)

    return _pcall(
        body, name=name, ins=[q, kv, kv, bias, sinks], dep=dep, grid=(c.nb,),
        in_specs=_attn_specs(c) + [_bias_spec(), SMEM],
        out_specs=[pl.BlockSpec((BLOCK, dq), lambda n: (n, 0)), _probs_spec()],
        out_shape=[jax.ShapeDtypeStruct((S, dq), BF16),
                   jax.ShapeDtypeStruct((c.nb, KV_PAIRS, PAIR_ROWS, 2 * BLOCK), BF16)],
        compiler_params=_params(1))


def _final_loss(c, h, g, target, dep=None):
    S, D, tm = c.S, c.D, c.tm

    def body(h_ref, g_ref, t_ref, dh_ref, dg_ref, loss_ref):
        i = pl.program_id(0)
        h = h_ref[...]
        gg = g_ref[...]
        r = _rstd(h)
        xh = h * r
        err = xh * gg - t_ref[...]
        lp = jnp.sum(jnp.sum(err * err, axis=1, keepdims=True), axis=0, keepdims=True) * (0.5 / D)
        dx, dg = _rms_bwd(err * (1.0 / D), h, gg, 0.0)
        dh_ref[...] = dx

        @pl.when(i == 0)
        def _():
            dg_ref[...] = dg
            loss_ref[...] = jnp.broadcast_to(lp, loss_ref.shape)

        @pl.when(i > 0)
        def _():
            dg_ref[...] += dg
            loss_ref[...] += jnp.broadcast_to(lp, loss_ref.shape)

    row = pl.BlockSpec((tm, D), lambda i: (i, 0))
    return _pcall(
        body, name="final_loss", ins=[h, g, target], dep=dep, grid=(S // tm,),
        in_specs=[row, pl.BlockSpec((1, D), lambda i: (0, 0)), row],
        out_specs=[row, pl.BlockSpec((1, D), lambda i: (0, 0)), pl.BlockSpec((1, LANES), lambda i: (0, 0))],
        out_shape=[jax.ShapeDtypeStruct((S, D), F32), jax.ShapeDtypeStruct((1, D), F32),
                   jax.ShapeDtypeStruct((1, LANES), F32)],
        compiler_params=_params(1))


def _rms_bwd_epilogue(h_idx, g_idx, res_idx):
    def epilogue(dhn, in_refs, out_refs):
        dh, dg = _rms_bwd(dhn, in_refs[h_idx][...], in_refs[g_idx][...], in_refs[res_idx][...])
        out_refs[0][...] = dh
        i = pl.program_id(0)

        @pl.when(i == 0)
        def _():
            out_refs[1][...] = dg

        @pl.when(i > 0)
        def _():
            out_refs[1][...] += dg
    return epilogue


def _stream_outs(c, tm):
    S, D = c.S, c.D
    return ([jax.ShapeDtypeStruct((S, D), F32), jax.ShapeDtypeStruct((1, D), F32)],
            [pl.BlockSpec((tm, D), lambda i, j, k: (i, 0)), pl.BlockSpec((1, D), lambda i, j, k: (0, 0))])


def _row_specs(c, tm):
    D = c.D
    return [pl.BlockSpec((tm, D), lambda i, j, k: (i, 0)), pl.BlockSpec((1, D), lambda i, j, k: (0, 0)),
            pl.BlockSpec((tm, D), lambda i, j, k: (i, 0))]


def _bwd_rows_to_stream(c, name, dy_list, slab, blk_rows, blk_idx, n_in_cols, h, g, dres, dep=None):
    S, D, tm = c.S, c.D, c.tm
    nd = len(dy_list)

    def load_a(in_refs, out_refs):
        a = in_refs[0][...]
        for r in in_refs[1:nd]:
            a = a + r[...]
        return a.astype(BF16)

    shapes, specs = _stream_outs(c, tm)
    return _mm(
        name, [*dy_list, slab, h, g, dres],
        [pl.BlockSpec((tm, n_in_cols), lambda i, j, k: (i, 0))] * nd
        + [pl.BlockSpec((NDEV, blk_rows, n_in_cols), lambda i, j, k: (0, blk_idx, 0))] + _row_specs(c, tm),
        shapes, specs, grid=(S // tm, 1, 1), dims=NT, nk=1, acc_shape=None,
        load_a=load_a, load_b=_b_view(nd, NDEV * blk_rows), epilogue=_rms_bwd_epilogue(nd + 1, nd + 2, nd + 3),
        dep=dep)


def _bwd_q_kv_to_stream(c, name, dq, wq, dkv_list, wkv, h, g_q, g_kv, dres, dep=None):
    S, D, tm = c.S, c.D, c.tm
    nkv = len(dkv_list)

    def body(dq_ref, wq_ref, *rest):
        dkv_refs = rest[:nkv]
        wkv_ref, h_ref, gq_ref, gkv_ref, dres_ref, out_ref, dgq_ref, dgkv_ref = rest[nkv:]
        i = pl.program_id(0)
        dhn_q = _dot(dq_ref[...], wq_ref[...].reshape(D, c.DQ), NT)
        dkv = dkv_refs[0][...]
        for r in dkv_refs[1:]:
            dkv = dkv + r[...]
        dhn_kv = _dot(dkv.astype(BF16), wkv_ref[...].reshape(D, 2 * c.DKV), NT)
        h = h_ref[...]
        r = _rstd(h)
        xh = h * r
        dxh = dhn_q * gq_ref[...] + dhn_kv * gkv_ref[...]
        out_ref[...] = dres_ref[...] + r * (dxh - xh * jnp.mean(dxh * xh, axis=-1, keepdims=True))
        dgq = jnp.sum(dhn_q * xh, axis=0, keepdims=True)
        dgkv = jnp.sum(dhn_kv * xh, axis=0, keepdims=True)

        @pl.when(i == 0)
        def _():
            dgq_ref[...] = dgq
            dgkv_ref[...] = dgkv

        @pl.when(i > 0)
        def _():
            dgq_ref[...] += dgq
            dgkv_ref[...] += dgkv

    row = pl.BlockSpec((tm, D), lambda i: (i, 0))
    gain = pl.BlockSpec((1, D), lambda i: (0, 0))
    return _pcall(
        body, name=name, ins=[dq, wq, *dkv_list, wkv, h, g_q, g_kv, dres], dep=dep, grid=(S // tm,),
        in_specs=[pl.BlockSpec((tm, c.DQ), lambda i: (i, 0)), pl.BlockSpec((NDEV, c.qr, c.DQ), lambda i: (0, 0, 0))]
        + [pl.BlockSpec((tm, 2 * c.DKV), lambda i: (i, 0))] * nkv
        + [pl.BlockSpec((NDEV, c.kr, 2 * c.DKV), lambda i: (0, 0, 0)), row, gain, gain, row],
        out_specs=[row, gain, gain],
        out_shape=[jax.ShapeDtypeStruct((S, D), F32), jax.ShapeDtypeStruct((1, D), F32),
                   jax.ShapeDtypeStruct((1, D), F32)],
        compiler_params=_params(1))


def _bwd_cols_to_stream(c, name, dy, col, ci, h, g, dres, dep=None):
    S, D, cw, tm = c.S, c.D, c.cw, c.tmw
    K = NDEV * cw
    shapes, specs = _stream_outs(c, tm)
    return _mm(
        name, [dy, col, h, g, dres],
        [pl.BlockSpec((tm, K), lambda i, j, k: (i, 0)),
         _const((NDEV, None, cw, D), lambda i, j, k: (0, ci, 0, 0))] + _row_specs(c, tm),
        shapes, specs, grid=(S // tm, 1, 1), dims=NN, nk=1, acc_shape=None,
        load_a=_bf(0), load_b=_b_view(1, K), epilogue=_rms_bwd_epilogue(2, 3, 4), dep=dep)


def _bwd_rows_data(c, name, dy, slab, blk_rows, blk_idx, dep=None):
    S, D, tm = c.S, c.D, c.tm
    K = NDEV * blk_rows

    def epilogue(acc, in_refs, out_refs):
        out_refs[0][...] = acc.astype(BF16)

    return _mm(
        name, [dy, slab],
        [pl.BlockSpec((tm, D), lambda i, j, k: (i, 0)),
         pl.BlockSpec((NDEV, blk_rows, D), lambda i, j, k: (0, blk_idx, 0))],
        [jax.ShapeDtypeStruct((S, K), BF16)], [pl.BlockSpec((tm, K), lambda i, j, k: (i, 0))],
        grid=(S // tm, 1, 1), dims=NT, nk=1, acc_shape=None,
        load_a=_bf(0), load_b=_b_view(1, K), epilogue=epilogue, dep=dep)[0]


def _owner_blocks(out, acc, blk_rows, n_b, unslot):
    if unslot is None:
        out[...] = acc.reshape(NDEV, blk_rows, n_b).astype(BF16)
        return
    per_blk = blk_rows // HEAD_DIM
    for h in range(N_HEADS):
        s = (h // (2 * Q_PER_KV)) * 2 * Q_PER_KV + (h % Q_PER_KV) * 2 + (h // Q_PER_KV) % 2
        if unslot == "rows":
            out[h // per_blk, (h % per_blk) * HEAD_DIM:(h % per_blk + 1) * HEAD_DIM, :] = (
                acc[s * HEAD_DIM:(s + 1) * HEAD_DIM, :].astype(BF16))
        else:
            out[:, :, h * HEAD_DIM:(h + 1) * HEAD_DIM] = (
                acc[:, s * HEAD_DIM:(s + 1) * HEAD_DIM].reshape(NDEV, blk_rows, HEAD_DIM).astype(BF16))


def _bwd_rows_data_w(c, name, dy, slab, x, unslot=None, dep=None):
    S, D, tm = c.S, c.D, c.tm
    blk_rows = slab.shape[1]
    K = NDEV * blk_rows
    nk = S // tm

    def body(dy_ref, w_ref, x_ref, dx_ref, dw_ref, acc):
        k = pl.program_id(0)
        dyb = dy_ref[...].astype(BF16)
        dx_ref[...] = _dot(dyb, w_ref[...].reshape(K, D), NT).astype(BF16)
        part = _dot(x_ref[...], dyb, TN)

        @pl.when(k == 0)
        def _():
            acc[...] = part

        @pl.when(k > 0)
        def _():
            acc[...] += part

        @pl.when(k == nk - 1)
        def _():
            _owner_blocks(dw_ref, acc[...], blk_rows, D, unslot)

    return _pcall(
        body, name=name, ins=[dy, slab, x], dep=dep, grid=(nk,),
        in_specs=[pl.BlockSpec((tm, D), lambda k: (k, 0)), pl.BlockSpec((NDEV, blk_rows, D), lambda k: (0, 0, 0)),
                  pl.BlockSpec((tm, K), lambda k: (k, 0))],
        out_specs=[pl.BlockSpec((tm, K), lambda k: (k, 0)), pl.BlockSpec((NDEV, blk_rows, D), lambda k: (0, 0, 0))],
        out_shape=[jax.ShapeDtypeStruct((S, K), BF16), jax.ShapeDtypeStruct((NDEV, blk_rows, D), BF16)],
        scratch_shapes=[pltpu.VMEM((K, D), F32)], compiler_params=_params(1))


def _wgrad_rows(c, name, a, b_list, n_a, n_b, unslot=None, dep=None):
    S, tm = c.S, c.tm
    nb_in = len(b_list)
    blk_rows = n_a // NDEV

    def load_b(in_refs):
        b = in_refs[1][...]
        for r in in_refs[2:1 + nb_in]:
            b = b + r[...]
        return b.astype(BF16)

    def epilogue(acc, in_refs, out_refs):
        _owner_blocks(out_refs[0], acc, blk_rows, n_b, unslot)

    return _mm(
        name, [a, *b_list],
        [pl.BlockSpec((tm, n_a), lambda i, j, k: (k, 0))] + [pl.BlockSpec((tm, n_b), lambda i, j, k: (k, 0))] * nb_in,
        [jax.ShapeDtypeStruct((NDEV, blk_rows, n_b), BF16)],
        [pl.BlockSpec((NDEV, blk_rows, n_b), lambda i, j, k: (0, 0, 0))],
        grid=(1, 1, S // tm), dims=TN, nk=S // tm, acc_shape=(n_a, n_b),
        load_a=_bf(0), load_b=load_b, epilogue=epilogue, dep=dep)[0]


def _wgrad_cols(c, name, a, b, dep=None):
    S, D, cw = c.S, c.D, c.cw

    def body(a_ref, b_ref, out_ref):
        out_ref[...] = _dot(a_ref[...], b_ref[...], TN).astype(BF16)

    return _pcall(
        body, name=name, ins=[a, b], dep=dep, grid=(NDEV,),
        in_specs=[_const((S, D), lambda j: (0, 0)), pl.BlockSpec((S, cw), lambda j: (0, j))],
        out_specs=pl.BlockSpec((None, D, cw), lambda j: (j, 0, 0)),
        out_shape=jax.ShapeDtypeStruct((NDEV, D, cw), BF16), compiler_params=_params(1))


def _ffn_bwd_data(c, name, dh, p, col, ci, rows, h, g, dep=None):
    S, D, fw, tm = c.S, c.D, c.fw, c.tmw
    F = NDEV * fw

    def body(dh_ref, p_ref, w1t_ref, w2_ref, h_ref, g_ref, da_ref, out_ref, dg_ref, dhb_ref):
        i = pl.program_id(0)
        dh = dh_ref[...]
        dhb = dh.astype(BF16)
        dhb_ref[...] = dhb
        for j in range(NDEV):
            cols = slice(j * fw, (j + 1) * fw)
            da_ref[:, cols] = (_dot(dhb, w2_ref[j], NT) * (2.0 * p_ref[:, cols].astype(F32))).astype(BF16)
        dx, dg = _rms_bwd(_dot(da_ref[...], w1t_ref[...].reshape(F, D), NN), h_ref[...], g_ref[...], dh)
        out_ref[...] = dx

        @pl.when(i == 0)
        def _():
            dg_ref[...] = dg

        @pl.when(i > 0)
        def _():
            dg_ref[...] += dg

    row = pl.BlockSpec((tm, D), lambda i: (i, 0))
    wide = pl.BlockSpec((tm, F), lambda i: (i, 0))
    return _pcall(
        body, name=name, ins=[dh, p, col, rows, h, g], dep=dep, grid=(S // tm,),
        in_specs=[row, wide, _const((NDEV, None, fw, D), lambda i: (0, ci, 0, 0)),
                  _const((NDEV, c.fr, D), lambda i: (0, 0, 0)),
                  row, pl.BlockSpec((1, D), lambda i: (0, 0))],
        out_specs=[wide, row, pl.BlockSpec((1, D), lambda i: (0, 0)), row],
        out_shape=[jax.ShapeDtypeStruct((S, F), BF16), jax.ShapeDtypeStruct((S, D), F32),
                   jax.ShapeDtypeStruct((1, D), F32), jax.ShapeDtypeStruct((S, D), BF16)],
        compiler_params=_params(1))


def _ffn_bwd_w(c, name, hn, da, p, dhb, dep=None):
    S, D, fw = c.S, c.D, c.fw

    def body(hn_ref, da_ref, p_ref, dhb_ref, dw1_ref, dw2_ref):
        dw1_ref[...] = _dot(hn_ref[...], da_ref[...], TN).astype(BF16)
        pf = p_ref[...].astype(F32)
        dw2_ref[...] = _dot((pf * pf).astype(BF16), dhb_ref[...], TN).astype(BF16)

    panel = pl.BlockSpec((S, fw), lambda j: (0, j))
    return _pcall(
        body, name=name, ins=[hn, da, p, dhb], dep=dep, grid=(NDEV,),
        in_specs=[_const((S, D), lambda j: (0, 0)), panel, panel, _const((S, D), lambda j: (0, 0))],
        out_specs=[pl.BlockSpec((None, D, fw), lambda j: (j, 0, 0)), pl.BlockSpec((None, c.fr, D), lambda j: (j, 0, 0))],
        out_shape=[jax.ShapeDtypeStruct((NDEV, D, fw), BF16), jax.ShapeDtypeStruct((NDEV, c.fr, D), BF16)],
        compiler_params=_params(1))


def _attn_bwd(c, name, q, kv, do, probs, dep=None):
    S, dq, dkv = c.S, c.DQ, c.DKV
    nb = c.nb
    scale = HEAD_DIM ** -0.5

    def body(q_ref, kvc_ref, kvp_ref, do_ref, probs_ref, dq_ref, dkv_ref, dbias_ref, dsink_ref, dsink_acc,
             ds_sc, p_sc, qm_sc, dom_sc):
        n = pl.program_id(0)

        @pl.when(n == 0)
        def _():
            dkv_ref[...] = jnp.zeros_like(dkv_ref)
            dbias_ref[...] = jnp.zeros_like(dbias_ref)
            dsink_acc[...] = jnp.zeros_like(dsink_acc)

        low = _low_lanes()
        first = _first_key()
        rows_c = pl.ds(pl.multiple_of(n * BLOCK, BLOCK), BLOCK)
        rows_p = pl.ds(pl.multiple_of(jnp.maximum(n - 1, 0) * BLOCK, BLOCK), BLOCK)
        for kvp in range(KV_PAIRS):
            k2, v2 = _pair_kv(kvc_ref, kvp_ref, kvp, dkv)
            for g in range(Q_PER_KV):
                grp = kvp * Q_PER_KV + g
                halves = []
                for par in range(2):
                    rows = slice((2 * g + par) * BLOCK, (2 * g + par + 1) * BLOCK)
                    qh = _head_operand(q_ref, grp, par, low, scale=scale)
                    doh = _head_operand(do_ref, grp, par, low)
                    saved = probs_ref[kvp, rows, :]
                    ps = saved[:, 0:1].astype(F32)
                    p16 = jnp.where(first, jnp.zeros_like(saved), saved)
                    p = p16.astype(F32)
                    dp = _dot(doh, v2, NT)
                    delta = jnp.sum(p * dp, axis=-1, keepdims=True)
                    ds = p * (dp - delta)
                    dbias_ref[kvp, rows, :] += ds
                    dsink_acc[rows, kvp:kvp + 1] += -(ps * delta)
                    ds16 = ds.astype(BF16)
                    halves.append(_dot(ds16, k2, NN) * scale)
                    ds_sc[rows, :] = ds16
                    p_sc[rows, :] = p16
                    qm_sc[rows, :] = qh
                    dom_sc[rows, :] = doh
                dq_ref[:, grp * LANES:(grp + 1) * LANES] = jnp.where(low, halves[0], halves[1]).astype(BF16)
            dk2 = _dot(ds_sc[...], qm_sc[...], TN)
            dv2 = _dot(p_sc[...], dom_sc[...], TN)
            lanes = slice(kvp * LANES, (kvp + 1) * LANES)
            vlanes = slice(dkv + kvp * LANES, dkv + (kvp + 1) * LANES)
            dkv_ref[rows_p, lanes] += dk2[:BLOCK]
            dkv_ref[rows_c, lanes] += dk2[BLOCK:]
            dkv_ref[rows_p, vlanes] += dv2[:BLOCK]
            dkv_ref[rows_c, vlanes] += dv2[BLOCK:]

        @pl.when(n == nb - 1)
        def _():
            dsink_ref[...] = jnp.sum(dsink_acc[...].reshape(2 * Q_PER_KV, BLOCK, KV_PAIRS), axis=1)

    return _pcall(
        body, name=name, ins=[q, kv, kv, do, probs], dep=dep, grid=(nb,),
        in_specs=_attn_specs(c) + [pl.BlockSpec((BLOCK, dq), lambda n: (n, 0)), _probs_spec()],
        out_specs=[pl.BlockSpec((BLOCK, dq), lambda n: (n, 0)), pl.BlockSpec((S, 2 * dkv), lambda n: (0, 0)),
                   pl.BlockSpec((KV_PAIRS, PAIR_ROWS, 2 * BLOCK), lambda n: (0, 0, 0)),
                   pl.BlockSpec((2 * Q_PER_KV, KV_PAIRS), lambda n: (0, 0))],
        out_shape=[jax.ShapeDtypeStruct((S, dq), BF16), jax.ShapeDtypeStruct((S, 2 * dkv), F32),
                   jax.ShapeDtypeStruct((KV_PAIRS, PAIR_ROWS, 2 * BLOCK), F32),
                   jax.ShapeDtypeStruct((2 * Q_PER_KV, KV_PAIRS), F32)],
        scratch_shapes=[pltpu.VMEM((PAIR_ROWS, KV_PAIRS), F32), pltpu.VMEM((PAIR_ROWS, 2 * BLOCK), BF16),
                        pltpu.VMEM((PAIR_ROWS, 2 * BLOCK), BF16), pltpu.VMEM((PAIR_ROWS, LANES), BF16),
                        pltpu.VMEM((PAIR_ROWS, LANES), BF16)],
        compiler_params=_params(1))


def _sgu_bwd(c, name, a, z, dgated, ln_g, wc, wc_t, b_t, dep=None):
    S, AW, gd, tm = c.S, c.AW, c.gd, c.tms

    def body(a_ref, z_ref, dg_ref, lng_ref, wc_ref, wct_ref, bt_ref, dz_ref, dws_ref, dbt_ref, dlng_ref, dvn_ref):
        i = pl.program_id(0)

        @pl.when(i == 0)
        def _():
            dws_ref[...] = jnp.zeros_like(dws_ref)
            dbt_ref[...] = jnp.zeros_like(dbt_ref)
            dlng_ref[...] = jnp.zeros_like(dlng_ref)

        lng = lng_ref[...]
        va = a_ref[:, AW:].astype(F32)
        xc = va - jnp.mean(va, axis=-1, keepdims=True)
        rstd = lax.rsqrt(jnp.mean(xc * xc, axis=-1, keepdims=True) + EPS)
        xh = xc * rstd
        vn = (xh * lng).astype(BF16)
        causal = _sgu_masks()
        for ch in range(tm // CHUNK):
            rows = slice(ch * CHUNK, (ch + 1) * CHUNK)
            for g in range(A_GROUPS):
                cols = slice(g * gd, (g + 1) * gd)
                blk = vn[rows, cols]
                mixed = _dot(wc_ref[g], blk, NN) + bt_ref[:, g:g + 1]
                dgb = dg_ref[rows, cols].astype(F32)
                dm = dgb * a_ref[rows, cols].astype(F32)
                dbt_ref[:, g:g + 1] += jnp.sum(dm, axis=1, keepdims=True)
                dm16 = dm.astype(BF16)
                dws_ref[g] += jnp.where(causal, _dot(dm16, blk, NT), 0.0)
                dvn_ref[rows, cols] = _dot(wct_ref[g], dm16, NN)
                dz_ref[rows, cols] = (dgb * mixed * _gelu_grad(z_ref[rows, cols].astype(F32))).astype(BF16)
        dvn = dvn_ref[...]
        dlng_ref[...] += jnp.sum(dvn * xh, axis=0, keepdims=True)
        dxh = dvn * lng
        dva = rstd * (dxh - jnp.mean(dxh, axis=-1, keepdims=True) - xh * jnp.mean(dxh * xh, axis=-1, keepdims=True))
        dz_ref[:, AW:] = (dva * _gelu_grad(z_ref[:, AW:].astype(F32))).astype(BF16)

    wide = pl.BlockSpec((tm, 2 * AW), lambda i: (i, 0))
    wsp = pl.BlockSpec((A_GROUPS, CHUNK, CHUNK), lambda i: (0, 0, 0))
    btsp = pl.BlockSpec((CHUNK, A_GROUPS), lambda i: (0, 0))
    return _pcall(
        body, name=name, ins=[a, z, dgated, ln_g, wc, wc_t, b_t], dep=dep, grid=(S // tm,),
        in_specs=[wide, wide, pl.BlockSpec((tm, AW), lambda i: (i, 0)), pl.BlockSpec((1, AW), lambda i: (0, 0)),
                  wsp, wsp, btsp],
        out_specs=[wide, wsp, btsp, pl.BlockSpec((1, AW), lambda i: (0, 0))],
        out_shape=[jax.ShapeDtypeStruct((S, 2 * AW), BF16), jax.ShapeDtypeStruct((A_GROUPS, CHUNK, CHUNK), F32),
                   jax.ShapeDtypeStruct((CHUNK, A_GROUPS), F32), jax.ShapeDtypeStruct((1, AW), F32)],
        scratch_shapes=[pltpu.VMEM((tm, AW), F32)], compiler_params=_params(1))


def _adam_update(g, w_ref, m_ref, v_ref, out_refs):
    g_ref, d_ref, nm_ref, nv_ref = out_refs
    bc1 = 1.0 - ADAM_B1 ** ADAM_STEP
    bc2 = 1.0 - ADAM_B2 ** ADAM_STEP
    nm = ADAM_B1 * m_ref[...] + (1.0 - ADAM_B1) * g
    nv = ADAM_B2 * v_ref[...] + (1.0 - ADAM_B2) * (g * g)
    g_ref[...] = g
    nm_ref[...] = nm
    nv_ref[...] = nv
    d_ref[...] = -ADAM_LR * ((nm * (1.0 / bc1)) / (jnp.sqrt(nv * (1.0 / bc2)) + ADAM_EPS) + ADAM_WD * w_ref[...])


def _adamw_packed(name, sums, offs, ws, ms, vs, dep=None):
    n = len(ws)

    def body(*refs):
        s_ref, w_refs, m_refs, v_refs = refs[0], refs[1:1 + n], refs[1 + n:1 + 2 * n], refs[1 + 2 * n:1 + 3 * n]
        outs = refs[1 + 3 * n:]
        for k in range(n):
            _adam_update(s_ref[offs[k]:offs[k + 1], :], w_refs[k], m_refs[k], v_refs[k], outs[4 * k:4 * k + 4])

    ins = [sums, *ws, *ms, *vs]
    shapes = [jax.ShapeDtypeStruct(w.shape, F32) for w in ws for _ in range(4)]
    return _pcall(body, name=name, ins=ins, in_specs=[_whole(t) for t in ins], dep=dep,
                  out_shape=shapes, out_specs=[_whole(t) for t in shapes], compiler_params=_params(0))


def _adamw(name, parts, part_block, part_index, w, m, v, tr, row_off=0, n_rows=None, prev=None, dep=None):
    R, C = w.shape
    n_rows = R if n_rows is None else n_rows
    assert n_rows % tr == 0 and row_off % tr == 0

    def body(p_ref, w_ref, m_ref, v_ref, *rest):
        g = p_ref[0].astype(F32)
        for s in range(1, part_block[0]):
            g = g + p_ref[s].astype(F32)
        _adam_update(g, w_ref, m_ref, v_ref, rest[-4:])

    ob = row_off // tr
    row = pl.BlockSpec((tr, C), lambda i: (ob + i, 0))
    out = jax.ShapeDtypeStruct((R, C), F32)
    chained = prev is not None
    return _pcall(
        body, name=name, ins=[parts, w, m, v] + (list(prev) if chained else []), dep=dep, grid=(n_rows // tr,),
        in_specs=[pl.BlockSpec(part_block, part_index), row, row, row] + ([ANY] * 4 if chained else []),
        out_specs=[row, row, row, row], out_shape=[out, out, out, out],
        input_output_aliases={4 + t: t for t in range(4)} if chained else {}, compiler_params=_params(1))


def _sum_parts(name, parts, dep=None):
    def body(p_ref, out_ref):
        g = p_ref[0]
        for s in range(1, parts.shape[0]):
            g = g + p_ref[s]
        out_ref[...] = g

    return _pcall(body, name=name, ins=[parts], in_specs=[_whole(parts)], dep=dep,
                  out_shape=jax.ShapeDtypeStruct(parts.shape[1:], F32), compiler_params=_params(0))


def _place():
    return lax.axis_index("x"), lax.axis_index("y"), lax.axis_index("c")


def _slot(px, py, pc):
    return 4 * px + 2 * py + pc


def _peer(k, x, y, c):
    return x ^ ((k >> 2) & 1), y ^ ((k >> 1) & 1), c ^ (k & 1)


SEND_PEERS = {"exchange": tuple(range(1, NDEV)), "gather": (1, 2, 4, 6), "forward": (2, 4, 6),
              "broadcast": tuple(range(1, NDEV))}


def _n_sems(mode, n_lands):
    return n_lands * (len(SEND_PEERS[mode]) + (mode != "forward"))


def _send_copies(mode, src_refs, land_refs, send_sems, recv_sems):
    x, y, c = _place()
    me = _slot(x, y, c)
    peers = SEND_PEERS[mode]
    remote, local = [], []
    for i, k in enumerate(peers):
        peer = _peer(k, x, y, c)
        for a, land in enumerate(land_refs):
            if mode == "exchange":
                src, dst, to = src_refs[a].at[_slot(*peer)], land.at[me], peer
            elif mode in ("gather", "broadcast"):
                src, dst, to = src_refs[a], land.at[me], peer
            else:
                src = dst = land.at[_slot(*peer)]
                to = (x, y, 1 - c)
            s = a * len(peers) + i
            remote.append(pltpu.make_async_remote_copy(src_ref=src, dst_ref=dst, send_sem=send_sems.at[s],
                                                       recv_sem=recv_sems.at[s], device_id=to, device_id_type=MESH))
    if mode != "forward":
        for a, land in enumerate(land_refs):
            src = src_refs[a].at[me] if mode == "exchange" else src_refs[a]
            local.append(pltpu.make_async_copy(src, land.at[me], send_sems.at[len(land_refs) * len(peers) + a]))
    return remote, local


def _send_start_groups(name, groups, mode, collective_id=None, dep=None):
    sizes = [(len(s), len(l)) for s, l in groups]
    flat = [t for s, l in groups for t in (*s, *l)]
    n_in, ng = len(flat), len(groups)

    def body(*refs):
        sems, token, at = refs[n_in:n_in + 2 * ng], refs[-1], 0
        if collective_id is not None:
            x, y, c = _place()
            barrier = pltpu.get_barrier_semaphore()
            shake = (1,) if mode == "forward" else SEND_PEERS[mode]
            for k in shake:
                pl.semaphore_signal(barrier, inc=1, device_id=_peer(k, x, y, c), device_id_type=MESH)
            pl.semaphore_wait(barrier, len(shake))
        for gi, (ns, nl) in enumerate(sizes):
            remote, local = _send_copies(mode, refs[at:at + ns], refs[at + ns:at + ns + nl], sems[2 * gi],
                                         sems[2 * gi + 1])
            for cp in remote + local:
                cp.start()
            at += ns + nl
        token[...] = jnp.zeros_like(token)

    sem_shapes = [pltpu.SemaphoreType.DMA((_n_sems(mode, nl),)) for _, nl in sizes for _ in range(2)]
    if any(dep is t for t in flat):
        dep = None
    out = _pcall(
        body, name=name, ins=[pltpu.with_memory_space_constraint(t, pltpu.HBM) for t in flat],
        in_specs=[HBM] * n_in, dep=dep,
        out_shape=(*sem_shapes, *[pltpu.HBM(t.shape, t.dtype) for t in flat], jax.ShapeDtypeStruct((8, LANES), F32)),
        out_specs=(*[SEM] * (2 * ng), *[HBM] * n_in, pl.BlockSpec(memory_space=pltpu.VMEM)),
        input_output_aliases={i: 2 * ng + i for i in range(n_in)},
        compiler_params=pltpu.CompilerParams(has_side_effects=pltpu.SideEffectType.DATAFLOW_SIDE_EFFECTING,
                                             collective_id=collective_id))
    started, at = [], 2 * ng
    for gi, (ns, nl) in enumerate(sizes):
        started.append((out[-1], out[2 * gi], out[2 * gi + 1], list(out[at:at + ns]), list(out[at + ns:at + ns + nl])))
        at += ns + nl
    return started


def _send_start(name, srcs, lands, mode, collective_id=None, dep=None):
    if mode == "forward":
        collective_id = SIBLING_COLLECTIVE_ID
    return _send_start_groups(name, [(srcs, lands)], mode, collective_id=collective_id, dep=dep)[0]


def _send_wait(name, started, mode, dep=None):
    _, send_sems, recv_sems, srcs, lands = started
    n_src, n = len(srcs), len(lands)

    def body(*refs):
        src_refs, land_refs = refs[:n_src], refs[n_src:n_src + n]
        ssem, rsem = refs[n_src + n], refs[n_src + n + 1]
        remote, local = _send_copies(mode, src_refs, land_refs, ssem, rsem)
        for cp in remote:
            cp.wait_send()
            cp.wait_recv()
        for cp in local:
            cp.wait()

    thru = [pltpu.HBM(t.shape, t.dtype) for t in [*srcs, *lands]]
    out = _pcall(
        body, name=name, ins=[*srcs, *lands, send_sems, recv_sems], in_specs=[HBM] * (n_src + n) + [SEM, SEM], dep=dep,
        out_shape=tuple(thru), out_specs=tuple([HBM] * (n_src + n)),
        input_output_aliases={i: i for i in range(n_src + n)},
        compiler_params=pltpu.CompilerParams(has_side_effects=pltpu.SideEffectType.DATAFLOW_SIDE_EFFECTING))
    return list(out[n_src:])


def _landing(block):
    return lax.empty((NDEV, *block.shape), block.dtype)


def _rows128(t):
    flat = t.reshape(-1)
    n = flat.shape[0]
    rows = -(-n // (8 * LANES)) * 8
    return jnp.pad(flat, (0, rows * LANES - n)).reshape(rows, LANES)


def kernel(x, mix_norm_g, ffn_norm_g, a_w_in, a_ln_g, a_w_spatial, a_b_spatial, a_w_out, kv_norm_g, w_k, w_v, b_w_q, b_sinks, b_w_o, rel_bias, ffn_w1, ffn_w2, final_norm_g, loss_target, m_mix_norm_g, m_ffn_norm_g, m_a_w_in, m_a_ln_g, m_a_w_spatial, m_a_b_spatial, m_a_w_out, m_kv_norm_g, m_w_k, m_w_v, m_b_w_q, m_b_sinks, m_b_w_o, m_rel_bias, m_ffn_w1, m_ffn_w2, m_final_norm_g, v_mix_norm_g, v_ffn_norm_g, v_a_w_in, v_a_ln_g, v_a_w_spatial, v_a_b_spatial, v_a_w_out, v_kv_norm_g, v_w_k, v_w_v, v_b_w_q, v_b_sinks, v_b_w_o, v_rel_bias, v_ffn_w1, v_ffn_w2, v_final_norm_g):
    c = _config(x, a_w_in, a_w_out, w_k, b_w_q, b_w_o, ffn_w1, ffn_w2)
    S, D, LA, LB, LF = c.S, c.D, c.LA, c.LB, c.LF
    weights = dict(mix_norm_g=mix_norm_g, ffn_norm_g=ffn_norm_g, a_w_in=a_w_in, a_ln_g=a_ln_g, a_w_spatial=a_w_spatial,
                   a_b_spatial=a_b_spatial, a_w_out=a_w_out, kv_norm_g=kv_norm_g, w_k=w_k, w_v=w_v, b_w_q=b_w_q,
                   b_sinks=b_sinks, b_w_o=b_w_o, rel_bias=rel_bias, ffn_w1=ffn_w1, ffn_w2=ffn_w2,
                   final_norm_g=final_norm_g)
    m_in = dict(mix_norm_g=m_mix_norm_g, ffn_norm_g=m_ffn_norm_g, a_w_in=m_a_w_in, a_ln_g=m_a_ln_g,
                a_w_spatial=m_a_w_spatial, a_b_spatial=m_a_b_spatial, a_w_out=m_a_w_out, kv_norm_g=m_kv_norm_g,
                w_k=m_w_k, w_v=m_w_v, b_w_q=m_b_w_q, b_sinks=m_b_sinks, b_w_o=m_b_w_o, rel_bias=m_rel_bias,
                ffn_w1=m_ffn_w1, ffn_w2=m_ffn_w2, final_norm_g=m_final_norm_g)
    v_in = dict(mix_norm_g=v_mix_norm_g, ffn_norm_g=v_ffn_norm_g, a_w_in=v_a_w_in, a_ln_g=v_a_ln_g,
                a_w_spatial=v_a_w_spatial, a_b_spatial=v_a_b_spatial, a_w_out=v_a_w_out, kv_norm_g=v_kv_norm_g,
                w_k=v_w_k, w_v=v_w_v, b_w_q=v_b_w_q, b_sinks=v_b_sinks, b_w_o=v_b_w_o, rel_bias=v_rel_bias,
                ffn_w1=v_ffn_w1, ffn_w2=v_ffn_w2, final_norm_g=v_final_norm_g)
    names = list(weights)
    seq = _Seq()
    me = _slot(*_place())
    bf = lambda t: t.astype(BF16)

    tr = lambda t: bf(jnp.swapaxes(t, -1, -2))

    def start(tag, some):
        got = seq(_send_start_groups, f"weights_start_{tag}", [(grp, [_landing(t) for t in grp]) for grp in some],
                  "gather")
        seq.last = got[0][0]
        return got

    started = start("first", [[tr(a_w_in[0])[None], a_ln_g]])
    _, a_w_in_, a_w_out_, ffn_w1_, ffn_w2_, w_k_, w_v_, b_w_q_, b_w_o_ = lax.optimization_barrier(
        (started[0][0], a_w_in, a_w_out, ffn_w1, ffn_w2, w_k, w_v, b_w_q, b_w_o))
    groups = []
    for l in range(LA):
        groups += [[tr(a_w_in_[l])[None]], [bf(a_w_out_[l])], [tr(ffn_w1_[l])[None], bf(ffn_w2_[l])]]
    gb = 3 * LA
    for l in range(LB):
        extra = [bf(jnp.concatenate([w_k_, w_v_], axis=1))] if l == 0 else []
        groups += [extra + [bf(_slot_cols(b_w_q_[l])), bf(b_w_o_[l])], [tr(ffn_w1_[LA + l])[None], bf(ffn_w2_[LA + l])]]
    started += start("rest", groups[1:])
    forwarding = {}

    def forward(i):
        lands = seq(_send_wait, f"weights_wait{i}", started[i], "gather")
        forwarding[i] = seq(_send_start, f"weights_forward{i}", [], lands, "forward")

    def arrive(i):
        if i not in forwarding:
            forward(i)
        return seq(_send_wait, f"weights_arrive{i}", forwarding[i], "forward")

    causal = jnp.tril(jnp.ones((CHUNK, CHUNK), bool))
    wsp = jnp.where(causal[None, None], a_w_spatial, 0.0)
    wsp16 = wsp.astype(BF16)
    wsp16_t = jnp.swapaxes(wsp, -1, -2).astype(BF16)
    bsp_t = jnp.swapaxes(a_b_spatial, -1, -2)
    mix_g = mix_norm_g.reshape(-1, 1, D)
    ffn_g = ffn_norm_g.reshape(-1, 1, D)
    kv_g = kv_norm_g.reshape(1, D)
    fin_g = final_norm_g.reshape(1, D)
    onehot = _bucket_onehot()
    bias = _slot_bias(seq(_band_bias, rel_bias.T, onehot).reshape(N_HEADS, BLOCK, 2 * BLOCK))

    h = x.reshape(S, D)
    sav_a, sav_b, wts_a, wts_b = [], [], [], []
    for l in range(LA):
        got = arrive(3 * l)
        w_in = got[0]
        if l == 0:
            ln_g_full = jnp.transpose(got[1], (1, 0, 2)).reshape(LA, 1, c.AW)
        z, a, hn = seq(_a_in_fwd, c, f"a_in_fwd{l}", h, mix_g[l], w_in, 0)
        forward(3 * l + 1)
        gated = seq(_sgu_fwd, c, f"sgu_fwd{l}", a, ln_g_full[l], wsp16[l], bsp_t[l])
        (wout,) = arrive(3 * l + 1)
        if l > 0:
            forward(3 * l + 2)
        h1 = seq(_mm_res, c, f"a_out_fwd{l}", gated, wout, c.ar, 0, h)
        w1, rows = arrive(3 * l + 2)
        if l == LA - 1:
            forward(gb)
        p, h2, hnf = seq(_ffn_fwd, c, f"ffn_fwd{l}", h1, ffn_g[l], w1, 0, rows)
        sav_a.append((h, z, a, hn, gated, h1, p, hnf))
        wts_a.append((w_in, 0, w1, 0, rows, wout, 0))
        h = h2
    h_kv = h
    for l in range(LB):
        got = arrive(gb + 2 * l)
        if l == 0:
            wkv, got = got[0], got[1:]
        wq, wo = got[0], _slot_rows(got[1])
        if l == 0:
            kv, hkv, q, hn = seq(_rms2_mm_rows, c, "kv_q_fwd", h, [(kv_g, wkv), (mix_g[LA], wq)])
        else:
            q, hn = seq(_rms_mm_rows, c, f"q_fwd{l}", h, mix_g[LA + l], wq, c.qr, 0, c.DQ)
        forward(gb + 2 * l + 1)
        o, probs = seq(_attn_fwd, c, f"attn_fwd{l}", q, kv, bias, b_sinks[l])
        h1 = seq(_mm_res, c, f"o_fwd{l}", o, wo, c.orr, 0, h)
        w1, rows = arrive(gb + 2 * l + 1)
        if l + 1 < LB:
            forward(gb + 2 * l + 2)
        p, h2, hnf = seq(_ffn_fwd, c, f"ffn_fwd{LA + l}", h1, ffn_g[LA + l], w1, 0, rows)
        sav_b.append((h, q, hn, o, probs, h1, p, hnf))
        wts_b.append((wq, wo, w1, rows))
        h = h2
    dh, d_fin_g, loss_row = seq(_final_loss, c, h, fin_g, loss_target.reshape(S, D))

    results = {}
    in_flight, exchanges = [], []

    def update(k, parts, layer, col_blk=0):
        w = weights[k]
        rows_l, ncols = (w.shape[-2], w.shape[-1]) if w.ndim == 3 else w.shape
        flat = lambda t: t.reshape(-1, ncols)
        tr = min(256, rows_l)
        results[k] = seq(_adamw, f"adamw_{k}{layer}", parts, (NDEV, tr, ncols), lambda i: (0, i, col_blk),
                         flat(w), flat(m_in[k]), flat(v_in[k]), tr, row_off=layer * rows_l, n_rows=rows_l,
                         prev=results.get(k))

    def land(tag, entry):
        lands = seq(_send_wait, f"grads_wait_{tag}", entry[1], "exchange")
        for keys, parts in zip(entry[0], lands):
            for k, layer, col_blk in keys:
                update(k, parts, layer, col_blk)

    def send(tag, items):
        slabs = [t for _, t in items]
        own = [lax.empty(t.shape, t.dtype) for t in slabs]
        exchanges.append(tag)
        st = seq(_send_start, f"grads_start_{tag}", slabs, own, "exchange",
                 collective_id=SIBLING_COLLECTIVE_ID + len(exchanges))
        in_flight.append((tag, ([k for k, _ in items], st)))
        while len(in_flight) > EXCHANGE_LAG:
            land(*in_flight.pop(0))

    d_mix_g, d_ffn_g = [None] * LF, [None] * LF
    dkv_list, dbias_list, dsink_list = [], [], [None] * LB

    def ffn_bwd(lf, dh, h1, p, hnf, w1, w1_i, rows):
        da, dh1, d_ffn_g[lf], dhb = seq(_ffn_bwd_data, c, f"ffn_bwd_data{lf}", dh, p, w1, w1_i, rows, h1, ffn_g[lf])
        dw1, dw2 = seq(_ffn_bwd_w, c, f"ffn_bwd_w{lf}", hnf, da, p, dhb)
        send(f"ffn{lf}", [([("ffn_w1", lf, 0)], dw1), ([("ffn_w2", lf, 0)], dw2)])
        return dh1

    for l in reversed(range(LB)):
        h0, q, hn, o, probs, h1, p, hnf = sav_b[l]
        wq, wo, w1, rows = wts_b[l]
        dh1 = ffn_bwd(LA + l, dh, h1, p, hnf, w1, 0, rows)
        do, dwo = seq(_bwd_rows_data_w, c, f"o_bwd{l}", dh1, wo, o, unslot="rows")
        dq, dkv, dbias, dsink = seq(_attn_bwd, c, f"attn_bwd{l}", q, kv, do, probs)
        dsink_list[l] = dsink.reshape(Q_PER_KV, 2, KV_PAIRS).transpose(2, 1, 0).reshape(1, N_HEADS)
        dkv_list.append(dkv)
        dbias_list.append(dbias.reshape(N_HEADS, -1))
        dwq = seq(_wgrad_rows, c, f"q_bwd_w{l}", hn, [dq], D, c.DQ, unslot="cols")
        send(f"attn{l}", [([("b_w_o", l, 0)], dwo), ([("b_w_q", l, 0)], dwq)])
        if l > 0:
            dh, d_mix_g[LA + l] = seq(_bwd_rows_to_stream, c, f"q_bwd_data{l}", [dq], wq, c.qr, 0, c.DQ, h0,
                                      mix_g[LA + l], dh1)
    dwkv = seq(_wgrad_rows, c, "kv_bwd_w", hkv, dkv_list, D, 2 * c.DKV)
    send("kv", [([("w_k", 0, 0), ("w_v", 0, 1)], dwkv)])
    dh, d_mix_g[LA], d_kv_g = seq(_bwd_q_kv_to_stream, c, "q_kv_bwd_data", dq, wq, dkv_list, wkv, h_kv, mix_g[LA],
                                  kv_g, dh1)
    d_rel_t = seq(_band_bias_grad, dbias_list, onehot)
    d_rel_t = d_rel_t.reshape(KV_PAIRS, Q_PER_KV, 2, N_BUCKETS).swapaxes(1, 2).reshape(N_HEADS, N_BUCKETS)
    d_wsp, d_bsp, d_lng = [None] * LA, [None] * LA, [None] * LA
    for l in reversed(range(LA)):
        h0, z, a, hn, gated, h1, p, hnf = sav_a[l]
        w_in, in_i, w1, w1_i, rows, wout, wout_i = wts_a[l]
        dh1 = ffn_bwd(l, dh, h1, p, hnf, w1, w1_i, rows)
        dgated = seq(_bwd_rows_data, c, f"a_out_bwd_data{l}", dh1, wout, c.ar, wout_i)
        dwout = seq(_wgrad_rows, c, f"a_out_bwd_w{l}", gated, [dh1], c.AW, D)
        send(f"a_out{l}", [([("a_w_out", l, 0)], dwout)])
        dz, d_wsp[l], dbt, d_lng[l] = seq(_sgu_bwd, c, f"sgu_bwd{l}", a, z, dgated, ln_g_full[l], wsp16[l],
                                          wsp16_t[l], bsp_t[l])
        d_bsp[l] = dbt.T
        dwin = seq(_wgrad_cols, c, f"a_in_bwd_w{l}", hn, dz)
        send(f"a_in{l}", [([("a_w_in", l, 0)], dwin)])
        dh, d_mix_g[l] = seq(_bwd_cols_to_stream, c, f"a_in_bwd_data{l}", dz, w_in, in_i, h0, mix_g[l], dh1)
    grad_x = dh.reshape(1, S, D)

    small = {
        "mix_norm_g": jnp.concatenate(d_mix_g, axis=0), "ffn_norm_g": jnp.concatenate(d_ffn_g, axis=0),
        "a_w_spatial": jnp.stack(d_wsp), "a_b_spatial": jnp.stack(d_bsp), "kv_norm_g": d_kv_g,
        "b_sinks": jnp.concatenate(dsink_list, axis=0), "rel_bias": d_rel_t.T, "final_norm_g": d_fin_g,
    }
    small_names = list(small)
    packs = [_rows128(small[k]) for k in small_names] + [_rows128(jnp.concatenate(d_lng, axis=0)), _rows128(loss_row)]
    offs = [int(o) for o in np.cumsum([0] + [p.shape[0] for p in packs])]
    Rs = offs[-1] + (-offs[-1]) % (8 * NDEV)
    packed = jnp.concatenate(packs + [jnp.zeros((Rs - offs[-1], LANES), F32)], axis=0)
    slab = packed.reshape(NDEV, Rs // NDEV, LANES)
    st = seq(_send_start, "small_grads_start", [slab], [lax.empty(slab.shape, slab.dtype)], "exchange")
    while len(in_flight) > 2:
        land(*in_flight.pop(0))
    (parts,) = seq(_send_wait, "small_grads_wait", st, "exchange")
    mine = seq(_sum_parts, "small_grads_sum", parts)
    st = seq(_send_start, "small_sums_start", [mine], [_landing(mine)], "broadcast")
    while in_flight:
        land(*in_flight.pop(0))
    (sums,) = seq(_send_wait, "small_sums_wait", st, "broadcast")
    sums = sums.reshape(Rs, LANES)
    loss = sums[offs[-2], 0]

    grads, deltas, new_m, new_v = {}, {}, {}, {}

    def put(k, outs, shape):
        grads[k], deltas[k], new_m[k], new_v[k] = (t.reshape(shape) for t in outs)

    outs = seq(_adamw_packed, "adamw_small", sums, offs, *[[_rows128(d[k]) for k in small_names]
                                                          for d in (weights, m_in, v_in)])
    for n_, k in enumerate(small_names):
        shape = weights[k].shape
        size = int(np.prod(shape))
        put(k, [t.reshape(-1)[:size] for t in outs[4 * n_:4 * n_ + 4]], shape)
    lng_sum = sums[offs[-3]:offs[-2]].reshape(-1)[:LA * c.AW].reshape(LA, c.AW)
    lng_mine = lax.dynamic_slice_in_dim(lng_sum, me * c.ar, c.ar, axis=1)
    lng_parts = jnp.concatenate([lng_mine[None], jnp.zeros((NDEV - 1, LA, c.ar), F32)], axis=0)
    put("a_ln_g", seq(_adamw, "adamw_ln_g", lng_parts, (NDEV, LA, c.ar), lambda i: (0, 0, 0),
                      a_ln_g, m_in["a_ln_g"], v_in["a_ln_g"], LA), a_ln_g.shape)
    for k in ("a_w_in", "ffn_w1", "ffn_w2", "a_w_out", "b_w_o", "b_w_q", "w_k", "w_v"):
        put(k, results[k], weights[k].shape)

    return (loss, grad_x, *[grads[k] for k in names], *[deltas[k] for k in names],
            *[new_m[k] for k in names], *[new_v[k] for k in names])
```

```python
import numpy as np
import math
import jax
import jax.numpy as jnp
from jax import lax
from jax.experimental import pallas as pl
from jax.experimental.pallas import tpu as pltpu

F32 = jnp.float32
BF16 = jnp.bfloat16

NDEV = 8
EPS = 1e-6
CHUNK = 128
A_GROUPS = 8
N_HEADS = 16
N_KV_HEADS = 4
Q_PER_KV = N_HEADS // N_KV_HEADS
HEAD_DIM = 64
BLOCK = 128
N_BUCKETS = 32
MAX_DISTANCE = 128
ADAM_LR, ADAM_B1, ADAM_B2, ADAM_EPS, ADAM_WD, ADAM_STEP = 0.001, 0.9, 0.999, 1e-08, 0.01, 10
LANES = 128
VMEM_LIMIT = 56 * 1024 * 1024
INV_SQRT2 = 0.7071067811865476
INV_SQRT_2PI = 0.3989422804014327
EXCHANGE_LAG = 6
SIBLING_COLLECTIVE_ID = 0

HBM = pl.BlockSpec(memory_space=pltpu.HBM)
SMEM = pl.BlockSpec(memory_space=pltpu.SMEM)
ANY = pl.BlockSpec(memory_space=pl.ANY)
SEM = pl.BlockSpec(memory_space=pltpu.SEMAPHORE)
MESH = pl.DeviceIdType.MESH


def _params(n_grid):
    return pltpu.CompilerParams(dimension_semantics=("arbitrary",) * n_grid, vmem_limit_bytes=VMEM_LIMIT)


def _const(block, index_map):
    return pl.BlockSpec(block, index_map, pipeline_mode=pl.Buffered(1))


def _pcall(body, *, ins, in_specs, dep=None, **kw):
    n_in = len(ins)
    if dep is None or any(dep is t for t in ins):
        return pl.pallas_call(body, in_specs=list(in_specs), **kw)(*ins)

    def with_dep(*refs):
        body(*refs[:n_in], *refs[n_in + 1:])

    return pl.pallas_call(with_dep, in_specs=[*in_specs, ANY], **kw)(*ins, dep)


class _Seq:
    def __init__(self):
        self.last = None

    def __call__(self, fn, *args, **kw):
        out = fn(*args, dep=self.last, **kw)
        self.last = out[0] if isinstance(out, (tuple, list)) else out
        return out


def _rstd(h):
    return lax.rsqrt(jnp.mean(h * h, axis=-1, keepdims=True) + EPS)


def _rms_bwd(dhn, h, g, dres):
    r = _rstd(h)
    xh = h * r
    dg = jnp.sum(dhn * xh, axis=0, keepdims=True)
    dxh = dhn * g
    dx = r * (dxh - xh * jnp.mean(dxh * xh, axis=-1, keepdims=True))
    return dres + dx, dg


def _gelu(z):
    return 0.5 * z * (1.0 + lax.erf(z * INV_SQRT2))


def _gelu_grad(z):
    return 0.5 * (1.0 + lax.erf(z * INV_SQRT2)) + z * (jnp.exp(-0.5 * z * z) * INV_SQRT_2PI)


def _dot(a, b, dims):
    return lax.dot_general(a, b, (dims, ((), ())), preferred_element_type=F32)


NN = ((1,), (0,))
NT = ((1,), (1,))
TN = ((0,), (0,))


def _mm(name, ins, in_specs, out_shapes, out_specs, *, grid, dims, nk, acc_shape, load_a, load_b, epilogue,
        dep=None):
    n_in, n_out = len(ins), len(out_shapes)
    kax = len(grid) - 1

    def body(*refs):
        in_refs = refs[:n_in]
        out_refs = refs[n_in:n_in + n_out]
        a = load_a(in_refs, out_refs)
        b = load_b(in_refs)
        prod = _dot(a, b, dims)
        if nk == 1:
            epilogue(prod, in_refs, out_refs)
        else:
            acc = refs[n_in + n_out]
            k = pl.program_id(kax)

            @pl.when(k == 0)
            def _():
                acc[...] = prod

            @pl.when(k > 0)
            def _():
                acc[...] += prod

            @pl.when(k == nk - 1)
            def _():
                epilogue(acc[...], in_refs, out_refs)

    return _pcall(
        body, name=name, ins=ins, in_specs=in_specs, dep=dep, grid=grid, out_specs=out_specs, out_shape=out_shapes,
        scratch_shapes=[pltpu.VMEM(acc_shape, F32)] if nk > 1 else [], compiler_params=_params(len(grid)))


def _bf(ref_idx):
    return lambda in_refs, *_: in_refs[ref_idx][...].astype(BF16)


def _b_view(ref_idx, rows):
    def load(in_refs):
        b = in_refs[ref_idx][...]
        return b.reshape(rows, b.shape[-1])
    return load


class Cfg:
    pass


def _config(x, a_w_in, a_w_out, w_k, b_w_q, b_w_o, ffn_w1, ffn_w2):
    c = Cfg()
    c.S, c.D = x.shape[1], x.shape[2]
    c.LA, _, c.cw = a_w_in.shape
    c.AW2 = NDEV * c.cw
    c.AW = c.AW2 // 2
    c.gd = c.AW // A_GROUPS
    c.ar = a_w_out.shape[1]
    c.LF, _, c.fw = ffn_w1.shape
    c.fr = ffn_w2.shape[1]
    c.LB, c.qr, c.DQ = b_w_q.shape
    c.orr = b_w_o.shape[1]
    c.kr, c.DKV = w_k.shape
    c.tm = min(1024, c.S)
    c.tmw = min(512, c.S)
    c.tms = min(512, c.S)
    c.nb = c.S // BLOCK
    assert c.cw == c.fw == c.fr and c.AW == NDEV * c.ar and c.D == NDEV * c.qr == NDEV * c.kr
    assert c.DQ == NDEV * c.orr == N_HEADS * HEAD_DIM and c.DKV == N_KV_HEADS * HEAD_DIM
    assert c.S % c.tm == 0 and c.S % c.tmw == 0 and c.S % c.tms == 0 and c.tms % CHUNK == 0 and c.gd % LANES == 0
    assert c.LA >= 1 and c.LB >= 1 and c.LF == c.LA + c.LB
    return c


def _cached_rms(h_idx, g_idx, hn_out_idx, jax_axis=1):
    def load(in_refs, out_refs):
        hn_ref = out_refs[hn_out_idx]

        @pl.when(pl.program_id(jax_axis) == 0)
        def _():
            h = in_refs[h_idx][...]
            hn_ref[...] = (h * _rstd(h) * in_refs[g_idx][...]).astype(BF16)

        return hn_ref[...]
    return load


def _a_in_fwd(c, name, h, g, col, ci, dep=None):
    S, D, cw, tm = c.S, c.D, c.cw, c.tmw

    def body(h_ref, g_ref, w_ref, z_ref, a_ref, hn_ref):
        h = h_ref[...]
        hn = (h * _rstd(h) * g_ref[...]).astype(BF16)
        hn_ref[...] = hn
        for j in range(NDEV):
            cols = slice(j * cw, (j + 1) * cw)
            z = _dot(hn, w_ref[j], NT)
            z_ref[:, cols] = z.astype(BF16)
            a_ref[:, cols] = _gelu(z).astype(BF16)

    row = pl.BlockSpec((tm, D), lambda i: (i, 0))
    wide = pl.BlockSpec((tm, c.AW2), lambda i: (i, 0))
    return _pcall(
        body, name=name, ins=[h, g, col], dep=dep, grid=(S // tm,),
        in_specs=[row, pl.BlockSpec((1, D), lambda i: (0, 0)), _const((NDEV, None, cw, D), lambda i: (0, ci, 0, 0))],
        out_specs=[wide, wide, row],
        out_shape=[jax.ShapeDtypeStruct((S, c.AW2), BF16), jax.ShapeDtypeStruct((S, c.AW2), BF16),
                   jax.ShapeDtypeStruct((S, D), BF16)],
        compiler_params=_params(1))


def _rms_mm_rows(c, name, h, g, slab, blk_rows, blk_idx, n_out, dep=None):
    S, D, tm = c.S, c.D, c.tm

    def epilogue(acc, in_refs, out_refs):
        out_refs[0][...] = acc.astype(BF16)

    return _mm(
        name, [h, slab, g],
        [pl.BlockSpec((tm, D), lambda i, j, k: (i, 0)),
         pl.BlockSpec((NDEV, blk_rows, n_out), lambda i, j, k: (0, blk_idx, 0)),
         pl.BlockSpec((1, D), lambda i, j, k: (0, 0))],
        [jax.ShapeDtypeStruct((S, n_out), BF16), jax.ShapeDtypeStruct((S, D), BF16)],
        [pl.BlockSpec((tm, n_out), lambda i, j, k: (i, 0)), pl.BlockSpec((tm, D), lambda i, j, k: (i, 0))],
        grid=(S // tm, 1, 1), dims=NN, nk=1, acc_shape=None,
        load_a=_cached_rms(0, 2, 1), load_b=_b_view(1, NDEV * blk_rows), epilogue=epilogue, dep=dep)


def _rms2_mm_rows(c, name, h, projections, dep=None):
    S, D, tm = c.S, c.D, c.tm
    n_p = len(projections)

    def body(h_ref, *refs):
        ins, outs = refs[:2 * n_p], refs[2 * n_p:]
        h = h_ref[...]
        xh = h * _rstd(h)
        for k in range(n_p):
            hn = (xh * ins[2 * k][...]).astype(BF16)
            outs[2 * k + 1][...] = hn
            w = ins[2 * k + 1][...]
            outs[2 * k][...] = _dot(hn, w.reshape(D, w.shape[-1]), NN).astype(BF16)

    row = pl.BlockSpec((tm, D), lambda i: (i, 0))
    ins, in_specs, shapes, specs = [h], [row], [], []
    for g, slab in projections:
        n = slab.shape[-1]
        ins += [g, slab]
        in_specs += [pl.BlockSpec((1, D), lambda i: (0, 0)), pl.BlockSpec(slab.shape, lambda i: (0, 0, 0))]
        shapes += [jax.ShapeDtypeStruct((S, n), BF16), jax.ShapeDtypeStruct((S, D), BF16)]
        specs += [pl.BlockSpec((tm, n), lambda i: (i, 0)), row]
    return _pcall(body, name=name, ins=ins, in_specs=in_specs, dep=dep, grid=(S // tm,), out_specs=specs,
                  out_shape=shapes, compiler_params=_params(1))


def _mm_res(c, name, a, slab, blk_rows, blk_idx, res, dep=None):
    S, D, tm = c.S, c.D, c.tm
    K = NDEV * blk_rows

    def epilogue(acc, in_refs, out_refs):
        out_refs[0][...] = in_refs[2][...] + acc

    return _mm(
        name, [a, slab, res],
        [pl.BlockSpec((tm, K), lambda i, j, k: (i, 0)),
         pl.BlockSpec((NDEV, blk_rows, D), lambda i, j, k: (0, blk_idx, 0)),
         pl.BlockSpec((tm, D), lambda i, j, k: (i, 0))],
        [jax.ShapeDtypeStruct((S, D), F32)], [pl.BlockSpec((tm, D), lambda i, j, k: (i, 0))],
        grid=(S // tm, 1, 1), dims=NN, nk=1, acc_shape=None,
        load_a=_bf(0), load_b=_b_view(1, K), epilogue=epilogue, dep=dep)[0]


def _sgu_masks():
    ii = lax.broadcasted_iota(jnp.int32, (CHUNK, CHUNK), 0)
    jj = lax.broadcasted_iota(jnp.int32, (CHUNK, CHUNK), 1)
    return ii >= jj


def _sgu_fwd(c, name, a, ln_g, wc, b_t, dep=None):
    S, AW, gd, tm = c.S, c.AW, c.gd, c.tms

    def body(a_ref, lng_ref, wc_ref, bt_ref, out_ref):
        va = a_ref[:, AW:].astype(F32)
        xc = va - jnp.mean(va, axis=-1, keepdims=True)
        vn = (xc * lax.rsqrt(jnp.mean(xc * xc, axis=-1, keepdims=True) + EPS) * lng_ref[...]).astype(BF16)
        for ch in range(tm // CHUNK):
            rows = slice(ch * CHUNK, (ch + 1) * CHUNK)
            for g in range(A_GROUPS):
                cols = slice(g * gd, (g + 1) * gd)
                mixed = _dot(wc_ref[g], vn[rows, cols], NN) + bt_ref[:, g:g + 1]
                out_ref[rows, cols] = (a_ref[rows, cols].astype(F32) * mixed).astype(BF16)

    return _pcall(
        body, name=name, ins=[a, ln_g, wc, b_t], dep=dep, grid=(S // tm,),
        in_specs=[pl.BlockSpec((tm, 2 * AW), lambda i: (i, 0)), pl.BlockSpec((1, AW), lambda i: (0, 0)),
                  pl.BlockSpec((A_GROUPS, CHUNK, CHUNK), lambda i: (0, 0, 0)),
                  pl.BlockSpec((CHUNK, A_GROUPS), lambda i: (0, 0))],
        out_specs=pl.BlockSpec((tm, AW), lambda i: (i, 0)),
        out_shape=jax.ShapeDtypeStruct((S, AW), BF16), compiler_params=_params(1))


def _ffn_fwd(c, name, h, g, col, ci, rows, dep=None):
    S, D, fw, tm = c.S, c.D, c.fw, c.tmw
    F = NDEV * fw

    def body(h_ref, g_ref, w1_ref, w2_ref, p_ref, out_ref, hn_ref, r_ref):
        h = h_ref[...]
        hn = (h * _rstd(h) * g_ref[...]).astype(BF16)
        hn_ref[...] = hn
        for j in range(NDEV):
            cols = slice(j * fw, (j + 1) * fw)
            p = jnp.maximum(_dot(hn, w1_ref[j], NT), 0.0)
            p_ref[:, cols] = p.astype(BF16)
            r_ref[:, cols] = (p * p).astype(BF16)
        out_ref[...] = h + _dot(r_ref[...], w2_ref[...].reshape(F, D), NN)

    row = pl.BlockSpec((tm, D), lambda i: (i, 0))
    return _pcall(
        body, name=name, ins=[h, g, col, rows], dep=dep, grid=(S // tm,),
        in_specs=[row, pl.BlockSpec((1, D), lambda i: (0, 0)),
                  _const((NDEV, None, fw, D), lambda i: (0, ci, 0, 0)), _const((NDEV, c.fr, D), lambda i: (0, 0, 0))],
        out_specs=[pl.BlockSpec((tm, F), lambda i: (i, 0)), row, row],
        out_shape=[jax.ShapeDtypeStruct((S, F), BF16), jax.ShapeDtypeStruct((S, D), F32),
                   jax.ShapeDtypeStruct((S, D), BF16)],
        scratch_shapes=[pltpu.VMEM((tm, F), BF16)], compiler_params=_params(1))


def _bucket_table():
    qi = np.arange(BLOCK)[:, None]
    kj = np.arange(2 * BLOCK)[None, :]
    d = np.maximum(qi + BLOCK - kj, 0)
    max_exact = N_BUCKETS // 2
    ratio = np.log(np.maximum(d, 1).astype(np.float32) / np.float32(max_exact)) / np.float32(
        math.log(MAX_DISTANCE / max_exact))
    large = np.minimum(max_exact + (ratio.astype(np.float32) * np.float32(N_BUCKETS - max_exact)).astype(np.int32),
                       N_BUCKETS - 1)
    return np.where(d < max_exact, d, large).astype(np.int32)


def _bucket_onehot():
    b = jnp.asarray(_bucket_table().reshape(1, -1))
    return (b == lax.broadcasted_iota(jnp.int32, (N_BUCKETS, b.shape[1]), 0)).astype(F32)


def _whole(t):
    return pl.BlockSpec(t.shape, lambda: (0,) * t.ndim)


def _band_bias(rel_bias_t, onehot, dep=None):
    def body(r_ref, oh_ref, out_ref):
        out_ref[...] = lax.dot_general(r_ref[...], oh_ref[...], (NN, ((), ())), preferred_element_type=F32,
                                       precision=lax.Precision.HIGHEST)

    n = onehot.shape[1]
    return _pcall(body, name="band_bias", ins=[rel_bias_t, onehot], in_specs=[_whole(rel_bias_t), _whole(onehot)],
                  dep=dep, out_shape=jax.ShapeDtypeStruct((N_HEADS, n), F32), compiler_params=_params(0))


def _band_bias_grad(dbias_list, onehot, dep=None):
    n_in = len(dbias_list)

    def body(*refs):
        oh_ref, out_ref = refs[n_in], refs[n_in + 1]
        d = refs[0][...]
        for r in refs[1:n_in]:
            d = d + r[...]
        out_ref[...] = lax.dot_general(d, oh_ref[...], (NT, ((), ())), preferred_element_type=F32,
                                       precision=lax.Precision.HIGHEST)

    ins = [*dbias_list, onehot]
    return _pcall(body, name="band_bias_grad", ins=ins, in_specs=[_whole(t) for t in ins], dep=dep,
                  out_shape=jax.ShapeDtypeStruct((N_HEADS, N_BUCKETS), F32), compiler_params=_params(0))


KV_PAIRS = N_KV_HEADS // 2
PAIR_ROWS = 2 * Q_PER_KV * BLOCK
MASKED = float(np.finfo(np.float32).min) / 2


def _slot_cols(w):
    lead = w.shape[:-1]
    return w.reshape(*lead, KV_PAIRS, 2, Q_PER_KV, HEAD_DIM).swapaxes(-3, -2).reshape(*lead, N_HEADS * HEAD_DIM)


def _slot_rows(blocks):
    n = blocks.shape[-1]
    return blocks.reshape(KV_PAIRS, 2, Q_PER_KV, HEAD_DIM, n).swapaxes(1, 2).reshape(blocks.shape)


def _slot_bias(bias):
    qi = np.arange(BLOCK)[:, None]
    kj = np.arange(2 * BLOCK)[None, :]
    dist = qi + BLOCK - kj
    window = (dist >= 0) & (dist < BLOCK)
    b = bias.reshape(KV_PAIRS, 2, Q_PER_KV, BLOCK, 2 * BLOCK).swapaxes(1, 2).reshape(KV_PAIRS, PAIR_ROWS, 2 * BLOCK)
    tile = lambda mk: jnp.asarray(np.tile(mk, (2 * Q_PER_KV, 1)))[None]
    return jnp.stack([jnp.where(tile(window & (kj >= BLOCK)), b, MASKED), jnp.where(tile(window), b, MASKED)])


def _pair_kv(kvc_ref, kvp_ref, kvp, dkv):
    lanes = slice(kvp * LANES, (kvp + 1) * LANES)
    vlanes = slice(dkv + kvp * LANES, dkv + (kvp + 1) * LANES)
    k2 = jnp.concatenate([kvp_ref[:, lanes], kvc_ref[:, lanes]], axis=0)
    v2 = jnp.concatenate([kvp_ref[:, vlanes], kvc_ref[:, vlanes]], axis=0)
    return k2, v2


def _head_operand(ref, grp, par, low, scale=None):
    xg = ref[:, grp * LANES:(grp + 1) * LANES]
    if scale is not None:
        xg = xg * scale
    zero = jnp.zeros_like(xg)
    return jnp.where(low, xg, zero) if par == 0 else jnp.where(low, zero, xg)


def _head_probs(qh, k2, bias_rows, sink):
    s = _dot(qh, k2, NT) + bias_rows
    m = jnp.maximum(jnp.max(s, axis=-1, keepdims=True), sink)
    e = jnp.exp(s - m)
    es = jnp.exp(sink - m)
    inv = 1.0 / (jnp.sum(e, axis=-1, keepdims=True) + es)
    return e * inv, es * inv


def _attn_specs(c):
    dq, dkv2 = c.DQ, 2 * c.DKV
    return [pl.BlockSpec((BLOCK, dq), lambda n: (n, 0)),
            pl.BlockSpec((BLOCK, dkv2), lambda n: (n, 0)),
            pl.BlockSpec((BLOCK, dkv2), lambda n: (jnp.maximum(n - 1, 0), 0))]


def _bias_spec():
    return pl.BlockSpec((None, KV_PAIRS, PAIR_ROWS, 2 * BLOCK), lambda n: (jnp.minimum(n, 1), 0, 0, 0))


def _low_lanes():
    return lax.broadcasted_iota(jnp.int32, (BLOCK, LANES), 1) < HEAD_DIM


def _first_key():
    return lax.broadcasted_iota(jnp.int32, (BLOCK, 2 * BLOCK), 1) == 0


def _probs_spec():
    return pl.BlockSpec((None, KV_PAIRS, PAIR_ROWS, 2 * BLOCK), lambda n: (n, 0, 0, 0))


def _attn_fwd(c, name, q, kv, bias, sinks, dep=None):
    S, dq = c.S, c.DQ

    def body(q_ref, kvc_ref, kvp_ref, bias_ref, sink_ref, o_ref, probs_ref):
        low = _low_lanes()
        first = _first_key()
        for kvp in range(KV_PAIRS):
            k2, v2 = _pair_kv(kvc_ref, kvp_ref, kvp, c.DKV)
            for g in range(Q_PER_KV):
                grp = kvp * Q_PER_KV + g
                halves = []
                for par in range(2):
                    rows = slice((2 * g + par) * BLOCK, (2 * g + par + 1) * BLOCK)
                    qh = _head_operand(q_ref, grp, par, low, scale=HEAD_DIM ** -0.5)
                    p, ps = _head_probs(qh, k2, bias_ref[kvp, rows, :], sink_ref[(2 * kvp + par) * Q_PER_KV + g])
                    probs_ref[kvp, rows, :] = jnp.where(first, ps, p).astype(BF16)
                    halves.append(_dot(p.astype(BF16), v2, NN))
                o_ref[:, grp * LANES:(grp + 1) * LANES] = jnp.where(low, halves[0], halves[1]).astype(BF16)

    return _pcall(
        body, name=name, ins=[q, kv, kv, bias, sinks], dep=dep, grid=(c.nb,),
        in_specs=_attn_specs(c) + [_bias_spec(), SMEM],
        out_specs=[pl.BlockSpec((BLOCK, dq), lambda n: (n, 0)), _probs_spec()],
        out_shape=[jax.ShapeDtypeStruct((S, dq), BF16),
                   jax.ShapeDtypeStruct((c.nb, KV_PAIRS, PAIR_ROWS, 2 * BLOCK), BF16)],
        compiler_params=_params(1))


def _final_loss(c, h, g, target, dep=None):
    S, D, tm = c.S, c.D, c.tm

    def body(h_ref, g_ref, t_ref, dh_ref, dg_ref, loss_ref):
        i = pl.program_id(0)
        h = h_ref[...]
        gg = g_ref[...]
        r = _rstd(h)
        xh = h * r
        err = xh * gg - t_ref[...]
        lp = jnp.sum(jnp.sum(err * err, axis=1, keepdims=True), axis=0, keepdims=True) * (0.5 / D)
        dx, dg = _rms_bwd(err * (1.0 / D), h, gg, 0.0)
        dh_ref[...] = dx

        @pl.when(i == 0)
        def _():
            dg_ref[...] = dg
            loss_ref[...] = jnp.broadcast_to(lp, loss_ref.shape)

        @pl.when(i > 0)
        def _():
            dg_ref[...] += dg
            loss_ref[...] += jnp.broadcast_to(lp, loss_ref.shape)

    row = pl.BlockSpec((tm, D), lambda i: (i, 0))
    return _pcall(
        body, name="final_loss", ins=[h, g, target], dep=dep, grid=(S // tm,),
        in_specs=[row, pl.BlockSpec((1, D), lambda i: (0, 0)), row],
        out_specs=[row, pl.BlockSpec((1, D), lambda i: (0, 0)), pl.BlockSpec((1, LANES), lambda i: (0, 0))],
        out_shape=[jax.ShapeDtypeStruct((S, D), F32), jax.ShapeDtypeStruct((1, D), F32),
                   jax.ShapeDtypeStruct((1, LANES), F32)],
        compiler_params=_params(1))


def _rms_bwd_epilogue(h_idx, g_idx, res_idx):
    def epilogue(dhn, in_refs, out_refs):
        dh, dg = _rms_bwd(dhn, in_refs[h_idx][...], in_refs[g_idx][...], in_refs[res_idx][...])
        out_refs[0][...] = dh
        i = pl.program_id(0)

        @pl.when(i == 0)
        def _():
            out_refs[1][...] = dg

        @pl.when(i > 0)
        def _():
            out_refs[1][...] += dg
    return epilogue


def _stream_outs(c, tm):
    S, D = c.S, c.D
    return ([jax.ShapeDtypeStruct((S, D), F32), jax.ShapeDtypeStruct((1, D), F32)],
            [pl.BlockSpec((tm, D), lambda i, j, k: (i, 0)), pl.BlockSpec((1, D), lambda i, j, k: (0, 0))])


def _row_specs(c, tm):
    D = c.D
    return [pl.BlockSpec((tm, D), lambda i, j, k: (i, 0)), pl.BlockSpec((1, D), lambda i, j, k: (0, 0)),
            pl.BlockSpec((tm, D), lambda i, j, k: (i, 0))]


def _bwd_rows_to_stream(c, name, dy_list, slab, blk_rows, blk_idx, n_in_cols, h, g, dres, dep=None):
    S, D, tm = c.S, c.D, c.tm
    nd = len(dy_list)

    def load_a(in_refs, out_refs):
        a = in_refs[0][...]
        for r in in_refs[1:nd]:
            a = a + r[...]
        return a.astype(BF16)

    shapes, specs = _stream_outs(c, tm)
    return _mm(
        name, [*dy_list, slab, h, g, dres],
        [pl.BlockSpec((tm, n_in_cols), lambda i, j, k: (i, 0))] * nd
        + [pl.BlockSpec((NDEV, blk_rows, n_in_cols), lambda i, j, k: (0, blk_idx, 0))] + _row_specs(c, tm),
        shapes, specs, grid=(S // tm, 1, 1), dims=NT, nk=1, acc_shape=None,
        load_a=load_a, load_b=_b_view(nd, NDEV * blk_rows), epilogue=_rms_bwd_epilogue(nd + 1, nd + 2, nd + 3),
        dep=dep)


def _bwd_q_kv_to_stream(c, name, dq, wq, dkv_list, wkv, h, g_q, g_kv, dres, dep=None):
    S, D, tm = c.S, c.D, c.tm
    nkv = len(dkv_list)

    def body(dq_ref, wq_ref, *rest):
        dkv_refs = rest[:nkv]
        wkv_ref, h_ref, gq_ref, gkv_ref, dres_ref, out_ref, dgq_ref, dgkv_ref = rest[nkv:]
        i = pl.program_id(0)
        dhn_q = _dot(dq_ref[...], wq_ref[...].reshape(D, c.DQ), NT)
        dkv = dkv_refs[0][...]
        for r in dkv_refs[1:]:
            dkv = dkv + r[...]
        dhn_kv = _dot(dkv.astype(BF16), wkv_ref[...].reshape(D, 2 * c.DKV), NT)
        h = h_ref[...]
        r = _rstd(h)
        xh = h * r
        dxh = dhn_q * gq_ref[...] + dhn_kv * gkv_ref[...]
        out_ref[...] = dres_ref[...] + r * (dxh - xh * jnp.mean(dxh * xh, axis=-1, keepdims=True))
        dgq = jnp.sum(dhn_q * xh, axis=0, keepdims=True)
        dgkv = jnp.sum(dhn_kv * xh, axis=0, keepdims=True)

        @pl.when(i == 0)
        def _():
            dgq_ref[...] = dgq
            dgkv_ref[...] = dgkv

        @pl.when(i > 0)
        def _():
            dgq_ref[...] += dgq
            dgkv_ref[...] += dgkv

    row = pl.BlockSpec((tm, D), lambda i: (i, 0))
    gain = pl.BlockSpec((1, D), lambda i: (0, 0))
    return _pcall(
        body, name=name, ins=[dq, wq, *dkv_list, wkv, h, g_q, g_kv, dres], dep=dep, grid=(S // tm,),
        in_specs=[pl.BlockSpec((tm, c.DQ), lambda i: (i, 0)), pl.BlockSpec((NDEV, c.qr, c.DQ), lambda i: (0, 0, 0))]
        + [pl.BlockSpec((tm, 2 * c.DKV), lambda i: (i, 0))] * nkv
        + [pl.BlockSpec((NDEV, c.kr, 2 * c.DKV), lambda i: (0, 0, 0)), row, gain, gain, row],
        out_specs=[row, gain, gain],
        out_shape=[jax.ShapeDtypeStruct((S, D), F32), jax.ShapeDtypeStruct((1, D), F32),
                   jax.ShapeDtypeStruct((1, D), F32)],
        compiler_params=_params(1))


def _bwd_cols_to_stream(c, name, dy, col, ci, h, g, dres, dep=None):
    S, D, cw, tm = c.S, c.D, c.cw, c.tmw
    K = NDEV * cw
    shapes, specs = _stream_outs(c, tm)
    return _mm(
        name, [dy, col, h, g, dres],
        [pl.BlockSpec((tm, K), lambda i, j, k: (i, 0)),
         _const((NDEV, None, cw, D), lambda i, j, k: (0, ci, 0, 0))] + _row_specs(c, tm),
        shapes, specs, grid=(S // tm, 1, 1), dims=NN, nk=1, acc_shape=None,
        load_a=_bf(0), load_b=_b_view(1, K), epilogue=_rms_bwd_epilogue(2, 3, 4), dep=dep)


def _bwd_rows_data(c, name, dy, slab, blk_rows, blk_idx, dep=None):
    S, D, tm = c.S, c.D, c.tm
    K = NDEV * blk_rows

    def epilogue(acc, in_refs, out_refs):
        out_refs[0][...] = acc.astype(BF16)

    return _mm(
        name, [dy, slab],
        [pl.BlockSpec((tm, D), lambda i, j, k: (i, 0)),
         pl.BlockSpec((NDEV, blk_rows, D), lambda i, j, k: (0, blk_idx, 0))],
        [jax.ShapeDtypeStruct((S, K), BF16)], [pl.BlockSpec((tm, K), lambda i, j, k: (i, 0))],
        grid=(S // tm, 1, 1), dims=NT, nk=1, acc_shape=None,
        load_a=_bf(0), load_b=_b_view(1, K), epilogue=epilogue, dep=dep)[0]


def _owner_blocks(out, acc, blk_rows, n_b, unslot):
    if unslot is None:
        out[...] = acc.reshape(NDEV, blk_rows, n_b).astype(BF16)
        return
    per_blk = blk_rows // HEAD_DIM
    for h in range(N_HEADS):
        s = (h // (2 * Q_PER_KV)) * 2 * Q_PER_KV + (h % Q_PER_KV) * 2 + (h // Q_PER_KV) % 2
        if unslot == "rows":
            out[h // per_blk, (h % per_blk) * HEAD_DIM:(h % per_blk + 1) * HEAD_DIM, :] = (
                acc[s * HEAD_DIM:(s + 1) * HEAD_DIM, :].astype(BF16))
        else:
            out[:, :, h * HEAD_DIM:(h + 1) * HEAD_DIM] = (
                acc[:, s * HEAD_DIM:(s + 1) * HEAD_DIM].reshape(NDEV, blk_rows, HEAD_DIM).astype(BF16))


def _bwd_rows_data_w(c, name, dy, slab, x, unslot=None, dep=None):
    S, D, tm = c.S, c.D, c.tm
    blk_rows = slab.shape[1]
    K = NDEV * blk_rows
    nk = S // tm

    def body(dy_ref, w_ref, x_ref, dx_ref, dw_ref, acc):
        k = pl.program_id(0)
        dyb = dy_ref[...].astype(BF16)
        dx_ref[...] = _dot(dyb, w_ref[...].reshape(K, D), NT).astype(BF16)
        part = _dot(x_ref[...], dyb, TN)

        @pl.when(k == 0)
        def _():
            acc[...] = part

        @pl.when(k > 0)
        def _():
            acc[...] += part

        @pl.when(k == nk - 1)
        def _():
            _owner_blocks(dw_ref, acc[...], blk_rows, D, unslot)

    return _pcall(
        body, name=name, ins=[dy, slab, x], dep=dep, grid=(nk,),
        in_specs=[pl.BlockSpec((tm, D), lambda k: (k, 0)), pl.BlockSpec((NDEV, blk_rows, D), lambda k: (0, 0, 0)),
                  pl.BlockSpec((tm, K), lambda k: (k, 0))],
        out_specs=[pl.BlockSpec((tm, K), lambda k: (k, 0)), pl.BlockSpec((NDEV, blk_rows, D), lambda k: (0, 0, 0))],
        out_shape=[jax.ShapeDtypeStruct((S, K), BF16), jax.ShapeDtypeStruct((NDEV, blk_rows, D), BF16)],
        scratch_shapes=[pltpu.VMEM((K, D), F32)], compiler_params=_params(1))


def _wgrad_rows(c, name, a, b_list, n_a, n_b, unslot=None, dep=None):
    S, tm = c.S, c.tm
    nb_in = len(b_list)
    blk_rows = n_a // NDEV

    def load_b(in_refs):
        b = in_refs[1][...]
        for r in in_refs[2:1 + nb_in]:
            b = b + r[...]
        return b.astype(BF16)

    def epilogue(acc, in_refs, out_refs):
        _owner_blocks(out_refs[0], acc, blk_rows, n_b, unslot)

    return _mm(
        name, [a, *b_list],
        [pl.BlockSpec((tm, n_a), lambda i, j, k: (k, 0))] + [pl.BlockSpec((tm, n_b), lambda i, j, k: (k, 0))] * nb_in,
        [jax.ShapeDtypeStruct((NDEV, blk_rows, n_b), BF16)],
        [pl.BlockSpec((NDEV, blk_rows, n_b), lambda i, j, k: (0, 0, 0))],
        grid=(1, 1, S // tm), dims=TN, nk=S // tm, acc_shape=(n_a, n_b),
        load_a=_bf(0), load_b=load_b, epilogue=epilogue, dep=dep)[0]


def _wgrad_cols(c, name, a, b, dep=None):
    S, D, cw = c.S, c.D, c.cw

    def body(a_ref, b_ref, out_ref):
        out_ref[...] = _dot(a_ref[...], b_ref[...], TN).astype(BF16)

    return _pcall(
        body, name=name, ins=[a, b], dep=dep, grid=(NDEV,),
        in_specs=[_const((S, D), lambda j: (0, 0)), pl.BlockSpec((S, cw), lambda j: (0, j))],
        out_specs=pl.BlockSpec((None, D, cw), lambda j: (j, 0, 0)),
        out_shape=jax.ShapeDtypeStruct((NDEV, D, cw), BF16), compiler_params=_params(1))


def _ffn_bwd_data(c, name, dh, p, col, ci, rows, h, g, dep=None):
    S, D, fw, tm = c.S, c.D, c.fw, c.tmw
    F = NDEV * fw

    def body(dh_ref, p_ref, w1t_ref, w2_ref, h_ref, g_ref, da_ref, out_ref, dg_ref, dhb_ref):
        i = pl.program_id(0)
        dh = dh_ref[...]
        dhb = dh.astype(BF16)
        dhb_ref[...] = dhb
        for j in range(NDEV):
            cols = slice(j * fw, (j + 1) * fw)
            da_ref[:, cols] = (_dot(dhb, w2_ref[j], NT) * (2.0 * p_ref[:, cols].astype(F32))).astype(BF16)
        dx, dg = _rms_bwd(_dot(da_ref[...], w1t_ref[...].reshape(F, D), NN), h_ref[...], g_ref[...], dh)
        out_ref[...] = dx

        @pl.when(i == 0)
        def _():
            dg_ref[...] = dg

        @pl.when(i > 0)
        def _():
            dg_ref[...] += dg

    row = pl.BlockSpec((tm, D), lambda i: (i, 0))
    wide = pl.BlockSpec((tm, F), lambda i: (i, 0))
    return _pcall(
        body, name=name, ins=[dh, p, col, rows, h, g], dep=dep, grid=(S // tm,),
        in_specs=[row, wide, _const((NDEV, None, fw, D), lambda i: (0, ci, 0, 0)),
                  _const((NDEV, c.fr, D), lambda i: (0, 0, 0)),
                  row, pl.BlockSpec((1, D), lambda i: (0, 0))],
        out_specs=[wide, row, pl.BlockSpec((1, D), lambda i: (0, 0)), row],
        out_shape=[jax.ShapeDtypeStruct((S, F), BF16), jax.ShapeDtypeStruct((S, D), F32),
                   jax.ShapeDtypeStruct((1, D), F32), jax.ShapeDtypeStruct((S, D), BF16)],
        compiler_params=_params(1))


def _ffn_bwd_w(c, name, hn, da, p, dhb, dep=None):
    S, D, fw = c.S, c.D, c.fw

    def body(hn_ref, da_ref, p_ref, dhb_ref, dw1_ref, dw2_ref):
        dw1_ref[...] = _dot(hn_ref[...], da_ref[...], TN).astype(BF16)
        pf = p_ref[...].astype(F32)
        dw2_ref[...] = _dot((pf * pf).astype(BF16), dhb_ref[...], TN).astype(BF16)

    panel = pl.BlockSpec((S, fw), lambda j: (0, j))
    return _pcall(
        body, name=name, ins=[hn, da, p, dhb], dep=dep, grid=(NDEV,),
        in_specs=[_const((S, D), lambda j: (0, 0)), panel, panel, _const((S, D), lambda j: (0, 0))],
        out_specs=[pl.BlockSpec((None, D, fw), lambda j: (j, 0, 0)), pl.BlockSpec((None, c.fr, D), lambda j: (j, 0, 0))],
        out_shape=[jax.ShapeDtypeStruct((NDEV, D, fw), BF16), jax.ShapeDtypeStruct((NDEV, c.fr, D), BF16)],
        compiler_params=_params(1))


def _attn_bwd(c, name, q, kv, do, probs, dep=None):
    S, dq, dkv = c.S, c.DQ, c.DKV
    nb = c.nb
    scale = HEAD_DIM ** -0.5

    def body(q_ref, kvc_ref, kvp_ref, do_ref, probs_ref, dq_ref, dkv_ref, dbias_ref, dsink_ref, dsink_acc,
             ds_sc, p_sc, qm_sc, dom_sc):
        n = pl.program_id(0)

        @pl.when(n == 0)
        def _():
            dkv_ref[...] = jnp.zeros_like(dkv_ref)
            dbias_ref[...] = jnp.zeros_like(dbias_ref)
            dsink_acc[...] = jnp.zeros_like(dsink_acc)

        low = _low_lanes()
        first = _first_key()
        rows_c = pl.ds(pl.multiple_of(n * BLOCK, BLOCK), BLOCK)
        rows_p = pl.ds(pl.multiple_of(jnp.maximum(n - 1, 0) * BLOCK, BLOCK), BLOCK)
        for kvp in range(KV_PAIRS):
            k2, v2 = _pair_kv(kvc_ref, kvp_ref, kvp, dkv)
            for g in range(Q_PER_KV):
                grp = kvp * Q_PER_KV + g
                halves = []
                for par in range(2):
                    rows = slice((2 * g + par) * BLOCK, (2 * g + par + 1) * BLOCK)
                    qh = _head_operand(q_ref, grp, par, low, scale=scale)
                    doh = _head_operand(do_ref, grp, par, low)
                    saved = probs_ref[kvp, rows, :]
                    ps = saved[:, 0:1].astype(F32)
                    p16 = jnp.where(first, jnp.zeros_like(saved), saved)
                    p = p16.astype(F32)
                    dp = _dot(doh, v2, NT)
                    delta = jnp.sum(p * dp, axis=-1, keepdims=True)
                    ds = p * (dp - delta)
                    dbias_ref[kvp, rows, :] += ds
                    dsink_acc[rows, kvp:kvp + 1] += -(ps * delta)
                    ds16 = ds.astype(BF16)
                    halves.append(_dot(ds16, k2, NN) * scale)
                    ds_sc[rows, :] = ds16
                    p_sc[rows, :] = p16
                    qm_sc[rows, :] = qh
                    dom_sc[rows, :] = doh
                dq_ref[:, grp * LANES:(grp + 1) * LANES] = jnp.where(low, halves[0], halves[1]).astype(BF16)
            dk2 = _dot(ds_sc[...], qm_sc[...], TN)
            dv2 = _dot(p_sc[...], dom_sc[...], TN)
            lanes = slice(kvp * LANES, (kvp + 1) * LANES)
            vlanes = slice(dkv + kvp * LANES, dkv + (kvp + 1) * LANES)
            dkv_ref[rows_p, lanes] += dk2[:BLOCK]
            dkv_ref[rows_c, lanes] += dk2[BLOCK:]
            dkv_ref[rows_p, vlanes] += dv2[:BLOCK]
            dkv_ref[rows_c, vlanes] += dv2[BLOCK:]

        @pl.when(n == nb - 1)
        def _():
            dsink_ref[...] = jnp.sum(dsink_acc[...].reshape(2 * Q_PER_KV, BLOCK, KV_PAIRS), axis=1)

    return _pcall(
        body, name=name, ins=[q, kv, kv, do, probs], dep=dep, grid=(nb,),
        in_specs=_attn_specs(c) + [pl.BlockSpec((BLOCK, dq), lambda n: (n, 0)), _probs_spec()],
        out_specs=[pl.BlockSpec((BLOCK, dq), lambda n: (n, 0)), pl.BlockSpec((S, 2 * dkv), lambda n: (0, 0)),
                   pl.BlockSpec((KV_PAIRS, PAIR_ROWS, 2 * BLOCK), lambda n: (0, 0, 0)),
                   pl.BlockSpec((2 * Q_PER_KV, KV_PAIRS), lambda n: (0, 0))],
        out_shape=[jax.ShapeDtypeStruct((S, dq), BF16), jax.ShapeDtypeStruct((S, 2 * dkv), F32),
                   jax.ShapeDtypeStruct((KV_PAIRS, PAIR_ROWS, 2 * BLOCK), F32),
                   jax.ShapeDtypeStruct((2 * Q_PER_KV, KV_PAIRS), F32)],
        scratch_shapes=[pltpu.VMEM((PAIR_ROWS, KV_PAIRS), F32), pltpu.VMEM((PAIR_ROWS, 2 * BLOCK), BF16),
                        pltpu.VMEM((PAIR_ROWS, 2 * BLOCK), BF16), pltpu.VMEM((PAIR_ROWS, LANES), BF16),
                        pltpu.VMEM((PAIR_ROWS, LANES), BF16)],
        compiler_params=_params(1))


def _sgu_bwd(c, name, a, z, dgated, ln_g, wc, wc_t, b_t, dep=None):
    S, AW, gd, tm = c.S, c.AW, c.gd, c.tms

    def body(a_ref, z_ref, dg_ref, lng_ref, wc_ref, wct_ref, bt_ref, dz_ref, dws_ref, dbt_ref, dlng_ref, dvn_ref):
        i = pl.program_id(0)

        @pl.when(i == 0)
        def _():
            dws_ref[...] = jnp.zeros_like(dws_ref)
            dbt_ref[...] = jnp.zeros_like(dbt_ref)
            dlng_ref[...] = jnp.zeros_like(dlng_ref)

        lng = lng_ref[...]
        va = a_ref[:, AW:].astype(F32)
        xc = va - jnp.mean(va, axis=-1, keepdims=True)
        rstd = lax.rsqrt(jnp.mean(xc * xc, axis=-1, keepdims=True) + EPS)
        xh = xc * rstd
        vn = (xh * lng).astype(BF16)
        causal = _sgu_masks()
        for ch in range(tm // CHUNK):
            rows = slice(ch * CHUNK, (ch + 1) * CHUNK)
            for g in range(A_GROUPS):
                cols = slice(g * gd, (g + 1) * gd)
                blk = vn[rows, cols]
                mixed = _dot(wc_ref[g], blk, NN) + bt_ref[:, g:g + 1]
                dgb = dg_ref[rows, cols].astype(F32)
                dm = dgb * a_ref[rows, cols].astype(F32)
                dbt_ref[:, g:g + 1] += jnp.sum(dm, axis=1, keepdims=True)
                dm16 = dm.astype(BF16)
                dws_ref[g] += jnp.where(causal, _dot(dm16, blk, NT), 0.0)
                dvn_ref[rows, cols] = _dot(wct_ref[g], dm16, NN)
                dz_ref[rows, cols] = (dgb * mixed * _gelu_grad(z_ref[rows, cols].astype(F32))).astype(BF16)
        dvn = dvn_ref[...]
        dlng_ref[...] += jnp.sum(dvn * xh, axis=0, keepdims=True)
        dxh = dvn * lng
        dva = rstd * (dxh - jnp.mean(dxh, axis=-1, keepdims=True) - xh * jnp.mean(dxh * xh, axis=-1, keepdims=True))
        dz_ref[:, AW:] = (dva * _gelu_grad(z_ref[:, AW:].astype(F32))).astype(BF16)

    wide = pl.BlockSpec((tm, 2 * AW), lambda i: (i, 0))
    wsp = pl.BlockSpec((A_GROUPS, CHUNK, CHUNK), lambda i: (0, 0, 0))
    btsp = pl.BlockSpec((CHUNK, A_GROUPS), lambda i: (0, 0))
    return _pcall(
        body, name=name, ins=[a, z, dgated, ln_g, wc, wc_t, b_t], dep=dep, grid=(S // tm,),
        in_specs=[wide, wide, pl.BlockSpec((tm, AW), lambda i: (i, 0)), pl.BlockSpec((1, AW), lambda i: (0, 0)),
                  wsp, wsp, btsp],
        out_specs=[wide, wsp, btsp, pl.BlockSpec((1, AW), lambda i: (0, 0))],
        out_shape=[jax.ShapeDtypeStruct((S, 2 * AW), BF16), jax.ShapeDtypeStruct((A_GROUPS, CHUNK, CHUNK), F32),
                   jax.ShapeDtypeStruct((CHUNK, A_GROUPS), F32), jax.ShapeDtypeStruct((1, AW), F32)],
        scratch_shapes=[pltpu.VMEM((tm, AW), F32)], compiler_params=_params(1))


def _adam_update(g, w_ref, m_ref, v_ref, out_refs):
    g_ref, d_ref, nm_ref, nv_ref = out_refs
    bc1 = 1.0 - ADAM_B1 ** ADAM_STEP
    bc2 = 1.0 - ADAM_B2 ** ADAM_STEP
    nm = ADAM_B1 * m_ref[...] + (1.0 - ADAM_B1) * g
    nv = ADAM_B2 * v_ref[...] + (1.0 - ADAM_B2) * (g * g)
    g_ref[...] = g
    nm_ref[...] = nm
    nv_ref[...] = nv
    d_ref[...] = -ADAM_LR * ((nm * (1.0 / bc1)) / (jnp.sqrt(nv * (1.0 / bc2)) + ADAM_EPS) + ADAM_WD * w_ref[...])


def _adamw_packed(name, sums, offs, ws, ms, vs, dep=None):
    n = len(ws)

    def body(*refs):
        s_ref, w_refs, m_refs, v_refs = refs[0], refs[1:1 + n], refs[1 + n:1 + 2 * n], refs[1 + 2 * n:1 + 3 * n]
        outs = refs[1 + 3 * n:]
        for k in range(n):
            _adam_update(s_ref[offs[k]:offs[k + 1], :], w_refs[k], m_refs[k], v_refs[k], outs[4 * k:4 * k + 4])

    ins = [sums, *ws, *ms, *vs]
    shapes = [jax.ShapeDtypeStruct(w.shape, F32) for w in ws for _ in range(4)]
    return _pcall(body, name=name, ins=ins, in_specs=[_whole(t) for t in ins], dep=dep,
                  out_shape=shapes, out_specs=[_whole(t) for t in shapes], compiler_params=_params(0))


def _adamw(name, parts, part_block, part_index, w, m, v, tr, row_off=0, n_rows=None, prev=None, dep=None):
    R, C = w.shape
    n_rows = R if n_rows is None else n_rows
    assert n_rows % tr == 0 and row_off % tr == 0

    def body(p_ref, w_ref, m_ref, v_ref, *rest):
        g = p_ref[0].astype(F32)
        for s in range(1, part_block[0]):
            g = g + p_ref[s].astype(F32)
        _adam_update(g, w_ref, m_ref, v_ref, rest[-4:])

    ob = row_off // tr
    row = pl.BlockSpec((tr, C), lambda i: (ob + i, 0))
    out = jax.ShapeDtypeStruct((R, C), F32)
    chained = prev is not None
    return _pcall(
        body, name=name, ins=[parts, w, m, v] + (list(prev) if chained else []), dep=dep, grid=(n_rows // tr,),
        in_specs=[pl.BlockSpec(part_block, part_index), row, row, row] + ([ANY] * 4 if chained else []),
        out_specs=[row, row, row, row], out_shape=[out, out, out, out],
        input_output_aliases={4 + t: t for t in range(4)} if chained else {}, compiler_params=_params(1))


def _sum_parts(name, parts, dep=None):
    def body(p_ref, out_ref):
        g = p_ref[0]
        for s in range(1, parts.shape[0]):
            g = g + p_ref[s]
        out_ref[...] = g

    return _pcall(body, name=name, ins=[parts], in_specs=[_whole(parts)], dep=dep,
                  out_shape=jax.ShapeDtypeStruct(parts.shape[1:], F32), compiler_params=_params(0))


def _place():
    return lax.axis_index("x"), lax.axis_index("y"), lax.axis_index("c")


def _slot(px, py, pc):
    return 4 * px + 2 * py + pc


def _peer(k, x, y, c):
    return x ^ ((k >> 2) & 1), y ^ ((k >> 1) & 1), c ^ (k & 1)


SEND_PEERS = {"exchange": tuple(range(1, NDEV)), "gather": (1, 2, 4, 6), "forward": (2, 4, 6),
              "broadcast": tuple(range(1, NDEV))}


def _n_sems(mode, n_lands):
    return n_lands * (len(SEND_PEERS[mode]) + (mode != "forward"))


def _send_copies(mode, src_refs, land_refs, send_sems, recv_sems):
    x, y, c = _place()
    me = _slot(x, y, c)
    peers = SEND_PEERS[mode]
    remote, local = [], []
    for i, k in enumerate(peers):
        peer = _peer(k, x, y, c)
        for a, land in enumerate(land_refs):
            if mode == "exchange":
                src, dst, to = src_refs[a].at[_slot(*peer)], land.at[me], peer
            elif mode in ("gather", "broadcast"):
                src, dst, to = src_refs[a], land.at[me], peer
            else:
                src = dst = land.at[_slot(*peer)]
                to = (x, y, 1 - c)
            s = a * len(peers) + i
            remote.append(pltpu.make_async_remote_copy(src_ref=src, dst_ref=dst, send_sem=send_sems.at[s],
                                                       recv_sem=recv_sems.at[s], device_id=to, device_id_type=MESH))
    if mode != "forward":
        for a, land in enumerate(land_refs):
            src = src_refs[a].at[me] if mode == "exchange" else src_refs[a]
            local.append(pltpu.make_async_copy(src, land.at[me], send_sems.at[len(land_refs) * len(peers) + a]))
    return remote, local


def _send_start_groups(name, groups, mode, collective_id=None, dep=None):
    sizes = [(len(s), len(l)) for s, l in groups]
    flat = [t for s, l in groups for t in (*s, *l)]
    n_in, ng = len(flat), len(groups)

    def body(*refs):
        sems, token, at = refs[n_in:n_in + 2 * ng], refs[-1], 0
        if collective_id is not None:
            x, y, c = _place()
            barrier = pltpu.get_barrier_semaphore()
            shake = (1,) if mode == "forward" else SEND_PEERS[mode]
            for k in shake:
                pl.semaphore_signal(barrier, inc=1, device_id=_peer(k, x, y, c), device_id_type=MESH)
            pl.semaphore_wait(barrier, len(shake))
        for gi, (ns, nl) in enumerate(sizes):
            remote, local = _send_copies(mode, refs[at:at + ns], refs[at + ns:at + ns + nl], sems[2 * gi],
                                         sems[2 * gi + 1])
            for cp in remote + local:
                cp.start()
            at += ns + nl
        token[...] = jnp.zeros_like(token)

    sem_shapes = [pltpu.SemaphoreType.DMA((_n_sems(mode, nl),)) for _, nl in sizes for _ in range(2)]
    if any(dep is t for t in flat):
        dep = None
    out = _pcall(
        body, name=name, ins=[pltpu.with_memory_space_constraint(t, pltpu.HBM) for t in flat],
        in_specs=[HBM] * n_in, dep=dep,
        out_shape=(*sem_shapes, *[pltpu.HBM(t.shape, t.dtype) for t in flat], jax.ShapeDtypeStruct((8, LANES), F32)),
        out_specs=(*[SEM] * (2 * ng), *[HBM] * n_in, pl.BlockSpec(memory_space=pltpu.VMEM)),
        input_output_aliases={i: 2 * ng + i for i in range(n_in)},
        compiler_params=pltpu.CompilerParams(has_side_effects=pltpu.SideEffectType.DATAFLOW_SIDE_EFFECTING,
                                             collective_id=collective_id))
    started, at = [], 2 * ng
    for gi, (ns, nl) in enumerate(sizes):
        started.append((out[-1], out[2 * gi], out[2 * gi + 1], list(out[at:at + ns]), list(out[at + ns:at + ns + nl])))
        at += ns + nl
    return started


def _send_start(name, srcs, lands, mode, collective_id=None, dep=None):
    if mode == "forward":
        collective_id = SIBLING_COLLECTIVE_ID
    return _send_start_groups(name, [(srcs, lands)], mode, collective_id=collective_id, dep=dep)[0]


def _send_wait(name, started, mode, dep=None):
    _, send_sems, recv_sems, srcs, lands = started
    n_src, n = len(srcs), len(lands)

    def body(*refs):
        src_refs, land_refs = refs[:n_src], refs[n_src:n_src + n]
        ssem, rsem = refs[n_src + n], refs[n_src + n + 1]
        remote, local = _send_copies(mode, src_refs, land_refs, ssem, rsem)
        for cp in remote:
            cp.wait_send()
            cp.wait_recv()
        for cp in local:
            cp.wait()

    thru = [pltpu.HBM(t.shape, t.dtype) for t in [*srcs, *lands]]
    out = _pcall(
        body, name=name, ins=[*srcs, *lands, send_sems, recv_sems], in_specs=[HBM] * (n_src + n) + [SEM, SEM], dep=dep,
        out_shape=tuple(thru), out_specs=tuple([HBM] * (n_src + n)),
        input_output_aliases={i: i for i in range(n_src + n)},
        compiler_params=pltpu.CompilerParams(has_side_effects=pltpu.SideEffectType.DATAFLOW_SIDE_EFFECTING))
    return list(out[n_src:])


def _landing(block):
    return lax.empty((NDEV, *block.shape), block.dtype)


def _rows128(t):
    flat = t.reshape(-1)
    n = flat.shape[0]
    rows = -(-n // (8 * LANES)) * 8
    return jnp.pad(flat, (0, rows * LANES - n)).reshape(rows, LANES)


def kernel(x, mix_norm_g, ffn_norm_g, a_w_in, a_ln_g, a_w_spatial, a_b_spatial, a_w_out, kv_norm_g, w_k, w_v, b_w_q, b_sinks, b_w_o, rel_bias, ffn_w1, ffn_w2, final_norm_g, loss_target, m_mix_norm_g, m_ffn_norm_g, m_a_w_in, m_a_ln_g, m_a_w_spatial, m_a_b_spatial, m_a_w_out, m_kv_norm_g, m_w_k, m_w_v, m_b_w_q, m_b_sinks, m_b_w_o, m_rel_bias, m_ffn_w1, m_ffn_w2, m_final_norm_g, v_mix_norm_g, v_ffn_norm_g, v_a_w_in, v_a_ln_g, v_a_w_spatial, v_a_b_spatial, v_a_w_out, v_kv_norm_g, v_w_k, v_w_v, v_b_w_q, v_b_sinks, v_b_w_o, v_rel_bias, v_ffn_w1, v_ffn_w2, v_final_norm_g):
    c = _config(x, a_w_in, a_w_out, w_k, b_w_q, b_w_o, ffn_w1, ffn_w2)
    S, D, LA, LB, LF = c.S, c.D, c.LA, c.LB, c.LF
    weights = dict(mix_norm_g=mix_norm_g, ffn_norm_g=ffn_norm_g, a_w_in=a_w_in, a_ln_g=a_ln_g, a_w_spatial=a_w_spatial,
                   a_b_spatial=a_b_spatial, a_w_out=a_w_out, kv_norm_g=kv_norm_g, w_k=w_k, w_v=w_v, b_w_q=b_w_q,
                   b_sinks=b_sinks, b_w_o=b_w_o, rel_bias=rel_bias, ffn_w1=ffn_w1, ffn_w2=ffn_w2,
                   final_norm_g=final_norm_g)
    m_in = dict(mix_norm_g=m_mix_norm_g, ffn_norm_g=m_ffn_norm_g, a_w_in=m_a_w_in, a_ln_g=m_a_ln_g,
                a_w_spatial=m_a_w_spatial, a_b_spatial=m_a_b_spatial, a_w_out=m_a_w_out, kv_norm_g=m_kv_norm_g,
                w_k=m_w_k, w_v=m_w_v, b_w_q=m_b_w_q, b_sinks=m_b_sinks, b_w_o=m_b_w_o, rel_bias=m_rel_bias,
                ffn_w1=m_ffn_w1, ffn_w2=m_ffn_w2, final_norm_g=m_final_norm_g)
    v_in = dict(mix_norm_g=v_mix_norm_g, ffn_norm_g=v_ffn_norm_g, a_w_in=v_a_w_in, a_ln_g=v_a_ln_g,
                a_w_spatial=v_a_w_spatial, a_b_spatial=v_a_b_spatial, a_w_out=v_a_w_out, kv_norm_g=v_kv_norm_g,
                w_k=v_w_k, w_v=v_w_v, b_w_q=v_b_w_q, b_sinks=v_b_sinks, b_w_o=v_b_w_o, rel_bias=v_rel_bias,
                ffn_w1=v_ffn_w1, ffn_w2=v_ffn_w2, final_norm_g=v_final_norm_g)
    names = list(weights)
    seq = _Seq()
    me = _slot(*_place())
    bf = lambda t: t.astype(BF16)

    tr = lambda t: bf(jnp.swapaxes(t, -1, -2))

    def start(tag, some):
        got = seq(_send_start_groups, f"weights_start_{tag}", [(grp, [_landing(t) for t in grp]) for grp in some],
                  "gather")
        seq.last = got[0][0]
        return got

    started = start("first", [[tr(a_w_in[0])[None], a_ln_g]])
    _, a_w_in_, a_w_out_, ffn_w1_, ffn_w2_, w_k_, w_v_, b_w_q_, b_w_o_ = lax.optimization_barrier(
        (started[0][0], a_w_in, a_w_out, ffn_w1, ffn_w2, w_k, w_v, b_w_q, b_w_o))
    groups = []
    for l in range(LA):
        groups += [[tr(a_w_in_[l])[None]], [bf(a_w_out_[l])], [tr(ffn_w1_[l])[None], bf(ffn_w2_[l])]]
    gb = 3 * LA
    for l in range(LB):
        extra = [bf(jnp.concatenate([w_k_, w_v_], axis=1))] if l == 0 else []
        groups += [extra + [bf(_slot_cols(b_w_q_[l])), bf(b_w_o_[l])], [tr(ffn_w1_[LA + l])[None], bf(ffn_w2_[LA + l])]]
    started += start("rest", groups[1:])
    forwarding = {}

    def forward(i):
        lands = seq(_send_wait, f"weights_wait{i}", started[i], "gather")
        forwarding[i] = seq(_send_start, f"weights_forward{i}", [], lands, "forward")

    def arrive(i):
        if i not in forwarding:
            forward(i)
        return seq(_send_wait, f"weights_arrive{i}", forwarding[i], "forward")

    causal = jnp.tril(jnp.ones((CHUNK, CHUNK), bool))
    wsp = jnp.where(causal[None, None], a_w_spatial, 0.0)
    wsp16 = wsp.astype(BF16)
    wsp16_t = jnp.swapaxes(wsp, -1, -2).astype(BF16)
    bsp_t = jnp.swapaxes(a_b_spatial, -1, -2)
    mix_g = mix_norm_g.reshape(-1, 1, D)
    ffn_g = ffn_norm_g.reshape(-1, 1, D)
    kv_g = kv_norm_g.reshape(1, D)
    fin_g = final_norm_g.reshape(1, D)
    onehot = _bucket_onehot()
    bias = _slot_bias(seq(_band_bias, rel_bias.T, onehot).reshape(N_HEADS, BLOCK, 2 * BLOCK))

    h = x.reshape(S, D)
    sav_a, sav_b, wts_a, wts_b = [], [], [], []
    for l in range(LA):
        got = arrive(3 * l)
        w_in = got[0]
        if l == 0:
            ln_g_full = jnp.transpose(got[1], (1, 0, 2)).reshape(LA, 1, c.AW)
        z, a, hn = seq(_a_in_fwd, c, f"a_in_fwd{l}", h, mix_g[l], w_in, 0)
        forward(3 * l + 1)
        gated = seq(_sgu_fwd, c, f"sgu_fwd{l}", a, ln_g_full[l], wsp16[l], bsp_t[l])
        (wout,) = arrive(3 * l + 1)
        if l > 0:
            forward(3 * l + 2)
        h1 = seq(_mm_res, c, f"a_out_fwd{l}", gated, wout, c.ar, 0, h)
        w1, rows = arrive(3 * l + 2)
        if l == LA - 1:
            forward(gb)
        p, h2, hnf = seq(_ffn_fwd, c, f"ffn_fwd{l}", h1, ffn_g[l], w1, 0, rows)
        sav_a.append((h, z, a, hn, gated, h1, p, hnf))
        wts_a.append((w_in, 0, w1, 0, rows, wout, 0))
        h = h2
    h_kv = h
    for l in range(LB):
        got = arrive(gb + 2 * l)
        if l == 0:
            wkv, got = got[0], got[1:]
        wq, wo = got[0], _slot_rows(got[1])
        if l == 0:
            kv, hkv, q, hn = seq(_rms2_mm_rows, c, "kv_q_fwd", h, [(kv_g, wkv), (mix_g[LA], wq)])
        else:
            q, hn = seq(_rms_mm_rows, c, f"q_fwd{l}", h, mix_g[LA + l], wq, c.qr, 0, c.DQ)
        forward(gb + 2 * l + 1)
        o, probs = seq(_attn_fwd, c, f"attn_fwd{l}", q, kv, bias, b_sinks[l])
        h1 = seq(_mm_res, c, f"o_fwd{l}", o, wo, c.orr, 0, h)
        w1, rows = arrive(gb + 2 * l + 1)
        if l + 1 < LB:
            forward(gb + 2 * l + 2)
        p, h2, hnf = seq(_ffn_fwd, c, f"ffn_fwd{LA + l}", h1, ffn_g[LA + l], w1, 0, rows)
        sav_b.append((h, q, hn, o, probs, h1, p, hnf))
        wts_b.append((wq, wo, w1, rows))
        h = h2
    dh, d_fin_g, loss_row = seq(_final_loss, c, h, fin_g, loss_target.reshape(S, D))

    results = {}
    in_flight, exchanges = [], []

    def update(k, parts, layer, col_blk=0):
        w = weights[k]
        rows_l, ncols = (w.shape[-2], w.shape[-1]) if w.ndim == 3 else w.shape
        flat = lambda t: t.reshape(-1, ncols)
        tr = min(128, rows_l)
        results[k] = seq(_adamw, f"adamw_{k}{layer}", parts, (NDEV, tr, ncols), lambda i: (0, i, col_blk),
                         flat(w), flat(m_in[k]), flat(v_in[k]), tr, row_off=layer * rows_l, n_rows=rows_l,
                         prev=results.get(k))

    def land(tag, entry):
        lands = seq(_send_wait, f"grads_wait_{tag}", entry[1], "exchange")
        for keys, parts in zip(entry[0], lands):
            for k, layer, col_blk in keys:
                update(k, parts, layer, col_blk)

    def send(tag, items):
        slabs = [t for _, t in items]
        own = [lax.empty(t.shape, t.dtype) for t in slabs]
        exchanges.append(tag)
        st = seq(_send_start, f"grads_start_{tag}", slabs, own, "exchange",
                 collective_id=SIBLING_COLLECTIVE_ID + len(exchanges))
        in_flight.append((tag, ([k for k, _ in items], st)))
        while len(in_flight) > EXCHANGE_LAG:
            land(*in_flight.pop(0))

    d_mix_g, d_ffn_g = [None] * LF, [None] * LF
    dkv_list, dbias_list, dsink_list = [], [], [None] * LB

    def ffn_bwd(lf, dh, h1, p, hnf, w1, w1_i, rows):
        da, dh1, d_ffn_g[lf], dhb = seq(_ffn_bwd_data, c, f"ffn_bwd_data{lf}", dh, p, w1, w1_i, rows, h1, ffn_g[lf])
        dw1, dw2 = seq(_ffn_bwd_w, c, f"ffn_bwd_w{lf}", hnf, da, p, dhb)
        send(f"ffn{lf}", [([("ffn_w1", lf, 0)], dw1), ([("ffn_w2", lf, 0)], dw2)])
        return dh1

    for l in reversed(range(LB)):
        h0, q, hn, o, probs, h1, p, hnf = sav_b[l]
        wq, wo, w1, rows = wts_b[l]
        dh1 = ffn_bwd(LA + l, dh, h1, p, hnf, w1, 0, rows)
        do, dwo = seq(_bwd_rows_data_w, c, f"o_bwd{l}", dh1, wo, o, unslot="rows")
        dq, dkv, dbias, dsink = seq(_attn_bwd, c, f"attn_bwd{l}", q, kv, do, probs)
        dsink_list[l] = dsink.reshape(Q_PER_KV, 2, KV_PAIRS).transpose(2, 1, 0).reshape(1, N_HEADS)
        dkv_list.append(dkv)
        dbias_list.append(dbias.reshape(N_HEADS, -1))
        dwq = seq(_wgrad_rows, c, f"q_bwd_w{l}", hn, [dq], D, c.DQ, unslot="cols")
        send(f"attn{l}", [([("b_w_o", l, 0)], dwo), ([("b_w_q", l, 0)], dwq)])
        if l > 0:
            dh, d_mix_g[LA + l] = seq(_bwd_rows_to_stream, c, f"q_bwd_data{l}", [dq], wq, c.qr, 0, c.DQ, h0,
                                      mix_g[LA + l], dh1)
    dwkv = seq(_wgrad_rows, c, "kv_bwd_w", hkv, dkv_list, D, 2 * c.DKV)
    send("kv", [([("w_k", 0, 0), ("w_v", 0, 1)], dwkv)])
    dh, d_mix_g[LA], d_kv_g = seq(_bwd_q_kv_to_stream, c, "q_kv_bwd_data", dq, wq, dkv_list, wkv, h_kv, mix_g[LA],
                                  kv_g, dh1)
    d_rel_t = seq(_band_bias_grad, dbias_list, onehot)
    d_rel_t = d_rel_t.reshape(KV_PAIRS, Q_PER_KV, 2, N_BUCKETS).swapaxes(1, 2).reshape(N_HEADS, N_BUCKETS)
    d_wsp, d_bsp, d_lng = [None] * LA, [None] * LA, [None] * LA
    for l in reversed(range(LA)):
        h0, z, a, hn, gated, h1, p, hnf = sav_a[l]
        w_in, in_i, w1, w1_i, rows, wout, wout_i = wts_a[l]
        dh1 = ffn_bwd(l, dh, h1, p, hnf, w1, w1_i, rows)
        dgated = seq(_bwd_rows_data, c, f"a_out_bwd_data{l}", dh1, wout, c.ar, wout_i)
        dwout = seq(_wgrad_rows, c, f"a_out_bwd_w{l}", gated, [dh1], c.AW, D)
        send(f"a_out{l}", [([("a_w_out", l, 0)], dwout)])
        dz, d_wsp[l], dbt, d_lng[l] = seq(_sgu_bwd, c, f"sgu_bwd{l}", a, z, dgated, ln_g_full[l], wsp16[l],
                                          wsp16_t[l], bsp_t[l])
        d_bsp[l] = dbt.T
        dwin = seq(_wgrad_cols, c, f"a_in_bwd_w{l}", hn, dz)
        send(f"a_in{l}", [([("a_w_in", l, 0)], dwin)])
        dh, d_mix_g[l] = seq(_bwd_cols_to_stream, c, f"a_in_bwd_data{l}", dz, w_in, in_i, h0, mix_g[l], dh1)
    grad_x = dh.reshape(1, S, D)

    small = {
        "mix_norm_g": jnp.concatenate(d_mix_g, axis=0), "ffn_norm_g": jnp.concatenate(d_ffn_g, axis=0),
        "a_w_spatial": jnp.stack(d_wsp), "a_b_spatial": jnp.stack(d_bsp), "kv_norm_g": d_kv_g,
        "b_sinks": jnp.concatenate(dsink_list, axis=0), "rel_bias": d_rel_t.T, "final_norm_g": d_fin_g,
    }
    small_names = list(small)
    packs = [_rows128(small[k]) for k in small_names] + [_rows128(jnp.concatenate(d_lng, axis=0)), _rows128(loss_row)]
    offs = [int(o) for o in np.cumsum([0] + [p.shape[0] for p in packs])]
    Rs = offs[-1] + (-offs[-1]) % (8 * NDEV)
    packed = jnp.concatenate(packs + [jnp.zeros((Rs - offs[-1], LANES), F32)], axis=0)
    slab = packed.reshape(NDEV, Rs // NDEV, LANES)
    st = seq(_send_start, "small_grads_start", [slab], [lax.empty(slab.shape, slab.dtype)], "exchange")
    while len(in_flight) > 1:
        land(*in_flight.pop(0))
    (parts,) = seq(_send_wait, "small_grads_wait", st, "exchange")
    mine = seq(_sum_parts, "small_grads_sum", parts)
    st = seq(_send_start, "small_sums_start", [mine], [_landing(mine)], "broadcast")
    while in_flight:
        land(*in_flight.pop(0))
    (sums,) = seq(_send_wait, "small_sums_wait", st, "broadcast")
    sums = sums.reshape(Rs, LANES)
    loss = sums[offs[-2], 0]

    grads, deltas, new_m, new_v = {}, {}, {}, {}

    def put(k, outs, shape):
        grads[k], deltas[k], new_m[k], new_v[k] = (t.reshape(shape) for t in outs)

    outs = seq(_adamw_packed, "adamw_small", sums, offs, *[[_rows128(d[k]) for k in small_names]
                                                          for d in (weights, m_in, v_in)])
    for n_, k in enumerate(small_names):
        shape = weights[k].shape
        size = int(np.prod(shape))
        put(k, [t.reshape(-1)[:size] for t in outs[4 * n_:4 * n_ + 4]], shape)
    lng_sum = sums[offs[-3]:offs[-2]].reshape(-1)[:LA * c.AW].reshape(LA, c.AW)
    lng_mine = lax.dynamic_slice_in_dim(lng_sum, me * c.ar, c.ar, axis=1)
    lng_parts = jnp.concatenate([lng_mine[None], jnp.zeros((NDEV - 1, LA, c.ar), F32)], axis=0)
    put("a_ln_g", seq(_adamw, "adamw_ln_g", lng_parts, (NDEV, LA, c.ar), lambda i: (0, 0, 0),
                      a_ln_g, m_in["a_ln_g"], v_in["a_ln_g"], LA), a_ln_g.shape)
    for k in ("a_w_in", "ffn_w1", "ffn_w2", "a_w_out", "b_w_o", "b_w_q", "w_k", "w_v"):
        put(k, results[k], weights[k].shape)

    return (loss, grad_x, *[grads[k] for k in names], *[deltas[k] for k in names],
            *[new_m[k] for k in names], *[new_v[k] for k in names])
```
